```python
import math
import jax, jax.numpy as jnp
from jax import lax
import numpy as np

D_MODEL = 1024
BATCH = 8
SEQ = 4096
DEPTH = 1

HGRN_EXPAND = 128
HGRN_HEADS = D_MODEL // HGRN_EXPAND
HGRN_DK = HGRN_EXPAND
HGRN_DV = D_MODEL // HGRN_HEADS
HGRN_K_WIDTH = HGRN_HEADS * HGRN_DK
HGRN_V_WIDTH = HGRN_HEADS * HGRN_DV
CHUNK = 64
CONV_CH = D_MODEL
CONV_K = 31
D_FF = 2816
FFN_RESIDUAL = 0.5
N_MOD = 9
EPS = 1e-6
IN_SIZES = (HGRN_K_WIDTH, HGRN_K_WIDTH, HGRN_V_WIDTH, HGRN_V_WIDTH, 2 * CONV_CH, D_MODEL, D_MODEL)
IN_WIDTH = sum(IN_SIZES)
IN_SPLITS = tuple(int(s) for s in np.cumsum(IN_SIZES)[:-1])

kernel_name = "hybrid_hgrn2_conformer_macaron_adaln"


def rms_norm(x, g):
    xf = x.astype(jnp.float32)
    y = xf * lax.rsqrt(jnp.mean(xf * xf, axis=-1, keepdims=True) + EPS)
    return (y * g.astype(jnp.float32)).astype(x.dtype)


def layer_norm(x, g, b):
    xf = x.astype(jnp.float32)
    mu = jnp.mean(xf, axis=-1, keepdims=True)
    xc = xf - mu
    y = xc * lax.rsqrt(jnp.mean(xc * xc, axis=-1, keepdims=True) + EPS)
    return (y * g.astype(jnp.float32) + b.astype(jnp.float32)).astype(x.dtype)


def swiglu(h, w_in, w_out):
    a, b = jnp.split(h @ w_in, 2, axis=-1)
    return (jax.nn.silu(a) * b) @ w_out


def hgrn2_chunked(q, k, v, logf):
    B, S, H, DK = q.shape
    DV = v.shape[-1]
    nc = S // CHUNK

    def to_chunks(t):
        return t.reshape(B, nc, CHUNK, H, t.shape[-1]).transpose(1, 0, 3, 2, 4)

    causal = jnp.tril(jnp.ones((CHUNK, CHUNK), dtype=bool))

    def step(state, inp):
        qc, kc, vc, gc = inp
        b = jnp.cumsum(gc, axis=-2)
        diff = b[:, :, :, None, :] - b[:, :, None, :, :]
        decay = jnp.exp(jnp.where(causal[:, :, None], diff, -jnp.inf))
        att = jnp.einsum('bhtd,bhsd,bhtsd->bhts', qc, kc, decay)
        o_intra = jnp.einsum('bhts,bhsv->bhtv', att, vc)
        o_inter = jnp.einsum('bhtd,bhdv->bhtv', qc * jnp.exp(b), state)
        b_last = b[:, :, -1, :]
        k_dec = kc * jnp.exp(b_last[:, :, None, :] - b)
        new_state = jnp.exp(b_last)[..., None] * state + jnp.einsum('bhsd,bhsv->bhdv', k_dec, vc)
        return new_state, o_intra + o_inter

    state0 = jnp.zeros((B, H, DK, DV), jnp.float32)
    _, o = lax.scan(step, state0, (to_chunks(q), to_chunks(k), to_chunks(v), to_chunks(logf)))
    return o.transpose(1, 0, 3, 2, 4).reshape(B, S, H, DV)


def causal_depthwise_conv(u, w, b):
    y = lax.conv_general_dilated(
        u, w[:, None, :].astype(u.dtype), window_strides=(1,), padding=[(CONV_K - 1, 0)],
        dimension_numbers=('NWC', 'WIO', 'NWC'), feature_group_count=u.shape[-1])
    return y + b


def token_mixer(h, lb, w_in, hgrn_g, hgrn_w_o, conv_w, conv_b, conv_ln_g, conv_ln_b, conv_w_o, w_out):
    B, S, _ = h.shape
    f32 = jnp.float32
    q, f, i, og, u, ga, gb = jnp.split(h @ w_in, IN_SPLITS, axis=-1)
    q = (jax.nn.silu(q.astype(f32)) * (HGRN_DK ** -0.5)).reshape(B, S, HGRN_HEADS, HGRN_DK)
    fg = lb + (1.0 - lb) * jax.nn.sigmoid(f.astype(f32))
    logf = jnp.log(fg).reshape(B, S, HGRN_HEADS, HGRN_DK)
    k = (1.0 - fg).reshape(B, S, HGRN_HEADS, HGRN_DK)
    v = i.astype(f32).reshape(B, S, HGRN_HEADS, HGRN_DV)
    o = hgrn2_chunked(q, k, v, logf)
    o = o * lax.rsqrt(jnp.mean(o * o, axis=-1, keepdims=True) + EPS)
    o = o * hgrn_g.astype(f32).reshape(HGRN_HEADS, HGRN_DV)
    o = (o.reshape(B, S, HGRN_V_WIDTH) * jax.nn.silu(og.astype(f32))).astype(h.dtype)
    y_a = o @ hgrn_w_o
    ua, ub = jnp.split(u, 2, axis=-1)
    u = ua * jax.nn.sigmoid(ub)
    u = causal_depthwise_conv(u, conv_w, conv_b)
    u = jax.nn.silu(layer_norm(u, conv_ln_g, conv_ln_b))
    y_b = u @ conv_w_o
    merged = jax.nn.sigmoid(ga) * y_a + jax.nn.sigmoid(gb) * y_b
    return merged @ w_out


def _fwd_setup_inputs(seed: int = 0) -> dict:
    key = jax.random.key(seed)
    ks = jax.random.split(key, 24)
    D, L = D_MODEL, DEPTH
    nrm = lambda k, shape, fan_in: jax.random.normal(k, shape, jnp.float32) * (fan_in ** -0.5)
    gain = lambda k, shape: 1.0 + 0.02 * jax.random.normal(k, shape, jnp.float32)
    small = lambda k, shape: 0.02 * jax.random.normal(k, shape, jnp.float32)
    return {
        "x": jax.random.normal(ks[0], (BATCH, SEQ, D), jnp.float32),
        "c": jax.random.normal(ks[1], (BATCH, D), jnp.float32),
        "ada_w": nrm(ks[2], (L, D, N_MOD * D), D),
        "ada_b": small(ks[3], (L, N_MOD * D)),
        "norm_ffn1": gain(ks[4], (L, D)),
        "ffn1_w_in": nrm(ks[5], (L, D, 2 * D_FF), D),
        "ffn1_w_out": nrm(ks[6], (L, D_FF, D), D_FF),
        "norm_mix": gain(ks[7], (L, D)),
        "mix_w_in": nrm(ks[8], (L, D, IN_WIDTH), D),
        "hgrn_lb": 0.1 * jax.random.normal(ks[9], (L + 1, HGRN_K_WIDTH), jnp.float32),
        "hgrn_g": gain(ks[10], (L, HGRN_V_WIDTH)),
        "hgrn_w_o": nrm(ks[11], (L, HGRN_V_WIDTH, D), HGRN_V_WIDTH),
        "conv_w": nrm(ks[12], (L, CONV_K, CONV_CH), CONV_K),
        "conv_b": small(ks[13], (L, CONV_CH)),
        "conv_ln_g": gain(ks[14], (L, CONV_CH)),
        "conv_ln_b": small(ks[15], (L, CONV_CH)),
        "conv_w_o": nrm(ks[16], (L, CONV_CH, D), CONV_CH),
        "mix_w_out": nrm(ks[17], (L, D, D), D),
        "norm_ffn2": gain(ks[18], (L, D)),
        "ffn2_w_in": nrm(ks[19], (L, D, 2 * D_FF), D),
        "ffn2_w_out": nrm(ks[20], (L, D_FF, D), D_FF),
        "norm_final": gain(ks[21], (D,)),
    }


def _fwd_reference(x, c, ada_w, ada_b, norm_ffn1, ffn1_w_in, ffn1_w_out, norm_mix, mix_w_in,
              hgrn_lb, hgrn_g, hgrn_w_o, conv_w, conv_b, conv_ln_g, conv_ln_b, conv_w_o,
              mix_w_out, norm_ffn2, ffn2_w_in, ffn2_w_out, norm_final):
    B = x.shape[0]
    lb_all = jnp.cumsum(jax.nn.softmax(hgrn_lb.astype(jnp.float32), axis=0), axis=0)
    cs = jax.nn.silu(c)
    for l in range(DEPTH):
        mod = (cs @ ada_w[l] + ada_b[l]).reshape(B, N_MOD, D_MODEL)
        sh1, sc1, g1, sh2, sc2, g2, sh3, sc3, g3 = [mod[:, j, None, :] for j in range(N_MOD)]
        h = rms_norm(x, norm_ffn1[l]) * (1.0 + sc1) + sh1
        x = x + FFN_RESIDUAL * g1 * swiglu(h, ffn1_w_in[l], ffn1_w_out[l])
        h = rms_norm(x, norm_mix[l]) * (1.0 + sc2) + sh2
        x = x + g2 * token_mixer(h, lb_all[l], mix_w_in[l], hgrn_g[l], hgrn_w_o[l], conv_w[l],
                                 conv_b[l], conv_ln_g[l], conv_ln_b[l], conv_w_o[l], mix_w_out[l])
        h = rms_norm(x, norm_ffn2[l]) * (1.0 + sc3) + sh3
        x = x + FFN_RESIDUAL * g3 * swiglu(h, ffn2_w_in[l], ffn2_w_out[l])
    return rms_norm(x, norm_final)


import jax as _jax
import jax.numpy as _jnp

TWIN_FORMAT = 'train_step'
FWD_PARAMS = ['x', 'c', 'ada_w', 'ada_b', 'norm_ffn1', 'ffn1_w_in', 'ffn1_w_out', 'norm_mix', 'mix_w_in', 'hgrn_lb', 'hgrn_g', 'hgrn_w_o', 'conv_w', 'conv_b', 'conv_ln_g', 'conv_ln_b', 'conv_w_o', 'mix_w_out', 'norm_ffn2', 'ffn2_w_in', 'ffn2_w_out', 'norm_final']
TWIN_WEIGHTS = ['ada_w', 'ada_b', 'norm_ffn1', 'ffn1_w_in', 'ffn1_w_out', 'norm_mix', 'mix_w_in', 'hgrn_lb', 'hgrn_g', 'hgrn_w_o', 'conv_w', 'conv_b', 'conv_ln_g', 'conv_ln_b', 'conv_w_o', 'mix_w_out', 'norm_ffn2', 'ffn2_w_in', 'ffn2_w_out', 'norm_final']
TWIN_DIFF_INPUT = 'x'
TWIN_INPUTS = ['x', 'c', 'ada_w', 'ada_b', 'norm_ffn1', 'ffn1_w_in', 'ffn1_w_out', 'norm_mix', 'mix_w_in', 'hgrn_lb', 'hgrn_g', 'hgrn_w_o', 'conv_w', 'conv_b', 'conv_ln_g', 'conv_ln_b', 'conv_w_o', 'mix_w_out', 'norm_ffn2', 'ffn2_w_in', 'ffn2_w_out', 'norm_final', 'loss_target', 'm_ada_w', 'm_ada_b', 'm_norm_ffn1', 'm_ffn1_w_in', 'm_ffn1_w_out', 'm_norm_mix', 'm_mix_w_in', 'm_hgrn_lb', 'm_hgrn_g', 'm_hgrn_w_o', 'm_conv_w', 'm_conv_b', 'm_conv_ln_g', 'm_conv_ln_b', 'm_conv_w_o', 'm_mix_w_out', 'm_norm_ffn2', 'm_ffn2_w_in', 'm_ffn2_w_out', 'm_norm_final', 'v_ada_w', 'v_ada_b', 'v_norm_ffn1', 'v_ffn1_w_in', 'v_ffn1_w_out', 'v_norm_mix', 'v_mix_w_in', 'v_hgrn_lb', 'v_hgrn_g', 'v_hgrn_w_o', 'v_conv_w', 'v_conv_b', 'v_conv_ln_g', 'v_conv_ln_b', 'v_conv_w_o', 'v_mix_w_out', 'v_norm_ffn2', 'v_ffn2_w_in', 'v_ffn2_w_out', 'v_norm_final']
TWIN_OUTPUTS = ['loss', 'grad_x', 'grad_ada_w', 'grad_ada_b', 'grad_norm_ffn1', 'grad_ffn1_w_in', 'grad_ffn1_w_out', 'grad_norm_mix', 'grad_mix_w_in', 'grad_hgrn_lb', 'grad_hgrn_g', 'grad_hgrn_w_o', 'grad_conv_w', 'grad_conv_b', 'grad_conv_ln_g', 'grad_conv_ln_b', 'grad_conv_w_o', 'grad_mix_w_out', 'grad_norm_ffn2', 'grad_ffn2_w_in', 'grad_ffn2_w_out', 'grad_norm_final', 'delta_ada_w', 'delta_ada_b', 'delta_norm_ffn1', 'delta_ffn1_w_in', 'delta_ffn1_w_out', 'delta_norm_mix', 'delta_mix_w_in', 'delta_hgrn_lb', 'delta_hgrn_g', 'delta_hgrn_w_o', 'delta_conv_w', 'delta_conv_b', 'delta_conv_ln_g', 'delta_conv_ln_b', 'delta_conv_w_o', 'delta_mix_w_out', 'delta_norm_ffn2', 'delta_ffn2_w_in', 'delta_ffn2_w_out', 'delta_norm_final', 'new_m_ada_w', 'new_m_ada_b', 'new_m_norm_ffn1', 'new_m_ffn1_w_in', 'new_m_ffn1_w_out', 'new_m_norm_mix', 'new_m_mix_w_in', 'new_m_hgrn_lb', 'new_m_hgrn_g', 'new_m_hgrn_w_o', 'new_m_conv_w', 'new_m_conv_b', 'new_m_conv_ln_g', 'new_m_conv_ln_b', 'new_m_conv_w_o', 'new_m_mix_w_out', 'new_m_norm_ffn2', 'new_m_ffn2_w_in', 'new_m_ffn2_w_out', 'new_m_norm_final', 'new_v_ada_w', 'new_v_ada_b', 'new_v_norm_ffn1', 'new_v_ffn1_w_in', 'new_v_ffn1_w_out', 'new_v_norm_mix', 'new_v_mix_w_in', 'new_v_hgrn_lb', 'new_v_hgrn_g', 'new_v_hgrn_w_o', 'new_v_conv_w', 'new_v_conv_b', 'new_v_conv_ln_g', 'new_v_conv_ln_b', 'new_v_conv_w_o', 'new_v_mix_w_out', 'new_v_norm_ffn2', 'new_v_ffn2_w_in', 'new_v_ffn2_w_out', 'new_v_norm_final']
TWIN_LEAF_KINDS = {'loss': 'loss', 'grad_x': 'grad_x', 'grad_ada_w': 'grad_w', 'grad_ada_b': 'grad_w', 'grad_norm_ffn1': 'grad_w', 'grad_ffn1_w_in': 'grad_w', 'grad_ffn1_w_out': 'grad_w', 'grad_norm_mix': 'grad_w', 'grad_mix_w_in': 'grad_w', 'grad_hgrn_lb': 'grad_w', 'grad_hgrn_g': 'grad_w', 'grad_hgrn_w_o': 'grad_w', 'grad_conv_w': 'grad_w', 'grad_conv_b': 'grad_w', 'grad_conv_ln_g': 'grad_w', 'grad_conv_ln_b': 'grad_w', 'grad_conv_w_o': 'grad_w', 'grad_mix_w_out': 'grad_w', 'grad_norm_ffn2': 'grad_w', 'grad_ffn2_w_in': 'grad_w', 'grad_ffn2_w_out': 'grad_w', 'grad_norm_final': 'grad_w', 'delta_ada_w': 'delta_w', 'delta_ada_b': 'delta_w', 'delta_norm_ffn1': 'delta_w', 'delta_ffn1_w_in': 'delta_w', 'delta_ffn1_w_out': 'delta_w', 'delta_norm_mix': 'delta_w', 'delta_mix_w_in': 'delta_w', 'delta_hgrn_lb': 'delta_w', 'delta_hgrn_g': 'delta_w', 'delta_hgrn_w_o': 'delta_w', 'delta_conv_w': 'delta_w', 'delta_conv_b': 'delta_w', 'delta_conv_ln_g': 'delta_w', 'delta_conv_ln_b': 'delta_w', 'delta_conv_w_o': 'delta_w', 'delta_mix_w_out': 'delta_w', 'delta_norm_ffn2': 'delta_w', 'delta_ffn2_w_in': 'delta_w', 'delta_ffn2_w_out': 'delta_w', 'delta_norm_final': 'delta_w', 'new_m_ada_w': 'new_m', 'new_m_ada_b': 'new_m', 'new_m_norm_ffn1': 'new_m', 'new_m_ffn1_w_in': 'new_m', 'new_m_ffn1_w_out': 'new_m', 'new_m_norm_mix': 'new_m', 'new_m_mix_w_in': 'new_m', 'new_m_hgrn_lb': 'new_m', 'new_m_hgrn_g': 'new_m', 'new_m_hgrn_w_o': 'new_m', 'new_m_conv_w': 'new_m', 'new_m_conv_b': 'new_m', 'new_m_conv_ln_g': 'new_m', 'new_m_conv_ln_b': 'new_m', 'new_m_conv_w_o': 'new_m', 'new_m_mix_w_out': 'new_m', 'new_m_norm_ffn2': 'new_m', 'new_m_ffn2_w_in': 'new_m', 'new_m_ffn2_w_out': 'new_m', 'new_m_norm_final': 'new_m', 'new_v_ada_w': 'new_v', 'new_v_ada_b': 'new_v', 'new_v_norm_ffn1': 'new_v', 'new_v_ffn1_w_in': 'new_v', 'new_v_ffn1_w_out': 'new_v', 'new_v_norm_mix': 'new_v', 'new_v_mix_w_in': 'new_v', 'new_v_hgrn_lb': 'new_v', 'new_v_hgrn_g': 'new_v', 'new_v_hgrn_w_o': 'new_v', 'new_v_conv_w': 'new_v', 'new_v_conv_b': 'new_v', 'new_v_conv_ln_g': 'new_v', 'new_v_conv_ln_b': 'new_v', 'new_v_conv_w_o': 'new_v', 'new_v_mix_w_out': 'new_v', 'new_v_norm_ffn2': 'new_v', 'new_v_ffn2_w_in': 'new_v', 'new_v_ffn2_w_out': 'new_v', 'new_v_norm_final': 'new_v'}


def _forward(args):
    return _fwd_reference(*[args[k] for k in FWD_PARAMS])


def _output_shape():
    out = _jax.eval_shape(lambda: _forward(_fwd_setup_inputs(0)))
    return out.shape, out.dtype

N_MICROBATCH = 1
ADAM_LR = 0.001
ADAM_B1 = 0.9
ADAM_B2 = 0.999
ADAM_EPS = 1e-08
ADAM_WD = 0.01
ADAM_STEP = 10
PER_EXAMPLE_BATCH_AXIS = {'x': 0, 'c': 0, 'loss_target': 0}
SHARED_INPUTS = []
_WEIGHT_DTYPES = {'ada_w': _jnp.float32, 'ada_b': _jnp.float32, 'norm_ffn1': _jnp.float32, 'ffn1_w_in': _jnp.float32, 'ffn1_w_out': _jnp.float32, 'norm_mix': _jnp.float32, 'mix_w_in': _jnp.float32, 'hgrn_lb': _jnp.float32, 'hgrn_g': _jnp.float32, 'hgrn_w_o': _jnp.float32, 'conv_w': _jnp.float32, 'conv_b': _jnp.float32, 'conv_ln_g': _jnp.float32, 'conv_ln_b': _jnp.float32, 'conv_w_o': _jnp.float32, 'mix_w_out': _jnp.float32, 'norm_ffn2': _jnp.float32, 'ffn2_w_in': _jnp.float32, 'ffn2_w_out': _jnp.float32, 'norm_final': _jnp.float32}
MOMENT_SCALE = {'ada_w': 4.812922e-02, 'ada_b': 8.218126e-02, 'norm_ffn1': 7.903904e-02, 'ffn1_w_in': 3.569230e-02, 'ffn1_w_out': 5.817271e-02, 'norm_mix': 6.572108e-02, 'mix_w_in': 2.976828e-02, 'hgrn_lb': 3.100873e-03, 'hgrn_g': 4.673280e-02, 'hgrn_w_o': 4.886748e-02, 'conv_w': 3.532283e-02, 'conv_b': 5.189814e-02, 'conv_ln_g': 4.587929e-02, 'conv_ln_b': 3.932067e-02, 'conv_w_o': 3.584705e-02, 'mix_w_out': 5.977485e-02, 'norm_ffn2': 6.882072e-02, 'ffn2_w_in': 3.167631e-02, 'ffn2_w_out': 5.209868e-02, 'norm_final': 3.221430e+01}


def _to_microbatches(a, axis):
    t = _jnp.moveaxis(a, axis, 0)
    t = t.reshape((N_MICROBATCH, t.shape[0] // N_MICROBATCH) + t.shape[1:])
    return _jnp.moveaxis(t, 1, axis + 1)


def setup_inputs(seed: int = 0) -> dict:
    inp = _fwd_setup_inputs(seed)
    key = _jax.random.fold_in(_jax.random.key(seed), 7919)
    shape, _ = _output_shape()
    out = dict(inp)
    out["loss_target"] = _jax.random.normal(_jax.random.fold_in(key, 0), shape, _jnp.float32)
    for i, name in enumerate(TWIN_WEIGHTS):
        w = inp[name].astype(_jnp.float32)
        if MOMENT_SCALE is None:
            s = _jnp.sqrt(_jnp.mean(_jnp.square(w)) + 1e-30)
        else:
            s = MOMENT_SCALE[name]
        km, kv = _jax.random.split(_jax.random.fold_in(key, i + 1))
        out[name] = w
        out["m_" + name] = s * _jax.random.normal(km, w.shape, _jnp.float32)
        out["v_" + name] = (s * s) * _jax.random.uniform(kv, w.shape, _jnp.float32, 0.5, 1.5)
    if N_MICROBATCH > 1:
        for name, axis in PER_EXAMPLE_BATCH_AXIS.items():
            out[name] = _to_microbatches(out[name], axis)
    return {'x': out['x'], 'c': out['c'], 'ada_w': out['ada_w'], 'ada_b': out['ada_b'], 'norm_ffn1': out['norm_ffn1'], 'ffn1_w_in': out['ffn1_w_in'], 'ffn1_w_out': out['ffn1_w_out'], 'norm_mix': out['norm_mix'], 'mix_w_in': out['mix_w_in'], 'hgrn_lb': out['hgrn_lb'], 'hgrn_g': out['hgrn_g'], 'hgrn_w_o': out['hgrn_w_o'], 'conv_w': out['conv_w'], 'conv_b': out['conv_b'], 'conv_ln_g': out['conv_ln_g'], 'conv_ln_b': out['conv_ln_b'], 'conv_w_o': out['conv_w_o'], 'mix_w_out': out['mix_w_out'], 'norm_ffn2': out['norm_ffn2'], 'ffn2_w_in': out['ffn2_w_in'], 'ffn2_w_out': out['ffn2_w_out'], 'norm_final': out['norm_final'], 'loss_target': out['loss_target'], 'm_ada_w': out['m_ada_w'], 'm_ada_b': out['m_ada_b'], 'm_norm_ffn1': out['m_norm_ffn1'], 'm_ffn1_w_in': out['m_ffn1_w_in'], 'm_ffn1_w_out': out['m_ffn1_w_out'], 'm_norm_mix': out['m_norm_mix'], 'm_mix_w_in': out['m_mix_w_in'], 'm_hgrn_lb': out['m_hgrn_lb'], 'm_hgrn_g': out['m_hgrn_g'], 'm_hgrn_w_o': out['m_hgrn_w_o'], 'm_conv_w': out['m_conv_w'], 'm_conv_b': out['m_conv_b'], 'm_conv_ln_g': out['m_conv_ln_g'], 'm_conv_ln_b': out['m_conv_ln_b'], 'm_conv_w_o': out['m_conv_w_o'], 'm_mix_w_out': out['m_mix_w_out'], 'm_norm_ffn2': out['m_norm_ffn2'], 'm_ffn2_w_in': out['m_ffn2_w_in'], 'm_ffn2_w_out': out['m_ffn2_w_out'], 'm_norm_final': out['m_norm_final'], 'v_ada_w': out['v_ada_w'], 'v_ada_b': out['v_ada_b'], 'v_norm_ffn1': out['v_norm_ffn1'], 'v_ffn1_w_in': out['v_ffn1_w_in'], 'v_ffn1_w_out': out['v_ffn1_w_out'], 'v_norm_mix': out['v_norm_mix'], 'v_mix_w_in': out['v_mix_w_in'], 'v_hgrn_lb': out['v_hgrn_lb'], 'v_hgrn_g': out['v_hgrn_g'], 'v_hgrn_w_o': out['v_hgrn_w_o'], 'v_conv_w': out['v_conv_w'], 'v_conv_b': out['v_conv_b'], 'v_conv_ln_g': out['v_conv_ln_g'], 'v_conv_ln_b': out['v_conv_ln_b'], 'v_conv_w_o': out['v_conv_w_o'], 'v_mix_w_out': out['v_mix_w_out'], 'v_norm_ffn2': out['v_norm_ffn2'], 'v_ffn2_w_in': out['v_ffn2_w_in'], 'v_ffn2_w_out': out['v_ffn2_w_out'], 'v_norm_final': out['v_norm_final']}


def _loss(weights, diff, rest, loss_target):
    with _jax.named_scope("forward"):
        args = {**rest, TWIN_DIFF_INPUT: diff, **{k: w.astype(_WEIGHT_DTYPES[k]) for k, w in weights.items()}}
        y = _forward(args)
    with _jax.named_scope("loss_head"):
        err = _jnp.square(y.astype(_jnp.float32) - loss_target)
        return 0.5 * _jnp.sum(_jnp.mean(err, axis=-1)) if err.ndim else 0.5 * err


def _adamw(w, g, m, v):
    m = ADAM_B1 * m + (1.0 - ADAM_B1) * g
    v = ADAM_B2 * v + (1.0 - ADAM_B2) * _jnp.square(g)
    m_hat = m / (1.0 - ADAM_B1 ** ADAM_STEP)
    v_hat = v / (1.0 - ADAM_B2 ** ADAM_STEP)
    delta = -ADAM_LR * (m_hat / (_jnp.sqrt(v_hat) + ADAM_EPS) + ADAM_WD * w)
    return delta, m, v


def reference(x, c, ada_w, ada_b, norm_ffn1, ffn1_w_in, ffn1_w_out, norm_mix, mix_w_in, hgrn_lb, hgrn_g, hgrn_w_o, conv_w, conv_b, conv_ln_g, conv_ln_b, conv_w_o, mix_w_out, norm_ffn2, ffn2_w_in, ffn2_w_out, norm_final, loss_target, m_ada_w, m_ada_b, m_norm_ffn1, m_ffn1_w_in, m_ffn1_w_out, m_norm_mix, m_mix_w_in, m_hgrn_lb, m_hgrn_g, m_hgrn_w_o, m_conv_w, m_conv_b, m_conv_ln_g, m_conv_ln_b, m_conv_w_o, m_mix_w_out, m_norm_ffn2, m_ffn2_w_in, m_ffn2_w_out, m_norm_final, v_ada_w, v_ada_b, v_norm_ffn1, v_ffn1_w_in, v_ffn1_w_out, v_norm_mix, v_mix_w_in, v_hgrn_lb, v_hgrn_g, v_hgrn_w_o, v_conv_w, v_conv_b, v_conv_ln_g, v_conv_ln_b, v_conv_w_o, v_mix_w_out, v_norm_ffn2, v_ffn2_w_in, v_ffn2_w_out, v_norm_final):
    given = dict(x=x, c=c, ada_w=ada_w, ada_b=ada_b, norm_ffn1=norm_ffn1, ffn1_w_in=ffn1_w_in, ffn1_w_out=ffn1_w_out, norm_mix=norm_mix, mix_w_in=mix_w_in, hgrn_lb=hgrn_lb, hgrn_g=hgrn_g, hgrn_w_o=hgrn_w_o, conv_w=conv_w, conv_b=conv_b, conv_ln_g=conv_ln_g, conv_ln_b=conv_ln_b, conv_w_o=conv_w_o, mix_w_out=mix_w_out, norm_ffn2=norm_ffn2, ffn2_w_in=ffn2_w_in, ffn2_w_out=ffn2_w_out, norm_final=norm_final, loss_target=loss_target, m_ada_w=m_ada_w, m_ada_b=m_ada_b, m_norm_ffn1=m_norm_ffn1, m_ffn1_w_in=m_ffn1_w_in, m_ffn1_w_out=m_ffn1_w_out, m_norm_mix=m_norm_mix, m_mix_w_in=m_mix_w_in, m_hgrn_lb=m_hgrn_lb, m_hgrn_g=m_hgrn_g, m_hgrn_w_o=m_hgrn_w_o, m_conv_w=m_conv_w, m_conv_b=m_conv_b, m_conv_ln_g=m_conv_ln_g, m_conv_ln_b=m_conv_ln_b, m_conv_w_o=m_conv_w_o, m_mix_w_out=m_mix_w_out, m_norm_ffn2=m_norm_ffn2, m_ffn2_w_in=m_ffn2_w_in, m_ffn2_w_out=m_ffn2_w_out, m_norm_final=m_norm_final, v_ada_w=v_ada_w, v_ada_b=v_ada_b, v_norm_ffn1=v_norm_ffn1, v_ffn1_w_in=v_ffn1_w_in, v_ffn1_w_out=v_ffn1_w_out, v_norm_mix=v_norm_mix, v_mix_w_in=v_mix_w_in, v_hgrn_lb=v_hgrn_lb, v_hgrn_g=v_hgrn_g, v_hgrn_w_o=v_hgrn_w_o, v_conv_w=v_conv_w, v_conv_b=v_conv_b, v_conv_ln_g=v_conv_ln_g, v_conv_ln_b=v_conv_ln_b, v_conv_w_o=v_conv_w_o, v_mix_w_out=v_mix_w_out, v_norm_ffn2=v_norm_ffn2, v_ffn2_w_in=v_ffn2_w_in, v_ffn2_w_out=v_ffn2_w_out, v_norm_final=v_norm_final)
    weights = {n: given[n] for n in TWIN_WEIGHTS}
    shared = {n: given[n] for n in SHARED_INPUTS}
    per_example = {n: given[n] for n in ['x', 'c']}
    grad_fn = _jax.value_and_grad(_loss, argnums=(0, 1))

    def one_microbatch(ex, loss_target):
        ex = dict(ex)
        diff = ex.pop(TWIN_DIFF_INPUT)
        return grad_fn(weights, diff, {**shared, **ex}, loss_target)

    if N_MICROBATCH == 1:
        loss, (grad_w, grad_x) = one_microbatch(per_example, given["loss_target"])
    else:
        def body(carry, xs):
            loss_sum, grad_sum = carry
            l_k, (gw_k, gx_k) = one_microbatch(xs[0], xs[1])
            with _jax.named_scope("update"):
                return (loss_sum + l_k, _jax.tree.map(_jnp.add, grad_sum, gw_k)), gx_k

        init = (_jnp.zeros((), _jnp.float32), _jax.tree.map(_jnp.zeros_like, weights))
        (loss, grad_w), grad_x = _jax.lax.scan(body, init, (per_example, given["loss_target"]))
    with _jax.named_scope("update"):
        delta_w, new_m, new_v = {}, {}, {}
        for n in TWIN_WEIGHTS:
            delta_w[n], new_m[n], new_v[n] = _adamw(weights[n], grad_w[n], given["m_" + n], given["v_" + n])
    return (loss, grad_x, *[grad_w[n] for n in TWIN_WEIGHTS], *[delta_w[n] for n in TWIN_WEIGHTS],
            *[new_m[n] for n in TWIN_WEIGHTS], *[new_v[n] for n in TWIN_WEIGHTS])
```

```python
import functools

import jax
import jax.numpy as jnp
from jax import lax
from jax.experimental import pallas as pl
from jax.experimental.pallas import tpu as pltpu

F32 = jnp.float32
MM = jnp.bfloat16
ACT = jnp.bfloat16

D = 1024
D_FF = 2816
HEADS = 8
HD = 128
CHUNK = 64
SUB = 16
NSUB = CHUNK // SUB
CONV_K = 31
HALO = 32
EPS = 1e-6
N_DEV = 8
NEG = -1e30
Q_SCALE = HD ** -0.5

ADAM_LR = 0.001
ADAM_B1 = 0.9
ADAM_B2 = 0.999
ADAM_EPS = 1e-08
ADAM_WD = 0.01
ADAM_STEP = 10

VMEM_LIMIT = 60 * 1024 * 1024
MESH = pl.DeviceIdType.MESH


def _cparams(n_axes):
    return pltpu.CompilerParams(dimension_semantics=("arbitrary",) * n_axes, vmem_limit_bytes=VMEM_LIMIT)


def _mm(a, b):
    return lax.dot_general(a.astype(MM), b.astype(MM), (((1,), (0,)), ((), ())), preferred_element_type=F32)


def _mm_nt(a, b):
    return lax.dot_general(a.astype(MM), b.astype(MM), (((1,), (1,)), ((), ())), preferred_element_type=F32)


def _mm_tn(a, b):
    return lax.dot_general(a.astype(MM), b.astype(MM), (((0,), (0,)), ((), ())), preferred_element_type=F32)


def _sig(x):
    return 1.0 / (1.0 + jnp.exp(-x))


def _colsum(x):
    return jnp.sum(x, axis=0, keepdims=True)


def _rowmean(x):
    return jnp.mean(x, axis=-1, keepdims=True)


def _modnorm_fwd(xv, g, sh, sc):
    r = lax.rsqrt(_rowmean(xv * xv) + EPS)
    xh = xv * r
    n = xh * g
    return n * (1.0 + sc) + sh, xh, n, r


def _modnorm_bwd(dh, xh, n, r, g, sc):
    dsc = _colsum(dh * n)
    dsh = _colsum(dh)
    dn = dh * (1.0 + sc)
    dg = _colsum(dn * xh)
    dxh = dn * g
    dx = r * (dxh - xh * _rowmean(dxh * xh))
    return dx, dsh, dsc, dg


def _ffn_fwd(x, mod, mo, gnorm, w_in, w_out, res, name):
    T = x.shape[0]
    tm = min(512, T)
    tn = D_FF // 2
    nj = D_FF // tn

    def body(x_ref, mod_ref, g_ref, wa_ref, wb_ref, wo_ref, xo_ref, a_ref, b_ref, f_ref, h_ref, acc_scr):
        j = pl.program_id(1)

        @pl.when(j == 0)
        def _():
            h, _, _, _ = _modnorm_fwd(x_ref[...], g_ref[...], mod_ref[mo:mo + 1, :], mod_ref[mo + 1:mo + 2, :])
            h_ref[...] = h.astype(ACT)
            acc_scr[...] = jnp.zeros_like(acc_scr)

        h = h_ref[...]
        a = _mm(h, wa_ref[...])
        b = _mm(h, wb_ref[...])
        a_ref[...] = a.astype(ACT)
        b_ref[...] = b.astype(ACT)
        s = a * _sig(a) * b
        acc_scr[...] += _mm(s, wo_ref[...])

        @pl.when(j == nj - 1)
        def _():
            f = acc_scr[...]
            f_ref[...] = f
            xo_ref[...] = x_ref[...] + res * mod_ref[mo + 2:mo + 3, :] * f

    return pl.pallas_call(
        body, name=name, grid=(T // tm, nj),
        in_specs=[
            pl.BlockSpec((tm, D), lambda i, j: (i, 0)),
            pl.BlockSpec((9, D), lambda i, j: (0, 0)),
            pl.BlockSpec((1, D), lambda i, j: (0, 0)),
            pl.BlockSpec((D, tn), lambda i, j: (0, j)),
            pl.BlockSpec((D, tn), lambda i, j: (0, j + nj)),
            pl.BlockSpec((tn, D), lambda i, j: (j, 0)),
        ],
        out_specs=[
            pl.BlockSpec((tm, D), lambda i, j: (i, 0)),
            pl.BlockSpec((tm, tn), lambda i, j: (i, j)),
            pl.BlockSpec((tm, tn), lambda i, j: (i, j)),
            pl.BlockSpec((tm, D), lambda i, j: (i, 0)),
            pl.BlockSpec((tm, D), lambda i, j: (i, 0)),
        ],
        out_shape=[
            jax.ShapeDtypeStruct((T, D), F32),
            jax.ShapeDtypeStruct((T, D_FF), ACT),
            jax.ShapeDtypeStruct((T, D_FF), ACT),
            jax.ShapeDtypeStruct((T, D), F32),
            jax.ShapeDtypeStruct((T, D), ACT),
        ],
        scratch_shapes=[pltpu.VMEM((tm, D), F32)],
        compiler_params=_cparams(2),
    )(x, mod, gnorm, w_in, w_in, w_out)


def _ffn_bwd(x, h, dxo, f, a, b, mod, mo, gnorm, w_in, w_out, res, name):
    T = x.shape[0]
    tm = min(512, T)
    ni = T // tm
    tn = 256
    nj = D_FF // tn

    def body(x_ref, h_ref, dxo_ref, f_ref, a_ref, b_ref, mod_ref, g_ref, wa_ref, wb_ref, wo_ref,
             dx_ref, dwa_ref, dwb_ref, dwo_ref, sm_ref, dh_scr):
        j = pl.program_id(0)
        i = pl.program_id(1)
        gate = mod_ref[mo + 2:mo + 3, :]
        hb = h_ref[...]
        dxo_v = dxo_ref[...]
        df = (res * gate * dxo_v).astype(MM)
        av = a_ref[...].astype(F32)
        bv = b_ref[...].astype(F32)
        sg = _sig(av)
        sa = av * sg
        s = (sa * bv).astype(MM)
        ds = _mm_nt(df, wo_ref[...])
        da = (ds * bv * sg * (1.0 + av * (1.0 - sg))).astype(MM)
        db = (ds * sa).astype(MM)

        @pl.when(i == 0)
        def _():
            dwa_ref[...] = jnp.zeros_like(dwa_ref)
            dwb_ref[...] = jnp.zeros_like(dwb_ref)
            dwo_ref[...] = jnp.zeros_like(dwo_ref)

        dwo_ref[...] += _mm_tn(s, df)
        dwa_ref[...] += _mm_tn(hb, da)
        dwb_ref[...] += _mm_tn(hb, db)
        dh_part = _mm_nt(da, wa_ref[...]) + _mm_nt(db, wb_ref[...])

        @pl.when(j == 0)
        def _():
            dh_scr[i] = dh_part

        @pl.when(j > 0)
        def _():
            dh_scr[i] += dh_part

        @pl.when((j == 0) & (i == 0))
        def _():
            sm_ref[...] = jnp.zeros_like(sm_ref)

        @pl.when(j == nj - 1)
        def _():
            sc = mod_ref[mo + 1:mo + 2, :]
            _, xh, n, r = _modnorm_fwd(x_ref[...], g_ref[...], mod_ref[mo:mo + 1, :], sc)
            dxn, dsh, dsc, dg = _modnorm_bwd(dh_scr[i], xh, n, r, g_ref[...], sc)
            dx_ref[...] = dxo_v + dxn
            sm_ref[0:1, :] += dsh
            sm_ref[1:2, :] += dsc
            sm_ref[2:3, :] += _colsum(dxo_v * f_ref[...]) * res
            sm_ref[3:4, :] += dg

    return pl.pallas_call(
        body, name=name, grid=(nj, ni),
        in_specs=[
            pl.BlockSpec((tm, D), lambda j, i: (i, 0)),
            pl.BlockSpec((tm, D), lambda j, i: (i, 0)),
            pl.BlockSpec((tm, D), lambda j, i: (i, 0)),
            pl.BlockSpec((tm, D), lambda j, i: (i, 0)),
            pl.BlockSpec((tm, tn), lambda j, i: (i, j)),
            pl.BlockSpec((tm, tn), lambda j, i: (i, j)),
            pl.BlockSpec((9, D), lambda j, i: (0, 0)),
            pl.BlockSpec((1, D), lambda j, i: (0, 0)),
            pl.BlockSpec((D, tn), lambda j, i: (0, j)),
            pl.BlockSpec((D, tn), lambda j, i: (0, j + nj)),
            pl.BlockSpec((tn, D), lambda j, i: (j, 0)),
        ],
        out_specs=[
            pl.BlockSpec((tm, D), lambda j, i: (jnp.where(j == nj - 1, i, 0), 0)),
            pl.BlockSpec((D, tn), lambda j, i: (0, j)),
            pl.BlockSpec((D, tn), lambda j, i: (0, j)),
            pl.BlockSpec((tn, D), lambda j, i: (j, 0)),
            pl.BlockSpec((8, D), lambda j, i: (0, 0)),
        ],
        out_shape=[
            jax.ShapeDtypeStruct((T, D), F32),
            jax.ShapeDtypeStruct((D, D_FF), F32),
            jax.ShapeDtypeStruct((D, D_FF), F32),
            jax.ShapeDtypeStruct((D_FF, D), F32),
            jax.ShapeDtypeStruct((8, D), F32),
        ],
        scratch_shapes=[pltpu.VMEM((ni, tm, D), F32)],
        compiler_params=_cparams(2),
    )(x, h, dxo, f, a, b, mod, gnorm, w_in, w_in, w_out)


def _head(x, target, gfin):
    T = x.shape[0]
    tm = min(512, T)
    ni = T // tm

    def body(x_ref, t_ref, g_ref, dx_ref, sm_ref):
        i = pl.program_id(0)

        @pl.when(i == 0)
        def _():
            sm_ref[...] = jnp.zeros_like(sm_ref)

        xv = x_ref[...]
        g = g_ref[...]
        r = lax.rsqrt(_rowmean(xv * xv) + EPS)
        xh = xv * r
        e = xh * g - t_ref[...]
        sm_ref[1:2, :] += _colsum(e * e) * (0.5 / D)
        dy = e * (1.0 / D)
        sm_ref[0:1, :] += _colsum(dy * xh)
        dxh = dy * g
        dx_ref[...] = r * (dxh - xh * _rowmean(dxh * xh))

        @pl.when(i == ni - 1)
        def _():
            sm_ref[1:2, :] = jnp.broadcast_to(jnp.sum(sm_ref[1:2, :], axis=-1, keepdims=True), (1, D))

    return pl.pallas_call(
        body, name="head_loss", grid=(ni,),
        in_specs=[pl.BlockSpec((tm, D), lambda i: (i, 0)), pl.BlockSpec((tm, D), lambda i: (i, 0)),
                  pl.BlockSpec((1, D), lambda i: (0, 0))],
        out_specs=[pl.BlockSpec((tm, D), lambda i: (i, 0)), pl.BlockSpec((8, D), lambda i: (0, 0))],
        out_shape=[jax.ShapeDtypeStruct((T, D), F32), jax.ShapeDtypeStruct((8, D), F32)],
        compiler_params=_cparams(1),
    )(x, target, gfin)


def _mixin_fwd(x, mod, mo, gnorm, w):
    T = x.shape[0]
    tm = min(1024, T)

    def body(x_ref, mod_ref, g_ref, w_ref, p_ref, h_ref):
        @pl.when(pl.program_id(1) == 0)
        def _():
            h, _, _, _ = _modnorm_fwd(x_ref[...], g_ref[...], mod_ref[mo:mo + 1, :], mod_ref[mo + 1:mo + 2, :])
            h_ref[...] = h.astype(ACT)

        p_ref[0] = _mm(h_ref[...], w_ref[0])

    return pl.pallas_call(
        body, name="mixin_fwd", grid=(T // tm, 8),
        in_specs=[pl.BlockSpec((tm, D), lambda i, k: (i, 0)), pl.BlockSpec((9, D), lambda i, k: (0, 0)),
                  pl.BlockSpec((1, D), lambda i, k: (0, 0)), pl.BlockSpec((1, D, D), lambda i, k: (k, 0, 0))],
        out_specs=[pl.BlockSpec((1, tm, D), lambda i, k: (k, i, 0)), pl.BlockSpec((tm, D), lambda i, k: (i, 0))],
        out_shape=[jax.ShapeDtypeStruct((8, T, D), F32), jax.ShapeDtypeStruct((T, D), ACT)],
        compiler_params=_cparams(2),
    )(x, mod, gnorm, w)


def _mixin_bwd(x, h, dxo, dp, mod, mo, gnorm, w):
    T = x.shape[0]
    tm = min(512, T)
    ni = T // tm

    def body(x_ref, h_ref, dxo_ref, dp_ref, mod_ref, g_ref, w_ref, dx_ref, dw_ref, sm_ref, dh_scr):
        k = pl.program_id(0)
        i = pl.program_id(1)
        dpk = dp_ref[0].astype(MM)

        @pl.when(i == 0)
        def _():
            dw_ref[...] = jnp.zeros_like(dw_ref)

        dw_ref[0] += _mm_tn(h_ref[...], dpk)
        dh_part = _mm_nt(dpk, w_ref[0])

        @pl.when(k == 0)
        def _():
            dh_scr[i] = dh_part

        @pl.when(k > 0)
        def _():
            dh_scr[i] += dh_part

        @pl.when((k == 0) & (i == 0))
        def _():
            sm_ref[...] = jnp.zeros_like(sm_ref)

        @pl.when(k == 7)
        def _():
            sc = mod_ref[mo + 1:mo + 2, :]
            _, xh, n, r = _modnorm_fwd(x_ref[...], g_ref[...], mod_ref[mo:mo + 1, :], sc)
            dxn, dsh, dsc, dg = _modnorm_bwd(dh_scr[i], xh, n, r, g_ref[...], sc)
            dx_ref[...] = dxo_ref[...] + dxn
            sm_ref[0:1, :] += dsh
            sm_ref[1:2, :] += dsc
            sm_ref[3:4, :] += dg

    return pl.pallas_call(
        body, name="mixin_bwd", grid=(8, ni),
        in_specs=[pl.BlockSpec((tm, D), lambda k, i: (i, 0)), pl.BlockSpec((tm, D), lambda k, i: (i, 0)),
                  pl.BlockSpec((tm, D), lambda k, i: (i, 0)),
                  pl.BlockSpec((1, tm, D), lambda k, i: (k, i, 0)), pl.BlockSpec((9, D), lambda k, i: (0, 0)),
                  pl.BlockSpec((1, D), lambda k, i: (0, 0)), pl.BlockSpec((1, D, D), lambda k, i: (k, 0, 0))],
        out_specs=[pl.BlockSpec((tm, D), lambda k, i: (jnp.where(k == 7, i, 0), 0)),
                   pl.BlockSpec((1, D, D), lambda k, i: (k, 0, 0)),
                   pl.BlockSpec((8, D), lambda k, i: (0, 0))],
        out_shape=[jax.ShapeDtypeStruct((T, D), F32), jax.ShapeDtypeStruct((8, D, D), F32),
                   jax.ShapeDtypeStruct((8, D), F32)],
        scratch_shapes=[pltpu.VMEM((ni, tm, D), F32)],
        compiler_params=_cparams(2),
    )(x, h, dxo, dp, mod, gnorm, w)


def _hgrn_consts():
    rows = jnp.arange(SUB * HD) // HD
    e = (rows[:, None] == jnp.arange(HD)[None, :]).astype(MM)
    return e, e.T


def _rows_bcast(ref, cb, first, n):
    parts = [jnp.broadcast_to(ref[pl.ds(c * CHUNK + first, 1), :], (n, HD)) for c in range(cb // CHUNK)]
    return jnp.concatenate(parts, axis=0)


def _hgrn_pre(qr, fr, lb_ref, b_scr, cb):
    z = lb_ref[...]
    lb = _sig(z[0:1, :] - z[1:2, :])
    sq = _sig(qr)
    q = qr * sq * Q_SCALE
    sf = _sig(fr)
    fg = lb + (1.0 - lb) * sf
    lf = jnp.log(fg)
    k = 1.0 - fg
    tl = lax.broadcasted_iota(jnp.int32, (cb, HD), 0) % CHUNK
    bc = lf
    sh = 1
    while sh < CHUNK:
        bc = bc + jnp.where(tl >= sh, pltpu.roll(bc, sh, 0), 0.0)
        sh *= 2
    b_scr[...] = bc
    bl = _rows_bcast(b_scr, cb, CHUNK - 1, CHUNK)
    br = [None] + [_rows_bcast(b_scr, cb, SUB * i - 1, CHUNK) for i in range(1, NSUB)]
    sb = tl // SUB
    bref = jnp.where(sb == 0, bc, jnp.where(sb == 1, br[1], jnp.where(sb == 2, br[2], br[3])))
    eb = jnp.exp(bc)
    ekd = jnp.exp(bl - bc)
    eqo = jnp.exp(bc - bref)
    eko = [None] + [jnp.exp(jnp.where(tl < SUB * i, br[i] - bc, NEG)) for i in range(1, NSUB)]
    return dict(lb=lb, sq=sq, q=q, sf=sf, fg=fg, k=k, tl=tl, sb=sb, b=bc, bl=bl, eb=eb, ekd=ekd, eqo=eqo,
                eko=eko, qe=q * eb, kd=k * ekd, qo=q * eqo, ko=[None] + [k * eko[i] for i in range(1, NSUB)])


def _pad_rows(x):
    return jnp.concatenate([x, jnp.zeros_like(x)], axis=0)


def _by_subblock(sbc, parts):
    out = jnp.zeros_like(parts[1])
    for i in range(1, NSUB):
        out = jnp.where(sbc == i, parts[i], out)
    return out


def _hgrn_fwd(p, hgrn_lb, hgrn_g):
    T = p.shape[1]
    cb = min(512, T)
    nch = cb // CHUNK
    ncb = T // cb
    e_mat, _ = _hgrn_consts()

    def body(p_ref, lb_ref, g_ref, e_ref, o_ref, oa_ref, a_ref, s_ref, st_scr, q_scr, k_scr, b_scr, z_scr):
        @pl.when(pl.program_id(1) == 0)
        def _():
            st_scr[...] = jnp.zeros_like(st_scr)

        v = p_ref[2]
        og = p_ref[3]
        pre = _hgrn_pre(p_ref[0], p_ref[1], lb_ref, b_scr, cb)
        q_scr[...] = pre["q"]
        k_scr[...] = pre["k"]
        ti = lax.broadcasted_iota(jnp.int32, (SUB, HD), 0)

        def zbody(c, carry):
            for i in range(NSUB):
                r0 = pl.multiple_of(c * CHUNK + SUB * i, SUB)
                qi = q_scr[pl.ds(r0, SUB), :]
                bi = b_scr[pl.ds(r0, SUB), :]
                for s in range(SUB):
                    krow = k_scr[pl.ds(r0 + s, 1), :]
                    brow = b_scr[pl.ds(r0 + s, 1), :]
                    zz = qi * krow * jnp.exp(jnp.where(ti >= s, bi - brow, NEG))
                    z_scr[i, pl.ds(pl.multiple_of(c * SUB, SUB), SUB), s * HD:(s + 1) * HD] = zz.astype(MM)
            return carry

        lax.fori_loop(0, nch, zbody, 0)
        adiag = [_mm(z_scr[i], e_ref[...]) for i in range(NSUB)]
        sbc = lax.broadcasted_iota(jnp.int32, (CHUNK, HD), 0) // SUB
        o_parts = []
        for c in range(nch):
            rs = slice(c * CHUNK, (c + 1) * CHUNK)
            qo_c = pre["qo"][rs]
            offs = [None] + [_mm_nt(qo_c, _pad_rows(pre["ko"][i][rs])) for i in range(1, NSUB)]
            a_c = _by_subblock(sbc, offs)
            dparts = []
            for i in range(NSUB):
                blk = adiag[i][c * SUB:(c + 1) * SUB]
                dparts.append(blk if i == 0 else pltpu.roll(blk, SUB * i, 1))
            a_c = a_c + jnp.concatenate(dparts, axis=0)
            a_ref[0, rs, :] = a_c
            st = st_scr[...]
            s_ref[0, c] = st
            o_c = _mm(a_c, _pad_rows(v[rs])) + _mm_nt(pre["qe"][rs], st)
            st_scr[...] = st * jnp.exp(b_scr[pl.ds(c * CHUNK + CHUNK - 1, 1), :]) + _mm_tn(v[rs], pre["kd"][rs])
            o_parts.append(o_c)
        o = jnp.concatenate(o_parts, axis=0)
        o_ref[...] = o
        on = o * lax.rsqrt(_rowmean(o * o) + EPS) * g_ref[...]
        oa_ref[...] = (on * og * _sig(og)).astype(ACT)

    return pl.pallas_call(
        body, name="hgrn_fwd", grid=(HEADS, ncb),
        in_specs=[pl.BlockSpec((4, cb, HD), lambda h, c: (0, c, h)),
                  pl.BlockSpec((2, HD), lambda h, c: (0, h)),
                  pl.BlockSpec((1, HD), lambda h, c: (0, h)),
                  pl.BlockSpec((SUB * HD, HD), lambda h, c: (0, 0))],
        out_specs=[pl.BlockSpec((cb, HD), lambda h, c: (c, h)),
                   pl.BlockSpec((cb, HD), lambda h, c: (c, h)),
                   pl.BlockSpec((1, cb, HD), lambda h, c: (h, c, 0)),
                   pl.BlockSpec((1, nch, HD, HD), lambda h, c: (h, c, 0, 0))],
        out_shape=[jax.ShapeDtypeStruct((T, D), F32), jax.ShapeDtypeStruct((T, D), ACT),
                   jax.ShapeDtypeStruct((HEADS, T, HD), F32),
                   jax.ShapeDtypeStruct((HEADS, T // CHUNK, HD, HD), F32)],
        scratch_shapes=[pltpu.VMEM((HD, HD), F32), pltpu.VMEM((cb, HD), F32), pltpu.VMEM((cb, HD), F32),
                        pltpu.VMEM((cb, HD), F32), pltpu.VMEM((NSUB, nch * SUB, SUB * HD), MM)],
        compiler_params=_cparams(2),
    )(p, hgrn_lb, hgrn_g, e_mat)


def _hgrn_bwd(p, o, a_all, s_all, doa, hgrn_lb, hgrn_g, dp):
    T = p.shape[1]
    cb = min(512, T)
    nch = cb // CHUNK
    ncb = T // cb
    _, et_mat = _hgrn_consts()

    def body(p_ref, o_ref, a_ref, s_ref, doa_ref, lb_ref, g_ref, et_ref, dp_in, dp_ref, sm_ref,
             dst_scr, q_scr, k_scr, b_scr, x_scr, dqd_scr, dkd_scr):
        del dp_in

        @pl.when(pl.program_id(1) == 0)
        def _():
            dst_scr[...] = jnp.zeros_like(dst_scr)
            sm_ref[...] = jnp.zeros_like(sm_ref)

        qr = p_ref[0]
        v = p_ref[2]
        og = p_ref[3]
        pre = _hgrn_pre(qr, p_ref[1], lb_ref, b_scr, cb)
        q, k = pre["q"], pre["k"]
        q_scr[...] = q
        k_scr[...] = k
        g = g_ref[...]
        ov = o_ref[...]
        r = lax.rsqrt(_rowmean(ov * ov) + EPS)
        oh = ov * r
        sgo = _sig(og)
        doa_v = doa_ref[...]
        don = doa_v * og * sgo
        dog = doa_v * oh * g * sgo * (1.0 + og * (1.0 - sgo))
        sm_ref[1:2, :] += _colsum(don * oh)
        doh = don * g
        do = r * (doh - oh * _rowmean(doh * oh))

        sbc = lax.broadcasted_iota(jnp.int32, (CHUNK, HD), 0) // SUB
        row_i = lax.broadcasted_iota(jnp.int32, (CHUNK, HD), 0)
        lane_i = lax.broadcasted_iota(jnp.int32, (CHUNK, HD), 1)
        causal = lane_i <= row_i
        da_parts, dv_parts, dqoff_parts, dkoff_parts = [], [], [], []
        for c in range(nch):
            rs = slice(c * CHUNK, (c + 1) * CHUNK)
            do_c = do[rs]
            da_c = jnp.where(causal, _mm_nt(do_c, _pad_rows(v[rs])), 0.0)
            da_parts.append(da_c)
            dv_parts.append(_mm_tn(a_ref[0, rs, :], do_c)[:CHUNK])
            qo_c = pre["qo"][rs]
            dqoff_parts.append(_by_subblock(
                sbc, [None] + [_mm(da_c, _pad_rows(pre["ko"][i][rs])) for i in range(1, NSUB)]))
            dko = jnp.zeros((CHUNK, HD), F32)
            for i in range(1, NSUB):
                dko = dko + pre["eko"][i][rs] * _mm_tn(jnp.where(sbc == i, da_c, 0.0), qo_c)[:CHUNK]
            dkoff_parts.append(dko)
        for i in range(NSUB):
            rows = []
            for c in range(nch):
                blk = da_parts[c][SUB * i:SUB * (i + 1)]
                rows.append(blk if i == 0 else pltpu.roll(blk, HD - SUB * i, 1))
            x_scr[i] = _mm(jnp.concatenate(rows, axis=0), et_ref[...])
        ti = lax.broadcasted_iota(jnp.int32, (SUB, HD), 0)

        def dbody(c, carry):
            for i in range(NSUB):
                r0 = pl.multiple_of(c * CHUNK + SUB * i, SUB)
                qi = q_scr[pl.ds(r0, SUB), :]
                bi = b_scr[pl.ds(r0, SUB), :]
                dq_acc = jnp.zeros((SUB, HD), F32)
                dk_acc = jnp.zeros((SUB, HD), F32)
                for s in range(SUB):
                    krow = k_scr[pl.ds(r0 + s, 1), :]
                    brow = b_scr[pl.ds(r0 + s, 1), :]
                    xs = x_scr[i, pl.ds(pl.multiple_of(c * SUB, SUB), SUB), s * HD:(s + 1) * HD]
                    w = xs * jnp.exp(jnp.where(ti >= s, bi - brow, NEG))
                    dq_acc = dq_acc + w * krow
                    dk_acc = jnp.where(ti == s, _colsum(w * qi), dk_acc)
                dqd_scr[pl.ds(r0, SUB), :] = dq_acc
                dkd_scr[pl.ds(r0, SUB), :] = dk_acc
            return carry

        lax.fori_loop(0, nch, dbody, 0)
        dqe_parts, dkdec_parts, dvi_parts, debl_parts = [None] * nch, [None] * nch, [None] * nch, [None] * nch
        for c in reversed(range(nch)):
            rs = slice(c * CHUNK, (c + 1) * CHUNK)
            st = s_ref[0, c]
            dst = dst_scr[...]
            do_c = do[rs]
            dqe_parts[c] = _mm(do_c, st)
            dkdec_parts[c] = _mm(v[rs], dst)
            dvi_parts[c] = _mm_nt(pre["kd"][rs], dst)
            debl_parts[c] = _colsum(dst * st)
            dst_scr[...] = dst * jnp.exp(b_scr[pl.ds(c * CHUNK + CHUNK - 1, 1), :]) + _mm_tn(do_c, pre["qe"][rs])
        dqe = jnp.concatenate(dqe_parts, axis=0)
        dkdec = jnp.concatenate(dkdec_parts, axis=0)
        dq_tot = jnp.concatenate(dqoff_parts, axis=0) * pre["eqo"] + dqd_scr[...] + dqe * pre["eb"]
        dk_inter = dkdec * pre["ekd"]
        dk_tot = jnp.concatenate(dkoff_parts, axis=0) + dkd_scr[...] + dk_inter
        db = q * dq_tot - k * dk_tot
        kdk = k * dk_inter
        dbl = jnp.concatenate(
            [jnp.broadcast_to(jnp.exp(b_scr[pl.ds(c * CHUNK + CHUNK - 1, 1), :]) * debl_parts[c]
                              + _colsum(kdk[c * CHUNK:(c + 1) * CHUNK]), (CHUNK, HD)) for c in range(nch)], axis=0)
        tl = pre["tl"]
        rc = db
        sh = 1
        while sh < CHUNK:
            rc = rc + jnp.where(tl + sh < CHUNK, pltpu.roll(rc, cb - sh, 0), 0.0)
            sh *= 2
        dlf = rc + dbl
        dfg = dlf / pre["fg"] - dk_tot
        sf = pre["sf"]
        lb = pre["lb"]
        sm_ref[0:1, :] += _colsum(dfg * (1.0 - sf))
        sq = pre["sq"]
        dp_ref[0] = dq_tot * Q_SCALE * sq * (1.0 + qr * (1.0 - sq))
        dp_ref[1] = dfg * (1.0 - lb) * sf * (1.0 - sf)
        dp_ref[2] = jnp.concatenate(dv_parts, axis=0) + jnp.concatenate(dvi_parts, axis=0)
        dp_ref[3] = dog

    rev = lambda c: ncb - 1 - c
    return pl.pallas_call(
        body, name="hgrn_bwd", grid=(HEADS, ncb),
        in_specs=[pl.BlockSpec((4, cb, HD), lambda h, c: (0, rev(c), h)),
                  pl.BlockSpec((cb, HD), lambda h, c: (rev(c), h)),
                  pl.BlockSpec((1, cb, HD), lambda h, c: (h, rev(c), 0)),
                  pl.BlockSpec((1, nch, HD, HD), lambda h, c: (h, rev(c), 0, 0)),
                  pl.BlockSpec((cb, HD), lambda h, c: (rev(c), h)),
                  pl.BlockSpec((2, HD), lambda h, c: (0, h)),
                  pl.BlockSpec((1, HD), lambda h, c: (0, h)),
                  pl.BlockSpec((HD, SUB * HD), lambda h, c: (0, 0)),
                  pl.BlockSpec(memory_space=pl.ANY)],
        out_specs=[pl.BlockSpec((4, cb, HD), lambda h, c: (0, rev(c), h)),
                   pl.BlockSpec((8, HD), lambda h, c: (0, h))],
        out_shape=[jax.ShapeDtypeStruct(dp.shape, F32), jax.ShapeDtypeStruct((8, D), F32)],
        input_output_aliases={8: 0},
        scratch_shapes=[pltpu.VMEM((HD, HD), F32), pltpu.VMEM((cb, HD), F32), pltpu.VMEM((cb, HD), F32),
                        pltpu.VMEM((cb, HD), F32), pltpu.VMEM((NSUB, nch * SUB, SUB * HD), F32),
                        pltpu.VMEM((cb, HD), F32), pltpu.VMEM((cb, HD), F32)],
        compiler_params=_cparams(2),
    )(p, o, a_all, s_all, doa, hgrn_lb, hgrn_g, et_mat, dp)


def _ln_fwd(u1, g, b):
    mu = _rowmean(u1)
    xc = u1 - mu
    rs = lax.rsqrt(_rowmean(xc * xc) + EPS)
    xh = xc * rs
    return xh * g + b, xh, rs


def _conv_fwd(p, cw, cb_, lng, lnb):
    T = p.shape[1]
    tm = min(512, T)

    def body(p_ref, cw_ref, cb_ref, g_ref, b_ref, u1_ref, u2_ref, buf):
        @pl.when(pl.program_id(0) == 0)
        def _():
            buf[0:HALO, :] = jnp.zeros((HALO, D), F32)

        buf[HALO:HALO + tm, :] = p_ref[0] * _sig(p_ref[1])
        acc = jnp.broadcast_to(cb_ref[...], (tm, D))
        for j in range(CONV_K):
            acc = acc + cw_ref[j:j + 1, :] * buf[pl.ds(HALO - (CONV_K - 1) + j, tm), :]
        u1_ref[...] = acc
        y, _, _ = _ln_fwd(acc, g_ref[...], b_ref[...])
        u2_ref[...] = (y * _sig(y)).astype(ACT)
        buf[0:HALO, :] = buf[tm:tm + HALO, :]

    return pl.pallas_call(
        body, name="conv_fwd", grid=(T // tm,),
        in_specs=[pl.BlockSpec((2, tm, D), lambda i: (2, i, 0)), pl.BlockSpec((HALO, D), lambda i: (0, 0)),
                  pl.BlockSpec((1, D), lambda i: (0, 0)), pl.BlockSpec((1, D), lambda i: (0, 0)),
                  pl.BlockSpec((1, D), lambda i: (0, 0))],
        out_specs=[pl.BlockSpec((tm, D), lambda i: (i, 0)), pl.BlockSpec((tm, D), lambda i: (i, 0))],
        out_shape=[jax.ShapeDtypeStruct((T, D), F32), jax.ShapeDtypeStruct((T, D), ACT)],
        scratch_shapes=[pltpu.VMEM((HALO + tm, D), F32)],
        compiler_params=_cparams(1),
    )(p, cw, cb_, lng, lnb)


def _conv_bwd(p, u1, du2, cw, lng, lnb, dp):
    T = p.shape[1]
    tm = min(512, T)
    ni = T // tm
    hb = tm // HALO

    def body(p_ref, ph_ref, u1_ref, du2_ref, cw_ref, g_ref, b_ref, dp_in, dp_ref, dcw_ref, sm_ref, ubuf, dbuf):
        del dp_in
        step = pl.program_id(0)

        @pl.when(step == 0)
        def _():
            dbuf[tm:tm + HALO, :] = jnp.zeros((HALO, D), F32)
            dcw_ref[...] = jnp.zeros_like(dcw_ref)
            sm_ref[...] = jnp.zeros_like(sm_ref)

        ua = p_ref[0]
        sgb = _sig(p_ref[1])
        halo = ph_ref[0] * _sig(ph_ref[1])
        ubuf[0:HALO, :] = jnp.where(step == ni - 1, 0.0, halo)
        ubuf[HALO:HALO + tm, :] = ua * sgb
        g = g_ref[...]
        y, xh, rs = _ln_fwd(u1_ref[...], g, b_ref[...])
        sy = _sig(y)
        dy = du2_ref[...] * sy * (1.0 + y * (1.0 - sy))
        sm_ref[1:2, :] += _colsum(dy * xh)
        sm_ref[2:3, :] += _colsum(dy)
        dxh = dy * g
        du1 = rs * (dxh - _rowmean(dxh) - xh * _rowmean(dxh * xh))
        sm_ref[0:1, :] += _colsum(du1)
        dbuf[0:tm, :] = du1
        du0 = jnp.zeros((tm, D), F32)
        for j in range(CONV_K):
            du0 = du0 + cw_ref[j:j + 1, :] * dbuf[pl.ds(CONV_K - 1 - j, tm), :]
            dcw_ref[j:j + 1, :] += _colsum(du1 * ubuf[pl.ds(HALO - (CONV_K - 1) + j, tm), :])
        dp_ref[0] = du0 * sgb
        dp_ref[1] = du0 * ua * sgb * (1.0 - sgb)
        dbuf[tm:tm + HALO, :] = dbuf[0:HALO, :]

    rev = lambda i: ni - 1 - i
    return pl.pallas_call(
        body, name="conv_bwd", grid=(ni,),
        in_specs=[pl.BlockSpec((2, tm, D), lambda i: (2, rev(i), 0)),
                  pl.BlockSpec((2, HALO, D), lambda i: (2, jnp.maximum(rev(i) * hb - 1, 0), 0)),
                  pl.BlockSpec((tm, D), lambda i: (rev(i), 0)), pl.BlockSpec((tm, D), lambda i: (rev(i), 0)),
                  pl.BlockSpec((HALO, D), lambda i: (0, 0)), pl.BlockSpec((1, D), lambda i: (0, 0)),
                  pl.BlockSpec((1, D), lambda i: (0, 0)), pl.BlockSpec(memory_space=pl.ANY)],
        out_specs=[pl.BlockSpec((2, tm, D), lambda i: (2, rev(i), 0)),
                   pl.BlockSpec((HALO, D), lambda i: (0, 0)), pl.BlockSpec((8, D), lambda i: (0, 0))],
        out_shape=[jax.ShapeDtypeStruct(dp.shape, F32), jax.ShapeDtypeStruct((HALO, D), F32),
                   jax.ShapeDtypeStruct((8, D), F32)],
        input_output_aliases={7: 0},
        scratch_shapes=[pltpu.VMEM((HALO + tm, D), F32), pltpu.VMEM((tm + HALO, D), F32)],
        compiler_params=_cparams(1),
    )(p, p, u1, du2, cw, lng, lnb, dp)


def _mixout_fwd(x, oa, u2, p, mod, mo, w_a, w_b, w_o):
    T = x.shape[0]
    tm = min(512, T)

    def body(x_ref, oa_ref, u2_ref, p_ref, mod_ref, wa_ref, wb_ref, wo_ref, xo_ref, ya_ref, yb_ref, mo_ref):
        ya = _mm(oa_ref[...], wa_ref[...])
        yb = _mm(u2_ref[...], wb_ref[...])
        ya_ref[...] = ya.astype(ACT)
        yb_ref[...] = yb.astype(ACT)
        merged = _sig(p_ref[0]) * ya + _sig(p_ref[1]) * yb
        out = _mm(merged, wo_ref[...])
        mo_ref[...] = out
        xo_ref[...] = x_ref[...] + mod_ref[mo + 2:mo + 3, :] * out

    tile = pl.BlockSpec((tm, D), lambda i: (i, 0))
    wspec = pl.BlockSpec((D, D), lambda i: (0, 0))
    return pl.pallas_call(
        body, name="mixout_fwd", grid=(T // tm,),
        in_specs=[tile, tile, tile, pl.BlockSpec((2, tm, D), lambda i: (3, i, 0)),
                  pl.BlockSpec((9, D), lambda i: (0, 0)), wspec, wspec, wspec],
        out_specs=[tile, tile, tile, tile],
        out_shape=[jax.ShapeDtypeStruct((T, D), F32), jax.ShapeDtypeStruct((T, D), ACT),
                   jax.ShapeDtypeStruct((T, D), ACT), jax.ShapeDtypeStruct((T, D), F32)],
        compiler_params=_cparams(1),
    )(x, oa, u2, p, mod, w_a, w_b, w_o)


def _mixout_bwd(dxo, oa, u2, ya, yb, mout, p, mod, mo, w_a, w_b, w_o):
    T = dxo.shape[0]
    tm = min(256, T)

    def body(dxo_ref, oa_ref, u2_ref, ya_ref, yb_ref, mo_ref, p_ref, mod_ref, wa_ref, wb_ref, wo_ref,
             dp_ref, doa_ref, du2_ref, dwa_ref, dwb_ref, dwo_ref, sm_ref):
        @pl.when(pl.program_id(0) == 0)
        def _():
            dwa_ref[...] = jnp.zeros_like(dwa_ref)
            dwb_ref[...] = jnp.zeros_like(dwb_ref)
            dwo_ref[...] = jnp.zeros_like(dwo_ref)
            sm_ref[...] = jnp.zeros_like(sm_ref)

        dxo_v = dxo_ref[...]
        sm_ref[2:3, :] += _colsum(dxo_v * mo_ref[...])
        dmo = (mod_ref[mo + 2:mo + 3, :] * dxo_v).astype(MM)
        ya = ya_ref[...].astype(F32)
        yb = yb_ref[...].astype(F32)
        sga = _sig(p_ref[0])
        sgb = _sig(p_ref[1])
        merged = (sga * ya + sgb * yb).astype(MM)
        dwo_ref[...] += _mm_tn(merged, dmo)
        dmg = _mm_nt(dmo, wo_ref[...])
        dp_ref[0] = dmg * ya * sga * (1.0 - sga)
        dp_ref[1] = dmg * yb * sgb * (1.0 - sgb)
        dya = (dmg * sga).astype(MM)
        dyb = (dmg * sgb).astype(MM)
        dwa_ref[...] += _mm_tn(oa_ref[...], dya)
        dwb_ref[...] += _mm_tn(u2_ref[...], dyb)
        doa_ref[...] = _mm_nt(dya, wa_ref[...])
        du2_ref[...] = _mm_nt(dyb, wb_ref[...])

    tile = pl.BlockSpec((tm, D), lambda i: (i, 0))
    wspec = pl.BlockSpec((D, D), lambda i: (0, 0))
    return pl.pallas_call(
        body, name="mixout_bwd", grid=(T // tm,),
        in_specs=[tile, tile, tile, tile, tile, tile, pl.BlockSpec((2, tm, D), lambda i: (3, i, 0)),
                  pl.BlockSpec((9, D), lambda i: (0, 0)), wspec, wspec, wspec],
        out_specs=[pl.BlockSpec((2, tm, D), lambda i: (3, i, 0)), tile, tile, wspec, wspec, wspec,
                   pl.BlockSpec((8, D), lambda i: (0, 0))],
        out_shape=[jax.ShapeDtypeStruct((8, T, D), F32), jax.ShapeDtypeStruct((T, D), F32),
                   jax.ShapeDtypeStruct((T, D), F32), jax.ShapeDtypeStruct((D, D), F32),
                   jax.ShapeDtypeStruct((D, D), F32), jax.ShapeDtypeStruct((D, D), F32),
                   jax.ShapeDtypeStruct((8, D), F32)],
        compiler_params=_cparams(1),
    )(dxo, oa, u2, ya, yb, mout, p, mod, w_a, w_b, w_o)


def _ada_fwd(cs_all, ada_w, ada_b_cols):
    def body(cs_ref, w_ref, b_ref, out_ref):
        out_ref[...] = jnp.dot(cs_ref[...], w_ref[...], preferred_element_type=F32,
                               precision=lax.Precision.HIGHEST) + b_ref[...]

    return pl.pallas_call(
        body, name="ada_fwd", out_shape=jax.ShapeDtypeStruct((N_DEV, ada_w.shape[1]), F32),
        compiler_params=pltpu.CompilerParams(vmem_limit_bytes=VMEM_LIMIT),
    )(cs_all, ada_w, ada_b_cols)


def _ada_wgrad(cs_all, dmod_cols):
    cs_t = jnp.pad(cs_all.T, ((0, 0), (0, HD - N_DEV)))
    dm = jnp.pad(dmod_cols, ((0, HD - N_DEV), (0, 0)))

    def body(cs_ref, d_ref, out_ref):
        out_ref[...] = jnp.dot(cs_ref[...], d_ref[...], preferred_element_type=F32,
                               precision=lax.Precision.HIGHEST)

    return pl.pallas_call(
        body, name="ada_wgrad", out_shape=jax.ShapeDtypeStruct((D, dmod_cols.shape[1]), F32),
        compiler_params=pltpu.CompilerParams(vmem_limit_bytes=VMEM_LIMIT),
    )(cs_t, dm)


def _adam_math(w, g, m, v):
    m2 = ADAM_B1 * m + (1.0 - ADAM_B1) * g
    v2 = ADAM_B2 * v + (1.0 - ADAM_B2) * (g * g)
    m_hat = m2 / (1.0 - ADAM_B1 ** ADAM_STEP)
    v_hat = v2 / (1.0 - ADAM_B2 ** ADAM_STEP)
    delta = -ADAM_LR * (m_hat / (jnp.sqrt(v_hat) + ADAM_EPS) + ADAM_WD * w)
    return delta, m2, v2


def _adamw(w, m, v, g, name):
    R, C = w.shape
    slots = g.ndim == 3
    tr = R
    for cand in (256, 176):
        if R % cand == 0 and R > cand:
            tr = cand
            break

    def body(w_ref, m_ref, v_ref, g_ref, go_ref, d_ref, mo_ref, vo_ref):
        if slots:
            gv = g_ref[0].astype(F32)
            for s in range(1, N_DEV):
                gv = gv + g_ref[s].astype(F32)
        else:
            gv = g_ref[...]
        go_ref[...] = gv
        d_ref[...], mo_ref[...], vo_ref[...] = _adam_math(w_ref[...], gv, m_ref[...], v_ref[...])

    tile = pl.BlockSpec((tr, C), lambda i: (i, 0))
    gspec = pl.BlockSpec((N_DEV, tr, C), lambda i: (0, i, 0)) if slots else tile
    sds = jax.ShapeDtypeStruct((R, C), F32)
    return pl.pallas_call(
        body, name=name, grid=(R // tr,), in_specs=[tile, tile, tile, gspec], out_specs=[tile] * 4,
        out_shape=[sds] * 4, compiler_params=_cparams(1),
    )(w, m, v, g)


def _sum_slots(pack):
    def body(p_ref, out_ref):
        acc = p_ref[0]
        for s in range(1, N_DEV):
            acc = acc + p_ref[s]
        out_ref[...] = acc

    return pl.pallas_call(body, name="sum_small", out_shape=jax.ShapeDtypeStruct(pack.shape[1:], F32))(pack)


def _me():
    return lax.axis_index("x"), lax.axis_index("y"), lax.axis_index("c")


def _peer(r):
    x, y, c = _me()
    px = 1 - x if r & 4 else x
    py = 1 - y if r & 2 else y
    pc = 1 - c if r & 1 else c
    return (px, py, pc), 4 * px + 2 * py + pc


def _allgather_small(x):
    R, C = x.shape

    def body(x_ref, out_ref, send_sems, recv_sems):
        mx, my, mc = _me()
        me = 4 * mx + 2 * my + mc
        mine = out_ref.at[pl.ds(pl.multiple_of(me * R, 8), R), :]
        copies = []
        for r in range(1, N_DEV):
            dev, _ = _peer(r)
            copies.append(pltpu.make_async_remote_copy(
                src_ref=x_ref, dst_ref=mine, send_sem=send_sems.at[r - 1], recv_sem=recv_sems.at[r - 1],
                device_id=dev, device_id_type=MESH))
        for cp in copies:
            cp.start()
        mine[...] = x_ref[...]
        for r in range(1, N_DEV):
            _, idx = _peer(r)
            theirs = out_ref.at[pl.ds(pl.multiple_of(idx * R, 8), R), :]
            pltpu.make_async_remote_copy(
                src_ref=x_ref, dst_ref=theirs, send_sem=send_sems.at[r - 1], recv_sem=recv_sems.at[r - 1],
                device_id=_peer(r)[0], device_id_type=MESH).wait_recv()
        for cp in copies:
            cp.wait_send()

    return pl.pallas_call(
        body, name="allgather_small_%dx%d" % (R, C),
        out_shape=jax.ShapeDtypeStruct((N_DEV * R, C), F32),
        in_specs=[pl.BlockSpec(memory_space=pltpu.VMEM)], out_specs=pl.BlockSpec(memory_space=pltpu.VMEM),
        scratch_shapes=[pltpu.SemaphoreType.DMA((N_DEV - 1,)), pltpu.SemaphoreType.DMA((N_DEV - 1,))],
    )(x)


def _exchange(arrays, gather, name):
    n = len(arrays)

    def body(*refs):
        ins, outs = refs[:n], refs[n:2 * n]
        send_sems, recv_sems, local_sems = refs[2 * n:]
        mx, my, mc = _me()
        me = 4 * mx + 2 * my + mc
        copies = []
        for a in range(n):
            own = pltpu.make_async_copy(ins[a] if gather else ins[a].at[me], outs[a].at[me], local_sems.at[a])
            own.start()
            copies.append(own)
            for r in range(1, N_DEV):
                dev, idx = _peer(r)
                k = a * (N_DEV - 1) + r - 1
                rc = pltpu.make_async_remote_copy(
                    src_ref=ins[a] if gather else ins[a].at[idx], dst_ref=outs[a].at[me],
                    send_sem=send_sems.at[k], recv_sem=recv_sems.at[k], device_id=dev, device_id_type=MESH)
                rc.start()
                copies.append(rc)
        for a in range(n):
            for r in range(1, N_DEV):
                dev, idx = _peer(r)
                k = a * (N_DEV - 1) + r - 1
                pltpu.make_async_remote_copy(
                    src_ref=ins[a] if gather else ins[a].at[idx], dst_ref=outs[a].at[idx],
                    send_sem=send_sems.at[k], recv_sem=recv_sems.at[k], device_id=dev,
                    device_id_type=MESH).wait_recv()
        for a in range(n):
            copies[a * N_DEV].wait()
            for r in range(1, N_DEV):
                copies[a * N_DEV + r].wait_send()

    out_shape = [jax.ShapeDtypeStruct(((N_DEV,) + a.shape) if gather else a.shape, a.dtype) for a in arrays]
    return pl.pallas_call(
        body, name=name, out_shape=out_shape,
        in_specs=[pl.BlockSpec(memory_space=pl.ANY)] * n, out_specs=[pl.BlockSpec(memory_space=pl.ANY)] * n,
        scratch_shapes=[pltpu.SemaphoreType.DMA((n * (N_DEV - 1),)), pltpu.SemaphoreType.DMA((n * (N_DEV - 1),)),
                        pltpu.SemaphoreType.DMA((n,))],
    )(*arrays)


def _local_step(x, target, mod, small, wts):
    x1, a1, b1, f1, h1 = _ffn_fwd(x, mod, 0, small["norm_ffn1"], wts["ffn1_w_in"], wts["ffn1_w_out"], 0.5,
                                  "ffn1_fwd")
    p, h2 = _mixin_fwd(x1, mod, 3, small["norm_mix"], wts["mix_w_in"])
    o, oa, a_all, s_all = _hgrn_fwd(p, small["hgrn_lb"], small["hgrn_g"])
    u1, u2 = _conv_fwd(p, wts["conv_w"], small["conv_b"], small["conv_ln_g"], small["conv_ln_b"])
    x2, ya, yb, mout = _mixout_fwd(x1, oa, u2, p, mod, 3, wts["hgrn_w_o"], wts["conv_w_o"], wts["mix_w_out"])
    x3, a3, b3, f3, h3 = _ffn_fwd(x2, mod, 6, small["norm_ffn2"], wts["ffn2_w_in"], wts["ffn2_w_out"], 0.5,
                                  "ffn2_fwd")
    dx3, sm_head = _head(x3, target, small["norm_final"])

    dx2, dw2_a, dw2_b, dw2_out, sm3 = _ffn_bwd(x2, h3, dx3, f3, a3, b3, mod, 6, small["norm_ffn2"],
                                               wts["ffn2_w_in"], wts["ffn2_w_out"], 0.5, "ffn2_bwd")
    dp, doa, du2, dwh_o, dwc_o, dwm_o, sm_mo = _mixout_bwd(dx2, oa, u2, ya, yb, mout, p, mod, 3, wts["hgrn_w_o"],
                                                          wts["conv_w_o"], wts["mix_w_out"])
    dp, dcw, sm_cv = _conv_bwd(p, u1, du2, wts["conv_w"], small["conv_ln_g"], small["conv_ln_b"], dp)
    dp, sm_hg = _hgrn_bwd(p, o, a_all, s_all, doa, small["hgrn_lb"], small["hgrn_g"], dp)
    dx1, dwm_in, sm2 = _mixin_bwd(x1, h2, dx2, dp, mod, 3, small["norm_mix"], wts["mix_w_in"])
    dx0, dw1_a, dw1_b, dw1_out, sm1 = _ffn_bwd(x, h1, dx1, f1, a1, b1, mod, 0, small["norm_ffn1"],
                                               wts["ffn1_w_in"], wts["ffn1_w_out"], 0.5, "ffn1_bwd")

    dmod = jnp.concatenate([sm1[0:3], sm2[0:2], sm_mo[2:3], sm3[0:3]], axis=0)
    gsmall = dict(norm_ffn1=sm1[3:4], norm_mix=sm2[3:4], lb0=sm_hg[0:1], hgrn_g=sm_hg[1:2], conv_b=sm_cv[0:1],
                  conv_ln_g=sm_cv[1:2], conv_ln_b=sm_cv[2:3], norm_ffn2=sm3[3:4], norm_final=sm_head[0:1])
    gw = dict(ffn1_w_in=(dw1_a, dw1_b), ffn1_w_out=dw1_out, mix_w_in=dwm_in, hgrn_w_o=dwh_o, conv_w=dcw[:CONV_K],
              conv_w_o=dwc_o, mix_w_out=dwm_o, ffn2_w_in=(dw2_a, dw2_b), ffn2_w_out=dw2_out)
    return sm_head[1, 0], dx0, dmod, gsmall, gw


SMALL_ORDER = ("norm_ffn1", "norm_mix", "lb0", "hgrn_g", "conv_b", "conv_ln_g", "conv_ln_b", "norm_ffn2",
               "norm_final")
PACK_ROWS = 24


def kernel(x, c, ada_w, ada_b, norm_ffn1, ffn1_w_in, ffn1_w_out, norm_mix, mix_w_in, hgrn_lb, hgrn_g, hgrn_w_o, conv_w, conv_b, conv_ln_g, conv_ln_b, conv_w_o, mix_w_out, norm_ffn2, ffn2_w_in, ffn2_w_out, norm_final, loss_target, m_ada_w, m_ada_b, m_norm_ffn1, m_ffn1_w_in, m_ffn1_w_out, m_norm_mix, m_mix_w_in, m_hgrn_lb, m_hgrn_g, m_hgrn_w_o, m_conv_w, m_conv_b, m_conv_ln_g, m_conv_ln_b, m_conv_w_o, m_mix_w_out, m_norm_ffn2, m_ffn2_w_in, m_ffn2_w_out, m_norm_final, v_ada_w, v_ada_b, v_norm_ffn1, v_ffn1_w_in, v_ffn1_w_out, v_norm_mix, v_mix_w_in, v_hgrn_lb, v_hgrn_g, v_hgrn_w_o, v_conv_w, v_conv_b, v_conv_ln_g, v_conv_ln_b, v_conv_w_o, v_mix_w_out, v_norm_ffn2, v_ffn2_w_in, v_ffn2_w_out, v_norm_final):
    mx, my, mc = _me()
    me = 4 * mx + 2 * my + mc
    ncol = ada_w.shape[2]

    cs = jnp.broadcast_to(c * jax.nn.sigmoid(c), (8, D))
    cs_all = _allgather_small(cs).reshape(N_DEV, 8, D)[:, 0, :]
    ada_b_cols = lax.dynamic_slice(ada_b, (0, me * ncol), (1, ncol))
    mod_cols = _ada_fwd(cs_all, ada_w[0], ada_b_cols)
    mod_all = _allgather_small(mod_cols).reshape(N_DEV, N_DEV, ncol)
    mod = lax.dynamic_index_in_dim(mod_all, me, axis=1, keepdims=False).reshape(9, D)

    names = ("ffn1_w_in", "ffn1_w_out", "mix_w_in", "hgrn_w_o", "conv_w_o", "mix_w_out", "ffn2_w_in", "ffn2_w_out")
    shards = dict(ffn1_w_in=ffn1_w_in, ffn1_w_out=ffn1_w_out, mix_w_in=mix_w_in, hgrn_w_o=hgrn_w_o,
                  conv_w_o=conv_w_o, mix_w_out=mix_w_out, ffn2_w_in=ffn2_w_in, ffn2_w_out=ffn2_w_out)
    gathered = _exchange([shards[n][0].astype(MM) for n in names] + [conv_w[0]], True, "allgather_weights")
    full = dict(zip(names, gathered[:-1]))
    wts = dict(
        ffn1_w_in=full["ffn1_w_in"].transpose(1, 0, 2).reshape(D, 2 * D_FF),
        ffn1_w_out=full["ffn1_w_out"].reshape(D_FF, D),
        mix_w_in=full["mix_w_in"],
        hgrn_w_o=full["hgrn_w_o"].reshape(D, D),
        conv_w_o=full["conv_w_o"].reshape(D, D),
        mix_w_out=full["mix_w_out"].reshape(D, D),
        ffn2_w_in=full["ffn2_w_in"].transpose(1, 0, 2).reshape(D, 2 * D_FF),
        ffn2_w_out=full["ffn2_w_out"].reshape(D_FF, D),
        conv_w=jnp.pad(gathered[-1].transpose(1, 0, 2).reshape(CONV_K, D), ((0, HALO - CONV_K), (0, 0))),
    )
    small = dict(norm_ffn1=norm_ffn1, norm_mix=norm_mix, hgrn_lb=hgrn_lb, hgrn_g=hgrn_g, conv_b=conv_b,
                 conv_ln_g=conv_ln_g, conv_ln_b=conv_ln_b, norm_ffn2=norm_ffn2, norm_final=norm_final.reshape(1, D))

    loss_local, dx, dmod, gsmall, gw = _local_step(x[0], loss_target[0], mod, small, wts)
    loss = lax.psum(loss_local, ("x", "y", "c"))

    pack = jnp.concatenate([dmod] + [gsmall[n] for n in SMALL_ORDER]
                           + [jnp.zeros((PACK_ROWS - 9 - len(SMALL_ORDER), D), F32)], axis=0)
    pack_all = _allgather_small(pack).reshape(N_DEV, PACK_ROWS, D)
    tot = _sum_slots(pack_all)
    gs = {n: tot[9 + i:10 + i] for i, n in enumerate(SMALL_ORDER)}
    dmod_all = pack_all[:, 0:9, :].reshape(N_DEV, 9 * D)
    g_ada_b = tot[0:9].reshape(1, 9 * D)
    g_ada_w = _ada_wgrad(cs_all, lax.dynamic_slice(dmod_all, (0, me * ncol), (N_DEV, ncol)))
    z = hgrn_lb.astype(F32)
    p0 = jax.nn.sigmoid(z[0:1] - z[1:2])
    dz0 = p0 * (1.0 - p0) * gs["lb0"]
    g_hgrn_lb = jnp.concatenate([dz0, -dz0], axis=0)

    half = N_DEV // 2
    w_in_shards = lambda ab: jnp.concatenate(
        [t.reshape(D, half, -1).transpose(1, 0, 2).astype(MM) for t in ab], axis=0)
    send = [
        w_in_shards(gw["ffn1_w_in"]),
        gw["ffn1_w_out"].reshape(N_DEV, -1, D).astype(MM),
        gw["mix_w_in"].astype(MM),
        gw["hgrn_w_o"].reshape(N_DEV, -1, D).astype(MM),
        gw["conv_w_o"].reshape(N_DEV, -1, D).astype(MM),
        gw["mix_w_out"].reshape(N_DEV, -1, D).astype(MM),
        w_in_shards(gw["ffn2_w_in"]),
        gw["ffn2_w_out"].reshape(N_DEV, -1, D).astype(MM),
        gw["conv_w"].reshape(CONV_K, N_DEV, -1).transpose(1, 0, 2),
    ]
    recv = dict(zip(names + ("conv_w",), _exchange(send, False, "scatter_grads")))

    res = {}
    res["ada_w"] = _adamw(ada_w[0], m_ada_w[0], v_ada_w[0], g_ada_w, "adamw_ada_w")
    big = dict(ffn1_w_in=(ffn1_w_in, m_ffn1_w_in, v_ffn1_w_in), ffn1_w_out=(ffn1_w_out, m_ffn1_w_out, v_ffn1_w_out),
               mix_w_in=(mix_w_in, m_mix_w_in, v_mix_w_in), hgrn_w_o=(hgrn_w_o, m_hgrn_w_o, v_hgrn_w_o),
               conv_w=(conv_w, m_conv_w, v_conv_w), conv_w_o=(conv_w_o, m_conv_w_o, v_conv_w_o),
               mix_w_out=(mix_w_out, m_mix_w_out, v_mix_w_out), ffn2_w_in=(ffn2_w_in, m_ffn2_w_in, v_ffn2_w_in),
               ffn2_w_out=(ffn2_w_out, m_ffn2_w_out, v_ffn2_w_out))
    for n, (w, m, v) in big.items():
        res[n] = _adamw(w[0], m[0], v[0], recv[n], "adamw_" + n)
    sm_names = ("ada_b", "norm_ffn1", "norm_mix", "hgrn_lb", "hgrn_g", "conv_b", "conv_ln_g", "conv_ln_b",
                "norm_ffn2", "norm_final")
    sm_w = dict(ada_b=(ada_b, m_ada_b, v_ada_b), norm_ffn1=(norm_ffn1, m_norm_ffn1, v_norm_ffn1),
                norm_mix=(norm_mix, m_norm_mix, v_norm_mix), hgrn_lb=(hgrn_lb, m_hgrn_lb, v_hgrn_lb),
                hgrn_g=(hgrn_g, m_hgrn_g, v_hgrn_g), conv_b=(conv_b, m_conv_b, v_conv_b),
                conv_ln_g=(conv_ln_g, m_conv_ln_g, v_conv_ln_g), conv_ln_b=(conv_ln_b, m_conv_ln_b, v_conv_ln_b),
                norm_ffn2=(norm_ffn2, m_norm_ffn2, v_norm_ffn2), norm_final=(norm_final, m_norm_final, v_norm_final))
    sm_g = dict(gs, ada_b=g_ada_b, hgrn_lb=g_hgrn_lb)
    rows = {n: sm_w[n][0].size // D for n in sm_names}
    n_rows = sum(rows.values())
    pad = (-n_rows) % 8
    stack = lambda parts: jnp.concatenate([q.reshape(-1, D) for q in parts] + [jnp.ones((pad, D), F32)], axis=0)
    st = _adamw(stack([sm_w[n][0] for n in sm_names]), stack([sm_w[n][1] for n in sm_names]),
                stack([sm_w[n][2] for n in sm_names]), stack([sm_g[n] for n in sm_names]), "adamw_small")
    off = 0
    for n in sm_names:
        res[n] = tuple(t[off:off + rows[n]].reshape(sm_w[n][0].shape) for t in st)
        off += rows[n]

    order = ("ada_w", "ada_b", "norm_ffn1", "ffn1_w_in", "ffn1_w_out", "norm_mix", "mix_w_in", "hgrn_lb", "hgrn_g",
             "hgrn_w_o", "conv_w", "conv_b", "conv_ln_g", "conv_ln_b", "conv_w_o", "mix_w_out", "norm_ffn2",
             "ffn2_w_in", "ffn2_w_out", "norm_final")
    lead = lambda n, t: t[None] if n in big or n == "ada_w" else t
    outs = [loss, dx[None]]
    for j in range(4):
        outs += [lead(n, res[n][j]) for n in order]
    return tuple(outs)
```

```python
import functools

import jax
import jax.numpy as jnp
from jax import lax
from jax.experimental import pallas as pl
from jax.experimental.pallas import tpu as pltpu

F32 = jnp.float32
MM = jnp.bfloat16
ACT = jnp.bfloat16

D = 1024
D_FF = 2816
HEADS = 8
HD = 128
CHUNK = 64
SUB = 16
NSUB = CHUNK // SUB
CONV_K = 31
HALO = 32
EPS = 1e-6
N_DEV = 8
NEG = -1e30
Q_SCALE = HD ** -0.5

ADAM_LR = 0.001
ADAM_B1 = 0.9
ADAM_B2 = 0.999
ADAM_EPS = 1e-08
ADAM_WD = 0.01
ADAM_STEP = 10

VMEM_LIMIT = 60 * 1024 * 1024
MESH = pl.DeviceIdType.MESH


def _cparams(n_axes):
    return pltpu.CompilerParams(dimension_semantics=("arbitrary",) * n_axes, vmem_limit_bytes=VMEM_LIMIT)


def _mm(a, b):
    return lax.dot_general(a.astype(MM), b.astype(MM), (((1,), (0,)), ((), ())), preferred_element_type=F32)


def _mm_nt(a, b):
    return lax.dot_general(a.astype(MM), b.astype(MM), (((1,), (1,)), ((), ())), preferred_element_type=F32)


def _mm_tn(a, b):
    return lax.dot_general(a.astype(MM), b.astype(MM), (((0,), (0,)), ((), ())), preferred_element_type=F32)


def _sig(x):
    return 1.0 / (1.0 + jnp.exp(-x))


def _colsum(x):
    return jnp.sum(x, axis=0, keepdims=True)


def _rowmean(x):
    return jnp.mean(x, axis=-1, keepdims=True)


def _modnorm_fwd(xv, g, sh, sc):
    r = lax.rsqrt(_rowmean(xv * xv) + EPS)
    xh = xv * r
    n = xh * g
    return n * (1.0 + sc) + sh, xh, n, r


def _modnorm_bwd(dh, xh, n, r, g, sc):
    dsc = _colsum(dh * n)
    dsh = _colsum(dh)
    dn = dh * (1.0 + sc)
    dg = _colsum(dn * xh)
    dxh = dn * g
    dx = r * (dxh - xh * _rowmean(dxh * xh))
    return dx, dsh, dsc, dg


def _ffn_fwd(x, mod, mo, gnorm, w_in, w_out, res, name, carry):
    T = x.shape[0]
    tm = min(512, T)
    tn = D_FF // 2
    nj = D_FF // tn

    def body(x_ref, mod_ref, g_ref, wa_ref, wb_ref, wo_ref, xo_ref, a_ref, b_ref, f_ref, h_ref, acc_scr):
        j = pl.program_id(1)

        @pl.when(j == 0)
        def _():
            h, _, _, _ = _modnorm_fwd(x_ref[...], g_ref[...], mod_ref[mo:mo + 1, :], mod_ref[mo + 1:mo + 2, :])
            h_ref[...] = h.astype(ACT)
            acc_scr[...] = jnp.zeros_like(acc_scr)

        h = h_ref[...]
        a = _mm(h, wa_ref[...])
        b = _mm(h, wb_ref[...])
        a_ref[...] = a.astype(ACT)
        b_ref[...] = b.astype(ACT)
        s = a * _sig(a) * b
        acc_scr[...] += _mm(s, wo_ref[...])

        @pl.when(j == nj - 1)
        def _():
            f = acc_scr[...]
            f_ref[...] = f
            xo_ref[...] = x_ref[...] + res * mod_ref[mo + 2:mo + 3, :] * f

    out = _gridded(
        body, carry, name=name, grid=(T // tm, nj),
        in_specs=[
            pl.BlockSpec((tm, D), lambda i, j: (i, 0)),
            pl.BlockSpec((9, D), lambda i, j: (0, 0)),
            pl.BlockSpec((1, D), lambda i, j: (0, 0)),
            pl.BlockSpec((D, tn), lambda i, j: (0, j)),
            pl.BlockSpec((D, tn), lambda i, j: (0, j + nj)),
            pl.BlockSpec((tn, D), lambda i, j: (j, 0)),
        ],
        out_specs=[
            pl.BlockSpec((tm, D), lambda i, j: (i, 0)),
            pl.BlockSpec((tm, tn), lambda i, j: (i, j)),
            pl.BlockSpec((tm, tn), lambda i, j: (i, j)),
            pl.BlockSpec((tm, D), lambda i, j: (i, 0)),
            pl.BlockSpec((tm, D), lambda i, j: (i, 0)),
        ],
        out_shape=[
            jax.ShapeDtypeStruct((T, D), F32),
            jax.ShapeDtypeStruct((T, D_FF), ACT),
            jax.ShapeDtypeStruct((T, D_FF), ACT),
            jax.ShapeDtypeStruct((T, D), F32),
            jax.ShapeDtypeStruct((T, D), ACT),
        ],
        scratch_shapes=[pltpu.VMEM((tm, D), F32)],
    )(x, mod, gnorm, w_in, w_in, w_out)
    return out[:5], out[5:]


def _ffn_bwd(x, h, dxo, f, a, b, mod, mo, gnorm, w_in, w_out, res, name, carry):
    T = x.shape[0]
    tm = min(512, T)
    ni = T // tm
    tn = 256
    nj = D_FF // tn

    def body(x_ref, h_ref, dxo_ref, f_ref, a_ref, b_ref, mod_ref, g_ref, wa_ref, wb_ref, wo_ref,
             dx_ref, dwa_ref, dwb_ref, dwo_ref, sm_ref, dh_scr):
        j = pl.program_id(0)
        i = pl.program_id(1)
        gate = mod_ref[mo + 2:mo + 3, :]
        hb = h_ref[...]
        dxo_v = dxo_ref[...]
        df = (res * gate * dxo_v).astype(MM)
        av = a_ref[...].astype(F32)
        bv = b_ref[...].astype(F32)
        sg = _sig(av)
        sa = av * sg
        s = (sa * bv).astype(MM)
        ds = _mm_nt(df, wo_ref[...])
        da = (ds * bv * sg * (1.0 + av * (1.0 - sg))).astype(MM)
        db = (ds * sa).astype(MM)

        @pl.when(i == 0)
        def _():
            dwa_ref[...] = jnp.zeros_like(dwa_ref)
            dwb_ref[...] = jnp.zeros_like(dwb_ref)
            dwo_ref[...] = jnp.zeros_like(dwo_ref)

        dwo_ref[...] += _mm_tn(s, df)
        dwa_ref[...] += _mm_tn(hb, da)
        dwb_ref[...] += _mm_tn(hb, db)
        dh_part = _mm_nt(da, wa_ref[...]) + _mm_nt(db, wb_ref[...])

        @pl.when(j == 0)
        def _():
            dh_scr[i] = dh_part

        @pl.when(j > 0)
        def _():
            dh_scr[i] += dh_part

        @pl.when((j == 0) & (i == 0))
        def _():
            sm_ref[...] = jnp.zeros_like(sm_ref)

        @pl.when(j == nj - 1)
        def _():
            sc = mod_ref[mo + 1:mo + 2, :]
            _, xh, n, r = _modnorm_fwd(x_ref[...], g_ref[...], mod_ref[mo:mo + 1, :], sc)
            dxn, dsh, dsc, dg = _modnorm_bwd(dh_scr[i], xh, n, r, g_ref[...], sc)
            dx_ref[...] = dxo_v + dxn
            sm_ref[0:1, :] += dsh
            sm_ref[1:2, :] += dsc
            sm_ref[2:3, :] += _colsum(dxo_v * f_ref[...]) * res
            sm_ref[3:4, :] += dg

    out = _gridded(
        body, carry, name=name, grid=(nj, ni),
        in_specs=[
            pl.BlockSpec((tm, D), lambda j, i: (i, 0)),
            pl.BlockSpec((tm, D), lambda j, i: (i, 0)),
            pl.BlockSpec((tm, D), lambda j, i: (i, 0)),
            pl.BlockSpec((tm, D), lambda j, i: (i, 0)),
            pl.BlockSpec((tm, tn), lambda j, i: (i, j)),
            pl.BlockSpec((tm, tn), lambda j, i: (i, j)),
            pl.BlockSpec((9, D), lambda j, i: (0, 0)),
            pl.BlockSpec((1, D), lambda j, i: (0, 0)),
            pl.BlockSpec((D, tn), lambda j, i: (0, j)),
            pl.BlockSpec((D, tn), lambda j, i: (0, j + nj)),
            pl.BlockSpec((tn, D), lambda j, i: (j, 0)),
        ],
        out_specs=[
            pl.BlockSpec((tm, D), lambda j, i: (jnp.where(j == nj - 1, i, 0), 0)),
            pl.BlockSpec((D, tn), lambda j, i: (0, j)),
            pl.BlockSpec((D, tn), lambda j, i: (0, j)),
            pl.BlockSpec((tn, D), lambda j, i: (j, 0)),
            pl.BlockSpec((8, D), lambda j, i: (0, 0)),
        ],
        out_shape=[
            jax.ShapeDtypeStruct((T, D), F32),
            jax.ShapeDtypeStruct((D, D_FF), F32),
            jax.ShapeDtypeStruct((D, D_FF), F32),
            jax.ShapeDtypeStruct((D_FF, D), F32),
            jax.ShapeDtypeStruct((8, D), F32),
        ],
        scratch_shapes=[pltpu.VMEM((ni, tm, D), F32)],
    )(x, h, dxo, f, a, b, mod, gnorm, w_in, w_in, w_out)
    return out[:5], out[5:]


def _head(x, target, gfin):
    T = x.shape[0]
    tm = min(512, T)
    ni = T // tm

    def body(x_ref, t_ref, g_ref, dx_ref, sm_ref):
        i = pl.program_id(0)

        @pl.when(i == 0)
        def _():
            sm_ref[...] = jnp.zeros_like(sm_ref)

        xv = x_ref[...]
        g = g_ref[...]
        r = lax.rsqrt(_rowmean(xv * xv) + EPS)
        xh = xv * r
        e = xh * g - t_ref[...]
        sm_ref[1:2, :] += _colsum(e * e) * (0.5 / D)
        dy = e * (1.0 / D)
        sm_ref[0:1, :] += _colsum(dy * xh)
        dxh = dy * g
        dx_ref[...] = r * (dxh - xh * _rowmean(dxh * xh))

        @pl.when(i == ni - 1)
        def _():
            sm_ref[1:2, :] = jnp.broadcast_to(jnp.sum(sm_ref[1:2, :], axis=-1, keepdims=True), (1, D))

    return pl.pallas_call(
        body, name="head_loss", grid=(ni,),
        in_specs=[pl.BlockSpec((tm, D), lambda i: (i, 0)), pl.BlockSpec((tm, D), lambda i: (i, 0)),
                  pl.BlockSpec((1, D), lambda i: (0, 0))],
        out_specs=[pl.BlockSpec((tm, D), lambda i: (i, 0)), pl.BlockSpec((8, D), lambda i: (0, 0))],
        out_shape=[jax.ShapeDtypeStruct((T, D), F32), jax.ShapeDtypeStruct((8, D), F32)],
        compiler_params=_cparams(1),
    )(x, target, gfin)


def _mixin_fwd(x, mod, mo, gnorm, w, carry):
    T = x.shape[0]
    tm = min(1024, T)

    def body(x_ref, mod_ref, g_ref, w_ref, p_ref, h_ref):
        @pl.when(pl.program_id(1) == 0)
        def _():
            h, _, _, _ = _modnorm_fwd(x_ref[...], g_ref[...], mod_ref[mo:mo + 1, :], mod_ref[mo + 1:mo + 2, :])
            h_ref[...] = h.astype(ACT)

        p_ref[0] = _mm(h_ref[...], w_ref[0])

    out = _gridded(
        body, carry, name="mixin_fwd", grid=(T // tm, 8),
        in_specs=[pl.BlockSpec((tm, D), lambda i, k: (i, 0)), pl.BlockSpec((9, D), lambda i, k: (0, 0)),
                  pl.BlockSpec((1, D), lambda i, k: (0, 0)), pl.BlockSpec((1, D, D), lambda i, k: (k, 0, 0))],
        out_specs=[pl.BlockSpec((1, tm, D), lambda i, k: (k, i, 0)), pl.BlockSpec((tm, D), lambda i, k: (i, 0))],
        out_shape=[jax.ShapeDtypeStruct((8, T, D), F32), jax.ShapeDtypeStruct((T, D), ACT)],
    )(x, mod, gnorm, w)
    return out[:2], out[2:]


def _mixin_bwd(x, h, dxo, dp, mod, mo, gnorm, w, carry):
    T = x.shape[0]
    tm = min(512, T)
    ni = T // tm

    def body(x_ref, h_ref, dxo_ref, dp_ref, mod_ref, g_ref, w_ref, dx_ref, dw_ref, sm_ref, dh_scr):
        k = pl.program_id(0)
        i = pl.program_id(1)
        dpk = dp_ref[0].astype(MM)

        @pl.when(i == 0)
        def _():
            dw_ref[...] = jnp.zeros_like(dw_ref)

        dw_ref[0] += _mm_tn(h_ref[...], dpk)
        dh_part = _mm_nt(dpk, w_ref[0])

        @pl.when(k == 0)
        def _():
            dh_scr[i] = dh_part

        @pl.when(k > 0)
        def _():
            dh_scr[i] += dh_part

        @pl.when((k == 0) & (i == 0))
        def _():
            sm_ref[...] = jnp.zeros_like(sm_ref)

        @pl.when(k == 7)
        def _():
            sc = mod_ref[mo + 1:mo + 2, :]
            _, xh, n, r = _modnorm_fwd(x_ref[...], g_ref[...], mod_ref[mo:mo + 1, :], sc)
            dxn, dsh, dsc, dg = _modnorm_bwd(dh_scr[i], xh, n, r, g_ref[...], sc)
            dx_ref[...] = dxo_ref[...] + dxn
            sm_ref[0:1, :] += dsh
            sm_ref[1:2, :] += dsc
            sm_ref[3:4, :] += dg

    out = _gridded(
        body, carry, name="mixin_bwd", grid=(8, ni),
        in_specs=[pl.BlockSpec((tm, D), lambda k, i: (i, 0)), pl.BlockSpec((tm, D), lambda k, i: (i, 0)),
                  pl.BlockSpec((tm, D), lambda k, i: (i, 0)),
                  pl.BlockSpec((1, tm, D), lambda k, i: (k, i, 0)), pl.BlockSpec((9, D), lambda k, i: (0, 0)),
                  pl.BlockSpec((1, D), lambda k, i: (0, 0)), pl.BlockSpec((1, D, D), lambda k, i: (k, 0, 0))],
        out_specs=[pl.BlockSpec((tm, D), lambda k, i: (jnp.where(k == 7, i, 0), 0)),
                   pl.BlockSpec((1, D, D), lambda k, i: (k, 0, 0)),
                   pl.BlockSpec((8, D), lambda k, i: (0, 0))],
        out_shape=[jax.ShapeDtypeStruct((T, D), F32), jax.ShapeDtypeStruct((8, D, D), F32),
                   jax.ShapeDtypeStruct((8, D), F32)],
        scratch_shapes=[pltpu.VMEM((ni, tm, D), F32)],
    )(x, h, dxo, dp, mod, gnorm, w)
    return out[:3], out[3:]


def _hgrn_consts():
    rows = jnp.arange(SUB * HD) // HD
    e = (rows[:, None] == jnp.arange(HD)[None, :]).astype(MM)
    return e, e.T


def _rows_bcast(ref, cb, first, n):
    parts = [jnp.broadcast_to(ref[pl.ds(c * CHUNK + first, 1), :], (n, HD)) for c in range(cb // CHUNK)]
    return jnp.concatenate(parts, axis=0)


def _hgrn_pre(qr, fr, lb_ref, b_scr, cb):
    z = lb_ref[...]
    lb = _sig(z[0:1, :] - z[1:2, :])
    sq = _sig(qr)
    q = qr * sq * Q_SCALE
    sf = _sig(fr)
    fg = lb + (1.0 - lb) * sf
    lf = jnp.log(fg)
    k = 1.0 - fg
    tl = lax.broadcasted_iota(jnp.int32, (cb, HD), 0) % CHUNK
    bc = lf
    sh = 1
    while sh < CHUNK:
        bc = bc + jnp.where(tl >= sh, pltpu.roll(bc, sh, 0), 0.0)
        sh *= 2
    b_scr[...] = bc
    bl = _rows_bcast(b_scr, cb, CHUNK - 1, CHUNK)
    br = [None] + [_rows_bcast(b_scr, cb, SUB * i - 1, CHUNK) for i in range(1, NSUB)]
    sb = tl // SUB
    bref = jnp.where(sb == 0, bc, jnp.where(sb == 1, br[1], jnp.where(sb == 2, br[2], br[3])))
    eb = jnp.exp(bc)
    ekd = jnp.exp(bl - bc)
    eqo = jnp.exp(bc - bref)
    eko = [None] + [jnp.exp(jnp.where(tl < SUB * i, br[i] - bc, NEG)) for i in range(1, NSUB)]
    return dict(lb=lb, sq=sq, q=q, sf=sf, fg=fg, k=k, tl=tl, sb=sb, b=bc, bl=bl, eb=eb, ekd=ekd, eqo=eqo,
                eko=eko, qe=q * eb, kd=k * ekd, qo=q * eqo, ko=[None] + [k * eko[i] for i in range(1, NSUB)])


def _pad_rows(x):
    return jnp.concatenate([x, jnp.zeros_like(x)], axis=0)


def _by_subblock(sbc, parts):
    out = jnp.zeros_like(parts[1])
    for i in range(1, NSUB):
        out = jnp.where(sbc == i, parts[i], out)
    return out


def _hgrn_fwd(p, hgrn_lb, hgrn_g, carry):
    T = p.shape[1]
    cb = min(512, T)
    nch = cb // CHUNK
    ncb = T // cb
    e_mat, _ = _hgrn_consts()

    def body(p_ref, lb_ref, g_ref, e_ref, o_ref, oa_ref, a_ref, s_ref, st_scr, q_scr, k_scr, b_scr, z_scr):
        @pl.when(pl.program_id(1) == 0)
        def _():
            st_scr[...] = jnp.zeros_like(st_scr)

        v = p_ref[2]
        og = p_ref[3]
        pre = _hgrn_pre(p_ref[0], p_ref[1], lb_ref, b_scr, cb)
        q_scr[...] = pre["q"]
        k_scr[...] = pre["k"]
        ti = lax.broadcasted_iota(jnp.int32, (SUB, HD), 0)

        def zbody(c, carry):
            for i in range(NSUB):
                r0 = pl.multiple_of(c * CHUNK + SUB * i, SUB)
                qi = q_scr[pl.ds(r0, SUB), :]
                bi = b_scr[pl.ds(r0, SUB), :]
                for s in range(SUB):
                    krow = k_scr[pl.ds(r0 + s, 1), :]
                    brow = b_scr[pl.ds(r0 + s, 1), :]
                    zz = qi * krow * jnp.exp(jnp.where(ti >= s, bi - brow, NEG))
                    z_scr[i, pl.ds(pl.multiple_of(c * SUB, SUB), SUB), s * HD:(s + 1) * HD] = zz.astype(MM)
            return carry

        lax.fori_loop(0, nch, zbody, 0)
        adiag = [_mm(z_scr[i], e_ref[...]) for i in range(NSUB)]
        sbc = lax.broadcasted_iota(jnp.int32, (CHUNK, HD), 0) // SUB
        o_parts = []
        for c in range(nch):
            rs = slice(c * CHUNK, (c + 1) * CHUNK)
            qo_c = pre["qo"][rs]
            offs = [None] + [_mm_nt(qo_c, _pad_rows(pre["ko"][i][rs])) for i in range(1, NSUB)]
            a_c = _by_subblock(sbc, offs)
            dparts = []
            for i in range(NSUB):
                blk = adiag[i][c * SUB:(c + 1) * SUB]
                dparts.append(blk if i == 0 else pltpu.roll(blk, SUB * i, 1))
            a_c = a_c + jnp.concatenate(dparts, axis=0)
            a_ref[0, rs, :] = a_c
            st = st_scr[...]
            s_ref[0, c] = st
            o_c = _mm(a_c, _pad_rows(v[rs])) + _mm_nt(pre["qe"][rs], st)
            st_scr[...] = st * jnp.exp(b_scr[pl.ds(c * CHUNK + CHUNK - 1, 1), :]) + _mm_tn(v[rs], pre["kd"][rs])
            o_parts.append(o_c)
        o = jnp.concatenate(o_parts, axis=0)
        o_ref[...] = o
        on = o * lax.rsqrt(_rowmean(o * o) + EPS) * g_ref[...]
        oa_ref[...] = (on * og * _sig(og)).astype(ACT)

    out = _gridded(
        body, carry, name="hgrn_fwd", grid=(HEADS, ncb),
        in_specs=[pl.BlockSpec((4, cb, HD), lambda h, c: (0, c, h)),
                  pl.BlockSpec((2, HD), lambda h, c: (0, h)),
                  pl.BlockSpec((1, HD), lambda h, c: (0, h)),
                  pl.BlockSpec((SUB * HD, HD), lambda h, c: (0, 0))],
        out_specs=[pl.BlockSpec((cb, HD), lambda h, c: (c, h)),
                   pl.BlockSpec((cb, HD), lambda h, c: (c, h)),
                   pl.BlockSpec((1, cb, HD), lambda h, c: (h, c, 0)),
                   pl.BlockSpec((1, nch, HD, HD), lambda h, c: (h, c, 0, 0))],
        out_shape=[jax.ShapeDtypeStruct((T, D), F32), jax.ShapeDtypeStruct((T, D), ACT),
                   jax.ShapeDtypeStruct((HEADS, T, HD), F32),
                   jax.ShapeDtypeStruct((HEADS, T // CHUNK, HD, HD), F32)],
        scratch_shapes=[pltpu.VMEM((HD, HD), F32), pltpu.VMEM((cb, HD), F32), pltpu.VMEM((cb, HD), F32),
                        pltpu.VMEM((cb, HD), F32), pltpu.VMEM((NSUB, nch * SUB, SUB * HD), MM)],
    )(p, hgrn_lb, hgrn_g, e_mat)
    return out[:4], out[4:]


def _hgrn_bwd(p, o, a_all, s_all, doa, hgrn_lb, hgrn_g, dp, carry):
    T = p.shape[1]
    cb = min(512, T)
    nch = cb // CHUNK
    ncb = T // cb
    _, et_mat = _hgrn_consts()

    def body(p_ref, o_ref, a_ref, s_ref, doa_ref, lb_ref, g_ref, et_ref, dp_in, dp_ref, sm_ref,
             dst_scr, q_scr, k_scr, b_scr, x_scr, dqd_scr, dkd_scr):
        del dp_in

        @pl.when(pl.program_id(1) == 0)
        def _():
            dst_scr[...] = jnp.zeros_like(dst_scr)
            sm_ref[...] = jnp.zeros_like(sm_ref)

        qr = p_ref[0]
        v = p_ref[2]
        og = p_ref[3]
        pre = _hgrn_pre(qr, p_ref[1], lb_ref, b_scr, cb)
        q, k = pre["q"], pre["k"]
        q_scr[...] = q
        k_scr[...] = k
        g = g_ref[...]
        ov = o_ref[...]
        r = lax.rsqrt(_rowmean(ov * ov) + EPS)
        oh = ov * r
        sgo = _sig(og)
        doa_v = doa_ref[...]
        don = doa_v * og * sgo
        dog = doa_v * oh * g * sgo * (1.0 + og * (1.0 - sgo))
        sm_ref[1:2, :] += _colsum(don * oh)
        doh = don * g
        do = r * (doh - oh * _rowmean(doh * oh))

        sbc = lax.broadcasted_iota(jnp.int32, (CHUNK, HD), 0) // SUB
        row_i = lax.broadcasted_iota(jnp.int32, (CHUNK, HD), 0)
        lane_i = lax.broadcasted_iota(jnp.int32, (CHUNK, HD), 1)
        causal = lane_i <= row_i
        da_parts, dv_parts, dqoff_parts, dkoff_parts = [], [], [], []
        for c in range(nch):
            rs = slice(c * CHUNK, (c + 1) * CHUNK)
            do_c = do[rs]
            da_c = jnp.where(causal, _mm_nt(do_c, _pad_rows(v[rs])), 0.0)
            da_parts.append(da_c)
            dv_parts.append(_mm_tn(a_ref[0, rs, :], do_c)[:CHUNK])
            qo_c = pre["qo"][rs]
            dqoff_parts.append(_by_subblock(
                sbc, [None] + [_mm(da_c, _pad_rows(pre["ko"][i][rs])) for i in range(1, NSUB)]))
            dko = jnp.zeros((CHUNK, HD), F32)
            for i in range(1, NSUB):
                dko = dko + pre["eko"][i][rs] * _mm_tn(jnp.where(sbc == i, da_c, 0.0), qo_c)[:CHUNK]
            dkoff_parts.append(dko)
        for i in range(NSUB):
            rows = []
            for c in range(nch):
                blk = da_parts[c][SUB * i:SUB * (i + 1)]
                rows.append(blk if i == 0 else pltpu.roll(blk, HD - SUB * i, 1))
            x_scr[i] = _mm(jnp.concatenate(rows, axis=0), et_ref[...])
        ti = lax.broadcasted_iota(jnp.int32, (SUB, HD), 0)

        def dbody(c, carry):
            for i in range(NSUB):
                r0 = pl.multiple_of(c * CHUNK + SUB * i, SUB)
                qi = q_scr[pl.ds(r0, SUB), :]
                bi = b_scr[pl.ds(r0, SUB), :]
                dq_acc = jnp.zeros((SUB, HD), F32)
                dk_acc = jnp.zeros((SUB, HD), F32)
                for s in range(SUB):
                    krow = k_scr[pl.ds(r0 + s, 1), :]
                    brow = b_scr[pl.ds(r0 + s, 1), :]
                    xs = x_scr[i, pl.ds(pl.multiple_of(c * SUB, SUB), SUB), s * HD:(s + 1) * HD]
                    w = xs * jnp.exp(jnp.where(ti >= s, bi - brow, NEG))
                    dq_acc = dq_acc + w * krow
                    dk_acc = jnp.where(ti == s, _colsum(w * qi), dk_acc)
                dqd_scr[pl.ds(r0, SUB), :] = dq_acc
                dkd_scr[pl.ds(r0, SUB), :] = dk_acc
            return carry

        lax.fori_loop(0, nch, dbody, 0)
        dqe_parts, dkdec_parts, dvi_parts, debl_parts = [None] * nch, [None] * nch, [None] * nch, [None] * nch
        for c in reversed(range(nch)):
            rs = slice(c * CHUNK, (c + 1) * CHUNK)
            st = s_ref[0, c]
            dst = dst_scr[...]
            do_c = do[rs]
            dqe_parts[c] = _mm(do_c, st)
            dkdec_parts[c] = _mm(v[rs], dst)
            dvi_parts[c] = _mm_nt(pre["kd"][rs], dst)
            debl_parts[c] = _colsum(dst * st)
            dst_scr[...] = dst * jnp.exp(b_scr[pl.ds(c * CHUNK + CHUNK - 1, 1), :]) + _mm_tn(do_c, pre["qe"][rs])
        dqe = jnp.concatenate(dqe_parts, axis=0)
        dkdec = jnp.concatenate(dkdec_parts, axis=0)
        dq_tot = jnp.concatenate(dqoff_parts, axis=0) * pre["eqo"] + dqd_scr[...] + dqe * pre["eb"]
        dk_inter = dkdec * pre["ekd"]
        dk_tot = jnp.concatenate(dkoff_parts, axis=0) + dkd_scr[...] + dk_inter
        db = q * dq_tot - k * dk_tot
        kdk = k * dk_inter
        dbl = jnp.concatenate(
            [jnp.broadcast_to(jnp.exp(b_scr[pl.ds(c * CHUNK + CHUNK - 1, 1), :]) * debl_parts[c]
                              + _colsum(kdk[c * CHUNK:(c + 1) * CHUNK]), (CHUNK, HD)) for c in range(nch)], axis=0)
        tl = pre["tl"]
        rc = db
        sh = 1
        while sh < CHUNK:
            rc = rc + jnp.where(tl + sh < CHUNK, pltpu.roll(rc, cb - sh, 0), 0.0)
            sh *= 2
        dlf = rc + dbl
        dfg = dlf / pre["fg"] - dk_tot
        sf = pre["sf"]
        lb = pre["lb"]
        sm_ref[0:1, :] += _colsum(dfg * (1.0 - sf))
        sq = pre["sq"]
        dp_ref[0] = dq_tot * Q_SCALE * sq * (1.0 + qr * (1.0 - sq))
        dp_ref[1] = dfg * (1.0 - lb) * sf * (1.0 - sf)
        dp_ref[2] = jnp.concatenate(dv_parts, axis=0) + jnp.concatenate(dvi_parts, axis=0)
        dp_ref[3] = dog

    rev = lambda c: ncb - 1 - c
    out = _gridded(
        body, carry, name="hgrn_bwd", grid=(HEADS, ncb),
        in_specs=[pl.BlockSpec((4, cb, HD), lambda h, c: (0, rev(c), h)),
                  pl.BlockSpec((cb, HD), lambda h, c: (rev(c), h)),
                  pl.BlockSpec((1, cb, HD), lambda h, c: (h, rev(c), 0)),
                  pl.BlockSpec((1, nch, HD, HD), lambda h, c: (h, rev(c), 0, 0)),
                  pl.BlockSpec((cb, HD), lambda h, c: (rev(c), h)),
                  pl.BlockSpec((2, HD), lambda h, c: (0, h)),
                  pl.BlockSpec((1, HD), lambda h, c: (0, h)),
                  pl.BlockSpec((HD, SUB * HD), lambda h, c: (0, 0)),
                  pl.BlockSpec(memory_space=pl.ANY)],
        out_specs=[pl.BlockSpec((4, cb, HD), lambda h, c: (0, rev(c), h)),
                   pl.BlockSpec((8, HD), lambda h, c: (0, h))],
        out_shape=[jax.ShapeDtypeStruct(dp.shape, F32), jax.ShapeDtypeStruct((8, D), F32)],
        aliases={8: 0},
        scratch_shapes=[pltpu.VMEM((HD, HD), F32), pltpu.VMEM((cb, HD), F32), pltpu.VMEM((cb, HD), F32),
                        pltpu.VMEM((cb, HD), F32), pltpu.VMEM((NSUB, nch * SUB, SUB * HD), F32),
                        pltpu.VMEM((cb, HD), F32), pltpu.VMEM((cb, HD), F32)],
    )(p, o, a_all, s_all, doa, hgrn_lb, hgrn_g, et_mat, dp)
    return out[:2], out[2:]


def _ln_fwd(u1, g, b):
    mu = _rowmean(u1)
    xc = u1 - mu
    rs = lax.rsqrt(_rowmean(xc * xc) + EPS)
    xh = xc * rs
    return xh * g + b, xh, rs


def _conv_fwd(p, cw, cb_, lng, lnb):
    T = p.shape[1]
    tm = min(512, T)

    def body(p_ref, cw_ref, cb_ref, g_ref, b_ref, u1_ref, u2_ref, buf):
        @pl.when(pl.program_id(0) == 0)
        def _():
            buf[0:HALO, :] = jnp.zeros((HALO, D), F32)

        buf[HALO:HALO + tm, :] = p_ref[0] * _sig(p_ref[1])
        acc = jnp.broadcast_to(cb_ref[...], (tm, D))
        for j in range(CONV_K):
            acc = acc + cw_ref[j:j + 1, :] * buf[pl.ds(HALO - (CONV_K - 1) + j, tm), :]
        u1_ref[...] = acc
        y, _, _ = _ln_fwd(acc, g_ref[...], b_ref[...])
        u2_ref[...] = (y * _sig(y)).astype(ACT)
        buf[0:HALO, :] = buf[tm:tm + HALO, :]

    return pl.pallas_call(
        body, name="conv_fwd", grid=(T // tm,),
        in_specs=[pl.BlockSpec((2, tm, D), lambda i: (2, i, 0)), pl.BlockSpec((HALO, D), lambda i: (0, 0)),
                  pl.BlockSpec((1, D), lambda i: (0, 0)), pl.BlockSpec((1, D), lambda i: (0, 0)),
                  pl.BlockSpec((1, D), lambda i: (0, 0))],
        out_specs=[pl.BlockSpec((tm, D), lambda i: (i, 0)), pl.BlockSpec((tm, D), lambda i: (i, 0))],
        out_shape=[jax.ShapeDtypeStruct((T, D), F32), jax.ShapeDtypeStruct((T, D), ACT)],
        scratch_shapes=[pltpu.VMEM((HALO + tm, D), F32)],
        compiler_params=_cparams(1),
    )(p, cw, cb_, lng, lnb)


def _conv_bwd(p, u1, du2, cw, lng, lnb, dp):
    T = p.shape[1]
    tm = min(512, T)
    ni = T // tm
    hb = tm // HALO

    def body(p_ref, ph_ref, u1_ref, du2_ref, cw_ref, g_ref, b_ref, dp_in, dp_ref, dcw_ref, sm_ref, ubuf, dbuf):
        del dp_in
        step = pl.program_id(0)

        @pl.when(step == 0)
        def _():
            dbuf[tm:tm + HALO, :] = jnp.zeros((HALO, D), F32)
            dcw_ref[...] = jnp.zeros_like(dcw_ref)
            sm_ref[...] = jnp.zeros_like(sm_ref)

        ua = p_ref[0]
        sgb = _sig(p_ref[1])
        halo = ph_ref[0] * _sig(ph_ref[1])
        ubuf[0:HALO, :] = jnp.where(step == ni - 1, 0.0, halo)
        ubuf[HALO:HALO + tm, :] = ua * sgb
        g = g_ref[...]
        y, xh, rs = _ln_fwd(u1_ref[...], g, b_ref[...])
        sy = _sig(y)
        dy = du2_ref[...] * sy * (1.0 + y * (1.0 - sy))
        sm_ref[1:2, :] += _colsum(dy * xh)
        sm_ref[2:3, :] += _colsum(dy)
        dxh = dy * g
        du1 = rs * (dxh - _rowmean(dxh) - xh * _rowmean(dxh * xh))
        sm_ref[0:1, :] += _colsum(du1)
        dbuf[0:tm, :] = du1
        du0 = jnp.zeros((tm, D), F32)
        for j in range(CONV_K):
            du0 = du0 + cw_ref[j:j + 1, :] * dbuf[pl.ds(CONV_K - 1 - j, tm), :]
            dcw_ref[j:j + 1, :] += _colsum(du1 * ubuf[pl.ds(HALO - (CONV_K - 1) + j, tm), :])
        dp_ref[0] = du0 * sgb
        dp_ref[1] = du0 * ua * sgb * (1.0 - sgb)
        dbuf[tm:tm + HALO, :] = dbuf[0:HALO, :]

    rev = lambda i: ni - 1 - i
    return pl.pallas_call(
        body, name="conv_bwd", grid=(ni,),
        in_specs=[pl.BlockSpec((2, tm, D), lambda i: (2, rev(i), 0)),
                  pl.BlockSpec((2, HALO, D), lambda i: (2, jnp.maximum(rev(i) * hb - 1, 0), 0)),
                  pl.BlockSpec((tm, D), lambda i: (rev(i), 0)), pl.BlockSpec((tm, D), lambda i: (rev(i), 0)),
                  pl.BlockSpec((HALO, D), lambda i: (0, 0)), pl.BlockSpec((1, D), lambda i: (0, 0)),
                  pl.BlockSpec((1, D), lambda i: (0, 0)), pl.BlockSpec(memory_space=pl.ANY)],
        out_specs=[pl.BlockSpec((2, tm, D), lambda i: (2, rev(i), 0)),
                   pl.BlockSpec((HALO, D), lambda i: (0, 0)), pl.BlockSpec((8, D), lambda i: (0, 0))],
        out_shape=[jax.ShapeDtypeStruct(dp.shape, F32), jax.ShapeDtypeStruct((HALO, D), F32),
                   jax.ShapeDtypeStruct((8, D), F32)],
        input_output_aliases={7: 0},
        scratch_shapes=[pltpu.VMEM((HALO + tm, D), F32), pltpu.VMEM((tm + HALO, D), F32)],
        compiler_params=_cparams(1),
    )(p, p, u1, du2, cw, lng, lnb, dp)


def _mixout_fwd(x, oa, u2, p, mod, mo, w_a, w_b, w_o):
    T = x.shape[0]
    tm = min(512, T)

    def body(x_ref, oa_ref, u2_ref, p_ref, mod_ref, wa_ref, wb_ref, wo_ref, xo_ref, ya_ref, yb_ref, mo_ref):
        ya = _mm(oa_ref[...], wa_ref[...])
        yb = _mm(u2_ref[...], wb_ref[...])
        ya_ref[...] = ya.astype(ACT)
        yb_ref[...] = yb.astype(ACT)
        merged = _sig(p_ref[0]) * ya + _sig(p_ref[1]) * yb
        out = _mm(merged, wo_ref[...])
        mo_ref[...] = out
        xo_ref[...] = x_ref[...] + mod_ref[mo + 2:mo + 3, :] * out

    tile = pl.BlockSpec((tm, D), lambda i: (i, 0))
    wspec = pl.BlockSpec((D, D), lambda i: (0, 0))
    return pl.pallas_call(
        body, name="mixout_fwd", grid=(T // tm,),
        in_specs=[tile, tile, tile, pl.BlockSpec((2, tm, D), lambda i: (3, i, 0)),
                  pl.BlockSpec((9, D), lambda i: (0, 0)), wspec, wspec, wspec],
        out_specs=[tile, tile, tile, tile],
        out_shape=[jax.ShapeDtypeStruct((T, D), F32), jax.ShapeDtypeStruct((T, D), ACT),
                   jax.ShapeDtypeStruct((T, D), ACT), jax.ShapeDtypeStruct((T, D), F32)],
        compiler_params=_cparams(1),
    )(x, oa, u2, p, mod, w_a, w_b, w_o)


def _mixout_bwd(dxo, oa, u2, ya, yb, mout, p, mod, mo, w_a, w_b, w_o):
    T = dxo.shape[0]
    tm = min(256, T)

    def body(dxo_ref, oa_ref, u2_ref, ya_ref, yb_ref, mo_ref, p_ref, mod_ref, wa_ref, wb_ref, wo_ref,
             dp_ref, doa_ref, du2_ref, dwa_ref, dwb_ref, dwo_ref, sm_ref):
        @pl.when(pl.program_id(0) == 0)
        def _():
            dwa_ref[...] = jnp.zeros_like(dwa_ref)
            dwb_ref[...] = jnp.zeros_like(dwb_ref)
            dwo_ref[...] = jnp.zeros_like(dwo_ref)
            sm_ref[...] = jnp.zeros_like(sm_ref)

        dxo_v = dxo_ref[...]
        sm_ref[2:3, :] += _colsum(dxo_v * mo_ref[...])
        dmo = (mod_ref[mo + 2:mo + 3, :] * dxo_v).astype(MM)
        ya = ya_ref[...].astype(F32)
        yb = yb_ref[...].astype(F32)
        sga = _sig(p_ref[0])
        sgb = _sig(p_ref[1])
        merged = (sga * ya + sgb * yb).astype(MM)
        dwo_ref[...] += _mm_tn(merged, dmo)
        dmg = _mm_nt(dmo, wo_ref[...])
        dp_ref[0] = dmg * ya * sga * (1.0 - sga)
        dp_ref[1] = dmg * yb * sgb * (1.0 - sgb)
        dya = (dmg * sga).astype(MM)
        dyb = (dmg * sgb).astype(MM)
        dwa_ref[...] += _mm_tn(oa_ref[...], dya)
        dwb_ref[...] += _mm_tn(u2_ref[...], dyb)
        doa_ref[...] = _mm_nt(dya, wa_ref[...])
        du2_ref[...] = _mm_nt(dyb, wb_ref[...])

    tile = pl.BlockSpec((tm, D), lambda i: (i, 0))
    wspec = pl.BlockSpec((D, D), lambda i: (0, 0))
    return pl.pallas_call(
        body, name="mixout_bwd", grid=(T // tm,),
        in_specs=[tile, tile, tile, tile, tile, tile, pl.BlockSpec((2, tm, D), lambda i: (3, i, 0)),
                  pl.BlockSpec((9, D), lambda i: (0, 0)), wspec, wspec, wspec],
        out_specs=[pl.BlockSpec((2, tm, D), lambda i: (3, i, 0)), tile, tile, wspec, wspec, wspec,
                   pl.BlockSpec((8, D), lambda i: (0, 0))],
        out_shape=[jax.ShapeDtypeStruct((8, T, D), F32), jax.ShapeDtypeStruct((T, D), F32),
                   jax.ShapeDtypeStruct((T, D), F32), jax.ShapeDtypeStruct((D, D), F32),
                   jax.ShapeDtypeStruct((D, D), F32), jax.ShapeDtypeStruct((D, D), F32),
                   jax.ShapeDtypeStruct((8, D), F32)],
        compiler_params=_cparams(1),
    )(dxo, oa, u2, ya, yb, mout, p, mod, w_a, w_b, w_o)


def _ada_fwd(cs_all, ada_w, ada_b_cols):
    def body(cs_ref, w_ref, b_ref, out_ref):
        out_ref[...] = jnp.dot(cs_ref[...], w_ref[...], preferred_element_type=F32,
                               precision=lax.Precision.HIGHEST) + b_ref[...]

    return pl.pallas_call(
        body, name="ada_fwd", out_shape=jax.ShapeDtypeStruct((N_DEV, ada_w.shape[1]), F32),
        compiler_params=pltpu.CompilerParams(vmem_limit_bytes=VMEM_LIMIT),
    )(cs_all, ada_w, ada_b_cols)


def _ada_wgrad(cs_all, dmod_cols):
    cs_t = jnp.pad(cs_all.T, ((0, 0), (0, HD - N_DEV)))
    dm = jnp.pad(dmod_cols, ((0, HD - N_DEV), (0, 0)))

    def body(cs_ref, d_ref, out_ref):
        out_ref[...] = jnp.dot(cs_ref[...], d_ref[...], preferred_element_type=F32,
                               precision=lax.Precision.HIGHEST)

    return pl.pallas_call(
        body, name="ada_wgrad", out_shape=jax.ShapeDtypeStruct((D, dmod_cols.shape[1]), F32),
        compiler_params=pltpu.CompilerParams(vmem_limit_bytes=VMEM_LIMIT),
    )(cs_t, dm)


def _adam_math(w, g, m, v):
    m2 = ADAM_B1 * m + (1.0 - ADAM_B1) * g
    v2 = ADAM_B2 * v + (1.0 - ADAM_B2) * (g * g)
    m_hat = m2 / (1.0 - ADAM_B1 ** ADAM_STEP)
    v_hat = v2 / (1.0 - ADAM_B2 ** ADAM_STEP)
    delta = -ADAM_LR * (m_hat / (jnp.sqrt(v_hat) + ADAM_EPS) + ADAM_WD * w)
    return delta, m2, v2


def _adamw(w, m, v, g, name):
    R, C = w.shape
    slots = g.ndim == 3
    tr = R
    for cand in (256, 176):
        if R % cand == 0 and R > cand:
            tr = cand
            break

    def body(w_ref, m_ref, v_ref, g_ref, go_ref, d_ref, mo_ref, vo_ref):
        if slots:
            gv = g_ref[0].astype(F32)
            for s in range(1, N_DEV):
                gv = gv + g_ref[s].astype(F32)
        else:
            gv = g_ref[...]
        go_ref[...] = gv
        d_ref[...], mo_ref[...], vo_ref[...] = _adam_math(w_ref[...], gv, m_ref[...], v_ref[...])

    tile = pl.BlockSpec((tr, C), lambda i: (i, 0))
    gspec = pl.BlockSpec((N_DEV, tr, C), lambda i: (0, i, 0)) if slots else tile
    sds = jax.ShapeDtypeStruct((R, C), F32)
    return pl.pallas_call(
        body, name=name, grid=(R // tr,), in_specs=[tile, tile, tile, gspec], out_specs=[tile] * 4,
        out_shape=[sds] * 4, compiler_params=_cparams(1),
    )(w, m, v, g)


def _sum_slots(pack):
    def body(p_ref, out_ref):
        acc = p_ref[0]
        for s in range(1, N_DEV):
            acc = acc + p_ref[s]
        out_ref[...] = acc

    return pl.pallas_call(body, name="sum_small", out_shape=jax.ShapeDtypeStruct(pack.shape[1:], F32))(pack)


def _me():
    return lax.axis_index("x"), lax.axis_index("y"), lax.axis_index("c")


def _peer(r):
    x, y, c = _me()
    px = 1 - x if r & 4 else x
    py = 1 - y if r & 2 else y
    pc = 1 - c if r & 1 else c
    return (px, py, pc), 4 * px + 2 * py + pc


def _allgather_small(x):
    R, C = x.shape

    def body(x_ref, out_ref, send_sems, recv_sems):
        mx, my, mc = _me()
        me = 4 * mx + 2 * my + mc
        mine = out_ref.at[pl.ds(pl.multiple_of(me * R, 8), R), :]
        copies = []
        for r in range(1, N_DEV):
            dev, _ = _peer(r)
            copies.append(pltpu.make_async_remote_copy(
                src_ref=x_ref, dst_ref=mine, send_sem=send_sems.at[r - 1], recv_sem=recv_sems.at[r - 1],
                device_id=dev, device_id_type=MESH))
        for cp in copies:
            cp.start()
        mine[...] = x_ref[...]
        for r in range(1, N_DEV):
            _, idx = _peer(r)
            theirs = out_ref.at[pl.ds(pl.multiple_of(idx * R, 8), R), :]
            pltpu.make_async_remote_copy(
                src_ref=x_ref, dst_ref=theirs, send_sem=send_sems.at[r - 1], recv_sem=recv_sems.at[r - 1],
                device_id=_peer(r)[0], device_id_type=MESH).wait_recv()
        for cp in copies:
            cp.wait_send()

    return pl.pallas_call(
        body, name="allgather_small_%dx%d" % (R, C),
        out_shape=jax.ShapeDtypeStruct((N_DEV * R, C), F32),
        in_specs=[pl.BlockSpec(memory_space=pltpu.VMEM)], out_specs=pl.BlockSpec(memory_space=pltpu.VMEM),
        scratch_shapes=[pltpu.SemaphoreType.DMA((N_DEV - 1,)), pltpu.SemaphoreType.DMA((N_DEV - 1,))],
    )(x)


def _xchg_copies(ins, outs, sems, gather):
    send_sems, recv_sems, local_sems = sems
    mx, my, mc = _me()
    me = 4 * mx + 2 * my + mc
    own, sends, recvs = [], [], []
    for a in range(len(ins)):
        own.append(pltpu.make_async_copy(ins[a] if gather else ins[a].at[me], outs[a].at[me], local_sems.at[a]))
        for r in range(1, N_DEV):
            dev, idx = _peer(r)
            k = a * (N_DEV - 1) + r - 1
            src = ins[a] if gather else ins[a].at[idx]
            sends.append(pltpu.make_async_remote_copy(
                src_ref=src, dst_ref=outs[a].at[me], send_sem=send_sems.at[k], recv_sem=recv_sems.at[k],
                device_id=dev, device_id_type=MESH))
            recvs.append(pltpu.make_async_remote_copy(
                src_ref=src, dst_ref=outs[a].at[idx], send_sem=send_sems.at[k], recv_sem=recv_sems.at[k],
                device_id=dev, device_id_type=MESH))
    return own, sends, recvs


def _xchg_start(ins, outs, sems, gather):
    own, sends, _ = _xchg_copies(ins, outs, sems, gather)
    for cp in own + sends:
        cp.start()


def _xchg_wait(ins, outs, sems, gather):
    own, sends, recvs = _xchg_copies(ins, outs, sems, gather)
    for cp in recvs:
        cp.wait_recv()
    for cp in own:
        cp.wait()
    for cp in sends:
        cp.wait_send()


def _xchg_specs(arrays, gather):
    n = len(arrays)
    out_shape = [jax.ShapeDtypeStruct(((N_DEV,) + a.shape) if gather else a.shape, a.dtype) for a in arrays]
    sems = [pltpu.SemaphoreType.DMA((n * (N_DEV - 1),)), pltpu.SemaphoreType.DMA((n * (N_DEV - 1),)),
            pltpu.SemaphoreType.DMA((n,))]
    return out_shape, sems


def _exchange(arrays, gather, name):
    n = len(arrays)

    def body(*refs):
        _xchg_start(refs[:n], refs[n:2 * n], refs[2 * n:], gather)
        _xchg_wait(refs[:n], refs[n:2 * n], refs[2 * n:], gather)

    out_shape, sems = _xchg_specs(arrays, gather)
    return pl.pallas_call(
        body, name=name, out_shape=out_shape,
        in_specs=[pl.BlockSpec(memory_space=pl.ANY)] * n, out_specs=[pl.BlockSpec(memory_space=pl.ANY)] * n,
        scratch_shapes=sems,
    )(*arrays)


def _gridded(body, carry, *, name, grid, in_specs, out_specs, out_shape, scratch_shapes=(), aliases=None):
    if carry is None:
        return pl.pallas_call(
            body, name=name, grid=grid, in_specs=list(in_specs), out_specs=list(out_specs),
            out_shape=list(out_shape), scratch_shapes=list(scratch_shapes), input_output_aliases=aliases or {},
            compiler_params=_cparams(len(grid)))
    arrays, gather = carry
    n, n_in, n_out, n_scr = len(arrays), len(in_specs), len(out_specs), len(scratch_shapes)
    c_shape, c_sems = _xchg_specs(arrays, gather)

    def wrapped(*refs):
        ins, cin = refs[:n_in], refs[n_in:n_in + n]
        o0 = n_in + n
        outs, cout = refs[o0:o0 + n_out], refs[o0 + n_out:o0 + n_out + n]
        s0 = o0 + n_out + n
        scr, sems = refs[s0:s0 + n_scr], refs[s0 + n_scr:]
        first = pl.program_id(0) == 0
        last = pl.program_id(0) == grid[0] - 1
        for ax in range(1, len(grid)):
            first = first & (pl.program_id(ax) == 0)
            last = last & (pl.program_id(ax) == grid[ax] - 1)

        @pl.when(first)
        def _():
            _xchg_start(cin, cout, sems, gather)

        body(*ins, *outs, *scr)

        @pl.when(last)
        def _():
            _xchg_wait(cin, cout, sems, gather)

    hbm = pl.BlockSpec(memory_space=pl.ANY)
    res = pl.pallas_call(
        wrapped, name=name, grid=grid, in_specs=list(in_specs) + [hbm] * n, out_specs=list(out_specs) + [hbm] * n,
        out_shape=list(out_shape) + c_shape, scratch_shapes=list(scratch_shapes) + c_sems,
        input_output_aliases=aliases or {}, compiler_params=_cparams(len(grid)),
    )
    return lambda *args: res(*args, *arrays)


def _local_step(x, target, mod, small, sh):
    w1_in, w1_out = _exchange([sh["ffn1_w_in"], sh["ffn1_w_out"]], True, "allgather_ffn1")
    w1_in, w1_out = _full_w_in(w1_in), w1_out.reshape(D_FF, D)
    (x1, a1, b1, f1, h1), (wm_in,) = _ffn_fwd(x, mod, 0, small["norm_ffn1"], w1_in, w1_out, 0.5, "ffn1_fwd",
                                              ([sh["mix_w_in"]], True))
    (p, h2), (wh_o, wc_o, wm_o, cw) = _mixin_fwd(
        x1, mod, 3, small["norm_mix"], wm_in,
        ([sh["hgrn_w_o"], sh["conv_w_o"], sh["mix_w_out"], sh["conv_w"]], True))
    wh_o, wc_o, wm_o = wh_o.reshape(D, D), wc_o.reshape(D, D), wm_o.reshape(D, D)
    cw = jnp.pad(cw.transpose(1, 0, 2).reshape(CONV_K, D), ((0, HALO - CONV_K), (0, 0)))
    (o, oa, a_all, s_all), (w2_in, w2_out) = _hgrn_fwd(p, small["hgrn_lb"], small["hgrn_g"],
                                                       ([sh["ffn2_w_in"], sh["ffn2_w_out"]], True))
    w2_in, w2_out = _full_w_in(w2_in), w2_out.reshape(D_FF, D)
    u1, u2 = _conv_fwd(p, cw, small["conv_b"], small["conv_ln_g"], small["conv_ln_b"])
    x2, ya, yb, mout = _mixout_fwd(x1, oa, u2, p, mod, 3, wh_o, wc_o, wm_o)
    (x3, a3, b3, f3, h3), _ = _ffn_fwd(x2, mod, 6, small["norm_ffn2"], w2_in, w2_out, 0.5, "ffn2_fwd", None)
    dx3, sm_head = _head(x3, target, small["norm_final"])

    (dx2, dw2_a, dw2_b, dw2_out, sm3), _ = _ffn_bwd(x2, h3, dx3, f3, a3, b3, mod, 6, small["norm_ffn2"], w2_in,
                                                    w2_out, 0.5, "ffn2_bwd", None)
    dp, doa, du2, dwh_o, dwc_o, dwm_o, sm_mo = _mixout_bwd(dx2, oa, u2, ya, yb, mout, p, mod, 3, wh_o, wc_o, wm_o)
    dp, dcw, sm_cv = _conv_bwd(p, u1, du2, cw, small["conv_ln_g"], small["conv_ln_b"], dp)
    rows = lambda t: t.reshape(N_DEV, -1, D).astype(MM)
    (dp, sm_hg), (r2_in, r2_out) = _hgrn_bwd(p, o, a_all, s_all, doa, small["hgrn_lb"], small["hgrn_g"], dp,
                                             ([_w_in_shards(dw2_a, dw2_b), rows(dw2_out)], False))
    (dx1, dwm_in, sm2), (rh_o, rc_o, rm_o, rcw) = _mixin_bwd(
        x1, h2, dx2, dp, mod, 3, small["norm_mix"], wm_in,
        ([rows(dwh_o), rows(dwc_o), rows(dwm_o), dcw[:CONV_K].reshape(CONV_K, N_DEV, -1).transpose(1, 0, 2)], False))
    (dx0, dw1_a, dw1_b, dw1_out, sm1), (rm_in,) = _ffn_bwd(x, h1, dx1, f1, a1, b1, mod, 0, small["norm_ffn1"],
                                                          w1_in, w1_out, 0.5, "ffn1_bwd",
                                                          ([dwm_in.astype(MM)], False))
    r1_in, r1_out = _exchange([_w_in_shards(dw1_a, dw1_b), rows(dw1_out)], False, "scatter_ffn1")

    dmod = jnp.concatenate([sm1[0:3], sm2[0:2], sm_mo[2:3], sm3[0:3]], axis=0)
    gsmall = dict(norm_ffn1=sm1[3:4], norm_mix=sm2[3:4], lb0=sm_hg[0:1], hgrn_g=sm_hg[1:2], conv_b=sm_cv[0:1],
                  conv_ln_g=sm_cv[1:2], conv_ln_b=sm_cv[2:3], norm_ffn2=sm3[3:4], norm_final=sm_head[0:1])
    recv = dict(ffn1_w_in=r1_in, ffn1_w_out=r1_out, mix_w_in=rm_in, hgrn_w_o=rh_o, conv_w=rcw, conv_w_o=rc_o,
                mix_w_out=rm_o, ffn2_w_in=r2_in, ffn2_w_out=r2_out)
    return sm_head[1, 0], dx0, dmod, gsmall, recv


def _full_w_in(g):
    return g.transpose(1, 0, 2).reshape(D, -1)


def _w_in_shards(dwa, dwb):
    half = N_DEV // 2
    return jnp.concatenate([t.reshape(D, half, -1).transpose(1, 0, 2).astype(MM) for t in (dwa, dwb)], axis=0)


SMALL_ORDER = ("norm_ffn1", "norm_mix", "lb0", "hgrn_g", "conv_b", "conv_ln_g", "conv_ln_b", "norm_ffn2",
               "norm_final")
PACK_ROWS = 24


def kernel(x, c, ada_w, ada_b, norm_ffn1, ffn1_w_in, ffn1_w_out, norm_mix, mix_w_in, hgrn_lb, hgrn_g, hgrn_w_o, conv_w, conv_b, conv_ln_g, conv_ln_b, conv_w_o, mix_w_out, norm_ffn2, ffn2_w_in, ffn2_w_out, norm_final, loss_target, m_ada_w, m_ada_b, m_norm_ffn1, m_ffn1_w_in, m_ffn1_w_out, m_norm_mix, m_mix_w_in, m_hgrn_lb, m_hgrn_g, m_hgrn_w_o, m_conv_w, m_conv_b, m_conv_ln_g, m_conv_ln_b, m_conv_w_o, m_mix_w_out, m_norm_ffn2, m_ffn2_w_in, m_ffn2_w_out, m_norm_final, v_ada_w, v_ada_b, v_norm_ffn1, v_ffn1_w_in, v_ffn1_w_out, v_norm_mix, v_mix_w_in, v_hgrn_lb, v_hgrn_g, v_hgrn_w_o, v_conv_w, v_conv_b, v_conv_ln_g, v_conv_ln_b, v_conv_w_o, v_mix_w_out, v_norm_ffn2, v_ffn2_w_in, v_ffn2_w_out, v_norm_final):
    mx, my, mc = _me()
    me = 4 * mx + 2 * my + mc
    ncol = ada_w.shape[2]

    cs = jnp.broadcast_to(c * jax.nn.sigmoid(c), (8, D))
    cs_all = _allgather_small(cs).reshape(N_DEV, 8, D)[:, 0, :]
    ada_b_cols = lax.dynamic_slice(ada_b, (0, me * ncol), (1, ncol))
    mod_cols = _ada_fwd(cs_all, ada_w[0], ada_b_cols)
    mod_all = _allgather_small(mod_cols).reshape(N_DEV, N_DEV, ncol)
    mod = lax.dynamic_index_in_dim(mod_all, me, axis=1, keepdims=False).reshape(9, D)

    sh = dict(ffn1_w_in=ffn1_w_in, ffn1_w_out=ffn1_w_out, mix_w_in=mix_w_in, hgrn_w_o=hgrn_w_o,
              conv_w_o=conv_w_o, mix_w_out=mix_w_out, ffn2_w_in=ffn2_w_in, ffn2_w_out=ffn2_w_out)
    sh = {n: w[0].astype(MM) for n, w in sh.items()}
    sh["conv_w"] = conv_w[0]
    small = dict(norm_ffn1=norm_ffn1, norm_mix=norm_mix, hgrn_lb=hgrn_lb, hgrn_g=hgrn_g, conv_b=conv_b,
                 conv_ln_g=conv_ln_g, conv_ln_b=conv_ln_b, norm_ffn2=norm_ffn2, norm_final=norm_final.reshape(1, D))

    loss_local, dx, dmod, gsmall, recv = _local_step(x[0], loss_target[0], mod, small, sh)
    loss = lax.psum(loss_local, ("x", "y", "c"))

    pack = jnp.concatenate([dmod] + [gsmall[n] for n in SMALL_ORDER]
                           + [jnp.zeros((PACK_ROWS - 9 - len(SMALL_ORDER), D), F32)], axis=0)
    pack_all = _allgather_small(pack).reshape(N_DEV, PACK_ROWS, D)
    tot = _sum_slots(pack_all)
    gs = {n: tot[9 + i:10 + i] for i, n in enumerate(SMALL_ORDER)}
    dmod_all = pack_all[:, 0:9, :].reshape(N_DEV, 9 * D)
    g_ada_b = tot[0:9].reshape(1, 9 * D)
    g_ada_w = _ada_wgrad(cs_all, lax.dynamic_slice(dmod_all, (0, me * ncol), (N_DEV, ncol)))
    z = hgrn_lb.astype(F32)
    p0 = jax.nn.sigmoid(z[0:1] - z[1:2])
    dz0 = p0 * (1.0 - p0) * gs["lb0"]
    g_hgrn_lb = jnp.concatenate([dz0, -dz0], axis=0)

    res = {}
    res["ada_w"] = _adamw(ada_w[0], m_ada_w[0], v_ada_w[0], g_ada_w, "adamw_ada_w")
    big = dict(ffn1_w_in=(ffn1_w_in, m_ffn1_w_in, v_ffn1_w_in), ffn1_w_out=(ffn1_w_out, m_ffn1_w_out, v_ffn1_w_out),
               mix_w_in=(mix_w_in, m_mix_w_in, v_mix_w_in), hgrn_w_o=(hgrn_w_o, m_hgrn_w_o, v_hgrn_w_o),
               conv_w=(conv_w, m_conv_w, v_conv_w), conv_w_o=(conv_w_o, m_conv_w_o, v_conv_w_o),
               mix_w_out=(mix_w_out, m_mix_w_out, v_mix_w_out), ffn2_w_in=(ffn2_w_in, m_ffn2_w_in, v_ffn2_w_in),
               ffn2_w_out=(ffn2_w_out, m_ffn2_w_out, v_ffn2_w_out))
    for n, (w, m, v) in big.items():
        res[n] = _adamw(w[0], m[0], v[0], recv[n], "adamw_" + n)
    sm_names = ("ada_b", "norm_ffn1", "norm_mix", "hgrn_lb", "hgrn_g", "conv_b", "conv_ln_g", "conv_ln_b",
                "norm_ffn2", "norm_final")
    sm_w = dict(ada_b=(ada_b, m_ada_b, v_ada_b), norm_ffn1=(norm_ffn1, m_norm_ffn1, v_norm_ffn1),
                norm_mix=(norm_mix, m_norm_mix, v_norm_mix), hgrn_lb=(hgrn_lb, m_hgrn_lb, v_hgrn_lb),
                hgrn_g=(hgrn_g, m_hgrn_g, v_hgrn_g), conv_b=(conv_b, m_conv_b, v_conv_b),
                conv_ln_g=(conv_ln_g, m_conv_ln_g, v_conv_ln_g), conv_ln_b=(conv_ln_b, m_conv_ln_b, v_conv_ln_b),
                norm_ffn2=(norm_ffn2, m_norm_ffn2, v_norm_ffn2), norm_final=(norm_final, m_norm_final, v_norm_final))
    sm_g = dict(gs, ada_b=g_ada_b, hgrn_lb=g_hgrn_lb)
    rows = {n: sm_w[n][0].size // D for n in sm_names}
    n_rows = sum(rows.values())
    pad = (-n_rows) % 8
    stack = lambda parts: jnp.concatenate([q.reshape(-1, D) for q in parts] + [jnp.ones((pad, D), F32)], axis=0)
    st = _adamw(stack([sm_w[n][0] for n in sm_names]), stack([sm_w[n][1] for n in sm_names]),
                stack([sm_w[n][2] for n in sm_names]), stack([sm_g[n] for n in sm_names]), "adamw_small")
    off = 0
    for n in sm_names:
        res[n] = tuple(t[off:off + rows[n]].reshape(sm_w[n][0].shape) for t in st)
        off += rows[n]

    order = ("ada_w", "ada_b", "norm_ffn1", "ffn1_w_in", "ffn1_w_out", "norm_mix", "mix_w_in", "hgrn_lb", "hgrn_g",
             "hgrn_w_o", "conv_w", "conv_b", "conv_ln_g", "conv_ln_b", "conv_w_o", "mix_w_out", "norm_ffn2",
             "ffn2_w_in", "ffn2_w_out", "norm_final")
    lead = lambda n, t: t[None] if n in big or n == "ada_w" else t
    outs = [loss, dx[None]]
    for j in range(4):
        outs += [lead(n, res[n][j]) for n in order]
    return tuple(outs)
```

```python
import functools

import jax
import jax.numpy as jnp
from jax import lax
from jax.experimental import pallas as pl
from jax.experimental.pallas import tpu as pltpu

F32 = jnp.float32
MM = jnp.bfloat16
ACT = jnp.bfloat16

D = 1024
D_FF = 2816
HEADS = 8
HD = 128
CHUNK = 64
SUB = 16
NSUB = CHUNK // SUB
CONV_K = 31
HALO = 32
EPS = 1e-6
N_DEV = 8
NEG = -1e30
Q_SCALE = HD ** -0.5

ADAM_LR = 0.001
ADAM_B1 = 0.9
ADAM_B2 = 0.999
ADAM_EPS = 1e-08
ADAM_WD = 0.01
ADAM_STEP = 10

VMEM_LIMIT = 60 * 1024 * 1024
MESH = pl.DeviceIdType.MESH


def _cparams(n_axes):
    return pltpu.CompilerParams(dimension_semantics=("arbitrary",) * n_axes, vmem_limit_bytes=VMEM_LIMIT)


def _mm(a, b):
    return lax.dot_general(a.astype(MM), b.astype(MM), (((1,), (0,)), ((), ())), preferred_element_type=F32)


def _mm_nt(a, b):
    return lax.dot_general(a.astype(MM), b.astype(MM), (((1,), (1,)), ((), ())), preferred_element_type=F32)


def _mm_tn(a, b):
    return lax.dot_general(a.astype(MM), b.astype(MM), (((0,), (0,)), ((), ())), preferred_element_type=F32)


def _sig(x):
    return 1.0 / (1.0 + jnp.exp(-x))


def _colsum(x):
    return jnp.sum(x, axis=0, keepdims=True)


def _rowmean(x):
    return jnp.mean(x, axis=-1, keepdims=True)


def _modnorm_fwd(xv, g, sh, sc):
    r = lax.rsqrt(_rowmean(xv * xv) + EPS)
    xh = xv * r
    n = xh * g
    return n * (1.0 + sc) + sh, xh, n, r


def _modnorm_bwd(dh, xh, n, r, g, sc):
    dsc = _colsum(dh * n)
    dsh = _colsum(dh)
    dn = dh * (1.0 + sc)
    dg = _colsum(dn * xh)
    dxh = dn * g
    dx = r * (dxh - xh * _rowmean(dxh * xh))
    return dx, dsh, dsc, dg


def _ffn_fwd(x, mod, mo, gnorm, w_in, w_out, res, name, carry):
    T = x.shape[0]
    tm = min(512, T)
    tn = D_FF // 2
    nj = D_FF // tn

    def body(x_ref, mod_ref, g_ref, wa_ref, wb_ref, wo_ref, xo_ref, a_ref, b_ref, f_ref, h_ref, acc_scr):
        j = pl.program_id(1)

        @pl.when(j == 0)
        def _():
            h, _, _, _ = _modnorm_fwd(x_ref[...], g_ref[...], mod_ref[mo:mo + 1, :], mod_ref[mo + 1:mo + 2, :])
            h_ref[...] = h.astype(ACT)
            acc_scr[...] = jnp.zeros_like(acc_scr)

        h = h_ref[...]
        a = _mm(h, wa_ref[...])
        b = _mm(h, wb_ref[...])
        a_ref[...] = a.astype(ACT)
        b_ref[...] = b.astype(ACT)
        s = a * _sig(a) * b
        acc_scr[...] += _mm(s, wo_ref[...])

        @pl.when(j == nj - 1)
        def _():
            f = acc_scr[...]
            f_ref[...] = f
            xo_ref[...] = x_ref[...] + res * mod_ref[mo + 2:mo + 3, :] * f

    out = _gridded(
        body, carry, name=name, grid=(T // tm, nj),
        in_specs=[
            pl.BlockSpec((tm, D), lambda i, j: (i, 0)),
            pl.BlockSpec((9, D), lambda i, j: (0, 0)),
            pl.BlockSpec((1, D), lambda i, j: (0, 0)),
            pl.BlockSpec((D, tn), lambda i, j: (0, j)),
            pl.BlockSpec((D, tn), lambda i, j: (0, j + nj)),
            pl.BlockSpec((tn, D), lambda i, j: (j, 0)),
        ],
        out_specs=[
            pl.BlockSpec((tm, D), lambda i, j: (i, 0)),
            pl.BlockSpec((tm, tn), lambda i, j: (i, j)),
            pl.BlockSpec((tm, tn), lambda i, j: (i, j)),
            pl.BlockSpec((tm, D), lambda i, j: (i, 0)),
            pl.BlockSpec((tm, D), lambda i, j: (i, 0)),
        ],
        out_shape=[
            jax.ShapeDtypeStruct((T, D), F32),
            jax.ShapeDtypeStruct((T, D_FF), ACT),
            jax.ShapeDtypeStruct((T, D_FF), ACT),
            jax.ShapeDtypeStruct((T, D), F32),
            jax.ShapeDtypeStruct((T, D), ACT),
        ],
        scratch_shapes=[pltpu.VMEM((tm, D), F32)],
    )(x, mod, gnorm, w_in, w_in, w_out)
    return out[:5], out[5:]


def _ffn_bwd(x, h, dxo, f, a, b, mod, mo, gnorm, w_in, w_out, res, name, carry):
    T = x.shape[0]
    tm = min(512, T)
    ni = T // tm
    tn = 256
    nj = D_FF // tn

    def body(x_ref, h_ref, dxo_ref, f_ref, a_ref, b_ref, mod_ref, g_ref, wa_ref, wb_ref, wo_ref,
             dx_ref, dwa_ref, dwb_ref, dwo_ref, sm_ref, dh_scr):
        j = pl.program_id(0)
        i = pl.program_id(1)

        @pl.when(i == 0)
        def _():
            dwa_ref[...] = jnp.zeros_like(dwa_ref)
            dwb_ref[...] = jnp.zeros_like(dwb_ref)
            dwo_ref[...] = jnp.zeros_like(dwo_ref)

        @pl.when(j == 0)
        def _():
            dh_scr[i] = jnp.zeros((tm, D), F32)

        @pl.when((j == 0) & (i == 0))
        def _():
            sm_ref[...] = jnp.zeros_like(sm_ref)

        gate = mod_ref[mo + 2:mo + 3, :]
        hb = h_ref[...]
        dxo_v = dxo_ref[...]
        df = (res * gate * dxo_v).astype(MM)
        av = a_ref[...].astype(F32)
        bv = b_ref[...].astype(F32)
        sg = _sig(av)
        sa = av * sg
        s = (sa * bv).astype(MM)
        ds = _mm_nt(df, wo_ref[...])
        da = (ds * bv * sg * (1.0 + av * (1.0 - sg))).astype(MM)
        db = (ds * sa).astype(MM)
        dwo_ref[...] += _mm_tn(s, df)
        dwa_ref[...] += _mm_tn(hb, da)
        dwb_ref[...] += _mm_tn(hb, db)
        dh_scr[i] += _mm_nt(da, wa_ref[...]) + _mm_nt(db, wb_ref[...])

        @pl.when(j == nj - 1)
        def _():
            sc = mod_ref[mo + 1:mo + 2, :]
            _, xh, n, r = _modnorm_fwd(x_ref[...], g_ref[...], mod_ref[mo:mo + 1, :], sc)
            dxn, dsh, dsc, dg = _modnorm_bwd(dh_scr[i], xh, n, r, g_ref[...], sc)
            dx_ref[...] = dxo_v + dxn
            sm_ref[0:1, :] += dsh
            sm_ref[1:2, :] += dsc
            sm_ref[2:3, :] += _colsum(dxo_v * f_ref[...]) * res
            sm_ref[3:4, :] += dg

    out = _gridded(
        body, carry, name=name, grid=(nj, ni),
        in_specs=[
            pl.BlockSpec((tm, D), lambda j, i: (jnp.where(j == nj - 1, i, 0), 0)),
            pl.BlockSpec((tm, D), lambda j, i: (i, 0)),
            pl.BlockSpec((tm, D), lambda j, i: (i, 0)),
            pl.BlockSpec((tm, D), lambda j, i: (jnp.where(j == nj - 1, i, 0), 0)),
            pl.BlockSpec((tm, tn), lambda j, i: (i, j)),
            pl.BlockSpec((tm, tn), lambda j, i: (i, j)),
            pl.BlockSpec((9, D), lambda j, i: (0, 0)),
            pl.BlockSpec((1, D), lambda j, i: (0, 0)),
            pl.BlockSpec((D, tn), lambda j, i: (0, j)),
            pl.BlockSpec((D, tn), lambda j, i: (0, j + nj)),
            pl.BlockSpec((tn, D), lambda j, i: (j, 0)),
        ],
        out_specs=[
            pl.BlockSpec((tm, D), lambda j, i: (jnp.where(j == nj - 1, i, 0), 0)),
            pl.BlockSpec((D, tn), lambda j, i: (0, j)),
            pl.BlockSpec((D, tn), lambda j, i: (0, j)),
            pl.BlockSpec((tn, D), lambda j, i: (j, 0)),
            pl.BlockSpec((8, D), lambda j, i: (0, 0)),
        ],
        out_shape=[
            jax.ShapeDtypeStruct((T, D), F32),
            jax.ShapeDtypeStruct((D, D_FF), F32),
            jax.ShapeDtypeStruct((D, D_FF), F32),
            jax.ShapeDtypeStruct((D_FF, D), F32),
            jax.ShapeDtypeStruct((8, D), F32),
        ],
        scratch_shapes=[pltpu.VMEM((ni, tm, D), F32)],
    )(x, h, dxo, f, a, b, mod, gnorm, w_in, w_in, w_out)
    return out[:5], out[5:]


def _head(x, target, gfin):
    T = x.shape[0]
    tm = min(512, T)
    ni = T // tm

    def body(x_ref, t_ref, g_ref, dx_ref, sm_ref):
        i = pl.program_id(0)

        @pl.when(i == 0)
        def _():
            sm_ref[...] = jnp.zeros_like(sm_ref)

        xv = x_ref[...]
        g = g_ref[...]
        r = lax.rsqrt(_rowmean(xv * xv) + EPS)
        xh = xv * r
        e = xh * g - t_ref[...]
        sm_ref[1:2, :] += _colsum(e * e) * (0.5 / D)
        dy = e * (1.0 / D)
        sm_ref[0:1, :] += _colsum(dy * xh)
        dxh = dy * g
        dx_ref[...] = r * (dxh - xh * _rowmean(dxh * xh))

        @pl.when(i == ni - 1)
        def _():
            sm_ref[1:2, :] = jnp.broadcast_to(jnp.sum(sm_ref[1:2, :], axis=-1, keepdims=True), (1, D))

    return pl.pallas_call(
        body, name="head_loss", grid=(ni,),
        in_specs=[pl.BlockSpec((tm, D), lambda i: (i, 0)), pl.BlockSpec((tm, D), lambda i: (i, 0)),
                  pl.BlockSpec((1, D), lambda i: (0, 0))],
        out_specs=[pl.BlockSpec((tm, D), lambda i: (i, 0)), pl.BlockSpec((8, D), lambda i: (0, 0))],
        out_shape=[jax.ShapeDtypeStruct((T, D), F32), jax.ShapeDtypeStruct((8, D), F32)],
        compiler_params=_cparams(1),
    )(x, target, gfin)


def _mixin_fwd(x, mod, mo, gnorm, w, carry):
    T = x.shape[0]
    tm = min(1024, T)

    def body(x_ref, mod_ref, g_ref, w_ref, p_ref, h_ref):
        @pl.when(pl.program_id(1) == 0)
        def _():
            h, _, _, _ = _modnorm_fwd(x_ref[...], g_ref[...], mod_ref[mo:mo + 1, :], mod_ref[mo + 1:mo + 2, :])
            h_ref[...] = h.astype(ACT)

        p_ref[0] = _mm(h_ref[...], w_ref[0])

    out = _gridded(
        body, carry, name="mixin_fwd", grid=(T // tm, 8),
        in_specs=[pl.BlockSpec((tm, D), lambda i, k: (i, 0)), pl.BlockSpec((9, D), lambda i, k: (0, 0)),
                  pl.BlockSpec((1, D), lambda i, k: (0, 0)), pl.BlockSpec((1, D, D), lambda i, k: (k, 0, 0))],
        out_specs=[pl.BlockSpec((1, tm, D), lambda i, k: (k, i, 0)), pl.BlockSpec((tm, D), lambda i, k: (i, 0))],
        out_shape=[jax.ShapeDtypeStruct((8, T, D), F32), jax.ShapeDtypeStruct((T, D), ACT)],
    )(x, mod, gnorm, w)
    return out[:2], out[2:]


def _mixin_bwd(x, h, dxo, dp, mod, mo, gnorm, w, carry):
    T = x.shape[0]
    tm = min(512, T)
    ni = T // tm

    def body(x_ref, h_ref, dxo_ref, dp_ref, mod_ref, g_ref, w_ref, dx_ref, dw_ref, sm_ref, dh_scr):
        k = pl.program_id(0)
        i = pl.program_id(1)

        @pl.when(i == 0)
        def _():
            dw_ref[...] = jnp.zeros_like(dw_ref)

        @pl.when(k == 0)
        def _():
            dh_scr[i] = jnp.zeros((tm, D), F32)

        @pl.when((k == 0) & (i == 0))
        def _():
            sm_ref[...] = jnp.zeros_like(sm_ref)

        dpk = dp_ref[0].astype(MM)
        dw_ref[0] += _mm_tn(h_ref[...], dpk)
        dh_scr[i] += _mm_nt(dpk, w_ref[0])

        @pl.when(k == 7)
        def _():
            sc = mod_ref[mo + 1:mo + 2, :]
            _, xh, n, r = _modnorm_fwd(x_ref[...], g_ref[...], mod_ref[mo:mo + 1, :], sc)
            dxn, dsh, dsc, dg = _modnorm_bwd(dh_scr[i], xh, n, r, g_ref[...], sc)
            dx_ref[...] = dxo_ref[...] + dxn
            sm_ref[0:1, :] += dsh
            sm_ref[1:2, :] += dsc
            sm_ref[3:4, :] += dg

    out = _gridded(
        body, carry, name="mixin_bwd", grid=(8, ni),
        in_specs=[pl.BlockSpec((tm, D), lambda k, i: (jnp.where(k == 7, i, 0), 0)),
                  pl.BlockSpec((tm, D), lambda k, i: (i, 0)),
                  pl.BlockSpec((tm, D), lambda k, i: (jnp.where(k == 7, i, 0), 0)),
                  pl.BlockSpec((1, tm, D), lambda k, i: (k, i, 0)), pl.BlockSpec((9, D), lambda k, i: (0, 0)),
                  pl.BlockSpec((1, D), lambda k, i: (0, 0)), pl.BlockSpec((1, D, D), lambda k, i: (k, 0, 0))],
        out_specs=[pl.BlockSpec((tm, D), lambda k, i: (jnp.where(k == 7, i, 0), 0)),
                   pl.BlockSpec((1, D, D), lambda k, i: (k, 0, 0)),
                   pl.BlockSpec((8, D), lambda k, i: (0, 0))],
        out_shape=[jax.ShapeDtypeStruct((T, D), F32), jax.ShapeDtypeStruct((8, D, D), F32),
                   jax.ShapeDtypeStruct((8, D), F32)],
        scratch_shapes=[pltpu.VMEM((ni, tm, D), F32)],
    )(x, h, dxo, dp, mod, gnorm, w)
    return out[:3], out[3:]


def _hgrn_consts():
    rows = jnp.arange(SUB * HD) // HD
    e = (rows[:, None] == jnp.arange(HD)[None, :]).astype(MM)
    return e, e.T


def _rows_bcast(ref, cb, first, n):
    parts = [jnp.broadcast_to(ref[pl.ds(c * CHUNK + first, 1), :], (n, HD)) for c in range(cb // CHUNK)]
    return jnp.concatenate(parts, axis=0)


def _hgrn_pre(qr, fr, lb_ref, b_scr, cb):
    z = lb_ref[...]
    lb = _sig(z[0:1, :] - z[1:2, :])
    sq = _sig(qr)
    q = qr * sq * Q_SCALE
    sf = _sig(fr)
    fg = lb + (1.0 - lb) * sf
    lf = jnp.log(fg)
    k = 1.0 - fg
    tl = lax.broadcasted_iota(jnp.int32, (cb, HD), 0) % CHUNK
    bc = lf
    sh = 1
    while sh < CHUNK:
        bc = bc + jnp.where(tl >= sh, pltpu.roll(bc, sh, 0), 0.0)
        sh *= 2
    b_scr[...] = bc
    bl = _rows_bcast(b_scr, cb, CHUNK - 1, CHUNK)
    br = [None] + [_rows_bcast(b_scr, cb, SUB * i - 1, CHUNK) for i in range(1, NSUB)]
    sb = tl // SUB
    bref = jnp.where(sb == 0, bc, jnp.where(sb == 1, br[1], jnp.where(sb == 2, br[2], br[3])))
    eb = jnp.exp(bc)
    ekd = jnp.exp(bl - bc)
    eqo = jnp.exp(bc - bref)
    eko = [None] + [jnp.exp(jnp.where(tl < SUB * i, br[i] - bc, NEG)) for i in range(1, NSUB)]
    return dict(lb=lb, sq=sq, q=q, sf=sf, fg=fg, k=k, tl=tl, sb=sb, b=bc, bl=bl, eb=eb, ekd=ekd, eqo=eqo,
                eko=eko, qe=q * eb, kd=k * ekd, qo=q * eqo, ko=[None] + [k * eko[i] for i in range(1, NSUB)])


def _pad_rows(x):
    return jnp.concatenate([x, jnp.zeros_like(x)], axis=0)


def _by_subblock(sbc, parts):
    out = jnp.zeros_like(parts[1])
    for i in range(1, NSUB):
        out = jnp.where(sbc == i, parts[i], out)
    return out


def _hgrn_fwd(p, hgrn_lb, hgrn_g, carry):
    T = p.shape[1]
    cb = min(512, T)
    nch = cb // CHUNK
    ncb = T // cb
    e_mat, _ = _hgrn_consts()

    def body(p_ref, lb_ref, g_ref, e_ref, o_ref, oa_ref, a_ref, s_ref, st_scr, q_scr, k_scr, b_scr, z_scr):
        @pl.when(pl.program_id(1) == 0)
        def _():
            st_scr[...] = jnp.zeros_like(st_scr)

        v = p_ref[2]
        og = p_ref[3]
        pre = _hgrn_pre(p_ref[0], p_ref[1], lb_ref, b_scr, cb)
        q_scr[...] = pre["q"]
        k_scr[...] = pre["k"]
        ti = lax.broadcasted_iota(jnp.int32, (SUB, HD), 0)

        def zbody(c, carry):
            for i in range(NSUB):
                r0 = pl.multiple_of(c * CHUNK + SUB * i, SUB)
                qi = q_scr[pl.ds(r0, SUB), :]
                bi = b_scr[pl.ds(r0, SUB), :]
                for s in range(SUB):
                    krow = k_scr[pl.ds(r0 + s, 1), :]
                    brow = b_scr[pl.ds(r0 + s, 1), :]
                    zz = qi * krow * jnp.exp(jnp.where(ti >= s, bi - brow, NEG))
                    z_scr[i, pl.ds(pl.multiple_of(c * SUB, SUB), SUB), s * HD:(s + 1) * HD] = zz.astype(MM)
            return carry

        lax.fori_loop(0, nch, zbody, 0)
        adiag = [_mm(z_scr[i], e_ref[...]) for i in range(NSUB)]
        sbc = lax.broadcasted_iota(jnp.int32, (CHUNK, HD), 0) // SUB
        o_parts = []
        st = st_scr[...]
        for c in range(nch):
            rs = slice(c * CHUNK, (c + 1) * CHUNK)
            qo_c = pre["qo"][rs]
            offs = [None] + [_mm_nt(qo_c, _pad_rows(pre["ko"][i][rs])) for i in range(1, NSUB)]
            a_c = _by_subblock(sbc, offs)
            dparts = []
            for i in range(NSUB):
                blk = adiag[i][c * SUB:(c + 1) * SUB]
                dparts.append(blk if i == 0 else pltpu.roll(blk, SUB * i, 1))
            a_c = a_c + jnp.concatenate(dparts, axis=0)
            a_ref[0, rs, :] = a_c
            s_ref[0, c] = st
            o_c = _mm(a_c, _pad_rows(v[rs])) + _mm_nt(pre["qe"][rs], st)
            st = st * jnp.exp(b_scr[pl.ds(c * CHUNK + CHUNK - 1, 1), :]) + _mm_tn(v[rs], pre["kd"][rs])
            o_parts.append(o_c)
        st_scr[...] = st
        o = jnp.concatenate(o_parts, axis=0)
        o_ref[...] = o
        on = o * lax.rsqrt(_rowmean(o * o) + EPS) * g_ref[...]
        oa_ref[...] = (on * og * _sig(og)).astype(ACT)

    out = _gridded(
        body, carry, name="hgrn_fwd", grid=(HEADS, ncb),
        in_specs=[pl.BlockSpec((4, cb, HD), lambda h, c: (0, c, h)),
                  pl.BlockSpec((2, HD), lambda h, c: (0, h)),
                  pl.BlockSpec((1, HD), lambda h, c: (0, h)),
                  pl.BlockSpec((SUB * HD, HD), lambda h, c: (0, 0))],
        out_specs=[pl.BlockSpec((cb, HD), lambda h, c: (c, h)),
                   pl.BlockSpec((cb, HD), lambda h, c: (c, h)),
                   pl.BlockSpec((1, cb, HD), lambda h, c: (h, c, 0)),
                   pl.BlockSpec((1, nch, HD, HD), lambda h, c: (h, c, 0, 0))],
        out_shape=[jax.ShapeDtypeStruct((T, D), F32), jax.ShapeDtypeStruct((T, D), ACT),
                   jax.ShapeDtypeStruct((HEADS, T, HD), F32),
                   jax.ShapeDtypeStruct((HEADS, T // CHUNK, HD, HD), F32)],
        scratch_shapes=[pltpu.VMEM((HD, HD), F32), pltpu.VMEM((cb, HD), F32), pltpu.VMEM((cb, HD), F32),
                        pltpu.VMEM((cb, HD), F32), pltpu.VMEM((NSUB, nch * SUB, SUB * HD), MM)],
    )(p, hgrn_lb, hgrn_g, e_mat)
    return out[:4], out[4:]


def _hgrn_bwd(p, o, a_all, s_all, doa, hgrn_lb, hgrn_g, dp, carry):
    T = p.shape[1]
    cb = min(512, T)
    nch = cb // CHUNK
    ncb = T // cb
    _, et_mat = _hgrn_consts()

    def body(p_ref, o_ref, a_ref, s_ref, doa_ref, lb_ref, g_ref, et_ref, dp_in, dp_ref, sm_ref,
             dst_scr, q_scr, k_scr, b_scr, x_scr, dqd_scr, dkd_scr):
        del dp_in

        @pl.when(pl.program_id(1) == 0)
        def _():
            dst_scr[...] = jnp.zeros_like(dst_scr)
            sm_ref[...] = jnp.zeros_like(sm_ref)

        qr = p_ref[0]
        v = p_ref[2]
        og = p_ref[3]
        pre = _hgrn_pre(qr, p_ref[1], lb_ref, b_scr, cb)
        q, k = pre["q"], pre["k"]
        q_scr[...] = q
        k_scr[...] = k
        g = g_ref[...]
        ov = o_ref[...]
        r = lax.rsqrt(_rowmean(ov * ov) + EPS)
        oh = ov * r
        sgo = _sig(og)
        doa_v = doa_ref[...]
        don = doa_v * og * sgo
        dog = doa_v * oh * g * sgo * (1.0 + og * (1.0 - sgo))
        sm_ref[1:2, :] += _colsum(don * oh)
        doh = don * g
        do = r * (doh - oh * _rowmean(doh * oh))

        sbc = lax.broadcasted_iota(jnp.int32, (CHUNK, HD), 0) // SUB
        row_i = lax.broadcasted_iota(jnp.int32, (CHUNK, HD), 0)
        lane_i = lax.broadcasted_iota(jnp.int32, (CHUNK, HD), 1)
        causal = lane_i <= row_i
        da_parts, dv_parts, dqoff_parts, dkoff_parts = [], [], [], []
        for c in range(nch):
            rs = slice(c * CHUNK, (c + 1) * CHUNK)
            do_c = do[rs]
            da_c = jnp.where(causal, _mm_nt(do_c, _pad_rows(v[rs])), 0.0)
            da_parts.append(da_c)
            dv_parts.append(_mm_tn(a_ref[0, rs, :], do_c)[:CHUNK])
            qo_c = pre["qo"][rs]
            dqoff_parts.append(_by_subblock(
                sbc, [None] + [_mm(da_c, _pad_rows(pre["ko"][i][rs])) for i in range(1, NSUB)]))
            dko = jnp.zeros((CHUNK, HD), F32)
            for i in range(1, NSUB):
                dko = dko + pre["eko"][i][rs] * _mm_tn(jnp.where(sbc == i, da_c, 0.0), qo_c)[:CHUNK]
            dkoff_parts.append(dko)
        for i in range(NSUB):
            rows = []
            for c in range(nch):
                blk = da_parts[c][SUB * i:SUB * (i + 1)]
                rows.append(blk if i == 0 else pltpu.roll(blk, HD - SUB * i, 1))
            x_scr[i] = _mm(jnp.concatenate(rows, axis=0), et_ref[...])
        ti = lax.broadcasted_iota(jnp.int32, (SUB, HD), 0)

        def dbody(c, carry):
            for i in range(NSUB):
                r0 = pl.multiple_of(c * CHUNK + SUB * i, SUB)
                qi = q_scr[pl.ds(r0, SUB), :]
                bi = b_scr[pl.ds(r0, SUB), :]
                dq_acc = jnp.zeros((SUB, HD), F32)
                dk_acc = jnp.zeros((SUB, HD), F32)
                for s in range(SUB):
                    krow = k_scr[pl.ds(r0 + s, 1), :]
                    brow = b_scr[pl.ds(r0 + s, 1), :]
                    xs = x_scr[i, pl.ds(pl.multiple_of(c * SUB, SUB), SUB), s * HD:(s + 1) * HD]
                    w = xs * jnp.exp(jnp.where(ti >= s, bi - brow, NEG))
                    dq_acc = dq_acc + w * krow
                    dk_acc = jnp.where(ti == s, _colsum(w * qi), dk_acc)
                dqd_scr[pl.ds(r0, SUB), :] = dq_acc
                dkd_scr[pl.ds(r0, SUB), :] = dk_acc
            return carry

        lax.fori_loop(0, nch, dbody, 0)
        dqe_parts, dkdec_parts, dvi_parts, debl_parts = [None] * nch, [None] * nch, [None] * nch, [None] * nch
        dst = dst_scr[...]
        for c in reversed(range(nch)):
            rs = slice(c * CHUNK, (c + 1) * CHUNK)
            st = s_ref[0, c]
            do_c = do[rs]
            dqe_parts[c] = _mm(do_c, st)
            dkdec_parts[c] = _mm(v[rs], dst)
            dvi_parts[c] = _mm_nt(pre["kd"][rs], dst)
            debl_parts[c] = _colsum(dst * st)
            dst = dst * jnp.exp(b_scr[pl.ds(c * CHUNK + CHUNK - 1, 1), :]) + _mm_tn(do_c, pre["qe"][rs])
        dst_scr[...] = dst
        dqe = jnp.concatenate(dqe_parts, axis=0)
        dkdec = jnp.concatenate(dkdec_parts, axis=0)
        dq_tot = jnp.concatenate(dqoff_parts, axis=0) * pre["eqo"] + dqd_scr[...] + dqe * pre["eb"]
        dk_inter = dkdec * pre["ekd"]
        dk_tot = jnp.concatenate(dkoff_parts, axis=0) + dkd_scr[...] + dk_inter
        db = q * dq_tot - k * dk_tot
        kdk = k * dk_inter
        dbl = jnp.concatenate(
            [jnp.broadcast_to(jnp.exp(b_scr[pl.ds(c * CHUNK + CHUNK - 1, 1), :]) * debl_parts[c]
                              + _colsum(kdk[c * CHUNK:(c + 1) * CHUNK]), (CHUNK, HD)) for c in range(nch)], axis=0)
        tl = pre["tl"]
        rc = db
        sh = 1
        while sh < CHUNK:
            rc = rc + jnp.where(tl + sh < CHUNK, pltpu.roll(rc, cb - sh, 0), 0.0)
            sh *= 2
        dlf = rc + dbl
        dfg = dlf / pre["fg"] - dk_tot
        sf = pre["sf"]
        lb = pre["lb"]
        sm_ref[0:1, :] += _colsum(dfg * (1.0 - sf))
        sq = pre["sq"]
        dp_ref[0] = dq_tot * Q_SCALE * sq * (1.0 + qr * (1.0 - sq))
        dp_ref[1] = dfg * (1.0 - lb) * sf * (1.0 - sf)
        dp_ref[2] = jnp.concatenate(dv_parts, axis=0) + jnp.concatenate(dvi_parts, axis=0)
        dp_ref[3] = dog

    rev = lambda c: ncb - 1 - c
    out = _gridded(
        body, carry, name="hgrn_bwd", grid=(HEADS, ncb),
        in_specs=[pl.BlockSpec((4, cb, HD), lambda h, c: (0, rev(c), h)),
                  pl.BlockSpec((cb, HD), lambda h, c: (rev(c), h)),
                  pl.BlockSpec((1, cb, HD), lambda h, c: (h, rev(c), 0)),
                  pl.BlockSpec((1, nch, HD, HD), lambda h, c: (h, rev(c), 0, 0)),
                  pl.BlockSpec((cb, HD), lambda h, c: (rev(c), h)),
                  pl.BlockSpec((2, HD), lambda h, c: (0, h)),
                  pl.BlockSpec((1, HD), lambda h, c: (0, h)),
                  pl.BlockSpec((HD, SUB * HD), lambda h, c: (0, 0)),
                  pl.BlockSpec(memory_space=pl.ANY)],
        out_specs=[pl.BlockSpec((4, cb, HD), lambda h, c: (0, rev(c), h)),
                   pl.BlockSpec((8, HD), lambda h, c: (0, h))],
        out_shape=[jax.ShapeDtypeStruct(dp.shape, F32), jax.ShapeDtypeStruct((8, D), F32)],
        aliases={8: 0},
        scratch_shapes=[pltpu.VMEM((HD, HD), F32), pltpu.VMEM((cb, HD), F32), pltpu.VMEM((cb, HD), F32),
                        pltpu.VMEM((cb, HD), F32), pltpu.VMEM((NSUB, nch * SUB, SUB * HD), F32),
                        pltpu.VMEM((cb, HD), F32), pltpu.VMEM((cb, HD), F32)],
    )(p, o, a_all, s_all, doa, hgrn_lb, hgrn_g, et_mat, dp)
    return out[:2], out[2:]


def _ln_fwd(u1, g, b):
    mu = _rowmean(u1)
    xc = u1 - mu
    rs = lax.rsqrt(_rowmean(xc * xc) + EPS)
    xh = xc * rs
    return xh * g + b, xh, rs


CONV_RB = 64
LANES = 128


def _shift_rows(src, sh, ls, n):
    for r in range(1, 8):
        sh[r - 1, 0:n, :] = src[pl.ds(r, n), ls]


def _tap(src, sh, ls, off, r0, rows):
    r = off % 8
    if r == 0:
        return src[pl.ds(r0 + off, rows), ls]
    return sh[r - 1, pl.ds(r0 + off - r, rows), :]


def _conv_fwd(p, cw, cb_, lng, lnb):
    T = p.shape[1]
    tm = min(512, T)
    n = HALO + tm - 8

    def body(p_ref, cw_ref, cb_ref, g_ref, b_ref, u1_ref, u2_ref, buf, sh):
        @pl.when(pl.program_id(0) == 0)
        def _():
            buf[0:HALO, :] = jnp.zeros((HALO, D), F32)

        buf[HALO:HALO + tm, :] = p_ref[0] * _sig(p_ref[1])
        for lb in range(D // LANES):
            ls = slice(lb * LANES, (lb + 1) * LANES)
            _shift_rows(buf, sh, ls, n)
            taps = [cw_ref[j:j + 1, ls] for j in range(CONV_K)]
            bias = cb_ref[:, ls]

            def rows_body(rb, carry):
                r0 = pl.multiple_of(rb * CONV_RB, CONV_RB)
                acc = jnp.broadcast_to(bias, (CONV_RB, LANES))
                for j in range(CONV_K):
                    acc = acc + taps[j] * _tap(buf, sh, ls, HALO - (CONV_K - 1) + j, r0, CONV_RB)
                u1_ref[pl.ds(r0, CONV_RB), ls] = acc
                return carry

            lax.fori_loop(0, tm // CONV_RB, rows_body, 0)
        y, _, _ = _ln_fwd(u1_ref[...], g_ref[...], b_ref[...])
        u2_ref[...] = (y * _sig(y)).astype(ACT)
        buf[0:HALO, :] = buf[tm:tm + HALO, :]

    return pl.pallas_call(
        body, name="conv_fwd", grid=(T // tm,),
        in_specs=[pl.BlockSpec((2, tm, D), lambda i: (2, i, 0)), pl.BlockSpec((HALO, D), lambda i: (0, 0)),
                  pl.BlockSpec((1, D), lambda i: (0, 0)), pl.BlockSpec((1, D), lambda i: (0, 0)),
                  pl.BlockSpec((1, D), lambda i: (0, 0))],
        out_specs=[pl.BlockSpec((tm, D), lambda i: (i, 0)), pl.BlockSpec((tm, D), lambda i: (i, 0))],
        out_shape=[jax.ShapeDtypeStruct((T, D), F32), jax.ShapeDtypeStruct((T, D), ACT)],
        scratch_shapes=[pltpu.VMEM((HALO + tm, D), F32), pltpu.VMEM((7, n, LANES), F32)],
        compiler_params=_cparams(1),
    )(p, cw, cb_, lng, lnb)


def _conv_bwd(p, u1, du2, cw, lng, lnb, dp):
    T = p.shape[1]
    tm = min(512, T)
    ni = T // tm
    hb = tm // HALO

    n = HALO + tm - 8

    def body(p_ref, ph_ref, u1_ref, du2_ref, cw_ref, g_ref, b_ref, dp_in, dp_ref, dcw_ref, sm_ref, ubuf, dbuf,
             sh, dacc):
        del dp_in
        step = pl.program_id(0)

        @pl.when(step == 0)
        def _():
            dbuf[tm:tm + HALO, :] = jnp.zeros((HALO, D), F32)
            dcw_ref[...] = jnp.zeros_like(dcw_ref)
            sm_ref[...] = jnp.zeros_like(sm_ref)

        ua = p_ref[0]
        sgb = _sig(p_ref[1])
        halo = ph_ref[0] * _sig(ph_ref[1])
        ubuf[0:HALO, :] = jnp.where(step == ni - 1, 0.0, halo)
        ubuf[HALO:HALO + tm, :] = ua * sgb
        g = g_ref[...]
        y, xh, rs = _ln_fwd(u1_ref[...], g, b_ref[...])
        sy = _sig(y)
        dy = du2_ref[...] * sy * (1.0 + y * (1.0 - sy))
        sm_ref[1:2, :] += _colsum(dy * xh)
        sm_ref[2:3, :] += _colsum(dy)
        dxh = dy * g
        du1 = rs * (dxh - _rowmean(dxh) - xh * _rowmean(dxh * xh))
        sm_ref[0:1, :] += _colsum(du1)
        dbuf[0:tm, :] = du1
        for lb in range(D // LANES):
            ls = slice(lb * LANES, (lb + 1) * LANES)
            taps = [cw_ref[j:j + 1, ls] for j in range(CONV_K)]
            _shift_rows(dbuf, sh, ls, n)

            def du0_body(rb, carry):
                r0 = pl.multiple_of(rb * CONV_RB, CONV_RB)
                acc = jnp.zeros((CONV_RB, LANES), F32)
                for j in range(CONV_K):
                    acc = acc + taps[j] * _tap(dbuf, sh, ls, CONV_K - 1 - j, r0, CONV_RB)
                dp_ref[0, pl.ds(r0, CONV_RB), ls] = acc
                return carry

            lax.fori_loop(0, tm // CONV_RB, du0_body, 0)
            _shift_rows(ubuf, sh, ls, n)
            dacc[...] = jnp.zeros_like(dacc)

            def dcw_body(rb, carry):
                r0 = pl.multiple_of(rb * CONV_RB, CONV_RB)
                d = dbuf[pl.ds(r0, CONV_RB), ls]
                for j in range(CONV_K):
                    prod = d * _tap(ubuf, sh, ls, HALO - (CONV_K - 1) + j, r0, CONV_RB)
                    dacc[8 * j:8 * j + 8, :] += jnp.sum(prod.reshape(CONV_RB // 8, 8, LANES), axis=0)
                return carry

            lax.fori_loop(0, tm // CONV_RB, dcw_body, 0)
            for j in range(CONV_K):
                dcw_ref[j:j + 1, ls] += _colsum(dacc[8 * j:8 * j + 8, :])
        du0 = dp_ref[0]
        dp_ref[0] = du0 * sgb
        dp_ref[1] = du0 * ua * sgb * (1.0 - sgb)
        dbuf[tm:tm + HALO, :] = dbuf[0:HALO, :]

    rev = lambda i: ni - 1 - i
    return pl.pallas_call(
        body, name="conv_bwd", grid=(ni,),
        in_specs=[pl.BlockSpec((2, tm, D), lambda i: (2, rev(i), 0)),
                  pl.BlockSpec((2, HALO, D), lambda i: (2, jnp.maximum(rev(i) * hb - 1, 0), 0)),
                  pl.BlockSpec((tm, D), lambda i: (rev(i), 0)), pl.BlockSpec((tm, D), lambda i: (rev(i), 0)),
                  pl.BlockSpec((HALO, D), lambda i: (0, 0)), pl.BlockSpec((1, D), lambda i: (0, 0)),
                  pl.BlockSpec((1, D), lambda i: (0, 0)), pl.BlockSpec(memory_space=pl.ANY)],
        out_specs=[pl.BlockSpec((2, tm, D), lambda i: (2, rev(i), 0)),
                   pl.BlockSpec((HALO, D), lambda i: (0, 0)), pl.BlockSpec((8, D), lambda i: (0, 0))],
        out_shape=[jax.ShapeDtypeStruct(dp.shape, F32), jax.ShapeDtypeStruct((HALO, D), F32),
                   jax.ShapeDtypeStruct((8, D), F32)],
        input_output_aliases={7: 0},
        scratch_shapes=[pltpu.VMEM((HALO + tm, D), F32), pltpu.VMEM((tm + HALO, D), F32),
                        pltpu.VMEM((7, n, LANES), F32), pltpu.VMEM((8 * CONV_K, LANES), F32)],
        compiler_params=_cparams(1),
    )(p, p, u1, du2, cw, lng, lnb, dp)


def _mixout_fwd(x, oa, u2, p, mod, mo, w_a, w_b, w_o):
    T = x.shape[0]
    tm = min(512, T)

    def body(x_ref, oa_ref, u2_ref, p_ref, mod_ref, wa_ref, wb_ref, wo_ref, xo_ref, ya_ref, yb_ref, mo_ref):
        ya = _mm(oa_ref[...], wa_ref[...])
        yb = _mm(u2_ref[...], wb_ref[...])
        ya_ref[...] = ya.astype(ACT)
        yb_ref[...] = yb.astype(ACT)
        merged = _sig(p_ref[0]) * ya + _sig(p_ref[1]) * yb
        out = _mm(merged, wo_ref[...])
        mo_ref[...] = out
        xo_ref[...] = x_ref[...] + mod_ref[mo + 2:mo + 3, :] * out

    tile = pl.BlockSpec((tm, D), lambda i: (i, 0))
    wspec = pl.BlockSpec((D, D), lambda i: (0, 0))
    return pl.pallas_call(
        body, name="mixout_fwd", grid=(T // tm,),
        in_specs=[tile, tile, tile, pl.BlockSpec((2, tm, D), lambda i: (3, i, 0)),
                  pl.BlockSpec((9, D), lambda i: (0, 0)), wspec, wspec, wspec],
        out_specs=[tile, tile, tile, tile],
        out_shape=[jax.ShapeDtypeStruct((T, D), F32), jax.ShapeDtypeStruct((T, D), ACT),
                   jax.ShapeDtypeStruct((T, D), ACT), jax.ShapeDtypeStruct((T, D), F32)],
        compiler_params=_cparams(1),
    )(x, oa, u2, p, mod, w_a, w_b, w_o)


def _mixout_bwd(dxo, oa, u2, ya, yb, mout, p, mod, mo, w_a, w_b, w_o):
    T = dxo.shape[0]
    tm = min(256, T)

    def body(dxo_ref, oa_ref, u2_ref, ya_ref, yb_ref, mo_ref, p_ref, mod_ref, wa_ref, wb_ref, wo_ref,
             dp_ref, doa_ref, du2_ref, dwa_ref, dwb_ref, dwo_ref, sm_ref):
        @pl.when(pl.program_id(0) == 0)
        def _():
            dwa_ref[...] = jnp.zeros_like(dwa_ref)
            dwb_ref[...] = jnp.zeros_like(dwb_ref)
            dwo_ref[...] = jnp.zeros_like(dwo_ref)
            sm_ref[...] = jnp.zeros_like(sm_ref)

        dxo_v = dxo_ref[...]
        sm_ref[2:3, :] += _colsum(dxo_v * mo_ref[...])
        dmo = (mod_ref[mo + 2:mo + 3, :] * dxo_v).astype(MM)
        ya = ya_ref[...].astype(F32)
        yb = yb_ref[...].astype(F32)
        sga = _sig(p_ref[0])
        sgb = _sig(p_ref[1])
        merged = (sga * ya + sgb * yb).astype(MM)
        dwo_ref[...] += _mm_tn(merged, dmo)
        dmg = _mm_nt(dmo, wo_ref[...])
        dp_ref[0] = dmg * ya * sga * (1.0 - sga)
        dp_ref[1] = dmg * yb * sgb * (1.0 - sgb)
        dya = (dmg * sga).astype(MM)
        dyb = (dmg * sgb).astype(MM)
        dwa_ref[...] += _mm_tn(oa_ref[...], dya)
        dwb_ref[...] += _mm_tn(u2_ref[...], dyb)
        doa_ref[...] = _mm_nt(dya, wa_ref[...])
        du2_ref[...] = _mm_nt(dyb, wb_ref[...])

    tile = pl.BlockSpec((tm, D), lambda i: (i, 0))
    wspec = pl.BlockSpec((D, D), lambda i: (0, 0))
    return pl.pallas_call(
        body, name="mixout_bwd", grid=(T // tm,),
        in_specs=[tile, tile, tile, tile, tile, tile, pl.BlockSpec((2, tm, D), lambda i: (3, i, 0)),
                  pl.BlockSpec((9, D), lambda i: (0, 0)), wspec, wspec, wspec],
        out_specs=[pl.BlockSpec((2, tm, D), lambda i: (3, i, 0)), tile, tile, wspec, wspec, wspec,
                   pl.BlockSpec((8, D), lambda i: (0, 0))],
        out_shape=[jax.ShapeDtypeStruct((8, T, D), F32), jax.ShapeDtypeStruct((T, D), F32),
                   jax.ShapeDtypeStruct((T, D), F32), jax.ShapeDtypeStruct((D, D), F32),
                   jax.ShapeDtypeStruct((D, D), F32), jax.ShapeDtypeStruct((D, D), F32),
                   jax.ShapeDtypeStruct((8, D), F32)],
        compiler_params=_cparams(1),
    )(dxo, oa, u2, ya, yb, mout, p, mod, w_a, w_b, w_o)


def _ada_fwd(cs_all, ada_w, ada_b_cols):
    def body(cs_ref, w_ref, b_ref, out_ref):
        out_ref[...] = jnp.dot(cs_ref[...], w_ref[...], preferred_element_type=F32,
                               precision=lax.Precision.HIGHEST) + b_ref[...]

    return pl.pallas_call(
        body, name="ada_fwd", out_shape=jax.ShapeDtypeStruct((N_DEV, ada_w.shape[1]), F32),
        compiler_params=pltpu.CompilerParams(vmem_limit_bytes=VMEM_LIMIT),
    )(cs_all, ada_w, ada_b_cols)


def _ada_wgrad(cs_all, dmod_cols):
    cs_t = jnp.pad(cs_all.T, ((0, 0), (0, HD - N_DEV)))
    dm = jnp.pad(dmod_cols, ((0, HD - N_DEV), (0, 0)))

    def body(cs_ref, d_ref, out_ref):
        out_ref[...] = jnp.dot(cs_ref[...], d_ref[...], preferred_element_type=F32,
                               precision=lax.Precision.HIGHEST)

    return pl.pallas_call(
        body, name="ada_wgrad", out_shape=jax.ShapeDtypeStruct((D, dmod_cols.shape[1]), F32),
        compiler_params=pltpu.CompilerParams(vmem_limit_bytes=VMEM_LIMIT),
    )(cs_t, dm)


def _adam_math(w, g, m, v):
    m2 = ADAM_B1 * m + (1.0 - ADAM_B1) * g
    v2 = ADAM_B2 * v + (1.0 - ADAM_B2) * (g * g)
    m_hat = m2 / (1.0 - ADAM_B1 ** ADAM_STEP)
    v_hat = v2 / (1.0 - ADAM_B2 ** ADAM_STEP)
    delta = -ADAM_LR * (m_hat / (jnp.sqrt(v_hat) + ADAM_EPS) + ADAM_WD * w)
    return delta, m2, v2


def _adamw(w, m, v, g, name):
    R, C = w.shape
    slots = g.ndim == 3
    tr = R
    for cand in (256, 176):
        if R % cand == 0 and R > cand:
            tr = cand
            break

    def body(w_ref, m_ref, v_ref, g_ref, go_ref, d_ref, mo_ref, vo_ref):
        if slots:
            gv = g_ref[0].astype(F32)
            for s in range(1, N_DEV):
                gv = gv + g_ref[s].astype(F32)
        else:
            gv = g_ref[...]
        go_ref[...] = gv
        d_ref[...], mo_ref[...], vo_ref[...] = _adam_math(w_ref[...], gv, m_ref[...], v_ref[...])

    tile = pl.BlockSpec((tr, C), lambda i: (i, 0))
    gspec = pl.BlockSpec((N_DEV, tr, C), lambda i: (0, i, 0)) if slots else tile
    sds = jax.ShapeDtypeStruct((R, C), F32)
    return pl.pallas_call(
        body, name=name, grid=(R // tr,), in_specs=[tile, tile, tile, gspec], out_specs=[tile] * 4,
        out_shape=[sds] * 4, compiler_params=_cparams(1),
    )(w, m, v, g)


def _sum_slots(pack):
    def body(p_ref, out_ref):
        acc = p_ref[0]
        for s in range(1, N_DEV):
            acc = acc + p_ref[s]
        out_ref[...] = acc

    return pl.pallas_call(body, name="sum_small", out_shape=jax.ShapeDtypeStruct(pack.shape[1:], F32))(pack)


def _me():
    return lax.axis_index("x"), lax.axis_index("y"), lax.axis_index("c")


def _peer(r):
    x, y, c = _me()
    px = 1 - x if r & 4 else x
    py = 1 - y if r & 2 else y
    pc = 1 - c if r & 1 else c
    return (px, py, pc), 4 * px + 2 * py + pc


def _allgather_small(x):
    R, C = x.shape

    def body(x_ref, out_ref, send_sems, recv_sems):
        mx, my, mc = _me()
        me = 4 * mx + 2 * my + mc
        mine = out_ref.at[pl.ds(pl.multiple_of(me * R, 8), R), :]
        copies = []
        for r in range(1, N_DEV):
            dev, _ = _peer(r)
            copies.append(pltpu.make_async_remote_copy(
                src_ref=x_ref, dst_ref=mine, send_sem=send_sems.at[r - 1], recv_sem=recv_sems.at[r - 1],
                device_id=dev, device_id_type=MESH))
        for cp in copies:
            cp.start()
        mine[...] = x_ref[...]
        for r in range(1, N_DEV):
            _, idx = _peer(r)
            theirs = out_ref.at[pl.ds(pl.multiple_of(idx * R, 8), R), :]
            pltpu.make_async_remote_copy(
                src_ref=x_ref, dst_ref=theirs, send_sem=send_sems.at[r - 1], recv_sem=recv_sems.at[r - 1],
                device_id=_peer(r)[0], device_id_type=MESH).wait_recv()
        for cp in copies:
            cp.wait_send()

    return pl.pallas_call(
        body, name="allgather_small_%dx%d" % (R, C),
        out_shape=jax.ShapeDtypeStruct((N_DEV * R, C), F32),
        in_specs=[pl.BlockSpec(memory_space=pltpu.VMEM)], out_specs=pl.BlockSpec(memory_space=pltpu.VMEM),
        scratch_shapes=[pltpu.SemaphoreType.DMA((N_DEV - 1,)), pltpu.SemaphoreType.DMA((N_DEV - 1,))],
    )(x)


def _xchg_copies(ins, outs, sems, gather):
    send_sems, recv_sems, local_sems = sems
    mx, my, mc = _me()
    me = 4 * mx + 2 * my + mc
    sibling = _peer(1)[0]

    def rdma(a, r, dev, src, slot):
        k = a * (N_DEV - 1) + r - 1
        return pltpu.make_async_remote_copy(
            src_ref=src, dst_ref=outs[a].at[slot], send_sem=send_sems.at[k], recv_sem=recv_sems.at[k],
            device_id=dev, device_id_type=MESH)

    own, sends, relays, recvs = [], [], [], []
    for a in range(len(ins)):
        own.append(pltpu.make_async_copy(ins[a] if gather else ins[a].at[me], outs[a].at[me], local_sems.at[a]))
        for r in range(1, N_DEV):
            dev, idx = _peer(r)
            if not gather:
                sends.append(rdma(a, r, dev, ins[a].at[idx], me))
                recvs.append(rdma(a, r, dev, ins[a].at[idx], idx))
            elif r == 1:
                sends.append(rdma(a, r, dev, ins[a], me))
                recvs.append(rdma(a, r, dev, ins[a], idx))
            elif r % 2 == 0:
                sends.append(rdma(a, r, dev, ins[a], me))
                relays.append((rdma(a, r, dev, ins[a], idx), rdma(a, r + 1, sibling, outs[a].at[idx], idx)))
            else:
                recvs.append(rdma(a, r, sibling, ins[a], idx))
    return own, sends, relays, recvs


def _xchg_start(ins, outs, sems, gather):
    own, sends, _, _ = _xchg_copies(ins, outs, sems, gather)
    for cp in own + sends:
        cp.start()


def _xchg_wait(ins, outs, sems, gather):
    own, sends, relays, recvs = _xchg_copies(ins, outs, sems, gather)
    for arrival, relay in relays:
        arrival.wait_recv()
        relay.start()
    for cp in recvs:
        cp.wait_recv()
    for cp in own:
        cp.wait()
    for cp in sends + [relay for _, relay in relays]:
        cp.wait_send()


def _xchg_specs(arrays, gather):
    n = len(arrays)
    out_shape = [jax.ShapeDtypeStruct(((N_DEV,) + a.shape) if gather else a.shape, a.dtype) for a in arrays]
    sems = [pltpu.SemaphoreType.DMA((n * (N_DEV - 1),)), pltpu.SemaphoreType.DMA((n * (N_DEV - 1),)),
            pltpu.SemaphoreType.DMA((n,))]
    return out_shape, sems


def _exchange(arrays, gather, name):
    n = len(arrays)

    def body(*refs):
        _xchg_start(refs[:n], refs[n:2 * n], refs[2 * n:], gather)
        _xchg_wait(refs[:n], refs[n:2 * n], refs[2 * n:], gather)

    out_shape, sems = _xchg_specs(arrays, gather)
    return pl.pallas_call(
        body, name=name, out_shape=out_shape,
        in_specs=[pl.BlockSpec(memory_space=pl.ANY)] * n, out_specs=[pl.BlockSpec(memory_space=pl.ANY)] * n,
        scratch_shapes=sems,
    )(*arrays)


def _gridded(body, carry, *, name, grid, in_specs, out_specs, out_shape, scratch_shapes=(), aliases=None):
    if carry is None:
        return pl.pallas_call(
            body, name=name, grid=grid, in_specs=list(in_specs), out_specs=list(out_specs),
            out_shape=list(out_shape), scratch_shapes=list(scratch_shapes), input_output_aliases=aliases or {},
            compiler_params=_cparams(len(grid)))
    arrays, gather = carry
    n, n_in, n_out, n_scr = len(arrays), len(in_specs), len(out_specs), len(scratch_shapes)
    c_shape, c_sems = _xchg_specs(arrays, gather)

    def wrapped(*refs):
        ins, cin = refs[:n_in], refs[n_in:n_in + n]
        o0 = n_in + n
        outs, cout = refs[o0:o0 + n_out], refs[o0 + n_out:o0 + n_out + n]
        s0 = o0 + n_out + n
        scr, sems = refs[s0:s0 + n_scr], refs[s0 + n_scr:]
        first = pl.program_id(0) == 0
        last = pl.program_id(0) == grid[0] - 1
        for ax in range(1, len(grid)):
            first = first & (pl.program_id(ax) == 0)
            last = last & (pl.program_id(ax) == grid[ax] - 1)

        @pl.when(first)
        def _():
            _xchg_start(cin, cout, sems, gather)

        body(*ins, *outs, *scr)

        @pl.when(last)
        def _():
            _xchg_wait(cin, cout, sems, gather)

    hbm = pl.BlockSpec(memory_space=pl.ANY)
    res = pl.pallas_call(
        wrapped, name=name, grid=grid, in_specs=list(in_specs) + [hbm] * n, out_specs=list(out_specs) + [hbm] * n,
        out_shape=list(out_shape) + c_shape, scratch_shapes=list(scratch_shapes) + c_sems,
        input_output_aliases=aliases or {}, compiler_params=_cparams(len(grid)),
    )
    return lambda *args: res(*args, *arrays)


def _local_step(x, target, mod, small, sh):
    w1_in, w1_out = _exchange([sh["ffn1_w_in"], sh["ffn1_w_out"]], True, "allgather_ffn1")
    w1_in, w1_out = _full_w_in(w1_in), w1_out.reshape(D_FF, D)
    (x1, a1, b1, f1, h1), (wm_in,) = _ffn_fwd(x, mod, 0, small["norm_ffn1"], w1_in, w1_out, 0.5, "ffn1_fwd",
                                              ([sh["mix_w_in"]], True))
    (p, h2), (wh_o, wc_o, wm_o, cw) = _mixin_fwd(
        x1, mod, 3, small["norm_mix"], wm_in,
        ([sh["hgrn_w_o"], sh["conv_w_o"], sh["mix_w_out"], sh["conv_w"]], True))
    wh_o, wc_o, wm_o = wh_o.reshape(D, D), wc_o.reshape(D, D), wm_o.reshape(D, D)
    cw = jnp.pad(cw.transpose(1, 0, 2).reshape(CONV_K, D), ((0, HALO - CONV_K), (0, 0)))
    (o, oa, a_all, s_all), (w2_in, w2_out) = _hgrn_fwd(p, small["hgrn_lb"], small["hgrn_g"],
                                                       ([sh["ffn2_w_in"], sh["ffn2_w_out"]], True))
    w2_in, w2_out = _full_w_in(w2_in), w2_out.reshape(D_FF, D)
    u1, u2 = _conv_fwd(p, cw, small["conv_b"], small["conv_ln_g"], small["conv_ln_b"])
    x2, ya, yb, mout = _mixout_fwd(x1, oa, u2, p, mod, 3, wh_o, wc_o, wm_o)
    (x3, a3, b3, f3, h3), _ = _ffn_fwd(x2, mod, 6, small["norm_ffn2"], w2_in, w2_out, 0.5, "ffn2_fwd", None)
    dx3, sm_head = _head(x3, target, small["norm_final"])

    (dx2, dw2_a, dw2_b, dw2_out, sm3), _ = _ffn_bwd(x2, h3, dx3, f3, a3, b3, mod, 6, small["norm_ffn2"], w2_in,
                                                    w2_out, 0.5, "ffn2_bwd", None)
    dp, doa, du2, dwh_o, dwc_o, dwm_o, sm_mo = _mixout_bwd(dx2, oa, u2, ya, yb, mout, p, mod, 3, wh_o, wc_o, wm_o)
    dp, dcw, sm_cv = _conv_bwd(p, u1, du2, cw, small["conv_ln_g"], small["conv_ln_b"], dp)
    rows = lambda t: t.reshape(N_DEV, -1, D).astype(MM)
    (dp, sm_hg), (r2_in, r2_out) = _hgrn_bwd(p, o, a_all, s_all, doa, small["hgrn_lb"], small["hgrn_g"], dp,
                                             ([_w_in_shards(dw2_a, dw2_b), rows(dw2_out)], False))
    (dx1, dwm_in, sm2), (rh_o, rc_o, rm_o, rcw) = _mixin_bwd(
        x1, h2, dx2, dp, mod, 3, small["norm_mix"], wm_in,
        ([rows(dwh_o), rows(dwc_o), rows(dwm_o), dcw[:CONV_K].reshape(CONV_K, N_DEV, -1).transpose(1, 0, 2)], False))
    (dx0, dw1_a, dw1_b, dw1_out, sm1), (rm_in,) = _ffn_bwd(x, h1, dx1, f1, a1, b1, mod, 0, small["norm_ffn1"],
                                                          w1_in, w1_out, 0.5, "ffn1_bwd",
                                                          ([dwm_in.astype(MM)], False))
    r1_in, r1_out = _exchange([_w_in_shards(dw1_a, dw1_b), rows(dw1_out)], False, "scatter_ffn1")

    dmod = jnp.concatenate([sm1[0:3], sm2[0:2], sm_mo[2:3], sm3[0:3]], axis=0)
    gsmall = dict(norm_ffn1=sm1[3:4], norm_mix=sm2[3:4], lb0=sm_hg[0:1], hgrn_g=sm_hg[1:2], conv_b=sm_cv[0:1],
                  conv_ln_g=sm_cv[1:2], conv_ln_b=sm_cv[2:3], norm_ffn2=sm3[3:4], norm_final=sm_head[0:1])
    recv = dict(ffn1_w_in=r1_in, ffn1_w_out=r1_out, mix_w_in=rm_in, hgrn_w_o=rh_o, conv_w=rcw, conv_w_o=rc_o,
                mix_w_out=rm_o, ffn2_w_in=r2_in, ffn2_w_out=r2_out)
    return sm_head[1, 0], dx0, dmod, gsmall, recv


def _full_w_in(g):
    return g.transpose(1, 0, 2).reshape(D, -1)


def _w_in_shards(dwa, dwb):
    half = N_DEV // 2
    return jnp.concatenate([t.reshape(D, half, -1).transpose(1, 0, 2).astype(MM) for t in (dwa, dwb)], axis=0)


SMALL_ORDER = ("norm_ffn1", "norm_mix", "lb0", "hgrn_g", "conv_b", "conv_ln_g", "conv_ln_b", "norm_ffn2",
               "norm_final")
PACK_ROWS = 24


def kernel(x, c, ada_w, ada_b, norm_ffn1, ffn1_w_in, ffn1_w_out, norm_mix, mix_w_in, hgrn_lb, hgrn_g, hgrn_w_o, conv_w, conv_b, conv_ln_g, conv_ln_b, conv_w_o, mix_w_out, norm_ffn2, ffn2_w_in, ffn2_w_out, norm_final, loss_target, m_ada_w, m_ada_b, m_norm_ffn1, m_ffn1_w_in, m_ffn1_w_out, m_norm_mix, m_mix_w_in, m_hgrn_lb, m_hgrn_g, m_hgrn_w_o, m_conv_w, m_conv_b, m_conv_ln_g, m_conv_ln_b, m_conv_w_o, m_mix_w_out, m_norm_ffn2, m_ffn2_w_in, m_ffn2_w_out, m_norm_final, v_ada_w, v_ada_b, v_norm_ffn1, v_ffn1_w_in, v_ffn1_w_out, v_norm_mix, v_mix_w_in, v_hgrn_lb, v_hgrn_g, v_hgrn_w_o, v_conv_w, v_conv_b, v_conv_ln_g, v_conv_ln_b, v_conv_w_o, v_mix_w_out, v_norm_ffn2, v_ffn2_w_in, v_ffn2_w_out, v_norm_final):
    mx, my, mc = _me()
    me = 4 * mx + 2 * my + mc
    ncol = ada_w.shape[2]

    cs = jnp.broadcast_to(c * jax.nn.sigmoid(c), (8, D))
    cs_all = _allgather_small(cs).reshape(N_DEV, 8, D)[:, 0, :]
    ada_b_cols = lax.dynamic_slice(ada_b, (0, me * ncol), (1, ncol))
    mod_cols = _ada_fwd(cs_all, ada_w[0], ada_b_cols)
    mod_all = _allgather_small(mod_cols).reshape(N_DEV, N_DEV, ncol)
    mod = lax.dynamic_index_in_dim(mod_all, me, axis=1, keepdims=False).reshape(9, D)

    sh = dict(ffn1_w_in=ffn1_w_in, ffn1_w_out=ffn1_w_out, mix_w_in=mix_w_in, hgrn_w_o=hgrn_w_o,
              conv_w_o=conv_w_o, mix_w_out=mix_w_out, ffn2_w_in=ffn2_w_in, ffn2_w_out=ffn2_w_out)
    sh = {n: w[0].astype(MM) for n, w in sh.items()}
    sh["conv_w"] = conv_w[0]
    small = dict(norm_ffn1=norm_ffn1, norm_mix=norm_mix, hgrn_lb=hgrn_lb, hgrn_g=hgrn_g, conv_b=conv_b,
                 conv_ln_g=conv_ln_g, conv_ln_b=conv_ln_b, norm_ffn2=norm_ffn2, norm_final=norm_final.reshape(1, D))

    loss_local, dx, dmod, gsmall, recv = _local_step(x[0], loss_target[0], mod, small, sh)
    loss = lax.psum(loss_local, ("x", "y", "c"))

    pack = jnp.concatenate([dmod] + [gsmall[n] for n in SMALL_ORDER]
                           + [jnp.zeros((PACK_ROWS - 9 - len(SMALL_ORDER), D), F32)], axis=0)
    pack_all = _allgather_small(pack).reshape(N_DEV, PACK_ROWS, D)
    tot = _sum_slots(pack_all)
    gs = {n: tot[9 + i:10 + i] for i, n in enumerate(SMALL_ORDER)}
    dmod_all = pack_all[:, 0:9, :].reshape(N_DEV, 9 * D)
    g_ada_b = tot[0:9].reshape(1, 9 * D)
    g_ada_w = _ada_wgrad(cs_all, lax.dynamic_slice(dmod_all, (0, me * ncol), (N_DEV, ncol)))
    z = hgrn_lb.astype(F32)
    p0 = jax.nn.sigmoid(z[0:1] - z[1:2])
    dz0 = p0 * (1.0 - p0) * gs["lb0"]
    g_hgrn_lb = jnp.concatenate([dz0, -dz0], axis=0)

    res = {}
    res["ada_w"] = _adamw(ada_w[0], m_ada_w[0], v_ada_w[0], g_ada_w, "adamw_ada_w")
    big = dict(ffn1_w_in=(ffn1_w_in, m_ffn1_w_in, v_ffn1_w_in), ffn1_w_out=(ffn1_w_out, m_ffn1_w_out, v_ffn1_w_out),
               mix_w_in=(mix_w_in, m_mix_w_in, v_mix_w_in), hgrn_w_o=(hgrn_w_o, m_hgrn_w_o, v_hgrn_w_o),
               conv_w=(conv_w, m_conv_w, v_conv_w), conv_w_o=(conv_w_o, m_conv_w_o, v_conv_w_o),
               mix_w_out=(mix_w_out, m_mix_w_out, v_mix_w_out), ffn2_w_in=(ffn2_w_in, m_ffn2_w_in, v_ffn2_w_in),
               ffn2_w_out=(ffn2_w_out, m_ffn2_w_out, v_ffn2_w_out))
    for n, (w, m, v) in big.items():
        res[n] = _adamw(w[0], m[0], v[0], recv[n], "adamw_" + n)
    sm_names = ("ada_b", "norm_ffn1", "norm_mix", "hgrn_lb", "hgrn_g", "conv_b", "conv_ln_g", "conv_ln_b",
                "norm_ffn2", "norm_final")
    sm_w = dict(ada_b=(ada_b, m_ada_b, v_ada_b), norm_ffn1=(norm_ffn1, m_norm_ffn1, v_norm_ffn1),
                norm_mix=(norm_mix, m_norm_mix, v_norm_mix), hgrn_lb=(hgrn_lb, m_hgrn_lb, v_hgrn_lb),
                hgrn_g=(hgrn_g, m_hgrn_g, v_hgrn_g), conv_b=(conv_b, m_conv_b, v_conv_b),
                conv_ln_g=(conv_ln_g, m_conv_ln_g, v_conv_ln_g), conv_ln_b=(conv_ln_b, m_conv_ln_b, v_conv_ln_b),
                norm_ffn2=(norm_ffn2, m_norm_ffn2, v_norm_ffn2), norm_final=(norm_final, m_norm_final, v_norm_final))
    sm_g = dict(gs, ada_b=g_ada_b, hgrn_lb=g_hgrn_lb)
    rows = {n: sm_w[n][0].size // D for n in sm_names}
    n_rows = sum(rows.values())
    pad = (-n_rows) % 8
    stack = lambda parts: jnp.concatenate([q.reshape(-1, D) for q in parts] + [jnp.ones((pad, D), F32)], axis=0)
    st = _adamw(stack([sm_w[n][0] for n in sm_names]), stack([sm_w[n][1] for n in sm_names]),
                stack([sm_w[n][2] for n in sm_names]), stack([sm_g[n] for n in sm_names]), "adamw_small")
    off = 0
    for n in sm_names:
        res[n] = tuple(t[off:off + rows[n]].reshape(sm_w[n][0].shape) for t in st)
        off += rows[n]

    order = ("ada_w", "ada_b", "norm_ffn1", "ffn1_w_in", "ffn1_w_out", "norm_mix", "mix_w_in", "hgrn_lb", "hgrn_g",
             "hgrn_w_o", "conv_w", "conv_b", "conv_ln_g", "conv_ln_b", "conv_w_o", "mix_w_out", "norm_ffn2",
             "ffn2_w_in", "ffn2_w_out", "norm_final")
    lead = lambda n, t: t[None] if n in big or n == "ada_w" else t
    outs = [loss, dx[None]]
    for j in range(4):
        outs += [lead(n, res[n][j]) for n in order]
    return tuple(outs)
```

```python
import functools

import jax
import jax.numpy as jnp
from jax import lax
from jax.experimental import pallas as pl
from jax.experimental.pallas import tpu as pltpu

F32 = jnp.float32
MM = jnp.bfloat16
ACT = jnp.bfloat16

D = 1024
D_FF = 2816
HEADS = 8
HD = 128
CHUNK = 64
SUB = 16
NSUB = CHUNK // SUB
CONV_K = 31
HALO = 32
EPS = 1e-6
N_DEV = 8
NEG = -1e30
Q_SCALE = HD ** -0.5

ADAM_LR = 0.001
ADAM_B1 = 0.9
ADAM_B2 = 0.999
ADAM_EPS = 1e-08
ADAM_WD = 0.01
ADAM_STEP = 10

VMEM_LIMIT = 60 * 1024 * 1024
MESH = pl.DeviceIdType.MESH


def _cparams(n_axes):
    return pltpu.CompilerParams(dimension_semantics=("arbitrary",) * n_axes, vmem_limit_bytes=VMEM_LIMIT)


def _mm(a, b):
    return lax.dot_general(a.astype(MM), b.astype(MM), (((1,), (0,)), ((), ())), preferred_element_type=F32)


def _mm_nt(a, b):
    return lax.dot_general(a.astype(MM), b.astype(MM), (((1,), (1,)), ((), ())), preferred_element_type=F32)


def _mm_tn(a, b):
    return lax.dot_general(a.astype(MM), b.astype(MM), (((0,), (0,)), ((), ())), preferred_element_type=F32)


def _sig(x):
    return 1.0 / (1.0 + jnp.exp(-x))


def _colsum(x):
    return jnp.sum(x, axis=0, keepdims=True)


def _rowmean(x):
    return jnp.mean(x, axis=-1, keepdims=True)


def _modnorm_fwd(xv, g, sh, sc):
    r = lax.rsqrt(_rowmean(xv * xv) + EPS)
    xh = xv * r
    n = xh * g
    return n * (1.0 + sc) + sh, xh, n, r


def _modnorm_bwd(dh, xh, n, r, g, sc):
    dsc = _colsum(dh * n)
    dsh = _colsum(dh)
    dn = dh * (1.0 + sc)
    dg = _colsum(dn * xh)
    dxh = dn * g
    dx = r * (dxh - xh * _rowmean(dxh * xh))
    return dx, dsh, dsc, dg


def _ffn_fwd(x, mod, mo, gnorm, w_in, w_out, res, name, carry):
    T = x.shape[0]
    tm = min(512, T)
    tn = D_FF // 2
    nj = D_FF // tn

    def body(x_ref, mod_ref, g_ref, wa_ref, wb_ref, wo_ref, xo_ref, a_ref, b_ref, f_ref, h_ref, acc_scr):
        j = pl.program_id(1)

        @pl.when(j == 0)
        def _():
            h, _, _, _ = _modnorm_fwd(x_ref[...], g_ref[...], mod_ref[mo:mo + 1, :], mod_ref[mo + 1:mo + 2, :])
            h_ref[...] = h.astype(ACT)
            acc_scr[...] = jnp.zeros_like(acc_scr)

        h = h_ref[...]
        a = _mm(h, wa_ref[...])
        b = _mm(h, wb_ref[...])
        a_ref[...] = a.astype(ACT)
        b_ref[...] = b.astype(ACT)
        s = a * _sig(a) * b
        acc_scr[...] += _mm(s, wo_ref[...])

        @pl.when(j == nj - 1)
        def _():
            f = acc_scr[...]
            f_ref[...] = f
            xo_ref[...] = x_ref[...] + res * mod_ref[mo + 2:mo + 3, :] * f

    out = _gridded(
        body, carry, name=name, grid=(T // tm, nj),
        in_specs=[
            pl.BlockSpec((tm, D), lambda i, j: (i, 0)),
            pl.BlockSpec((9, D), lambda i, j: (0, 0)),
            pl.BlockSpec((1, D), lambda i, j: (0, 0)),
            pl.BlockSpec((D, tn), lambda i, j: (0, j)),
            pl.BlockSpec((D, tn), lambda i, j: (0, j + nj)),
            pl.BlockSpec((tn, D), lambda i, j: (j, 0)),
        ],
        out_specs=[
            pl.BlockSpec((tm, D), lambda i, j: (i, 0)),
            pl.BlockSpec((tm, tn), lambda i, j: (i, j)),
            pl.BlockSpec((tm, tn), lambda i, j: (i, j)),
            pl.BlockSpec((tm, D), lambda i, j: (i, 0)),
            pl.BlockSpec((tm, D), lambda i, j: (i, 0)),
        ],
        out_shape=[
            jax.ShapeDtypeStruct((T, D), F32),
            jax.ShapeDtypeStruct((T, D_FF), ACT),
            jax.ShapeDtypeStruct((T, D_FF), ACT),
            jax.ShapeDtypeStruct((T, D), F32),
            jax.ShapeDtypeStruct((T, D), ACT),
        ],
        scratch_shapes=[pltpu.VMEM((tm, D), F32)],
    )(x, mod, gnorm, w_in, w_in, w_out)
    return out[:5], out[5:]


def _ffn_bwd(x, h, dxo, f, a, b, mod, mo, gnorm, w_in, w_out, res, name, carry):
    T = x.shape[0]
    tm = min(512, T)
    ni = T // tm
    tn = 256
    nj = D_FF // tn

    def body(x_ref, h_ref, dxo_ref, f_ref, a_ref, b_ref, mod_ref, g_ref, wa_ref, wb_ref, wo_ref,
             dx_ref, dwa_ref, dwb_ref, dwo_ref, sm_ref, dh_scr):
        j = pl.program_id(0)
        i = pl.program_id(1)

        @pl.when(i == 0)
        def _():
            dwa_ref[...] = jnp.zeros_like(dwa_ref)
            dwb_ref[...] = jnp.zeros_like(dwb_ref)
            dwo_ref[...] = jnp.zeros_like(dwo_ref)

        @pl.when(j == 0)
        def _():
            dh_scr[i] = jnp.zeros((tm, D), F32)

        @pl.when((j == 0) & (i == 0))
        def _():
            sm_ref[...] = jnp.zeros_like(sm_ref)

        gate = mod_ref[mo + 2:mo + 3, :]
        hb = h_ref[...]
        dxo_v = dxo_ref[...]
        df = (res * gate * dxo_v).astype(MM)
        av = a_ref[...].astype(F32)
        bv = b_ref[...].astype(F32)
        sg = _sig(av)
        sa = av * sg
        s = (sa * bv).astype(MM)
        ds = _mm_nt(df, wo_ref[...])
        da = (ds * bv * sg * (1.0 + av * (1.0 - sg))).astype(MM)
        db = (ds * sa).astype(MM)
        dwo_ref[...] += _mm_tn(s, df)
        dwa_ref[...] += _mm_tn(hb, da)
        dwb_ref[...] += _mm_tn(hb, db)
        dh_scr[i] += _mm_nt(da, wa_ref[...]) + _mm_nt(db, wb_ref[...])

        @pl.when(j == nj - 1)
        def _():
            sc = mod_ref[mo + 1:mo + 2, :]
            _, xh, n, r = _modnorm_fwd(x_ref[...], g_ref[...], mod_ref[mo:mo + 1, :], sc)
            dxn, dsh, dsc, dg = _modnorm_bwd(dh_scr[i], xh, n, r, g_ref[...], sc)
            dx_ref[...] = dxo_v + dxn
            sm_ref[0:1, :] += dsh
            sm_ref[1:2, :] += dsc
            sm_ref[2:3, :] += _colsum(dxo_v * f_ref[...]) * res
            sm_ref[3:4, :] += dg

    out = _gridded(
        body, carry, name=name, grid=(nj, ni),
        in_specs=[
            pl.BlockSpec((tm, D), lambda j, i: (jnp.where(j == nj - 1, i, 0), 0)),
            pl.BlockSpec((tm, D), lambda j, i: (i, 0)),
            pl.BlockSpec((tm, D), lambda j, i: (i, 0)),
            pl.BlockSpec((tm, D), lambda j, i: (jnp.where(j == nj - 1, i, 0), 0)),
            pl.BlockSpec((tm, tn), lambda j, i: (i, j)),
            pl.BlockSpec((tm, tn), lambda j, i: (i, j)),
            pl.BlockSpec((9, D), lambda j, i: (0, 0)),
            pl.BlockSpec((1, D), lambda j, i: (0, 0)),
            pl.BlockSpec((D, tn), lambda j, i: (0, j)),
            pl.BlockSpec((D, tn), lambda j, i: (0, j + nj)),
            pl.BlockSpec((tn, D), lambda j, i: (j, 0)),
        ],
        out_specs=[
            pl.BlockSpec((tm, D), lambda j, i: (jnp.where(j == nj - 1, i, 0), 0)),
            pl.BlockSpec((D, tn), lambda j, i: (0, j)),
            pl.BlockSpec((D, tn), lambda j, i: (0, j)),
            pl.BlockSpec((tn, D), lambda j, i: (j, 0)),
            pl.BlockSpec((8, D), lambda j, i: (0, 0)),
        ],
        out_shape=[
            jax.ShapeDtypeStruct((T, D), F32),
            jax.ShapeDtypeStruct((D, D_FF), F32),
            jax.ShapeDtypeStruct((D, D_FF), F32),
            jax.ShapeDtypeStruct((D_FF, D), F32),
            jax.ShapeDtypeStruct((8, D), F32),
        ],
        scratch_shapes=[pltpu.VMEM((ni, tm, D), F32)],
    )(x, h, dxo, f, a, b, mod, gnorm, w_in, w_in, w_out)
    return out[:5], out[5:]


def _head(x, target, gfin):
    T = x.shape[0]
    tm = min(512, T)
    ni = T // tm

    def body(x_ref, t_ref, g_ref, dx_ref, sm_ref):
        i = pl.program_id(0)

        @pl.when(i == 0)
        def _():
            sm_ref[...] = jnp.zeros_like(sm_ref)

        xv = x_ref[...]
        g = g_ref[...]
        r = lax.rsqrt(_rowmean(xv * xv) + EPS)
        xh = xv * r
        e = xh * g - t_ref[...]
        sm_ref[1:2, :] += _colsum(e * e) * (0.5 / D)
        dy = e * (1.0 / D)
        sm_ref[0:1, :] += _colsum(dy * xh)
        dxh = dy * g
        dx_ref[...] = r * (dxh - xh * _rowmean(dxh * xh))

        @pl.when(i == ni - 1)
        def _():
            sm_ref[1:2, :] = jnp.broadcast_to(jnp.sum(sm_ref[1:2, :], axis=-1, keepdims=True), (1, D))

    return pl.pallas_call(
        body, name="head_loss", grid=(ni,),
        in_specs=[pl.BlockSpec((tm, D), lambda i: (i, 0)), pl.BlockSpec((tm, D), lambda i: (i, 0)),
                  pl.BlockSpec((1, D), lambda i: (0, 0))],
        out_specs=[pl.BlockSpec((tm, D), lambda i: (i, 0)), pl.BlockSpec((8, D), lambda i: (0, 0))],
        out_shape=[jax.ShapeDtypeStruct((T, D), F32), jax.ShapeDtypeStruct((8, D), F32)],
        compiler_params=_cparams(1),
    )(x, target, gfin)


def _mixin_fwd(x, mod, mo, gnorm, w, carry):
    T = x.shape[0]
    tm = min(1024, T)

    def body(x_ref, mod_ref, g_ref, w_ref, p_ref, h_ref):
        @pl.when(pl.program_id(1) == 0)
        def _():
            h, _, _, _ = _modnorm_fwd(x_ref[...], g_ref[...], mod_ref[mo:mo + 1, :], mod_ref[mo + 1:mo + 2, :])
            h_ref[...] = h.astype(ACT)

        p_ref[0] = _mm(h_ref[...], w_ref[0])

    out = _gridded(
        body, carry, name="mixin_fwd", grid=(T // tm, 8),
        in_specs=[pl.BlockSpec((tm, D), lambda i, k: (i, 0)), pl.BlockSpec((9, D), lambda i, k: (0, 0)),
                  pl.BlockSpec((1, D), lambda i, k: (0, 0)), pl.BlockSpec((1, D, D), lambda i, k: (k, 0, 0))],
        out_specs=[pl.BlockSpec((1, tm, D), lambda i, k: (k, i, 0)), pl.BlockSpec((tm, D), lambda i, k: (i, 0))],
        out_shape=[jax.ShapeDtypeStruct((8, T, D), F32), jax.ShapeDtypeStruct((T, D), ACT)],
    )(x, mod, gnorm, w)
    return out[:2], out[2:]


def _mixin_bwd(x, h, dxo, dp, mod, mo, gnorm, w, carry):
    T = x.shape[0]
    tm = min(512, T)
    ni = T // tm

    def body(x_ref, h_ref, dxo_ref, dp_ref, mod_ref, g_ref, w_ref, dx_ref, dw_ref, sm_ref, dh_scr):
        k = pl.program_id(0)
        i = pl.program_id(1)

        @pl.when(i == 0)
        def _():
            dw_ref[...] = jnp.zeros_like(dw_ref)

        @pl.when(k == 0)
        def _():
            dh_scr[i] = jnp.zeros((tm, D), F32)

        @pl.when((k == 0) & (i == 0))
        def _():
            sm_ref[...] = jnp.zeros_like(sm_ref)

        dpk = dp_ref[0].astype(MM)
        dw_ref[0] += _mm_tn(h_ref[...], dpk)
        dh_scr[i] += _mm_nt(dpk, w_ref[0])

        @pl.when(k == 7)
        def _():
            sc = mod_ref[mo + 1:mo + 2, :]
            _, xh, n, r = _modnorm_fwd(x_ref[...], g_ref[...], mod_ref[mo:mo + 1, :], sc)
            dxn, dsh, dsc, dg = _modnorm_bwd(dh_scr[i], xh, n, r, g_ref[...], sc)
            dx_ref[...] = dxo_ref[...] + dxn
            sm_ref[0:1, :] += dsh
            sm_ref[1:2, :] += dsc
            sm_ref[3:4, :] += dg

    out = _gridded(
        body, carry, name="mixin_bwd", grid=(8, ni),
        in_specs=[pl.BlockSpec((tm, D), lambda k, i: (jnp.where(k == 7, i, 0), 0)),
                  pl.BlockSpec((tm, D), lambda k, i: (i, 0)),
                  pl.BlockSpec((tm, D), lambda k, i: (jnp.where(k == 7, i, 0), 0)),
                  pl.BlockSpec((1, tm, D), lambda k, i: (k, i, 0)), pl.BlockSpec((9, D), lambda k, i: (0, 0)),
                  pl.BlockSpec((1, D), lambda k, i: (0, 0)), pl.BlockSpec((1, D, D), lambda k, i: (k, 0, 0))],
        out_specs=[pl.BlockSpec((tm, D), lambda k, i: (jnp.where(k == 7, i, 0), 0)),
                   pl.BlockSpec((1, D, D), lambda k, i: (k, 0, 0)),
                   pl.BlockSpec((8, D), lambda k, i: (0, 0))],
        out_shape=[jax.ShapeDtypeStruct((T, D), F32), jax.ShapeDtypeStruct((8, D, D), F32),
                   jax.ShapeDtypeStruct((8, D), F32)],
        scratch_shapes=[pltpu.VMEM((ni, tm, D), F32)],
    )(x, h, dxo, dp, mod, gnorm, w)
    return out[:3], out[3:]


def _hgrn_consts():
    rows = jnp.arange(SUB * HD) // HD
    e = (rows[:, None] == jnp.arange(HD)[None, :]).astype(MM)
    return e, e.T


def _rows_bcast(ref, cb, first, n):
    parts = [jnp.broadcast_to(ref[pl.ds(c * CHUNK + first, 1), :], (n, HD)) for c in range(cb // CHUNK)]
    return jnp.concatenate(parts, axis=0)


def _hgrn_pre(qr, fr, lb_ref, b_scr, cb):
    z = lb_ref[...]
    lb = _sig(z[0:1, :] - z[1:2, :])
    sq = _sig(qr)
    q = qr * sq * Q_SCALE
    sf = _sig(fr)
    fg = lb + (1.0 - lb) * sf
    lf = jnp.log(fg)
    k = 1.0 - fg
    tl = lax.broadcasted_iota(jnp.int32, (cb, HD), 0) % CHUNK
    bc = lf
    sh = 1
    while sh < CHUNK:
        bc = bc + jnp.where(tl >= sh, pltpu.roll(bc, sh, 0), 0.0)
        sh *= 2
    b_scr[...] = bc
    bl = _rows_bcast(b_scr, cb, CHUNK - 1, CHUNK)
    br = [None] + [_rows_bcast(b_scr, cb, SUB * i - 1, CHUNK) for i in range(1, NSUB)]
    sb = tl // SUB
    bref = jnp.where(sb == 0, bc, jnp.where(sb == 1, br[1], jnp.where(sb == 2, br[2], br[3])))
    eb = jnp.exp(bc)
    ekd = jnp.exp(bl - bc)
    eqo = jnp.exp(bc - bref)
    eko = [None] + [jnp.exp(jnp.where(tl < SUB * i, br[i] - bc, NEG)) for i in range(1, NSUB)]
    return dict(lb=lb, sq=sq, q=q, sf=sf, fg=fg, k=k, tl=tl, sb=sb, b=bc, bl=bl, eb=eb, ekd=ekd, eqo=eqo,
                eko=eko, qe=q * eb, kd=k * ekd, qo=q * eqo, ko=[None] + [k * eko[i] for i in range(1, NSUB)])


def _pad_rows(x):
    return jnp.concatenate([x, jnp.zeros_like(x)], axis=0)


def _by_subblock(sbc, parts):
    out = jnp.zeros_like(parts[1])
    for i in range(1, NSUB):
        out = jnp.where(sbc == i, parts[i], out)
    return out


def _hgrn_fwd(p, hgrn_lb, hgrn_g, carry):
    T = p.shape[1]
    cb = min(512, T)
    nch = cb // CHUNK
    ncb = T // cb
    e_mat, _ = _hgrn_consts()

    def body(p_ref, lb_ref, g_ref, e_ref, o_ref, oa_ref, a_ref, s_ref, st_scr, q_scr, k_scr, b_scr, z_scr):
        @pl.when(pl.program_id(1) == 0)
        def _():
            st_scr[...] = jnp.zeros_like(st_scr)

        v = p_ref[2]
        og = p_ref[3]
        pre = _hgrn_pre(p_ref[0], p_ref[1], lb_ref, b_scr, cb)
        q_scr[...] = pre["q"]
        k_scr[...] = pre["k"]
        ti = lax.broadcasted_iota(jnp.int32, (SUB, HD), 0)

        def zbody(c, carry):
            for i in range(NSUB):
                r0 = pl.multiple_of(c * CHUNK + SUB * i, SUB)
                qi = q_scr[pl.ds(r0, SUB), :]
                bi = b_scr[pl.ds(r0, SUB), :]
                for s in range(SUB):
                    krow = k_scr[pl.ds(r0 + s, 1), :]
                    brow = b_scr[pl.ds(r0 + s, 1), :]
                    zz = qi * krow * jnp.exp(jnp.where(ti >= s, bi - brow, NEG))
                    z_scr[i, pl.ds(pl.multiple_of(c * SUB, SUB), SUB), s * HD:(s + 1) * HD] = zz.astype(MM)
            return carry

        lax.fori_loop(0, nch, zbody, 0)
        adiag = [_mm(z_scr[i], e_ref[...]) for i in range(NSUB)]
        sbc = lax.broadcasted_iota(jnp.int32, (CHUNK, HD), 0) // SUB
        chunks = [slice(c * CHUNK, (c + 1) * CHUNK) for c in range(nch)]
        offs = [[_mm_nt(pre["qo"][rs], _pad_rows(pre["ko"][i][rs])) for i in range(1, NSUB)] for rs in chunks]
        kv = [_mm_tn(v[rs], pre["kd"][rs]) for rs in chunks]
        a_parts = []
        for c in range(nch):
            dparts = []
            for i in range(NSUB):
                blk = adiag[i][c * SUB:(c + 1) * SUB]
                dparts.append(blk if i == 0 else pltpu.roll(blk, SUB * i, 1))
            a_parts.append(_by_subblock(sbc, [None] + offs[c]) + jnp.concatenate(dparts, axis=0))
        a_ref[0] = jnp.concatenate(a_parts, axis=0)
        o_intra = [_mm(a_parts[c], _pad_rows(v[rs])) for c, rs in enumerate(chunks)]
        states = []
        st = st_scr[...]
        for c in range(nch):
            states.append(st)
            st = st * jnp.exp(b_scr[pl.ds(c * CHUNK + CHUNK - 1, 1), :]) + kv[c]
        st_scr[...] = st
        for c in range(nch):
            s_ref[0, c] = states[c]
        o = jnp.concatenate([o_intra[c] + _mm_nt(pre["qe"][rs], states[c]) for c, rs in enumerate(chunks)], axis=0)
        o_ref[...] = o
        on = o * lax.rsqrt(_rowmean(o * o) + EPS) * g_ref[...]
        oa_ref[...] = (on * og * _sig(og)).astype(ACT)

    out = _gridded(
        body, carry, name="hgrn_fwd", grid=(HEADS, ncb),
        in_specs=[pl.BlockSpec((4, cb, HD), lambda h, c: (0, c, h)),
                  pl.BlockSpec((2, HD), lambda h, c: (0, h)),
                  pl.BlockSpec((1, HD), lambda h, c: (0, h)),
                  pl.BlockSpec((SUB * HD, HD), lambda h, c: (0, 0))],
        out_specs=[pl.BlockSpec((cb, HD), lambda h, c: (c, h)),
                   pl.BlockSpec((cb, HD), lambda h, c: (c, h)),
                   pl.BlockSpec((1, cb, HD), lambda h, c: (h, c, 0)),
                   pl.BlockSpec((1, nch, HD, HD), lambda h, c: (h, c, 0, 0))],
        out_shape=[jax.ShapeDtypeStruct((T, D), F32), jax.ShapeDtypeStruct((T, D), ACT),
                   jax.ShapeDtypeStruct((HEADS, T, HD), F32),
                   jax.ShapeDtypeStruct((HEADS, T // CHUNK, HD, HD), F32)],
        scratch_shapes=[pltpu.VMEM((HD, HD), F32), pltpu.VMEM((cb, HD), F32), pltpu.VMEM((cb, HD), F32),
                        pltpu.VMEM((cb, HD), F32), pltpu.VMEM((NSUB, nch * SUB, SUB * HD), MM)],
    )(p, hgrn_lb, hgrn_g, e_mat)
    return out[:4], out[4:]


def _hgrn_bwd(p, o, a_all, s_all, doa, hgrn_lb, hgrn_g, dp, carry):
    T = p.shape[1]
    cb = min(512, T)
    nch = cb // CHUNK
    ncb = T // cb
    _, et_mat = _hgrn_consts()

    def body(p_ref, o_ref, a_ref, s_ref, doa_ref, lb_ref, g_ref, et_ref, dp_in, dp_ref, sm_ref,
             dst_scr, q_scr, k_scr, b_scr, x_scr, dqd_scr, dkd_scr):
        del dp_in

        @pl.when(pl.program_id(1) == 0)
        def _():
            dst_scr[...] = jnp.zeros_like(dst_scr)
            sm_ref[...] = jnp.zeros_like(sm_ref)

        qr = p_ref[0]
        v = p_ref[2]
        og = p_ref[3]
        pre = _hgrn_pre(qr, p_ref[1], lb_ref, b_scr, cb)
        q, k = pre["q"], pre["k"]
        q_scr[...] = q
        k_scr[...] = k
        g = g_ref[...]
        ov = o_ref[...]
        r = lax.rsqrt(_rowmean(ov * ov) + EPS)
        oh = ov * r
        sgo = _sig(og)
        doa_v = doa_ref[...]
        don = doa_v * og * sgo
        dog = doa_v * oh * g * sgo * (1.0 + og * (1.0 - sgo))
        sm_ref[1:2, :] += _colsum(don * oh)
        doh = don * g
        do = r * (doh - oh * _rowmean(doh * oh))

        sbc = lax.broadcasted_iota(jnp.int32, (CHUNK, HD), 0) // SUB
        row_i = lax.broadcasted_iota(jnp.int32, (CHUNK, HD), 0)
        lane_i = lax.broadcasted_iota(jnp.int32, (CHUNK, HD), 1)
        causal = lane_i <= row_i
        chunks = [slice(c * CHUNK, (c + 1) * CHUNK) for c in range(nch)]
        da_parts = [jnp.where(causal, _mm_nt(do[rs], _pad_rows(v[rs])), 0.0) for rs in chunks]
        dv_parts = [_mm_tn(a_ref[0, rs, :], do[rs])[:CHUNK] for rs in chunks]
        dqoff_mm = [[_mm(da_parts[c], _pad_rows(pre["ko"][i][rs])) for i in range(1, NSUB)]
                    for c, rs in enumerate(chunks)]
        dkoff_mm = [[_mm_tn(jnp.where(sbc == i, da_parts[c], 0.0), pre["qo"][rs])[:CHUNK] for i in range(1, NSUB)]
                    for c, rs in enumerate(chunks)]
        dqoff_parts = [_by_subblock(sbc, [None] + dqoff_mm[c]) for c in range(nch)]
        dkoff_parts = []
        for c, rs in enumerate(chunks):
            dko = pre["eko"][1][rs] * dkoff_mm[c][0]
            for i in range(2, NSUB):
                dko = dko + pre["eko"][i][rs] * dkoff_mm[c][i - 1]
            dkoff_parts.append(dko)
        for i in range(NSUB):
            rows = []
            for c in range(nch):
                blk = da_parts[c][SUB * i:SUB * (i + 1)]
                rows.append(blk if i == 0 else pltpu.roll(blk, HD - SUB * i, 1))
            x_scr[i] = _mm(jnp.concatenate(rows, axis=0), et_ref[...])
        ti = lax.broadcasted_iota(jnp.int32, (SUB, HD), 0)

        def dbody(c, carry):
            for i in range(NSUB):
                r0 = pl.multiple_of(c * CHUNK + SUB * i, SUB)
                qi = q_scr[pl.ds(r0, SUB), :]
                bi = b_scr[pl.ds(r0, SUB), :]
                dq_acc = jnp.zeros((SUB, HD), F32)
                dk_acc = jnp.zeros((SUB, HD), F32)
                for s in range(SUB):
                    krow = k_scr[pl.ds(r0 + s, 1), :]
                    brow = b_scr[pl.ds(r0 + s, 1), :]
                    xs = x_scr[i, pl.ds(pl.multiple_of(c * SUB, SUB), SUB), s * HD:(s + 1) * HD]
                    w = xs * jnp.exp(jnp.where(ti >= s, bi - brow, NEG))
                    dq_acc = dq_acc + w * krow
                    dk_acc = jnp.where(ti == s, _colsum(w * qi), dk_acc)
                dqd_scr[pl.ds(r0, SUB), :] = dq_acc
                dkd_scr[pl.ds(r0, SUB), :] = dk_acc
            return carry

        lax.fori_loop(0, nch, dbody, 0)
        qdo = [_mm_tn(do[rs], pre["qe"][rs]) for rs in chunks]
        dsts = [None] * nch
        dst = dst_scr[...]
        for c in reversed(range(nch)):
            dsts[c] = dst
            dst = dst * jnp.exp(b_scr[pl.ds(c * CHUNK + CHUNK - 1, 1), :]) + qdo[c]
        dst_scr[...] = dst
        sts = [s_ref[0, c] for c in range(nch)]
        dqe_parts = [_mm(do[rs], sts[c]) for c, rs in enumerate(chunks)]
        dkdec_parts = [_mm(v[rs], dsts[c]) for c, rs in enumerate(chunks)]
        dvi_parts = [_mm_nt(pre["kd"][rs], dsts[c]) for c, rs in enumerate(chunks)]
        debl_parts = [_colsum(dsts[c] * sts[c]) for c in range(nch)]
        dqe = jnp.concatenate(dqe_parts, axis=0)
        dkdec = jnp.concatenate(dkdec_parts, axis=0)
        dq_tot = jnp.concatenate(dqoff_parts, axis=0) * pre["eqo"] + dqd_scr[...] + dqe * pre["eb"]
        dk_inter = dkdec * pre["ekd"]
        dk_tot = jnp.concatenate(dkoff_parts, axis=0) + dkd_scr[...] + dk_inter
        db = q * dq_tot - k * dk_tot
        kdk = k * dk_inter
        dbl = jnp.concatenate(
            [jnp.broadcast_to(jnp.exp(b_scr[pl.ds(c * CHUNK + CHUNK - 1, 1), :]) * debl_parts[c]
                              + _colsum(kdk[c * CHUNK:(c + 1) * CHUNK]), (CHUNK, HD)) for c in range(nch)], axis=0)
        tl = pre["tl"]
        rc = db
        sh = 1
        while sh < CHUNK:
            rc = rc + jnp.where(tl + sh < CHUNK, pltpu.roll(rc, cb - sh, 0), 0.0)
            sh *= 2
        dlf = rc + dbl
        dfg = dlf / pre["fg"] - dk_tot
        sf = pre["sf"]
        lb = pre["lb"]
        sm_ref[0:1, :] += _colsum(dfg * (1.0 - sf))
        sq = pre["sq"]
        dp_ref[0] = dq_tot * Q_SCALE * sq * (1.0 + qr * (1.0 - sq))
        dp_ref[1] = dfg * (1.0 - lb) * sf * (1.0 - sf)
        dp_ref[2] = jnp.concatenate(dv_parts, axis=0) + jnp.concatenate(dvi_parts, axis=0)
        dp_ref[3] = dog

    rev = lambda c: ncb - 1 - c
    out = _gridded(
        body, carry, name="hgrn_bwd", grid=(HEADS, ncb),
        in_specs=[pl.BlockSpec((4, cb, HD), lambda h, c: (0, rev(c), h)),
                  pl.BlockSpec((cb, HD), lambda h, c: (rev(c), h)),
                  pl.BlockSpec((1, cb, HD), lambda h, c: (h, rev(c), 0)),
                  pl.BlockSpec((1, nch, HD, HD), lambda h, c: (h, rev(c), 0, 0)),
                  pl.BlockSpec((cb, HD), lambda h, c: (rev(c), h)),
                  pl.BlockSpec((2, HD), lambda h, c: (0, h)),
                  pl.BlockSpec((1, HD), lambda h, c: (0, h)),
                  pl.BlockSpec((HD, SUB * HD), lambda h, c: (0, 0)),
                  pl.BlockSpec(memory_space=pl.ANY)],
        out_specs=[pl.BlockSpec((4, cb, HD), lambda h, c: (0, rev(c), h)),
                   pl.BlockSpec((8, HD), lambda h, c: (0, h))],
        out_shape=[jax.ShapeDtypeStruct(dp.shape, F32), jax.ShapeDtypeStruct((8, D), F32)],
        aliases={8: 0},
        scratch_shapes=[pltpu.VMEM((HD, HD), F32), pltpu.VMEM((cb, HD), F32), pltpu.VMEM((cb, HD), F32),
                        pltpu.VMEM((cb, HD), F32), pltpu.VMEM((NSUB, nch * SUB, SUB * HD), F32),
                        pltpu.VMEM((cb, HD), F32), pltpu.VMEM((cb, HD), F32)],
    )(p, o, a_all, s_all, doa, hgrn_lb, hgrn_g, et_mat, dp)
    return out[:2], out[2:]


def _ln_fwd(u1, g, b):
    mu = _rowmean(u1)
    xc = u1 - mu
    rs = lax.rsqrt(_rowmean(xc * xc) + EPS)
    xh = xc * rs
    return xh * g + b, xh, rs


CONV_RB = 64
LANES = 128


def _shift_rows(src, sh, ls, n):
    for r in range(1, 8):
        sh[r - 1, 0:n, :] = src[pl.ds(r, n), ls]


def _tap(src, sh, ls, off, r0, rows):
    r = off % 8
    if r == 0:
        return src[pl.ds(r0 + off, rows), ls]
    return sh[r - 1, pl.ds(r0 + off - r, rows), :]


def _conv_fwd(p, cw, cb_, lng, lnb):
    T = p.shape[1]
    tm = min(512, T)
    n = HALO + tm - 8

    def body(p_ref, cw_ref, cb_ref, g_ref, b_ref, u1_ref, u2_ref, buf, sh):
        @pl.when(pl.program_id(0) == 0)
        def _():
            buf[0:HALO, :] = jnp.zeros((HALO, D), F32)

        buf[HALO:HALO + tm, :] = p_ref[0] * _sig(p_ref[1])
        for lb in range(D // LANES):
            ls = slice(lb * LANES, (lb + 1) * LANES)
            _shift_rows(buf, sh, ls, n)
            taps = [cw_ref[j:j + 1, ls] for j in range(CONV_K)]
            bias = cb_ref[:, ls]

            def rows_body(rb, carry):
                r0 = pl.multiple_of(rb * CONV_RB, CONV_RB)
                acc = jnp.broadcast_to(bias, (CONV_RB, LANES))
                for j in range(CONV_K):
                    acc = acc + taps[j] * _tap(buf, sh, ls, HALO - (CONV_K - 1) + j, r0, CONV_RB)
                u1_ref[pl.ds(r0, CONV_RB), ls] = acc
                return carry

            lax.fori_loop(0, tm // CONV_RB, rows_body, 0)
        y, _, _ = _ln_fwd(u1_ref[...], g_ref[...], b_ref[...])
        u2_ref[...] = (y * _sig(y)).astype(ACT)
        buf[0:HALO, :] = buf[tm:tm + HALO, :]

    return pl.pallas_call(
        body, name="conv_fwd", grid=(T // tm,),
        in_specs=[pl.BlockSpec((2, tm, D), lambda i: (2, i, 0)), pl.BlockSpec((HALO, D), lambda i: (0, 0)),
                  pl.BlockSpec((1, D), lambda i: (0, 0)), pl.BlockSpec((1, D), lambda i: (0, 0)),
                  pl.BlockSpec((1, D), lambda i: (0, 0))],
        out_specs=[pl.BlockSpec((tm, D), lambda i: (i, 0)), pl.BlockSpec((tm, D), lambda i: (i, 0))],
        out_shape=[jax.ShapeDtypeStruct((T, D), F32), jax.ShapeDtypeStruct((T, D), ACT)],
        scratch_shapes=[pltpu.VMEM((HALO + tm, D), F32), pltpu.VMEM((7, n, LANES), F32)],
        compiler_params=_cparams(1),
    )(p, cw, cb_, lng, lnb)


def _conv_bwd(p, u1, du2, cw, lng, lnb, dp):
    T = p.shape[1]
    tm = min(512, T)
    ni = T // tm
    hb = tm // HALO

    n = HALO + tm - 8

    def body(p_ref, ph_ref, u1_ref, du2_ref, cw_ref, g_ref, b_ref, dp_in, dp_ref, dcw_ref, sm_ref, ubuf, dbuf,
             sh, dacc):
        del dp_in
        step = pl.program_id(0)

        @pl.when(step == 0)
        def _():
            dbuf[tm:tm + HALO, :] = jnp.zeros((HALO, D), F32)
            dcw_ref[...] = jnp.zeros_like(dcw_ref)
            sm_ref[...] = jnp.zeros_like(sm_ref)

        ua = p_ref[0]
        sgb = _sig(p_ref[1])
        halo = ph_ref[0] * _sig(ph_ref[1])
        ubuf[0:HALO, :] = jnp.where(step == ni - 1, 0.0, halo)
        ubuf[HALO:HALO + tm, :] = ua * sgb
        g = g_ref[...]
        y, xh, rs = _ln_fwd(u1_ref[...], g, b_ref[...])
        sy = _sig(y)
        dy = du2_ref[...] * sy * (1.0 + y * (1.0 - sy))
        sm_ref[1:2, :] += _colsum(dy * xh)
        sm_ref[2:3, :] += _colsum(dy)
        dxh = dy * g
        du1 = rs * (dxh - _rowmean(dxh) - xh * _rowmean(dxh * xh))
        sm_ref[0:1, :] += _colsum(du1)
        dbuf[0:tm, :] = du1
        for lb in range(D // LANES):
            ls = slice(lb * LANES, (lb + 1) * LANES)
            taps = [cw_ref[j:j + 1, ls] for j in range(CONV_K)]
            _shift_rows(dbuf, sh, ls, n)

            def du0_body(rb, carry):
                r0 = pl.multiple_of(rb * CONV_RB, CONV_RB)
                acc = jnp.zeros((CONV_RB, LANES), F32)
                for j in range(CONV_K):
                    acc = acc + taps[j] * _tap(dbuf, sh, ls, CONV_K - 1 - j, r0, CONV_RB)
                dp_ref[0, pl.ds(r0, CONV_RB), ls] = acc
                return carry

            lax.fori_loop(0, tm // CONV_RB, du0_body, 0)
            _shift_rows(ubuf, sh, ls, n)
            dacc[...] = jnp.zeros_like(dacc)

            def dcw_body(rb, carry):
                r0 = pl.multiple_of(rb * CONV_RB, CONV_RB)
                d = dbuf[pl.ds(r0, CONV_RB), ls]
                for j in range(CONV_K):
                    prod = d * _tap(ubuf, sh, ls, HALO - (CONV_K - 1) + j, r0, CONV_RB)
                    dacc[8 * j:8 * j + 8, :] += jnp.sum(prod.reshape(CONV_RB // 8, 8, LANES), axis=0)
                return carry

            lax.fori_loop(0, tm // CONV_RB, dcw_body, 0)
            for j in range(CONV_K):
                dcw_ref[j:j + 1, ls] += _colsum(dacc[8 * j:8 * j + 8, :])
        du0 = dp_ref[0]
        dp_ref[0] = du0 * sgb
        dp_ref[1] = du0 * ua * sgb * (1.0 - sgb)
        dbuf[tm:tm + HALO, :] = dbuf[0:HALO, :]

    rev = lambda i: ni - 1 - i
    return pl.pallas_call(
        body, name="conv_bwd", grid=(ni,),
        in_specs=[pl.BlockSpec((2, tm, D), lambda i: (2, rev(i), 0)),
                  pl.BlockSpec((2, HALO, D), lambda i: (2, jnp.maximum(rev(i) * hb - 1, 0), 0)),
                  pl.BlockSpec((tm, D), lambda i: (rev(i), 0)), pl.BlockSpec((tm, D), lambda i: (rev(i), 0)),
                  pl.BlockSpec((HALO, D), lambda i: (0, 0)), pl.BlockSpec((1, D), lambda i: (0, 0)),
                  pl.BlockSpec((1, D), lambda i: (0, 0)), pl.BlockSpec(memory_space=pl.ANY)],
        out_specs=[pl.BlockSpec((2, tm, D), lambda i: (2, rev(i), 0)),
                   pl.BlockSpec((HALO, D), lambda i: (0, 0)), pl.BlockSpec((8, D), lambda i: (0, 0))],
        out_shape=[jax.ShapeDtypeStruct(dp.shape, F32), jax.ShapeDtypeStruct((HALO, D), F32),
                   jax.ShapeDtypeStruct((8, D), F32)],
        input_output_aliases={7: 0},
        scratch_shapes=[pltpu.VMEM((HALO + tm, D), F32), pltpu.VMEM((tm + HALO, D), F32),
                        pltpu.VMEM((7, n, LANES), F32), pltpu.VMEM((8 * CONV_K, LANES), F32)],
        compiler_params=_cparams(1),
    )(p, p, u1, du2, cw, lng, lnb, dp)


def _mixout_fwd(x, oa, u2, p, mod, mo, w_a, w_b, w_o):
    T = x.shape[0]
    tm = min(512, T)

    def body(x_ref, oa_ref, u2_ref, p_ref, mod_ref, wa_ref, wb_ref, wo_ref, xo_ref, ya_ref, yb_ref, mo_ref):
        ya = _mm(oa_ref[...], wa_ref[...])
        yb = _mm(u2_ref[...], wb_ref[...])
        ya_ref[...] = ya.astype(ACT)
        yb_ref[...] = yb.astype(ACT)
        merged = _sig(p_ref[0]) * ya + _sig(p_ref[1]) * yb
        out = _mm(merged, wo_ref[...])
        mo_ref[...] = out
        xo_ref[...] = x_ref[...] + mod_ref[mo + 2:mo + 3, :] * out

    tile = pl.BlockSpec((tm, D), lambda i: (i, 0))
    wspec = pl.BlockSpec((D, D), lambda i: (0, 0))
    return pl.pallas_call(
        body, name="mixout_fwd", grid=(T // tm,),
        in_specs=[tile, tile, tile, pl.BlockSpec((2, tm, D), lambda i: (3, i, 0)),
                  pl.BlockSpec((9, D), lambda i: (0, 0)), wspec, wspec, wspec],
        out_specs=[tile, tile, tile, tile],
        out_shape=[jax.ShapeDtypeStruct((T, D), F32), jax.ShapeDtypeStruct((T, D), ACT),
                   jax.ShapeDtypeStruct((T, D), ACT), jax.ShapeDtypeStruct((T, D), F32)],
        compiler_params=_cparams(1),
    )(x, oa, u2, p, mod, w_a, w_b, w_o)


def _mixout_bwd(dxo, oa, u2, ya, yb, mout, p, mod, mo, w_a, w_b, w_o):
    T = dxo.shape[0]
    tm = min(256, T)

    def body(dxo_ref, oa_ref, u2_ref, ya_ref, yb_ref, mo_ref, p_ref, mod_ref, wa_ref, wb_ref, wo_ref,
             dp_ref, doa_ref, du2_ref, dwa_ref, dwb_ref, dwo_ref, sm_ref):
        @pl.when(pl.program_id(0) == 0)
        def _():
            dwa_ref[...] = jnp.zeros_like(dwa_ref)
            dwb_ref[...] = jnp.zeros_like(dwb_ref)
            dwo_ref[...] = jnp.zeros_like(dwo_ref)
            sm_ref[...] = jnp.zeros_like(sm_ref)

        dxo_v = dxo_ref[...]
        sm_ref[2:3, :] += _colsum(dxo_v * mo_ref[...])
        dmo = (mod_ref[mo + 2:mo + 3, :] * dxo_v).astype(MM)
        ya = ya_ref[...].astype(F32)
        yb = yb_ref[...].astype(F32)
        sga = _sig(p_ref[0])
        sgb = _sig(p_ref[1])
        merged = (sga * ya + sgb * yb).astype(MM)
        dwo_ref[...] += _mm_tn(merged, dmo)
        dmg = _mm_nt(dmo, wo_ref[...])
        dp_ref[0] = dmg * ya * sga * (1.0 - sga)
        dp_ref[1] = dmg * yb * sgb * (1.0 - sgb)
        dya = (dmg * sga).astype(MM)
        dyb = (dmg * sgb).astype(MM)
        dwa_ref[...] += _mm_tn(oa_ref[...], dya)
        dwb_ref[...] += _mm_tn(u2_ref[...], dyb)
        doa_ref[...] = _mm_nt(dya, wa_ref[...])
        du2_ref[...] = _mm_nt(dyb, wb_ref[...])

    tile = pl.BlockSpec((tm, D), lambda i: (i, 0))
    wspec = pl.BlockSpec((D, D), lambda i: (0, 0))
    return pl.pallas_call(
        body, name="mixout_bwd", grid=(T // tm,),
        in_specs=[tile, tile, tile, tile, tile, tile, pl.BlockSpec((2, tm, D), lambda i: (3, i, 0)),
                  pl.BlockSpec((9, D), lambda i: (0, 0)), wspec, wspec, wspec],
        out_specs=[pl.BlockSpec((2, tm, D), lambda i: (3, i, 0)), tile, tile, wspec, wspec, wspec,
                   pl.BlockSpec((8, D), lambda i: (0, 0))],
        out_shape=[jax.ShapeDtypeStruct((8, T, D), F32), jax.ShapeDtypeStruct((T, D), F32),
                   jax.ShapeDtypeStruct((T, D), F32), jax.ShapeDtypeStruct((D, D), F32),
                   jax.ShapeDtypeStruct((D, D), F32), jax.ShapeDtypeStruct((D, D), F32),
                   jax.ShapeDtypeStruct((8, D), F32)],
        compiler_params=_cparams(1),
    )(dxo, oa, u2, ya, yb, mout, p, mod, w_a, w_b, w_o)


def _ada_fwd(cs_all, ada_w, ada_b_cols):
    def body(cs_ref, w_ref, b_ref, out_ref):
        out_ref[...] = jnp.dot(cs_ref[...], w_ref[...], preferred_element_type=F32,
                               precision=lax.Precision.HIGHEST) + b_ref[...]

    return pl.pallas_call(
        body, name="ada_fwd", out_shape=jax.ShapeDtypeStruct((N_DEV, ada_w.shape[1]), F32),
        compiler_params=pltpu.CompilerParams(vmem_limit_bytes=VMEM_LIMIT),
    )(cs_all, ada_w, ada_b_cols)


def _ada_wgrad(cs_all, dmod_cols):
    cs_t = jnp.pad(cs_all.T, ((0, 0), (0, HD - N_DEV)))
    dm = jnp.pad(dmod_cols, ((0, HD - N_DEV), (0, 0)))

    def body(cs_ref, d_ref, out_ref):
        out_ref[...] = jnp.dot(cs_ref[...], d_ref[...], preferred_element_type=F32,
                               precision=lax.Precision.HIGHEST)

    return pl.pallas_call(
        body, name="ada_wgrad", out_shape=jax.ShapeDtypeStruct((D, dmod_cols.shape[1]), F32),
        compiler_params=pltpu.CompilerParams(vmem_limit_bytes=VMEM_LIMIT),
    )(cs_t, dm)


def _adam_math(w, g, m, v):
    m2 = ADAM_B1 * m + (1.0 - ADAM_B1) * g
    v2 = ADAM_B2 * v + (1.0 - ADAM_B2) * (g * g)
    m_hat = m2 / (1.0 - ADAM_B1 ** ADAM_STEP)
    v_hat = v2 / (1.0 - ADAM_B2 ** ADAM_STEP)
    delta = -ADAM_LR * (m_hat / (jnp.sqrt(v_hat) + ADAM_EPS) + ADAM_WD * w)
    return delta, m2, v2


def _adamw(w, m, v, g, name):
    R, C = w.shape
    slots = g.ndim == 3
    tr = R
    for cand in (256, 176):
        if R % cand == 0 and R > cand:
            tr = cand
            break

    def body(w_ref, m_ref, v_ref, g_ref, go_ref, d_ref, mo_ref, vo_ref):
        if slots:
            gv = g_ref[0].astype(F32)
            for s in range(1, N_DEV):
                gv = gv + g_ref[s].astype(F32)
        else:
            gv = g_ref[...]
        go_ref[...] = gv
        d_ref[...], mo_ref[...], vo_ref[...] = _adam_math(w_ref[...], gv, m_ref[...], v_ref[...])

    tile = pl.BlockSpec((tr, C), lambda i: (i, 0))
    gspec = pl.BlockSpec((N_DEV, tr, C), lambda i: (0, i, 0)) if slots else tile
    sds = jax.ShapeDtypeStruct((R, C), F32)
    return pl.pallas_call(
        body, name=name, grid=(R // tr,), in_specs=[tile, tile, tile, gspec], out_specs=[tile] * 4,
        out_shape=[sds] * 4, compiler_params=_cparams(1),
    )(w, m, v, g)


def _sum_slots(pack):
    def body(p_ref, out_ref):
        acc = p_ref[0]
        for s in range(1, N_DEV):
            acc = acc + p_ref[s]
        out_ref[...] = acc

    return pl.pallas_call(body, name="sum_small", out_shape=jax.ShapeDtypeStruct(pack.shape[1:], F32))(pack)


def _me():
    return lax.axis_index("x"), lax.axis_index("y"), lax.axis_index("c")


def _peer(r):
    x, y, c = _me()
    px = 1 - x if r & 4 else x
    py = 1 - y if r & 2 else y
    pc = 1 - c if r & 1 else c
    return (px, py, pc), 4 * px + 2 * py + pc


def _allgather_small(x):
    R, C = x.shape

    def body(x_ref, out_ref, send_sems, recv_sems):
        mx, my, mc = _me()
        me = 4 * mx + 2 * my + mc
        mine = out_ref.at[pl.ds(pl.multiple_of(me * R, 8), R), :]
        copies = []
        for r in range(1, N_DEV):
            dev, _ = _peer(r)
            copies.append(pltpu.make_async_remote_copy(
                src_ref=x_ref, dst_ref=mine, send_sem=send_sems.at[r - 1], recv_sem=recv_sems.at[r - 1],
                device_id=dev, device_id_type=MESH))
        for cp in copies:
            cp.start()
        mine[...] = x_ref[...]
        for r in range(1, N_DEV):
            _, idx = _peer(r)
            theirs = out_ref.at[pl.ds(pl.multiple_of(idx * R, 8), R), :]
            pltpu.make_async_remote_copy(
                src_ref=x_ref, dst_ref=theirs, send_sem=send_sems.at[r - 1], recv_sem=recv_sems.at[r - 1],
                device_id=_peer(r)[0], device_id_type=MESH).wait_recv()
        for cp in copies:
            cp.wait_send()

    return pl.pallas_call(
        body, name="allgather_small_%dx%d" % (R, C),
        out_shape=jax.ShapeDtypeStruct((N_DEV * R, C), F32),
        in_specs=[pl.BlockSpec(memory_space=pltpu.VMEM)], out_specs=pl.BlockSpec(memory_space=pltpu.VMEM),
        scratch_shapes=[pltpu.SemaphoreType.DMA((N_DEV - 1,)), pltpu.SemaphoreType.DMA((N_DEV - 1,))],
    )(x)


def _xchg_copies(ins, outs, sems, gather):
    send_sems, recv_sems, local_sems = sems
    mx, my, mc = _me()
    me = 4 * mx + 2 * my + mc
    sibling = _peer(1)[0]

    def rdma(a, r, dev, src, slot):
        k = a * (N_DEV - 1) + r - 1
        return pltpu.make_async_remote_copy(
            src_ref=src, dst_ref=outs[a].at[slot], send_sem=send_sems.at[k], recv_sem=recv_sems.at[k],
            device_id=dev, device_id_type=MESH)

    own, sends, relays, recvs = [], [], [], []
    for a in range(len(ins)):
        own.append(pltpu.make_async_copy(ins[a] if gather else ins[a].at[me], outs[a].at[me], local_sems.at[a]))
        for r in range(1, N_DEV):
            dev, idx = _peer(r)
            if not gather:
                sends.append(rdma(a, r, dev, ins[a].at[idx], me))
                recvs.append(rdma(a, r, dev, ins[a].at[idx], idx))
            elif r == 1:
                sends.append(rdma(a, r, dev, ins[a], me))
                recvs.append(rdma(a, r, dev, ins[a], idx))
            elif r % 2 == 0:
                sends.append(rdma(a, r, dev, ins[a], me))
                relays.append((rdma(a, r, dev, ins[a], idx), rdma(a, r + 1, sibling, outs[a].at[idx], idx)))
            else:
                recvs.append(rdma(a, r, sibling, ins[a], idx))
    return own, sends, relays, recvs


def _xchg_start(ins, outs, sems, gather):
    own, sends, _, _ = _xchg_copies(ins, outs, sems, gather)
    for cp in own + sends:
        cp.start()


def _xchg_wait(ins, outs, sems, gather):
    own, sends, relays, recvs = _xchg_copies(ins, outs, sems, gather)
    for arrival, relay in relays:
        arrival.wait_recv()
        relay.start()
    for cp in recvs:
        cp.wait_recv()
    for cp in own:
        cp.wait()
    for cp in sends + [relay for _, relay in relays]:
        cp.wait_send()


def _xchg_specs(arrays, gather):
    n = len(arrays)
    out_shape = [jax.ShapeDtypeStruct(((N_DEV,) + a.shape) if gather else a.shape, a.dtype) for a in arrays]
    sems = [pltpu.SemaphoreType.DMA((n * (N_DEV - 1),)), pltpu.SemaphoreType.DMA((n * (N_DEV - 1),)),
            pltpu.SemaphoreType.DMA((n,))]
    return out_shape, sems


def _exchange(arrays, gather, name):
    n = len(arrays)

    def body(*refs):
        _xchg_start(refs[:n], refs[n:2 * n], refs[2 * n:], gather)
        _xchg_wait(refs[:n], refs[n:2 * n], refs[2 * n:], gather)

    out_shape, sems = _xchg_specs(arrays, gather)
    return pl.pallas_call(
        body, name=name, out_shape=out_shape,
        in_specs=[pl.BlockSpec(memory_space=pl.ANY)] * n, out_specs=[pl.BlockSpec(memory_space=pl.ANY)] * n,
        scratch_shapes=sems,
    )(*arrays)


def _gridded(body, carry, *, name, grid, in_specs, out_specs, out_shape, scratch_shapes=(), aliases=None):
    if carry is None:
        return pl.pallas_call(
            body, name=name, grid=grid, in_specs=list(in_specs), out_specs=list(out_specs),
            out_shape=list(out_shape), scratch_shapes=list(scratch_shapes), input_output_aliases=aliases or {},
            compiler_params=_cparams(len(grid)))
    arrays, gather = carry
    n, n_in, n_out, n_scr = len(arrays), len(in_specs), len(out_specs), len(scratch_shapes)
    c_shape, c_sems = _xchg_specs(arrays, gather)

    def wrapped(*refs):
        ins, cin = refs[:n_in], refs[n_in:n_in + n]
        o0 = n_in + n
        outs, cout = refs[o0:o0 + n_out], refs[o0 + n_out:o0 + n_out + n]
        s0 = o0 + n_out + n
        scr, sems = refs[s0:s0 + n_scr], refs[s0 + n_scr:]
        first = pl.program_id(0) == 0
        last = pl.program_id(0) == grid[0] - 1
        for ax in range(1, len(grid)):
            first = first & (pl.program_id(ax) == 0)
            last = last & (pl.program_id(ax) == grid[ax] - 1)

        @pl.when(first)
        def _():
            _xchg_start(cin, cout, sems, gather)

        body(*ins, *outs, *scr)

        @pl.when(last)
        def _():
            _xchg_wait(cin, cout, sems, gather)

    hbm = pl.BlockSpec(memory_space=pl.ANY)
    res = pl.pallas_call(
        wrapped, name=name, grid=grid, in_specs=list(in_specs) + [hbm] * n, out_specs=list(out_specs) + [hbm] * n,
        out_shape=list(out_shape) + c_shape, scratch_shapes=list(scratch_shapes) + c_sems,
        input_output_aliases=aliases or {}, compiler_params=_cparams(len(grid)),
    )
    return lambda *args: res(*args, *arrays)


def _local_step(x, target, mod, small, sh):
    w1_in, w1_out = _exchange([sh["ffn1_w_in"], sh["ffn1_w_out"]], True, "allgather_ffn1")
    w1_in, w1_out = _full_w_in(w1_in), w1_out.reshape(D_FF, D)
    (x1, a1, b1, f1, h1), (wm_in,) = _ffn_fwd(x, mod, 0, small["norm_ffn1"], w1_in, w1_out, 0.5, "ffn1_fwd",
                                              ([sh["mix_w_in"]], True))
    (p, h2), (wh_o, wc_o, wm_o, cw) = _mixin_fwd(
        x1, mod, 3, small["norm_mix"], wm_in,
        ([sh["hgrn_w_o"], sh["conv_w_o"], sh["mix_w_out"], sh["conv_w"]], True))
    wh_o, wc_o, wm_o = wh_o.reshape(D, D), wc_o.reshape(D, D), wm_o.reshape(D, D)
    cw = jnp.pad(cw.transpose(1, 0, 2).reshape(CONV_K, D), ((0, HALO - CONV_K), (0, 0)))
    (o, oa, a_all, s_all), (w2_in, w2_out) = _hgrn_fwd(p, small["hgrn_lb"], small["hgrn_g"],
                                                       ([sh["ffn2_w_in"], sh["ffn2_w_out"]], True))
    w2_in, w2_out = _full_w_in(w2_in), w2_out.reshape(D_FF, D)
    u1, u2 = _conv_fwd(p, cw, small["conv_b"], small["conv_ln_g"], small["conv_ln_b"])
    x2, ya, yb, mout = _mixout_fwd(x1, oa, u2, p, mod, 3, wh_o, wc_o, wm_o)
    (x3, a3, b3, f3, h3), _ = _ffn_fwd(x2, mod, 6, small["norm_ffn2"], w2_in, w2_out, 0.5, "ffn2_fwd", None)
    dx3, sm_head = _head(x3, target, small["norm_final"])

    (dx2, dw2_a, dw2_b, dw2_out, sm3), _ = _ffn_bwd(x2, h3, dx3, f3, a3, b3, mod, 6, small["norm_ffn2"], w2_in,
                                                    w2_out, 0.5, "ffn2_bwd", None)
    dp, doa, du2, dwh_o, dwc_o, dwm_o, sm_mo = _mixout_bwd(dx2, oa, u2, ya, yb, mout, p, mod, 3, wh_o, wc_o, wm_o)
    dp, dcw, sm_cv = _conv_bwd(p, u1, du2, cw, small["conv_ln_g"], small["conv_ln_b"], dp)
    rows = lambda t: t.reshape(N_DEV, -1, D).astype(MM)
    (dp, sm_hg), (r2_in, r2_out) = _hgrn_bwd(p, o, a_all, s_all, doa, small["hgrn_lb"], small["hgrn_g"], dp,
                                             ([_w_in_shards(dw2_a, dw2_b), rows(dw2_out)], False))
    (dx1, dwm_in, sm2), (rh_o, rc_o, rm_o, rcw) = _mixin_bwd(
        x1, h2, dx2, dp, mod, 3, small["norm_mix"], wm_in,
        ([rows(dwh_o), rows(dwc_o), rows(dwm_o), dcw[:CONV_K].reshape(CONV_K, N_DEV, -1).transpose(1, 0, 2)], False))
    (dx0, dw1_a, dw1_b, dw1_out, sm1), (rm_in,) = _ffn_bwd(x, h1, dx1, f1, a1, b1, mod, 0, small["norm_ffn1"],
                                                          w1_in, w1_out, 0.5, "ffn1_bwd",
                                                          ([dwm_in.astype(MM)], False))
    r1_in, r1_out = _exchange([_w_in_shards(dw1_a, dw1_b), rows(dw1_out)], False, "scatter_ffn1")

    dmod = jnp.concatenate([sm1[0:3], sm2[0:2], sm_mo[2:3], sm3[0:3]], axis=0)
    gsmall = dict(norm_ffn1=sm1[3:4], norm_mix=sm2[3:4], lb0=sm_hg[0:1], hgrn_g=sm_hg[1:2], conv_b=sm_cv[0:1],
                  conv_ln_g=sm_cv[1:2], conv_ln_b=sm_cv[2:3], norm_ffn2=sm3[3:4], norm_final=sm_head[0:1])
    recv = dict(ffn1_w_in=r1_in, ffn1_w_out=r1_out, mix_w_in=rm_in, hgrn_w_o=rh_o, conv_w=rcw, conv_w_o=rc_o,
                mix_w_out=rm_o, ffn2_w_in=r2_in, ffn2_w_out=r2_out)
    return sm_head[1, 0], dx0, dmod, gsmall, recv


def _full_w_in(g):
    return g.transpose(1, 0, 2).reshape(D, -1)


def _w_in_shards(dwa, dwb):
    half = N_DEV // 2
    return jnp.concatenate([t.reshape(D, half, -1).transpose(1, 0, 2).astype(MM) for t in (dwa, dwb)], axis=0)


SMALL_ORDER = ("norm_ffn1", "norm_mix", "lb0", "hgrn_g", "conv_b", "conv_ln_g", "conv_ln_b", "norm_ffn2",
               "norm_final")
PACK_ROWS = 24


def kernel(x, c, ada_w, ada_b, norm_ffn1, ffn1_w_in, ffn1_w_out, norm_mix, mix_w_in, hgrn_lb, hgrn_g, hgrn_w_o, conv_w, conv_b, conv_ln_g, conv_ln_b, conv_w_o, mix_w_out, norm_ffn2, ffn2_w_in, ffn2_w_out, norm_final, loss_target, m_ada_w, m_ada_b, m_norm_ffn1, m_ffn1_w_in, m_ffn1_w_out, m_norm_mix, m_mix_w_in, m_hgrn_lb, m_hgrn_g, m_hgrn_w_o, m_conv_w, m_conv_b, m_conv_ln_g, m_conv_ln_b, m_conv_w_o, m_mix_w_out, m_norm_ffn2, m_ffn2_w_in, m_ffn2_w_out, m_norm_final, v_ada_w, v_ada_b, v_norm_ffn1, v_ffn1_w_in, v_ffn1_w_out, v_norm_mix, v_mix_w_in, v_hgrn_lb, v_hgrn_g, v_hgrn_w_o, v_conv_w, v_conv_b, v_conv_ln_g, v_conv_ln_b, v_conv_w_o, v_mix_w_out, v_norm_ffn2, v_ffn2_w_in, v_ffn2_w_out, v_norm_final):
    mx, my, mc = _me()
    me = 4 * mx + 2 * my + mc
    ncol = ada_w.shape[2]

    cs = jnp.broadcast_to(c * jax.nn.sigmoid(c), (8, D))
    cs_all = _allgather_small(cs).reshape(N_DEV, 8, D)[:, 0, :]
    ada_b_cols = lax.dynamic_slice(ada_b, (0, me * ncol), (1, ncol))
    mod_cols = _ada_fwd(cs_all, ada_w[0], ada_b_cols)
    mod_all = _allgather_small(mod_cols).reshape(N_DEV, N_DEV, ncol)
    mod = lax.dynamic_index_in_dim(mod_all, me, axis=1, keepdims=False).reshape(9, D)

    sh = dict(ffn1_w_in=ffn1_w_in, ffn1_w_out=ffn1_w_out, mix_w_in=mix_w_in, hgrn_w_o=hgrn_w_o,
              conv_w_o=conv_w_o, mix_w_out=mix_w_out, ffn2_w_in=ffn2_w_in, ffn2_w_out=ffn2_w_out)
    sh = {n: w[0].astype(MM) for n, w in sh.items()}
    sh["conv_w"] = conv_w[0]
    small = dict(norm_ffn1=norm_ffn1, norm_mix=norm_mix, hgrn_lb=hgrn_lb, hgrn_g=hgrn_g, conv_b=conv_b,
                 conv_ln_g=conv_ln_g, conv_ln_b=conv_ln_b, norm_ffn2=norm_ffn2, norm_final=norm_final.reshape(1, D))

    loss_local, dx, dmod, gsmall, recv = _local_step(x[0], loss_target[0], mod, small, sh)
    loss = lax.psum(loss_local, ("x", "y", "c"))

    pack = jnp.concatenate([dmod] + [gsmall[n] for n in SMALL_ORDER]
                           + [jnp.zeros((PACK_ROWS - 9 - len(SMALL_ORDER), D), F32)], axis=0)
    pack_all = _allgather_small(pack).reshape(N_DEV, PACK_ROWS, D)
    tot = _sum_slots(pack_all)
    gs = {n: tot[9 + i:10 + i] for i, n in enumerate(SMALL_ORDER)}
    dmod_all = pack_all[:, 0:9, :].reshape(N_DEV, 9 * D)
    g_ada_b = tot[0:9].reshape(1, 9 * D)
    g_ada_w = _ada_wgrad(cs_all, lax.dynamic_slice(dmod_all, (0, me * ncol), (N_DEV, ncol)))
    z = hgrn_lb.astype(F32)
    p0 = jax.nn.sigmoid(z[0:1] - z[1:2])
    dz0 = p0 * (1.0 - p0) * gs["lb0"]
    g_hgrn_lb = jnp.concatenate([dz0, -dz0], axis=0)

    res = {}
    res["ada_w"] = _adamw(ada_w[0], m_ada_w[0], v_ada_w[0], g_ada_w, "adamw_ada_w")
    big = dict(ffn1_w_in=(ffn1_w_in, m_ffn1_w_in, v_ffn1_w_in), ffn1_w_out=(ffn1_w_out, m_ffn1_w_out, v_ffn1_w_out),
               mix_w_in=(mix_w_in, m_mix_w_in, v_mix_w_in), hgrn_w_o=(hgrn_w_o, m_hgrn_w_o, v_hgrn_w_o),
               conv_w=(conv_w, m_conv_w, v_conv_w), conv_w_o=(conv_w_o, m_conv_w_o, v_conv_w_o),
               mix_w_out=(mix_w_out, m_mix_w_out, v_mix_w_out), ffn2_w_in=(ffn2_w_in, m_ffn2_w_in, v_ffn2_w_in),
               ffn2_w_out=(ffn2_w_out, m_ffn2_w_out, v_ffn2_w_out))
    for n, (w, m, v) in big.items():
        res[n] = _adamw(w[0], m[0], v[0], recv[n], "adamw_" + n)
    sm_names = ("ada_b", "norm_ffn1", "norm_mix", "hgrn_lb", "hgrn_g", "conv_b", "conv_ln_g", "conv_ln_b",
                "norm_ffn2", "norm_final")
    sm_w = dict(ada_b=(ada_b, m_ada_b, v_ada_b), norm_ffn1=(norm_ffn1, m_norm_ffn1, v_norm_ffn1),
                norm_mix=(norm_mix, m_norm_mix, v_norm_mix), hgrn_lb=(hgrn_lb, m_hgrn_lb, v_hgrn_lb),
                hgrn_g=(hgrn_g, m_hgrn_g, v_hgrn_g), conv_b=(conv_b, m_conv_b, v_conv_b),
                conv_ln_g=(conv_ln_g, m_conv_ln_g, v_conv_ln_g), conv_ln_b=(conv_ln_b, m_conv_ln_b, v_conv_ln_b),
                norm_ffn2=(norm_ffn2, m_norm_ffn2, v_norm_ffn2), norm_final=(norm_final, m_norm_final, v_norm_final))
    sm_g = dict(gs, ada_b=g_ada_b, hgrn_lb=g_hgrn_lb)
    rows = {n: sm_w[n][0].size // D for n in sm_names}
    n_rows = sum(rows.values())
    pad = (-n_rows) % 8
    stack = lambda parts: jnp.concatenate([q.reshape(-1, D) for q in parts] + [jnp.ones((pad, D), F32)], axis=0)
    st = _adamw(stack([sm_w[n][0] for n in sm_names]), stack([sm_w[n][1] for n in sm_names]),
                stack([sm_w[n][2] for n in sm_names]), stack([sm_g[n] for n in sm_names]), "adamw_small")
    off = 0
    for n in sm_names:
        res[n] = tuple(t[off:off + rows[n]].reshape(sm_w[n][0].shape) for t in st)
        off += rows[n]

    order = ("ada_w", "ada_b", "norm_ffn1", "ffn1_w_in", "ffn1_w_out", "norm_mix", "mix_w_in", "hgrn_lb", "hgrn_g",
             "hgrn_w_o", "conv_w", "conv_b", "conv_ln_g", "conv_ln_b", "conv_w_o", "mix_w_out", "norm_ffn2",
             "ffn2_w_in", "ffn2_w_out", "norm_final")
    lead = lambda n, t: t[None] if n in big or n == "ada_w" else t
    outs = [loss, dx[None]]
    for j in range(4):
        outs += [lead(n, res[n][j]) for n in order]
    return tuple(outs)
```

```python
import functools

import jax
import jax.numpy as jnp
from jax import lax
from jax.experimental import pallas as pl
from jax.experimental.pallas import tpu as pltpu

F32 = jnp.float32
MM = jnp.bfloat16
ACT = jnp.bfloat16

D = 1024
D_FF = 2816
HEADS = 8
HD = 128
CHUNK = 64
SUB = 16
NSUB = CHUNK // SUB
CONV_K = 31
HALO = 32
EPS = 1e-6
N_DEV = 8
NEG = -1e30
Q_SCALE = HD ** -0.5

ADAM_LR = 0.001
ADAM_B1 = 0.9
ADAM_B2 = 0.999
ADAM_EPS = 1e-08
ADAM_WD = 0.01
ADAM_STEP = 10

VMEM_LIMIT = 60 * 1024 * 1024
MESH = pl.DeviceIdType.MESH


def _cparams(n_axes):
    return pltpu.CompilerParams(dimension_semantics=("arbitrary",) * n_axes, vmem_limit_bytes=VMEM_LIMIT)


def _mm(a, b):
    return lax.dot_general(a.astype(MM), b.astype(MM), (((1,), (0,)), ((), ())), preferred_element_type=F32)


def _mm_nt(a, b):
    return lax.dot_general(a.astype(MM), b.astype(MM), (((1,), (1,)), ((), ())), preferred_element_type=F32)


def _mm_tn(a, b):
    return lax.dot_general(a.astype(MM), b.astype(MM), (((0,), (0,)), ((), ())), preferred_element_type=F32)


def _sig(x):
    return 1.0 / (1.0 + jnp.exp(-x))


def _colsum(x):
    return jnp.sum(x, axis=0, keepdims=True)


def _rowmean(x):
    return jnp.mean(x, axis=-1, keepdims=True)


def _modnorm_fwd(xv, g, sh, sc):
    r = lax.rsqrt(_rowmean(xv * xv) + EPS)
    xh = xv * r
    n = xh * g
    return n * (1.0 + sc) + sh, xh, n, r


def _modnorm_bwd(dh, xh, n, r, g, sc):
    dsc = _colsum(dh * n)
    dsh = _colsum(dh)
    dn = dh * (1.0 + sc)
    dg = _colsum(dn * xh)
    dxh = dn * g
    dx = r * (dxh - xh * _rowmean(dxh * xh))
    return dx, dsh, dsc, dg


def _ffn_fwd(x, mod, mo, gnorm, w_in, w_out, res, name, carry):
    T = x.shape[0]
    tm = min(512, T)
    tn = D_FF // 2
    nj = D_FF // tn

    def body(x_ref, mod_ref, g_ref, wa_ref, wb_ref, wo_ref, xo_ref, a_ref, b_ref, f_ref, h_ref, acc_scr):
        j = pl.program_id(1)

        @pl.when(j == 0)
        def _():
            h, _, _, _ = _modnorm_fwd(x_ref[...], g_ref[...], mod_ref[mo:mo + 1, :], mod_ref[mo + 1:mo + 2, :])
            h_ref[...] = h.astype(ACT)
            acc_scr[...] = jnp.zeros_like(acc_scr)

        h = h_ref[...]
        a = _mm(h, wa_ref[...])
        b = _mm(h, wb_ref[...])
        a_ref[...] = a.astype(ACT)
        b_ref[...] = b.astype(ACT)
        s = a * _sig(a) * b
        acc_scr[...] += _mm(s, wo_ref[...])

        @pl.when(j == nj - 1)
        def _():
            f = acc_scr[...]
            f_ref[...] = f
            xo_ref[...] = x_ref[...] + res * mod_ref[mo + 2:mo + 3, :] * f

    out = _gridded(
        body, carry, name=name, grid=(T // tm, nj),
        in_specs=[
            pl.BlockSpec((tm, D), lambda i, j: (i, 0)),
            pl.BlockSpec((9, D), lambda i, j: (0, 0)),
            pl.BlockSpec((1, D), lambda i, j: (0, 0)),
            pl.BlockSpec((D, tn), lambda i, j: (0, j)),
            pl.BlockSpec((D, tn), lambda i, j: (0, j + nj)),
            pl.BlockSpec((tn, D), lambda i, j: (j, 0)),
        ],
        out_specs=[
            pl.BlockSpec((tm, D), lambda i, j: (i, 0)),
            pl.BlockSpec((tm, tn), lambda i, j: (i, j)),
            pl.BlockSpec((tm, tn), lambda i, j: (i, j)),
            pl.BlockSpec((tm, D), lambda i, j: (i, 0)),
            pl.BlockSpec((tm, D), lambda i, j: (i, 0)),
        ],
        out_shape=[
            jax.ShapeDtypeStruct((T, D), F32),
            jax.ShapeDtypeStruct((T, D_FF), ACT),
            jax.ShapeDtypeStruct((T, D_FF), ACT),
            jax.ShapeDtypeStruct((T, D), F32),
            jax.ShapeDtypeStruct((T, D), ACT),
        ],
        scratch_shapes=[pltpu.VMEM((tm, D), F32)],
    )(x, mod, gnorm, w_in, w_in, w_out)
    return out[:5], out[5:]


def _ffn_bwd(x, h, dxo, f, a, b, mod, mo, gnorm, w_in, w_out, res, name, carry):
    T = x.shape[0]
    tm = min(512, T)
    ni = T // tm
    tn = 256
    nj = D_FF // tn

    def body(x_ref, h_ref, dxo_ref, f_ref, a_ref, b_ref, mod_ref, g_ref, wa_ref, wb_ref, wo_ref,
             dx_ref, dwa_ref, dwb_ref, dwo_ref, sm_ref, dh_scr, acc_a, acc_b, acc_o):
        j = pl.program_id(0)
        i = pl.program_id(1)

        @pl.when(i == 0)
        def _():
            acc_a[...] = jnp.zeros_like(acc_a)
            acc_b[...] = jnp.zeros_like(acc_b)
            acc_o[...] = jnp.zeros_like(acc_o)

        @pl.when(j == 0)
        def _():
            dh_scr[i] = jnp.zeros((tm, D), F32)

        @pl.when((j == 0) & (i == 0))
        def _():
            sm_ref[...] = jnp.zeros_like(sm_ref)

        gate = mod_ref[mo + 2:mo + 3, :]
        hb = h_ref[...]
        dxo_v = dxo_ref[...]
        df = (res * gate * dxo_v).astype(MM)
        av = a_ref[...].astype(F32)
        bv = b_ref[...].astype(F32)
        sg = _sig(av)
        sa = av * sg
        s = (sa * bv).astype(MM)
        ds = _mm_nt(df, wo_ref[...])
        da = (ds * bv * sg * (1.0 + av * (1.0 - sg))).astype(MM)
        db = (ds * sa).astype(MM)
        acc_o[...] += _mm_tn(s, df)
        acc_a[...] += _mm_tn(hb, da)
        acc_b[...] += _mm_tn(hb, db)
        dh_scr[i] += _mm_nt(da, wa_ref[...]) + _mm_nt(db, wb_ref[...])

        @pl.when(i == ni - 1)
        def _():
            dwa_ref[...] = acc_a[...].astype(MM)
            dwb_ref[...] = acc_b[...].astype(MM)
            dwo_ref[...] = acc_o[...].astype(MM)

        @pl.when(j == nj - 1)
        def _():
            sc = mod_ref[mo + 1:mo + 2, :]
            _, xh, n, r = _modnorm_fwd(x_ref[...], g_ref[...], mod_ref[mo:mo + 1, :], sc)
            dxn, dsh, dsc, dg = _modnorm_bwd(dh_scr[i], xh, n, r, g_ref[...], sc)
            dx_ref[...] = dxo_v + dxn
            sm_ref[0:1, :] += dsh
            sm_ref[1:2, :] += dsc
            sm_ref[2:3, :] += _colsum(dxo_v * f_ref[...]) * res
            sm_ref[3:4, :] += dg

    out = _gridded(
        body, carry, name=name, grid=(nj, ni),
        in_specs=[
            pl.BlockSpec((tm, D), lambda j, i: (jnp.where(j == nj - 1, i, 0), 0)),
            pl.BlockSpec((tm, D), lambda j, i: (i, 0)),
            pl.BlockSpec((tm, D), lambda j, i: (i, 0)),
            pl.BlockSpec((tm, D), lambda j, i: (jnp.where(j == nj - 1, i, 0), 0)),
            pl.BlockSpec((tm, tn), lambda j, i: (i, j)),
            pl.BlockSpec((tm, tn), lambda j, i: (i, j)),
            pl.BlockSpec((9, D), lambda j, i: (0, 0)),
            pl.BlockSpec((1, D), lambda j, i: (0, 0)),
            pl.BlockSpec((D, tn), lambda j, i: (0, j)),
            pl.BlockSpec((D, tn), lambda j, i: (0, j + nj)),
            pl.BlockSpec((tn, D), lambda j, i: (j, 0)),
        ],
        out_specs=[
            pl.BlockSpec((tm, D), lambda j, i: (jnp.where(j == nj - 1, i, 0), 0)),
            pl.BlockSpec((D, tn), lambda j, i: (0, j)),
            pl.BlockSpec((D, tn), lambda j, i: (0, j)),
            pl.BlockSpec((tn, D), lambda j, i: (j, 0)),
            pl.BlockSpec((8, D), lambda j, i: (0, 0)),
        ],
        out_shape=[
            jax.ShapeDtypeStruct((T, D), F32),
            jax.ShapeDtypeStruct((D, D_FF), MM),
            jax.ShapeDtypeStruct((D, D_FF), MM),
            jax.ShapeDtypeStruct((D_FF, D), MM),
            jax.ShapeDtypeStruct((8, D), F32),
        ],
        scratch_shapes=[pltpu.VMEM((ni, tm, D), F32), pltpu.VMEM((D, tn), F32), pltpu.VMEM((D, tn), F32),
                        pltpu.VMEM((tn, D), F32)],
    )(x, h, dxo, f, a, b, mod, gnorm, w_in, w_in, w_out)
    return out[:5], out[5:]


def _head(x, target, gfin):
    T = x.shape[0]
    tm = min(512, T)
    ni = T // tm

    def body(x_ref, t_ref, g_ref, dx_ref, sm_ref):
        i = pl.program_id(0)

        @pl.when(i == 0)
        def _():
            sm_ref[...] = jnp.zeros_like(sm_ref)

        xv = x_ref[...]
        g = g_ref[...]
        r = lax.rsqrt(_rowmean(xv * xv) + EPS)
        xh = xv * r
        e = xh * g - t_ref[...]
        sm_ref[1:2, :] += _colsum(e * e) * (0.5 / D)
        dy = e * (1.0 / D)
        sm_ref[0:1, :] += _colsum(dy * xh)
        dxh = dy * g
        dx_ref[...] = r * (dxh - xh * _rowmean(dxh * xh))

        @pl.when(i == ni - 1)
        def _():
            sm_ref[1:2, :] = jnp.broadcast_to(jnp.sum(sm_ref[1:2, :], axis=-1, keepdims=True), (1, D))

    return pl.pallas_call(
        body, name="head_loss", grid=(ni,),
        in_specs=[pl.BlockSpec((tm, D), lambda i: (i, 0)), pl.BlockSpec((tm, D), lambda i: (i, 0)),
                  pl.BlockSpec((1, D), lambda i: (0, 0))],
        out_specs=[pl.BlockSpec((tm, D), lambda i: (i, 0)), pl.BlockSpec((8, D), lambda i: (0, 0))],
        out_shape=[jax.ShapeDtypeStruct((T, D), F32), jax.ShapeDtypeStruct((8, D), F32)],
        compiler_params=_cparams(1),
    )(x, target, gfin)


def _mixin_fwd(x, mod, mo, gnorm, w, carry):
    T = x.shape[0]
    tm = min(1024, T)

    def body(x_ref, mod_ref, g_ref, w_ref, p_ref, h_ref):
        @pl.when(pl.program_id(1) == 0)
        def _():
            h, _, _, _ = _modnorm_fwd(x_ref[...], g_ref[...], mod_ref[mo:mo + 1, :], mod_ref[mo + 1:mo + 2, :])
            h_ref[...] = h.astype(ACT)

        p_ref[0] = _mm(h_ref[...], w_ref[0])

    out = _gridded(
        body, carry, name="mixin_fwd", grid=(T // tm, 8),
        in_specs=[pl.BlockSpec((tm, D), lambda i, k: (i, 0)), pl.BlockSpec((9, D), lambda i, k: (0, 0)),
                  pl.BlockSpec((1, D), lambda i, k: (0, 0)), pl.BlockSpec((1, D, D), lambda i, k: (k, 0, 0))],
        out_specs=[pl.BlockSpec((1, tm, D), lambda i, k: (k, i, 0)), pl.BlockSpec((tm, D), lambda i, k: (i, 0))],
        out_shape=[jax.ShapeDtypeStruct((8, T, D), F32), jax.ShapeDtypeStruct((T, D), ACT)],
    )(x, mod, gnorm, w)
    return out[:2], out[2:]


def _mixin_bwd(x, h, dxo, dp, mod, mo, gnorm, w, carry):
    T = x.shape[0]
    tm = min(512, T)
    ni = T // tm

    def body(x_ref, h_ref, dxo_ref, dp_ref, mod_ref, g_ref, w_ref, dx_ref, dw_ref, sm_ref, dh_scr, acc):
        k = pl.program_id(0)
        i = pl.program_id(1)

        @pl.when(i == 0)
        def _():
            acc[...] = jnp.zeros_like(acc)

        @pl.when(k == 0)
        def _():
            dh_scr[i] = jnp.zeros((tm, D), F32)

        @pl.when((k == 0) & (i == 0))
        def _():
            sm_ref[...] = jnp.zeros_like(sm_ref)

        dpk = dp_ref[0].astype(MM)
        acc[...] += _mm_tn(h_ref[...], dpk)
        dh_scr[i] += _mm_nt(dpk, w_ref[0])

        @pl.when(i == ni - 1)
        def _():
            dw_ref[0] = acc[...].astype(MM)

        @pl.when(k == 7)
        def _():
            sc = mod_ref[mo + 1:mo + 2, :]
            _, xh, n, r = _modnorm_fwd(x_ref[...], g_ref[...], mod_ref[mo:mo + 1, :], sc)
            dxn, dsh, dsc, dg = _modnorm_bwd(dh_scr[i], xh, n, r, g_ref[...], sc)
            dx_ref[...] = dxo_ref[...] + dxn
            sm_ref[0:1, :] += dsh
            sm_ref[1:2, :] += dsc
            sm_ref[3:4, :] += dg

    out = _gridded(
        body, carry, name="mixin_bwd", grid=(8, ni),
        in_specs=[pl.BlockSpec((tm, D), lambda k, i: (jnp.where(k == 7, i, 0), 0)),
                  pl.BlockSpec((tm, D), lambda k, i: (i, 0)),
                  pl.BlockSpec((tm, D), lambda k, i: (jnp.where(k == 7, i, 0), 0)),
                  pl.BlockSpec((1, tm, D), lambda k, i: (k, i, 0)), pl.BlockSpec((9, D), lambda k, i: (0, 0)),
                  pl.BlockSpec((1, D), lambda k, i: (0, 0)), pl.BlockSpec((1, D, D), lambda k, i: (k, 0, 0))],
        out_specs=[pl.BlockSpec((tm, D), lambda k, i: (jnp.where(k == 7, i, 0), 0)),
                   pl.BlockSpec((1, D, D), lambda k, i: (k, 0, 0)),
                   pl.BlockSpec((8, D), lambda k, i: (0, 0))],
        out_shape=[jax.ShapeDtypeStruct((T, D), F32), jax.ShapeDtypeStruct((8, D, D), MM),
                   jax.ShapeDtypeStruct((8, D), F32)],
        scratch_shapes=[pltpu.VMEM((ni, tm, D), F32), pltpu.VMEM((D, D), F32)],
    )(x, h, dxo, dp, mod, gnorm, w)
    return out[:3], out[3:]


def _hgrn_consts():
    rows = jnp.arange(SUB * HD) // HD
    e = (rows[:, None] == jnp.arange(HD)[None, :]).astype(MM)
    return e, e.T


def _rows_bcast(ref, cb, first, n):
    parts = [jnp.broadcast_to(ref[pl.ds(c * CHUNK + first, 1), :], (n, HD)) for c in range(cb // CHUNK)]
    return jnp.concatenate(parts, axis=0)


def _hgrn_pre(qr, fr, lb_ref, b_scr, cb):
    z = lb_ref[...]
    lb = _sig(z[0:1, :] - z[1:2, :])
    sq = _sig(qr)
    q = qr * sq * Q_SCALE
    sf = _sig(fr)
    fg = lb + (1.0 - lb) * sf
    lf = jnp.log(fg)
    k = 1.0 - fg
    tl = lax.broadcasted_iota(jnp.int32, (cb, HD), 0) % CHUNK
    bc = lf
    sh = 1
    while sh < CHUNK:
        bc = bc + jnp.where(tl >= sh, pltpu.roll(bc, sh, 0), 0.0)
        sh *= 2
    b_scr[...] = bc
    bl = _rows_bcast(b_scr, cb, CHUNK - 1, CHUNK)
    br = [None] + [_rows_bcast(b_scr, cb, SUB * i - 1, CHUNK) for i in range(1, NSUB)]
    sb = tl // SUB
    bref = jnp.where(sb == 0, bc, jnp.where(sb == 1, br[1], jnp.where(sb == 2, br[2], br[3])))
    eb = jnp.exp(bc)
    ekd = jnp.exp(bl - bc)
    eqo = jnp.exp(bc - bref)
    eko = [None] + [jnp.exp(jnp.where(tl < SUB * i, br[i] - bc, NEG)) for i in range(1, NSUB)]
    return dict(lb=lb, sq=sq, q=q, sf=sf, fg=fg, k=k, tl=tl, sb=sb, b=bc, bl=bl, eb=eb, ekd=ekd, eqo=eqo,
                eko=eko, qe=q * eb, kd=k * ekd, qo=q * eqo, ko=[None] + [k * eko[i] for i in range(1, NSUB)])


def _pad_rows(x):
    return jnp.concatenate([x, jnp.zeros_like(x)], axis=0)


def _by_subblock(sbc, parts):
    out = jnp.zeros_like(parts[1])
    for i in range(1, NSUB):
        out = jnp.where(sbc == i, parts[i], out)
    return out


def _hgrn_fwd(p, hgrn_lb, hgrn_g, carry):
    T = p.shape[1]
    cb = min(512, T)
    nch = cb // CHUNK
    ncb = T // cb
    e_mat, _ = _hgrn_consts()

    def body(p_ref, lb_ref, g_ref, e_ref, o_ref, oa_ref, a_ref, s_ref, st_scr, q_scr, k_scr, b_scr, z_scr):
        @pl.when(pl.program_id(1) == 0)
        def _():
            st_scr[...] = jnp.zeros_like(st_scr)

        v = p_ref[2]
        og = p_ref[3]
        pre = _hgrn_pre(p_ref[0], p_ref[1], lb_ref, b_scr, cb)
        q_scr[...] = pre["q"]
        k_scr[...] = pre["k"]
        ti = lax.broadcasted_iota(jnp.int32, (SUB, HD), 0)

        def zbody(c, carry):
            for i in range(NSUB):
                r0 = pl.multiple_of(c * CHUNK + SUB * i, SUB)
                qi = q_scr[pl.ds(r0, SUB), :]
                bi = b_scr[pl.ds(r0, SUB), :]
                for s in range(SUB):
                    krow = k_scr[pl.ds(r0 + s, 1), :]
                    brow = b_scr[pl.ds(r0 + s, 1), :]
                    if s < 8:
                        zz = qi * krow * jnp.exp(jnp.where(ti >= s, bi - brow, NEG))
                    else:
                        lo = qi[8:] * krow * jnp.exp(jnp.where(ti[8:] >= s, bi[8:] - brow, NEG))
                        zz = jnp.concatenate([jnp.zeros((8, HD), F32), lo], axis=0)
                    z_scr[i, pl.ds(pl.multiple_of(c * SUB, SUB), SUB), s * HD:(s + 1) * HD] = zz.astype(MM)
            return carry

        lax.fori_loop(0, nch, zbody, 0)
        adiag = [_mm(z_scr[i], e_ref[...]) for i in range(NSUB)]
        sbc = lax.broadcasted_iota(jnp.int32, (CHUNK, HD), 0) // SUB
        chunks = [slice(c * CHUNK, (c + 1) * CHUNK) for c in range(nch)]
        offs = [[_mm_nt(pre["qo"][rs], _pad_rows(pre["ko"][i][rs])) for i in range(1, NSUB)] for rs in chunks]
        kv = [_mm_tn(v[rs], pre["kd"][rs]) for rs in chunks]
        a_parts = []
        for c in range(nch):
            dparts = []
            for i in range(NSUB):
                blk = adiag[i][c * SUB:(c + 1) * SUB]
                dparts.append(blk if i == 0 else pltpu.roll(blk, SUB * i, 1))
            a_parts.append(_by_subblock(sbc, [None] + offs[c]) + jnp.concatenate(dparts, axis=0))
        a_ref[0] = jnp.concatenate(a_parts, axis=0)
        o_intra = [_mm(a_parts[c], _pad_rows(v[rs])) for c, rs in enumerate(chunks)]
        states = []
        st = st_scr[...]
        for c in range(nch):
            states.append(st)
            st = st * jnp.exp(b_scr[pl.ds(c * CHUNK + CHUNK - 1, 1), :]) + kv[c]
        st_scr[...] = st
        for c in range(nch):
            s_ref[0, c] = states[c]
        o = jnp.concatenate([o_intra[c] + _mm_nt(pre["qe"][rs], states[c]) for c, rs in enumerate(chunks)], axis=0)
        o_ref[...] = o
        on = o * lax.rsqrt(_rowmean(o * o) + EPS) * g_ref[...]
        oa_ref[...] = (on * og * _sig(og)).astype(ACT)

    out = _gridded(
        body, carry, name="hgrn_fwd", grid=(HEADS, ncb),
        in_specs=[pl.BlockSpec((4, cb, HD), lambda h, c: (0, c, h)),
                  pl.BlockSpec((2, HD), lambda h, c: (0, h)),
                  pl.BlockSpec((1, HD), lambda h, c: (0, h)),
                  pl.BlockSpec((SUB * HD, HD), lambda h, c: (0, 0))],
        out_specs=[pl.BlockSpec((cb, HD), lambda h, c: (c, h)),
                   pl.BlockSpec((cb, HD), lambda h, c: (c, h)),
                   pl.BlockSpec((1, cb, HD), lambda h, c: (h, c, 0)),
                   pl.BlockSpec((1, nch, HD, HD), lambda h, c: (h, c, 0, 0))],
        out_shape=[jax.ShapeDtypeStruct((T, D), F32), jax.ShapeDtypeStruct((T, D), ACT),
                   jax.ShapeDtypeStruct((HEADS, T, HD), F32),
                   jax.ShapeDtypeStruct((HEADS, T // CHUNK, HD, HD), F32)],
        scratch_shapes=[pltpu.VMEM((HD, HD), F32), pltpu.VMEM((cb, HD), F32), pltpu.VMEM((cb, HD), F32),
                        pltpu.VMEM((cb, HD), F32), pltpu.VMEM((NSUB, nch * SUB, SUB * HD), MM)],
    )(p, hgrn_lb, hgrn_g, e_mat)
    return out[:4], out[4:]


def _hgrn_bwd(p, o, a_all, s_all, doa, hgrn_lb, hgrn_g, dp, carry):
    T = p.shape[1]
    cb = min(512, T)
    nch = cb // CHUNK
    ncb = T // cb
    _, et_mat = _hgrn_consts()

    def body(p_ref, o_ref, a_ref, s_ref, doa_ref, lb_ref, g_ref, et_ref, dp_in, dp_ref, sm_ref,
             dst_scr, q_scr, k_scr, b_scr, x_scr, dqd_scr, dkd_scr):
        del dp_in

        @pl.when(pl.program_id(1) == 0)
        def _():
            dst_scr[...] = jnp.zeros_like(dst_scr)
            sm_ref[...] = jnp.zeros_like(sm_ref)

        qr = p_ref[0]
        v = p_ref[2]
        og = p_ref[3]
        pre = _hgrn_pre(qr, p_ref[1], lb_ref, b_scr, cb)
        q, k = pre["q"], pre["k"]
        q_scr[...] = q
        k_scr[...] = k
        g = g_ref[...]
        ov = o_ref[...]
        r = lax.rsqrt(_rowmean(ov * ov) + EPS)
        oh = ov * r
        sgo = _sig(og)
        doa_v = doa_ref[...]
        don = doa_v * og * sgo
        dog = doa_v * oh * g * sgo * (1.0 + og * (1.0 - sgo))
        sm_ref[1:2, :] += _colsum(don * oh)
        doh = don * g
        do = r * (doh - oh * _rowmean(doh * oh))

        sbc = lax.broadcasted_iota(jnp.int32, (CHUNK, HD), 0) // SUB
        row_i = lax.broadcasted_iota(jnp.int32, (CHUNK, HD), 0)
        lane_i = lax.broadcasted_iota(jnp.int32, (CHUNK, HD), 1)
        causal = lane_i <= row_i
        chunks = [slice(c * CHUNK, (c + 1) * CHUNK) for c in range(nch)]
        da_parts = [jnp.where(causal, _mm_nt(do[rs], _pad_rows(v[rs])), 0.0) for rs in chunks]
        dv_parts = [_mm_tn(a_ref[0, rs, :], do[rs])[:CHUNK] for rs in chunks]
        dqoff_mm = [[_mm(da_parts[c], _pad_rows(pre["ko"][i][rs])) for i in range(1, NSUB)]
                    for c, rs in enumerate(chunks)]
        dkoff_mm = [[_mm_tn(jnp.where(sbc == i, da_parts[c], 0.0), pre["qo"][rs])[:CHUNK] for i in range(1, NSUB)]
                    for c, rs in enumerate(chunks)]
        dqoff_parts = [_by_subblock(sbc, [None] + dqoff_mm[c]) for c in range(nch)]
        dkoff_parts = []
        for c, rs in enumerate(chunks):
            dko = pre["eko"][1][rs] * dkoff_mm[c][0]
            for i in range(2, NSUB):
                dko = dko + pre["eko"][i][rs] * dkoff_mm[c][i - 1]
            dkoff_parts.append(dko)
        for i in range(NSUB):
            rows = []
            for c in range(nch):
                blk = da_parts[c][SUB * i:SUB * (i + 1)]
                rows.append(blk if i == 0 else pltpu.roll(blk, HD - SUB * i, 1))
            x_scr[i] = _mm(jnp.concatenate(rows, axis=0), et_ref[...])
        ti = lax.broadcasted_iota(jnp.int32, (SUB, HD), 0)

        def dbody(c, carry):
            for i in range(NSUB):
                r0 = pl.multiple_of(c * CHUNK + SUB * i, SUB)
                qi = q_scr[pl.ds(r0, SUB), :]
                bi = b_scr[pl.ds(r0, SUB), :]
                dq_hi = jnp.zeros((8, HD), F32)
                dq_lo = jnp.zeros((8, HD), F32)
                dk_hi = jnp.zeros((8, HD), F32)
                dk_lo = jnp.zeros((8, HD), F32)
                c0 = pl.multiple_of(c * SUB, SUB)
                t8 = ti[:8]
                for s in range(SUB):
                    krow = k_scr[pl.ds(r0 + s, 1), :]
                    brow = b_scr[pl.ds(r0 + s, 1), :]
                    w_lo = (x_scr[i, pl.ds(c0 + 8, 8), s * HD:(s + 1) * HD]
                            * jnp.exp(jnp.where(t8 + 8 >= s, bi[8:] - brow, NEG)))
                    dq_lo = dq_lo + w_lo * krow
                    col = _colsum(w_lo * qi[8:])
                    if s < 8:
                        w_hi = (x_scr[i, pl.ds(c0, 8), s * HD:(s + 1) * HD]
                                * jnp.exp(jnp.where(t8 >= s, bi[:8] - brow, NEG)))
                        dq_hi = dq_hi + w_hi * krow
                        dk_hi = jnp.where(t8 == s, col + _colsum(w_hi * qi[:8]), dk_hi)
                    else:
                        dk_lo = jnp.where(t8 + 8 == s, col, dk_lo)
                dqd_scr[pl.ds(r0, SUB), :] = jnp.concatenate([dq_hi, dq_lo], axis=0)
                dkd_scr[pl.ds(r0, SUB), :] = jnp.concatenate([dk_hi, dk_lo], axis=0)
            return carry

        lax.fori_loop(0, nch, dbody, 0)
        qdo = [_mm_tn(do[rs], pre["qe"][rs]) for rs in chunks]
        dsts = [None] * nch
        dst = dst_scr[...]
        for c in reversed(range(nch)):
            dsts[c] = dst
            dst = dst * jnp.exp(b_scr[pl.ds(c * CHUNK + CHUNK - 1, 1), :]) + qdo[c]
        dst_scr[...] = dst
        sts = [s_ref[0, c] for c in range(nch)]
        dqe_parts = [_mm(do[rs], sts[c]) for c, rs in enumerate(chunks)]
        dkdec_parts = [_mm(v[rs], dsts[c]) for c, rs in enumerate(chunks)]
        dvi_parts = [_mm_nt(pre["kd"][rs], dsts[c]) for c, rs in enumerate(chunks)]
        debl_parts = [_colsum(dsts[c] * sts[c]) for c in range(nch)]
        dqe = jnp.concatenate(dqe_parts, axis=0)
        dkdec = jnp.concatenate(dkdec_parts, axis=0)
        dq_tot = jnp.concatenate(dqoff_parts, axis=0) * pre["eqo"] + dqd_scr[...] + dqe * pre["eb"]
        dk_inter = dkdec * pre["ekd"]
        dk_tot = jnp.concatenate(dkoff_parts, axis=0) + dkd_scr[...] + dk_inter
        db = q * dq_tot - k * dk_tot
        kdk = k * dk_inter
        dbl = jnp.concatenate(
            [jnp.broadcast_to(jnp.exp(b_scr[pl.ds(c * CHUNK + CHUNK - 1, 1), :]) * debl_parts[c]
                              + _colsum(kdk[c * CHUNK:(c + 1) * CHUNK]), (CHUNK, HD)) for c in range(nch)], axis=0)
        tl = pre["tl"]
        rc = db
        sh = 1
        while sh < CHUNK:
            rc = rc + jnp.where(tl + sh < CHUNK, pltpu.roll(rc, cb - sh, 0), 0.0)
            sh *= 2
        dlf = rc + dbl
        dfg = dlf / pre["fg"] - dk_tot
        sf = pre["sf"]
        lb = pre["lb"]
        sm_ref[0:1, :] += _colsum(dfg * (1.0 - sf))
        sq = pre["sq"]
        dp_ref[0] = dq_tot * Q_SCALE * sq * (1.0 + qr * (1.0 - sq))
        dp_ref[1] = dfg * (1.0 - lb) * sf * (1.0 - sf)
        dp_ref[2] = jnp.concatenate(dv_parts, axis=0) + jnp.concatenate(dvi_parts, axis=0)
        dp_ref[3] = dog

    rev = lambda c: ncb - 1 - c
    out = _gridded(
        body, carry, name="hgrn_bwd", grid=(HEADS, ncb),
        in_specs=[pl.BlockSpec((4, cb, HD), lambda h, c: (0, rev(c), h)),
                  pl.BlockSpec((cb, HD), lambda h, c: (rev(c), h)),
                  pl.BlockSpec((1, cb, HD), lambda h, c: (h, rev(c), 0)),
                  pl.BlockSpec((1, nch, HD, HD), lambda h, c: (h, rev(c), 0, 0)),
                  pl.BlockSpec((cb, HD), lambda h, c: (rev(c), h)),
                  pl.BlockSpec((2, HD), lambda h, c: (0, h)),
                  pl.BlockSpec((1, HD), lambda h, c: (0, h)),
                  pl.BlockSpec((HD, SUB * HD), lambda h, c: (0, 0)),
                  pl.BlockSpec(memory_space=pl.ANY)],
        out_specs=[pl.BlockSpec((4, cb, HD), lambda h, c: (0, rev(c), h)),
                   pl.BlockSpec((8, HD), lambda h, c: (0, h))],
        out_shape=[jax.ShapeDtypeStruct(dp.shape, F32), jax.ShapeDtypeStruct((8, D), F32)],
        aliases={8: 0},
        scratch_shapes=[pltpu.VMEM((HD, HD), F32), pltpu.VMEM((cb, HD), F32), pltpu.VMEM((cb, HD), F32),
                        pltpu.VMEM((cb, HD), F32), pltpu.VMEM((NSUB, nch * SUB, SUB * HD), F32),
                        pltpu.VMEM((cb, HD), F32), pltpu.VMEM((cb, HD), F32)],
    )(p, o, a_all, s_all, doa, hgrn_lb, hgrn_g, et_mat, dp)
    return out[:2], out[2:]


def _ln_fwd(u1, g, b):
    mu = _rowmean(u1)
    xc = u1 - mu
    rs = lax.rsqrt(_rowmean(xc * xc) + EPS)
    xh = xc * rs
    return xh * g + b, xh, rs


CONV_RB = 64
LANES = 128


def _shift_rows(src, sh, ls, n):
    for r in range(1, 8):
        sh[r - 1, 0:n, :] = src[pl.ds(r, n), ls]


def _tap(src, sh, ls, off, r0, rows):
    r = off % 8
    if r == 0:
        return src[pl.ds(r0 + off, rows), ls]
    return sh[r - 1, pl.ds(r0 + off - r, rows), :]


def _conv_fwd(p, cw, cb_, lng, lnb):
    T = p.shape[1]
    tm = min(512, T)
    n = HALO + tm - 8

    def body(p_ref, cw_ref, cb_ref, g_ref, b_ref, u1_ref, u2_ref, buf, sh):
        @pl.when(pl.program_id(0) == 0)
        def _():
            buf[0:HALO, :] = jnp.zeros((HALO, D), F32)

        buf[HALO:HALO + tm, :] = p_ref[0] * _sig(p_ref[1])
        for lb in range(D // LANES):
            ls = slice(lb * LANES, (lb + 1) * LANES)
            _shift_rows(buf, sh, ls, n)
            taps = [cw_ref[j:j + 1, ls] for j in range(CONV_K)]
            bias = cb_ref[:, ls]

            def rows_body(rb, carry):
                r0 = pl.multiple_of(rb * CONV_RB, CONV_RB)
                acc = jnp.broadcast_to(bias, (CONV_RB, LANES))
                for j in range(CONV_K):
                    acc = acc + taps[j] * _tap(buf, sh, ls, HALO - (CONV_K - 1) + j, r0, CONV_RB)
                u1_ref[pl.ds(r0, CONV_RB), ls] = acc
                return carry

            lax.fori_loop(0, tm // CONV_RB, rows_body, 0)
        y, _, _ = _ln_fwd(u1_ref[...], g_ref[...], b_ref[...])
        u2_ref[...] = (y * _sig(y)).astype(ACT)
        buf[0:HALO, :] = buf[tm:tm + HALO, :]

    return pl.pallas_call(
        body, name="conv_fwd", grid=(T // tm,),
        in_specs=[pl.BlockSpec((2, tm, D), lambda i: (2, i, 0)), pl.BlockSpec((HALO, D), lambda i: (0, 0)),
                  pl.BlockSpec((1, D), lambda i: (0, 0)), pl.BlockSpec((1, D), lambda i: (0, 0)),
                  pl.BlockSpec((1, D), lambda i: (0, 0))],
        out_specs=[pl.BlockSpec((tm, D), lambda i: (i, 0)), pl.BlockSpec((tm, D), lambda i: (i, 0))],
        out_shape=[jax.ShapeDtypeStruct((T, D), F32), jax.ShapeDtypeStruct((T, D), ACT)],
        scratch_shapes=[pltpu.VMEM((HALO + tm, D), F32), pltpu.VMEM((7, n, LANES), F32)],
        compiler_params=_cparams(1),
    )(p, cw, cb_, lng, lnb)


def _conv_bwd(p, u1, du2, cw, lng, lnb, dp):
    T = p.shape[1]
    tm = min(512, T)
    ni = T // tm
    hb = tm // HALO

    n = HALO + tm - 8

    def body(p_ref, ph_ref, u1_ref, du2_ref, cw_ref, g_ref, b_ref, dp_in, dp_ref, dcw_ref, sm_ref, ubuf, dbuf,
             sh, dacc):
        del dp_in
        step = pl.program_id(0)

        @pl.when(step == 0)
        def _():
            dbuf[tm:tm + HALO, :] = jnp.zeros((HALO, D), F32)
            dcw_ref[...] = jnp.zeros_like(dcw_ref)
            sm_ref[...] = jnp.zeros_like(sm_ref)

        ua = p_ref[0]
        sgb = _sig(p_ref[1])
        halo = ph_ref[0] * _sig(ph_ref[1])
        ubuf[0:HALO, :] = jnp.where(step == ni - 1, 0.0, halo)
        ubuf[HALO:HALO + tm, :] = ua * sgb
        g = g_ref[...]
        y, xh, rs = _ln_fwd(u1_ref[...], g, b_ref[...])
        sy = _sig(y)
        dy = du2_ref[...] * sy * (1.0 + y * (1.0 - sy))
        sm_ref[1:2, :] += _colsum(dy * xh)
        sm_ref[2:3, :] += _colsum(dy)
        dxh = dy * g
        du1 = rs * (dxh - _rowmean(dxh) - xh * _rowmean(dxh * xh))
        sm_ref[0:1, :] += _colsum(du1)
        dbuf[0:tm, :] = du1
        for lb in range(D // LANES):
            ls = slice(lb * LANES, (lb + 1) * LANES)
            taps = [cw_ref[j:j + 1, ls] for j in range(CONV_K)]
            _shift_rows(dbuf, sh, ls, n)

            def du0_body(rb, carry):
                r0 = pl.multiple_of(rb * CONV_RB, CONV_RB)
                acc = jnp.zeros((CONV_RB, LANES), F32)
                for j in range(CONV_K):
                    acc = acc + taps[j] * _tap(dbuf, sh, ls, CONV_K - 1 - j, r0, CONV_RB)
                dp_ref[0, pl.ds(r0, CONV_RB), ls] = acc
                return carry

            lax.fori_loop(0, tm // CONV_RB, du0_body, 0)
            _shift_rows(ubuf, sh, ls, n)
            dacc[...] = jnp.zeros_like(dacc)

            def dcw_body(rb, carry):
                r0 = pl.multiple_of(rb * CONV_RB, CONV_RB)
                d = dbuf[pl.ds(r0, CONV_RB), ls]
                for j in range(CONV_K):
                    prod = d * _tap(ubuf, sh, ls, HALO - (CONV_K - 1) + j, r0, CONV_RB)
                    dacc[8 * j:8 * j + 8, :] += jnp.sum(prod.reshape(CONV_RB // 8, 8, LANES), axis=0)
                return carry

            lax.fori_loop(0, tm // CONV_RB, dcw_body, 0)
            for j in range(CONV_K):
                dcw_ref[j:j + 1, ls] += _colsum(dacc[8 * j:8 * j + 8, :])
        du0 = dp_ref[0]
        dp_ref[0] = du0 * sgb
        dp_ref[1] = du0 * ua * sgb * (1.0 - sgb)
        dbuf[tm:tm + HALO, :] = dbuf[0:HALO, :]

    rev = lambda i: ni - 1 - i
    return pl.pallas_call(
        body, name="conv_bwd", grid=(ni,),
        in_specs=[pl.BlockSpec((2, tm, D), lambda i: (2, rev(i), 0)),
                  pl.BlockSpec((2, HALO, D), lambda i: (2, jnp.maximum(rev(i) * hb - 1, 0), 0)),
                  pl.BlockSpec((tm, D), lambda i: (rev(i), 0)), pl.BlockSpec((tm, D), lambda i: (rev(i), 0)),
                  pl.BlockSpec((HALO, D), lambda i: (0, 0)), pl.BlockSpec((1, D), lambda i: (0, 0)),
                  pl.BlockSpec((1, D), lambda i: (0, 0)), pl.BlockSpec(memory_space=pl.ANY)],
        out_specs=[pl.BlockSpec((2, tm, D), lambda i: (2, rev(i), 0)),
                   pl.BlockSpec((HALO, D), lambda i: (0, 0)), pl.BlockSpec((8, D), lambda i: (0, 0))],
        out_shape=[jax.ShapeDtypeStruct(dp.shape, F32), jax.ShapeDtypeStruct((HALO, D), F32),
                   jax.ShapeDtypeStruct((8, D), F32)],
        input_output_aliases={7: 0},
        scratch_shapes=[pltpu.VMEM((HALO + tm, D), F32), pltpu.VMEM((tm + HALO, D), F32),
                        pltpu.VMEM((7, n, LANES), F32), pltpu.VMEM((8 * CONV_K, LANES), F32)],
        compiler_params=_cparams(1),
    )(p, p, u1, du2, cw, lng, lnb, dp)


def _mixout_fwd(x, oa, u2, p, mod, mo, w_a, w_b, w_o):
    T = x.shape[0]
    tm = min(512, T)

    def body(x_ref, oa_ref, u2_ref, p_ref, mod_ref, wa_ref, wb_ref, wo_ref, xo_ref, ya_ref, yb_ref, mo_ref):
        ya = _mm(oa_ref[...], wa_ref[...])
        yb = _mm(u2_ref[...], wb_ref[...])
        ya_ref[...] = ya.astype(ACT)
        yb_ref[...] = yb.astype(ACT)
        merged = _sig(p_ref[0]) * ya + _sig(p_ref[1]) * yb
        out = _mm(merged, wo_ref[...])
        mo_ref[...] = out
        xo_ref[...] = x_ref[...] + mod_ref[mo + 2:mo + 3, :] * out

    tile = pl.BlockSpec((tm, D), lambda i: (i, 0))
    wspec = pl.BlockSpec((D, D), lambda i: (0, 0))
    return pl.pallas_call(
        body, name="mixout_fwd", grid=(T // tm,),
        in_specs=[tile, tile, tile, pl.BlockSpec((2, tm, D), lambda i: (3, i, 0)),
                  pl.BlockSpec((9, D), lambda i: (0, 0)), wspec, wspec, wspec],
        out_specs=[tile, tile, tile, tile],
        out_shape=[jax.ShapeDtypeStruct((T, D), F32), jax.ShapeDtypeStruct((T, D), ACT),
                   jax.ShapeDtypeStruct((T, D), ACT), jax.ShapeDtypeStruct((T, D), F32)],
        compiler_params=_cparams(1),
    )(x, oa, u2, p, mod, w_a, w_b, w_o)


def _mixout_bwd(dxo, oa, u2, ya, yb, mout, p, mod, mo, w_a, w_b, w_o):
    T = dxo.shape[0]
    tm = min(256, T)

    def body(dxo_ref, oa_ref, u2_ref, ya_ref, yb_ref, mo_ref, p_ref, mod_ref, wa_ref, wb_ref, wo_ref,
             dp_ref, doa_ref, du2_ref, dwa_ref, dwb_ref, dwo_ref, sm_ref):
        @pl.when(pl.program_id(0) == 0)
        def _():
            dwa_ref[...] = jnp.zeros_like(dwa_ref)
            dwb_ref[...] = jnp.zeros_like(dwb_ref)
            dwo_ref[...] = jnp.zeros_like(dwo_ref)
            sm_ref[...] = jnp.zeros_like(sm_ref)

        dxo_v = dxo_ref[...]
        sm_ref[2:3, :] += _colsum(dxo_v * mo_ref[...])
        dmo = (mod_ref[mo + 2:mo + 3, :] * dxo_v).astype(MM)
        ya = ya_ref[...].astype(F32)
        yb = yb_ref[...].astype(F32)
        sga = _sig(p_ref[0])
        sgb = _sig(p_ref[1])
        merged = (sga * ya + sgb * yb).astype(MM)
        dwo_ref[...] += _mm_tn(merged, dmo)
        dmg = _mm_nt(dmo, wo_ref[...])
        dp_ref[0] = dmg * ya * sga * (1.0 - sga)
        dp_ref[1] = dmg * yb * sgb * (1.0 - sgb)
        dya = (dmg * sga).astype(MM)
        dyb = (dmg * sgb).astype(MM)
        dwa_ref[...] += _mm_tn(oa_ref[...], dya)
        dwb_ref[...] += _mm_tn(u2_ref[...], dyb)
        doa_ref[...] = _mm_nt(dya, wa_ref[...])
        du2_ref[...] = _mm_nt(dyb, wb_ref[...])

    tile = pl.BlockSpec((tm, D), lambda i: (i, 0))
    wspec = pl.BlockSpec((D, D), lambda i: (0, 0))
    return pl.pallas_call(
        body, name="mixout_bwd", grid=(T // tm,),
        in_specs=[tile, tile, tile, tile, tile, tile, pl.BlockSpec((2, tm, D), lambda i: (3, i, 0)),
                  pl.BlockSpec((9, D), lambda i: (0, 0)), wspec, wspec, wspec],
        out_specs=[pl.BlockSpec((2, tm, D), lambda i: (3, i, 0)), tile, tile, wspec, wspec, wspec,
                   pl.BlockSpec((8, D), lambda i: (0, 0))],
        out_shape=[jax.ShapeDtypeStruct((8, T, D), F32), jax.ShapeDtypeStruct((T, D), F32),
                   jax.ShapeDtypeStruct((T, D), F32), jax.ShapeDtypeStruct((D, D), F32),
                   jax.ShapeDtypeStruct((D, D), F32), jax.ShapeDtypeStruct((D, D), F32),
                   jax.ShapeDtypeStruct((8, D), F32)],
        compiler_params=_cparams(1),
    )(dxo, oa, u2, ya, yb, mout, p, mod, w_a, w_b, w_o)


def _ada_fwd(cs_all, ada_w, ada_b_cols):
    def body(cs_ref, w_ref, b_ref, out_ref):
        out_ref[...] = jnp.dot(cs_ref[...], w_ref[...], preferred_element_type=F32,
                               precision=lax.Precision.HIGHEST) + b_ref[...]

    return pl.pallas_call(
        body, name="ada_fwd", out_shape=jax.ShapeDtypeStruct((N_DEV, ada_w.shape[1]), F32),
        compiler_params=pltpu.CompilerParams(vmem_limit_bytes=VMEM_LIMIT),
    )(cs_all, ada_w, ada_b_cols)


def _ada_wgrad(cs_all, dmod_cols):
    cs_t = jnp.pad(cs_all.T, ((0, 0), (0, HD - N_DEV)))
    dm = jnp.pad(dmod_cols, ((0, HD - N_DEV), (0, 0)))

    def body(cs_ref, d_ref, out_ref):
        out_ref[...] = jnp.dot(cs_ref[...], d_ref[...], preferred_element_type=F32,
                               precision=lax.Precision.HIGHEST)

    return pl.pallas_call(
        body, name="ada_wgrad", out_shape=jax.ShapeDtypeStruct((D, dmod_cols.shape[1]), F32),
        compiler_params=pltpu.CompilerParams(vmem_limit_bytes=VMEM_LIMIT),
    )(cs_t, dm)


def _adam_math(w, g, m, v):
    m2 = ADAM_B1 * m + (1.0 - ADAM_B1) * g
    v2 = ADAM_B2 * v + (1.0 - ADAM_B2) * (g * g)
    m_hat = m2 / (1.0 - ADAM_B1 ** ADAM_STEP)
    v_hat = v2 / (1.0 - ADAM_B2 ** ADAM_STEP)
    delta = -ADAM_LR * (m_hat / (jnp.sqrt(v_hat) + ADAM_EPS) + ADAM_WD * w)
    return delta, m2, v2


def _adamw(w, m, v, g, name):
    R, C = w.shape
    slots = g.ndim == 3
    tr = R
    for cand in (256, 176):
        if R % cand == 0 and R > cand:
            tr = cand
            break

    def body(w_ref, m_ref, v_ref, g_ref, go_ref, d_ref, mo_ref, vo_ref):
        if slots:
            gv = g_ref[0].astype(F32)
            for s in range(1, N_DEV):
                gv = gv + g_ref[s].astype(F32)
        else:
            gv = g_ref[...]
        go_ref[...] = gv
        d_ref[...], mo_ref[...], vo_ref[...] = _adam_math(w_ref[...], gv, m_ref[...], v_ref[...])

    tile = pl.BlockSpec((tr, C), lambda i: (i, 0))
    gspec = pl.BlockSpec((N_DEV, tr, C), lambda i: (0, i, 0)) if slots else tile
    sds = jax.ShapeDtypeStruct((R, C), F32)
    return pl.pallas_call(
        body, name=name, grid=(R // tr,), in_specs=[tile, tile, tile, gspec], out_specs=[tile] * 4,
        out_shape=[sds] * 4, compiler_params=_cparams(1),
    )(w, m, v, g)


def _sum_slots(pack):
    def body(p_ref, out_ref):
        acc = p_ref[0]
        for s in range(1, N_DEV):
            acc = acc + p_ref[s]
        out_ref[...] = acc

    return pl.pallas_call(body, name="sum_small", out_shape=jax.ShapeDtypeStruct(pack.shape[1:], F32))(pack)


def _me():
    return lax.axis_index("x"), lax.axis_index("y"), lax.axis_index("c")


def _peer(r):
    x, y, c = _me()
    px = 1 - x if r & 4 else x
    py = 1 - y if r & 2 else y
    pc = 1 - c if r & 1 else c
    return (px, py, pc), 4 * px + 2 * py + pc


def _allgather_small(x):
    R, C = x.shape

    def body(x_ref, out_ref, send_sems, recv_sems):
        mx, my, mc = _me()
        me = 4 * mx + 2 * my + mc
        mine = out_ref.at[pl.ds(pl.multiple_of(me * R, 8), R), :]
        copies = []
        for r in range(1, N_DEV):
            dev, _ = _peer(r)
            copies.append(pltpu.make_async_remote_copy(
                src_ref=x_ref, dst_ref=mine, send_sem=send_sems.at[r - 1], recv_sem=recv_sems.at[r - 1],
                device_id=dev, device_id_type=MESH))
        for cp in copies:
            cp.start()
        mine[...] = x_ref[...]
        for r in range(1, N_DEV):
            _, idx = _peer(r)
            theirs = out_ref.at[pl.ds(pl.multiple_of(idx * R, 8), R), :]
            pltpu.make_async_remote_copy(
                src_ref=x_ref, dst_ref=theirs, send_sem=send_sems.at[r - 1], recv_sem=recv_sems.at[r - 1],
                device_id=_peer(r)[0], device_id_type=MESH).wait_recv()
        for cp in copies:
            cp.wait_send()

    return pl.pallas_call(
        body, name="allgather_small_%dx%d" % (R, C),
        out_shape=jax.ShapeDtypeStruct((N_DEV * R, C), F32),
        in_specs=[pl.BlockSpec(memory_space=pltpu.VMEM)], out_specs=pl.BlockSpec(memory_space=pltpu.VMEM),
        scratch_shapes=[pltpu.SemaphoreType.DMA((N_DEV - 1,)), pltpu.SemaphoreType.DMA((N_DEV - 1,))],
    )(x)


def _xchg_copies(ins, outs, sems, gather):
    send_sems, recv_sems, local_sems = sems
    mx, my, mc = _me()
    me = 4 * mx + 2 * my + mc
    sibling = _peer(1)[0]

    def rdma(a, r, dev, src, slot):
        k = a * (N_DEV - 1) + r - 1
        return pltpu.make_async_remote_copy(
            src_ref=src, dst_ref=outs[a].at[slot], send_sem=send_sems.at[k], recv_sem=recv_sems.at[k],
            device_id=dev, device_id_type=MESH)

    own, sends, relays, recvs = [], [], [], []
    for a in range(len(ins)):
        own.append(pltpu.make_async_copy(ins[a] if gather else ins[a].at[me], outs[a].at[me], local_sems.at[a]))
        for r in range(1, N_DEV):
            dev, idx = _peer(r)
            if not gather:
                sends.append(rdma(a, r, dev, ins[a].at[idx], me))
                recvs.append(rdma(a, r, dev, ins[a].at[idx], idx))
            elif r == 1:
                sends.append(rdma(a, r, dev, ins[a], me))
                recvs.append(rdma(a, r, dev, ins[a], idx))
            elif r % 2 == 0:
                sends.append(rdma(a, r, dev, ins[a], me))
                relays.append((rdma(a, r, dev, ins[a], idx), rdma(a, r + 1, sibling, outs[a].at[idx], idx)))
            else:
                recvs.append(rdma(a, r, sibling, ins[a], idx))
    return own, sends, relays, recvs


def _xchg_start(ins, outs, sems, gather):
    own, sends, _, _ = _xchg_copies(ins, outs, sems, gather)
    for cp in own + sends:
        cp.start()


def _xchg_wait(ins, outs, sems, gather):
    own, sends, relays, recvs = _xchg_copies(ins, outs, sems, gather)
    for arrival, relay in relays:
        arrival.wait_recv()
        relay.start()
    for cp in recvs:
        cp.wait_recv()
    for cp in own:
        cp.wait()
    for cp in sends + [relay for _, relay in relays]:
        cp.wait_send()


def _xchg_specs(arrays, gather):
    n = len(arrays)
    out_shape = [jax.ShapeDtypeStruct(((N_DEV,) + a.shape) if gather else a.shape, a.dtype) for a in arrays]
    sems = [pltpu.SemaphoreType.DMA((n * (N_DEV - 1),)), pltpu.SemaphoreType.DMA((n * (N_DEV - 1),)),
            pltpu.SemaphoreType.DMA((n,))]
    return out_shape, sems


def _exchange(arrays, gather, name):
    n = len(arrays)

    def body(*refs):
        _xchg_start(refs[:n], refs[n:2 * n], refs[2 * n:], gather)
        _xchg_wait(refs[:n], refs[n:2 * n], refs[2 * n:], gather)

    out_shape, sems = _xchg_specs(arrays, gather)
    return pl.pallas_call(
        body, name=name, out_shape=out_shape,
        in_specs=[pl.BlockSpec(memory_space=pl.ANY)] * n, out_specs=[pl.BlockSpec(memory_space=pl.ANY)] * n,
        scratch_shapes=sems,
    )(*arrays)


def _gridded(body, carry, *, name, grid, in_specs, out_specs, out_shape, scratch_shapes=(), aliases=None):
    if carry is None:
        return pl.pallas_call(
            body, name=name, grid=grid, in_specs=list(in_specs), out_specs=list(out_specs),
            out_shape=list(out_shape), scratch_shapes=list(scratch_shapes), input_output_aliases=aliases or {},
            compiler_params=_cparams(len(grid)))
    arrays, gather = carry
    n, n_in, n_out, n_scr = len(arrays), len(in_specs), len(out_specs), len(scratch_shapes)
    c_shape, c_sems = _xchg_specs(arrays, gather)

    def wrapped(*refs):
        ins, cin = refs[:n_in], refs[n_in:n_in + n]
        o0 = n_in + n
        outs, cout = refs[o0:o0 + n_out], refs[o0 + n_out:o0 + n_out + n]
        s0 = o0 + n_out + n
        scr, sems = refs[s0:s0 + n_scr], refs[s0 + n_scr:]
        first = pl.program_id(0) == 0
        last = pl.program_id(0) == grid[0] - 1
        for ax in range(1, len(grid)):
            first = first & (pl.program_id(ax) == 0)
            last = last & (pl.program_id(ax) == grid[ax] - 1)

        @pl.when(first)
        def _():
            _xchg_start(cin, cout, sems, gather)

        body(*ins, *outs, *scr)

        @pl.when(last)
        def _():
            _xchg_wait(cin, cout, sems, gather)

    hbm = pl.BlockSpec(memory_space=pl.ANY)
    res = pl.pallas_call(
        wrapped, name=name, grid=grid, in_specs=list(in_specs) + [hbm] * n, out_specs=list(out_specs) + [hbm] * n,
        out_shape=list(out_shape) + c_shape, scratch_shapes=list(scratch_shapes) + c_sems,
        input_output_aliases=aliases or {}, compiler_params=_cparams(len(grid)),
    )
    return lambda *args: res(*args, *arrays)


def _local_step(x, target, mod, small, sh):
    w1_in, w1_out = _exchange([sh["ffn1_w_in"], sh["ffn1_w_out"]], True, "allgather_ffn1")
    w1_in, w1_out = _full_w_in(w1_in), w1_out.reshape(D_FF, D)
    (x1, a1, b1, f1, h1), (wm_in,) = _ffn_fwd(x, mod, 0, small["norm_ffn1"], w1_in, w1_out, 0.5, "ffn1_fwd",
                                              ([sh["mix_w_in"]], True))
    (p, h2), (wh_o, wc_o, wm_o, cw) = _mixin_fwd(
        x1, mod, 3, small["norm_mix"], wm_in,
        ([sh["hgrn_w_o"], sh["conv_w_o"], sh["mix_w_out"], sh["conv_w"]], True))
    wh_o, wc_o, wm_o = wh_o.reshape(D, D), wc_o.reshape(D, D), wm_o.reshape(D, D)
    cw = jnp.pad(cw.transpose(1, 0, 2).reshape(CONV_K, D), ((0, HALO - CONV_K), (0, 0)))
    (o, oa, a_all, s_all), (w2_in, w2_out) = _hgrn_fwd(p, small["hgrn_lb"], small["hgrn_g"],
                                                       ([sh["ffn2_w_in"], sh["ffn2_w_out"]], True))
    w2_in, w2_out = _full_w_in(w2_in), w2_out.reshape(D_FF, D)
    u1, u2 = _conv_fwd(p, cw, small["conv_b"], small["conv_ln_g"], small["conv_ln_b"])
    x2, ya, yb, mout = _mixout_fwd(x1, oa, u2, p, mod, 3, wh_o, wc_o, wm_o)
    (x3, a3, b3, f3, h3), _ = _ffn_fwd(x2, mod, 6, small["norm_ffn2"], w2_in, w2_out, 0.5, "ffn2_fwd", None)
    dx3, sm_head = _head(x3, target, small["norm_final"])

    (dx2, dw2_a, dw2_b, dw2_out, sm3), _ = _ffn_bwd(x2, h3, dx3, f3, a3, b3, mod, 6, small["norm_ffn2"], w2_in,
                                                    w2_out, 0.5, "ffn2_bwd", None)
    dp, doa, du2, dwh_o, dwc_o, dwm_o, sm_mo = _mixout_bwd(dx2, oa, u2, ya, yb, mout, p, mod, 3, wh_o, wc_o, wm_o)
    dp, dcw, sm_cv = _conv_bwd(p, u1, du2, cw, small["conv_ln_g"], small["conv_ln_b"], dp)
    rows = lambda t: t.reshape(N_DEV, -1, D).astype(MM)
    (dp, sm_hg), (r2_in, r2_out) = _hgrn_bwd(p, o, a_all, s_all, doa, small["hgrn_lb"], small["hgrn_g"], dp,
                                             ([_w_in_shards(dw2_a, dw2_b), rows(dw2_out)], False))
    (dx1, dwm_in, sm2), (rh_o, rc_o, rm_o, rcw) = _mixin_bwd(
        x1, h2, dx2, dp, mod, 3, small["norm_mix"], wm_in,
        ([rows(dwh_o), rows(dwc_o), rows(dwm_o), dcw[:CONV_K].reshape(CONV_K, N_DEV, -1).transpose(1, 0, 2)], False))
    (dx0, dw1_a, dw1_b, dw1_out, sm1), (rm_in,) = _ffn_bwd(x, h1, dx1, f1, a1, b1, mod, 0, small["norm_ffn1"],
                                                          w1_in, w1_out, 0.5, "ffn1_bwd",
                                                          ([dwm_in], False))
    r1_in, r1_out = _exchange([_w_in_shards(dw1_a, dw1_b), rows(dw1_out)], False, "scatter_ffn1")

    dmod = jnp.concatenate([sm1[0:3], sm2[0:2], sm_mo[2:3], sm3[0:3]], axis=0)
    gsmall = dict(norm_ffn1=sm1[3:4], norm_mix=sm2[3:4], lb0=sm_hg[0:1], hgrn_g=sm_hg[1:2], conv_b=sm_cv[0:1],
                  conv_ln_g=sm_cv[1:2], conv_ln_b=sm_cv[2:3], norm_ffn2=sm3[3:4], norm_final=sm_head[0:1])
    recv = dict(ffn1_w_in=r1_in, ffn1_w_out=r1_out, mix_w_in=rm_in, hgrn_w_o=rh_o, conv_w=rcw, conv_w_o=rc_o,
                mix_w_out=rm_o, ffn2_w_in=r2_in, ffn2_w_out=r2_out)
    return sm_head[1, 0], dx0, dmod, gsmall, recv


def _full_w_in(g):
    return g.transpose(1, 0, 2).reshape(D, -1)


def _w_in_shards(dwa, dwb):
    half = N_DEV // 2
    return jnp.concatenate([t.reshape(D, half, -1).transpose(1, 0, 2) for t in (dwa, dwb)], axis=0)


SMALL_ORDER = ("norm_ffn1", "norm_mix", "lb0", "hgrn_g", "conv_b", "conv_ln_g", "conv_ln_b", "norm_ffn2",
               "norm_final")
PACK_ROWS = 24


def kernel(x, c, ada_w, ada_b, norm_ffn1, ffn1_w_in, ffn1_w_out, norm_mix, mix_w_in, hgrn_lb, hgrn_g, hgrn_w_o, conv_w, conv_b, conv_ln_g, conv_ln_b, conv_w_o, mix_w_out, norm_ffn2, ffn2_w_in, ffn2_w_out, norm_final, loss_target, m_ada_w, m_ada_b, m_norm_ffn1, m_ffn1_w_in, m_ffn1_w_out, m_norm_mix, m_mix_w_in, m_hgrn_lb, m_hgrn_g, m_hgrn_w_o, m_conv_w, m_conv_b, m_conv_ln_g, m_conv_ln_b, m_conv_w_o, m_mix_w_out, m_norm_ffn2, m_ffn2_w_in, m_ffn2_w_out, m_norm_final, v_ada_w, v_ada_b, v_norm_ffn1, v_ffn1_w_in, v_ffn1_w_out, v_norm_mix, v_mix_w_in, v_hgrn_lb, v_hgrn_g, v_hgrn_w_o, v_conv_w, v_conv_b, v_conv_ln_g, v_conv_ln_b, v_conv_w_o, v_mix_w_out, v_norm_ffn2, v_ffn2_w_in, v_ffn2_w_out, v_norm_final):
    mx, my, mc = _me()
    me = 4 * mx + 2 * my + mc
    ncol = ada_w.shape[2]

    cs = jnp.broadcast_to(c * jax.nn.sigmoid(c), (8, D))
    cs_all = _allgather_small(cs).reshape(N_DEV, 8, D)[:, 0, :]
    ada_b_cols = lax.dynamic_slice(ada_b, (0, me * ncol), (1, ncol))
    mod_cols = _ada_fwd(cs_all, ada_w[0], ada_b_cols)
    mod_all = _allgather_small(mod_cols).reshape(N_DEV, N_DEV, ncol)
    mod = lax.dynamic_index_in_dim(mod_all, me, axis=1, keepdims=False).reshape(9, D)

    sh = dict(ffn1_w_in=ffn1_w_in, ffn1_w_out=ffn1_w_out, mix_w_in=mix_w_in, hgrn_w_o=hgrn_w_o,
              conv_w_o=conv_w_o, mix_w_out=mix_w_out, ffn2_w_in=ffn2_w_in, ffn2_w_out=ffn2_w_out)
    sh = {n: w[0].astype(MM) for n, w in sh.items()}
    sh["conv_w"] = conv_w[0]
    small = dict(norm_ffn1=norm_ffn1, norm_mix=norm_mix, hgrn_lb=hgrn_lb, hgrn_g=hgrn_g, conv_b=conv_b,
                 conv_ln_g=conv_ln_g, conv_ln_b=conv_ln_b, norm_ffn2=norm_ffn2, norm_final=norm_final.reshape(1, D))

    loss_local, dx, dmod, gsmall, recv = _local_step(x[0], loss_target[0], mod, small, sh)
    loss = lax.psum(loss_local, ("x", "y", "c"))

    pack = jnp.concatenate([dmod] + [gsmall[n] for n in SMALL_ORDER]
                           + [jnp.zeros((PACK_ROWS - 9 - len(SMALL_ORDER), D), F32)], axis=0)
    pack_all = _allgather_small(pack).reshape(N_DEV, PACK_ROWS, D)
    tot = _sum_slots(pack_all)
    gs = {n: tot[9 + i:10 + i] for i, n in enumerate(SMALL_ORDER)}
    dmod_all = pack_all[:, 0:9, :].reshape(N_DEV, 9 * D)
    g_ada_b = tot[0:9].reshape(1, 9 * D)
    g_ada_w = _ada_wgrad(cs_all, lax.dynamic_slice(dmod_all, (0, me * ncol), (N_DEV, ncol)))
    z = hgrn_lb.astype(F32)
    p0 = jax.nn.sigmoid(z[0:1] - z[1:2])
    dz0 = p0 * (1.0 - p0) * gs["lb0"]
    g_hgrn_lb = jnp.concatenate([dz0, -dz0], axis=0)

    res = {}
    res["ada_w"] = _adamw(ada_w[0], m_ada_w[0], v_ada_w[0], g_ada_w, "adamw_ada_w")
    big = dict(ffn1_w_in=(ffn1_w_in, m_ffn1_w_in, v_ffn1_w_in), ffn1_w_out=(ffn1_w_out, m_ffn1_w_out, v_ffn1_w_out),
               mix_w_in=(mix_w_in, m_mix_w_in, v_mix_w_in), hgrn_w_o=(hgrn_w_o, m_hgrn_w_o, v_hgrn_w_o),
               conv_w=(conv_w, m_conv_w, v_conv_w), conv_w_o=(conv_w_o, m_conv_w_o, v_conv_w_o),
               mix_w_out=(mix_w_out, m_mix_w_out, v_mix_w_out), ffn2_w_in=(ffn2_w_in, m_ffn2_w_in, v_ffn2_w_in),
               ffn2_w_out=(ffn2_w_out, m_ffn2_w_out, v_ffn2_w_out))
    for n, (w, m, v) in big.items():
        res[n] = _adamw(w[0], m[0], v[0], recv[n], "adamw_" + n)
    sm_names = ("ada_b", "norm_ffn1", "norm_mix", "hgrn_lb", "hgrn_g", "conv_b", "conv_ln_g", "conv_ln_b",
                "norm_ffn2", "norm_final")
    sm_w = dict(ada_b=(ada_b, m_ada_b, v_ada_b), norm_ffn1=(norm_ffn1, m_norm_ffn1, v_norm_ffn1),
                norm_mix=(norm_mix, m_norm_mix, v_norm_mix), hgrn_lb=(hgrn_lb, m_hgrn_lb, v_hgrn_lb),
                hgrn_g=(hgrn_g, m_hgrn_g, v_hgrn_g), conv_b=(conv_b, m_conv_b, v_conv_b),
                conv_ln_g=(conv_ln_g, m_conv_ln_g, v_conv_ln_g), conv_ln_b=(conv_ln_b, m_conv_ln_b, v_conv_ln_b),
                norm_ffn2=(norm_ffn2, m_norm_ffn2, v_norm_ffn2), norm_final=(norm_final, m_norm_final, v_norm_final))
    sm_g = dict(gs, ada_b=g_ada_b, hgrn_lb=g_hgrn_lb)
    rows = {n: sm_w[n][0].size // D for n in sm_names}
    n_rows = sum(rows.values())
    pad = (-n_rows) % 8
    stack = lambda parts: jnp.concatenate([q.reshape(-1, D) for q in parts] + [jnp.ones((pad, D), F32)], axis=0)
    st = _adamw(stack([sm_w[n][0] for n in sm_names]), stack([sm_w[n][1] for n in sm_names]),
                stack([sm_w[n][2] for n in sm_names]), stack([sm_g[n] for n in sm_names]), "adamw_small")
    off = 0
    for n in sm_names:
        res[n] = tuple(t[off:off + rows[n]].reshape(sm_w[n][0].shape) for t in st)
        off += rows[n]

    order = ("ada_w", "ada_b", "norm_ffn1", "ffn1_w_in", "ffn1_w_out", "norm_mix", "mix_w_in", "hgrn_lb", "hgrn_g",
             "hgrn_w_o", "conv_w", "conv_b", "conv_ln_g", "conv_ln_b", "conv_w_o", "mix_w_out", "norm_ffn2",
             "ffn2_w_in", "ffn2_w_out", "norm_final")
    lead = lambda n, t: t[None] if n in big or n == "ada_w" else t
    outs = [loss, dx[None]]
    for j in range(4):
        outs += [lead(n, res[n][j]) for n in order]
    return tuple(outs)
```

```python
import functools

import jax
import jax.numpy as jnp
from jax import lax
from jax.experimental import pallas as pl
from jax.experimental.pallas import tpu as pltpu

F32 = jnp.float32
MM = jnp.bfloat16
ACT = jnp.bfloat16

D = 1024
D_FF = 2816
HEADS = 8
HD = 128
CHUNK = 64
SUB = 16
NSUB = CHUNK // SUB
CONV_K = 31
HALO = 32
EPS = 1e-6
N_DEV = 8
NEG = -1e30
Q_SCALE = HD ** -0.5

ADAM_LR = 0.001
ADAM_B1 = 0.9
ADAM_B2 = 0.999
ADAM_EPS = 1e-08
ADAM_WD = 0.01
ADAM_STEP = 10

VMEM_LIMIT = 60 * 1024 * 1024
MESH = pl.DeviceIdType.MESH


def _cparams(n_axes):
    return pltpu.CompilerParams(dimension_semantics=("arbitrary",) * n_axes, vmem_limit_bytes=VMEM_LIMIT)


def _mm(a, b):
    return lax.dot_general(a.astype(MM), b.astype(MM), (((1,), (0,)), ((), ())), preferred_element_type=F32)


def _mm_nt(a, b):
    return lax.dot_general(a.astype(MM), b.astype(MM), (((1,), (1,)), ((), ())), preferred_element_type=F32)


def _mm_tn(a, b):
    return lax.dot_general(a.astype(MM), b.astype(MM), (((0,), (0,)), ((), ())), preferred_element_type=F32)


def _sig(x):
    return 1.0 / (1.0 + jnp.exp(-x))


def _colsum(x):
    return jnp.sum(x, axis=0, keepdims=True)


def _rowmean(x):
    return jnp.mean(x, axis=-1, keepdims=True)


def _modnorm_fwd(xv, g, sh, sc):
    r = lax.rsqrt(_rowmean(xv * xv) + EPS)
    xh = xv * r
    n = xh * g
    return n * (1.0 + sc) + sh, xh, n, r


def _modnorm_bwd(dh, xh, n, r, g, sc):
    dsc = _colsum(dh * n)
    dsh = _colsum(dh)
    dn = dh * (1.0 + sc)
    dg = _colsum(dn * xh)
    dxh = dn * g
    dx = r * (dxh - xh * _rowmean(dxh * xh))
    return dx, dsh, dsc, dg


def _ffn_fwd(x, mod, mo, gnorm, w_in, w_out, res, name, carry):
    T = x.shape[0]
    tm = min(512, T)
    tn = D_FF // 2
    nj = D_FF // tn

    def body(x_ref, mod_ref, g_ref, wa_ref, wb_ref, wo_ref, xo_ref, a_ref, b_ref, f_ref, h_ref, acc_scr):
        j = pl.program_id(1)

        @pl.when(j == 0)
        def _():
            h, _, _, _ = _modnorm_fwd(x_ref[...], g_ref[...], mod_ref[mo:mo + 1, :], mod_ref[mo + 1:mo + 2, :])
            h_ref[...] = h.astype(ACT)
            acc_scr[...] = jnp.zeros_like(acc_scr)

        h = h_ref[...]
        a = _mm(h, wa_ref[...])
        b = _mm(h, wb_ref[...])
        a_ref[...] = a.astype(ACT)
        b_ref[...] = b.astype(ACT)
        s = a * _sig(a) * b
        acc_scr[...] += _mm(s, wo_ref[...])

        @pl.when(j == nj - 1)
        def _():
            f = acc_scr[...]
            f_ref[...] = f
            xo_ref[...] = x_ref[...] + res * mod_ref[mo + 2:mo + 3, :] * f

    out = _gridded(
        body, carry, name=name, grid=(T // tm, nj),
        in_specs=[
            pl.BlockSpec((tm, D), lambda i, j: (i, 0)),
            pl.BlockSpec((9, D), lambda i, j: (0, 0)),
            pl.BlockSpec((1, D), lambda i, j: (0, 0)),
            pl.BlockSpec((D, tn), lambda i, j: (0, j)),
            pl.BlockSpec((D, tn), lambda i, j: (0, j + nj)),
            pl.BlockSpec((tn, D), lambda i, j: (j, 0)),
        ],
        out_specs=[
            pl.BlockSpec((tm, D), lambda i, j: (i, 0)),
            pl.BlockSpec((tm, tn), lambda i, j: (i, j)),
            pl.BlockSpec((tm, tn), lambda i, j: (i, j)),
            pl.BlockSpec((tm, D), lambda i, j: (i, 0)),
            pl.BlockSpec((tm, D), lambda i, j: (i, 0)),
        ],
        out_shape=[
            jax.ShapeDtypeStruct((T, D), F32),
            jax.ShapeDtypeStruct((T, D_FF), ACT),
            jax.ShapeDtypeStruct((T, D_FF), ACT),
            jax.ShapeDtypeStruct((T, D), F32),
            jax.ShapeDtypeStruct((T, D), ACT),
        ],
        scratch_shapes=[pltpu.VMEM((tm, D), F32)],
    )(x, mod, gnorm, w_in, w_in, w_out)
    return out[:5], out[5:]


def _ffn_bwd(x, h, dxo, f, a, b, mod, mo, gnorm, w_in, w_out, res, name, carry):
    T = x.shape[0]
    tm = min(512, T)
    ni = T // tm
    tn = 256
    nj = D_FF // tn

    def body(x_ref, h_ref, dxo_ref, f_ref, a_ref, b_ref, mod_ref, g_ref, wa_ref, wb_ref, wo_ref,
             dx_ref, dwa_ref, dwb_ref, dwo_ref, sm_ref, dh_scr, acc_a, acc_b, acc_o, df_all):
        j = pl.program_id(0)
        i = pl.program_id(1)

        @pl.when(i == 0)
        def _():
            acc_a[...] = jnp.zeros_like(acc_a)
            acc_b[...] = jnp.zeros_like(acc_b)
            acc_o[...] = jnp.zeros_like(acc_o)

        @pl.when(j == 0)
        def _():
            dh_scr[i] = jnp.zeros((tm, D), F32)
            df_all[i] = (res * mod_ref[mo + 2:mo + 3, :] * dxo_ref[...]).astype(MM)

        @pl.when((j == 0) & (i == 0))
        def _():
            sm_ref[...] = jnp.zeros_like(sm_ref)

        hb = h_ref[...]
        df = df_all[i]
        av = a_ref[...].astype(F32)
        bv = b_ref[...].astype(F32)
        sg = _sig(av)
        sa = av * sg
        s = (sa * bv).astype(MM)
        ds = _mm_nt(df, wo_ref[...])
        da = (ds * bv * sg * (1.0 + av * (1.0 - sg))).astype(MM)
        db = (ds * sa).astype(MM)
        acc_o[...] += _mm_tn(s, df)
        acc_a[...] += _mm_tn(hb, da)
        acc_b[...] += _mm_tn(hb, db)
        dh_scr[i] += _mm_nt(da, wa_ref[...]) + _mm_nt(db, wb_ref[...])

        @pl.when(i == ni - 1)
        def _():
            dwa_ref[...] = acc_a[...].astype(MM)
            dwb_ref[...] = acc_b[...].astype(MM)
            dwo_ref[...] = acc_o[...].astype(MM)

        @pl.when(j == nj - 1)
        def _():
            sc = mod_ref[mo + 1:mo + 2, :]
            _, xh, n, r = _modnorm_fwd(x_ref[...], g_ref[...], mod_ref[mo:mo + 1, :], sc)
            dxn, dsh, dsc, dg = _modnorm_bwd(dh_scr[i], xh, n, r, g_ref[...], sc)
            dxo_v = dxo_ref[...]
            dx_ref[...] = dxo_v + dxn
            sm_ref[0:1, :] += dsh
            sm_ref[1:2, :] += dsc
            sm_ref[2:3, :] += _colsum(dxo_v * f_ref[...]) * res
            sm_ref[3:4, :] += dg

    out = _gridded(
        body, carry, name=name, grid=(nj, ni),
        in_specs=[
            pl.BlockSpec((tm, D), lambda j, i: (jnp.where(j == nj - 1, i, 0), 0)),
            pl.BlockSpec((tm, D), lambda j, i: (i, 0)),
            pl.BlockSpec((tm, D), lambda j, i: (jnp.where((j == 0) | (j == nj - 1), i, ni - 1), 0)),
            pl.BlockSpec((tm, D), lambda j, i: (jnp.where(j == nj - 1, i, 0), 0)),
            pl.BlockSpec((tm, tn), lambda j, i: (i, j)),
            pl.BlockSpec((tm, tn), lambda j, i: (i, j)),
            pl.BlockSpec((9, D), lambda j, i: (0, 0)),
            pl.BlockSpec((1, D), lambda j, i: (0, 0)),
            pl.BlockSpec((D, tn), lambda j, i: (0, j)),
            pl.BlockSpec((D, tn), lambda j, i: (0, j + nj)),
            pl.BlockSpec((tn, D), lambda j, i: (j, 0)),
        ],
        out_specs=[
            pl.BlockSpec((tm, D), lambda j, i: (jnp.where(j == nj - 1, i, 0), 0)),
            pl.BlockSpec((D, tn), lambda j, i: (0, j)),
            pl.BlockSpec((D, tn), lambda j, i: (0, j)),
            pl.BlockSpec((tn, D), lambda j, i: (j, 0)),
            pl.BlockSpec((8, D), lambda j, i: (0, 0)),
        ],
        out_shape=[
            jax.ShapeDtypeStruct((T, D), F32),
            jax.ShapeDtypeStruct((D, D_FF), MM),
            jax.ShapeDtypeStruct((D, D_FF), MM),
            jax.ShapeDtypeStruct((D_FF, D), MM),
            jax.ShapeDtypeStruct((8, D), F32),
        ],
        scratch_shapes=[pltpu.VMEM((ni, tm, D), F32), pltpu.VMEM((D, tn), F32), pltpu.VMEM((D, tn), F32),
                        pltpu.VMEM((tn, D), F32), pltpu.VMEM((ni, tm, D), MM)],
    )(x, h, dxo, f, a, b, mod, gnorm, w_in, w_in, w_out)
    return out[:5], out[5:]


def _head(x, target, gfin):
    T = x.shape[0]
    tm = min(512, T)
    ni = T // tm

    def body(x_ref, t_ref, g_ref, dx_ref, sm_ref):
        i = pl.program_id(0)

        @pl.when(i == 0)
        def _():
            sm_ref[...] = jnp.zeros_like(sm_ref)

        xv = x_ref[...]
        g = g_ref[...]
        r = lax.rsqrt(_rowmean(xv * xv) + EPS)
        xh = xv * r
        e = xh * g - t_ref[...]
        sm_ref[1:2, :] += _colsum(e * e) * (0.5 / D)
        dy = e * (1.0 / D)
        sm_ref[0:1, :] += _colsum(dy * xh)
        dxh = dy * g
        dx_ref[...] = r * (dxh - xh * _rowmean(dxh * xh))

        @pl.when(i == ni - 1)
        def _():
            sm_ref[1:2, :] = jnp.broadcast_to(jnp.sum(sm_ref[1:2, :], axis=-1, keepdims=True), (1, D))

    return pl.pallas_call(
        body, name="head_loss", grid=(ni,),
        in_specs=[pl.BlockSpec((tm, D), lambda i: (i, 0)), pl.BlockSpec((tm, D), lambda i: (i, 0)),
                  pl.BlockSpec((1, D), lambda i: (0, 0))],
        out_specs=[pl.BlockSpec((tm, D), lambda i: (i, 0)), pl.BlockSpec((8, D), lambda i: (0, 0))],
        out_shape=[jax.ShapeDtypeStruct((T, D), F32), jax.ShapeDtypeStruct((8, D), F32)],
        compiler_params=_cparams(1),
    )(x, target, gfin)


def _mixin_fwd(x, mod, mo, gnorm, w, carry):
    T = x.shape[0]
    tm = min(1024, T)
    ni = T // tm

    def body(x_ref, mod_ref, g_ref, w_ref, p_ref, h_ref, h_all):
        i = pl.program_id(1)

        @pl.when(pl.program_id(0) == 0)
        def _():
            h, _, _, _ = _modnorm_fwd(x_ref[...], g_ref[...], mod_ref[mo:mo + 1, :], mod_ref[mo + 1:mo + 2, :])
            h_all[i] = h.astype(ACT)
            h_ref[...] = h.astype(ACT)

        p_ref[0] = _mm(h_all[i], w_ref[0])

    first = lambda k, i: (jnp.where(k == 0, i, ni - 1), 0)
    out = _gridded(
        body, carry, name="mixin_fwd", grid=(8, ni),
        in_specs=[pl.BlockSpec((tm, D), first), pl.BlockSpec((9, D), lambda k, i: (0, 0)),
                  pl.BlockSpec((1, D), lambda k, i: (0, 0)), pl.BlockSpec((1, D, D), lambda k, i: (k, 0, 0))],
        out_specs=[pl.BlockSpec((1, tm, D), lambda k, i: (k, i, 0)), pl.BlockSpec((tm, D), first)],
        out_shape=[jax.ShapeDtypeStruct((8, T, D), F32), jax.ShapeDtypeStruct((T, D), ACT)],
        scratch_shapes=[pltpu.VMEM((ni, tm, D), ACT)],
    )(x, mod, gnorm, w)
    return out[:2], out[2:]


def _mixin_bwd(x, h, dxo, dp, mod, mo, gnorm, w, carry):
    T = x.shape[0]
    tm = min(512, T)
    ni = T // tm

    def body(x_ref, h_ref, dxo_ref, dp_ref, mod_ref, g_ref, w_ref, dx_ref, dw_ref, sm_ref, dh_scr, acc):
        k = pl.program_id(0)
        i = pl.program_id(1)

        @pl.when(i == 0)
        def _():
            acc[...] = jnp.zeros_like(acc)

        @pl.when(k == 0)
        def _():
            dh_scr[i] = jnp.zeros((tm, D), F32)

        @pl.when((k == 0) & (i == 0))
        def _():
            sm_ref[...] = jnp.zeros_like(sm_ref)

        dpk = dp_ref[0].astype(MM)
        acc[...] += _mm_tn(h_ref[...], dpk)
        dh_scr[i] += _mm_nt(dpk, w_ref[0])

        @pl.when(i == ni - 1)
        def _():
            dw_ref[0] = acc[...].astype(MM)

        @pl.when(k == 7)
        def _():
            sc = mod_ref[mo + 1:mo + 2, :]
            _, xh, n, r = _modnorm_fwd(x_ref[...], g_ref[...], mod_ref[mo:mo + 1, :], sc)
            dxn, dsh, dsc, dg = _modnorm_bwd(dh_scr[i], xh, n, r, g_ref[...], sc)
            dx_ref[...] = dxo_ref[...] + dxn
            sm_ref[0:1, :] += dsh
            sm_ref[1:2, :] += dsc
            sm_ref[3:4, :] += dg

    out = _gridded(
        body, carry, name="mixin_bwd", grid=(8, ni),
        in_specs=[pl.BlockSpec((tm, D), lambda k, i: (jnp.where(k == 7, i, 0), 0)),
                  pl.BlockSpec((tm, D), lambda k, i: (i, 0)),
                  pl.BlockSpec((tm, D), lambda k, i: (jnp.where(k == 7, i, 0), 0)),
                  pl.BlockSpec((1, tm, D), lambda k, i: (k, i, 0)), pl.BlockSpec((9, D), lambda k, i: (0, 0)),
                  pl.BlockSpec((1, D), lambda k, i: (0, 0)), pl.BlockSpec((1, D, D), lambda k, i: (k, 0, 0))],
        out_specs=[pl.BlockSpec((tm, D), lambda k, i: (jnp.where(k == 7, i, 0), 0)),
                   pl.BlockSpec((1, D, D), lambda k, i: (k, 0, 0)),
                   pl.BlockSpec((8, D), lambda k, i: (0, 0))],
        out_shape=[jax.ShapeDtypeStruct((T, D), F32), jax.ShapeDtypeStruct((8, D, D), MM),
                   jax.ShapeDtypeStruct((8, D), F32)],
        scratch_shapes=[pltpu.VMEM((ni, tm, D), F32), pltpu.VMEM((D, D), F32)],
    )(x, h, dxo, dp, mod, gnorm, w)
    return out[:3], out[3:]


def _hgrn_consts():
    rows = jnp.arange(SUB * HD) // HD
    e = (rows[:, None] == jnp.arange(HD)[None, :]).astype(MM)
    return e, e.T


def _rows_bcast(ref, cb, first, n):
    parts = [jnp.broadcast_to(ref[pl.ds(c * CHUNK + first, 1), :], (n, HD)) for c in range(cb // CHUNK)]
    return jnp.concatenate(parts, axis=0)


def _hgrn_pre(qr, fr, lb_ref, b_scr, cb):
    z = lb_ref[...]
    lb = _sig(z[0:1, :] - z[1:2, :])
    sq = _sig(qr)
    q = qr * sq * Q_SCALE
    sf = _sig(fr)
    fg = lb + (1.0 - lb) * sf
    lf = jnp.log(fg)
    k = 1.0 - fg
    tl = lax.broadcasted_iota(jnp.int32, (cb, HD), 0) % CHUNK
    bc = lf
    sh = 1
    while sh < CHUNK:
        bc = bc + jnp.where(tl >= sh, pltpu.roll(bc, sh, 0), 0.0)
        sh *= 2
    b_scr[...] = bc
    bl = _rows_bcast(b_scr, cb, CHUNK - 1, CHUNK)
    br = [None] + [_rows_bcast(b_scr, cb, SUB * i - 1, CHUNK) for i in range(1, NSUB)]
    sb = tl // SUB
    bref = jnp.where(sb == 0, bc, jnp.where(sb == 1, br[1], jnp.where(sb == 2, br[2], br[3])))
    eb = jnp.exp(bc)
    ekd = jnp.exp(bl - bc)
    eqo = jnp.exp(bc - bref)
    eko = [None] + [jnp.exp(jnp.where(tl < SUB * i, br[i] - bc, NEG)) for i in range(1, NSUB)]
    return dict(lb=lb, sq=sq, q=q, sf=sf, fg=fg, k=k, tl=tl, sb=sb, b=bc, bl=bl, eb=eb, ekd=ekd, eqo=eqo,
                eko=eko, qe=q * eb, kd=k * ekd, qo=q * eqo, ko=[None] + [k * eko[i] for i in range(1, NSUB)])


def _pad_rows(x):
    return jnp.concatenate([x, jnp.zeros_like(x)], axis=0)


def _by_subblock(sbc, parts):
    out = jnp.zeros_like(parts[1])
    for i in range(1, NSUB):
        out = jnp.where(sbc == i, parts[i], out)
    return out


def _hgrn_fwd(p, hgrn_lb, hgrn_g, carry):
    T = p.shape[1]
    cb = min(512, T)
    nch = cb // CHUNK
    ncb = T // cb
    e_mat, _ = _hgrn_consts()

    def body(p_ref, lb_ref, g_ref, e_ref, o_ref, oa_ref, a_ref, s_ref, st_scr, q_scr, k_scr, b_scr, z_scr):
        @pl.when(pl.program_id(1) == 0)
        def _():
            st_scr[...] = jnp.zeros_like(st_scr)

        v = p_ref[2]
        og = p_ref[3]
        pre = _hgrn_pre(p_ref[0], p_ref[1], lb_ref, b_scr, cb)
        q_scr[...] = pre["q"]
        k_scr[...] = pre["k"]
        ti = lax.broadcasted_iota(jnp.int32, (SUB, HD), 0)

        def zbody(c, carry):
            for i in range(NSUB):
                r0 = pl.multiple_of(c * CHUNK + SUB * i, SUB)
                qi = q_scr[pl.ds(r0, SUB), :]
                bi = b_scr[pl.ds(r0, SUB), :]
                for s in range(SUB):
                    krow = k_scr[pl.ds(r0 + s, 1), :]
                    brow = b_scr[pl.ds(r0 + s, 1), :]
                    if s < 8:
                        zz = qi * krow * jnp.exp(jnp.where(ti >= s, bi - brow, NEG))
                    else:
                        lo = qi[8:] * krow * jnp.exp(jnp.where(ti[8:] >= s, bi[8:] - brow, NEG))
                        zz = jnp.concatenate([jnp.zeros((8, HD), F32), lo], axis=0)
                    z_scr[i, pl.ds(pl.multiple_of(c * SUB, SUB), SUB), s * HD:(s + 1) * HD] = zz.astype(MM)
            return carry

        lax.fori_loop(0, nch, zbody, 0)
        adiag = [_mm(z_scr[i], e_ref[...]) for i in range(NSUB)]
        sbc = lax.broadcasted_iota(jnp.int32, (CHUNK, HD), 0) // SUB
        chunks = [slice(c * CHUNK, (c + 1) * CHUNK) for c in range(nch)]
        offs = [[_mm_nt(pre["qo"][rs], _pad_rows(pre["ko"][i][rs])) for i in range(1, NSUB)] for rs in chunks]
        kv = [_mm_tn(v[rs], pre["kd"][rs]) for rs in chunks]
        a_parts = []
        for c in range(nch):
            dparts = []
            for i in range(NSUB):
                blk = adiag[i][c * SUB:(c + 1) * SUB]
                dparts.append(blk if i == 0 else pltpu.roll(blk, SUB * i, 1))
            a_parts.append(_by_subblock(sbc, [None] + offs[c]) + jnp.concatenate(dparts, axis=0))
        a_ref[0] = jnp.concatenate(a_parts, axis=0)
        o_intra = [_mm(a_parts[c], _pad_rows(v[rs])) for c, rs in enumerate(chunks)]
        states = []
        st = st_scr[...]
        for c in range(nch):
            states.append(st)
            st = st * jnp.exp(b_scr[pl.ds(c * CHUNK + CHUNK - 1, 1), :]) + kv[c]
        st_scr[...] = st
        for c in range(nch):
            s_ref[0, c] = states[c]
        o = jnp.concatenate([o_intra[c] + _mm_nt(pre["qe"][rs], states[c]) for c, rs in enumerate(chunks)], axis=0)
        o_ref[...] = o
        on = o * lax.rsqrt(_rowmean(o * o) + EPS) * g_ref[...]
        oa_ref[...] = (on * og * _sig(og)).astype(ACT)

    out = _gridded(
        body, carry, name="hgrn_fwd", grid=(HEADS, ncb),
        in_specs=[pl.BlockSpec((4, cb, HD), lambda h, c: (0, c, h)),
                  pl.BlockSpec((2, HD), lambda h, c: (0, h)),
                  pl.BlockSpec((1, HD), lambda h, c: (0, h)),
                  pl.BlockSpec((SUB * HD, HD), lambda h, c: (0, 0))],
        out_specs=[pl.BlockSpec((cb, HD), lambda h, c: (c, h)),
                   pl.BlockSpec((cb, HD), lambda h, c: (c, h)),
                   pl.BlockSpec((1, cb, HD), lambda h, c: (h, c, 0)),
                   pl.BlockSpec((1, nch, HD, HD), lambda h, c: (h, c, 0, 0))],
        out_shape=[jax.ShapeDtypeStruct((T, D), F32), jax.ShapeDtypeStruct((T, D), ACT),
                   jax.ShapeDtypeStruct((HEADS, T, HD), F32),
                   jax.ShapeDtypeStruct((HEADS, T // CHUNK, HD, HD), F32)],
        scratch_shapes=[pltpu.VMEM((HD, HD), F32), pltpu.VMEM((cb, HD), F32), pltpu.VMEM((cb, HD), F32),
                        pltpu.VMEM((cb, HD), F32), pltpu.VMEM((NSUB, nch * SUB, SUB * HD), MM)],
    )(p, hgrn_lb, hgrn_g, e_mat)
    return out[:4], out[4:]


def _hgrn_bwd(p, o, a_all, s_all, doa, hgrn_lb, hgrn_g, dp, carry):
    T = p.shape[1]
    cb = min(512, T)
    nch = cb // CHUNK
    ncb = T // cb
    _, et_mat = _hgrn_consts()

    def body(p_ref, o_ref, a_ref, s_ref, doa_ref, lb_ref, g_ref, et_ref, dp_in, dp_ref, sm_ref,
             dst_scr, q_scr, k_scr, b_scr, x_scr, dqd_scr, dkd_scr):
        del dp_in

        @pl.when(pl.program_id(1) == 0)
        def _():
            dst_scr[...] = jnp.zeros_like(dst_scr)
            sm_ref[...] = jnp.zeros_like(sm_ref)

        qr = p_ref[0]
        v = p_ref[2]
        og = p_ref[3]
        pre = _hgrn_pre(qr, p_ref[1], lb_ref, b_scr, cb)
        q, k = pre["q"], pre["k"]
        q_scr[...] = q
        k_scr[...] = k
        g = g_ref[...]
        ov = o_ref[...]
        r = lax.rsqrt(_rowmean(ov * ov) + EPS)
        oh = ov * r
        sgo = _sig(og)
        doa_v = doa_ref[...]
        don = doa_v * og * sgo
        dog = doa_v * oh * g * sgo * (1.0 + og * (1.0 - sgo))
        sm_ref[1:2, :] += _colsum(don * oh)
        doh = don * g
        do = r * (doh - oh * _rowmean(doh * oh))

        sbc = lax.broadcasted_iota(jnp.int32, (CHUNK, HD), 0) // SUB
        row_i = lax.broadcasted_iota(jnp.int32, (CHUNK, HD), 0)
        lane_i = lax.broadcasted_iota(jnp.int32, (CHUNK, HD), 1)
        causal = lane_i <= row_i
        chunks = [slice(c * CHUNK, (c + 1) * CHUNK) for c in range(nch)]
        da_parts = [jnp.where(causal, _mm_nt(do[rs], _pad_rows(v[rs])), 0.0) for rs in chunks]
        dv_parts = [_mm_tn(a_ref[0, rs, :], do[rs])[:CHUNK] for rs in chunks]
        dqoff_mm = [[_mm(da_parts[c], _pad_rows(pre["ko"][i][rs])) for i in range(1, NSUB)]
                    for c, rs in enumerate(chunks)]
        dkoff_mm = [[_mm_tn(jnp.where(sbc == i, da_parts[c], 0.0), pre["qo"][rs])[:CHUNK] for i in range(1, NSUB)]
                    for c, rs in enumerate(chunks)]
        dqoff_parts = [_by_subblock(sbc, [None] + dqoff_mm[c]) for c in range(nch)]
        dkoff_parts = []
        for c, rs in enumerate(chunks):
            dko = pre["eko"][1][rs] * dkoff_mm[c][0]
            for i in range(2, NSUB):
                dko = dko + pre["eko"][i][rs] * dkoff_mm[c][i - 1]
            dkoff_parts.append(dko)
        for i in range(NSUB):
            rows = []
            for c in range(nch):
                blk = da_parts[c][SUB * i:SUB * (i + 1)]
                rows.append(blk if i == 0 else pltpu.roll(blk, HD - SUB * i, 1))
            x_scr[i] = _mm(jnp.concatenate(rows, axis=0), et_ref[...])
        ti = lax.broadcasted_iota(jnp.int32, (SUB, HD), 0)

        def dbody(c, carry):
            for i in range(NSUB):
                r0 = pl.multiple_of(c * CHUNK + SUB * i, SUB)
                qi = q_scr[pl.ds(r0, SUB), :]
                bi = b_scr[pl.ds(r0, SUB), :]
                dq_hi = jnp.zeros((8, HD), F32)
                dq_lo = jnp.zeros((8, HD), F32)
                dk_hi = jnp.zeros((8, HD), F32)
                dk_lo = jnp.zeros((8, HD), F32)
                c0 = pl.multiple_of(c * SUB, SUB)
                t8 = ti[:8]
                for s in range(SUB):
                    krow = k_scr[pl.ds(r0 + s, 1), :]
                    brow = b_scr[pl.ds(r0 + s, 1), :]
                    w_lo = (x_scr[i, pl.ds(c0 + 8, 8), s * HD:(s + 1) * HD]
                            * jnp.exp(jnp.where(t8 + 8 >= s, bi[8:] - brow, NEG)))
                    dq_lo = dq_lo + w_lo * krow
                    col = _colsum(w_lo * qi[8:])
                    if s < 8:
                        w_hi = (x_scr[i, pl.ds(c0, 8), s * HD:(s + 1) * HD]
                                * jnp.exp(jnp.where(t8 >= s, bi[:8] - brow, NEG)))
                        dq_hi = dq_hi + w_hi * krow
                        dk_hi = jnp.where(t8 == s, col + _colsum(w_hi * qi[:8]), dk_hi)
                    else:
                        dk_lo = jnp.where(t8 + 8 == s, col, dk_lo)
                dqd_scr[pl.ds(r0, SUB), :] = jnp.concatenate([dq_hi, dq_lo], axis=0)
                dkd_scr[pl.ds(r0, SUB), :] = jnp.concatenate([dk_hi, dk_lo], axis=0)
            return carry

        lax.fori_loop(0, nch, dbody, 0)
        qdo = [_mm_tn(do[rs], pre["qe"][rs]) for rs in chunks]
        dsts = [None] * nch
        dst = dst_scr[...]
        for c in reversed(range(nch)):
            dsts[c] = dst
            dst = dst * jnp.exp(b_scr[pl.ds(c * CHUNK + CHUNK - 1, 1), :]) + qdo[c]
        dst_scr[...] = dst
        sts = [s_ref[0, c] for c in range(nch)]
        dqe_parts = [_mm(do[rs], sts[c]) for c, rs in enumerate(chunks)]
        dkdec_parts = [_mm(v[rs], dsts[c]) for c, rs in enumerate(chunks)]
        dvi_parts = [_mm_nt(pre["kd"][rs], dsts[c]) for c, rs in enumerate(chunks)]
        debl_parts = [_colsum(dsts[c] * sts[c]) for c in range(nch)]
        dqe = jnp.concatenate(dqe_parts, axis=0)
        dkdec = jnp.concatenate(dkdec_parts, axis=0)
        dq_tot = jnp.concatenate(dqoff_parts, axis=0) * pre["eqo"] + dqd_scr[...] + dqe * pre["eb"]
        dk_inter = dkdec * pre["ekd"]
        dk_tot = jnp.concatenate(dkoff_parts, axis=0) + dkd_scr[...] + dk_inter
        db = q * dq_tot - k * dk_tot
        kdk = k * dk_inter
        dbl = jnp.concatenate(
            [jnp.broadcast_to(jnp.exp(b_scr[pl.ds(c * CHUNK + CHUNK - 1, 1), :]) * debl_parts[c]
                              + _colsum(kdk[c * CHUNK:(c + 1) * CHUNK]), (CHUNK, HD)) for c in range(nch)], axis=0)
        tl = pre["tl"]
        rc = db
        sh = 1
        while sh < CHUNK:
            rc = rc + jnp.where(tl + sh < CHUNK, pltpu.roll(rc, cb - sh, 0), 0.0)
            sh *= 2
        dlf = rc + dbl
        dfg = dlf / pre["fg"] - dk_tot
        sf = pre["sf"]
        lb = pre["lb"]
        sm_ref[0:1, :] += _colsum(dfg * (1.0 - sf))
        sq = pre["sq"]
        dp_ref[0] = (dq_tot * Q_SCALE * sq * (1.0 + qr * (1.0 - sq))).astype(ACT)
        dp_ref[1] = (dfg * (1.0 - lb) * sf * (1.0 - sf)).astype(ACT)
        dp_ref[2] = (jnp.concatenate(dv_parts, axis=0) + jnp.concatenate(dvi_parts, axis=0)).astype(ACT)
        dp_ref[3] = dog.astype(ACT)

    rev = lambda c: ncb - 1 - c
    out = _gridded(
        body, carry, name="hgrn_bwd", grid=(HEADS, ncb),
        in_specs=[pl.BlockSpec((4, cb, HD), lambda h, c: (0, rev(c), h)),
                  pl.BlockSpec((cb, HD), lambda h, c: (rev(c), h)),
                  pl.BlockSpec((1, cb, HD), lambda h, c: (h, rev(c), 0)),
                  pl.BlockSpec((1, nch, HD, HD), lambda h, c: (h, rev(c), 0, 0)),
                  pl.BlockSpec((cb, HD), lambda h, c: (rev(c), h)),
                  pl.BlockSpec((2, HD), lambda h, c: (0, h)),
                  pl.BlockSpec((1, HD), lambda h, c: (0, h)),
                  pl.BlockSpec((HD, SUB * HD), lambda h, c: (0, 0)),
                  pl.BlockSpec(memory_space=pl.ANY)],
        out_specs=[pl.BlockSpec((4, cb, HD), lambda h, c: (0, rev(c), h)),
                   pl.BlockSpec((8, HD), lambda h, c: (0, h))],
        out_shape=[jax.ShapeDtypeStruct(dp.shape, dp.dtype), jax.ShapeDtypeStruct((8, D), F32)],
        aliases={8: 0},
        scratch_shapes=[pltpu.VMEM((HD, HD), F32), pltpu.VMEM((cb, HD), F32), pltpu.VMEM((cb, HD), F32),
                        pltpu.VMEM((cb, HD), F32), pltpu.VMEM((NSUB, nch * SUB, SUB * HD), F32),
                        pltpu.VMEM((cb, HD), F32), pltpu.VMEM((cb, HD), F32)],
    )(p, o, a_all, s_all, doa, hgrn_lb, hgrn_g, et_mat, dp)
    return out[:2], out[2:]


def _ln_fwd(u1, g, b):
    mu = _rowmean(u1)
    xc = u1 - mu
    rs = lax.rsqrt(_rowmean(xc * xc) + EPS)
    xh = xc * rs
    return xh * g + b, xh, rs


CONV_RB = 64
LANES = 128


def _shift_rows(src, sh, ls, n):
    for r in range(1, 8):
        sh[r - 1, 0:n, :] = src[pl.ds(r, n), ls]


def _tap(src, sh, ls, off, r0, rows):
    r = off % 8
    if r == 0:
        return src[pl.ds(r0 + off, rows), ls]
    return sh[r - 1, pl.ds(r0 + off - r, rows), :]


def _conv_fwd(p, cw, cb_, lng, lnb):
    T = p.shape[1]
    tm = min(512, T)
    n = HALO + tm - 8

    def body(p_ref, cw_ref, cb_ref, g_ref, b_ref, u1_ref, u2_ref, buf, sh):
        @pl.when(pl.program_id(0) == 0)
        def _():
            buf[0:HALO, :] = jnp.zeros((HALO, D), F32)

        buf[HALO:HALO + tm, :] = p_ref[0] * _sig(p_ref[1])
        for lb in range(D // LANES):
            ls = slice(lb * LANES, (lb + 1) * LANES)
            _shift_rows(buf, sh, ls, n)
            taps = [cw_ref[j:j + 1, ls] for j in range(CONV_K)]
            bias = cb_ref[:, ls]

            def rows_body(rb, carry):
                r0 = pl.multiple_of(rb * CONV_RB, CONV_RB)
                acc = jnp.broadcast_to(bias, (CONV_RB, LANES))
                for j in range(CONV_K):
                    acc = acc + taps[j] * _tap(buf, sh, ls, HALO - (CONV_K - 1) + j, r0, CONV_RB)
                u1_ref[pl.ds(r0, CONV_RB), ls] = acc
                return carry

            lax.fori_loop(0, tm // CONV_RB, rows_body, 0)
        y, _, _ = _ln_fwd(u1_ref[...], g_ref[...], b_ref[...])
        u2_ref[...] = (y * _sig(y)).astype(ACT)
        buf[0:HALO, :] = buf[tm:tm + HALO, :]

    return pl.pallas_call(
        body, name="conv_fwd", grid=(T // tm,),
        in_specs=[pl.BlockSpec((2, tm, D), lambda i: (2, i, 0)), pl.BlockSpec((HALO, D), lambda i: (0, 0)),
                  pl.BlockSpec((1, D), lambda i: (0, 0)), pl.BlockSpec((1, D), lambda i: (0, 0)),
                  pl.BlockSpec((1, D), lambda i: (0, 0))],
        out_specs=[pl.BlockSpec((tm, D), lambda i: (i, 0)), pl.BlockSpec((tm, D), lambda i: (i, 0))],
        out_shape=[jax.ShapeDtypeStruct((T, D), F32), jax.ShapeDtypeStruct((T, D), ACT)],
        scratch_shapes=[pltpu.VMEM((HALO + tm, D), F32), pltpu.VMEM((7, n, LANES), F32)],
        compiler_params=_cparams(1),
    )(p, cw, cb_, lng, lnb)


def _conv_bwd(p, u1, du2, cw, lng, lnb, dp):
    T = p.shape[1]
    tm = min(512, T)
    ni = T // tm
    hb = tm // HALO

    n = HALO + tm - 8

    def body(p_ref, ph_ref, u1_ref, du2_ref, cw_ref, g_ref, b_ref, dp_in, dp_ref, dcw_ref, sm_ref, ubuf, dbuf,
             sh, dacc):
        del dp_in
        step = pl.program_id(0)

        @pl.when(step == 0)
        def _():
            dbuf[tm:tm + HALO, :] = jnp.zeros((HALO, D), F32)
            dcw_ref[...] = jnp.zeros_like(dcw_ref)
            sm_ref[...] = jnp.zeros_like(sm_ref)

        ua = p_ref[0]
        sgb = _sig(p_ref[1])
        halo = ph_ref[0] * _sig(ph_ref[1])
        ubuf[0:HALO, :] = jnp.where(step == ni - 1, 0.0, halo)
        ubuf[HALO:HALO + tm, :] = ua * sgb
        g = g_ref[...]
        y, xh, rs = _ln_fwd(u1_ref[...], g, b_ref[...])
        sy = _sig(y)
        dy = du2_ref[...] * sy * (1.0 + y * (1.0 - sy))
        sm_ref[1:2, :] += _colsum(dy * xh)
        sm_ref[2:3, :] += _colsum(dy)
        dxh = dy * g
        du1 = rs * (dxh - _rowmean(dxh) - xh * _rowmean(dxh * xh))
        sm_ref[0:1, :] += _colsum(du1)
        dbuf[0:tm, :] = du1
        for lb in range(D // LANES):
            ls = slice(lb * LANES, (lb + 1) * LANES)
            taps = [cw_ref[j:j + 1, ls] for j in range(CONV_K)]
            _shift_rows(dbuf, sh, ls, n)

            def du0_body(rb, carry):
                r0 = pl.multiple_of(rb * CONV_RB, CONV_RB)
                acc = jnp.zeros((CONV_RB, LANES), F32)
                for j in range(CONV_K):
                    acc = acc + taps[j] * _tap(dbuf, sh, ls, CONV_K - 1 - j, r0, CONV_RB)
                dp_ref[0, pl.ds(r0, CONV_RB), ls] = acc.astype(ACT)
                return carry

            lax.fori_loop(0, tm // CONV_RB, du0_body, 0)
            _shift_rows(ubuf, sh, ls, n)
            dacc[...] = jnp.zeros_like(dacc)

            def dcw_body(rb, carry):
                r0 = pl.multiple_of(rb * CONV_RB, CONV_RB)
                d = dbuf[pl.ds(r0, CONV_RB), ls]
                for j in range(CONV_K):
                    prod = d * _tap(ubuf, sh, ls, HALO - (CONV_K - 1) + j, r0, CONV_RB)
                    dacc[8 * j:8 * j + 8, :] += jnp.sum(prod.reshape(CONV_RB // 8, 8, LANES), axis=0)
                return carry

            lax.fori_loop(0, tm // CONV_RB, dcw_body, 0)
            for j in range(CONV_K):
                dcw_ref[j:j + 1, ls] += _colsum(dacc[8 * j:8 * j + 8, :])
        du0 = dp_ref[0].astype(F32)
        dp_ref[0] = (du0 * sgb).astype(ACT)
        dp_ref[1] = (du0 * ua * sgb * (1.0 - sgb)).astype(ACT)
        dbuf[tm:tm + HALO, :] = dbuf[0:HALO, :]

    rev = lambda i: ni - 1 - i
    return pl.pallas_call(
        body, name="conv_bwd", grid=(ni,),
        in_specs=[pl.BlockSpec((2, tm, D), lambda i: (2, rev(i), 0)),
                  pl.BlockSpec((2, HALO, D), lambda i: (2, jnp.maximum(rev(i) * hb - 1, 0), 0)),
                  pl.BlockSpec((tm, D), lambda i: (rev(i), 0)), pl.BlockSpec((tm, D), lambda i: (rev(i), 0)),
                  pl.BlockSpec((HALO, D), lambda i: (0, 0)), pl.BlockSpec((1, D), lambda i: (0, 0)),
                  pl.BlockSpec((1, D), lambda i: (0, 0)), pl.BlockSpec(memory_space=pl.ANY)],
        out_specs=[pl.BlockSpec((2, tm, D), lambda i: (2, rev(i), 0)),
                   pl.BlockSpec((HALO, D), lambda i: (0, 0)), pl.BlockSpec((8, D), lambda i: (0, 0))],
        out_shape=[jax.ShapeDtypeStruct(dp.shape, dp.dtype), jax.ShapeDtypeStruct((HALO, D), F32),
                   jax.ShapeDtypeStruct((8, D), F32)],
        input_output_aliases={7: 0},
        scratch_shapes=[pltpu.VMEM((HALO + tm, D), F32), pltpu.VMEM((tm + HALO, D), F32),
                        pltpu.VMEM((7, n, LANES), F32), pltpu.VMEM((8 * CONV_K, LANES), F32)],
        compiler_params=_cparams(1),
    )(p, p, u1, du2, cw, lng, lnb, dp)


def _mixout_fwd(x, oa, u2, p, mod, mo, w_a, w_b, w_o):
    T = x.shape[0]
    tm = min(512, T)

    def body(x_ref, oa_ref, u2_ref, p_ref, mod_ref, wa_ref, wb_ref, wo_ref, xo_ref, ya_ref, yb_ref, mo_ref):
        ya = _mm(oa_ref[...], wa_ref[...])
        yb = _mm(u2_ref[...], wb_ref[...])
        ya_ref[...] = ya.astype(ACT)
        yb_ref[...] = yb.astype(ACT)
        merged = _sig(p_ref[0]) * ya + _sig(p_ref[1]) * yb
        out = _mm(merged, wo_ref[...])
        mo_ref[...] = out
        xo_ref[...] = x_ref[...] + mod_ref[mo + 2:mo + 3, :] * out

    tile = pl.BlockSpec((tm, D), lambda i: (i, 0))
    wspec = pl.BlockSpec((D, D), lambda i: (0, 0))
    return pl.pallas_call(
        body, name="mixout_fwd", grid=(T // tm,),
        in_specs=[tile, tile, tile, pl.BlockSpec((2, tm, D), lambda i: (3, i, 0)),
                  pl.BlockSpec((9, D), lambda i: (0, 0)), wspec, wspec, wspec],
        out_specs=[tile, tile, tile, tile],
        out_shape=[jax.ShapeDtypeStruct((T, D), F32), jax.ShapeDtypeStruct((T, D), ACT),
                   jax.ShapeDtypeStruct((T, D), ACT), jax.ShapeDtypeStruct((T, D), F32)],
        compiler_params=_cparams(1),
    )(x, oa, u2, p, mod, w_a, w_b, w_o)


def _mixout_bwd(dxo, oa, u2, ya, yb, mout, p, mod, mo, w_a, w_b, w_o):
    T = dxo.shape[0]
    tm = min(256, T)

    def body(dxo_ref, oa_ref, u2_ref, ya_ref, yb_ref, mo_ref, p_ref, mod_ref, wa_ref, wb_ref, wo_ref,
             dp_ref, doa_ref, du2_ref, dwa_ref, dwb_ref, dwo_ref, sm_ref):
        @pl.when(pl.program_id(0) == 0)
        def _():
            dwa_ref[...] = jnp.zeros_like(dwa_ref)
            dwb_ref[...] = jnp.zeros_like(dwb_ref)
            dwo_ref[...] = jnp.zeros_like(dwo_ref)
            sm_ref[...] = jnp.zeros_like(sm_ref)

        dxo_v = dxo_ref[...]
        sm_ref[2:3, :] += _colsum(dxo_v * mo_ref[...])
        dmo = (mod_ref[mo + 2:mo + 3, :] * dxo_v).astype(MM)
        ya = ya_ref[...].astype(F32)
        yb = yb_ref[...].astype(F32)
        sga = _sig(p_ref[0])
        sgb = _sig(p_ref[1])
        merged = (sga * ya + sgb * yb).astype(MM)
        dwo_ref[...] += _mm_tn(merged, dmo)
        dmg = _mm_nt(dmo, wo_ref[...])
        dp_ref[0] = (dmg * ya * sga * (1.0 - sga)).astype(ACT)
        dp_ref[1] = (dmg * yb * sgb * (1.0 - sgb)).astype(ACT)
        dya = (dmg * sga).astype(MM)
        dyb = (dmg * sgb).astype(MM)
        dwa_ref[...] += _mm_tn(oa_ref[...], dya)
        dwb_ref[...] += _mm_tn(u2_ref[...], dyb)
        doa_ref[...] = _mm_nt(dya, wa_ref[...])
        du2_ref[...] = _mm_nt(dyb, wb_ref[...])

    tile = pl.BlockSpec((tm, D), lambda i: (i, 0))
    wspec = pl.BlockSpec((D, D), lambda i: (0, 0))
    return pl.pallas_call(
        body, name="mixout_bwd", grid=(T // tm,),
        in_specs=[tile, tile, tile, tile, tile, tile, pl.BlockSpec((2, tm, D), lambda i: (3, i, 0)),
                  pl.BlockSpec((9, D), lambda i: (0, 0)), wspec, wspec, wspec],
        out_specs=[pl.BlockSpec((2, tm, D), lambda i: (3, i, 0)), tile, tile, wspec, wspec, wspec,
                   pl.BlockSpec((8, D), lambda i: (0, 0))],
        out_shape=[jax.ShapeDtypeStruct((8, T, D), ACT), jax.ShapeDtypeStruct((T, D), F32),
                   jax.ShapeDtypeStruct((T, D), F32), jax.ShapeDtypeStruct((D, D), F32),
                   jax.ShapeDtypeStruct((D, D), F32), jax.ShapeDtypeStruct((D, D), F32),
                   jax.ShapeDtypeStruct((8, D), F32)],
        compiler_params=_cparams(1),
    )(dxo, oa, u2, ya, yb, mout, p, mod, w_a, w_b, w_o)


def _ada_fwd(cs_all, ada_w, ada_b_cols):
    def body(cs_ref, w_ref, b_ref, out_ref):
        out_ref[...] = jnp.dot(cs_ref[...], w_ref[...], preferred_element_type=F32,
                               precision=lax.Precision.HIGHEST) + b_ref[...]

    return pl.pallas_call(
        body, name="ada_fwd", out_shape=jax.ShapeDtypeStruct((N_DEV, ada_w.shape[1]), F32),
        compiler_params=pltpu.CompilerParams(vmem_limit_bytes=VMEM_LIMIT),
    )(cs_all, ada_w, ada_b_cols)


def _ada_wgrad(cs_all, dmod_cols):
    cs_t = jnp.pad(cs_all.T, ((0, 0), (0, HD - N_DEV)))
    dm = jnp.pad(dmod_cols, ((0, HD - N_DEV), (0, 0)))

    def body(cs_ref, d_ref, out_ref):
        out_ref[...] = jnp.dot(cs_ref[...], d_ref[...], preferred_element_type=F32,
                               precision=lax.Precision.HIGHEST)

    return pl.pallas_call(
        body, name="ada_wgrad", out_shape=jax.ShapeDtypeStruct((D, dmod_cols.shape[1]), F32),
        compiler_params=pltpu.CompilerParams(vmem_limit_bytes=VMEM_LIMIT),
    )(cs_t, dm)


def _adam_math(w, g, m, v):
    m2 = ADAM_B1 * m + (1.0 - ADAM_B1) * g
    v2 = ADAM_B2 * v + (1.0 - ADAM_B2) * (g * g)
    m_hat = m2 / (1.0 - ADAM_B1 ** ADAM_STEP)
    v_hat = v2 / (1.0 - ADAM_B2 ** ADAM_STEP)
    delta = -ADAM_LR * (m_hat / (jnp.sqrt(v_hat) + ADAM_EPS) + ADAM_WD * w)
    return delta, m2, v2


def _adamw(w, m, v, g, name):
    R, C = w.shape
    slots = g.ndim == 3
    tr = R
    for cand in (256, 176):
        if R % cand == 0 and R > cand:
            tr = cand
            break

    def body(w_ref, m_ref, v_ref, g_ref, go_ref, d_ref, mo_ref, vo_ref):
        if slots:
            gv = g_ref[0].astype(F32)
            for s in range(1, N_DEV):
                gv = gv + g_ref[s].astype(F32)
        else:
            gv = g_ref[...]
        go_ref[...] = gv
        d_ref[...], mo_ref[...], vo_ref[...] = _adam_math(w_ref[...], gv, m_ref[...], v_ref[...])

    tile = pl.BlockSpec((tr, C), lambda i: (i, 0))
    gspec = pl.BlockSpec((N_DEV, tr, C), lambda i: (0, i, 0)) if slots else tile
    sds = jax.ShapeDtypeStruct((R, C), F32)
    return pl.pallas_call(
        body, name=name, grid=(R // tr,), in_specs=[tile, tile, tile, gspec], out_specs=[tile] * 4,
        out_shape=[sds] * 4, compiler_params=_cparams(1),
    )(w, m, v, g)


def _sum_slots(pack):
    def body(p_ref, out_ref):
        acc = p_ref[0]
        for s in range(1, N_DEV):
            acc = acc + p_ref[s]
        out_ref[...] = acc

    return pl.pallas_call(body, name="sum_small", out_shape=jax.ShapeDtypeStruct(pack.shape[1:], F32))(pack)


def _me():
    return lax.axis_index("x"), lax.axis_index("y"), lax.axis_index("c")


def _peer(r):
    x, y, c = _me()
    px = 1 - x if r & 4 else x
    py = 1 - y if r & 2 else y
    pc = 1 - c if r & 1 else c
    return (px, py, pc), 4 * px + 2 * py + pc


def _allgather_small(x):
    R, C = x.shape

    def body(x_ref, out_ref, send_sems, recv_sems):
        mx, my, mc = _me()
        me = 4 * mx + 2 * my + mc
        mine = out_ref.at[pl.ds(pl.multiple_of(me * R, 8), R), :]
        copies = []
        for r in range(1, N_DEV):
            dev, _ = _peer(r)
            copies.append(pltpu.make_async_remote_copy(
                src_ref=x_ref, dst_ref=mine, send_sem=send_sems.at[r - 1], recv_sem=recv_sems.at[r - 1],
                device_id=dev, device_id_type=MESH))
        for cp in copies:
            cp.start()
        mine[...] = x_ref[...]
        for r in range(1, N_DEV):
            _, idx = _peer(r)
            theirs = out_ref.at[pl.ds(pl.multiple_of(idx * R, 8), R), :]
            pltpu.make_async_remote_copy(
                src_ref=x_ref, dst_ref=theirs, send_sem=send_sems.at[r - 1], recv_sem=recv_sems.at[r - 1],
                device_id=_peer(r)[0], device_id_type=MESH).wait_recv()
        for cp in copies:
            cp.wait_send()

    return pl.pallas_call(
        body, name="allgather_small_%dx%d" % (R, C),
        out_shape=jax.ShapeDtypeStruct((N_DEV * R, C), F32),
        in_specs=[pl.BlockSpec(memory_space=pltpu.VMEM)], out_specs=pl.BlockSpec(memory_space=pltpu.VMEM),
        scratch_shapes=[pltpu.SemaphoreType.DMA((N_DEV - 1,)), pltpu.SemaphoreType.DMA((N_DEV - 1,))],
    )(x)


def _xchg_copies(ins, outs, sems, gather):
    send_sems, recv_sems, local_sems = sems
    mx, my, mc = _me()
    me = 4 * mx + 2 * my + mc
    sibling = _peer(1)[0]

    def rdma(a, r, dev, src, slot):
        k = a * (N_DEV - 1) + r - 1
        return pltpu.make_async_remote_copy(
            src_ref=src, dst_ref=outs[a].at[slot], send_sem=send_sems.at[k], recv_sem=recv_sems.at[k],
            device_id=dev, device_id_type=MESH)

    own, sends, relays, recvs = [], [], [], []
    for a in range(len(ins)):
        own.append(pltpu.make_async_copy(ins[a] if gather else ins[a].at[me], outs[a].at[me], local_sems.at[a]))
        for r in range(1, N_DEV):
            dev, idx = _peer(r)
            if not gather:
                sends.append(rdma(a, r, dev, ins[a].at[idx], me))
                recvs.append(rdma(a, r, dev, ins[a].at[idx], idx))
            elif r == 1:
                sends.append(rdma(a, r, dev, ins[a], me))
                recvs.append(rdma(a, r, dev, ins[a], idx))
            elif r % 2 == 0:
                sends.append(rdma(a, r, dev, ins[a], me))
                relays.append((rdma(a, r, dev, ins[a], idx), rdma(a, r + 1, sibling, outs[a].at[idx], idx)))
            else:
                recvs.append(rdma(a, r, sibling, ins[a], idx))
    return own, sends, relays, recvs


def _xchg_start(ins, outs, sems, gather):
    own, sends, _, _ = _xchg_copies(ins, outs, sems, gather)
    for cp in own + sends:
        cp.start()


def _xchg_wait(ins, outs, sems, gather):
    own, sends, relays, recvs = _xchg_copies(ins, outs, sems, gather)
    for arrival, relay in relays:
        arrival.wait_recv()
        relay.start()
    for cp in recvs:
        cp.wait_recv()
    for cp in own:
        cp.wait()
    for cp in sends + [relay for _, relay in relays]:
        cp.wait_send()


def _xchg_specs(arrays, gather):
    n = len(arrays)
    out_shape = [jax.ShapeDtypeStruct(((N_DEV,) + a.shape) if gather else a.shape, a.dtype) for a in arrays]
    sems = [pltpu.SemaphoreType.DMA((n * (N_DEV - 1),)), pltpu.SemaphoreType.DMA((n * (N_DEV - 1),)),
            pltpu.SemaphoreType.DMA((n,))]
    return out_shape, sems


def _exchange(arrays, gather, name):
    n = len(arrays)

    def body(*refs):
        _xchg_start(refs[:n], refs[n:2 * n], refs[2 * n:], gather)
        _xchg_wait(refs[:n], refs[n:2 * n], refs[2 * n:], gather)

    out_shape, sems = _xchg_specs(arrays, gather)
    return pl.pallas_call(
        body, name=name, out_shape=out_shape,
        in_specs=[pl.BlockSpec(memory_space=pl.ANY)] * n, out_specs=[pl.BlockSpec(memory_space=pl.ANY)] * n,
        scratch_shapes=sems,
    )(*arrays)


def _gridded(body, carry, *, name, grid, in_specs, out_specs, out_shape, scratch_shapes=(), aliases=None):
    if carry is None:
        return pl.pallas_call(
            body, name=name, grid=grid, in_specs=list(in_specs), out_specs=list(out_specs),
            out_shape=list(out_shape), scratch_shapes=list(scratch_shapes), input_output_aliases=aliases or {},
            compiler_params=_cparams(len(grid)))
    arrays, gather = carry
    n, n_in, n_out, n_scr = len(arrays), len(in_specs), len(out_specs), len(scratch_shapes)
    c_shape, c_sems = _xchg_specs(arrays, gather)

    def wrapped(*refs):
        ins, cin = refs[:n_in], refs[n_in:n_in + n]
        o0 = n_in + n
        outs, cout = refs[o0:o0 + n_out], refs[o0 + n_out:o0 + n_out + n]
        s0 = o0 + n_out + n
        scr, sems = refs[s0:s0 + n_scr], refs[s0 + n_scr:]
        first = pl.program_id(0) == 0
        last = pl.program_id(0) == grid[0] - 1
        for ax in range(1, len(grid)):
            first = first & (pl.program_id(ax) == 0)
            last = last & (pl.program_id(ax) == grid[ax] - 1)

        @pl.when(first)
        def _():
            _xchg_start(cin, cout, sems, gather)

        body(*ins, *outs, *scr)

        @pl.when(last)
        def _():
            _xchg_wait(cin, cout, sems, gather)

    hbm = pl.BlockSpec(memory_space=pl.ANY)
    res = pl.pallas_call(
        wrapped, name=name, grid=grid, in_specs=list(in_specs) + [hbm] * n, out_specs=list(out_specs) + [hbm] * n,
        out_shape=list(out_shape) + c_shape, scratch_shapes=list(scratch_shapes) + c_sems,
        input_output_aliases=aliases or {}, compiler_params=_cparams(len(grid)),
    )
    return lambda *args: res(*args, *arrays)


def _local_step(x, target, mod, small, sh):
    w1_in, w1_out = _exchange([sh["ffn1_w_in"], sh["ffn1_w_out"]], True, "allgather_ffn1")
    w1_in, w1_out = _full_w_in(w1_in), w1_out.reshape(D_FF, D)
    (x1, a1, b1, f1, h1), (wm_in,) = _ffn_fwd(x, mod, 0, small["norm_ffn1"], w1_in, w1_out, 0.5, "ffn1_fwd",
                                              ([sh["mix_w_in"]], True))
    (p, h2), (wh_o, wc_o, wm_o, cw) = _mixin_fwd(
        x1, mod, 3, small["norm_mix"], wm_in,
        ([sh["hgrn_w_o"], sh["conv_w_o"], sh["mix_w_out"], sh["conv_w"]], True))
    wh_o, wc_o, wm_o = wh_o.reshape(D, D), wc_o.reshape(D, D), wm_o.reshape(D, D)
    cw = jnp.pad(cw.transpose(1, 0, 2).reshape(CONV_K, D), ((0, HALO - CONV_K), (0, 0)))
    (o, oa, a_all, s_all), (w2_in, w2_out) = _hgrn_fwd(p, small["hgrn_lb"], small["hgrn_g"],
                                                       ([sh["ffn2_w_in"], sh["ffn2_w_out"]], True))
    w2_in, w2_out = _full_w_in(w2_in), w2_out.reshape(D_FF, D)
    u1, u2 = _conv_fwd(p, cw, small["conv_b"], small["conv_ln_g"], small["conv_ln_b"])
    x2, ya, yb, mout = _mixout_fwd(x1, oa, u2, p, mod, 3, wh_o, wc_o, wm_o)
    (x3, a3, b3, f3, h3), _ = _ffn_fwd(x2, mod, 6, small["norm_ffn2"], w2_in, w2_out, 0.5, "ffn2_fwd", None)
    dx3, sm_head = _head(x3, target, small["norm_final"])

    (dx2, dw2_a, dw2_b, dw2_out, sm3), _ = _ffn_bwd(x2, h3, dx3, f3, a3, b3, mod, 6, small["norm_ffn2"], w2_in,
                                                    w2_out, 0.5, "ffn2_bwd", None)
    dp, doa, du2, dwh_o, dwc_o, dwm_o, sm_mo = _mixout_bwd(dx2, oa, u2, ya, yb, mout, p, mod, 3, wh_o, wc_o, wm_o)
    dp, dcw, sm_cv = _conv_bwd(p, u1, du2, cw, small["conv_ln_g"], small["conv_ln_b"], dp)
    rows = lambda t: t.reshape(N_DEV, -1, D).astype(MM)
    (dp, sm_hg), (r2_in, r2_out) = _hgrn_bwd(p, o, a_all, s_all, doa, small["hgrn_lb"], small["hgrn_g"], dp,
                                             ([_w_in_shards(dw2_a, dw2_b), rows(dw2_out)], False))
    (dx1, dwm_in, sm2), (rh_o, rc_o, rm_o, rcw) = _mixin_bwd(
        x1, h2, dx2, dp, mod, 3, small["norm_mix"], wm_in,
        ([rows(dwh_o), rows(dwc_o), rows(dwm_o), dcw[:CONV_K].reshape(CONV_K, N_DEV, -1).transpose(1, 0, 2)], False))
    (dx0, dw1_a, dw1_b, dw1_out, sm1), (rm_in,) = _ffn_bwd(x, h1, dx1, f1, a1, b1, mod, 0, small["norm_ffn1"],
                                                          w1_in, w1_out, 0.5, "ffn1_bwd",
                                                          ([dwm_in], False))
    r1_in, r1_out = _exchange([_w_in_shards(dw1_a, dw1_b), rows(dw1_out)], False, "scatter_ffn1")

    dmod = jnp.concatenate([sm1[0:3], sm2[0:2], sm_mo[2:3], sm3[0:3]], axis=0)
    gsmall = dict(norm_ffn1=sm1[3:4], norm_mix=sm2[3:4], lb0=sm_hg[0:1], hgrn_g=sm_hg[1:2], conv_b=sm_cv[0:1],
                  conv_ln_g=sm_cv[1:2], conv_ln_b=sm_cv[2:3], norm_ffn2=sm3[3:4], norm_final=sm_head[0:1])
    recv = dict(ffn1_w_in=r1_in, ffn1_w_out=r1_out, mix_w_in=rm_in, hgrn_w_o=rh_o, conv_w=rcw, conv_w_o=rc_o,
                mix_w_out=rm_o, ffn2_w_in=r2_in, ffn2_w_out=r2_out)
    return sm_head[1, 0], dx0, dmod, gsmall, recv


def _full_w_in(g):
    return g.transpose(1, 0, 2).reshape(D, -1)


def _w_in_shards(dwa, dwb):
    half = N_DEV // 2
    return jnp.concatenate([t.reshape(D, half, -1).transpose(1, 0, 2) for t in (dwa, dwb)], axis=0)


SMALL_ORDER = ("norm_ffn1", "norm_mix", "lb0", "hgrn_g", "conv_b", "conv_ln_g", "conv_ln_b", "norm_ffn2",
               "norm_final")
PACK_ROWS = 24


def kernel(x, c, ada_w, ada_b, norm_ffn1, ffn1_w_in, ffn1_w_out, norm_mix, mix_w_in, hgrn_lb, hgrn_g, hgrn_w_o, conv_w, conv_b, conv_ln_g, conv_ln_b, conv_w_o, mix_w_out, norm_ffn2, ffn2_w_in, ffn2_w_out, norm_final, loss_target, m_ada_w, m_ada_b, m_norm_ffn1, m_ffn1_w_in, m_ffn1_w_out, m_norm_mix, m_mix_w_in, m_hgrn_lb, m_hgrn_g, m_hgrn_w_o, m_conv_w, m_conv_b, m_conv_ln_g, m_conv_ln_b, m_conv_w_o, m_mix_w_out, m_norm_ffn2, m_ffn2_w_in, m_ffn2_w_out, m_norm_final, v_ada_w, v_ada_b, v_norm_ffn1, v_ffn1_w_in, v_ffn1_w_out, v_norm_mix, v_mix_w_in, v_hgrn_lb, v_hgrn_g, v_hgrn_w_o, v_conv_w, v_conv_b, v_conv_ln_g, v_conv_ln_b, v_conv_w_o, v_mix_w_out, v_norm_ffn2, v_ffn2_w_in, v_ffn2_w_out, v_norm_final):
    mx, my, mc = _me()
    me = 4 * mx + 2 * my + mc
    ncol = ada_w.shape[2]

    cs = jnp.broadcast_to(c * jax.nn.sigmoid(c), (8, D))
    cs_all = _allgather_small(cs).reshape(N_DEV, 8, D)[:, 0, :]
    ada_b_cols = lax.dynamic_slice(ada_b, (0, me * ncol), (1, ncol))
    mod_cols = _ada_fwd(cs_all, ada_w[0], ada_b_cols)
    mod_all = _allgather_small(mod_cols).reshape(N_DEV, N_DEV, ncol)
    mod = lax.dynamic_index_in_dim(mod_all, me, axis=1, keepdims=False).reshape(9, D)

    sh = dict(ffn1_w_in=ffn1_w_in, ffn1_w_out=ffn1_w_out, mix_w_in=mix_w_in, hgrn_w_o=hgrn_w_o,
              conv_w_o=conv_w_o, mix_w_out=mix_w_out, ffn2_w_in=ffn2_w_in, ffn2_w_out=ffn2_w_out)
    sh = {n: w[0].astype(MM) for n, w in sh.items()}
    sh["conv_w"] = conv_w[0]
    small = dict(norm_ffn1=norm_ffn1, norm_mix=norm_mix, hgrn_lb=hgrn_lb, hgrn_g=hgrn_g, conv_b=conv_b,
                 conv_ln_g=conv_ln_g, conv_ln_b=conv_ln_b, norm_ffn2=norm_ffn2, norm_final=norm_final.reshape(1, D))

    loss_local, dx, dmod, gsmall, recv = _local_step(x[0], loss_target[0], mod, small, sh)
    loss = lax.psum(loss_local, ("x", "y", "c"))

    pack = jnp.concatenate([dmod] + [gsmall[n] for n in SMALL_ORDER]
                           + [jnp.zeros((PACK_ROWS - 9 - len(SMALL_ORDER), D), F32)], axis=0)
    pack_all = _allgather_small(pack).reshape(N_DEV, PACK_ROWS, D)
    tot = _sum_slots(pack_all)
    gs = {n: tot[9 + i:10 + i] for i, n in enumerate(SMALL_ORDER)}
    dmod_all = pack_all[:, 0:9, :].reshape(N_DEV, 9 * D)
    g_ada_b = tot[0:9].reshape(1, 9 * D)
    g_ada_w = _ada_wgrad(cs_all, lax.dynamic_slice(dmod_all, (0, me * ncol), (N_DEV, ncol)))
    z = hgrn_lb.astype(F32)
    p0 = jax.nn.sigmoid(z[0:1] - z[1:2])
    dz0 = p0 * (1.0 - p0) * gs["lb0"]
    g_hgrn_lb = jnp.concatenate([dz0, -dz0], axis=0)

    res = {}
    res["ada_w"] = _adamw(ada_w[0], m_ada_w[0], v_ada_w[0], g_ada_w, "adamw_ada_w")
    big = dict(ffn1_w_in=(ffn1_w_in, m_ffn1_w_in, v_ffn1_w_in), ffn1_w_out=(ffn1_w_out, m_ffn1_w_out, v_ffn1_w_out),
               mix_w_in=(mix_w_in, m_mix_w_in, v_mix_w_in), hgrn_w_o=(hgrn_w_o, m_hgrn_w_o, v_hgrn_w_o),
               conv_w=(conv_w, m_conv_w, v_conv_w), conv_w_o=(conv_w_o, m_conv_w_o, v_conv_w_o),
               mix_w_out=(mix_w_out, m_mix_w_out, v_mix_w_out), ffn2_w_in=(ffn2_w_in, m_ffn2_w_in, v_ffn2_w_in),
               ffn2_w_out=(ffn2_w_out, m_ffn2_w_out, v_ffn2_w_out))
    for n, (w, m, v) in big.items():
        res[n] = _adamw(w[0], m[0], v[0], recv[n], "adamw_" + n)
    sm_names = ("ada_b", "norm_ffn1", "norm_mix", "hgrn_lb", "hgrn_g", "conv_b", "conv_ln_g", "conv_ln_b",
                "norm_ffn2", "norm_final")
    sm_w = dict(ada_b=(ada_b, m_ada_b, v_ada_b), norm_ffn1=(norm_ffn1, m_norm_ffn1, v_norm_ffn1),
                norm_mix=(norm_mix, m_norm_mix, v_norm_mix), hgrn_lb=(hgrn_lb, m_hgrn_lb, v_hgrn_lb),
                hgrn_g=(hgrn_g, m_hgrn_g, v_hgrn_g), conv_b=(conv_b, m_conv_b, v_conv_b),
                conv_ln_g=(conv_ln_g, m_conv_ln_g, v_conv_ln_g), conv_ln_b=(conv_ln_b, m_conv_ln_b, v_conv_ln_b),
                norm_ffn2=(norm_ffn2, m_norm_ffn2, v_norm_ffn2), norm_final=(norm_final, m_norm_final, v_norm_final))
    sm_g = dict(gs, ada_b=g_ada_b, hgrn_lb=g_hgrn_lb)
    rows = {n: sm_w[n][0].size // D for n in sm_names}
    n_rows = sum(rows.values())
    pad = (-n_rows) % 8
    stack = lambda parts: jnp.concatenate([q.reshape(-1, D) for q in parts] + [jnp.ones((pad, D), F32)], axis=0)
    st = _adamw(stack([sm_w[n][0] for n in sm_names]), stack([sm_w[n][1] for n in sm_names]),
                stack([sm_w[n][2] for n in sm_names]), stack([sm_g[n] for n in sm_names]), "adamw_small")
    off = 0
    for n in sm_names:
        res[n] = tuple(t[off:off + rows[n]].reshape(sm_w[n][0].shape) for t in st)
        off += rows[n]

    order = ("ada_w", "ada_b", "norm_ffn1", "ffn1_w_in", "ffn1_w_out", "norm_mix", "mix_w_in", "hgrn_lb", "hgrn_g",
             "hgrn_w_o", "conv_w", "conv_b", "conv_ln_g", "conv_ln_b", "conv_w_o", "mix_w_out", "norm_ffn2",
             "ffn2_w_in", "ffn2_w_out", "norm_final")
    lead = lambda n, t: t[None] if n in big or n == "ada_w" else t
    outs = [loss, dx[None]]
    for j in range(4):
        outs += [lead(n, res[n][j]) for n in order]
    return tuple(outs)
```

```python
import functools

import jax
import jax.numpy as jnp
from jax import lax
from jax.experimental import pallas as pl
from jax.experimental.pallas import tpu as pltpu

F32 = jnp.float32
MM = jnp.bfloat16
ACT = jnp.bfloat16

D = 1024
D_FF = 2816
HEADS = 8
HD = 128
CHUNK = 64
SUB = 16
NSUB = CHUNK // SUB
CONV_K = 31
HALO = 32
EPS = 1e-6
N_DEV = 8
NEG = -1e30
Q_SCALE = HD ** -0.5

ADAM_LR = 0.001
ADAM_B1 = 0.9
ADAM_B2 = 0.999
ADAM_EPS = 1e-08
ADAM_WD = 0.01
ADAM_STEP = 10

VMEM_LIMIT = 60 * 1024 * 1024
MESH = pl.DeviceIdType.MESH


def _cparams(n_axes):
    return pltpu.CompilerParams(dimension_semantics=("arbitrary",) * n_axes, vmem_limit_bytes=VMEM_LIMIT)


def _mm(a, b):
    return lax.dot_general(a.astype(MM), b.astype(MM), (((1,), (0,)), ((), ())), preferred_element_type=F32)


def _mm_nt(a, b):
    return lax.dot_general(a.astype(MM), b.astype(MM), (((1,), (1,)), ((), ())), preferred_element_type=F32)


def _mm_tn(a, b):
    return lax.dot_general(a.astype(MM), b.astype(MM), (((0,), (0,)), ((), ())), preferred_element_type=F32)


def _sig(x):
    return 1.0 / (1.0 + jnp.exp(-x))


def _colsum(x):
    return jnp.sum(x, axis=0, keepdims=True)


def _rowmean(x):
    return jnp.mean(x, axis=-1, keepdims=True)


def _modnorm_fwd(xv, g, sh, sc):
    r = lax.rsqrt(_rowmean(xv * xv) + EPS)
    xh = xv * r
    n = xh * g
    return n * (1.0 + sc) + sh, xh, n, r


def _modnorm_bwd(dh, xh, n, r, g, sc):
    dsc = _colsum(dh * n)
    dsh = _colsum(dh)
    dn = dh * (1.0 + sc)
    dg = _colsum(dn * xh)
    dxh = dn * g
    dx = r * (dxh - xh * _rowmean(dxh * xh))
    return dx, dsh, dsc, dg


def _ffn_fwd(x, mod, mo, gnorm, w_in, w_out, res, name, carry):
    T = x.shape[0]
    tm = min(512, T)
    tn = D_FF // 2
    nj = D_FF // tn

    def body(x_ref, mod_ref, g_ref, wa_ref, wb_ref, wo_ref, xo_ref, a_ref, b_ref, f_ref, h_ref, acc_scr):
        j = pl.program_id(1)

        @pl.when(j == 0)
        def _():
            h, _, _, _ = _modnorm_fwd(x_ref[...], g_ref[...], mod_ref[mo:mo + 1, :], mod_ref[mo + 1:mo + 2, :])
            h_ref[...] = h.astype(ACT)
            acc_scr[...] = jnp.zeros_like(acc_scr)

        h = h_ref[...]
        a = _mm(h, wa_ref[...])
        b = _mm(h, wb_ref[...])
        a_ref[...] = a.astype(ACT)
        b_ref[...] = b.astype(ACT)
        s = a * _sig(a) * b
        acc_scr[...] += _mm(s, wo_ref[...])

        @pl.when(j == nj - 1)
        def _():
            f = acc_scr[...]
            f_ref[...] = f
            xo_ref[...] = x_ref[...] + res * mod_ref[mo + 2:mo + 3, :] * f

    out = _gridded(
        body, carry, name=name, grid=(T // tm, nj),
        in_specs=[
            pl.BlockSpec((tm, D), lambda i, j: (i, 0)),
            pl.BlockSpec((9, D), lambda i, j: (0, 0)),
            pl.BlockSpec((1, D), lambda i, j: (0, 0)),
            pl.BlockSpec((D, tn), lambda i, j: (0, j)),
            pl.BlockSpec((D, tn), lambda i, j: (0, j + nj)),
            pl.BlockSpec((tn, D), lambda i, j: (j, 0)),
        ],
        out_specs=[
            pl.BlockSpec((tm, D), lambda i, j: (i, 0)),
            pl.BlockSpec((tm, tn), lambda i, j: (i, j)),
            pl.BlockSpec((tm, tn), lambda i, j: (i, j)),
            pl.BlockSpec((tm, D), lambda i, j: (i, 0)),
            pl.BlockSpec((tm, D), lambda i, j: (i, 0)),
        ],
        out_shape=[
            jax.ShapeDtypeStruct((T, D), F32),
            jax.ShapeDtypeStruct((T, D_FF), ACT),
            jax.ShapeDtypeStruct((T, D_FF), ACT),
            jax.ShapeDtypeStruct((T, D), F32),
            jax.ShapeDtypeStruct((T, D), ACT),
        ],
        scratch_shapes=[pltpu.VMEM((tm, D), F32)],
    )(x, mod, gnorm, w_in, w_in, w_out)
    return out[:5], out[5:]


def _ffn_bwd_w(h, dxo, a, b, mod, mo, w_out, res, name, carry):
    T = h.shape[0]
    tm = min(1024, T)
    ni = T // tm
    tn = 256
    nj = D_FF // tn

    def body(h_ref, dxo_ref, a_ref, b_ref, mod_ref, wo_ref, da_ref, db_ref, dwa_ref, dwb_ref, dwo_ref,
             acc_a, acc_b, acc_o, df_all, h_all):
        j = pl.program_id(0)
        i = pl.program_id(1)

        @pl.when(i == 0)
        def _():
            acc_a[...] = jnp.zeros_like(acc_a)
            acc_b[...] = jnp.zeros_like(acc_b)
            acc_o[...] = jnp.zeros_like(acc_o)

        @pl.when(j == 0)
        def _():
            df_all[i] = (res * mod_ref[mo + 2:mo + 3, :] * dxo_ref[...]).astype(MM)
            h_all[i] = h_ref[...]

        hb = h_all[i]
        df = df_all[i]
        av = a_ref[...].astype(F32)
        bv = b_ref[...].astype(F32)
        sg = _sig(av)
        sa = av * sg
        s = (sa * bv).astype(MM)
        ds = _mm_nt(df, wo_ref[...])
        da = (ds * bv * sg * (1.0 + av * (1.0 - sg))).astype(MM)
        db = (ds * sa).astype(MM)
        da_ref[...] = da
        db_ref[...] = db
        acc_o[...] += _mm_tn(s, df)
        acc_a[...] += _mm_tn(hb, da)
        acc_b[...] += _mm_tn(hb, db)

        @pl.when(i == ni - 1)
        def _():
            dwa_ref[...] = acc_a[...].astype(MM)
            dwb_ref[...] = acc_b[...].astype(MM)
            dwo_ref[...] = acc_o[...].astype(MM)

    first = lambda j, i: (jnp.where(j == 0, i, ni - 1), 0)
    out = _gridded(
        body, carry, name=name, grid=(nj, ni),
        in_specs=[
            pl.BlockSpec((tm, D), first),
            pl.BlockSpec((tm, D), first),
            pl.BlockSpec((tm, tn), lambda j, i: (i, j)),
            pl.BlockSpec((tm, tn), lambda j, i: (i, j)),
            pl.BlockSpec((9, D), lambda j, i: (0, 0)),
            pl.BlockSpec((tn, D), lambda j, i: (j, 0)),
        ],
        out_specs=[
            pl.BlockSpec((tm, tn), lambda j, i: (i, j)),
            pl.BlockSpec((tm, tn), lambda j, i: (i, j)),
            pl.BlockSpec((D, tn), lambda j, i: (0, j)),
            pl.BlockSpec((D, tn), lambda j, i: (0, j)),
            pl.BlockSpec((tn, D), lambda j, i: (j, 0)),
        ],
        out_shape=[
            jax.ShapeDtypeStruct((T, D_FF), MM),
            jax.ShapeDtypeStruct((T, D_FF), MM),
            jax.ShapeDtypeStruct((D, D_FF), MM),
            jax.ShapeDtypeStruct((D, D_FF), MM),
            jax.ShapeDtypeStruct((D_FF, D), MM),
        ],
        scratch_shapes=[pltpu.VMEM((D, tn), F32), pltpu.VMEM((D, tn), F32), pltpu.VMEM((tn, D), F32),
                        pltpu.VMEM((ni, tm, D), MM), pltpu.VMEM((ni, tm, D), MM)],
    )(h, dxo, a, b, mod, w_out)
    return out[:5], out[5:]


def _ffn_bwd_x(x, dxo, f, da, db, mod, mo, gnorm, w_in, res, name, carry):
    T = x.shape[0]
    tm = min(512, T)
    tn = D_FF // 2
    nj = D_FF // tn

    def body(x_ref, dxo_ref, f_ref, da_ref, db_ref, mod_ref, g_ref, wa_ref, wb_ref, dx_ref, sm_ref, dh_scr):
        i = pl.program_id(0)
        j = pl.program_id(1)

        @pl.when((j == 0) & (i == 0))
        def _():
            sm_ref[...] = jnp.zeros_like(sm_ref)

        @pl.when(j == 0)
        def _():
            dh_scr[...] = jnp.zeros_like(dh_scr)

        dh_scr[...] += _mm_nt(da_ref[...], wa_ref[...]) + _mm_nt(db_ref[...], wb_ref[...])

        @pl.when(j == nj - 1)
        def _():
            sc = mod_ref[mo + 1:mo + 2, :]
            _, xh, n, r = _modnorm_fwd(x_ref[...], g_ref[...], mod_ref[mo:mo + 1, :], sc)
            dxn, dsh, dsc, dg = _modnorm_bwd(dh_scr[...], xh, n, r, g_ref[...], sc)
            dxo_v = dxo_ref[...]
            dx_ref[...] = dxo_v + dxn
            sm_ref[0:1, :] += dsh
            sm_ref[1:2, :] += dsc
            sm_ref[2:3, :] += _colsum(dxo_v * f_ref[...]) * res
            sm_ref[3:4, :] += dg

    tile = pl.BlockSpec((tm, D), lambda i, j: (i, 0))
    out = _gridded(
        body, carry, name=name, grid=(T // tm, nj),
        in_specs=[tile, tile, tile,
                  pl.BlockSpec((tm, tn), lambda i, j: (i, j)), pl.BlockSpec((tm, tn), lambda i, j: (i, j)),
                  pl.BlockSpec((9, D), lambda i, j: (0, 0)), pl.BlockSpec((1, D), lambda i, j: (0, 0)),
                  pl.BlockSpec((D, tn), lambda i, j: (0, j)), pl.BlockSpec((D, tn), lambda i, j: (0, j + nj))],
        out_specs=[tile, pl.BlockSpec((8, D), lambda i, j: (0, 0))],
        out_shape=[jax.ShapeDtypeStruct((T, D), F32), jax.ShapeDtypeStruct((8, D), F32)],
        scratch_shapes=[pltpu.VMEM((tm, D), F32)],
    )(x, dxo, f, da, db, mod, gnorm, w_in, w_in)
    return out[:2], out[2:]


def _head(x, target, gfin):
    T = x.shape[0]
    tm = min(512, T)
    ni = T // tm

    def body(x_ref, t_ref, g_ref, dx_ref, sm_ref):
        i = pl.program_id(0)

        @pl.when(i == 0)
        def _():
            sm_ref[...] = jnp.zeros_like(sm_ref)

        xv = x_ref[...]
        g = g_ref[...]
        r = lax.rsqrt(_rowmean(xv * xv) + EPS)
        xh = xv * r
        e = xh * g - t_ref[...]
        sm_ref[1:2, :] += _colsum(e * e) * (0.5 / D)
        dy = e * (1.0 / D)
        sm_ref[0:1, :] += _colsum(dy * xh)
        dxh = dy * g
        dx_ref[...] = r * (dxh - xh * _rowmean(dxh * xh))

        @pl.when(i == ni - 1)
        def _():
            sm_ref[1:2, :] = jnp.broadcast_to(jnp.sum(sm_ref[1:2, :], axis=-1, keepdims=True), (1, D))

    return pl.pallas_call(
        body, name="head_loss", grid=(ni,),
        in_specs=[pl.BlockSpec((tm, D), lambda i: (i, 0)), pl.BlockSpec((tm, D), lambda i: (i, 0)),
                  pl.BlockSpec((1, D), lambda i: (0, 0))],
        out_specs=[pl.BlockSpec((tm, D), lambda i: (i, 0)), pl.BlockSpec((8, D), lambda i: (0, 0))],
        out_shape=[jax.ShapeDtypeStruct((T, D), F32), jax.ShapeDtypeStruct((8, D), F32)],
        compiler_params=_cparams(1),
    )(x, target, gfin)


def _mixin_fwd(x, mod, mo, gnorm, w, carry):
    T = x.shape[0]
    tm = min(1024, T)
    ni = T // tm

    def body(x_ref, mod_ref, g_ref, w_ref, p_ref, h_ref, h_all):
        i = pl.program_id(1)

        @pl.when(pl.program_id(0) == 0)
        def _():
            h, _, _, _ = _modnorm_fwd(x_ref[...], g_ref[...], mod_ref[mo:mo + 1, :], mod_ref[mo + 1:mo + 2, :])
            h_all[i] = h.astype(ACT)
            h_ref[...] = h.astype(ACT)

        p_ref[0] = _mm(h_all[i], w_ref[0])

    first = lambda k, i: (jnp.where(k == 0, i, ni - 1), 0)
    out = _gridded(
        body, carry, name="mixin_fwd", grid=(8, ni),
        in_specs=[pl.BlockSpec((tm, D), first), pl.BlockSpec((9, D), lambda k, i: (0, 0)),
                  pl.BlockSpec((1, D), lambda k, i: (0, 0)), pl.BlockSpec((1, D, D), lambda k, i: (k, 0, 0))],
        out_specs=[pl.BlockSpec((1, tm, D), lambda k, i: (k, i, 0)), pl.BlockSpec((tm, D), first)],
        out_shape=[jax.ShapeDtypeStruct((8, T, D), F32), jax.ShapeDtypeStruct((T, D), ACT)],
        scratch_shapes=[pltpu.VMEM((ni, tm, D), ACT)],
    )(x, mod, gnorm, w)
    return out[:2], out[2:]


def _mixin_bwd(x, h, dxo, dp, mod, mo, gnorm, w, carry):
    T = x.shape[0]
    tm = min(512, T)
    ni = T // tm

    def body(x_ref, h_ref, dxo_ref, dp_ref, mod_ref, g_ref, w_ref, dx_ref, dw_ref, sm_ref, dh_scr, acc):
        k = pl.program_id(0)
        i = pl.program_id(1)

        @pl.when(i == 0)
        def _():
            acc[...] = jnp.zeros_like(acc)

        @pl.when(k == 0)
        def _():
            dh_scr[i] = jnp.zeros((tm, D), F32)

        @pl.when((k == 0) & (i == 0))
        def _():
            sm_ref[...] = jnp.zeros_like(sm_ref)

        dpk = dp_ref[0].astype(MM)
        acc[...] += _mm_tn(h_ref[...], dpk)
        dh_scr[i] += _mm_nt(dpk, w_ref[0])

        @pl.when(i == ni - 1)
        def _():
            dw_ref[0] = acc[...].astype(MM)

        @pl.when(k == 7)
        def _():
            sc = mod_ref[mo + 1:mo + 2, :]
            _, xh, n, r = _modnorm_fwd(x_ref[...], g_ref[...], mod_ref[mo:mo + 1, :], sc)
            dxn, dsh, dsc, dg = _modnorm_bwd(dh_scr[i], xh, n, r, g_ref[...], sc)
            dx_ref[...] = dxo_ref[...] + dxn
            sm_ref[0:1, :] += dsh
            sm_ref[1:2, :] += dsc
            sm_ref[3:4, :] += dg

    out = _gridded(
        body, carry, name="mixin_bwd", grid=(8, ni),
        in_specs=[pl.BlockSpec((tm, D), lambda k, i: (jnp.where(k == 7, i, 0), 0)),
                  pl.BlockSpec((tm, D), lambda k, i: (i, 0)),
                  pl.BlockSpec((tm, D), lambda k, i: (jnp.where(k == 7, i, 0), 0)),
                  pl.BlockSpec((1, tm, D), lambda k, i: (k, i, 0)), pl.BlockSpec((9, D), lambda k, i: (0, 0)),
                  pl.BlockSpec((1, D), lambda k, i: (0, 0)), pl.BlockSpec((1, D, D), lambda k, i: (k, 0, 0))],
        out_specs=[pl.BlockSpec((tm, D), lambda k, i: (jnp.where(k == 7, i, 0), 0)),
                   pl.BlockSpec((1, D, D), lambda k, i: (k, 0, 0)),
                   pl.BlockSpec((8, D), lambda k, i: (0, 0))],
        out_shape=[jax.ShapeDtypeStruct((T, D), F32), jax.ShapeDtypeStruct((8, D, D), MM),
                   jax.ShapeDtypeStruct((8, D), F32)],
        scratch_shapes=[pltpu.VMEM((ni, tm, D), F32), pltpu.VMEM((D, D), F32)],
    )(x, h, dxo, dp, mod, gnorm, w)
    return out[:3], out[3:]


def _hgrn_consts():
    rows = jnp.arange(SUB * HD) // HD
    e = (rows[:, None] == jnp.arange(HD)[None, :]).astype(MM)
    return e, e.T


def _rows_bcast(ref, cb, first, n):
    parts = [jnp.broadcast_to(ref[pl.ds(c * CHUNK + first, 1), :], (n, HD)) for c in range(cb // CHUNK)]
    return jnp.concatenate(parts, axis=0)


def _hgrn_pre(qr, fr, lb_ref, b_scr, cb):
    z = lb_ref[...]
    lb = _sig(z[0:1, :] - z[1:2, :])
    sq = _sig(qr)
    q = qr * sq * Q_SCALE
    sf = _sig(fr)
    fg = lb + (1.0 - lb) * sf
    lf = jnp.log(fg)
    k = 1.0 - fg
    tl = lax.broadcasted_iota(jnp.int32, (cb, HD), 0) % CHUNK
    bc = lf
    sh = 1
    while sh < CHUNK:
        bc = bc + jnp.where(tl >= sh, pltpu.roll(bc, sh, 0), 0.0)
        sh *= 2
    b_scr[...] = bc
    bl = _rows_bcast(b_scr, cb, CHUNK - 1, CHUNK)
    br = [None] + [_rows_bcast(b_scr, cb, SUB * i - 1, CHUNK) for i in range(1, NSUB)]
    sb = tl // SUB
    bref = jnp.where(sb == 0, bc, jnp.where(sb == 1, br[1], jnp.where(sb == 2, br[2], br[3])))
    eb = jnp.exp(bc)
    ekd = jnp.exp(bl - bc)
    eqo = jnp.exp(bc - bref)
    eko = [None] + [jnp.exp(jnp.where(tl < SUB * i, br[i] - bc, NEG)) for i in range(1, NSUB)]
    return dict(lb=lb, sq=sq, q=q, sf=sf, fg=fg, k=k, tl=tl, sb=sb, b=bc, bl=bl, eb=eb, ekd=ekd, eqo=eqo,
                eko=eko, qe=q * eb, kd=k * ekd, qo=q * eqo, ko=[None] + [k * eko[i] for i in range(1, NSUB)])


def _pad_rows(x):
    return jnp.concatenate([x, jnp.zeros_like(x)], axis=0)


def _by_subblock(sbc, parts):
    out = jnp.zeros_like(parts[1])
    for i in range(1, NSUB):
        out = jnp.where(sbc == i, parts[i], out)
    return out


def _hgrn_fwd(p, hgrn_lb, hgrn_g, carry):
    T = p.shape[1]
    cb = min(512, T)
    nch = cb // CHUNK
    ncb = T // cb
    e_mat, _ = _hgrn_consts()

    def body(p_ref, lb_ref, g_ref, e_ref, o_ref, oa_ref, a_ref, s_ref, st_scr, q_scr, k_scr, b_scr, z_scr):
        @pl.when(pl.program_id(1) == 0)
        def _():
            st_scr[...] = jnp.zeros_like(st_scr)

        v = p_ref[2]
        og = p_ref[3]
        pre = _hgrn_pre(p_ref[0], p_ref[1], lb_ref, b_scr, cb)
        q_scr[...] = pre["q"]
        k_scr[...] = pre["k"]
        ti = lax.broadcasted_iota(jnp.int32, (SUB, HD), 0)

        def zbody(c, carry):
            for i in range(NSUB):
                r0 = pl.multiple_of(c * CHUNK + SUB * i, SUB)
                qi = q_scr[pl.ds(r0, SUB), :]
                bi = b_scr[pl.ds(r0, SUB), :]
                for s in range(SUB):
                    krow = k_scr[pl.ds(r0 + s, 1), :]
                    brow = b_scr[pl.ds(r0 + s, 1), :]
                    if s < 8:
                        zz = qi * krow * jnp.exp(jnp.where(ti >= s, bi - brow, NEG))
                    else:
                        lo = qi[8:] * krow * jnp.exp(jnp.where(ti[8:] >= s, bi[8:] - brow, NEG))
                        zz = jnp.concatenate([jnp.zeros((8, HD), F32), lo], axis=0)
                    z_scr[i, pl.ds(pl.multiple_of(c * SUB, SUB), SUB), s * HD:(s + 1) * HD] = zz.astype(MM)
            return carry

        lax.fori_loop(0, nch, zbody, 0)
        adiag = [_mm(z_scr[i], e_ref[...]) for i in range(NSUB)]
        sbc = lax.broadcasted_iota(jnp.int32, (CHUNK, HD), 0) // SUB
        chunks = [slice(c * CHUNK, (c + 1) * CHUNK) for c in range(nch)]
        offs = [[_mm_nt(pre["qo"][rs], _pad_rows(pre["ko"][i][rs])) for i in range(1, NSUB)] for rs in chunks]
        kv = [_mm_tn(v[rs], pre["kd"][rs]) for rs in chunks]
        a_parts = []
        for c in range(nch):
            dparts = []
            for i in range(NSUB):
                blk = adiag[i][c * SUB:(c + 1) * SUB]
                dparts.append(blk if i == 0 else pltpu.roll(blk, SUB * i, 1))
            a_parts.append(_by_subblock(sbc, [None] + offs[c]) + jnp.concatenate(dparts, axis=0))
        a_ref[0] = jnp.concatenate(a_parts, axis=0)
        o_intra = [_mm(a_parts[c], _pad_rows(v[rs])) for c, rs in enumerate(chunks)]
        states = []
        st = st_scr[...]
        for c in range(nch):
            states.append(st)
            st = st * jnp.exp(b_scr[pl.ds(c * CHUNK + CHUNK - 1, 1), :]) + kv[c]
        st_scr[...] = st
        for c in range(nch):
            s_ref[0, c] = states[c]
        o = jnp.concatenate([o_intra[c] + _mm_nt(pre["qe"][rs], states[c]) for c, rs in enumerate(chunks)], axis=0)
        o_ref[...] = o
        on = o * lax.rsqrt(_rowmean(o * o) + EPS) * g_ref[...]
        oa_ref[...] = (on * og * _sig(og)).astype(ACT)

    out = _gridded(
        body, carry, name="hgrn_fwd", grid=(HEADS, ncb),
        in_specs=[pl.BlockSpec((4, cb, HD), lambda h, c: (0, c, h)),
                  pl.BlockSpec((2, HD), lambda h, c: (0, h)),
                  pl.BlockSpec((1, HD), lambda h, c: (0, h)),
                  pl.BlockSpec((SUB * HD, HD), lambda h, c: (0, 0))],
        out_specs=[pl.BlockSpec((cb, HD), lambda h, c: (c, h)),
                   pl.BlockSpec((cb, HD), lambda h, c: (c, h)),
                   pl.BlockSpec((1, cb, HD), lambda h, c: (h, c, 0)),
                   pl.BlockSpec((1, nch, HD, HD), lambda h, c: (h, c, 0, 0))],
        out_shape=[jax.ShapeDtypeStruct((T, D), F32), jax.ShapeDtypeStruct((T, D), ACT),
                   jax.ShapeDtypeStruct((HEADS, T, HD), F32),
                   jax.ShapeDtypeStruct((HEADS, T // CHUNK, HD, HD), F32)],
        scratch_shapes=[pltpu.VMEM((HD, HD), F32), pltpu.VMEM((cb, HD), F32), pltpu.VMEM((cb, HD), F32),
                        pltpu.VMEM((cb, HD), F32), pltpu.VMEM((NSUB, nch * SUB, SUB * HD), MM)],
    )(p, hgrn_lb, hgrn_g, e_mat)
    return out[:4], out[4:]


def _hgrn_bwd(p, o, a_all, s_all, doa, hgrn_lb, hgrn_g, dp, carry):
    T = p.shape[1]
    cb = min(512, T)
    nch = cb // CHUNK
    ncb = T // cb
    _, et_mat = _hgrn_consts()

    def body(p_ref, o_ref, a_ref, s_ref, doa_ref, lb_ref, g_ref, et_ref, dp_in, dp_ref, sm_ref,
             dst_scr, q_scr, k_scr, b_scr, x_scr, dqd_scr, dkd_scr):
        del dp_in

        @pl.when(pl.program_id(1) == 0)
        def _():
            dst_scr[...] = jnp.zeros_like(dst_scr)
            sm_ref[...] = jnp.zeros_like(sm_ref)

        qr = p_ref[0]
        v = p_ref[2]
        og = p_ref[3]
        pre = _hgrn_pre(qr, p_ref[1], lb_ref, b_scr, cb)
        q, k = pre["q"], pre["k"]
        q_scr[...] = q
        k_scr[...] = k
        g = g_ref[...]
        ov = o_ref[...]
        r = lax.rsqrt(_rowmean(ov * ov) + EPS)
        oh = ov * r
        sgo = _sig(og)
        doa_v = doa_ref[...]
        don = doa_v * og * sgo
        dog = doa_v * oh * g * sgo * (1.0 + og * (1.0 - sgo))
        sm_ref[1:2, :] += _colsum(don * oh)
        doh = don * g
        do = r * (doh - oh * _rowmean(doh * oh))

        sbc = lax.broadcasted_iota(jnp.int32, (CHUNK, HD), 0) // SUB
        row_i = lax.broadcasted_iota(jnp.int32, (CHUNK, HD), 0)
        lane_i = lax.broadcasted_iota(jnp.int32, (CHUNK, HD), 1)
        causal = lane_i <= row_i
        chunks = [slice(c * CHUNK, (c + 1) * CHUNK) for c in range(nch)]
        da_parts = [jnp.where(causal, _mm_nt(do[rs], _pad_rows(v[rs])), 0.0) for rs in chunks]
        dv_parts = [_mm_tn(a_ref[0, rs, :], do[rs])[:CHUNK] for rs in chunks]
        dqoff_mm = [[_mm(da_parts[c], _pad_rows(pre["ko"][i][rs])) for i in range(1, NSUB)]
                    for c, rs in enumerate(chunks)]
        dkoff_mm = [[_mm_tn(jnp.where(sbc == i, da_parts[c], 0.0), pre["qo"][rs])[:CHUNK] for i in range(1, NSUB)]
                    for c, rs in enumerate(chunks)]
        dqoff_parts = [_by_subblock(sbc, [None] + dqoff_mm[c]) for c in range(nch)]
        dkoff_parts = []
        for c, rs in enumerate(chunks):
            dko = pre["eko"][1][rs] * dkoff_mm[c][0]
            for i in range(2, NSUB):
                dko = dko + pre["eko"][i][rs] * dkoff_mm[c][i - 1]
            dkoff_parts.append(dko)
        for i in range(NSUB):
            rows = []
            for c in range(nch):
                blk = da_parts[c][SUB * i:SUB * (i + 1)]
                rows.append(blk if i == 0 else pltpu.roll(blk, HD - SUB * i, 1))
            x_scr[i] = _mm(jnp.concatenate(rows, axis=0), et_ref[...])
        ti = lax.broadcasted_iota(jnp.int32, (SUB, HD), 0)

        def dbody(c, carry):
            for i in range(NSUB):
                r0 = pl.multiple_of(c * CHUNK + SUB * i, SUB)
                qi = q_scr[pl.ds(r0, SUB), :]
                bi = b_scr[pl.ds(r0, SUB), :]
                dq_hi = jnp.zeros((8, HD), F32)
                dq_lo = jnp.zeros((8, HD), F32)
                dk_hi = jnp.zeros((8, HD), F32)
                dk_lo = jnp.zeros((8, HD), F32)
                c0 = pl.multiple_of(c * SUB, SUB)
                t8 = ti[:8]
                for s in range(SUB):
                    krow = k_scr[pl.ds(r0 + s, 1), :]
                    brow = b_scr[pl.ds(r0 + s, 1), :]
                    w_lo = (x_scr[i, pl.ds(c0 + 8, 8), s * HD:(s + 1) * HD]
                            * jnp.exp(jnp.where(t8 + 8 >= s, bi[8:] - brow, NEG)))
                    dq_lo = dq_lo + w_lo * krow
                    col = _colsum(w_lo * qi[8:])
                    if s < 8:
                        w_hi = (x_scr[i, pl.ds(c0, 8), s * HD:(s + 1) * HD]
                                * jnp.exp(jnp.where(t8 >= s, bi[:8] - brow, NEG)))
                        dq_hi = dq_hi + w_hi * krow
                        dk_hi = jnp.where(t8 == s, col + _colsum(w_hi * qi[:8]), dk_hi)
                    else:
                        dk_lo = jnp.where(t8 + 8 == s, col, dk_lo)
                dqd_scr[pl.ds(r0, SUB), :] = jnp.concatenate([dq_hi, dq_lo], axis=0)
                dkd_scr[pl.ds(r0, SUB), :] = jnp.concatenate([dk_hi, dk_lo], axis=0)
            return carry

        lax.fori_loop(0, nch, dbody, 0)
        qdo = [_mm_tn(do[rs], pre["qe"][rs]) for rs in chunks]
        dsts = [None] * nch
        dst = dst_scr[...]
        for c in reversed(range(nch)):
            dsts[c] = dst
            dst = dst * jnp.exp(b_scr[pl.ds(c * CHUNK + CHUNK - 1, 1), :]) + qdo[c]
        dst_scr[...] = dst
        sts = [s_ref[0, c] for c in range(nch)]
        dqe_parts = [_mm(do[rs], sts[c]) for c, rs in enumerate(chunks)]
        dkdec_parts = [_mm(v[rs], dsts[c]) for c, rs in enumerate(chunks)]
        dvi_parts = [_mm_nt(pre["kd"][rs], dsts[c]) for c, rs in enumerate(chunks)]
        debl_parts = [_colsum(dsts[c] * sts[c]) for c in range(nch)]
        dqe = jnp.concatenate(dqe_parts, axis=0)
        dkdec = jnp.concatenate(dkdec_parts, axis=0)
        dq_tot = jnp.concatenate(dqoff_parts, axis=0) * pre["eqo"] + dqd_scr[...] + dqe * pre["eb"]
        dk_inter = dkdec * pre["ekd"]
        dk_tot = jnp.concatenate(dkoff_parts, axis=0) + dkd_scr[...] + dk_inter
        db = q * dq_tot - k * dk_tot
        kdk = k * dk_inter
        dbl = jnp.concatenate(
            [jnp.broadcast_to(jnp.exp(b_scr[pl.ds(c * CHUNK + CHUNK - 1, 1), :]) * debl_parts[c]
                              + _colsum(kdk[c * CHUNK:(c + 1) * CHUNK]), (CHUNK, HD)) for c in range(nch)], axis=0)
        tl = pre["tl"]
        rc = db
        sh = 1
        while sh < CHUNK:
            rc = rc + jnp.where(tl + sh < CHUNK, pltpu.roll(rc, cb - sh, 0), 0.0)
            sh *= 2
        dlf = rc + dbl
        dfg = dlf / pre["fg"] - dk_tot
        sf = pre["sf"]
        lb = pre["lb"]
        sm_ref[0:1, :] += _colsum(dfg * (1.0 - sf))
        sq = pre["sq"]
        dp_ref[0] = (dq_tot * Q_SCALE * sq * (1.0 + qr * (1.0 - sq))).astype(ACT)
        dp_ref[1] = (dfg * (1.0 - lb) * sf * (1.0 - sf)).astype(ACT)
        dp_ref[2] = (jnp.concatenate(dv_parts, axis=0) + jnp.concatenate(dvi_parts, axis=0)).astype(ACT)
        dp_ref[3] = dog.astype(ACT)

    rev = lambda c: ncb - 1 - c
    out = _gridded(
        body, carry, name="hgrn_bwd", grid=(HEADS, ncb),
        in_specs=[pl.BlockSpec((4, cb, HD), lambda h, c: (0, rev(c), h)),
                  pl.BlockSpec((cb, HD), lambda h, c: (rev(c), h)),
                  pl.BlockSpec((1, cb, HD), lambda h, c: (h, rev(c), 0)),
                  pl.BlockSpec((1, nch, HD, HD), lambda h, c: (h, rev(c), 0, 0)),
                  pl.BlockSpec((cb, HD), lambda h, c: (rev(c), h)),
                  pl.BlockSpec((2, HD), lambda h, c: (0, h)),
                  pl.BlockSpec((1, HD), lambda h, c: (0, h)),
                  pl.BlockSpec((HD, SUB * HD), lambda h, c: (0, 0)),
                  pl.BlockSpec(memory_space=pl.ANY)],
        out_specs=[pl.BlockSpec((4, cb, HD), lambda h, c: (0, rev(c), h)),
                   pl.BlockSpec((8, HD), lambda h, c: (0, h))],
        out_shape=[jax.ShapeDtypeStruct(dp.shape, dp.dtype), jax.ShapeDtypeStruct((8, D), F32)],
        aliases={8: 0},
        scratch_shapes=[pltpu.VMEM((HD, HD), F32), pltpu.VMEM((cb, HD), F32), pltpu.VMEM((cb, HD), F32),
                        pltpu.VMEM((cb, HD), F32), pltpu.VMEM((NSUB, nch * SUB, SUB * HD), F32),
                        pltpu.VMEM((cb, HD), F32), pltpu.VMEM((cb, HD), F32)],
    )(p, o, a_all, s_all, doa, hgrn_lb, hgrn_g, et_mat, dp)
    return out[:2], out[2:]


def _ln_fwd(u1, g, b):
    mu = _rowmean(u1)
    xc = u1 - mu
    rs = lax.rsqrt(_rowmean(xc * xc) + EPS)
    xh = xc * rs
    return xh * g + b, xh, rs


CONV_RB = 64
LANES = 128


def _shift_rows(src, sh, ls, n):
    for r in range(1, 8):
        sh[r - 1, 0:n, :] = src[pl.ds(r, n), ls]


def _tap(src, sh, ls, off, r0, rows):
    r = off % 8
    if r == 0:
        return src[pl.ds(r0 + off, rows), ls]
    return sh[r - 1, pl.ds(r0 + off - r, rows), :]


def _conv_fwd(p, cw, cb_, lng, lnb):
    T = p.shape[1]
    tm = min(512, T)
    n = HALO + tm - 8

    def body(p_ref, cw_ref, cb_ref, g_ref, b_ref, u1_ref, u2_ref, buf, sh):
        @pl.when(pl.program_id(0) == 0)
        def _():
            buf[0:HALO, :] = jnp.zeros((HALO, D), F32)

        buf[HALO:HALO + tm, :] = p_ref[0] * _sig(p_ref[1])
        for lb in range(D // LANES):
            ls = slice(lb * LANES, (lb + 1) * LANES)
            _shift_rows(buf, sh, ls, n)
            taps = [cw_ref[j:j + 1, ls] for j in range(CONV_K)]
            bias = cb_ref[:, ls]

            def rows_body(rb, carry):
                r0 = pl.multiple_of(rb * CONV_RB, CONV_RB)
                acc = jnp.broadcast_to(bias, (CONV_RB, LANES))
                for j in range(CONV_K):
                    acc = acc + taps[j] * _tap(buf, sh, ls, HALO - (CONV_K - 1) + j, r0, CONV_RB)
                u1_ref[pl.ds(r0, CONV_RB), ls] = acc
                return carry

            lax.fori_loop(0, tm // CONV_RB, rows_body, 0)
        y, _, _ = _ln_fwd(u1_ref[...], g_ref[...], b_ref[...])
        u2_ref[...] = (y * _sig(y)).astype(ACT)
        buf[0:HALO, :] = buf[tm:tm + HALO, :]

    return pl.pallas_call(
        body, name="conv_fwd", grid=(T // tm,),
        in_specs=[pl.BlockSpec((2, tm, D), lambda i: (2, i, 0)), pl.BlockSpec((HALO, D), lambda i: (0, 0)),
                  pl.BlockSpec((1, D), lambda i: (0, 0)), pl.BlockSpec((1, D), lambda i: (0, 0)),
                  pl.BlockSpec((1, D), lambda i: (0, 0))],
        out_specs=[pl.BlockSpec((tm, D), lambda i: (i, 0)), pl.BlockSpec((tm, D), lambda i: (i, 0))],
        out_shape=[jax.ShapeDtypeStruct((T, D), F32), jax.ShapeDtypeStruct((T, D), ACT)],
        scratch_shapes=[pltpu.VMEM((HALO + tm, D), F32), pltpu.VMEM((7, n, LANES), F32)],
        compiler_params=_cparams(1),
    )(p, cw, cb_, lng, lnb)


def _conv_bwd(p, u1, du2, cw, lng, lnb, dp):
    T = p.shape[1]
    tm = min(512, T)
    ni = T // tm
    hb = tm // HALO

    n = HALO + tm - 8

    def body(p_ref, ph_ref, u1_ref, du2_ref, cw_ref, g_ref, b_ref, dp_in, dp_ref, dcw_ref, sm_ref, ubuf, dbuf,
             sh, dacc):
        del dp_in
        step = pl.program_id(0)

        @pl.when(step == 0)
        def _():
            dbuf[tm:tm + HALO, :] = jnp.zeros((HALO, D), F32)
            dcw_ref[...] = jnp.zeros_like(dcw_ref)
            sm_ref[...] = jnp.zeros_like(sm_ref)

        ua = p_ref[0]
        sgb = _sig(p_ref[1])
        halo = ph_ref[0] * _sig(ph_ref[1])
        ubuf[0:HALO, :] = jnp.where(step == ni - 1, 0.0, halo)
        ubuf[HALO:HALO + tm, :] = ua * sgb
        g = g_ref[...]
        y, xh, rs = _ln_fwd(u1_ref[...], g, b_ref[...])
        sy = _sig(y)
        dy = du2_ref[...] * sy * (1.0 + y * (1.0 - sy))
        sm_ref[1:2, :] += _colsum(dy * xh)
        sm_ref[2:3, :] += _colsum(dy)
        dxh = dy * g
        du1 = rs * (dxh - _rowmean(dxh) - xh * _rowmean(dxh * xh))
        sm_ref[0:1, :] += _colsum(du1)
        dbuf[0:tm, :] = du1
        for lb in range(D // LANES):
            ls = slice(lb * LANES, (lb + 1) * LANES)
            taps = [cw_ref[j:j + 1, ls] for j in range(CONV_K)]
            _shift_rows(dbuf, sh, ls, n)

            def du0_body(rb, carry):
                r0 = pl.multiple_of(rb * CONV_RB, CONV_RB)
                acc = jnp.zeros((CONV_RB, LANES), F32)
                for j in range(CONV_K):
                    acc = acc + taps[j] * _tap(dbuf, sh, ls, CONV_K - 1 - j, r0, CONV_RB)
                dp_ref[0, pl.ds(r0, CONV_RB), ls] = acc.astype(ACT)
                return carry

            lax.fori_loop(0, tm // CONV_RB, du0_body, 0)
            _shift_rows(ubuf, sh, ls, n)
            dacc[...] = jnp.zeros_like(dacc)

            def dcw_body(rb, carry):
                r0 = pl.multiple_of(rb * CONV_RB, CONV_RB)
                d = dbuf[pl.ds(r0, CONV_RB), ls]
                for j in range(CONV_K):
                    prod = d * _tap(ubuf, sh, ls, HALO - (CONV_K - 1) + j, r0, CONV_RB)
                    dacc[8 * j:8 * j + 8, :] += jnp.sum(prod.reshape(CONV_RB // 8, 8, LANES), axis=0)
                return carry

            lax.fori_loop(0, tm // CONV_RB, dcw_body, 0)
            for j in range(CONV_K):
                dcw_ref[j:j + 1, ls] += _colsum(dacc[8 * j:8 * j + 8, :])
        du0 = dp_ref[0].astype(F32)
        dp_ref[0] = (du0 * sgb).astype(ACT)
        dp_ref[1] = (du0 * ua * sgb * (1.0 - sgb)).astype(ACT)
        dbuf[tm:tm + HALO, :] = dbuf[0:HALO, :]

    rev = lambda i: ni - 1 - i
    return pl.pallas_call(
        body, name="conv_bwd", grid=(ni,),
        in_specs=[pl.BlockSpec((2, tm, D), lambda i: (2, rev(i), 0)),
                  pl.BlockSpec((2, HALO, D), lambda i: (2, jnp.maximum(rev(i) * hb - 1, 0), 0)),
                  pl.BlockSpec((tm, D), lambda i: (rev(i), 0)), pl.BlockSpec((tm, D), lambda i: (rev(i), 0)),
                  pl.BlockSpec((HALO, D), lambda i: (0, 0)), pl.BlockSpec((1, D), lambda i: (0, 0)),
                  pl.BlockSpec((1, D), lambda i: (0, 0)), pl.BlockSpec(memory_space=pl.ANY)],
        out_specs=[pl.BlockSpec((2, tm, D), lambda i: (2, rev(i), 0)),
                   pl.BlockSpec((HALO, D), lambda i: (0, 0)), pl.BlockSpec((8, D), lambda i: (0, 0))],
        out_shape=[jax.ShapeDtypeStruct(dp.shape, dp.dtype), jax.ShapeDtypeStruct((HALO, D), F32),
                   jax.ShapeDtypeStruct((8, D), F32)],
        input_output_aliases={7: 0},
        scratch_shapes=[pltpu.VMEM((HALO + tm, D), F32), pltpu.VMEM((tm + HALO, D), F32),
                        pltpu.VMEM((7, n, LANES), F32), pltpu.VMEM((8 * CONV_K, LANES), F32)],
        compiler_params=_cparams(1),
    )(p, p, u1, du2, cw, lng, lnb, dp)


def _mixout_fwd(x, oa, u2, p, mod, mo, w_a, w_b, w_o):
    T = x.shape[0]
    tm = min(512, T)

    def body(x_ref, oa_ref, u2_ref, p_ref, mod_ref, wa_ref, wb_ref, wo_ref, xo_ref, ya_ref, yb_ref, mo_ref):
        ya = _mm(oa_ref[...], wa_ref[...])
        yb = _mm(u2_ref[...], wb_ref[...])
        ya_ref[...] = ya.astype(ACT)
        yb_ref[...] = yb.astype(ACT)
        merged = _sig(p_ref[0]) * ya + _sig(p_ref[1]) * yb
        out = _mm(merged, wo_ref[...])
        mo_ref[...] = out
        xo_ref[...] = x_ref[...] + mod_ref[mo + 2:mo + 3, :] * out

    tile = pl.BlockSpec((tm, D), lambda i: (i, 0))
    wspec = pl.BlockSpec((D, D), lambda i: (0, 0))
    return pl.pallas_call(
        body, name="mixout_fwd", grid=(T // tm,),
        in_specs=[tile, tile, tile, pl.BlockSpec((2, tm, D), lambda i: (3, i, 0)),
                  pl.BlockSpec((9, D), lambda i: (0, 0)), wspec, wspec, wspec],
        out_specs=[tile, tile, tile, tile],
        out_shape=[jax.ShapeDtypeStruct((T, D), F32), jax.ShapeDtypeStruct((T, D), ACT),
                   jax.ShapeDtypeStruct((T, D), ACT), jax.ShapeDtypeStruct((T, D), F32)],
        compiler_params=_cparams(1),
    )(x, oa, u2, p, mod, w_a, w_b, w_o)


def _mixout_bwd(dxo, oa, u2, ya, yb, mout, p, mod, mo, w_a, w_b, w_o):
    T = dxo.shape[0]
    tm = min(256, T)

    def body(dxo_ref, oa_ref, u2_ref, ya_ref, yb_ref, mo_ref, p_ref, mod_ref, wa_ref, wb_ref, wo_ref,
             dp_ref, doa_ref, du2_ref, dwa_ref, dwb_ref, dwo_ref, sm_ref):
        @pl.when(pl.program_id(0) == 0)
        def _():
            dwa_ref[...] = jnp.zeros_like(dwa_ref)
            dwb_ref[...] = jnp.zeros_like(dwb_ref)
            dwo_ref[...] = jnp.zeros_like(dwo_ref)
            sm_ref[...] = jnp.zeros_like(sm_ref)

        dxo_v = dxo_ref[...]
        sm_ref[2:3, :] += _colsum(dxo_v * mo_ref[...])
        dmo = (mod_ref[mo + 2:mo + 3, :] * dxo_v).astype(MM)
        ya = ya_ref[...].astype(F32)
        yb = yb_ref[...].astype(F32)
        sga = _sig(p_ref[0])
        sgb = _sig(p_ref[1])
        merged = (sga * ya + sgb * yb).astype(MM)
        dwo_ref[...] += _mm_tn(merged, dmo)
        dmg = _mm_nt(dmo, wo_ref[...])
        dp_ref[0] = (dmg * ya * sga * (1.0 - sga)).astype(ACT)
        dp_ref[1] = (dmg * yb * sgb * (1.0 - sgb)).astype(ACT)
        dya = (dmg * sga).astype(MM)
        dyb = (dmg * sgb).astype(MM)
        dwa_ref[...] += _mm_tn(oa_ref[...], dya)
        dwb_ref[...] += _mm_tn(u2_ref[...], dyb)
        doa_ref[...] = _mm_nt(dya, wa_ref[...])
        du2_ref[...] = _mm_nt(dyb, wb_ref[...])

    tile = pl.BlockSpec((tm, D), lambda i: (i, 0))
    wspec = pl.BlockSpec((D, D), lambda i: (0, 0))
    return pl.pallas_call(
        body, name="mixout_bwd", grid=(T // tm,),
        in_specs=[tile, tile, tile, tile, tile, tile, pl.BlockSpec((2, tm, D), lambda i: (3, i, 0)),
                  pl.BlockSpec((9, D), lambda i: (0, 0)), wspec, wspec, wspec],
        out_specs=[pl.BlockSpec((2, tm, D), lambda i: (3, i, 0)), tile, tile, wspec, wspec, wspec,
                   pl.BlockSpec((8, D), lambda i: (0, 0))],
        out_shape=[jax.ShapeDtypeStruct((8, T, D), ACT), jax.ShapeDtypeStruct((T, D), F32),
                   jax.ShapeDtypeStruct((T, D), F32), jax.ShapeDtypeStruct((D, D), F32),
                   jax.ShapeDtypeStruct((D, D), F32), jax.ShapeDtypeStruct((D, D), F32),
                   jax.ShapeDtypeStruct((8, D), F32)],
        compiler_params=_cparams(1),
    )(dxo, oa, u2, ya, yb, mout, p, mod, w_a, w_b, w_o)


def _ada_fwd(cs_all, ada_w, ada_b_cols):
    def body(cs_ref, w_ref, b_ref, out_ref):
        out_ref[...] = jnp.dot(cs_ref[...], w_ref[...], preferred_element_type=F32,
                               precision=lax.Precision.HIGHEST) + b_ref[...]

    return pl.pallas_call(
        body, name="ada_fwd", out_shape=jax.ShapeDtypeStruct((N_DEV, ada_w.shape[1]), F32),
        compiler_params=pltpu.CompilerParams(vmem_limit_bytes=VMEM_LIMIT),
    )(cs_all, ada_w, ada_b_cols)


def _ada_wgrad(cs_all, dmod_cols):
    cs_t = jnp.pad(cs_all.T, ((0, 0), (0, HD - N_DEV)))
    dm = jnp.pad(dmod_cols, ((0, HD - N_DEV), (0, 0)))

    def body(cs_ref, d_ref, out_ref):
        out_ref[...] = jnp.dot(cs_ref[...], d_ref[...], preferred_element_type=F32,
                               precision=lax.Precision.HIGHEST)

    return pl.pallas_call(
        body, name="ada_wgrad", out_shape=jax.ShapeDtypeStruct((D, dmod_cols.shape[1]), F32),
        compiler_params=pltpu.CompilerParams(vmem_limit_bytes=VMEM_LIMIT),
    )(cs_t, dm)


def _adam_math(w, g, m, v):
    m2 = ADAM_B1 * m + (1.0 - ADAM_B1) * g
    v2 = ADAM_B2 * v + (1.0 - ADAM_B2) * (g * g)
    m_hat = m2 / (1.0 - ADAM_B1 ** ADAM_STEP)
    v_hat = v2 / (1.0 - ADAM_B2 ** ADAM_STEP)
    delta = -ADAM_LR * (m_hat / (jnp.sqrt(v_hat) + ADAM_EPS) + ADAM_WD * w)
    return delta, m2, v2


def _adamw(w, m, v, g, name):
    R, C = w.shape
    slots = g.ndim == 3
    tr = R
    for cand in (256, 176):
        if R % cand == 0 and R > cand:
            tr = cand
            break

    def body(w_ref, m_ref, v_ref, g_ref, go_ref, d_ref, mo_ref, vo_ref):
        if slots:
            gv = g_ref[0].astype(F32)
            for s in range(1, N_DEV):
                gv = gv + g_ref[s].astype(F32)
        else:
            gv = g_ref[...]
        go_ref[...] = gv
        d_ref[...], mo_ref[...], vo_ref[...] = _adam_math(w_ref[...], gv, m_ref[...], v_ref[...])

    tile = pl.BlockSpec((tr, C), lambda i: (i, 0))
    gspec = pl.BlockSpec((N_DEV, tr, C), lambda i: (0, i, 0)) if slots else tile
    sds = jax.ShapeDtypeStruct((R, C), F32)
    return pl.pallas_call(
        body, name=name, grid=(R // tr,), in_specs=[tile, tile, tile, gspec], out_specs=[tile] * 4,
        out_shape=[sds] * 4, compiler_params=_cparams(1),
    )(w, m, v, g)


def _sum_slots(pack):
    def body(p_ref, out_ref):
        acc = p_ref[0]
        for s in range(1, N_DEV):
            acc = acc + p_ref[s]
        out_ref[...] = acc

    return pl.pallas_call(body, name="sum_small", out_shape=jax.ShapeDtypeStruct(pack.shape[1:], F32))(pack)


def _me():
    return lax.axis_index("x"), lax.axis_index("y"), lax.axis_index("c")


def _peer(r):
    x, y, c = _me()
    px = 1 - x if r & 4 else x
    py = 1 - y if r & 2 else y
    pc = 1 - c if r & 1 else c
    return (px, py, pc), 4 * px + 2 * py + pc


def _allgather_small(x):
    R, C = x.shape

    def body(x_ref, out_ref, send_sems, recv_sems):
        mx, my, mc = _me()
        me = 4 * mx + 2 * my + mc
        mine = out_ref.at[pl.ds(pl.multiple_of(me * R, 8), R), :]
        copies = []
        for r in range(1, N_DEV):
            dev, _ = _peer(r)
            copies.append(pltpu.make_async_remote_copy(
                src_ref=x_ref, dst_ref=mine, send_sem=send_sems.at[r - 1], recv_sem=recv_sems.at[r - 1],
                device_id=dev, device_id_type=MESH))
        for cp in copies:
            cp.start()
        mine[...] = x_ref[...]
        for r in range(1, N_DEV):
            _, idx = _peer(r)
            theirs = out_ref.at[pl.ds(pl.multiple_of(idx * R, 8), R), :]
            pltpu.make_async_remote_copy(
                src_ref=x_ref, dst_ref=theirs, send_sem=send_sems.at[r - 1], recv_sem=recv_sems.at[r - 1],
                device_id=_peer(r)[0], device_id_type=MESH).wait_recv()
        for cp in copies:
            cp.wait_send()

    return pl.pallas_call(
        body, name="allgather_small_%dx%d" % (R, C),
        out_shape=jax.ShapeDtypeStruct((N_DEV * R, C), F32),
        in_specs=[pl.BlockSpec(memory_space=pltpu.VMEM)], out_specs=pl.BlockSpec(memory_space=pltpu.VMEM),
        scratch_shapes=[pltpu.SemaphoreType.DMA((N_DEV - 1,)), pltpu.SemaphoreType.DMA((N_DEV - 1,))],
    )(x)


def _xchg_copies(ins, outs, sems, gather):
    send_sems, recv_sems, local_sems = sems
    mx, my, mc = _me()
    me = 4 * mx + 2 * my + mc
    sibling = _peer(1)[0]

    def rdma(a, r, dev, src, slot):
        k = a * (N_DEV - 1) + r - 1
        return pltpu.make_async_remote_copy(
            src_ref=src, dst_ref=outs[a].at[slot], send_sem=send_sems.at[k], recv_sem=recv_sems.at[k],
            device_id=dev, device_id_type=MESH)

    own, sends, relays, recvs = [], [], [], []
    for a in range(len(ins)):
        own.append(pltpu.make_async_copy(ins[a] if gather else ins[a].at[me], outs[a].at[me], local_sems.at[a]))
        for r in range(1, N_DEV):
            dev, idx = _peer(r)
            if not gather:
                sends.append(rdma(a, r, dev, ins[a].at[idx], me))
                recvs.append(rdma(a, r, dev, ins[a].at[idx], idx))
            elif r == 1:
                sends.append(rdma(a, r, dev, ins[a], me))
                recvs.append(rdma(a, r, dev, ins[a], idx))
            elif r % 2 == 0:
                sends.append(rdma(a, r, dev, ins[a], me))
                relays.append((rdma(a, r, dev, ins[a], idx), rdma(a, r + 1, sibling, outs[a].at[idx], idx)))
            else:
                recvs.append(rdma(a, r, sibling, ins[a], idx))
    return own, sends, relays, recvs


def _xchg_start(ins, outs, sems, gather):
    own, sends, _, _ = _xchg_copies(ins, outs, sems, gather)
    for cp in own + sends:
        cp.start()


def _xchg_wait(ins, outs, sems, gather):
    own, sends, relays, recvs = _xchg_copies(ins, outs, sems, gather)
    for arrival, relay in relays:
        arrival.wait_recv()
        relay.start()
    for cp in recvs:
        cp.wait_recv()
    for cp in own:
        cp.wait()
    for cp in sends + [relay for _, relay in relays]:
        cp.wait_send()


def _xchg_specs(arrays, gather):
    n = len(arrays)
    out_shape = [jax.ShapeDtypeStruct(((N_DEV,) + a.shape) if gather else a.shape, a.dtype) for a in arrays]
    sems = [pltpu.SemaphoreType.DMA((n * (N_DEV - 1),)), pltpu.SemaphoreType.DMA((n * (N_DEV - 1),)),
            pltpu.SemaphoreType.DMA((n,))]
    return out_shape, sems


def _exchange(arrays, gather, name):
    n = len(arrays)

    def body(*refs):
        _xchg_start(refs[:n], refs[n:2 * n], refs[2 * n:], gather)
        _xchg_wait(refs[:n], refs[n:2 * n], refs[2 * n:], gather)

    out_shape, sems = _xchg_specs(arrays, gather)
    return pl.pallas_call(
        body, name=name, out_shape=out_shape,
        in_specs=[pl.BlockSpec(memory_space=pl.ANY)] * n, out_specs=[pl.BlockSpec(memory_space=pl.ANY)] * n,
        scratch_shapes=sems,
    )(*arrays)


def _gridded(body, carry, *, name, grid, in_specs, out_specs, out_shape, scratch_shapes=(), aliases=None):
    if carry is None:
        return pl.pallas_call(
            body, name=name, grid=grid, in_specs=list(in_specs), out_specs=list(out_specs),
            out_shape=list(out_shape), scratch_shapes=list(scratch_shapes), input_output_aliases=aliases or {},
            compiler_params=_cparams(len(grid)))
    arrays, gather = carry
    n, n_in, n_out, n_scr = len(arrays), len(in_specs), len(out_specs), len(scratch_shapes)
    c_shape, c_sems = _xchg_specs(arrays, gather)

    def wrapped(*refs):
        ins, cin = refs[:n_in], refs[n_in:n_in + n]
        o0 = n_in + n
        outs, cout = refs[o0:o0 + n_out], refs[o0 + n_out:o0 + n_out + n]
        s0 = o0 + n_out + n
        scr, sems = refs[s0:s0 + n_scr], refs[s0 + n_scr:]
        first = pl.program_id(0) == 0
        last = pl.program_id(0) == grid[0] - 1
        for ax in range(1, len(grid)):
            first = first & (pl.program_id(ax) == 0)
            last = last & (pl.program_id(ax) == grid[ax] - 1)

        @pl.when(first)
        def _():
            _xchg_start(cin, cout, sems, gather)

        body(*ins, *outs, *scr)

        @pl.when(last)
        def _():
            _xchg_wait(cin, cout, sems, gather)

    hbm = pl.BlockSpec(memory_space=pl.ANY)
    res = pl.pallas_call(
        wrapped, name=name, grid=grid, in_specs=list(in_specs) + [hbm] * n, out_specs=list(out_specs) + [hbm] * n,
        out_shape=list(out_shape) + c_shape, scratch_shapes=list(scratch_shapes) + c_sems,
        input_output_aliases=aliases or {}, compiler_params=_cparams(len(grid)),
    )
    return lambda *args: res(*args, *arrays)


def _local_step(x, target, mod, small, sh):
    w1_in, w1_out = _exchange([sh["ffn1_w_in"], sh["ffn1_w_out"]], True, "allgather_ffn1")
    w1_in, w1_out = _full_w_in(w1_in), w1_out.reshape(D_FF, D)
    (x1, a1, b1, f1, h1), (wm_in,) = _ffn_fwd(x, mod, 0, small["norm_ffn1"], w1_in, w1_out, 0.5, "ffn1_fwd",
                                              ([sh["mix_w_in"]], True))
    (p, h2), (wh_o, wc_o, wm_o, cw) = _mixin_fwd(
        x1, mod, 3, small["norm_mix"], wm_in,
        ([sh["hgrn_w_o"], sh["conv_w_o"], sh["mix_w_out"], sh["conv_w"]], True))
    wh_o, wc_o, wm_o = wh_o.reshape(D, D), wc_o.reshape(D, D), wm_o.reshape(D, D)
    cw = jnp.pad(cw.transpose(1, 0, 2).reshape(CONV_K, D), ((0, HALO - CONV_K), (0, 0)))
    (o, oa, a_all, s_all), (w2_in, w2_out) = _hgrn_fwd(p, small["hgrn_lb"], small["hgrn_g"],
                                                       ([sh["ffn2_w_in"], sh["ffn2_w_out"]], True))
    w2_in, w2_out = _full_w_in(w2_in), w2_out.reshape(D_FF, D)
    u1, u2 = _conv_fwd(p, cw, small["conv_b"], small["conv_ln_g"], small["conv_ln_b"])
    x2, ya, yb, mout = _mixout_fwd(x1, oa, u2, p, mod, 3, wh_o, wc_o, wm_o)
    (x3, a3, b3, f3, h3), _ = _ffn_fwd(x2, mod, 6, small["norm_ffn2"], w2_in, w2_out, 0.5, "ffn2_fwd", None)
    dx3, sm_head = _head(x3, target, small["norm_final"])

    (da3, db3, dw2_a, dw2_b, dw2_out), _ = _ffn_bwd_w(h3, dx3, a3, b3, mod, 6, w2_out, 0.5, "ffn2_bwd_w", None)
    (dx2, sm3), _ = _ffn_bwd_x(x2, dx3, f3, da3, db3, mod, 6, small["norm_ffn2"], w2_in, 0.5, "ffn2_bwd_x", None)
    dp, doa, du2, dwh_o, dwc_o, dwm_o, sm_mo = _mixout_bwd(dx2, oa, u2, ya, yb, mout, p, mod, 3, wh_o, wc_o, wm_o)
    dp, dcw, sm_cv = _conv_bwd(p, u1, du2, cw, small["conv_ln_g"], small["conv_ln_b"], dp)
    rows = lambda t: t.reshape(N_DEV, -1, D).astype(MM)
    (dp, sm_hg), (r2_in, r2_out) = _hgrn_bwd(p, o, a_all, s_all, doa, small["hgrn_lb"], small["hgrn_g"], dp,
                                             ([_w_in_shards(dw2_a, dw2_b), rows(dw2_out)], False))
    (dx1, dwm_in, sm2), (rh_o, rc_o, rm_o, rcw) = _mixin_bwd(
        x1, h2, dx2, dp, mod, 3, small["norm_mix"], wm_in,
        ([rows(dwh_o), rows(dwc_o), rows(dwm_o), dcw[:CONV_K].reshape(CONV_K, N_DEV, -1).transpose(1, 0, 2)], False))
    (da1, db1, dw1_a, dw1_b, dw1_out), (rm_in,) = _ffn_bwd_w(h1, dx1, a1, b1, mod, 0, w1_out, 0.5, "ffn1_bwd_w",
                                                            ([dwm_in], False))
    (dx0, sm1), (r1_in, r1_out) = _ffn_bwd_x(x, dx1, f1, da1, db1, mod, 0, small["norm_ffn1"], w1_in, 0.5,
                                             "ffn1_bwd_x", ([_w_in_shards(dw1_a, dw1_b), rows(dw1_out)], False))

    dmod = jnp.concatenate([sm1[0:3], sm2[0:2], sm_mo[2:3], sm3[0:3]], axis=0)
    gsmall = dict(norm_ffn1=sm1[3:4], norm_mix=sm2[3:4], lb0=sm_hg[0:1], hgrn_g=sm_hg[1:2], conv_b=sm_cv[0:1],
                  conv_ln_g=sm_cv[1:2], conv_ln_b=sm_cv[2:3], norm_ffn2=sm3[3:4], norm_final=sm_head[0:1])
    recv = dict(ffn1_w_in=r1_in, ffn1_w_out=r1_out, mix_w_in=rm_in, hgrn_w_o=rh_o, conv_w=rcw, conv_w_o=rc_o,
                mix_w_out=rm_o, ffn2_w_in=r2_in, ffn2_w_out=r2_out)
    return sm_head[1, 0], dx0, dmod, gsmall, recv


def _full_w_in(g):
    return g.transpose(1, 0, 2).reshape(D, -1)


def _w_in_shards(dwa, dwb):
    half = N_DEV // 2
    return jnp.concatenate([t.reshape(D, half, -1).transpose(1, 0, 2) for t in (dwa, dwb)], axis=0)


SMALL_ORDER = ("norm_ffn1", "norm_mix", "lb0", "hgrn_g", "conv_b", "conv_ln_g", "conv_ln_b", "norm_ffn2",
               "norm_final")
PACK_ROWS = 24


def kernel(x, c, ada_w, ada_b, norm_ffn1, ffn1_w_in, ffn1_w_out, norm_mix, mix_w_in, hgrn_lb, hgrn_g, hgrn_w_o, conv_w, conv_b, conv_ln_g, conv_ln_b, conv_w_o, mix_w_out, norm_ffn2, ffn2_w_in, ffn2_w_out, norm_final, loss_target, m_ada_w, m_ada_b, m_norm_ffn1, m_ffn1_w_in, m_ffn1_w_out, m_norm_mix, m_mix_w_in, m_hgrn_lb, m_hgrn_g, m_hgrn_w_o, m_conv_w, m_conv_b, m_conv_ln_g, m_conv_ln_b, m_conv_w_o, m_mix_w_out, m_norm_ffn2, m_ffn2_w_in, m_ffn2_w_out, m_norm_final, v_ada_w, v_ada_b, v_norm_ffn1, v_ffn1_w_in, v_ffn1_w_out, v_norm_mix, v_mix_w_in, v_hgrn_lb, v_hgrn_g, v_hgrn_w_o, v_conv_w, v_conv_b, v_conv_ln_g, v_conv_ln_b, v_conv_w_o, v_mix_w_out, v_norm_ffn2, v_ffn2_w_in, v_ffn2_w_out, v_norm_final):
    mx, my, mc = _me()
    me = 4 * mx + 2 * my + mc
    ncol = ada_w.shape[2]

    cs = jnp.broadcast_to(c * jax.nn.sigmoid(c), (8, D))
    cs_all = _allgather_small(cs).reshape(N_DEV, 8, D)[:, 0, :]
    ada_b_cols = lax.dynamic_slice(ada_b, (0, me * ncol), (1, ncol))
    mod_cols = _ada_fwd(cs_all, ada_w[0], ada_b_cols)
    mod_all = _allgather_small(mod_cols).reshape(N_DEV, N_DEV, ncol)
    mod = lax.dynamic_index_in_dim(mod_all, me, axis=1, keepdims=False).reshape(9, D)

    sh = dict(ffn1_w_in=ffn1_w_in, ffn1_w_out=ffn1_w_out, mix_w_in=mix_w_in, hgrn_w_o=hgrn_w_o,
              conv_w_o=conv_w_o, mix_w_out=mix_w_out, ffn2_w_in=ffn2_w_in, ffn2_w_out=ffn2_w_out)
    sh = {n: w[0].astype(MM) for n, w in sh.items()}
    sh["conv_w"] = conv_w[0]
    small = dict(norm_ffn1=norm_ffn1, norm_mix=norm_mix, hgrn_lb=hgrn_lb, hgrn_g=hgrn_g, conv_b=conv_b,
                 conv_ln_g=conv_ln_g, conv_ln_b=conv_ln_b, norm_ffn2=norm_ffn2, norm_final=norm_final.reshape(1, D))

    loss_local, dx, dmod, gsmall, recv = _local_step(x[0], loss_target[0], mod, small, sh)
    loss = lax.psum(loss_local, ("x", "y", "c"))

    pack = jnp.concatenate([dmod] + [gsmall[n] for n in SMALL_ORDER]
                           + [jnp.zeros((PACK_ROWS - 9 - len(SMALL_ORDER), D), F32)], axis=0)
    pack_all = _allgather_small(pack).reshape(N_DEV, PACK_ROWS, D)
    tot = _sum_slots(pack_all)
    gs = {n: tot[9 + i:10 + i] for i, n in enumerate(SMALL_ORDER)}
    dmod_all = pack_all[:, 0:9, :].reshape(N_DEV, 9 * D)
    g_ada_b = tot[0:9].reshape(1, 9 * D)
    g_ada_w = _ada_wgrad(cs_all, lax.dynamic_slice(dmod_all, (0, me * ncol), (N_DEV, ncol)))
    z = hgrn_lb.astype(F32)
    p0 = jax.nn.sigmoid(z[0:1] - z[1:2])
    dz0 = p0 * (1.0 - p0) * gs["lb0"]
    g_hgrn_lb = jnp.concatenate([dz0, -dz0], axis=0)

    res = {}
    res["ada_w"] = _adamw(ada_w[0], m_ada_w[0], v_ada_w[0], g_ada_w, "adamw_ada_w")
    big = dict(ffn1_w_in=(ffn1_w_in, m_ffn1_w_in, v_ffn1_w_in), ffn1_w_out=(ffn1_w_out, m_ffn1_w_out, v_ffn1_w_out),
               mix_w_in=(mix_w_in, m_mix_w_in, v_mix_w_in), hgrn_w_o=(hgrn_w_o, m_hgrn_w_o, v_hgrn_w_o),
               conv_w=(conv_w, m_conv_w, v_conv_w), conv_w_o=(conv_w_o, m_conv_w_o, v_conv_w_o),
               mix_w_out=(mix_w_out, m_mix_w_out, v_mix_w_out), ffn2_w_in=(ffn2_w_in, m_ffn2_w_in, v_ffn2_w_in),
               ffn2_w_out=(ffn2_w_out, m_ffn2_w_out, v_ffn2_w_out))
    for n, (w, m, v) in big.items():
        res[n] = _adamw(w[0], m[0], v[0], recv[n], "adamw_" + n)
    sm_names = ("ada_b", "norm_ffn1", "norm_mix", "hgrn_lb", "hgrn_g", "conv_b", "conv_ln_g", "conv_ln_b",
                "norm_ffn2", "norm_final")
    sm_w = dict(ada_b=(ada_b, m_ada_b, v_ada_b), norm_ffn1=(norm_ffn1, m_norm_ffn1, v_norm_ffn1),
                norm_mix=(norm_mix, m_norm_mix, v_norm_mix), hgrn_lb=(hgrn_lb, m_hgrn_lb, v_hgrn_lb),
                hgrn_g=(hgrn_g, m_hgrn_g, v_hgrn_g), conv_b=(conv_b, m_conv_b, v_conv_b),
                conv_ln_g=(conv_ln_g, m_conv_ln_g, v_conv_ln_g), conv_ln_b=(conv_ln_b, m_conv_ln_b, v_conv_ln_b),
                norm_ffn2=(norm_ffn2, m_norm_ffn2, v_norm_ffn2), norm_final=(norm_final, m_norm_final, v_norm_final))
    sm_g = dict(gs, ada_b=g_ada_b, hgrn_lb=g_hgrn_lb)
    rows = {n: sm_w[n][0].size // D for n in sm_names}
    n_rows = sum(rows.values())
    pad = (-n_rows) % 8
    stack = lambda parts: jnp.concatenate([q.reshape(-1, D) for q in parts] + [jnp.ones((pad, D), F32)], axis=0)
    st = _adamw(stack([sm_w[n][0] for n in sm_names]), stack([sm_w[n][1] for n in sm_names]),
                stack([sm_w[n][2] for n in sm_names]), stack([sm_g[n] for n in sm_names]), "adamw_small")
    off = 0
    for n in sm_names:
        res[n] = tuple(t[off:off + rows[n]].reshape(sm_w[n][0].shape) for t in st)
        off += rows[n]

    order = ("ada_w", "ada_b", "norm_ffn1", "ffn1_w_in", "ffn1_w_out", "norm_mix", "mix_w_in", "hgrn_lb", "hgrn_g",
             "hgrn_w_o", "conv_w", "conv_b", "conv_ln_g", "conv_ln_b", "conv_w_o", "mix_w_out", "norm_ffn2",
             "ffn2_w_in", "ffn2_w_out", "norm_final")
    lead = lambda n, t: t[None] if n in big or n == "ada_w" else t
    outs = [loss, dx[None]]
    for j in range(4):
        outs += [lead(n, res[n][j]) for n in order]
    return tuple(outs)
```

```python
import functools

import jax
import jax.numpy as jnp
from jax import lax
from jax.experimental import pallas as pl
from jax.experimental.pallas import tpu as pltpu

F32 = jnp.float32
MM = jnp.bfloat16
ACT = jnp.bfloat16

D = 1024
D_FF = 2816
HEADS = 8
HD = 128
CHUNK = 64
SUB = 16
NSUB = CHUNK // SUB
CONV_K = 31
HALO = 32
EPS = 1e-6
N_DEV = 8
NEG = -1e30
Q_SCALE = HD ** -0.5

ADAM_LR = 0.001
ADAM_B1 = 0.9
ADAM_B2 = 0.999
ADAM_EPS = 1e-08
ADAM_WD = 0.01
ADAM_STEP = 10

VMEM_LIMIT = 60 * 1024 * 1024
MESH = pl.DeviceIdType.MESH


def _cparams(n_axes):
    return pltpu.CompilerParams(dimension_semantics=("arbitrary",) * n_axes, vmem_limit_bytes=VMEM_LIMIT)


def _mm(a, b):
    return lax.dot_general(a.astype(MM), b.astype(MM), (((1,), (0,)), ((), ())), preferred_element_type=F32)


def _mm_nt(a, b):
    return lax.dot_general(a.astype(MM), b.astype(MM), (((1,), (1,)), ((), ())), preferred_element_type=F32)


def _mm_tn(a, b):
    return lax.dot_general(a.astype(MM), b.astype(MM), (((0,), (0,)), ((), ())), preferred_element_type=F32)


def _sig(x):
    return 1.0 / (1.0 + jnp.exp(-x))


def _colsum(x):
    return jnp.sum(x, axis=0, keepdims=True)


def _rowmean(x):
    return jnp.mean(x, axis=-1, keepdims=True)


def _modnorm_fwd(xv, g, sh, sc):
    r = lax.rsqrt(_rowmean(xv * xv) + EPS)
    xh = xv * r
    n = xh * g
    return n * (1.0 + sc) + sh, xh, n, r


def _modnorm_bwd(dh, xh, n, r, g, sc):
    dsc = _colsum(dh * n)
    dsh = _colsum(dh)
    dn = dh * (1.0 + sc)
    dg = _colsum(dn * xh)
    dxh = dn * g
    dx = r * (dxh - xh * _rowmean(dxh * xh))
    return dx, dsh, dsc, dg


def _ffn_fwd(x, mod, mo, gnorm, w_in, w_out, res, name, carry):
    T = x.shape[0]
    tm = min(512, T)
    tn = D_FF // 2
    nj = D_FF // tn

    def body(x_ref, mod_ref, g_ref, wa_ref, wb_ref, wo_ref, xo_ref, a_ref, b_ref, f_ref, h_ref, acc_scr):
        j = pl.program_id(1)

        @pl.when(j == 0)
        def _():
            h, _, _, _ = _modnorm_fwd(x_ref[...], g_ref[...], mod_ref[mo:mo + 1, :], mod_ref[mo + 1:mo + 2, :])
            h_ref[...] = h.astype(ACT)
            acc_scr[...] = jnp.zeros_like(acc_scr)

        h = h_ref[...]
        a = _mm(h, wa_ref[...])
        b = _mm(h, wb_ref[...])
        a_ref[...] = a.astype(ACT)
        b_ref[...] = b.astype(ACT)
        s = a * _sig(a) * b
        acc_scr[...] += _mm(s, wo_ref[...])

        @pl.when(j == nj - 1)
        def _():
            f = acc_scr[...]
            f_ref[...] = f
            xo_ref[...] = x_ref[...] + res * mod_ref[mo + 2:mo + 3, :] * f

    out = _gridded(
        body, carry, name=name, grid=(T // tm, nj),
        in_specs=[
            pl.BlockSpec((tm, D), lambda i, j: (i, 0)),
            pl.BlockSpec((9, D), lambda i, j: (0, 0)),
            pl.BlockSpec((1, D), lambda i, j: (0, 0)),
            pl.BlockSpec((D, tn), lambda i, j: (0, j)),
            pl.BlockSpec((D, tn), lambda i, j: (0, j + nj)),
            pl.BlockSpec((tn, D), lambda i, j: (j, 0)),
        ],
        out_specs=[
            pl.BlockSpec((tm, D), lambda i, j: (i, 0)),
            pl.BlockSpec((tm, tn), lambda i, j: (i, j)),
            pl.BlockSpec((tm, tn), lambda i, j: (i, j)),
            pl.BlockSpec((tm, D), lambda i, j: (i, 0)),
            pl.BlockSpec((tm, D), lambda i, j: (i, 0)),
        ],
        out_shape=[
            jax.ShapeDtypeStruct((T, D), F32),
            jax.ShapeDtypeStruct((T, D_FF), ACT),
            jax.ShapeDtypeStruct((T, D_FF), ACT),
            jax.ShapeDtypeStruct((T, D), F32),
            jax.ShapeDtypeStruct((T, D), ACT),
        ],
        scratch_shapes=[pltpu.VMEM((tm, D), F32)],
    )(x, mod, gnorm, w_in, w_in, w_out)
    return out[:5], out[5:]


def _ffn_bwd_w(h, dxo, a, b, mod, mo, w_out, res, name, carry):
    T = h.shape[0]
    tm = min(1024, T)
    ni = T // tm
    tn = 256
    nj = D_FF // tn

    def body(h_ref, dxo_ref, a_ref, b_ref, mod_ref, wo_ref, da_ref, db_ref, dwa_ref, dwb_ref, dwo_ref,
             acc_a, acc_b, acc_o, df_all, h_all):
        j = pl.program_id(0)
        i = pl.program_id(1)

        @pl.when(i == 0)
        def _():
            acc_a[...] = jnp.zeros_like(acc_a)
            acc_b[...] = jnp.zeros_like(acc_b)
            acc_o[...] = jnp.zeros_like(acc_o)

        @pl.when(j == 0)
        def _():
            df_all[i] = (res * mod_ref[mo + 2:mo + 3, :] * dxo_ref[...]).astype(MM)
            h_all[i] = h_ref[...]

        hb = h_all[i]
        df = df_all[i]
        av = a_ref[...].astype(F32)
        bv = b_ref[...].astype(F32)
        sg = _sig(av)
        sa = av * sg
        s = (sa * bv).astype(MM)
        ds = _mm_nt(df, wo_ref[...])
        da = (ds * bv * sg * (1.0 + av * (1.0 - sg))).astype(MM)
        db = (ds * sa).astype(MM)
        da_ref[...] = da
        db_ref[...] = db
        acc_o[...] += _mm_tn(s, df)
        acc_a[...] += _mm_tn(hb, da)
        acc_b[...] += _mm_tn(hb, db)

        @pl.when(i == ni - 1)
        def _():
            dwa_ref[...] = acc_a[...].astype(MM)
            dwb_ref[...] = acc_b[...].astype(MM)
            dwo_ref[...] = acc_o[...].astype(MM)

    first = lambda j, i: (jnp.where(j == 0, i, ni - 1), 0)
    out = _gridded(
        body, carry, name=name, grid=(nj, ni),
        in_specs=[
            pl.BlockSpec((tm, D), first),
            pl.BlockSpec((tm, D), first),
            pl.BlockSpec((tm, tn), lambda j, i: (i, j)),
            pl.BlockSpec((tm, tn), lambda j, i: (i, j)),
            pl.BlockSpec((9, D), lambda j, i: (0, 0)),
            pl.BlockSpec((tn, D), lambda j, i: (j, 0)),
        ],
        out_specs=[
            pl.BlockSpec((tm, tn), lambda j, i: (i, j)),
            pl.BlockSpec((tm, tn), lambda j, i: (i, j)),
            pl.BlockSpec((D, tn), lambda j, i: (0, j)),
            pl.BlockSpec((D, tn), lambda j, i: (0, j)),
            pl.BlockSpec((tn, D), lambda j, i: (j, 0)),
        ],
        out_shape=[
            jax.ShapeDtypeStruct((T, D_FF), MM),
            jax.ShapeDtypeStruct((T, D_FF), MM),
            jax.ShapeDtypeStruct((D, D_FF), MM),
            jax.ShapeDtypeStruct((D, D_FF), MM),
            jax.ShapeDtypeStruct((D_FF, D), MM),
        ],
        scratch_shapes=[pltpu.VMEM((D, tn), F32), pltpu.VMEM((D, tn), F32), pltpu.VMEM((tn, D), F32),
                        pltpu.VMEM((ni, tm, D), MM), pltpu.VMEM((ni, tm, D), MM)],
    )(h, dxo, a, b, mod, w_out)
    return out[:5], out[5:]


def _ffn_bwd_x(x, dxo, f, da, db, mod, mo, gnorm, w_in, res, name, carry):
    T = x.shape[0]
    tm = min(512, T)
    ni = T // tm
    tn = D_FF // 2
    nj = D_FF // tn

    def body(x_ref, dxo_ref, f_ref, da_ref, db_ref, mod_ref, g_ref, wa_ref, wb_ref, dx_ref, sm_ref, dh_scr):
        j = pl.program_id(0)
        i = pl.program_id(1)

        @pl.when((j == 0) & (i == 0))
        def _():
            sm_ref[...] = jnp.zeros_like(sm_ref)

        @pl.when(j == 0)
        def _():
            dh_scr[i] = jnp.zeros((tm, D), F32)

        dh_scr[i] += _mm_nt(da_ref[...], wa_ref[...]) + _mm_nt(db_ref[...], wb_ref[...])

        @pl.when(j == nj - 1)
        def _():
            sc = mod_ref[mo + 1:mo + 2, :]
            _, xh, n, r = _modnorm_fwd(x_ref[...], g_ref[...], mod_ref[mo:mo + 1, :], sc)
            dxn, dsh, dsc, dg = _modnorm_bwd(dh_scr[i], xh, n, r, g_ref[...], sc)
            dxo_v = dxo_ref[...]
            dx_ref[...] = dxo_v + dxn
            sm_ref[0:1, :] += dsh
            sm_ref[1:2, :] += dsc
            sm_ref[2:3, :] += _colsum(dxo_v * f_ref[...]) * res
            sm_ref[3:4, :] += dg

    last = pl.BlockSpec((tm, D), lambda j, i: (jnp.where(j == nj - 1, i, 0), 0))
    out = _gridded(
        body, carry, name=name, grid=(nj, ni),
        in_specs=[last, last, last,
                  pl.BlockSpec((tm, tn), lambda j, i: (i, j)), pl.BlockSpec((tm, tn), lambda j, i: (i, j)),
                  pl.BlockSpec((9, D), lambda j, i: (0, 0)), pl.BlockSpec((1, D), lambda j, i: (0, 0)),
                  pl.BlockSpec((D, tn), lambda j, i: (0, j)), pl.BlockSpec((D, tn), lambda j, i: (0, j + nj))],
        out_specs=[last, pl.BlockSpec((8, D), lambda j, i: (0, 0))],
        out_shape=[jax.ShapeDtypeStruct((T, D), F32), jax.ShapeDtypeStruct((8, D), F32)],
        scratch_shapes=[pltpu.VMEM((ni, tm, D), F32)],
    )(x, dxo, f, da, db, mod, gnorm, w_in, w_in)
    return out[:2], out[2:]


def _head(x, target, gfin):
    T = x.shape[0]
    tm = min(512, T)
    ni = T // tm

    def body(x_ref, t_ref, g_ref, dx_ref, sm_ref):
        i = pl.program_id(0)

        @pl.when(i == 0)
        def _():
            sm_ref[...] = jnp.zeros_like(sm_ref)

        xv = x_ref[...]
        g = g_ref[...]
        r = lax.rsqrt(_rowmean(xv * xv) + EPS)
        xh = xv * r
        e = xh * g - t_ref[...]
        sm_ref[1:2, :] += _colsum(e * e) * (0.5 / D)
        dy = e * (1.0 / D)
        sm_ref[0:1, :] += _colsum(dy * xh)
        dxh = dy * g
        dx_ref[...] = r * (dxh - xh * _rowmean(dxh * xh))

        @pl.when(i == ni - 1)
        def _():
            sm_ref[1:2, :] = jnp.broadcast_to(jnp.sum(sm_ref[1:2, :], axis=-1, keepdims=True), (1, D))

    return pl.pallas_call(
        body, name="head_loss", grid=(ni,),
        in_specs=[pl.BlockSpec((tm, D), lambda i: (i, 0)), pl.BlockSpec((tm, D), lambda i: (i, 0)),
                  pl.BlockSpec((1, D), lambda i: (0, 0))],
        out_specs=[pl.BlockSpec((tm, D), lambda i: (i, 0)), pl.BlockSpec((8, D), lambda i: (0, 0))],
        out_shape=[jax.ShapeDtypeStruct((T, D), F32), jax.ShapeDtypeStruct((8, D), F32)],
        compiler_params=_cparams(1),
    )(x, target, gfin)


def _mixin_fwd(x, mod, mo, gnorm, w, carry):
    T = x.shape[0]
    tm = min(1024, T)
    ni = T // tm

    def body(x_ref, mod_ref, g_ref, w_ref, p_ref, h_ref, h_all):
        i = pl.program_id(1)

        @pl.when(pl.program_id(0) == 0)
        def _():
            h, _, _, _ = _modnorm_fwd(x_ref[...], g_ref[...], mod_ref[mo:mo + 1, :], mod_ref[mo + 1:mo + 2, :])
            h_all[i] = h.astype(ACT)
            h_ref[...] = h.astype(ACT)

        p_ref[0] = _mm(h_all[i], w_ref[0])

    first = lambda k, i: (jnp.where(k == 0, i, ni - 1), 0)
    out = _gridded(
        body, carry, name="mixin_fwd", grid=(8, ni),
        in_specs=[pl.BlockSpec((tm, D), first), pl.BlockSpec((9, D), lambda k, i: (0, 0)),
                  pl.BlockSpec((1, D), lambda k, i: (0, 0)), pl.BlockSpec((1, D, D), lambda k, i: (k, 0, 0))],
        out_specs=[pl.BlockSpec((1, tm, D), lambda k, i: (k, i, 0)), pl.BlockSpec((tm, D), first)],
        out_shape=[jax.ShapeDtypeStruct((8, T, D), F32), jax.ShapeDtypeStruct((T, D), ACT)],
        scratch_shapes=[pltpu.VMEM((ni, tm, D), ACT)],
    )(x, mod, gnorm, w)
    return out[:2], out[2:]


def _mixin_bwd(x, h, dxo, dp, mod, mo, gnorm, w, carry):
    T = x.shape[0]
    tm = min(512, T)
    ni = T // tm

    def body(x_ref, h_ref, dxo_ref, dp_ref, mod_ref, g_ref, w_ref, dx_ref, dw_ref, sm_ref, dh_scr, acc):
        k = pl.program_id(0)
        i = pl.program_id(1)

        @pl.when(i == 0)
        def _():
            acc[...] = jnp.zeros_like(acc)

        @pl.when(k == 0)
        def _():
            dh_scr[i] = jnp.zeros((tm, D), F32)

        @pl.when((k == 0) & (i == 0))
        def _():
            sm_ref[...] = jnp.zeros_like(sm_ref)

        dpk = dp_ref[0].astype(MM)
        acc[...] += _mm_tn(h_ref[...], dpk)
        dh_scr[i] += _mm_nt(dpk, w_ref[0])

        @pl.when(i == ni - 1)
        def _():
            dw_ref[0] = acc[...].astype(MM)

        @pl.when(k == 7)
        def _():
            sc = mod_ref[mo + 1:mo + 2, :]
            _, xh, n, r = _modnorm_fwd(x_ref[...], g_ref[...], mod_ref[mo:mo + 1, :], sc)
            dxn, dsh, dsc, dg = _modnorm_bwd(dh_scr[i], xh, n, r, g_ref[...], sc)
            dx_ref[...] = dxo_ref[...] + dxn
            sm_ref[0:1, :] += dsh
            sm_ref[1:2, :] += dsc
            sm_ref[3:4, :] += dg

    out = _gridded(
        body, carry, name="mixin_bwd", grid=(8, ni),
        in_specs=[pl.BlockSpec((tm, D), lambda k, i: (jnp.where(k == 7, i, 0), 0)),
                  pl.BlockSpec((tm, D), lambda k, i: (i, 0)),
                  pl.BlockSpec((tm, D), lambda k, i: (jnp.where(k == 7, i, 0), 0)),
                  pl.BlockSpec((1, tm, D), lambda k, i: (k, i, 0)), pl.BlockSpec((9, D), lambda k, i: (0, 0)),
                  pl.BlockSpec((1, D), lambda k, i: (0, 0)), pl.BlockSpec((1, D, D), lambda k, i: (k, 0, 0))],
        out_specs=[pl.BlockSpec((tm, D), lambda k, i: (jnp.where(k == 7, i, 0), 0)),
                   pl.BlockSpec((1, D, D), lambda k, i: (k, 0, 0)),
                   pl.BlockSpec((8, D), lambda k, i: (0, 0))],
        out_shape=[jax.ShapeDtypeStruct((T, D), F32), jax.ShapeDtypeStruct((8, D, D), MM),
                   jax.ShapeDtypeStruct((8, D), F32)],
        scratch_shapes=[pltpu.VMEM((ni, tm, D), F32), pltpu.VMEM((D, D), F32)],
    )(x, h, dxo, dp, mod, gnorm, w)
    return out[:3], out[3:]


def _hgrn_consts():
    rows = jnp.arange(SUB * HD) // HD
    e = (rows[:, None] == jnp.arange(HD)[None, :]).astype(MM)
    return e, e.T


def _rows_bcast(ref, cb, first, n):
    parts = [jnp.broadcast_to(ref[pl.ds(c * CHUNK + first, 1), :], (n, HD)) for c in range(cb // CHUNK)]
    return jnp.concatenate(parts, axis=0)


def _hgrn_pre(qr, fr, lb_ref, b_scr, cb):
    z = lb_ref[...]
    lb = _sig(z[0:1, :] - z[1:2, :])
    sq = _sig(qr)
    q = qr * sq * Q_SCALE
    sf = _sig(fr)
    fg = lb + (1.0 - lb) * sf
    lf = jnp.log(fg)
    k = 1.0 - fg
    tl = lax.broadcasted_iota(jnp.int32, (cb, HD), 0) % CHUNK
    bc = lf
    sh = 1
    while sh < CHUNK:
        bc = bc + jnp.where(tl >= sh, pltpu.roll(bc, sh, 0), 0.0)
        sh *= 2
    b_scr[...] = bc
    bl = _rows_bcast(b_scr, cb, CHUNK - 1, CHUNK)
    br = [None] + [_rows_bcast(b_scr, cb, SUB * i - 1, CHUNK) for i in range(1, NSUB)]
    sb = tl // SUB
    bref = jnp.where(sb == 0, bc, jnp.where(sb == 1, br[1], jnp.where(sb == 2, br[2], br[3])))
    eb = jnp.exp(bc)
    ekd = jnp.exp(bl - bc)
    eqo = jnp.exp(bc - bref)
    eko = [None] + [jnp.exp(jnp.where(tl < SUB * i, br[i] - bc, NEG)) for i in range(1, NSUB)]
    return dict(lb=lb, sq=sq, q=q, sf=sf, fg=fg, k=k, tl=tl, sb=sb, b=bc, bl=bl, eb=eb, ekd=ekd, eqo=eqo,
                eko=eko, qe=q * eb, kd=k * ekd, qo=q * eqo, ko=[None] + [k * eko[i] for i in range(1, NSUB)])


def _pad_rows(x):
    return jnp.concatenate([x, jnp.zeros_like(x)], axis=0)


def _by_subblock(sbc, parts):
    out = jnp.zeros_like(parts[1])
    for i in range(1, NSUB):
        out = jnp.where(sbc == i, parts[i], out)
    return out


def _hgrn_fwd(p, hgrn_lb, hgrn_g, carry):
    T = p.shape[1]
    cb = min(512, T)
    nch = cb // CHUNK
    ncb = T // cb
    e_mat, _ = _hgrn_consts()

    def body(p_ref, lb_ref, g_ref, e_ref, o_ref, oa_ref, a_ref, s_ref, st_scr, q_scr, k_scr, b_scr, z_scr):
        @pl.when(pl.program_id(1) == 0)
        def _():
            st_scr[...] = jnp.zeros_like(st_scr)

        v = p_ref[2]
        og = p_ref[3]
        pre = _hgrn_pre(p_ref[0], p_ref[1], lb_ref, b_scr, cb)
        q_scr[...] = pre["q"]
        k_scr[...] = pre["k"]
        ti = lax.broadcasted_iota(jnp.int32, (SUB, HD), 0)

        def zbody(c, carry):
            for i in range(NSUB):
                r0 = pl.multiple_of(c * CHUNK + SUB * i, SUB)
                qi = q_scr[pl.ds(r0, SUB), :]
                bi = b_scr[pl.ds(r0, SUB), :]
                for s in range(SUB):
                    krow = k_scr[pl.ds(r0 + s, 1), :]
                    brow = b_scr[pl.ds(r0 + s, 1), :]
                    if s < 8:
                        zz = qi * krow * jnp.exp(jnp.where(ti >= s, bi - brow, NEG))
                    else:
                        lo = qi[8:] * krow * jnp.exp(jnp.where(ti[8:] >= s, bi[8:] - brow, NEG))
                        zz = jnp.concatenate([jnp.zeros((8, HD), F32), lo], axis=0)
                    z_scr[i, pl.ds(pl.multiple_of(c * SUB, SUB), SUB), s * HD:(s + 1) * HD] = zz.astype(MM)
            return carry

        lax.fori_loop(0, nch, zbody, 0)
        adiag = [_mm(z_scr[i], e_ref[...]) for i in range(NSUB)]
        sbc = lax.broadcasted_iota(jnp.int32, (CHUNK, HD), 0) // SUB
        chunks = [slice(c * CHUNK, (c + 1) * CHUNK) for c in range(nch)]
        offs = [[_mm_nt(pre["qo"][rs], _pad_rows(pre["ko"][i][rs])) for i in range(1, NSUB)] for rs in chunks]
        kv = [_mm_tn(v[rs], pre["kd"][rs]) for rs in chunks]
        a_parts = []
        for c in range(nch):
            dparts = []
            for i in range(NSUB):
                blk = adiag[i][c * SUB:(c + 1) * SUB]
                dparts.append(blk if i == 0 else pltpu.roll(blk, SUB * i, 1))
            a_parts.append(_by_subblock(sbc, [None] + offs[c]) + jnp.concatenate(dparts, axis=0))
        a_ref[0] = jnp.concatenate(a_parts, axis=0)
        o_intra = [_mm(a_parts[c], _pad_rows(v[rs])) for c, rs in enumerate(chunks)]
        states = []
        st = st_scr[...]
        for c in range(nch):
            states.append(st)
            st = st * jnp.exp(b_scr[pl.ds(c * CHUNK + CHUNK - 1, 1), :]) + kv[c]
        st_scr[...] = st
        for c in range(nch):
            s_ref[0, c] = states[c]
        o = jnp.concatenate([o_intra[c] + _mm_nt(pre["qe"][rs], states[c]) for c, rs in enumerate(chunks)], axis=0)
        o_ref[...] = o
        on = o * lax.rsqrt(_rowmean(o * o) + EPS) * g_ref[...]
        oa_ref[...] = (on * og * _sig(og)).astype(ACT)

    out = _gridded(
        body, carry, name="hgrn_fwd", grid=(HEADS, ncb),
        in_specs=[pl.BlockSpec((4, cb, HD), lambda h, c: (0, c, h)),
                  pl.BlockSpec((2, HD), lambda h, c: (0, h)),
                  pl.BlockSpec((1, HD), lambda h, c: (0, h)),
                  pl.BlockSpec((SUB * HD, HD), lambda h, c: (0, 0))],
        out_specs=[pl.BlockSpec((cb, HD), lambda h, c: (c, h)),
                   pl.BlockSpec((cb, HD), lambda h, c: (c, h)),
                   pl.BlockSpec((1, cb, HD), lambda h, c: (h, c, 0)),
                   pl.BlockSpec((1, nch, HD, HD), lambda h, c: (h, c, 0, 0))],
        out_shape=[jax.ShapeDtypeStruct((T, D), F32), jax.ShapeDtypeStruct((T, D), ACT),
                   jax.ShapeDtypeStruct((HEADS, T, HD), F32),
                   jax.ShapeDtypeStruct((HEADS, T // CHUNK, HD, HD), F32)],
        scratch_shapes=[pltpu.VMEM((HD, HD), F32), pltpu.VMEM((cb, HD), F32), pltpu.VMEM((cb, HD), F32),
                        pltpu.VMEM((cb, HD), F32), pltpu.VMEM((NSUB, nch * SUB, SUB * HD), MM)],
    )(p, hgrn_lb, hgrn_g, e_mat)
    return out[:4], out[4:]


def _hgrn_bwd(p, o, a_all, s_all, doa, hgrn_lb, hgrn_g, dp, carry):
    T = p.shape[1]
    cb = min(512, T)
    nch = cb // CHUNK
    ncb = T // cb
    _, et_mat = _hgrn_consts()

    def body(p_ref, o_ref, a_ref, s_ref, doa_ref, lb_ref, g_ref, et_ref, dp_in, dp_ref, sm_ref,
             dst_scr, q_scr, k_scr, b_scr, x_scr, dqd_scr, dkd_scr):
        del dp_in

        @pl.when(pl.program_id(1) == 0)
        def _():
            dst_scr[...] = jnp.zeros_like(dst_scr)
            sm_ref[...] = jnp.zeros_like(sm_ref)

        qr = p_ref[0]
        v = p_ref[2]
        og = p_ref[3]
        pre = _hgrn_pre(qr, p_ref[1], lb_ref, b_scr, cb)
        q, k = pre["q"], pre["k"]
        q_scr[...] = q
        k_scr[...] = k
        g = g_ref[...]
        ov = o_ref[...]
        r = lax.rsqrt(_rowmean(ov * ov) + EPS)
        oh = ov * r
        sgo = _sig(og)
        doa_v = doa_ref[...]
        don = doa_v * og * sgo
        dog = doa_v * oh * g * sgo * (1.0 + og * (1.0 - sgo))
        sm_ref[1:2, :] += _colsum(don * oh)
        doh = don * g
        do = r * (doh - oh * _rowmean(doh * oh))

        sbc = lax.broadcasted_iota(jnp.int32, (CHUNK, HD), 0) // SUB
        row_i = lax.broadcasted_iota(jnp.int32, (CHUNK, HD), 0)
        lane_i = lax.broadcasted_iota(jnp.int32, (CHUNK, HD), 1)
        causal = lane_i <= row_i
        chunks = [slice(c * CHUNK, (c + 1) * CHUNK) for c in range(nch)]
        da_parts = [jnp.where(causal, _mm_nt(do[rs], _pad_rows(v[rs])), 0.0) for rs in chunks]
        dv_parts = [_mm_tn(a_ref[0, rs, :], do[rs])[:CHUNK] for rs in chunks]
        dqoff_mm = [[_mm(da_parts[c], _pad_rows(pre["ko"][i][rs])) for i in range(1, NSUB)]
                    for c, rs in enumerate(chunks)]
        dkoff_mm = [[_mm_tn(jnp.where(sbc == i, da_parts[c], 0.0), pre["qo"][rs])[:CHUNK] for i in range(1, NSUB)]
                    for c, rs in enumerate(chunks)]
        dqoff_parts = [_by_subblock(sbc, [None] + dqoff_mm[c]) for c in range(nch)]
        dkoff_parts = []
        for c, rs in enumerate(chunks):
            dko = pre["eko"][1][rs] * dkoff_mm[c][0]
            for i in range(2, NSUB):
                dko = dko + pre["eko"][i][rs] * dkoff_mm[c][i - 1]
            dkoff_parts.append(dko)
        for i in range(NSUB):
            rows = []
            for c in range(nch):
                blk = da_parts[c][SUB * i:SUB * (i + 1)]
                rows.append(blk if i == 0 else pltpu.roll(blk, HD - SUB * i, 1))
            x_scr[i] = _mm(jnp.concatenate(rows, axis=0), et_ref[...])
        ti = lax.broadcasted_iota(jnp.int32, (SUB, HD), 0)

        def dbody(c, carry):
            for i in range(NSUB):
                r0 = pl.multiple_of(c * CHUNK + SUB * i, SUB)
                qi = q_scr[pl.ds(r0, SUB), :]
                bi = b_scr[pl.ds(r0, SUB), :]
                dq_hi = jnp.zeros((8, HD), F32)
                dq_lo = jnp.zeros((8, HD), F32)
                dk_hi = jnp.zeros((8, HD), F32)
                dk_lo = jnp.zeros((8, HD), F32)
                c0 = pl.multiple_of(c * SUB, SUB)
                t8 = ti[:8]
                for s in range(SUB):
                    krow = k_scr[pl.ds(r0 + s, 1), :]
                    brow = b_scr[pl.ds(r0 + s, 1), :]
                    w_lo = (x_scr[i, pl.ds(c0 + 8, 8), s * HD:(s + 1) * HD]
                            * jnp.exp(jnp.where(t8 + 8 >= s, bi[8:] - brow, NEG)))
                    dq_lo = dq_lo + w_lo * krow
                    col = _colsum(w_lo * qi[8:])
                    if s < 8:
                        w_hi = (x_scr[i, pl.ds(c0, 8), s * HD:(s + 1) * HD]
                                * jnp.exp(jnp.where(t8 >= s, bi[:8] - brow, NEG)))
                        dq_hi = dq_hi + w_hi * krow
                        dk_hi = jnp.where(t8 == s, col + _colsum(w_hi * qi[:8]), dk_hi)
                    else:
                        dk_lo = jnp.where(t8 + 8 == s, col, dk_lo)
                dqd_scr[pl.ds(r0, SUB), :] = jnp.concatenate([dq_hi, dq_lo], axis=0)
                dkd_scr[pl.ds(r0, SUB), :] = jnp.concatenate([dk_hi, dk_lo], axis=0)
            return carry

        lax.fori_loop(0, nch, dbody, 0)
        qdo = [_mm_tn(do[rs], pre["qe"][rs]) for rs in chunks]
        dsts = [None] * nch
        dst = dst_scr[...]
        for c in reversed(range(nch)):
            dsts[c] = dst
            dst = dst * jnp.exp(b_scr[pl.ds(c * CHUNK + CHUNK - 1, 1), :]) + qdo[c]
        dst_scr[...] = dst
        sts = [s_ref[0, c] for c in range(nch)]
        dqe_parts = [_mm(do[rs], sts[c]) for c, rs in enumerate(chunks)]
        dkdec_parts = [_mm(v[rs], dsts[c]) for c, rs in enumerate(chunks)]
        dvi_parts = [_mm_nt(pre["kd"][rs], dsts[c]) for c, rs in enumerate(chunks)]
        debl_parts = [_colsum(dsts[c] * sts[c]) for c in range(nch)]
        dqe = jnp.concatenate(dqe_parts, axis=0)
        dkdec = jnp.concatenate(dkdec_parts, axis=0)
        dq_tot = jnp.concatenate(dqoff_parts, axis=0) * pre["eqo"] + dqd_scr[...] + dqe * pre["eb"]
        dk_inter = dkdec * pre["ekd"]
        dk_tot = jnp.concatenate(dkoff_parts, axis=0) + dkd_scr[...] + dk_inter
        db = q * dq_tot - k * dk_tot
        kdk = k * dk_inter
        dbl = jnp.concatenate(
            [jnp.broadcast_to(jnp.exp(b_scr[pl.ds(c * CHUNK + CHUNK - 1, 1), :]) * debl_parts[c]
                              + _colsum(kdk[c * CHUNK:(c + 1) * CHUNK]), (CHUNK, HD)) for c in range(nch)], axis=0)
        tl = pre["tl"]
        rc = db
        sh = 1
        while sh < CHUNK:
            rc = rc + jnp.where(tl + sh < CHUNK, pltpu.roll(rc, cb - sh, 0), 0.0)
            sh *= 2
        dlf = rc + dbl
        dfg = dlf / pre["fg"] - dk_tot
        sf = pre["sf"]
        lb = pre["lb"]
        sm_ref[0:1, :] += _colsum(dfg * (1.0 - sf))
        sq = pre["sq"]
        dp_ref[0] = (dq_tot * Q_SCALE * sq * (1.0 + qr * (1.0 - sq))).astype(ACT)
        dp_ref[1] = (dfg * (1.0 - lb) * sf * (1.0 - sf)).astype(ACT)
        dp_ref[2] = (jnp.concatenate(dv_parts, axis=0) + jnp.concatenate(dvi_parts, axis=0)).astype(ACT)
        dp_ref[3] = dog.astype(ACT)

    rev = lambda c: ncb - 1 - c
    out = _gridded(
        body, carry, name="hgrn_bwd", grid=(HEADS, ncb),
        in_specs=[pl.BlockSpec((4, cb, HD), lambda h, c: (0, rev(c), h)),
                  pl.BlockSpec((cb, HD), lambda h, c: (rev(c), h)),
                  pl.BlockSpec((1, cb, HD), lambda h, c: (h, rev(c), 0)),
                  pl.BlockSpec((1, nch, HD, HD), lambda h, c: (h, rev(c), 0, 0)),
                  pl.BlockSpec((cb, HD), lambda h, c: (rev(c), h)),
                  pl.BlockSpec((2, HD), lambda h, c: (0, h)),
                  pl.BlockSpec((1, HD), lambda h, c: (0, h)),
                  pl.BlockSpec((HD, SUB * HD), lambda h, c: (0, 0)),
                  pl.BlockSpec(memory_space=pl.ANY)],
        out_specs=[pl.BlockSpec((4, cb, HD), lambda h, c: (0, rev(c), h)),
                   pl.BlockSpec((8, HD), lambda h, c: (0, h))],
        out_shape=[jax.ShapeDtypeStruct(dp.shape, dp.dtype), jax.ShapeDtypeStruct((8, D), F32)],
        aliases={8: 0},
        scratch_shapes=[pltpu.VMEM((HD, HD), F32), pltpu.VMEM((cb, HD), F32), pltpu.VMEM((cb, HD), F32),
                        pltpu.VMEM((cb, HD), F32), pltpu.VMEM((NSUB, nch * SUB, SUB * HD), F32),
                        pltpu.VMEM((cb, HD), F32), pltpu.VMEM((cb, HD), F32)],
    )(p, o, a_all, s_all, doa, hgrn_lb, hgrn_g, et_mat, dp)
    return out[:2], out[2:]


def _ln_fwd(u1, g, b):
    mu = _rowmean(u1)
    xc = u1 - mu
    rs = lax.rsqrt(_rowmean(xc * xc) + EPS)
    xh = xc * rs
    return xh * g + b, xh, rs


CONV_RB = 64
LANES = 128


def _shift_rows(src, sh, ls, n):
    for r in range(1, 8):
        sh[r - 1, 0:n, :] = src[pl.ds(r, n), ls]


def _tap(src, sh, ls, off, r0, rows):
    r = off % 8
    if r == 0:
        return src[pl.ds(r0 + off, rows), ls]
    return sh[r - 1, pl.ds(r0 + off - r, rows), :]


def _conv_fwd(p, cw, cb_, lng, lnb):
    T = p.shape[1]
    tm = min(512, T)
    n = HALO + tm - 8

    def body(p_ref, cw_ref, cb_ref, g_ref, b_ref, u1_ref, u2_ref, buf, sh):
        @pl.when(pl.program_id(0) == 0)
        def _():
            buf[0:HALO, :] = jnp.zeros((HALO, D), F32)

        buf[HALO:HALO + tm, :] = p_ref[0] * _sig(p_ref[1])
        for lb in range(D // LANES):
            ls = slice(lb * LANES, (lb + 1) * LANES)
            _shift_rows(buf, sh, ls, n)
            taps = [cw_ref[j:j + 1, ls] for j in range(CONV_K)]
            bias = cb_ref[:, ls]

            def rows_body(rb, carry):
                r0 = pl.multiple_of(rb * CONV_RB, CONV_RB)
                acc = jnp.broadcast_to(bias, (CONV_RB, LANES))
                for j in range(CONV_K):
                    acc = acc + taps[j] * _tap(buf, sh, ls, HALO - (CONV_K - 1) + j, r0, CONV_RB)
                u1_ref[pl.ds(r0, CONV_RB), ls] = acc
                return carry

            lax.fori_loop(0, tm // CONV_RB, rows_body, 0)
        y, _, _ = _ln_fwd(u1_ref[...], g_ref[...], b_ref[...])
        u2_ref[...] = (y * _sig(y)).astype(ACT)
        buf[0:HALO, :] = buf[tm:tm + HALO, :]

    return pl.pallas_call(
        body, name="conv_fwd", grid=(T // tm,),
        in_specs=[pl.BlockSpec((2, tm, D), lambda i: (2, i, 0)), pl.BlockSpec((HALO, D), lambda i: (0, 0)),
                  pl.BlockSpec((1, D), lambda i: (0, 0)), pl.BlockSpec((1, D), lambda i: (0, 0)),
                  pl.BlockSpec((1, D), lambda i: (0, 0))],
        out_specs=[pl.BlockSpec((tm, D), lambda i: (i, 0)), pl.BlockSpec((tm, D), lambda i: (i, 0))],
        out_shape=[jax.ShapeDtypeStruct((T, D), F32), jax.ShapeDtypeStruct((T, D), ACT)],
        scratch_shapes=[pltpu.VMEM((HALO + tm, D), F32), pltpu.VMEM((7, n, LANES), F32)],
        compiler_params=_cparams(1),
    )(p, cw, cb_, lng, lnb)


def _conv_bwd(p, u1, du2, cw, lng, lnb, dp):
    T = p.shape[1]
    tm = min(512, T)
    ni = T // tm
    hb = tm // HALO

    n = HALO + tm - 8

    def body(p_ref, ph_ref, u1_ref, du2_ref, cw_ref, g_ref, b_ref, dp_in, dp_ref, dcw_ref, sm_ref, ubuf, dbuf,
             sh, dacc):
        del dp_in
        step = pl.program_id(0)

        @pl.when(step == 0)
        def _():
            dbuf[tm:tm + HALO, :] = jnp.zeros((HALO, D), F32)
            dcw_ref[...] = jnp.zeros_like(dcw_ref)
            sm_ref[...] = jnp.zeros_like(sm_ref)

        ua = p_ref[0]
        sgb = _sig(p_ref[1])
        halo = ph_ref[0] * _sig(ph_ref[1])
        ubuf[0:HALO, :] = jnp.where(step == ni - 1, 0.0, halo)
        ubuf[HALO:HALO + tm, :] = ua * sgb
        g = g_ref[...]
        y, xh, rs = _ln_fwd(u1_ref[...], g, b_ref[...])
        sy = _sig(y)
        dy = du2_ref[...] * sy * (1.0 + y * (1.0 - sy))
        sm_ref[1:2, :] += _colsum(dy * xh)
        sm_ref[2:3, :] += _colsum(dy)
        dxh = dy * g
        du1 = rs * (dxh - _rowmean(dxh) - xh * _rowmean(dxh * xh))
        sm_ref[0:1, :] += _colsum(du1)
        dbuf[0:tm, :] = du1
        for lb in range(D // LANES):
            ls = slice(lb * LANES, (lb + 1) * LANES)
            taps = [cw_ref[j:j + 1, ls] for j in range(CONV_K)]
            _shift_rows(dbuf, sh, ls, n)

            def du0_body(rb, carry):
                r0 = pl.multiple_of(rb * CONV_RB, CONV_RB)
                acc = jnp.zeros((CONV_RB, LANES), F32)
                for j in range(CONV_K):
                    acc = acc + taps[j] * _tap(dbuf, sh, ls, CONV_K - 1 - j, r0, CONV_RB)
                dp_ref[0, pl.ds(r0, CONV_RB), ls] = acc.astype(ACT)
                return carry

            lax.fori_loop(0, tm // CONV_RB, du0_body, 0)
            _shift_rows(ubuf, sh, ls, n)
            dacc[...] = jnp.zeros_like(dacc)

            def dcw_body(rb, carry):
                r0 = pl.multiple_of(rb * CONV_RB, CONV_RB)
                d = dbuf[pl.ds(r0, CONV_RB), ls]
                for j in range(CONV_K):
                    prod = d * _tap(ubuf, sh, ls, HALO - (CONV_K - 1) + j, r0, CONV_RB)
                    dacc[8 * j:8 * j + 8, :] += jnp.sum(prod.reshape(CONV_RB // 8, 8, LANES), axis=0)
                return carry

            lax.fori_loop(0, tm // CONV_RB, dcw_body, 0)
            for j in range(CONV_K):
                dcw_ref[j:j + 1, ls] += _colsum(dacc[8 * j:8 * j + 8, :])
        du0 = dp_ref[0].astype(F32)
        dp_ref[0] = (du0 * sgb).astype(ACT)
        dp_ref[1] = (du0 * ua * sgb * (1.0 - sgb)).astype(ACT)
        dbuf[tm:tm + HALO, :] = dbuf[0:HALO, :]

    rev = lambda i: ni - 1 - i
    return pl.pallas_call(
        body, name="conv_bwd", grid=(ni,),
        in_specs=[pl.BlockSpec((2, tm, D), lambda i: (2, rev(i), 0)),
                  pl.BlockSpec((2, HALO, D), lambda i: (2, jnp.maximum(rev(i) * hb - 1, 0), 0)),
                  pl.BlockSpec((tm, D), lambda i: (rev(i), 0)), pl.BlockSpec((tm, D), lambda i: (rev(i), 0)),
                  pl.BlockSpec((HALO, D), lambda i: (0, 0)), pl.BlockSpec((1, D), lambda i: (0, 0)),
                  pl.BlockSpec((1, D), lambda i: (0, 0)), pl.BlockSpec(memory_space=pl.ANY)],
        out_specs=[pl.BlockSpec((2, tm, D), lambda i: (2, rev(i), 0)),
                   pl.BlockSpec((HALO, D), lambda i: (0, 0)), pl.BlockSpec((8, D), lambda i: (0, 0))],
        out_shape=[jax.ShapeDtypeStruct(dp.shape, dp.dtype), jax.ShapeDtypeStruct((HALO, D), F32),
                   jax.ShapeDtypeStruct((8, D), F32)],
        input_output_aliases={7: 0},
        scratch_shapes=[pltpu.VMEM((HALO + tm, D), F32), pltpu.VMEM((tm + HALO, D), F32),
                        pltpu.VMEM((7, n, LANES), F32), pltpu.VMEM((8 * CONV_K, LANES), F32)],
        compiler_params=_cparams(1),
    )(p, p, u1, du2, cw, lng, lnb, dp)


def _mixout_fwd(x, oa, u2, p, mod, mo, w_a, w_b, w_o):
    T = x.shape[0]
    tm = min(512, T)

    def body(x_ref, oa_ref, u2_ref, p_ref, mod_ref, wa_ref, wb_ref, wo_ref, xo_ref, ya_ref, yb_ref, mo_ref):
        ya = _mm(oa_ref[...], wa_ref[...])
        yb = _mm(u2_ref[...], wb_ref[...])
        ya_ref[...] = ya.astype(ACT)
        yb_ref[...] = yb.astype(ACT)
        merged = _sig(p_ref[0]) * ya + _sig(p_ref[1]) * yb
        out = _mm(merged, wo_ref[...])
        mo_ref[...] = out
        xo_ref[...] = x_ref[...] + mod_ref[mo + 2:mo + 3, :] * out

    tile = pl.BlockSpec((tm, D), lambda i: (i, 0))
    wspec = pl.BlockSpec((D, D), lambda i: (0, 0))
    return pl.pallas_call(
        body, name="mixout_fwd", grid=(T // tm,),
        in_specs=[tile, tile, tile, pl.BlockSpec((2, tm, D), lambda i: (3, i, 0)),
                  pl.BlockSpec((9, D), lambda i: (0, 0)), wspec, wspec, wspec],
        out_specs=[tile, tile, tile, tile],
        out_shape=[jax.ShapeDtypeStruct((T, D), F32), jax.ShapeDtypeStruct((T, D), ACT),
                   jax.ShapeDtypeStruct((T, D), ACT), jax.ShapeDtypeStruct((T, D), F32)],
        compiler_params=_cparams(1),
    )(x, oa, u2, p, mod, w_a, w_b, w_o)


def _mixout_bwd(dxo, oa, u2, ya, yb, mout, p, mod, mo, w_a, w_b, w_o):
    T = dxo.shape[0]
    tm = min(256, T)

    def body(dxo_ref, oa_ref, u2_ref, ya_ref, yb_ref, mo_ref, p_ref, mod_ref, wa_ref, wb_ref, wo_ref,
             dp_ref, doa_ref, du2_ref, dwa_ref, dwb_ref, dwo_ref, sm_ref):
        @pl.when(pl.program_id(0) == 0)
        def _():
            dwa_ref[...] = jnp.zeros_like(dwa_ref)
            dwb_ref[...] = jnp.zeros_like(dwb_ref)
            dwo_ref[...] = jnp.zeros_like(dwo_ref)
            sm_ref[...] = jnp.zeros_like(sm_ref)

        dxo_v = dxo_ref[...]
        sm_ref[2:3, :] += _colsum(dxo_v * mo_ref[...])
        dmo = (mod_ref[mo + 2:mo + 3, :] * dxo_v).astype(MM)
        ya = ya_ref[...].astype(F32)
        yb = yb_ref[...].astype(F32)
        sga = _sig(p_ref[0])
        sgb = _sig(p_ref[1])
        merged = (sga * ya + sgb * yb).astype(MM)
        dwo_ref[...] += _mm_tn(merged, dmo)
        dmg = _mm_nt(dmo, wo_ref[...])
        dp_ref[0] = (dmg * ya * sga * (1.0 - sga)).astype(ACT)
        dp_ref[1] = (dmg * yb * sgb * (1.0 - sgb)).astype(ACT)
        dya = (dmg * sga).astype(MM)
        dyb = (dmg * sgb).astype(MM)
        dwa_ref[...] += _mm_tn(oa_ref[...], dya)
        dwb_ref[...] += _mm_tn(u2_ref[...], dyb)
        doa_ref[...] = _mm_nt(dya, wa_ref[...])
        du2_ref[...] = _mm_nt(dyb, wb_ref[...])

    tile = pl.BlockSpec((tm, D), lambda i: (i, 0))
    wspec = pl.BlockSpec((D, D), lambda i: (0, 0))
    return pl.pallas_call(
        body, name="mixout_bwd", grid=(T // tm,),
        in_specs=[tile, tile, tile, tile, tile, tile, pl.BlockSpec((2, tm, D), lambda i: (3, i, 0)),
                  pl.BlockSpec((9, D), lambda i: (0, 0)), wspec, wspec, wspec],
        out_specs=[pl.BlockSpec((2, tm, D), lambda i: (3, i, 0)), tile, tile, wspec, wspec, wspec,
                   pl.BlockSpec((8, D), lambda i: (0, 0))],
        out_shape=[jax.ShapeDtypeStruct((8, T, D), ACT), jax.ShapeDtypeStruct((T, D), F32),
                   jax.ShapeDtypeStruct((T, D), F32), jax.ShapeDtypeStruct((D, D), F32),
                   jax.ShapeDtypeStruct((D, D), F32), jax.ShapeDtypeStruct((D, D), F32),
                   jax.ShapeDtypeStruct((8, D), F32)],
        compiler_params=_cparams(1),
    )(dxo, oa, u2, ya, yb, mout, p, mod, w_a, w_b, w_o)


def _ada_fwd(cs_all, ada_w, ada_b_cols):
    def body(cs_ref, w_ref, b_ref, out_ref):
        out_ref[...] = jnp.dot(cs_ref[...], w_ref[...], preferred_element_type=F32,
                               precision=lax.Precision.HIGHEST) + b_ref[...]

    return pl.pallas_call(
        body, name="ada_fwd", out_shape=jax.ShapeDtypeStruct((N_DEV, ada_w.shape[1]), F32),
        compiler_params=pltpu.CompilerParams(vmem_limit_bytes=VMEM_LIMIT),
    )(cs_all, ada_w, ada_b_cols)


def _ada_wgrad(cs_all, dmod_cols):
    cs_t = jnp.pad(cs_all.T, ((0, 0), (0, HD - N_DEV)))
    dm = jnp.pad(dmod_cols, ((0, HD - N_DEV), (0, 0)))

    def body(cs_ref, d_ref, out_ref):
        out_ref[...] = jnp.dot(cs_ref[...], d_ref[...], preferred_element_type=F32,
                               precision=lax.Precision.HIGHEST)

    return pl.pallas_call(
        body, name="ada_wgrad", out_shape=jax.ShapeDtypeStruct((D, dmod_cols.shape[1]), F32),
        compiler_params=pltpu.CompilerParams(vmem_limit_bytes=VMEM_LIMIT),
    )(cs_t, dm)


def _adam_math(w, g, m, v):
    m2 = ADAM_B1 * m + (1.0 - ADAM_B1) * g
    v2 = ADAM_B2 * v + (1.0 - ADAM_B2) * (g * g)
    m_hat = m2 / (1.0 - ADAM_B1 ** ADAM_STEP)
    v_hat = v2 / (1.0 - ADAM_B2 ** ADAM_STEP)
    delta = -ADAM_LR * (m_hat / (jnp.sqrt(v_hat) + ADAM_EPS) + ADAM_WD * w)
    return delta, m2, v2


def _adamw(w, m, v, g, name):
    R, C = w.shape
    slots = g.ndim == 3
    n_slots = g.shape[0] if slots else 0
    tr = R
    for cand in (256, 176):
        if R % cand == 0 and R > cand:
            tr = cand
            break

    def body(w_ref, m_ref, v_ref, g_ref, go_ref, d_ref, mo_ref, vo_ref):
        if slots:
            gv = g_ref[0].astype(F32)
            for s in range(1, n_slots):
                gv = gv + g_ref[s].astype(F32)
        else:
            gv = g_ref[...]
        go_ref[...] = gv
        d_ref[...], mo_ref[...], vo_ref[...] = _adam_math(w_ref[...], gv, m_ref[...], v_ref[...])

    tile = pl.BlockSpec((tr, C), lambda i: (i, 0))
    gspec = pl.BlockSpec((n_slots, tr, C), lambda i: (0, i, 0)) if slots else tile
    sds = jax.ShapeDtypeStruct((R, C), F32)
    return pl.pallas_call(
        body, name=name, grid=(R // tr,), in_specs=[tile, tile, tile, gspec], out_specs=[tile] * 4,
        out_shape=[sds] * 4, compiler_params=_cparams(1),
    )(w, m, v, g)


def _sum_slots(pack):
    def body(p_ref, out_ref):
        acc = p_ref[0]
        for s in range(1, N_DEV):
            acc = acc + p_ref[s]
        out_ref[...] = acc

    return pl.pallas_call(body, name="sum_small", out_shape=jax.ShapeDtypeStruct(pack.shape[1:], F32))(pack)


def _me():
    return lax.axis_index("x"), lax.axis_index("y"), lax.axis_index("c")


def _peer(r):
    x, y, c = _me()
    px = 1 - x if r & 4 else x
    py = 1 - y if r & 2 else y
    pc = 1 - c if r & 1 else c
    return (px, py, pc), 4 * px + 2 * py + pc


def _allgather_small(x):
    R, C = x.shape

    def body(x_ref, out_ref, send_sems, recv_sems):
        mx, my, mc = _me()
        me = 4 * mx + 2 * my + mc
        mine = out_ref.at[pl.ds(pl.multiple_of(me * R, 8), R), :]
        copies = []
        for r in range(1, N_DEV):
            dev, _ = _peer(r)
            copies.append(pltpu.make_async_remote_copy(
                src_ref=x_ref, dst_ref=mine, send_sem=send_sems.at[r - 1], recv_sem=recv_sems.at[r - 1],
                device_id=dev, device_id_type=MESH))
        for cp in copies:
            cp.start()
        mine[...] = x_ref[...]
        for r in range(1, N_DEV):
            _, idx = _peer(r)
            theirs = out_ref.at[pl.ds(pl.multiple_of(idx * R, 8), R), :]
            pltpu.make_async_remote_copy(
                src_ref=x_ref, dst_ref=theirs, send_sem=send_sems.at[r - 1], recv_sem=recv_sems.at[r - 1],
                device_id=_peer(r)[0], device_id_type=MESH).wait_recv()
        for cp in copies:
            cp.wait_send()

    return pl.pallas_call(
        body, name="allgather_small_%dx%d" % (R, C),
        out_shape=jax.ShapeDtypeStruct((N_DEV * R, C), F32),
        in_specs=[pl.BlockSpec(memory_space=pltpu.VMEM)], out_specs=pl.BlockSpec(memory_space=pltpu.VMEM),
        scratch_shapes=[pltpu.SemaphoreType.DMA((N_DEV - 1,)), pltpu.SemaphoreType.DMA((N_DEV - 1,))],
    )(x)


N_CHIP = N_DEV // 2


def _xchg_copies(ins, outs, sems, mode):
    send_sems, recv_sems, local_sems = sems
    mx, my, mc = _me()
    me = 4 * mx + 2 * my + mc
    my_chip = 2 * mx + my
    sibling = _peer(1)[0]

    def rdma(a, r, dev, src, slot):
        k = a * (N_DEV - 1) + r - 1
        return pltpu.make_async_remote_copy(
            src_ref=src, dst_ref=outs[a].at[slot], send_sem=send_sems.at[k], recv_sem=recv_sems.at[k],
            device_id=dev, device_id_type=MESH)

    own, sends, relays, recvs = [], [], [], []
    for a in range(len(ins)):
        if mode == "pair":
            for chip in range(N_CHIP):
                src = ins[a].at[2 * chip + 1 - mc]
                sends.append(rdma(a, chip + 1, sibling, src, chip))
                recvs.append(rdma(a, chip + 1, sibling, src, chip))
            continue
        if mode == "quad":
            own.append(pltpu.make_async_copy(ins[a].at[my_chip], outs[a].at[my_chip], local_sems.at[a]))
            for r in (2, 4, 6):
                dev, idx = _peer(r)
                chip = idx // 2
                sends.append(rdma(a, r, dev, ins[a].at[chip], my_chip))
                recvs.append(rdma(a, r, dev, ins[a].at[chip], chip))
            continue
        gather = mode == "gather"
        own.append(pltpu.make_async_copy(ins[a] if gather else ins[a].at[me], outs[a].at[me], local_sems.at[a]))
        for r in range(1, N_DEV):
            dev, idx = _peer(r)
            if not gather:
                sends.append(rdma(a, r, dev, ins[a].at[idx], me))
                recvs.append(rdma(a, r, dev, ins[a].at[idx], idx))
            elif r == 1:
                sends.append(rdma(a, r, dev, ins[a], me))
                recvs.append(rdma(a, r, dev, ins[a], idx))
            elif r % 2 == 0:
                sends.append(rdma(a, r, dev, ins[a], me))
                relays.append((rdma(a, r, dev, ins[a], idx), rdma(a, r + 1, sibling, outs[a].at[idx], idx)))
            else:
                recvs.append(rdma(a, r, sibling, ins[a], idx))
    return own, sends, relays, recvs


def _xchg_start(ins, outs, sems, mode):
    own, sends, _, _ = _xchg_copies(ins, outs, sems, mode)
    for cp in own + sends:
        cp.start()


def _xchg_wait(ins, outs, sems, mode):
    own, sends, relays, recvs = _xchg_copies(ins, outs, sems, mode)
    for arrival, relay in relays:
        arrival.wait_recv()
        relay.start()
    for cp in recvs:
        cp.wait_recv()
    for cp in own:
        cp.wait()
    for cp in sends + [relay for _, relay in relays]:
        cp.wait_send()


def _xchg_specs(arrays, mode):
    n = len(arrays)
    shape = {"gather": lambda s: (N_DEV,) + s, "scatter": lambda s: s, "pair": lambda s: (N_CHIP,) + s[1:],
             "quad": lambda s: s}[mode]
    out_shape = [jax.ShapeDtypeStruct(shape(a.shape), a.dtype) for a in arrays]
    sems = [pltpu.SemaphoreType.DMA((n * (N_DEV - 1),)), pltpu.SemaphoreType.DMA((n * (N_DEV - 1),)),
            pltpu.SemaphoreType.DMA((n,))]
    return out_shape, sems


def _exchange(arrays, mode, name):
    n = len(arrays)

    def body(*refs):
        _xchg_start(refs[:n], refs[n:2 * n], refs[2 * n:], mode)
        _xchg_wait(refs[:n], refs[n:2 * n], refs[2 * n:], mode)

    out_shape, sems = _xchg_specs(arrays, mode)
    return pl.pallas_call(
        body, name=name, out_shape=out_shape,
        in_specs=[pl.BlockSpec(memory_space=pl.ANY)] * n, out_specs=[pl.BlockSpec(memory_space=pl.ANY)] * n,
        scratch_shapes=sems,
    )(*arrays)


def _gridded(body, carry, *, name, grid, in_specs, out_specs, out_shape, scratch_shapes=(), aliases=None):
    if carry is None:
        return pl.pallas_call(
            body, name=name, grid=grid, in_specs=list(in_specs), out_specs=list(out_specs),
            out_shape=list(out_shape), scratch_shapes=list(scratch_shapes), input_output_aliases=aliases or {},
            compiler_params=_cparams(len(grid)))
    arrays, mode = carry
    n, n_in, n_out, n_scr = len(arrays), len(in_specs), len(out_specs), len(scratch_shapes)
    c_shape, c_sems = _xchg_specs(arrays, mode)

    def wrapped(*refs):
        ins, cin = refs[:n_in], refs[n_in:n_in + n]
        o0 = n_in + n
        outs, cout = refs[o0:o0 + n_out], refs[o0 + n_out:o0 + n_out + n]
        s0 = o0 + n_out + n
        scr, sems = refs[s0:s0 + n_scr], refs[s0 + n_scr:]
        first = pl.program_id(0) == 0
        last = pl.program_id(0) == grid[0] - 1
        for ax in range(1, len(grid)):
            first = first & (pl.program_id(ax) == 0)
            last = last & (pl.program_id(ax) == grid[ax] - 1)

        @pl.when(first)
        def _():
            _xchg_start(cin, cout, sems, mode)

        body(*ins, *outs, *scr)

        @pl.when(last)
        def _():
            _xchg_wait(cin, cout, sems, mode)

    hbm = pl.BlockSpec(memory_space=pl.ANY)
    res = pl.pallas_call(
        wrapped, name=name, grid=grid, in_specs=list(in_specs) + [hbm] * n, out_specs=list(out_specs) + [hbm] * n,
        out_shape=list(out_shape) + c_shape, scratch_shapes=list(scratch_shapes) + c_sems,
        input_output_aliases=aliases or {}, compiler_params=_cparams(len(grid)),
    )
    return lambda *args: res(*args, *arrays)


def _local_step(x, target, mod, small, sh):
    w1_in, w1_out = _exchange([sh["ffn1_w_in"], sh["ffn1_w_out"]], "gather", "allgather_ffn1")
    w1_in, w1_out = _full_w_in(w1_in), w1_out.reshape(D_FF, D)
    (x1, a1, b1, f1, h1), (wm_in,) = _ffn_fwd(x, mod, 0, small["norm_ffn1"], w1_in, w1_out, 0.5, "ffn1_fwd",
                                              ([sh["mix_w_in"]], "gather"))
    (p, h2), (wh_o, wc_o, wm_o, cw) = _mixin_fwd(
        x1, mod, 3, small["norm_mix"], wm_in,
        ([sh["hgrn_w_o"], sh["conv_w_o"], sh["mix_w_out"], sh["conv_w"]], "gather"))
    wh_o, wc_o, wm_o = wh_o.reshape(D, D), wc_o.reshape(D, D), wm_o.reshape(D, D)
    cw = jnp.pad(cw.transpose(1, 0, 2).reshape(CONV_K, D), ((0, HALO - CONV_K), (0, 0)))
    (o, oa, a_all, s_all), (w2_in, w2_out) = _hgrn_fwd(p, small["hgrn_lb"], small["hgrn_g"],
                                                       ([sh["ffn2_w_in"], sh["ffn2_w_out"]], "gather"))
    w2_in, w2_out = _full_w_in(w2_in), w2_out.reshape(D_FF, D)
    u1, u2 = _conv_fwd(p, cw, small["conv_b"], small["conv_ln_g"], small["conv_ln_b"])
    x2, ya, yb, mout = _mixout_fwd(x1, oa, u2, p, mod, 3, wh_o, wc_o, wm_o)
    (x3, a3, b3, f3, h3), _ = _ffn_fwd(x2, mod, 6, small["norm_ffn2"], w2_in, w2_out, 0.5, "ffn2_fwd", None)
    dx3, sm_head = _head(x3, target, small["norm_final"])

    (da3, db3, dw2_a, dw2_b, dw2_out), _ = _ffn_bwd_w(h3, dx3, a3, b3, mod, 6, w2_out, 0.5, "ffn2_bwd_w", None)
    (dx2, sm3), _ = _ffn_bwd_x(x2, dx3, f3, da3, db3, mod, 6, small["norm_ffn2"], w2_in, 0.5, "ffn2_bwd_x", None)
    dp, doa, du2, dwh_o, dwc_o, dwm_o, sm_mo = _mixout_bwd(dx2, oa, u2, ya, yb, mout, p, mod, 3, wh_o, wc_o, wm_o)
    dp, dcw, sm_cv = _conv_bwd(p, u1, du2, cw, small["conv_ln_g"], small["conv_ln_b"], dp)
    rows = lambda t: t.reshape(N_DEV, -1, D).astype(MM)
    (dp, sm_hg), (r2_in, r2_out) = _hgrn_bwd(p, o, a_all, s_all, doa, small["hgrn_lb"], small["hgrn_g"], dp,
                                             ([_w_in_shards(dw2_a, dw2_b), rows(dw2_out)], "scatter"))
    (dx1, dwm_in, sm2), (rh_o, rc_o, rm_o, rcw) = _mixin_bwd(
        x1, h2, dx2, dp, mod, 3, small["norm_mix"], wm_in,
        ([rows(dwh_o), rows(dwc_o), rows(dwm_o), dcw[:CONV_K].reshape(CONV_K, N_DEV, -1).transpose(1, 0, 2)],
         "scatter"))
    (da1, db1, dw1_a, dw1_b, dw1_out), (rm_in,) = _ffn_bwd_w(h1, dx1, a1, b1, mod, 0, w1_out, 0.5, "ffn1_bwd_w",
                                                            (_pair_reduce([dwm_in], "pair_mix"), "quad"))
    (dx0, sm1), (r1_in, r1_out) = _ffn_bwd_x(
        x, dx1, f1, da1, db1, mod, 0, small["norm_ffn1"], w1_in, 0.5, "ffn1_bwd_x",
        (_pair_reduce([_w_in_shards(dw1_a, dw1_b), rows(dw1_out)], "pair_ffn1"), "quad"))

    dmod = jnp.concatenate([sm1[0:3], sm2[0:2], sm_mo[2:3], sm3[0:3]], axis=0)
    gsmall = dict(norm_ffn1=sm1[3:4], norm_mix=sm2[3:4], lb0=sm_hg[0:1], hgrn_g=sm_hg[1:2], conv_b=sm_cv[0:1],
                  conv_ln_g=sm_cv[1:2], conv_ln_b=sm_cv[2:3], norm_ffn2=sm3[3:4], norm_final=sm_head[0:1])
    recv = dict(ffn1_w_in=r1_in, ffn1_w_out=r1_out, mix_w_in=rm_in, hgrn_w_o=rh_o, conv_w=rcw, conv_w_o=rc_o,
                mix_w_out=rm_o, ffn2_w_in=r2_in, ffn2_w_out=r2_out)
    return sm_head[1, 0], dx0, dmod, gsmall, recv


def _add2(a, b, name):
    n, R, C = a.shape
    tr = R
    for cand in (256, 176):
        if R % cand == 0 and R > cand:
            tr = cand
            break

    def body(a_ref, b_ref, out_ref):
        out_ref[...] = (a_ref[...].astype(F32) + b_ref[...].astype(F32)).astype(out_ref.dtype)

    blk = pl.BlockSpec((1, tr, C), lambda s, i: (s, i, 0))
    return pl.pallas_call(body, name=name, grid=(n, R // tr), in_specs=[blk, blk], out_specs=blk,
                          out_shape=jax.ShapeDtypeStruct(a.shape, a.dtype), compiler_params=_cparams(2))(a, b)


def _pair_reduce(arrays, name):
    theirs = _exchange(arrays, "pair", name)
    mc = lax.axis_index("c")
    mine = [lax.dynamic_index_in_dim(a.reshape((N_CHIP, 2) + a.shape[1:]), mc, axis=1, keepdims=False)
            for a in arrays]
    return [_add2(m, t, "%s_add%d" % (name, i)) for i, (m, t) in enumerate(zip(mine, theirs))]


def _full_w_in(g):
    return g.transpose(1, 0, 2).reshape(D, -1)


def _w_in_shards(dwa, dwb):
    half = N_DEV // 2
    return jnp.concatenate([t.reshape(D, half, -1).transpose(1, 0, 2) for t in (dwa, dwb)], axis=0)


SMALL_ORDER = ("norm_ffn1", "norm_mix", "lb0", "hgrn_g", "conv_b", "conv_ln_g", "conv_ln_b", "norm_ffn2",
               "norm_final")
PACK_ROWS = 24


def kernel(x, c, ada_w, ada_b, norm_ffn1, ffn1_w_in, ffn1_w_out, norm_mix, mix_w_in, hgrn_lb, hgrn_g, hgrn_w_o, conv_w, conv_b, conv_ln_g, conv_ln_b, conv_w_o, mix_w_out, norm_ffn2, ffn2_w_in, ffn2_w_out, norm_final, loss_target, m_ada_w, m_ada_b, m_norm_ffn1, m_ffn1_w_in, m_ffn1_w_out, m_norm_mix, m_mix_w_in, m_hgrn_lb, m_hgrn_g, m_hgrn_w_o, m_conv_w, m_conv_b, m_conv_ln_g, m_conv_ln_b, m_conv_w_o, m_mix_w_out, m_norm_ffn2, m_ffn2_w_in, m_ffn2_w_out, m_norm_final, v_ada_w, v_ada_b, v_norm_ffn1, v_ffn1_w_in, v_ffn1_w_out, v_norm_mix, v_mix_w_in, v_hgrn_lb, v_hgrn_g, v_hgrn_w_o, v_conv_w, v_conv_b, v_conv_ln_g, v_conv_ln_b, v_conv_w_o, v_mix_w_out, v_norm_ffn2, v_ffn2_w_in, v_ffn2_w_out, v_norm_final):
    mx, my, mc = _me()
    me = 4 * mx + 2 * my + mc
    ncol = ada_w.shape[2]

    cs = jnp.broadcast_to(c * jax.nn.sigmoid(c), (8, D))
    cs_all = _allgather_small(cs).reshape(N_DEV, 8, D)[:, 0, :]
    ada_b_cols = lax.dynamic_slice(ada_b, (0, me * ncol), (1, ncol))
    mod_cols = _ada_fwd(cs_all, ada_w[0], ada_b_cols)
    mod_all = _allgather_small(mod_cols).reshape(N_DEV, N_DEV, ncol)
    mod = lax.dynamic_index_in_dim(mod_all, me, axis=1, keepdims=False).reshape(9, D)

    sh = dict(ffn1_w_in=ffn1_w_in, ffn1_w_out=ffn1_w_out, mix_w_in=mix_w_in, hgrn_w_o=hgrn_w_o,
              conv_w_o=conv_w_o, mix_w_out=mix_w_out, ffn2_w_in=ffn2_w_in, ffn2_w_out=ffn2_w_out)
    sh = {n: w[0].astype(MM) for n, w in sh.items()}
    sh["conv_w"] = conv_w[0]
    small = dict(norm_ffn1=norm_ffn1, norm_mix=norm_mix, hgrn_lb=hgrn_lb, hgrn_g=hgrn_g, conv_b=conv_b,
                 conv_ln_g=conv_ln_g, conv_ln_b=conv_ln_b, norm_ffn2=norm_ffn2, norm_final=norm_final.reshape(1, D))

    loss_local, dx, dmod, gsmall, recv = _local_step(x[0], loss_target[0], mod, small, sh)
    loss = lax.psum(loss_local, ("x", "y", "c"))

    pack = jnp.concatenate([dmod] + [gsmall[n] for n in SMALL_ORDER]
                           + [jnp.zeros((PACK_ROWS - 9 - len(SMALL_ORDER), D), F32)], axis=0)
    pack_all = _allgather_small(pack).reshape(N_DEV, PACK_ROWS, D)
    tot = _sum_slots(pack_all)
    gs = {n: tot[9 + i:10 + i] for i, n in enumerate(SMALL_ORDER)}
    dmod_all = pack_all[:, 0:9, :].reshape(N_DEV, 9 * D)
    g_ada_b = tot[0:9].reshape(1, 9 * D)
    g_ada_w = _ada_wgrad(cs_all, lax.dynamic_slice(dmod_all, (0, me * ncol), (N_DEV, ncol)))
    z = hgrn_lb.astype(F32)
    p0 = jax.nn.sigmoid(z[0:1] - z[1:2])
    dz0 = p0 * (1.0 - p0) * gs["lb0"]
    g_hgrn_lb = jnp.concatenate([dz0, -dz0], axis=0)

    res = {}
    res["ada_w"] = _adamw(ada_w[0], m_ada_w[0], v_ada_w[0], g_ada_w, "adamw_ada_w")
    big = dict(ffn1_w_in=(ffn1_w_in, m_ffn1_w_in, v_ffn1_w_in), ffn1_w_out=(ffn1_w_out, m_ffn1_w_out, v_ffn1_w_out),
               mix_w_in=(mix_w_in, m_mix_w_in, v_mix_w_in), hgrn_w_o=(hgrn_w_o, m_hgrn_w_o, v_hgrn_w_o),
               conv_w=(conv_w, m_conv_w, v_conv_w), conv_w_o=(conv_w_o, m_conv_w_o, v_conv_w_o),
               mix_w_out=(mix_w_out, m_mix_w_out, v_mix_w_out), ffn2_w_in=(ffn2_w_in, m_ffn2_w_in, v_ffn2_w_in),
               ffn2_w_out=(ffn2_w_out, m_ffn2_w_out, v_ffn2_w_out))
    for n, (w, m, v) in big.items():
        res[n] = _adamw(w[0], m[0], v[0], recv[n], "adamw_" + n)
    sm_names = ("ada_b", "norm_ffn1", "norm_mix", "hgrn_lb", "hgrn_g", "conv_b", "conv_ln_g", "conv_ln_b",
                "norm_ffn2", "norm_final")
    sm_w = dict(ada_b=(ada_b, m_ada_b, v_ada_b), norm_ffn1=(norm_ffn1, m_norm_ffn1, v_norm_ffn1),
                norm_mix=(norm_mix, m_norm_mix, v_norm_mix), hgrn_lb=(hgrn_lb, m_hgrn_lb, v_hgrn_lb),
                hgrn_g=(hgrn_g, m_hgrn_g, v_hgrn_g), conv_b=(conv_b, m_conv_b, v_conv_b),
                conv_ln_g=(conv_ln_g, m_conv_ln_g, v_conv_ln_g), conv_ln_b=(conv_ln_b, m_conv_ln_b, v_conv_ln_b),
                norm_ffn2=(norm_ffn2, m_norm_ffn2, v_norm_ffn2), norm_final=(norm_final, m_norm_final, v_norm_final))
    sm_g = dict(gs, ada_b=g_ada_b, hgrn_lb=g_hgrn_lb)
    rows = {n: sm_w[n][0].size // D for n in sm_names}
    n_rows = sum(rows.values())
    pad = (-n_rows) % 8
    stack = lambda parts: jnp.concatenate([q.reshape(-1, D) for q in parts] + [jnp.ones((pad, D), F32)], axis=0)
    st = _adamw(stack([sm_w[n][0] for n in sm_names]), stack([sm_w[n][1] for n in sm_names]),
                stack([sm_w[n][2] for n in sm_names]), stack([sm_g[n] for n in sm_names]), "adamw_small")
    off = 0
    for n in sm_names:
        res[n] = tuple(t[off:off + rows[n]].reshape(sm_w[n][0].shape) for t in st)
        off += rows[n]

    order = ("ada_w", "ada_b", "norm_ffn1", "ffn1_w_in", "ffn1_w_out", "norm_mix", "mix_w_in", "hgrn_lb", "hgrn_g",
             "hgrn_w_o", "conv_w", "conv_b", "conv_ln_g", "conv_ln_b", "conv_w_o", "mix_w_out", "norm_ffn2",
             "ffn2_w_in", "ffn2_w_out", "norm_final")
    lead = lambda n, t: t[None] if n in big or n == "ada_w" else t
    outs = [loss, dx[None]]
    for j in range(4):
        outs += [lead(n, res[n][j]) for n in order]
    return tuple(outs)
```

```python
import functools

import jax
import jax.numpy as jnp
from jax import lax
from jax.experimental import pallas as pl
from jax.experimental.pallas import tpu as pltpu

F32 = jnp.float32
MM = jnp.bfloat16
ACT = jnp.bfloat16

D = 1024
D_FF = 2816
HEADS = 8
HD = 128
CHUNK = 64
SUB = 16
NSUB = CHUNK // SUB
CONV_K = 31
HALO = 32
EPS = 1e-6
N_DEV = 8
NEG = -1e30
Q_SCALE = HD ** -0.5

ADAM_LR = 0.001
ADAM_B1 = 0.9
ADAM_B2 = 0.999
ADAM_EPS = 1e-08
ADAM_WD = 0.01
ADAM_STEP = 10

VMEM_LIMIT = 60 * 1024 * 1024
MESH = pl.DeviceIdType.MESH


def _cparams(n_axes):
    return pltpu.CompilerParams(dimension_semantics=("arbitrary",) * n_axes, vmem_limit_bytes=VMEM_LIMIT)


def _mm(a, b):
    return lax.dot_general(a.astype(MM), b.astype(MM), (((1,), (0,)), ((), ())), preferred_element_type=F32)


def _mm_nt(a, b):
    return lax.dot_general(a.astype(MM), b.astype(MM), (((1,), (1,)), ((), ())), preferred_element_type=F32)


def _mm_tn(a, b):
    return lax.dot_general(a.astype(MM), b.astype(MM), (((0,), (0,)), ((), ())), preferred_element_type=F32)


def _sig(x):
    return 1.0 / (1.0 + jnp.exp(-x))


def _colsum(x):
    return jnp.sum(x, axis=0, keepdims=True)


def _rowmean(x):
    return jnp.mean(x, axis=-1, keepdims=True)


def _modnorm_fwd(xv, g, sh, sc):
    r = lax.rsqrt(_rowmean(xv * xv) + EPS)
    xh = xv * r
    n = xh * g
    return n * (1.0 + sc) + sh, xh, n, r


def _modnorm_bwd(dh, xh, n, r, g, sc):
    dsc = _colsum(dh * n)
    dsh = _colsum(dh)
    dn = dh * (1.0 + sc)
    dg = _colsum(dn * xh)
    dxh = dn * g
    dx = r * (dxh - xh * _rowmean(dxh * xh))
    return dx, dsh, dsc, dg


def _ffn_fwd(x, mod, mo, gnorm, w_in, w_out, res, name, carry):
    T = x.shape[0]
    tm = min(512, T)
    tn = D_FF // 2
    nj = D_FF // tn

    def body(x_ref, mod_ref, g_ref, wa_ref, wb_ref, wo_ref, xo_ref, a_ref, b_ref, f_ref, h_ref, acc_scr):
        j = pl.program_id(1)

        @pl.when(j == 0)
        def _():
            h, _, _, _ = _modnorm_fwd(x_ref[...], g_ref[...], mod_ref[mo:mo + 1, :], mod_ref[mo + 1:mo + 2, :])
            h_ref[...] = h.astype(ACT)
            acc_scr[...] = jnp.zeros_like(acc_scr)

        h = h_ref[...]
        a = _mm(h, wa_ref[...])
        b = _mm(h, wb_ref[...])
        a_ref[...] = a.astype(ACT)
        b_ref[...] = b.astype(ACT)
        s = a * _sig(a) * b
        acc_scr[...] += _mm(s, wo_ref[...])

        @pl.when(j == nj - 1)
        def _():
            f = acc_scr[...]
            f_ref[...] = f
            xo_ref[...] = x_ref[...] + res * mod_ref[mo + 2:mo + 3, :] * f

    out = _gridded(
        body, carry, name=name, grid=(T // tm, nj),
        in_specs=[
            pl.BlockSpec((tm, D), lambda i, j: (i, 0)),
            pl.BlockSpec((9, D), lambda i, j: (0, 0)),
            pl.BlockSpec((1, D), lambda i, j: (0, 0)),
            pl.BlockSpec((D, tn), lambda i, j: (0, j)),
            pl.BlockSpec((D, tn), lambda i, j: (0, j + nj)),
            pl.BlockSpec((tn, D), lambda i, j: (j, 0)),
        ],
        out_specs=[
            pl.BlockSpec((tm, D), lambda i, j: (i, 0)),
            pl.BlockSpec((tm, tn), lambda i, j: (i, j)),
            pl.BlockSpec((tm, tn), lambda i, j: (i, j)),
            pl.BlockSpec((tm, D), lambda i, j: (i, 0)),
            pl.BlockSpec((tm, D), lambda i, j: (i, 0)),
        ],
        out_shape=[
            jax.ShapeDtypeStruct((T, D), F32),
            jax.ShapeDtypeStruct((T, D_FF), ACT),
            jax.ShapeDtypeStruct((T, D_FF), ACT),
            jax.ShapeDtypeStruct((T, D), F32),
            jax.ShapeDtypeStruct((T, D), ACT),
        ],
        scratch_shapes=[pltpu.VMEM((tm, D), F32)],
    )(x, mod, gnorm, w_in, w_in, w_out)
    return out[:5], out[5:]


def _ffn_bwd_w(h, dxo, a, b, mod, mo, w_out, res, name, carry):
    T = h.shape[0]
    tm = min(1024, T)
    ni = T // tm
    tn = 256
    nj = D_FF // tn

    def body(h_ref, dxo_ref, a_ref, b_ref, mod_ref, wo_ref, da_ref, db_ref, dwa_ref, dwb_ref, dwo_ref,
             acc_a, acc_b, acc_o, df_all, h_all):
        j = pl.program_id(0)
        i = pl.program_id(1)

        @pl.when(i == 0)
        def _():
            acc_a[...] = jnp.zeros_like(acc_a)
            acc_b[...] = jnp.zeros_like(acc_b)
            acc_o[...] = jnp.zeros_like(acc_o)

        @pl.when(j == 0)
        def _():
            df_all[i] = (res * mod_ref[mo + 2:mo + 3, :] * dxo_ref[...]).astype(MM)
            h_all[i] = h_ref[...]

        hb = h_all[i]
        df = df_all[i]
        av = a_ref[...].astype(F32)
        bv = b_ref[...].astype(F32)
        sg = _sig(av)
        sa = av * sg
        s = (sa * bv).astype(MM)
        ds = _mm_nt(df, wo_ref[...])
        da = (ds * bv * sg * (1.0 + av * (1.0 - sg))).astype(MM)
        db = (ds * sa).astype(MM)
        da_ref[...] = da
        db_ref[...] = db
        acc_o[...] += _mm_tn(s, df)
        acc_a[...] += _mm_tn(hb, da)
        acc_b[...] += _mm_tn(hb, db)

        @pl.when(i == ni - 1)
        def _():
            dwa_ref[...] = acc_a[...].astype(MM)
            dwb_ref[...] = acc_b[...].astype(MM)
            dwo_ref[...] = acc_o[...].astype(MM)

    first = lambda j, i: (jnp.where(j == 0, i, ni - 1), 0)
    out = _gridded(
        body, carry, name=name, grid=(nj, ni),
        in_specs=[
            pl.BlockSpec((tm, D), first),
            pl.BlockSpec((tm, D), first),
            pl.BlockSpec((tm, tn), lambda j, i: (i, j)),
            pl.BlockSpec((tm, tn), lambda j, i: (i, j)),
            pl.BlockSpec((9, D), lambda j, i: (0, 0)),
            pl.BlockSpec((tn, D), lambda j, i: (j, 0)),
        ],
        out_specs=[
            pl.BlockSpec((tm, tn), lambda j, i: (i, j)),
            pl.BlockSpec((tm, tn), lambda j, i: (i, j)),
            pl.BlockSpec((D, tn), lambda j, i: (0, j)),
            pl.BlockSpec((D, tn), lambda j, i: (0, j)),
            pl.BlockSpec((tn, D), lambda j, i: (j, 0)),
        ],
        out_shape=[
            jax.ShapeDtypeStruct((T, D_FF), MM),
            jax.ShapeDtypeStruct((T, D_FF), MM),
            jax.ShapeDtypeStruct((D, D_FF), MM),
            jax.ShapeDtypeStruct((D, D_FF), MM),
            jax.ShapeDtypeStruct((D_FF, D), MM),
        ],
        scratch_shapes=[pltpu.VMEM((D, tn), F32), pltpu.VMEM((D, tn), F32), pltpu.VMEM((tn, D), F32),
                        pltpu.VMEM((ni, tm, D), MM), pltpu.VMEM((ni, tm, D), MM)],
    )(h, dxo, a, b, mod, w_out)
    return out[:5], out[5:]


def _ffn_bwd_x(x, dxo, f, da, db, mod, mo, gnorm, w_in, res, name, carry):
    T = x.shape[0]
    tm = min(512, T)
    ni = T // tm
    tn = D_FF // 2
    nj = D_FF // tn

    def body(x_ref, dxo_ref, f_ref, da_ref, db_ref, mod_ref, g_ref, wa_ref, wb_ref, dx_ref, sm_ref, dh_scr):
        j = pl.program_id(0)
        i = pl.program_id(1)

        @pl.when((j == 0) & (i == 0))
        def _():
            sm_ref[...] = jnp.zeros_like(sm_ref)

        @pl.when(j == 0)
        def _():
            dh_scr[i] = jnp.zeros((tm, D), F32)

        dh_scr[i] += _mm_nt(da_ref[...], wa_ref[...]) + _mm_nt(db_ref[...], wb_ref[...])

        @pl.when(j == nj - 1)
        def _():
            sc = mod_ref[mo + 1:mo + 2, :]
            _, xh, n, r = _modnorm_fwd(x_ref[...], g_ref[...], mod_ref[mo:mo + 1, :], sc)
            dxn, dsh, dsc, dg = _modnorm_bwd(dh_scr[i], xh, n, r, g_ref[...], sc)
            dxo_v = dxo_ref[...]
            dx_ref[...] = dxo_v + dxn
            sm_ref[0:1, :] += dsh
            sm_ref[1:2, :] += dsc
            sm_ref[2:3, :] += _colsum(dxo_v * f_ref[...]) * res
            sm_ref[3:4, :] += dg

    last = pl.BlockSpec((tm, D), lambda j, i: (jnp.where(j == nj - 1, i, 0), 0))
    out = _gridded(
        body, carry, name=name, grid=(nj, ni),
        in_specs=[last, last, last,
                  pl.BlockSpec((tm, tn), lambda j, i: (i, j)), pl.BlockSpec((tm, tn), lambda j, i: (i, j)),
                  pl.BlockSpec((9, D), lambda j, i: (0, 0)), pl.BlockSpec((1, D), lambda j, i: (0, 0)),
                  pl.BlockSpec((D, tn), lambda j, i: (0, j)), pl.BlockSpec((D, tn), lambda j, i: (0, j + nj))],
        out_specs=[last, pl.BlockSpec((8, D), lambda j, i: (0, 0))],
        out_shape=[jax.ShapeDtypeStruct((T, D), F32), jax.ShapeDtypeStruct((8, D), F32)],
        scratch_shapes=[pltpu.VMEM((ni, tm, D), F32)],
    )(x, dxo, f, da, db, mod, gnorm, w_in, w_in)
    return out[:2], out[2:]


def _head(x, target, gfin):
    T = x.shape[0]
    tm = min(512, T)
    ni = T // tm

    def body(x_ref, t_ref, g_ref, dx_ref, sm_ref):
        i = pl.program_id(0)

        @pl.when(i == 0)
        def _():
            sm_ref[...] = jnp.zeros_like(sm_ref)

        xv = x_ref[...]
        g = g_ref[...]
        r = lax.rsqrt(_rowmean(xv * xv) + EPS)
        xh = xv * r
        e = xh * g - t_ref[...]
        sm_ref[1:2, :] += _colsum(e * e) * (0.5 / D)
        dy = e * (1.0 / D)
        sm_ref[0:1, :] += _colsum(dy * xh)
        dxh = dy * g
        dx_ref[...] = r * (dxh - xh * _rowmean(dxh * xh))

        @pl.when(i == ni - 1)
        def _():
            sm_ref[1:2, :] = jnp.broadcast_to(jnp.sum(sm_ref[1:2, :], axis=-1, keepdims=True), (1, D))

    return pl.pallas_call(
        body, name="head_loss", grid=(ni,),
        in_specs=[pl.BlockSpec((tm, D), lambda i: (i, 0)), pl.BlockSpec((tm, D), lambda i: (i, 0)),
                  pl.BlockSpec((1, D), lambda i: (0, 0))],
        out_specs=[pl.BlockSpec((tm, D), lambda i: (i, 0)), pl.BlockSpec((8, D), lambda i: (0, 0))],
        out_shape=[jax.ShapeDtypeStruct((T, D), F32), jax.ShapeDtypeStruct((8, D), F32)],
        compiler_params=_cparams(1),
    )(x, target, gfin)


def _mixin_fwd(x, mod, mo, gnorm, w, carry):
    T = x.shape[0]
    tm = min(1024, T)
    ni = T // tm

    def body(x_ref, mod_ref, g_ref, w_ref, p_ref, h_ref, h_all):
        i = pl.program_id(1)

        @pl.when(pl.program_id(0) == 0)
        def _():
            h, _, _, _ = _modnorm_fwd(x_ref[...], g_ref[...], mod_ref[mo:mo + 1, :], mod_ref[mo + 1:mo + 2, :])
            h_all[i] = h.astype(ACT)
            h_ref[...] = h.astype(ACT)

        p_ref[0] = _mm(h_all[i], w_ref[0])

    first = lambda k, i: (jnp.where(k == 0, i, ni - 1), 0)
    out = _gridded(
        body, carry, name="mixin_fwd", grid=(8, ni),
        in_specs=[pl.BlockSpec((tm, D), first), pl.BlockSpec((9, D), lambda k, i: (0, 0)),
                  pl.BlockSpec((1, D), lambda k, i: (0, 0)), pl.BlockSpec((1, D, D), lambda k, i: (k, 0, 0))],
        out_specs=[pl.BlockSpec((1, tm, D), lambda k, i: (k, i, 0)), pl.BlockSpec((tm, D), first)],
        out_shape=[jax.ShapeDtypeStruct((8, T, D), F32), jax.ShapeDtypeStruct((T, D), ACT)],
        scratch_shapes=[pltpu.VMEM((ni, tm, D), ACT)],
    )(x, mod, gnorm, w)
    return out[:2], out[2:]


def _mixin_bwd(x, h, dxo, dp, mod, mo, gnorm, w, carry):
    T = x.shape[0]
    tm = min(512, T)
    ni = T // tm

    def body(x_ref, h_ref, dxo_ref, dp_ref, mod_ref, g_ref, w_ref, dx_ref, dw_ref, sm_ref, dh_scr, acc):
        k = pl.program_id(0)
        i = pl.program_id(1)

        @pl.when(i == 0)
        def _():
            acc[...] = jnp.zeros_like(acc)

        @pl.when(k == 0)
        def _():
            dh_scr[i] = jnp.zeros((tm, D), F32)

        @pl.when((k == 0) & (i == 0))
        def _():
            sm_ref[...] = jnp.zeros_like(sm_ref)

        dpk = dp_ref[0].astype(MM)
        acc[...] += _mm_tn(h_ref[...], dpk)
        dh_scr[i] += _mm_nt(dpk, w_ref[0])

        @pl.when(i == ni - 1)
        def _():
            dw_ref[0] = acc[...].astype(MM)

        @pl.when(k == 7)
        def _():
            sc = mod_ref[mo + 1:mo + 2, :]
            _, xh, n, r = _modnorm_fwd(x_ref[...], g_ref[...], mod_ref[mo:mo + 1, :], sc)
            dxn, dsh, dsc, dg = _modnorm_bwd(dh_scr[i], xh, n, r, g_ref[...], sc)
            dx_ref[...] = dxo_ref[...] + dxn
            sm_ref[0:1, :] += dsh
            sm_ref[1:2, :] += dsc
            sm_ref[3:4, :] += dg

    out = _gridded(
        body, carry, name="mixin_bwd", grid=(8, ni),
        in_specs=[pl.BlockSpec((tm, D), lambda k, i: (jnp.where(k == 7, i, 0), 0)),
                  pl.BlockSpec((tm, D), lambda k, i: (i, 0)),
                  pl.BlockSpec((tm, D), lambda k, i: (jnp.where(k == 7, i, 0), 0)),
                  pl.BlockSpec((1, tm, D), lambda k, i: (k, i, 0)), pl.BlockSpec((9, D), lambda k, i: (0, 0)),
                  pl.BlockSpec((1, D), lambda k, i: (0, 0)), pl.BlockSpec((1, D, D), lambda k, i: (k, 0, 0))],
        out_specs=[pl.BlockSpec((tm, D), lambda k, i: (jnp.where(k == 7, i, 0), 0)),
                   pl.BlockSpec((1, D, D), lambda k, i: (k, 0, 0)),
                   pl.BlockSpec((8, D), lambda k, i: (0, 0))],
        out_shape=[jax.ShapeDtypeStruct((T, D), F32), jax.ShapeDtypeStruct((8, D, D), MM),
                   jax.ShapeDtypeStruct((8, D), F32)],
        scratch_shapes=[pltpu.VMEM((ni, tm, D), F32), pltpu.VMEM((D, D), F32)],
    )(x, h, dxo, dp, mod, gnorm, w)
    return out[:3], out[3:]


def _hgrn_consts():
    rows = jnp.arange(SUB * HD) // HD
    e = (rows[:, None] == jnp.arange(HD)[None, :]).astype(MM)
    return e, e.T


def _rows_bcast(ref, cb, first, n):
    parts = [jnp.broadcast_to(ref[pl.ds(c * CHUNK + first, 1), :], (n, HD)) for c in range(cb // CHUNK)]
    return jnp.concatenate(parts, axis=0)


def _hgrn_pre(qr, fr, lb_ref, b_scr, cb):
    z = lb_ref[...]
    lb = _sig(z[0:1, :] - z[1:2, :])
    sq = _sig(qr)
    q = qr * sq * Q_SCALE
    sf = _sig(fr)
    fg = lb + (1.0 - lb) * sf
    lf = jnp.log(fg)
    k = 1.0 - fg
    tl = lax.broadcasted_iota(jnp.int32, (cb, HD), 0) % CHUNK
    bc = lf
    sh = 1
    while sh < CHUNK:
        bc = bc + jnp.where(tl >= sh, pltpu.roll(bc, sh, 0), 0.0)
        sh *= 2
    b_scr[...] = bc
    bl = _rows_bcast(b_scr, cb, CHUNK - 1, CHUNK)
    br = [None] + [_rows_bcast(b_scr, cb, SUB * i - 1, CHUNK) for i in range(1, NSUB)]
    sb = tl // SUB
    bref = jnp.where(sb == 0, bc, jnp.where(sb == 1, br[1], jnp.where(sb == 2, br[2], br[3])))
    eb = jnp.exp(bc)
    ekd = jnp.exp(bl - bc)
    eqo = jnp.exp(bc - bref)
    eko = [None] + [jnp.exp(jnp.where(tl < SUB * i, br[i] - bc, NEG)) for i in range(1, NSUB)]
    return dict(lb=lb, sq=sq, q=q, sf=sf, fg=fg, k=k, tl=tl, sb=sb, b=bc, bl=bl, eb=eb, ekd=ekd, eqo=eqo,
                eko=eko, qe=q * eb, kd=k * ekd, qo=q * eqo, ko=[None] + [k * eko[i] for i in range(1, NSUB)])


def _pad_rows(x):
    return jnp.concatenate([x, jnp.zeros_like(x)], axis=0)


def _by_subblock(sbc, parts):
    out = jnp.zeros_like(parts[1])
    for i in range(1, NSUB):
        out = jnp.where(sbc == i, parts[i], out)
    return out


def _hgrn_fwd(p, hgrn_lb, hgrn_g, carry):
    T = p.shape[1]
    cb = min(512, T)
    nch = cb // CHUNK
    ncb = T // cb
    e_mat, _ = _hgrn_consts()

    def body(p_ref, lb_ref, g_ref, e_ref, o_ref, oa_ref, a_ref, s_ref, st_scr, q_scr, k_scr, b_scr, z_scr):
        @pl.when(pl.program_id(1) == 0)
        def _():
            st_scr[...] = jnp.zeros_like(st_scr)

        v = p_ref[2]
        og = p_ref[3]
        pre = _hgrn_pre(p_ref[0], p_ref[1], lb_ref, b_scr, cb)
        q_scr[...] = pre["q"]
        k_scr[...] = pre["k"]
        ti = lax.broadcasted_iota(jnp.int32, (SUB, HD), 0)

        def zbody(c, carry):
            for i in range(NSUB):
                r0 = pl.multiple_of(c * CHUNK + SUB * i, SUB)
                qi = q_scr[pl.ds(r0, SUB), :]
                bi = b_scr[pl.ds(r0, SUB), :]
                for s in range(SUB):
                    krow = k_scr[pl.ds(r0 + s, 1), :]
                    brow = b_scr[pl.ds(r0 + s, 1), :]
                    if s < 8:
                        zz = qi * krow * jnp.exp(jnp.where(ti >= s, bi - brow, NEG))
                    else:
                        lo = qi[8:] * krow * jnp.exp(jnp.where(ti[8:] >= s, bi[8:] - brow, NEG))
                        zz = jnp.concatenate([jnp.zeros((8, HD), F32), lo], axis=0)
                    z_scr[i, pl.ds(pl.multiple_of(c * SUB, SUB), SUB), s * HD:(s + 1) * HD] = zz.astype(MM)
            return carry

        lax.fori_loop(0, nch, zbody, 0)
        adiag = [_mm(z_scr[i], e_ref[...]) for i in range(NSUB)]
        sbc = lax.broadcasted_iota(jnp.int32, (CHUNK, HD), 0) // SUB
        chunks = [slice(c * CHUNK, (c + 1) * CHUNK) for c in range(nch)]
        offs = [[_mm_nt(pre["qo"][rs], _pad_rows(pre["ko"][i][rs])) for i in range(1, NSUB)] for rs in chunks]
        kv = [_mm_tn(v[rs], pre["kd"][rs]) for rs in chunks]
        a_parts = []
        for c in range(nch):
            dparts = []
            for i in range(NSUB):
                blk = adiag[i][c * SUB:(c + 1) * SUB]
                dparts.append(blk if i == 0 else pltpu.roll(blk, SUB * i, 1))
            a_parts.append(_by_subblock(sbc, [None] + offs[c]) + jnp.concatenate(dparts, axis=0))
        a_ref[0] = jnp.concatenate(a_parts, axis=0)
        o_intra = [_mm(a_parts[c], _pad_rows(v[rs])) for c, rs in enumerate(chunks)]
        states = []
        st = st_scr[...]
        for c in range(nch):
            states.append(st)
            st = st * jnp.exp(b_scr[pl.ds(c * CHUNK + CHUNK - 1, 1), :]) + kv[c]
        st_scr[...] = st
        for c in range(nch):
            s_ref[0, c] = states[c]
        o = jnp.concatenate([o_intra[c] + _mm_nt(pre["qe"][rs], states[c]) for c, rs in enumerate(chunks)], axis=0)
        o_ref[...] = o
        on = o * lax.rsqrt(_rowmean(o * o) + EPS) * g_ref[...]
        oa_ref[...] = (on * og * _sig(og)).astype(ACT)

    out = _gridded(
        body, carry, name="hgrn_fwd", grid=(HEADS, ncb),
        in_specs=[pl.BlockSpec((4, cb, HD), lambda h, c: (0, c, h)),
                  pl.BlockSpec((2, HD), lambda h, c: (0, h)),
                  pl.BlockSpec((1, HD), lambda h, c: (0, h)),
                  pl.BlockSpec((SUB * HD, HD), lambda h, c: (0, 0))],
        out_specs=[pl.BlockSpec((cb, HD), lambda h, c: (c, h)),
                   pl.BlockSpec((cb, HD), lambda h, c: (c, h)),
                   pl.BlockSpec((1, cb, HD), lambda h, c: (h, c, 0)),
                   pl.BlockSpec((1, nch, HD, HD), lambda h, c: (h, c, 0, 0))],
        out_shape=[jax.ShapeDtypeStruct((T, D), F32), jax.ShapeDtypeStruct((T, D), ACT),
                   jax.ShapeDtypeStruct((HEADS, T, HD), F32),
                   jax.ShapeDtypeStruct((HEADS, T // CHUNK, HD, HD), F32)],
        scratch_shapes=[pltpu.VMEM((HD, HD), F32), pltpu.VMEM((cb, HD), F32), pltpu.VMEM((cb, HD), F32),
                        pltpu.VMEM((cb, HD), F32), pltpu.VMEM((NSUB, nch * SUB, SUB * HD), MM)],
    )(p, hgrn_lb, hgrn_g, e_mat)
    return out[:4], out[4:]


def _hgrn_bwd(p, o, a_all, s_all, doa, hgrn_lb, hgrn_g, dp, carry):
    T = p.shape[1]
    cb = min(512, T)
    nch = cb // CHUNK
    ncb = T // cb
    _, et_mat = _hgrn_consts()

    def body(p_ref, o_ref, a_ref, s_ref, doa_ref, lb_ref, g_ref, et_ref, dp_in, dp_ref, sm_ref,
             dst_scr, q_scr, k_scr, b_scr, x_scr, dqd_scr, dkd_scr):
        del dp_in

        @pl.when(pl.program_id(1) == 0)
        def _():
            dst_scr[...] = jnp.zeros_like(dst_scr)
            sm_ref[...] = jnp.zeros_like(sm_ref)

        qr = p_ref[0]
        v = p_ref[2]
        og = p_ref[3]
        pre = _hgrn_pre(qr, p_ref[1], lb_ref, b_scr, cb)
        q, k = pre["q"], pre["k"]
        q_scr[...] = q
        k_scr[...] = k
        g = g_ref[...]
        ov = o_ref[...]
        r = lax.rsqrt(_rowmean(ov * ov) + EPS)
        oh = ov * r
        sgo = _sig(og)
        doa_v = doa_ref[...]
        don = doa_v * og * sgo
        dog = doa_v * oh * g * sgo * (1.0 + og * (1.0 - sgo))
        sm_ref[1:2, :] += _colsum(don * oh)
        doh = don * g
        do = r * (doh - oh * _rowmean(doh * oh))

        sbc = lax.broadcasted_iota(jnp.int32, (CHUNK, HD), 0) // SUB
        row_i = lax.broadcasted_iota(jnp.int32, (CHUNK, HD), 0)
        lane_i = lax.broadcasted_iota(jnp.int32, (CHUNK, HD), 1)
        causal = lane_i <= row_i
        chunks = [slice(c * CHUNK, (c + 1) * CHUNK) for c in range(nch)]
        da_parts = [jnp.where(causal, _mm_nt(do[rs], _pad_rows(v[rs])), 0.0) for rs in chunks]
        dv_parts = [_mm_tn(a_ref[0, rs, :], do[rs])[:CHUNK] for rs in chunks]
        dqoff_mm = [[_mm(da_parts[c], _pad_rows(pre["ko"][i][rs])) for i in range(1, NSUB)]
                    for c, rs in enumerate(chunks)]
        dkoff_mm = [[_mm_tn(jnp.where(sbc == i, da_parts[c], 0.0), pre["qo"][rs])[:CHUNK] for i in range(1, NSUB)]
                    for c, rs in enumerate(chunks)]
        dqoff_parts = [_by_subblock(sbc, [None] + dqoff_mm[c]) for c in range(nch)]
        dkoff_parts = []
        for c, rs in enumerate(chunks):
            dko = pre["eko"][1][rs] * dkoff_mm[c][0]
            for i in range(2, NSUB):
                dko = dko + pre["eko"][i][rs] * dkoff_mm[c][i - 1]
            dkoff_parts.append(dko)
        for i in range(NSUB):
            rows = []
            for c in range(nch):
                blk = da_parts[c][SUB * i:SUB * (i + 1)]
                rows.append(blk if i == 0 else pltpu.roll(blk, HD - SUB * i, 1))
            x_scr[i] = _mm(jnp.concatenate(rows, axis=0), et_ref[...])
        ti = lax.broadcasted_iota(jnp.int32, (SUB, HD), 0)

        def dbody(c, carry):
            for i in range(NSUB):
                r0 = pl.multiple_of(c * CHUNK + SUB * i, SUB)
                qi = q_scr[pl.ds(r0, SUB), :]
                bi = b_scr[pl.ds(r0, SUB), :]
                dq_hi = jnp.zeros((8, HD), F32)
                dq_lo = jnp.zeros((8, HD), F32)
                dk_hi = jnp.zeros((8, HD), F32)
                dk_lo = jnp.zeros((8, HD), F32)
                c0 = pl.multiple_of(c * SUB, SUB)
                t8 = ti[:8]
                for s in range(SUB):
                    krow = k_scr[pl.ds(r0 + s, 1), :]
                    brow = b_scr[pl.ds(r0 + s, 1), :]
                    w_lo = (x_scr[i, pl.ds(c0 + 8, 8), s * HD:(s + 1) * HD]
                            * jnp.exp(jnp.where(t8 + 8 >= s, bi[8:] - brow, NEG)))
                    dq_lo = dq_lo + w_lo * krow
                    col = _colsum(w_lo * qi[8:])
                    if s < 8:
                        w_hi = (x_scr[i, pl.ds(c0, 8), s * HD:(s + 1) * HD]
                                * jnp.exp(jnp.where(t8 >= s, bi[:8] - brow, NEG)))
                        dq_hi = dq_hi + w_hi * krow
                        dk_hi = jnp.where(t8 == s, col + _colsum(w_hi * qi[:8]), dk_hi)
                    else:
                        dk_lo = jnp.where(t8 + 8 == s, col, dk_lo)
                dqd_scr[pl.ds(r0, SUB), :] = jnp.concatenate([dq_hi, dq_lo], axis=0)
                dkd_scr[pl.ds(r0, SUB), :] = jnp.concatenate([dk_hi, dk_lo], axis=0)
            return carry

        lax.fori_loop(0, nch, dbody, 0)
        qdo = [_mm_tn(do[rs], pre["qe"][rs]) for rs in chunks]
        dsts = [None] * nch
        dst = dst_scr[...]
        for c in reversed(range(nch)):
            dsts[c] = dst
            dst = dst * jnp.exp(b_scr[pl.ds(c * CHUNK + CHUNK - 1, 1), :]) + qdo[c]
        dst_scr[...] = dst
        sts = [s_ref[0, c] for c in range(nch)]
        dqe_parts = [_mm(do[rs], sts[c]) for c, rs in enumerate(chunks)]
        dkdec_parts = [_mm(v[rs], dsts[c]) for c, rs in enumerate(chunks)]
        dvi_parts = [_mm_nt(pre["kd"][rs], dsts[c]) for c, rs in enumerate(chunks)]
        debl_parts = [_colsum(dsts[c] * sts[c]) for c in range(nch)]
        dqe = jnp.concatenate(dqe_parts, axis=0)
        dkdec = jnp.concatenate(dkdec_parts, axis=0)
        dq_tot = jnp.concatenate(dqoff_parts, axis=0) * pre["eqo"] + dqd_scr[...] + dqe * pre["eb"]
        dk_inter = dkdec * pre["ekd"]
        dk_tot = jnp.concatenate(dkoff_parts, axis=0) + dkd_scr[...] + dk_inter
        db = q * dq_tot - k * dk_tot
        kdk = k * dk_inter
        dbl = jnp.concatenate(
            [jnp.broadcast_to(jnp.exp(b_scr[pl.ds(c * CHUNK + CHUNK - 1, 1), :]) * debl_parts[c]
                              + _colsum(kdk[c * CHUNK:(c + 1) * CHUNK]), (CHUNK, HD)) for c in range(nch)], axis=0)
        tl = pre["tl"]
        rc = db
        sh = 1
        while sh < CHUNK:
            rc = rc + jnp.where(tl + sh < CHUNK, pltpu.roll(rc, cb - sh, 0), 0.0)
            sh *= 2
        dlf = rc + dbl
        dfg = dlf / pre["fg"] - dk_tot
        sf = pre["sf"]
        lb = pre["lb"]
        sm_ref[0:1, :] += _colsum(dfg * (1.0 - sf))
        sq = pre["sq"]
        dp_ref[0] = (dq_tot * Q_SCALE * sq * (1.0 + qr * (1.0 - sq))).astype(ACT)
        dp_ref[1] = (dfg * (1.0 - lb) * sf * (1.0 - sf)).astype(ACT)
        dp_ref[2] = (jnp.concatenate(dv_parts, axis=0) + jnp.concatenate(dvi_parts, axis=0)).astype(ACT)
        dp_ref[3] = dog.astype(ACT)

    rev = lambda c: ncb - 1 - c
    out = _gridded(
        body, carry, name="hgrn_bwd", grid=(HEADS, ncb),
        in_specs=[pl.BlockSpec((4, cb, HD), lambda h, c: (0, rev(c), h)),
                  pl.BlockSpec((cb, HD), lambda h, c: (rev(c), h)),
                  pl.BlockSpec((1, cb, HD), lambda h, c: (h, rev(c), 0)),
                  pl.BlockSpec((1, nch, HD, HD), lambda h, c: (h, rev(c), 0, 0)),
                  pl.BlockSpec((cb, HD), lambda h, c: (rev(c), h)),
                  pl.BlockSpec((2, HD), lambda h, c: (0, h)),
                  pl.BlockSpec((1, HD), lambda h, c: (0, h)),
                  pl.BlockSpec((HD, SUB * HD), lambda h, c: (0, 0)),
                  pl.BlockSpec(memory_space=pl.ANY)],
        out_specs=[pl.BlockSpec((4, cb, HD), lambda h, c: (0, rev(c), h)),
                   pl.BlockSpec((8, HD), lambda h, c: (0, h))],
        out_shape=[jax.ShapeDtypeStruct(dp.shape, dp.dtype), jax.ShapeDtypeStruct((8, D), F32)],
        aliases={8: 0},
        scratch_shapes=[pltpu.VMEM((HD, HD), F32), pltpu.VMEM((cb, HD), F32), pltpu.VMEM((cb, HD), F32),
                        pltpu.VMEM((cb, HD), F32), pltpu.VMEM((NSUB, nch * SUB, SUB * HD), F32),
                        pltpu.VMEM((cb, HD), F32), pltpu.VMEM((cb, HD), F32)],
    )(p, o, a_all, s_all, doa, hgrn_lb, hgrn_g, et_mat, dp)
    return out[:2], out[2:]


def _ln_fwd(u1, g, b):
    mu = _rowmean(u1)
    xc = u1 - mu
    rs = lax.rsqrt(_rowmean(xc * xc) + EPS)
    xh = xc * rs
    return xh * g + b, xh, rs


CONV_RB = 64
LANES = 128


def _shift_rows(src, sh, ls, n):
    for r in range(1, 8):
        sh[r - 1, 0:n, :] = src[pl.ds(r, n), ls]


def _tap(src, sh, ls, off, r0, rows):
    r = off % 8
    if r == 0:
        return src[pl.ds(r0 + off, rows), ls]
    return sh[r - 1, pl.ds(r0 + off - r, rows), :]


def _conv_fwd(p, cw, cb_, lng, lnb):
    T = p.shape[1]
    tm = min(512, T)
    n = HALO + tm - 8

    def body(p_ref, cw_ref, cb_ref, g_ref, b_ref, u1_ref, u2_ref, buf, sh):
        @pl.when(pl.program_id(0) == 0)
        def _():
            buf[0:HALO, :] = jnp.zeros((HALO, D), F32)

        buf[HALO:HALO + tm, :] = p_ref[0] * _sig(p_ref[1])
        for lb in range(D // LANES):
            ls = slice(lb * LANES, (lb + 1) * LANES)
            _shift_rows(buf, sh, ls, n)
            taps = [cw_ref[j:j + 1, ls] for j in range(CONV_K)]
            bias = cb_ref[:, ls]

            def rows_body(rb, carry):
                r0 = pl.multiple_of(rb * CONV_RB, CONV_RB)
                acc = jnp.broadcast_to(bias, (CONV_RB, LANES))
                for j in range(CONV_K):
                    acc = acc + taps[j] * _tap(buf, sh, ls, HALO - (CONV_K - 1) + j, r0, CONV_RB)
                u1_ref[pl.ds(r0, CONV_RB), ls] = acc
                return carry

            lax.fori_loop(0, tm // CONV_RB, rows_body, 0)
        y, _, _ = _ln_fwd(u1_ref[...], g_ref[...], b_ref[...])
        u2_ref[...] = (y * _sig(y)).astype(ACT)
        buf[0:HALO, :] = buf[tm:tm + HALO, :]

    return pl.pallas_call(
        body, name="conv_fwd", grid=(T // tm,),
        in_specs=[pl.BlockSpec((2, tm, D), lambda i: (2, i, 0)), pl.BlockSpec((HALO, D), lambda i: (0, 0)),
                  pl.BlockSpec((1, D), lambda i: (0, 0)), pl.BlockSpec((1, D), lambda i: (0, 0)),
                  pl.BlockSpec((1, D), lambda i: (0, 0))],
        out_specs=[pl.BlockSpec((tm, D), lambda i: (i, 0)), pl.BlockSpec((tm, D), lambda i: (i, 0))],
        out_shape=[jax.ShapeDtypeStruct((T, D), F32), jax.ShapeDtypeStruct((T, D), ACT)],
        scratch_shapes=[pltpu.VMEM((HALO + tm, D), F32), pltpu.VMEM((7, n, LANES), F32)],
        compiler_params=_cparams(1),
    )(p, cw, cb_, lng, lnb)


def _conv_bwd(p, u1, du2, cw, lng, lnb, dp):
    T = p.shape[1]
    tm = min(512, T)
    ni = T // tm
    hb = tm // HALO

    n = HALO + tm - 8

    def body(p_ref, ph_ref, u1_ref, du2_ref, cw_ref, g_ref, b_ref, dp_in, dp_ref, dcw_ref, sm_ref, ubuf, dbuf,
             sh, dacc):
        del dp_in
        step = pl.program_id(0)

        @pl.when(step == 0)
        def _():
            dbuf[tm:tm + HALO, :] = jnp.zeros((HALO, D), F32)
            dcw_ref[...] = jnp.zeros_like(dcw_ref)
            sm_ref[...] = jnp.zeros_like(sm_ref)

        ua = p_ref[0]
        sgb = _sig(p_ref[1])
        halo = ph_ref[0] * _sig(ph_ref[1])
        ubuf[0:HALO, :] = jnp.where(step == ni - 1, 0.0, halo)
        ubuf[HALO:HALO + tm, :] = ua * sgb
        g = g_ref[...]
        y, xh, rs = _ln_fwd(u1_ref[...], g, b_ref[...])
        sy = _sig(y)
        dy = du2_ref[...] * sy * (1.0 + y * (1.0 - sy))
        sm_ref[1:2, :] += _colsum(dy * xh)
        sm_ref[2:3, :] += _colsum(dy)
        dxh = dy * g
        du1 = rs * (dxh - _rowmean(dxh) - xh * _rowmean(dxh * xh))
        sm_ref[0:1, :] += _colsum(du1)
        dbuf[0:tm, :] = du1
        for lb in range(D // LANES):
            ls = slice(lb * LANES, (lb + 1) * LANES)
            taps = [cw_ref[j:j + 1, ls] for j in range(CONV_K)]
            _shift_rows(dbuf, sh, ls, n)

            def du0_body(rb, carry):
                r0 = pl.multiple_of(rb * CONV_RB, CONV_RB)
                acc = jnp.zeros((CONV_RB, LANES), F32)
                for j in range(CONV_K):
                    acc = acc + taps[j] * _tap(dbuf, sh, ls, CONV_K - 1 - j, r0, CONV_RB)
                dp_ref[0, pl.ds(r0, CONV_RB), ls] = acc.astype(ACT)
                return carry

            lax.fori_loop(0, tm // CONV_RB, du0_body, 0)
            _shift_rows(ubuf, sh, ls, n)
            dacc[...] = jnp.zeros_like(dacc)

            def dcw_body(rb, carry):
                r0 = pl.multiple_of(rb * CONV_RB, CONV_RB)
                d = dbuf[pl.ds(r0, CONV_RB), ls]
                for j in range(CONV_K):
                    prod = d * _tap(ubuf, sh, ls, HALO - (CONV_K - 1) + j, r0, CONV_RB)
                    dacc[8 * j:8 * j + 8, :] += jnp.sum(prod.reshape(CONV_RB // 8, 8, LANES), axis=0)
                return carry

            lax.fori_loop(0, tm // CONV_RB, dcw_body, 0)
            for j in range(CONV_K):
                dcw_ref[j:j + 1, ls] += _colsum(dacc[8 * j:8 * j + 8, :])
        du0 = dp_ref[0].astype(F32)
        dp_ref[0] = (du0 * sgb).astype(ACT)
        dp_ref[1] = (du0 * ua * sgb * (1.0 - sgb)).astype(ACT)
        dbuf[tm:tm + HALO, :] = dbuf[0:HALO, :]

    rev = lambda i: ni - 1 - i
    return pl.pallas_call(
        body, name="conv_bwd", grid=(ni,),
        in_specs=[pl.BlockSpec((2, tm, D), lambda i: (2, rev(i), 0)),
                  pl.BlockSpec((2, HALO, D), lambda i: (2, jnp.maximum(rev(i) * hb - 1, 0), 0)),
                  pl.BlockSpec((tm, D), lambda i: (rev(i), 0)), pl.BlockSpec((tm, D), lambda i: (rev(i), 0)),
                  pl.BlockSpec((HALO, D), lambda i: (0, 0)), pl.BlockSpec((1, D), lambda i: (0, 0)),
                  pl.BlockSpec((1, D), lambda i: (0, 0)), pl.BlockSpec(memory_space=pl.ANY)],
        out_specs=[pl.BlockSpec((2, tm, D), lambda i: (2, rev(i), 0)),
                   pl.BlockSpec((HALO, D), lambda i: (0, 0)), pl.BlockSpec((8, D), lambda i: (0, 0))],
        out_shape=[jax.ShapeDtypeStruct(dp.shape, dp.dtype), jax.ShapeDtypeStruct((HALO, D), F32),
                   jax.ShapeDtypeStruct((8, D), F32)],
        input_output_aliases={7: 0},
        scratch_shapes=[pltpu.VMEM((HALO + tm, D), F32), pltpu.VMEM((tm + HALO, D), F32),
                        pltpu.VMEM((7, n, LANES), F32), pltpu.VMEM((8 * CONV_K, LANES), F32)],
        compiler_params=_cparams(1),
    )(p, p, u1, du2, cw, lng, lnb, dp)


def _mixout_fwd(x, oa, u2, p, mod, mo, w_a, w_b, w_o):
    T = x.shape[0]
    tm = min(512, T)

    def body(x_ref, oa_ref, u2_ref, p_ref, mod_ref, wa_ref, wb_ref, wo_ref, xo_ref, ya_ref, yb_ref, mo_ref):
        ya = _mm(oa_ref[...], wa_ref[...])
        yb = _mm(u2_ref[...], wb_ref[...])
        ya_ref[...] = ya.astype(ACT)
        yb_ref[...] = yb.astype(ACT)
        merged = _sig(p_ref[0]) * ya + _sig(p_ref[1]) * yb
        out = _mm(merged, wo_ref[...])
        mo_ref[...] = out
        xo_ref[...] = x_ref[...] + mod_ref[mo + 2:mo + 3, :] * out

    tile = pl.BlockSpec((tm, D), lambda i: (i, 0))
    wspec = pl.BlockSpec((D, D), lambda i: (0, 0))
    return pl.pallas_call(
        body, name="mixout_fwd", grid=(T // tm,),
        in_specs=[tile, tile, tile, pl.BlockSpec((2, tm, D), lambda i: (3, i, 0)),
                  pl.BlockSpec((9, D), lambda i: (0, 0)), wspec, wspec, wspec],
        out_specs=[tile, tile, tile, tile],
        out_shape=[jax.ShapeDtypeStruct((T, D), F32), jax.ShapeDtypeStruct((T, D), ACT),
                   jax.ShapeDtypeStruct((T, D), ACT), jax.ShapeDtypeStruct((T, D), F32)],
        compiler_params=_cparams(1),
    )(x, oa, u2, p, mod, w_a, w_b, w_o)


def _mixout_bwd(dxo, oa, u2, ya, yb, mout, p, mod, mo, w_a, w_b, w_o):
    T = dxo.shape[0]
    tm = min(256, T)

    def body(dxo_ref, oa_ref, u2_ref, ya_ref, yb_ref, mo_ref, p_ref, mod_ref, wa_ref, wb_ref, wo_ref,
             dp_ref, doa_ref, du2_ref, dwa_ref, dwb_ref, dwo_ref, sm_ref):
        @pl.when(pl.program_id(0) == 0)
        def _():
            dwa_ref[...] = jnp.zeros_like(dwa_ref)
            dwb_ref[...] = jnp.zeros_like(dwb_ref)
            dwo_ref[...] = jnp.zeros_like(dwo_ref)
            sm_ref[...] = jnp.zeros_like(sm_ref)

        dxo_v = dxo_ref[...]
        sm_ref[2:3, :] += _colsum(dxo_v * mo_ref[...])
        dmo = (mod_ref[mo + 2:mo + 3, :] * dxo_v).astype(MM)
        ya = ya_ref[...].astype(F32)
        yb = yb_ref[...].astype(F32)
        sga = _sig(p_ref[0])
        sgb = _sig(p_ref[1])
        merged = (sga * ya + sgb * yb).astype(MM)
        dwo_ref[...] += _mm_tn(merged, dmo)
        dmg = _mm_nt(dmo, wo_ref[...])
        dp_ref[0] = (dmg * ya * sga * (1.0 - sga)).astype(ACT)
        dp_ref[1] = (dmg * yb * sgb * (1.0 - sgb)).astype(ACT)
        dya = (dmg * sga).astype(MM)
        dyb = (dmg * sgb).astype(MM)
        dwa_ref[...] += _mm_tn(oa_ref[...], dya)
        dwb_ref[...] += _mm_tn(u2_ref[...], dyb)
        doa_ref[...] = _mm_nt(dya, wa_ref[...])
        du2_ref[...] = _mm_nt(dyb, wb_ref[...])

    tile = pl.BlockSpec((tm, D), lambda i: (i, 0))
    wspec = pl.BlockSpec((D, D), lambda i: (0, 0))
    return pl.pallas_call(
        body, name="mixout_bwd", grid=(T // tm,),
        in_specs=[tile, tile, tile, tile, tile, tile, pl.BlockSpec((2, tm, D), lambda i: (3, i, 0)),
                  pl.BlockSpec((9, D), lambda i: (0, 0)), wspec, wspec, wspec],
        out_specs=[pl.BlockSpec((2, tm, D), lambda i: (3, i, 0)), tile, tile, wspec, wspec, wspec,
                   pl.BlockSpec((8, D), lambda i: (0, 0))],
        out_shape=[jax.ShapeDtypeStruct((8, T, D), ACT), jax.ShapeDtypeStruct((T, D), F32),
                   jax.ShapeDtypeStruct((T, D), F32), jax.ShapeDtypeStruct((D, D), F32),
                   jax.ShapeDtypeStruct((D, D), F32), jax.ShapeDtypeStruct((D, D), F32),
                   jax.ShapeDtypeStruct((8, D), F32)],
        compiler_params=_cparams(1),
    )(dxo, oa, u2, ya, yb, mout, p, mod, w_a, w_b, w_o)


def _ada_wgrad(cs_all, dmod_cols):
    cs_t = jnp.pad(cs_all.T, ((0, 0), (0, HD - N_DEV)))
    dm = jnp.pad(dmod_cols, ((0, HD - N_DEV), (0, 0)))

    def body(cs_ref, d_ref, out_ref):
        out_ref[...] = jnp.dot(cs_ref[...], d_ref[...], preferred_element_type=F32,
                               precision=lax.Precision.HIGHEST)

    return pl.pallas_call(
        body, name="ada_wgrad", out_shape=jax.ShapeDtypeStruct((D, dmod_cols.shape[1]), F32),
        compiler_params=pltpu.CompilerParams(vmem_limit_bytes=VMEM_LIMIT),
    )(cs_t, dm)


def _adam_math(w, g, m, v):
    m2 = ADAM_B1 * m + (1.0 - ADAM_B1) * g
    v2 = ADAM_B2 * v + (1.0 - ADAM_B2) * (g * g)
    m_hat = m2 / (1.0 - ADAM_B1 ** ADAM_STEP)
    v_hat = v2 / (1.0 - ADAM_B2 ** ADAM_STEP)
    delta = -ADAM_LR * (m_hat / (jnp.sqrt(v_hat) + ADAM_EPS) + ADAM_WD * w)
    return delta, m2, v2


def _adamw(w, m, v, g, name):
    R, C = w.shape
    slots = g.ndim == 3
    n_slots = g.shape[0] if slots else 0
    tr = R
    for cand in (256, 176):
        if R % cand == 0 and R > cand:
            tr = cand
            break

    def body(w_ref, m_ref, v_ref, g_ref, go_ref, d_ref, mo_ref, vo_ref):
        if slots:
            gv = g_ref[0].astype(F32)
            for s in range(1, n_slots):
                gv = gv + g_ref[s].astype(F32)
        else:
            gv = g_ref[...]
        go_ref[...] = gv
        d_ref[...], mo_ref[...], vo_ref[...] = _adam_math(w_ref[...], gv, m_ref[...], v_ref[...])

    tile = pl.BlockSpec((tr, C), lambda i: (i, 0))
    gspec = pl.BlockSpec((n_slots, tr, C), lambda i: (0, i, 0)) if slots else tile
    sds = jax.ShapeDtypeStruct((R, C), F32)
    return pl.pallas_call(
        body, name=name, grid=(R // tr,), in_specs=[tile, tile, tile, gspec], out_specs=[tile] * 4,
        out_shape=[sds] * 4, compiler_params=_cparams(1),
    )(w, m, v, g)


def _sum_slots(pack):
    def body(p_ref, out_ref):
        acc = p_ref[0]
        for s in range(1, N_DEV):
            acc = acc + p_ref[s]
        out_ref[...] = acc

    return pl.pallas_call(body, name="sum_small", out_shape=jax.ShapeDtypeStruct(pack.shape[1:], F32))(pack)


def _me():
    return lax.axis_index("x"), lax.axis_index("y"), lax.axis_index("c")


def _peer(r):
    x, y, c = _me()
    px = 1 - x if r & 4 else x
    py = 1 - y if r & 2 else y
    pc = 1 - c if r & 1 else c
    return (px, py, pc), 4 * px + 2 * py + pc


def _small_gather(x_ref, out_ref, send_sems, recv_sems):
    R = x_ref.shape[0]
    mx, my, mc = _me()
    me = 4 * mx + 2 * my + mc
    mine = out_ref.at[pl.ds(pl.multiple_of(me * R, 8), R), :]
    copies = []
    for r in range(1, N_DEV):
        dev, _ = _peer(r)
        copies.append(pltpu.make_async_remote_copy(
            src_ref=x_ref, dst_ref=mine, send_sem=send_sems.at[r - 1], recv_sem=recv_sems.at[r - 1],
            device_id=dev, device_id_type=MESH))
    for cp in copies:
        cp.start()
    mine[...] = x_ref[...]
    for r in range(1, N_DEV):
        dev, idx = _peer(r)
        theirs = out_ref.at[pl.ds(pl.multiple_of(idx * R, 8), R), :]
        pltpu.make_async_remote_copy(
            src_ref=x_ref, dst_ref=theirs, send_sem=send_sems.at[r - 1], recv_sem=recv_sems.at[r - 1],
            device_id=dev, device_id_type=MESH).wait_recv()
    for cp in copies:
        cp.wait_send()


def _prologue(cs, ada_w, ada_b_cols, big):
    n = len(big)
    ncol = ada_w.shape[1]
    big_shape, big_sems = _xchg_specs(big, "gather")

    def body(cs_ref, w_ref, b_ref, *rest):
        big_in, cs_all, mod_all, big_out = rest[:n], rest[n], rest[n + 1], rest[n + 2:2 * n + 2]
        mod_scr, s1, r1, s2, r2 = rest[2 * n + 2:2 * n + 7]
        sems = rest[2 * n + 7:]
        _xchg_start(big_in, big_out, sems, "gather")
        _small_gather(cs_ref, cs_all, s1, r1)
        pick = (lax.broadcasted_iota(jnp.int32, (N_DEV, N_DEV * 8), 1)
                == 8 * lax.broadcasted_iota(jnp.int32, (N_DEV, N_DEV * 8), 0)).astype(F32)
        per_device = jnp.dot(pick, cs_all[...], preferred_element_type=F32, precision=lax.Precision.HIGHEST)
        mod_scr[...] = jnp.dot(per_device, w_ref[...], preferred_element_type=F32,
                               precision=lax.Precision.HIGHEST) + b_ref[...]
        _small_gather(mod_scr, mod_all, s2, r2)
        _xchg_wait(big_in, big_out, sems, "gather")

    vmem = pl.BlockSpec(memory_space=pltpu.VMEM)
    hbm = pl.BlockSpec(memory_space=pl.ANY)
    dma7 = pltpu.SemaphoreType.DMA((N_DEV - 1,))
    out = pl.pallas_call(
        body, name="prologue",
        out_shape=[jax.ShapeDtypeStruct((N_DEV * 8, D), F32), jax.ShapeDtypeStruct((N_DEV * 8, ncol), F32)]
        + big_shape,
        in_specs=[vmem, vmem, vmem] + [hbm] * n, out_specs=[vmem, vmem] + [hbm] * n,
        scratch_shapes=[pltpu.VMEM((8, ncol), F32), dma7, dma7, dma7, dma7] + big_sems,
        compiler_params=pltpu.CompilerParams(vmem_limit_bytes=VMEM_LIMIT),
    )(cs, ada_w, ada_b_cols, *big)
    return out[0], out[1], out[2:]


def _allgather_small(x):
    R, C = x.shape

    def body(x_ref, out_ref, send_sems, recv_sems):
        _small_gather(x_ref, out_ref, send_sems, recv_sems)

    return pl.pallas_call(
        body, name="allgather_small_%dx%d" % (R, C),
        out_shape=jax.ShapeDtypeStruct((N_DEV * R, C), F32),
        in_specs=[pl.BlockSpec(memory_space=pltpu.VMEM)], out_specs=pl.BlockSpec(memory_space=pltpu.VMEM),
        scratch_shapes=[pltpu.SemaphoreType.DMA((N_DEV - 1,)), pltpu.SemaphoreType.DMA((N_DEV - 1,))],
    )(x)


N_CHIP = N_DEV // 2


def _xchg_copies(ins, outs, sems, mode):
    send_sems, recv_sems, local_sems = sems
    mx, my, mc = _me()
    me = 4 * mx + 2 * my + mc
    my_chip = 2 * mx + my
    sibling = _peer(1)[0]

    def rdma(a, r, dev, src, slot):
        k = a * (N_DEV - 1) + r - 1
        return pltpu.make_async_remote_copy(
            src_ref=src, dst_ref=outs[a].at[slot], send_sem=send_sems.at[k], recv_sem=recv_sems.at[k],
            device_id=dev, device_id_type=MESH)

    own, sends, relays, recvs = [], [], [], []
    for a in range(len(ins)):
        if mode == "pair":
            for chip in range(N_CHIP):
                src = ins[a].at[2 * chip + 1 - mc]
                sends.append(rdma(a, chip + 1, sibling, src, chip))
                recvs.append(rdma(a, chip + 1, sibling, src, chip))
            continue
        if mode == "quad":
            own.append(pltpu.make_async_copy(ins[a].at[my_chip], outs[a].at[my_chip], local_sems.at[a]))
            for r in (2, 4, 6):
                dev, idx = _peer(r)
                chip = idx // 2
                sends.append(rdma(a, r, dev, ins[a].at[chip], my_chip))
                recvs.append(rdma(a, r, dev, ins[a].at[chip], chip))
            continue
        gather = mode == "gather"
        own.append(pltpu.make_async_copy(ins[a] if gather else ins[a].at[me], outs[a].at[me], local_sems.at[a]))
        for r in range(1, N_DEV):
            dev, idx = _peer(r)
            if not gather:
                sends.append(rdma(a, r, dev, ins[a].at[idx], me))
                recvs.append(rdma(a, r, dev, ins[a].at[idx], idx))
            elif r == 1:
                sends.append(rdma(a, r, dev, ins[a], me))
                recvs.append(rdma(a, r, dev, ins[a], idx))
            elif r % 2 == 0:
                sends.append(rdma(a, r, dev, ins[a], me))
                relays.append((rdma(a, r, dev, ins[a], idx), rdma(a, r + 1, sibling, outs[a].at[idx], idx)))
            else:
                recvs.append(rdma(a, r, sibling, ins[a], idx))
    return own, sends, relays, recvs


def _xchg_start(ins, outs, sems, mode):
    own, sends, _, _ = _xchg_copies(ins, outs, sems, mode)
    for cp in own + sends:
        cp.start()


def _xchg_wait(ins, outs, sems, mode):
    own, sends, relays, recvs = _xchg_copies(ins, outs, sems, mode)
    for arrival, relay in relays:
        arrival.wait_recv()
        relay.start()
    for cp in recvs:
        cp.wait_recv()
    for cp in own:
        cp.wait()
    for cp in sends + [relay for _, relay in relays]:
        cp.wait_send()


def _xchg_specs(arrays, mode):
    n = len(arrays)
    shape = {"gather": lambda s: (N_DEV,) + s, "scatter": lambda s: s, "pair": lambda s: (N_CHIP,) + s[1:],
             "quad": lambda s: s}[mode]
    out_shape = [jax.ShapeDtypeStruct(shape(a.shape), a.dtype) for a in arrays]
    sems = [pltpu.SemaphoreType.DMA((n * (N_DEV - 1),)), pltpu.SemaphoreType.DMA((n * (N_DEV - 1),)),
            pltpu.SemaphoreType.DMA((n,))]
    return out_shape, sems


def _exchange(arrays, mode, name):
    n = len(arrays)

    def body(*refs):
        _xchg_start(refs[:n], refs[n:2 * n], refs[2 * n:], mode)
        _xchg_wait(refs[:n], refs[n:2 * n], refs[2 * n:], mode)

    out_shape, sems = _xchg_specs(arrays, mode)
    return pl.pallas_call(
        body, name=name, out_shape=out_shape,
        in_specs=[pl.BlockSpec(memory_space=pl.ANY)] * n, out_specs=[pl.BlockSpec(memory_space=pl.ANY)] * n,
        scratch_shapes=sems,
    )(*arrays)


def _gridded(body, carry, *, name, grid, in_specs, out_specs, out_shape, scratch_shapes=(), aliases=None):
    if carry is None:
        return pl.pallas_call(
            body, name=name, grid=grid, in_specs=list(in_specs), out_specs=list(out_specs),
            out_shape=list(out_shape), scratch_shapes=list(scratch_shapes), input_output_aliases=aliases or {},
            compiler_params=_cparams(len(grid)))
    arrays, mode = carry
    n, n_in, n_out, n_scr = len(arrays), len(in_specs), len(out_specs), len(scratch_shapes)
    c_shape, c_sems = _xchg_specs(arrays, mode)

    def wrapped(*refs):
        ins, cin = refs[:n_in], refs[n_in:n_in + n]
        o0 = n_in + n
        outs, cout = refs[o0:o0 + n_out], refs[o0 + n_out:o0 + n_out + n]
        s0 = o0 + n_out + n
        scr, sems = refs[s0:s0 + n_scr], refs[s0 + n_scr:]
        first = pl.program_id(0) == 0
        last = pl.program_id(0) == grid[0] - 1
        for ax in range(1, len(grid)):
            first = first & (pl.program_id(ax) == 0)
            last = last & (pl.program_id(ax) == grid[ax] - 1)

        @pl.when(first)
        def _():
            _xchg_start(cin, cout, sems, mode)

        body(*ins, *outs, *scr)

        @pl.when(last)
        def _():
            _xchg_wait(cin, cout, sems, mode)

    hbm = pl.BlockSpec(memory_space=pl.ANY)
    res = pl.pallas_call(
        wrapped, name=name, grid=grid, in_specs=list(in_specs) + [hbm] * n, out_specs=list(out_specs) + [hbm] * n,
        out_shape=list(out_shape) + c_shape, scratch_shapes=list(scratch_shapes) + c_sems,
        input_output_aliases=aliases or {}, compiler_params=_cparams(len(grid)),
    )
    return lambda *args: res(*args, *arrays)


def _local_step(x, target, mod, small, sh, w1):
    w1_in, w1_out = _full_w_in(w1[0]), w1[1].reshape(D_FF, D)
    (x1, a1, b1, f1, h1), (wm_in,) = _ffn_fwd(x, mod, 0, small["norm_ffn1"], w1_in, w1_out, 0.5, "ffn1_fwd",
                                              ([sh["mix_w_in"]], "gather"))
    (p, h2), (wh_o, wc_o, wm_o, cw) = _mixin_fwd(
        x1, mod, 3, small["norm_mix"], wm_in,
        ([sh["hgrn_w_o"], sh["conv_w_o"], sh["mix_w_out"], sh["conv_w"]], "gather"))
    wh_o, wc_o, wm_o = wh_o.reshape(D, D), wc_o.reshape(D, D), wm_o.reshape(D, D)
    cw = jnp.pad(cw.transpose(1, 0, 2).reshape(CONV_K, D), ((0, HALO - CONV_K), (0, 0)))
    (o, oa, a_all, s_all), (w2_in, w2_out) = _hgrn_fwd(p, small["hgrn_lb"], small["hgrn_g"],
                                                       ([sh["ffn2_w_in"], sh["ffn2_w_out"]], "gather"))
    w2_in, w2_out = _full_w_in(w2_in), w2_out.reshape(D_FF, D)
    u1, u2 = _conv_fwd(p, cw, small["conv_b"], small["conv_ln_g"], small["conv_ln_b"])
    x2, ya, yb, mout = _mixout_fwd(x1, oa, u2, p, mod, 3, wh_o, wc_o, wm_o)
    (x3, a3, b3, f3, h3), _ = _ffn_fwd(x2, mod, 6, small["norm_ffn2"], w2_in, w2_out, 0.5, "ffn2_fwd", None)
    dx3, sm_head = _head(x3, target, small["norm_final"])

    (da3, db3, dw2_a, dw2_b, dw2_out), _ = _ffn_bwd_w(h3, dx3, a3, b3, mod, 6, w2_out, 0.5, "ffn2_bwd_w", None)
    (dx2, sm3), _ = _ffn_bwd_x(x2, dx3, f3, da3, db3, mod, 6, small["norm_ffn2"], w2_in, 0.5, "ffn2_bwd_x", None)
    dp, doa, du2, dwh_o, dwc_o, dwm_o, sm_mo = _mixout_bwd(dx2, oa, u2, ya, yb, mout, p, mod, 3, wh_o, wc_o, wm_o)
    dp, dcw, sm_cv = _conv_bwd(p, u1, du2, cw, small["conv_ln_g"], small["conv_ln_b"], dp)
    rows = lambda t: t.reshape(N_DEV, -1, D).astype(MM)
    (dp, sm_hg), (r2_in, r2_out) = _hgrn_bwd(p, o, a_all, s_all, doa, small["hgrn_lb"], small["hgrn_g"], dp,
                                             ([_w_in_shards(dw2_a, dw2_b), rows(dw2_out)], "scatter"))
    (dx1, dwm_in, sm2), (rh_o, rc_o, rm_o, rcw) = _mixin_bwd(
        x1, h2, dx2, dp, mod, 3, small["norm_mix"], wm_in,
        ([rows(dwh_o), rows(dwc_o), rows(dwm_o), dcw[:CONV_K].reshape(CONV_K, N_DEV, -1).transpose(1, 0, 2)],
         "scatter"))
    (da1, db1, dw1_a, dw1_b, dw1_out), (rm_in,) = _ffn_bwd_w(h1, dx1, a1, b1, mod, 0, w1_out, 0.5, "ffn1_bwd_w",
                                                            (_pair_reduce([dwm_in], "pair_mix"), "quad"))
    (dx0, sm1), (r1_in, r1_out) = _ffn_bwd_x(
        x, dx1, f1, da1, db1, mod, 0, small["norm_ffn1"], w1_in, 0.5, "ffn1_bwd_x",
        (_pair_reduce([_w_in_shards(dw1_a, dw1_b), rows(dw1_out)], "pair_ffn1"), "quad"))

    dmod = jnp.concatenate([sm1[0:3], sm2[0:2], sm_mo[2:3], sm3[0:3]], axis=0)
    gsmall = dict(norm_ffn1=sm1[3:4], norm_mix=sm2[3:4], lb0=sm_hg[0:1], hgrn_g=sm_hg[1:2], conv_b=sm_cv[0:1],
                  conv_ln_g=sm_cv[1:2], conv_ln_b=sm_cv[2:3], norm_ffn2=sm3[3:4], norm_final=sm_head[0:1])
    recv = dict(ffn1_w_in=r1_in, ffn1_w_out=r1_out, mix_w_in=rm_in, hgrn_w_o=rh_o, conv_w=rcw, conv_w_o=rc_o,
                mix_w_out=rm_o, ffn2_w_in=r2_in, ffn2_w_out=r2_out)
    return sm_head[1, 0], dx0, dmod, gsmall, recv


def _pair_add(mine, theirs, core, name):
    _, R, C = theirs.shape

    def body(core_ref, a_ref, b_ref, out_ref):
        del core_ref
        out_ref[0] = (a_ref[0, 0].astype(F32) + b_ref[0].astype(F32)).astype(out_ref.dtype)

    blk = pl.BlockSpec((1, R, C), lambda s, core_ref: (s, 0, 0))
    grid_spec = pltpu.PrefetchScalarGridSpec(
        num_scalar_prefetch=1, grid=(N_CHIP,),
        in_specs=[pl.BlockSpec((1, 1, R, C), lambda s, core_ref: (s, core_ref[0], 0, 0)), blk], out_specs=blk)
    return pl.pallas_call(body, name=name, grid_spec=grid_spec,
                          out_shape=jax.ShapeDtypeStruct(theirs.shape, mine.dtype), compiler_params=_cparams(1),
                          )(core, mine.reshape(N_CHIP, 2, R, C), theirs)


def _pair_reduce(arrays, name):
    theirs = _exchange(arrays, "pair", name)
    core = lax.axis_index("c").astype(jnp.int32).reshape(1)
    return [_pair_add(a, t, core, "%s_add%d" % (name, i)) for i, (a, t) in enumerate(zip(arrays, theirs))]


def _full_w_in(g):
    return g.transpose(1, 0, 2).reshape(D, -1)


def _w_in_shards(dwa, dwb):
    half = N_DEV // 2
    return jnp.concatenate([t.reshape(D, half, -1).transpose(1, 0, 2) for t in (dwa, dwb)], axis=0)


SMALL_ORDER = ("norm_ffn1", "norm_mix", "lb0", "hgrn_g", "conv_b", "conv_ln_g", "conv_ln_b", "norm_ffn2",
               "norm_final")
PACK_ROWS = 24


def kernel(x, c, ada_w, ada_b, norm_ffn1, ffn1_w_in, ffn1_w_out, norm_mix, mix_w_in, hgrn_lb, hgrn_g, hgrn_w_o, conv_w, conv_b, conv_ln_g, conv_ln_b, conv_w_o, mix_w_out, norm_ffn2, ffn2_w_in, ffn2_w_out, norm_final, loss_target, m_ada_w, m_ada_b, m_norm_ffn1, m_ffn1_w_in, m_ffn1_w_out, m_norm_mix, m_mix_w_in, m_hgrn_lb, m_hgrn_g, m_hgrn_w_o, m_conv_w, m_conv_b, m_conv_ln_g, m_conv_ln_b, m_conv_w_o, m_mix_w_out, m_norm_ffn2, m_ffn2_w_in, m_ffn2_w_out, m_norm_final, v_ada_w, v_ada_b, v_norm_ffn1, v_ffn1_w_in, v_ffn1_w_out, v_norm_mix, v_mix_w_in, v_hgrn_lb, v_hgrn_g, v_hgrn_w_o, v_conv_w, v_conv_b, v_conv_ln_g, v_conv_ln_b, v_conv_w_o, v_mix_w_out, v_norm_ffn2, v_ffn2_w_in, v_ffn2_w_out, v_norm_final):
    mx, my, mc = _me()
    me = 4 * mx + 2 * my + mc
    ncol = ada_w.shape[2]

    sh = dict(ffn1_w_in=ffn1_w_in, ffn1_w_out=ffn1_w_out, mix_w_in=mix_w_in, hgrn_w_o=hgrn_w_o,
              conv_w_o=conv_w_o, mix_w_out=mix_w_out, ffn2_w_in=ffn2_w_in, ffn2_w_out=ffn2_w_out)
    sh = {n: w[0].astype(MM) for n, w in sh.items()}
    sh["conv_w"] = conv_w[0]
    small = dict(norm_ffn1=norm_ffn1, norm_mix=norm_mix, hgrn_lb=hgrn_lb, hgrn_g=hgrn_g, conv_b=conv_b,
                 conv_ln_g=conv_ln_g, conv_ln_b=conv_ln_b, norm_ffn2=norm_ffn2, norm_final=norm_final.reshape(1, D))

    cs = jnp.broadcast_to(c * jax.nn.sigmoid(c), (8, D))
    ada_b_cols = lax.dynamic_slice(ada_b, (0, me * ncol), (1, ncol))
    cs_all, mod_all, w1 = _prologue(cs, ada_w[0], ada_b_cols, [sh["ffn1_w_in"], sh["ffn1_w_out"]])
    cs_all = cs_all.reshape(N_DEV, 8, D)[:, 0, :]
    mod = lax.dynamic_index_in_dim(mod_all.reshape(N_DEV, N_DEV, ncol), me, axis=1, keepdims=False).reshape(9, D)

    loss_local, dx, dmod, gsmall, recv = _local_step(x[0], loss_target[0], mod, small, sh, w1)
    loss = lax.psum(loss_local, ("x", "y", "c"))

    pack = jnp.concatenate([dmod] + [gsmall[n] for n in SMALL_ORDER]
                           + [jnp.zeros((PACK_ROWS - 9 - len(SMALL_ORDER), D), F32)], axis=0)
    pack_all = _allgather_small(pack).reshape(N_DEV, PACK_ROWS, D)
    tot = _sum_slots(pack_all)
    gs = {n: tot[9 + i:10 + i] for i, n in enumerate(SMALL_ORDER)}
    dmod_all = pack_all[:, 0:9, :].reshape(N_DEV, 9 * D)
    g_ada_b = tot[0:9].reshape(1, 9 * D)
    g_ada_w = _ada_wgrad(cs_all, lax.dynamic_slice(dmod_all, (0, me * ncol), (N_DEV, ncol)))
    z = hgrn_lb.astype(F32)
    p0 = jax.nn.sigmoid(z[0:1] - z[1:2])
    dz0 = p0 * (1.0 - p0) * gs["lb0"]
    g_hgrn_lb = jnp.concatenate([dz0, -dz0], axis=0)

    res = {}
    res["ada_w"] = _adamw(ada_w[0], m_ada_w[0], v_ada_w[0], g_ada_w, "adamw_ada_w")
    big = dict(ffn1_w_in=(ffn1_w_in, m_ffn1_w_in, v_ffn1_w_in), ffn1_w_out=(ffn1_w_out, m_ffn1_w_out, v_ffn1_w_out),
               mix_w_in=(mix_w_in, m_mix_w_in, v_mix_w_in), hgrn_w_o=(hgrn_w_o, m_hgrn_w_o, v_hgrn_w_o),
               conv_w=(conv_w, m_conv_w, v_conv_w), conv_w_o=(conv_w_o, m_conv_w_o, v_conv_w_o),
               mix_w_out=(mix_w_out, m_mix_w_out, v_mix_w_out), ffn2_w_in=(ffn2_w_in, m_ffn2_w_in, v_ffn2_w_in),
               ffn2_w_out=(ffn2_w_out, m_ffn2_w_out, v_ffn2_w_out))
    for n, (w, m, v) in big.items():
        res[n] = _adamw(w[0], m[0], v[0], recv[n], "adamw_" + n)
    sm_names = ("ada_b", "norm_ffn1", "norm_mix", "hgrn_lb", "hgrn_g", "conv_b", "conv_ln_g", "conv_ln_b",
                "norm_ffn2", "norm_final")
    sm_w = dict(ada_b=(ada_b, m_ada_b, v_ada_b), norm_ffn1=(norm_ffn1, m_norm_ffn1, v_norm_ffn1),
                norm_mix=(norm_mix, m_norm_mix, v_norm_mix), hgrn_lb=(hgrn_lb, m_hgrn_lb, v_hgrn_lb),
                hgrn_g=(hgrn_g, m_hgrn_g, v_hgrn_g), conv_b=(conv_b, m_conv_b, v_conv_b),
                conv_ln_g=(conv_ln_g, m_conv_ln_g, v_conv_ln_g), conv_ln_b=(conv_ln_b, m_conv_ln_b, v_conv_ln_b),
                norm_ffn2=(norm_ffn2, m_norm_ffn2, v_norm_ffn2), norm_final=(norm_final, m_norm_final, v_norm_final))
    sm_g = dict(gs, ada_b=g_ada_b, hgrn_lb=g_hgrn_lb)
    rows = {n: sm_w[n][0].size // D for n in sm_names}
    n_rows = sum(rows.values())
    pad = (-n_rows) % 8
    stack = lambda parts: jnp.concatenate([q.reshape(-1, D) for q in parts] + [jnp.ones((pad, D), F32)], axis=0)
    st = _adamw(stack([sm_w[n][0] for n in sm_names]), stack([sm_w[n][1] for n in sm_names]),
                stack([sm_w[n][2] for n in sm_names]), stack([sm_g[n] for n in sm_names]), "adamw_small")
    off = 0
    for n in sm_names:
        res[n] = tuple(t[off:off + rows[n]].reshape(sm_w[n][0].shape) for t in st)
        off += rows[n]

    order = ("ada_w", "ada_b", "norm_ffn1", "ffn1_w_in", "ffn1_w_out", "norm_mix", "mix_w_in", "hgrn_lb", "hgrn_g",
             "hgrn_w_o", "conv_w", "conv_b", "conv_ln_g", "conv_ln_b", "conv_w_o", "mix_w_out", "norm_ffn2",
             "ffn2_w_in", "ffn2_w_out", "norm_final")
    lead = lambda n, t: t[None] if n in big or n == "ada_w" else t
    outs = [loss, dx[None]]
    for j in range(4):
        outs += [lead(n, res[n][j]) for n in order]
    return tuple(outs)
```

```python
import functools

import jax
import jax.numpy as jnp
from jax import lax
from jax.experimental import pallas as pl
from jax.experimental.pallas import tpu as pltpu

F32 = jnp.float32
MM = jnp.bfloat16
ACT = jnp.bfloat16

D = 1024
D_FF = 2816
HEADS = 8
HD = 128
CHUNK = 64
SUB = 16
NSUB = CHUNK // SUB
CONV_K = 31
HALO = 32
EPS = 1e-6
N_DEV = 8
NEG = -1e30
Q_SCALE = HD ** -0.5

ADAM_LR = 0.001
ADAM_B1 = 0.9
ADAM_B2 = 0.999
ADAM_EPS = 1e-08
ADAM_WD = 0.01
ADAM_STEP = 10

VMEM_LIMIT = 60 * 1024 * 1024
MESH = pl.DeviceIdType.MESH


def _cparams(n_axes):
    return pltpu.CompilerParams(dimension_semantics=("arbitrary",) * n_axes, vmem_limit_bytes=VMEM_LIMIT)


def _mm(a, b):
    return lax.dot_general(a.astype(MM), b.astype(MM), (((1,), (0,)), ((), ())), preferred_element_type=F32)


def _mm_nt(a, b):
    return lax.dot_general(a.astype(MM), b.astype(MM), (((1,), (1,)), ((), ())), preferred_element_type=F32)


def _mm_tn(a, b):
    return lax.dot_general(a.astype(MM), b.astype(MM), (((0,), (0,)), ((), ())), preferred_element_type=F32)


def _sig(x):
    return 1.0 / (1.0 + jnp.exp(-x))


def _colsum(x):
    return jnp.sum(x, axis=0, keepdims=True)


def _rowmean(x):
    return jnp.mean(x, axis=-1, keepdims=True)


def _modnorm_fwd(xv, g, sh, sc):
    r = lax.rsqrt(_rowmean(xv * xv) + EPS)
    xh = xv * r
    n = xh * g
    return n * (1.0 + sc) + sh, xh, n, r


def _modnorm_bwd(dh, xh, n, r, g, sc):
    dsc = _colsum(dh * n)
    dsh = _colsum(dh)
    dn = dh * (1.0 + sc)
    dg = _colsum(dn * xh)
    dxh = dn * g
    dx = r * (dxh - xh * _rowmean(dxh * xh))
    return dx, dsh, dsc, dg


def _ffn_fwd(x, mod, mo, gnorm, w_in_t, w_out, res, name, carry):
    T = x.shape[0]
    tm = min(512, T)
    tn = D_FF // 2
    nj = D_FF // tn

    def body(x_ref, mod_ref, g_ref, wi_ref, wo_ref, xo_ref, a_ref, b_ref, f_ref, h_ref, acc_scr):
        j = pl.program_id(1)

        @pl.when(j == 0)
        def _():
            h, _, _, _ = _modnorm_fwd(x_ref[...], g_ref[...], mod_ref[mo:mo + 1, :], mod_ref[mo + 1:mo + 2, :])
            h_ref[...] = h.astype(ACT)
            acc_scr[...] = jnp.zeros_like(acc_scr)

        h = h_ref[...]
        a = _mm_nt(h, wi_ref[0])
        b = _mm_nt(h, wi_ref[1])
        a_ref[...] = a.astype(ACT)
        b_ref[...] = b.astype(ACT)
        s = a * _sig(a) * b
        acc_scr[...] += _mm(s, wo_ref[...])

        @pl.when(j == nj - 1)
        def _():
            f = acc_scr[...]
            f_ref[...] = f
            xo_ref[...] = x_ref[...] + res * mod_ref[mo + 2:mo + 3, :] * f

    out = _gridded(
        body, carry, name=name, grid=(T // tm, nj),
        in_specs=[
            pl.BlockSpec((tm, D), lambda i, j: (i, 0)),
            pl.BlockSpec((9, D), lambda i, j: (0, 0)),
            pl.BlockSpec((1, D), lambda i, j: (0, 0)),
            pl.BlockSpec((2, tn, D), lambda i, j: (0, j, 0)),
            pl.BlockSpec((tn, D), lambda i, j: (j, 0)),
        ],
        out_specs=[
            pl.BlockSpec((tm, D), lambda i, j: (i, 0)),
            pl.BlockSpec((tm, tn), lambda i, j: (i, j)),
            pl.BlockSpec((tm, tn), lambda i, j: (i, j)),
            pl.BlockSpec((tm, D), lambda i, j: (i, 0)),
            pl.BlockSpec((tm, D), lambda i, j: (i, 0)),
        ],
        out_shape=[
            jax.ShapeDtypeStruct((T, D), F32),
            jax.ShapeDtypeStruct((T, D_FF), ACT),
            jax.ShapeDtypeStruct((T, D_FF), ACT),
            jax.ShapeDtypeStruct((T, D), F32),
            jax.ShapeDtypeStruct((T, D), ACT),
        ],
        scratch_shapes=[pltpu.VMEM((tm, D), F32)],
    )(x, mod, gnorm, w_in_t, w_out)
    return out[:5], out[5:]


def _ffn_bwd_w(h, dxo, a, b, mod, mo, w_out, res, name, carry):
    T = h.shape[0]
    tm = min(1024, T)
    ni = T // tm
    tn = 256
    nj = D_FF // tn

    def body(h_ref, dxo_ref, a_ref, b_ref, mod_ref, wo_ref, da_ref, db_ref, dwi_ref, dwo_ref,
             acc_i, acc_o, df_all, h_all):
        j = pl.program_id(0)
        i = pl.program_id(1)

        @pl.when(i == 0)
        def _():
            acc_i[...] = jnp.zeros_like(acc_i)
            acc_o[...] = jnp.zeros_like(acc_o)

        @pl.when(j == 0)
        def _():
            df_all[i] = (res * mod_ref[mo + 2:mo + 3, :] * dxo_ref[...]).astype(MM)
            h_all[i] = h_ref[...]

        hb = h_all[i]
        df = df_all[i]
        av = a_ref[...].astype(F32)
        bv = b_ref[...].astype(F32)
        sg = _sig(av)
        sa = av * sg
        s = (sa * bv).astype(MM)
        ds = _mm_nt(df, wo_ref[...])
        da = (ds * bv * sg * (1.0 + av * (1.0 - sg))).astype(MM)
        db = (ds * sa).astype(MM)
        da_ref[...] = da
        db_ref[...] = db
        acc_o[...] += _mm_tn(s, df)
        acc_i[0] += _mm_tn(da, hb)
        acc_i[1] += _mm_tn(db, hb)

        @pl.when(i == ni - 1)
        def _():
            dwi_ref[...] = acc_i[...].astype(MM)
            dwo_ref[...] = acc_o[...].astype(MM)

    first = lambda j, i: (jnp.where(j == 0, i, ni - 1), 0)
    out = _gridded(
        body, carry, name=name, grid=(nj, ni),
        in_specs=[
            pl.BlockSpec((tm, D), first),
            pl.BlockSpec((tm, D), first),
            pl.BlockSpec((tm, tn), lambda j, i: (i, j)),
            pl.BlockSpec((tm, tn), lambda j, i: (i, j)),
            pl.BlockSpec((9, D), lambda j, i: (0, 0)),
            pl.BlockSpec((tn, D), lambda j, i: (j, 0)),
        ],
        out_specs=[
            pl.BlockSpec((tm, tn), lambda j, i: (i, j)),
            pl.BlockSpec((tm, tn), lambda j, i: (i, j)),
            pl.BlockSpec((2, tn, D), lambda j, i: (0, j, 0)),
            pl.BlockSpec((tn, D), lambda j, i: (j, 0)),
        ],
        out_shape=[
            jax.ShapeDtypeStruct((T, D_FF), MM),
            jax.ShapeDtypeStruct((T, D_FF), MM),
            jax.ShapeDtypeStruct((2, D_FF, D), MM),
            jax.ShapeDtypeStruct((D_FF, D), MM),
        ],
        scratch_shapes=[pltpu.VMEM((2, tn, D), F32), pltpu.VMEM((tn, D), F32),
                        pltpu.VMEM((ni, tm, D), MM), pltpu.VMEM((ni, tm, D), MM)],
    )(h, dxo, a, b, mod, w_out)
    return out[:4], out[4:]


def _ffn_bwd_x(x, dxo, f, da, db, mod, mo, gnorm, w_in_t, res, name, carry):
    T = x.shape[0]
    tm = min(512, T)
    ni = T // tm
    tn = D_FF // 2
    nj = D_FF // tn

    def body(x_ref, dxo_ref, f_ref, da_ref, db_ref, mod_ref, g_ref, wi_ref, dx_ref, sm_ref, dh_scr):
        j = pl.program_id(0)
        i = pl.program_id(1)

        @pl.when((j == 0) & (i == 0))
        def _():
            sm_ref[...] = jnp.zeros_like(sm_ref)

        @pl.when(j == 0)
        def _():
            dh_scr[i] = jnp.zeros((tm, D), F32)

        dh_scr[i] += _mm(da_ref[...], wi_ref[0]) + _mm(db_ref[...], wi_ref[1])

        @pl.when(j == nj - 1)
        def _():
            sc = mod_ref[mo + 1:mo + 2, :]
            _, xh, n, r = _modnorm_fwd(x_ref[...], g_ref[...], mod_ref[mo:mo + 1, :], sc)
            dxn, dsh, dsc, dg = _modnorm_bwd(dh_scr[i], xh, n, r, g_ref[...], sc)
            dxo_v = dxo_ref[...]
            dx_ref[...] = dxo_v + dxn
            sm_ref[0:1, :] += dsh
            sm_ref[1:2, :] += dsc
            sm_ref[2:3, :] += _colsum(dxo_v * f_ref[...]) * res
            sm_ref[3:4, :] += dg

    last = pl.BlockSpec((tm, D), lambda j, i: (jnp.where(j == nj - 1, i, 0), 0))
    out = _gridded(
        body, carry, name=name, grid=(nj, ni),
        in_specs=[last, last, last,
                  pl.BlockSpec((tm, tn), lambda j, i: (i, j)), pl.BlockSpec((tm, tn), lambda j, i: (i, j)),
                  pl.BlockSpec((9, D), lambda j, i: (0, 0)), pl.BlockSpec((1, D), lambda j, i: (0, 0)),
                  pl.BlockSpec((2, tn, D), lambda j, i: (0, j, 0))],
        out_specs=[last, pl.BlockSpec((8, D), lambda j, i: (0, 0))],
        out_shape=[jax.ShapeDtypeStruct((T, D), F32), jax.ShapeDtypeStruct((8, D), F32)],
        scratch_shapes=[pltpu.VMEM((ni, tm, D), F32)],
    )(x, dxo, f, da, db, mod, gnorm, w_in_t)
    return out[:2], out[2:]


def _head(x, target, gfin):
    T = x.shape[0]
    tm = min(512, T)
    ni = T // tm

    def body(x_ref, t_ref, g_ref, dx_ref, sm_ref):
        i = pl.program_id(0)

        @pl.when(i == 0)
        def _():
            sm_ref[...] = jnp.zeros_like(sm_ref)

        xv = x_ref[...]
        g = g_ref[...]
        r = lax.rsqrt(_rowmean(xv * xv) + EPS)
        xh = xv * r
        e = xh * g - t_ref[...]
        sm_ref[1:2, :] += _colsum(e * e) * (0.5 / D)
        dy = e * (1.0 / D)
        sm_ref[0:1, :] += _colsum(dy * xh)
        dxh = dy * g
        dx_ref[...] = r * (dxh - xh * _rowmean(dxh * xh))

        @pl.when(i == ni - 1)
        def _():
            sm_ref[1:2, :] = jnp.broadcast_to(jnp.sum(sm_ref[1:2, :], axis=-1, keepdims=True), (1, D))

    return pl.pallas_call(
        body, name="head_loss", grid=(ni,),
        in_specs=[pl.BlockSpec((tm, D), lambda i: (i, 0)), pl.BlockSpec((tm, D), lambda i: (i, 0)),
                  pl.BlockSpec((1, D), lambda i: (0, 0))],
        out_specs=[pl.BlockSpec((tm, D), lambda i: (i, 0)), pl.BlockSpec((8, D), lambda i: (0, 0))],
        out_shape=[jax.ShapeDtypeStruct((T, D), F32), jax.ShapeDtypeStruct((8, D), F32)],
        compiler_params=_cparams(1),
    )(x, target, gfin)


def _mixin_fwd(x, mod, mo, gnorm, w, carry):
    T = x.shape[0]
    tm = min(1024, T)
    ni = T // tm

    def body(x_ref, mod_ref, g_ref, w_ref, p_ref, h_ref, h_all):
        i = pl.program_id(1)

        @pl.when(pl.program_id(0) == 0)
        def _():
            h, _, _, _ = _modnorm_fwd(x_ref[...], g_ref[...], mod_ref[mo:mo + 1, :], mod_ref[mo + 1:mo + 2, :])
            h_all[i] = h.astype(ACT)
            h_ref[...] = h.astype(ACT)

        p_ref[0] = _mm(h_all[i], w_ref[0])

    first = lambda k, i: (jnp.where(k == 0, i, ni - 1), 0)
    out = _gridded(
        body, carry, name="mixin_fwd", grid=(8, ni),
        in_specs=[pl.BlockSpec((tm, D), first), pl.BlockSpec((9, D), lambda k, i: (0, 0)),
                  pl.BlockSpec((1, D), lambda k, i: (0, 0)), pl.BlockSpec((1, D, D), lambda k, i: (k, 0, 0))],
        out_specs=[pl.BlockSpec((1, tm, D), lambda k, i: (k, i, 0)), pl.BlockSpec((tm, D), first)],
        out_shape=[jax.ShapeDtypeStruct((8, T, D), F32), jax.ShapeDtypeStruct((T, D), ACT)],
        scratch_shapes=[pltpu.VMEM((ni, tm, D), ACT)],
    )(x, mod, gnorm, w)
    return out[:2], out[2:]


def _mixin_bwd(x, h, dxo, dp, mod, mo, gnorm, w, carry):
    T = x.shape[0]
    tm = min(512, T)
    ni = T // tm

    def body(x_ref, h_ref, dxo_ref, dp_ref, mod_ref, g_ref, w_ref, dx_ref, dw_ref, sm_ref, dh_scr, acc):
        k = pl.program_id(0)
        i = pl.program_id(1)

        @pl.when(i == 0)
        def _():
            acc[...] = jnp.zeros_like(acc)

        @pl.when(k == 0)
        def _():
            dh_scr[i] = jnp.zeros((tm, D), F32)

        @pl.when((k == 0) & (i == 0))
        def _():
            sm_ref[...] = jnp.zeros_like(sm_ref)

        dpk = dp_ref[0].astype(MM)
        acc[...] += _mm_tn(h_ref[...], dpk)
        dh_scr[i] += _mm_nt(dpk, w_ref[0])

        @pl.when(i == ni - 1)
        def _():
            dw_ref[0] = acc[...].astype(MM)

        @pl.when(k == 7)
        def _():
            sc = mod_ref[mo + 1:mo + 2, :]
            _, xh, n, r = _modnorm_fwd(x_ref[...], g_ref[...], mod_ref[mo:mo + 1, :], sc)
            dxn, dsh, dsc, dg = _modnorm_bwd(dh_scr[i], xh, n, r, g_ref[...], sc)
            dx_ref[...] = dxo_ref[...] + dxn
            sm_ref[0:1, :] += dsh
            sm_ref[1:2, :] += dsc
            sm_ref[3:4, :] += dg

    out = _gridded(
        body, carry, name="mixin_bwd", grid=(8, ni),
        in_specs=[pl.BlockSpec((tm, D), lambda k, i: (jnp.where(k == 7, i, 0), 0)),
                  pl.BlockSpec((tm, D), lambda k, i: (i, 0)),
                  pl.BlockSpec((tm, D), lambda k, i: (jnp.where(k == 7, i, 0), 0)),
                  pl.BlockSpec((1, tm, D), lambda k, i: (k, i, 0)), pl.BlockSpec((9, D), lambda k, i: (0, 0)),
                  pl.BlockSpec((1, D), lambda k, i: (0, 0)), pl.BlockSpec((1, D, D), lambda k, i: (k, 0, 0))],
        out_specs=[pl.BlockSpec((tm, D), lambda k, i: (jnp.where(k == 7, i, 0), 0)),
                   pl.BlockSpec((1, D, D), lambda k, i: (k, 0, 0)),
                   pl.BlockSpec((8, D), lambda k, i: (0, 0))],
        out_shape=[jax.ShapeDtypeStruct((T, D), F32), jax.ShapeDtypeStruct((8, D, D), MM),
                   jax.ShapeDtypeStruct((8, D), F32)],
        scratch_shapes=[pltpu.VMEM((ni, tm, D), F32), pltpu.VMEM((D, D), F32)],
    )(x, h, dxo, dp, mod, gnorm, w)
    return out[:3], out[3:]


def _hgrn_consts():
    rows = jnp.arange(SUB * HD) // HD
    e = (rows[:, None] == jnp.arange(HD)[None, :]).astype(MM)
    return e, e.T


def _rows_bcast(ref, cb, first, n):
    parts = [jnp.broadcast_to(ref[pl.ds(c * CHUNK + first, 1), :], (n, HD)) for c in range(cb // CHUNK)]
    return jnp.concatenate(parts, axis=0)


def _hgrn_pre(qr, fr, lb_ref, b_scr, cb):
    z = lb_ref[...]
    lb = _sig(z[0:1, :] - z[1:2, :])
    sq = _sig(qr)
    q = qr * sq * Q_SCALE
    sf = _sig(fr)
    fg = lb + (1.0 - lb) * sf
    lf = jnp.log(fg)
    k = 1.0 - fg
    tl = lax.broadcasted_iota(jnp.int32, (cb, HD), 0) % CHUNK
    bc = lf
    sh = 1
    while sh < CHUNK:
        bc = bc + jnp.where(tl >= sh, pltpu.roll(bc, sh, 0), 0.0)
        sh *= 2
    b_scr[...] = bc
    bl = _rows_bcast(b_scr, cb, CHUNK - 1, CHUNK)
    br = [None] + [_rows_bcast(b_scr, cb, SUB * i - 1, CHUNK) for i in range(1, NSUB)]
    sb = tl // SUB
    bref = jnp.where(sb == 0, bc, jnp.where(sb == 1, br[1], jnp.where(sb == 2, br[2], br[3])))
    eb = jnp.exp(bc)
    ekd = jnp.exp(bl - bc)
    eqo = jnp.exp(bc - bref)
    eko = [None] + [jnp.exp(jnp.where(tl < SUB * i, br[i] - bc, NEG)) for i in range(1, NSUB)]
    return dict(lb=lb, sq=sq, q=q, sf=sf, fg=fg, k=k, tl=tl, sb=sb, b=bc, bl=bl, eb=eb, ekd=ekd, eqo=eqo,
                eko=eko, qe=q * eb, kd=k * ekd, qo=q * eqo, ko=[None] + [k * eko[i] for i in range(1, NSUB)])


def _pad_rows(x):
    return jnp.concatenate([x, jnp.zeros_like(x)], axis=0)


def _by_subblock(sbc, parts):
    out = jnp.zeros_like(parts[1])
    for i in range(1, NSUB):
        out = jnp.where(sbc == i, parts[i], out)
    return out


def _hgrn_fwd(p, hgrn_lb, hgrn_g, carry):
    T = p.shape[1]
    cb = min(512, T)
    nch = cb // CHUNK
    ncb = T // cb
    e_mat, _ = _hgrn_consts()

    def body(p_ref, lb_ref, g_ref, e_ref, o_ref, oa_ref, a_ref, s_ref, st_scr, q_scr, k_scr, b_scr, z_scr):
        @pl.when(pl.program_id(1) == 0)
        def _():
            st_scr[...] = jnp.zeros_like(st_scr)

        v = p_ref[2]
        og = p_ref[3]
        pre = _hgrn_pre(p_ref[0], p_ref[1], lb_ref, b_scr, cb)
        q_scr[...] = pre["q"]
        k_scr[...] = pre["k"]
        ti = lax.broadcasted_iota(jnp.int32, (SUB, HD), 0)

        def zbody(c, carry):
            for i in range(NSUB):
                r0 = pl.multiple_of(c * CHUNK + SUB * i, SUB)
                qi = q_scr[pl.ds(r0, SUB), :]
                bi = b_scr[pl.ds(r0, SUB), :]
                for s in range(SUB):
                    krow = k_scr[pl.ds(r0 + s, 1), :]
                    brow = b_scr[pl.ds(r0 + s, 1), :]
                    if s < 8:
                        zz = qi * krow * jnp.exp(jnp.where(ti >= s, bi - brow, NEG))
                    else:
                        lo = qi[8:] * krow * jnp.exp(jnp.where(ti[8:] >= s, bi[8:] - brow, NEG))
                        zz = jnp.concatenate([jnp.zeros((8, HD), F32), lo], axis=0)
                    z_scr[i, pl.ds(pl.multiple_of(c * SUB, SUB), SUB), s * HD:(s + 1) * HD] = zz.astype(MM)
            return carry

        lax.fori_loop(0, nch, zbody, 0)
        adiag = [_mm(z_scr[i], e_ref[...]) for i in range(NSUB)]
        sbc = lax.broadcasted_iota(jnp.int32, (CHUNK, HD), 0) // SUB
        chunks = [slice(c * CHUNK, (c + 1) * CHUNK) for c in range(nch)]
        offs = [[_mm_nt(pre["qo"][rs], _pad_rows(pre["ko"][i][rs])) for i in range(1, NSUB)] for rs in chunks]
        kv = [_mm_tn(v[rs], pre["kd"][rs]) for rs in chunks]
        a_parts = []
        for c in range(nch):
            dparts = []
            for i in range(NSUB):
                blk = adiag[i][c * SUB:(c + 1) * SUB]
                dparts.append(blk if i == 0 else pltpu.roll(blk, SUB * i, 1))
            a_parts.append(_by_subblock(sbc, [None] + offs[c]) + jnp.concatenate(dparts, axis=0))
        a_ref[0] = jnp.concatenate(a_parts, axis=0)
        o_intra = [_mm(a_parts[c], _pad_rows(v[rs])) for c, rs in enumerate(chunks)]
        states = []
        st = st_scr[...]
        for c in range(nch):
            states.append(st)
            st = st * jnp.exp(b_scr[pl.ds(c * CHUNK + CHUNK - 1, 1), :]) + kv[c]
        st_scr[...] = st
        for c in range(nch):
            s_ref[0, c] = states[c]
        o = jnp.concatenate([o_intra[c] + _mm_nt(pre["qe"][rs], states[c]) for c, rs in enumerate(chunks)], axis=0)
        o_ref[...] = o
        on = o * lax.rsqrt(_rowmean(o * o) + EPS) * g_ref[...]
        oa_ref[...] = (on * og * _sig(og)).astype(ACT)

    out = _gridded(
        body, carry, name="hgrn_fwd", grid=(HEADS, ncb),
        in_specs=[pl.BlockSpec((4, cb, HD), lambda h, c: (0, c, h)),
                  pl.BlockSpec((2, HD), lambda h, c: (0, h)),
                  pl.BlockSpec((1, HD), lambda h, c: (0, h)),
                  pl.BlockSpec((SUB * HD, HD), lambda h, c: (0, 0))],
        out_specs=[pl.BlockSpec((cb, HD), lambda h, c: (c, h)),
                   pl.BlockSpec((cb, HD), lambda h, c: (c, h)),
                   pl.BlockSpec((1, cb, HD), lambda h, c: (h, c, 0)),
                   pl.BlockSpec((1, nch, HD, HD), lambda h, c: (h, c, 0, 0))],
        out_shape=[jax.ShapeDtypeStruct((T, D), F32), jax.ShapeDtypeStruct((T, D), ACT),
                   jax.ShapeDtypeStruct((HEADS, T, HD), F32),
                   jax.ShapeDtypeStruct((HEADS, T // CHUNK, HD, HD), F32)],
        scratch_shapes=[pltpu.VMEM((HD, HD), F32), pltpu.VMEM((cb, HD), F32), pltpu.VMEM((cb, HD), F32),
                        pltpu.VMEM((cb, HD), F32), pltpu.VMEM((NSUB, nch * SUB, SUB * HD), MM)],
    )(p, hgrn_lb, hgrn_g, e_mat)
    return out[:4], out[4:]


def _hgrn_bwd(p, o, a_all, s_all, doa, hgrn_lb, hgrn_g, dp, carry):
    T = p.shape[1]
    cb = min(512, T)
    nch = cb // CHUNK
    ncb = T // cb
    _, et_mat = _hgrn_consts()

    def body(p_ref, o_ref, a_ref, s_ref, doa_ref, lb_ref, g_ref, et_ref, dp_in, dp_ref, sm_ref,
             dst_scr, q_scr, k_scr, b_scr, x_scr, dqd_scr, dkd_scr):
        del dp_in

        @pl.when(pl.program_id(1) == 0)
        def _():
            dst_scr[...] = jnp.zeros_like(dst_scr)
            sm_ref[...] = jnp.zeros_like(sm_ref)

        qr = p_ref[0]
        v = p_ref[2]
        og = p_ref[3]
        pre = _hgrn_pre(qr, p_ref[1], lb_ref, b_scr, cb)
        q, k = pre["q"], pre["k"]
        q_scr[...] = q
        k_scr[...] = k
        g = g_ref[...]
        ov = o_ref[...]
        r = lax.rsqrt(_rowmean(ov * ov) + EPS)
        oh = ov * r
        sgo = _sig(og)
        doa_v = doa_ref[...]
        don = doa_v * og * sgo
        dog = doa_v * oh * g * sgo * (1.0 + og * (1.0 - sgo))
        sm_ref[1:2, :] += _colsum(don * oh)
        doh = don * g
        do = r * (doh - oh * _rowmean(doh * oh))

        sbc = lax.broadcasted_iota(jnp.int32, (CHUNK, HD), 0) // SUB
        row_i = lax.broadcasted_iota(jnp.int32, (CHUNK, HD), 0)
        lane_i = lax.broadcasted_iota(jnp.int32, (CHUNK, HD), 1)
        causal = lane_i <= row_i
        chunks = [slice(c * CHUNK, (c + 1) * CHUNK) for c in range(nch)]
        da_parts = [jnp.where(causal, _mm_nt(do[rs], _pad_rows(v[rs])), 0.0) for rs in chunks]
        dv_parts = [_mm_tn(a_ref[0, rs, :], do[rs])[:CHUNK] for rs in chunks]
        dqoff_mm = [[_mm(da_parts[c], _pad_rows(pre["ko"][i][rs])) for i in range(1, NSUB)]
                    for c, rs in enumerate(chunks)]
        dkoff_mm = [[_mm_tn(jnp.where(sbc == i, da_parts[c], 0.0), pre["qo"][rs])[:CHUNK] for i in range(1, NSUB)]
                    for c, rs in enumerate(chunks)]
        dqoff_parts = [_by_subblock(sbc, [None] + dqoff_mm[c]) for c in range(nch)]
        dkoff_parts = []
        for c, rs in enumerate(chunks):
            dko = pre["eko"][1][rs] * dkoff_mm[c][0]
            for i in range(2, NSUB):
                dko = dko + pre["eko"][i][rs] * dkoff_mm[c][i - 1]
            dkoff_parts.append(dko)
        for i in range(NSUB):
            rows = []
            for c in range(nch):
                blk = da_parts[c][SUB * i:SUB * (i + 1)]
                rows.append(blk if i == 0 else pltpu.roll(blk, HD - SUB * i, 1))
            x_scr[i] = _mm(jnp.concatenate(rows, axis=0), et_ref[...])
        ti = lax.broadcasted_iota(jnp.int32, (SUB, HD), 0)

        def dbody(c, carry):
            for i in range(NSUB):
                r0 = pl.multiple_of(c * CHUNK + SUB * i, SUB)
                qi = q_scr[pl.ds(r0, SUB), :]
                bi = b_scr[pl.ds(r0, SUB), :]
                dq_hi = jnp.zeros((8, HD), F32)
                dq_lo = jnp.zeros((8, HD), F32)
                dk_hi = jnp.zeros((8, HD), F32)
                dk_lo = jnp.zeros((8, HD), F32)
                c0 = pl.multiple_of(c * SUB, SUB)
                t8 = ti[:8]
                for s in range(SUB):
                    krow = k_scr[pl.ds(r0 + s, 1), :]
                    brow = b_scr[pl.ds(r0 + s, 1), :]
                    w_lo = (x_scr[i, pl.ds(c0 + 8, 8), s * HD:(s + 1) * HD]
                            * jnp.exp(jnp.where(t8 + 8 >= s, bi[8:] - brow, NEG)))
                    dq_lo = dq_lo + w_lo * krow
                    col = _colsum(w_lo * qi[8:])
                    if s < 8:
                        w_hi = (x_scr[i, pl.ds(c0, 8), s * HD:(s + 1) * HD]
                                * jnp.exp(jnp.where(t8 >= s, bi[:8] - brow, NEG)))
                        dq_hi = dq_hi + w_hi * krow
                        dk_hi = jnp.where(t8 == s, col + _colsum(w_hi * qi[:8]), dk_hi)
                    else:
                        dk_lo = jnp.where(t8 + 8 == s, col, dk_lo)
                dqd_scr[pl.ds(r0, SUB), :] = jnp.concatenate([dq_hi, dq_lo], axis=0)
                dkd_scr[pl.ds(r0, SUB), :] = jnp.concatenate([dk_hi, dk_lo], axis=0)
            return carry

        lax.fori_loop(0, nch, dbody, 0)
        qdo = [_mm_tn(do[rs], pre["qe"][rs]) for rs in chunks]
        dsts = [None] * nch
        dst = dst_scr[...]
        for c in reversed(range(nch)):
            dsts[c] = dst
            dst = dst * jnp.exp(b_scr[pl.ds(c * CHUNK + CHUNK - 1, 1), :]) + qdo[c]
        dst_scr[...] = dst
        sts = [s_ref[0, c] for c in range(nch)]
        dqe_parts = [_mm(do[rs], sts[c]) for c, rs in enumerate(chunks)]
        dkdec_parts = [_mm(v[rs], dsts[c]) for c, rs in enumerate(chunks)]
        dvi_parts = [_mm_nt(pre["kd"][rs], dsts[c]) for c, rs in enumerate(chunks)]
        debl_parts = [_colsum(dsts[c] * sts[c]) for c in range(nch)]
        dqe = jnp.concatenate(dqe_parts, axis=0)
        dkdec = jnp.concatenate(dkdec_parts, axis=0)
        dq_tot = jnp.concatenate(dqoff_parts, axis=0) * pre["eqo"] + dqd_scr[...] + dqe * pre["eb"]
        dk_inter = dkdec * pre["ekd"]
        dk_tot = jnp.concatenate(dkoff_parts, axis=0) + dkd_scr[...] + dk_inter
        db = q * dq_tot - k * dk_tot
        kdk = k * dk_inter
        dbl = jnp.concatenate(
            [jnp.broadcast_to(jnp.exp(b_scr[pl.ds(c * CHUNK + CHUNK - 1, 1), :]) * debl_parts[c]
                              + _colsum(kdk[c * CHUNK:(c + 1) * CHUNK]), (CHUNK, HD)) for c in range(nch)], axis=0)
        tl = pre["tl"]
        rc = db
        sh = 1
        while sh < CHUNK:
            rc = rc + jnp.where(tl + sh < CHUNK, pltpu.roll(rc, cb - sh, 0), 0.0)
            sh *= 2
        dlf = rc + dbl
        dfg = dlf / pre["fg"] - dk_tot
        sf = pre["sf"]
        lb = pre["lb"]
        sm_ref[0:1, :] += _colsum(dfg * (1.0 - sf))
        sq = pre["sq"]
        dp_ref[0] = (dq_tot * Q_SCALE * sq * (1.0 + qr * (1.0 - sq))).astype(ACT)
        dp_ref[1] = (dfg * (1.0 - lb) * sf * (1.0 - sf)).astype(ACT)
        dp_ref[2] = (jnp.concatenate(dv_parts, axis=0) + jnp.concatenate(dvi_parts, axis=0)).astype(ACT)
        dp_ref[3] = dog.astype(ACT)

    rev = lambda c: ncb - 1 - c
    out = _gridded(
        body, carry, name="hgrn_bwd", grid=(HEADS, ncb),
        in_specs=[pl.BlockSpec((4, cb, HD), lambda h, c: (0, rev(c), h)),
                  pl.BlockSpec((cb, HD), lambda h, c: (rev(c), h)),
                  pl.BlockSpec((1, cb, HD), lambda h, c: (h, rev(c), 0)),
                  pl.BlockSpec((1, nch, HD, HD), lambda h, c: (h, rev(c), 0, 0)),
                  pl.BlockSpec((cb, HD), lambda h, c: (rev(c), h)),
                  pl.BlockSpec((2, HD), lambda h, c: (0, h)),
                  pl.BlockSpec((1, HD), lambda h, c: (0, h)),
                  pl.BlockSpec((HD, SUB * HD), lambda h, c: (0, 0)),
                  pl.BlockSpec(memory_space=pl.ANY)],
        out_specs=[pl.BlockSpec((4, cb, HD), lambda h, c: (0, rev(c), h)),
                   pl.BlockSpec((8, HD), lambda h, c: (0, h))],
        out_shape=[jax.ShapeDtypeStruct(dp.shape, dp.dtype), jax.ShapeDtypeStruct((8, D), F32)],
        aliases={8: 0},
        scratch_shapes=[pltpu.VMEM((HD, HD), F32), pltpu.VMEM((cb, HD), F32), pltpu.VMEM((cb, HD), F32),
                        pltpu.VMEM((cb, HD), F32), pltpu.VMEM((NSUB, nch * SUB, SUB * HD), F32),
                        pltpu.VMEM((cb, HD), F32), pltpu.VMEM((cb, HD), F32)],
    )(p, o, a_all, s_all, doa, hgrn_lb, hgrn_g, et_mat, dp)
    return out[:2], out[2:]


def _ln_fwd(u1, g, b):
    mu = _rowmean(u1)
    xc = u1 - mu
    rs = lax.rsqrt(_rowmean(xc * xc) + EPS)
    xh = xc * rs
    return xh * g + b, xh, rs


CONV_RB = 64
LANES = 128


def _shift_rows(src, sh, ls, n):
    for r in range(1, 8):
        sh[r - 1, 0:n, :] = src[pl.ds(r, n), ls]


def _tap(src, sh, ls, off, r0, rows):
    r = off % 8
    if r == 0:
        return src[pl.ds(r0 + off, rows), ls]
    return sh[r - 1, pl.ds(r0 + off - r, rows), :]


def _conv_fwd(p, cw, cb_, lng, lnb):
    T = p.shape[1]
    tm = min(512, T)
    n = HALO + tm - 8

    def body(p_ref, cw_ref, cb_ref, g_ref, b_ref, u1_ref, u2_ref, buf, sh):
        @pl.when(pl.program_id(0) == 0)
        def _():
            buf[0:HALO, :] = jnp.zeros((HALO, D), F32)

        buf[HALO:HALO + tm, :] = p_ref[0] * _sig(p_ref[1])
        for lb in range(D // LANES):
            ls = slice(lb * LANES, (lb + 1) * LANES)
            _shift_rows(buf, sh, ls, n)
            taps = [cw_ref[j:j + 1, ls] for j in range(CONV_K)]
            bias = cb_ref[:, ls]

            def rows_body(rb, carry):
                r0 = pl.multiple_of(rb * CONV_RB, CONV_RB)
                acc = jnp.broadcast_to(bias, (CONV_RB, LANES))
                for j in range(CONV_K):
                    acc = acc + taps[j] * _tap(buf, sh, ls, HALO - (CONV_K - 1) + j, r0, CONV_RB)
                u1_ref[pl.ds(r0, CONV_RB), ls] = acc
                return carry

            lax.fori_loop(0, tm // CONV_RB, rows_body, 0)
        y, _, _ = _ln_fwd(u1_ref[...], g_ref[...], b_ref[...])
        u2_ref[...] = (y * _sig(y)).astype(ACT)
        buf[0:HALO, :] = buf[tm:tm + HALO, :]

    return pl.pallas_call(
        body, name="conv_fwd", grid=(T // tm,),
        in_specs=[pl.BlockSpec((2, tm, D), lambda i: (2, i, 0)), pl.BlockSpec((HALO, D), lambda i: (0, 0)),
                  pl.BlockSpec((1, D), lambda i: (0, 0)), pl.BlockSpec((1, D), lambda i: (0, 0)),
                  pl.BlockSpec((1, D), lambda i: (0, 0))],
        out_specs=[pl.BlockSpec((tm, D), lambda i: (i, 0)), pl.BlockSpec((tm, D), lambda i: (i, 0))],
        out_shape=[jax.ShapeDtypeStruct((T, D), F32), jax.ShapeDtypeStruct((T, D), ACT)],
        scratch_shapes=[pltpu.VMEM((HALO + tm, D), F32), pltpu.VMEM((7, n, LANES), F32)],
        compiler_params=_cparams(1),
    )(p, cw, cb_, lng, lnb)


def _conv_bwd(p, u1, du2, cw, lng, lnb, dp):
    T = p.shape[1]
    tm = min(512, T)
    ni = T // tm
    hb = tm // HALO

    n = HALO + tm - 8

    def body(p_ref, ph_ref, u1_ref, du2_ref, cw_ref, g_ref, b_ref, dp_in, dp_ref, dcw_ref, sm_ref, ubuf, dbuf,
             sh, dacc):
        del dp_in
        step = pl.program_id(0)

        @pl.when(step == 0)
        def _():
            dbuf[tm:tm + HALO, :] = jnp.zeros((HALO, D), F32)
            dcw_ref[...] = jnp.zeros_like(dcw_ref)
            sm_ref[...] = jnp.zeros_like(sm_ref)

        ua = p_ref[0]
        sgb = _sig(p_ref[1])
        halo = ph_ref[0] * _sig(ph_ref[1])
        ubuf[0:HALO, :] = jnp.where(step == ni - 1, 0.0, halo)
        ubuf[HALO:HALO + tm, :] = ua * sgb
        g = g_ref[...]
        y, xh, rs = _ln_fwd(u1_ref[...], g, b_ref[...])
        sy = _sig(y)
        dy = du2_ref[...] * sy * (1.0 + y * (1.0 - sy))
        sm_ref[1:2, :] += _colsum(dy * xh)
        sm_ref[2:3, :] += _colsum(dy)
        dxh = dy * g
        du1 = rs * (dxh - _rowmean(dxh) - xh * _rowmean(dxh * xh))
        sm_ref[0:1, :] += _colsum(du1)
        dbuf[0:tm, :] = du1
        for lb in range(D // LANES):
            ls = slice(lb * LANES, (lb + 1) * LANES)
            taps = [cw_ref[j:j + 1, ls] for j in range(CONV_K)]
            _shift_rows(dbuf, sh, ls, n)

            def du0_body(rb, carry):
                r0 = pl.multiple_of(rb * CONV_RB, CONV_RB)
                acc = jnp.zeros((CONV_RB, LANES), F32)
                for j in range(CONV_K):
                    acc = acc + taps[j] * _tap(dbuf, sh, ls, CONV_K - 1 - j, r0, CONV_RB)
                dp_ref[0, pl.ds(r0, CONV_RB), ls] = acc.astype(ACT)
                return carry

            lax.fori_loop(0, tm // CONV_RB, du0_body, 0)
            _shift_rows(ubuf, sh, ls, n)
            dacc[...] = jnp.zeros_like(dacc)

            def dcw_body(rb, carry):
                r0 = pl.multiple_of(rb * CONV_RB, CONV_RB)
                d = dbuf[pl.ds(r0, CONV_RB), ls]
                for j in range(CONV_K):
                    prod = d * _tap(ubuf, sh, ls, HALO - (CONV_K - 1) + j, r0, CONV_RB)
                    dacc[8 * j:8 * j + 8, :] += jnp.sum(prod.reshape(CONV_RB // 8, 8, LANES), axis=0)
                return carry

            lax.fori_loop(0, tm // CONV_RB, dcw_body, 0)
            for j in range(CONV_K):
                dcw_ref[j:j + 1, ls] += _colsum(dacc[8 * j:8 * j + 8, :])
        du0 = dp_ref[0].astype(F32)
        dp_ref[0] = (du0 * sgb).astype(ACT)
        dp_ref[1] = (du0 * ua * sgb * (1.0 - sgb)).astype(ACT)
        dbuf[tm:tm + HALO, :] = dbuf[0:HALO, :]

    rev = lambda i: ni - 1 - i
    return pl.pallas_call(
        body, name="conv_bwd", grid=(ni,),
        in_specs=[pl.BlockSpec((2, tm, D), lambda i: (2, rev(i), 0)),
                  pl.BlockSpec((2, HALO, D), lambda i: (2, jnp.maximum(rev(i) * hb - 1, 0), 0)),
                  pl.BlockSpec((tm, D), lambda i: (rev(i), 0)), pl.BlockSpec((tm, D), lambda i: (rev(i), 0)),
                  pl.BlockSpec((HALO, D), lambda i: (0, 0)), pl.BlockSpec((1, D), lambda i: (0, 0)),
                  pl.BlockSpec((1, D), lambda i: (0, 0)), pl.BlockSpec(memory_space=pl.ANY)],
        out_specs=[pl.BlockSpec((2, tm, D), lambda i: (2, rev(i), 0)),
                   pl.BlockSpec((HALO, D), lambda i: (0, 0)), pl.BlockSpec((8, D), lambda i: (0, 0))],
        out_shape=[jax.ShapeDtypeStruct(dp.shape, dp.dtype), jax.ShapeDtypeStruct((HALO, D), F32),
                   jax.ShapeDtypeStruct((8, D), F32)],
        input_output_aliases={7: 0},
        scratch_shapes=[pltpu.VMEM((HALO + tm, D), F32), pltpu.VMEM((tm + HALO, D), F32),
                        pltpu.VMEM((7, n, LANES), F32), pltpu.VMEM((8 * CONV_K, LANES), F32)],
        compiler_params=_cparams(1),
    )(p, p, u1, du2, cw, lng, lnb, dp)


def _mixout_fwd(x, oa, u2, p, mod, mo, w_a, w_b, w_o):
    T = x.shape[0]
    tm = min(512, T)

    def body(x_ref, oa_ref, u2_ref, p_ref, mod_ref, wa_ref, wb_ref, wo_ref, xo_ref, ya_ref, yb_ref, mo_ref):
        ya = _mm(oa_ref[...], wa_ref[...])
        yb = _mm(u2_ref[...], wb_ref[...])
        ya_ref[...] = ya.astype(ACT)
        yb_ref[...] = yb.astype(ACT)
        merged = _sig(p_ref[0]) * ya + _sig(p_ref[1]) * yb
        out = _mm(merged, wo_ref[...])
        mo_ref[...] = out
        xo_ref[...] = x_ref[...] + mod_ref[mo + 2:mo + 3, :] * out

    tile = pl.BlockSpec((tm, D), lambda i: (i, 0))
    wspec = pl.BlockSpec((D, D), lambda i: (0, 0))
    return pl.pallas_call(
        body, name="mixout_fwd", grid=(T // tm,),
        in_specs=[tile, tile, tile, pl.BlockSpec((2, tm, D), lambda i: (3, i, 0)),
                  pl.BlockSpec((9, D), lambda i: (0, 0)), wspec, wspec, wspec],
        out_specs=[tile, tile, tile, tile],
        out_shape=[jax.ShapeDtypeStruct((T, D), F32), jax.ShapeDtypeStruct((T, D), ACT),
                   jax.ShapeDtypeStruct((T, D), ACT), jax.ShapeDtypeStruct((T, D), F32)],
        compiler_params=_cparams(1),
    )(x, oa, u2, p, mod, w_a, w_b, w_o)


def _mixout_bwd(dxo, oa, u2, ya, yb, mout, p, mod, mo, w_a, w_b, w_o):
    T = dxo.shape[0]
    tm = min(256, T)

    def body(dxo_ref, oa_ref, u2_ref, ya_ref, yb_ref, mo_ref, p_ref, mod_ref, wa_ref, wb_ref, wo_ref,
             dp_ref, doa_ref, du2_ref, dwa_ref, dwb_ref, dwo_ref, sm_ref):
        @pl.when(pl.program_id(0) == 0)
        def _():
            dwa_ref[...] = jnp.zeros_like(dwa_ref)
            dwb_ref[...] = jnp.zeros_like(dwb_ref)
            dwo_ref[...] = jnp.zeros_like(dwo_ref)
            sm_ref[...] = jnp.zeros_like(sm_ref)

        dxo_v = dxo_ref[...]
        sm_ref[2:3, :] += _colsum(dxo_v * mo_ref[...])
        dmo = (mod_ref[mo + 2:mo + 3, :] * dxo_v).astype(MM)
        ya = ya_ref[...].astype(F32)
        yb = yb_ref[...].astype(F32)
        sga = _sig(p_ref[0])
        sgb = _sig(p_ref[1])
        merged = (sga * ya + sgb * yb).astype(MM)
        dwo_ref[...] += _mm_tn(merged, dmo)
        dmg = _mm_nt(dmo, wo_ref[...])
        dp_ref[0] = (dmg * ya * sga * (1.0 - sga)).astype(ACT)
        dp_ref[1] = (dmg * yb * sgb * (1.0 - sgb)).astype(ACT)
        dya = (dmg * sga).astype(MM)
        dyb = (dmg * sgb).astype(MM)
        dwa_ref[...] += _mm_tn(oa_ref[...], dya)
        dwb_ref[...] += _mm_tn(u2_ref[...], dyb)
        doa_ref[...] = _mm_nt(dya, wa_ref[...])
        du2_ref[...] = _mm_nt(dyb, wb_ref[...])

    tile = pl.BlockSpec((tm, D), lambda i: (i, 0))
    wspec = pl.BlockSpec((D, D), lambda i: (0, 0))
    return pl.pallas_call(
        body, name="mixout_bwd", grid=(T // tm,),
        in_specs=[tile, tile, tile, tile, tile, tile, pl.BlockSpec((2, tm, D), lambda i: (3, i, 0)),
                  pl.BlockSpec((9, D), lambda i: (0, 0)), wspec, wspec, wspec],
        out_specs=[pl.BlockSpec((2, tm, D), lambda i: (3, i, 0)), tile, tile, wspec, wspec, wspec,
                   pl.BlockSpec((8, D), lambda i: (0, 0))],
        out_shape=[jax.ShapeDtypeStruct((8, T, D), ACT), jax.ShapeDtypeStruct((T, D), F32),
                   jax.ShapeDtypeStruct((T, D), F32), jax.ShapeDtypeStruct((D, D), F32),
                   jax.ShapeDtypeStruct((D, D), F32), jax.ShapeDtypeStruct((D, D), F32),
                   jax.ShapeDtypeStruct((8, D), F32)],
        compiler_params=_cparams(1),
    )(dxo, oa, u2, ya, yb, mout, p, mod, w_a, w_b, w_o)


def _ada_wgrad(cs_all, dmod_cols):
    cs_t = jnp.pad(cs_all.T, ((0, 0), (0, HD - N_DEV)))
    dm = jnp.pad(dmod_cols, ((0, HD - N_DEV), (0, 0)))

    def body(cs_ref, d_ref, out_ref):
        out_ref[...] = jnp.dot(cs_ref[...], d_ref[...], preferred_element_type=F32,
                               precision=lax.Precision.HIGHEST)

    return pl.pallas_call(
        body, name="ada_wgrad", out_shape=jax.ShapeDtypeStruct((D, dmod_cols.shape[1]), F32),
        compiler_params=pltpu.CompilerParams(vmem_limit_bytes=VMEM_LIMIT),
    )(cs_t, dm)


def _adam_math(w, g, m, v):
    m2 = ADAM_B1 * m + (1.0 - ADAM_B1) * g
    v2 = ADAM_B2 * v + (1.0 - ADAM_B2) * (g * g)
    m_hat = m2 / (1.0 - ADAM_B1 ** ADAM_STEP)
    v_hat = v2 / (1.0 - ADAM_B2 ** ADAM_STEP)
    delta = -ADAM_LR * (m_hat / (jnp.sqrt(v_hat) + ADAM_EPS) + ADAM_WD * w)
    return delta, m2, v2


def _adamw(w, m, v, g, name):
    R, C = w.shape
    slots = g.ndim == 3
    n_slots = g.shape[0] if slots else 0
    tr = R
    for cand in (256, 176):
        if R % cand == 0 and R > cand:
            tr = cand
            break

    def body(w_ref, m_ref, v_ref, g_ref, go_ref, d_ref, mo_ref, vo_ref):
        if slots:
            gv = g_ref[0].astype(F32)
            for s in range(1, n_slots):
                gv = gv + g_ref[s].astype(F32)
        else:
            gv = g_ref[...]
        go_ref[...] = gv
        d_ref[...], mo_ref[...], vo_ref[...] = _adam_math(w_ref[...], gv, m_ref[...], v_ref[...])

    tile = pl.BlockSpec((tr, C), lambda i: (i, 0))
    gspec = pl.BlockSpec((n_slots, tr, C), lambda i: (0, i, 0)) if slots else tile
    sds = jax.ShapeDtypeStruct((R, C), F32)
    return pl.pallas_call(
        body, name=name, grid=(R // tr,), in_specs=[tile, tile, tile, gspec], out_specs=[tile] * 4,
        out_shape=[sds] * 4, compiler_params=_cparams(1),
    )(w, m, v, g)


def _sum_slots(pack, name, tr):
    n, R, C = pack.shape

    def body(p_ref, out_ref):
        acc = p_ref[0].astype(F32)
        for s in range(1, n):
            acc = acc + p_ref[s].astype(F32)
        out_ref[...] = acc

    return pl.pallas_call(
        body, name=name, grid=(R // tr,), in_specs=[pl.BlockSpec((n, tr, C), lambda i: (0, i, 0))],
        out_specs=pl.BlockSpec((tr, C), lambda i: (i, 0)), out_shape=jax.ShapeDtypeStruct((R, C), F32),
        compiler_params=_cparams(1))(pack)


def _me():
    return lax.axis_index("x"), lax.axis_index("y"), lax.axis_index("c")


def _peer(r):
    x, y, c = _me()
    px = 1 - x if r & 4 else x
    py = 1 - y if r & 2 else y
    pc = 1 - c if r & 1 else c
    return (px, py, pc), 4 * px + 2 * py + pc


def _small_gather(x_ref, out_ref, send_sems, recv_sems):
    R = x_ref.shape[0]
    mx, my, mc = _me()
    me = 4 * mx + 2 * my + mc
    mine = out_ref.at[pl.ds(pl.multiple_of(me * R, 8), R), :]
    copies = []
    for r in range(1, N_DEV):
        dev, _ = _peer(r)
        copies.append(pltpu.make_async_remote_copy(
            src_ref=x_ref, dst_ref=mine, send_sem=send_sems.at[r - 1], recv_sem=recv_sems.at[r - 1],
            device_id=dev, device_id_type=MESH))
    for cp in copies:
        cp.start()
    mine[...] = x_ref[...]
    for r in range(1, N_DEV):
        dev, idx = _peer(r)
        theirs = out_ref.at[pl.ds(pl.multiple_of(idx * R, 8), R), :]
        pltpu.make_async_remote_copy(
            src_ref=x_ref, dst_ref=theirs, send_sem=send_sems.at[r - 1], recv_sem=recv_sems.at[r - 1],
            device_id=dev, device_id_type=MESH).wait_recv()
    for cp in copies:
        cp.wait_send()


def _prologue(cs, ada_w, ada_b_cols, big):
    n = len(big)
    ncol = ada_w.shape[1]
    big_shape, big_sems = _xchg_specs(big, "gather")

    def body(cs_ref, w_ref, b_ref, *rest):
        big_in, cs_all, mod_all, big_out = rest[:n], rest[n], rest[n + 1], rest[n + 2:2 * n + 2]
        mod_scr, s1, r1, s2, r2 = rest[2 * n + 2:2 * n + 7]
        sems = rest[2 * n + 7:]
        _xchg_start(big_in, big_out, sems, "gather")
        _small_gather(cs_ref, cs_all, s1, r1)
        pick = (lax.broadcasted_iota(jnp.int32, (N_DEV, N_DEV * 8), 1)
                == 8 * lax.broadcasted_iota(jnp.int32, (N_DEV, N_DEV * 8), 0)).astype(F32)
        per_device = jnp.dot(pick, cs_all[...], preferred_element_type=F32, precision=lax.Precision.HIGHEST)
        mod_scr[...] = jnp.dot(per_device, w_ref[...], preferred_element_type=F32,
                               precision=lax.Precision.HIGHEST) + b_ref[...]
        _small_gather(mod_scr, mod_all, s2, r2)
        _xchg_wait(big_in, big_out, sems, "gather")

    vmem = pl.BlockSpec(memory_space=pltpu.VMEM)
    hbm = pl.BlockSpec(memory_space=pl.ANY)
    dma7 = pltpu.SemaphoreType.DMA((N_DEV - 1,))
    out = pl.pallas_call(
        body, name="prologue",
        out_shape=[jax.ShapeDtypeStruct((N_DEV * 8, D), F32), jax.ShapeDtypeStruct((N_DEV * 8, ncol), F32)]
        + big_shape,
        in_specs=[vmem, vmem, vmem] + [hbm] * n, out_specs=[vmem, vmem] + [hbm] * n,
        scratch_shapes=[pltpu.VMEM((8, ncol), F32), dma7, dma7, dma7, dma7] + big_sems,
        compiler_params=pltpu.CompilerParams(vmem_limit_bytes=VMEM_LIMIT),
    )(cs, ada_w, ada_b_cols, *big)
    return out[0], out[1], out[2:]


def _allgather_small(x):
    R, C = x.shape

    def body(x_ref, out_ref, send_sems, recv_sems):
        _small_gather(x_ref, out_ref, send_sems, recv_sems)

    return pl.pallas_call(
        body, name="allgather_small_%dx%d" % (R, C),
        out_shape=jax.ShapeDtypeStruct((N_DEV * R, C), F32),
        in_specs=[pl.BlockSpec(memory_space=pltpu.VMEM)], out_specs=pl.BlockSpec(memory_space=pltpu.VMEM),
        scratch_shapes=[pltpu.SemaphoreType.DMA((N_DEV - 1,)), pltpu.SemaphoreType.DMA((N_DEV - 1,))],
    )(x)


N_CHIP = N_DEV // 2


def _xchg_copies(ins, outs, sems, mode):
    send_sems, recv_sems, local_sems = sems
    mx, my, mc = _me()
    me = 4 * mx + 2 * my + mc
    my_chip = 2 * mx + my
    sibling = _peer(1)[0]

    def rdma(a, r, dev, src, slot):
        k = a * (N_DEV - 1) + r - 1
        return pltpu.make_async_remote_copy(
            src_ref=src, dst_ref=outs[a].at[slot], send_sem=send_sems.at[k], recv_sem=recv_sems.at[k],
            device_id=dev, device_id_type=MESH)

    own, sends, relays, recvs = [], [], [], []
    for a in range(len(ins)):
        if mode == "pair":
            for chip in range(N_CHIP):
                src = ins[a].at[2 * chip + 1 - mc]
                sends.append(rdma(a, chip + 1, sibling, src, chip))
                recvs.append(rdma(a, chip + 1, sibling, src, chip))
            continue
        if mode == "quad":
            own.append(pltpu.make_async_copy(ins[a].at[my_chip], outs[a].at[my_chip], local_sems.at[a]))
            for r in (2, 4, 6):
                dev, idx = _peer(r)
                chip = idx // 2
                sends.append(rdma(a, r, dev, ins[a].at[chip], my_chip))
                recvs.append(rdma(a, r, dev, ins[a].at[chip], chip))
            continue
        gather = mode == "gather"
        own.append(pltpu.make_async_copy(ins[a] if gather else ins[a].at[me], outs[a].at[me], local_sems.at[a]))
        for r in range(1, N_DEV):
            dev, idx = _peer(r)
            if not gather:
                sends.append(rdma(a, r, dev, ins[a].at[idx], me))
                recvs.append(rdma(a, r, dev, ins[a].at[idx], idx))
            elif r == 1:
                sends.append(rdma(a, r, dev, ins[a], me))
                recvs.append(rdma(a, r, dev, ins[a], idx))
            elif r % 2 == 0:
                sends.append(rdma(a, r, dev, ins[a], me))
                relays.append((rdma(a, r, dev, ins[a], idx), rdma(a, r + 1, sibling, outs[a].at[idx], idx)))
            else:
                recvs.append(rdma(a, r, sibling, ins[a], idx))
    return own, sends, relays, recvs


def _xchg_start(ins, outs, sems, mode):
    own, sends, _, _ = _xchg_copies(ins, outs, sems, mode)
    for cp in own + sends:
        cp.start()


def _xchg_wait(ins, outs, sems, mode):
    own, sends, relays, recvs = _xchg_copies(ins, outs, sems, mode)
    for arrival, relay in relays:
        arrival.wait_recv()
        relay.start()
    for cp in recvs:
        cp.wait_recv()
    for cp in own:
        cp.wait()
    for cp in sends + [relay for _, relay in relays]:
        cp.wait_send()


def _xchg_specs(arrays, mode):
    n = len(arrays)
    shape = {"gather": lambda s: (N_DEV,) + s, "scatter": lambda s: s, "pair": lambda s: (N_CHIP,) + s[1:],
             "quad": lambda s: s}[mode]
    out_shape = [jax.ShapeDtypeStruct(shape(a.shape), a.dtype) for a in arrays]
    sems = [pltpu.SemaphoreType.DMA((n * (N_DEV - 1),)), pltpu.SemaphoreType.DMA((n * (N_DEV - 1),)),
            pltpu.SemaphoreType.DMA((n,))]
    return out_shape, sems


def _exchange(arrays, mode, name):
    n = len(arrays)

    def body(*refs):
        _xchg_start(refs[:n], refs[n:2 * n], refs[2 * n:], mode)
        _xchg_wait(refs[:n], refs[n:2 * n], refs[2 * n:], mode)

    out_shape, sems = _xchg_specs(arrays, mode)
    return pl.pallas_call(
        body, name=name, out_shape=out_shape,
        in_specs=[pl.BlockSpec(memory_space=pl.ANY)] * n, out_specs=[pl.BlockSpec(memory_space=pl.ANY)] * n,
        scratch_shapes=sems,
    )(*arrays)


def _gridded(body, carry, *, name, grid, in_specs, out_specs, out_shape, scratch_shapes=(), aliases=None):
    if carry is None:
        return pl.pallas_call(
            body, name=name, grid=grid, in_specs=list(in_specs), out_specs=list(out_specs),
            out_shape=list(out_shape), scratch_shapes=list(scratch_shapes), input_output_aliases=aliases or {},
            compiler_params=_cparams(len(grid)))
    arrays, mode = carry
    n, n_in, n_out, n_scr = len(arrays), len(in_specs), len(out_specs), len(scratch_shapes)
    c_shape, c_sems = _xchg_specs(arrays, mode)

    def wrapped(*refs):
        ins, cin = refs[:n_in], refs[n_in:n_in + n]
        o0 = n_in + n
        outs, cout = refs[o0:o0 + n_out], refs[o0 + n_out:o0 + n_out + n]
        s0 = o0 + n_out + n
        scr, sems = refs[s0:s0 + n_scr], refs[s0 + n_scr:]
        first = pl.program_id(0) == 0
        last = pl.program_id(0) == grid[0] - 1
        for ax in range(1, len(grid)):
            first = first & (pl.program_id(ax) == 0)
            last = last & (pl.program_id(ax) == grid[ax] - 1)

        @pl.when(first)
        def _():
            _xchg_start(cin, cout, sems, mode)

        body(*ins, *outs, *scr)

        @pl.when(last)
        def _():
            _xchg_wait(cin, cout, sems, mode)

    hbm = pl.BlockSpec(memory_space=pl.ANY)
    res = pl.pallas_call(
        wrapped, name=name, grid=grid, in_specs=list(in_specs) + [hbm] * n, out_specs=list(out_specs) + [hbm] * n,
        out_shape=list(out_shape) + c_shape, scratch_shapes=list(scratch_shapes) + c_sems,
        input_output_aliases=aliases or {}, compiler_params=_cparams(len(grid)),
    )
    return lambda *args: res(*args, *arrays)


def _local_step(x, target, mod, small, sh, w1):
    w1_in, w1_out = w1[0].reshape(2, D_FF, D), w1[1].reshape(D_FF, D)
    (x1, a1, b1, f1, h1), (wm_in,) = _ffn_fwd(x, mod, 0, small["norm_ffn1"], w1_in, w1_out, 0.5, "ffn1_fwd",
                                              ([sh["mix_w_in"]], "gather"))
    (p, h2), (wh_o, wc_o, wm_o, cw) = _mixin_fwd(
        x1, mod, 3, small["norm_mix"], wm_in,
        ([sh["hgrn_w_o"], sh["conv_w_o"], sh["mix_w_out"], sh["conv_w"]], "gather"))
    wh_o, wc_o, wm_o = wh_o.reshape(D, D), wc_o.reshape(D, D), wm_o.reshape(D, D)
    cw = jnp.pad(cw.transpose(1, 0, 2).reshape(CONV_K, D), ((0, HALO - CONV_K), (0, 0)))
    (o, oa, a_all, s_all), (w2_in, w2_out) = _hgrn_fwd(p, small["hgrn_lb"], small["hgrn_g"],
                                                       ([sh["ffn2_w_in"], sh["ffn2_w_out"]], "gather"))
    w2_in, w2_out = w2_in.reshape(2, D_FF, D), w2_out.reshape(D_FF, D)
    u1, u2 = _conv_fwd(p, cw, small["conv_b"], small["conv_ln_g"], small["conv_ln_b"])
    x2, ya, yb, mout = _mixout_fwd(x1, oa, u2, p, mod, 3, wh_o, wc_o, wm_o)
    (x3, a3, b3, f3, h3), _ = _ffn_fwd(x2, mod, 6, small["norm_ffn2"], w2_in, w2_out, 0.5, "ffn2_fwd", None)
    dx3, sm_head = _head(x3, target, small["norm_final"])

    (da3, db3, dw2_in, dw2_out), _ = _ffn_bwd_w(h3, dx3, a3, b3, mod, 6, w2_out, 0.5, "ffn2_bwd_w", None)
    (dx2, sm3), _ = _ffn_bwd_x(x2, dx3, f3, da3, db3, mod, 6, small["norm_ffn2"], w2_in, 0.5, "ffn2_bwd_x", None)
    dp, doa, du2, dwh_o, dwc_o, dwm_o, sm_mo = _mixout_bwd(dx2, oa, u2, ya, yb, mout, p, mod, 3, wh_o, wc_o, wm_o)
    dp, dcw, sm_cv = _conv_bwd(p, u1, du2, cw, small["conv_ln_g"], small["conv_ln_b"], dp)
    rows = lambda t: t.reshape(N_DEV, -1, D).astype(MM)
    (dp, sm_hg), (r2_in, r2_out) = _hgrn_bwd(p, o, a_all, s_all, doa, small["hgrn_lb"], small["hgrn_g"], dp,
                                             ([rows(dw2_in), rows(dw2_out)], "scatter"))
    (dx1, dwm_in, sm2), (rh_o, rc_o, rm_o, rcw) = _mixin_bwd(
        x1, h2, dx2, dp, mod, 3, small["norm_mix"], wm_in,
        ([rows(dwh_o), rows(dwc_o), rows(dwm_o), dcw[:CONV_K].reshape(CONV_K, N_DEV, -1).transpose(1, 0, 2)],
         "scatter"))
    (da1, db1, dw1_in, dw1_out), (rm_in,) = _ffn_bwd_w(h1, dx1, a1, b1, mod, 0, w1_out, 0.5, "ffn1_bwd_w",
                                                      (_pair_reduce([dwm_in], "pair_mix"), "quad"))
    (dx0, sm1), (r1_in, r1_out) = _ffn_bwd_x(
        x, dx1, f1, da1, db1, mod, 0, small["norm_ffn1"], w1_in, 0.5, "ffn1_bwd_x",
        (_pair_reduce([rows(dw1_in), rows(dw1_out)], "pair_ffn1"), "quad"))

    dmod = jnp.concatenate([sm1[0:3], sm2[0:2], sm_mo[2:3], sm3[0:3]], axis=0)
    gsmall = dict(norm_ffn1=sm1[3:4], norm_mix=sm2[3:4], lb0=sm_hg[0:1], hgrn_g=sm_hg[1:2], conv_b=sm_cv[0:1],
                  conv_ln_g=sm_cv[1:2], conv_ln_b=sm_cv[2:3], norm_ffn2=sm3[3:4], norm_final=sm_head[0:1])
    recv = dict(ffn1_w_in=r1_in, ffn1_w_out=r1_out, mix_w_in=rm_in, hgrn_w_o=rh_o, conv_w=rcw, conv_w_o=rc_o,
                mix_w_out=rm_o, ffn2_w_in=r2_in, ffn2_w_out=r2_out)
    return sm_head[1, 0], dx0, dmod, gsmall, recv


def _pair_add(mine, theirs, core, name):
    _, R, C = theirs.shape

    def body(core_ref, a_ref, b_ref, out_ref):
        del core_ref
        out_ref[0] = (a_ref[0, 0].astype(F32) + b_ref[0].astype(F32)).astype(out_ref.dtype)

    blk = pl.BlockSpec((1, R, C), lambda s, core_ref: (s, 0, 0))
    grid_spec = pltpu.PrefetchScalarGridSpec(
        num_scalar_prefetch=1, grid=(N_CHIP,),
        in_specs=[pl.BlockSpec((1, 1, R, C), lambda s, core_ref: (s, core_ref[0], 0, 0)), blk], out_specs=blk)
    return pl.pallas_call(body, name=name, grid_spec=grid_spec,
                          out_shape=jax.ShapeDtypeStruct(theirs.shape, mine.dtype), compiler_params=_cparams(1),
                          )(core, mine.reshape(N_CHIP, 2, R, C), theirs)


def _pair_reduce(arrays, name):
    theirs = _exchange(arrays, "pair", name)
    core = lax.axis_index("c").astype(jnp.int32).reshape(1)
    return [_pair_add(a, t, core, "%s_add%d" % (name, i)) for i, (a, t) in enumerate(zip(arrays, theirs))]


SMALL_ORDER = ("norm_ffn1", "norm_mix", "lb0", "hgrn_g", "conv_b", "conv_ln_g", "conv_ln_b", "norm_ffn2",
               "norm_final")
PACK_ROWS = 24


def kernel(x, c, ada_w, ada_b, norm_ffn1, ffn1_w_in, ffn1_w_out, norm_mix, mix_w_in, hgrn_lb, hgrn_g, hgrn_w_o, conv_w, conv_b, conv_ln_g, conv_ln_b, conv_w_o, mix_w_out, norm_ffn2, ffn2_w_in, ffn2_w_out, norm_final, loss_target, m_ada_w, m_ada_b, m_norm_ffn1, m_ffn1_w_in, m_ffn1_w_out, m_norm_mix, m_mix_w_in, m_hgrn_lb, m_hgrn_g, m_hgrn_w_o, m_conv_w, m_conv_b, m_conv_ln_g, m_conv_ln_b, m_conv_w_o, m_mix_w_out, m_norm_ffn2, m_ffn2_w_in, m_ffn2_w_out, m_norm_final, v_ada_w, v_ada_b, v_norm_ffn1, v_ffn1_w_in, v_ffn1_w_out, v_norm_mix, v_mix_w_in, v_hgrn_lb, v_hgrn_g, v_hgrn_w_o, v_conv_w, v_conv_b, v_conv_ln_g, v_conv_ln_b, v_conv_w_o, v_mix_w_out, v_norm_ffn2, v_ffn2_w_in, v_ffn2_w_out, v_norm_final):
    mx, my, mc = _me()
    me = 4 * mx + 2 * my + mc
    ncol = ada_w.shape[2]

    sh = dict(ffn1_w_out=ffn1_w_out, mix_w_in=mix_w_in, hgrn_w_o=hgrn_w_o, conv_w_o=conv_w_o,
              mix_w_out=mix_w_out, ffn2_w_out=ffn2_w_out)
    sh = {n: w[0].astype(MM) for n, w in sh.items()}
    sh["ffn1_w_in"] = ffn1_w_in[0].T.astype(MM)
    sh["ffn2_w_in"] = ffn2_w_in[0].T.astype(MM)
    sh["conv_w"] = conv_w[0]
    small = dict(norm_ffn1=norm_ffn1, norm_mix=norm_mix, hgrn_lb=hgrn_lb, hgrn_g=hgrn_g, conv_b=conv_b,
                 conv_ln_g=conv_ln_g, conv_ln_b=conv_ln_b, norm_ffn2=norm_ffn2, norm_final=norm_final.reshape(1, D))

    cs = jnp.broadcast_to(c * jax.nn.sigmoid(c), (8, D))
    ada_b_cols = lax.dynamic_slice(ada_b, (0, me * ncol), (1, ncol))
    cs_all, mod_all, w1 = _prologue(cs, ada_w[0], ada_b_cols, [sh["ffn1_w_in"], sh["ffn1_w_out"]])
    cs_all = cs_all.reshape(N_DEV, 8, D)[:, 0, :]
    mod = lax.dynamic_index_in_dim(mod_all.reshape(N_DEV, N_DEV, ncol), me, axis=1, keepdims=False).reshape(9, D)

    loss_local, dx, dmod, gsmall, recv = _local_step(x[0], loss_target[0], mod, small, sh, w1)
    loss = lax.psum(loss_local, ("x", "y", "c"))

    pack = jnp.concatenate([dmod] + [gsmall[n] for n in SMALL_ORDER]
                           + [jnp.zeros((PACK_ROWS - 9 - len(SMALL_ORDER), D), F32)], axis=0)
    pack_all = _allgather_small(pack).reshape(N_DEV, PACK_ROWS, D)
    tot = _sum_slots(pack_all, "sum_small", PACK_ROWS)
    gs = {n: tot[9 + i:10 + i] for i, n in enumerate(SMALL_ORDER)}
    dmod_all = pack_all[:, 0:9, :].reshape(N_DEV, 9 * D)
    g_ada_b = tot[0:9].reshape(1, 9 * D)
    g_ada_w = _ada_wgrad(cs_all, lax.dynamic_slice(dmod_all, (0, me * ncol), (N_DEV, ncol)))
    z = hgrn_lb.astype(F32)
    p0 = jax.nn.sigmoid(z[0:1] - z[1:2])
    dz0 = p0 * (1.0 - p0) * gs["lb0"]
    g_hgrn_lb = jnp.concatenate([dz0, -dz0], axis=0)

    res = {}
    res["ada_w"] = _adamw(ada_w[0], m_ada_w[0], v_ada_w[0], g_ada_w, "adamw_ada_w")
    big = dict(ffn1_w_in=(ffn1_w_in, m_ffn1_w_in, v_ffn1_w_in), ffn1_w_out=(ffn1_w_out, m_ffn1_w_out, v_ffn1_w_out),
               mix_w_in=(mix_w_in, m_mix_w_in, v_mix_w_in), hgrn_w_o=(hgrn_w_o, m_hgrn_w_o, v_hgrn_w_o),
               conv_w=(conv_w, m_conv_w, v_conv_w), conv_w_o=(conv_w_o, m_conv_w_o, v_conv_w_o),
               mix_w_out=(mix_w_out, m_mix_w_out, v_mix_w_out), ffn2_w_in=(ffn2_w_in, m_ffn2_w_in, v_ffn2_w_in),
               ffn2_w_out=(ffn2_w_out, m_ffn2_w_out, v_ffn2_w_out))
    for n, (w, m, v) in big.items():
        g = recv[n]
        if n in ("ffn1_w_in", "ffn2_w_in"):
            g = _sum_slots(g, "sum_" + n, g.shape[1] // 4).T
        res[n] = _adamw(w[0], m[0], v[0], g, "adamw_" + n)
    sm_names = ("ada_b", "norm_ffn1", "norm_mix", "hgrn_lb", "hgrn_g", "conv_b", "conv_ln_g", "conv_ln_b",
                "norm_ffn2", "norm_final")
    sm_w = dict(ada_b=(ada_b, m_ada_b, v_ada_b), norm_ffn1=(norm_ffn1, m_norm_ffn1, v_norm_ffn1),
                norm_mix=(norm_mix, m_norm_mix, v_norm_mix), hgrn_lb=(hgrn_lb, m_hgrn_lb, v_hgrn_lb),
                hgrn_g=(hgrn_g, m_hgrn_g, v_hgrn_g), conv_b=(conv_b, m_conv_b, v_conv_b),
                conv_ln_g=(conv_ln_g, m_conv_ln_g, v_conv_ln_g), conv_ln_b=(conv_ln_b, m_conv_ln_b, v_conv_ln_b),
                norm_ffn2=(norm_ffn2, m_norm_ffn2, v_norm_ffn2), norm_final=(norm_final, m_norm_final, v_norm_final))
    sm_g = dict(gs, ada_b=g_ada_b, hgrn_lb=g_hgrn_lb)
    rows = {n: sm_w[n][0].size // D for n in sm_names}
    n_rows = sum(rows.values())
    pad = (-n_rows) % 8
    stack = lambda parts: jnp.concatenate([q.reshape(-1, D) for q in parts] + [jnp.ones((pad, D), F32)], axis=0)
    st = _adamw(stack([sm_w[n][0] for n in sm_names]), stack([sm_w[n][1] for n in sm_names]),
                stack([sm_w[n][2] for n in sm_names]), stack([sm_g[n] for n in sm_names]), "adamw_small")
    off = 0
    for n in sm_names:
        res[n] = tuple(t[off:off + rows[n]].reshape(sm_w[n][0].shape) for t in st)
        off += rows[n]

    order = ("ada_w", "ada_b", "norm_ffn1", "ffn1_w_in", "ffn1_w_out", "norm_mix", "mix_w_in", "hgrn_lb", "hgrn_g",
             "hgrn_w_o", "conv_w", "conv_b", "conv_ln_g", "conv_ln_b", "conv_w_o", "mix_w_out", "norm_ffn2",
             "ffn2_w_in", "ffn2_w_out", "norm_final")
    lead = lambda n, t: t[None] if n in big or n == "ada_w" else t
    outs = [loss, dx[None]]
    for j in range(4):
        outs += [lead(n, res[n][j]) for n in order]
    return tuple(outs)
```

```python
import functools

import jax
import jax.numpy as jnp
from jax import lax
from jax.experimental import pallas as pl
from jax.experimental.pallas import tpu as pltpu

F32 = jnp.float32
MM = jnp.bfloat16
ACT = jnp.bfloat16

D = 1024
D_FF = 2816
HEADS = 8
HD = 128
CHUNK = 64
SUB = 16
NSUB = CHUNK // SUB
CONV_K = 31
HALO = 32
EPS = 1e-6
N_DEV = 8
NEG = -1e30
Q_SCALE = HD ** -0.5

ADAM_LR = 0.001
ADAM_B1 = 0.9
ADAM_B2 = 0.999
ADAM_EPS = 1e-08
ADAM_WD = 0.01
ADAM_STEP = 10

VMEM_LIMIT = 60 * 1024 * 1024
MESH = pl.DeviceIdType.MESH


def _cparams(n_axes):
    return pltpu.CompilerParams(dimension_semantics=("arbitrary",) * n_axes, vmem_limit_bytes=VMEM_LIMIT)


def _mm(a, b):
    return lax.dot_general(a.astype(MM), b.astype(MM), (((1,), (0,)), ((), ())), preferred_element_type=F32)


def _mm_nt(a, b):
    return lax.dot_general(a.astype(MM), b.astype(MM), (((1,), (1,)), ((), ())), preferred_element_type=F32)


def _mm_tn(a, b):
    return lax.dot_general(a.astype(MM), b.astype(MM), (((0,), (0,)), ((), ())), preferred_element_type=F32)


def _sig(x):
    return 1.0 / (1.0 + jnp.exp(-x))


def _colsum(x):
    return jnp.sum(x, axis=0, keepdims=True)


def _rowmean(x):
    return jnp.mean(x, axis=-1, keepdims=True)


def _modnorm_fwd(xv, g, sh, sc):
    r = lax.rsqrt(_rowmean(xv * xv) + EPS)
    xh = xv * r
    n = xh * g
    return n * (1.0 + sc) + sh, xh, n, r


def _modnorm_bwd(dh, xh, n, r, g, sc):
    dsc = _colsum(dh * n)
    dsh = _colsum(dh)
    dn = dh * (1.0 + sc)
    dg = _colsum(dn * xh)
    dxh = dn * g
    dx = r * (dxh - xh * _rowmean(dxh * xh))
    return dx, dsh, dsc, dg


def _ffn_fwd(x, mod, mo, gnorm, w_in_t, w_out, res, name, carry):
    T = x.shape[0]
    tm = min(512, T)
    tn = D_FF // 2

    def body(x_ref, mod_ref, g_ref, wi_ref, wo_ref, xo_ref, a_ref, b_ref, f_ref, h_ref):
        xv = x_ref[...]
        h, _, _, _ = _modnorm_fwd(xv, g_ref[...], mod_ref[mo:mo + 1, :], mod_ref[mo + 1:mo + 2, :])
        h = h.astype(ACT)
        h_ref[...] = h
        f = None
        for c0 in range(0, D_FF, tn):
            a = _mm_nt(h, wi_ref[0, c0:c0 + tn, :])
            b = _mm_nt(h, wi_ref[1, c0:c0 + tn, :])
            a_ref[:, c0:c0 + tn] = a.astype(ACT)
            b_ref[:, c0:c0 + tn] = b.astype(ACT)
            part = _mm(a * _sig(a) * b, wo_ref[c0:c0 + tn, :])
            f = part if f is None else f + part
        f_ref[...] = f
        xo_ref[...] = xv + res * mod_ref[mo + 2:mo + 3, :] * f

    tile = pl.BlockSpec((tm, D), lambda i: (i, 0))
    wide = pl.BlockSpec((tm, D_FF), lambda i: (i, 0))
    out = _gridded(
        body, carry, name=name, grid=(T // tm,),
        in_specs=[
            tile,
            pl.BlockSpec((9, D), lambda i: (0, 0)),
            pl.BlockSpec((1, D), lambda i: (0, 0)),
            pl.BlockSpec((2, D_FF, D), lambda i: (0, 0, 0), pipeline_mode=pl.Buffered(1)),
            pl.BlockSpec((D_FF, D), lambda i: (0, 0), pipeline_mode=pl.Buffered(1)),
        ],
        out_specs=[tile, wide, wide, tile, tile],
        out_shape=[
            jax.ShapeDtypeStruct((T, D), F32),
            jax.ShapeDtypeStruct((T, D_FF), ACT),
            jax.ShapeDtypeStruct((T, D_FF), ACT),
            jax.ShapeDtypeStruct((T, D), F32),
            jax.ShapeDtypeStruct((T, D), ACT),
        ],
    )(x, mod, gnorm, w_in_t, w_out)
    return out[:5], out[5:]


def _ffn_bwd_w(h, dxo, a, b, mod, mo, w_out, res, name, carry):
    T = h.shape[0]
    tm = min(1024, T)
    ni = T // tm
    tn = 256
    nj = D_FF // tn

    def body(h_ref, dxo_ref, a_ref, b_ref, mod_ref, wo_ref, da_ref, db_ref, dwi_ref, dwo_ref,
             acc_i, acc_o, df_all, h_all):
        j = pl.program_id(0)
        i = pl.program_id(1)

        @pl.when(i == 0)
        def _():
            acc_i[...] = jnp.zeros_like(acc_i)
            acc_o[...] = jnp.zeros_like(acc_o)

        @pl.when(j == 0)
        def _():
            df_all[i] = (res * mod_ref[mo + 2:mo + 3, :] * dxo_ref[...]).astype(MM)
            h_all[i] = h_ref[...]

        hb = h_all[i]
        df = df_all[i]
        av = a_ref[...].astype(F32)
        bv = b_ref[...].astype(F32)
        sg = _sig(av)
        sa = av * sg
        s = (sa * bv).astype(MM)
        ds = _mm_nt(df, wo_ref[...])
        da = (ds * bv * sg * (1.0 + av * (1.0 - sg))).astype(MM)
        db = (ds * sa).astype(MM)
        da_ref[...] = da
        db_ref[...] = db
        acc_o[...] += _mm_tn(s, df)
        acc_i[0] += _mm_tn(da, hb)
        acc_i[1] += _mm_tn(db, hb)

        @pl.when(i == ni - 1)
        def _():
            dwi_ref[...] = acc_i[...].astype(MM)
            dwo_ref[...] = acc_o[...].astype(MM)

    first = lambda j, i: (jnp.where(j == 0, i, ni - 1), 0)
    out = _gridded(
        body, carry, name=name, grid=(nj, ni),
        in_specs=[
            pl.BlockSpec((tm, D), first),
            pl.BlockSpec((tm, D), first),
            pl.BlockSpec((tm, tn), lambda j, i: (i, j)),
            pl.BlockSpec((tm, tn), lambda j, i: (i, j)),
            pl.BlockSpec((9, D), lambda j, i: (0, 0)),
            pl.BlockSpec((tn, D), lambda j, i: (j, 0)),
        ],
        out_specs=[
            pl.BlockSpec((tm, tn), lambda j, i: (i, j)),
            pl.BlockSpec((tm, tn), lambda j, i: (i, j)),
            pl.BlockSpec((2, tn, D), lambda j, i: (0, j, 0)),
            pl.BlockSpec((tn, D), lambda j, i: (j, 0)),
        ],
        out_shape=[
            jax.ShapeDtypeStruct((T, D_FF), MM),
            jax.ShapeDtypeStruct((T, D_FF), MM),
            jax.ShapeDtypeStruct((2, D_FF, D), MM),
            jax.ShapeDtypeStruct((D_FF, D), MM),
        ],
        scratch_shapes=[pltpu.VMEM((2, tn, D), F32), pltpu.VMEM((tn, D), F32),
                        pltpu.VMEM((ni, tm, D), MM), pltpu.VMEM((ni, tm, D), MM)],
    )(h, dxo, a, b, mod, w_out)
    return out[:4], out[4:]


def _ffn_bwd_x(x, dxo, f, da, db, mod, mo, gnorm, w_in_t, res, name, carry):
    T = x.shape[0]
    tm = min(512, T)
    ni = T // tm
    tn = D_FF // 2
    nj = D_FF // tn

    def body(x_ref, dxo_ref, f_ref, da_ref, db_ref, mod_ref, g_ref, wi_ref, dx_ref, sm_ref, dh_scr):
        j = pl.program_id(0)
        i = pl.program_id(1)

        @pl.when((j == 0) & (i == 0))
        def _():
            sm_ref[...] = jnp.zeros_like(sm_ref)

        @pl.when(j == 0)
        def _():
            dh_scr[i] = jnp.zeros((tm, D), F32)

        dh_scr[i] += _mm(da_ref[...], wi_ref[0]) + _mm(db_ref[...], wi_ref[1])

        @pl.when(j == nj - 1)
        def _():
            sc = mod_ref[mo + 1:mo + 2, :]
            _, xh, n, r = _modnorm_fwd(x_ref[...], g_ref[...], mod_ref[mo:mo + 1, :], sc)
            dxn, dsh, dsc, dg = _modnorm_bwd(dh_scr[i], xh, n, r, g_ref[...], sc)
            dxo_v = dxo_ref[...]
            dx_ref[...] = dxo_v + dxn
            sm_ref[0:1, :] += dsh
            sm_ref[1:2, :] += dsc
            sm_ref[2:3, :] += _colsum(dxo_v * f_ref[...]) * res
            sm_ref[3:4, :] += dg

    last = pl.BlockSpec((tm, D), lambda j, i: (jnp.where(j == nj - 1, i, 0), 0))
    out = _gridded(
        body, carry, name=name, grid=(nj, ni),
        in_specs=[last, last, last,
                  pl.BlockSpec((tm, tn), lambda j, i: (i, j)), pl.BlockSpec((tm, tn), lambda j, i: (i, j)),
                  pl.BlockSpec((9, D), lambda j, i: (0, 0)), pl.BlockSpec((1, D), lambda j, i: (0, 0)),
                  pl.BlockSpec((2, tn, D), lambda j, i: (0, j, 0))],
        out_specs=[last, pl.BlockSpec((8, D), lambda j, i: (0, 0))],
        out_shape=[jax.ShapeDtypeStruct((T, D), F32), jax.ShapeDtypeStruct((8, D), F32)],
        scratch_shapes=[pltpu.VMEM((ni, tm, D), F32)],
    )(x, dxo, f, da, db, mod, gnorm, w_in_t)
    return out[:2], out[2:]


def _head(x, target, gfin):
    T = x.shape[0]
    tm = min(512, T)
    ni = T // tm

    def body(x_ref, t_ref, g_ref, dx_ref, sm_ref):
        i = pl.program_id(0)

        @pl.when(i == 0)
        def _():
            sm_ref[...] = jnp.zeros_like(sm_ref)

        xv = x_ref[...]
        g = g_ref[...]
        r = lax.rsqrt(_rowmean(xv * xv) + EPS)
        xh = xv * r
        e = xh * g - t_ref[...]
        sm_ref[1:2, :] += _colsum(e * e) * (0.5 / D)
        dy = e * (1.0 / D)
        sm_ref[0:1, :] += _colsum(dy * xh)
        dxh = dy * g
        dx_ref[...] = r * (dxh - xh * _rowmean(dxh * xh))

        @pl.when(i == ni - 1)
        def _():
            sm_ref[1:2, :] = jnp.broadcast_to(jnp.sum(sm_ref[1:2, :], axis=-1, keepdims=True), (1, D))

    return pl.pallas_call(
        body, name="head_loss", grid=(ni,),
        in_specs=[pl.BlockSpec((tm, D), lambda i: (i, 0)), pl.BlockSpec((tm, D), lambda i: (i, 0)),
                  pl.BlockSpec((1, D), lambda i: (0, 0))],
        out_specs=[pl.BlockSpec((tm, D), lambda i: (i, 0)), pl.BlockSpec((8, D), lambda i: (0, 0))],
        out_shape=[jax.ShapeDtypeStruct((T, D), F32), jax.ShapeDtypeStruct((8, D), F32)],
        compiler_params=_cparams(1),
    )(x, target, gfin)


def _mixin_fwd(x, mod, mo, gnorm, w, carry):
    T = x.shape[0]
    tm = min(1024, T)
    ni = T // tm

    def body(x_ref, mod_ref, g_ref, w_ref, p_ref, h_ref, h_all):
        i = pl.program_id(1)

        @pl.when(pl.program_id(0) == 0)
        def _():
            h, _, _, _ = _modnorm_fwd(x_ref[...], g_ref[...], mod_ref[mo:mo + 1, :], mod_ref[mo + 1:mo + 2, :])
            h_all[i] = h.astype(ACT)
            h_ref[...] = h.astype(ACT)

        p_ref[0] = _mm(h_all[i], w_ref[0])

    first = lambda k, i: (jnp.where(k == 0, i, ni - 1), 0)
    out = _gridded(
        body, carry, name="mixin_fwd", grid=(8, ni),
        in_specs=[pl.BlockSpec((tm, D), first), pl.BlockSpec((9, D), lambda k, i: (0, 0)),
                  pl.BlockSpec((1, D), lambda k, i: (0, 0)), pl.BlockSpec((1, D, D), lambda k, i: (k, 0, 0))],
        out_specs=[pl.BlockSpec((1, tm, D), lambda k, i: (k, i, 0)), pl.BlockSpec((tm, D), first)],
        out_shape=[jax.ShapeDtypeStruct((8, T, D), F32), jax.ShapeDtypeStruct((T, D), ACT)],
        scratch_shapes=[pltpu.VMEM((ni, tm, D), ACT)],
    )(x, mod, gnorm, w)
    return out[:2], out[2:]


def _mixin_bwd(x, h, dxo, dp, mod, mo, gnorm, w, carry):
    T = x.shape[0]
    tm = min(512, T)
    ni = T // tm

    def body(x_ref, h_ref, dxo_ref, dp_ref, mod_ref, g_ref, w_ref, dx_ref, dw_ref, sm_ref, dh_scr, acc):
        k = pl.program_id(0)
        i = pl.program_id(1)

        @pl.when(i == 0)
        def _():
            acc[...] = jnp.zeros_like(acc)

        @pl.when(k == 0)
        def _():
            dh_scr[i] = jnp.zeros((tm, D), F32)

        @pl.when((k == 0) & (i == 0))
        def _():
            sm_ref[...] = jnp.zeros_like(sm_ref)

        dpk = dp_ref[0].astype(MM)
        acc[...] += _mm_tn(h_ref[...], dpk)
        dh_scr[i] += _mm_nt(dpk, w_ref[0])

        @pl.when(i == ni - 1)
        def _():
            dw_ref[0] = acc[...].astype(MM)

        @pl.when(k == 7)
        def _():
            sc = mod_ref[mo + 1:mo + 2, :]
            _, xh, n, r = _modnorm_fwd(x_ref[...], g_ref[...], mod_ref[mo:mo + 1, :], sc)
            dxn, dsh, dsc, dg = _modnorm_bwd(dh_scr[i], xh, n, r, g_ref[...], sc)
            dx_ref[...] = dxo_ref[...] + dxn
            sm_ref[0:1, :] += dsh
            sm_ref[1:2, :] += dsc
            sm_ref[3:4, :] += dg

    out = _gridded(
        body, carry, name="mixin_bwd", grid=(8, ni),
        in_specs=[pl.BlockSpec((tm, D), lambda k, i: (jnp.where(k == 7, i, 0), 0)),
                  pl.BlockSpec((tm, D), lambda k, i: (i, 0)),
                  pl.BlockSpec((tm, D), lambda k, i: (jnp.where(k == 7, i, 0), 0)),
                  pl.BlockSpec((1, tm, D), lambda k, i: (k, i, 0)), pl.BlockSpec((9, D), lambda k, i: (0, 0)),
                  pl.BlockSpec((1, D), lambda k, i: (0, 0)), pl.BlockSpec((1, D, D), lambda k, i: (k, 0, 0))],
        out_specs=[pl.BlockSpec((tm, D), lambda k, i: (jnp.where(k == 7, i, 0), 0)),
                   pl.BlockSpec((1, D, D), lambda k, i: (k, 0, 0)),
                   pl.BlockSpec((8, D), lambda k, i: (0, 0))],
        out_shape=[jax.ShapeDtypeStruct((T, D), F32), jax.ShapeDtypeStruct((8, D, D), MM),
                   jax.ShapeDtypeStruct((8, D), F32)],
        scratch_shapes=[pltpu.VMEM((ni, tm, D), F32), pltpu.VMEM((D, D), F32)],
    )(x, h, dxo, dp, mod, gnorm, w)
    return out[:3], out[3:]


def _hgrn_consts():
    rows = jnp.arange(SUB * HD) // HD
    e = (rows[:, None] == jnp.arange(HD)[None, :]).astype(MM)
    return e, e.T


def _rows_bcast(ref, cb, first, n):
    parts = [jnp.broadcast_to(ref[pl.ds(c * CHUNK + first, 1), :], (n, HD)) for c in range(cb // CHUNK)]
    return jnp.concatenate(parts, axis=0)


def _hgrn_pre(qr, fr, lb_ref, b_scr, cb):
    z = lb_ref[...]
    lb = _sig(z[0:1, :] - z[1:2, :])
    sq = _sig(qr)
    q = qr * sq * Q_SCALE
    sf = _sig(fr)
    fg = lb + (1.0 - lb) * sf
    lf = jnp.log(fg)
    k = 1.0 - fg
    tl = lax.broadcasted_iota(jnp.int32, (cb, HD), 0) % CHUNK
    bc = lf
    sh = 1
    while sh < CHUNK:
        bc = bc + jnp.where(tl >= sh, pltpu.roll(bc, sh, 0), 0.0)
        sh *= 2
    b_scr[...] = bc
    bl = _rows_bcast(b_scr, cb, CHUNK - 1, CHUNK)
    br = [None] + [_rows_bcast(b_scr, cb, SUB * i - 1, CHUNK) for i in range(1, NSUB)]
    sb = tl // SUB
    bref = jnp.where(sb == 0, bc, jnp.where(sb == 1, br[1], jnp.where(sb == 2, br[2], br[3])))
    eb = jnp.exp(bc)
    ekd = jnp.exp(bl - bc)
    eqo = jnp.exp(bc - bref)
    eko = [None] + [jnp.exp(jnp.where(tl < SUB * i, br[i] - bc, NEG)) for i in range(1, NSUB)]
    return dict(lb=lb, sq=sq, q=q, sf=sf, fg=fg, k=k, tl=tl, sb=sb, b=bc, bl=bl, eb=eb, ekd=ekd, eqo=eqo,
                eko=eko, qe=q * eb, kd=k * ekd, qo=q * eqo, ko=[None] + [k * eko[i] for i in range(1, NSUB)])


def _pad_rows(x):
    return jnp.concatenate([x, jnp.zeros_like(x)], axis=0)


def _by_subblock(sbc, parts):
    out = jnp.zeros_like(parts[1])
    for i in range(1, NSUB):
        out = jnp.where(sbc == i, parts[i], out)
    return out


def _hgrn_fwd(p, hgrn_lb, hgrn_g, carry):
    T = p.shape[1]
    cb = min(512, T)
    nch = cb // CHUNK
    ncb = T // cb
    e_mat, _ = _hgrn_consts()

    def body(p_ref, lb_ref, g_ref, e_ref, o_ref, oa_ref, a_ref, s_ref, st_scr, q_scr, k_scr, b_scr, z_scr):
        @pl.when(pl.program_id(1) == 0)
        def _():
            st_scr[...] = jnp.zeros_like(st_scr)

        v = p_ref[2]
        og = p_ref[3]
        pre = _hgrn_pre(p_ref[0], p_ref[1], lb_ref, b_scr, cb)
        q_scr[...] = pre["q"]
        k_scr[...] = pre["k"]
        ti = lax.broadcasted_iota(jnp.int32, (SUB, HD), 0)

        def zbody(c, carry):
            for i in range(NSUB):
                r0 = pl.multiple_of(c * CHUNK + SUB * i, SUB)
                qi = q_scr[pl.ds(r0, SUB), :]
                bi = b_scr[pl.ds(r0, SUB), :]
                for s in range(SUB):
                    krow = k_scr[pl.ds(r0 + s, 1), :]
                    brow = b_scr[pl.ds(r0 + s, 1), :]
                    if s < 8:
                        zz = qi * krow * jnp.exp(jnp.where(ti >= s, bi - brow, NEG))
                    else:
                        lo = qi[8:] * krow * jnp.exp(jnp.where(ti[8:] >= s, bi[8:] - brow, NEG))
                        zz = jnp.concatenate([jnp.zeros((8, HD), F32), lo], axis=0)
                    z_scr[i, pl.ds(pl.multiple_of(c * SUB, SUB), SUB), s * HD:(s + 1) * HD] = zz.astype(MM)
            return carry

        lax.fori_loop(0, nch, zbody, 0)
        adiag = [_mm(z_scr[i], e_ref[...]) for i in range(NSUB)]
        sbc = lax.broadcasted_iota(jnp.int32, (CHUNK, HD), 0) // SUB
        chunks = [slice(c * CHUNK, (c + 1) * CHUNK) for c in range(nch)]
        offs = [[_mm_nt(pre["qo"][rs], _pad_rows(pre["ko"][i][rs])) for i in range(1, NSUB)] for rs in chunks]
        kv = [_mm_tn(v[rs], pre["kd"][rs]) for rs in chunks]
        a_parts = []
        for c in range(nch):
            dparts = []
            for i in range(NSUB):
                blk = adiag[i][c * SUB:(c + 1) * SUB]
                dparts.append(blk if i == 0 else pltpu.roll(blk, SUB * i, 1))
            a_parts.append(_by_subblock(sbc, [None] + offs[c]) + jnp.concatenate(dparts, axis=0))
        a_ref[0] = jnp.concatenate(a_parts, axis=0)
        o_intra = [_mm(a_parts[c], _pad_rows(v[rs])) for c, rs in enumerate(chunks)]
        states = []
        st = st_scr[...]
        for c in range(nch):
            states.append(st)
            st = st * jnp.exp(b_scr[pl.ds(c * CHUNK + CHUNK - 1, 1), :]) + kv[c]
        st_scr[...] = st
        for c in range(nch):
            s_ref[0, c] = states[c]
        o = jnp.concatenate([o_intra[c] + _mm_nt(pre["qe"][rs], states[c]) for c, rs in enumerate(chunks)], axis=0)
        o_ref[...] = o
        on = o * lax.rsqrt(_rowmean(o * o) + EPS) * g_ref[...]
        oa_ref[...] = (on * og * _sig(og)).astype(ACT)

    out = _gridded(
        body, carry, name="hgrn_fwd", grid=(HEADS, ncb),
        in_specs=[pl.BlockSpec((4, cb, HD), lambda h, c: (0, c, h)),
                  pl.BlockSpec((2, HD), lambda h, c: (0, h)),
                  pl.BlockSpec((1, HD), lambda h, c: (0, h)),
                  pl.BlockSpec((SUB * HD, HD), lambda h, c: (0, 0))],
        out_specs=[pl.BlockSpec((cb, HD), lambda h, c: (c, h)),
                   pl.BlockSpec((cb, HD), lambda h, c: (c, h)),
                   pl.BlockSpec((1, cb, HD), lambda h, c: (h, c, 0)),
                   pl.BlockSpec((1, nch, HD, HD), lambda h, c: (h, c, 0, 0))],
        out_shape=[jax.ShapeDtypeStruct((T, D), F32), jax.ShapeDtypeStruct((T, D), ACT),
                   jax.ShapeDtypeStruct((HEADS, T, HD), F32),
                   jax.ShapeDtypeStruct((HEADS, T // CHUNK, HD, HD), F32)],
        scratch_shapes=[pltpu.VMEM((HD, HD), F32), pltpu.VMEM((cb, HD), F32), pltpu.VMEM((cb, HD), F32),
                        pltpu.VMEM((cb, HD), F32), pltpu.VMEM((NSUB, nch * SUB, SUB * HD), MM)],
    )(p, hgrn_lb, hgrn_g, e_mat)
    return out[:4], out[4:]


def _hgrn_bwd(p, o, a_all, s_all, doa, hgrn_lb, hgrn_g, dp, carry):
    T = p.shape[1]
    cb = min(512, T)
    nch = cb // CHUNK
    ncb = T // cb
    _, et_mat = _hgrn_consts()

    def body(p_ref, o_ref, a_ref, s_ref, doa_ref, lb_ref, g_ref, et_ref, dp_in, dp_ref, sm_ref,
             dst_scr, q_scr, k_scr, b_scr, x_scr, dqd_scr, dkd_scr):
        del dp_in

        @pl.when(pl.program_id(1) == 0)
        def _():
            dst_scr[...] = jnp.zeros_like(dst_scr)
            sm_ref[...] = jnp.zeros_like(sm_ref)

        qr = p_ref[0]
        v = p_ref[2]
        og = p_ref[3]
        pre = _hgrn_pre(qr, p_ref[1], lb_ref, b_scr, cb)
        q, k = pre["q"], pre["k"]
        q_scr[...] = q
        k_scr[...] = k
        g = g_ref[...]
        ov = o_ref[...]
        r = lax.rsqrt(_rowmean(ov * ov) + EPS)
        oh = ov * r
        sgo = _sig(og)
        doa_v = doa_ref[...]
        don = doa_v * og * sgo
        dog = doa_v * oh * g * sgo * (1.0 + og * (1.0 - sgo))
        sm_ref[1:2, :] += _colsum(don * oh)
        doh = don * g
        do = r * (doh - oh * _rowmean(doh * oh))

        sbc = lax.broadcasted_iota(jnp.int32, (CHUNK, HD), 0) // SUB
        row_i = lax.broadcasted_iota(jnp.int32, (CHUNK, HD), 0)
        lane_i = lax.broadcasted_iota(jnp.int32, (CHUNK, HD), 1)
        causal = lane_i <= row_i
        chunks = [slice(c * CHUNK, (c + 1) * CHUNK) for c in range(nch)]
        da_parts = [jnp.where(causal, _mm_nt(do[rs], _pad_rows(v[rs])), 0.0) for rs in chunks]
        dv_parts = [_mm_tn(a_ref[0, rs, :], do[rs])[:CHUNK] for rs in chunks]
        dqoff_mm = [[_mm(da_parts[c], _pad_rows(pre["ko"][i][rs])) for i in range(1, NSUB)]
                    for c, rs in enumerate(chunks)]
        dkoff_mm = [[_mm_tn(jnp.where(sbc == i, da_parts[c], 0.0), pre["qo"][rs])[:CHUNK] for i in range(1, NSUB)]
                    for c, rs in enumerate(chunks)]
        dqoff_parts = [_by_subblock(sbc, [None] + dqoff_mm[c]) for c in range(nch)]
        dkoff_parts = []
        for c, rs in enumerate(chunks):
            dko = pre["eko"][1][rs] * dkoff_mm[c][0]
            for i in range(2, NSUB):
                dko = dko + pre["eko"][i][rs] * dkoff_mm[c][i - 1]
            dkoff_parts.append(dko)
        for i in range(NSUB):
            rows = []
            for c in range(nch):
                blk = da_parts[c][SUB * i:SUB * (i + 1)]
                rows.append(blk if i == 0 else pltpu.roll(blk, HD - SUB * i, 1))
            x_scr[i] = _mm(jnp.concatenate(rows, axis=0), et_ref[...])
        ti = lax.broadcasted_iota(jnp.int32, (SUB, HD), 0)

        def dbody(c, carry):
            for i in range(NSUB):
                r0 = pl.multiple_of(c * CHUNK + SUB * i, SUB)
                qi = q_scr[pl.ds(r0, SUB), :]
                bi = b_scr[pl.ds(r0, SUB), :]
                dq_hi = jnp.zeros((8, HD), F32)
                dq_lo = jnp.zeros((8, HD), F32)
                dk_hi = jnp.zeros((8, HD), F32)
                dk_lo = jnp.zeros((8, HD), F32)
                c0 = pl.multiple_of(c * SUB, SUB)
                t8 = ti[:8]
                for s in range(SUB):
                    krow = k_scr[pl.ds(r0 + s, 1), :]
                    brow = b_scr[pl.ds(r0 + s, 1), :]
                    w_lo = (x_scr[i, pl.ds(c0 + 8, 8), s * HD:(s + 1) * HD]
                            * jnp.exp(jnp.where(t8 + 8 >= s, bi[8:] - brow, NEG)))
                    dq_lo = dq_lo + w_lo * krow
                    col = _colsum(w_lo * qi[8:])
                    if s < 8:
                        w_hi = (x_scr[i, pl.ds(c0, 8), s * HD:(s + 1) * HD]
                                * jnp.exp(jnp.where(t8 >= s, bi[:8] - brow, NEG)))
                        dq_hi = dq_hi + w_hi * krow
                        dk_hi = jnp.where(t8 == s, col + _colsum(w_hi * qi[:8]), dk_hi)
                    else:
                        dk_lo = jnp.where(t8 + 8 == s, col, dk_lo)
                dqd_scr[pl.ds(r0, SUB), :] = jnp.concatenate([dq_hi, dq_lo], axis=0)
                dkd_scr[pl.ds(r0, SUB), :] = jnp.concatenate([dk_hi, dk_lo], axis=0)
            return carry

        lax.fori_loop(0, nch, dbody, 0)
        qdo = [_mm_tn(do[rs], pre["qe"][rs]) for rs in chunks]
        dsts = [None] * nch
        dst = dst_scr[...]
        for c in reversed(range(nch)):
            dsts[c] = dst
            dst = dst * jnp.exp(b_scr[pl.ds(c * CHUNK + CHUNK - 1, 1), :]) + qdo[c]
        dst_scr[...] = dst
        sts = [s_ref[0, c] for c in range(nch)]
        dqe_parts = [_mm(do[rs], sts[c]) for c, rs in enumerate(chunks)]
        dkdec_parts = [_mm(v[rs], dsts[c]) for c, rs in enumerate(chunks)]
        dvi_parts = [_mm_nt(pre["kd"][rs], dsts[c]) for c, rs in enumerate(chunks)]
        debl_parts = [_colsum(dsts[c] * sts[c]) for c in range(nch)]
        dqe = jnp.concatenate(dqe_parts, axis=0)
        dkdec = jnp.concatenate(dkdec_parts, axis=0)
        dq_tot = jnp.concatenate(dqoff_parts, axis=0) * pre["eqo"] + dqd_scr[...] + dqe * pre["eb"]
        dk_inter = dkdec * pre["ekd"]
        dk_tot = jnp.concatenate(dkoff_parts, axis=0) + dkd_scr[...] + dk_inter
        db = q * dq_tot - k * dk_tot
        kdk = k * dk_inter
        dbl = jnp.concatenate(
            [jnp.broadcast_to(jnp.exp(b_scr[pl.ds(c * CHUNK + CHUNK - 1, 1), :]) * debl_parts[c]
                              + _colsum(kdk[c * CHUNK:(c + 1) * CHUNK]), (CHUNK, HD)) for c in range(nch)], axis=0)
        tl = pre["tl"]
        rc = db
        sh = 1
        while sh < CHUNK:
            rc = rc + jnp.where(tl + sh < CHUNK, pltpu.roll(rc, cb - sh, 0), 0.0)
            sh *= 2
        dlf = rc + dbl
        dfg = dlf / pre["fg"] - dk_tot
        sf = pre["sf"]
        lb = pre["lb"]
        sm_ref[0:1, :] += _colsum(dfg * (1.0 - sf))
        sq = pre["sq"]
        dp_ref[0] = (dq_tot * Q_SCALE * sq * (1.0 + qr * (1.0 - sq))).astype(ACT)
        dp_ref[1] = (dfg * (1.0 - lb) * sf * (1.0 - sf)).astype(ACT)
        dp_ref[2] = (jnp.concatenate(dv_parts, axis=0) + jnp.concatenate(dvi_parts, axis=0)).astype(ACT)
        dp_ref[3] = dog.astype(ACT)

    rev = lambda c: ncb - 1 - c
    out = _gridded(
        body, carry, name="hgrn_bwd", grid=(HEADS, ncb),
        in_specs=[pl.BlockSpec((4, cb, HD), lambda h, c: (0, rev(c), h)),
                  pl.BlockSpec((cb, HD), lambda h, c: (rev(c), h)),
                  pl.BlockSpec((1, cb, HD), lambda h, c: (h, rev(c), 0)),
                  pl.BlockSpec((1, nch, HD, HD), lambda h, c: (h, rev(c), 0, 0)),
                  pl.BlockSpec((cb, HD), lambda h, c: (rev(c), h)),
                  pl.BlockSpec((2, HD), lambda h, c: (0, h)),
                  pl.BlockSpec((1, HD), lambda h, c: (0, h)),
                  pl.BlockSpec((HD, SUB * HD), lambda h, c: (0, 0)),
                  pl.BlockSpec(memory_space=pl.ANY)],
        out_specs=[pl.BlockSpec((4, cb, HD), lambda h, c: (0, rev(c), h)),
                   pl.BlockSpec((8, HD), lambda h, c: (0, h))],
        out_shape=[jax.ShapeDtypeStruct(dp.shape, dp.dtype), jax.ShapeDtypeStruct((8, D), F32)],
        aliases={8: 0},
        scratch_shapes=[pltpu.VMEM((HD, HD), F32), pltpu.VMEM((cb, HD), F32), pltpu.VMEM((cb, HD), F32),
                        pltpu.VMEM((cb, HD), F32), pltpu.VMEM((NSUB, nch * SUB, SUB * HD), F32),
                        pltpu.VMEM((cb, HD), F32), pltpu.VMEM((cb, HD), F32)],
    )(p, o, a_all, s_all, doa, hgrn_lb, hgrn_g, et_mat, dp)
    return out[:2], out[2:]


def _ln_fwd(u1, g, b):
    mu = _rowmean(u1)
    xc = u1 - mu
    rs = lax.rsqrt(_rowmean(xc * xc) + EPS)
    xh = xc * rs
    return xh * g + b, xh, rs


CONV_RB = 64
LANES = 128


def _shift_rows(src, sh, ls, n):
    for r in range(1, 8):
        sh[r - 1, 0:n, :] = src[pl.ds(r, n), ls]


def _tap(src, sh, ls, off, r0, rows):
    r = off % 8
    if r == 0:
        return src[pl.ds(r0 + off, rows), ls]
    return sh[r - 1, pl.ds(r0 + off - r, rows), :]


def _conv_fwd(p, cw, cb_, lng, lnb):
    T = p.shape[1]
    tm = min(512, T)
    n = HALO + tm - 8

    def body(p_ref, cw_ref, cb_ref, g_ref, b_ref, u1_ref, u2_ref, buf, sh):
        @pl.when(pl.program_id(0) == 0)
        def _():
            buf[0:HALO, :] = jnp.zeros((HALO, D), F32)

        buf[HALO:HALO + tm, :] = p_ref[0] * _sig(p_ref[1])
        for lb in range(D // LANES):
            ls = slice(lb * LANES, (lb + 1) * LANES)
            _shift_rows(buf, sh, ls, n)
            taps = [cw_ref[j:j + 1, ls] for j in range(CONV_K)]
            bias = cb_ref[:, ls]

            def rows_body(rb, carry):
                r0 = pl.multiple_of(rb * CONV_RB, CONV_RB)
                acc = jnp.broadcast_to(bias, (CONV_RB, LANES))
                for j in range(CONV_K):
                    acc = acc + taps[j] * _tap(buf, sh, ls, HALO - (CONV_K - 1) + j, r0, CONV_RB)
                u1_ref[pl.ds(r0, CONV_RB), ls] = acc
                return carry

            lax.fori_loop(0, tm // CONV_RB, rows_body, 0)
        y, _, _ = _ln_fwd(u1_ref[...], g_ref[...], b_ref[...])
        u2_ref[...] = (y * _sig(y)).astype(ACT)
        buf[0:HALO, :] = buf[tm:tm + HALO, :]

    return pl.pallas_call(
        body, name="conv_fwd", grid=(T // tm,),
        in_specs=[pl.BlockSpec((2, tm, D), lambda i: (2, i, 0)), pl.BlockSpec((HALO, D), lambda i: (0, 0)),
                  pl.BlockSpec((1, D), lambda i: (0, 0)), pl.BlockSpec((1, D), lambda i: (0, 0)),
                  pl.BlockSpec((1, D), lambda i: (0, 0))],
        out_specs=[pl.BlockSpec((tm, D), lambda i: (i, 0)), pl.BlockSpec((tm, D), lambda i: (i, 0))],
        out_shape=[jax.ShapeDtypeStruct((T, D), F32), jax.ShapeDtypeStruct((T, D), ACT)],
        scratch_shapes=[pltpu.VMEM((HALO + tm, D), F32), pltpu.VMEM((7, n, LANES), F32)],
        compiler_params=_cparams(1),
    )(p, cw, cb_, lng, lnb)


def _conv_bwd(p, u1, du2, cw, lng, lnb, dp):
    T = p.shape[1]
    tm = min(512, T)
    ni = T // tm
    hb = tm // HALO

    n = HALO + tm - 8

    def body(p_ref, ph_ref, u1_ref, du2_ref, cw_ref, g_ref, b_ref, dp_in, dp_ref, dcw_ref, sm_ref, ubuf, dbuf,
             sh, dacc):
        del dp_in
        step = pl.program_id(0)

        @pl.when(step == 0)
        def _():
            dbuf[tm:tm + HALO, :] = jnp.zeros((HALO, D), F32)
            dcw_ref[...] = jnp.zeros_like(dcw_ref)
            sm_ref[...] = jnp.zeros_like(sm_ref)

        ua = p_ref[0]
        sgb = _sig(p_ref[1])
        halo = ph_ref[0] * _sig(ph_ref[1])
        ubuf[0:HALO, :] = jnp.where(step == ni - 1, 0.0, halo)
        ubuf[HALO:HALO + tm, :] = ua * sgb
        g = g_ref[...]
        y, xh, rs = _ln_fwd(u1_ref[...], g, b_ref[...])
        sy = _sig(y)
        dy = du2_ref[...] * sy * (1.0 + y * (1.0 - sy))
        sm_ref[1:2, :] += _colsum(dy * xh)
        sm_ref[2:3, :] += _colsum(dy)
        dxh = dy * g
        du1 = rs * (dxh - _rowmean(dxh) - xh * _rowmean(dxh * xh))
        sm_ref[0:1, :] += _colsum(du1)
        dbuf[0:tm, :] = du1
        for lb in range(D // LANES):
            ls = slice(lb * LANES, (lb + 1) * LANES)
            taps = [cw_ref[j:j + 1, ls] for j in range(CONV_K)]
            _shift_rows(dbuf, sh, ls, n)

            def du0_body(rb, carry):
                r0 = pl.multiple_of(rb * CONV_RB, CONV_RB)
                acc = jnp.zeros((CONV_RB, LANES), F32)
                for j in range(CONV_K):
                    acc = acc + taps[j] * _tap(dbuf, sh, ls, CONV_K - 1 - j, r0, CONV_RB)
                dp_ref[0, pl.ds(r0, CONV_RB), ls] = acc.astype(ACT)
                return carry

            lax.fori_loop(0, tm // CONV_RB, du0_body, 0)
            _shift_rows(ubuf, sh, ls, n)
            dacc[...] = jnp.zeros_like(dacc)

            def dcw_body(rb, carry):
                r0 = pl.multiple_of(rb * CONV_RB, CONV_RB)
                d = dbuf[pl.ds(r0, CONV_RB), ls]
                for j in range(CONV_K):
                    prod = d * _tap(ubuf, sh, ls, HALO - (CONV_K - 1) + j, r0, CONV_RB)
                    dacc[8 * j:8 * j + 8, :] += jnp.sum(prod.reshape(CONV_RB // 8, 8, LANES), axis=0)
                return carry

            lax.fori_loop(0, tm // CONV_RB, dcw_body, 0)
            for j in range(CONV_K):
                dcw_ref[j:j + 1, ls] += _colsum(dacc[8 * j:8 * j + 8, :])
        du0 = dp_ref[0].astype(F32)
        dp_ref[0] = (du0 * sgb).astype(ACT)
        dp_ref[1] = (du0 * ua * sgb * (1.0 - sgb)).astype(ACT)
        dbuf[tm:tm + HALO, :] = dbuf[0:HALO, :]

    rev = lambda i: ni - 1 - i
    return pl.pallas_call(
        body, name="conv_bwd", grid=(ni,),
        in_specs=[pl.BlockSpec((2, tm, D), lambda i: (2, rev(i), 0)),
                  pl.BlockSpec((2, HALO, D), lambda i: (2, jnp.maximum(rev(i) * hb - 1, 0), 0)),
                  pl.BlockSpec((tm, D), lambda i: (rev(i), 0)), pl.BlockSpec((tm, D), lambda i: (rev(i), 0)),
                  pl.BlockSpec((HALO, D), lambda i: (0, 0)), pl.BlockSpec((1, D), lambda i: (0, 0)),
                  pl.BlockSpec((1, D), lambda i: (0, 0)), pl.BlockSpec(memory_space=pl.ANY)],
        out_specs=[pl.BlockSpec((2, tm, D), lambda i: (2, rev(i), 0)),
                   pl.BlockSpec((HALO, D), lambda i: (0, 0)), pl.BlockSpec((8, D), lambda i: (0, 0))],
        out_shape=[jax.ShapeDtypeStruct(dp.shape, dp.dtype), jax.ShapeDtypeStruct((HALO, D), F32),
                   jax.ShapeDtypeStruct((8, D), F32)],
        input_output_aliases={7: 0},
        scratch_shapes=[pltpu.VMEM((HALO + tm, D), F32), pltpu.VMEM((tm + HALO, D), F32),
                        pltpu.VMEM((7, n, LANES), F32), pltpu.VMEM((8 * CONV_K, LANES), F32)],
        compiler_params=_cparams(1),
    )(p, p, u1, du2, cw, lng, lnb, dp)


def _mixout_fwd(x, oa, u2, p, mod, mo, w_a, w_b, w_o):
    T = x.shape[0]
    tm = min(512, T)

    def body(x_ref, oa_ref, u2_ref, p_ref, mod_ref, wa_ref, wb_ref, wo_ref, xo_ref, ya_ref, yb_ref, mo_ref):
        ya = _mm(oa_ref[...], wa_ref[...])
        yb = _mm(u2_ref[...], wb_ref[...])
        ya_ref[...] = ya.astype(ACT)
        yb_ref[...] = yb.astype(ACT)
        merged = _sig(p_ref[0]) * ya + _sig(p_ref[1]) * yb
        out = _mm(merged, wo_ref[...])
        mo_ref[...] = out
        xo_ref[...] = x_ref[...] + mod_ref[mo + 2:mo + 3, :] * out

    tile = pl.BlockSpec((tm, D), lambda i: (i, 0))
    wspec = pl.BlockSpec((D, D), lambda i: (0, 0))
    return pl.pallas_call(
        body, name="mixout_fwd", grid=(T // tm,),
        in_specs=[tile, tile, tile, pl.BlockSpec((2, tm, D), lambda i: (3, i, 0)),
                  pl.BlockSpec((9, D), lambda i: (0, 0)), wspec, wspec, wspec],
        out_specs=[tile, tile, tile, tile],
        out_shape=[jax.ShapeDtypeStruct((T, D), F32), jax.ShapeDtypeStruct((T, D), ACT),
                   jax.ShapeDtypeStruct((T, D), ACT), jax.ShapeDtypeStruct((T, D), F32)],
        compiler_params=_cparams(1),
    )(x, oa, u2, p, mod, w_a, w_b, w_o)


def _mixout_bwd(dxo, oa, u2, ya, yb, mout, p, mod, mo, w_a, w_b, w_o):
    T = dxo.shape[0]
    tm = min(256, T)

    def body(dxo_ref, oa_ref, u2_ref, ya_ref, yb_ref, mo_ref, p_ref, mod_ref, wa_ref, wb_ref, wo_ref,
             dp_ref, doa_ref, du2_ref, dwa_ref, dwb_ref, dwo_ref, sm_ref):
        @pl.when(pl.program_id(0) == 0)
        def _():
            dwa_ref[...] = jnp.zeros_like(dwa_ref)
            dwb_ref[...] = jnp.zeros_like(dwb_ref)
            dwo_ref[...] = jnp.zeros_like(dwo_ref)
            sm_ref[...] = jnp.zeros_like(sm_ref)

        dxo_v = dxo_ref[...]
        sm_ref[2:3, :] += _colsum(dxo_v * mo_ref[...])
        dmo = (mod_ref[mo + 2:mo + 3, :] * dxo_v).astype(MM)
        ya = ya_ref[...].astype(F32)
        yb = yb_ref[...].astype(F32)
        sga = _sig(p_ref[0])
        sgb = _sig(p_ref[1])
        merged = (sga * ya + sgb * yb).astype(MM)
        dwo_ref[...] += _mm_tn(merged, dmo)
        dmg = _mm_nt(dmo, wo_ref[...])
        dp_ref[0] = (dmg * ya * sga * (1.0 - sga)).astype(ACT)
        dp_ref[1] = (dmg * yb * sgb * (1.0 - sgb)).astype(ACT)
        dya = (dmg * sga).astype(MM)
        dyb = (dmg * sgb).astype(MM)
        dwa_ref[...] += _mm_tn(oa_ref[...], dya)
        dwb_ref[...] += _mm_tn(u2_ref[...], dyb)
        doa_ref[...] = _mm_nt(dya, wa_ref[...])
        du2_ref[...] = _mm_nt(dyb, wb_ref[...])

    tile = pl.BlockSpec((tm, D), lambda i: (i, 0))
    wspec = pl.BlockSpec((D, D), lambda i: (0, 0))
    return pl.pallas_call(
        body, name="mixout_bwd", grid=(T // tm,),
        in_specs=[tile, tile, tile, tile, tile, tile, pl.BlockSpec((2, tm, D), lambda i: (3, i, 0)),
                  pl.BlockSpec((9, D), lambda i: (0, 0)), wspec, wspec, wspec],
        out_specs=[pl.BlockSpec((2, tm, D), lambda i: (3, i, 0)), tile, tile, wspec, wspec, wspec,
                   pl.BlockSpec((8, D), lambda i: (0, 0))],
        out_shape=[jax.ShapeDtypeStruct((8, T, D), ACT), jax.ShapeDtypeStruct((T, D), F32),
                   jax.ShapeDtypeStruct((T, D), F32), jax.ShapeDtypeStruct((D, D), F32),
                   jax.ShapeDtypeStruct((D, D), F32), jax.ShapeDtypeStruct((D, D), F32),
                   jax.ShapeDtypeStruct((8, D), F32)],
        compiler_params=_cparams(1),
    )(dxo, oa, u2, ya, yb, mout, p, mod, w_a, w_b, w_o)


def _ada_wgrad(cs_all, dmod_cols):
    cs_t = jnp.pad(cs_all.T, ((0, 0), (0, HD - N_DEV)))
    dm = jnp.pad(dmod_cols, ((0, HD - N_DEV), (0, 0)))

    def body(cs_ref, d_ref, out_ref):
        out_ref[...] = jnp.dot(cs_ref[...], d_ref[...], preferred_element_type=F32,
                               precision=lax.Precision.HIGHEST)

    return pl.pallas_call(
        body, name="ada_wgrad", out_shape=jax.ShapeDtypeStruct((D, dmod_cols.shape[1]), F32),
        compiler_params=pltpu.CompilerParams(vmem_limit_bytes=VMEM_LIMIT),
    )(cs_t, dm)


def _adam_math(w, g, m, v):
    m2 = ADAM_B1 * m + (1.0 - ADAM_B1) * g
    v2 = ADAM_B2 * v + (1.0 - ADAM_B2) * (g * g)
    m_hat = m2 / (1.0 - ADAM_B1 ** ADAM_STEP)
    v_hat = v2 / (1.0 - ADAM_B2 ** ADAM_STEP)
    delta = -ADAM_LR * (m_hat / (jnp.sqrt(v_hat) + ADAM_EPS) + ADAM_WD * w)
    return delta, m2, v2


def _adamw(w, m, v, g, name):
    R, C = w.shape
    slots = g.ndim == 3
    n_slots = g.shape[0] if slots else 0
    tr = R
    for cand in (256, 176):
        if R % cand == 0 and R > cand:
            tr = cand
            break

    def body(w_ref, m_ref, v_ref, g_ref, go_ref, d_ref, mo_ref, vo_ref):
        if slots:
            gv = g_ref[0].astype(F32)
            for s in range(1, n_slots):
                gv = gv + g_ref[s].astype(F32)
        else:
            gv = g_ref[...]
        go_ref[...] = gv
        d_ref[...], mo_ref[...], vo_ref[...] = _adam_math(w_ref[...], gv, m_ref[...], v_ref[...])

    tile = pl.BlockSpec((tr, C), lambda i: (i, 0))
    gspec = pl.BlockSpec((n_slots, tr, C), lambda i: (0, i, 0)) if slots else tile
    sds = jax.ShapeDtypeStruct((R, C), F32)
    return pl.pallas_call(
        body, name=name, grid=(R // tr,), in_specs=[tile, tile, tile, gspec], out_specs=[tile] * 4,
        out_shape=[sds] * 4, compiler_params=_cparams(1),
    )(w, m, v, g)


def _sum_slots(pack, name, tr):
    n, R, C = pack.shape

    def body(p_ref, out_ref):
        acc = p_ref[0].astype(F32)
        for s in range(1, n):
            acc = acc + p_ref[s].astype(F32)
        out_ref[...] = acc

    return pl.pallas_call(
        body, name=name, grid=(R // tr,), in_specs=[pl.BlockSpec((n, tr, C), lambda i: (0, i, 0))],
        out_specs=pl.BlockSpec((tr, C), lambda i: (i, 0)), out_shape=jax.ShapeDtypeStruct((R, C), F32),
        compiler_params=_cparams(1))(pack)


def _me():
    return lax.axis_index("x"), lax.axis_index("y"), lax.axis_index("c")


def _peer(r):
    x, y, c = _me()
    px = 1 - x if r & 4 else x
    py = 1 - y if r & 2 else y
    pc = 1 - c if r & 1 else c
    return (px, py, pc), 4 * px + 2 * py + pc


def _small_gather(x_ref, out_ref, send_sems, recv_sems):
    R = x_ref.shape[0]
    mx, my, mc = _me()
    me = 4 * mx + 2 * my + mc
    mine = out_ref.at[pl.ds(pl.multiple_of(me * R, 8), R), :]
    copies = []
    for r in range(1, N_DEV):
        dev, _ = _peer(r)
        copies.append(pltpu.make_async_remote_copy(
            src_ref=x_ref, dst_ref=mine, send_sem=send_sems.at[r - 1], recv_sem=recv_sems.at[r - 1],
            device_id=dev, device_id_type=MESH))
    for cp in copies:
        cp.start()
    mine[...] = x_ref[...]
    for r in range(1, N_DEV):
        dev, idx = _peer(r)
        theirs = out_ref.at[pl.ds(pl.multiple_of(idx * R, 8), R), :]
        pltpu.make_async_remote_copy(
            src_ref=x_ref, dst_ref=theirs, send_sem=send_sems.at[r - 1], recv_sem=recv_sems.at[r - 1],
            device_id=dev, device_id_type=MESH).wait_recv()
    for cp in copies:
        cp.wait_send()


def _prologue(cs, ada_w, ada_b_cols, big):
    n = len(big)
    ncol = ada_w.shape[1]
    big_shape, big_sems = _xchg_specs(big, "gather")

    def body(cs_ref, w_ref, b_ref, *rest):
        big_in, cs_all, mod_all, big_out = rest[:n], rest[n], rest[n + 1], rest[n + 2:2 * n + 2]
        mod_scr, s1, r1, s2, r2 = rest[2 * n + 2:2 * n + 7]
        sems = rest[2 * n + 7:]
        _xchg_start(big_in, big_out, sems, "gather")
        _small_gather(cs_ref, cs_all, s1, r1)
        pick = (lax.broadcasted_iota(jnp.int32, (N_DEV, N_DEV * 8), 1)
                == 8 * lax.broadcasted_iota(jnp.int32, (N_DEV, N_DEV * 8), 0)).astype(F32)
        per_device = jnp.dot(pick, cs_all[...], preferred_element_type=F32, precision=lax.Precision.HIGHEST)
        mod_scr[...] = jnp.dot(per_device, w_ref[...], preferred_element_type=F32,
                               precision=lax.Precision.HIGHEST) + b_ref[...]
        _small_gather(mod_scr, mod_all, s2, r2)
        _xchg_wait(big_in, big_out, sems, "gather")

    vmem = pl.BlockSpec(memory_space=pltpu.VMEM)
    hbm = pl.BlockSpec(memory_space=pl.ANY)
    dma7 = pltpu.SemaphoreType.DMA((N_DEV - 1,))
    out = pl.pallas_call(
        body, name="prologue",
        out_shape=[jax.ShapeDtypeStruct((N_DEV * 8, D), F32), jax.ShapeDtypeStruct((N_DEV * 8, ncol), F32)]
        + big_shape,
        in_specs=[vmem, vmem, vmem] + [hbm] * n, out_specs=[vmem, vmem] + [hbm] * n,
        scratch_shapes=[pltpu.VMEM((8, ncol), F32), dma7, dma7, dma7, dma7] + big_sems,
        compiler_params=pltpu.CompilerParams(vmem_limit_bytes=VMEM_LIMIT),
    )(cs, ada_w, ada_b_cols, *big)
    return out[0], out[1], out[2:]


def _allgather_small(x):
    R, C = x.shape

    def body(x_ref, out_ref, send_sems, recv_sems):
        _small_gather(x_ref, out_ref, send_sems, recv_sems)

    return pl.pallas_call(
        body, name="allgather_small_%dx%d" % (R, C),
        out_shape=jax.ShapeDtypeStruct((N_DEV * R, C), F32),
        in_specs=[pl.BlockSpec(memory_space=pltpu.VMEM)], out_specs=pl.BlockSpec(memory_space=pltpu.VMEM),
        scratch_shapes=[pltpu.SemaphoreType.DMA((N_DEV - 1,)), pltpu.SemaphoreType.DMA((N_DEV - 1,))],
    )(x)


N_CHIP = N_DEV // 2


def _xchg_copies(ins, outs, sems, mode):
    send_sems, recv_sems, local_sems = sems
    mx, my, mc = _me()
    me = 4 * mx + 2 * my + mc
    my_chip = 2 * mx + my
    sibling = _peer(1)[0]

    def rdma(a, r, dev, src, slot):
        k = a * (N_DEV - 1) + r - 1
        return pltpu.make_async_remote_copy(
            src_ref=src, dst_ref=outs[a].at[slot], send_sem=send_sems.at[k], recv_sem=recv_sems.at[k],
            device_id=dev, device_id_type=MESH)

    own, sends, relays, recvs = [], [], [], []
    for a in range(len(ins)):
        if mode == "pair":
            for chip in range(N_CHIP):
                src = ins[a].at[2 * chip + 1 - mc]
                sends.append(rdma(a, chip + 1, sibling, src, chip))
                recvs.append(rdma(a, chip + 1, sibling, src, chip))
            continue
        if mode == "quad":
            own.append(pltpu.make_async_copy(ins[a].at[my_chip], outs[a].at[my_chip], local_sems.at[a]))
            for r in (2, 4, 6):
                dev, idx = _peer(r)
                chip = idx // 2
                sends.append(rdma(a, r, dev, ins[a].at[chip], my_chip))
                recvs.append(rdma(a, r, dev, ins[a].at[chip], chip))
            continue
        gather = mode == "gather"
        own.append(pltpu.make_async_copy(ins[a] if gather else ins[a].at[me], outs[a].at[me], local_sems.at[a]))
        for r in range(1, N_DEV):
            dev, idx = _peer(r)
            if not gather:
                sends.append(rdma(a, r, dev, ins[a].at[idx], me))
                recvs.append(rdma(a, r, dev, ins[a].at[idx], idx))
            elif r == 1:
                sends.append(rdma(a, r, dev, ins[a], me))
                recvs.append(rdma(a, r, dev, ins[a], idx))
            elif r % 2 == 0:
                sends.append(rdma(a, r, dev, ins[a], me))
                relays.append((rdma(a, r, dev, ins[a], idx), rdma(a, r + 1, sibling, outs[a].at[idx], idx)))
            else:
                recvs.append(rdma(a, r, sibling, ins[a], idx))
    return own, sends, relays, recvs


def _xchg_start(ins, outs, sems, mode):
    own, sends, _, _ = _xchg_copies(ins, outs, sems, mode)
    for cp in own + sends:
        cp.start()


def _xchg_wait(ins, outs, sems, mode):
    own, sends, relays, recvs = _xchg_copies(ins, outs, sems, mode)
    for arrival, relay in relays:
        arrival.wait_recv()
        relay.start()
    for cp in recvs:
        cp.wait_recv()
    for cp in own:
        cp.wait()
    for cp in sends + [relay for _, relay in relays]:
        cp.wait_send()


def _xchg_specs(arrays, mode):
    n = len(arrays)
    shape = {"gather": lambda s: (N_DEV,) + s, "scatter": lambda s: s, "pair": lambda s: (N_CHIP,) + s[1:],
             "quad": lambda s: s}[mode]
    out_shape = [jax.ShapeDtypeStruct(shape(a.shape), a.dtype) for a in arrays]
    sems = [pltpu.SemaphoreType.DMA((n * (N_DEV - 1),)), pltpu.SemaphoreType.DMA((n * (N_DEV - 1),)),
            pltpu.SemaphoreType.DMA((n,))]
    return out_shape, sems


def _exchange(arrays, mode, name):
    n = len(arrays)

    def body(*refs):
        _xchg_start(refs[:n], refs[n:2 * n], refs[2 * n:], mode)
        _xchg_wait(refs[:n], refs[n:2 * n], refs[2 * n:], mode)

    out_shape, sems = _xchg_specs(arrays, mode)
    return pl.pallas_call(
        body, name=name, out_shape=out_shape,
        in_specs=[pl.BlockSpec(memory_space=pl.ANY)] * n, out_specs=[pl.BlockSpec(memory_space=pl.ANY)] * n,
        scratch_shapes=sems,
    )(*arrays)


def _gridded(body, carry, *, name, grid, in_specs, out_specs, out_shape, scratch_shapes=(), aliases=None):
    if carry is None:
        return pl.pallas_call(
            body, name=name, grid=grid, in_specs=list(in_specs), out_specs=list(out_specs),
            out_shape=list(out_shape), scratch_shapes=list(scratch_shapes), input_output_aliases=aliases or {},
            compiler_params=_cparams(len(grid)))
    arrays, mode = carry
    n, n_in, n_out, n_scr = len(arrays), len(in_specs), len(out_specs), len(scratch_shapes)
    c_shape, c_sems = _xchg_specs(arrays, mode)

    def wrapped(*refs):
        ins, cin = refs[:n_in], refs[n_in:n_in + n]
        o0 = n_in + n
        outs, cout = refs[o0:o0 + n_out], refs[o0 + n_out:o0 + n_out + n]
        s0 = o0 + n_out + n
        scr, sems = refs[s0:s0 + n_scr], refs[s0 + n_scr:]
        first = pl.program_id(0) == 0
        last = pl.program_id(0) == grid[0] - 1
        for ax in range(1, len(grid)):
            first = first & (pl.program_id(ax) == 0)
            last = last & (pl.program_id(ax) == grid[ax] - 1)

        @pl.when(first)
        def _():
            _xchg_start(cin, cout, sems, mode)

        body(*ins, *outs, *scr)

        @pl.when(last)
        def _():
            _xchg_wait(cin, cout, sems, mode)

    hbm = pl.BlockSpec(memory_space=pl.ANY)
    res = pl.pallas_call(
        wrapped, name=name, grid=grid, in_specs=list(in_specs) + [hbm] * n, out_specs=list(out_specs) + [hbm] * n,
        out_shape=list(out_shape) + c_shape, scratch_shapes=list(scratch_shapes) + c_sems,
        input_output_aliases=aliases or {}, compiler_params=_cparams(len(grid)),
    )
    return lambda *args: res(*args, *arrays)


def _local_step(x, target, mod, small, sh, w1):
    w1_in, w1_out = w1[0].reshape(2, D_FF, D), w1[1].reshape(D_FF, D)
    (x1, a1, b1, f1, h1), (wm_in,) = _ffn_fwd(x, mod, 0, small["norm_ffn1"], w1_in, w1_out, 0.5, "ffn1_fwd",
                                              ([sh["mix_w_in"]], "gather"))
    (p, h2), (wh_o, wc_o, wm_o, cw) = _mixin_fwd(
        x1, mod, 3, small["norm_mix"], wm_in,
        ([sh["hgrn_w_o"], sh["conv_w_o"], sh["mix_w_out"], sh["conv_w"]], "gather"))
    wh_o, wc_o, wm_o = wh_o.reshape(D, D), wc_o.reshape(D, D), wm_o.reshape(D, D)
    cw = jnp.pad(cw.transpose(1, 0, 2).reshape(CONV_K, D), ((0, HALO - CONV_K), (0, 0)))
    (o, oa, a_all, s_all), (w2_in, w2_out) = _hgrn_fwd(p, small["hgrn_lb"], small["hgrn_g"],
                                                       ([sh["ffn2_w_in"], sh["ffn2_w_out"]], "gather"))
    w2_in, w2_out = w2_in.reshape(2, D_FF, D), w2_out.reshape(D_FF, D)
    u1, u2 = _conv_fwd(p, cw, small["conv_b"], small["conv_ln_g"], small["conv_ln_b"])
    x2, ya, yb, mout = _mixout_fwd(x1, oa, u2, p, mod, 3, wh_o, wc_o, wm_o)
    (x3, a3, b3, f3, h3), _ = _ffn_fwd(x2, mod, 6, small["norm_ffn2"], w2_in, w2_out, 0.5, "ffn2_fwd", None)
    dx3, sm_head = _head(x3, target, small["norm_final"])

    (da3, db3, dw2_in, dw2_out), _ = _ffn_bwd_w(h3, dx3, a3, b3, mod, 6, w2_out, 0.5, "ffn2_bwd_w", None)
    (dx2, sm3), _ = _ffn_bwd_x(x2, dx3, f3, da3, db3, mod, 6, small["norm_ffn2"], w2_in, 0.5, "ffn2_bwd_x", None)
    dp, doa, du2, dwh_o, dwc_o, dwm_o, sm_mo = _mixout_bwd(dx2, oa, u2, ya, yb, mout, p, mod, 3, wh_o, wc_o, wm_o)
    dp, dcw, sm_cv = _conv_bwd(p, u1, du2, cw, small["conv_ln_g"], small["conv_ln_b"], dp)
    rows = lambda t: t.reshape(N_DEV, -1, D).astype(MM)
    (dp, sm_hg), (r2_in, r2_out) = _hgrn_bwd(p, o, a_all, s_all, doa, small["hgrn_lb"], small["hgrn_g"], dp,
                                             ([rows(dw2_in), rows(dw2_out)], "scatter"))
    (dx1, dwm_in, sm2), (rh_o, rc_o, rm_o, rcw) = _mixin_bwd(
        x1, h2, dx2, dp, mod, 3, small["norm_mix"], wm_in,
        ([rows(dwh_o), rows(dwc_o), rows(dwm_o), dcw[:CONV_K].reshape(CONV_K, N_DEV, -1).transpose(1, 0, 2)],
         "scatter"))
    (da1, db1, dw1_in, dw1_out), (rm_in,) = _ffn_bwd_w(h1, dx1, a1, b1, mod, 0, w1_out, 0.5, "ffn1_bwd_w",
                                                      (_pair_reduce([dwm_in], "pair_mix"), "quad"))
    (dx0, sm1), (r1_in, r1_out) = _ffn_bwd_x(
        x, dx1, f1, da1, db1, mod, 0, small["norm_ffn1"], w1_in, 0.5, "ffn1_bwd_x",
        (_pair_reduce([rows(dw1_in), rows(dw1_out)], "pair_ffn1"), "quad"))

    dmod = jnp.concatenate([sm1[0:3], sm2[0:2], sm_mo[2:3], sm3[0:3]], axis=0)
    gsmall = dict(norm_ffn1=sm1[3:4], norm_mix=sm2[3:4], lb0=sm_hg[0:1], hgrn_g=sm_hg[1:2], conv_b=sm_cv[0:1],
                  conv_ln_g=sm_cv[1:2], conv_ln_b=sm_cv[2:3], norm_ffn2=sm3[3:4], norm_final=sm_head[0:1])
    recv = dict(ffn1_w_in=r1_in, ffn1_w_out=r1_out, mix_w_in=rm_in, hgrn_w_o=rh_o, conv_w=rcw, conv_w_o=rc_o,
                mix_w_out=rm_o, ffn2_w_in=r2_in, ffn2_w_out=r2_out)
    return sm_head[1, 0], dx0, dmod, gsmall, recv


def _pair_add(mine, theirs, core, name):
    _, R, C = theirs.shape

    def body(core_ref, a_ref, b_ref, out_ref):
        del core_ref
        out_ref[0] = (a_ref[0, 0].astype(F32) + b_ref[0].astype(F32)).astype(out_ref.dtype)

    blk = pl.BlockSpec((1, R, C), lambda s, core_ref: (s, 0, 0))
    grid_spec = pltpu.PrefetchScalarGridSpec(
        num_scalar_prefetch=1, grid=(N_CHIP,),
        in_specs=[pl.BlockSpec((1, 1, R, C), lambda s, core_ref: (s, core_ref[0], 0, 0)), blk], out_specs=blk)
    return pl.pallas_call(body, name=name, grid_spec=grid_spec,
                          out_shape=jax.ShapeDtypeStruct(theirs.shape, mine.dtype), compiler_params=_cparams(1),
                          )(core, mine.reshape(N_CHIP, 2, R, C), theirs)


def _pair_reduce(arrays, name):
    theirs = _exchange(arrays, "pair", name)
    core = lax.axis_index("c").astype(jnp.int32).reshape(1)
    return [_pair_add(a, t, core, "%s_add%d" % (name, i)) for i, (a, t) in enumerate(zip(arrays, theirs))]


SMALL_ORDER = ("norm_ffn1", "norm_mix", "lb0", "hgrn_g", "conv_b", "conv_ln_g", "conv_ln_b", "norm_ffn2",
               "norm_final")
PACK_ROWS = 24


def kernel(x, c, ada_w, ada_b, norm_ffn1, ffn1_w_in, ffn1_w_out, norm_mix, mix_w_in, hgrn_lb, hgrn_g, hgrn_w_o, conv_w, conv_b, conv_ln_g, conv_ln_b, conv_w_o, mix_w_out, norm_ffn2, ffn2_w_in, ffn2_w_out, norm_final, loss_target, m_ada_w, m_ada_b, m_norm_ffn1, m_ffn1_w_in, m_ffn1_w_out, m_norm_mix, m_mix_w_in, m_hgrn_lb, m_hgrn_g, m_hgrn_w_o, m_conv_w, m_conv_b, m_conv_ln_g, m_conv_ln_b, m_conv_w_o, m_mix_w_out, m_norm_ffn2, m_ffn2_w_in, m_ffn2_w_out, m_norm_final, v_ada_w, v_ada_b, v_norm_ffn1, v_ffn1_w_in, v_ffn1_w_out, v_norm_mix, v_mix_w_in, v_hgrn_lb, v_hgrn_g, v_hgrn_w_o, v_conv_w, v_conv_b, v_conv_ln_g, v_conv_ln_b, v_conv_w_o, v_mix_w_out, v_norm_ffn2, v_ffn2_w_in, v_ffn2_w_out, v_norm_final):
    mx, my, mc = _me()
    me = 4 * mx + 2 * my + mc
    ncol = ada_w.shape[2]

    sh = dict(ffn1_w_out=ffn1_w_out, mix_w_in=mix_w_in, hgrn_w_o=hgrn_w_o, conv_w_o=conv_w_o,
              mix_w_out=mix_w_out, ffn2_w_out=ffn2_w_out)
    sh = {n: w[0].astype(MM) for n, w in sh.items()}
    sh["ffn1_w_in"] = ffn1_w_in[0].T.astype(MM)
    sh["ffn2_w_in"] = ffn2_w_in[0].T.astype(MM)
    sh["conv_w"] = conv_w[0]
    small = dict(norm_ffn1=norm_ffn1, norm_mix=norm_mix, hgrn_lb=hgrn_lb, hgrn_g=hgrn_g, conv_b=conv_b,
                 conv_ln_g=conv_ln_g, conv_ln_b=conv_ln_b, norm_ffn2=norm_ffn2, norm_final=norm_final.reshape(1, D))

    cs = jnp.broadcast_to(c * jax.nn.sigmoid(c), (8, D))
    ada_b_cols = lax.dynamic_slice(ada_b, (0, me * ncol), (1, ncol))
    cs_all, mod_all, w1 = _prologue(cs, ada_w[0], ada_b_cols, [sh["ffn1_w_in"], sh["ffn1_w_out"]])
    cs_all = cs_all.reshape(N_DEV, 8, D)[:, 0, :]
    mod = lax.dynamic_index_in_dim(mod_all.reshape(N_DEV, N_DEV, ncol), me, axis=1, keepdims=False).reshape(9, D)

    loss_local, dx, dmod, gsmall, recv = _local_step(x[0], loss_target[0], mod, small, sh, w1)
    loss = lax.psum(loss_local, ("x", "y", "c"))

    pack = jnp.concatenate([dmod] + [gsmall[n] for n in SMALL_ORDER]
                           + [jnp.zeros((PACK_ROWS - 9 - len(SMALL_ORDER), D), F32)], axis=0)
    pack_all = _allgather_small(pack).reshape(N_DEV, PACK_ROWS, D)
    tot = _sum_slots(pack_all, "sum_small", PACK_ROWS)
    gs = {n: tot[9 + i:10 + i] for i, n in enumerate(SMALL_ORDER)}
    dmod_all = pack_all[:, 0:9, :].reshape(N_DEV, 9 * D)
    g_ada_b = tot[0:9].reshape(1, 9 * D)
    g_ada_w = _ada_wgrad(cs_all, lax.dynamic_slice(dmod_all, (0, me * ncol), (N_DEV, ncol)))
    z = hgrn_lb.astype(F32)
    p0 = jax.nn.sigmoid(z[0:1] - z[1:2])
    dz0 = p0 * (1.0 - p0) * gs["lb0"]
    g_hgrn_lb = jnp.concatenate([dz0, -dz0], axis=0)

    res = {}
    res["ada_w"] = _adamw(ada_w[0], m_ada_w[0], v_ada_w[0], g_ada_w, "adamw_ada_w")
    big = dict(ffn1_w_in=(ffn1_w_in, m_ffn1_w_in, v_ffn1_w_in), ffn1_w_out=(ffn1_w_out, m_ffn1_w_out, v_ffn1_w_out),
               mix_w_in=(mix_w_in, m_mix_w_in, v_mix_w_in), hgrn_w_o=(hgrn_w_o, m_hgrn_w_o, v_hgrn_w_o),
               conv_w=(conv_w, m_conv_w, v_conv_w), conv_w_o=(conv_w_o, m_conv_w_o, v_conv_w_o),
               mix_w_out=(mix_w_out, m_mix_w_out, v_mix_w_out), ffn2_w_in=(ffn2_w_in, m_ffn2_w_in, v_ffn2_w_in),
               ffn2_w_out=(ffn2_w_out, m_ffn2_w_out, v_ffn2_w_out))
    for n, (w, m, v) in big.items():
        g = recv[n]
        if n in ("ffn1_w_in", "ffn2_w_in"):
            g = _sum_slots(g, "sum_" + n, g.shape[1] // 4).T
        res[n] = _adamw(w[0], m[0], v[0], g, "adamw_" + n)
    sm_names = ("ada_b", "norm_ffn1", "norm_mix", "hgrn_lb", "hgrn_g", "conv_b", "conv_ln_g", "conv_ln_b",
                "norm_ffn2", "norm_final")
    sm_w = dict(ada_b=(ada_b, m_ada_b, v_ada_b), norm_ffn1=(norm_ffn1, m_norm_ffn1, v_norm_ffn1),
                norm_mix=(norm_mix, m_norm_mix, v_norm_mix), hgrn_lb=(hgrn_lb, m_hgrn_lb, v_hgrn_lb),
                hgrn_g=(hgrn_g, m_hgrn_g, v_hgrn_g), conv_b=(conv_b, m_conv_b, v_conv_b),
                conv_ln_g=(conv_ln_g, m_conv_ln_g, v_conv_ln_g), conv_ln_b=(conv_ln_b, m_conv_ln_b, v_conv_ln_b),
                norm_ffn2=(norm_ffn2, m_norm_ffn2, v_norm_ffn2), norm_final=(norm_final, m_norm_final, v_norm_final))
    sm_g = dict(gs, ada_b=g_ada_b, hgrn_lb=g_hgrn_lb)
    rows = {n: sm_w[n][0].size // D for n in sm_names}
    n_rows = sum(rows.values())
    pad = (-n_rows) % 8
    stack = lambda parts: jnp.concatenate([q.reshape(-1, D) for q in parts] + [jnp.ones((pad, D), F32)], axis=0)
    st = _adamw(stack([sm_w[n][0] for n in sm_names]), stack([sm_w[n][1] for n in sm_names]),
                stack([sm_w[n][2] for n in sm_names]), stack([sm_g[n] for n in sm_names]), "adamw_small")
    off = 0
    for n in sm_names:
        res[n] = tuple(t[off:off + rows[n]].reshape(sm_w[n][0].shape) for t in st)
        off += rows[n]

    order = ("ada_w", "ada_b", "norm_ffn1", "ffn1_w_in", "ffn1_w_out", "norm_mix", "mix_w_in", "hgrn_lb", "hgrn_g",
             "hgrn_w_o", "conv_w", "conv_b", "conv_ln_g", "conv_ln_b", "conv_w_o", "mix_w_out", "norm_ffn2",
             "ffn2_w_in", "ffn2_w_out", "norm_final")
    lead = lambda n, t: t[None] if n in big or n == "ada_w" else t
    outs = [loss, dx[None]]
    for j in range(4):
        outs += [lead(n, res[n][j]) for n in order]
    return tuple(outs)
```

```python
import functools

import jax
import jax.numpy as jnp
from jax import lax
from jax.experimental import pallas as pl
from jax.experimental.pallas import tpu as pltpu

F32 = jnp.float32
MM = jnp.bfloat16
ACT = jnp.bfloat16

D = 1024
D_FF = 2816
HEADS = 8
HD = 128
CHUNK = 64
SUB = 16
NSUB = CHUNK // SUB
HGRN_BLOCK = 1024
CONV_K = 31
HALO = 32
EPS = 1e-6
N_DEV = 8
NEG = -1e30
Q_SCALE = HD ** -0.5

ADAM_LR = 0.001
ADAM_B1 = 0.9
ADAM_B2 = 0.999
ADAM_EPS = 1e-08
ADAM_WD = 0.01
ADAM_STEP = 10

VMEM_LIMIT = 60 * 1024 * 1024
MESH = pl.DeviceIdType.MESH


def _cparams(n_axes):
    return pltpu.CompilerParams(dimension_semantics=("arbitrary",) * n_axes, vmem_limit_bytes=VMEM_LIMIT)


def _mm(a, b):
    return lax.dot_general(a.astype(MM), b.astype(MM), (((1,), (0,)), ((), ())), preferred_element_type=F32)


def _mm_nt(a, b):
    return lax.dot_general(a.astype(MM), b.astype(MM), (((1,), (1,)), ((), ())), preferred_element_type=F32)


def _mm_tn(a, b):
    return lax.dot_general(a.astype(MM), b.astype(MM), (((0,), (0,)), ((), ())), preferred_element_type=F32)


def _sig(x):
    return 1.0 / (1.0 + jnp.exp(-x))


def _colsum(x):
    return jnp.sum(x, axis=0, keepdims=True)


def _rowmean(x):
    return jnp.mean(x, axis=-1, keepdims=True)


def _modnorm_fwd(xv, g, sh, sc):
    r = lax.rsqrt(_rowmean(xv * xv) + EPS)
    xh = xv * r
    n = xh * g
    return n * (1.0 + sc) + sh, xh, n, r


def _modnorm_bwd(dh, xh, n, r, g, sc):
    dsc = _colsum(dh * n)
    dsh = _colsum(dh)
    dn = dh * (1.0 + sc)
    dg = _colsum(dn * xh)
    dxh = dn * g
    dx = r * (dxh - xh * _rowmean(dxh * xh))
    return dx, dsh, dsc, dg


def _ffn_fwd(x, mod, mo, gnorm, w_in_t, w_out, res, name, carry):
    T = x.shape[0]
    tm = min(512, T)
    tn = D_FF // 2

    def body(x_ref, mod_ref, g_ref, wi_ref, wo_ref, xo_ref, a_ref, b_ref, f_ref, h_ref):
        xv = x_ref[...]
        h, _, _, _ = _modnorm_fwd(xv, g_ref[...], mod_ref[mo:mo + 1, :], mod_ref[mo + 1:mo + 2, :])
        h = h.astype(ACT)
        h_ref[...] = h
        f = None
        for c0 in range(0, D_FF, tn):
            a = _mm_nt(h, wi_ref[0, c0:c0 + tn, :])
            b = _mm_nt(h, wi_ref[1, c0:c0 + tn, :])
            a_ref[:, c0:c0 + tn] = a.astype(ACT)
            b_ref[:, c0:c0 + tn] = b.astype(ACT)
            part = _mm(a * _sig(a) * b, wo_ref[c0:c0 + tn, :])
            f = part if f is None else f + part
        f_ref[...] = f
        xo_ref[...] = xv + res * mod_ref[mo + 2:mo + 3, :] * f

    tile = pl.BlockSpec((tm, D), lambda i: (i, 0))
    wide = pl.BlockSpec((tm, D_FF), lambda i: (i, 0))
    out = _gridded(
        body, carry, name=name, grid=(T // tm,),
        in_specs=[
            tile,
            pl.BlockSpec((9, D), lambda i: (0, 0)),
            pl.BlockSpec((1, D), lambda i: (0, 0)),
            pl.BlockSpec((2, D_FF, D), lambda i: (0, 0, 0), pipeline_mode=pl.Buffered(1)),
            pl.BlockSpec((D_FF, D), lambda i: (0, 0), pipeline_mode=pl.Buffered(1)),
        ],
        out_specs=[tile, wide, wide, tile, tile],
        out_shape=[
            jax.ShapeDtypeStruct((T, D), F32),
            jax.ShapeDtypeStruct((T, D_FF), ACT),
            jax.ShapeDtypeStruct((T, D_FF), ACT),
            jax.ShapeDtypeStruct((T, D), F32),
            jax.ShapeDtypeStruct((T, D), ACT),
        ],
    )(x, mod, gnorm, w_in_t, w_out)
    return out[:5], out[5:]


def _ffn_bwd_w(h, dxo, a, b, mod, mo, w_out, res, name, carry):
    T = h.shape[0]
    tm = min(1024, T)
    ni = T // tm
    tn = 256
    nj = D_FF // tn

    def body(h_ref, dxo_ref, a_ref, b_ref, mod_ref, wo_ref, da_ref, db_ref, dwi_ref, dwo_ref,
             acc_i, acc_o, df_all, h_all):
        j = pl.program_id(0)
        i = pl.program_id(1)

        @pl.when(i == 0)
        def _():
            acc_i[...] = jnp.zeros_like(acc_i)
            acc_o[...] = jnp.zeros_like(acc_o)

        @pl.when(j == 0)
        def _():
            df_all[i] = (res * mod_ref[mo + 2:mo + 3, :] * dxo_ref[...]).astype(MM)
            h_all[i] = h_ref[...]

        hb = h_all[i]
        df = df_all[i]
        av = a_ref[...].astype(F32)
        bv = b_ref[...].astype(F32)
        sg = _sig(av)
        sa = av * sg
        s = (sa * bv).astype(MM)
        ds = _mm_nt(df, wo_ref[...])
        da = (ds * bv * sg * (1.0 + av * (1.0 - sg))).astype(MM)
        db = (ds * sa).astype(MM)
        da_ref[...] = da
        db_ref[...] = db
        acc_o[...] += _mm_tn(s, df)
        acc_i[0] += _mm_tn(da, hb)
        acc_i[1] += _mm_tn(db, hb)

        @pl.when(i == ni - 1)
        def _():
            dwi_ref[...] = acc_i[...].astype(MM)
            dwo_ref[...] = acc_o[...].astype(MM)

    first = lambda j, i: (jnp.where(j == 0, i, ni - 1), 0)
    out = _gridded(
        body, carry, name=name, grid=(nj, ni),
        in_specs=[
            pl.BlockSpec((tm, D), first),
            pl.BlockSpec((tm, D), first),
            pl.BlockSpec((tm, tn), lambda j, i: (i, j)),
            pl.BlockSpec((tm, tn), lambda j, i: (i, j)),
            pl.BlockSpec((9, D), lambda j, i: (0, 0)),
            pl.BlockSpec((tn, D), lambda j, i: (j, 0)),
        ],
        out_specs=[
            pl.BlockSpec((tm, tn), lambda j, i: (i, j)),
            pl.BlockSpec((tm, tn), lambda j, i: (i, j)),
            pl.BlockSpec((2, tn, D), lambda j, i: (0, j, 0)),
            pl.BlockSpec((tn, D), lambda j, i: (j, 0)),
        ],
        out_shape=[
            jax.ShapeDtypeStruct((T, D_FF), MM),
            jax.ShapeDtypeStruct((T, D_FF), MM),
            jax.ShapeDtypeStruct((2, D_FF, D), MM),
            jax.ShapeDtypeStruct((D_FF, D), MM),
        ],
        scratch_shapes=[pltpu.VMEM((2, tn, D), F32), pltpu.VMEM((tn, D), F32),
                        pltpu.VMEM((ni, tm, D), MM), pltpu.VMEM((ni, tm, D), MM)],
    )(h, dxo, a, b, mod, w_out)
    return out[:4], out[4:]


def _ffn_bwd_x(x, dxo, f, da, db, mod, mo, gnorm, w_in_t, res, name, carry):
    T = x.shape[0]
    tm = min(512, T)
    ni = T // tm
    tn = D_FF // 2
    nj = D_FF // tn

    def body(x_ref, dxo_ref, f_ref, da_ref, db_ref, mod_ref, g_ref, wi_ref, dx_ref, sm_ref, dh_scr):
        j = pl.program_id(0)
        i = pl.program_id(1)

        @pl.when((j == 0) & (i == 0))
        def _():
            sm_ref[...] = jnp.zeros_like(sm_ref)

        @pl.when(j == 0)
        def _():
            dh_scr[i] = jnp.zeros((tm, D), F32)

        dh_scr[i] += _mm(da_ref[...], wi_ref[0]) + _mm(db_ref[...], wi_ref[1])

        @pl.when(j == nj - 1)
        def _():
            sc = mod_ref[mo + 1:mo + 2, :]
            _, xh, n, r = _modnorm_fwd(x_ref[...], g_ref[...], mod_ref[mo:mo + 1, :], sc)
            dxn, dsh, dsc, dg = _modnorm_bwd(dh_scr[i], xh, n, r, g_ref[...], sc)
            dxo_v = dxo_ref[...]
            dx_ref[...] = dxo_v + dxn
            sm_ref[0:1, :] += dsh
            sm_ref[1:2, :] += dsc
            sm_ref[2:3, :] += _colsum(dxo_v * f_ref[...]) * res
            sm_ref[3:4, :] += dg

    last = pl.BlockSpec((tm, D), lambda j, i: (jnp.where(j == nj - 1, i, 0), 0))
    out = _gridded(
        body, carry, name=name, grid=(nj, ni),
        in_specs=[last, last, last,
                  pl.BlockSpec((tm, tn), lambda j, i: (i, j)), pl.BlockSpec((tm, tn), lambda j, i: (i, j)),
                  pl.BlockSpec((9, D), lambda j, i: (0, 0)), pl.BlockSpec((1, D), lambda j, i: (0, 0)),
                  pl.BlockSpec((2, tn, D), lambda j, i: (0, j, 0))],
        out_specs=[last, pl.BlockSpec((8, D), lambda j, i: (0, 0))],
        out_shape=[jax.ShapeDtypeStruct((T, D), F32), jax.ShapeDtypeStruct((8, D), F32)],
        scratch_shapes=[pltpu.VMEM((ni, tm, D), F32)],
    )(x, dxo, f, da, db, mod, gnorm, w_in_t)
    return out[:2], out[2:]


def _head(x, target, gfin):
    T = x.shape[0]
    tm = min(512, T)
    ni = T // tm

    def body(x_ref, t_ref, g_ref, dx_ref, sm_ref):
        i = pl.program_id(0)

        @pl.when(i == 0)
        def _():
            sm_ref[...] = jnp.zeros_like(sm_ref)

        xv = x_ref[...]
        g = g_ref[...]
        r = lax.rsqrt(_rowmean(xv * xv) + EPS)
        xh = xv * r
        e = xh * g - t_ref[...]
        sm_ref[1:2, :] += _colsum(e * e) * (0.5 / D)
        dy = e * (1.0 / D)
        sm_ref[0:1, :] += _colsum(dy * xh)
        dxh = dy * g
        dx_ref[...] = r * (dxh - xh * _rowmean(dxh * xh))

        @pl.when(i == ni - 1)
        def _():
            sm_ref[1:2, :] = jnp.broadcast_to(jnp.sum(sm_ref[1:2, :], axis=-1, keepdims=True), (1, D))

    return pl.pallas_call(
        body, name="head_loss", grid=(ni,),
        in_specs=[pl.BlockSpec((tm, D), lambda i: (i, 0)), pl.BlockSpec((tm, D), lambda i: (i, 0)),
                  pl.BlockSpec((1, D), lambda i: (0, 0))],
        out_specs=[pl.BlockSpec((tm, D), lambda i: (i, 0)), pl.BlockSpec((8, D), lambda i: (0, 0))],
        out_shape=[jax.ShapeDtypeStruct((T, D), F32), jax.ShapeDtypeStruct((8, D), F32)],
        compiler_params=_cparams(1),
    )(x, target, gfin)


def _mixin_fwd(x, mod, mo, gnorm, w, carry):
    T = x.shape[0]
    tm = min(1024, T)
    ni = T // tm

    def body(x_ref, mod_ref, g_ref, w_ref, p_ref, h_ref, h_all):
        i = pl.program_id(1)

        @pl.when(pl.program_id(0) == 0)
        def _():
            h, _, _, _ = _modnorm_fwd(x_ref[...], g_ref[...], mod_ref[mo:mo + 1, :], mod_ref[mo + 1:mo + 2, :])
            h_all[i] = h.astype(ACT)
            h_ref[...] = h.astype(ACT)

        p_ref[0] = _mm(h_all[i], w_ref[0])

    first = lambda k, i: (jnp.where(k == 0, i, ni - 1), 0)
    out = _gridded(
        body, carry, name="mixin_fwd", grid=(8, ni),
        in_specs=[pl.BlockSpec((tm, D), first), pl.BlockSpec((9, D), lambda k, i: (0, 0)),
                  pl.BlockSpec((1, D), lambda k, i: (0, 0)), pl.BlockSpec((1, D, D), lambda k, i: (k, 0, 0))],
        out_specs=[pl.BlockSpec((1, tm, D), lambda k, i: (k, i, 0)), pl.BlockSpec((tm, D), first)],
        out_shape=[jax.ShapeDtypeStruct((8, T, D), F32), jax.ShapeDtypeStruct((T, D), ACT)],
        scratch_shapes=[pltpu.VMEM((ni, tm, D), ACT)],
    )(x, mod, gnorm, w)
    return out[:2], out[2:]


def _mixin_bwd(x, h, dxo, dp, mod, mo, gnorm, w, carry):
    T = x.shape[0]
    tm = min(512, T)
    ni = T // tm

    def body(x_ref, h_ref, dxo_ref, dp_ref, mod_ref, g_ref, w_ref, dx_ref, dw_ref, sm_ref, dh_scr, acc):
        k = pl.program_id(0)
        i = pl.program_id(1)

        @pl.when(i == 0)
        def _():
            acc[...] = jnp.zeros_like(acc)

        @pl.when(k == 0)
        def _():
            dh_scr[i] = jnp.zeros((tm, D), F32)

        @pl.when((k == 0) & (i == 0))
        def _():
            sm_ref[...] = jnp.zeros_like(sm_ref)

        dpk = dp_ref[0].astype(MM)
        acc[...] += _mm_tn(h_ref[...], dpk)
        dh_scr[i] += _mm_nt(dpk, w_ref[0])

        @pl.when(i == ni - 1)
        def _():
            dw_ref[0] = acc[...].astype(MM)

        @pl.when(k == 7)
        def _():
            sc = mod_ref[mo + 1:mo + 2, :]
            _, xh, n, r = _modnorm_fwd(x_ref[...], g_ref[...], mod_ref[mo:mo + 1, :], sc)
            dxn, dsh, dsc, dg = _modnorm_bwd(dh_scr[i], xh, n, r, g_ref[...], sc)
            dx_ref[...] = dxo_ref[...] + dxn
            sm_ref[0:1, :] += dsh
            sm_ref[1:2, :] += dsc
            sm_ref[3:4, :] += dg

    out = _gridded(
        body, carry, name="mixin_bwd", grid=(8, ni),
        in_specs=[pl.BlockSpec((tm, D), lambda k, i: (jnp.where(k == 7, i, 0), 0)),
                  pl.BlockSpec((tm, D), lambda k, i: (i, 0)),
                  pl.BlockSpec((tm, D), lambda k, i: (jnp.where(k == 7, i, 0), 0)),
                  pl.BlockSpec((1, tm, D), lambda k, i: (k, i, 0)), pl.BlockSpec((9, D), lambda k, i: (0, 0)),
                  pl.BlockSpec((1, D), lambda k, i: (0, 0)), pl.BlockSpec((1, D, D), lambda k, i: (k, 0, 0))],
        out_specs=[pl.BlockSpec((tm, D), lambda k, i: (jnp.where(k == 7, i, 0), 0)),
                   pl.BlockSpec((1, D, D), lambda k, i: (k, 0, 0)),
                   pl.BlockSpec((8, D), lambda k, i: (0, 0))],
        out_shape=[jax.ShapeDtypeStruct((T, D), F32), jax.ShapeDtypeStruct((8, D, D), MM),
                   jax.ShapeDtypeStruct((8, D), F32)],
        scratch_shapes=[pltpu.VMEM((ni, tm, D), F32), pltpu.VMEM((D, D), F32)],
    )(x, h, dxo, dp, mod, gnorm, w)
    return out[:3], out[3:]


def _hgrn_consts():
    rows = jnp.arange(SUB * HD) // HD
    e = (rows[:, None] == jnp.arange(HD)[None, :]).astype(MM)
    return e, e.T


def _rows_bcast(ref, cb, first, n):
    parts = [jnp.broadcast_to(ref[pl.ds(c * CHUNK + first, 1), :], (n, HD)) for c in range(cb // CHUNK)]
    return jnp.concatenate(parts, axis=0)


def _hgrn_pre(qr, fr, lb_ref, b_scr, cb):
    z = lb_ref[...]
    lb = _sig(z[0:1, :] - z[1:2, :])
    sq = _sig(qr)
    q = qr * sq * Q_SCALE
    sf = _sig(fr)
    fg = lb + (1.0 - lb) * sf
    lf = jnp.log(fg)
    k = 1.0 - fg
    tl = lax.broadcasted_iota(jnp.int32, (cb, HD), 0) % CHUNK
    bc = lf
    sh = 1
    while sh < CHUNK:
        bc = bc + jnp.where(tl >= sh, pltpu.roll(bc, sh, 0), 0.0)
        sh *= 2
    b_scr[...] = bc
    bl = _rows_bcast(b_scr, cb, CHUNK - 1, CHUNK)
    br = [None] + [_rows_bcast(b_scr, cb, SUB * i - 1, CHUNK) for i in range(1, NSUB)]
    sb = tl // SUB
    bref = jnp.where(sb == 0, bc, jnp.where(sb == 1, br[1], jnp.where(sb == 2, br[2], br[3])))
    eb = jnp.exp(bc)
    ekd = jnp.exp(bl - bc)
    eqo = jnp.exp(bc - bref)
    eko = [None] + [jnp.exp(jnp.where(tl < SUB * i, br[i] - bc, NEG)) for i in range(1, NSUB)]
    return dict(lb=lb, sq=sq, q=q, sf=sf, fg=fg, k=k, tl=tl, sb=sb, b=bc, bl=bl, eb=eb, ekd=ekd, eqo=eqo,
                eko=eko, qe=q * eb, kd=k * ekd, qo=q * eqo, ko=[None] + [k * eko[i] for i in range(1, NSUB)])


def _pad_rows(x):
    return jnp.concatenate([x, jnp.zeros_like(x)], axis=0)


def _by_subblock(sbc, parts):
    out = jnp.zeros_like(parts[1])
    for i in range(1, NSUB):
        out = jnp.where(sbc == i, parts[i], out)
    return out


def _hgrn_fwd(p, hgrn_lb, hgrn_g, carry):
    T = p.shape[1]
    cb = min(HGRN_BLOCK, T)
    nch = cb // CHUNK
    ncb = T // cb
    e_mat, _ = _hgrn_consts()

    def body(p_ref, lb_ref, g_ref, e_ref, o_ref, oa_ref, a_ref, s_ref, st_scr, q_scr, k_scr, b_scr, z_scr):
        @pl.when(pl.program_id(1) == 0)
        def _():
            st_scr[...] = jnp.zeros_like(st_scr)

        v = p_ref[2]
        og = p_ref[3]
        pre = _hgrn_pre(p_ref[0], p_ref[1], lb_ref, b_scr, cb)
        q_scr[...] = pre["q"]
        k_scr[...] = pre["k"]
        ti = lax.broadcasted_iota(jnp.int32, (SUB, HD), 0)

        def zbody(c, carry):
            for i in range(NSUB):
                r0 = pl.multiple_of(c * CHUNK + SUB * i, SUB)
                qi = q_scr[pl.ds(r0, SUB), :]
                bi = b_scr[pl.ds(r0, SUB), :]
                for s in range(SUB):
                    krow = k_scr[pl.ds(r0 + s, 1), :]
                    brow = b_scr[pl.ds(r0 + s, 1), :]
                    if s < 8:
                        zz = qi * krow * jnp.exp(jnp.where(ti >= s, bi - brow, NEG))
                    else:
                        lo = qi[8:] * krow * jnp.exp(jnp.where(ti[8:] >= s, bi[8:] - brow, NEG))
                        zz = jnp.concatenate([jnp.zeros((8, HD), F32), lo], axis=0)
                    z_scr[i, pl.ds(pl.multiple_of(c * SUB, SUB), SUB), s * HD:(s + 1) * HD] = zz.astype(MM)
            return carry

        lax.fori_loop(0, nch, zbody, 0)
        adiag = [_mm(z_scr[i], e_ref[...]) for i in range(NSUB)]
        sbc = lax.broadcasted_iota(jnp.int32, (CHUNK, HD), 0) // SUB
        chunks = [slice(c * CHUNK, (c + 1) * CHUNK) for c in range(nch)]
        offs = [[_mm_nt(pre["qo"][rs], _pad_rows(pre["ko"][i][rs])) for i in range(1, NSUB)] for rs in chunks]
        kv = [_mm_tn(v[rs], pre["kd"][rs]) for rs in chunks]
        a_parts = []
        for c in range(nch):
            dparts = []
            for i in range(NSUB):
                blk = adiag[i][c * SUB:(c + 1) * SUB]
                dparts.append(blk if i == 0 else pltpu.roll(blk, SUB * i, 1))
            a_parts.append(_by_subblock(sbc, [None] + offs[c]) + jnp.concatenate(dparts, axis=0))
        a_ref[0] = jnp.concatenate(a_parts, axis=0)
        o_intra = [_mm(a_parts[c], _pad_rows(v[rs])) for c, rs in enumerate(chunks)]
        states = []
        st = st_scr[...]
        for c in range(nch):
            states.append(st)
            st = st * jnp.exp(b_scr[pl.ds(c * CHUNK + CHUNK - 1, 1), :]) + kv[c]
        st_scr[...] = st
        for c in range(nch):
            s_ref[0, c] = states[c]
        o = jnp.concatenate([o_intra[c] + _mm_nt(pre["qe"][rs], states[c]) for c, rs in enumerate(chunks)], axis=0)
        o_ref[...] = o
        on = o * lax.rsqrt(_rowmean(o * o) + EPS) * g_ref[...]
        oa_ref[...] = (on * og * _sig(og)).astype(ACT)

    out = _gridded(
        body, carry, name="hgrn_fwd", grid=(HEADS, ncb),
        in_specs=[pl.BlockSpec((4, cb, HD), lambda h, c: (0, c, h)),
                  pl.BlockSpec((2, HD), lambda h, c: (0, h)),
                  pl.BlockSpec((1, HD), lambda h, c: (0, h)),
                  pl.BlockSpec((SUB * HD, HD), lambda h, c: (0, 0))],
        out_specs=[pl.BlockSpec((cb, HD), lambda h, c: (c, h)),
                   pl.BlockSpec((cb, HD), lambda h, c: (c, h)),
                   pl.BlockSpec((1, cb, HD), lambda h, c: (h, c, 0)),
                   pl.BlockSpec((1, nch, HD, HD), lambda h, c: (h, c, 0, 0))],
        out_shape=[jax.ShapeDtypeStruct((T, D), F32), jax.ShapeDtypeStruct((T, D), ACT),
                   jax.ShapeDtypeStruct((HEADS, T, HD), F32),
                   jax.ShapeDtypeStruct((HEADS, T // CHUNK, HD, HD), F32)],
        scratch_shapes=[pltpu.VMEM((HD, HD), F32), pltpu.VMEM((cb, HD), F32), pltpu.VMEM((cb, HD), F32),
                        pltpu.VMEM((cb, HD), F32), pltpu.VMEM((NSUB, nch * SUB, SUB * HD), MM)],
    )(p, hgrn_lb, hgrn_g, e_mat)
    return out[:4], out[4:]


def _hgrn_bwd(p, o, a_all, s_all, doa, hgrn_lb, hgrn_g, dp, carry):
    T = p.shape[1]
    cb = min(HGRN_BLOCK, T)
    nch = cb // CHUNK
    ncb = T // cb
    _, et_mat = _hgrn_consts()

    def body(p_ref, o_ref, a_ref, s_ref, doa_ref, lb_ref, g_ref, et_ref, dp_in, dp_ref, sm_ref,
             dst_scr, q_scr, k_scr, b_scr, x_scr, dqd_scr, dkd_scr):
        del dp_in

        @pl.when(pl.program_id(1) == 0)
        def _():
            dst_scr[...] = jnp.zeros_like(dst_scr)
            sm_ref[...] = jnp.zeros_like(sm_ref)

        qr = p_ref[0]
        v = p_ref[2]
        og = p_ref[3]
        pre = _hgrn_pre(qr, p_ref[1], lb_ref, b_scr, cb)
        q, k = pre["q"], pre["k"]
        q_scr[...] = q
        k_scr[...] = k
        g = g_ref[...]
        ov = o_ref[...]
        r = lax.rsqrt(_rowmean(ov * ov) + EPS)
        oh = ov * r
        sgo = _sig(og)
        doa_v = doa_ref[...]
        don = doa_v * og * sgo
        dog = doa_v * oh * g * sgo * (1.0 + og * (1.0 - sgo))
        sm_ref[1:2, :] += _colsum(don * oh)
        doh = don * g
        do = r * (doh - oh * _rowmean(doh * oh))

        sbc = lax.broadcasted_iota(jnp.int32, (CHUNK, HD), 0) // SUB
        row_i = lax.broadcasted_iota(jnp.int32, (CHUNK, HD), 0)
        lane_i = lax.broadcasted_iota(jnp.int32, (CHUNK, HD), 1)
        causal = lane_i <= row_i
        chunks = [slice(c * CHUNK, (c + 1) * CHUNK) for c in range(nch)]
        da_parts = [jnp.where(causal, _mm_nt(do[rs], _pad_rows(v[rs])), 0.0) for rs in chunks]
        dv_parts = [_mm_tn(a_ref[0, rs, :], do[rs])[:CHUNK] for rs in chunks]
        dqoff_mm = [[_mm(da_parts[c], _pad_rows(pre["ko"][i][rs])) for i in range(1, NSUB)]
                    for c, rs in enumerate(chunks)]
        dkoff_mm = [[_mm_tn(jnp.where(sbc == i, da_parts[c], 0.0), pre["qo"][rs])[:CHUNK] for i in range(1, NSUB)]
                    for c, rs in enumerate(chunks)]
        dqoff_parts = [_by_subblock(sbc, [None] + dqoff_mm[c]) for c in range(nch)]
        dkoff_parts = []
        for c, rs in enumerate(chunks):
            dko = pre["eko"][1][rs] * dkoff_mm[c][0]
            for i in range(2, NSUB):
                dko = dko + pre["eko"][i][rs] * dkoff_mm[c][i - 1]
            dkoff_parts.append(dko)
        for i in range(NSUB):
            rows = []
            for c in range(nch):
                blk = da_parts[c][SUB * i:SUB * (i + 1)]
                rows.append(blk if i == 0 else pltpu.roll(blk, HD - SUB * i, 1))
            x_scr[i] = _mm(jnp.concatenate(rows, axis=0), et_ref[...])
        ti = lax.broadcasted_iota(jnp.int32, (SUB, HD), 0)

        def dbody(c, carry):
            for i in range(NSUB):
                r0 = pl.multiple_of(c * CHUNK + SUB * i, SUB)
                qi = q_scr[pl.ds(r0, SUB), :]
                bi = b_scr[pl.ds(r0, SUB), :]
                dq_hi = jnp.zeros((8, HD), F32)
                dq_lo = jnp.zeros((8, HD), F32)
                dk_hi = jnp.zeros((8, HD), F32)
                dk_lo = jnp.zeros((8, HD), F32)
                c0 = pl.multiple_of(c * SUB, SUB)
                t8 = ti[:8]
                for s in range(SUB):
                    krow = k_scr[pl.ds(r0 + s, 1), :]
                    brow = b_scr[pl.ds(r0 + s, 1), :]
                    w_lo = (x_scr[i, pl.ds(c0 + 8, 8), s * HD:(s + 1) * HD]
                            * jnp.exp(jnp.where(t8 + 8 >= s, bi[8:] - brow, NEG)))
                    dq_lo = dq_lo + w_lo * krow
                    col = _colsum(w_lo * qi[8:])
                    if s < 8:
                        w_hi = (x_scr[i, pl.ds(c0, 8), s * HD:(s + 1) * HD]
                                * jnp.exp(jnp.where(t8 >= s, bi[:8] - brow, NEG)))
                        dq_hi = dq_hi + w_hi * krow
                        dk_hi = jnp.where(t8 == s, col + _colsum(w_hi * qi[:8]), dk_hi)
                    else:
                        dk_lo = jnp.where(t8 + 8 == s, col, dk_lo)
                dqd_scr[pl.ds(r0, SUB), :] = jnp.concatenate([dq_hi, dq_lo], axis=0)
                dkd_scr[pl.ds(r0, SUB), :] = jnp.concatenate([dk_hi, dk_lo], axis=0)
            return carry

        lax.fori_loop(0, nch, dbody, 0)
        qdo = [_mm_tn(do[rs], pre["qe"][rs]) for rs in chunks]
        dsts = [None] * nch
        dst = dst_scr[...]
        for c in reversed(range(nch)):
            dsts[c] = dst
            dst = dst * jnp.exp(b_scr[pl.ds(c * CHUNK + CHUNK - 1, 1), :]) + qdo[c]
        dst_scr[...] = dst
        sts = [s_ref[0, c] for c in range(nch)]
        dqe_parts = [_mm(do[rs], sts[c]) for c, rs in enumerate(chunks)]
        dkdec_parts = [_mm(v[rs], dsts[c]) for c, rs in enumerate(chunks)]
        dvi_parts = [_mm_nt(pre["kd"][rs], dsts[c]) for c, rs in enumerate(chunks)]
        debl_parts = [_colsum(dsts[c] * sts[c]) for c in range(nch)]
        dqe = jnp.concatenate(dqe_parts, axis=0)
        dkdec = jnp.concatenate(dkdec_parts, axis=0)
        dq_tot = jnp.concatenate(dqoff_parts, axis=0) * pre["eqo"] + dqd_scr[...] + dqe * pre["eb"]
        dk_inter = dkdec * pre["ekd"]
        dk_tot = jnp.concatenate(dkoff_parts, axis=0) + dkd_scr[...] + dk_inter
        db = q * dq_tot - k * dk_tot
        kdk = k * dk_inter
        dbl = jnp.concatenate(
            [jnp.broadcast_to(jnp.exp(b_scr[pl.ds(c * CHUNK + CHUNK - 1, 1), :]) * debl_parts[c]
                              + _colsum(kdk[c * CHUNK:(c + 1) * CHUNK]), (CHUNK, HD)) for c in range(nch)], axis=0)
        tl = pre["tl"]
        rc = db
        sh = 1
        while sh < CHUNK:
            rc = rc + jnp.where(tl + sh < CHUNK, pltpu.roll(rc, cb - sh, 0), 0.0)
            sh *= 2
        dlf = rc + dbl
        dfg = dlf / pre["fg"] - dk_tot
        sf = pre["sf"]
        lb = pre["lb"]
        sm_ref[0:1, :] += _colsum(dfg * (1.0 - sf))
        sq = pre["sq"]
        dp_ref[0] = (dq_tot * Q_SCALE * sq * (1.0 + qr * (1.0 - sq))).astype(ACT)
        dp_ref[1] = (dfg * (1.0 - lb) * sf * (1.0 - sf)).astype(ACT)
        dp_ref[2] = (jnp.concatenate(dv_parts, axis=0) + jnp.concatenate(dvi_parts, axis=0)).astype(ACT)
        dp_ref[3] = dog.astype(ACT)

    rev = lambda c: ncb - 1 - c
    out = _gridded(
        body, carry, name="hgrn_bwd", grid=(HEADS, ncb),
        in_specs=[pl.BlockSpec((4, cb, HD), lambda h, c: (0, rev(c), h)),
                  pl.BlockSpec((cb, HD), lambda h, c: (rev(c), h)),
                  pl.BlockSpec((1, cb, HD), lambda h, c: (h, rev(c), 0)),
                  pl.BlockSpec((1, nch, HD, HD), lambda h, c: (h, rev(c), 0, 0)),
                  pl.BlockSpec((cb, HD), lambda h, c: (rev(c), h)),
                  pl.BlockSpec((2, HD), lambda h, c: (0, h)),
                  pl.BlockSpec((1, HD), lambda h, c: (0, h)),
                  pl.BlockSpec((HD, SUB * HD), lambda h, c: (0, 0)),
                  pl.BlockSpec(memory_space=pl.ANY)],
        out_specs=[pl.BlockSpec((4, cb, HD), lambda h, c: (0, rev(c), h)),
                   pl.BlockSpec((8, HD), lambda h, c: (0, h))],
        out_shape=[jax.ShapeDtypeStruct(dp.shape, dp.dtype), jax.ShapeDtypeStruct((8, D), F32)],
        aliases={8: 0},
        scratch_shapes=[pltpu.VMEM((HD, HD), F32), pltpu.VMEM((cb, HD), F32), pltpu.VMEM((cb, HD), F32),
                        pltpu.VMEM((cb, HD), F32), pltpu.VMEM((NSUB, nch * SUB, SUB * HD), F32),
                        pltpu.VMEM((cb, HD), F32), pltpu.VMEM((cb, HD), F32)],
    )(p, o, a_all, s_all, doa, hgrn_lb, hgrn_g, et_mat, dp)
    return out[:2], out[2:]


def _ln_fwd(u1, g, b):
    mu = _rowmean(u1)
    xc = u1 - mu
    rs = lax.rsqrt(_rowmean(xc * xc) + EPS)
    xh = xc * rs
    return xh * g + b, xh, rs


CONV_RB = 64
LANES = 128


def _shift_rows(src, sh, ls, n):
    for r in range(1, 8):
        sh[r - 1, 0:n, :] = src[pl.ds(r, n), ls]


def _tap(src, sh, ls, off, r0, rows):
    r = off % 8
    if r == 0:
        return src[pl.ds(r0 + off, rows), ls]
    return sh[r - 1, pl.ds(r0 + off - r, rows), :]


def _conv_fwd(p, cw, cb_, lng, lnb):
    T = p.shape[1]
    tm = min(512, T)
    n = HALO + tm - 8

    def body(p_ref, cw_ref, cb_ref, g_ref, b_ref, u1_ref, u2_ref, buf, sh):
        @pl.when(pl.program_id(0) == 0)
        def _():
            buf[0:HALO, :] = jnp.zeros((HALO, D), F32)

        buf[HALO:HALO + tm, :] = p_ref[0] * _sig(p_ref[1])
        for lb in range(D // LANES):
            ls = slice(lb * LANES, (lb + 1) * LANES)
            _shift_rows(buf, sh, ls, n)
            taps = [cw_ref[j:j + 1, ls] for j in range(CONV_K)]
            bias = cb_ref[:, ls]

            def rows_body(rb, carry):
                r0 = pl.multiple_of(rb * CONV_RB, CONV_RB)
                acc = jnp.broadcast_to(bias, (CONV_RB, LANES))
                for j in range(CONV_K):
                    acc = acc + taps[j] * _tap(buf, sh, ls, HALO - (CONV_K - 1) + j, r0, CONV_RB)
                u1_ref[pl.ds(r0, CONV_RB), ls] = acc
                return carry

            lax.fori_loop(0, tm // CONV_RB, rows_body, 0)
        y, _, _ = _ln_fwd(u1_ref[...], g_ref[...], b_ref[...])
        u2_ref[...] = (y * _sig(y)).astype(ACT)
        buf[0:HALO, :] = buf[tm:tm + HALO, :]

    return pl.pallas_call(
        body, name="conv_fwd", grid=(T // tm,),
        in_specs=[pl.BlockSpec((2, tm, D), lambda i: (2, i, 0)), pl.BlockSpec((HALO, D), lambda i: (0, 0)),
                  pl.BlockSpec((1, D), lambda i: (0, 0)), pl.BlockSpec((1, D), lambda i: (0, 0)),
                  pl.BlockSpec((1, D), lambda i: (0, 0))],
        out_specs=[pl.BlockSpec((tm, D), lambda i: (i, 0)), pl.BlockSpec((tm, D), lambda i: (i, 0))],
        out_shape=[jax.ShapeDtypeStruct((T, D), F32), jax.ShapeDtypeStruct((T, D), ACT)],
        scratch_shapes=[pltpu.VMEM((HALO + tm, D), F32), pltpu.VMEM((7, n, LANES), F32)],
        compiler_params=_cparams(1),
    )(p, cw, cb_, lng, lnb)


def _conv_bwd(p, u1, du2, cw, lng, lnb, dp):
    T = p.shape[1]
    tm = min(512, T)
    ni = T // tm
    hb = tm // HALO

    n = HALO + tm - 8

    def body(p_ref, ph_ref, u1_ref, du2_ref, cw_ref, g_ref, b_ref, dp_in, dp_ref, dcw_ref, sm_ref, ubuf, dbuf,
             sh, dacc):
        del dp_in
        step = pl.program_id(0)

        @pl.when(step == 0)
        def _():
            dbuf[tm:tm + HALO, :] = jnp.zeros((HALO, D), F32)
            dcw_ref[...] = jnp.zeros_like(dcw_ref)
            sm_ref[...] = jnp.zeros_like(sm_ref)

        ua = p_ref[0]
        sgb = _sig(p_ref[1])
        halo = ph_ref[0] * _sig(ph_ref[1])
        ubuf[0:HALO, :] = jnp.where(step == ni - 1, 0.0, halo)
        ubuf[HALO:HALO + tm, :] = ua * sgb
        g = g_ref[...]
        y, xh, rs = _ln_fwd(u1_ref[...], g, b_ref[...])
        sy = _sig(y)
        dy = du2_ref[...] * sy * (1.0 + y * (1.0 - sy))
        sm_ref[1:2, :] += _colsum(dy * xh)
        sm_ref[2:3, :] += _colsum(dy)
        dxh = dy * g
        du1 = rs * (dxh - _rowmean(dxh) - xh * _rowmean(dxh * xh))
        sm_ref[0:1, :] += _colsum(du1)
        dbuf[0:tm, :] = du1
        for lb in range(D // LANES):
            ls = slice(lb * LANES, (lb + 1) * LANES)
            taps = [cw_ref[j:j + 1, ls] for j in range(CONV_K)]
            _shift_rows(dbuf, sh, ls, n)

            def du0_body(rb, carry):
                r0 = pl.multiple_of(rb * CONV_RB, CONV_RB)
                acc = jnp.zeros((CONV_RB, LANES), F32)
                for j in range(CONV_K):
                    acc = acc + taps[j] * _tap(dbuf, sh, ls, CONV_K - 1 - j, r0, CONV_RB)
                dp_ref[0, pl.ds(r0, CONV_RB), ls] = acc.astype(ACT)
                return carry

            lax.fori_loop(0, tm // CONV_RB, du0_body, 0)
            _shift_rows(ubuf, sh, ls, n)
            dacc[...] = jnp.zeros_like(dacc)

            def dcw_body(rb, carry):
                r0 = pl.multiple_of(rb * CONV_RB, CONV_RB)
                d = dbuf[pl.ds(r0, CONV_RB), ls]
                for j in range(CONV_K):
                    prod = d * _tap(ubuf, sh, ls, HALO - (CONV_K - 1) + j, r0, CONV_RB)
                    dacc[8 * j:8 * j + 8, :] += jnp.sum(prod.reshape(CONV_RB // 8, 8, LANES), axis=0)
                return carry

            lax.fori_loop(0, tm // CONV_RB, dcw_body, 0)
            for j in range(CONV_K):
                dcw_ref[j:j + 1, ls] += _colsum(dacc[8 * j:8 * j + 8, :])
        du0 = dp_ref[0].astype(F32)
        dp_ref[0] = (du0 * sgb).astype(ACT)
        dp_ref[1] = (du0 * ua * sgb * (1.0 - sgb)).astype(ACT)
        dbuf[tm:tm + HALO, :] = dbuf[0:HALO, :]

    rev = lambda i: ni - 1 - i
    return pl.pallas_call(
        body, name="conv_bwd", grid=(ni,),
        in_specs=[pl.BlockSpec((2, tm, D), lambda i: (2, rev(i), 0)),
                  pl.BlockSpec((2, HALO, D), lambda i: (2, jnp.maximum(rev(i) * hb - 1, 0), 0)),
                  pl.BlockSpec((tm, D), lambda i: (rev(i), 0)), pl.BlockSpec((tm, D), lambda i: (rev(i), 0)),
                  pl.BlockSpec((HALO, D), lambda i: (0, 0)), pl.BlockSpec((1, D), lambda i: (0, 0)),
                  pl.BlockSpec((1, D), lambda i: (0, 0)), pl.BlockSpec(memory_space=pl.ANY)],
        out_specs=[pl.BlockSpec((2, tm, D), lambda i: (2, rev(i), 0)),
                   pl.BlockSpec((HALO, D), lambda i: (0, 0)), pl.BlockSpec((8, D), lambda i: (0, 0))],
        out_shape=[jax.ShapeDtypeStruct(dp.shape, dp.dtype), jax.ShapeDtypeStruct((HALO, D), F32),
                   jax.ShapeDtypeStruct((8, D), F32)],
        input_output_aliases={7: 0},
        scratch_shapes=[pltpu.VMEM((HALO + tm, D), F32), pltpu.VMEM((tm + HALO, D), F32),
                        pltpu.VMEM((7, n, LANES), F32), pltpu.VMEM((8 * CONV_K, LANES), F32)],
        compiler_params=_cparams(1),
    )(p, p, u1, du2, cw, lng, lnb, dp)


def _mixout_fwd(x, oa, u2, p, mod, mo, w_a, w_b, w_o):
    T = x.shape[0]
    tm = min(512, T)

    def body(x_ref, oa_ref, u2_ref, p_ref, mod_ref, wa_ref, wb_ref, wo_ref, xo_ref, ya_ref, yb_ref, mo_ref):
        ya = _mm(oa_ref[...], wa_ref[...])
        yb = _mm(u2_ref[...], wb_ref[...])
        ya_ref[...] = ya.astype(ACT)
        yb_ref[...] = yb.astype(ACT)
        merged = _sig(p_ref[0]) * ya + _sig(p_ref[1]) * yb
        out = _mm(merged, wo_ref[...])
        mo_ref[...] = out
        xo_ref[...] = x_ref[...] + mod_ref[mo + 2:mo + 3, :] * out

    tile = pl.BlockSpec((tm, D), lambda i: (i, 0))
    wspec = pl.BlockSpec((D, D), lambda i: (0, 0))
    return pl.pallas_call(
        body, name="mixout_fwd", grid=(T // tm,),
        in_specs=[tile, tile, tile, pl.BlockSpec((2, tm, D), lambda i: (3, i, 0)),
                  pl.BlockSpec((9, D), lambda i: (0, 0)), wspec, wspec, wspec],
        out_specs=[tile, tile, tile, tile],
        out_shape=[jax.ShapeDtypeStruct((T, D), F32), jax.ShapeDtypeStruct((T, D), ACT),
                   jax.ShapeDtypeStruct((T, D), ACT), jax.ShapeDtypeStruct((T, D), F32)],
        compiler_params=_cparams(1),
    )(x, oa, u2, p, mod, w_a, w_b, w_o)


def _mixout_bwd(dxo, oa, u2, ya, yb, mout, p, mod, mo, w_a, w_b, w_o):
    T = dxo.shape[0]
    tm = min(256, T)

    def body(dxo_ref, oa_ref, u2_ref, ya_ref, yb_ref, mo_ref, p_ref, mod_ref, wa_ref, wb_ref, wo_ref,
             dp_ref, doa_ref, du2_ref, dwa_ref, dwb_ref, dwo_ref, sm_ref):
        @pl.when(pl.program_id(0) == 0)
        def _():
            dwa_ref[...] = jnp.zeros_like(dwa_ref)
            dwb_ref[...] = jnp.zeros_like(dwb_ref)
            dwo_ref[...] = jnp.zeros_like(dwo_ref)
            sm_ref[...] = jnp.zeros_like(sm_ref)

        dxo_v = dxo_ref[...]
        sm_ref[2:3, :] += _colsum(dxo_v * mo_ref[...])
        dmo = (mod_ref[mo + 2:mo + 3, :] * dxo_v).astype(MM)
        ya = ya_ref[...].astype(F32)
        yb = yb_ref[...].astype(F32)
        sga = _sig(p_ref[0])
        sgb = _sig(p_ref[1])
        merged = (sga * ya + sgb * yb).astype(MM)
        dwo_ref[...] += _mm_tn(merged, dmo)
        dmg = _mm_nt(dmo, wo_ref[...])
        dp_ref[0] = (dmg * ya * sga * (1.0 - sga)).astype(ACT)
        dp_ref[1] = (dmg * yb * sgb * (1.0 - sgb)).astype(ACT)
        dya = (dmg * sga).astype(MM)
        dyb = (dmg * sgb).astype(MM)
        dwa_ref[...] += _mm_tn(oa_ref[...], dya)
        dwb_ref[...] += _mm_tn(u2_ref[...], dyb)
        doa_ref[...] = _mm_nt(dya, wa_ref[...])
        du2_ref[...] = _mm_nt(dyb, wb_ref[...])

    tile = pl.BlockSpec((tm, D), lambda i: (i, 0))
    wspec = pl.BlockSpec((D, D), lambda i: (0, 0))
    return pl.pallas_call(
        body, name="mixout_bwd", grid=(T // tm,),
        in_specs=[tile, tile, tile, tile, tile, tile, pl.BlockSpec((2, tm, D), lambda i: (3, i, 0)),
                  pl.BlockSpec((9, D), lambda i: (0, 0)), wspec, wspec, wspec],
        out_specs=[pl.BlockSpec((2, tm, D), lambda i: (3, i, 0)), tile, tile, wspec, wspec, wspec,
                   pl.BlockSpec((8, D), lambda i: (0, 0))],
        out_shape=[jax.ShapeDtypeStruct((8, T, D), ACT), jax.ShapeDtypeStruct((T, D), F32),
                   jax.ShapeDtypeStruct((T, D), F32), jax.ShapeDtypeStruct((D, D), F32),
                   jax.ShapeDtypeStruct((D, D), F32), jax.ShapeDtypeStruct((D, D), F32),
                   jax.ShapeDtypeStruct((8, D), F32)],
        compiler_params=_cparams(1),
    )(dxo, oa, u2, ya, yb, mout, p, mod, w_a, w_b, w_o)


def _ada_wgrad(cs_all, dmod_cols):
    cs_t = jnp.pad(cs_all.T, ((0, 0), (0, HD - N_DEV)))
    dm = jnp.pad(dmod_cols, ((0, HD - N_DEV), (0, 0)))

    def body(cs_ref, d_ref, out_ref):
        out_ref[...] = jnp.dot(cs_ref[...], d_ref[...], preferred_element_type=F32,
                               precision=lax.Precision.HIGHEST)

    return pl.pallas_call(
        body, name="ada_wgrad", out_shape=jax.ShapeDtypeStruct((D, dmod_cols.shape[1]), F32),
        compiler_params=pltpu.CompilerParams(vmem_limit_bytes=VMEM_LIMIT),
    )(cs_t, dm)


def _adam_math(w, g, m, v):
    m2 = ADAM_B1 * m + (1.0 - ADAM_B1) * g
    v2 = ADAM_B2 * v + (1.0 - ADAM_B2) * (g * g)
    m_hat = m2 / (1.0 - ADAM_B1 ** ADAM_STEP)
    v_hat = v2 / (1.0 - ADAM_B2 ** ADAM_STEP)
    delta = -ADAM_LR * (m_hat / (jnp.sqrt(v_hat) + ADAM_EPS) + ADAM_WD * w)
    return delta, m2, v2


def _adamw(w, m, v, g, name):
    R, C = w.shape
    slots = g.ndim == 3
    n_slots = g.shape[0] if slots else 0
    tr = R
    for cand in (256, 176):
        if R % cand == 0 and R > cand:
            tr = cand
            break

    def body(w_ref, m_ref, v_ref, g_ref, go_ref, d_ref, mo_ref, vo_ref):
        if slots:
            gv = g_ref[0].astype(F32)
            for s in range(1, n_slots):
                gv = gv + g_ref[s].astype(F32)
        else:
            gv = g_ref[...]
        go_ref[...] = gv
        d_ref[...], mo_ref[...], vo_ref[...] = _adam_math(w_ref[...], gv, m_ref[...], v_ref[...])

    tile = pl.BlockSpec((tr, C), lambda i: (i, 0))
    gspec = pl.BlockSpec((n_slots, tr, C), lambda i: (0, i, 0)) if slots else tile
    sds = jax.ShapeDtypeStruct((R, C), F32)
    return pl.pallas_call(
        body, name=name, grid=(R // tr,), in_specs=[tile, tile, tile, gspec], out_specs=[tile] * 4,
        out_shape=[sds] * 4, compiler_params=_cparams(1),
    )(w, m, v, g)


def _sum_slots(pack, name, tr):
    n, R, C = pack.shape

    def body(p_ref, out_ref):
        acc = p_ref[0].astype(F32)
        for s in range(1, n):
            acc = acc + p_ref[s].astype(F32)
        out_ref[...] = acc

    return pl.pallas_call(
        body, name=name, grid=(R // tr,), in_specs=[pl.BlockSpec((n, tr, C), lambda i: (0, i, 0))],
        out_specs=pl.BlockSpec((tr, C), lambda i: (i, 0)), out_shape=jax.ShapeDtypeStruct((R, C), F32),
        compiler_params=_cparams(1))(pack)


def _me():
    return lax.axis_index("x"), lax.axis_index("y"), lax.axis_index("c")


def _peer(r):
    x, y, c = _me()
    px = 1 - x if r & 4 else x
    py = 1 - y if r & 2 else y
    pc = 1 - c if r & 1 else c
    return (px, py, pc), 4 * px + 2 * py + pc


def _small_gather(x_ref, out_ref, send_sems, recv_sems):
    R = x_ref.shape[0]
    mx, my, mc = _me()
    me = 4 * mx + 2 * my + mc
    mine = out_ref.at[pl.ds(pl.multiple_of(me * R, 8), R), :]
    copies = []
    for r in range(1, N_DEV):
        dev, _ = _peer(r)
        copies.append(pltpu.make_async_remote_copy(
            src_ref=x_ref, dst_ref=mine, send_sem=send_sems.at[r - 1], recv_sem=recv_sems.at[r - 1],
            device_id=dev, device_id_type=MESH))
    for cp in copies:
        cp.start()
    mine[...] = x_ref[...]
    for r in range(1, N_DEV):
        dev, idx = _peer(r)
        theirs = out_ref.at[pl.ds(pl.multiple_of(idx * R, 8), R), :]
        pltpu.make_async_remote_copy(
            src_ref=x_ref, dst_ref=theirs, send_sem=send_sems.at[r - 1], recv_sem=recv_sems.at[r - 1],
            device_id=dev, device_id_type=MESH).wait_recv()
    for cp in copies:
        cp.wait_send()


def _prologue(cs, ada_w, ada_b_cols, big):
    n = len(big)
    ncol = ada_w.shape[1]
    big_shape, big_sems = _xchg_specs(big, "gather")

    def body(cs_ref, w_ref, b_ref, *rest):
        big_in, cs_all, mod_all, big_out = rest[:n], rest[n], rest[n + 1], rest[n + 2:2 * n + 2]
        mod_scr, s1, r1, s2, r2 = rest[2 * n + 2:2 * n + 7]
        sems = rest[2 * n + 7:]
        _xchg_start(big_in, big_out, sems, "gather")
        _small_gather(cs_ref, cs_all, s1, r1)
        pick = (lax.broadcasted_iota(jnp.int32, (N_DEV, N_DEV * 8), 1)
                == 8 * lax.broadcasted_iota(jnp.int32, (N_DEV, N_DEV * 8), 0)).astype(F32)
        per_device = jnp.dot(pick, cs_all[...], preferred_element_type=F32, precision=lax.Precision.HIGHEST)
        mod_scr[...] = jnp.dot(per_device, w_ref[...], preferred_element_type=F32,
                               precision=lax.Precision.HIGHEST) + b_ref[...]
        _small_gather(mod_scr, mod_all, s2, r2)
        _xchg_wait(big_in, big_out, sems, "gather")

    vmem = pl.BlockSpec(memory_space=pltpu.VMEM)
    hbm = pl.BlockSpec(memory_space=pl.ANY)
    dma7 = pltpu.SemaphoreType.DMA((N_DEV - 1,))
    out = pl.pallas_call(
        body, name="prologue",
        out_shape=[jax.ShapeDtypeStruct((N_DEV * 8, D), F32), jax.ShapeDtypeStruct((N_DEV * 8, ncol), F32)]
        + big_shape,
        in_specs=[vmem, vmem, vmem] + [hbm] * n, out_specs=[vmem, vmem] + [hbm] * n,
        scratch_shapes=[pltpu.VMEM((8, ncol), F32), dma7, dma7, dma7, dma7] + big_sems,
        compiler_params=pltpu.CompilerParams(vmem_limit_bytes=VMEM_LIMIT),
    )(cs, ada_w, ada_b_cols, *big)
    return out[0], out[1], out[2:]


def _allgather_small(x):
    R, C = x.shape

    def body(x_ref, out_ref, send_sems, recv_sems):
        _small_gather(x_ref, out_ref, send_sems, recv_sems)

    return pl.pallas_call(
        body, name="allgather_small_%dx%d" % (R, C),
        out_shape=jax.ShapeDtypeStruct((N_DEV * R, C), F32),
        in_specs=[pl.BlockSpec(memory_space=pltpu.VMEM)], out_specs=pl.BlockSpec(memory_space=pltpu.VMEM),
        scratch_shapes=[pltpu.SemaphoreType.DMA((N_DEV - 1,)), pltpu.SemaphoreType.DMA((N_DEV - 1,))],
    )(x)


N_CHIP = N_DEV // 2


def _xchg_copies(ins, outs, sems, mode):
    send_sems, recv_sems, local_sems = sems
    mx, my, mc = _me()
    me = 4 * mx + 2 * my + mc
    my_chip = 2 * mx + my
    sibling = _peer(1)[0]

    def rdma(a, r, dev, src, slot):
        k = a * (N_DEV - 1) + r - 1
        return pltpu.make_async_remote_copy(
            src_ref=src, dst_ref=outs[a].at[slot], send_sem=send_sems.at[k], recv_sem=recv_sems.at[k],
            device_id=dev, device_id_type=MESH)

    own, sends, relays, recvs = [], [], [], []
    for a in range(len(ins)):
        if mode == "pair":
            for chip in range(N_CHIP):
                src = ins[a].at[2 * chip + 1 - mc]
                sends.append(rdma(a, chip + 1, sibling, src, chip))
                recvs.append(rdma(a, chip + 1, sibling, src, chip))
            continue
        if mode == "quad":
            own.append(pltpu.make_async_copy(ins[a].at[my_chip], outs[a].at[my_chip], local_sems.at[a]))
            for r in (2, 4, 6):
                dev, idx = _peer(r)
                chip = idx // 2
                sends.append(rdma(a, r, dev, ins[a].at[chip], my_chip))
                recvs.append(rdma(a, r, dev, ins[a].at[chip], chip))
            continue
        gather = mode == "gather"
        own.append(pltpu.make_async_copy(ins[a] if gather else ins[a].at[me], outs[a].at[me], local_sems.at[a]))
        for r in range(1, N_DEV):
            dev, idx = _peer(r)
            if not gather:
                sends.append(rdma(a, r, dev, ins[a].at[idx], me))
                recvs.append(rdma(a, r, dev, ins[a].at[idx], idx))
            elif r == 1:
                sends.append(rdma(a, r, dev, ins[a], me))
                recvs.append(rdma(a, r, dev, ins[a], idx))
            elif r % 2 == 0:
                sends.append(rdma(a, r, dev, ins[a], me))
                relays.append((rdma(a, r, dev, ins[a], idx), rdma(a, r + 1, sibling, outs[a].at[idx], idx)))
            else:
                recvs.append(rdma(a, r, sibling, ins[a], idx))
    return own, sends, relays, recvs


def _xchg_start(ins, outs, sems, mode):
    own, sends, _, _ = _xchg_copies(ins, outs, sems, mode)
    for cp in own + sends:
        cp.start()


def _xchg_wait(ins, outs, sems, mode):
    own, sends, relays, recvs = _xchg_copies(ins, outs, sems, mode)
    for arrival, relay in relays:
        arrival.wait_recv()
        relay.start()
    for cp in recvs:
        cp.wait_recv()
    for cp in own:
        cp.wait()
    for cp in sends + [relay for _, relay in relays]:
        cp.wait_send()


def _xchg_specs(arrays, mode):
    n = len(arrays)
    shape = {"gather": lambda s: (N_DEV,) + s, "scatter": lambda s: s, "pair": lambda s: (N_CHIP,) + s[1:],
             "quad": lambda s: s}[mode]
    out_shape = [jax.ShapeDtypeStruct(shape(a.shape), a.dtype) for a in arrays]
    sems = [pltpu.SemaphoreType.DMA((n * (N_DEV - 1),)), pltpu.SemaphoreType.DMA((n * (N_DEV - 1),)),
            pltpu.SemaphoreType.DMA((n,))]
    return out_shape, sems


def _exchange(arrays, mode, name):
    n = len(arrays)

    def body(*refs):
        _xchg_start(refs[:n], refs[n:2 * n], refs[2 * n:], mode)
        _xchg_wait(refs[:n], refs[n:2 * n], refs[2 * n:], mode)

    out_shape, sems = _xchg_specs(arrays, mode)
    return pl.pallas_call(
        body, name=name, out_shape=out_shape,
        in_specs=[pl.BlockSpec(memory_space=pl.ANY)] * n, out_specs=[pl.BlockSpec(memory_space=pl.ANY)] * n,
        scratch_shapes=sems,
    )(*arrays)


def _gridded(body, carry, *, name, grid, in_specs, out_specs, out_shape, scratch_shapes=(), aliases=None):
    if carry is None:
        return pl.pallas_call(
            body, name=name, grid=grid, in_specs=list(in_specs), out_specs=list(out_specs),
            out_shape=list(out_shape), scratch_shapes=list(scratch_shapes), input_output_aliases=aliases or {},
            compiler_params=_cparams(len(grid)))
    arrays, mode = carry
    n, n_in, n_out, n_scr = len(arrays), len(in_specs), len(out_specs), len(scratch_shapes)
    c_shape, c_sems = _xchg_specs(arrays, mode)

    def wrapped(*refs):
        ins, cin = refs[:n_in], refs[n_in:n_in + n]
        o0 = n_in + n
        outs, cout = refs[o0:o0 + n_out], refs[o0 + n_out:o0 + n_out + n]
        s0 = o0 + n_out + n
        scr, sems = refs[s0:s0 + n_scr], refs[s0 + n_scr:]
        first = pl.program_id(0) == 0
        last = pl.program_id(0) == grid[0] - 1
        for ax in range(1, len(grid)):
            first = first & (pl.program_id(ax) == 0)
            last = last & (pl.program_id(ax) == grid[ax] - 1)

        @pl.when(first)
        def _():
            _xchg_start(cin, cout, sems, mode)

        body(*ins, *outs, *scr)

        @pl.when(last)
        def _():
            _xchg_wait(cin, cout, sems, mode)

    hbm = pl.BlockSpec(memory_space=pl.ANY)
    res = pl.pallas_call(
        wrapped, name=name, grid=grid, in_specs=list(in_specs) + [hbm] * n, out_specs=list(out_specs) + [hbm] * n,
        out_shape=list(out_shape) + c_shape, scratch_shapes=list(scratch_shapes) + c_sems,
        input_output_aliases=aliases or {}, compiler_params=_cparams(len(grid)),
    )
    return lambda *args: res(*args, *arrays)


def _local_step(x, target, mod, small, sh, w1):
    w1_in, w1_out = w1[0].reshape(2, D_FF, D), w1[1].reshape(D_FF, D)
    (x1, a1, b1, f1, h1), (wm_in,) = _ffn_fwd(x, mod, 0, small["norm_ffn1"], w1_in, w1_out, 0.5, "ffn1_fwd",
                                              ([sh["mix_w_in"]], "gather"))
    (p, h2), (wh_o, wc_o, wm_o, cw) = _mixin_fwd(
        x1, mod, 3, small["norm_mix"], wm_in,
        ([sh["hgrn_w_o"], sh["conv_w_o"], sh["mix_w_out"], sh["conv_w"]], "gather"))
    wh_o, wc_o, wm_o = wh_o.reshape(D, D), wc_o.reshape(D, D), wm_o.reshape(D, D)
    cw = jnp.pad(cw.transpose(1, 0, 2).reshape(CONV_K, D), ((0, HALO - CONV_K), (0, 0)))
    (o, oa, a_all, s_all), (w2_in, w2_out) = _hgrn_fwd(p, small["hgrn_lb"], small["hgrn_g"],
                                                       ([sh["ffn2_w_in"], sh["ffn2_w_out"]], "gather"))
    w2_in, w2_out = w2_in.reshape(2, D_FF, D), w2_out.reshape(D_FF, D)
    u1, u2 = _conv_fwd(p, cw, small["conv_b"], small["conv_ln_g"], small["conv_ln_b"])
    x2, ya, yb, mout = _mixout_fwd(x1, oa, u2, p, mod, 3, wh_o, wc_o, wm_o)
    (x3, a3, b3, f3, h3), _ = _ffn_fwd(x2, mod, 6, small["norm_ffn2"], w2_in, w2_out, 0.5, "ffn2_fwd", None)
    dx3, sm_head = _head(x3, target, small["norm_final"])

    (da3, db3, dw2_in, dw2_out), _ = _ffn_bwd_w(h3, dx3, a3, b3, mod, 6, w2_out, 0.5, "ffn2_bwd_w", None)
    (dx2, sm3), _ = _ffn_bwd_x(x2, dx3, f3, da3, db3, mod, 6, small["norm_ffn2"], w2_in, 0.5, "ffn2_bwd_x", None)
    dp, doa, du2, dwh_o, dwc_o, dwm_o, sm_mo = _mixout_bwd(dx2, oa, u2, ya, yb, mout, p, mod, 3, wh_o, wc_o, wm_o)
    dp, dcw, sm_cv = _conv_bwd(p, u1, du2, cw, small["conv_ln_g"], small["conv_ln_b"], dp)
    rows = lambda t: t.reshape(N_DEV, -1, D).astype(MM)
    (dp, sm_hg), (r2_in, r2_out) = _hgrn_bwd(p, o, a_all, s_all, doa, small["hgrn_lb"], small["hgrn_g"], dp,
                                             ([rows(dw2_in), rows(dw2_out)], "scatter"))
    (dx1, dwm_in, sm2), (rh_o, rc_o, rm_o, rcw) = _mixin_bwd(
        x1, h2, dx2, dp, mod, 3, small["norm_mix"], wm_in,
        ([rows(dwh_o), rows(dwc_o), rows(dwm_o), dcw[:CONV_K].reshape(CONV_K, N_DEV, -1).transpose(1, 0, 2)],
         "scatter"))
    (da1, db1, dw1_in, dw1_out), (rm_in,) = _ffn_bwd_w(h1, dx1, a1, b1, mod, 0, w1_out, 0.5, "ffn1_bwd_w",
                                                      (_pair_reduce([dwm_in], "pair_mix"), "quad"))
    (dx0, sm1), (r1_in, r1_out) = _ffn_bwd_x(
        x, dx1, f1, da1, db1, mod, 0, small["norm_ffn1"], w1_in, 0.5, "ffn1_bwd_x",
        (_pair_reduce([rows(dw1_in), rows(dw1_out)], "pair_ffn1"), "quad"))

    dmod = jnp.concatenate([sm1[0:3], sm2[0:2], sm_mo[2:3], sm3[0:3]], axis=0)
    gsmall = dict(norm_ffn1=sm1[3:4], norm_mix=sm2[3:4], lb0=sm_hg[0:1], hgrn_g=sm_hg[1:2], conv_b=sm_cv[0:1],
                  conv_ln_g=sm_cv[1:2], conv_ln_b=sm_cv[2:3], norm_ffn2=sm3[3:4], norm_final=sm_head[0:1])
    recv = dict(ffn1_w_in=r1_in, ffn1_w_out=r1_out, mix_w_in=rm_in, hgrn_w_o=rh_o, conv_w=rcw, conv_w_o=rc_o,
                mix_w_out=rm_o, ffn2_w_in=r2_in, ffn2_w_out=r2_out)
    return sm_head[1, 0], dx0, dmod, gsmall, recv


def _pair_add(mine, theirs, core, name):
    _, R, C = theirs.shape

    def body(core_ref, a_ref, b_ref, out_ref):
        del core_ref
        out_ref[0] = (a_ref[0, 0].astype(F32) + b_ref[0].astype(F32)).astype(out_ref.dtype)

    blk = pl.BlockSpec((1, R, C), lambda s, core_ref: (s, 0, 0))
    grid_spec = pltpu.PrefetchScalarGridSpec(
        num_scalar_prefetch=1, grid=(N_CHIP,),
        in_specs=[pl.BlockSpec((1, 1, R, C), lambda s, core_ref: (s, core_ref[0], 0, 0)), blk], out_specs=blk)
    return pl.pallas_call(body, name=name, grid_spec=grid_spec,
                          out_shape=jax.ShapeDtypeStruct(theirs.shape, mine.dtype), compiler_params=_cparams(1),
                          )(core, mine.reshape(N_CHIP, 2, R, C), theirs)


def _pair_reduce(arrays, name):
    theirs = _exchange(arrays, "pair", name)
    core = lax.axis_index("c").astype(jnp.int32).reshape(1)
    return [_pair_add(a, t, core, "%s_add%d" % (name, i)) for i, (a, t) in enumerate(zip(arrays, theirs))]


SMALL_ORDER = ("norm_ffn1", "norm_mix", "lb0", "hgrn_g", "conv_b", "conv_ln_g", "conv_ln_b", "norm_ffn2",
               "norm_final")
PACK_ROWS = 24


def kernel(x, c, ada_w, ada_b, norm_ffn1, ffn1_w_in, ffn1_w_out, norm_mix, mix_w_in, hgrn_lb, hgrn_g, hgrn_w_o, conv_w, conv_b, conv_ln_g, conv_ln_b, conv_w_o, mix_w_out, norm_ffn2, ffn2_w_in, ffn2_w_out, norm_final, loss_target, m_ada_w, m_ada_b, m_norm_ffn1, m_ffn1_w_in, m_ffn1_w_out, m_norm_mix, m_mix_w_in, m_hgrn_lb, m_hgrn_g, m_hgrn_w_o, m_conv_w, m_conv_b, m_conv_ln_g, m_conv_ln_b, m_conv_w_o, m_mix_w_out, m_norm_ffn2, m_ffn2_w_in, m_ffn2_w_out, m_norm_final, v_ada_w, v_ada_b, v_norm_ffn1, v_ffn1_w_in, v_ffn1_w_out, v_norm_mix, v_mix_w_in, v_hgrn_lb, v_hgrn_g, v_hgrn_w_o, v_conv_w, v_conv_b, v_conv_ln_g, v_conv_ln_b, v_conv_w_o, v_mix_w_out, v_norm_ffn2, v_ffn2_w_in, v_ffn2_w_out, v_norm_final):
    mx, my, mc = _me()
    me = 4 * mx + 2 * my + mc
    ncol = ada_w.shape[2]

    sh = dict(ffn1_w_out=ffn1_w_out, mix_w_in=mix_w_in, hgrn_w_o=hgrn_w_o, conv_w_o=conv_w_o,
              mix_w_out=mix_w_out, ffn2_w_out=ffn2_w_out)
    sh = {n: w[0].astype(MM) for n, w in sh.items()}
    sh["ffn1_w_in"] = ffn1_w_in[0].T.astype(MM)
    sh["ffn2_w_in"] = ffn2_w_in[0].T.astype(MM)
    sh["conv_w"] = conv_w[0]
    small = dict(norm_ffn1=norm_ffn1, norm_mix=norm_mix, hgrn_lb=hgrn_lb, hgrn_g=hgrn_g, conv_b=conv_b,
                 conv_ln_g=conv_ln_g, conv_ln_b=conv_ln_b, norm_ffn2=norm_ffn2, norm_final=norm_final.reshape(1, D))

    cs = jnp.broadcast_to(c * jax.nn.sigmoid(c), (8, D))
    ada_b_cols = lax.dynamic_slice(ada_b, (0, me * ncol), (1, ncol))
    cs_all, mod_all, w1 = _prologue(cs, ada_w[0], ada_b_cols, [sh["ffn1_w_in"], sh["ffn1_w_out"]])
    cs_all = cs_all.reshape(N_DEV, 8, D)[:, 0, :]
    mod = lax.dynamic_index_in_dim(mod_all.reshape(N_DEV, N_DEV, ncol), me, axis=1, keepdims=False).reshape(9, D)

    loss_local, dx, dmod, gsmall, recv = _local_step(x[0], loss_target[0], mod, small, sh, w1)
    loss = lax.psum(loss_local, ("x", "y", "c"))

    pack = jnp.concatenate([dmod] + [gsmall[n] for n in SMALL_ORDER]
                           + [jnp.zeros((PACK_ROWS - 9 - len(SMALL_ORDER), D), F32)], axis=0)
    pack_all = _allgather_small(pack).reshape(N_DEV, PACK_ROWS, D)
    tot = _sum_slots(pack_all, "sum_small", PACK_ROWS)
    gs = {n: tot[9 + i:10 + i] for i, n in enumerate(SMALL_ORDER)}
    dmod_all = pack_all[:, 0:9, :].reshape(N_DEV, 9 * D)
    g_ada_b = tot[0:9].reshape(1, 9 * D)
    g_ada_w = _ada_wgrad(cs_all, lax.dynamic_slice(dmod_all, (0, me * ncol), (N_DEV, ncol)))
    z = hgrn_lb.astype(F32)
    p0 = jax.nn.sigmoid(z[0:1] - z[1:2])
    dz0 = p0 * (1.0 - p0) * gs["lb0"]
    g_hgrn_lb = jnp.concatenate([dz0, -dz0], axis=0)

    res = {}
    res["ada_w"] = _adamw(ada_w[0], m_ada_w[0], v_ada_w[0], g_ada_w, "adamw_ada_w")
    big = dict(ffn1_w_in=(ffn1_w_in, m_ffn1_w_in, v_ffn1_w_in), ffn1_w_out=(ffn1_w_out, m_ffn1_w_out, v_ffn1_w_out),
               mix_w_in=(mix_w_in, m_mix_w_in, v_mix_w_in), hgrn_w_o=(hgrn_w_o, m_hgrn_w_o, v_hgrn_w_o),
               conv_w=(conv_w, m_conv_w, v_conv_w), conv_w_o=(conv_w_o, m_conv_w_o, v_conv_w_o),
               mix_w_out=(mix_w_out, m_mix_w_out, v_mix_w_out), ffn2_w_in=(ffn2_w_in, m_ffn2_w_in, v_ffn2_w_in),
               ffn2_w_out=(ffn2_w_out, m_ffn2_w_out, v_ffn2_w_out))
    for n, (w, m, v) in big.items():
        g = recv[n]
        if n in ("ffn1_w_in", "ffn2_w_in"):
            g = _sum_slots(g, "sum_" + n, g.shape[1] // 4).T
        res[n] = _adamw(w[0], m[0], v[0], g, "adamw_" + n)
    sm_names = ("ada_b", "norm_ffn1", "norm_mix", "hgrn_lb", "hgrn_g", "conv_b", "conv_ln_g", "conv_ln_b",
                "norm_ffn2", "norm_final")
    sm_w = dict(ada_b=(ada_b, m_ada_b, v_ada_b), norm_ffn1=(norm_ffn1, m_norm_ffn1, v_norm_ffn1),
                norm_mix=(norm_mix, m_norm_mix, v_norm_mix), hgrn_lb=(hgrn_lb, m_hgrn_lb, v_hgrn_lb),
                hgrn_g=(hgrn_g, m_hgrn_g, v_hgrn_g), conv_b=(conv_b, m_conv_b, v_conv_b),
                conv_ln_g=(conv_ln_g, m_conv_ln_g, v_conv_ln_g), conv_ln_b=(conv_ln_b, m_conv_ln_b, v_conv_ln_b),
                norm_ffn2=(norm_ffn2, m_norm_ffn2, v_norm_ffn2), norm_final=(norm_final, m_norm_final, v_norm_final))
    sm_g = dict(gs, ada_b=g_ada_b, hgrn_lb=g_hgrn_lb)
    rows = {n: sm_w[n][0].size // D for n in sm_names}
    n_rows = sum(rows.values())
    pad = (-n_rows) % 8
    stack = lambda parts: jnp.concatenate([q.reshape(-1, D) for q in parts] + [jnp.ones((pad, D), F32)], axis=0)
    st = _adamw(stack([sm_w[n][0] for n in sm_names]), stack([sm_w[n][1] for n in sm_names]),
                stack([sm_w[n][2] for n in sm_names]), stack([sm_g[n] for n in sm_names]), "adamw_small")
    off = 0
    for n in sm_names:
        res[n] = tuple(t[off:off + rows[n]].reshape(sm_w[n][0].shape) for t in st)
        off += rows[n]

    order = ("ada_w", "ada_b", "norm_ffn1", "ffn1_w_in", "ffn1_w_out", "norm_mix", "mix_w_in", "hgrn_lb", "hgrn_g",
             "hgrn_w_o", "conv_w", "conv_b", "conv_ln_g", "conv_ln_b", "conv_w_o", "mix_w_out", "norm_ffn2",
             "ffn2_w_in", "ffn2_w_out", "norm_final")
    lead = lambda n, t: t[None] if n in big or n == "ada_w" else t
    outs = [loss, dx[None]]
    for j in range(4):
        outs += [lead(n, res[n][j]) for n in order]
    return tuple(outs)
```

```python
import jax
import jax.numpy as jnp
from jax import lax
from jax.experimental import pallas as pl
from jax.experimental.pallas import tpu as pltpu

F32 = jnp.float32
MM = jnp.bfloat16
ACT = jnp.bfloat16

D = 1024
D_FF = 2816
HEADS = 8
HD = 128
CHUNK = 64
SUB = 16
NSUB = CHUNK // SUB
HGRN_BLOCK = 1024
CONV_K = 31
HALO = 32
EPS = 1e-6
N_DEV = 8
NEG = -1e30
Q_SCALE = HD ** -0.5

ADAM_LR = 0.001
ADAM_B1 = 0.9
ADAM_B2 = 0.999
ADAM_EPS = 1e-08
ADAM_WD = 0.01
ADAM_STEP = 10

V7X_VMEM_BYTES = 64 * 1024 * 1024
VMEM_LIMIT = V7X_VMEM_BYTES - 4 * 1024 * 1024
MESH = pl.DeviceIdType.MESH


def _cparams(n_axes):
    return pltpu.CompilerParams(dimension_semantics=("arbitrary",) * n_axes, vmem_limit_bytes=VMEM_LIMIT)


def _mm(a, b):
    return lax.dot_general(a.astype(MM), b.astype(MM), (((1,), (0,)), ((), ())), preferred_element_type=F32)


def _mm_nt(a, b):
    return lax.dot_general(a.astype(MM), b.astype(MM), (((1,), (1,)), ((), ())), preferred_element_type=F32)


def _mm_tn(a, b):
    return lax.dot_general(a.astype(MM), b.astype(MM), (((0,), (0,)), ((), ())), preferred_element_type=F32)


def _sig(x):
    return 1.0 / (1.0 + jnp.exp(-x))


def _colsum(x):
    return jnp.sum(x, axis=0, keepdims=True)


def _rowmean(x):
    return jnp.mean(x, axis=-1, keepdims=True)


def _modnorm_fwd(xv, g, sh, sc):
    r = lax.rsqrt(_rowmean(xv * xv) + EPS)
    xh = xv * r
    n = xh * g
    return n * (1.0 + sc) + sh, xh, n, r


def _modnorm_bwd(dh, xh, n, r, g, sc):
    dsc = _colsum(dh * n)
    dsh = _colsum(dh)
    dn = dh * (1.0 + sc)
    dg = _colsum(dn * xh)
    dxh = dn * g
    dx = r * (dxh - xh * _rowmean(dxh * xh))
    return dx, dsh, dsc, dg


def _ffn_fwd(x, mod, mo, gnorm, w_in_t, w_out, res, name, carry):
    T = x.shape[0]
    tm = min(512, T)
    tn = D_FF // 2

    def body(x_ref, mod_ref, g_ref, wi_ref, wo_ref, xo_ref, a_ref, b_ref, f_ref, h_ref):
        xv = x_ref[...]
        h, _, _, _ = _modnorm_fwd(xv, g_ref[...], mod_ref[mo:mo + 1, :], mod_ref[mo + 1:mo + 2, :])
        h = h.astype(ACT)
        h_ref[...] = h
        f = None
        for c0 in range(0, D_FF, tn):
            a = _mm_nt(h, wi_ref[0, c0:c0 + tn, :])
            b = _mm_nt(h, wi_ref[1, c0:c0 + tn, :])
            a_ref[:, c0:c0 + tn] = a.astype(ACT)
            b_ref[:, c0:c0 + tn] = b.astype(ACT)
            part = _mm(a * _sig(a) * b, wo_ref[c0:c0 + tn, :])
            f = part if f is None else f + part
        f_ref[...] = f
        xo_ref[...] = xv + res * mod_ref[mo + 2:mo + 3, :] * f

    tile = pl.BlockSpec((tm, D), lambda i: (i, 0))
    wide = pl.BlockSpec((tm, D_FF), lambda i: (i, 0))
    out = _gridded(
        body, carry, name=name, grid=(T // tm,),
        in_specs=[
            tile,
            pl.BlockSpec((9, D), lambda i: (0, 0)),
            pl.BlockSpec((1, D), lambda i: (0, 0)),
            pl.BlockSpec((2, D_FF, D), lambda i: (0, 0, 0), pipeline_mode=pl.Buffered(1)),
            pl.BlockSpec((D_FF, D), lambda i: (0, 0), pipeline_mode=pl.Buffered(1)),
        ],
        out_specs=[tile, wide, wide, tile, tile],
        out_shape=[
            jax.ShapeDtypeStruct((T, D), F32),
            jax.ShapeDtypeStruct((T, D_FF), ACT),
            jax.ShapeDtypeStruct((T, D_FF), ACT),
            jax.ShapeDtypeStruct((T, D), F32),
            jax.ShapeDtypeStruct((T, D), ACT),
        ],
    )(x, mod, gnorm, w_in_t, w_out)
    return out[:5], out[5:]


def _ffn_bwd_w(h, df, a, b, w_out, name, carry):
    T = h.shape[0]
    tm = min(2048, T)
    ni = T // tm
    tn = 256
    nj = D_FF // tn

    def body(h_ref, df_ref, a_ref, b_ref, wo_ref, da_ref, db_ref, dwi_ref, dwo_ref, acc_i, acc_o):
        i = pl.program_id(1)

        @pl.when(i == 0)
        def _():
            acc_i[...] = jnp.zeros_like(acc_i)
            acc_o[...] = jnp.zeros_like(acc_o)

        hb = h_ref[...]
        df = df_ref[...]
        av = a_ref[...].astype(F32)
        bv = b_ref[...].astype(F32)
        sg = _sig(av)
        sa = av * sg
        s = (sa * bv).astype(MM)
        ds = _mm_nt(df, wo_ref[...])
        da = (ds * bv * sg * (1.0 + av * (1.0 - sg))).astype(MM)
        db = (ds * sa).astype(MM)
        da_ref[...] = da
        db_ref[...] = db
        acc_o[...] += _mm_tn(s, df)
        acc_i[0] += _mm_tn(da, hb)
        acc_i[1] += _mm_tn(db, hb)

        @pl.when(i == ni - 1)
        def _():
            dwi_ref[...] = acc_i[...].astype(MM)
            dwo_ref[...] = acc_o[...].astype(MM)

    out = _gridded(
        body, carry, name=name, grid=(nj, ni),
        in_specs=[
            pl.BlockSpec((tm, D), lambda j, i: (i, 0)),
            pl.BlockSpec((tm, D), lambda j, i: (i, 0)),
            pl.BlockSpec((tm, tn), lambda j, i: (i, j)),
            pl.BlockSpec((tm, tn), lambda j, i: (i, j)),
            pl.BlockSpec((tn, D), lambda j, i: (j, 0)),
        ],
        out_specs=[
            pl.BlockSpec((tm, tn), lambda j, i: (i, j)),
            pl.BlockSpec((tm, tn), lambda j, i: (i, j)),
            pl.BlockSpec((2, tn, D), lambda j, i: (0, j, 0)),
            pl.BlockSpec((tn, D), lambda j, i: (j, 0)),
        ],
        out_shape=[
            jax.ShapeDtypeStruct((T, D_FF), MM),
            jax.ShapeDtypeStruct((T, D_FF), MM),
            jax.ShapeDtypeStruct((2, D_FF, D), MM),
            jax.ShapeDtypeStruct((D_FF, D), MM),
        ],
        scratch_shapes=[pltpu.VMEM((2, tn, D), F32), pltpu.VMEM((tn, D), F32)],
    )(h, df, a, b, w_out)
    return out[:4], out[4:]


def _ffn_bwd_x(x, dxo, f, da, db, mod, mo, gnorm, w_in_t, res, name, carry):
    T = x.shape[0]
    tm = min(512, T)
    ni = T // tm
    tn = D_FF // 2
    nj = D_FF // tn

    def body(x_ref, dxo_ref, f_ref, da_ref, db_ref, mod_ref, g_ref, wi_ref, dx_ref, sm_ref, dh_scr):
        j = pl.program_id(0)
        i = pl.program_id(1)

        @pl.when((j == 0) & (i == 0))
        def _():
            sm_ref[...] = jnp.zeros_like(sm_ref)

        @pl.when(j == 0)
        def _():
            dh_scr[i] = jnp.zeros((tm, D), F32)

        dh_scr[i] += _mm(da_ref[...], wi_ref[0]) + _mm(db_ref[...], wi_ref[1])

        @pl.when(j == nj - 1)
        def _():
            sc = mod_ref[mo + 1:mo + 2, :]
            _, xh, n, r = _modnorm_fwd(x_ref[...], g_ref[...], mod_ref[mo:mo + 1, :], sc)
            dxn, dsh, dsc, dg = _modnorm_bwd(dh_scr[i], xh, n, r, g_ref[...], sc)
            dxo_v = dxo_ref[...]
            dx_ref[...] = dxo_v + dxn
            sm_ref[0:1, :] += dsh
            sm_ref[1:2, :] += dsc
            sm_ref[2:3, :] += _colsum(dxo_v * f_ref[...]) * res
            sm_ref[3:4, :] += dg

    last = pl.BlockSpec((tm, D), lambda j, i: (jnp.where(j == nj - 1, i, 0), 0))
    out = _gridded(
        body, carry, name=name, grid=(nj, ni),
        in_specs=[last, last, last,
                  pl.BlockSpec((tm, tn), lambda j, i: (i, j)), pl.BlockSpec((tm, tn), lambda j, i: (i, j)),
                  pl.BlockSpec((9, D), lambda j, i: (0, 0)), pl.BlockSpec((1, D), lambda j, i: (0, 0)),
                  pl.BlockSpec((2, tn, D), lambda j, i: (0, j, 0))],
        out_specs=[last, pl.BlockSpec((8, D), lambda j, i: (0, 0))],
        out_shape=[jax.ShapeDtypeStruct((T, D), F32), jax.ShapeDtypeStruct((8, D), F32)],
        scratch_shapes=[pltpu.VMEM((ni, tm, D), F32)],
    )(x, dxo, f, da, db, mod, gnorm, w_in_t)
    return out[:2], out[2:]


def _head(x, target, gfin, mod, gate_row, res):
    T = x.shape[0]
    tm = min(512, T)
    ni = T // tm

    def body(x_ref, t_ref, g_ref, mod_ref, dx_ref, df_ref, sm_ref):
        i = pl.program_id(0)

        @pl.when(i == 0)
        def _():
            sm_ref[...] = jnp.zeros_like(sm_ref)

        xv = x_ref[...]
        g = g_ref[...]
        r = lax.rsqrt(_rowmean(xv * xv) + EPS)
        xh = xv * r
        e = xh * g - t_ref[...]
        sm_ref[1:2, :] += _colsum(e * e) * (0.5 / D)
        dy = e * (1.0 / D)
        sm_ref[0:1, :] += _colsum(dy * xh)
        dxh = dy * g
        dx = r * (dxh - xh * _rowmean(dxh * xh))
        dx_ref[...] = dx
        df_ref[...] = (res * mod_ref[gate_row:gate_row + 1, :] * dx).astype(MM)

        @pl.when(i == ni - 1)
        def _():
            sm_ref[1:2, :] = jnp.broadcast_to(jnp.sum(sm_ref[1:2, :], axis=-1, keepdims=True), (1, D))

    tile = pl.BlockSpec((tm, D), lambda i: (i, 0))
    return pl.pallas_call(
        body, name="head_loss", grid=(ni,),
        in_specs=[tile, tile, pl.BlockSpec((1, D), lambda i: (0, 0)), pl.BlockSpec((9, D), lambda i: (0, 0))],
        out_specs=[tile, tile, pl.BlockSpec((8, D), lambda i: (0, 0))],
        out_shape=[jax.ShapeDtypeStruct((T, D), F32), jax.ShapeDtypeStruct((T, D), MM),
                   jax.ShapeDtypeStruct((8, D), F32)],
        compiler_params=_cparams(1),
    )(x, target, gfin, mod)


def _mixin_fwd(x, mod, mo, gnorm, w, carry):
    T = x.shape[0]
    tm = min(1024, T)
    ni = T // tm

    def body(x_ref, mod_ref, g_ref, w_ref, p_ref, h_ref, h_all):
        i = pl.program_id(1)

        @pl.when(pl.program_id(0) == 0)
        def _():
            h, _, _, _ = _modnorm_fwd(x_ref[...], g_ref[...], mod_ref[mo:mo + 1, :], mod_ref[mo + 1:mo + 2, :])
            h_all[i] = h.astype(ACT)
            h_ref[...] = h.astype(ACT)

        p_ref[0] = _mm(h_all[i], w_ref[0])

    first = lambda k, i: (jnp.where(k == 0, i, ni - 1), 0)
    out = _gridded(
        body, carry, name="mixin_fwd", grid=(8, ni),
        in_specs=[pl.BlockSpec((tm, D), first), pl.BlockSpec((9, D), lambda k, i: (0, 0)),
                  pl.BlockSpec((1, D), lambda k, i: (0, 0)), pl.BlockSpec((1, D, D), lambda k, i: (k, 0, 0))],
        out_specs=[pl.BlockSpec((1, tm, D), lambda k, i: (k, i, 0)), pl.BlockSpec((tm, D), first)],
        out_shape=[jax.ShapeDtypeStruct((8, T, D), F32), jax.ShapeDtypeStruct((T, D), ACT)],
        scratch_shapes=[pltpu.VMEM((ni, tm, D), ACT)],
    )(x, mod, gnorm, w)
    return out[:2], out[2:]


def _mixin_bwd(x, h, dxo, dp, mod, mo, gnorm, w, next_gate, next_res, carry):
    T = x.shape[0]
    tm = min(512, T)
    ni = T // tm

    def body(x_ref, h_ref, dxo_ref, dp_ref, mod_ref, g_ref, w_ref, dx_ref, dw_ref, sm_ref, df_ref, dh_scr, acc):
        k = pl.program_id(0)
        i = pl.program_id(1)

        @pl.when(i == 0)
        def _():
            acc[...] = jnp.zeros_like(acc)

        @pl.when(k == 0)
        def _():
            dh_scr[i] = jnp.zeros((tm, D), F32)

        @pl.when((k == 0) & (i == 0))
        def _():
            sm_ref[...] = jnp.zeros_like(sm_ref)

        dpk = dp_ref[0].astype(MM)
        acc[...] += _mm_tn(h_ref[...], dpk)
        dh_scr[i] += _mm_nt(dpk, w_ref[0])

        @pl.when(i == ni - 1)
        def _():
            dw_ref[0] = acc[...].astype(MM)

        @pl.when(k == 7)
        def _():
            sc = mod_ref[mo + 1:mo + 2, :]
            _, xh, n, r = _modnorm_fwd(x_ref[...], g_ref[...], mod_ref[mo:mo + 1, :], sc)
            dxn, dsh, dsc, dg = _modnorm_bwd(dh_scr[i], xh, n, r, g_ref[...], sc)
            dx = dxo_ref[...] + dxn
            dx_ref[...] = dx
            df_ref[...] = (next_res * mod_ref[next_gate:next_gate + 1, :] * dx).astype(MM)
            sm_ref[0:1, :] += dsh
            sm_ref[1:2, :] += dsc
            sm_ref[3:4, :] += dg

    last = pl.BlockSpec((tm, D), lambda k, i: (jnp.where(k == 7, i, 0), 0))
    out = _gridded(
        body, carry, name="mixin_bwd", grid=(8, ni),
        in_specs=[pl.BlockSpec((tm, D), lambda k, i: (jnp.where(k == 7, i, 0), 0)),
                  pl.BlockSpec((tm, D), lambda k, i: (i, 0)),
                  pl.BlockSpec((tm, D), lambda k, i: (jnp.where(k == 7, i, 0), 0)),
                  pl.BlockSpec((1, tm, D), lambda k, i: (k, i, 0)), pl.BlockSpec((9, D), lambda k, i: (0, 0)),
                  pl.BlockSpec((1, D), lambda k, i: (0, 0)), pl.BlockSpec((1, D, D), lambda k, i: (k, 0, 0))],
        out_specs=[last, pl.BlockSpec((1, D, D), lambda k, i: (k, 0, 0)), pl.BlockSpec((8, D), lambda k, i: (0, 0)),
                   last],
        out_shape=[jax.ShapeDtypeStruct((T, D), F32), jax.ShapeDtypeStruct((8, D, D), MM),
                   jax.ShapeDtypeStruct((8, D), F32), jax.ShapeDtypeStruct((T, D), MM)],
        scratch_shapes=[pltpu.VMEM((ni, tm, D), F32), pltpu.VMEM((D, D), F32)],
    )(x, h, dxo, dp, mod, gnorm, w)
    return out[:4], out[4:]


def _hgrn_consts():
    rows = jnp.arange(SUB * HD) // HD
    e = (rows[:, None] == jnp.arange(HD)[None, :]).astype(MM)
    return e, e.T


def _rows_bcast(ref, cb, first, n):
    parts = [jnp.broadcast_to(ref[pl.ds(c * CHUNK + first, 1), :], (n, HD)) for c in range(cb // CHUNK)]
    return jnp.concatenate(parts, axis=0)


def _hgrn_pre(qr, fr, lb_ref, b_scr, cb):
    z = lb_ref[...]
    lb = _sig(z[0:1, :] - z[1:2, :])
    sq = _sig(qr)
    q = qr * sq * Q_SCALE
    sf = _sig(fr)
    fg = lb + (1.0 - lb) * sf
    lf = jnp.log(fg)
    k = 1.0 - fg
    tl = lax.broadcasted_iota(jnp.int32, (cb, HD), 0) % CHUNK
    bc = lf
    sh = 1
    while sh < CHUNK:
        bc = bc + jnp.where(tl >= sh, pltpu.roll(bc, sh, 0), 0.0)
        sh *= 2
    b_scr[...] = bc
    bl = _rows_bcast(b_scr, cb, CHUNK - 1, CHUNK)
    br = [None] + [_rows_bcast(b_scr, cb, SUB * i - 1, CHUNK) for i in range(1, NSUB)]
    sb = tl // SUB
    bref = jnp.where(sb == 0, bc, jnp.where(sb == 1, br[1], jnp.where(sb == 2, br[2], br[3])))
    eb = jnp.exp(bc)
    ekd = jnp.exp(bl - bc)
    eqo = jnp.exp(bc - bref)
    eko = [None] + [jnp.exp(jnp.where(tl < SUB * i, br[i] - bc, NEG)) for i in range(1, NSUB)]
    return dict(lb=lb, sq=sq, q=q, sf=sf, fg=fg, k=k, tl=tl, sb=sb, b=bc, bl=bl, eb=eb, ekd=ekd, eqo=eqo,
                eko=eko, qe=q * eb, kd=k * ekd, qo=q * eqo, ko=[None] + [k * eko[i] for i in range(1, NSUB)])


def _pad_rows(x):
    return jnp.concatenate([x, jnp.zeros_like(x)], axis=0)


def _by_subblock(sbc, parts):
    out = jnp.zeros_like(parts[1])
    for i in range(1, NSUB):
        out = jnp.where(sbc == i, parts[i], out)
    return out


def _hgrn_fwd(p, hgrn_lb, hgrn_g, carry):
    T = p.shape[1]
    cb = min(HGRN_BLOCK, T)
    nch = cb // CHUNK
    ncb = T // cb
    e_mat, _ = _hgrn_consts()

    def body(p_ref, lb_ref, g_ref, e_ref, o_ref, oa_ref, a_ref, s_ref, st_scr, q_scr, k_scr, b_scr, z_scr):
        @pl.when(pl.program_id(1) == 0)
        def _():
            st_scr[...] = jnp.zeros_like(st_scr)

        v = p_ref[2]
        og = p_ref[3]
        pre = _hgrn_pre(p_ref[0], p_ref[1], lb_ref, b_scr, cb)
        q_scr[...] = pre["q"]
        k_scr[...] = pre["k"]
        ti = lax.broadcasted_iota(jnp.int32, (SUB, HD), 0)

        def zbody(c, carry):
            for i in range(NSUB):
                r0 = pl.multiple_of(c * CHUNK + SUB * i, SUB)
                qi = q_scr[pl.ds(r0, SUB), :]
                bi = b_scr[pl.ds(r0, SUB), :]
                for s in range(SUB):
                    krow = k_scr[pl.ds(r0 + s, 1), :]
                    brow = b_scr[pl.ds(r0 + s, 1), :]
                    if s < 8:
                        zz = qi * krow * jnp.exp(jnp.where(ti >= s, bi - brow, NEG))
                    else:
                        lo = qi[8:] * krow * jnp.exp(jnp.where(ti[8:] >= s, bi[8:] - brow, NEG))
                        zz = jnp.concatenate([jnp.zeros((8, HD), F32), lo], axis=0)
                    z_scr[i, pl.ds(pl.multiple_of(c * SUB, SUB), SUB), s * HD:(s + 1) * HD] = zz.astype(MM)
            return carry

        lax.fori_loop(0, nch, zbody, 0)
        adiag = [_mm(z_scr[i], e_ref[...]) for i in range(NSUB)]
        sbc = lax.broadcasted_iota(jnp.int32, (CHUNK, HD), 0) // SUB
        chunks = [slice(c * CHUNK, (c + 1) * CHUNK) for c in range(nch)]
        offs = [[_mm_nt(pre["qo"][rs], _pad_rows(pre["ko"][i][rs])) for i in range(1, NSUB)] for rs in chunks]
        kv = [_mm_tn(v[rs], pre["kd"][rs]) for rs in chunks]
        a_parts = []
        for c in range(nch):
            dparts = []
            for i in range(NSUB):
                blk = adiag[i][c * SUB:(c + 1) * SUB]
                dparts.append(blk if i == 0 else pltpu.roll(blk, SUB * i, 1))
            a_parts.append(_by_subblock(sbc, [None] + offs[c]) + jnp.concatenate(dparts, axis=0))
        a_ref[0] = jnp.concatenate(a_parts, axis=0)
        o_intra = [_mm(a_parts[c], _pad_rows(v[rs])) for c, rs in enumerate(chunks)]
        states = []
        st = st_scr[...]
        for c in range(nch):
            states.append(st)
            st = st * jnp.exp(b_scr[pl.ds(c * CHUNK + CHUNK - 1, 1), :]) + kv[c]
        st_scr[...] = st
        for c in range(nch):
            s_ref[0, c] = states[c]
        o = jnp.concatenate([o_intra[c] + _mm_nt(pre["qe"][rs], states[c]) for c, rs in enumerate(chunks)], axis=0)
        o_ref[...] = o
        on = o * lax.rsqrt(_rowmean(o * o) + EPS) * g_ref[...]
        oa_ref[...] = (on * og * _sig(og)).astype(ACT)

    out = _gridded(
        body, carry, name="hgrn_fwd", grid=(HEADS, ncb),
        in_specs=[pl.BlockSpec((4, cb, HD), lambda h, c: (0, c, h)),
                  pl.BlockSpec((2, HD), lambda h, c: (0, h)),
                  pl.BlockSpec((1, HD), lambda h, c: (0, h)),
                  pl.BlockSpec((SUB * HD, HD), lambda h, c: (0, 0))],
        out_specs=[pl.BlockSpec((cb, HD), lambda h, c: (c, h)),
                   pl.BlockSpec((cb, HD), lambda h, c: (c, h)),
                   pl.BlockSpec((1, cb, HD), lambda h, c: (h, c, 0)),
                   pl.BlockSpec((1, nch, HD, HD), lambda h, c: (h, c, 0, 0))],
        out_shape=[jax.ShapeDtypeStruct((T, D), F32), jax.ShapeDtypeStruct((T, D), ACT),
                   jax.ShapeDtypeStruct((HEADS, T, HD), F32),
                   jax.ShapeDtypeStruct((HEADS, T // CHUNK, HD, HD), F32)],
        scratch_shapes=[pltpu.VMEM((HD, HD), F32), pltpu.VMEM((cb, HD), F32), pltpu.VMEM((cb, HD), F32),
                        pltpu.VMEM((cb, HD), F32), pltpu.VMEM((NSUB, nch * SUB, SUB * HD), MM)],
    )(p, hgrn_lb, hgrn_g, e_mat)
    return out[:4], out[4:]


def _hgrn_bwd(p, o, a_all, s_all, doa, hgrn_lb, hgrn_g, dp, carry):
    T = p.shape[1]
    cb = min(HGRN_BLOCK, T)
    nch = cb // CHUNK
    ncb = T // cb
    _, et_mat = _hgrn_consts()

    def body(p_ref, o_ref, a_ref, s_ref, doa_ref, lb_ref, g_ref, et_ref, dp_in, dp_ref, sm_ref,
             dst_scr, q_scr, k_scr, b_scr, x_scr, dqd_scr, dkd_scr):
        del dp_in

        @pl.when(pl.program_id(1) == 0)
        def _():
            dst_scr[...] = jnp.zeros_like(dst_scr)
            sm_ref[...] = jnp.zeros_like(sm_ref)

        qr = p_ref[0]
        v = p_ref[2]
        og = p_ref[3]
        pre = _hgrn_pre(qr, p_ref[1], lb_ref, b_scr, cb)
        q, k = pre["q"], pre["k"]
        q_scr[...] = q
        k_scr[...] = k
        g = g_ref[...]
        ov = o_ref[...]
        r = lax.rsqrt(_rowmean(ov * ov) + EPS)
        oh = ov * r
        sgo = _sig(og)
        doa_v = doa_ref[...]
        don = doa_v * og * sgo
        dog = doa_v * oh * g * sgo * (1.0 + og * (1.0 - sgo))
        sm_ref[1:2, :] += _colsum(don * oh)
        doh = don * g
        do = r * (doh - oh * _rowmean(doh * oh))

        sbc = lax.broadcasted_iota(jnp.int32, (CHUNK, HD), 0) // SUB
        row_i = lax.broadcasted_iota(jnp.int32, (CHUNK, HD), 0)
        lane_i = lax.broadcasted_iota(jnp.int32, (CHUNK, HD), 1)
        causal = lane_i <= row_i
        chunks = [slice(c * CHUNK, (c + 1) * CHUNK) for c in range(nch)]
        da_parts = [jnp.where(causal, _mm_nt(do[rs], _pad_rows(v[rs])), 0.0) for rs in chunks]
        dv_parts = [_mm_tn(a_ref[0, rs, :], do[rs])[:CHUNK] for rs in chunks]
        dqoff_mm = [[_mm(da_parts[c], _pad_rows(pre["ko"][i][rs])) for i in range(1, NSUB)]
                    for c, rs in enumerate(chunks)]
        dkoff_mm = [[_mm_tn(jnp.where(sbc == i, da_parts[c], 0.0), pre["qo"][rs])[:CHUNK] for i in range(1, NSUB)]
                    for c, rs in enumerate(chunks)]
        dqoff_parts = [_by_subblock(sbc, [None] + dqoff_mm[c]) for c in range(nch)]
        dkoff_parts = []
        for c, rs in enumerate(chunks):
            dko = pre["eko"][1][rs] * dkoff_mm[c][0]
            for i in range(2, NSUB):
                dko = dko + pre["eko"][i][rs] * dkoff_mm[c][i - 1]
            dkoff_parts.append(dko)
        for i in range(NSUB):
            rows = []
            for c in range(nch):
                blk = da_parts[c][SUB * i:SUB * (i + 1)]
                rows.append(blk if i == 0 else pltpu.roll(blk, HD - SUB * i, 1))
            x_scr[i] = _mm(jnp.concatenate(rows, axis=0), et_ref[...])
        ti = lax.broadcasted_iota(jnp.int32, (SUB, HD), 0)

        def dbody(c, carry):
            for i in range(NSUB):
                r0 = pl.multiple_of(c * CHUNK + SUB * i, SUB)
                qi = q_scr[pl.ds(r0, SUB), :]
                bi = b_scr[pl.ds(r0, SUB), :]
                dq_hi = jnp.zeros((8, HD), F32)
                dq_lo = jnp.zeros((8, HD), F32)
                dk_hi = jnp.zeros((8, HD), F32)
                dk_lo = jnp.zeros((8, HD), F32)
                c0 = pl.multiple_of(c * SUB, SUB)
                t8 = ti[:8]
                for s in range(SUB):
                    krow = k_scr[pl.ds(r0 + s, 1), :]
                    brow = b_scr[pl.ds(r0 + s, 1), :]
                    w_lo = (x_scr[i, pl.ds(c0 + 8, 8), s * HD:(s + 1) * HD]
                            * jnp.exp(jnp.where(t8 + 8 >= s, bi[8:] - brow, NEG)))
                    dq_lo = dq_lo + w_lo * krow
                    col = _colsum(w_lo * qi[8:])
                    if s < 8:
                        w_hi = (x_scr[i, pl.ds(c0, 8), s * HD:(s + 1) * HD]
                                * jnp.exp(jnp.where(t8 >= s, bi[:8] - brow, NEG)))
                        dq_hi = dq_hi + w_hi * krow
                        dk_hi = jnp.where(t8 == s, col + _colsum(w_hi * qi[:8]), dk_hi)
                    else:
                        dk_lo = jnp.where(t8 + 8 == s, col, dk_lo)
                dqd_scr[pl.ds(r0, SUB), :] = jnp.concatenate([dq_hi, dq_lo], axis=0)
                dkd_scr[pl.ds(r0, SUB), :] = jnp.concatenate([dk_hi, dk_lo], axis=0)
            return carry

        lax.fori_loop(0, nch, dbody, 0)
        qdo = [_mm_tn(do[rs], pre["qe"][rs]) for rs in chunks]
        dsts = [None] * nch
        dst = dst_scr[...]
        for c in reversed(range(nch)):
            dsts[c] = dst
            dst = dst * jnp.exp(b_scr[pl.ds(c * CHUNK + CHUNK - 1, 1), :]) + qdo[c]
        dst_scr[...] = dst
        sts = [s_ref[0, c] for c in range(nch)]
        dqe_parts = [_mm(do[rs], sts[c]) for c, rs in enumerate(chunks)]
        dkdec_parts = [_mm(v[rs], dsts[c]) for c, rs in enumerate(chunks)]
        dvi_parts = [_mm_nt(pre["kd"][rs], dsts[c]) for c, rs in enumerate(chunks)]
        debl_parts = [_colsum(dsts[c] * sts[c]) for c in range(nch)]
        dqe = jnp.concatenate(dqe_parts, axis=0)
        dkdec = jnp.concatenate(dkdec_parts, axis=0)
        dq_tot = jnp.concatenate(dqoff_parts, axis=0) * pre["eqo"] + dqd_scr[...] + dqe * pre["eb"]
        dk_inter = dkdec * pre["ekd"]
        dk_tot = jnp.concatenate(dkoff_parts, axis=0) + dkd_scr[...] + dk_inter
        db = q * dq_tot - k * dk_tot
        kdk = k * dk_inter
        dbl = jnp.concatenate(
            [jnp.broadcast_to(jnp.exp(b_scr[pl.ds(c * CHUNK + CHUNK - 1, 1), :]) * debl_parts[c]
                              + _colsum(kdk[c * CHUNK:(c + 1) * CHUNK]), (CHUNK, HD)) for c in range(nch)], axis=0)
        tl = pre["tl"]
        rc = db
        sh = 1
        while sh < CHUNK:
            rc = rc + jnp.where(tl + sh < CHUNK, pltpu.roll(rc, cb - sh, 0), 0.0)
            sh *= 2
        dlf = rc + dbl
        dfg = dlf / pre["fg"] - dk_tot
        sf = pre["sf"]
        lb = pre["lb"]
        sm_ref[0:1, :] += _colsum(dfg * (1.0 - sf))
        sq = pre["sq"]
        dp_ref[0] = (dq_tot * Q_SCALE * sq * (1.0 + qr * (1.0 - sq))).astype(ACT)
        dp_ref[1] = (dfg * (1.0 - lb) * sf * (1.0 - sf)).astype(ACT)
        dp_ref[2] = (jnp.concatenate(dv_parts, axis=0) + jnp.concatenate(dvi_parts, axis=0)).astype(ACT)
        dp_ref[3] = dog.astype(ACT)

    rev = lambda c: ncb - 1 - c
    out = _gridded(
        body, carry, name="hgrn_bwd", grid=(HEADS, ncb),
        in_specs=[pl.BlockSpec((4, cb, HD), lambda h, c: (0, rev(c), h)),
                  pl.BlockSpec((cb, HD), lambda h, c: (rev(c), h)),
                  pl.BlockSpec((1, cb, HD), lambda h, c: (h, rev(c), 0)),
                  pl.BlockSpec((1, nch, HD, HD), lambda h, c: (h, rev(c), 0, 0)),
                  pl.BlockSpec((cb, HD), lambda h, c: (rev(c), h)),
                  pl.BlockSpec((2, HD), lambda h, c: (0, h)),
                  pl.BlockSpec((1, HD), lambda h, c: (0, h)),
                  pl.BlockSpec((HD, SUB * HD), lambda h, c: (0, 0)),
                  pl.BlockSpec(memory_space=pl.ANY)],
        out_specs=[pl.BlockSpec((4, cb, HD), lambda h, c: (0, rev(c), h)),
                   pl.BlockSpec((8, HD), lambda h, c: (0, h))],
        out_shape=[jax.ShapeDtypeStruct(dp.shape, dp.dtype), jax.ShapeDtypeStruct((8, D), F32)],
        aliases={8: 0},
        scratch_shapes=[pltpu.VMEM((HD, HD), F32), pltpu.VMEM((cb, HD), F32), pltpu.VMEM((cb, HD), F32),
                        pltpu.VMEM((cb, HD), F32), pltpu.VMEM((NSUB, nch * SUB, SUB * HD), F32),
                        pltpu.VMEM((cb, HD), F32), pltpu.VMEM((cb, HD), F32)],
    )(p, o, a_all, s_all, doa, hgrn_lb, hgrn_g, et_mat, dp)
    return out[:2], out[2:]


def _ln_fwd(u1, g, b):
    mu = _rowmean(u1)
    xc = u1 - mu
    rs = lax.rsqrt(_rowmean(xc * xc) + EPS)
    xh = xc * rs
    return xh * g + b, xh, rs


CONV_RB = 64
LANES = 128


def _shift_rows(src, sh, ls, n):
    for r in range(1, 8):
        sh[r - 1, 0:n, :] = src[pl.ds(r, n), ls]


def _tap(src, sh, ls, off, r0, rows):
    r = off % 8
    if r == 0:
        return src[pl.ds(r0 + off, rows), ls]
    return sh[r - 1, pl.ds(r0 + off - r, rows), :]


def _conv_fwd(p, cw, cb_, lng, lnb):
    T = p.shape[1]
    tm = min(512, T)
    n = HALO + tm - 8

    def body(p_ref, cw_ref, cb_ref, g_ref, b_ref, u1_ref, u2_ref, buf, sh):
        @pl.when(pl.program_id(0) == 0)
        def _():
            buf[0:HALO, :] = jnp.zeros((HALO, D), F32)

        buf[HALO:HALO + tm, :] = p_ref[0] * _sig(p_ref[1])
        for lb in range(D // LANES):
            ls = slice(lb * LANES, (lb + 1) * LANES)
            _shift_rows(buf, sh, ls, n)
            taps = [cw_ref[j:j + 1, ls] for j in range(CONV_K)]
            bias = cb_ref[:, ls]

            def rows_body(rb, carry):
                r0 = pl.multiple_of(rb * CONV_RB, CONV_RB)
                acc = jnp.broadcast_to(bias, (CONV_RB, LANES))
                for j in range(CONV_K):
                    acc = acc + taps[j] * _tap(buf, sh, ls, HALO - (CONV_K - 1) + j, r0, CONV_RB)
                u1_ref[pl.ds(r0, CONV_RB), ls] = acc
                return carry

            lax.fori_loop(0, tm // CONV_RB, rows_body, 0)
        y, _, _ = _ln_fwd(u1_ref[...], g_ref[...], b_ref[...])
        u2_ref[...] = (y * _sig(y)).astype(ACT)
        buf[0:HALO, :] = buf[tm:tm + HALO, :]

    return pl.pallas_call(
        body, name="conv_fwd", grid=(T // tm,),
        in_specs=[pl.BlockSpec((2, tm, D), lambda i: (2, i, 0)), pl.BlockSpec((HALO, D), lambda i: (0, 0)),
                  pl.BlockSpec((1, D), lambda i: (0, 0)), pl.BlockSpec((1, D), lambda i: (0, 0)),
                  pl.BlockSpec((1, D), lambda i: (0, 0))],
        out_specs=[pl.BlockSpec((tm, D), lambda i: (i, 0)), pl.BlockSpec((tm, D), lambda i: (i, 0))],
        out_shape=[jax.ShapeDtypeStruct((T, D), F32), jax.ShapeDtypeStruct((T, D), ACT)],
        scratch_shapes=[pltpu.VMEM((HALO + tm, D), F32), pltpu.VMEM((7, n, LANES), F32)],
        compiler_params=_cparams(1),
    )(p, cw, cb_, lng, lnb)


def _conv_bwd(p, u1, du2, cw, lng, lnb, dp):
    T = p.shape[1]
    tm = min(512, T)
    ni = T // tm
    hb = tm // HALO

    n = HALO + tm - 8

    def body(p_ref, ph_ref, u1_ref, du2_ref, cw_ref, g_ref, b_ref, dp_in, dp_ref, dcw_ref, sm_ref, ubuf, dbuf,
             sh, dacc):
        del dp_in
        step = pl.program_id(0)

        @pl.when(step == 0)
        def _():
            dbuf[tm:tm + HALO, :] = jnp.zeros((HALO, D), F32)
            dcw_ref[...] = jnp.zeros_like(dcw_ref)
            sm_ref[...] = jnp.zeros_like(sm_ref)

        ua = p_ref[0]
        sgb = _sig(p_ref[1])
        halo = ph_ref[0] * _sig(ph_ref[1])
        ubuf[0:HALO, :] = jnp.where(step == ni - 1, 0.0, halo)
        ubuf[HALO:HALO + tm, :] = ua * sgb
        g = g_ref[...]
        y, xh, rs = _ln_fwd(u1_ref[...], g, b_ref[...])
        sy = _sig(y)
        dy = du2_ref[...] * sy * (1.0 + y * (1.0 - sy))
        sm_ref[1:2, :] += _colsum(dy * xh)
        sm_ref[2:3, :] += _colsum(dy)
        dxh = dy * g
        du1 = rs * (dxh - _rowmean(dxh) - xh * _rowmean(dxh * xh))
        sm_ref[0:1, :] += _colsum(du1)
        dbuf[0:tm, :] = du1
        for lb in range(D // LANES):
            ls = slice(lb * LANES, (lb + 1) * LANES)
            taps = [cw_ref[j:j + 1, ls] for j in range(CONV_K)]
            _shift_rows(dbuf, sh, ls, n)

            def du0_body(rb, carry):
                r0 = pl.multiple_of(rb * CONV_RB, CONV_RB)
                acc = jnp.zeros((CONV_RB, LANES), F32)
                for j in range(CONV_K):
                    acc = acc + taps[j] * _tap(dbuf, sh, ls, CONV_K - 1 - j, r0, CONV_RB)
                dp_ref[0, pl.ds(r0, CONV_RB), ls] = acc.astype(ACT)
                return carry

            lax.fori_loop(0, tm // CONV_RB, du0_body, 0)
            _shift_rows(ubuf, sh, ls, n)
            dacc[...] = jnp.zeros_like(dacc)

            def dcw_body(rb, carry):
                r0 = pl.multiple_of(rb * CONV_RB, CONV_RB)
                d = dbuf[pl.ds(r0, CONV_RB), ls]
                for j in range(CONV_K):
                    prod = d * _tap(ubuf, sh, ls, HALO - (CONV_K - 1) + j, r0, CONV_RB)
                    dacc[8 * j:8 * j + 8, :] += jnp.sum(prod.reshape(CONV_RB // 8, 8, LANES), axis=0)
                return carry

            lax.fori_loop(0, tm // CONV_RB, dcw_body, 0)
            for j in range(CONV_K):
                dcw_ref[j:j + 1, ls] += _colsum(dacc[8 * j:8 * j + 8, :])
        du0 = dp_ref[0].astype(F32)
        dp_ref[0] = (du0 * sgb).astype(ACT)
        dp_ref[1] = (du0 * ua * sgb * (1.0 - sgb)).astype(ACT)
        dbuf[tm:tm + HALO, :] = dbuf[0:HALO, :]

    rev = lambda i: ni - 1 - i
    return pl.pallas_call(
        body, name="conv_bwd", grid=(ni,),
        in_specs=[pl.BlockSpec((2, tm, D), lambda i: (2, rev(i), 0)),
                  pl.BlockSpec((2, HALO, D), lambda i: (2, jnp.maximum(rev(i) * hb - 1, 0), 0)),
                  pl.BlockSpec((tm, D), lambda i: (rev(i), 0)), pl.BlockSpec((tm, D), lambda i: (rev(i), 0)),
                  pl.BlockSpec((HALO, D), lambda i: (0, 0)), pl.BlockSpec((1, D), lambda i: (0, 0)),
                  pl.BlockSpec((1, D), lambda i: (0, 0)), pl.BlockSpec(memory_space=pl.ANY)],
        out_specs=[pl.BlockSpec((2, tm, D), lambda i: (2, rev(i), 0)),
                   pl.BlockSpec((HALO, D), lambda i: (0, 0)), pl.BlockSpec((8, D), lambda i: (0, 0))],
        out_shape=[jax.ShapeDtypeStruct(dp.shape, dp.dtype), jax.ShapeDtypeStruct((HALO, D), F32),
                   jax.ShapeDtypeStruct((8, D), F32)],
        input_output_aliases={7: 0},
        scratch_shapes=[pltpu.VMEM((HALO + tm, D), F32), pltpu.VMEM((tm + HALO, D), F32),
                        pltpu.VMEM((7, n, LANES), F32), pltpu.VMEM((8 * CONV_K, LANES), F32)],
        compiler_params=_cparams(1),
    )(p, p, u1, du2, cw, lng, lnb, dp)


def _mixout_fwd(x, oa, u2, p, mod, mo, w_a, w_b, w_o):
    T = x.shape[0]
    tm = min(512, T)

    def body(x_ref, oa_ref, u2_ref, p_ref, mod_ref, wa_ref, wb_ref, wo_ref, xo_ref, ya_ref, yb_ref, mo_ref):
        ya = _mm(oa_ref[...], wa_ref[...])
        yb = _mm(u2_ref[...], wb_ref[...])
        ya_ref[...] = ya.astype(ACT)
        yb_ref[...] = yb.astype(ACT)
        merged = _sig(p_ref[0]) * ya + _sig(p_ref[1]) * yb
        out = _mm(merged, wo_ref[...])
        mo_ref[...] = out
        xo_ref[...] = x_ref[...] + mod_ref[mo + 2:mo + 3, :] * out

    tile = pl.BlockSpec((tm, D), lambda i: (i, 0))
    wspec = pl.BlockSpec((D, D), lambda i: (0, 0))
    return pl.pallas_call(
        body, name="mixout_fwd", grid=(T // tm,),
        in_specs=[tile, tile, tile, pl.BlockSpec((2, tm, D), lambda i: (3, i, 0)),
                  pl.BlockSpec((9, D), lambda i: (0, 0)), wspec, wspec, wspec],
        out_specs=[tile, tile, tile, tile],
        out_shape=[jax.ShapeDtypeStruct((T, D), F32), jax.ShapeDtypeStruct((T, D), ACT),
                   jax.ShapeDtypeStruct((T, D), ACT), jax.ShapeDtypeStruct((T, D), F32)],
        compiler_params=_cparams(1),
    )(x, oa, u2, p, mod, w_a, w_b, w_o)


def _mixout_bwd(dxo, oa, u2, ya, yb, mout, p, mod, mo, w_a, w_b, w_o):
    T = dxo.shape[0]
    tm = min(256, T)

    def body(dxo_ref, oa_ref, u2_ref, ya_ref, yb_ref, mo_ref, p_ref, mod_ref, wa_ref, wb_ref, wo_ref,
             dp_ref, doa_ref, du2_ref, dwa_ref, dwb_ref, dwo_ref, sm_ref):
        @pl.when(pl.program_id(0) == 0)
        def _():
            dwa_ref[...] = jnp.zeros_like(dwa_ref)
            dwb_ref[...] = jnp.zeros_like(dwb_ref)
            dwo_ref[...] = jnp.zeros_like(dwo_ref)
            sm_ref[...] = jnp.zeros_like(sm_ref)

        dxo_v = dxo_ref[...]
        sm_ref[2:3, :] += _colsum(dxo_v * mo_ref[...])
        dmo = (mod_ref[mo + 2:mo + 3, :] * dxo_v).astype(MM)
        ya = ya_ref[...].astype(F32)
        yb = yb_ref[...].astype(F32)
        sga = _sig(p_ref[0])
        sgb = _sig(p_ref[1])
        merged = (sga * ya + sgb * yb).astype(MM)
        dwo_ref[...] += _mm_tn(merged, dmo)
        dmg = _mm_nt(dmo, wo_ref[...])
        dp_ref[0] = (dmg * ya * sga * (1.0 - sga)).astype(ACT)
        dp_ref[1] = (dmg * yb * sgb * (1.0 - sgb)).astype(ACT)
        dya = (dmg * sga).astype(MM)
        dyb = (dmg * sgb).astype(MM)
        dwa_ref[...] += _mm_tn(oa_ref[...], dya)
        dwb_ref[...] += _mm_tn(u2_ref[...], dyb)
        doa_ref[...] = _mm_nt(dya, wa_ref[...])
        du2_ref[...] = _mm_nt(dyb, wb_ref[...])

    tile = pl.BlockSpec((tm, D), lambda i: (i, 0))
    wspec = pl.BlockSpec((D, D), lambda i: (0, 0))
    return pl.pallas_call(
        body, name="mixout_bwd", grid=(T // tm,),
        in_specs=[tile, tile, tile, tile, tile, tile, pl.BlockSpec((2, tm, D), lambda i: (3, i, 0)),
                  pl.BlockSpec((9, D), lambda i: (0, 0)), wspec, wspec, wspec],
        out_specs=[pl.BlockSpec((2, tm, D), lambda i: (3, i, 0)), tile, tile, wspec, wspec, wspec,
                   pl.BlockSpec((8, D), lambda i: (0, 0))],
        out_shape=[jax.ShapeDtypeStruct((8, T, D), ACT), jax.ShapeDtypeStruct((T, D), F32),
                   jax.ShapeDtypeStruct((T, D), F32), jax.ShapeDtypeStruct((D, D), F32),
                   jax.ShapeDtypeStruct((D, D), F32), jax.ShapeDtypeStruct((D, D), F32),
                   jax.ShapeDtypeStruct((8, D), F32)],
        compiler_params=_cparams(1),
    )(dxo, oa, u2, ya, yb, mout, p, mod, w_a, w_b, w_o)


def _ada_wgrad(cs_all, dmod_cols):
    cs_t = jnp.pad(cs_all.T, ((0, 0), (0, HD - N_DEV)))
    dm = jnp.pad(dmod_cols, ((0, HD - N_DEV), (0, 0)))

    def body(cs_ref, d_ref, out_ref):
        out_ref[...] = jnp.dot(cs_ref[...], d_ref[...], preferred_element_type=F32,
                               precision=lax.Precision.HIGHEST)

    return pl.pallas_call(
        body, name="ada_wgrad", out_shape=jax.ShapeDtypeStruct((D, dmod_cols.shape[1]), F32),
        compiler_params=pltpu.CompilerParams(vmem_limit_bytes=VMEM_LIMIT),
    )(cs_t, dm)


def _adam_math(w, g, m, v):
    m2 = ADAM_B1 * m + (1.0 - ADAM_B1) * g
    v2 = ADAM_B2 * v + (1.0 - ADAM_B2) * (g * g)
    m_hat = m2 / (1.0 - ADAM_B1 ** ADAM_STEP)
    v_hat = v2 / (1.0 - ADAM_B2 ** ADAM_STEP)
    delta = -ADAM_LR * (m_hat / (jnp.sqrt(v_hat) + ADAM_EPS) + ADAM_WD * w)
    return delta, m2, v2


def _adamw(w, m, v, g, name):
    R, C = w.shape
    slots = g.ndim == 3
    n_slots = g.shape[0] if slots else 0
    tr = R
    for cand in (256, 176):
        if R % cand == 0 and R > cand:
            tr = cand
            break

    def body(w_ref, m_ref, v_ref, g_ref, go_ref, d_ref, mo_ref, vo_ref):
        if slots:
            gv = g_ref[0].astype(F32)
            for s in range(1, n_slots):
                gv = gv + g_ref[s].astype(F32)
        else:
            gv = g_ref[...]
        go_ref[...] = gv
        d_ref[...], mo_ref[...], vo_ref[...] = _adam_math(w_ref[...], gv, m_ref[...], v_ref[...])

    tile = pl.BlockSpec((tr, C), lambda i: (i, 0))
    gspec = pl.BlockSpec((n_slots, tr, C), lambda i: (0, i, 0)) if slots else tile
    sds = jax.ShapeDtypeStruct((R, C), F32)
    return pl.pallas_call(
        body, name=name, grid=(R // tr,), in_specs=[tile, tile, tile, gspec], out_specs=[tile] * 4,
        out_shape=[sds] * 4, compiler_params=_cparams(1),
    )(w, m, v, g)


def _sum_slots(pack, name, tr):
    n, R, C = pack.shape

    def body(p_ref, out_ref):
        acc = p_ref[0].astype(F32)
        for s in range(1, n):
            acc = acc + p_ref[s].astype(F32)
        out_ref[...] = acc

    return pl.pallas_call(
        body, name=name, grid=(R // tr,), in_specs=[pl.BlockSpec((n, tr, C), lambda i: (0, i, 0))],
        out_specs=pl.BlockSpec((tr, C), lambda i: (i, 0)), out_shape=jax.ShapeDtypeStruct((R, C), F32),
        compiler_params=_cparams(1))(pack)


def _me():
    return lax.axis_index("x"), lax.axis_index("y"), lax.axis_index("c")


def _peer(r):
    x, y, c = _me()
    px = 1 - x if r & 4 else x
    py = 1 - y if r & 2 else y
    pc = 1 - c if r & 1 else c
    return (px, py, pc), 4 * px + 2 * py + pc


def _small_gather(x_ref, out_ref, send_sems, recv_sems):
    R = x_ref.shape[0]
    mx, my, mc = _me()
    me = 4 * mx + 2 * my + mc
    mine = out_ref.at[pl.ds(pl.multiple_of(me * R, 8), R), :]
    copies = []
    for r in range(1, N_DEV):
        dev, _ = _peer(r)
        copies.append(pltpu.make_async_remote_copy(
            src_ref=x_ref, dst_ref=mine, send_sem=send_sems.at[r - 1], recv_sem=recv_sems.at[r - 1],
            device_id=dev, device_id_type=MESH))
    for cp in copies:
        cp.start()
    mine[...] = x_ref[...]
    for r in range(1, N_DEV):
        dev, idx = _peer(r)
        theirs = out_ref.at[pl.ds(pl.multiple_of(idx * R, 8), R), :]
        pltpu.make_async_remote_copy(
            src_ref=x_ref, dst_ref=theirs, send_sem=send_sems.at[r - 1], recv_sem=recv_sems.at[r - 1],
            device_id=dev, device_id_type=MESH).wait_recv()
    for cp in copies:
        cp.wait_send()


def _prologue(cs, ada_w, ada_b_cols, big):
    n = len(big)
    ncol = ada_w.shape[1]
    big_shape, big_sems = _xchg_specs(big, "gather")

    def body(cs_ref, w_ref, b_ref, *rest):
        big_in, cs_all, mod_all, big_out = rest[:n], rest[n], rest[n + 1], rest[n + 2:2 * n + 2]
        mod_scr, s1, r1, s2, r2 = rest[2 * n + 2:2 * n + 7]
        sems = rest[2 * n + 7:]
        _xchg_start(big_in, big_out, sems, "gather")
        _small_gather(cs_ref, cs_all, s1, r1)
        pick = (lax.broadcasted_iota(jnp.int32, (N_DEV, N_DEV * 8), 1)
                == 8 * lax.broadcasted_iota(jnp.int32, (N_DEV, N_DEV * 8), 0)).astype(F32)
        per_device = jnp.dot(pick, cs_all[...], preferred_element_type=F32, precision=lax.Precision.HIGHEST)
        mod_scr[...] = jnp.dot(per_device, w_ref[...], preferred_element_type=F32,
                               precision=lax.Precision.HIGHEST) + b_ref[...]
        _small_gather(mod_scr, mod_all, s2, r2)
        _xchg_wait(big_in, big_out, sems, "gather")

    vmem = pl.BlockSpec(memory_space=pltpu.VMEM)
    hbm = pl.BlockSpec(memory_space=pl.ANY)
    dma7 = pltpu.SemaphoreType.DMA((N_DEV - 1,))
    out = pl.pallas_call(
        body, name="prologue",
        out_shape=[jax.ShapeDtypeStruct((N_DEV * 8, D), F32), jax.ShapeDtypeStruct((N_DEV * 8, ncol), F32)]
        + big_shape,
        in_specs=[vmem, vmem, vmem] + [hbm] * n, out_specs=[vmem, vmem] + [hbm] * n,
        scratch_shapes=[pltpu.VMEM((8, ncol), F32), dma7, dma7, dma7, dma7] + big_sems,
        compiler_params=pltpu.CompilerParams(vmem_limit_bytes=VMEM_LIMIT),
    )(cs, ada_w, ada_b_cols, *big)
    return out[0], out[1], out[2:]


def _allgather_small(x):
    R, C = x.shape

    def body(x_ref, out_ref, send_sems, recv_sems):
        _small_gather(x_ref, out_ref, send_sems, recv_sems)

    return pl.pallas_call(
        body, name="allgather_small_%dx%d" % (R, C),
        out_shape=jax.ShapeDtypeStruct((N_DEV * R, C), F32),
        in_specs=[pl.BlockSpec(memory_space=pltpu.VMEM)], out_specs=pl.BlockSpec(memory_space=pltpu.VMEM),
        scratch_shapes=[pltpu.SemaphoreType.DMA((N_DEV - 1,)), pltpu.SemaphoreType.DMA((N_DEV - 1,))],
    )(x)


N_CHIP = N_DEV // 2


def _xchg_copies(ins, outs, sems, mode):
    send_sems, recv_sems, local_sems = sems
    mx, my, mc = _me()
    me = 4 * mx + 2 * my + mc
    my_chip = 2 * mx + my
    sibling = _peer(1)[0]

    def rdma(a, r, dev, src, slot):
        k = a * (N_DEV - 1) + r - 1
        return pltpu.make_async_remote_copy(
            src_ref=src, dst_ref=outs[a].at[slot], send_sem=send_sems.at[k], recv_sem=recv_sems.at[k],
            device_id=dev, device_id_type=MESH)

    own, sends, relays, recvs = [], [], [], []
    for a in range(len(ins)):
        if mode == "pair":
            for chip in range(N_CHIP):
                src = ins[a].at[2 * chip + 1 - mc]
                sends.append(rdma(a, chip + 1, sibling, src, chip))
                recvs.append(rdma(a, chip + 1, sibling, src, chip))
            continue
        if mode == "quad":
            own.append(pltpu.make_async_copy(ins[a].at[my_chip], outs[a].at[my_chip], local_sems.at[a]))
            for r in (2, 4, 6):
                dev, idx = _peer(r)
                chip = idx // 2
                sends.append(rdma(a, r, dev, ins[a].at[chip], my_chip))
                recvs.append(rdma(a, r, dev, ins[a].at[chip], chip))
            continue
        gather = mode == "gather"
        own.append(pltpu.make_async_copy(ins[a] if gather else ins[a].at[me], outs[a].at[me], local_sems.at[a]))
        for r in range(1, N_DEV):
            dev, idx = _peer(r)
            if not gather:
                sends.append(rdma(a, r, dev, ins[a].at[idx], me))
                recvs.append(rdma(a, r, dev, ins[a].at[idx], idx))
            elif r == 1:
                sends.append(rdma(a, r, dev, ins[a], me))
                recvs.append(rdma(a, r, dev, ins[a], idx))
            elif r % 2 == 0:
                sends.append(rdma(a, r, dev, ins[a], me))
                relays.append((rdma(a, r, dev, ins[a], idx), rdma(a, r + 1, sibling, outs[a].at[idx], idx)))
            else:
                recvs.append(rdma(a, r, sibling, ins[a], idx))
    return own, sends, relays, recvs


def _xchg_start(ins, outs, sems, mode):
    own, sends, _, _ = _xchg_copies(ins, outs, sems, mode)
    for cp in own + sends:
        cp.start()


def _xchg_wait(ins, outs, sems, mode):
    own, sends, relays, recvs = _xchg_copies(ins, outs, sems, mode)
    for arrival, relay in relays:
        arrival.wait_recv()
        relay.start()
    for cp in recvs:
        cp.wait_recv()
    for cp in own:
        cp.wait()
    for cp in sends + [relay for _, relay in relays]:
        cp.wait_send()


def _xchg_specs(arrays, mode):
    n = len(arrays)
    shape = {"gather": lambda s: (N_DEV,) + s, "scatter": lambda s: s, "pair": lambda s: (N_CHIP,) + s[1:],
             "quad": lambda s: s}[mode]
    out_shape = [jax.ShapeDtypeStruct(shape(a.shape), a.dtype) for a in arrays]
    sems = [pltpu.SemaphoreType.DMA((n * (N_DEV - 1),)), pltpu.SemaphoreType.DMA((n * (N_DEV - 1),)),
            pltpu.SemaphoreType.DMA((n,))]
    return out_shape, sems


def _exchange(arrays, mode, name):
    n = len(arrays)

    def body(*refs):
        _xchg_start(refs[:n], refs[n:2 * n], refs[2 * n:], mode)
        _xchg_wait(refs[:n], refs[n:2 * n], refs[2 * n:], mode)

    out_shape, sems = _xchg_specs(arrays, mode)
    return pl.pallas_call(
        body, name=name, out_shape=out_shape,
        in_specs=[pl.BlockSpec(memory_space=pl.ANY)] * n, out_specs=[pl.BlockSpec(memory_space=pl.ANY)] * n,
        scratch_shapes=sems,
    )(*arrays)


def _gridded(body, carry, *, name, grid, in_specs, out_specs, out_shape, scratch_shapes=(), aliases=None):
    if carry is None:
        return pl.pallas_call(
            body, name=name, grid=grid, in_specs=list(in_specs), out_specs=list(out_specs),
            out_shape=list(out_shape), scratch_shapes=list(scratch_shapes), input_output_aliases=aliases or {},
            compiler_params=_cparams(len(grid)))
    arrays, mode = carry
    n, n_in, n_out, n_scr = len(arrays), len(in_specs), len(out_specs), len(scratch_shapes)
    c_shape, c_sems = _xchg_specs(arrays, mode)

    def wrapped(*refs):
        ins, cin = refs[:n_in], refs[n_in:n_in + n]
        o0 = n_in + n
        outs, cout = refs[o0:o0 + n_out], refs[o0 + n_out:o0 + n_out + n]
        s0 = o0 + n_out + n
        scr, sems = refs[s0:s0 + n_scr], refs[s0 + n_scr:]
        first = pl.program_id(0) == 0
        last = pl.program_id(0) == grid[0] - 1
        for ax in range(1, len(grid)):
            first = first & (pl.program_id(ax) == 0)
            last = last & (pl.program_id(ax) == grid[ax] - 1)

        @pl.when(first)
        def _():
            _xchg_start(cin, cout, sems, mode)

        body(*ins, *outs, *scr)

        @pl.when(last)
        def _():
            _xchg_wait(cin, cout, sems, mode)

    hbm = pl.BlockSpec(memory_space=pl.ANY)
    res = pl.pallas_call(
        wrapped, name=name, grid=grid, in_specs=list(in_specs) + [hbm] * n, out_specs=list(out_specs) + [hbm] * n,
        out_shape=list(out_shape) + c_shape, scratch_shapes=list(scratch_shapes) + c_sems,
        input_output_aliases=aliases or {}, compiler_params=_cparams(len(grid)),
    )
    return lambda *args: res(*args, *arrays)


def _local_step(x, target, mod, small, sh, w1):
    w1_in, w1_out = w1[0].reshape(2, D_FF, D), w1[1].reshape(D_FF, D)
    (x1, a1, b1, f1, h1), (wm_in,) = _ffn_fwd(x, mod, 0, small["norm_ffn1"], w1_in, w1_out, 0.5, "ffn1_fwd",
                                              ([sh["mix_w_in"]], "gather"))
    (p, h2), (wh_o, wc_o, wm_o, cw) = _mixin_fwd(
        x1, mod, 3, small["norm_mix"], wm_in,
        ([sh["hgrn_w_o"], sh["conv_w_o"], sh["mix_w_out"], sh["conv_w"]], "gather"))
    wh_o, wc_o, wm_o = wh_o.reshape(D, D), wc_o.reshape(D, D), wm_o.reshape(D, D)
    cw = jnp.pad(cw.transpose(1, 0, 2).reshape(CONV_K, D), ((0, HALO - CONV_K), (0, 0)))
    (o, oa, a_all, s_all), (w2_in, w2_out) = _hgrn_fwd(p, small["hgrn_lb"], small["hgrn_g"],
                                                       ([sh["ffn2_w_in"], sh["ffn2_w_out"]], "gather"))
    w2_in, w2_out = w2_in.reshape(2, D_FF, D), w2_out.reshape(D_FF, D)
    u1, u2 = _conv_fwd(p, cw, small["conv_b"], small["conv_ln_g"], small["conv_ln_b"])
    x2, ya, yb, mout = _mixout_fwd(x1, oa, u2, p, mod, 3, wh_o, wc_o, wm_o)
    (x3, a3, b3, f3, h3), _ = _ffn_fwd(x2, mod, 6, small["norm_ffn2"], w2_in, w2_out, 0.5, "ffn2_fwd", None)
    dx3, df3, sm_head = _head(x3, target, small["norm_final"], mod, 8, 0.5)

    (da3, db3, dw2_in, dw2_out), _ = _ffn_bwd_w(h3, df3, a3, b3, w2_out, "ffn2_bwd_w", None)
    (dx2, sm3), _ = _ffn_bwd_x(x2, dx3, f3, da3, db3, mod, 6, small["norm_ffn2"], w2_in, 0.5, "ffn2_bwd_x", None)
    dp, doa, du2, dwh_o, dwc_o, dwm_o, sm_mo = _mixout_bwd(dx2, oa, u2, ya, yb, mout, p, mod, 3, wh_o, wc_o, wm_o)
    dp, dcw, sm_cv = _conv_bwd(p, u1, du2, cw, small["conv_ln_g"], small["conv_ln_b"], dp)
    rows = lambda t: t.reshape(N_DEV, -1, D).astype(MM)
    (dp, sm_hg), (r2_in, r2_out) = _hgrn_bwd(p, o, a_all, s_all, doa, small["hgrn_lb"], small["hgrn_g"], dp,
                                             ([rows(dw2_in), rows(dw2_out)], "scatter"))
    (dx1, dwm_in, sm2, df1), (rh_o, rc_o, rm_o, rcw) = _mixin_bwd(
        x1, h2, dx2, dp, mod, 3, small["norm_mix"], wm_in, 2, 0.5,
        ([rows(dwh_o), rows(dwc_o), rows(dwm_o), dcw[:CONV_K].reshape(CONV_K, N_DEV, -1).transpose(1, 0, 2)],
         "scatter"))
    (da1, db1, dw1_in, dw1_out), (rm_in,) = _ffn_bwd_w(h1, df1, a1, b1, w1_out, "ffn1_bwd_w",
                                                      (_pair_reduce([dwm_in], "pair_mix"), "quad"))
    (dx0, sm1), (r1_in, r1_out) = _ffn_bwd_x(
        x, dx1, f1, da1, db1, mod, 0, small["norm_ffn1"], w1_in, 0.5, "ffn1_bwd_x",
        (_pair_reduce([rows(dw1_in), rows(dw1_out)], "pair_ffn1"), "quad"))

    dmod = jnp.concatenate([sm1[0:3], sm2[0:2], sm_mo[2:3], sm3[0:3]], axis=0)
    gsmall = dict(norm_ffn1=sm1[3:4], norm_mix=sm2[3:4], lb0=sm_hg[0:1], hgrn_g=sm_hg[1:2], conv_b=sm_cv[0:1],
                  conv_ln_g=sm_cv[1:2], conv_ln_b=sm_cv[2:3], norm_ffn2=sm3[3:4], norm_final=sm_head[0:1])
    recv = dict(ffn1_w_in=r1_in, ffn1_w_out=r1_out, mix_w_in=rm_in, hgrn_w_o=rh_o, conv_w=rcw, conv_w_o=rc_o,
                mix_w_out=rm_o, ffn2_w_in=r2_in, ffn2_w_out=r2_out)
    return sm_head[1, 0], dx0, dmod, gsmall, recv


def _pair_add(mine, theirs, core, name):
    _, R, C = theirs.shape

    def body(core_ref, a_ref, b_ref, out_ref):
        del core_ref
        out_ref[0] = (a_ref[0, 0].astype(F32) + b_ref[0].astype(F32)).astype(out_ref.dtype)

    blk = pl.BlockSpec((1, R, C), lambda s, core_ref: (s, 0, 0))
    grid_spec = pltpu.PrefetchScalarGridSpec(
        num_scalar_prefetch=1, grid=(N_CHIP,),
        in_specs=[pl.BlockSpec((1, 1, R, C), lambda s, core_ref: (s, core_ref[0], 0, 0)), blk], out_specs=blk)
    return pl.pallas_call(body, name=name, grid_spec=grid_spec,
                          out_shape=jax.ShapeDtypeStruct(theirs.shape, mine.dtype), compiler_params=_cparams(1),
                          )(core, mine.reshape(N_CHIP, 2, R, C), theirs)


def _pair_reduce(arrays, name):
    theirs = _exchange(arrays, "pair", name)
    core = lax.axis_index("c").astype(jnp.int32).reshape(1)
    return [_pair_add(a, t, core, "%s_add%d" % (name, i)) for i, (a, t) in enumerate(zip(arrays, theirs))]


SMALL_ORDER = ("norm_ffn1", "norm_mix", "lb0", "hgrn_g", "conv_b", "conv_ln_g", "conv_ln_b", "norm_ffn2",
               "norm_final")
PACK_ROWS = 24


def kernel(x, c, ada_w, ada_b, norm_ffn1, ffn1_w_in, ffn1_w_out, norm_mix, mix_w_in, hgrn_lb, hgrn_g, hgrn_w_o, conv_w, conv_b, conv_ln_g, conv_ln_b, conv_w_o, mix_w_out, norm_ffn2, ffn2_w_in, ffn2_w_out, norm_final, loss_target, m_ada_w, m_ada_b, m_norm_ffn1, m_ffn1_w_in, m_ffn1_w_out, m_norm_mix, m_mix_w_in, m_hgrn_lb, m_hgrn_g, m_hgrn_w_o, m_conv_w, m_conv_b, m_conv_ln_g, m_conv_ln_b, m_conv_w_o, m_mix_w_out, m_norm_ffn2, m_ffn2_w_in, m_ffn2_w_out, m_norm_final, v_ada_w, v_ada_b, v_norm_ffn1, v_ffn1_w_in, v_ffn1_w_out, v_norm_mix, v_mix_w_in, v_hgrn_lb, v_hgrn_g, v_hgrn_w_o, v_conv_w, v_conv_b, v_conv_ln_g, v_conv_ln_b, v_conv_w_o, v_mix_w_out, v_norm_ffn2, v_ffn2_w_in, v_ffn2_w_out, v_norm_final):
    mx, my, mc = _me()
    me = 4 * mx + 2 * my + mc
    ncol = ada_w.shape[2]

    sh = dict(ffn1_w_out=ffn1_w_out, mix_w_in=mix_w_in, hgrn_w_o=hgrn_w_o, conv_w_o=conv_w_o,
              mix_w_out=mix_w_out, ffn2_w_out=ffn2_w_out)
    sh = {n: w[0].astype(MM) for n, w in sh.items()}
    sh["ffn1_w_in"] = ffn1_w_in[0].T.astype(MM)
    sh["ffn2_w_in"] = ffn2_w_in[0].T.astype(MM)
    sh["conv_w"] = conv_w[0]
    small = dict(norm_ffn1=norm_ffn1, norm_mix=norm_mix, hgrn_lb=hgrn_lb, hgrn_g=hgrn_g, conv_b=conv_b,
                 conv_ln_g=conv_ln_g, conv_ln_b=conv_ln_b, norm_ffn2=norm_ffn2, norm_final=norm_final.reshape(1, D))

    cs = jnp.broadcast_to(c * jax.nn.sigmoid(c), (8, D))
    ada_b_cols = lax.dynamic_slice(ada_b, (0, me * ncol), (1, ncol))
    cs_all, mod_all, w1 = _prologue(cs, ada_w[0], ada_b_cols, [sh["ffn1_w_in"], sh["ffn1_w_out"]])
    cs_all = cs_all.reshape(N_DEV, 8, D)[:, 0, :]
    mod = lax.dynamic_index_in_dim(mod_all.reshape(N_DEV, N_DEV, ncol), me, axis=1, keepdims=False).reshape(9, D)

    loss_local, dx, dmod, gsmall, recv = _local_step(x[0], loss_target[0], mod, small, sh, w1)
    loss = lax.psum(loss_local, ("x", "y", "c"))

    pack = jnp.concatenate([dmod] + [gsmall[n] for n in SMALL_ORDER]
                           + [jnp.zeros((PACK_ROWS - 9 - len(SMALL_ORDER), D), F32)], axis=0)
    pack_all = _allgather_small(pack).reshape(N_DEV, PACK_ROWS, D)
    tot = _sum_slots(pack_all, "sum_small", PACK_ROWS)
    gs = {n: tot[9 + i:10 + i] for i, n in enumerate(SMALL_ORDER)}
    dmod_all = pack_all[:, 0:9, :].reshape(N_DEV, 9 * D)
    g_ada_b = tot[0:9].reshape(1, 9 * D)
    g_ada_w = _ada_wgrad(cs_all, lax.dynamic_slice(dmod_all, (0, me * ncol), (N_DEV, ncol)))
    z = hgrn_lb.astype(F32)
    p0 = jax.nn.sigmoid(z[0:1] - z[1:2])
    dz0 = p0 * (1.0 - p0) * gs["lb0"]
    g_hgrn_lb = jnp.concatenate([dz0, -dz0], axis=0)

    res = {}
    res["ada_w"] = _adamw(ada_w[0], m_ada_w[0], v_ada_w[0], g_ada_w, "adamw_ada_w")
    big = dict(ffn1_w_in=(ffn1_w_in, m_ffn1_w_in, v_ffn1_w_in), ffn1_w_out=(ffn1_w_out, m_ffn1_w_out, v_ffn1_w_out),
               mix_w_in=(mix_w_in, m_mix_w_in, v_mix_w_in), hgrn_w_o=(hgrn_w_o, m_hgrn_w_o, v_hgrn_w_o),
               conv_w=(conv_w, m_conv_w, v_conv_w), conv_w_o=(conv_w_o, m_conv_w_o, v_conv_w_o),
               mix_w_out=(mix_w_out, m_mix_w_out, v_mix_w_out), ffn2_w_in=(ffn2_w_in, m_ffn2_w_in, v_ffn2_w_in),
               ffn2_w_out=(ffn2_w_out, m_ffn2_w_out, v_ffn2_w_out))
    for n, (w, m, v) in big.items():
        g = recv[n]
        if n in ("ffn1_w_in", "ffn2_w_in"):
            g = _sum_slots(g, "sum_" + n, g.shape[1] // 4).T
        res[n] = _adamw(w[0], m[0], v[0], g, "adamw_" + n)
    sm_names = ("ada_b", "norm_ffn1", "norm_mix", "hgrn_lb", "hgrn_g", "conv_b", "conv_ln_g", "conv_ln_b",
                "norm_ffn2", "norm_final")
    sm_w = dict(ada_b=(ada_b, m_ada_b, v_ada_b), norm_ffn1=(norm_ffn1, m_norm_ffn1, v_norm_ffn1),
                norm_mix=(norm_mix, m_norm_mix, v_norm_mix), hgrn_lb=(hgrn_lb, m_hgrn_lb, v_hgrn_lb),
                hgrn_g=(hgrn_g, m_hgrn_g, v_hgrn_g), conv_b=(conv_b, m_conv_b, v_conv_b),
                conv_ln_g=(conv_ln_g, m_conv_ln_g, v_conv_ln_g), conv_ln_b=(conv_ln_b, m_conv_ln_b, v_conv_ln_b),
                norm_ffn2=(norm_ffn2, m_norm_ffn2, v_norm_ffn2), norm_final=(norm_final, m_norm_final, v_norm_final))
    sm_g = dict(gs, ada_b=g_ada_b, hgrn_lb=g_hgrn_lb)
    rows = {n: sm_w[n][0].size // D for n in sm_names}
    n_rows = sum(rows.values())
    pad = (-n_rows) % 8
    stack = lambda parts: jnp.concatenate([q.reshape(-1, D) for q in parts] + [jnp.ones((pad, D), F32)], axis=0)
    st = _adamw(stack([sm_w[n][0] for n in sm_names]), stack([sm_w[n][1] for n in sm_names]),
                stack([sm_w[n][2] for n in sm_names]), stack([sm_g[n] for n in sm_names]), "adamw_small")
    off = 0
    for n in sm_names:
        res[n] = tuple(t[off:off + rows[n]].reshape(sm_w[n][0].shape) for t in st)
        off += rows[n]

    order = ("ada_w", "ada_b", "norm_ffn1", "ffn1_w_in", "ffn1_w_out", "norm_mix", "mix_w_in", "hgrn_lb", "hgrn_g",
             "hgrn_w_o", "conv_w", "conv_b", "conv_ln_g", "conv_ln_b", "conv_w_o", "mix_w_out", "norm_ffn2",
             "ffn2_w_in", "ffn2_w_out", "norm_final")
    lead = lambda n, t: t[None] if n in big or n == "ada_w" else t
    outs = [loss, dx[None]]
    for j in range(4):
        outs += [lead(n, res[n][j]) for n in order]
    return tuple(outs)
```

```python
import jax
import jax.numpy as jnp
from jax import lax
from jax.experimental import pallas as pl
from jax.experimental.pallas import tpu as pltpu

F32 = jnp.float32
MM = jnp.bfloat16
ACT = jnp.bfloat16

D = 1024
D_FF = 2816
HEADS = 8
HD = 128
CHUNK = 64
SUB = 16
NSUB = CHUNK // SUB
HGRN_BLOCK = 1024
DIAG_SAFE_EXP = 60.0
CONV_K = 31
HALO = 32
EPS = 1e-6
N_DEV = 8
NEG = -1e30
Q_SCALE = HD ** -0.5

ADAM_LR = 0.001
ADAM_B1 = 0.9
ADAM_B2 = 0.999
ADAM_EPS = 1e-08
ADAM_WD = 0.01
ADAM_STEP = 10

V7X_VMEM_BYTES = 64 * 1024 * 1024
VMEM_LIMIT = V7X_VMEM_BYTES - 4 * 1024 * 1024
MESH = pl.DeviceIdType.MESH


def _cparams(n_axes):
    return pltpu.CompilerParams(dimension_semantics=("arbitrary",) * n_axes, vmem_limit_bytes=VMEM_LIMIT)


def _mm(a, b):
    return lax.dot_general(a.astype(MM), b.astype(MM), (((1,), (0,)), ((), ())), preferred_element_type=F32)


def _mm_nt(a, b):
    return lax.dot_general(a.astype(MM), b.astype(MM), (((1,), (1,)), ((), ())), preferred_element_type=F32)


def _mm_tn(a, b):
    return lax.dot_general(a.astype(MM), b.astype(MM), (((0,), (0,)), ((), ())), preferred_element_type=F32)


def _sig(x):
    return 1.0 / (1.0 + jnp.exp(-x))


def _colsum(x):
    return jnp.sum(x, axis=0, keepdims=True)


def _rowmean(x):
    return jnp.mean(x, axis=-1, keepdims=True)


def _modnorm_fwd(xv, g, sh, sc):
    r = lax.rsqrt(_rowmean(xv * xv) + EPS)
    xh = xv * r
    n = xh * g
    return n * (1.0 + sc) + sh, xh, n, r


def _modnorm_bwd(dh, xh, n, r, g, sc):
    dsc = _colsum(dh * n)
    dsh = _colsum(dh)
    dn = dh * (1.0 + sc)
    dg = _colsum(dn * xh)
    dxh = dn * g
    dx = r * (dxh - xh * _rowmean(dxh * xh))
    return dx, dsh, dsc, dg


def _ffn_fwd(x, mod, mo, gnorm, w_in_t, w_out, res, name, carry):
    T = x.shape[0]
    tm = min(512, T)
    tn = D_FF // 2

    def body(x_ref, mod_ref, g_ref, wi_ref, wo_ref, xo_ref, a_ref, b_ref, f_ref, h_ref):
        xv = x_ref[...]
        h, _, _, _ = _modnorm_fwd(xv, g_ref[...], mod_ref[mo:mo + 1, :], mod_ref[mo + 1:mo + 2, :])
        h = h.astype(ACT)
        h_ref[...] = h
        f = None
        for c0 in range(0, D_FF, tn):
            a = _mm_nt(h, wi_ref[0, c0:c0 + tn, :])
            b = _mm_nt(h, wi_ref[1, c0:c0 + tn, :])
            a_ref[:, c0:c0 + tn] = a.astype(ACT)
            b_ref[:, c0:c0 + tn] = b.astype(ACT)
            part = _mm(a * _sig(a) * b, wo_ref[c0:c0 + tn, :])
            f = part if f is None else f + part
        f_ref[...] = f
        xo_ref[...] = xv + res * mod_ref[mo + 2:mo + 3, :] * f

    tile = pl.BlockSpec((tm, D), lambda i: (i, 0))
    wide = pl.BlockSpec((tm, D_FF), lambda i: (i, 0))
    out = _gridded(
        body, carry, name=name, grid=(T // tm,),
        in_specs=[
            tile,
            pl.BlockSpec((9, D), lambda i: (0, 0)),
            pl.BlockSpec((1, D), lambda i: (0, 0)),
            pl.BlockSpec((2, D_FF, D), lambda i: (0, 0, 0), pipeline_mode=pl.Buffered(1)),
            pl.BlockSpec((D_FF, D), lambda i: (0, 0), pipeline_mode=pl.Buffered(1)),
        ],
        out_specs=[tile, wide, wide, tile, tile],
        out_shape=[
            jax.ShapeDtypeStruct((T, D), F32),
            jax.ShapeDtypeStruct((T, D_FF), ACT),
            jax.ShapeDtypeStruct((T, D_FF), ACT),
            jax.ShapeDtypeStruct((T, D), F32),
            jax.ShapeDtypeStruct((T, D), ACT),
        ],
    )(x, mod, gnorm, w_in_t, w_out)
    return out[:5], out[5:]


def _ffn_bwd_w(h, df, a, b, w_out, name, carry):
    T = h.shape[0]
    tm = min(2048, T)
    ni = T // tm
    tn = 256
    nj = D_FF // tn

    def body(h_ref, df_ref, a_ref, b_ref, wo_ref, da_ref, db_ref, dwi_ref, dwo_ref, acc_i, acc_o):
        i = pl.program_id(1)

        @pl.when(i == 0)
        def _():
            acc_i[...] = jnp.zeros_like(acc_i)
            acc_o[...] = jnp.zeros_like(acc_o)

        hb = h_ref[...]
        df = df_ref[...]
        av = a_ref[...].astype(F32)
        bv = b_ref[...].astype(F32)
        sg = _sig(av)
        sa = av * sg
        s = (sa * bv).astype(MM)
        ds = _mm_nt(df, wo_ref[...])
        da = (ds * bv * sg * (1.0 + av * (1.0 - sg))).astype(MM)
        db = (ds * sa).astype(MM)
        da_ref[...] = da
        db_ref[...] = db
        acc_o[...] += _mm_tn(s, df)
        acc_i[0] += _mm_tn(da, hb)
        acc_i[1] += _mm_tn(db, hb)

        @pl.when(i == ni - 1)
        def _():
            dwi_ref[...] = acc_i[...].astype(MM)
            dwo_ref[...] = acc_o[...].astype(MM)

    out = _gridded(
        body, carry, name=name, grid=(nj, ni),
        in_specs=[
            pl.BlockSpec((tm, D), lambda j, i: (i, 0)),
            pl.BlockSpec((tm, D), lambda j, i: (i, 0)),
            pl.BlockSpec((tm, tn), lambda j, i: (i, j)),
            pl.BlockSpec((tm, tn), lambda j, i: (i, j)),
            pl.BlockSpec((tn, D), lambda j, i: (j, 0)),
        ],
        out_specs=[
            pl.BlockSpec((tm, tn), lambda j, i: (i, j)),
            pl.BlockSpec((tm, tn), lambda j, i: (i, j)),
            pl.BlockSpec((2, tn, D), lambda j, i: (0, j, 0)),
            pl.BlockSpec((tn, D), lambda j, i: (j, 0)),
        ],
        out_shape=[
            jax.ShapeDtypeStruct((T, D_FF), MM),
            jax.ShapeDtypeStruct((T, D_FF), MM),
            jax.ShapeDtypeStruct((2, D_FF, D), MM),
            jax.ShapeDtypeStruct((D_FF, D), MM),
        ],
        scratch_shapes=[pltpu.VMEM((2, tn, D), F32), pltpu.VMEM((tn, D), F32)],
    )(h, df, a, b, w_out)
    return out[:4], out[4:]


def _ffn_bwd_x(x, dxo, f, da, db, mod, mo, gnorm, w_in_t, res, name, carry):
    T = x.shape[0]
    tm = min(512, T)
    ni = T // tm
    tn = D_FF // 2
    nj = D_FF // tn

    def body(x_ref, dxo_ref, f_ref, da_ref, db_ref, mod_ref, g_ref, wi_ref, dx_ref, sm_ref, dh_scr):
        j = pl.program_id(0)
        i = pl.program_id(1)

        @pl.when((j == 0) & (i == 0))
        def _():
            sm_ref[...] = jnp.zeros_like(sm_ref)

        @pl.when(j == 0)
        def _():
            dh_scr[i] = jnp.zeros((tm, D), F32)

        dh_scr[i] += _mm(da_ref[...], wi_ref[0]) + _mm(db_ref[...], wi_ref[1])

        @pl.when(j == nj - 1)
        def _():
            sc = mod_ref[mo + 1:mo + 2, :]
            _, xh, n, r = _modnorm_fwd(x_ref[...], g_ref[...], mod_ref[mo:mo + 1, :], sc)
            dxn, dsh, dsc, dg = _modnorm_bwd(dh_scr[i], xh, n, r, g_ref[...], sc)
            dxo_v = dxo_ref[...]
            dx_ref[...] = dxo_v + dxn
            sm_ref[0:1, :] += dsh
            sm_ref[1:2, :] += dsc
            sm_ref[2:3, :] += _colsum(dxo_v * f_ref[...]) * res
            sm_ref[3:4, :] += dg

    last = pl.BlockSpec((tm, D), lambda j, i: (jnp.where(j == nj - 1, i, 0), 0))
    out = _gridded(
        body, carry, name=name, grid=(nj, ni),
        in_specs=[last, last, last,
                  pl.BlockSpec((tm, tn), lambda j, i: (i, j)), pl.BlockSpec((tm, tn), lambda j, i: (i, j)),
                  pl.BlockSpec((9, D), lambda j, i: (0, 0)), pl.BlockSpec((1, D), lambda j, i: (0, 0)),
                  pl.BlockSpec((2, tn, D), lambda j, i: (0, j, 0))],
        out_specs=[last, pl.BlockSpec((8, D), lambda j, i: (0, 0))],
        out_shape=[jax.ShapeDtypeStruct((T, D), F32), jax.ShapeDtypeStruct((8, D), F32)],
        scratch_shapes=[pltpu.VMEM((ni, tm, D), F32)],
    )(x, dxo, f, da, db, mod, gnorm, w_in_t)
    return out[:2], out[2:]


def _head(x, target, gfin, mod, gate_row, res):
    T = x.shape[0]
    tm = min(512, T)
    ni = T // tm

    def body(x_ref, t_ref, g_ref, mod_ref, dx_ref, df_ref, sm_ref):
        i = pl.program_id(0)

        @pl.when(i == 0)
        def _():
            sm_ref[...] = jnp.zeros_like(sm_ref)

        xv = x_ref[...]
        g = g_ref[...]
        r = lax.rsqrt(_rowmean(xv * xv) + EPS)
        xh = xv * r
        e = xh * g - t_ref[...]
        sm_ref[1:2, :] += _colsum(e * e) * (0.5 / D)
        dy = e * (1.0 / D)
        sm_ref[0:1, :] += _colsum(dy * xh)
        dxh = dy * g
        dx = r * (dxh - xh * _rowmean(dxh * xh))
        dx_ref[...] = dx
        df_ref[...] = (res * mod_ref[gate_row:gate_row + 1, :] * dx).astype(MM)

        @pl.when(i == ni - 1)
        def _():
            sm_ref[1:2, :] = jnp.broadcast_to(jnp.sum(sm_ref[1:2, :], axis=-1, keepdims=True), (1, D))

    tile = pl.BlockSpec((tm, D), lambda i: (i, 0))
    return pl.pallas_call(
        body, name="head_loss", grid=(ni,),
        in_specs=[tile, tile, pl.BlockSpec((1, D), lambda i: (0, 0)), pl.BlockSpec((9, D), lambda i: (0, 0))],
        out_specs=[tile, tile, pl.BlockSpec((8, D), lambda i: (0, 0))],
        out_shape=[jax.ShapeDtypeStruct((T, D), F32), jax.ShapeDtypeStruct((T, D), MM),
                   jax.ShapeDtypeStruct((8, D), F32)],
        compiler_params=_cparams(1),
    )(x, target, gfin, mod)


def _mixin_fwd(x, mod, mo, gnorm, w, carry):
    T = x.shape[0]
    tm = min(1024, T)
    ni = T // tm

    def body(x_ref, mod_ref, g_ref, w_ref, p_ref, h_ref, h_all):
        i = pl.program_id(1)

        @pl.when(pl.program_id(0) == 0)
        def _():
            h, _, _, _ = _modnorm_fwd(x_ref[...], g_ref[...], mod_ref[mo:mo + 1, :], mod_ref[mo + 1:mo + 2, :])
            h_all[i] = h.astype(ACT)
            h_ref[...] = h.astype(ACT)

        p_ref[0] = _mm(h_all[i], w_ref[0])

    first = lambda k, i: (jnp.where(k == 0, i, ni - 1), 0)
    out = _gridded(
        body, carry, name="mixin_fwd", grid=(8, ni),
        in_specs=[pl.BlockSpec((tm, D), first), pl.BlockSpec((9, D), lambda k, i: (0, 0)),
                  pl.BlockSpec((1, D), lambda k, i: (0, 0)), pl.BlockSpec((1, D, D), lambda k, i: (k, 0, 0))],
        out_specs=[pl.BlockSpec((1, tm, D), lambda k, i: (k, i, 0)), pl.BlockSpec((tm, D), first)],
        out_shape=[jax.ShapeDtypeStruct((8, T, D), F32), jax.ShapeDtypeStruct((T, D), ACT)],
        scratch_shapes=[pltpu.VMEM((ni, tm, D), ACT)],
    )(x, mod, gnorm, w)
    return out[:2], out[2:]


def _mixin_bwd(x, h, dxo, dp, mod, mo, gnorm, w, next_gate, next_res, carry):
    T = x.shape[0]
    tm = min(512, T)
    ni = T // tm

    def body(x_ref, h_ref, dxo_ref, dp_ref, mod_ref, g_ref, w_ref, dx_ref, dw_ref, sm_ref, df_ref, dh_scr, acc):
        k = pl.program_id(0)
        i = pl.program_id(1)

        @pl.when(i == 0)
        def _():
            acc[...] = jnp.zeros_like(acc)

        @pl.when(k == 0)
        def _():
            dh_scr[i] = jnp.zeros((tm, D), F32)

        @pl.when((k == 0) & (i == 0))
        def _():
            sm_ref[...] = jnp.zeros_like(sm_ref)

        dpk = dp_ref[0].astype(MM)
        acc[...] += _mm_tn(h_ref[...], dpk)
        dh_scr[i] += _mm_nt(dpk, w_ref[0])

        @pl.when(i == ni - 1)
        def _():
            dw_ref[0] = acc[...].astype(MM)

        @pl.when(k == 7)
        def _():
            sc = mod_ref[mo + 1:mo + 2, :]
            _, xh, n, r = _modnorm_fwd(x_ref[...], g_ref[...], mod_ref[mo:mo + 1, :], sc)
            dxn, dsh, dsc, dg = _modnorm_bwd(dh_scr[i], xh, n, r, g_ref[...], sc)
            dx = dxo_ref[...] + dxn
            dx_ref[...] = dx
            df_ref[...] = (next_res * mod_ref[next_gate:next_gate + 1, :] * dx).astype(MM)
            sm_ref[0:1, :] += dsh
            sm_ref[1:2, :] += dsc
            sm_ref[3:4, :] += dg

    last = pl.BlockSpec((tm, D), lambda k, i: (jnp.where(k == 7, i, 0), 0))
    out = _gridded(
        body, carry, name="mixin_bwd", grid=(8, ni),
        in_specs=[pl.BlockSpec((tm, D), lambda k, i: (jnp.where(k == 7, i, 0), 0)),
                  pl.BlockSpec((tm, D), lambda k, i: (i, 0)),
                  pl.BlockSpec((tm, D), lambda k, i: (jnp.where(k == 7, i, 0), 0)),
                  pl.BlockSpec((1, tm, D), lambda k, i: (k, i, 0)), pl.BlockSpec((9, D), lambda k, i: (0, 0)),
                  pl.BlockSpec((1, D), lambda k, i: (0, 0)), pl.BlockSpec((1, D, D), lambda k, i: (k, 0, 0))],
        out_specs=[last, pl.BlockSpec((1, D, D), lambda k, i: (k, 0, 0)), pl.BlockSpec((8, D), lambda k, i: (0, 0)),
                   last],
        out_shape=[jax.ShapeDtypeStruct((T, D), F32), jax.ShapeDtypeStruct((8, D, D), MM),
                   jax.ShapeDtypeStruct((8, D), F32), jax.ShapeDtypeStruct((T, D), MM)],
        scratch_shapes=[pltpu.VMEM((ni, tm, D), F32), pltpu.VMEM((D, D), F32)],
    )(x, h, dxo, dp, mod, gnorm, w)
    return out[:4], out[4:]


def _hgrn_consts():
    rows = jnp.arange(SUB * HD) // HD
    e = (rows[:, None] == jnp.arange(HD)[None, :]).astype(MM)
    return e, e.T


def _rows_bcast(ref, cb, first, n):
    parts = [jnp.broadcast_to(ref[pl.ds(c * CHUNK + first, 1), :], (n, HD)) for c in range(cb // CHUNK)]
    return jnp.concatenate(parts, axis=0)


def _hgrn_pre(qr, fr, lb_ref, b_scr, cb):
    z = lb_ref[...]
    lb = _sig(z[0:1, :] - z[1:2, :])
    sq = _sig(qr)
    q = qr * sq * Q_SCALE
    sf = _sig(fr)
    fg = lb + (1.0 - lb) * sf
    lf = jnp.log(fg)
    k = 1.0 - fg
    tl = lax.broadcasted_iota(jnp.int32, (cb, HD), 0) % CHUNK
    bc = lf
    sh = 1
    while sh < CHUNK:
        bc = bc + jnp.where(tl >= sh, pltpu.roll(bc, sh, 0), 0.0)
        sh *= 2
    b_scr[...] = bc
    bl = _rows_bcast(b_scr, cb, CHUNK - 1, CHUNK)
    br = [None] + [_rows_bcast(b_scr, cb, SUB * i - 1, CHUNK) for i in range(1, NSUB)]
    sb = tl // SUB
    bref = jnp.where(sb == 0, bc, jnp.where(sb == 1, br[1], jnp.where(sb == 2, br[2], br[3])))
    eb = jnp.exp(bc)
    ekd = jnp.exp(bl - bc)
    eqo = jnp.exp(bc - bref)
    eko = [None] + [jnp.exp(jnp.where(tl < SUB * i, br[i] - bc, NEG)) for i in range(1, NSUB)]
    starts = [jnp.broadcast_to(b_scr[pl.ds(r, 1), :], (SUB, HD)) for r in range(0, cb, SUB)]
    grow = jnp.concatenate(starts, axis=0) - bc
    edq = jnp.exp(-grow)
    edk = jnp.exp(jnp.minimum(grow, DIAG_SAFE_EXP))
    return dict(lb=lb, sq=sq, q=q, sf=sf, fg=fg, k=k, tl=tl, sb=sb, b=bc, bl=bl, eb=eb, ekd=ekd, eqo=eqo,
                eko=eko, qe=q * eb, kd=k * ekd, qo=q * eqo, ko=[None] + [k * eko[i] for i in range(1, NSUB)],
                edq=edq, edk=edk, qd=q * edq, kdd=k * edk, diag_safe=jnp.max(grow) < DIAG_SAFE_EXP)


def _pad_rows(x):
    return jnp.concatenate([x, jnp.zeros_like(x)], axis=0)


def _by_subblock(sbc, parts):
    out = jnp.zeros_like(parts[1])
    for i in range(1, NSUB):
        out = jnp.where(sbc == i, parts[i], out)
    return out


def _hgrn_fwd(p, hgrn_lb, hgrn_g, carry):
    T = p.shape[1]
    cb = min(HGRN_BLOCK, T)
    nch = cb // CHUNK
    ncb = T // cb
    e_mat, _ = _hgrn_consts()

    def body(p_ref, lb_ref, g_ref, e_ref, o_ref, oa_ref, a_ref, s_ref, st_scr, q_scr, k_scr, b_scr, z_scr, ad_scr):
        @pl.when(pl.program_id(1) == 0)
        def _():
            st_scr[...] = jnp.zeros_like(st_scr)

        v = p_ref[2]
        og = p_ref[3]
        pre = _hgrn_pre(p_ref[0], p_ref[1], lb_ref, b_scr, cb)
        chunks = [slice(c * CHUNK, (c + 1) * CHUNK) for c in range(nch)]
        row_i = lax.broadcasted_iota(jnp.int32, (CHUNK, HD), 0)
        lane_i = lax.broadcasted_iota(jnp.int32, (CHUNK, HD), 1)
        sbc = row_i // SUB
        own_block = (lane_i // SUB == sbc) & (lane_i <= row_i)

        @pl.when(pre["diag_safe"])
        def _():
            for rs in chunks:
                ad_scr[rs, :] = jnp.where(own_block, _mm_nt(pre["qd"][rs], _pad_rows(pre["kdd"][rs])), 0.0)

        @pl.when(jnp.logical_not(pre["diag_safe"]))
        def _():
            q_scr[...] = pre["q"]
            k_scr[...] = pre["k"]
            ti = lax.broadcasted_iota(jnp.int32, (SUB, HD), 0)

            def zbody(c, carry):
                for i in range(NSUB):
                    r0 = pl.multiple_of(c * CHUNK + SUB * i, SUB)
                    qi = q_scr[pl.ds(r0, SUB), :]
                    bi = b_scr[pl.ds(r0, SUB), :]
                    for s in range(SUB):
                        krow = k_scr[pl.ds(r0 + s, 1), :]
                        brow = b_scr[pl.ds(r0 + s, 1), :]
                        if s < 8:
                            zz = qi * krow * jnp.exp(jnp.where(ti >= s, bi - brow, NEG))
                        else:
                            lo = qi[8:] * krow * jnp.exp(jnp.where(ti[8:] >= s, bi[8:] - brow, NEG))
                            zz = jnp.concatenate([jnp.zeros((8, HD), F32), lo], axis=0)
                        z_scr[i, pl.ds(pl.multiple_of(c * SUB, SUB), SUB), s * HD:(s + 1) * HD] = zz.astype(MM)
                return carry

            lax.fori_loop(0, nch, zbody, 0)
            adiag = [_mm(z_scr[i], e_ref[...]) for i in range(NSUB)]
            for c, rs in enumerate(chunks):
                dparts = []
                for i in range(NSUB):
                    blk = adiag[i][c * SUB:(c + 1) * SUB]
                    dparts.append(blk if i == 0 else pltpu.roll(blk, SUB * i, 1))
                ad_scr[rs, :] = jnp.concatenate(dparts, axis=0)

        offs = [[_mm_nt(pre["qo"][rs], _pad_rows(pre["ko"][i][rs])) for i in range(1, NSUB)] for rs in chunks]
        kv = [_mm_tn(v[rs], pre["kd"][rs]) for rs in chunks]
        a_parts = [_by_subblock(sbc, [None] + offs[c]) + ad_scr[rs, :] for c, rs in enumerate(chunks)]
        a_ref[0] = jnp.concatenate(a_parts, axis=0)
        o_intra = [_mm(a_parts[c], _pad_rows(v[rs])) for c, rs in enumerate(chunks)]
        states = []
        st = st_scr[...]
        for c in range(nch):
            states.append(st)
            st = st * jnp.exp(b_scr[pl.ds(c * CHUNK + CHUNK - 1, 1), :]) + kv[c]
        st_scr[...] = st
        for c in range(nch):
            s_ref[0, c] = states[c]
        o = jnp.concatenate([o_intra[c] + _mm_nt(pre["qe"][rs], states[c]) for c, rs in enumerate(chunks)], axis=0)
        o_ref[...] = o
        on = o * lax.rsqrt(_rowmean(o * o) + EPS) * g_ref[...]
        oa_ref[...] = (on * og * _sig(og)).astype(ACT)

    out = _gridded(
        body, carry, name="hgrn_fwd", grid=(HEADS, ncb),
        in_specs=[pl.BlockSpec((4, cb, HD), lambda h, c: (0, c, h)),
                  pl.BlockSpec((2, HD), lambda h, c: (0, h)),
                  pl.BlockSpec((1, HD), lambda h, c: (0, h)),
                  pl.BlockSpec((SUB * HD, HD), lambda h, c: (0, 0))],
        out_specs=[pl.BlockSpec((cb, HD), lambda h, c: (c, h)),
                   pl.BlockSpec((cb, HD), lambda h, c: (c, h)),
                   pl.BlockSpec((1, cb, HD), lambda h, c: (h, c, 0)),
                   pl.BlockSpec((1, nch, HD, HD), lambda h, c: (h, c, 0, 0))],
        out_shape=[jax.ShapeDtypeStruct((T, D), F32), jax.ShapeDtypeStruct((T, D), ACT),
                   jax.ShapeDtypeStruct((HEADS, T, HD), F32),
                   jax.ShapeDtypeStruct((HEADS, T // CHUNK, HD, HD), F32)],
        scratch_shapes=[pltpu.VMEM((HD, HD), F32), pltpu.VMEM((cb, HD), F32), pltpu.VMEM((cb, HD), F32),
                        pltpu.VMEM((cb, HD), F32), pltpu.VMEM((NSUB, nch * SUB, SUB * HD), MM),
                        pltpu.VMEM((cb, HD), F32)],
    )(p, hgrn_lb, hgrn_g, e_mat)
    return out[:4], out[4:]


def _hgrn_bwd(p, o, a_all, s_all, doa, hgrn_lb, hgrn_g, dp, carry):
    T = p.shape[1]
    cb = min(HGRN_BLOCK, T)
    nch = cb // CHUNK
    ncb = T // cb
    _, et_mat = _hgrn_consts()

    def body(p_ref, o_ref, a_ref, s_ref, doa_ref, lb_ref, g_ref, et_ref, dp_in, dp_ref, sm_ref,
             dst_scr, q_scr, k_scr, b_scr, x_scr, dqd_scr, dkd_scr):
        del dp_in

        @pl.when(pl.program_id(1) == 0)
        def _():
            dst_scr[...] = jnp.zeros_like(dst_scr)
            sm_ref[...] = jnp.zeros_like(sm_ref)

        qr = p_ref[0]
        v = p_ref[2]
        og = p_ref[3]
        pre = _hgrn_pre(qr, p_ref[1], lb_ref, b_scr, cb)
        q, k = pre["q"], pre["k"]
        g = g_ref[...]
        ov = o_ref[...]
        r = lax.rsqrt(_rowmean(ov * ov) + EPS)
        oh = ov * r
        sgo = _sig(og)
        doa_v = doa_ref[...]
        don = doa_v * og * sgo
        dog = doa_v * oh * g * sgo * (1.0 + og * (1.0 - sgo))
        sm_ref[1:2, :] += _colsum(don * oh)
        doh = don * g
        do = r * (doh - oh * _rowmean(doh * oh))

        sbc = lax.broadcasted_iota(jnp.int32, (CHUNK, HD), 0) // SUB
        row_i = lax.broadcasted_iota(jnp.int32, (CHUNK, HD), 0)
        lane_i = lax.broadcasted_iota(jnp.int32, (CHUNK, HD), 1)
        causal = lane_i <= row_i
        chunks = [slice(c * CHUNK, (c + 1) * CHUNK) for c in range(nch)]
        da_parts = [jnp.where(causal, _mm_nt(do[rs], _pad_rows(v[rs])), 0.0) for rs in chunks]
        dv_parts = [_mm_tn(a_ref[0, rs, :], do[rs])[:CHUNK] for rs in chunks]
        dqoff_mm = [[_mm(da_parts[c], _pad_rows(pre["ko"][i][rs])) for i in range(1, NSUB)]
                    for c, rs in enumerate(chunks)]
        dkoff_mm = [[_mm_tn(jnp.where(sbc == i, da_parts[c], 0.0), pre["qo"][rs])[:CHUNK] for i in range(1, NSUB)]
                    for c, rs in enumerate(chunks)]
        dqoff_parts = [_by_subblock(sbc, [None] + dqoff_mm[c]) for c in range(nch)]
        dkoff_parts = []
        for c, rs in enumerate(chunks):
            dko = pre["eko"][1][rs] * dkoff_mm[c][0]
            for i in range(2, NSUB):
                dko = dko + pre["eko"][i][rs] * dkoff_mm[c][i - 1]
            dkoff_parts.append(dko)
        own_block = lane_i // SUB == sbc

        @pl.when(pre["diag_safe"])
        def _():
            for c, rs in enumerate(chunks):
                da_own = jnp.where(own_block, da_parts[c], 0.0)
                dqd_scr[rs, :] = pre["edq"][rs] * _mm(da_own, _pad_rows(pre["kdd"][rs]))
                dkd_scr[rs, :] = pre["edk"][rs] * _mm_tn(da_own, pre["qd"][rs])[:CHUNK]

        @pl.when(jnp.logical_not(pre["diag_safe"]))
        def _():
            q_scr[...] = q
            k_scr[...] = k
            for i in range(NSUB):
                rows = []
                for c in range(nch):
                    blk = da_parts[c][SUB * i:SUB * (i + 1)]
                    rows.append(blk if i == 0 else pltpu.roll(blk, HD - SUB * i, 1))
                x_scr[i] = _mm(jnp.concatenate(rows, axis=0), et_ref[...])
            ti = lax.broadcasted_iota(jnp.int32, (SUB, HD), 0)

            def dbody(c, carry):
                for i in range(NSUB):
                    r0 = pl.multiple_of(c * CHUNK + SUB * i, SUB)
                    qi = q_scr[pl.ds(r0, SUB), :]
                    bi = b_scr[pl.ds(r0, SUB), :]
                    dq_hi = jnp.zeros((8, HD), F32)
                    dq_lo = jnp.zeros((8, HD), F32)
                    dk_hi = jnp.zeros((8, HD), F32)
                    dk_lo = jnp.zeros((8, HD), F32)
                    c0 = pl.multiple_of(c * SUB, SUB)
                    t8 = ti[:8]
                    for s in range(SUB):
                        krow = k_scr[pl.ds(r0 + s, 1), :]
                        brow = b_scr[pl.ds(r0 + s, 1), :]
                        w_lo = (x_scr[i, pl.ds(c0 + 8, 8), s * HD:(s + 1) * HD]
                                * jnp.exp(jnp.where(t8 + 8 >= s, bi[8:] - brow, NEG)))
                        dq_lo = dq_lo + w_lo * krow
                        col = _colsum(w_lo * qi[8:])
                        if s < 8:
                            w_hi = (x_scr[i, pl.ds(c0, 8), s * HD:(s + 1) * HD]
                                    * jnp.exp(jnp.where(t8 >= s, bi[:8] - brow, NEG)))
                            dq_hi = dq_hi + w_hi * krow
                            dk_hi = jnp.where(t8 == s, col + _colsum(w_hi * qi[:8]), dk_hi)
                        else:
                            dk_lo = jnp.where(t8 + 8 == s, col, dk_lo)
                    dqd_scr[pl.ds(r0, SUB), :] = jnp.concatenate([dq_hi, dq_lo], axis=0)
                    dkd_scr[pl.ds(r0, SUB), :] = jnp.concatenate([dk_hi, dk_lo], axis=0)
                return carry

            lax.fori_loop(0, nch, dbody, 0)

        qdo = [_mm_tn(do[rs], pre["qe"][rs]) for rs in chunks]
        dsts = [None] * nch
        dst = dst_scr[...]
        for c in reversed(range(nch)):
            dsts[c] = dst
            dst = dst * jnp.exp(b_scr[pl.ds(c * CHUNK + CHUNK - 1, 1), :]) + qdo[c]
        dst_scr[...] = dst
        sts = [s_ref[0, c] for c in range(nch)]
        dqe_parts = [_mm(do[rs], sts[c]) for c, rs in enumerate(chunks)]
        dkdec_parts = [_mm(v[rs], dsts[c]) for c, rs in enumerate(chunks)]
        dvi_parts = [_mm_nt(pre["kd"][rs], dsts[c]) for c, rs in enumerate(chunks)]
        debl_parts = [_colsum(dsts[c] * sts[c]) for c in range(nch)]
        dqe = jnp.concatenate(dqe_parts, axis=0)
        dkdec = jnp.concatenate(dkdec_parts, axis=0)
        dq_tot = jnp.concatenate(dqoff_parts, axis=0) * pre["eqo"] + dqd_scr[...] + dqe * pre["eb"]
        dk_inter = dkdec * pre["ekd"]
        dk_tot = jnp.concatenate(dkoff_parts, axis=0) + dkd_scr[...] + dk_inter
        db = q * dq_tot - k * dk_tot
        kdk = k * dk_inter
        dbl = jnp.concatenate(
            [jnp.broadcast_to(jnp.exp(b_scr[pl.ds(c * CHUNK + CHUNK - 1, 1), :]) * debl_parts[c]
                              + _colsum(kdk[c * CHUNK:(c + 1) * CHUNK]), (CHUNK, HD)) for c in range(nch)], axis=0)
        tl = pre["tl"]
        rc = db
        sh = 1
        while sh < CHUNK:
            rc = rc + jnp.where(tl + sh < CHUNK, pltpu.roll(rc, cb - sh, 0), 0.0)
            sh *= 2
        dlf = rc + dbl
        dfg = dlf / pre["fg"] - dk_tot
        sf = pre["sf"]
        lb = pre["lb"]
        sm_ref[0:1, :] += _colsum(dfg * (1.0 - sf))
        sq = pre["sq"]
        dp_ref[0] = (dq_tot * Q_SCALE * sq * (1.0 + qr * (1.0 - sq))).astype(ACT)
        dp_ref[1] = (dfg * (1.0 - lb) * sf * (1.0 - sf)).astype(ACT)
        dp_ref[2] = (jnp.concatenate(dv_parts, axis=0) + jnp.concatenate(dvi_parts, axis=0)).astype(ACT)
        dp_ref[3] = dog.astype(ACT)

    rev = lambda c: ncb - 1 - c
    out = _gridded(
        body, carry, name="hgrn_bwd", grid=(HEADS, ncb),
        in_specs=[pl.BlockSpec((4, cb, HD), lambda h, c: (0, rev(c), h)),
                  pl.BlockSpec((cb, HD), lambda h, c: (rev(c), h)),
                  pl.BlockSpec((1, cb, HD), lambda h, c: (h, rev(c), 0)),
                  pl.BlockSpec((1, nch, HD, HD), lambda h, c: (h, rev(c), 0, 0)),
                  pl.BlockSpec((cb, HD), lambda h, c: (rev(c), h)),
                  pl.BlockSpec((2, HD), lambda h, c: (0, h)),
                  pl.BlockSpec((1, HD), lambda h, c: (0, h)),
                  pl.BlockSpec((HD, SUB * HD), lambda h, c: (0, 0)),
                  pl.BlockSpec(memory_space=pl.ANY)],
        out_specs=[pl.BlockSpec((4, cb, HD), lambda h, c: (0, rev(c), h)),
                   pl.BlockSpec((8, HD), lambda h, c: (0, h))],
        out_shape=[jax.ShapeDtypeStruct(dp.shape, dp.dtype), jax.ShapeDtypeStruct((8, D), F32)],
        aliases={8: 0},
        scratch_shapes=[pltpu.VMEM((HD, HD), F32), pltpu.VMEM((cb, HD), F32), pltpu.VMEM((cb, HD), F32),
                        pltpu.VMEM((cb, HD), F32), pltpu.VMEM((NSUB, nch * SUB, SUB * HD), F32),
                        pltpu.VMEM((cb, HD), F32), pltpu.VMEM((cb, HD), F32)],
    )(p, o, a_all, s_all, doa, hgrn_lb, hgrn_g, et_mat, dp)
    return out[:2], out[2:]


def _ln_fwd(u1, g, b):
    mu = _rowmean(u1)
    xc = u1 - mu
    rs = lax.rsqrt(_rowmean(xc * xc) + EPS)
    xh = xc * rs
    return xh * g + b, xh, rs


CONV_RB = 64
LANES = 128


def _shift_rows(src, sh, ls, n):
    for r in range(1, 8):
        sh[r - 1, 0:n, :] = src[pl.ds(r, n), ls]


def _tap(src, sh, ls, off, r0, rows):
    r = off % 8
    if r == 0:
        return src[pl.ds(r0 + off, rows), ls]
    return sh[r - 1, pl.ds(r0 + off - r, rows), :]


def _conv_fwd(p, cw, cb_, lng, lnb):
    T = p.shape[1]
    tm = min(512, T)
    n = HALO + tm - 8

    def body(p_ref, cw_ref, cb_ref, g_ref, b_ref, u1_ref, u2_ref, buf, sh):
        @pl.when(pl.program_id(0) == 0)
        def _():
            buf[0:HALO, :] = jnp.zeros((HALO, D), F32)

        buf[HALO:HALO + tm, :] = p_ref[0] * _sig(p_ref[1])
        for lb in range(D // LANES):
            ls = slice(lb * LANES, (lb + 1) * LANES)
            _shift_rows(buf, sh, ls, n)
            taps = [cw_ref[j:j + 1, ls] for j in range(CONV_K)]
            bias = cb_ref[:, ls]

            def rows_body(rb, carry):
                r0 = pl.multiple_of(rb * CONV_RB, CONV_RB)
                acc = jnp.broadcast_to(bias, (CONV_RB, LANES))
                for j in range(CONV_K):
                    acc = acc + taps[j] * _tap(buf, sh, ls, HALO - (CONV_K - 1) + j, r0, CONV_RB)
                u1_ref[pl.ds(r0, CONV_RB), ls] = acc
                return carry

            lax.fori_loop(0, tm // CONV_RB, rows_body, 0)
        y, _, _ = _ln_fwd(u1_ref[...], g_ref[...], b_ref[...])
        u2_ref[...] = (y * _sig(y)).astype(ACT)
        buf[0:HALO, :] = buf[tm:tm + HALO, :]

    return pl.pallas_call(
        body, name="conv_fwd", grid=(T // tm,),
        in_specs=[pl.BlockSpec((2, tm, D), lambda i: (2, i, 0)), pl.BlockSpec((HALO, D), lambda i: (0, 0)),
                  pl.BlockSpec((1, D), lambda i: (0, 0)), pl.BlockSpec((1, D), lambda i: (0, 0)),
                  pl.BlockSpec((1, D), lambda i: (0, 0))],
        out_specs=[pl.BlockSpec((tm, D), lambda i: (i, 0)), pl.BlockSpec((tm, D), lambda i: (i, 0))],
        out_shape=[jax.ShapeDtypeStruct((T, D), F32), jax.ShapeDtypeStruct((T, D), ACT)],
        scratch_shapes=[pltpu.VMEM((HALO + tm, D), F32), pltpu.VMEM((7, n, LANES), F32)],
        compiler_params=_cparams(1),
    )(p, cw, cb_, lng, lnb)


def _conv_bwd(p, u1, du2, cw, lng, lnb, dp):
    T = p.shape[1]
    tm = min(512, T)
    ni = T // tm
    hb = tm // HALO

    n = HALO + tm - 8

    def body(p_ref, ph_ref, u1_ref, du2_ref, cw_ref, g_ref, b_ref, dp_in, dp_ref, dcw_ref, sm_ref, ubuf, dbuf,
             sh, dacc):
        del dp_in
        step = pl.program_id(0)

        @pl.when(step == 0)
        def _():
            dbuf[tm:tm + HALO, :] = jnp.zeros((HALO, D), F32)
            dcw_ref[...] = jnp.zeros_like(dcw_ref)
            sm_ref[...] = jnp.zeros_like(sm_ref)

        ua = p_ref[0]
        sgb = _sig(p_ref[1])
        halo = ph_ref[0] * _sig(ph_ref[1])
        ubuf[0:HALO, :] = jnp.where(step == ni - 1, 0.0, halo)
        ubuf[HALO:HALO + tm, :] = ua * sgb
        g = g_ref[...]
        y, xh, rs = _ln_fwd(u1_ref[...], g, b_ref[...])
        sy = _sig(y)
        dy = du2_ref[...] * sy * (1.0 + y * (1.0 - sy))
        sm_ref[1:2, :] += _colsum(dy * xh)
        sm_ref[2:3, :] += _colsum(dy)
        dxh = dy * g
        du1 = rs * (dxh - _rowmean(dxh) - xh * _rowmean(dxh * xh))
        sm_ref[0:1, :] += _colsum(du1)
        dbuf[0:tm, :] = du1
        for lb in range(D // LANES):
            ls = slice(lb * LANES, (lb + 1) * LANES)
            taps = [cw_ref[j:j + 1, ls] for j in range(CONV_K)]
            _shift_rows(dbuf, sh, ls, n)

            def du0_body(rb, carry):
                r0 = pl.multiple_of(rb * CONV_RB, CONV_RB)
                acc = jnp.zeros((CONV_RB, LANES), F32)
                for j in range(CONV_K):
                    acc = acc + taps[j] * _tap(dbuf, sh, ls, CONV_K - 1 - j, r0, CONV_RB)
                dp_ref[0, pl.ds(r0, CONV_RB), ls] = acc.astype(ACT)
                return carry

            lax.fori_loop(0, tm // CONV_RB, du0_body, 0)
            _shift_rows(ubuf, sh, ls, n)
            dacc[...] = jnp.zeros_like(dacc)

            def dcw_body(rb, carry):
                r0 = pl.multiple_of(rb * CONV_RB, CONV_RB)
                d = dbuf[pl.ds(r0, CONV_RB), ls]
                for j in range(CONV_K):
                    prod = d * _tap(ubuf, sh, ls, HALO - (CONV_K - 1) + j, r0, CONV_RB)
                    dacc[8 * j:8 * j + 8, :] += jnp.sum(prod.reshape(CONV_RB // 8, 8, LANES), axis=0)
                return carry

            lax.fori_loop(0, tm // CONV_RB, dcw_body, 0)
            for j in range(CONV_K):
                dcw_ref[j:j + 1, ls] += _colsum(dacc[8 * j:8 * j + 8, :])
        du0 = dp_ref[0].astype(F32)
        dp_ref[0] = (du0 * sgb).astype(ACT)
        dp_ref[1] = (du0 * ua * sgb * (1.0 - sgb)).astype(ACT)
        dbuf[tm:tm + HALO, :] = dbuf[0:HALO, :]

    rev = lambda i: ni - 1 - i
    return pl.pallas_call(
        body, name="conv_bwd", grid=(ni,),
        in_specs=[pl.BlockSpec((2, tm, D), lambda i: (2, rev(i), 0)),
                  pl.BlockSpec((2, HALO, D), lambda i: (2, jnp.maximum(rev(i) * hb - 1, 0), 0)),
                  pl.BlockSpec((tm, D), lambda i: (rev(i), 0)), pl.BlockSpec((tm, D), lambda i: (rev(i), 0)),
                  pl.BlockSpec((HALO, D), lambda i: (0, 0)), pl.BlockSpec((1, D), lambda i: (0, 0)),
                  pl.BlockSpec((1, D), lambda i: (0, 0)), pl.BlockSpec(memory_space=pl.ANY)],
        out_specs=[pl.BlockSpec((2, tm, D), lambda i: (2, rev(i), 0)),
                   pl.BlockSpec((HALO, D), lambda i: (0, 0)), pl.BlockSpec((8, D), lambda i: (0, 0))],
        out_shape=[jax.ShapeDtypeStruct(dp.shape, dp.dtype), jax.ShapeDtypeStruct((HALO, D), F32),
                   jax.ShapeDtypeStruct((8, D), F32)],
        input_output_aliases={7: 0},
        scratch_shapes=[pltpu.VMEM((HALO + tm, D), F32), pltpu.VMEM((tm + HALO, D), F32),
                        pltpu.VMEM((7, n, LANES), F32), pltpu.VMEM((8 * CONV_K, LANES), F32)],
        compiler_params=_cparams(1),
    )(p, p, u1, du2, cw, lng, lnb, dp)


def _mixout_fwd(x, oa, u2, p, mod, mo, w_a, w_b, w_o):
    T = x.shape[0]
    tm = min(512, T)

    def body(x_ref, oa_ref, u2_ref, p_ref, mod_ref, wa_ref, wb_ref, wo_ref, xo_ref, ya_ref, yb_ref, mo_ref):
        ya = _mm(oa_ref[...], wa_ref[...])
        yb = _mm(u2_ref[...], wb_ref[...])
        ya_ref[...] = ya.astype(ACT)
        yb_ref[...] = yb.astype(ACT)
        merged = _sig(p_ref[0]) * ya + _sig(p_ref[1]) * yb
        out = _mm(merged, wo_ref[...])
        mo_ref[...] = out
        xo_ref[...] = x_ref[...] + mod_ref[mo + 2:mo + 3, :] * out

    tile = pl.BlockSpec((tm, D), lambda i: (i, 0))
    wspec = pl.BlockSpec((D, D), lambda i: (0, 0))
    return pl.pallas_call(
        body, name="mixout_fwd", grid=(T // tm,),
        in_specs=[tile, tile, tile, pl.BlockSpec((2, tm, D), lambda i: (3, i, 0)),
                  pl.BlockSpec((9, D), lambda i: (0, 0)), wspec, wspec, wspec],
        out_specs=[tile, tile, tile, tile],
        out_shape=[jax.ShapeDtypeStruct((T, D), F32), jax.ShapeDtypeStruct((T, D), ACT),
                   jax.ShapeDtypeStruct((T, D), ACT), jax.ShapeDtypeStruct((T, D), F32)],
        compiler_params=_cparams(1),
    )(x, oa, u2, p, mod, w_a, w_b, w_o)


def _mixout_bwd(dxo, oa, u2, ya, yb, mout, p, mod, mo, w_a, w_b, w_o):
    T = dxo.shape[0]
    tm = min(256, T)

    def body(dxo_ref, oa_ref, u2_ref, ya_ref, yb_ref, mo_ref, p_ref, mod_ref, wa_ref, wb_ref, wo_ref,
             dp_ref, doa_ref, du2_ref, dwa_ref, dwb_ref, dwo_ref, sm_ref):
        @pl.when(pl.program_id(0) == 0)
        def _():
            dwa_ref[...] = jnp.zeros_like(dwa_ref)
            dwb_ref[...] = jnp.zeros_like(dwb_ref)
            dwo_ref[...] = jnp.zeros_like(dwo_ref)
            sm_ref[...] = jnp.zeros_like(sm_ref)

        dxo_v = dxo_ref[...]
        sm_ref[2:3, :] += _colsum(dxo_v * mo_ref[...])
        dmo = (mod_ref[mo + 2:mo + 3, :] * dxo_v).astype(MM)
        ya = ya_ref[...].astype(F32)
        yb = yb_ref[...].astype(F32)
        sga = _sig(p_ref[0])
        sgb = _sig(p_ref[1])
        merged = (sga * ya + sgb * yb).astype(MM)
        dwo_ref[...] += _mm_tn(merged, dmo)
        dmg = _mm_nt(dmo, wo_ref[...])
        dp_ref[0] = (dmg * ya * sga * (1.0 - sga)).astype(ACT)
        dp_ref[1] = (dmg * yb * sgb * (1.0 - sgb)).astype(ACT)
        dya = (dmg * sga).astype(MM)
        dyb = (dmg * sgb).astype(MM)
        dwa_ref[...] += _mm_tn(oa_ref[...], dya)
        dwb_ref[...] += _mm_tn(u2_ref[...], dyb)
        doa_ref[...] = _mm_nt(dya, wa_ref[...])
        du2_ref[...] = _mm_nt(dyb, wb_ref[...])

    tile = pl.BlockSpec((tm, D), lambda i: (i, 0))
    wspec = pl.BlockSpec((D, D), lambda i: (0, 0))
    return pl.pallas_call(
        body, name="mixout_bwd", grid=(T // tm,),
        in_specs=[tile, tile, tile, tile, tile, tile, pl.BlockSpec((2, tm, D), lambda i: (3, i, 0)),
                  pl.BlockSpec((9, D), lambda i: (0, 0)), wspec, wspec, wspec],
        out_specs=[pl.BlockSpec((2, tm, D), lambda i: (3, i, 0)), tile, tile, wspec, wspec, wspec,
                   pl.BlockSpec((8, D), lambda i: (0, 0))],
        out_shape=[jax.ShapeDtypeStruct((8, T, D), ACT), jax.ShapeDtypeStruct((T, D), F32),
                   jax.ShapeDtypeStruct((T, D), F32), jax.ShapeDtypeStruct((D, D), F32),
                   jax.ShapeDtypeStruct((D, D), F32), jax.ShapeDtypeStruct((D, D), F32),
                   jax.ShapeDtypeStruct((8, D), F32)],
        compiler_params=_cparams(1),
    )(dxo, oa, u2, ya, yb, mout, p, mod, w_a, w_b, w_o)


def _ada_wgrad(cs_all, dmod_cols):
    cs_t = jnp.pad(cs_all.T, ((0, 0), (0, HD - N_DEV)))
    dm = jnp.pad(dmod_cols, ((0, HD - N_DEV), (0, 0)))

    def body(cs_ref, d_ref, out_ref):
        out_ref[...] = jnp.dot(cs_ref[...], d_ref[...], preferred_element_type=F32,
                               precision=lax.Precision.HIGHEST)

    return pl.pallas_call(
        body, name="ada_wgrad", out_shape=jax.ShapeDtypeStruct((D, dmod_cols.shape[1]), F32),
        compiler_params=pltpu.CompilerParams(vmem_limit_bytes=VMEM_LIMIT),
    )(cs_t, dm)


def _adam_math(w, g, m, v):
    m2 = ADAM_B1 * m + (1.0 - ADAM_B1) * g
    v2 = ADAM_B2 * v + (1.0 - ADAM_B2) * (g * g)
    m_hat = m2 / (1.0 - ADAM_B1 ** ADAM_STEP)
    v_hat = v2 / (1.0 - ADAM_B2 ** ADAM_STEP)
    delta = -ADAM_LR * (m_hat / (jnp.sqrt(v_hat) + ADAM_EPS) + ADAM_WD * w)
    return delta, m2, v2


def _adamw(w, m, v, g, name):
    R, C = w.shape
    slots = g.ndim == 3
    n_slots = g.shape[0] if slots else 0
    tr = R
    for cand in (256, 176):
        if R % cand == 0 and R > cand:
            tr = cand
            break

    def body(w_ref, m_ref, v_ref, g_ref, go_ref, d_ref, mo_ref, vo_ref):
        if slots:
            gv = g_ref[0].astype(F32)
            for s in range(1, n_slots):
                gv = gv + g_ref[s].astype(F32)
        else:
            gv = g_ref[...]
        go_ref[...] = gv
        d_ref[...], mo_ref[...], vo_ref[...] = _adam_math(w_ref[...], gv, m_ref[...], v_ref[...])

    tile = pl.BlockSpec((tr, C), lambda i: (i, 0))
    gspec = pl.BlockSpec((n_slots, tr, C), lambda i: (0, i, 0)) if slots else tile
    sds = jax.ShapeDtypeStruct((R, C), F32)
    return pl.pallas_call(
        body, name=name, grid=(R // tr,), in_specs=[tile, tile, tile, gspec], out_specs=[tile] * 4,
        out_shape=[sds] * 4, compiler_params=_cparams(1),
    )(w, m, v, g)


def _sum_slots(pack, name, tr):
    n, R, C = pack.shape

    def body(p_ref, out_ref):
        acc = p_ref[0].astype(F32)
        for s in range(1, n):
            acc = acc + p_ref[s].astype(F32)
        out_ref[...] = acc

    return pl.pallas_call(
        body, name=name, grid=(R // tr,), in_specs=[pl.BlockSpec((n, tr, C), lambda i: (0, i, 0))],
        out_specs=pl.BlockSpec((tr, C), lambda i: (i, 0)), out_shape=jax.ShapeDtypeStruct((R, C), F32),
        compiler_params=_cparams(1))(pack)


def _me():
    return lax.axis_index("x"), lax.axis_index("y"), lax.axis_index("c")


def _peer(r):
    x, y, c = _me()
    px = 1 - x if r & 4 else x
    py = 1 - y if r & 2 else y
    pc = 1 - c if r & 1 else c
    return (px, py, pc), 4 * px + 2 * py + pc


def _small_gather(x_ref, out_ref, send_sems, recv_sems):
    R = x_ref.shape[0]
    mx, my, mc = _me()
    me = 4 * mx + 2 * my + mc
    mine = out_ref.at[pl.ds(pl.multiple_of(me * R, 8), R), :]
    copies = []
    for r in range(1, N_DEV):
        dev, _ = _peer(r)
        copies.append(pltpu.make_async_remote_copy(
            src_ref=x_ref, dst_ref=mine, send_sem=send_sems.at[r - 1], recv_sem=recv_sems.at[r - 1],
            device_id=dev, device_id_type=MESH))
    for cp in copies:
        cp.start()
    mine[...] = x_ref[...]
    for r in range(1, N_DEV):
        dev, idx = _peer(r)
        theirs = out_ref.at[pl.ds(pl.multiple_of(idx * R, 8), R), :]
        pltpu.make_async_remote_copy(
            src_ref=x_ref, dst_ref=theirs, send_sem=send_sems.at[r - 1], recv_sem=recv_sems.at[r - 1],
            device_id=dev, device_id_type=MESH).wait_recv()
    for cp in copies:
        cp.wait_send()


def _prologue(cs, ada_w, ada_b_cols, big):
    n = len(big)
    ncol = ada_w.shape[1]
    big_shape, big_sems = _xchg_specs(big, "gather")

    def body(cs_ref, w_ref, b_ref, *rest):
        big_in, cs_all, mod_all, big_out = rest[:n], rest[n], rest[n + 1], rest[n + 2:2 * n + 2]
        mod_scr, s1, r1, s2, r2 = rest[2 * n + 2:2 * n + 7]
        sems = rest[2 * n + 7:]
        _xchg_start(big_in, big_out, sems, "gather")
        _small_gather(cs_ref, cs_all, s1, r1)
        pick = (lax.broadcasted_iota(jnp.int32, (N_DEV, N_DEV * 8), 1)
                == 8 * lax.broadcasted_iota(jnp.int32, (N_DEV, N_DEV * 8), 0)).astype(F32)
        per_device = jnp.dot(pick, cs_all[...], preferred_element_type=F32, precision=lax.Precision.HIGHEST)
        mod_scr[...] = jnp.dot(per_device, w_ref[...], preferred_element_type=F32,
                               precision=lax.Precision.HIGHEST) + b_ref[...]
        _small_gather(mod_scr, mod_all, s2, r2)
        _xchg_wait(big_in, big_out, sems, "gather")

    vmem = pl.BlockSpec(memory_space=pltpu.VMEM)
    hbm = pl.BlockSpec(memory_space=pl.ANY)
    dma7 = pltpu.SemaphoreType.DMA((N_DEV - 1,))
    out = pl.pallas_call(
        body, name="prologue",
        out_shape=[jax.ShapeDtypeStruct((N_DEV * 8, D), F32), jax.ShapeDtypeStruct((N_DEV * 8, ncol), F32)]
        + big_shape,
        in_specs=[vmem, vmem, vmem] + [hbm] * n, out_specs=[vmem, vmem] + [hbm] * n,
        scratch_shapes=[pltpu.VMEM((8, ncol), F32), dma7, dma7, dma7, dma7] + big_sems,
        compiler_params=pltpu.CompilerParams(vmem_limit_bytes=VMEM_LIMIT),
    )(cs, ada_w, ada_b_cols, *big)
    return out[0], out[1], out[2:]


def _allgather_small(x):
    R, C = x.shape

    def body(x_ref, out_ref, send_sems, recv_sems):
        _small_gather(x_ref, out_ref, send_sems, recv_sems)

    return pl.pallas_call(
        body, name="allgather_small_%dx%d" % (R, C),
        out_shape=jax.ShapeDtypeStruct((N_DEV * R, C), F32),
        in_specs=[pl.BlockSpec(memory_space=pltpu.VMEM)], out_specs=pl.BlockSpec(memory_space=pltpu.VMEM),
        scratch_shapes=[pltpu.SemaphoreType.DMA((N_DEV - 1,)), pltpu.SemaphoreType.DMA((N_DEV - 1,))],
    )(x)


N_CHIP = N_DEV // 2


def _xchg_copies(ins, outs, sems, mode):
    send_sems, recv_sems, local_sems = sems
    mx, my, mc = _me()
    me = 4 * mx + 2 * my + mc
    my_chip = 2 * mx + my
    sibling = _peer(1)[0]

    def rdma(a, r, dev, src, slot):
        k = a * (N_DEV - 1) + r - 1
        return pltpu.make_async_remote_copy(
            src_ref=src, dst_ref=outs[a].at[slot], send_sem=send_sems.at[k], recv_sem=recv_sems.at[k],
            device_id=dev, device_id_type=MESH)

    own, sends, relays, recvs = [], [], [], []
    for a in range(len(ins)):
        if mode == "pair":
            for chip in range(N_CHIP):
                src = ins[a].at[2 * chip + 1 - mc]
                sends.append(rdma(a, chip + 1, sibling, src, chip))
                recvs.append(rdma(a, chip + 1, sibling, src, chip))
            continue
        if mode == "quad":
            own.append(pltpu.make_async_copy(ins[a].at[my_chip], outs[a].at[my_chip], local_sems.at[a]))
            for r in (2, 4, 6):
                dev, idx = _peer(r)
                chip = idx // 2
                sends.append(rdma(a, r, dev, ins[a].at[chip], my_chip))
                recvs.append(rdma(a, r, dev, ins[a].at[chip], chip))
            continue
        gather = mode == "gather"
        own.append(pltpu.make_async_copy(ins[a] if gather else ins[a].at[me], outs[a].at[me], local_sems.at[a]))
        for r in range(1, N_DEV):
            dev, idx = _peer(r)
            if not gather:
                sends.append(rdma(a, r, dev, ins[a].at[idx], me))
                recvs.append(rdma(a, r, dev, ins[a].at[idx], idx))
            elif r == 1:
                sends.append(rdma(a, r, dev, ins[a], me))
                recvs.append(rdma(a, r, dev, ins[a], idx))
            elif r % 2 == 0:
                sends.append(rdma(a, r, dev, ins[a], me))
                relays.append((rdma(a, r, dev, ins[a], idx), rdma(a, r + 1, sibling, outs[a].at[idx], idx)))
            else:
                recvs.append(rdma(a, r, sibling, ins[a], idx))
    return own, sends, relays, recvs


def _xchg_start(ins, outs, sems, mode):
    own, sends, _, _ = _xchg_copies(ins, outs, sems, mode)
    for cp in own + sends:
        cp.start()


def _xchg_wait(ins, outs, sems, mode):
    own, sends, relays, recvs = _xchg_copies(ins, outs, sems, mode)
    for arrival, relay in relays:
        arrival.wait_recv()
        relay.start()
    for cp in recvs:
        cp.wait_recv()
    for cp in own:
        cp.wait()
    for cp in sends + [relay for _, relay in relays]:
        cp.wait_send()


def _xchg_specs(arrays, mode):
    n = len(arrays)
    shape = {"gather": lambda s: (N_DEV,) + s, "scatter": lambda s: s, "pair": lambda s: (N_CHIP,) + s[1:],
             "quad": lambda s: s}[mode]
    out_shape = [jax.ShapeDtypeStruct(shape(a.shape), a.dtype) for a in arrays]
    sems = [pltpu.SemaphoreType.DMA((n * (N_DEV - 1),)), pltpu.SemaphoreType.DMA((n * (N_DEV - 1),)),
            pltpu.SemaphoreType.DMA((n,))]
    return out_shape, sems


def _exchange(arrays, mode, name):
    n = len(arrays)

    def body(*refs):
        _xchg_start(refs[:n], refs[n:2 * n], refs[2 * n:], mode)
        _xchg_wait(refs[:n], refs[n:2 * n], refs[2 * n:], mode)

    out_shape, sems = _xchg_specs(arrays, mode)
    return pl.pallas_call(
        body, name=name, out_shape=out_shape,
        in_specs=[pl.BlockSpec(memory_space=pl.ANY)] * n, out_specs=[pl.BlockSpec(memory_space=pl.ANY)] * n,
        scratch_shapes=sems,
    )(*arrays)


def _gridded(body, carry, *, name, grid, in_specs, out_specs, out_shape, scratch_shapes=(), aliases=None):
    if carry is None:
        return pl.pallas_call(
            body, name=name, grid=grid, in_specs=list(in_specs), out_specs=list(out_specs),
            out_shape=list(out_shape), scratch_shapes=list(scratch_shapes), input_output_aliases=aliases or {},
            compiler_params=_cparams(len(grid)))
    arrays, mode = carry
    n, n_in, n_out, n_scr = len(arrays), len(in_specs), len(out_specs), len(scratch_shapes)
    c_shape, c_sems = _xchg_specs(arrays, mode)

    def wrapped(*refs):
        ins, cin = refs[:n_in], refs[n_in:n_in + n]
        o0 = n_in + n
        outs, cout = refs[o0:o0 + n_out], refs[o0 + n_out:o0 + n_out + n]
        s0 = o0 + n_out + n
        scr, sems = refs[s0:s0 + n_scr], refs[s0 + n_scr:]
        first = pl.program_id(0) == 0
        last = pl.program_id(0) == grid[0] - 1
        for ax in range(1, len(grid)):
            first = first & (pl.program_id(ax) == 0)
            last = last & (pl.program_id(ax) == grid[ax] - 1)

        @pl.when(first)
        def _():
            _xchg_start(cin, cout, sems, mode)

        body(*ins, *outs, *scr)

        @pl.when(last)
        def _():
            _xchg_wait(cin, cout, sems, mode)

    hbm = pl.BlockSpec(memory_space=pl.ANY)
    res = pl.pallas_call(
        wrapped, name=name, grid=grid, in_specs=list(in_specs) + [hbm] * n, out_specs=list(out_specs) + [hbm] * n,
        out_shape=list(out_shape) + c_shape, scratch_shapes=list(scratch_shapes) + c_sems,
        input_output_aliases=aliases or {}, compiler_params=_cparams(len(grid)),
    )
    return lambda *args: res(*args, *arrays)


def _local_step(x, target, mod, small, sh, w1):
    w1_in, w1_out = w1[0].reshape(2, D_FF, D), w1[1].reshape(D_FF, D)
    (x1, a1, b1, f1, h1), (wm_in,) = _ffn_fwd(x, mod, 0, small["norm_ffn1"], w1_in, w1_out, 0.5, "ffn1_fwd",
                                              ([sh["mix_w_in"]], "gather"))
    (p, h2), (wh_o, wc_o, wm_o, cw) = _mixin_fwd(
        x1, mod, 3, small["norm_mix"], wm_in,
        ([sh["hgrn_w_o"], sh["conv_w_o"], sh["mix_w_out"], sh["conv_w"]], "gather"))
    wh_o, wc_o, wm_o = wh_o.reshape(D, D), wc_o.reshape(D, D), wm_o.reshape(D, D)
    cw = jnp.pad(cw.transpose(1, 0, 2).reshape(CONV_K, D), ((0, HALO - CONV_K), (0, 0)))
    (o, oa, a_all, s_all), (w2_in, w2_out) = _hgrn_fwd(p, small["hgrn_lb"], small["hgrn_g"],
                                                       ([sh["ffn2_w_in"], sh["ffn2_w_out"]], "gather"))
    w2_in, w2_out = w2_in.reshape(2, D_FF, D), w2_out.reshape(D_FF, D)
    u1, u2 = _conv_fwd(p, cw, small["conv_b"], small["conv_ln_g"], small["conv_ln_b"])
    x2, ya, yb, mout = _mixout_fwd(x1, oa, u2, p, mod, 3, wh_o, wc_o, wm_o)
    (x3, a3, b3, f3, h3), _ = _ffn_fwd(x2, mod, 6, small["norm_ffn2"], w2_in, w2_out, 0.5, "ffn2_fwd", None)
    dx3, df3, sm_head = _head(x3, target, small["norm_final"], mod, 8, 0.5)

    (da3, db3, dw2_in, dw2_out), _ = _ffn_bwd_w(h3, df3, a3, b3, w2_out, "ffn2_bwd_w", None)
    (dx2, sm3), _ = _ffn_bwd_x(x2, dx3, f3, da3, db3, mod, 6, small["norm_ffn2"], w2_in, 0.5, "ffn2_bwd_x", None)
    dp, doa, du2, dwh_o, dwc_o, dwm_o, sm_mo = _mixout_bwd(dx2, oa, u2, ya, yb, mout, p, mod, 3, wh_o, wc_o, wm_o)
    dp, dcw, sm_cv = _conv_bwd(p, u1, du2, cw, small["conv_ln_g"], small["conv_ln_b"], dp)
    rows = lambda t: t.reshape(N_DEV, -1, D).astype(MM)
    (dp, sm_hg), (r2_in, r2_out) = _hgrn_bwd(p, o, a_all, s_all, doa, small["hgrn_lb"], small["hgrn_g"], dp,
                                             ([rows(dw2_in), rows(dw2_out)], "scatter"))
    (dx1, dwm_in, sm2, df1), (rh_o, rc_o, rm_o, rcw) = _mixin_bwd(
        x1, h2, dx2, dp, mod, 3, small["norm_mix"], wm_in, 2, 0.5,
        ([rows(dwh_o), rows(dwc_o), rows(dwm_o), dcw[:CONV_K].reshape(CONV_K, N_DEV, -1).transpose(1, 0, 2)],
         "scatter"))
    (da1, db1, dw1_in, dw1_out), (rm_in,) = _ffn_bwd_w(h1, df1, a1, b1, w1_out, "ffn1_bwd_w",
                                                      (_pair_reduce([dwm_in], "pair_mix"), "quad"))
    (dx0, sm1), (r1_in, r1_out) = _ffn_bwd_x(
        x, dx1, f1, da1, db1, mod, 0, small["norm_ffn1"], w1_in, 0.5, "ffn1_bwd_x",
        (_pair_reduce([rows(dw1_in), rows(dw1_out)], "pair_ffn1"), "quad"))

    dmod = jnp.concatenate([sm1[0:3], sm2[0:2], sm_mo[2:3], sm3[0:3]], axis=0)
    gsmall = dict(norm_ffn1=sm1[3:4], norm_mix=sm2[3:4], lb0=sm_hg[0:1], hgrn_g=sm_hg[1:2], conv_b=sm_cv[0:1],
                  conv_ln_g=sm_cv[1:2], conv_ln_b=sm_cv[2:3], norm_ffn2=sm3[3:4], norm_final=sm_head[0:1])
    recv = dict(ffn1_w_in=r1_in, ffn1_w_out=r1_out, mix_w_in=rm_in, hgrn_w_o=rh_o, conv_w=rcw, conv_w_o=rc_o,
                mix_w_out=rm_o, ffn2_w_in=r2_in, ffn2_w_out=r2_out)
    return sm_head[1, 0], dx0, dmod, gsmall, recv


def _pair_add(mine, theirs, core, name):
    _, R, C = theirs.shape

    def body(core_ref, a_ref, b_ref, out_ref):
        del core_ref
        out_ref[0] = (a_ref[0, 0].astype(F32) + b_ref[0].astype(F32)).astype(out_ref.dtype)

    blk = pl.BlockSpec((1, R, C), lambda s, core_ref: (s, 0, 0))
    grid_spec = pltpu.PrefetchScalarGridSpec(
        num_scalar_prefetch=1, grid=(N_CHIP,),
        in_specs=[pl.BlockSpec((1, 1, R, C), lambda s, core_ref: (s, core_ref[0], 0, 0)), blk], out_specs=blk)
    return pl.pallas_call(body, name=name, grid_spec=grid_spec,
                          out_shape=jax.ShapeDtypeStruct(theirs.shape, mine.dtype), compiler_params=_cparams(1),
                          )(core, mine.reshape(N_CHIP, 2, R, C), theirs)


def _pair_reduce(arrays, name):
    theirs = _exchange(arrays, "pair", name)
    core = lax.axis_index("c").astype(jnp.int32).reshape(1)
    return [_pair_add(a, t, core, "%s_add%d" % (name, i)) for i, (a, t) in enumerate(zip(arrays, theirs))]


SMALL_ORDER = ("norm_ffn1", "norm_mix", "lb0", "hgrn_g", "conv_b", "conv_ln_g", "conv_ln_b", "norm_ffn2",
               "norm_final")
PACK_ROWS = 24


def kernel(x, c, ada_w, ada_b, norm_ffn1, ffn1_w_in, ffn1_w_out, norm_mix, mix_w_in, hgrn_lb, hgrn_g, hgrn_w_o, conv_w, conv_b, conv_ln_g, conv_ln_b, conv_w_o, mix_w_out, norm_ffn2, ffn2_w_in, ffn2_w_out, norm_final, loss_target, m_ada_w, m_ada_b, m_norm_ffn1, m_ffn1_w_in, m_ffn1_w_out, m_norm_mix, m_mix_w_in, m_hgrn_lb, m_hgrn_g, m_hgrn_w_o, m_conv_w, m_conv_b, m_conv_ln_g, m_conv_ln_b, m_conv_w_o, m_mix_w_out, m_norm_ffn2, m_ffn2_w_in, m_ffn2_w_out, m_norm_final, v_ada_w, v_ada_b, v_norm_ffn1, v_ffn1_w_in, v_ffn1_w_out, v_norm_mix, v_mix_w_in, v_hgrn_lb, v_hgrn_g, v_hgrn_w_o, v_conv_w, v_conv_b, v_conv_ln_g, v_conv_ln_b, v_conv_w_o, v_mix_w_out, v_norm_ffn2, v_ffn2_w_in, v_ffn2_w_out, v_norm_final):
    mx, my, mc = _me()
    me = 4 * mx + 2 * my + mc
    ncol = ada_w.shape[2]

    sh = dict(ffn1_w_out=ffn1_w_out, mix_w_in=mix_w_in, hgrn_w_o=hgrn_w_o, conv_w_o=conv_w_o,
              mix_w_out=mix_w_out, ffn2_w_out=ffn2_w_out)
    sh = {n: w[0].astype(MM) for n, w in sh.items()}
    sh["ffn1_w_in"] = ffn1_w_in[0].T.astype(MM)
    sh["ffn2_w_in"] = ffn2_w_in[0].T.astype(MM)
    sh["conv_w"] = conv_w[0]
    small = dict(norm_ffn1=norm_ffn1, norm_mix=norm_mix, hgrn_lb=hgrn_lb, hgrn_g=hgrn_g, conv_b=conv_b,
                 conv_ln_g=conv_ln_g, conv_ln_b=conv_ln_b, norm_ffn2=norm_ffn2, norm_final=norm_final.reshape(1, D))

    cs = jnp.broadcast_to(c * jax.nn.sigmoid(c), (8, D))
    ada_b_cols = lax.dynamic_slice(ada_b, (0, me * ncol), (1, ncol))
    cs_all, mod_all, w1 = _prologue(cs, ada_w[0], ada_b_cols, [sh["ffn1_w_in"], sh["ffn1_w_out"]])
    cs_all = cs_all.reshape(N_DEV, 8, D)[:, 0, :]
    mod = lax.dynamic_index_in_dim(mod_all.reshape(N_DEV, N_DEV, ncol), me, axis=1, keepdims=False).reshape(9, D)

    loss_local, dx, dmod, gsmall, recv = _local_step(x[0], loss_target[0], mod, small, sh, w1)
    loss = lax.psum(loss_local, ("x", "y", "c"))

    pack = jnp.concatenate([dmod] + [gsmall[n] for n in SMALL_ORDER]
                           + [jnp.zeros((PACK_ROWS - 9 - len(SMALL_ORDER), D), F32)], axis=0)
    pack_all = _allgather_small(pack).reshape(N_DEV, PACK_ROWS, D)
    tot = _sum_slots(pack_all, "sum_small", PACK_ROWS)
    gs = {n: tot[9 + i:10 + i] for i, n in enumerate(SMALL_ORDER)}
    dmod_all = pack_all[:, 0:9, :].reshape(N_DEV, 9 * D)
    g_ada_b = tot[0:9].reshape(1, 9 * D)
    g_ada_w = _ada_wgrad(cs_all, lax.dynamic_slice(dmod_all, (0, me * ncol), (N_DEV, ncol)))
    z = hgrn_lb.astype(F32)
    p0 = jax.nn.sigmoid(z[0:1] - z[1:2])
    dz0 = p0 * (1.0 - p0) * gs["lb0"]
    g_hgrn_lb = jnp.concatenate([dz0, -dz0], axis=0)

    res = {}
    res["ada_w"] = _adamw(ada_w[0], m_ada_w[0], v_ada_w[0], g_ada_w, "adamw_ada_w")
    big = dict(ffn1_w_in=(ffn1_w_in, m_ffn1_w_in, v_ffn1_w_in), ffn1_w_out=(ffn1_w_out, m_ffn1_w_out, v_ffn1_w_out),
               mix_w_in=(mix_w_in, m_mix_w_in, v_mix_w_in), hgrn_w_o=(hgrn_w_o, m_hgrn_w_o, v_hgrn_w_o),
               conv_w=(conv_w, m_conv_w, v_conv_w), conv_w_o=(conv_w_o, m_conv_w_o, v_conv_w_o),
               mix_w_out=(mix_w_out, m_mix_w_out, v_mix_w_out), ffn2_w_in=(ffn2_w_in, m_ffn2_w_in, v_ffn2_w_in),
               ffn2_w_out=(ffn2_w_out, m_ffn2_w_out, v_ffn2_w_out))
    for n, (w, m, v) in big.items():
        g = recv[n]
        if n in ("ffn1_w_in", "ffn2_w_in"):
            g = _sum_slots(g, "sum_" + n, g.shape[1] // 4).T
        res[n] = _adamw(w[0], m[0], v[0], g, "adamw_" + n)
    sm_names = ("ada_b", "norm_ffn1", "norm_mix", "hgrn_lb", "hgrn_g", "conv_b", "conv_ln_g", "conv_ln_b",
                "norm_ffn2", "norm_final")
    sm_w = dict(ada_b=(ada_b, m_ada_b, v_ada_b), norm_ffn1=(norm_ffn1, m_norm_ffn1, v_norm_ffn1),
                norm_mix=(norm_mix, m_norm_mix, v_norm_mix), hgrn_lb=(hgrn_lb, m_hgrn_lb, v_hgrn_lb),
                hgrn_g=(hgrn_g, m_hgrn_g, v_hgrn_g), conv_b=(conv_b, m_conv_b, v_conv_b),
                conv_ln_g=(conv_ln_g, m_conv_ln_g, v_conv_ln_g), conv_ln_b=(conv_ln_b, m_conv_ln_b, v_conv_ln_b),
                norm_ffn2=(norm_ffn2, m_norm_ffn2, v_norm_ffn2), norm_final=(norm_final, m_norm_final, v_norm_final))
    sm_g = dict(gs, ada_b=g_ada_b, hgrn_lb=g_hgrn_lb)
    rows = {n: sm_w[n][0].size // D for n in sm_names}
    n_rows = sum(rows.values())
    pad = (-n_rows) % 8
    stack = lambda parts: jnp.concatenate([q.reshape(-1, D) for q in parts] + [jnp.ones((pad, D), F32)], axis=0)
    st = _adamw(stack([sm_w[n][0] for n in sm_names]), stack([sm_w[n][1] for n in sm_names]),
                stack([sm_w[n][2] for n in sm_names]), stack([sm_g[n] for n in sm_names]), "adamw_small")
    off = 0
    for n in sm_names:
        res[n] = tuple(t[off:off + rows[n]].reshape(sm_w[n][0].shape) for t in st)
        off += rows[n]

    order = ("ada_w", "ada_b", "norm_ffn1", "ffn1_w_in", "ffn1_w_out", "norm_mix", "mix_w_in", "hgrn_lb", "hgrn_g",
             "hgrn_w_o", "conv_w", "conv_b", "conv_ln_g", "conv_ln_b", "conv_w_o", "mix_w_out", "norm_ffn2",
             "ffn2_w_in", "ffn2_w_out", "norm_final")
    lead = lambda n, t: t[None] if n in big or n == "ada_w" else t
    outs = [loss, dx[None]]
    for j in range(4):
        outs += [lead(n, res[n][j]) for n in order]
    return tuple(outs)
```

```python
import jax
import jax.numpy as jnp
from jax import lax
from jax.experimental import pallas as pl
from jax.experimental.pallas import tpu as pltpu

F32 = jnp.float32
MM = jnp.bfloat16
ACT = jnp.bfloat16

D = 1024
D_FF = 2816
HEADS = 8
HD = 128
CHUNK = 64
SUB = 16
NSUB = CHUNK // SUB
HGRN_BLOCK = 1024
DIAG_SAFE_EXP = 60.0
CONV_K = 31
HALO = 32
EPS = 1e-6
N_DEV = 8
NEG = -1e30
Q_SCALE = HD ** -0.5

ADAM_LR = 0.001
ADAM_B1 = 0.9
ADAM_B2 = 0.999
ADAM_EPS = 1e-08
ADAM_WD = 0.01
ADAM_STEP = 10

V7X_VMEM_BYTES = 64 * 1024 * 1024
VMEM_LIMIT = V7X_VMEM_BYTES - 4 * 1024 * 1024
MESH = pl.DeviceIdType.MESH


def _cparams(n_axes):
    return pltpu.CompilerParams(dimension_semantics=("arbitrary",) * n_axes, vmem_limit_bytes=VMEM_LIMIT)


def _mm(a, b):
    return lax.dot_general(a.astype(MM), b.astype(MM), (((1,), (0,)), ((), ())), preferred_element_type=F32)


def _mm_nt(a, b):
    return lax.dot_general(a.astype(MM), b.astype(MM), (((1,), (1,)), ((), ())), preferred_element_type=F32)


def _mm_tn(a, b):
    return lax.dot_general(a.astype(MM), b.astype(MM), (((0,), (0,)), ((), ())), preferred_element_type=F32)


def _sig(x):
    return 1.0 / (1.0 + jnp.exp(-x))


def _colsum(x):
    return jnp.sum(x, axis=0, keepdims=True)


def _rowmean(x):
    return jnp.mean(x, axis=-1, keepdims=True)


def _modnorm_fwd(xv, g, sh, sc):
    r = lax.rsqrt(_rowmean(xv * xv) + EPS)
    xh = xv * r
    n = xh * g
    return n * (1.0 + sc) + sh, xh, n, r


def _modnorm_bwd(dh, xh, n, r, g, sc):
    dsc = _colsum(dh * n)
    dsh = _colsum(dh)
    dn = dh * (1.0 + sc)
    dg = _colsum(dn * xh)
    dxh = dn * g
    dx = r * (dxh - xh * _rowmean(dxh * xh))
    return dx, dsh, dsc, dg


def _ffn_fwd(x, mod, mo, gnorm, w_in_t, w_out, res, name, carry):
    T = x.shape[0]
    tm = min(512, T)
    tn = D_FF // 2

    def body(x_ref, mod_ref, g_ref, wi_ref, wo_ref, xo_ref, a_ref, b_ref, f_ref, h_ref):
        xv = x_ref[...]
        h, _, _, _ = _modnorm_fwd(xv, g_ref[...], mod_ref[mo:mo + 1, :], mod_ref[mo + 1:mo + 2, :])
        h = h.astype(ACT)
        h_ref[...] = h
        f = None
        for c0 in range(0, D_FF, tn):
            a = _mm_nt(h, wi_ref[0, c0:c0 + tn, :])
            b = _mm_nt(h, wi_ref[1, c0:c0 + tn, :])
            a_ref[:, c0:c0 + tn] = a.astype(ACT)
            b_ref[:, c0:c0 + tn] = b.astype(ACT)
            part = _mm(a * _sig(a) * b, wo_ref[c0:c0 + tn, :])
            f = part if f is None else f + part
        f_ref[...] = f
        xo_ref[...] = xv + res * mod_ref[mo + 2:mo + 3, :] * f

    tile = pl.BlockSpec((tm, D), lambda i: (i, 0))
    wide = pl.BlockSpec((tm, D_FF), lambda i: (i, 0))
    out = _gridded(
        body, carry, name=name, grid=(T // tm,),
        in_specs=[
            tile,
            pl.BlockSpec((9, D), lambda i: (0, 0)),
            pl.BlockSpec((1, D), lambda i: (0, 0)),
            pl.BlockSpec((2, D_FF, D), lambda i: (0, 0, 0), pipeline_mode=pl.Buffered(1)),
            pl.BlockSpec((D_FF, D), lambda i: (0, 0), pipeline_mode=pl.Buffered(1)),
        ],
        out_specs=[tile, wide, wide, tile, tile],
        out_shape=[
            jax.ShapeDtypeStruct((T, D), F32),
            jax.ShapeDtypeStruct((T, D_FF), ACT),
            jax.ShapeDtypeStruct((T, D_FF), ACT),
            jax.ShapeDtypeStruct((T, D), F32),
            jax.ShapeDtypeStruct((T, D), ACT),
        ],
    )(x, mod, gnorm, w_in_t, w_out)
    return out[:5], out[5:]


def _ffn_bwd_w(h, df, a, b, w_out, name, carry):
    T = h.shape[0]
    tm = min(2048, T)
    ni = T // tm
    tn = 256
    nj = D_FF // tn

    def body(h_ref, df_ref, a_ref, b_ref, wo_ref, da_ref, db_ref, dwi_ref, dwo_ref, acc_i, acc_o):
        i = pl.program_id(1)

        @pl.when(i == 0)
        def _():
            acc_i[...] = jnp.zeros_like(acc_i)
            acc_o[...] = jnp.zeros_like(acc_o)

        hb = h_ref[...]
        df = df_ref[...]
        av = a_ref[...].astype(F32)
        bv = b_ref[...].astype(F32)
        sg = _sig(av)
        sa = av * sg
        s = (sa * bv).astype(MM)
        ds = _mm_nt(df, wo_ref[...])
        da = (ds * bv * sg * (1.0 + av * (1.0 - sg))).astype(MM)
        db = (ds * sa).astype(MM)
        da_ref[...] = da
        db_ref[...] = db
        acc_o[...] += _mm_tn(s, df)
        acc_i[0] += _mm_tn(da, hb)
        acc_i[1] += _mm_tn(db, hb)

        @pl.when(i == ni - 1)
        def _():
            dwi_ref[...] = acc_i[...].astype(MM)
            dwo_ref[...] = acc_o[...].astype(MM)

    out = _gridded(
        body, carry, name=name, grid=(nj, ni),
        in_specs=[
            pl.BlockSpec((tm, D), lambda j, i: (i, 0)),
            pl.BlockSpec((tm, D), lambda j, i: (i, 0)),
            pl.BlockSpec((tm, tn), lambda j, i: (i, j)),
            pl.BlockSpec((tm, tn), lambda j, i: (i, j)),
            pl.BlockSpec((tn, D), lambda j, i: (j, 0)),
        ],
        out_specs=[
            pl.BlockSpec((tm, tn), lambda j, i: (i, j)),
            pl.BlockSpec((tm, tn), lambda j, i: (i, j)),
            pl.BlockSpec((2, tn, D), lambda j, i: (0, j, 0)),
            pl.BlockSpec((tn, D), lambda j, i: (j, 0)),
        ],
        out_shape=[
            jax.ShapeDtypeStruct((T, D_FF), MM),
            jax.ShapeDtypeStruct((T, D_FF), MM),
            jax.ShapeDtypeStruct((2, D_FF, D), MM),
            jax.ShapeDtypeStruct((D_FF, D), MM),
        ],
        scratch_shapes=[pltpu.VMEM((2, tn, D), F32), pltpu.VMEM((tn, D), F32)],
    )(h, df, a, b, w_out)
    return out[:4], out[4:]


def _ffn_bwd_x(x, dxo, f, da, db, mod, mo, gnorm, w_in_t, res, name, carry):
    T = x.shape[0]
    tm = min(512, T)
    ni = T // tm
    tn = D_FF // 2
    nj = D_FF // tn

    def body(x_ref, dxo_ref, f_ref, da_ref, db_ref, mod_ref, g_ref, wi_ref, dx_ref, sm_ref, dh_scr):
        j = pl.program_id(0)
        i = pl.program_id(1)

        @pl.when((j == 0) & (i == 0))
        def _():
            sm_ref[...] = jnp.zeros_like(sm_ref)

        @pl.when(j == 0)
        def _():
            dh_scr[i] = jnp.zeros((tm, D), F32)

        dh_scr[i] += _mm(da_ref[...], wi_ref[0]) + _mm(db_ref[...], wi_ref[1])

        @pl.when(j == nj - 1)
        def _():
            sc = mod_ref[mo + 1:mo + 2, :]
            _, xh, n, r = _modnorm_fwd(x_ref[...], g_ref[...], mod_ref[mo:mo + 1, :], sc)
            dxn, dsh, dsc, dg = _modnorm_bwd(dh_scr[i], xh, n, r, g_ref[...], sc)
            dxo_v = dxo_ref[...]
            dx_ref[...] = dxo_v + dxn
            sm_ref[0:1, :] += dsh
            sm_ref[1:2, :] += dsc
            sm_ref[2:3, :] += _colsum(dxo_v * f_ref[...]) * res
            sm_ref[3:4, :] += dg

    last = pl.BlockSpec((tm, D), lambda j, i: (jnp.where(j == nj - 1, i, 0), 0))
    out = _gridded(
        body, carry, name=name, grid=(nj, ni),
        in_specs=[last, last, last,
                  pl.BlockSpec((tm, tn), lambda j, i: (i, j)), pl.BlockSpec((tm, tn), lambda j, i: (i, j)),
                  pl.BlockSpec((9, D), lambda j, i: (0, 0)), pl.BlockSpec((1, D), lambda j, i: (0, 0)),
                  pl.BlockSpec((2, tn, D), lambda j, i: (0, j, 0))],
        out_specs=[last, pl.BlockSpec((8, D), lambda j, i: (0, 0))],
        out_shape=[jax.ShapeDtypeStruct((T, D), F32), jax.ShapeDtypeStruct((8, D), F32)],
        scratch_shapes=[pltpu.VMEM((ni, tm, D), F32)],
    )(x, dxo, f, da, db, mod, gnorm, w_in_t)
    return out[:2], out[2:]


def _head(x, target, gfin, mod, gate_row, res):
    T = x.shape[0]
    tm = min(512, T)
    ni = T // tm

    def body(x_ref, t_ref, g_ref, mod_ref, dx_ref, df_ref, sm_ref):
        i = pl.program_id(0)

        @pl.when(i == 0)
        def _():
            sm_ref[...] = jnp.zeros_like(sm_ref)

        xv = x_ref[...]
        g = g_ref[...]
        r = lax.rsqrt(_rowmean(xv * xv) + EPS)
        xh = xv * r
        e = xh * g - t_ref[...]
        sm_ref[1:2, :] += _colsum(e * e) * (0.5 / D)
        dy = e * (1.0 / D)
        sm_ref[0:1, :] += _colsum(dy * xh)
        dxh = dy * g
        dx = r * (dxh - xh * _rowmean(dxh * xh))
        dx_ref[...] = dx
        df_ref[...] = (res * mod_ref[gate_row:gate_row + 1, :] * dx).astype(MM)

        @pl.when(i == ni - 1)
        def _():
            sm_ref[1:2, :] = jnp.broadcast_to(jnp.sum(sm_ref[1:2, :], axis=-1, keepdims=True), (1, D))

    tile = pl.BlockSpec((tm, D), lambda i: (i, 0))
    return pl.pallas_call(
        body, name="head_loss", grid=(ni,),
        in_specs=[tile, tile, pl.BlockSpec((1, D), lambda i: (0, 0)), pl.BlockSpec((9, D), lambda i: (0, 0))],
        out_specs=[tile, tile, pl.BlockSpec((8, D), lambda i: (0, 0))],
        out_shape=[jax.ShapeDtypeStruct((T, D), F32), jax.ShapeDtypeStruct((T, D), MM),
                   jax.ShapeDtypeStruct((8, D), F32)],
        compiler_params=_cparams(1),
    )(x, target, gfin, mod)


def _mixin_fwd(x, mod, mo, gnorm, w, carry):
    T = x.shape[0]
    tm = min(1024, T)
    ni = T // tm

    def body(x_ref, mod_ref, g_ref, w_ref, p_ref, h_ref, h_all):
        i = pl.program_id(1)

        @pl.when(pl.program_id(0) == 0)
        def _():
            h, _, _, _ = _modnorm_fwd(x_ref[...], g_ref[...], mod_ref[mo:mo + 1, :], mod_ref[mo + 1:mo + 2, :])
            h_all[i] = h.astype(ACT)
            h_ref[...] = h.astype(ACT)

        p_ref[0] = _mm(h_all[i], w_ref[0])

    first = lambda k, i: (jnp.where(k == 0, i, ni - 1), 0)
    out = _gridded(
        body, carry, name="mixin_fwd", grid=(8, ni),
        in_specs=[pl.BlockSpec((tm, D), first), pl.BlockSpec((9, D), lambda k, i: (0, 0)),
                  pl.BlockSpec((1, D), lambda k, i: (0, 0)), pl.BlockSpec((1, D, D), lambda k, i: (k, 0, 0))],
        out_specs=[pl.BlockSpec((1, tm, D), lambda k, i: (k, i, 0)), pl.BlockSpec((tm, D), first)],
        out_shape=[jax.ShapeDtypeStruct((8, T, D), F32), jax.ShapeDtypeStruct((T, D), ACT)],
        scratch_shapes=[pltpu.VMEM((ni, tm, D), ACT)],
    )(x, mod, gnorm, w)
    return out[:2], out[2:]


def _mixin_bwd(x, h, dxo, dp, mod, mo, gnorm, w, next_gate, next_res, carry):
    T = x.shape[0]
    tm = min(512, T)
    ni = T // tm

    def body(x_ref, h_ref, dxo_ref, dp_ref, mod_ref, g_ref, w_ref, dx_ref, dw_ref, sm_ref, df_ref, dh_scr, acc):
        k = pl.program_id(0)
        i = pl.program_id(1)

        @pl.when(i == 0)
        def _():
            acc[...] = jnp.zeros_like(acc)

        @pl.when(k == 0)
        def _():
            dh_scr[i] = jnp.zeros((tm, D), F32)

        @pl.when((k == 0) & (i == 0))
        def _():
            sm_ref[...] = jnp.zeros_like(sm_ref)

        dpk = dp_ref[0].astype(MM)
        acc[...] += _mm_tn(h_ref[...], dpk)
        dh_scr[i] += _mm_nt(dpk, w_ref[0])

        @pl.when(i == ni - 1)
        def _():
            dw_ref[0] = acc[...].astype(MM)

        @pl.when(k == 7)
        def _():
            sc = mod_ref[mo + 1:mo + 2, :]
            _, xh, n, r = _modnorm_fwd(x_ref[...], g_ref[...], mod_ref[mo:mo + 1, :], sc)
            dxn, dsh, dsc, dg = _modnorm_bwd(dh_scr[i], xh, n, r, g_ref[...], sc)
            dx = dxo_ref[...] + dxn
            dx_ref[...] = dx
            df_ref[...] = (next_res * mod_ref[next_gate:next_gate + 1, :] * dx).astype(MM)
            sm_ref[0:1, :] += dsh
            sm_ref[1:2, :] += dsc
            sm_ref[3:4, :] += dg

    last = pl.BlockSpec((tm, D), lambda k, i: (jnp.where(k == 7, i, 0), 0))
    out = _gridded(
        body, carry, name="mixin_bwd", grid=(8, ni),
        in_specs=[pl.BlockSpec((tm, D), lambda k, i: (jnp.where(k == 7, i, 0), 0)),
                  pl.BlockSpec((tm, D), lambda k, i: (i, 0)),
                  pl.BlockSpec((tm, D), lambda k, i: (jnp.where(k == 7, i, 0), 0)),
                  pl.BlockSpec((1, tm, D), lambda k, i: (k, i, 0)), pl.BlockSpec((9, D), lambda k, i: (0, 0)),
                  pl.BlockSpec((1, D), lambda k, i: (0, 0)), pl.BlockSpec((1, D, D), lambda k, i: (k, 0, 0))],
        out_specs=[last, pl.BlockSpec((1, D, D), lambda k, i: (k, 0, 0)), pl.BlockSpec((8, D), lambda k, i: (0, 0)),
                   last],
        out_shape=[jax.ShapeDtypeStruct((T, D), F32), jax.ShapeDtypeStruct((8, D, D), MM),
                   jax.ShapeDtypeStruct((8, D), F32), jax.ShapeDtypeStruct((T, D), MM)],
        scratch_shapes=[pltpu.VMEM((ni, tm, D), F32), pltpu.VMEM((D, D), F32)],
    )(x, h, dxo, dp, mod, gnorm, w)
    return out[:4], out[4:]


def _hgrn_consts():
    rows = jnp.arange(SUB * HD) // HD
    e = (rows[:, None] == jnp.arange(HD)[None, :]).astype(MM)
    return e, e.T


def _rows_bcast(ref, cb, first, n):
    parts = [jnp.broadcast_to(ref[pl.ds(c * CHUNK + first, 1), :], (n, HD)) for c in range(cb // CHUNK)]
    return jnp.concatenate(parts, axis=0)


def _hgrn_pre(qr, fr, lb_ref, b_scr, cb):
    z = lb_ref[...]
    lb = _sig(z[0:1, :] - z[1:2, :])
    sq = _sig(qr)
    q = qr * sq * Q_SCALE
    sf = _sig(fr)
    fg = lb + (1.0 - lb) * sf
    lf = jnp.log(fg)
    k = 1.0 - fg
    tl = lax.broadcasted_iota(jnp.int32, (cb, HD), 0) % CHUNK
    bc = lf
    sh = 1
    while sh < CHUNK:
        bc = bc + jnp.where(tl >= sh, pltpu.roll(bc, sh, 0), 0.0)
        sh *= 2
    b_scr[...] = bc
    bl = _rows_bcast(b_scr, cb, CHUNK - 1, CHUNK)
    br = [None] + [_rows_bcast(b_scr, cb, SUB * i - 1, CHUNK) for i in range(1, NSUB)]
    sb = tl // SUB
    bref = jnp.where(sb == 0, bc, jnp.where(sb == 1, br[1], jnp.where(sb == 2, br[2], br[3])))
    eb = jnp.exp(bc)
    ekd = jnp.exp(bl - bc)
    eqo = jnp.exp(bc - bref)
    eko = [None] + [jnp.exp(jnp.where(tl < SUB * i, br[i] - bc, NEG)) for i in range(1, NSUB)]
    starts = [jnp.broadcast_to(b_scr[pl.ds(r, 1), :], (SUB, HD)) for r in range(0, cb, SUB)]
    grow = jnp.concatenate(starts, axis=0) - bc
    edq = jnp.exp(-grow)
    edk = jnp.exp(jnp.minimum(grow, DIAG_SAFE_EXP))
    return dict(lb=lb, sq=sq, q=q, sf=sf, fg=fg, k=k, tl=tl, sb=sb, b=bc, bl=bl, eb=eb, ekd=ekd, eqo=eqo,
                eko=eko, qe=q * eb, kd=k * ekd, qo=q * eqo, ko=[None] + [k * eko[i] for i in range(1, NSUB)],
                edq=edq, edk=edk, qd=q * edq, kdd=k * edk, diag_safe=jnp.max(grow) < DIAG_SAFE_EXP)


def _pad_rows(x):
    return jnp.concatenate([x, jnp.zeros_like(x)], axis=0)


def _by_subblock(sbc, parts):
    out = jnp.zeros_like(parts[1])
    for i in range(1, NSUB):
        out = jnp.where(sbc == i, parts[i], out)
    return out


def _hgrn_fwd(p, hgrn_lb, hgrn_g, carry):
    T = p.shape[1]
    cb = min(HGRN_BLOCK, T)
    nch = cb // CHUNK
    ncb = T // cb
    e_mat, _ = _hgrn_consts()

    def body(p_ref, lb_ref, g_ref, e_ref, o_ref, oa_ref, a_ref, s_ref, st_scr, q_scr, k_scr, b_scr, z_scr, ad_scr):
        @pl.when(pl.program_id(1) == 0)
        def _():
            st_scr[...] = jnp.zeros_like(st_scr)

        v = p_ref[2]
        og = p_ref[3]
        pre = _hgrn_pre(p_ref[0], p_ref[1], lb_ref, b_scr, cb)
        chunks = [slice(c * CHUNK, (c + 1) * CHUNK) for c in range(nch)]
        row_i = lax.broadcasted_iota(jnp.int32, (CHUNK, HD), 0)
        lane_i = lax.broadcasted_iota(jnp.int32, (CHUNK, HD), 1)
        sbc = row_i // SUB
        own_block = (lane_i // SUB == sbc) & (lane_i <= row_i)

        @pl.when(pre["diag_safe"])
        def _():
            for rs in chunks:
                ad_scr[rs, :] = jnp.where(own_block, _mm_nt(pre["qd"][rs], _pad_rows(pre["kdd"][rs])), 0.0)

        @pl.when(jnp.logical_not(pre["diag_safe"]))
        def _():
            q_scr[...] = pre["q"]
            k_scr[...] = pre["k"]
            ti = lax.broadcasted_iota(jnp.int32, (SUB, HD), 0)

            def zbody(c, carry):
                for i in range(NSUB):
                    r0 = pl.multiple_of(c * CHUNK + SUB * i, SUB)
                    qi = q_scr[pl.ds(r0, SUB), :]
                    bi = b_scr[pl.ds(r0, SUB), :]
                    for s in range(SUB):
                        krow = k_scr[pl.ds(r0 + s, 1), :]
                        brow = b_scr[pl.ds(r0 + s, 1), :]
                        if s < 8:
                            zz = qi * krow * jnp.exp(jnp.where(ti >= s, bi - brow, NEG))
                        else:
                            lo = qi[8:] * krow * jnp.exp(jnp.where(ti[8:] >= s, bi[8:] - brow, NEG))
                            zz = jnp.concatenate([jnp.zeros((8, HD), F32), lo], axis=0)
                        z_scr[i, pl.ds(pl.multiple_of(c * SUB, SUB), SUB), s * HD:(s + 1) * HD] = zz.astype(MM)
                return carry

            lax.fori_loop(0, nch, zbody, 0)
            adiag = [_mm(z_scr[i], e_ref[...]) for i in range(NSUB)]
            for c, rs in enumerate(chunks):
                dparts = []
                for i in range(NSUB):
                    blk = adiag[i][c * SUB:(c + 1) * SUB]
                    dparts.append(blk if i == 0 else pltpu.roll(blk, SUB * i, 1))
                ad_scr[rs, :] = jnp.concatenate(dparts, axis=0)

        offs = [[_mm_nt(pre["qo"][rs], _pad_rows(pre["ko"][i][rs])) for i in range(1, NSUB)] for rs in chunks]
        kv = [_mm_tn(v[rs], pre["kd"][rs]) for rs in chunks]
        a_parts = [_by_subblock(sbc, [None] + offs[c]) + ad_scr[rs, :] for c, rs in enumerate(chunks)]
        a_ref[0] = jnp.concatenate(a_parts, axis=0)
        o_intra = [_mm(a_parts[c], _pad_rows(v[rs])) for c, rs in enumerate(chunks)]
        states = []
        st = st_scr[...]
        for c in range(nch):
            states.append(st)
            st = st * jnp.exp(b_scr[pl.ds(c * CHUNK + CHUNK - 1, 1), :]) + kv[c]
        st_scr[...] = st
        for c in range(nch):
            s_ref[0, c] = states[c]
        o = jnp.concatenate([o_intra[c] + _mm_nt(pre["qe"][rs], states[c]) for c, rs in enumerate(chunks)], axis=0)
        o_ref[...] = o
        on = o * lax.rsqrt(_rowmean(o * o) + EPS) * g_ref[...]
        oa_ref[...] = (on * og * _sig(og)).astype(ACT)

    out = _gridded(
        body, carry, name="hgrn_fwd", grid=(HEADS, ncb),
        in_specs=[pl.BlockSpec((4, cb, HD), lambda h, c: (0, c, h)),
                  pl.BlockSpec((2, HD), lambda h, c: (0, h)),
                  pl.BlockSpec((1, HD), lambda h, c: (0, h)),
                  pl.BlockSpec((SUB * HD, HD), lambda h, c: (0, 0))],
        out_specs=[pl.BlockSpec((cb, HD), lambda h, c: (c, h)),
                   pl.BlockSpec((cb, HD), lambda h, c: (c, h)),
                   pl.BlockSpec((1, cb, HD), lambda h, c: (h, c, 0)),
                   pl.BlockSpec((1, nch, HD, HD), lambda h, c: (h, c, 0, 0))],
        out_shape=[jax.ShapeDtypeStruct((T, D), F32), jax.ShapeDtypeStruct((T, D), ACT),
                   jax.ShapeDtypeStruct((HEADS, T, HD), F32),
                   jax.ShapeDtypeStruct((HEADS, T // CHUNK, HD, HD), F32)],
        scratch_shapes=[pltpu.VMEM((HD, HD), F32), pltpu.VMEM((cb, HD), F32), pltpu.VMEM((cb, HD), F32),
                        pltpu.VMEM((cb, HD), F32), pltpu.VMEM((NSUB, nch * SUB, SUB * HD), MM),
                        pltpu.VMEM((cb, HD), F32)],
    )(p, hgrn_lb, hgrn_g, e_mat)
    return out[:4], out[4:]


def _hgrn_bwd(p, o, a_all, s_all, doa, hgrn_lb, hgrn_g, dp, carry):
    T = p.shape[1]
    cb = min(HGRN_BLOCK, T)
    nch = cb // CHUNK
    ncb = T // cb
    _, et_mat = _hgrn_consts()

    def body(p_ref, o_ref, a_ref, s_ref, doa_ref, lb_ref, g_ref, et_ref, dp_in, dp_ref, sm_ref,
             dst_scr, q_scr, k_scr, b_scr, x_scr, dqd_scr, dkd_scr):
        del dp_in

        @pl.when(pl.program_id(1) == 0)
        def _():
            dst_scr[...] = jnp.zeros_like(dst_scr)
            sm_ref[...] = jnp.zeros_like(sm_ref)

        qr = p_ref[0]
        v = p_ref[2]
        og = p_ref[3]
        pre = _hgrn_pre(qr, p_ref[1], lb_ref, b_scr, cb)
        q, k = pre["q"], pre["k"]
        g = g_ref[...]
        ov = o_ref[...]
        r = lax.rsqrt(_rowmean(ov * ov) + EPS)
        oh = ov * r
        sgo = _sig(og)
        doa_v = doa_ref[...]
        don = doa_v * og * sgo
        dog = doa_v * oh * g * sgo * (1.0 + og * (1.0 - sgo))
        sm_ref[1:2, :] += _colsum(don * oh)
        doh = don * g
        do = r * (doh - oh * _rowmean(doh * oh))

        sbc = lax.broadcasted_iota(jnp.int32, (CHUNK, HD), 0) // SUB
        row_i = lax.broadcasted_iota(jnp.int32, (CHUNK, HD), 0)
        lane_i = lax.broadcasted_iota(jnp.int32, (CHUNK, HD), 1)
        causal = lane_i <= row_i
        chunks = [slice(c * CHUNK, (c + 1) * CHUNK) for c in range(nch)]
        da_parts = [jnp.where(causal, _mm_nt(do[rs], _pad_rows(v[rs])), 0.0) for rs in chunks]
        dv_parts = [_mm_tn(a_ref[0, rs, :], do[rs])[:CHUNK] for rs in chunks]
        dqoff_mm = [[_mm(da_parts[c], _pad_rows(pre["ko"][i][rs])) for i in range(1, NSUB)]
                    for c, rs in enumerate(chunks)]
        dkoff_mm = [[_mm_tn(jnp.where(sbc == i, da_parts[c], 0.0), pre["qo"][rs])[:CHUNK] for i in range(1, NSUB)]
                    for c, rs in enumerate(chunks)]
        dqoff_parts = [_by_subblock(sbc, [None] + dqoff_mm[c]) for c in range(nch)]
        dkoff_parts = []
        for c, rs in enumerate(chunks):
            dko = pre["eko"][1][rs] * dkoff_mm[c][0]
            for i in range(2, NSUB):
                dko = dko + pre["eko"][i][rs] * dkoff_mm[c][i - 1]
            dkoff_parts.append(dko)
        own_block = lane_i // SUB == sbc

        @pl.when(pre["diag_safe"])
        def _():
            hi = dict(preferred_element_type=F32, precision=lax.Precision.HIGHEST)
            for c, rs in enumerate(chunks):
                da_own = jnp.where(own_block, da_parts[c], 0.0)
                dqd_scr[rs, :] = pre["edq"][rs] * lax.dot_general(
                    da_own, _pad_rows(pre["kdd"][rs]), (((1,), (0,)), ((), ())), **hi)
                dkd_scr[rs, :] = pre["edk"][rs] * lax.dot_general(
                    da_own, pre["qd"][rs], (((0,), (0,)), ((), ())), **hi)[:CHUNK]

        @pl.when(jnp.logical_not(pre["diag_safe"]))
        def _():
            q_scr[...] = q
            k_scr[...] = k
            for i in range(NSUB):
                rows = []
                for c in range(nch):
                    blk = da_parts[c][SUB * i:SUB * (i + 1)]
                    rows.append(blk if i == 0 else pltpu.roll(blk, HD - SUB * i, 1))
                x_scr[i] = _mm(jnp.concatenate(rows, axis=0), et_ref[...])
            ti = lax.broadcasted_iota(jnp.int32, (SUB, HD), 0)

            def dbody(c, carry):
                for i in range(NSUB):
                    r0 = pl.multiple_of(c * CHUNK + SUB * i, SUB)
                    qi = q_scr[pl.ds(r0, SUB), :]
                    bi = b_scr[pl.ds(r0, SUB), :]
                    dq_hi = jnp.zeros((8, HD), F32)
                    dq_lo = jnp.zeros((8, HD), F32)
                    dk_hi = jnp.zeros((8, HD), F32)
                    dk_lo = jnp.zeros((8, HD), F32)
                    c0 = pl.multiple_of(c * SUB, SUB)
                    t8 = ti[:8]
                    for s in range(SUB):
                        krow = k_scr[pl.ds(r0 + s, 1), :]
                        brow = b_scr[pl.ds(r0 + s, 1), :]
                        w_lo = (x_scr[i, pl.ds(c0 + 8, 8), s * HD:(s + 1) * HD]
                                * jnp.exp(jnp.where(t8 + 8 >= s, bi[8:] - brow, NEG)))
                        dq_lo = dq_lo + w_lo * krow
                        col = _colsum(w_lo * qi[8:])
                        if s < 8:
                            w_hi = (x_scr[i, pl.ds(c0, 8), s * HD:(s + 1) * HD]
                                    * jnp.exp(jnp.where(t8 >= s, bi[:8] - brow, NEG)))
                            dq_hi = dq_hi + w_hi * krow
                            dk_hi = jnp.where(t8 == s, col + _colsum(w_hi * qi[:8]), dk_hi)
                        else:
                            dk_lo = jnp.where(t8 + 8 == s, col, dk_lo)
                    dqd_scr[pl.ds(r0, SUB), :] = jnp.concatenate([dq_hi, dq_lo], axis=0)
                    dkd_scr[pl.ds(r0, SUB), :] = jnp.concatenate([dk_hi, dk_lo], axis=0)
                return carry

            lax.fori_loop(0, nch, dbody, 0)

        qdo = [_mm_tn(do[rs], pre["qe"][rs]) for rs in chunks]
        dsts = [None] * nch
        dst = dst_scr[...]
        for c in reversed(range(nch)):
            dsts[c] = dst
            dst = dst * jnp.exp(b_scr[pl.ds(c * CHUNK + CHUNK - 1, 1), :]) + qdo[c]
        dst_scr[...] = dst
        sts = [s_ref[0, c] for c in range(nch)]
        dqe_parts = [_mm(do[rs], sts[c]) for c, rs in enumerate(chunks)]
        dkdec_parts = [_mm(v[rs], dsts[c]) for c, rs in enumerate(chunks)]
        dvi_parts = [_mm_nt(pre["kd"][rs], dsts[c]) for c, rs in enumerate(chunks)]
        debl_parts = [_colsum(dsts[c] * sts[c]) for c in range(nch)]
        dqe = jnp.concatenate(dqe_parts, axis=0)
        dkdec = jnp.concatenate(dkdec_parts, axis=0)
        dq_tot = jnp.concatenate(dqoff_parts, axis=0) * pre["eqo"] + dqd_scr[...] + dqe * pre["eb"]
        dk_inter = dkdec * pre["ekd"]
        dk_tot = jnp.concatenate(dkoff_parts, axis=0) + dkd_scr[...] + dk_inter
        db = q * dq_tot - k * dk_tot
        kdk = k * dk_inter
        dbl = jnp.concatenate(
            [jnp.broadcast_to(jnp.exp(b_scr[pl.ds(c * CHUNK + CHUNK - 1, 1), :]) * debl_parts[c]
                              + _colsum(kdk[c * CHUNK:(c + 1) * CHUNK]), (CHUNK, HD)) for c in range(nch)], axis=0)
        tl = pre["tl"]
        rc = db
        sh = 1
        while sh < CHUNK:
            rc = rc + jnp.where(tl + sh < CHUNK, pltpu.roll(rc, cb - sh, 0), 0.0)
            sh *= 2
        dlf = rc + dbl
        dfg = dlf / pre["fg"] - dk_tot
        sf = pre["sf"]
        lb = pre["lb"]
        sm_ref[0:1, :] += _colsum(dfg * (1.0 - sf))
        sq = pre["sq"]
        dp_ref[0] = (dq_tot * Q_SCALE * sq * (1.0 + qr * (1.0 - sq))).astype(ACT)
        dp_ref[1] = (dfg * (1.0 - lb) * sf * (1.0 - sf)).astype(ACT)
        dp_ref[2] = (jnp.concatenate(dv_parts, axis=0) + jnp.concatenate(dvi_parts, axis=0)).astype(ACT)
        dp_ref[3] = dog.astype(ACT)

    rev = lambda c: ncb - 1 - c
    out = _gridded(
        body, carry, name="hgrn_bwd", grid=(HEADS, ncb),
        in_specs=[pl.BlockSpec((4, cb, HD), lambda h, c: (0, rev(c), h)),
                  pl.BlockSpec((cb, HD), lambda h, c: (rev(c), h)),
                  pl.BlockSpec((1, cb, HD), lambda h, c: (h, rev(c), 0)),
                  pl.BlockSpec((1, nch, HD, HD), lambda h, c: (h, rev(c), 0, 0)),
                  pl.BlockSpec((cb, HD), lambda h, c: (rev(c), h)),
                  pl.BlockSpec((2, HD), lambda h, c: (0, h)),
                  pl.BlockSpec((1, HD), lambda h, c: (0, h)),
                  pl.BlockSpec((HD, SUB * HD), lambda h, c: (0, 0)),
                  pl.BlockSpec(memory_space=pl.ANY)],
        out_specs=[pl.BlockSpec((4, cb, HD), lambda h, c: (0, rev(c), h)),
                   pl.BlockSpec((8, HD), lambda h, c: (0, h))],
        out_shape=[jax.ShapeDtypeStruct(dp.shape, dp.dtype), jax.ShapeDtypeStruct((8, D), F32)],
        aliases={8: 0},
        scratch_shapes=[pltpu.VMEM((HD, HD), F32), pltpu.VMEM((cb, HD), F32), pltpu.VMEM((cb, HD), F32),
                        pltpu.VMEM((cb, HD), F32), pltpu.VMEM((NSUB, nch * SUB, SUB * HD), F32),
                        pltpu.VMEM((cb, HD), F32), pltpu.VMEM((cb, HD), F32)],
    )(p, o, a_all, s_all, doa, hgrn_lb, hgrn_g, et_mat, dp)
    return out[:2], out[2:]


def _ln_fwd(u1, g, b):
    mu = _rowmean(u1)
    xc = u1 - mu
    rs = lax.rsqrt(_rowmean(xc * xc) + EPS)
    xh = xc * rs
    return xh * g + b, xh, rs


CONV_RB = 64
LANES = 128


def _shift_rows(src, sh, ls, n):
    for r in range(1, 8):
        sh[r - 1, 0:n, :] = src[pl.ds(r, n), ls]


def _tap(src, sh, ls, off, r0, rows):
    r = off % 8
    if r == 0:
        return src[pl.ds(r0 + off, rows), ls]
    return sh[r - 1, pl.ds(r0 + off - r, rows), :]


def _conv_fwd(p, cw, cb_, lng, lnb):
    T = p.shape[1]
    tm = min(512, T)
    n = HALO + tm - 8

    def body(p_ref, cw_ref, cb_ref, g_ref, b_ref, u1_ref, u2_ref, buf, sh):
        @pl.when(pl.program_id(0) == 0)
        def _():
            buf[0:HALO, :] = jnp.zeros((HALO, D), F32)

        buf[HALO:HALO + tm, :] = p_ref[0] * _sig(p_ref[1])
        for lb in range(D // LANES):
            ls = slice(lb * LANES, (lb + 1) * LANES)
            _shift_rows(buf, sh, ls, n)
            taps = [cw_ref[j:j + 1, ls] for j in range(CONV_K)]
            bias = cb_ref[:, ls]

            def rows_body(rb, carry):
                r0 = pl.multiple_of(rb * CONV_RB, CONV_RB)
                acc = jnp.broadcast_to(bias, (CONV_RB, LANES))
                for j in range(CONV_K):
                    acc = acc + taps[j] * _tap(buf, sh, ls, HALO - (CONV_K - 1) + j, r0, CONV_RB)
                u1_ref[pl.ds(r0, CONV_RB), ls] = acc
                return carry

            lax.fori_loop(0, tm // CONV_RB, rows_body, 0)
        y, _, _ = _ln_fwd(u1_ref[...], g_ref[...], b_ref[...])
        u2_ref[...] = (y * _sig(y)).astype(ACT)
        buf[0:HALO, :] = buf[tm:tm + HALO, :]

    return pl.pallas_call(
        body, name="conv_fwd", grid=(T // tm,),
        in_specs=[pl.BlockSpec((2, tm, D), lambda i: (2, i, 0)), pl.BlockSpec((HALO, D), lambda i: (0, 0)),
                  pl.BlockSpec((1, D), lambda i: (0, 0)), pl.BlockSpec((1, D), lambda i: (0, 0)),
                  pl.BlockSpec((1, D), lambda i: (0, 0))],
        out_specs=[pl.BlockSpec((tm, D), lambda i: (i, 0)), pl.BlockSpec((tm, D), lambda i: (i, 0))],
        out_shape=[jax.ShapeDtypeStruct((T, D), F32), jax.ShapeDtypeStruct((T, D), ACT)],
        scratch_shapes=[pltpu.VMEM((HALO + tm, D), F32), pltpu.VMEM((7, n, LANES), F32)],
        compiler_params=_cparams(1),
    )(p, cw, cb_, lng, lnb)


def _conv_bwd(p, u1, du2, cw, lng, lnb, dp):
    T = p.shape[1]
    tm = min(512, T)
    ni = T // tm
    hb = tm // HALO

    n = HALO + tm - 8

    def body(p_ref, ph_ref, u1_ref, du2_ref, cw_ref, g_ref, b_ref, dp_in, dp_ref, dcw_ref, sm_ref, ubuf, dbuf,
             sh, dacc):
        del dp_in
        step = pl.program_id(0)

        @pl.when(step == 0)
        def _():
            dbuf[tm:tm + HALO, :] = jnp.zeros((HALO, D), F32)
            dcw_ref[...] = jnp.zeros_like(dcw_ref)
            sm_ref[...] = jnp.zeros_like(sm_ref)

        ua = p_ref[0]
        sgb = _sig(p_ref[1])
        halo = ph_ref[0] * _sig(ph_ref[1])
        ubuf[0:HALO, :] = jnp.where(step == ni - 1, 0.0, halo)
        ubuf[HALO:HALO + tm, :] = ua * sgb
        g = g_ref[...]
        y, xh, rs = _ln_fwd(u1_ref[...], g, b_ref[...])
        sy = _sig(y)
        dy = du2_ref[...] * sy * (1.0 + y * (1.0 - sy))
        sm_ref[1:2, :] += _colsum(dy * xh)
        sm_ref[2:3, :] += _colsum(dy)
        dxh = dy * g
        du1 = rs * (dxh - _rowmean(dxh) - xh * _rowmean(dxh * xh))
        sm_ref[0:1, :] += _colsum(du1)
        dbuf[0:tm, :] = du1
        for lb in range(D // LANES):
            ls = slice(lb * LANES, (lb + 1) * LANES)
            taps = [cw_ref[j:j + 1, ls] for j in range(CONV_K)]
            _shift_rows(dbuf, sh, ls, n)

            def du0_body(rb, carry):
                r0 = pl.multiple_of(rb * CONV_RB, CONV_RB)
                acc = jnp.zeros((CONV_RB, LANES), F32)
                for j in range(CONV_K):
                    acc = acc + taps[j] * _tap(dbuf, sh, ls, CONV_K - 1 - j, r0, CONV_RB)
                dp_ref[0, pl.ds(r0, CONV_RB), ls] = acc.astype(ACT)
                return carry

            lax.fori_loop(0, tm // CONV_RB, du0_body, 0)
            _shift_rows(ubuf, sh, ls, n)
            dacc[...] = jnp.zeros_like(dacc)

            def dcw_body(rb, carry):
                r0 = pl.multiple_of(rb * CONV_RB, CONV_RB)
                d = dbuf[pl.ds(r0, CONV_RB), ls]
                for j in range(CONV_K):
                    prod = d * _tap(ubuf, sh, ls, HALO - (CONV_K - 1) + j, r0, CONV_RB)
                    dacc[8 * j:8 * j + 8, :] += jnp.sum(prod.reshape(CONV_RB // 8, 8, LANES), axis=0)
                return carry

            lax.fori_loop(0, tm // CONV_RB, dcw_body, 0)
            for j in range(CONV_K):
                dcw_ref[j:j + 1, ls] += _colsum(dacc[8 * j:8 * j + 8, :])
        du0 = dp_ref[0].astype(F32)
        dp_ref[0] = (du0 * sgb).astype(ACT)
        dp_ref[1] = (du0 * ua * sgb * (1.0 - sgb)).astype(ACT)
        dbuf[tm:tm + HALO, :] = dbuf[0:HALO, :]

    rev = lambda i: ni - 1 - i
    return pl.pallas_call(
        body, name="conv_bwd", grid=(ni,),
        in_specs=[pl.BlockSpec((2, tm, D), lambda i: (2, rev(i), 0)),
                  pl.BlockSpec((2, HALO, D), lambda i: (2, jnp.maximum(rev(i) * hb - 1, 0), 0)),
                  pl.BlockSpec((tm, D), lambda i: (rev(i), 0)), pl.BlockSpec((tm, D), lambda i: (rev(i), 0)),
                  pl.BlockSpec((HALO, D), lambda i: (0, 0)), pl.BlockSpec((1, D), lambda i: (0, 0)),
                  pl.BlockSpec((1, D), lambda i: (0, 0)), pl.BlockSpec(memory_space=pl.ANY)],
        out_specs=[pl.BlockSpec((2, tm, D), lambda i: (2, rev(i), 0)),
                   pl.BlockSpec((HALO, D), lambda i: (0, 0)), pl.BlockSpec((8, D), lambda i: (0, 0))],
        out_shape=[jax.ShapeDtypeStruct(dp.shape, dp.dtype), jax.ShapeDtypeStruct((HALO, D), F32),
                   jax.ShapeDtypeStruct((8, D), F32)],
        input_output_aliases={7: 0},
        scratch_shapes=[pltpu.VMEM((HALO + tm, D), F32), pltpu.VMEM((tm + HALO, D), F32),
                        pltpu.VMEM((7, n, LANES), F32), pltpu.VMEM((8 * CONV_K, LANES), F32)],
        compiler_params=_cparams(1),
    )(p, p, u1, du2, cw, lng, lnb, dp)


def _mixout_fwd(x, oa, u2, p, mod, mo, w_a, w_b, w_o):
    T = x.shape[0]
    tm = min(512, T)

    def body(x_ref, oa_ref, u2_ref, p_ref, mod_ref, wa_ref, wb_ref, wo_ref, xo_ref, ya_ref, yb_ref, mo_ref):
        ya = _mm(oa_ref[...], wa_ref[...])
        yb = _mm(u2_ref[...], wb_ref[...])
        ya_ref[...] = ya.astype(ACT)
        yb_ref[...] = yb.astype(ACT)
        merged = _sig(p_ref[0]) * ya + _sig(p_ref[1]) * yb
        out = _mm(merged, wo_ref[...])
        mo_ref[...] = out
        xo_ref[...] = x_ref[...] + mod_ref[mo + 2:mo + 3, :] * out

    tile = pl.BlockSpec((tm, D), lambda i: (i, 0))
    wspec = pl.BlockSpec((D, D), lambda i: (0, 0))
    return pl.pallas_call(
        body, name="mixout_fwd", grid=(T // tm,),
        in_specs=[tile, tile, tile, pl.BlockSpec((2, tm, D), lambda i: (3, i, 0)),
                  pl.BlockSpec((9, D), lambda i: (0, 0)), wspec, wspec, wspec],
        out_specs=[tile, tile, tile, tile],
        out_shape=[jax.ShapeDtypeStruct((T, D), F32), jax.ShapeDtypeStruct((T, D), ACT),
                   jax.ShapeDtypeStruct((T, D), ACT), jax.ShapeDtypeStruct((T, D), F32)],
        compiler_params=_cparams(1),
    )(x, oa, u2, p, mod, w_a, w_b, w_o)


def _mixout_bwd(dxo, oa, u2, ya, yb, mout, p, mod, mo, w_a, w_b, w_o):
    T = dxo.shape[0]
    tm = min(256, T)

    def body(dxo_ref, oa_ref, u2_ref, ya_ref, yb_ref, mo_ref, p_ref, mod_ref, wa_ref, wb_ref, wo_ref,
             dp_ref, doa_ref, du2_ref, dwa_ref, dwb_ref, dwo_ref, sm_ref):
        @pl.when(pl.program_id(0) == 0)
        def _():
            dwa_ref[...] = jnp.zeros_like(dwa_ref)
            dwb_ref[...] = jnp.zeros_like(dwb_ref)
            dwo_ref[...] = jnp.zeros_like(dwo_ref)
            sm_ref[...] = jnp.zeros_like(sm_ref)

        dxo_v = dxo_ref[...]
        sm_ref[2:3, :] += _colsum(dxo_v * mo_ref[...])
        dmo = (mod_ref[mo + 2:mo + 3, :] * dxo_v).astype(MM)
        ya = ya_ref[...].astype(F32)
        yb = yb_ref[...].astype(F32)
        sga = _sig(p_ref[0])
        sgb = _sig(p_ref[1])
        merged = (sga * ya + sgb * yb).astype(MM)
        dwo_ref[...] += _mm_tn(merged, dmo)
        dmg = _mm_nt(dmo, wo_ref[...])
        dp_ref[0] = (dmg * ya * sga * (1.0 - sga)).astype(ACT)
        dp_ref[1] = (dmg * yb * sgb * (1.0 - sgb)).astype(ACT)
        dya = (dmg * sga).astype(MM)
        dyb = (dmg * sgb).astype(MM)
        dwa_ref[...] += _mm_tn(oa_ref[...], dya)
        dwb_ref[...] += _mm_tn(u2_ref[...], dyb)
        doa_ref[...] = _mm_nt(dya, wa_ref[...])
        du2_ref[...] = _mm_nt(dyb, wb_ref[...])

    tile = pl.BlockSpec((tm, D), lambda i: (i, 0))
    wspec = pl.BlockSpec((D, D), lambda i: (0, 0))
    return pl.pallas_call(
        body, name="mixout_bwd", grid=(T // tm,),
        in_specs=[tile, tile, tile, tile, tile, tile, pl.BlockSpec((2, tm, D), lambda i: (3, i, 0)),
                  pl.BlockSpec((9, D), lambda i: (0, 0)), wspec, wspec, wspec],
        out_specs=[pl.BlockSpec((2, tm, D), lambda i: (3, i, 0)), tile, tile, wspec, wspec, wspec,
                   pl.BlockSpec((8, D), lambda i: (0, 0))],
        out_shape=[jax.ShapeDtypeStruct((8, T, D), ACT), jax.ShapeDtypeStruct((T, D), F32),
                   jax.ShapeDtypeStruct((T, D), F32), jax.ShapeDtypeStruct((D, D), F32),
                   jax.ShapeDtypeStruct((D, D), F32), jax.ShapeDtypeStruct((D, D), F32),
                   jax.ShapeDtypeStruct((8, D), F32)],
        compiler_params=_cparams(1),
    )(dxo, oa, u2, ya, yb, mout, p, mod, w_a, w_b, w_o)


def _ada_wgrad(cs_all, dmod_cols):
    cs_t = jnp.pad(cs_all.T, ((0, 0), (0, HD - N_DEV)))
    dm = jnp.pad(dmod_cols, ((0, HD - N_DEV), (0, 0)))

    def body(cs_ref, d_ref, out_ref):
        out_ref[...] = jnp.dot(cs_ref[...], d_ref[...], preferred_element_type=F32,
                               precision=lax.Precision.HIGHEST)

    return pl.pallas_call(
        body, name="ada_wgrad", out_shape=jax.ShapeDtypeStruct((D, dmod_cols.shape[1]), F32),
        compiler_params=pltpu.CompilerParams(vmem_limit_bytes=VMEM_LIMIT),
    )(cs_t, dm)


def _adam_math(w, g, m, v):
    m2 = ADAM_B1 * m + (1.0 - ADAM_B1) * g
    v2 = ADAM_B2 * v + (1.0 - ADAM_B2) * (g * g)
    m_hat = m2 / (1.0 - ADAM_B1 ** ADAM_STEP)
    v_hat = v2 / (1.0 - ADAM_B2 ** ADAM_STEP)
    delta = -ADAM_LR * (m_hat / (jnp.sqrt(v_hat) + ADAM_EPS) + ADAM_WD * w)
    return delta, m2, v2


def _adamw(w, m, v, g, name):
    R, C = w.shape
    slots = g.ndim == 3
    n_slots = g.shape[0] if slots else 0
    tr = R
    for cand in (256, 176):
        if R % cand == 0 and R > cand:
            tr = cand
            break

    def body(w_ref, m_ref, v_ref, g_ref, go_ref, d_ref, mo_ref, vo_ref):
        if slots:
            gv = g_ref[0].astype(F32)
            for s in range(1, n_slots):
                gv = gv + g_ref[s].astype(F32)
        else:
            gv = g_ref[...]
        go_ref[...] = gv
        d_ref[...], mo_ref[...], vo_ref[...] = _adam_math(w_ref[...], gv, m_ref[...], v_ref[...])

    tile = pl.BlockSpec((tr, C), lambda i: (i, 0))
    gspec = pl.BlockSpec((n_slots, tr, C), lambda i: (0, i, 0)) if slots else tile
    sds = jax.ShapeDtypeStruct((R, C), F32)
    return pl.pallas_call(
        body, name=name, grid=(R // tr,), in_specs=[tile, tile, tile, gspec], out_specs=[tile] * 4,
        out_shape=[sds] * 4, compiler_params=_cparams(1),
    )(w, m, v, g)


def _sum_slots(pack, name, tr):
    n, R, C = pack.shape

    def body(p_ref, out_ref):
        acc = p_ref[0].astype(F32)
        for s in range(1, n):
            acc = acc + p_ref[s].astype(F32)
        out_ref[...] = acc

    return pl.pallas_call(
        body, name=name, grid=(R // tr,), in_specs=[pl.BlockSpec((n, tr, C), lambda i: (0, i, 0))],
        out_specs=pl.BlockSpec((tr, C), lambda i: (i, 0)), out_shape=jax.ShapeDtypeStruct((R, C), F32),
        compiler_params=_cparams(1))(pack)


def _me():
    return lax.axis_index("x"), lax.axis_index("y"), lax.axis_index("c")


def _peer(r):
    x, y, c = _me()
    px = 1 - x if r & 4 else x
    py = 1 - y if r & 2 else y
    pc = 1 - c if r & 1 else c
    return (px, py, pc), 4 * px + 2 * py + pc


def _small_gather(x_ref, out_ref, send_sems, recv_sems):
    R = x_ref.shape[0]
    mx, my, mc = _me()
    me = 4 * mx + 2 * my + mc
    mine = out_ref.at[pl.ds(pl.multiple_of(me * R, 8), R), :]
    copies = []
    for r in range(1, N_DEV):
        dev, _ = _peer(r)
        copies.append(pltpu.make_async_remote_copy(
            src_ref=x_ref, dst_ref=mine, send_sem=send_sems.at[r - 1], recv_sem=recv_sems.at[r - 1],
            device_id=dev, device_id_type=MESH))
    for cp in copies:
        cp.start()
    mine[...] = x_ref[...]
    for r in range(1, N_DEV):
        dev, idx = _peer(r)
        theirs = out_ref.at[pl.ds(pl.multiple_of(idx * R, 8), R), :]
        pltpu.make_async_remote_copy(
            src_ref=x_ref, dst_ref=theirs, send_sem=send_sems.at[r - 1], recv_sem=recv_sems.at[r - 1],
            device_id=dev, device_id_type=MESH).wait_recv()
    for cp in copies:
        cp.wait_send()


def _prologue(cs, ada_w, ada_b_cols, big):
    n = len(big)
    ncol = ada_w.shape[1]
    big_shape, big_sems = _xchg_specs(big, "gather")

    def body(cs_ref, w_ref, b_ref, *rest):
        big_in, cs_all, mod_all, big_out = rest[:n], rest[n], rest[n + 1], rest[n + 2:2 * n + 2]
        mod_scr, s1, r1, s2, r2 = rest[2 * n + 2:2 * n + 7]
        sems = rest[2 * n + 7:]
        _xchg_start(big_in, big_out, sems, "gather")
        _small_gather(cs_ref, cs_all, s1, r1)
        pick = (lax.broadcasted_iota(jnp.int32, (N_DEV, N_DEV * 8), 1)
                == 8 * lax.broadcasted_iota(jnp.int32, (N_DEV, N_DEV * 8), 0)).astype(F32)
        per_device = jnp.dot(pick, cs_all[...], preferred_element_type=F32, precision=lax.Precision.HIGHEST)
        mod_scr[...] = jnp.dot(per_device, w_ref[...], preferred_element_type=F32,
                               precision=lax.Precision.HIGHEST) + b_ref[...]
        _small_gather(mod_scr, mod_all, s2, r2)
        _xchg_wait(big_in, big_out, sems, "gather")

    vmem = pl.BlockSpec(memory_space=pltpu.VMEM)
    hbm = pl.BlockSpec(memory_space=pl.ANY)
    dma7 = pltpu.SemaphoreType.DMA((N_DEV - 1,))
    out = pl.pallas_call(
        body, name="prologue",
        out_shape=[jax.ShapeDtypeStruct((N_DEV * 8, D), F32), jax.ShapeDtypeStruct((N_DEV * 8, ncol), F32)]
        + big_shape,
        in_specs=[vmem, vmem, vmem] + [hbm] * n, out_specs=[vmem, vmem] + [hbm] * n,
        scratch_shapes=[pltpu.VMEM((8, ncol), F32), dma7, dma7, dma7, dma7] + big_sems,
        compiler_params=pltpu.CompilerParams(vmem_limit_bytes=VMEM_LIMIT),
    )(cs, ada_w, ada_b_cols, *big)
    return out[0], out[1], out[2:]


def _allgather_small(x):
    R, C = x.shape

    def body(x_ref, out_ref, send_sems, recv_sems):
        _small_gather(x_ref, out_ref, send_sems, recv_sems)

    return pl.pallas_call(
        body, name="allgather_small_%dx%d" % (R, C),
        out_shape=jax.ShapeDtypeStruct((N_DEV * R, C), F32),
        in_specs=[pl.BlockSpec(memory_space=pltpu.VMEM)], out_specs=pl.BlockSpec(memory_space=pltpu.VMEM),
        scratch_shapes=[pltpu.SemaphoreType.DMA((N_DEV - 1,)), pltpu.SemaphoreType.DMA((N_DEV - 1,))],
    )(x)


N_CHIP = N_DEV // 2


def _xchg_copies(ins, outs, sems, mode):
    send_sems, recv_sems, local_sems = sems
    mx, my, mc = _me()
    me = 4 * mx + 2 * my + mc
    my_chip = 2 * mx + my
    sibling = _peer(1)[0]

    def rdma(a, r, dev, src, slot):
        k = a * (N_DEV - 1) + r - 1
        return pltpu.make_async_remote_copy(
            src_ref=src, dst_ref=outs[a].at[slot], send_sem=send_sems.at[k], recv_sem=recv_sems.at[k],
            device_id=dev, device_id_type=MESH)

    own, sends, relays, recvs = [], [], [], []
    for a in range(len(ins)):
        if mode == "pair":
            for chip in range(N_CHIP):
                src = ins[a].at[2 * chip + 1 - mc]
                sends.append(rdma(a, chip + 1, sibling, src, chip))
                recvs.append(rdma(a, chip + 1, sibling, src, chip))
            continue
        if mode == "quad":
            own.append(pltpu.make_async_copy(ins[a].at[my_chip], outs[a].at[my_chip], local_sems.at[a]))
            for r in (2, 4, 6):
                dev, idx = _peer(r)
                chip = idx // 2
                sends.append(rdma(a, r, dev, ins[a].at[chip], my_chip))
                recvs.append(rdma(a, r, dev, ins[a].at[chip], chip))
            continue
        gather = mode == "gather"
        own.append(pltpu.make_async_copy(ins[a] if gather else ins[a].at[me], outs[a].at[me], local_sems.at[a]))
        for r in range(1, N_DEV):
            dev, idx = _peer(r)
            if not gather:
                sends.append(rdma(a, r, dev, ins[a].at[idx], me))
                recvs.append(rdma(a, r, dev, ins[a].at[idx], idx))
            elif r == 1:
                sends.append(rdma(a, r, dev, ins[a], me))
                recvs.append(rdma(a, r, dev, ins[a], idx))
            elif r % 2 == 0:
                sends.append(rdma(a, r, dev, ins[a], me))
                relays.append((rdma(a, r, dev, ins[a], idx), rdma(a, r + 1, sibling, outs[a].at[idx], idx)))
            else:
                recvs.append(rdma(a, r, sibling, ins[a], idx))
    return own, sends, relays, recvs


def _xchg_start(ins, outs, sems, mode):
    own, sends, _, _ = _xchg_copies(ins, outs, sems, mode)
    for cp in own + sends:
        cp.start()


def _xchg_wait(ins, outs, sems, mode):
    own, sends, relays, recvs = _xchg_copies(ins, outs, sems, mode)
    for arrival, relay in relays:
        arrival.wait_recv()
        relay.start()
    for cp in recvs:
        cp.wait_recv()
    for cp in own:
        cp.wait()
    for cp in sends + [relay for _, relay in relays]:
        cp.wait_send()


def _xchg_specs(arrays, mode):
    n = len(arrays)
    shape = {"gather": lambda s: (N_DEV,) + s, "scatter": lambda s: s, "pair": lambda s: (N_CHIP,) + s[1:],
             "quad": lambda s: s}[mode]
    out_shape = [jax.ShapeDtypeStruct(shape(a.shape), a.dtype) for a in arrays]
    sems = [pltpu.SemaphoreType.DMA((n * (N_DEV - 1),)), pltpu.SemaphoreType.DMA((n * (N_DEV - 1),)),
            pltpu.SemaphoreType.DMA((n,))]
    return out_shape, sems


def _exchange(arrays, mode, name):
    n = len(arrays)

    def body(*refs):
        _xchg_start(refs[:n], refs[n:2 * n], refs[2 * n:], mode)
        _xchg_wait(refs[:n], refs[n:2 * n], refs[2 * n:], mode)

    out_shape, sems = _xchg_specs(arrays, mode)
    return pl.pallas_call(
        body, name=name, out_shape=out_shape,
        in_specs=[pl.BlockSpec(memory_space=pl.ANY)] * n, out_specs=[pl.BlockSpec(memory_space=pl.ANY)] * n,
        scratch_shapes=sems,
    )(*arrays)


def _gridded(body, carry, *, name, grid, in_specs, out_specs, out_shape, scratch_shapes=(), aliases=None):
    if carry is None:
        return pl.pallas_call(
            body, name=name, grid=grid, in_specs=list(in_specs), out_specs=list(out_specs),
            out_shape=list(out_shape), scratch_shapes=list(scratch_shapes), input_output_aliases=aliases or {},
            compiler_params=_cparams(len(grid)))
    arrays, mode = carry
    n, n_in, n_out, n_scr = len(arrays), len(in_specs), len(out_specs), len(scratch_shapes)
    c_shape, c_sems = _xchg_specs(arrays, mode)

    def wrapped(*refs):
        ins, cin = refs[:n_in], refs[n_in:n_in + n]
        o0 = n_in + n
        outs, cout = refs[o0:o0 + n_out], refs[o0 + n_out:o0 + n_out + n]
        s0 = o0 + n_out + n
        scr, sems = refs[s0:s0 + n_scr], refs[s0 + n_scr:]
        first = pl.program_id(0) == 0
        last = pl.program_id(0) == grid[0] - 1
        for ax in range(1, len(grid)):
            first = first & (pl.program_id(ax) == 0)
            last = last & (pl.program_id(ax) == grid[ax] - 1)

        @pl.when(first)
        def _():
            _xchg_start(cin, cout, sems, mode)

        body(*ins, *outs, *scr)

        @pl.when(last)
        def _():
            _xchg_wait(cin, cout, sems, mode)

    hbm = pl.BlockSpec(memory_space=pl.ANY)
    res = pl.pallas_call(
        wrapped, name=name, grid=grid, in_specs=list(in_specs) + [hbm] * n, out_specs=list(out_specs) + [hbm] * n,
        out_shape=list(out_shape) + c_shape, scratch_shapes=list(scratch_shapes) + c_sems,
        input_output_aliases=aliases or {}, compiler_params=_cparams(len(grid)),
    )
    return lambda *args: res(*args, *arrays)


def _local_step(x, target, mod, small, sh, w1):
    w1_in, w1_out = w1[0].reshape(2, D_FF, D), w1[1].reshape(D_FF, D)
    (x1, a1, b1, f1, h1), (wm_in,) = _ffn_fwd(x, mod, 0, small["norm_ffn1"], w1_in, w1_out, 0.5, "ffn1_fwd",
                                              ([sh["mix_w_in"]], "gather"))
    (p, h2), (wh_o, wc_o, wm_o, cw) = _mixin_fwd(
        x1, mod, 3, small["norm_mix"], wm_in,
        ([sh["hgrn_w_o"], sh["conv_w_o"], sh["mix_w_out"], sh["conv_w"]], "gather"))
    wh_o, wc_o, wm_o = wh_o.reshape(D, D), wc_o.reshape(D, D), wm_o.reshape(D, D)
    cw = jnp.pad(cw.transpose(1, 0, 2).reshape(CONV_K, D), ((0, HALO - CONV_K), (0, 0)))
    (o, oa, a_all, s_all), (w2_in, w2_out) = _hgrn_fwd(p, small["hgrn_lb"], small["hgrn_g"],
                                                       ([sh["ffn2_w_in"], sh["ffn2_w_out"]], "gather"))
    w2_in, w2_out = w2_in.reshape(2, D_FF, D), w2_out.reshape(D_FF, D)
    u1, u2 = _conv_fwd(p, cw, small["conv_b"], small["conv_ln_g"], small["conv_ln_b"])
    x2, ya, yb, mout = _mixout_fwd(x1, oa, u2, p, mod, 3, wh_o, wc_o, wm_o)
    (x3, a3, b3, f3, h3), _ = _ffn_fwd(x2, mod, 6, small["norm_ffn2"], w2_in, w2_out, 0.5, "ffn2_fwd", None)
    dx3, df3, sm_head = _head(x3, target, small["norm_final"], mod, 8, 0.5)

    (da3, db3, dw2_in, dw2_out), _ = _ffn_bwd_w(h3, df3, a3, b3, w2_out, "ffn2_bwd_w", None)
    (dx2, sm3), _ = _ffn_bwd_x(x2, dx3, f3, da3, db3, mod, 6, small["norm_ffn2"], w2_in, 0.5, "ffn2_bwd_x", None)
    dp, doa, du2, dwh_o, dwc_o, dwm_o, sm_mo = _mixout_bwd(dx2, oa, u2, ya, yb, mout, p, mod, 3, wh_o, wc_o, wm_o)
    dp, dcw, sm_cv = _conv_bwd(p, u1, du2, cw, small["conv_ln_g"], small["conv_ln_b"], dp)
    rows = lambda t: t.reshape(N_DEV, -1, D).astype(MM)
    (dp, sm_hg), (r2_in, r2_out) = _hgrn_bwd(p, o, a_all, s_all, doa, small["hgrn_lb"], small["hgrn_g"], dp,
                                             ([rows(dw2_in), rows(dw2_out)], "scatter"))
    (dx1, dwm_in, sm2, df1), (rh_o, rc_o, rm_o, rcw) = _mixin_bwd(
        x1, h2, dx2, dp, mod, 3, small["norm_mix"], wm_in, 2, 0.5,
        ([rows(dwh_o), rows(dwc_o), rows(dwm_o), dcw[:CONV_K].reshape(CONV_K, N_DEV, -1).transpose(1, 0, 2)],
         "scatter"))
    (da1, db1, dw1_in, dw1_out), (rm_in,) = _ffn_bwd_w(h1, df1, a1, b1, w1_out, "ffn1_bwd_w",
                                                      (_pair_reduce([dwm_in], "pair_mix"), "quad"))
    (dx0, sm1), (r1_in, r1_out) = _ffn_bwd_x(
        x, dx1, f1, da1, db1, mod, 0, small["norm_ffn1"], w1_in, 0.5, "ffn1_bwd_x",
        (_pair_reduce([rows(dw1_in), rows(dw1_out)], "pair_ffn1"), "quad"))

    dmod = jnp.concatenate([sm1[0:3], sm2[0:2], sm_mo[2:3], sm3[0:3]], axis=0)
    gsmall = dict(norm_ffn1=sm1[3:4], norm_mix=sm2[3:4], lb0=sm_hg[0:1], hgrn_g=sm_hg[1:2], conv_b=sm_cv[0:1],
                  conv_ln_g=sm_cv[1:2], conv_ln_b=sm_cv[2:3], norm_ffn2=sm3[3:4], norm_final=sm_head[0:1])
    recv = dict(ffn1_w_in=r1_in, ffn1_w_out=r1_out, mix_w_in=rm_in, hgrn_w_o=rh_o, conv_w=rcw, conv_w_o=rc_o,
                mix_w_out=rm_o, ffn2_w_in=r2_in, ffn2_w_out=r2_out)
    return sm_head[1, 0], dx0, dmod, gsmall, recv


def _pair_add(mine, theirs, core, name):
    _, R, C = theirs.shape

    def body(core_ref, a_ref, b_ref, out_ref):
        del core_ref
        out_ref[0] = (a_ref[0, 0].astype(F32) + b_ref[0].astype(F32)).astype(out_ref.dtype)

    blk = pl.BlockSpec((1, R, C), lambda s, core_ref: (s, 0, 0))
    grid_spec = pltpu.PrefetchScalarGridSpec(
        num_scalar_prefetch=1, grid=(N_CHIP,),
        in_specs=[pl.BlockSpec((1, 1, R, C), lambda s, core_ref: (s, core_ref[0], 0, 0)), blk], out_specs=blk)
    return pl.pallas_call(body, name=name, grid_spec=grid_spec,
                          out_shape=jax.ShapeDtypeStruct(theirs.shape, mine.dtype), compiler_params=_cparams(1),
                          )(core, mine.reshape(N_CHIP, 2, R, C), theirs)


def _pair_reduce(arrays, name):
    theirs = _exchange(arrays, "pair", name)
    core = lax.axis_index("c").astype(jnp.int32).reshape(1)
    return [_pair_add(a, t, core, "%s_add%d" % (name, i)) for i, (a, t) in enumerate(zip(arrays, theirs))]


SMALL_ORDER = ("norm_ffn1", "norm_mix", "lb0", "hgrn_g", "conv_b", "conv_ln_g", "conv_ln_b", "norm_ffn2",
               "norm_final")
PACK_ROWS = 24


def kernel(x, c, ada_w, ada_b, norm_ffn1, ffn1_w_in, ffn1_w_out, norm_mix, mix_w_in, hgrn_lb, hgrn_g, hgrn_w_o, conv_w, conv_b, conv_ln_g, conv_ln_b, conv_w_o, mix_w_out, norm_ffn2, ffn2_w_in, ffn2_w_out, norm_final, loss_target, m_ada_w, m_ada_b, m_norm_ffn1, m_ffn1_w_in, m_ffn1_w_out, m_norm_mix, m_mix_w_in, m_hgrn_lb, m_hgrn_g, m_hgrn_w_o, m_conv_w, m_conv_b, m_conv_ln_g, m_conv_ln_b, m_conv_w_o, m_mix_w_out, m_norm_ffn2, m_ffn2_w_in, m_ffn2_w_out, m_norm_final, v_ada_w, v_ada_b, v_norm_ffn1, v_ffn1_w_in, v_ffn1_w_out, v_norm_mix, v_mix_w_in, v_hgrn_lb, v_hgrn_g, v_hgrn_w_o, v_conv_w, v_conv_b, v_conv_ln_g, v_conv_ln_b, v_conv_w_o, v_mix_w_out, v_norm_ffn2, v_ffn2_w_in, v_ffn2_w_out, v_norm_final):
    mx, my, mc = _me()
    me = 4 * mx + 2 * my + mc
    ncol = ada_w.shape[2]

    sh = dict(ffn1_w_out=ffn1_w_out, mix_w_in=mix_w_in, hgrn_w_o=hgrn_w_o, conv_w_o=conv_w_o,
              mix_w_out=mix_w_out, ffn2_w_out=ffn2_w_out)
    sh = {n: w[0].astype(MM) for n, w in sh.items()}
    sh["ffn1_w_in"] = ffn1_w_in[0].T.astype(MM)
    sh["ffn2_w_in"] = ffn2_w_in[0].T.astype(MM)
    sh["conv_w"] = conv_w[0]
    small = dict(norm_ffn1=norm_ffn1, norm_mix=norm_mix, hgrn_lb=hgrn_lb, hgrn_g=hgrn_g, conv_b=conv_b,
                 conv_ln_g=conv_ln_g, conv_ln_b=conv_ln_b, norm_ffn2=norm_ffn2, norm_final=norm_final.reshape(1, D))

    cs = jnp.broadcast_to(c * jax.nn.sigmoid(c), (8, D))
    ada_b_cols = lax.dynamic_slice(ada_b, (0, me * ncol), (1, ncol))
    cs_all, mod_all, w1 = _prologue(cs, ada_w[0], ada_b_cols, [sh["ffn1_w_in"], sh["ffn1_w_out"]])
    cs_all = cs_all.reshape(N_DEV, 8, D)[:, 0, :]
    mod = lax.dynamic_index_in_dim(mod_all.reshape(N_DEV, N_DEV, ncol), me, axis=1, keepdims=False).reshape(9, D)

    loss_local, dx, dmod, gsmall, recv = _local_step(x[0], loss_target[0], mod, small, sh, w1)
    loss = lax.psum(loss_local, ("x", "y", "c"))

    pack = jnp.concatenate([dmod] + [gsmall[n] for n in SMALL_ORDER]
                           + [jnp.zeros((PACK_ROWS - 9 - len(SMALL_ORDER), D), F32)], axis=0)
    pack_all = _allgather_small(pack).reshape(N_DEV, PACK_ROWS, D)
    tot = _sum_slots(pack_all, "sum_small", PACK_ROWS)
    gs = {n: tot[9 + i:10 + i] for i, n in enumerate(SMALL_ORDER)}
    dmod_all = pack_all[:, 0:9, :].reshape(N_DEV, 9 * D)
    g_ada_b = tot[0:9].reshape(1, 9 * D)
    g_ada_w = _ada_wgrad(cs_all, lax.dynamic_slice(dmod_all, (0, me * ncol), (N_DEV, ncol)))
    z = hgrn_lb.astype(F32)
    p0 = jax.nn.sigmoid(z[0:1] - z[1:2])
    dz0 = p0 * (1.0 - p0) * gs["lb0"]
    g_hgrn_lb = jnp.concatenate([dz0, -dz0], axis=0)

    res = {}
    res["ada_w"] = _adamw(ada_w[0], m_ada_w[0], v_ada_w[0], g_ada_w, "adamw_ada_w")
    big = dict(ffn1_w_in=(ffn1_w_in, m_ffn1_w_in, v_ffn1_w_in), ffn1_w_out=(ffn1_w_out, m_ffn1_w_out, v_ffn1_w_out),
               mix_w_in=(mix_w_in, m_mix_w_in, v_mix_w_in), hgrn_w_o=(hgrn_w_o, m_hgrn_w_o, v_hgrn_w_o),
               conv_w=(conv_w, m_conv_w, v_conv_w), conv_w_o=(conv_w_o, m_conv_w_o, v_conv_w_o),
               mix_w_out=(mix_w_out, m_mix_w_out, v_mix_w_out), ffn2_w_in=(ffn2_w_in, m_ffn2_w_in, v_ffn2_w_in),
               ffn2_w_out=(ffn2_w_out, m_ffn2_w_out, v_ffn2_w_out))
    for n, (w, m, v) in big.items():
        g = recv[n]
        if n in ("ffn1_w_in", "ffn2_w_in"):
            g = _sum_slots(g, "sum_" + n, g.shape[1] // 4).T
        res[n] = _adamw(w[0], m[0], v[0], g, "adamw_" + n)
    sm_names = ("ada_b", "norm_ffn1", "norm_mix", "hgrn_lb", "hgrn_g", "conv_b", "conv_ln_g", "conv_ln_b",
                "norm_ffn2", "norm_final")
    sm_w = dict(ada_b=(ada_b, m_ada_b, v_ada_b), norm_ffn1=(norm_ffn1, m_norm_ffn1, v_norm_ffn1),
                norm_mix=(norm_mix, m_norm_mix, v_norm_mix), hgrn_lb=(hgrn_lb, m_hgrn_lb, v_hgrn_lb),
                hgrn_g=(hgrn_g, m_hgrn_g, v_hgrn_g), conv_b=(conv_b, m_conv_b, v_conv_b),
                conv_ln_g=(conv_ln_g, m_conv_ln_g, v_conv_ln_g), conv_ln_b=(conv_ln_b, m_conv_ln_b, v_conv_ln_b),
                norm_ffn2=(norm_ffn2, m_norm_ffn2, v_norm_ffn2), norm_final=(norm_final, m_norm_final, v_norm_final))
    sm_g = dict(gs, ada_b=g_ada_b, hgrn_lb=g_hgrn_lb)
    rows = {n: sm_w[n][0].size // D for n in sm_names}
    n_rows = sum(rows.values())
    pad = (-n_rows) % 8
    stack = lambda parts: jnp.concatenate([q.reshape(-1, D) for q in parts] + [jnp.ones((pad, D), F32)], axis=0)
    st = _adamw(stack([sm_w[n][0] for n in sm_names]), stack([sm_w[n][1] for n in sm_names]),
                stack([sm_w[n][2] for n in sm_names]), stack([sm_g[n] for n in sm_names]), "adamw_small")
    off = 0
    for n in sm_names:
        res[n] = tuple(t[off:off + rows[n]].reshape(sm_w[n][0].shape) for t in st)
        off += rows[n]

    order = ("ada_w", "ada_b", "norm_ffn1", "ffn1_w_in", "ffn1_w_out", "norm_mix", "mix_w_in", "hgrn_lb", "hgrn_g",
             "hgrn_w_o", "conv_w", "conv_b", "conv_ln_g", "conv_ln_b", "conv_w_o", "mix_w_out", "norm_ffn2",
             "ffn2_w_in", "ffn2_w_out", "norm_final")
    lead = lambda n, t: t[None] if n in big or n == "ada_w" else t
    outs = [loss, dx[None]]
    for j in range(4):
        outs += [lead(n, res[n][j]) for n in order]
    return tuple(outs)
```

```python
import jax
import jax.numpy as jnp
from jax import lax
from jax.experimental import pallas as pl
from jax.experimental.pallas import tpu as pltpu

F32 = jnp.float32
MM = jnp.bfloat16
ACT = jnp.bfloat16

D = 1024
D_FF = 2816
HEADS = 8
HD = 128
CHUNK = 64
SUB = 16
NSUB = CHUNK // SUB
HGRN_BLOCK = 1024
SAFE_EXP = 60.0
CONV_K = 31
HALO = 32
EPS = 1e-6
N_DEV = 8
NEG = -1e30
Q_SCALE = HD ** -0.5

ADAM_LR = 0.001
ADAM_B1 = 0.9
ADAM_B2 = 0.999
ADAM_EPS = 1e-08
ADAM_WD = 0.01
ADAM_STEP = 10

V7X_VMEM_BYTES = 64 * 1024 * 1024
VMEM_LIMIT = V7X_VMEM_BYTES - 4 * 1024 * 1024
MESH = pl.DeviceIdType.MESH


def _cparams(n_axes):
    return pltpu.CompilerParams(dimension_semantics=("arbitrary",) * n_axes, vmem_limit_bytes=VMEM_LIMIT)


def _mm(a, b):
    return lax.dot_general(a.astype(MM), b.astype(MM), (((1,), (0,)), ((), ())), preferred_element_type=F32)


def _mm_nt(a, b):
    return lax.dot_general(a.astype(MM), b.astype(MM), (((1,), (1,)), ((), ())), preferred_element_type=F32)


def _mm_tn(a, b):
    return lax.dot_general(a.astype(MM), b.astype(MM), (((0,), (0,)), ((), ())), preferred_element_type=F32)


def _sig(x):
    return 1.0 / (1.0 + jnp.exp(-x))


def _colsum(x):
    return jnp.sum(x, axis=0, keepdims=True)


def _rowmean(x):
    return jnp.mean(x, axis=-1, keepdims=True)


def _modnorm_fwd(xv, g, sh, sc):
    r = lax.rsqrt(_rowmean(xv * xv) + EPS)
    xh = xv * r
    n = xh * g
    return n * (1.0 + sc) + sh, xh, n, r


def _modnorm_bwd(dh, xh, n, r, g, sc):
    dsc = _colsum(dh * n)
    dsh = _colsum(dh)
    dn = dh * (1.0 + sc)
    dg = _colsum(dn * xh)
    dxh = dn * g
    dx = r * (dxh - xh * _rowmean(dxh * xh))
    return dx, dsh, dsc, dg


def _ffn_fwd(x, mod, mo, gnorm, w_in_t, w_out, res, name, carry):
    T = x.shape[0]
    tm = min(512, T)
    tn = D_FF // 2

    def body(x_ref, mod_ref, g_ref, wi_ref, wo_ref, xo_ref, a_ref, b_ref, f_ref, h_ref):
        xv = x_ref[...]
        h, _, _, _ = _modnorm_fwd(xv, g_ref[...], mod_ref[mo:mo + 1, :], mod_ref[mo + 1:mo + 2, :])
        h = h.astype(ACT)
        h_ref[...] = h
        f = None
        for c0 in range(0, D_FF, tn):
            a = _mm_nt(h, wi_ref[0, c0:c0 + tn, :])
            b = _mm_nt(h, wi_ref[1, c0:c0 + tn, :])
            a_ref[:, c0:c0 + tn] = a.astype(ACT)
            b_ref[:, c0:c0 + tn] = b.astype(ACT)
            part = _mm(a * _sig(a) * b, wo_ref[c0:c0 + tn, :])
            f = part if f is None else f + part
        f_ref[...] = f
        xo_ref[...] = xv + res * mod_ref[mo + 2:mo + 3, :] * f

    tile = pl.BlockSpec((tm, D), lambda i: (i, 0))
    wide = pl.BlockSpec((tm, D_FF), lambda i: (i, 0))
    out = _gridded(
        body, carry, name=name, grid=(T // tm,),
        in_specs=[
            tile,
            pl.BlockSpec((9, D), lambda i: (0, 0)),
            pl.BlockSpec((1, D), lambda i: (0, 0)),
            pl.BlockSpec((2, D_FF, D), lambda i: (0, 0, 0), pipeline_mode=pl.Buffered(1)),
            pl.BlockSpec((D_FF, D), lambda i: (0, 0), pipeline_mode=pl.Buffered(1)),
        ],
        out_specs=[tile, wide, wide, tile, tile],
        out_shape=[
            jax.ShapeDtypeStruct((T, D), F32),
            jax.ShapeDtypeStruct((T, D_FF), ACT),
            jax.ShapeDtypeStruct((T, D_FF), ACT),
            jax.ShapeDtypeStruct((T, D), F32),
            jax.ShapeDtypeStruct((T, D), ACT),
        ],
    )(x, mod, gnorm, w_in_t, w_out)
    return out[:5], out[5:]


def _ffn_bwd_w(h, df, a, b, w_out, name, carry):
    T = h.shape[0]
    tm = min(2048, T)
    ni = T // tm
    tn = 256
    nj = D_FF // tn

    def body(h_ref, df_ref, a_ref, b_ref, wo_ref, da_ref, db_ref, dwi_ref, dwo_ref, acc_i, acc_o):
        i = pl.program_id(1)

        @pl.when(i == 0)
        def _():
            acc_i[...] = jnp.zeros_like(acc_i)
            acc_o[...] = jnp.zeros_like(acc_o)

        hb = h_ref[...]
        df = df_ref[...]
        av = a_ref[...].astype(F32)
        bv = b_ref[...].astype(F32)
        sg = _sig(av)
        sa = av * sg
        s = (sa * bv).astype(MM)
        ds = _mm_nt(df, wo_ref[...])
        da = (ds * bv * sg * (1.0 + av * (1.0 - sg))).astype(MM)
        db = (ds * sa).astype(MM)
        da_ref[...] = da
        db_ref[...] = db
        acc_o[...] += _mm_tn(s, df)
        acc_i[0] += _mm_tn(da, hb)
        acc_i[1] += _mm_tn(db, hb)

        @pl.when(i == ni - 1)
        def _():
            dwi_ref[...] = acc_i[...].astype(MM)
            dwo_ref[...] = acc_o[...].astype(MM)

    out = _gridded(
        body, carry, name=name, grid=(nj, ni),
        in_specs=[
            pl.BlockSpec((tm, D), lambda j, i: (i, 0)),
            pl.BlockSpec((tm, D), lambda j, i: (i, 0)),
            pl.BlockSpec((tm, tn), lambda j, i: (i, j)),
            pl.BlockSpec((tm, tn), lambda j, i: (i, j)),
            pl.BlockSpec((tn, D), lambda j, i: (j, 0)),
        ],
        out_specs=[
            pl.BlockSpec((tm, tn), lambda j, i: (i, j)),
            pl.BlockSpec((tm, tn), lambda j, i: (i, j)),
            pl.BlockSpec((2, tn, D), lambda j, i: (0, j, 0)),
            pl.BlockSpec((tn, D), lambda j, i: (j, 0)),
        ],
        out_shape=[
            jax.ShapeDtypeStruct((T, D_FF), MM),
            jax.ShapeDtypeStruct((T, D_FF), MM),
            jax.ShapeDtypeStruct((2, D_FF, D), MM),
            jax.ShapeDtypeStruct((D_FF, D), MM),
        ],
        scratch_shapes=[pltpu.VMEM((2, tn, D), F32), pltpu.VMEM((tn, D), F32)],
    )(h, df, a, b, w_out)
    return out[:4], out[4:]


def _ffn_bwd_x(x, dxo, f, da, db, mod, mo, gnorm, w_in_t, res, name, carry):
    T = x.shape[0]
    tm = min(512, T)
    ni = T // tm
    tn = D_FF // 2
    nj = D_FF // tn

    def body(x_ref, dxo_ref, f_ref, da_ref, db_ref, mod_ref, g_ref, wi_ref, dx_ref, sm_ref, dh_scr):
        j = pl.program_id(0)
        i = pl.program_id(1)

        @pl.when((j == 0) & (i == 0))
        def _():
            sm_ref[...] = jnp.zeros_like(sm_ref)

        @pl.when(j == 0)
        def _():
            dh_scr[i] = jnp.zeros((tm, D), F32)

        dh_scr[i] += _mm(da_ref[...], wi_ref[0]) + _mm(db_ref[...], wi_ref[1])

        @pl.when(j == nj - 1)
        def _():
            sc = mod_ref[mo + 1:mo + 2, :]
            _, xh, n, r = _modnorm_fwd(x_ref[...], g_ref[...], mod_ref[mo:mo + 1, :], sc)
            dxn, dsh, dsc, dg = _modnorm_bwd(dh_scr[i], xh, n, r, g_ref[...], sc)
            dxo_v = dxo_ref[...]
            dx_ref[...] = dxo_v + dxn
            sm_ref[0:1, :] += dsh
            sm_ref[1:2, :] += dsc
            sm_ref[2:3, :] += _colsum(dxo_v * f_ref[...]) * res
            sm_ref[3:4, :] += dg

    last = pl.BlockSpec((tm, D), lambda j, i: (jnp.where(j == nj - 1, i, 0), 0))
    out = _gridded(
        body, carry, name=name, grid=(nj, ni),
        in_specs=[last, last, last,
                  pl.BlockSpec((tm, tn), lambda j, i: (i, j)), pl.BlockSpec((tm, tn), lambda j, i: (i, j)),
                  pl.BlockSpec((9, D), lambda j, i: (0, 0)), pl.BlockSpec((1, D), lambda j, i: (0, 0)),
                  pl.BlockSpec((2, tn, D), lambda j, i: (0, j, 0))],
        out_specs=[last, pl.BlockSpec((8, D), lambda j, i: (0, 0))],
        out_shape=[jax.ShapeDtypeStruct((T, D), F32), jax.ShapeDtypeStruct((8, D), F32)],
        scratch_shapes=[pltpu.VMEM((ni, tm, D), F32)],
    )(x, dxo, f, da, db, mod, gnorm, w_in_t)
    return out[:2], out[2:]


def _head(x, target, gfin, mod, gate_row, res):
    T = x.shape[0]
    tm = min(512, T)
    ni = T // tm

    def body(x_ref, t_ref, g_ref, mod_ref, dx_ref, df_ref, sm_ref):
        i = pl.program_id(0)

        @pl.when(i == 0)
        def _():
            sm_ref[...] = jnp.zeros_like(sm_ref)

        xv = x_ref[...]
        g = g_ref[...]
        r = lax.rsqrt(_rowmean(xv * xv) + EPS)
        xh = xv * r
        e = xh * g - t_ref[...]
        sm_ref[1:2, :] += _colsum(e * e) * (0.5 / D)
        dy = e * (1.0 / D)
        sm_ref[0:1, :] += _colsum(dy * xh)
        dxh = dy * g
        dx = r * (dxh - xh * _rowmean(dxh * xh))
        dx_ref[...] = dx
        df_ref[...] = (res * mod_ref[gate_row:gate_row + 1, :] * dx).astype(MM)

        @pl.when(i == ni - 1)
        def _():
            sm_ref[1:2, :] = jnp.broadcast_to(jnp.sum(sm_ref[1:2, :], axis=-1, keepdims=True), (1, D))

    tile = pl.BlockSpec((tm, D), lambda i: (i, 0))
    return pl.pallas_call(
        body, name="head_loss", grid=(ni,),
        in_specs=[tile, tile, pl.BlockSpec((1, D), lambda i: (0, 0)), pl.BlockSpec((9, D), lambda i: (0, 0))],
        out_specs=[tile, tile, pl.BlockSpec((8, D), lambda i: (0, 0))],
        out_shape=[jax.ShapeDtypeStruct((T, D), F32), jax.ShapeDtypeStruct((T, D), MM),
                   jax.ShapeDtypeStruct((8, D), F32)],
        compiler_params=_cparams(1),
    )(x, target, gfin, mod)


def _mixin_fwd(x, mod, mo, gnorm, w, carry):
    T = x.shape[0]
    tm = min(1024, T)
    ni = T // tm

    def body(x_ref, mod_ref, g_ref, w_ref, p_ref, h_ref, h_all):
        i = pl.program_id(1)

        @pl.when(pl.program_id(0) == 0)
        def _():
            h, _, _, _ = _modnorm_fwd(x_ref[...], g_ref[...], mod_ref[mo:mo + 1, :], mod_ref[mo + 1:mo + 2, :])
            h_all[i] = h.astype(ACT)
            h_ref[...] = h.astype(ACT)

        p_ref[0] = _mm(h_all[i], w_ref[0])

    first = lambda k, i: (jnp.where(k == 0, i, ni - 1), 0)
    out = _gridded(
        body, carry, name="mixin_fwd", grid=(8, ni),
        in_specs=[pl.BlockSpec((tm, D), first), pl.BlockSpec((9, D), lambda k, i: (0, 0)),
                  pl.BlockSpec((1, D), lambda k, i: (0, 0)), pl.BlockSpec((1, D, D), lambda k, i: (k, 0, 0))],
        out_specs=[pl.BlockSpec((1, tm, D), lambda k, i: (k, i, 0)), pl.BlockSpec((tm, D), first)],
        out_shape=[jax.ShapeDtypeStruct((8, T, D), F32), jax.ShapeDtypeStruct((T, D), ACT)],
        scratch_shapes=[pltpu.VMEM((ni, tm, D), ACT)],
    )(x, mod, gnorm, w)
    return out[:2], out[2:]


def _mixin_bwd(x, h, dxo, dp, mod, mo, gnorm, w, next_gate, next_res, carry):
    T = x.shape[0]
    tm = min(512, T)
    ni = T // tm

    def body(x_ref, h_ref, dxo_ref, dp_ref, mod_ref, g_ref, w_ref, dx_ref, dw_ref, sm_ref, df_ref, dh_scr, acc):
        k = pl.program_id(0)
        i = pl.program_id(1)

        @pl.when(i == 0)
        def _():
            acc[...] = jnp.zeros_like(acc)

        @pl.when(k == 0)
        def _():
            dh_scr[i] = jnp.zeros((tm, D), F32)

        @pl.when((k == 0) & (i == 0))
        def _():
            sm_ref[...] = jnp.zeros_like(sm_ref)

        dpk = dp_ref[0].astype(MM)
        acc[...] += _mm_tn(h_ref[...], dpk)
        dh_scr[i] += _mm_nt(dpk, w_ref[0])

        @pl.when(i == ni - 1)
        def _():
            dw_ref[0] = acc[...].astype(MM)

        @pl.when(k == 7)
        def _():
            sc = mod_ref[mo + 1:mo + 2, :]
            _, xh, n, r = _modnorm_fwd(x_ref[...], g_ref[...], mod_ref[mo:mo + 1, :], sc)
            dxn, dsh, dsc, dg = _modnorm_bwd(dh_scr[i], xh, n, r, g_ref[...], sc)
            dx = dxo_ref[...] + dxn
            dx_ref[...] = dx
            df_ref[...] = (next_res * mod_ref[next_gate:next_gate + 1, :] * dx).astype(MM)
            sm_ref[0:1, :] += dsh
            sm_ref[1:2, :] += dsc
            sm_ref[3:4, :] += dg

    last = pl.BlockSpec((tm, D), lambda k, i: (jnp.where(k == 7, i, 0), 0))
    out = _gridded(
        body, carry, name="mixin_bwd", grid=(8, ni),
        in_specs=[pl.BlockSpec((tm, D), lambda k, i: (jnp.where(k == 7, i, 0), 0)),
                  pl.BlockSpec((tm, D), lambda k, i: (i, 0)),
                  pl.BlockSpec((tm, D), lambda k, i: (jnp.where(k == 7, i, 0), 0)),
                  pl.BlockSpec((1, tm, D), lambda k, i: (k, i, 0)), pl.BlockSpec((9, D), lambda k, i: (0, 0)),
                  pl.BlockSpec((1, D), lambda k, i: (0, 0)), pl.BlockSpec((1, D, D), lambda k, i: (k, 0, 0))],
        out_specs=[last, pl.BlockSpec((1, D, D), lambda k, i: (k, 0, 0)), pl.BlockSpec((8, D), lambda k, i: (0, 0)),
                   last],
        out_shape=[jax.ShapeDtypeStruct((T, D), F32), jax.ShapeDtypeStruct((8, D, D), MM),
                   jax.ShapeDtypeStruct((8, D), F32), jax.ShapeDtypeStruct((T, D), MM)],
        scratch_shapes=[pltpu.VMEM((ni, tm, D), F32), pltpu.VMEM((D, D), F32)],
    )(x, h, dxo, dp, mod, gnorm, w)
    return out[:4], out[4:]


def _hgrn_consts():
    rows = jnp.arange(SUB * HD) // HD
    e = (rows[:, None] == jnp.arange(HD)[None, :]).astype(MM)
    return e, e.T


def _rows_bcast(ref, cb, first, n):
    parts = [jnp.broadcast_to(ref[pl.ds(c * CHUNK + first, 1), :], (n, HD)) for c in range(cb // CHUNK)]
    return jnp.concatenate(parts, axis=0)


def _hgrn_pre(qr, fr, lb_ref, b_scr, cb):
    z = lb_ref[...]
    lb = _sig(z[0:1, :] - z[1:2, :])
    sq = _sig(qr)
    q = qr * sq * Q_SCALE
    sf = _sig(fr)
    fg = lb + (1.0 - lb) * sf
    lf = jnp.log(fg)
    k = 1.0 - fg
    tl = lax.broadcasted_iota(jnp.int32, (cb, HD), 0) % CHUNK
    bc = lf
    sh = 1
    while sh < CHUNK:
        bc = bc + jnp.where(tl >= sh, pltpu.roll(bc, sh, 0), 0.0)
        sh *= 2
    b_scr[...] = bc
    bl = _rows_bcast(b_scr, cb, CHUNK - 1, CHUNK)
    eb = jnp.exp(bc)
    ekd = jnp.exp(bl - bc)
    ekf = jnp.exp(jnp.minimum(-bc, SAFE_EXP))
    return dict(lb=lb, sq=sq, q=q, sf=sf, fg=fg, k=k, tl=tl, b=bc, bl=bl, eb=eb, ekd=ekd, ekf=ekf,
                qe=q * eb, kd=k * ekd, kf=k * ekf, safe=jnp.max(-bc) < SAFE_EXP)


def _hgrn_sub(pre, b_scr, cb):
    bc, tl, q, k = pre["b"], pre["tl"], pre["q"], pre["k"]
    br = [None] + [_rows_bcast(b_scr, cb, SUB * i - 1, CHUNK) for i in range(1, NSUB)]
    sb = tl // SUB
    bref = jnp.where(sb == 0, bc, jnp.where(sb == 1, br[1], jnp.where(sb == 2, br[2], br[3])))
    eqo = jnp.exp(bc - bref)
    eko = [None] + [jnp.exp(jnp.where(tl < SUB * i, br[i] - bc, NEG)) for i in range(1, NSUB)]
    return dict(eqo=eqo, eko=eko, qo=q * eqo, ko=[None] + [k * eko[i] for i in range(1, NSUB)])


def _pad_rows(x):
    return jnp.concatenate([x, jnp.zeros_like(x)], axis=0)


def _by_subblock(sbc, parts):
    out = jnp.zeros_like(parts[1])
    for i in range(1, NSUB):
        out = jnp.where(sbc == i, parts[i], out)
    return out


def _hgrn_fwd(p, hgrn_lb, hgrn_g, carry):
    T = p.shape[1]
    cb = min(HGRN_BLOCK, T)
    nch = cb // CHUNK
    ncb = T // cb
    e_mat, _ = _hgrn_consts()

    def body(p_ref, lb_ref, g_ref, e_ref, o_ref, oa_ref, a_ref, s_ref, st_scr, q_scr, k_scr, b_scr, z_scr, ad_scr):
        @pl.when(pl.program_id(1) == 0)
        def _():
            st_scr[...] = jnp.zeros_like(st_scr)

        v = p_ref[2]
        og = p_ref[3]
        pre = _hgrn_pre(p_ref[0], p_ref[1], lb_ref, b_scr, cb)
        chunks = [slice(c * CHUNK, (c + 1) * CHUNK) for c in range(nch)]
        row_i = lax.broadcasted_iota(jnp.int32, (CHUNK, HD), 0)
        lane_i = lax.broadcasted_iota(jnp.int32, (CHUNK, HD), 1)
        sbc = row_i // SUB
        causal = lane_i <= row_i

        @pl.when(pre["safe"])
        def _():
            for rs in chunks:
                ad_scr[rs, :] = jnp.where(causal, _mm_nt(pre["qe"][rs], _pad_rows(pre["kf"][rs])), 0.0)

        @pl.when(jnp.logical_not(pre["safe"]))
        def _():
            sub = _hgrn_sub(pre, b_scr, cb)
            q_scr[...] = pre["q"]
            k_scr[...] = pre["k"]
            ti = lax.broadcasted_iota(jnp.int32, (SUB, HD), 0)

            def zbody(c, carry):
                for i in range(NSUB):
                    r0 = pl.multiple_of(c * CHUNK + SUB * i, SUB)
                    qi = q_scr[pl.ds(r0, SUB), :]
                    bi = b_scr[pl.ds(r0, SUB), :]
                    for s in range(SUB):
                        krow = k_scr[pl.ds(r0 + s, 1), :]
                        brow = b_scr[pl.ds(r0 + s, 1), :]
                        if s < 8:
                            zz = qi * krow * jnp.exp(jnp.where(ti >= s, bi - brow, NEG))
                        else:
                            lo = qi[8:] * krow * jnp.exp(jnp.where(ti[8:] >= s, bi[8:] - brow, NEG))
                            zz = jnp.concatenate([jnp.zeros((8, HD), F32), lo], axis=0)
                        z_scr[i, pl.ds(pl.multiple_of(c * SUB, SUB), SUB), s * HD:(s + 1) * HD] = zz.astype(MM)
                return carry

            lax.fori_loop(0, nch, zbody, 0)
            adiag = [_mm(z_scr[i], e_ref[...]) for i in range(NSUB)]
            offs = [[_mm_nt(sub["qo"][rs], _pad_rows(sub["ko"][i][rs])) for i in range(1, NSUB)] for rs in chunks]
            for c, rs in enumerate(chunks):
                dparts = []
                for i in range(NSUB):
                    blk = adiag[i][c * SUB:(c + 1) * SUB]
                    dparts.append(blk if i == 0 else pltpu.roll(blk, SUB * i, 1))
                ad_scr[rs, :] = _by_subblock(sbc, [None] + offs[c]) + jnp.concatenate(dparts, axis=0)

        kv = [_mm_tn(v[rs], pre["kd"][rs]) for rs in chunks]
        a_parts = [ad_scr[rs, :] for rs in chunks]
        a_ref[0] = ad_scr[...]
        o_intra = [_mm(a_parts[c], _pad_rows(v[rs])) for c, rs in enumerate(chunks)]
        states = []
        st = st_scr[...]
        for c in range(nch):
            states.append(st)
            st = st * jnp.exp(b_scr[pl.ds(c * CHUNK + CHUNK - 1, 1), :]) + kv[c]
        st_scr[...] = st
        for c in range(nch):
            s_ref[0, c] = states[c]
        o = jnp.concatenate([o_intra[c] + _mm_nt(pre["qe"][rs], states[c]) for c, rs in enumerate(chunks)], axis=0)
        o_ref[...] = o
        on = o * lax.rsqrt(_rowmean(o * o) + EPS) * g_ref[...]
        oa_ref[...] = (on * og * _sig(og)).astype(ACT)

    out = _gridded(
        body, carry, name="hgrn_fwd", grid=(HEADS, ncb),
        in_specs=[pl.BlockSpec((4, cb, HD), lambda h, c: (0, c, h)),
                  pl.BlockSpec((2, HD), lambda h, c: (0, h)),
                  pl.BlockSpec((1, HD), lambda h, c: (0, h)),
                  pl.BlockSpec((SUB * HD, HD), lambda h, c: (0, 0))],
        out_specs=[pl.BlockSpec((cb, HD), lambda h, c: (c, h)),
                   pl.BlockSpec((cb, HD), lambda h, c: (c, h)),
                   pl.BlockSpec((1, cb, HD), lambda h, c: (h, c, 0)),
                   pl.BlockSpec((1, nch, HD, HD), lambda h, c: (h, c, 0, 0))],
        out_shape=[jax.ShapeDtypeStruct((T, D), F32), jax.ShapeDtypeStruct((T, D), ACT),
                   jax.ShapeDtypeStruct((HEADS, T, HD), F32),
                   jax.ShapeDtypeStruct((HEADS, T // CHUNK, HD, HD), F32)],
        scratch_shapes=[pltpu.VMEM((HD, HD), F32), pltpu.VMEM((cb, HD), F32), pltpu.VMEM((cb, HD), F32),
                        pltpu.VMEM((cb, HD), F32), pltpu.VMEM((NSUB, nch * SUB, SUB * HD), MM),
                        pltpu.VMEM((cb, HD), F32)],
    )(p, hgrn_lb, hgrn_g, e_mat)
    return out[:4], out[4:]


def _hgrn_bwd(p, o, a_all, s_all, doa, hgrn_lb, hgrn_g, dp, carry):
    T = p.shape[1]
    cb = min(HGRN_BLOCK, T)
    nch = cb // CHUNK
    ncb = T // cb
    _, et_mat = _hgrn_consts()

    def body(p_ref, o_ref, a_ref, s_ref, doa_ref, lb_ref, g_ref, et_ref, dp_in, dp_ref, sm_ref,
             dst_scr, q_scr, k_scr, b_scr, x_scr, dqd_scr, dkd_scr):
        del dp_in

        @pl.when(pl.program_id(1) == 0)
        def _():
            dst_scr[...] = jnp.zeros_like(dst_scr)
            sm_ref[...] = jnp.zeros_like(sm_ref)

        qr = p_ref[0]
        v = p_ref[2]
        og = p_ref[3]
        pre = _hgrn_pre(qr, p_ref[1], lb_ref, b_scr, cb)
        q, k = pre["q"], pre["k"]
        g = g_ref[...]
        ov = o_ref[...]
        r = lax.rsqrt(_rowmean(ov * ov) + EPS)
        oh = ov * r
        sgo = _sig(og)
        doa_v = doa_ref[...]
        don = doa_v * og * sgo
        dog = doa_v * oh * g * sgo * (1.0 + og * (1.0 - sgo))
        sm_ref[1:2, :] += _colsum(don * oh)
        doh = don * g
        do = r * (doh - oh * _rowmean(doh * oh))

        sbc = lax.broadcasted_iota(jnp.int32, (CHUNK, HD), 0) // SUB
        row_i = lax.broadcasted_iota(jnp.int32, (CHUNK, HD), 0)
        lane_i = lax.broadcasted_iota(jnp.int32, (CHUNK, HD), 1)
        causal = lane_i <= row_i
        chunks = [slice(c * CHUNK, (c + 1) * CHUNK) for c in range(nch)]
        da_parts = [jnp.where(causal, _mm_nt(do[rs], _pad_rows(v[rs])), 0.0) for rs in chunks]
        dv_parts = [_mm_tn(a_ref[0, rs, :], do[rs])[:CHUNK] for rs in chunks]

        @pl.when(pre["safe"])
        def _():
            hi = dict(preferred_element_type=F32, precision=lax.Precision.HIGHEST)
            for c, rs in enumerate(chunks):
                dqd_scr[rs, :] = pre["eb"][rs] * lax.dot_general(
                    da_parts[c], _pad_rows(pre["kf"][rs]), (((1,), (0,)), ((), ())), **hi)
                dkd_scr[rs, :] = pre["ekf"][rs] * lax.dot_general(
                    da_parts[c], pre["qe"][rs], (((0,), (0,)), ((), ())), **hi)[:CHUNK]

        @pl.when(jnp.logical_not(pre["safe"]))
        def _():
            sub = _hgrn_sub(pre, b_scr, cb)
            dqoff_mm = [[_mm(da_parts[c], _pad_rows(sub["ko"][i][rs])) for i in range(1, NSUB)]
                        for c, rs in enumerate(chunks)]
            dkoff_mm = [[_mm_tn(jnp.where(sbc == i, da_parts[c], 0.0), sub["qo"][rs])[:CHUNK]
                         for i in range(1, NSUB)] for c, rs in enumerate(chunks)]
            dqoff_parts = [_by_subblock(sbc, [None] + dqoff_mm[c]) for c in range(nch)]
            dkoff_parts = []
            for c, rs in enumerate(chunks):
                dko = sub["eko"][1][rs] * dkoff_mm[c][0]
                for i in range(2, NSUB):
                    dko = dko + sub["eko"][i][rs] * dkoff_mm[c][i - 1]
                dkoff_parts.append(dko)
            q_scr[...] = q
            k_scr[...] = k
            for i in range(NSUB):
                rows = []
                for c in range(nch):
                    blk = da_parts[c][SUB * i:SUB * (i + 1)]
                    rows.append(blk if i == 0 else pltpu.roll(blk, HD - SUB * i, 1))
                x_scr[i] = _mm(jnp.concatenate(rows, axis=0), et_ref[...])
            ti = lax.broadcasted_iota(jnp.int32, (SUB, HD), 0)

            def dbody(c, carry):
                for i in range(NSUB):
                    r0 = pl.multiple_of(c * CHUNK + SUB * i, SUB)
                    qi = q_scr[pl.ds(r0, SUB), :]
                    bi = b_scr[pl.ds(r0, SUB), :]
                    dq_hi = jnp.zeros((8, HD), F32)
                    dq_lo = jnp.zeros((8, HD), F32)
                    dk_hi = jnp.zeros((8, HD), F32)
                    dk_lo = jnp.zeros((8, HD), F32)
                    c0 = pl.multiple_of(c * SUB, SUB)
                    t8 = ti[:8]
                    for s in range(SUB):
                        krow = k_scr[pl.ds(r0 + s, 1), :]
                        brow = b_scr[pl.ds(r0 + s, 1), :]
                        w_lo = (x_scr[i, pl.ds(c0 + 8, 8), s * HD:(s + 1) * HD]
                                * jnp.exp(jnp.where(t8 + 8 >= s, bi[8:] - brow, NEG)))
                        dq_lo = dq_lo + w_lo * krow
                        col = _colsum(w_lo * qi[8:])
                        if s < 8:
                            w_hi = (x_scr[i, pl.ds(c0, 8), s * HD:(s + 1) * HD]
                                    * jnp.exp(jnp.where(t8 >= s, bi[:8] - brow, NEG)))
                            dq_hi = dq_hi + w_hi * krow
                            dk_hi = jnp.where(t8 == s, col + _colsum(w_hi * qi[:8]), dk_hi)
                        else:
                            dk_lo = jnp.where(t8 + 8 == s, col, dk_lo)
                    dqd_scr[pl.ds(r0, SUB), :] = jnp.concatenate([dq_hi, dq_lo], axis=0)
                    dkd_scr[pl.ds(r0, SUB), :] = jnp.concatenate([dk_hi, dk_lo], axis=0)
                return carry

            lax.fori_loop(0, nch, dbody, 0)
            dqd_scr[...] += jnp.concatenate(dqoff_parts, axis=0) * sub["eqo"]
            dkd_scr[...] += jnp.concatenate(dkoff_parts, axis=0)

        qdo = [_mm_tn(do[rs], pre["qe"][rs]) for rs in chunks]
        dsts = [None] * nch
        dst = dst_scr[...]
        for c in reversed(range(nch)):
            dsts[c] = dst
            dst = dst * jnp.exp(b_scr[pl.ds(c * CHUNK + CHUNK - 1, 1), :]) + qdo[c]
        dst_scr[...] = dst
        sts = [s_ref[0, c] for c in range(nch)]
        dqe_parts = [_mm(do[rs], sts[c]) for c, rs in enumerate(chunks)]
        dkdec_parts = [_mm(v[rs], dsts[c]) for c, rs in enumerate(chunks)]
        dvi_parts = [_mm_nt(pre["kd"][rs], dsts[c]) for c, rs in enumerate(chunks)]
        debl_parts = [_colsum(dsts[c] * sts[c]) for c in range(nch)]
        dqe = jnp.concatenate(dqe_parts, axis=0)
        dkdec = jnp.concatenate(dkdec_parts, axis=0)
        dq_tot = dqd_scr[...] + dqe * pre["eb"]
        dk_inter = dkdec * pre["ekd"]
        dk_tot = dkd_scr[...] + dk_inter
        db = q * dq_tot - k * dk_tot
        kdk = k * dk_inter
        dbl = jnp.concatenate(
            [jnp.broadcast_to(jnp.exp(b_scr[pl.ds(c * CHUNK + CHUNK - 1, 1), :]) * debl_parts[c]
                              + _colsum(kdk[c * CHUNK:(c + 1) * CHUNK]), (CHUNK, HD)) for c in range(nch)], axis=0)
        tl = pre["tl"]
        rc = db
        sh = 1
        while sh < CHUNK:
            rc = rc + jnp.where(tl + sh < CHUNK, pltpu.roll(rc, cb - sh, 0), 0.0)
            sh *= 2
        dlf = rc + dbl
        dfg = dlf / pre["fg"] - dk_tot
        sf = pre["sf"]
        lb = pre["lb"]
        sm_ref[0:1, :] += _colsum(dfg * (1.0 - sf))
        sq = pre["sq"]
        dp_ref[0] = (dq_tot * Q_SCALE * sq * (1.0 + qr * (1.0 - sq))).astype(ACT)
        dp_ref[1] = (dfg * (1.0 - lb) * sf * (1.0 - sf)).astype(ACT)
        dp_ref[2] = (jnp.concatenate(dv_parts, axis=0) + jnp.concatenate(dvi_parts, axis=0)).astype(ACT)
        dp_ref[3] = dog.astype(ACT)

    rev = lambda c: ncb - 1 - c
    out = _gridded(
        body, carry, name="hgrn_bwd", grid=(HEADS, ncb),
        in_specs=[pl.BlockSpec((4, cb, HD), lambda h, c: (0, rev(c), h)),
                  pl.BlockSpec((cb, HD), lambda h, c: (rev(c), h)),
                  pl.BlockSpec((1, cb, HD), lambda h, c: (h, rev(c), 0)),
                  pl.BlockSpec((1, nch, HD, HD), lambda h, c: (h, rev(c), 0, 0)),
                  pl.BlockSpec((cb, HD), lambda h, c: (rev(c), h)),
                  pl.BlockSpec((2, HD), lambda h, c: (0, h)),
                  pl.BlockSpec((1, HD), lambda h, c: (0, h)),
                  pl.BlockSpec((HD, SUB * HD), lambda h, c: (0, 0)),
                  pl.BlockSpec(memory_space=pl.ANY)],
        out_specs=[pl.BlockSpec((4, cb, HD), lambda h, c: (0, rev(c), h)),
                   pl.BlockSpec((8, HD), lambda h, c: (0, h))],
        out_shape=[jax.ShapeDtypeStruct(dp.shape, dp.dtype), jax.ShapeDtypeStruct((8, D), F32)],
        aliases={8: 0},
        scratch_shapes=[pltpu.VMEM((HD, HD), F32), pltpu.VMEM((cb, HD), F32), pltpu.VMEM((cb, HD), F32),
                        pltpu.VMEM((cb, HD), F32), pltpu.VMEM((NSUB, nch * SUB, SUB * HD), F32),
                        pltpu.VMEM((cb, HD), F32), pltpu.VMEM((cb, HD), F32)],
    )(p, o, a_all, s_all, doa, hgrn_lb, hgrn_g, et_mat, dp)
    return out[:2], out[2:]


def _ln_fwd(u1, g, b):
    mu = _rowmean(u1)
    xc = u1 - mu
    rs = lax.rsqrt(_rowmean(xc * xc) + EPS)
    xh = xc * rs
    return xh * g + b, xh, rs


CONV_RB = 64
LANES = 128


def _shift_rows(src, sh, ls, n):
    for r in range(1, 8):
        sh[r - 1, 0:n, :] = src[pl.ds(r, n), ls]


def _tap(src, sh, ls, off, r0, rows):
    r = off % 8
    if r == 0:
        return src[pl.ds(r0 + off, rows), ls]
    return sh[r - 1, pl.ds(r0 + off - r, rows), :]


def _conv_fwd(p, cw, cb_, lng, lnb):
    T = p.shape[1]
    tm = min(512, T)
    n = HALO + tm - 8

    def body(p_ref, cw_ref, cb_ref, g_ref, b_ref, u1_ref, u2_ref, buf, sh):
        @pl.when(pl.program_id(0) == 0)
        def _():
            buf[0:HALO, :] = jnp.zeros((HALO, D), F32)

        buf[HALO:HALO + tm, :] = p_ref[0] * _sig(p_ref[1])
        for lb in range(D // LANES):
            ls = slice(lb * LANES, (lb + 1) * LANES)
            _shift_rows(buf, sh, ls, n)
            taps = [cw_ref[j:j + 1, ls] for j in range(CONV_K)]
            bias = cb_ref[:, ls]

            def rows_body(rb, carry):
                r0 = pl.multiple_of(rb * CONV_RB, CONV_RB)
                acc = jnp.broadcast_to(bias, (CONV_RB, LANES))
                for j in range(CONV_K):
                    acc = acc + taps[j] * _tap(buf, sh, ls, HALO - (CONV_K - 1) + j, r0, CONV_RB)
                u1_ref[pl.ds(r0, CONV_RB), ls] = acc
                return carry

            lax.fori_loop(0, tm // CONV_RB, rows_body, 0)
        y, _, _ = _ln_fwd(u1_ref[...], g_ref[...], b_ref[...])
        u2_ref[...] = (y * _sig(y)).astype(ACT)
        buf[0:HALO, :] = buf[tm:tm + HALO, :]

    return pl.pallas_call(
        body, name="conv_fwd", grid=(T // tm,),
        in_specs=[pl.BlockSpec((2, tm, D), lambda i: (2, i, 0)), pl.BlockSpec((HALO, D), lambda i: (0, 0)),
                  pl.BlockSpec((1, D), lambda i: (0, 0)), pl.BlockSpec((1, D), lambda i: (0, 0)),
                  pl.BlockSpec((1, D), lambda i: (0, 0))],
        out_specs=[pl.BlockSpec((tm, D), lambda i: (i, 0)), pl.BlockSpec((tm, D), lambda i: (i, 0))],
        out_shape=[jax.ShapeDtypeStruct((T, D), F32), jax.ShapeDtypeStruct((T, D), ACT)],
        scratch_shapes=[pltpu.VMEM((HALO + tm, D), F32), pltpu.VMEM((7, n, LANES), F32)],
        compiler_params=_cparams(1),
    )(p, cw, cb_, lng, lnb)


def _conv_bwd(p, u1, du2, cw, lng, lnb, dp):
    T = p.shape[1]
    tm = min(512, T)
    ni = T // tm
    hb = tm // HALO

    n = HALO + tm - 8

    def body(p_ref, ph_ref, u1_ref, du2_ref, cw_ref, g_ref, b_ref, dp_in, dp_ref, dcw_ref, sm_ref, ubuf, dbuf,
             sh, dacc):
        del dp_in
        step = pl.program_id(0)

        @pl.when(step == 0)
        def _():
            dbuf[tm:tm + HALO, :] = jnp.zeros((HALO, D), F32)
            dcw_ref[...] = jnp.zeros_like(dcw_ref)
            sm_ref[...] = jnp.zeros_like(sm_ref)

        ua = p_ref[0]
        sgb = _sig(p_ref[1])
        halo = ph_ref[0] * _sig(ph_ref[1])
        ubuf[0:HALO, :] = jnp.where(step == ni - 1, 0.0, halo)
        ubuf[HALO:HALO + tm, :] = ua * sgb
        g = g_ref[...]
        y, xh, rs = _ln_fwd(u1_ref[...], g, b_ref[...])
        sy = _sig(y)
        dy = du2_ref[...] * sy * (1.0 + y * (1.0 - sy))
        sm_ref[1:2, :] += _colsum(dy * xh)
        sm_ref[2:3, :] += _colsum(dy)
        dxh = dy * g
        du1 = rs * (dxh - _rowmean(dxh) - xh * _rowmean(dxh * xh))
        sm_ref[0:1, :] += _colsum(du1)
        dbuf[0:tm, :] = du1
        for lb in range(D // LANES):
            ls = slice(lb * LANES, (lb + 1) * LANES)
            taps = [cw_ref[j:j + 1, ls] for j in range(CONV_K)]
            _shift_rows(dbuf, sh, ls, n)

            def du0_body(rb, carry):
                r0 = pl.multiple_of(rb * CONV_RB, CONV_RB)
                acc = jnp.zeros((CONV_RB, LANES), F32)
                for j in range(CONV_K):
                    acc = acc + taps[j] * _tap(dbuf, sh, ls, CONV_K - 1 - j, r0, CONV_RB)
                dp_ref[0, pl.ds(r0, CONV_RB), ls] = acc.astype(ACT)
                return carry

            lax.fori_loop(0, tm // CONV_RB, du0_body, 0)
            _shift_rows(ubuf, sh, ls, n)
            dacc[...] = jnp.zeros_like(dacc)

            def dcw_body(rb, carry):
                r0 = pl.multiple_of(rb * CONV_RB, CONV_RB)
                d = dbuf[pl.ds(r0, CONV_RB), ls]
                for j in range(CONV_K):
                    prod = d * _tap(ubuf, sh, ls, HALO - (CONV_K - 1) + j, r0, CONV_RB)
                    dacc[8 * j:8 * j + 8, :] += jnp.sum(prod.reshape(CONV_RB // 8, 8, LANES), axis=0)
                return carry

            lax.fori_loop(0, tm // CONV_RB, dcw_body, 0)
            for j in range(CONV_K):
                dcw_ref[j:j + 1, ls] += _colsum(dacc[8 * j:8 * j + 8, :])
        du0 = dp_ref[0].astype(F32)
        dp_ref[0] = (du0 * sgb).astype(ACT)
        dp_ref[1] = (du0 * ua * sgb * (1.0 - sgb)).astype(ACT)
        dbuf[tm:tm + HALO, :] = dbuf[0:HALO, :]

    rev = lambda i: ni - 1 - i
    return pl.pallas_call(
        body, name="conv_bwd", grid=(ni,),
        in_specs=[pl.BlockSpec((2, tm, D), lambda i: (2, rev(i), 0)),
                  pl.BlockSpec((2, HALO, D), lambda i: (2, jnp.maximum(rev(i) * hb - 1, 0), 0)),
                  pl.BlockSpec((tm, D), lambda i: (rev(i), 0)), pl.BlockSpec((tm, D), lambda i: (rev(i), 0)),
                  pl.BlockSpec((HALO, D), lambda i: (0, 0)), pl.BlockSpec((1, D), lambda i: (0, 0)),
                  pl.BlockSpec((1, D), lambda i: (0, 0)), pl.BlockSpec(memory_space=pl.ANY)],
        out_specs=[pl.BlockSpec((2, tm, D), lambda i: (2, rev(i), 0)),
                   pl.BlockSpec((HALO, D), lambda i: (0, 0)), pl.BlockSpec((8, D), lambda i: (0, 0))],
        out_shape=[jax.ShapeDtypeStruct(dp.shape, dp.dtype), jax.ShapeDtypeStruct((HALO, D), F32),
                   jax.ShapeDtypeStruct((8, D), F32)],
        input_output_aliases={7: 0},
        scratch_shapes=[pltpu.VMEM((HALO + tm, D), F32), pltpu.VMEM((tm + HALO, D), F32),
                        pltpu.VMEM((7, n, LANES), F32), pltpu.VMEM((8 * CONV_K, LANES), F32)],
        compiler_params=_cparams(1),
    )(p, p, u1, du2, cw, lng, lnb, dp)


def _mixout_fwd(x, oa, u2, p, mod, mo, w_a, w_b, w_o):
    T = x.shape[0]
    tm = min(512, T)

    def body(x_ref, oa_ref, u2_ref, p_ref, mod_ref, wa_ref, wb_ref, wo_ref, xo_ref, ya_ref, yb_ref, mo_ref):
        ya = _mm(oa_ref[...], wa_ref[...])
        yb = _mm(u2_ref[...], wb_ref[...])
        ya_ref[...] = ya.astype(ACT)
        yb_ref[...] = yb.astype(ACT)
        merged = _sig(p_ref[0]) * ya + _sig(p_ref[1]) * yb
        out = _mm(merged, wo_ref[...])
        mo_ref[...] = out
        xo_ref[...] = x_ref[...] + mod_ref[mo + 2:mo + 3, :] * out

    tile = pl.BlockSpec((tm, D), lambda i: (i, 0))
    wspec = pl.BlockSpec((D, D), lambda i: (0, 0))
    return pl.pallas_call(
        body, name="mixout_fwd", grid=(T // tm,),
        in_specs=[tile, tile, tile, pl.BlockSpec((2, tm, D), lambda i: (3, i, 0)),
                  pl.BlockSpec((9, D), lambda i: (0, 0)), wspec, wspec, wspec],
        out_specs=[tile, tile, tile, tile],
        out_shape=[jax.ShapeDtypeStruct((T, D), F32), jax.ShapeDtypeStruct((T, D), ACT),
                   jax.ShapeDtypeStruct((T, D), ACT), jax.ShapeDtypeStruct((T, D), F32)],
        compiler_params=_cparams(1),
    )(x, oa, u2, p, mod, w_a, w_b, w_o)


def _mixout_bwd(dxo, oa, u2, ya, yb, mout, p, mod, mo, w_a, w_b, w_o):
    T = dxo.shape[0]
    tm = min(256, T)

    def body(dxo_ref, oa_ref, u2_ref, ya_ref, yb_ref, mo_ref, p_ref, mod_ref, wa_ref, wb_ref, wo_ref,
             dp_ref, doa_ref, du2_ref, dwa_ref, dwb_ref, dwo_ref, sm_ref):
        @pl.when(pl.program_id(0) == 0)
        def _():
            dwa_ref[...] = jnp.zeros_like(dwa_ref)
            dwb_ref[...] = jnp.zeros_like(dwb_ref)
            dwo_ref[...] = jnp.zeros_like(dwo_ref)
            sm_ref[...] = jnp.zeros_like(sm_ref)

        dxo_v = dxo_ref[...]
        sm_ref[2:3, :] += _colsum(dxo_v * mo_ref[...])
        dmo = (mod_ref[mo + 2:mo + 3, :] * dxo_v).astype(MM)
        ya = ya_ref[...].astype(F32)
        yb = yb_ref[...].astype(F32)
        sga = _sig(p_ref[0])
        sgb = _sig(p_ref[1])
        merged = (sga * ya + sgb * yb).astype(MM)
        dwo_ref[...] += _mm_tn(merged, dmo)
        dmg = _mm_nt(dmo, wo_ref[...])
        dp_ref[0] = (dmg * ya * sga * (1.0 - sga)).astype(ACT)
        dp_ref[1] = (dmg * yb * sgb * (1.0 - sgb)).astype(ACT)
        dya = (dmg * sga).astype(MM)
        dyb = (dmg * sgb).astype(MM)
        dwa_ref[...] += _mm_tn(oa_ref[...], dya)
        dwb_ref[...] += _mm_tn(u2_ref[...], dyb)
        doa_ref[...] = _mm_nt(dya, wa_ref[...])
        du2_ref[...] = _mm_nt(dyb, wb_ref[...])

    tile = pl.BlockSpec((tm, D), lambda i: (i, 0))
    wspec = pl.BlockSpec((D, D), lambda i: (0, 0))
    return pl.pallas_call(
        body, name="mixout_bwd", grid=(T // tm,),
        in_specs=[tile, tile, tile, tile, tile, tile, pl.BlockSpec((2, tm, D), lambda i: (3, i, 0)),
                  pl.BlockSpec((9, D), lambda i: (0, 0)), wspec, wspec, wspec],
        out_specs=[pl.BlockSpec((2, tm, D), lambda i: (3, i, 0)), tile, tile, wspec, wspec, wspec,
                   pl.BlockSpec((8, D), lambda i: (0, 0))],
        out_shape=[jax.ShapeDtypeStruct((8, T, D), ACT), jax.ShapeDtypeStruct((T, D), F32),
                   jax.ShapeDtypeStruct((T, D), F32), jax.ShapeDtypeStruct((D, D), F32),
                   jax.ShapeDtypeStruct((D, D), F32), jax.ShapeDtypeStruct((D, D), F32),
                   jax.ShapeDtypeStruct((8, D), F32)],
        compiler_params=_cparams(1),
    )(dxo, oa, u2, ya, yb, mout, p, mod, w_a, w_b, w_o)


def _ada_wgrad(cs_all, dmod_cols):
    cs_t = jnp.pad(cs_all.T, ((0, 0), (0, HD - N_DEV)))
    dm = jnp.pad(dmod_cols, ((0, HD - N_DEV), (0, 0)))

    def body(cs_ref, d_ref, out_ref):
        out_ref[...] = jnp.dot(cs_ref[...], d_ref[...], preferred_element_type=F32,
                               precision=lax.Precision.HIGHEST)

    return pl.pallas_call(
        body, name="ada_wgrad", out_shape=jax.ShapeDtypeStruct((D, dmod_cols.shape[1]), F32),
        compiler_params=pltpu.CompilerParams(vmem_limit_bytes=VMEM_LIMIT),
    )(cs_t, dm)


def _adam_math(w, g, m, v):
    m2 = ADAM_B1 * m + (1.0 - ADAM_B1) * g
    v2 = ADAM_B2 * v + (1.0 - ADAM_B2) * (g * g)
    m_hat = m2 / (1.0 - ADAM_B1 ** ADAM_STEP)
    v_hat = v2 / (1.0 - ADAM_B2 ** ADAM_STEP)
    delta = -ADAM_LR * (m_hat / (jnp.sqrt(v_hat) + ADAM_EPS) + ADAM_WD * w)
    return delta, m2, v2


def _adamw(w, m, v, g, name):
    R, C = w.shape
    slots = g.ndim == 3
    n_slots = g.shape[0] if slots else 0
    tr = R
    for cand in (256, 176):
        if R % cand == 0 and R > cand:
            tr = cand
            break

    def body(w_ref, m_ref, v_ref, g_ref, go_ref, d_ref, mo_ref, vo_ref):
        if slots:
            gv = g_ref[0].astype(F32)
            for s in range(1, n_slots):
                gv = gv + g_ref[s].astype(F32)
        else:
            gv = g_ref[...]
        go_ref[...] = gv
        d_ref[...], mo_ref[...], vo_ref[...] = _adam_math(w_ref[...], gv, m_ref[...], v_ref[...])

    tile = pl.BlockSpec((tr, C), lambda i: (i, 0))
    gspec = pl.BlockSpec((n_slots, tr, C), lambda i: (0, i, 0)) if slots else tile
    sds = jax.ShapeDtypeStruct((R, C), F32)
    return pl.pallas_call(
        body, name=name, grid=(R // tr,), in_specs=[tile, tile, tile, gspec], out_specs=[tile] * 4,
        out_shape=[sds] * 4, compiler_params=_cparams(1),
    )(w, m, v, g)


def _sum_slots(pack, name, tr):
    n, R, C = pack.shape

    def body(p_ref, out_ref):
        acc = p_ref[0].astype(F32)
        for s in range(1, n):
            acc = acc + p_ref[s].astype(F32)
        out_ref[...] = acc

    return pl.pallas_call(
        body, name=name, grid=(R // tr,), in_specs=[pl.BlockSpec((n, tr, C), lambda i: (0, i, 0))],
        out_specs=pl.BlockSpec((tr, C), lambda i: (i, 0)), out_shape=jax.ShapeDtypeStruct((R, C), F32),
        compiler_params=_cparams(1))(pack)


def _me():
    return lax.axis_index("x"), lax.axis_index("y"), lax.axis_index("c")


def _peer(r):
    x, y, c = _me()
    px = 1 - x if r & 4 else x
    py = 1 - y if r & 2 else y
    pc = 1 - c if r & 1 else c
    return (px, py, pc), 4 * px + 2 * py + pc


def _small_gather(x_ref, out_ref, send_sems, recv_sems):
    R = x_ref.shape[0]
    mx, my, mc = _me()
    me = 4 * mx + 2 * my + mc
    mine = out_ref.at[pl.ds(pl.multiple_of(me * R, 8), R), :]
    copies = []
    for r in range(1, N_DEV):
        dev, _ = _peer(r)
        copies.append(pltpu.make_async_remote_copy(
            src_ref=x_ref, dst_ref=mine, send_sem=send_sems.at[r - 1], recv_sem=recv_sems.at[r - 1],
            device_id=dev, device_id_type=MESH))
    for cp in copies:
        cp.start()
    mine[...] = x_ref[...]
    for r in range(1, N_DEV):
        dev, idx = _peer(r)
        theirs = out_ref.at[pl.ds(pl.multiple_of(idx * R, 8), R), :]
        pltpu.make_async_remote_copy(
            src_ref=x_ref, dst_ref=theirs, send_sem=send_sems.at[r - 1], recv_sem=recv_sems.at[r - 1],
            device_id=dev, device_id_type=MESH).wait_recv()
    for cp in copies:
        cp.wait_send()


def _prologue(cs, ada_w, ada_b_cols, big):
    n = len(big)
    ncol = ada_w.shape[1]
    big_shape, big_sems = _xchg_specs(big, "gather")

    def body(cs_ref, w_ref, b_ref, *rest):
        big_in, cs_all, mod_all, big_out = rest[:n], rest[n], rest[n + 1], rest[n + 2:2 * n + 2]
        mod_scr, s1, r1, s2, r2 = rest[2 * n + 2:2 * n + 7]
        sems = rest[2 * n + 7:]
        _xchg_start(big_in, big_out, sems, "gather")
        _small_gather(cs_ref, cs_all, s1, r1)
        pick = (lax.broadcasted_iota(jnp.int32, (N_DEV, N_DEV * 8), 1)
                == 8 * lax.broadcasted_iota(jnp.int32, (N_DEV, N_DEV * 8), 0)).astype(F32)
        per_device = jnp.dot(pick, cs_all[...], preferred_element_type=F32, precision=lax.Precision.HIGHEST)
        mod_scr[...] = jnp.dot(per_device, w_ref[...], preferred_element_type=F32,
                               precision=lax.Precision.HIGHEST) + b_ref[...]
        _small_gather(mod_scr, mod_all, s2, r2)
        _xchg_wait(big_in, big_out, sems, "gather")

    vmem = pl.BlockSpec(memory_space=pltpu.VMEM)
    hbm = pl.BlockSpec(memory_space=pl.ANY)
    dma7 = pltpu.SemaphoreType.DMA((N_DEV - 1,))
    out = pl.pallas_call(
        body, name="prologue",
        out_shape=[jax.ShapeDtypeStruct((N_DEV * 8, D), F32), jax.ShapeDtypeStruct((N_DEV * 8, ncol), F32)]
        + big_shape,
        in_specs=[vmem, vmem, vmem] + [hbm] * n, out_specs=[vmem, vmem] + [hbm] * n,
        scratch_shapes=[pltpu.VMEM((8, ncol), F32), dma7, dma7, dma7, dma7] + big_sems,
        compiler_params=pltpu.CompilerParams(vmem_limit_bytes=VMEM_LIMIT),
    )(cs, ada_w, ada_b_cols, *big)
    return out[0], out[1], out[2:]


def _allgather_small(x):
    R, C = x.shape

    def body(x_ref, out_ref, send_sems, recv_sems):
        _small_gather(x_ref, out_ref, send_sems, recv_sems)

    return pl.pallas_call(
        body, name="allgather_small_%dx%d" % (R, C),
        out_shape=jax.ShapeDtypeStruct((N_DEV * R, C), F32),
        in_specs=[pl.BlockSpec(memory_space=pltpu.VMEM)], out_specs=pl.BlockSpec(memory_space=pltpu.VMEM),
        scratch_shapes=[pltpu.SemaphoreType.DMA((N_DEV - 1,)), pltpu.SemaphoreType.DMA((N_DEV - 1,))],
    )(x)


N_CHIP = N_DEV // 2


def _xchg_copies(ins, outs, sems, mode):
    send_sems, recv_sems, local_sems = sems
    mx, my, mc = _me()
    me = 4 * mx + 2 * my + mc
    my_chip = 2 * mx + my
    sibling = _peer(1)[0]

    def rdma(a, r, dev, src, slot):
        k = a * (N_DEV - 1) + r - 1
        return pltpu.make_async_remote_copy(
            src_ref=src, dst_ref=outs[a].at[slot], send_sem=send_sems.at[k], recv_sem=recv_sems.at[k],
            device_id=dev, device_id_type=MESH)

    own, sends, relays, recvs = [], [], [], []
    for a in range(len(ins)):
        if mode == "pair":
            for chip in range(N_CHIP):
                src = ins[a].at[2 * chip + 1 - mc]
                sends.append(rdma(a, chip + 1, sibling, src, chip))
                recvs.append(rdma(a, chip + 1, sibling, src, chip))
            continue
        if mode == "quad":
            own.append(pltpu.make_async_copy(ins[a].at[my_chip], outs[a].at[my_chip], local_sems.at[a]))
            for r in (2, 4, 6):
                dev, idx = _peer(r)
                chip = idx // 2
                sends.append(rdma(a, r, dev, ins[a].at[chip], my_chip))
                recvs.append(rdma(a, r, dev, ins[a].at[chip], chip))
            continue
        gather = mode == "gather"
        own.append(pltpu.make_async_copy(ins[a] if gather else ins[a].at[me], outs[a].at[me], local_sems.at[a]))
        for r in range(1, N_DEV):
            dev, idx = _peer(r)
            if not gather:
                sends.append(rdma(a, r, dev, ins[a].at[idx], me))
                recvs.append(rdma(a, r, dev, ins[a].at[idx], idx))
            elif r == 1:
                sends.append(rdma(a, r, dev, ins[a], me))
                recvs.append(rdma(a, r, dev, ins[a], idx))
            elif r % 2 == 0:
                sends.append(rdma(a, r, dev, ins[a], me))
                relays.append((rdma(a, r, dev, ins[a], idx), rdma(a, r + 1, sibling, outs[a].at[idx], idx)))
            else:
                recvs.append(rdma(a, r, sibling, ins[a], idx))
    return own, sends, relays, recvs


def _xchg_start(ins, outs, sems, mode):
    own, sends, _, _ = _xchg_copies(ins, outs, sems, mode)
    for cp in own + sends:
        cp.start()


def _xchg_wait(ins, outs, sems, mode):
    own, sends, relays, recvs = _xchg_copies(ins, outs, sems, mode)
    for arrival, relay in relays:
        arrival.wait_recv()
        relay.start()
    for cp in recvs:
        cp.wait_recv()
    for cp in own:
        cp.wait()
    for cp in sends + [relay for _, relay in relays]:
        cp.wait_send()


def _xchg_specs(arrays, mode):
    n = len(arrays)
    shape = {"gather": lambda s: (N_DEV,) + s, "scatter": lambda s: s, "pair": lambda s: (N_CHIP,) + s[1:],
             "quad": lambda s: s}[mode]
    out_shape = [jax.ShapeDtypeStruct(shape(a.shape), a.dtype) for a in arrays]
    sems = [pltpu.SemaphoreType.DMA((n * (N_DEV - 1),)), pltpu.SemaphoreType.DMA((n * (N_DEV - 1),)),
            pltpu.SemaphoreType.DMA((n,))]
    return out_shape, sems


def _exchange(arrays, mode, name):
    n = len(arrays)

    def body(*refs):
        _xchg_start(refs[:n], refs[n:2 * n], refs[2 * n:], mode)
        _xchg_wait(refs[:n], refs[n:2 * n], refs[2 * n:], mode)

    out_shape, sems = _xchg_specs(arrays, mode)
    return pl.pallas_call(
        body, name=name, out_shape=out_shape,
        in_specs=[pl.BlockSpec(memory_space=pl.ANY)] * n, out_specs=[pl.BlockSpec(memory_space=pl.ANY)] * n,
        scratch_shapes=sems,
    )(*arrays)


def _gridded(body, carry, *, name, grid, in_specs, out_specs, out_shape, scratch_shapes=(), aliases=None):
    if carry is None:
        return pl.pallas_call(
            body, name=name, grid=grid, in_specs=list(in_specs), out_specs=list(out_specs),
            out_shape=list(out_shape), scratch_shapes=list(scratch_shapes), input_output_aliases=aliases or {},
            compiler_params=_cparams(len(grid)))
    arrays, mode = carry
    n, n_in, n_out, n_scr = len(arrays), len(in_specs), len(out_specs), len(scratch_shapes)
    c_shape, c_sems = _xchg_specs(arrays, mode)

    def wrapped(*refs):
        ins, cin = refs[:n_in], refs[n_in:n_in + n]
        o0 = n_in + n
        outs, cout = refs[o0:o0 + n_out], refs[o0 + n_out:o0 + n_out + n]
        s0 = o0 + n_out + n
        scr, sems = refs[s0:s0 + n_scr], refs[s0 + n_scr:]
        first = pl.program_id(0) == 0
        last = pl.program_id(0) == grid[0] - 1
        for ax in range(1, len(grid)):
            first = first & (pl.program_id(ax) == 0)
            last = last & (pl.program_id(ax) == grid[ax] - 1)

        @pl.when(first)
        def _():
            _xchg_start(cin, cout, sems, mode)

        body(*ins, *outs, *scr)

        @pl.when(last)
        def _():
            _xchg_wait(cin, cout, sems, mode)

    hbm = pl.BlockSpec(memory_space=pl.ANY)
    res = pl.pallas_call(
        wrapped, name=name, grid=grid, in_specs=list(in_specs) + [hbm] * n, out_specs=list(out_specs) + [hbm] * n,
        out_shape=list(out_shape) + c_shape, scratch_shapes=list(scratch_shapes) + c_sems,
        input_output_aliases=aliases or {}, compiler_params=_cparams(len(grid)),
    )
    return lambda *args: res(*args, *arrays)


def _local_step(x, target, mod, small, sh, w1):
    w1_in, w1_out = w1[0].reshape(2, D_FF, D), w1[1].reshape(D_FF, D)
    (x1, a1, b1, f1, h1), (wm_in,) = _ffn_fwd(x, mod, 0, small["norm_ffn1"], w1_in, w1_out, 0.5, "ffn1_fwd",
                                              ([sh["mix_w_in"]], "gather"))
    (p, h2), (wh_o, wc_o, wm_o, cw) = _mixin_fwd(
        x1, mod, 3, small["norm_mix"], wm_in,
        ([sh["hgrn_w_o"], sh["conv_w_o"], sh["mix_w_out"], sh["conv_w"]], "gather"))
    wh_o, wc_o, wm_o = wh_o.reshape(D, D), wc_o.reshape(D, D), wm_o.reshape(D, D)
    cw = jnp.pad(cw.transpose(1, 0, 2).reshape(CONV_K, D), ((0, HALO - CONV_K), (0, 0)))
    (o, oa, a_all, s_all), (w2_in, w2_out) = _hgrn_fwd(p, small["hgrn_lb"], small["hgrn_g"],
                                                       ([sh["ffn2_w_in"], sh["ffn2_w_out"]], "gather"))
    w2_in, w2_out = w2_in.reshape(2, D_FF, D), w2_out.reshape(D_FF, D)
    u1, u2 = _conv_fwd(p, cw, small["conv_b"], small["conv_ln_g"], small["conv_ln_b"])
    x2, ya, yb, mout = _mixout_fwd(x1, oa, u2, p, mod, 3, wh_o, wc_o, wm_o)
    (x3, a3, b3, f3, h3), _ = _ffn_fwd(x2, mod, 6, small["norm_ffn2"], w2_in, w2_out, 0.5, "ffn2_fwd", None)
    dx3, df3, sm_head = _head(x3, target, small["norm_final"], mod, 8, 0.5)

    (da3, db3, dw2_in, dw2_out), _ = _ffn_bwd_w(h3, df3, a3, b3, w2_out, "ffn2_bwd_w", None)
    (dx2, sm3), _ = _ffn_bwd_x(x2, dx3, f3, da3, db3, mod, 6, small["norm_ffn2"], w2_in, 0.5, "ffn2_bwd_x", None)
    dp, doa, du2, dwh_o, dwc_o, dwm_o, sm_mo = _mixout_bwd(dx2, oa, u2, ya, yb, mout, p, mod, 3, wh_o, wc_o, wm_o)
    dp, dcw, sm_cv = _conv_bwd(p, u1, du2, cw, small["conv_ln_g"], small["conv_ln_b"], dp)
    rows = lambda t: t.reshape(N_DEV, -1, D).astype(MM)
    (dp, sm_hg), (r2_in, r2_out) = _hgrn_bwd(p, o, a_all, s_all, doa, small["hgrn_lb"], small["hgrn_g"], dp,
                                             ([rows(dw2_in), rows(dw2_out)], "scatter"))
    (dx1, dwm_in, sm2, df1), (rh_o, rc_o, rm_o, rcw) = _mixin_bwd(
        x1, h2, dx2, dp, mod, 3, small["norm_mix"], wm_in, 2, 0.5,
        ([rows(dwh_o), rows(dwc_o), rows(dwm_o), dcw[:CONV_K].reshape(CONV_K, N_DEV, -1).transpose(1, 0, 2)],
         "scatter"))
    (da1, db1, dw1_in, dw1_out), (rm_in,) = _ffn_bwd_w(h1, df1, a1, b1, w1_out, "ffn1_bwd_w",
                                                      (_pair_reduce([dwm_in], "pair_mix"), "quad"))
    (dx0, sm1), (r1_in, r1_out) = _ffn_bwd_x(
        x, dx1, f1, da1, db1, mod, 0, small["norm_ffn1"], w1_in, 0.5, "ffn1_bwd_x",
        (_pair_reduce([rows(dw1_in), rows(dw1_out)], "pair_ffn1"), "quad"))

    dmod = jnp.concatenate([sm1[0:3], sm2[0:2], sm_mo[2:3], sm3[0:3]], axis=0)
    gsmall = dict(norm_ffn1=sm1[3:4], norm_mix=sm2[3:4], lb0=sm_hg[0:1], hgrn_g=sm_hg[1:2], conv_b=sm_cv[0:1],
                  conv_ln_g=sm_cv[1:2], conv_ln_b=sm_cv[2:3], norm_ffn2=sm3[3:4], norm_final=sm_head[0:1])
    recv = dict(ffn1_w_in=r1_in, ffn1_w_out=r1_out, mix_w_in=rm_in, hgrn_w_o=rh_o, conv_w=rcw, conv_w_o=rc_o,
                mix_w_out=rm_o, ffn2_w_in=r2_in, ffn2_w_out=r2_out)
    return sm_head[1, 0], dx0, dmod, gsmall, recv


def _pair_add(mine, theirs, core, name):
    _, R, C = theirs.shape

    def body(core_ref, a_ref, b_ref, out_ref):
        del core_ref
        out_ref[0] = (a_ref[0, 0].astype(F32) + b_ref[0].astype(F32)).astype(out_ref.dtype)

    blk = pl.BlockSpec((1, R, C), lambda s, core_ref: (s, 0, 0))
    grid_spec = pltpu.PrefetchScalarGridSpec(
        num_scalar_prefetch=1, grid=(N_CHIP,),
        in_specs=[pl.BlockSpec((1, 1, R, C), lambda s, core_ref: (s, core_ref[0], 0, 0)), blk], out_specs=blk)
    return pl.pallas_call(body, name=name, grid_spec=grid_spec,
                          out_shape=jax.ShapeDtypeStruct(theirs.shape, mine.dtype), compiler_params=_cparams(1),
                          )(core, mine.reshape(N_CHIP, 2, R, C), theirs)


def _pair_reduce(arrays, name):
    theirs = _exchange(arrays, "pair", name)
    core = lax.axis_index("c").astype(jnp.int32).reshape(1)
    return [_pair_add(a, t, core, "%s_add%d" % (name, i)) for i, (a, t) in enumerate(zip(arrays, theirs))]


SMALL_ORDER = ("norm_ffn1", "norm_mix", "lb0", "hgrn_g", "conv_b", "conv_ln_g", "conv_ln_b", "norm_ffn2",
               "norm_final")
PACK_ROWS = 24


def kernel(x, c, ada_w, ada_b, norm_ffn1, ffn1_w_in, ffn1_w_out, norm_mix, mix_w_in, hgrn_lb, hgrn_g, hgrn_w_o, conv_w, conv_b, conv_ln_g, conv_ln_b, conv_w_o, mix_w_out, norm_ffn2, ffn2_w_in, ffn2_w_out, norm_final, loss_target, m_ada_w, m_ada_b, m_norm_ffn1, m_ffn1_w_in, m_ffn1_w_out, m_norm_mix, m_mix_w_in, m_hgrn_lb, m_hgrn_g, m_hgrn_w_o, m_conv_w, m_conv_b, m_conv_ln_g, m_conv_ln_b, m_conv_w_o, m_mix_w_out, m_norm_ffn2, m_ffn2_w_in, m_ffn2_w_out, m_norm_final, v_ada_w, v_ada_b, v_norm_ffn1, v_ffn1_w_in, v_ffn1_w_out, v_norm_mix, v_mix_w_in, v_hgrn_lb, v_hgrn_g, v_hgrn_w_o, v_conv_w, v_conv_b, v_conv_ln_g, v_conv_ln_b, v_conv_w_o, v_mix_w_out, v_norm_ffn2, v_ffn2_w_in, v_ffn2_w_out, v_norm_final):
    mx, my, mc = _me()
    me = 4 * mx + 2 * my + mc
    ncol = ada_w.shape[2]

    sh = dict(ffn1_w_out=ffn1_w_out, mix_w_in=mix_w_in, hgrn_w_o=hgrn_w_o, conv_w_o=conv_w_o,
              mix_w_out=mix_w_out, ffn2_w_out=ffn2_w_out)
    sh = {n: w[0].astype(MM) for n, w in sh.items()}
    sh["ffn1_w_in"] = ffn1_w_in[0].T.astype(MM)
    sh["ffn2_w_in"] = ffn2_w_in[0].T.astype(MM)
    sh["conv_w"] = conv_w[0]
    small = dict(norm_ffn1=norm_ffn1, norm_mix=norm_mix, hgrn_lb=hgrn_lb, hgrn_g=hgrn_g, conv_b=conv_b,
                 conv_ln_g=conv_ln_g, conv_ln_b=conv_ln_b, norm_ffn2=norm_ffn2, norm_final=norm_final.reshape(1, D))

    cs = jnp.broadcast_to(c * jax.nn.sigmoid(c), (8, D))
    ada_b_cols = lax.dynamic_slice(ada_b, (0, me * ncol), (1, ncol))
    cs_all, mod_all, w1 = _prologue(cs, ada_w[0], ada_b_cols, [sh["ffn1_w_in"], sh["ffn1_w_out"]])
    cs_all = cs_all.reshape(N_DEV, 8, D)[:, 0, :]
    mod = lax.dynamic_index_in_dim(mod_all.reshape(N_DEV, N_DEV, ncol), me, axis=1, keepdims=False).reshape(9, D)

    loss_local, dx, dmod, gsmall, recv = _local_step(x[0], loss_target[0], mod, small, sh, w1)
    loss = lax.psum(loss_local, ("x", "y", "c"))

    pack = jnp.concatenate([dmod] + [gsmall[n] for n in SMALL_ORDER]
                           + [jnp.zeros((PACK_ROWS - 9 - len(SMALL_ORDER), D), F32)], axis=0)
    pack_all = _allgather_small(pack).reshape(N_DEV, PACK_ROWS, D)
    tot = _sum_slots(pack_all, "sum_small", PACK_ROWS)
    gs = {n: tot[9 + i:10 + i] for i, n in enumerate(SMALL_ORDER)}
    dmod_all = pack_all[:, 0:9, :].reshape(N_DEV, 9 * D)
    g_ada_b = tot[0:9].reshape(1, 9 * D)
    g_ada_w = _ada_wgrad(cs_all, lax.dynamic_slice(dmod_all, (0, me * ncol), (N_DEV, ncol)))
    z = hgrn_lb.astype(F32)
    p0 = jax.nn.sigmoid(z[0:1] - z[1:2])
    dz0 = p0 * (1.0 - p0) * gs["lb0"]
    g_hgrn_lb = jnp.concatenate([dz0, -dz0], axis=0)

    res = {}
    res["ada_w"] = _adamw(ada_w[0], m_ada_w[0], v_ada_w[0], g_ada_w, "adamw_ada_w")
    big = dict(ffn1_w_in=(ffn1_w_in, m_ffn1_w_in, v_ffn1_w_in), ffn1_w_out=(ffn1_w_out, m_ffn1_w_out, v_ffn1_w_out),
               mix_w_in=(mix_w_in, m_mix_w_in, v_mix_w_in), hgrn_w_o=(hgrn_w_o, m_hgrn_w_o, v_hgrn_w_o),
               conv_w=(conv_w, m_conv_w, v_conv_w), conv_w_o=(conv_w_o, m_conv_w_o, v_conv_w_o),
               mix_w_out=(mix_w_out, m_mix_w_out, v_mix_w_out), ffn2_w_in=(ffn2_w_in, m_ffn2_w_in, v_ffn2_w_in),
               ffn2_w_out=(ffn2_w_out, m_ffn2_w_out, v_ffn2_w_out))
    for n, (w, m, v) in big.items():
        g = recv[n]
        if n in ("ffn1_w_in", "ffn2_w_in"):
            g = _sum_slots(g, "sum_" + n, g.shape[1] // 4).T
        res[n] = _adamw(w[0], m[0], v[0], g, "adamw_" + n)
    sm_names = ("ada_b", "norm_ffn1", "norm_mix", "hgrn_lb", "hgrn_g", "conv_b", "conv_ln_g", "conv_ln_b",
                "norm_ffn2", "norm_final")
    sm_w = dict(ada_b=(ada_b, m_ada_b, v_ada_b), norm_ffn1=(norm_ffn1, m_norm_ffn1, v_norm_ffn1),
                norm_mix=(norm_mix, m_norm_mix, v_norm_mix), hgrn_lb=(hgrn_lb, m_hgrn_lb, v_hgrn_lb),
                hgrn_g=(hgrn_g, m_hgrn_g, v_hgrn_g), conv_b=(conv_b, m_conv_b, v_conv_b),
                conv_ln_g=(conv_ln_g, m_conv_ln_g, v_conv_ln_g), conv_ln_b=(conv_ln_b, m_conv_ln_b, v_conv_ln_b),
                norm_ffn2=(norm_ffn2, m_norm_ffn2, v_norm_ffn2), norm_final=(norm_final, m_norm_final, v_norm_final))
    sm_g = dict(gs, ada_b=g_ada_b, hgrn_lb=g_hgrn_lb)
    rows = {n: sm_w[n][0].size // D for n in sm_names}
    n_rows = sum(rows.values())
    pad = (-n_rows) % 8
    stack = lambda parts: jnp.concatenate([q.reshape(-1, D) for q in parts] + [jnp.ones((pad, D), F32)], axis=0)
    st = _adamw(stack([sm_w[n][0] for n in sm_names]), stack([sm_w[n][1] for n in sm_names]),
                stack([sm_w[n][2] for n in sm_names]), stack([sm_g[n] for n in sm_names]), "adamw_small")
    off = 0
    for n in sm_names:
        res[n] = tuple(t[off:off + rows[n]].reshape(sm_w[n][0].shape) for t in st)
        off += rows[n]

    order = ("ada_w", "ada_b", "norm_ffn1", "ffn1_w_in", "ffn1_w_out", "norm_mix", "mix_w_in", "hgrn_lb", "hgrn_g",
             "hgrn_w_o", "conv_w", "conv_b", "conv_ln_g", "conv_ln_b", "conv_w_o", "mix_w_out", "norm_ffn2",
             "ffn2_w_in", "ffn2_w_out", "norm_final")
    lead = lambda n, t: t[None] if n in big or n == "ada_w" else t
    outs = [loss, dx[None]]
    for j in range(4):
        outs += [lead(n, res[n][j]) for n in order]
    return tuple(outs)
```

```python
import jax
import jax.numpy as jnp
from jax import lax
from jax.experimental import pallas as pl
from jax.experimental.pallas import tpu as pltpu

F32 = jnp.float32
MM = jnp.bfloat16
ACT = jnp.bfloat16

D = 1024
D_FF = 2816
HEADS = 8
HD = 128
CHUNK = 64
SUB = 16
NSUB = CHUNK // SUB
HGRN_BLOCK = 1024
SAFE_EXP = 60.0
CONV_K = 31
HALO = 32
EPS = 1e-6
N_DEV = 8
NEG = -1e30
Q_SCALE = HD ** -0.5

ADAM_LR = 0.001
ADAM_B1 = 0.9
ADAM_B2 = 0.999
ADAM_EPS = 1e-08
ADAM_WD = 0.01
ADAM_STEP = 10

V7X_VMEM_BYTES = 64 * 1024 * 1024
VMEM_LIMIT = V7X_VMEM_BYTES - 4 * 1024 * 1024
MESH = pl.DeviceIdType.MESH


def _cparams(n_axes):
    return pltpu.CompilerParams(dimension_semantics=("arbitrary",) * n_axes, vmem_limit_bytes=VMEM_LIMIT)


def _mm(a, b):
    return lax.dot_general(a.astype(MM), b.astype(MM), (((1,), (0,)), ((), ())), preferred_element_type=F32)


def _mm_nt(a, b):
    return lax.dot_general(a.astype(MM), b.astype(MM), (((1,), (1,)), ((), ())), preferred_element_type=F32)


def _mm_tn(a, b):
    return lax.dot_general(a.astype(MM), b.astype(MM), (((0,), (0,)), ((), ())), preferred_element_type=F32)


def _sig(x):
    return 1.0 / (1.0 + jnp.exp(-x))


def _colsum(x):
    return jnp.sum(x, axis=0, keepdims=True)


def _rowmean(x):
    return jnp.mean(x, axis=-1, keepdims=True)


def _modnorm_fwd(xv, g, sh, sc):
    r = lax.rsqrt(_rowmean(xv * xv) + EPS)
    xh = xv * r
    n = xh * g
    return n * (1.0 + sc) + sh, xh, n, r


def _modnorm_bwd(dh, xh, n, r, g, sc):
    dsc = _colsum(dh * n)
    dsh = _colsum(dh)
    dn = dh * (1.0 + sc)
    dg = _colsum(dn * xh)
    dxh = dn * g
    dx = r * (dxh - xh * _rowmean(dxh * xh))
    return dx, dsh, dsc, dg


def _ffn_fwd(x, mod, mo, gnorm, w_in_t, w_out, res, name, carry):
    T = x.shape[0]
    tm = min(512, T)
    tn = D_FF // 2

    def body(x_ref, mod_ref, g_ref, wi_ref, wo_ref, xo_ref, a_ref, b_ref, f_ref, h_ref):
        xv = x_ref[...]
        h, _, _, _ = _modnorm_fwd(xv, g_ref[...], mod_ref[mo:mo + 1, :], mod_ref[mo + 1:mo + 2, :])
        h = h.astype(ACT)
        h_ref[...] = h
        f = None
        for c0 in range(0, D_FF, tn):
            a = _mm_nt(h, wi_ref[0, c0:c0 + tn, :])
            b = _mm_nt(h, wi_ref[1, c0:c0 + tn, :])
            a_ref[:, c0:c0 + tn] = a.astype(ACT)
            b_ref[:, c0:c0 + tn] = b.astype(ACT)
            part = _mm(a * _sig(a) * b, wo_ref[c0:c0 + tn, :])
            f = part if f is None else f + part
        f_ref[...] = f
        xo_ref[...] = xv + res * mod_ref[mo + 2:mo + 3, :] * f

    tile = pl.BlockSpec((tm, D), lambda i: (i, 0))
    wide = pl.BlockSpec((tm, D_FF), lambda i: (i, 0))
    out = _gridded(
        body, carry, name=name, grid=(T // tm,),
        in_specs=[
            tile,
            pl.BlockSpec((9, D), lambda i: (0, 0)),
            pl.BlockSpec((1, D), lambda i: (0, 0)),
            pl.BlockSpec((2, D_FF, D), lambda i: (0, 0, 0), pipeline_mode=pl.Buffered(1)),
            pl.BlockSpec((D_FF, D), lambda i: (0, 0), pipeline_mode=pl.Buffered(1)),
        ],
        out_specs=[tile, wide, wide, tile, tile],
        out_shape=[
            jax.ShapeDtypeStruct((T, D), F32),
            jax.ShapeDtypeStruct((T, D_FF), ACT),
            jax.ShapeDtypeStruct((T, D_FF), ACT),
            jax.ShapeDtypeStruct((T, D), F32),
            jax.ShapeDtypeStruct((T, D), ACT),
        ],
    )(x, mod, gnorm, w_in_t, w_out)
    return out[:5], out[5:]


def _ffn_bwd_w(h, df, a, b, w_out, name, carry):
    T = h.shape[0]
    tm = min(2048, T)
    ni = T // tm
    tn = 256
    nj = D_FF // tn

    def body(h_ref, df_ref, a_ref, b_ref, wo_ref, da_ref, db_ref, dwi_ref, dwo_ref, acc_i, acc_o):
        i = pl.program_id(1)

        @pl.when(i == 0)
        def _():
            acc_i[...] = jnp.zeros_like(acc_i)
            acc_o[...] = jnp.zeros_like(acc_o)

        hb = h_ref[...]
        df = df_ref[...]
        av = a_ref[...].astype(F32)
        bv = b_ref[...].astype(F32)
        sg = _sig(av)
        sa = av * sg
        s = (sa * bv).astype(MM)
        ds = _mm_nt(df, wo_ref[...])
        da = (ds * bv * sg * (1.0 + av * (1.0 - sg))).astype(MM)
        db = (ds * sa).astype(MM)
        da_ref[...] = da
        db_ref[...] = db
        acc_o[...] += _mm_tn(s, df)
        acc_i[0] += _mm_tn(da, hb)
        acc_i[1] += _mm_tn(db, hb)

        @pl.when(i == ni - 1)
        def _():
            dwi_ref[...] = acc_i[...].astype(MM)
            dwo_ref[...] = acc_o[...].astype(MM)

    out = _gridded(
        body, carry, name=name, grid=(nj, ni),
        in_specs=[
            pl.BlockSpec((tm, D), lambda j, i: (i, 0)),
            pl.BlockSpec((tm, D), lambda j, i: (i, 0)),
            pl.BlockSpec((tm, tn), lambda j, i: (i, j)),
            pl.BlockSpec((tm, tn), lambda j, i: (i, j)),
            pl.BlockSpec((tn, D), lambda j, i: (j, 0)),
        ],
        out_specs=[
            pl.BlockSpec((tm, tn), lambda j, i: (i, j)),
            pl.BlockSpec((tm, tn), lambda j, i: (i, j)),
            pl.BlockSpec((2, tn, D), lambda j, i: (0, j, 0)),
            pl.BlockSpec((tn, D), lambda j, i: (j, 0)),
        ],
        out_shape=[
            jax.ShapeDtypeStruct((T, D_FF), MM),
            jax.ShapeDtypeStruct((T, D_FF), MM),
            jax.ShapeDtypeStruct((2, D_FF, D), MM),
            jax.ShapeDtypeStruct((D_FF, D), MM),
        ],
        scratch_shapes=[pltpu.VMEM((2, tn, D), F32), pltpu.VMEM((tn, D), F32)],
    )(h, df, a, b, w_out)
    return out[:4], out[4:]


def _ffn_bwd_x(x, dxo, f, da, db, mod, mo, gnorm, w_in_t, res, name, carry):
    T = x.shape[0]
    tm = min(512, T)
    ni = T // tm
    tn = D_FF // 2
    nj = D_FF // tn

    def body(x_ref, dxo_ref, f_ref, da_ref, db_ref, mod_ref, g_ref, wi_ref, dx_ref, sm_ref, dh_scr):
        j = pl.program_id(0)
        i = pl.program_id(1)

        @pl.when((j == 0) & (i == 0))
        def _():
            sm_ref[...] = jnp.zeros_like(sm_ref)

        @pl.when(j == 0)
        def _():
            dh_scr[i] = jnp.zeros((tm, D), F32)

        dh_scr[i] += _mm(da_ref[...], wi_ref[0]) + _mm(db_ref[...], wi_ref[1])

        @pl.when(j == nj - 1)
        def _():
            sc = mod_ref[mo + 1:mo + 2, :]
            _, xh, n, r = _modnorm_fwd(x_ref[...], g_ref[...], mod_ref[mo:mo + 1, :], sc)
            dxn, dsh, dsc, dg = _modnorm_bwd(dh_scr[i], xh, n, r, g_ref[...], sc)
            dxo_v = dxo_ref[...]
            dx_ref[...] = dxo_v + dxn
            sm_ref[0:1, :] += dsh
            sm_ref[1:2, :] += dsc
            sm_ref[2:3, :] += _colsum(dxo_v * f_ref[...]) * res
            sm_ref[3:4, :] += dg

    last = pl.BlockSpec((tm, D), lambda j, i: (jnp.where(j == nj - 1, i, 0), 0))
    out = _gridded(
        body, carry, name=name, grid=(nj, ni),
        in_specs=[last, last, last,
                  pl.BlockSpec((tm, tn), lambda j, i: (i, j)), pl.BlockSpec((tm, tn), lambda j, i: (i, j)),
                  pl.BlockSpec((9, D), lambda j, i: (0, 0)), pl.BlockSpec((1, D), lambda j, i: (0, 0)),
                  pl.BlockSpec((2, tn, D), lambda j, i: (0, j, 0))],
        out_specs=[last, pl.BlockSpec((8, D), lambda j, i: (0, 0))],
        out_shape=[jax.ShapeDtypeStruct((T, D), F32), jax.ShapeDtypeStruct((8, D), F32)],
        scratch_shapes=[pltpu.VMEM((ni, tm, D), F32)],
    )(x, dxo, f, da, db, mod, gnorm, w_in_t)
    return out[:2], out[2:]


def _head(x, target, gfin, mod, gate_row, res):
    T = x.shape[0]
    tm = min(512, T)
    ni = T // tm

    def body(x_ref, t_ref, g_ref, mod_ref, dx_ref, df_ref, sm_ref):
        i = pl.program_id(0)

        @pl.when(i == 0)
        def _():
            sm_ref[...] = jnp.zeros_like(sm_ref)

        xv = x_ref[...]
        g = g_ref[...]
        r = lax.rsqrt(_rowmean(xv * xv) + EPS)
        xh = xv * r
        e = xh * g - t_ref[...]
        sm_ref[1:2, :] += _colsum(e * e) * (0.5 / D)
        dy = e * (1.0 / D)
        sm_ref[0:1, :] += _colsum(dy * xh)
        dxh = dy * g
        dx = r * (dxh - xh * _rowmean(dxh * xh))
        dx_ref[...] = dx
        df_ref[...] = (res * mod_ref[gate_row:gate_row + 1, :] * dx).astype(MM)

        @pl.when(i == ni - 1)
        def _():
            sm_ref[1:2, :] = jnp.broadcast_to(jnp.sum(sm_ref[1:2, :], axis=-1, keepdims=True), (1, D))

    tile = pl.BlockSpec((tm, D), lambda i: (i, 0))
    return pl.pallas_call(
        body, name="head_loss", grid=(ni,),
        in_specs=[tile, tile, pl.BlockSpec((1, D), lambda i: (0, 0)), pl.BlockSpec((9, D), lambda i: (0, 0))],
        out_specs=[tile, tile, pl.BlockSpec((8, D), lambda i: (0, 0))],
        out_shape=[jax.ShapeDtypeStruct((T, D), F32), jax.ShapeDtypeStruct((T, D), MM),
                   jax.ShapeDtypeStruct((8, D), F32)],
        compiler_params=_cparams(1),
    )(x, target, gfin, mod)


def _mixin_fwd(x, mod, mo, gnorm, w, carry):
    T = x.shape[0]
    tm = min(1024, T)
    ni = T // tm

    def body(x_ref, mod_ref, g_ref, w_ref, p_ref, h_ref, h_all):
        i = pl.program_id(1)

        @pl.when(pl.program_id(0) == 0)
        def _():
            h, _, _, _ = _modnorm_fwd(x_ref[...], g_ref[...], mod_ref[mo:mo + 1, :], mod_ref[mo + 1:mo + 2, :])
            h_all[i] = h.astype(ACT)
            h_ref[...] = h.astype(ACT)

        p_ref[0] = _mm(h_all[i], w_ref[0])

    first = lambda k, i: (jnp.where(k == 0, i, ni - 1), 0)
    out = _gridded(
        body, carry, name="mixin_fwd", grid=(8, ni),
        in_specs=[pl.BlockSpec((tm, D), first), pl.BlockSpec((9, D), lambda k, i: (0, 0)),
                  pl.BlockSpec((1, D), lambda k, i: (0, 0)), pl.BlockSpec((1, D, D), lambda k, i: (k, 0, 0))],
        out_specs=[pl.BlockSpec((1, tm, D), lambda k, i: (k, i, 0)), pl.BlockSpec((tm, D), first)],
        out_shape=[jax.ShapeDtypeStruct((8, T, D), F32), jax.ShapeDtypeStruct((T, D), ACT)],
        scratch_shapes=[pltpu.VMEM((ni, tm, D), ACT)],
    )(x, mod, gnorm, w)
    return out[:2], out[2:]


def _mixin_bwd(x, h, dxo, dp, mod, mo, gnorm, w, next_gate, next_res, carry):
    T = x.shape[0]
    tm = min(512, T)
    ni = T // tm

    def body(x_ref, h_ref, dxo_ref, dp_ref, mod_ref, g_ref, w_ref, dx_ref, dw_ref, sm_ref, df_ref, dh_scr, acc):
        k = pl.program_id(0)
        i = pl.program_id(1)

        @pl.when(i == 0)
        def _():
            acc[...] = jnp.zeros_like(acc)

        @pl.when(k == 0)
        def _():
            dh_scr[i] = jnp.zeros((tm, D), F32)

        @pl.when((k == 0) & (i == 0))
        def _():
            sm_ref[...] = jnp.zeros_like(sm_ref)

        dpk = dp_ref[0].astype(MM)
        acc[...] += _mm_tn(h_ref[...], dpk)
        dh_scr[i] += _mm_nt(dpk, w_ref[0])

        @pl.when(i == ni - 1)
        def _():
            dw_ref[0] = acc[...].astype(MM)

        @pl.when(k == 7)
        def _():
            sc = mod_ref[mo + 1:mo + 2, :]
            _, xh, n, r = _modnorm_fwd(x_ref[...], g_ref[...], mod_ref[mo:mo + 1, :], sc)
            dxn, dsh, dsc, dg = _modnorm_bwd(dh_scr[i], xh, n, r, g_ref[...], sc)
            dx = dxo_ref[...] + dxn
            dx_ref[...] = dx
            df_ref[...] = (next_res * mod_ref[next_gate:next_gate + 1, :] * dx).astype(MM)
            sm_ref[0:1, :] += dsh
            sm_ref[1:2, :] += dsc
            sm_ref[3:4, :] += dg

    last = pl.BlockSpec((tm, D), lambda k, i: (jnp.where(k == 7, i, 0), 0))
    out = _gridded(
        body, carry, name="mixin_bwd", grid=(8, ni),
        in_specs=[pl.BlockSpec((tm, D), lambda k, i: (jnp.where(k == 7, i, 0), 0)),
                  pl.BlockSpec((tm, D), lambda k, i: (i, 0)),
                  pl.BlockSpec((tm, D), lambda k, i: (jnp.where(k == 7, i, 0), 0)),
                  pl.BlockSpec((1, tm, D), lambda k, i: (k, i, 0)), pl.BlockSpec((9, D), lambda k, i: (0, 0)),
                  pl.BlockSpec((1, D), lambda k, i: (0, 0)), pl.BlockSpec((1, D, D), lambda k, i: (k, 0, 0))],
        out_specs=[last, pl.BlockSpec((1, D, D), lambda k, i: (k, 0, 0)), pl.BlockSpec((8, D), lambda k, i: (0, 0)),
                   last],
        out_shape=[jax.ShapeDtypeStruct((T, D), F32), jax.ShapeDtypeStruct((8, D, D), MM),
                   jax.ShapeDtypeStruct((8, D), F32), jax.ShapeDtypeStruct((T, D), MM)],
        scratch_shapes=[pltpu.VMEM((ni, tm, D), F32), pltpu.VMEM((D, D), F32)],
    )(x, h, dxo, dp, mod, gnorm, w)
    return out[:4], out[4:]


def _hgrn_consts():
    rows = jnp.arange(SUB * HD) // HD
    e = (rows[:, None] == jnp.arange(HD)[None, :]).astype(MM)
    return e, e.T


def _rows_bcast(ref, cb, first, n):
    parts = [jnp.broadcast_to(ref[pl.ds(c * CHUNK + first, 1), :], (n, HD)) for c in range(cb // CHUNK)]
    return jnp.concatenate(parts, axis=0)


def _hgrn_pre(qr, fr, lb_ref, b_scr, cb):
    z = lb_ref[...]
    lb = _sig(z[0:1, :] - z[1:2, :])
    sq = _sig(qr)
    q = qr * sq * Q_SCALE
    sf = _sig(fr)
    fg = lb + (1.0 - lb) * sf
    lf = jnp.log(fg)
    k = 1.0 - fg
    tl = lax.broadcasted_iota(jnp.int32, (cb, HD), 0) % CHUNK
    bc = lf
    sh = 1
    while sh < CHUNK:
        bc = bc + jnp.where(tl >= sh, pltpu.roll(bc, sh, 0), 0.0)
        sh *= 2
    b_scr[...] = bc
    bl = _rows_bcast(b_scr, cb, CHUNK - 1, CHUNK)
    eb = jnp.exp(bc)
    ekd = jnp.exp(bl - bc)
    ekf = jnp.exp(jnp.minimum(-bc, SAFE_EXP))
    return dict(lb=lb, sq=sq, q=q, sf=sf, fg=fg, k=k, tl=tl, b=bc, bl=bl, eb=eb, ekd=ekd, ekf=ekf,
                qe=q * eb, kd=k * ekd, kf=k * ekf, safe=jnp.max(-bc) < SAFE_EXP)


def _hgrn_sub(pre, b_scr, cb):
    bc, tl, q, k = pre["b"], pre["tl"], pre["q"], pre["k"]
    br = [None] + [_rows_bcast(b_scr, cb, SUB * i - 1, CHUNK) for i in range(1, NSUB)]
    sb = tl // SUB
    bref = jnp.where(sb == 0, bc, jnp.where(sb == 1, br[1], jnp.where(sb == 2, br[2], br[3])))
    eqo = jnp.exp(bc - bref)
    eko = [None] + [jnp.exp(jnp.where(tl < SUB * i, br[i] - bc, NEG)) for i in range(1, NSUB)]
    return dict(eqo=eqo, eko=eko, qo=q * eqo, ko=[None] + [k * eko[i] for i in range(1, NSUB)])


def _pad_rows(x):
    return jnp.concatenate([x, jnp.zeros_like(x)], axis=0)


def _by_subblock(sbc, parts):
    out = jnp.zeros_like(parts[1])
    for i in range(1, NSUB):
        out = jnp.where(sbc == i, parts[i], out)
    return out


def _hgrn_fwd(p, hgrn_lb, hgrn_g, carry):
    T = p.shape[1]
    cb = min(HGRN_BLOCK, T)
    nch = cb // CHUNK
    ncb = T // cb
    e_mat, _ = _hgrn_consts()

    def body(p_ref, lb_ref, g_ref, e_ref, o_ref, oa_ref, a_ref, s_ref, st_scr, q_scr, k_scr, b_scr, z_scr, ad_scr):
        @pl.when(pl.program_id(1) == 0)
        def _():
            st_scr[...] = jnp.zeros_like(st_scr)

        v = p_ref[2]
        og = p_ref[3]
        pre = _hgrn_pre(p_ref[0], p_ref[1], lb_ref, b_scr, cb)
        chunks = [slice(c * CHUNK, (c + 1) * CHUNK) for c in range(nch)]
        row_i = lax.broadcasted_iota(jnp.int32, (CHUNK, HD), 0)
        lane_i = lax.broadcasted_iota(jnp.int32, (CHUNK, HD), 1)
        sbc = row_i // SUB
        causal = lane_i <= row_i

        @pl.when(pre["safe"])
        def _():
            for rs in chunks:
                ad_scr[rs, :] = jnp.where(causal, _mm_nt(pre["qe"][rs], _pad_rows(pre["kf"][rs])), 0.0)

        @pl.when(jnp.logical_not(pre["safe"]))
        def _():
            sub = _hgrn_sub(pre, b_scr, cb)
            q_scr[...] = pre["q"]
            k_scr[...] = pre["k"]
            ti = lax.broadcasted_iota(jnp.int32, (SUB, HD), 0)

            def zbody(c, carry):
                for i in range(NSUB):
                    r0 = pl.multiple_of(c * CHUNK + SUB * i, SUB)
                    qi = q_scr[pl.ds(r0, SUB), :]
                    bi = b_scr[pl.ds(r0, SUB), :]
                    for s in range(SUB):
                        krow = k_scr[pl.ds(r0 + s, 1), :]
                        brow = b_scr[pl.ds(r0 + s, 1), :]
                        if s < 8:
                            zz = qi * krow * jnp.exp(jnp.where(ti >= s, bi - brow, NEG))
                        else:
                            lo = qi[8:] * krow * jnp.exp(jnp.where(ti[8:] >= s, bi[8:] - brow, NEG))
                            zz = jnp.concatenate([jnp.zeros((8, HD), F32), lo], axis=0)
                        z_scr[i, pl.ds(pl.multiple_of(c * SUB, SUB), SUB), s * HD:(s + 1) * HD] = zz.astype(MM)
                return carry

            lax.fori_loop(0, nch, zbody, 0)
            adiag = [_mm(z_scr[i], e_ref[...]) for i in range(NSUB)]
            offs = [[_mm_nt(sub["qo"][rs], _pad_rows(sub["ko"][i][rs])) for i in range(1, NSUB)] for rs in chunks]
            for c, rs in enumerate(chunks):
                dparts = []
                for i in range(NSUB):
                    blk = adiag[i][c * SUB:(c + 1) * SUB]
                    dparts.append(blk if i == 0 else pltpu.roll(blk, SUB * i, 1))
                ad_scr[rs, :] = _by_subblock(sbc, [None] + offs[c]) + jnp.concatenate(dparts, axis=0)

        kv = [_mm_tn(v[rs], pre["kd"][rs]) for rs in chunks]
        a_parts = [ad_scr[rs, :] for rs in chunks]
        a_ref[0] = ad_scr[...]
        o_intra = [_mm(a_parts[c], _pad_rows(v[rs])) for c, rs in enumerate(chunks)]
        states = []
        st = st_scr[...]
        for c in range(nch):
            states.append(st)
            st = st * jnp.exp(b_scr[pl.ds(c * CHUNK + CHUNK - 1, 1), :]) + kv[c]
        st_scr[...] = st
        for c in range(nch):
            s_ref[0, c] = states[c]
        o = jnp.concatenate([o_intra[c] + _mm_nt(pre["qe"][rs], states[c]) for c, rs in enumerate(chunks)], axis=0)
        o_ref[...] = o
        on = o * lax.rsqrt(_rowmean(o * o) + EPS) * g_ref[...]
        oa_ref[...] = (on * og * _sig(og)).astype(ACT)

    out = _gridded(
        body, carry, name="hgrn_fwd", grid=(HEADS, ncb),
        in_specs=[pl.BlockSpec((4, cb, HD), lambda h, c: (0, c, h)),
                  pl.BlockSpec((2, HD), lambda h, c: (0, h)),
                  pl.BlockSpec((1, HD), lambda h, c: (0, h)),
                  pl.BlockSpec((SUB * HD, HD), lambda h, c: (0, 0))],
        out_specs=[pl.BlockSpec((cb, HD), lambda h, c: (c, h)),
                   pl.BlockSpec((cb, HD), lambda h, c: (c, h)),
                   pl.BlockSpec((1, cb, HD), lambda h, c: (h, c, 0)),
                   pl.BlockSpec((1, nch, HD, HD), lambda h, c: (h, c, 0, 0))],
        out_shape=[jax.ShapeDtypeStruct((T, D), F32), jax.ShapeDtypeStruct((T, D), ACT),
                   jax.ShapeDtypeStruct((HEADS, T, HD), F32),
                   jax.ShapeDtypeStruct((HEADS, T // CHUNK, HD, HD), F32)],
        scratch_shapes=[pltpu.VMEM((HD, HD), F32), pltpu.VMEM((cb, HD), F32), pltpu.VMEM((cb, HD), F32),
                        pltpu.VMEM((cb, HD), F32), pltpu.VMEM((NSUB, nch * SUB, SUB * HD), MM),
                        pltpu.VMEM((cb, HD), F32)],
    )(p, hgrn_lb, hgrn_g, e_mat)
    return out[:4], out[4:]


def _hgrn_bwd(p, o, a_all, s_all, doa, hgrn_lb, hgrn_g, dp, carry):
    T = p.shape[1]
    cb = min(HGRN_BLOCK, T)
    nch = cb // CHUNK
    ncb = T // cb
    _, et_mat = _hgrn_consts()

    def body(p_ref, o_ref, a_ref, s_ref, doa_ref, lb_ref, g_ref, et_ref, dp_in, dp_ref, sm_ref,
             dst_scr, q_scr, k_scr, b_scr, x_scr, dqd_scr, dkd_scr):
        del dp_in

        @pl.when(pl.program_id(1) == 0)
        def _():
            dst_scr[...] = jnp.zeros_like(dst_scr)
            sm_ref[...] = jnp.zeros_like(sm_ref)

        qr = p_ref[0]
        v = p_ref[2]
        og = p_ref[3]
        pre = _hgrn_pre(qr, p_ref[1], lb_ref, b_scr, cb)
        q, k = pre["q"], pre["k"]
        g = g_ref[...]
        ov = o_ref[...]
        r = lax.rsqrt(_rowmean(ov * ov) + EPS)
        oh = ov * r
        sgo = _sig(og)
        doa_v = doa_ref[...]
        don = doa_v * og * sgo
        dog = doa_v * oh * g * sgo * (1.0 + og * (1.0 - sgo))
        sm_ref[1:2, :] += _colsum(don * oh)
        doh = don * g
        do = r * (doh - oh * _rowmean(doh * oh))

        sbc = lax.broadcasted_iota(jnp.int32, (CHUNK, HD), 0) // SUB
        row_i = lax.broadcasted_iota(jnp.int32, (CHUNK, HD), 0)
        lane_i = lax.broadcasted_iota(jnp.int32, (CHUNK, HD), 1)
        causal = lane_i <= row_i
        chunks = [slice(c * CHUNK, (c + 1) * CHUNK) for c in range(nch)]
        da_parts = [jnp.where(causal, _mm_nt(do[rs], _pad_rows(v[rs])), 0.0) for rs in chunks]
        dv_parts = [_mm_tn(a_ref[0, rs, :], do[rs])[:CHUNK] for rs in chunks]

        @pl.when(pre["safe"])
        def _():
            hi = dict(preferred_element_type=F32, precision=lax.Precision.HIGH)
            for c, rs in enumerate(chunks):
                dqd_scr[rs, :] = pre["eb"][rs] * lax.dot_general(
                    da_parts[c], _pad_rows(pre["kf"][rs]), (((1,), (0,)), ((), ())), **hi)
                dkd_scr[rs, :] = pre["ekf"][rs] * lax.dot_general(
                    da_parts[c], pre["qe"][rs], (((0,), (0,)), ((), ())), **hi)[:CHUNK]

        @pl.when(jnp.logical_not(pre["safe"]))
        def _():
            sub = _hgrn_sub(pre, b_scr, cb)
            dqoff_mm = [[_mm(da_parts[c], _pad_rows(sub["ko"][i][rs])) for i in range(1, NSUB)]
                        for c, rs in enumerate(chunks)]
            dkoff_mm = [[_mm_tn(jnp.where(sbc == i, da_parts[c], 0.0), sub["qo"][rs])[:CHUNK]
                         for i in range(1, NSUB)] for c, rs in enumerate(chunks)]
            dqoff_parts = [_by_subblock(sbc, [None] + dqoff_mm[c]) for c in range(nch)]
            dkoff_parts = []
            for c, rs in enumerate(chunks):
                dko = sub["eko"][1][rs] * dkoff_mm[c][0]
                for i in range(2, NSUB):
                    dko = dko + sub["eko"][i][rs] * dkoff_mm[c][i - 1]
                dkoff_parts.append(dko)
            q_scr[...] = q
            k_scr[...] = k
            for i in range(NSUB):
                rows = []
                for c in range(nch):
                    blk = da_parts[c][SUB * i:SUB * (i + 1)]
                    rows.append(blk if i == 0 else pltpu.roll(blk, HD - SUB * i, 1))
                x_scr[i] = _mm(jnp.concatenate(rows, axis=0), et_ref[...])
            ti = lax.broadcasted_iota(jnp.int32, (SUB, HD), 0)

            def dbody(c, carry):
                for i in range(NSUB):
                    r0 = pl.multiple_of(c * CHUNK + SUB * i, SUB)
                    qi = q_scr[pl.ds(r0, SUB), :]
                    bi = b_scr[pl.ds(r0, SUB), :]
                    dq_hi = jnp.zeros((8, HD), F32)
                    dq_lo = jnp.zeros((8, HD), F32)
                    dk_hi = jnp.zeros((8, HD), F32)
                    dk_lo = jnp.zeros((8, HD), F32)
                    c0 = pl.multiple_of(c * SUB, SUB)
                    t8 = ti[:8]
                    for s in range(SUB):
                        krow = k_scr[pl.ds(r0 + s, 1), :]
                        brow = b_scr[pl.ds(r0 + s, 1), :]
                        w_lo = (x_scr[i, pl.ds(c0 + 8, 8), s * HD:(s + 1) * HD]
                                * jnp.exp(jnp.where(t8 + 8 >= s, bi[8:] - brow, NEG)))
                        dq_lo = dq_lo + w_lo * krow
                        col = _colsum(w_lo * qi[8:])
                        if s < 8:
                            w_hi = (x_scr[i, pl.ds(c0, 8), s * HD:(s + 1) * HD]
                                    * jnp.exp(jnp.where(t8 >= s, bi[:8] - brow, NEG)))
                            dq_hi = dq_hi + w_hi * krow
                            dk_hi = jnp.where(t8 == s, col + _colsum(w_hi * qi[:8]), dk_hi)
                        else:
                            dk_lo = jnp.where(t8 + 8 == s, col, dk_lo)
                    dqd_scr[pl.ds(r0, SUB), :] = jnp.concatenate([dq_hi, dq_lo], axis=0)
                    dkd_scr[pl.ds(r0, SUB), :] = jnp.concatenate([dk_hi, dk_lo], axis=0)
                return carry

            lax.fori_loop(0, nch, dbody, 0)
            dqd_scr[...] += jnp.concatenate(dqoff_parts, axis=0) * sub["eqo"]
            dkd_scr[...] += jnp.concatenate(dkoff_parts, axis=0)

        qdo = [_mm_tn(do[rs], pre["qe"][rs]) for rs in chunks]
        dsts = [None] * nch
        dst = dst_scr[...]
        for c in reversed(range(nch)):
            dsts[c] = dst
            dst = dst * jnp.exp(b_scr[pl.ds(c * CHUNK + CHUNK - 1, 1), :]) + qdo[c]
        dst_scr[...] = dst
        sts = [s_ref[0, c] for c in range(nch)]
        dqe_parts = [_mm(do[rs], sts[c]) for c, rs in enumerate(chunks)]
        dkdec_parts = [_mm(v[rs], dsts[c]) for c, rs in enumerate(chunks)]
        dvi_parts = [_mm_nt(pre["kd"][rs], dsts[c]) for c, rs in enumerate(chunks)]
        debl_parts = [_colsum(dsts[c] * sts[c]) for c in range(nch)]
        dqe = jnp.concatenate(dqe_parts, axis=0)
        dkdec = jnp.concatenate(dkdec_parts, axis=0)
        dq_tot = dqd_scr[...] + dqe * pre["eb"]
        dk_inter = dkdec * pre["ekd"]
        dk_tot = dkd_scr[...] + dk_inter
        db = q * dq_tot - k * dk_tot
        kdk = k * dk_inter
        dbl = jnp.concatenate(
            [jnp.broadcast_to(jnp.exp(b_scr[pl.ds(c * CHUNK + CHUNK - 1, 1), :]) * debl_parts[c]
                              + _colsum(kdk[c * CHUNK:(c + 1) * CHUNK]), (CHUNK, HD)) for c in range(nch)], axis=0)
        tl = pre["tl"]
        rc = db
        sh = 1
        while sh < CHUNK:
            rc = rc + jnp.where(tl + sh < CHUNK, pltpu.roll(rc, cb - sh, 0), 0.0)
            sh *= 2
        dlf = rc + dbl
        dfg = dlf / pre["fg"] - dk_tot
        sf = pre["sf"]
        lb = pre["lb"]
        sm_ref[0:1, :] += _colsum(dfg * (1.0 - sf))
        sq = pre["sq"]
        dp_ref[0] = (dq_tot * Q_SCALE * sq * (1.0 + qr * (1.0 - sq))).astype(ACT)
        dp_ref[1] = (dfg * (1.0 - lb) * sf * (1.0 - sf)).astype(ACT)
        dp_ref[2] = (jnp.concatenate(dv_parts, axis=0) + jnp.concatenate(dvi_parts, axis=0)).astype(ACT)
        dp_ref[3] = dog.astype(ACT)

    rev = lambda c: ncb - 1 - c
    out = _gridded(
        body, carry, name="hgrn_bwd", grid=(HEADS, ncb),
        in_specs=[pl.BlockSpec((4, cb, HD), lambda h, c: (0, rev(c), h)),
                  pl.BlockSpec((cb, HD), lambda h, c: (rev(c), h)),
                  pl.BlockSpec((1, cb, HD), lambda h, c: (h, rev(c), 0)),
                  pl.BlockSpec((1, nch, HD, HD), lambda h, c: (h, rev(c), 0, 0)),
                  pl.BlockSpec((cb, HD), lambda h, c: (rev(c), h)),
                  pl.BlockSpec((2, HD), lambda h, c: (0, h)),
                  pl.BlockSpec((1, HD), lambda h, c: (0, h)),
                  pl.BlockSpec((HD, SUB * HD), lambda h, c: (0, 0)),
                  pl.BlockSpec(memory_space=pl.ANY)],
        out_specs=[pl.BlockSpec((4, cb, HD), lambda h, c: (0, rev(c), h)),
                   pl.BlockSpec((8, HD), lambda h, c: (0, h))],
        out_shape=[jax.ShapeDtypeStruct(dp.shape, dp.dtype), jax.ShapeDtypeStruct((8, D), F32)],
        aliases={8: 0},
        scratch_shapes=[pltpu.VMEM((HD, HD), F32), pltpu.VMEM((cb, HD), F32), pltpu.VMEM((cb, HD), F32),
                        pltpu.VMEM((cb, HD), F32), pltpu.VMEM((NSUB, nch * SUB, SUB * HD), F32),
                        pltpu.VMEM((cb, HD), F32), pltpu.VMEM((cb, HD), F32)],
    )(p, o, a_all, s_all, doa, hgrn_lb, hgrn_g, et_mat, dp)
    return out[:2], out[2:]


def _ln_fwd(u1, g, b):
    mu = _rowmean(u1)
    xc = u1 - mu
    rs = lax.rsqrt(_rowmean(xc * xc) + EPS)
    xh = xc * rs
    return xh * g + b, xh, rs


CONV_RB = 64
LANES = 128


def _shift_rows(src, sh, ls, n):
    for r in range(1, 8):
        sh[r - 1, 0:n, :] = src[pl.ds(r, n), ls]


def _tap(src, sh, ls, off, r0, rows):
    r = off % 8
    if r == 0:
        return src[pl.ds(r0 + off, rows), ls]
    return sh[r - 1, pl.ds(r0 + off - r, rows), :]


def _conv_fwd(p, cw, cb_, lng, lnb):
    T = p.shape[1]
    tm = min(512, T)
    n = HALO + tm - 8

    def body(p_ref, cw_ref, cb_ref, g_ref, b_ref, u1_ref, u2_ref, buf, sh):
        @pl.when(pl.program_id(0) == 0)
        def _():
            buf[0:HALO, :] = jnp.zeros((HALO, D), F32)

        buf[HALO:HALO + tm, :] = p_ref[0] * _sig(p_ref[1])
        for lb in range(D // LANES):
            ls = slice(lb * LANES, (lb + 1) * LANES)
            _shift_rows(buf, sh, ls, n)
            taps = [cw_ref[j:j + 1, ls] for j in range(CONV_K)]
            bias = cb_ref[:, ls]

            def rows_body(rb, carry):
                r0 = pl.multiple_of(rb * CONV_RB, CONV_RB)
                acc = jnp.broadcast_to(bias, (CONV_RB, LANES))
                for j in range(CONV_K):
                    acc = acc + taps[j] * _tap(buf, sh, ls, HALO - (CONV_K - 1) + j, r0, CONV_RB)
                u1_ref[pl.ds(r0, CONV_RB), ls] = acc
                return carry

            lax.fori_loop(0, tm // CONV_RB, rows_body, 0)
        y, _, _ = _ln_fwd(u1_ref[...], g_ref[...], b_ref[...])
        u2_ref[...] = (y * _sig(y)).astype(ACT)
        buf[0:HALO, :] = buf[tm:tm + HALO, :]

    return pl.pallas_call(
        body, name="conv_fwd", grid=(T // tm,),
        in_specs=[pl.BlockSpec((2, tm, D), lambda i: (2, i, 0)), pl.BlockSpec((HALO, D), lambda i: (0, 0)),
                  pl.BlockSpec((1, D), lambda i: (0, 0)), pl.BlockSpec((1, D), lambda i: (0, 0)),
                  pl.BlockSpec((1, D), lambda i: (0, 0))],
        out_specs=[pl.BlockSpec((tm, D), lambda i: (i, 0)), pl.BlockSpec((tm, D), lambda i: (i, 0))],
        out_shape=[jax.ShapeDtypeStruct((T, D), F32), jax.ShapeDtypeStruct((T, D), ACT)],
        scratch_shapes=[pltpu.VMEM((HALO + tm, D), F32), pltpu.VMEM((7, n, LANES), F32)],
        compiler_params=_cparams(1),
    )(p, cw, cb_, lng, lnb)


def _conv_bwd(p, u1, du2, cw, lng, lnb, dp):
    T = p.shape[1]
    tm = min(512, T)
    ni = T // tm
    hb = tm // HALO

    n = HALO + tm - 8

    def body(p_ref, ph_ref, u1_ref, du2_ref, cw_ref, g_ref, b_ref, dp_in, dp_ref, dcw_ref, sm_ref, ubuf, dbuf,
             sh, dacc):
        del dp_in
        step = pl.program_id(0)

        @pl.when(step == 0)
        def _():
            dbuf[tm:tm + HALO, :] = jnp.zeros((HALO, D), F32)
            dcw_ref[...] = jnp.zeros_like(dcw_ref)
            sm_ref[...] = jnp.zeros_like(sm_ref)

        ua = p_ref[0]
        sgb = _sig(p_ref[1])
        halo = ph_ref[0] * _sig(ph_ref[1])
        ubuf[0:HALO, :] = jnp.where(step == ni - 1, 0.0, halo)
        ubuf[HALO:HALO + tm, :] = ua * sgb
        g = g_ref[...]
        y, xh, rs = _ln_fwd(u1_ref[...], g, b_ref[...])
        sy = _sig(y)
        dy = du2_ref[...] * sy * (1.0 + y * (1.0 - sy))
        sm_ref[1:2, :] += _colsum(dy * xh)
        sm_ref[2:3, :] += _colsum(dy)
        dxh = dy * g
        du1 = rs * (dxh - _rowmean(dxh) - xh * _rowmean(dxh * xh))
        sm_ref[0:1, :] += _colsum(du1)
        dbuf[0:tm, :] = du1
        for lb in range(D // LANES):
            ls = slice(lb * LANES, (lb + 1) * LANES)
            taps = [cw_ref[j:j + 1, ls] for j in range(CONV_K)]
            _shift_rows(dbuf, sh, ls, n)

            def du0_body(rb, carry):
                r0 = pl.multiple_of(rb * CONV_RB, CONV_RB)
                acc = jnp.zeros((CONV_RB, LANES), F32)
                for j in range(CONV_K):
                    acc = acc + taps[j] * _tap(dbuf, sh, ls, CONV_K - 1 - j, r0, CONV_RB)
                dp_ref[0, pl.ds(r0, CONV_RB), ls] = acc.astype(ACT)
                return carry

            lax.fori_loop(0, tm // CONV_RB, du0_body, 0)
            _shift_rows(ubuf, sh, ls, n)
            dacc[...] = jnp.zeros_like(dacc)

            def dcw_body(rb, carry):
                r0 = pl.multiple_of(rb * CONV_RB, CONV_RB)
                d = dbuf[pl.ds(r0, CONV_RB), ls]
                for j in range(CONV_K):
                    prod = d * _tap(ubuf, sh, ls, HALO - (CONV_K - 1) + j, r0, CONV_RB)
                    dacc[8 * j:8 * j + 8, :] += jnp.sum(prod.reshape(CONV_RB // 8, 8, LANES), axis=0)
                return carry

            lax.fori_loop(0, tm // CONV_RB, dcw_body, 0)
            for j in range(CONV_K):
                dcw_ref[j:j + 1, ls] += _colsum(dacc[8 * j:8 * j + 8, :])
        du0 = dp_ref[0].astype(F32)
        dp_ref[0] = (du0 * sgb).astype(ACT)
        dp_ref[1] = (du0 * ua * sgb * (1.0 - sgb)).astype(ACT)
        dbuf[tm:tm + HALO, :] = dbuf[0:HALO, :]

    rev = lambda i: ni - 1 - i
    return pl.pallas_call(
        body, name="conv_bwd", grid=(ni,),
        in_specs=[pl.BlockSpec((2, tm, D), lambda i: (2, rev(i), 0)),
                  pl.BlockSpec((2, HALO, D), lambda i: (2, jnp.maximum(rev(i) * hb - 1, 0), 0)),
                  pl.BlockSpec((tm, D), lambda i: (rev(i), 0)), pl.BlockSpec((tm, D), lambda i: (rev(i), 0)),
                  pl.BlockSpec((HALO, D), lambda i: (0, 0)), pl.BlockSpec((1, D), lambda i: (0, 0)),
                  pl.BlockSpec((1, D), lambda i: (0, 0)), pl.BlockSpec(memory_space=pl.ANY)],
        out_specs=[pl.BlockSpec((2, tm, D), lambda i: (2, rev(i), 0)),
                   pl.BlockSpec((HALO, D), lambda i: (0, 0)), pl.BlockSpec((8, D), lambda i: (0, 0))],
        out_shape=[jax.ShapeDtypeStruct(dp.shape, dp.dtype), jax.ShapeDtypeStruct((HALO, D), F32),
                   jax.ShapeDtypeStruct((8, D), F32)],
        input_output_aliases={7: 0},
        scratch_shapes=[pltpu.VMEM((HALO + tm, D), F32), pltpu.VMEM((tm + HALO, D), F32),
                        pltpu.VMEM((7, n, LANES), F32), pltpu.VMEM((8 * CONV_K, LANES), F32)],
        compiler_params=_cparams(1),
    )(p, p, u1, du2, cw, lng, lnb, dp)


def _mixout_fwd(x, oa, u2, p, mod, mo, w_a, w_b, w_o):
    T = x.shape[0]
    tm = min(512, T)

    def body(x_ref, oa_ref, u2_ref, p_ref, mod_ref, wa_ref, wb_ref, wo_ref, xo_ref, ya_ref, yb_ref, mo_ref):
        ya = _mm(oa_ref[...], wa_ref[...])
        yb = _mm(u2_ref[...], wb_ref[...])
        ya_ref[...] = ya.astype(ACT)
        yb_ref[...] = yb.astype(ACT)
        merged = _sig(p_ref[0]) * ya + _sig(p_ref[1]) * yb
        out = _mm(merged, wo_ref[...])
        mo_ref[...] = out
        xo_ref[...] = x_ref[...] + mod_ref[mo + 2:mo + 3, :] * out

    tile = pl.BlockSpec((tm, D), lambda i: (i, 0))
    wspec = pl.BlockSpec((D, D), lambda i: (0, 0))
    return pl.pallas_call(
        body, name="mixout_fwd", grid=(T // tm,),
        in_specs=[tile, tile, tile, pl.BlockSpec((2, tm, D), lambda i: (3, i, 0)),
                  pl.BlockSpec((9, D), lambda i: (0, 0)), wspec, wspec, wspec],
        out_specs=[tile, tile, tile, tile],
        out_shape=[jax.ShapeDtypeStruct((T, D), F32), jax.ShapeDtypeStruct((T, D), ACT),
                   jax.ShapeDtypeStruct((T, D), ACT), jax.ShapeDtypeStruct((T, D), F32)],
        compiler_params=_cparams(1),
    )(x, oa, u2, p, mod, w_a, w_b, w_o)


def _mixout_bwd(dxo, oa, u2, ya, yb, mout, p, mod, mo, w_a, w_b, w_o):
    T = dxo.shape[0]
    tm = min(256, T)

    def body(dxo_ref, oa_ref, u2_ref, ya_ref, yb_ref, mo_ref, p_ref, mod_ref, wa_ref, wb_ref, wo_ref,
             dp_ref, doa_ref, du2_ref, dwa_ref, dwb_ref, dwo_ref, sm_ref):
        @pl.when(pl.program_id(0) == 0)
        def _():
            dwa_ref[...] = jnp.zeros_like(dwa_ref)
            dwb_ref[...] = jnp.zeros_like(dwb_ref)
            dwo_ref[...] = jnp.zeros_like(dwo_ref)
            sm_ref[...] = jnp.zeros_like(sm_ref)

        dxo_v = dxo_ref[...]
        sm_ref[2:3, :] += _colsum(dxo_v * mo_ref[...])
        dmo = (mod_ref[mo + 2:mo + 3, :] * dxo_v).astype(MM)
        ya = ya_ref[...].astype(F32)
        yb = yb_ref[...].astype(F32)
        sga = _sig(p_ref[0])
        sgb = _sig(p_ref[1])
        merged = (sga * ya + sgb * yb).astype(MM)
        dwo_ref[...] += _mm_tn(merged, dmo)
        dmg = _mm_nt(dmo, wo_ref[...])
        dp_ref[0] = (dmg * ya * sga * (1.0 - sga)).astype(ACT)
        dp_ref[1] = (dmg * yb * sgb * (1.0 - sgb)).astype(ACT)
        dya = (dmg * sga).astype(MM)
        dyb = (dmg * sgb).astype(MM)
        dwa_ref[...] += _mm_tn(oa_ref[...], dya)
        dwb_ref[...] += _mm_tn(u2_ref[...], dyb)
        doa_ref[...] = _mm_nt(dya, wa_ref[...])
        du2_ref[...] = _mm_nt(dyb, wb_ref[...])

    tile = pl.BlockSpec((tm, D), lambda i: (i, 0))
    wspec = pl.BlockSpec((D, D), lambda i: (0, 0))
    return pl.pallas_call(
        body, name="mixout_bwd", grid=(T // tm,),
        in_specs=[tile, tile, tile, tile, tile, tile, pl.BlockSpec((2, tm, D), lambda i: (3, i, 0)),
                  pl.BlockSpec((9, D), lambda i: (0, 0)), wspec, wspec, wspec],
        out_specs=[pl.BlockSpec((2, tm, D), lambda i: (3, i, 0)), tile, tile, wspec, wspec, wspec,
                   pl.BlockSpec((8, D), lambda i: (0, 0))],
        out_shape=[jax.ShapeDtypeStruct((8, T, D), ACT), jax.ShapeDtypeStruct((T, D), F32),
                   jax.ShapeDtypeStruct((T, D), F32), jax.ShapeDtypeStruct((D, D), F32),
                   jax.ShapeDtypeStruct((D, D), F32), jax.ShapeDtypeStruct((D, D), F32),
                   jax.ShapeDtypeStruct((8, D), F32)],
        compiler_params=_cparams(1),
    )(dxo, oa, u2, ya, yb, mout, p, mod, w_a, w_b, w_o)


def _ada_wgrad(cs_all, dmod_cols):
    cs_t = jnp.pad(cs_all.T, ((0, 0), (0, HD - N_DEV)))
    dm = jnp.pad(dmod_cols, ((0, HD - N_DEV), (0, 0)))

    def body(cs_ref, d_ref, out_ref):
        out_ref[...] = jnp.dot(cs_ref[...], d_ref[...], preferred_element_type=F32,
                               precision=lax.Precision.HIGHEST)

    return pl.pallas_call(
        body, name="ada_wgrad", out_shape=jax.ShapeDtypeStruct((D, dmod_cols.shape[1]), F32),
        compiler_params=pltpu.CompilerParams(vmem_limit_bytes=VMEM_LIMIT),
    )(cs_t, dm)


def _adam_math(w, g, m, v):
    m2 = ADAM_B1 * m + (1.0 - ADAM_B1) * g
    v2 = ADAM_B2 * v + (1.0 - ADAM_B2) * (g * g)
    m_hat = m2 / (1.0 - ADAM_B1 ** ADAM_STEP)
    v_hat = v2 / (1.0 - ADAM_B2 ** ADAM_STEP)
    delta = -ADAM_LR * (m_hat / (jnp.sqrt(v_hat) + ADAM_EPS) + ADAM_WD * w)
    return delta, m2, v2


def _adamw(w, m, v, g, name):
    R, C = w.shape
    slots = g.ndim == 3
    n_slots = g.shape[0] if slots else 0
    tr = R
    for cand in (256, 176):
        if R % cand == 0 and R > cand:
            tr = cand
            break

    def body(w_ref, m_ref, v_ref, g_ref, go_ref, d_ref, mo_ref, vo_ref):
        if slots:
            gv = g_ref[0].astype(F32)
            for s in range(1, n_slots):
                gv = gv + g_ref[s].astype(F32)
        else:
            gv = g_ref[...]
        go_ref[...] = gv
        d_ref[...], mo_ref[...], vo_ref[...] = _adam_math(w_ref[...], gv, m_ref[...], v_ref[...])

    tile = pl.BlockSpec((tr, C), lambda i: (i, 0))
    gspec = pl.BlockSpec((n_slots, tr, C), lambda i: (0, i, 0)) if slots else tile
    sds = jax.ShapeDtypeStruct((R, C), F32)
    return pl.pallas_call(
        body, name=name, grid=(R // tr,), in_specs=[tile, tile, tile, gspec], out_specs=[tile] * 4,
        out_shape=[sds] * 4, compiler_params=_cparams(1),
    )(w, m, v, g)


def _sum_slots(pack, name, tr):
    n, R, C = pack.shape

    def body(p_ref, out_ref):
        acc = p_ref[0].astype(F32)
        for s in range(1, n):
            acc = acc + p_ref[s].astype(F32)
        out_ref[...] = acc

    return pl.pallas_call(
        body, name=name, grid=(R // tr,), in_specs=[pl.BlockSpec((n, tr, C), lambda i: (0, i, 0))],
        out_specs=pl.BlockSpec((tr, C), lambda i: (i, 0)), out_shape=jax.ShapeDtypeStruct((R, C), F32),
        compiler_params=_cparams(1))(pack)


def _me():
    return lax.axis_index("x"), lax.axis_index("y"), lax.axis_index("c")


def _peer(r):
    x, y, c = _me()
    px = 1 - x if r & 4 else x
    py = 1 - y if r & 2 else y
    pc = 1 - c if r & 1 else c
    return (px, py, pc), 4 * px + 2 * py + pc


def _small_gather(x_ref, out_ref, send_sems, recv_sems):
    R = x_ref.shape[0]
    mx, my, mc = _me()
    me = 4 * mx + 2 * my + mc
    mine = out_ref.at[pl.ds(pl.multiple_of(me * R, 8), R), :]
    copies = []
    for r in range(1, N_DEV):
        dev, _ = _peer(r)
        copies.append(pltpu.make_async_remote_copy(
            src_ref=x_ref, dst_ref=mine, send_sem=send_sems.at[r - 1], recv_sem=recv_sems.at[r - 1],
            device_id=dev, device_id_type=MESH))
    for cp in copies:
        cp.start()
    mine[...] = x_ref[...]
    for r in range(1, N_DEV):
        dev, idx = _peer(r)
        theirs = out_ref.at[pl.ds(pl.multiple_of(idx * R, 8), R), :]
        pltpu.make_async_remote_copy(
            src_ref=x_ref, dst_ref=theirs, send_sem=send_sems.at[r - 1], recv_sem=recv_sems.at[r - 1],
            device_id=dev, device_id_type=MESH).wait_recv()
    for cp in copies:
        cp.wait_send()


def _prologue(cs, ada_w, ada_b_cols, big):
    n = len(big)
    ncol = ada_w.shape[1]
    big_shape, big_sems = _xchg_specs(big, "gather")

    def body(cs_ref, w_ref, b_ref, *rest):
        big_in, cs_all, mod_all, big_out = rest[:n], rest[n], rest[n + 1], rest[n + 2:2 * n + 2]
        mod_scr, s1, r1, s2, r2 = rest[2 * n + 2:2 * n + 7]
        sems = rest[2 * n + 7:]
        _xchg_start(big_in, big_out, sems, "gather")
        _small_gather(cs_ref, cs_all, s1, r1)
        pick = (lax.broadcasted_iota(jnp.int32, (N_DEV, N_DEV * 8), 1)
                == 8 * lax.broadcasted_iota(jnp.int32, (N_DEV, N_DEV * 8), 0)).astype(F32)
        per_device = jnp.dot(pick, cs_all[...], preferred_element_type=F32, precision=lax.Precision.HIGHEST)
        mod_scr[...] = jnp.dot(per_device, w_ref[...], preferred_element_type=F32,
                               precision=lax.Precision.HIGHEST) + b_ref[...]
        _small_gather(mod_scr, mod_all, s2, r2)
        _xchg_wait(big_in, big_out, sems, "gather")

    vmem = pl.BlockSpec(memory_space=pltpu.VMEM)
    hbm = pl.BlockSpec(memory_space=pl.ANY)
    dma7 = pltpu.SemaphoreType.DMA((N_DEV - 1,))
    out = pl.pallas_call(
        body, name="prologue",
        out_shape=[jax.ShapeDtypeStruct((N_DEV * 8, D), F32), jax.ShapeDtypeStruct((N_DEV * 8, ncol), F32)]
        + big_shape,
        in_specs=[vmem, vmem, vmem] + [hbm] * n, out_specs=[vmem, vmem] + [hbm] * n,
        scratch_shapes=[pltpu.VMEM((8, ncol), F32), dma7, dma7, dma7, dma7] + big_sems,
        compiler_params=pltpu.CompilerParams(vmem_limit_bytes=VMEM_LIMIT),
    )(cs, ada_w, ada_b_cols, *big)
    return out[0], out[1], out[2:]


def _allgather_small(x):
    R, C = x.shape

    def body(x_ref, out_ref, send_sems, recv_sems):
        _small_gather(x_ref, out_ref, send_sems, recv_sems)

    return pl.pallas_call(
        body, name="allgather_small_%dx%d" % (R, C),
        out_shape=jax.ShapeDtypeStruct((N_DEV * R, C), F32),
        in_specs=[pl.BlockSpec(memory_space=pltpu.VMEM)], out_specs=pl.BlockSpec(memory_space=pltpu.VMEM),
        scratch_shapes=[pltpu.SemaphoreType.DMA((N_DEV - 1,)), pltpu.SemaphoreType.DMA((N_DEV - 1,))],
    )(x)


N_CHIP = N_DEV // 2


def _xchg_copies(ins, outs, sems, mode):
    send_sems, recv_sems, local_sems = sems
    mx, my, mc = _me()
    me = 4 * mx + 2 * my + mc
    my_chip = 2 * mx + my
    sibling = _peer(1)[0]

    def rdma(a, r, dev, src, slot):
        k = a * (N_DEV - 1) + r - 1
        return pltpu.make_async_remote_copy(
            src_ref=src, dst_ref=outs[a].at[slot], send_sem=send_sems.at[k], recv_sem=recv_sems.at[k],
            device_id=dev, device_id_type=MESH)

    own, sends, relays, recvs = [], [], [], []
    for a in range(len(ins)):
        if mode == "pair":
            for chip in range(N_CHIP):
                src = ins[a].at[2 * chip + 1 - mc]
                sends.append(rdma(a, chip + 1, sibling, src, chip))
                recvs.append(rdma(a, chip + 1, sibling, src, chip))
            continue
        if mode == "quad":
            own.append(pltpu.make_async_copy(ins[a].at[my_chip], outs[a].at[my_chip], local_sems.at[a]))
            for r in (2, 4, 6):
                dev, idx = _peer(r)
                chip = idx // 2
                sends.append(rdma(a, r, dev, ins[a].at[chip], my_chip))
                recvs.append(rdma(a, r, dev, ins[a].at[chip], chip))
            continue
        gather = mode == "gather"
        own.append(pltpu.make_async_copy(ins[a] if gather else ins[a].at[me], outs[a].at[me], local_sems.at[a]))
        for r in range(1, N_DEV):
            dev, idx = _peer(r)
            if not gather:
                sends.append(rdma(a, r, dev, ins[a].at[idx], me))
                recvs.append(rdma(a, r, dev, ins[a].at[idx], idx))
            elif r == 1:
                sends.append(rdma(a, r, dev, ins[a], me))
                recvs.append(rdma(a, r, dev, ins[a], idx))
            elif r % 2 == 0:
                sends.append(rdma(a, r, dev, ins[a], me))
                relays.append((rdma(a, r, dev, ins[a], idx), rdma(a, r + 1, sibling, outs[a].at[idx], idx)))
            else:
                recvs.append(rdma(a, r, sibling, ins[a], idx))
    return own, sends, relays, recvs


def _xchg_start(ins, outs, sems, mode):
    own, sends, _, _ = _xchg_copies(ins, outs, sems, mode)
    for cp in own + sends:
        cp.start()


def _xchg_wait(ins, outs, sems, mode):
    own, sends, relays, recvs = _xchg_copies(ins, outs, sems, mode)
    for arrival, relay in relays:
        arrival.wait_recv()
        relay.start()
    for cp in recvs:
        cp.wait_recv()
    for cp in own:
        cp.wait()
    for cp in sends + [relay for _, relay in relays]:
        cp.wait_send()


def _xchg_specs(arrays, mode):
    n = len(arrays)
    shape = {"gather": lambda s: (N_DEV,) + s, "scatter": lambda s: s, "pair": lambda s: (N_CHIP,) + s[1:],
             "quad": lambda s: s}[mode]
    out_shape = [jax.ShapeDtypeStruct(shape(a.shape), a.dtype) for a in arrays]
    sems = [pltpu.SemaphoreType.DMA((n * (N_DEV - 1),)), pltpu.SemaphoreType.DMA((n * (N_DEV - 1),)),
            pltpu.SemaphoreType.DMA((n,))]
    return out_shape, sems


def _exchange(arrays, mode, name):
    n = len(arrays)

    def body(*refs):
        _xchg_start(refs[:n], refs[n:2 * n], refs[2 * n:], mode)
        _xchg_wait(refs[:n], refs[n:2 * n], refs[2 * n:], mode)

    out_shape, sems = _xchg_specs(arrays, mode)
    return pl.pallas_call(
        body, name=name, out_shape=out_shape,
        in_specs=[pl.BlockSpec(memory_space=pl.ANY)] * n, out_specs=[pl.BlockSpec(memory_space=pl.ANY)] * n,
        scratch_shapes=sems,
    )(*arrays)


def _gridded(body, carry, *, name, grid, in_specs, out_specs, out_shape, scratch_shapes=(), aliases=None):
    if carry is None:
        return pl.pallas_call(
            body, name=name, grid=grid, in_specs=list(in_specs), out_specs=list(out_specs),
            out_shape=list(out_shape), scratch_shapes=list(scratch_shapes), input_output_aliases=aliases or {},
            compiler_params=_cparams(len(grid)))
    arrays, mode = carry
    n, n_in, n_out, n_scr = len(arrays), len(in_specs), len(out_specs), len(scratch_shapes)
    c_shape, c_sems = _xchg_specs(arrays, mode)

    def wrapped(*refs):
        ins, cin = refs[:n_in], refs[n_in:n_in + n]
        o0 = n_in + n
        outs, cout = refs[o0:o0 + n_out], refs[o0 + n_out:o0 + n_out + n]
        s0 = o0 + n_out + n
        scr, sems = refs[s0:s0 + n_scr], refs[s0 + n_scr:]
        first = pl.program_id(0) == 0
        last = pl.program_id(0) == grid[0] - 1
        for ax in range(1, len(grid)):
            first = first & (pl.program_id(ax) == 0)
            last = last & (pl.program_id(ax) == grid[ax] - 1)

        @pl.when(first)
        def _():
            _xchg_start(cin, cout, sems, mode)

        body(*ins, *outs, *scr)

        @pl.when(last)
        def _():
            _xchg_wait(cin, cout, sems, mode)

    hbm = pl.BlockSpec(memory_space=pl.ANY)
    res = pl.pallas_call(
        wrapped, name=name, grid=grid, in_specs=list(in_specs) + [hbm] * n, out_specs=list(out_specs) + [hbm] * n,
        out_shape=list(out_shape) + c_shape, scratch_shapes=list(scratch_shapes) + c_sems,
        input_output_aliases=aliases or {}, compiler_params=_cparams(len(grid)),
    )
    return lambda *args: res(*args, *arrays)


def _local_step(x, target, mod, small, sh, w1):
    w1_in, w1_out = w1[0].reshape(2, D_FF, D), w1[1].reshape(D_FF, D)
    (x1, a1, b1, f1, h1), (wm_in,) = _ffn_fwd(x, mod, 0, small["norm_ffn1"], w1_in, w1_out, 0.5, "ffn1_fwd",
                                              ([sh["mix_w_in"]], "gather"))
    (p, h2), (wh_o, wc_o, wm_o, cw) = _mixin_fwd(
        x1, mod, 3, small["norm_mix"], wm_in,
        ([sh["hgrn_w_o"], sh["conv_w_o"], sh["mix_w_out"], sh["conv_w"]], "gather"))
    wh_o, wc_o, wm_o = wh_o.reshape(D, D), wc_o.reshape(D, D), wm_o.reshape(D, D)
    cw = jnp.pad(cw.transpose(1, 0, 2).reshape(CONV_K, D), ((0, HALO - CONV_K), (0, 0)))
    (o, oa, a_all, s_all), (w2_in, w2_out) = _hgrn_fwd(p, small["hgrn_lb"], small["hgrn_g"],
                                                       ([sh["ffn2_w_in"], sh["ffn2_w_out"]], "gather"))
    w2_in, w2_out = w2_in.reshape(2, D_FF, D), w2_out.reshape(D_FF, D)
    u1, u2 = _conv_fwd(p, cw, small["conv_b"], small["conv_ln_g"], small["conv_ln_b"])
    x2, ya, yb, mout = _mixout_fwd(x1, oa, u2, p, mod, 3, wh_o, wc_o, wm_o)
    (x3, a3, b3, f3, h3), _ = _ffn_fwd(x2, mod, 6, small["norm_ffn2"], w2_in, w2_out, 0.5, "ffn2_fwd", None)
    dx3, df3, sm_head = _head(x3, target, small["norm_final"], mod, 8, 0.5)

    (da3, db3, dw2_in, dw2_out), _ = _ffn_bwd_w(h3, df3, a3, b3, w2_out, "ffn2_bwd_w", None)
    (dx2, sm3), _ = _ffn_bwd_x(x2, dx3, f3, da3, db3, mod, 6, small["norm_ffn2"], w2_in, 0.5, "ffn2_bwd_x", None)
    dp, doa, du2, dwh_o, dwc_o, dwm_o, sm_mo = _mixout_bwd(dx2, oa, u2, ya, yb, mout, p, mod, 3, wh_o, wc_o, wm_o)
    dp, dcw, sm_cv = _conv_bwd(p, u1, du2, cw, small["conv_ln_g"], small["conv_ln_b"], dp)
    rows = lambda t: t.reshape(N_DEV, -1, D).astype(MM)
    (dp, sm_hg), (r2_in, r2_out) = _hgrn_bwd(p, o, a_all, s_all, doa, small["hgrn_lb"], small["hgrn_g"], dp,
                                             ([rows(dw2_in), rows(dw2_out)], "scatter"))
    (dx1, dwm_in, sm2, df1), (rh_o, rc_o, rm_o, rcw) = _mixin_bwd(
        x1, h2, dx2, dp, mod, 3, small["norm_mix"], wm_in, 2, 0.5,
        ([rows(dwh_o), rows(dwc_o), rows(dwm_o), dcw[:CONV_K].reshape(CONV_K, N_DEV, -1).transpose(1, 0, 2)],
         "scatter"))
    (da1, db1, dw1_in, dw1_out), (rm_in,) = _ffn_bwd_w(h1, df1, a1, b1, w1_out, "ffn1_bwd_w",
                                                      (_pair_reduce([dwm_in], "pair_mix"), "quad"))
    (dx0, sm1), (r1_in, r1_out) = _ffn_bwd_x(
        x, dx1, f1, da1, db1, mod, 0, small["norm_ffn1"], w1_in, 0.5, "ffn1_bwd_x",
        (_pair_reduce([rows(dw1_in), rows(dw1_out)], "pair_ffn1"), "quad"))

    dmod = jnp.concatenate([sm1[0:3], sm2[0:2], sm_mo[2:3], sm3[0:3]], axis=0)
    gsmall = dict(norm_ffn1=sm1[3:4], norm_mix=sm2[3:4], lb0=sm_hg[0:1], hgrn_g=sm_hg[1:2], conv_b=sm_cv[0:1],
                  conv_ln_g=sm_cv[1:2], conv_ln_b=sm_cv[2:3], norm_ffn2=sm3[3:4], norm_final=sm_head[0:1])
    recv = dict(ffn1_w_in=r1_in, ffn1_w_out=r1_out, mix_w_in=rm_in, hgrn_w_o=rh_o, conv_w=rcw, conv_w_o=rc_o,
                mix_w_out=rm_o, ffn2_w_in=r2_in, ffn2_w_out=r2_out)
    return sm_head[1, 0], dx0, dmod, gsmall, recv


def _pair_add(mine, theirs, core, name):
    _, R, C = theirs.shape

    def body(core_ref, a_ref, b_ref, out_ref):
        del core_ref
        out_ref[0] = (a_ref[0, 0].astype(F32) + b_ref[0].astype(F32)).astype(out_ref.dtype)

    blk = pl.BlockSpec((1, R, C), lambda s, core_ref: (s, 0, 0))
    grid_spec = pltpu.PrefetchScalarGridSpec(
        num_scalar_prefetch=1, grid=(N_CHIP,),
        in_specs=[pl.BlockSpec((1, 1, R, C), lambda s, core_ref: (s, core_ref[0], 0, 0)), blk], out_specs=blk)
    return pl.pallas_call(body, name=name, grid_spec=grid_spec,
                          out_shape=jax.ShapeDtypeStruct(theirs.shape, mine.dtype), compiler_params=_cparams(1),
                          )(core, mine.reshape(N_CHIP, 2, R, C), theirs)


def _pair_reduce(arrays, name):
    theirs = _exchange(arrays, "pair", name)
    core = lax.axis_index("c").astype(jnp.int32).reshape(1)
    return [_pair_add(a, t, core, "%s_add%d" % (name, i)) for i, (a, t) in enumerate(zip(arrays, theirs))]


SMALL_ORDER = ("norm_ffn1", "norm_mix", "lb0", "hgrn_g", "conv_b", "conv_ln_g", "conv_ln_b", "norm_ffn2",
               "norm_final")
PACK_ROWS = 24


def kernel(x, c, ada_w, ada_b, norm_ffn1, ffn1_w_in, ffn1_w_out, norm_mix, mix_w_in, hgrn_lb, hgrn_g, hgrn_w_o, conv_w, conv_b, conv_ln_g, conv_ln_b, conv_w_o, mix_w_out, norm_ffn2, ffn2_w_in, ffn2_w_out, norm_final, loss_target, m_ada_w, m_ada_b, m_norm_ffn1, m_ffn1_w_in, m_ffn1_w_out, m_norm_mix, m_mix_w_in, m_hgrn_lb, m_hgrn_g, m_hgrn_w_o, m_conv_w, m_conv_b, m_conv_ln_g, m_conv_ln_b, m_conv_w_o, m_mix_w_out, m_norm_ffn2, m_ffn2_w_in, m_ffn2_w_out, m_norm_final, v_ada_w, v_ada_b, v_norm_ffn1, v_ffn1_w_in, v_ffn1_w_out, v_norm_mix, v_mix_w_in, v_hgrn_lb, v_hgrn_g, v_hgrn_w_o, v_conv_w, v_conv_b, v_conv_ln_g, v_conv_ln_b, v_conv_w_o, v_mix_w_out, v_norm_ffn2, v_ffn2_w_in, v_ffn2_w_out, v_norm_final):
    mx, my, mc = _me()
    me = 4 * mx + 2 * my + mc
    ncol = ada_w.shape[2]

    sh = dict(ffn1_w_out=ffn1_w_out, mix_w_in=mix_w_in, hgrn_w_o=hgrn_w_o, conv_w_o=conv_w_o,
              mix_w_out=mix_w_out, ffn2_w_out=ffn2_w_out)
    sh = {n: w[0].astype(MM) for n, w in sh.items()}
    sh["ffn1_w_in"] = ffn1_w_in[0].T.astype(MM)
    sh["ffn2_w_in"] = ffn2_w_in[0].T.astype(MM)
    sh["conv_w"] = conv_w[0]
    small = dict(norm_ffn1=norm_ffn1, norm_mix=norm_mix, hgrn_lb=hgrn_lb, hgrn_g=hgrn_g, conv_b=conv_b,
                 conv_ln_g=conv_ln_g, conv_ln_b=conv_ln_b, norm_ffn2=norm_ffn2, norm_final=norm_final.reshape(1, D))

    cs = jnp.broadcast_to(c * jax.nn.sigmoid(c), (8, D))
    ada_b_cols = lax.dynamic_slice(ada_b, (0, me * ncol), (1, ncol))
    cs_all, mod_all, w1 = _prologue(cs, ada_w[0], ada_b_cols, [sh["ffn1_w_in"], sh["ffn1_w_out"]])
    cs_all = cs_all.reshape(N_DEV, 8, D)[:, 0, :]
    mod = lax.dynamic_index_in_dim(mod_all.reshape(N_DEV, N_DEV, ncol), me, axis=1, keepdims=False).reshape(9, D)

    loss_local, dx, dmod, gsmall, recv = _local_step(x[0], loss_target[0], mod, small, sh, w1)
    loss = lax.psum(loss_local, ("x", "y", "c"))

    pack = jnp.concatenate([dmod] + [gsmall[n] for n in SMALL_ORDER]
                           + [jnp.zeros((PACK_ROWS - 9 - len(SMALL_ORDER), D), F32)], axis=0)
    pack_all = _allgather_small(pack).reshape(N_DEV, PACK_ROWS, D)
    tot = _sum_slots(pack_all, "sum_small", PACK_ROWS)
    gs = {n: tot[9 + i:10 + i] for i, n in enumerate(SMALL_ORDER)}
    dmod_all = pack_all[:, 0:9, :].reshape(N_DEV, 9 * D)
    g_ada_b = tot[0:9].reshape(1, 9 * D)
    g_ada_w = _ada_wgrad(cs_all, lax.dynamic_slice(dmod_all, (0, me * ncol), (N_DEV, ncol)))
    z = hgrn_lb.astype(F32)
    p0 = jax.nn.sigmoid(z[0:1] - z[1:2])
    dz0 = p0 * (1.0 - p0) * gs["lb0"]
    g_hgrn_lb = jnp.concatenate([dz0, -dz0], axis=0)

    res = {}
    res["ada_w"] = _adamw(ada_w[0], m_ada_w[0], v_ada_w[0], g_ada_w, "adamw_ada_w")
    big = dict(ffn1_w_in=(ffn1_w_in, m_ffn1_w_in, v_ffn1_w_in), ffn1_w_out=(ffn1_w_out, m_ffn1_w_out, v_ffn1_w_out),
               mix_w_in=(mix_w_in, m_mix_w_in, v_mix_w_in), hgrn_w_o=(hgrn_w_o, m_hgrn_w_o, v_hgrn_w_o),
               conv_w=(conv_w, m_conv_w, v_conv_w), conv_w_o=(conv_w_o, m_conv_w_o, v_conv_w_o),
               mix_w_out=(mix_w_out, m_mix_w_out, v_mix_w_out), ffn2_w_in=(ffn2_w_in, m_ffn2_w_in, v_ffn2_w_in),
               ffn2_w_out=(ffn2_w_out, m_ffn2_w_out, v_ffn2_w_out))
    for n, (w, m, v) in big.items():
        g = recv[n]
        if n in ("ffn1_w_in", "ffn2_w_in"):
            g = _sum_slots(g, "sum_" + n, g.shape[1] // 4).T
        res[n] = _adamw(w[0], m[0], v[0], g, "adamw_" + n)
    sm_names = ("ada_b", "norm_ffn1", "norm_mix", "hgrn_lb", "hgrn_g", "conv_b", "conv_ln_g", "conv_ln_b",
                "norm_ffn2", "norm_final")
    sm_w = dict(ada_b=(ada_b, m_ada_b, v_ada_b), norm_ffn1=(norm_ffn1, m_norm_ffn1, v_norm_ffn1),
                norm_mix=(norm_mix, m_norm_mix, v_norm_mix), hgrn_lb=(hgrn_lb, m_hgrn_lb, v_hgrn_lb),
                hgrn_g=(hgrn_g, m_hgrn_g, v_hgrn_g), conv_b=(conv_b, m_conv_b, v_conv_b),
                conv_ln_g=(conv_ln_g, m_conv_ln_g, v_conv_ln_g), conv_ln_b=(conv_ln_b, m_conv_ln_b, v_conv_ln_b),
                norm_ffn2=(norm_ffn2, m_norm_ffn2, v_norm_ffn2), norm_final=(norm_final, m_norm_final, v_norm_final))
    sm_g = dict(gs, ada_b=g_ada_b, hgrn_lb=g_hgrn_lb)
    rows = {n: sm_w[n][0].size // D for n in sm_names}
    n_rows = sum(rows.values())
    pad = (-n_rows) % 8
    stack = lambda parts: jnp.concatenate([q.reshape(-1, D) for q in parts] + [jnp.ones((pad, D), F32)], axis=0)
    st = _adamw(stack([sm_w[n][0] for n in sm_names]), stack([sm_w[n][1] for n in sm_names]),
                stack([sm_w[n][2] for n in sm_names]), stack([sm_g[n] for n in sm_names]), "adamw_small")
    off = 0
    for n in sm_names:
        res[n] = tuple(t[off:off + rows[n]].reshape(sm_w[n][0].shape) for t in st)
        off += rows[n]

    order = ("ada_w", "ada_b", "norm_ffn1", "ffn1_w_in", "ffn1_w_out", "norm_mix", "mix_w_in", "hgrn_lb", "hgrn_g",
             "hgrn_w_o", "conv_w", "conv_b", "conv_ln_g", "conv_ln_b", "conv_w_o", "mix_w_out", "norm_ffn2",
             "ffn2_w_in", "ffn2_w_out", "norm_final")
    lead = lambda n, t: t[None] if n in big or n == "ada_w" else t
    outs = [loss, dx[None]]
    for j in range(4):
        outs += [lead(n, res[n][j]) for n in order]
    return tuple(outs)
```

```python
import jax
import jax.numpy as jnp
from jax import lax
from jax.experimental import pallas as pl
from jax.experimental.pallas import tpu as pltpu

F32 = jnp.float32
MM = jnp.bfloat16
ACT = jnp.bfloat16

D = 1024
D_FF = 2816
HEADS = 8
HD = 128
CHUNK = 64
SUB = 16
NSUB = CHUNK // SUB
HGRN_BLOCK = 1024
SAFE_EXP = 60.0
CONV_K = 31
HALO = 32
EPS = 1e-6
N_DEV = 8
NEG = -1e30
Q_SCALE = HD ** -0.5

ADAM_LR = 0.001
ADAM_B1 = 0.9
ADAM_B2 = 0.999
ADAM_EPS = 1e-08
ADAM_WD = 0.01
ADAM_STEP = 10

V7X_VMEM_BYTES = 64 * 1024 * 1024
VMEM_LIMIT = V7X_VMEM_BYTES - 4 * 1024 * 1024
MESH = pl.DeviceIdType.MESH


def _cparams(n_axes):
    return pltpu.CompilerParams(dimension_semantics=("arbitrary",) * n_axes, vmem_limit_bytes=VMEM_LIMIT)


def _mm(a, b):
    return lax.dot_general(a.astype(MM), b.astype(MM), (((1,), (0,)), ((), ())), preferred_element_type=F32)


def _mm_nt(a, b):
    return lax.dot_general(a.astype(MM), b.astype(MM), (((1,), (1,)), ((), ())), preferred_element_type=F32)


def _mm_tn(a, b):
    return lax.dot_general(a.astype(MM), b.astype(MM), (((0,), (0,)), ((), ())), preferred_element_type=F32)


def _sig(x):
    return 1.0 / (1.0 + jnp.exp(-x))


def _colsum(x):
    return jnp.sum(x, axis=0, keepdims=True)


def _rowmean(x):
    return jnp.mean(x, axis=-1, keepdims=True)


def _modnorm_fwd(xv, g, sh, sc):
    r = lax.rsqrt(_rowmean(xv * xv) + EPS)
    xh = xv * r
    n = xh * g
    return n * (1.0 + sc) + sh, xh, n, r


def _modnorm_bwd(dh, xh, n, r, g, sc):
    dsc = _colsum(dh * n)
    dsh = _colsum(dh)
    dn = dh * (1.0 + sc)
    dg = _colsum(dn * xh)
    dxh = dn * g
    dx = r * (dxh - xh * _rowmean(dxh * xh))
    return dx, dsh, dsc, dg


def _ffn_fwd(x, mod, mo, gnorm, w_in_t, w_out, res, name, carry):
    T = x.shape[0]
    tm = min(512, T)
    tn = D_FF // 2

    def body(x_ref, mod_ref, g_ref, wi_ref, wo_ref, xo_ref, a_ref, b_ref, f_ref, h_ref):
        xv = x_ref[...]
        h, _, _, _ = _modnorm_fwd(xv, g_ref[...], mod_ref[mo:mo + 1, :], mod_ref[mo + 1:mo + 2, :])
        h = h.astype(ACT)
        h_ref[...] = h
        f = None
        for c0 in range(0, D_FF, tn):
            a = _mm_nt(h, wi_ref[0, c0:c0 + tn, :])
            b = _mm_nt(h, wi_ref[1, c0:c0 + tn, :])
            a_ref[:, c0:c0 + tn] = a.astype(ACT)
            b_ref[:, c0:c0 + tn] = b.astype(ACT)
            part = _mm(a * _sig(a) * b, wo_ref[c0:c0 + tn, :])
            f = part if f is None else f + part
        f_ref[...] = f
        xo_ref[...] = xv + res * mod_ref[mo + 2:mo + 3, :] * f

    tile = pl.BlockSpec((tm, D), lambda i: (i, 0))
    wide = pl.BlockSpec((tm, D_FF), lambda i: (i, 0))
    out = _gridded(
        body, carry, name=name, grid=(T // tm,),
        in_specs=[
            tile,
            pl.BlockSpec((9, D), lambda i: (0, 0)),
            pl.BlockSpec((1, D), lambda i: (0, 0)),
            pl.BlockSpec((2, D_FF, D), lambda i: (0, 0, 0), pipeline_mode=pl.Buffered(1)),
            pl.BlockSpec((D_FF, D), lambda i: (0, 0), pipeline_mode=pl.Buffered(1)),
        ],
        out_specs=[tile, wide, wide, tile, tile],
        out_shape=[
            jax.ShapeDtypeStruct((T, D), F32),
            jax.ShapeDtypeStruct((T, D_FF), ACT),
            jax.ShapeDtypeStruct((T, D_FF), ACT),
            jax.ShapeDtypeStruct((T, D), F32),
            jax.ShapeDtypeStruct((T, D), ACT),
        ],
    )(x, mod, gnorm, w_in_t, w_out)
    return out[:5], out[5:]


def _ffn_bwd_w(h, df, a, b, w_out, name, carry):
    T = h.shape[0]
    tm = min(2048, T)
    ni = T // tm
    tn = 256
    nj = D_FF // tn

    def body(h_ref, df_ref, a_ref, b_ref, wo_ref, da_ref, db_ref, dwi_ref, dwo_ref, acc_i, acc_o):
        i = pl.program_id(1)

        @pl.when(i == 0)
        def _():
            acc_i[...] = jnp.zeros_like(acc_i)
            acc_o[...] = jnp.zeros_like(acc_o)

        hb = h_ref[...]
        df = df_ref[...]
        av = a_ref[...].astype(F32)
        bv = b_ref[...].astype(F32)
        sg = _sig(av)
        sa = av * sg
        s = (sa * bv).astype(MM)
        ds = _mm_nt(df, wo_ref[...])
        da = (ds * bv * sg * (1.0 + av * (1.0 - sg))).astype(MM)
        db = (ds * sa).astype(MM)
        da_ref[...] = da
        db_ref[...] = db
        acc_o[...] += _mm_tn(s, df)
        acc_i[0] += _mm_tn(da, hb)
        acc_i[1] += _mm_tn(db, hb)

        @pl.when(i == ni - 1)
        def _():
            dwi_ref[...] = acc_i[...].astype(MM)
            dwo_ref[...] = acc_o[...].astype(MM)

    out = _gridded(
        body, carry, name=name, grid=(nj, ni),
        in_specs=[
            pl.BlockSpec((tm, D), lambda j, i: (i, 0)),
            pl.BlockSpec((tm, D), lambda j, i: (i, 0)),
            pl.BlockSpec((tm, tn), lambda j, i: (i, j)),
            pl.BlockSpec((tm, tn), lambda j, i: (i, j)),
            pl.BlockSpec((tn, D), lambda j, i: (j, 0)),
        ],
        out_specs=[
            pl.BlockSpec((tm, tn), lambda j, i: (i, j)),
            pl.BlockSpec((tm, tn), lambda j, i: (i, j)),
            pl.BlockSpec((2, tn, D), lambda j, i: (0, j, 0)),
            pl.BlockSpec((tn, D), lambda j, i: (j, 0)),
        ],
        out_shape=[
            jax.ShapeDtypeStruct((T, D_FF), MM),
            jax.ShapeDtypeStruct((T, D_FF), MM),
            jax.ShapeDtypeStruct((2, D_FF, D), MM),
            jax.ShapeDtypeStruct((D_FF, D), MM),
        ],
        scratch_shapes=[pltpu.VMEM((2, tn, D), F32), pltpu.VMEM((tn, D), F32)],
    )(h, df, a, b, w_out)
    return out[:4], out[4:]


def _ffn_bwd_x(x, dxo, f, da, db, mod, mo, gnorm, w_in_t, res, name, carry):
    T = x.shape[0]
    tm = min(512, T)
    ni = T // tm
    tn = D_FF // 2
    nj = D_FF // tn

    def body(x_ref, dxo_ref, f_ref, da_ref, db_ref, mod_ref, g_ref, wi_ref, dx_ref, sm_ref, dh_scr):
        j = pl.program_id(0)
        i = pl.program_id(1)

        @pl.when((j == 0) & (i == 0))
        def _():
            sm_ref[...] = jnp.zeros_like(sm_ref)

        @pl.when(j == 0)
        def _():
            dh_scr[i] = jnp.zeros((tm, D), F32)

        dh_scr[i] += _mm(da_ref[...], wi_ref[0]) + _mm(db_ref[...], wi_ref[1])

        @pl.when(j == nj - 1)
        def _():
            sc = mod_ref[mo + 1:mo + 2, :]
            _, xh, n, r = _modnorm_fwd(x_ref[...], g_ref[...], mod_ref[mo:mo + 1, :], sc)
            dxn, dsh, dsc, dg = _modnorm_bwd(dh_scr[i], xh, n, r, g_ref[...], sc)
            dxo_v = dxo_ref[...]
            dx_ref[...] = dxo_v + dxn
            sm_ref[0:1, :] += dsh
            sm_ref[1:2, :] += dsc
            sm_ref[2:3, :] += _colsum(dxo_v * f_ref[...]) * res
            sm_ref[3:4, :] += dg

    last = pl.BlockSpec((tm, D), lambda j, i: (jnp.where(j == nj - 1, i, 0), 0))
    out = _gridded(
        body, carry, name=name, grid=(nj, ni),
        in_specs=[last, last, last,
                  pl.BlockSpec((tm, tn), lambda j, i: (i, j)), pl.BlockSpec((tm, tn), lambda j, i: (i, j)),
                  pl.BlockSpec((9, D), lambda j, i: (0, 0)), pl.BlockSpec((1, D), lambda j, i: (0, 0)),
                  pl.BlockSpec((2, tn, D), lambda j, i: (0, j, 0))],
        out_specs=[last, pl.BlockSpec((8, D), lambda j, i: (0, 0))],
        out_shape=[jax.ShapeDtypeStruct((T, D), F32), jax.ShapeDtypeStruct((8, D), F32)],
        scratch_shapes=[pltpu.VMEM((ni, tm, D), F32)],
    )(x, dxo, f, da, db, mod, gnorm, w_in_t)
    return out[:2], out[2:]


def _head(x, target, gfin, mod, gate_row, res):
    T = x.shape[0]
    tm = min(512, T)
    ni = T // tm

    def body(x_ref, t_ref, g_ref, mod_ref, dx_ref, df_ref, sm_ref):
        i = pl.program_id(0)

        @pl.when(i == 0)
        def _():
            sm_ref[...] = jnp.zeros_like(sm_ref)

        xv = x_ref[...]
        g = g_ref[...]
        r = lax.rsqrt(_rowmean(xv * xv) + EPS)
        xh = xv * r
        e = xh * g - t_ref[...]
        sm_ref[1:2, :] += _colsum(e * e) * (0.5 / D)
        dy = e * (1.0 / D)
        sm_ref[0:1, :] += _colsum(dy * xh)
        dxh = dy * g
        dx = r * (dxh - xh * _rowmean(dxh * xh))
        dx_ref[...] = dx
        df_ref[...] = (res * mod_ref[gate_row:gate_row + 1, :] * dx).astype(MM)

        @pl.when(i == ni - 1)
        def _():
            sm_ref[1:2, :] = jnp.broadcast_to(jnp.sum(sm_ref[1:2, :], axis=-1, keepdims=True), (1, D))

    tile = pl.BlockSpec((tm, D), lambda i: (i, 0))
    return pl.pallas_call(
        body, name="head_loss", grid=(ni,),
        in_specs=[tile, tile, pl.BlockSpec((1, D), lambda i: (0, 0)), pl.BlockSpec((9, D), lambda i: (0, 0))],
        out_specs=[tile, tile, pl.BlockSpec((8, D), lambda i: (0, 0))],
        out_shape=[jax.ShapeDtypeStruct((T, D), F32), jax.ShapeDtypeStruct((T, D), MM),
                   jax.ShapeDtypeStruct((8, D), F32)],
        compiler_params=_cparams(1),
    )(x, target, gfin, mod)


def _mixin_fwd(x, mod, mo, gnorm, w, carry):
    T = x.shape[0]
    tm = min(1024, T)
    ni = T // tm

    def body(x_ref, mod_ref, g_ref, w_ref, p_ref, h_ref, h_all):
        i = pl.program_id(1)

        @pl.when(pl.program_id(0) == 0)
        def _():
            h, _, _, _ = _modnorm_fwd(x_ref[...], g_ref[...], mod_ref[mo:mo + 1, :], mod_ref[mo + 1:mo + 2, :])
            h_all[i] = h.astype(ACT)
            h_ref[...] = h.astype(ACT)

        p_ref[0] = _mm(h_all[i], w_ref[0])

    first = lambda k, i: (jnp.where(k == 0, i, ni - 1), 0)
    out = _gridded(
        body, carry, name="mixin_fwd", grid=(8, ni),
        in_specs=[pl.BlockSpec((tm, D), first), pl.BlockSpec((9, D), lambda k, i: (0, 0)),
                  pl.BlockSpec((1, D), lambda k, i: (0, 0)), pl.BlockSpec((1, D, D), lambda k, i: (k, 0, 0))],
        out_specs=[pl.BlockSpec((1, tm, D), lambda k, i: (k, i, 0)), pl.BlockSpec((tm, D), first)],
        out_shape=[jax.ShapeDtypeStruct((8, T, D), F32), jax.ShapeDtypeStruct((T, D), ACT)],
        scratch_shapes=[pltpu.VMEM((ni, tm, D), ACT)],
    )(x, mod, gnorm, w)
    return out[:2], out[2:]


def _mixin_bwd(x, h, dxo, dp, mod, mo, gnorm, w, next_gate, next_res, carry):
    T = x.shape[0]
    tm = min(512, T)
    ni = T // tm

    def body(x_ref, h_ref, dxo_ref, dp_ref, mod_ref, g_ref, w_ref, dx_ref, dw_ref, sm_ref, df_ref, dh_scr, acc):
        k = pl.program_id(0)
        i = pl.program_id(1)

        @pl.when(i == 0)
        def _():
            acc[...] = jnp.zeros_like(acc)

        @pl.when(k == 0)
        def _():
            dh_scr[i] = jnp.zeros((tm, D), F32)

        @pl.when((k == 0) & (i == 0))
        def _():
            sm_ref[...] = jnp.zeros_like(sm_ref)

        dpk = dp_ref[0].astype(MM)
        acc[...] += _mm_tn(h_ref[...], dpk)
        dh_scr[i] += _mm_nt(dpk, w_ref[0])

        @pl.when(i == ni - 1)
        def _():
            dw_ref[0] = acc[...].astype(MM)

        @pl.when(k == 7)
        def _():
            sc = mod_ref[mo + 1:mo + 2, :]
            _, xh, n, r = _modnorm_fwd(x_ref[...], g_ref[...], mod_ref[mo:mo + 1, :], sc)
            dxn, dsh, dsc, dg = _modnorm_bwd(dh_scr[i], xh, n, r, g_ref[...], sc)
            dx = dxo_ref[...] + dxn
            dx_ref[...] = dx
            df_ref[...] = (next_res * mod_ref[next_gate:next_gate + 1, :] * dx).astype(MM)
            sm_ref[0:1, :] += dsh
            sm_ref[1:2, :] += dsc
            sm_ref[3:4, :] += dg

    last = pl.BlockSpec((tm, D), lambda k, i: (jnp.where(k == 7, i, 0), 0))
    out = _gridded(
        body, carry, name="mixin_bwd", grid=(8, ni),
        in_specs=[pl.BlockSpec((tm, D), lambda k, i: (jnp.where(k == 7, i, 0), 0)),
                  pl.BlockSpec((tm, D), lambda k, i: (i, 0)),
                  pl.BlockSpec((tm, D), lambda k, i: (jnp.where(k == 7, i, 0), 0)),
                  pl.BlockSpec((1, tm, D), lambda k, i: (k, i, 0)), pl.BlockSpec((9, D), lambda k, i: (0, 0)),
                  pl.BlockSpec((1, D), lambda k, i: (0, 0)), pl.BlockSpec((1, D, D), lambda k, i: (k, 0, 0))],
        out_specs=[last, pl.BlockSpec((1, D, D), lambda k, i: (k, 0, 0)), pl.BlockSpec((8, D), lambda k, i: (0, 0)),
                   last],
        out_shape=[jax.ShapeDtypeStruct((T, D), F32), jax.ShapeDtypeStruct((8, D, D), MM),
                   jax.ShapeDtypeStruct((8, D), F32), jax.ShapeDtypeStruct((T, D), MM)],
        scratch_shapes=[pltpu.VMEM((ni, tm, D), F32), pltpu.VMEM((D, D), F32)],
    )(x, h, dxo, dp, mod, gnorm, w)
    return out[:4], out[4:]


def _hgrn_consts():
    rows = jnp.arange(SUB * HD) // HD
    e = (rows[:, None] == jnp.arange(HD)[None, :]).astype(MM)
    return e, e.T


def _rows_bcast(ref, cb, first, n):
    parts = [jnp.broadcast_to(ref[pl.ds(c * CHUNK + first, 1), :], (n, HD)) for c in range(cb // CHUNK)]
    return jnp.concatenate(parts, axis=0)


def _hgrn_pre(qr, fr, lb_ref, b_scr, cb):
    z = lb_ref[...]
    lb = _sig(z[0:1, :] - z[1:2, :])
    sq = _sig(qr)
    q = qr * sq * Q_SCALE
    sf = _sig(fr)
    fg = lb + (1.0 - lb) * sf
    lf = jnp.log(fg)
    k = 1.0 - fg
    tl = lax.broadcasted_iota(jnp.int32, (cb, HD), 0) % CHUNK
    bc = lf
    sh = 1
    while sh < CHUNK:
        bc = bc + jnp.where(tl >= sh, pltpu.roll(bc, sh, 0), 0.0)
        sh *= 2
    b_scr[...] = bc
    bl = _rows_bcast(b_scr, cb, CHUNK - 1, CHUNK)
    eb = jnp.exp(bc)
    ekd = jnp.exp(bl - bc)
    ekf = jnp.exp(jnp.minimum(-bc, SAFE_EXP))
    return dict(lb=lb, sq=sq, q=q, sf=sf, fg=fg, k=k, tl=tl, b=bc, bl=bl, eb=eb, ekd=ekd, ekf=ekf,
                qe=q * eb, kd=k * ekd, kf=k * ekf, safe=jnp.max(-bc) < SAFE_EXP)


def _hgrn_pre_fused(p_ref, lb_ref, b_scr, q_scr, k_scr, qe_scr, kf_scr, kd_scr, cb):
    z = lb_ref[...]
    lb = _sig(z[0:1, :] - z[1:2, :])
    tl = lax.broadcasted_iota(jnp.int32, (CHUNK, HD), 0)

    def chunk(c, worst):
        r0 = pl.multiple_of(c * CHUNK, CHUNK)
        rs = pl.ds(r0, CHUNK)
        qr = p_ref[0, rs, :]
        q = qr * _sig(qr) * Q_SCALE
        fg = lb + (1.0 - lb) * _sig(p_ref[1, rs, :])
        k = 1.0 - fg
        bc = jnp.log(fg)
        sh = 1
        while sh < CHUNK:
            bc = bc + jnp.where(tl >= sh, pltpu.roll(bc, sh, 0), 0.0)
            sh *= 2
        b_scr[rs, :] = bc
        q_scr[rs, :] = q
        k_scr[rs, :] = k
        qe_scr[rs, :] = (q * jnp.exp(bc)).astype(MM)
        kf_scr[rs, :] = (k * jnp.exp(jnp.minimum(-bc, SAFE_EXP))).astype(MM)
        kd_scr[rs, :] = (k * jnp.exp(b_scr[pl.ds(r0 + CHUNK - 1, 1), :] - bc)).astype(MM)
        return jnp.maximum(worst, -bc)

    worst = lax.fori_loop(0, cb // CHUNK, chunk, jnp.zeros((CHUNK, HD), F32))
    return jnp.max(worst) < SAFE_EXP


def _hgrn_sub(pre, b_scr, cb):
    bc, tl, q, k = pre["b"], pre["tl"], pre["q"], pre["k"]
    br = [None] + [_rows_bcast(b_scr, cb, SUB * i - 1, CHUNK) for i in range(1, NSUB)]
    sb = tl // SUB
    bref = jnp.where(sb == 0, bc, jnp.where(sb == 1, br[1], jnp.where(sb == 2, br[2], br[3])))
    eqo = jnp.exp(bc - bref)
    eko = [None] + [jnp.exp(jnp.where(tl < SUB * i, br[i] - bc, NEG)) for i in range(1, NSUB)]
    return dict(eqo=eqo, eko=eko, qo=q * eqo, ko=[None] + [k * eko[i] for i in range(1, NSUB)])


def _pad_rows(x):
    return jnp.concatenate([x, jnp.zeros_like(x)], axis=0)


def _by_subblock(sbc, parts):
    out = jnp.zeros_like(parts[1])
    for i in range(1, NSUB):
        out = jnp.where(sbc == i, parts[i], out)
    return out


def _hgrn_fwd(p, hgrn_lb, hgrn_g, carry):
    T = p.shape[1]
    cb = min(HGRN_BLOCK, T)
    nch = cb // CHUNK
    ncb = T // cb
    e_mat, _ = _hgrn_consts()

    def body(p_ref, lb_ref, g_ref, e_ref, o_ref, oa_ref, a_ref, s_ref, st_scr, q_scr, k_scr, b_scr, z_scr, ad_scr,
             qe_scr, kf_scr, kd_scr):
        @pl.when(pl.program_id(1) == 0)
        def _():
            st_scr[...] = jnp.zeros_like(st_scr)

        safe = _hgrn_pre_fused(p_ref, lb_ref, b_scr, q_scr, k_scr, qe_scr, kf_scr, kd_scr, cb)
        chunks = [slice(c * CHUNK, (c + 1) * CHUNK) for c in range(nch)]
        row_i = lax.broadcasted_iota(jnp.int32, (CHUNK, HD), 0)
        lane_i = lax.broadcasted_iota(jnp.int32, (CHUNK, HD), 1)
        sbc = row_i // SUB
        causal = lane_i <= row_i

        @pl.when(safe)
        def _():
            for rs in chunks:
                ad_scr[rs, :] = jnp.where(causal, _mm_nt(qe_scr[rs, :], _pad_rows(kf_scr[rs, :])), 0.0)

        @pl.when(jnp.logical_not(safe))
        def _():
            tl = lax.broadcasted_iota(jnp.int32, (cb, HD), 0) % CHUNK
            sub = _hgrn_sub(dict(b=b_scr[...], tl=tl, q=q_scr[...], k=k_scr[...]), b_scr, cb)
            ti = lax.broadcasted_iota(jnp.int32, (SUB, HD), 0)

            def zbody(c, carry):
                for i in range(NSUB):
                    r0 = pl.multiple_of(c * CHUNK + SUB * i, SUB)
                    qi = q_scr[pl.ds(r0, SUB), :]
                    bi = b_scr[pl.ds(r0, SUB), :]
                    for s in range(SUB):
                        krow = k_scr[pl.ds(r0 + s, 1), :]
                        brow = b_scr[pl.ds(r0 + s, 1), :]
                        if s < 8:
                            zz = qi * krow * jnp.exp(jnp.where(ti >= s, bi - brow, NEG))
                        else:
                            lo = qi[8:] * krow * jnp.exp(jnp.where(ti[8:] >= s, bi[8:] - brow, NEG))
                            zz = jnp.concatenate([jnp.zeros((8, HD), F32), lo], axis=0)
                        z_scr[i, pl.ds(pl.multiple_of(c * SUB, SUB), SUB), s * HD:(s + 1) * HD] = zz.astype(MM)
                return carry

            lax.fori_loop(0, nch, zbody, 0)
            adiag = [_mm(z_scr[i], e_ref[...]) for i in range(NSUB)]
            offs = [[_mm_nt(sub["qo"][rs], _pad_rows(sub["ko"][i][rs])) for i in range(1, NSUB)] for rs in chunks]
            for c, rs in enumerate(chunks):
                dparts = []
                for i in range(NSUB):
                    blk = adiag[i][c * SUB:(c + 1) * SUB]
                    dparts.append(blk if i == 0 else pltpu.roll(blk, SUB * i, 1))
                ad_scr[rs, :] = _by_subblock(sbc, [None] + offs[c]) + jnp.concatenate(dparts, axis=0)

        kv = [_mm_tn(p_ref[2, rs, :], kd_scr[rs, :]) for rs in chunks]
        a_ref[0] = ad_scr[...]
        o_intra = [_mm(ad_scr[rs, :], _pad_rows(p_ref[2, rs, :])) for rs in chunks]
        states = []
        st = st_scr[...]
        for c in range(nch):
            states.append(st)
            st = st * jnp.exp(b_scr[pl.ds(c * CHUNK + CHUNK - 1, 1), :]) + kv[c]
        st_scr[...] = st
        g = g_ref[...]
        for c, rs in enumerate(chunks):
            s_ref[0, c] = states[c]
            o = o_intra[c] + _mm_nt(qe_scr[rs, :], states[c])
            o_ref[rs, :] = o
            og = p_ref[3, rs, :]
            oa_ref[rs, :] = (o * lax.rsqrt(_rowmean(o * o) + EPS) * g * og * _sig(og)).astype(ACT)

    out = _gridded(
        body, carry, name="hgrn_fwd", grid=(HEADS, ncb),
        in_specs=[pl.BlockSpec((4, cb, HD), lambda h, c: (0, c, h)),
                  pl.BlockSpec((2, HD), lambda h, c: (0, h)),
                  pl.BlockSpec((1, HD), lambda h, c: (0, h)),
                  pl.BlockSpec((SUB * HD, HD), lambda h, c: (0, 0))],
        out_specs=[pl.BlockSpec((cb, HD), lambda h, c: (c, h)),
                   pl.BlockSpec((cb, HD), lambda h, c: (c, h)),
                   pl.BlockSpec((1, cb, HD), lambda h, c: (h, c, 0)),
                   pl.BlockSpec((1, nch, HD, HD), lambda h, c: (h, c, 0, 0))],
        out_shape=[jax.ShapeDtypeStruct((T, D), F32), jax.ShapeDtypeStruct((T, D), ACT),
                   jax.ShapeDtypeStruct((HEADS, T, HD), F32),
                   jax.ShapeDtypeStruct((HEADS, T // CHUNK, HD, HD), F32)],
        scratch_shapes=[pltpu.VMEM((HD, HD), F32), pltpu.VMEM((cb, HD), F32), pltpu.VMEM((cb, HD), F32),
                        pltpu.VMEM((cb, HD), F32), pltpu.VMEM((NSUB, nch * SUB, SUB * HD), MM),
                        pltpu.VMEM((cb, HD), F32), pltpu.VMEM((cb, HD), MM), pltpu.VMEM((cb, HD), MM),
                        pltpu.VMEM((cb, HD), MM)],
    )(p, hgrn_lb, hgrn_g, e_mat)
    return out[:4], out[4:]


def _hgrn_bwd(p, o, a_all, s_all, doa, hgrn_lb, hgrn_g, dp, carry):
    T = p.shape[1]
    cb = min(HGRN_BLOCK, T)
    nch = cb // CHUNK
    ncb = T // cb
    _, et_mat = _hgrn_consts()

    def body(p_ref, o_ref, a_ref, s_ref, doa_ref, lb_ref, g_ref, et_ref, dp_in, dp_ref, sm_ref,
             dst_scr, q_scr, k_scr, b_scr, x_scr, dqd_scr, dkd_scr):
        del dp_in

        @pl.when(pl.program_id(1) == 0)
        def _():
            dst_scr[...] = jnp.zeros_like(dst_scr)
            sm_ref[...] = jnp.zeros_like(sm_ref)

        qr = p_ref[0]
        v = p_ref[2]
        og = p_ref[3]
        pre = _hgrn_pre(qr, p_ref[1], lb_ref, b_scr, cb)
        q, k = pre["q"], pre["k"]
        g = g_ref[...]
        ov = o_ref[...]
        r = lax.rsqrt(_rowmean(ov * ov) + EPS)
        oh = ov * r
        sgo = _sig(og)
        doa_v = doa_ref[...]
        don = doa_v * og * sgo
        dog = doa_v * oh * g * sgo * (1.0 + og * (1.0 - sgo))
        sm_ref[1:2, :] += _colsum(don * oh)
        doh = don * g
        do = r * (doh - oh * _rowmean(doh * oh))

        sbc = lax.broadcasted_iota(jnp.int32, (CHUNK, HD), 0) // SUB
        row_i = lax.broadcasted_iota(jnp.int32, (CHUNK, HD), 0)
        lane_i = lax.broadcasted_iota(jnp.int32, (CHUNK, HD), 1)
        causal = lane_i <= row_i
        chunks = [slice(c * CHUNK, (c + 1) * CHUNK) for c in range(nch)]
        da_parts = [jnp.where(causal, _mm_nt(do[rs], _pad_rows(v[rs])), 0.0) for rs in chunks]
        dv_parts = [_mm_tn(a_ref[0, rs, :], do[rs])[:CHUNK] for rs in chunks]

        @pl.when(pre["safe"])
        def _():
            hi = dict(preferred_element_type=F32, precision=lax.Precision.HIGH)
            for c, rs in enumerate(chunks):
                dqd_scr[rs, :] = pre["eb"][rs] * lax.dot_general(
                    da_parts[c], _pad_rows(pre["kf"][rs]), (((1,), (0,)), ((), ())), **hi)
                dkd_scr[rs, :] = pre["ekf"][rs] * lax.dot_general(
                    da_parts[c], pre["qe"][rs], (((0,), (0,)), ((), ())), **hi)[:CHUNK]

        @pl.when(jnp.logical_not(pre["safe"]))
        def _():
            sub = _hgrn_sub(pre, b_scr, cb)
            dqoff_mm = [[_mm(da_parts[c], _pad_rows(sub["ko"][i][rs])) for i in range(1, NSUB)]
                        for c, rs in enumerate(chunks)]
            dkoff_mm = [[_mm_tn(jnp.where(sbc == i, da_parts[c], 0.0), sub["qo"][rs])[:CHUNK]
                         for i in range(1, NSUB)] for c, rs in enumerate(chunks)]
            dqoff_parts = [_by_subblock(sbc, [None] + dqoff_mm[c]) for c in range(nch)]
            dkoff_parts = []
            for c, rs in enumerate(chunks):
                dko = sub["eko"][1][rs] * dkoff_mm[c][0]
                for i in range(2, NSUB):
                    dko = dko + sub["eko"][i][rs] * dkoff_mm[c][i - 1]
                dkoff_parts.append(dko)
            q_scr[...] = q
            k_scr[...] = k
            for i in range(NSUB):
                rows = []
                for c in range(nch):
                    blk = da_parts[c][SUB * i:SUB * (i + 1)]
                    rows.append(blk if i == 0 else pltpu.roll(blk, HD - SUB * i, 1))
                x_scr[i] = _mm(jnp.concatenate(rows, axis=0), et_ref[...])
            ti = lax.broadcasted_iota(jnp.int32, (SUB, HD), 0)

            def dbody(c, carry):
                for i in range(NSUB):
                    r0 = pl.multiple_of(c * CHUNK + SUB * i, SUB)
                    qi = q_scr[pl.ds(r0, SUB), :]
                    bi = b_scr[pl.ds(r0, SUB), :]
                    dq_hi = jnp.zeros((8, HD), F32)
                    dq_lo = jnp.zeros((8, HD), F32)
                    dk_hi = jnp.zeros((8, HD), F32)
                    dk_lo = jnp.zeros((8, HD), F32)
                    c0 = pl.multiple_of(c * SUB, SUB)
                    t8 = ti[:8]
                    for s in range(SUB):
                        krow = k_scr[pl.ds(r0 + s, 1), :]
                        brow = b_scr[pl.ds(r0 + s, 1), :]
                        w_lo = (x_scr[i, pl.ds(c0 + 8, 8), s * HD:(s + 1) * HD]
                                * jnp.exp(jnp.where(t8 + 8 >= s, bi[8:] - brow, NEG)))
                        dq_lo = dq_lo + w_lo * krow
                        col = _colsum(w_lo * qi[8:])
                        if s < 8:
                            w_hi = (x_scr[i, pl.ds(c0, 8), s * HD:(s + 1) * HD]
                                    * jnp.exp(jnp.where(t8 >= s, bi[:8] - brow, NEG)))
                            dq_hi = dq_hi + w_hi * krow
                            dk_hi = jnp.where(t8 == s, col + _colsum(w_hi * qi[:8]), dk_hi)
                        else:
                            dk_lo = jnp.where(t8 + 8 == s, col, dk_lo)
                    dqd_scr[pl.ds(r0, SUB), :] = jnp.concatenate([dq_hi, dq_lo], axis=0)
                    dkd_scr[pl.ds(r0, SUB), :] = jnp.concatenate([dk_hi, dk_lo], axis=0)
                return carry

            lax.fori_loop(0, nch, dbody, 0)
            dqd_scr[...] += jnp.concatenate(dqoff_parts, axis=0) * sub["eqo"]
            dkd_scr[...] += jnp.concatenate(dkoff_parts, axis=0)

        qdo = [_mm_tn(do[rs], pre["qe"][rs]) for rs in chunks]
        dsts = [None] * nch
        dst = dst_scr[...]
        for c in reversed(range(nch)):
            dsts[c] = dst
            dst = dst * jnp.exp(b_scr[pl.ds(c * CHUNK + CHUNK - 1, 1), :]) + qdo[c]
        dst_scr[...] = dst
        sts = [s_ref[0, c] for c in range(nch)]
        dqe_parts = [_mm(do[rs], sts[c]) for c, rs in enumerate(chunks)]
        dkdec_parts = [_mm(v[rs], dsts[c]) for c, rs in enumerate(chunks)]
        dvi_parts = [_mm_nt(pre["kd"][rs], dsts[c]) for c, rs in enumerate(chunks)]
        debl_parts = [_colsum(dsts[c] * sts[c]) for c in range(nch)]
        dqe = jnp.concatenate(dqe_parts, axis=0)
        dkdec = jnp.concatenate(dkdec_parts, axis=0)
        dq_tot = dqd_scr[...] + dqe * pre["eb"]
        dk_inter = dkdec * pre["ekd"]
        dk_tot = dkd_scr[...] + dk_inter
        db = q * dq_tot - k * dk_tot
        kdk = k * dk_inter
        dbl = jnp.concatenate(
            [jnp.broadcast_to(jnp.exp(b_scr[pl.ds(c * CHUNK + CHUNK - 1, 1), :]) * debl_parts[c]
                              + _colsum(kdk[c * CHUNK:(c + 1) * CHUNK]), (CHUNK, HD)) for c in range(nch)], axis=0)
        tl = pre["tl"]
        rc = db
        sh = 1
        while sh < CHUNK:
            rc = rc + jnp.where(tl + sh < CHUNK, pltpu.roll(rc, cb - sh, 0), 0.0)
            sh *= 2
        dlf = rc + dbl
        dfg = dlf / pre["fg"] - dk_tot
        sf = pre["sf"]
        lb = pre["lb"]
        sm_ref[0:1, :] += _colsum(dfg * (1.0 - sf))
        sq = pre["sq"]
        dp_ref[0] = (dq_tot * Q_SCALE * sq * (1.0 + qr * (1.0 - sq))).astype(ACT)
        dp_ref[1] = (dfg * (1.0 - lb) * sf * (1.0 - sf)).astype(ACT)
        dp_ref[2] = (jnp.concatenate(dv_parts, axis=0) + jnp.concatenate(dvi_parts, axis=0)).astype(ACT)
        dp_ref[3] = dog.astype(ACT)

    rev = lambda c: ncb - 1 - c
    out = _gridded(
        body, carry, name="hgrn_bwd", grid=(HEADS, ncb),
        in_specs=[pl.BlockSpec((4, cb, HD), lambda h, c: (0, rev(c), h)),
                  pl.BlockSpec((cb, HD), lambda h, c: (rev(c), h)),
                  pl.BlockSpec((1, cb, HD), lambda h, c: (h, rev(c), 0)),
                  pl.BlockSpec((1, nch, HD, HD), lambda h, c: (h, rev(c), 0, 0)),
                  pl.BlockSpec((cb, HD), lambda h, c: (rev(c), h)),
                  pl.BlockSpec((2, HD), lambda h, c: (0, h)),
                  pl.BlockSpec((1, HD), lambda h, c: (0, h)),
                  pl.BlockSpec((HD, SUB * HD), lambda h, c: (0, 0)),
                  pl.BlockSpec(memory_space=pl.ANY)],
        out_specs=[pl.BlockSpec((4, cb, HD), lambda h, c: (0, rev(c), h)),
                   pl.BlockSpec((8, HD), lambda h, c: (0, h))],
        out_shape=[jax.ShapeDtypeStruct(dp.shape, dp.dtype), jax.ShapeDtypeStruct((8, D), F32)],
        aliases={8: 0},
        scratch_shapes=[pltpu.VMEM((HD, HD), F32), pltpu.VMEM((cb, HD), F32), pltpu.VMEM((cb, HD), F32),
                        pltpu.VMEM((cb, HD), F32), pltpu.VMEM((NSUB, nch * SUB, SUB * HD), F32),
                        pltpu.VMEM((cb, HD), F32), pltpu.VMEM((cb, HD), F32)],
    )(p, o, a_all, s_all, doa, hgrn_lb, hgrn_g, et_mat, dp)
    return out[:2], out[2:]


def _ln_fwd(u1, g, b):
    mu = _rowmean(u1)
    xc = u1 - mu
    rs = lax.rsqrt(_rowmean(xc * xc) + EPS)
    xh = xc * rs
    return xh * g + b, xh, rs


CONV_RB = 64
LANES = 128


def _shift_rows(src, sh, ls, n):
    for r in range(1, 8):
        sh[r - 1, 0:n, :] = src[pl.ds(r, n), ls]


def _tap(src, sh, ls, off, r0, rows):
    r = off % 8
    if r == 0:
        return src[pl.ds(r0 + off, rows), ls]
    return sh[r - 1, pl.ds(r0 + off - r, rows), :]


def _conv_fwd(p, cw, cb_, lng, lnb):
    T = p.shape[1]
    tm = min(512, T)
    n = HALO + tm - 8

    def body(p_ref, cw_ref, cb_ref, g_ref, b_ref, u1_ref, u2_ref, buf, sh):
        @pl.when(pl.program_id(0) == 0)
        def _():
            buf[0:HALO, :] = jnp.zeros((HALO, D), F32)

        buf[HALO:HALO + tm, :] = p_ref[0] * _sig(p_ref[1])
        for lb in range(D // LANES):
            ls = slice(lb * LANES, (lb + 1) * LANES)
            _shift_rows(buf, sh, ls, n)
            taps = [cw_ref[j:j + 1, ls] for j in range(CONV_K)]
            bias = cb_ref[:, ls]

            def rows_body(rb, carry):
                r0 = pl.multiple_of(rb * CONV_RB, CONV_RB)
                acc = jnp.broadcast_to(bias, (CONV_RB, LANES))
                for j in range(CONV_K):
                    acc = acc + taps[j] * _tap(buf, sh, ls, HALO - (CONV_K - 1) + j, r0, CONV_RB)
                u1_ref[pl.ds(r0, CONV_RB), ls] = acc
                return carry

            lax.fori_loop(0, tm // CONV_RB, rows_body, 0)
        y, _, _ = _ln_fwd(u1_ref[...], g_ref[...], b_ref[...])
        u2_ref[...] = (y * _sig(y)).astype(ACT)
        buf[0:HALO, :] = buf[tm:tm + HALO, :]

    return pl.pallas_call(
        body, name="conv_fwd", grid=(T // tm,),
        in_specs=[pl.BlockSpec((2, tm, D), lambda i: (2, i, 0)), pl.BlockSpec((HALO, D), lambda i: (0, 0)),
                  pl.BlockSpec((1, D), lambda i: (0, 0)), pl.BlockSpec((1, D), lambda i: (0, 0)),
                  pl.BlockSpec((1, D), lambda i: (0, 0))],
        out_specs=[pl.BlockSpec((tm, D), lambda i: (i, 0)), pl.BlockSpec((tm, D), lambda i: (i, 0))],
        out_shape=[jax.ShapeDtypeStruct((T, D), F32), jax.ShapeDtypeStruct((T, D), ACT)],
        scratch_shapes=[pltpu.VMEM((HALO + tm, D), F32), pltpu.VMEM((7, n, LANES), F32)],
        compiler_params=_cparams(1),
    )(p, cw, cb_, lng, lnb)


def _conv_bwd(p, u1, du2, cw, lng, lnb, dp):
    T = p.shape[1]
    tm = min(512, T)
    ni = T // tm
    hb = tm // HALO

    n = HALO + tm - 8

    def body(p_ref, ph_ref, u1_ref, du2_ref, cw_ref, g_ref, b_ref, dp_in, dp_ref, dcw_ref, sm_ref, ubuf, dbuf,
             sh, dacc):
        del dp_in
        step = pl.program_id(0)

        @pl.when(step == 0)
        def _():
            dbuf[tm:tm + HALO, :] = jnp.zeros((HALO, D), F32)
            dcw_ref[...] = jnp.zeros_like(dcw_ref)
            sm_ref[...] = jnp.zeros_like(sm_ref)

        ua = p_ref[0]
        sgb = _sig(p_ref[1])
        halo = ph_ref[0] * _sig(ph_ref[1])
        ubuf[0:HALO, :] = jnp.where(step == ni - 1, 0.0, halo)
        ubuf[HALO:HALO + tm, :] = ua * sgb
        g = g_ref[...]
        y, xh, rs = _ln_fwd(u1_ref[...], g, b_ref[...])
        sy = _sig(y)
        dy = du2_ref[...] * sy * (1.0 + y * (1.0 - sy))
        sm_ref[1:2, :] += _colsum(dy * xh)
        sm_ref[2:3, :] += _colsum(dy)
        dxh = dy * g
        du1 = rs * (dxh - _rowmean(dxh) - xh * _rowmean(dxh * xh))
        sm_ref[0:1, :] += _colsum(du1)
        dbuf[0:tm, :] = du1
        for lb in range(D // LANES):
            ls = slice(lb * LANES, (lb + 1) * LANES)
            taps = [cw_ref[j:j + 1, ls] for j in range(CONV_K)]
            _shift_rows(dbuf, sh, ls, n)

            def du0_body(rb, carry):
                r0 = pl.multiple_of(rb * CONV_RB, CONV_RB)
                acc = jnp.zeros((CONV_RB, LANES), F32)
                for j in range(CONV_K):
                    acc = acc + taps[j] * _tap(dbuf, sh, ls, CONV_K - 1 - j, r0, CONV_RB)
                dp_ref[0, pl.ds(r0, CONV_RB), ls] = acc.astype(ACT)
                return carry

            lax.fori_loop(0, tm // CONV_RB, du0_body, 0)
            _shift_rows(ubuf, sh, ls, n)
            dacc[...] = jnp.zeros_like(dacc)

            def dcw_body(rb, carry):
                r0 = pl.multiple_of(rb * CONV_RB, CONV_RB)
                d = dbuf[pl.ds(r0, CONV_RB), ls]
                for j in range(CONV_K):
                    prod = d * _tap(ubuf, sh, ls, HALO - (CONV_K - 1) + j, r0, CONV_RB)
                    dacc[8 * j:8 * j + 8, :] += jnp.sum(prod.reshape(CONV_RB // 8, 8, LANES), axis=0)
                return carry

            lax.fori_loop(0, tm // CONV_RB, dcw_body, 0)
            for j in range(CONV_K):
                dcw_ref[j:j + 1, ls] += _colsum(dacc[8 * j:8 * j + 8, :])
        du0 = dp_ref[0].astype(F32)
        dp_ref[0] = (du0 * sgb).astype(ACT)
        dp_ref[1] = (du0 * ua * sgb * (1.0 - sgb)).astype(ACT)
        dbuf[tm:tm + HALO, :] = dbuf[0:HALO, :]

    rev = lambda i: ni - 1 - i
    return pl.pallas_call(
        body, name="conv_bwd", grid=(ni,),
        in_specs=[pl.BlockSpec((2, tm, D), lambda i: (2, rev(i), 0)),
                  pl.BlockSpec((2, HALO, D), lambda i: (2, jnp.maximum(rev(i) * hb - 1, 0), 0)),
                  pl.BlockSpec((tm, D), lambda i: (rev(i), 0)), pl.BlockSpec((tm, D), lambda i: (rev(i), 0)),
                  pl.BlockSpec((HALO, D), lambda i: (0, 0)), pl.BlockSpec((1, D), lambda i: (0, 0)),
                  pl.BlockSpec((1, D), lambda i: (0, 0)), pl.BlockSpec(memory_space=pl.ANY)],
        out_specs=[pl.BlockSpec((2, tm, D), lambda i: (2, rev(i), 0)),
                   pl.BlockSpec((HALO, D), lambda i: (0, 0)), pl.BlockSpec((8, D), lambda i: (0, 0))],
        out_shape=[jax.ShapeDtypeStruct(dp.shape, dp.dtype), jax.ShapeDtypeStruct((HALO, D), F32),
                   jax.ShapeDtypeStruct((8, D), F32)],
        input_output_aliases={7: 0},
        scratch_shapes=[pltpu.VMEM((HALO + tm, D), F32), pltpu.VMEM((tm + HALO, D), F32),
                        pltpu.VMEM((7, n, LANES), F32), pltpu.VMEM((8 * CONV_K, LANES), F32)],
        compiler_params=_cparams(1),
    )(p, p, u1, du2, cw, lng, lnb, dp)


def _mixout_fwd(x, oa, u2, p, mod, mo, w_a, w_b, w_o):
    T = x.shape[0]
    tm = min(512, T)

    def body(x_ref, oa_ref, u2_ref, p_ref, mod_ref, wa_ref, wb_ref, wo_ref, xo_ref, ya_ref, yb_ref, mo_ref):
        ya = _mm(oa_ref[...], wa_ref[...])
        yb = _mm(u2_ref[...], wb_ref[...])
        ya_ref[...] = ya.astype(ACT)
        yb_ref[...] = yb.astype(ACT)
        merged = _sig(p_ref[0]) * ya + _sig(p_ref[1]) * yb
        out = _mm(merged, wo_ref[...])
        mo_ref[...] = out
        xo_ref[...] = x_ref[...] + mod_ref[mo + 2:mo + 3, :] * out

    tile = pl.BlockSpec((tm, D), lambda i: (i, 0))
    wspec = pl.BlockSpec((D, D), lambda i: (0, 0))
    return pl.pallas_call(
        body, name="mixout_fwd", grid=(T // tm,),
        in_specs=[tile, tile, tile, pl.BlockSpec((2, tm, D), lambda i: (3, i, 0)),
                  pl.BlockSpec((9, D), lambda i: (0, 0)), wspec, wspec, wspec],
        out_specs=[tile, tile, tile, tile],
        out_shape=[jax.ShapeDtypeStruct((T, D), F32), jax.ShapeDtypeStruct((T, D), ACT),
                   jax.ShapeDtypeStruct((T, D), ACT), jax.ShapeDtypeStruct((T, D), F32)],
        compiler_params=_cparams(1),
    )(x, oa, u2, p, mod, w_a, w_b, w_o)


def _mixout_bwd(dxo, oa, u2, ya, yb, mout, p, mod, mo, w_a, w_b, w_o):
    T = dxo.shape[0]
    tm = min(256, T)

    def body(dxo_ref, oa_ref, u2_ref, ya_ref, yb_ref, mo_ref, p_ref, mod_ref, wa_ref, wb_ref, wo_ref,
             dp_ref, doa_ref, du2_ref, dwa_ref, dwb_ref, dwo_ref, sm_ref):
        @pl.when(pl.program_id(0) == 0)
        def _():
            dwa_ref[...] = jnp.zeros_like(dwa_ref)
            dwb_ref[...] = jnp.zeros_like(dwb_ref)
            dwo_ref[...] = jnp.zeros_like(dwo_ref)
            sm_ref[...] = jnp.zeros_like(sm_ref)

        dxo_v = dxo_ref[...]
        sm_ref[2:3, :] += _colsum(dxo_v * mo_ref[...])
        dmo = (mod_ref[mo + 2:mo + 3, :] * dxo_v).astype(MM)
        ya = ya_ref[...].astype(F32)
        yb = yb_ref[...].astype(F32)
        sga = _sig(p_ref[0])
        sgb = _sig(p_ref[1])
        merged = (sga * ya + sgb * yb).astype(MM)
        dwo_ref[...] += _mm_tn(merged, dmo)
        dmg = _mm_nt(dmo, wo_ref[...])
        dp_ref[0] = (dmg * ya * sga * (1.0 - sga)).astype(ACT)
        dp_ref[1] = (dmg * yb * sgb * (1.0 - sgb)).astype(ACT)
        dya = (dmg * sga).astype(MM)
        dyb = (dmg * sgb).astype(MM)
        dwa_ref[...] += _mm_tn(oa_ref[...], dya)
        dwb_ref[...] += _mm_tn(u2_ref[...], dyb)
        doa_ref[...] = _mm_nt(dya, wa_ref[...])
        du2_ref[...] = _mm_nt(dyb, wb_ref[...])

    tile = pl.BlockSpec((tm, D), lambda i: (i, 0))
    wspec = pl.BlockSpec((D, D), lambda i: (0, 0))
    return pl.pallas_call(
        body, name="mixout_bwd", grid=(T // tm,),
        in_specs=[tile, tile, tile, tile, tile, tile, pl.BlockSpec((2, tm, D), lambda i: (3, i, 0)),
                  pl.BlockSpec((9, D), lambda i: (0, 0)), wspec, wspec, wspec],
        out_specs=[pl.BlockSpec((2, tm, D), lambda i: (3, i, 0)), tile, tile, wspec, wspec, wspec,
                   pl.BlockSpec((8, D), lambda i: (0, 0))],
        out_shape=[jax.ShapeDtypeStruct((8, T, D), ACT), jax.ShapeDtypeStruct((T, D), F32),
                   jax.ShapeDtypeStruct((T, D), F32), jax.ShapeDtypeStruct((D, D), F32),
                   jax.ShapeDtypeStruct((D, D), F32), jax.ShapeDtypeStruct((D, D), F32),
                   jax.ShapeDtypeStruct((8, D), F32)],
        compiler_params=_cparams(1),
    )(dxo, oa, u2, ya, yb, mout, p, mod, w_a, w_b, w_o)


def _ada_wgrad(cs_all, dmod_cols):
    cs_t = jnp.pad(cs_all.T, ((0, 0), (0, HD - N_DEV)))
    dm = jnp.pad(dmod_cols, ((0, HD - N_DEV), (0, 0)))

    def body(cs_ref, d_ref, out_ref):
        out_ref[...] = jnp.dot(cs_ref[...], d_ref[...], preferred_element_type=F32,
                               precision=lax.Precision.HIGHEST)

    return pl.pallas_call(
        body, name="ada_wgrad", out_shape=jax.ShapeDtypeStruct((D, dmod_cols.shape[1]), F32),
        compiler_params=pltpu.CompilerParams(vmem_limit_bytes=VMEM_LIMIT),
    )(cs_t, dm)


def _adam_math(w, g, m, v):
    m2 = ADAM_B1 * m + (1.0 - ADAM_B1) * g
    v2 = ADAM_B2 * v + (1.0 - ADAM_B2) * (g * g)
    m_hat = m2 / (1.0 - ADAM_B1 ** ADAM_STEP)
    v_hat = v2 / (1.0 - ADAM_B2 ** ADAM_STEP)
    delta = -ADAM_LR * (m_hat / (jnp.sqrt(v_hat) + ADAM_EPS) + ADAM_WD * w)
    return delta, m2, v2


def _adamw(w, m, v, g, name):
    R, C = w.shape
    slots = g.ndim == 3
    n_slots = g.shape[0] if slots else 0
    tr = R
    for cand in (256, 176):
        if R % cand == 0 and R > cand:
            tr = cand
            break

    def body(w_ref, m_ref, v_ref, g_ref, go_ref, d_ref, mo_ref, vo_ref):
        if slots:
            gv = g_ref[0].astype(F32)
            for s in range(1, n_slots):
                gv = gv + g_ref[s].astype(F32)
        else:
            gv = g_ref[...]
        go_ref[...] = gv
        d_ref[...], mo_ref[...], vo_ref[...] = _adam_math(w_ref[...], gv, m_ref[...], v_ref[...])

    tile = pl.BlockSpec((tr, C), lambda i: (i, 0))
    gspec = pl.BlockSpec((n_slots, tr, C), lambda i: (0, i, 0)) if slots else tile
    sds = jax.ShapeDtypeStruct((R, C), F32)
    return pl.pallas_call(
        body, name=name, grid=(R // tr,), in_specs=[tile, tile, tile, gspec], out_specs=[tile] * 4,
        out_shape=[sds] * 4, compiler_params=_cparams(1),
    )(w, m, v, g)


def _sum_slots(pack, name, tr):
    n, R, C = pack.shape

    def body(p_ref, out_ref):
        acc = p_ref[0].astype(F32)
        for s in range(1, n):
            acc = acc + p_ref[s].astype(F32)
        out_ref[...] = acc

    return pl.pallas_call(
        body, name=name, grid=(R // tr,), in_specs=[pl.BlockSpec((n, tr, C), lambda i: (0, i, 0))],
        out_specs=pl.BlockSpec((tr, C), lambda i: (i, 0)), out_shape=jax.ShapeDtypeStruct((R, C), F32),
        compiler_params=_cparams(1))(pack)


def _me():
    return lax.axis_index("x"), lax.axis_index("y"), lax.axis_index("c")


def _peer(r):
    x, y, c = _me()
    px = 1 - x if r & 4 else x
    py = 1 - y if r & 2 else y
    pc = 1 - c if r & 1 else c
    return (px, py, pc), 4 * px + 2 * py + pc


def _small_gather(x_ref, out_ref, send_sems, recv_sems):
    R = x_ref.shape[0]
    mx, my, mc = _me()
    me = 4 * mx + 2 * my + mc
    mine = out_ref.at[pl.ds(pl.multiple_of(me * R, 8), R), :]
    copies = []
    for r in range(1, N_DEV):
        dev, _ = _peer(r)
        copies.append(pltpu.make_async_remote_copy(
            src_ref=x_ref, dst_ref=mine, send_sem=send_sems.at[r - 1], recv_sem=recv_sems.at[r - 1],
            device_id=dev, device_id_type=MESH))
    for cp in copies:
        cp.start()
    mine[...] = x_ref[...]
    for r in range(1, N_DEV):
        dev, idx = _peer(r)
        theirs = out_ref.at[pl.ds(pl.multiple_of(idx * R, 8), R), :]
        pltpu.make_async_remote_copy(
            src_ref=x_ref, dst_ref=theirs, send_sem=send_sems.at[r - 1], recv_sem=recv_sems.at[r - 1],
            device_id=dev, device_id_type=MESH).wait_recv()
    for cp in copies:
        cp.wait_send()


def _prologue(cs, ada_w, ada_b_cols, big):
    n = len(big)
    ncol = ada_w.shape[1]
    big_shape, big_sems = _xchg_specs(big, "gather")

    def body(cs_ref, w_ref, b_ref, *rest):
        big_in, cs_all, mod_all, big_out = rest[:n], rest[n], rest[n + 1], rest[n + 2:2 * n + 2]
        mod_scr, s1, r1, s2, r2 = rest[2 * n + 2:2 * n + 7]
        sems = rest[2 * n + 7:]
        _xchg_start(big_in, big_out, sems, "gather")
        _small_gather(cs_ref, cs_all, s1, r1)
        pick = (lax.broadcasted_iota(jnp.int32, (N_DEV, N_DEV * 8), 1)
                == 8 * lax.broadcasted_iota(jnp.int32, (N_DEV, N_DEV * 8), 0)).astype(F32)
        per_device = jnp.dot(pick, cs_all[...], preferred_element_type=F32, precision=lax.Precision.HIGHEST)
        mod_scr[...] = jnp.dot(per_device, w_ref[...], preferred_element_type=F32,
                               precision=lax.Precision.HIGHEST) + b_ref[...]
        _small_gather(mod_scr, mod_all, s2, r2)
        _xchg_wait(big_in, big_out, sems, "gather")

    vmem = pl.BlockSpec(memory_space=pltpu.VMEM)
    hbm = pl.BlockSpec(memory_space=pl.ANY)
    dma7 = pltpu.SemaphoreType.DMA((N_DEV - 1,))
    out = pl.pallas_call(
        body, name="prologue",
        out_shape=[jax.ShapeDtypeStruct((N_DEV * 8, D), F32), jax.ShapeDtypeStruct((N_DEV * 8, ncol), F32)]
        + big_shape,
        in_specs=[vmem, vmem, vmem] + [hbm] * n, out_specs=[vmem, vmem] + [hbm] * n,
        scratch_shapes=[pltpu.VMEM((8, ncol), F32), dma7, dma7, dma7, dma7] + big_sems,
        compiler_params=pltpu.CompilerParams(vmem_limit_bytes=VMEM_LIMIT),
    )(cs, ada_w, ada_b_cols, *big)
    return out[0], out[1], out[2:]


def _allgather_small(x):
    R, C = x.shape

    def body(x_ref, out_ref, send_sems, recv_sems):
        _small_gather(x_ref, out_ref, send_sems, recv_sems)

    return pl.pallas_call(
        body, name="allgather_small_%dx%d" % (R, C),
        out_shape=jax.ShapeDtypeStruct((N_DEV * R, C), F32),
        in_specs=[pl.BlockSpec(memory_space=pltpu.VMEM)], out_specs=pl.BlockSpec(memory_space=pltpu.VMEM),
        scratch_shapes=[pltpu.SemaphoreType.DMA((N_DEV - 1,)), pltpu.SemaphoreType.DMA((N_DEV - 1,))],
    )(x)


N_CHIP = N_DEV // 2


def _xchg_copies(ins, outs, sems, mode):
    send_sems, recv_sems, local_sems = sems
    mx, my, mc = _me()
    me = 4 * mx + 2 * my + mc
    my_chip = 2 * mx + my
    sibling = _peer(1)[0]

    def rdma(a, r, dev, src, slot):
        k = a * (N_DEV - 1) + r - 1
        return pltpu.make_async_remote_copy(
            src_ref=src, dst_ref=outs[a].at[slot], send_sem=send_sems.at[k], recv_sem=recv_sems.at[k],
            device_id=dev, device_id_type=MESH)

    own, sends, relays, recvs = [], [], [], []
    for a in range(len(ins)):
        if mode == "pair":
            for chip in range(N_CHIP):
                src = ins[a].at[2 * chip + 1 - mc]
                sends.append(rdma(a, chip + 1, sibling, src, chip))
                recvs.append(rdma(a, chip + 1, sibling, src, chip))
            continue
        if mode == "quad":
            own.append(pltpu.make_async_copy(ins[a].at[my_chip], outs[a].at[my_chip], local_sems.at[a]))
            for r in (2, 4, 6):
                dev, idx = _peer(r)
                chip = idx // 2
                sends.append(rdma(a, r, dev, ins[a].at[chip], my_chip))
                recvs.append(rdma(a, r, dev, ins[a].at[chip], chip))
            continue
        gather = mode == "gather"
        own.append(pltpu.make_async_copy(ins[a] if gather else ins[a].at[me], outs[a].at[me], local_sems.at[a]))
        for r in range(1, N_DEV):
            dev, idx = _peer(r)
            if not gather:
                sends.append(rdma(a, r, dev, ins[a].at[idx], me))
                recvs.append(rdma(a, r, dev, ins[a].at[idx], idx))
            elif r == 1:
                sends.append(rdma(a, r, dev, ins[a], me))
                recvs.append(rdma(a, r, dev, ins[a], idx))
            elif r % 2 == 0:
                sends.append(rdma(a, r, dev, ins[a], me))
                relays.append((rdma(a, r, dev, ins[a], idx), rdma(a, r + 1, sibling, outs[a].at[idx], idx)))
            else:
                recvs.append(rdma(a, r, sibling, ins[a], idx))
    return own, sends, relays, recvs


def _xchg_start(ins, outs, sems, mode):
    own, sends, _, _ = _xchg_copies(ins, outs, sems, mode)
    for cp in own + sends:
        cp.start()


def _xchg_wait(ins, outs, sems, mode):
    own, sends, relays, recvs = _xchg_copies(ins, outs, sems, mode)
    for arrival, relay in relays:
        arrival.wait_recv()
        relay.start()
    for cp in recvs:
        cp.wait_recv()
    for cp in own:
        cp.wait()
    for cp in sends + [relay for _, relay in relays]:
        cp.wait_send()


def _xchg_specs(arrays, mode):
    n = len(arrays)
    shape = {"gather": lambda s: (N_DEV,) + s, "scatter": lambda s: s, "pair": lambda s: (N_CHIP,) + s[1:],
             "quad": lambda s: s}[mode]
    out_shape = [jax.ShapeDtypeStruct(shape(a.shape), a.dtype) for a in arrays]
    sems = [pltpu.SemaphoreType.DMA((n * (N_DEV - 1),)), pltpu.SemaphoreType.DMA((n * (N_DEV - 1),)),
            pltpu.SemaphoreType.DMA((n,))]
    return out_shape, sems


def _exchange(arrays, mode, name):
    n = len(arrays)

    def body(*refs):
        _xchg_start(refs[:n], refs[n:2 * n], refs[2 * n:], mode)
        _xchg_wait(refs[:n], refs[n:2 * n], refs[2 * n:], mode)

    out_shape, sems = _xchg_specs(arrays, mode)
    return pl.pallas_call(
        body, name=name, out_shape=out_shape,
        in_specs=[pl.BlockSpec(memory_space=pl.ANY)] * n, out_specs=[pl.BlockSpec(memory_space=pl.ANY)] * n,
        scratch_shapes=sems,
    )(*arrays)


def _gridded(body, carry, *, name, grid, in_specs, out_specs, out_shape, scratch_shapes=(), aliases=None):
    if carry is None:
        return pl.pallas_call(
            body, name=name, grid=grid, in_specs=list(in_specs), out_specs=list(out_specs),
            out_shape=list(out_shape), scratch_shapes=list(scratch_shapes), input_output_aliases=aliases or {},
            compiler_params=_cparams(len(grid)))
    arrays, mode = carry
    n, n_in, n_out, n_scr = len(arrays), len(in_specs), len(out_specs), len(scratch_shapes)
    c_shape, c_sems = _xchg_specs(arrays, mode)

    def wrapped(*refs):
        ins, cin = refs[:n_in], refs[n_in:n_in + n]
        o0 = n_in + n
        outs, cout = refs[o0:o0 + n_out], refs[o0 + n_out:o0 + n_out + n]
        s0 = o0 + n_out + n
        scr, sems = refs[s0:s0 + n_scr], refs[s0 + n_scr:]
        first = pl.program_id(0) == 0
        last = pl.program_id(0) == grid[0] - 1
        for ax in range(1, len(grid)):
            first = first & (pl.program_id(ax) == 0)
            last = last & (pl.program_id(ax) == grid[ax] - 1)

        @pl.when(first)
        def _():
            _xchg_start(cin, cout, sems, mode)

        body(*ins, *outs, *scr)

        @pl.when(last)
        def _():
            _xchg_wait(cin, cout, sems, mode)

    hbm = pl.BlockSpec(memory_space=pl.ANY)
    res = pl.pallas_call(
        wrapped, name=name, grid=grid, in_specs=list(in_specs) + [hbm] * n, out_specs=list(out_specs) + [hbm] * n,
        out_shape=list(out_shape) + c_shape, scratch_shapes=list(scratch_shapes) + c_sems,
        input_output_aliases=aliases or {}, compiler_params=_cparams(len(grid)),
    )
    return lambda *args: res(*args, *arrays)


def _local_step(x, target, mod, small, sh, w1):
    w1_in, w1_out = w1[0].reshape(2, D_FF, D), w1[1].reshape(D_FF, D)
    (x1, a1, b1, f1, h1), (wm_in,) = _ffn_fwd(x, mod, 0, small["norm_ffn1"], w1_in, w1_out, 0.5, "ffn1_fwd",
                                              ([sh["mix_w_in"]], "gather"))
    (p, h2), (wh_o, wc_o, wm_o, cw) = _mixin_fwd(
        x1, mod, 3, small["norm_mix"], wm_in,
        ([sh["hgrn_w_o"], sh["conv_w_o"], sh["mix_w_out"], sh["conv_w"]], "gather"))
    wh_o, wc_o, wm_o = wh_o.reshape(D, D), wc_o.reshape(D, D), wm_o.reshape(D, D)
    cw = jnp.pad(cw.transpose(1, 0, 2).reshape(CONV_K, D), ((0, HALO - CONV_K), (0, 0)))
    (o, oa, a_all, s_all), (w2_in, w2_out) = _hgrn_fwd(p, small["hgrn_lb"], small["hgrn_g"],
                                                       ([sh["ffn2_w_in"], sh["ffn2_w_out"]], "gather"))
    w2_in, w2_out = w2_in.reshape(2, D_FF, D), w2_out.reshape(D_FF, D)
    u1, u2 = _conv_fwd(p, cw, small["conv_b"], small["conv_ln_g"], small["conv_ln_b"])
    x2, ya, yb, mout = _mixout_fwd(x1, oa, u2, p, mod, 3, wh_o, wc_o, wm_o)
    (x3, a3, b3, f3, h3), _ = _ffn_fwd(x2, mod, 6, small["norm_ffn2"], w2_in, w2_out, 0.5, "ffn2_fwd", None)
    dx3, df3, sm_head = _head(x3, target, small["norm_final"], mod, 8, 0.5)

    (da3, db3, dw2_in, dw2_out), _ = _ffn_bwd_w(h3, df3, a3, b3, w2_out, "ffn2_bwd_w", None)
    (dx2, sm3), _ = _ffn_bwd_x(x2, dx3, f3, da3, db3, mod, 6, small["norm_ffn2"], w2_in, 0.5, "ffn2_bwd_x", None)
    dp, doa, du2, dwh_o, dwc_o, dwm_o, sm_mo = _mixout_bwd(dx2, oa, u2, ya, yb, mout, p, mod, 3, wh_o, wc_o, wm_o)
    dp, dcw, sm_cv = _conv_bwd(p, u1, du2, cw, small["conv_ln_g"], small["conv_ln_b"], dp)
    rows = lambda t: t.reshape(N_DEV, -1, D).astype(MM)
    (dp, sm_hg), (r2_in, r2_out) = _hgrn_bwd(p, o, a_all, s_all, doa, small["hgrn_lb"], small["hgrn_g"], dp,
                                             ([rows(dw2_in), rows(dw2_out)], "scatter"))
    (dx1, dwm_in, sm2, df1), (rh_o, rc_o, rm_o, rcw) = _mixin_bwd(
        x1, h2, dx2, dp, mod, 3, small["norm_mix"], wm_in, 2, 0.5,
        ([rows(dwh_o), rows(dwc_o), rows(dwm_o), dcw[:CONV_K].reshape(CONV_K, N_DEV, -1).transpose(1, 0, 2)],
         "scatter"))
    (da1, db1, dw1_in, dw1_out), (rm_in,) = _ffn_bwd_w(h1, df1, a1, b1, w1_out, "ffn1_bwd_w",
                                                      (_pair_reduce([dwm_in], "pair_mix"), "quad"))
    (dx0, sm1), (r1_in, r1_out) = _ffn_bwd_x(
        x, dx1, f1, da1, db1, mod, 0, small["norm_ffn1"], w1_in, 0.5, "ffn1_bwd_x",
        (_pair_reduce([rows(dw1_in), rows(dw1_out)], "pair_ffn1"), "quad"))

    dmod = jnp.concatenate([sm1[0:3], sm2[0:2], sm_mo[2:3], sm3[0:3]], axis=0)
    gsmall = dict(norm_ffn1=sm1[3:4], norm_mix=sm2[3:4], lb0=sm_hg[0:1], hgrn_g=sm_hg[1:2], conv_b=sm_cv[0:1],
                  conv_ln_g=sm_cv[1:2], conv_ln_b=sm_cv[2:3], norm_ffn2=sm3[3:4], norm_final=sm_head[0:1])
    recv = dict(ffn1_w_in=r1_in, ffn1_w_out=r1_out, mix_w_in=rm_in, hgrn_w_o=rh_o, conv_w=rcw, conv_w_o=rc_o,
                mix_w_out=rm_o, ffn2_w_in=r2_in, ffn2_w_out=r2_out)
    return sm_head[1, 0], dx0, dmod, gsmall, recv


def _pair_add(mine, theirs, core, name):
    _, R, C = theirs.shape

    def body(core_ref, a_ref, b_ref, out_ref):
        del core_ref
        out_ref[0] = (a_ref[0, 0].astype(F32) + b_ref[0].astype(F32)).astype(out_ref.dtype)

    blk = pl.BlockSpec((1, R, C), lambda s, core_ref: (s, 0, 0))
    grid_spec = pltpu.PrefetchScalarGridSpec(
        num_scalar_prefetch=1, grid=(N_CHIP,),
        in_specs=[pl.BlockSpec((1, 1, R, C), lambda s, core_ref: (s, core_ref[0], 0, 0)), blk], out_specs=blk)
    return pl.pallas_call(body, name=name, grid_spec=grid_spec,
                          out_shape=jax.ShapeDtypeStruct(theirs.shape, mine.dtype), compiler_params=_cparams(1),
                          )(core, mine.reshape(N_CHIP, 2, R, C), theirs)


def _pair_reduce(arrays, name):
    theirs = _exchange(arrays, "pair", name)
    core = lax.axis_index("c").astype(jnp.int32).reshape(1)
    return [_pair_add(a, t, core, "%s_add%d" % (name, i)) for i, (a, t) in enumerate(zip(arrays, theirs))]


SMALL_ORDER = ("norm_ffn1", "norm_mix", "lb0", "hgrn_g", "conv_b", "conv_ln_g", "conv_ln_b", "norm_ffn2",
               "norm_final")
PACK_ROWS = 24


def kernel(x, c, ada_w, ada_b, norm_ffn1, ffn1_w_in, ffn1_w_out, norm_mix, mix_w_in, hgrn_lb, hgrn_g, hgrn_w_o, conv_w, conv_b, conv_ln_g, conv_ln_b, conv_w_o, mix_w_out, norm_ffn2, ffn2_w_in, ffn2_w_out, norm_final, loss_target, m_ada_w, m_ada_b, m_norm_ffn1, m_ffn1_w_in, m_ffn1_w_out, m_norm_mix, m_mix_w_in, m_hgrn_lb, m_hgrn_g, m_hgrn_w_o, m_conv_w, m_conv_b, m_conv_ln_g, m_conv_ln_b, m_conv_w_o, m_mix_w_out, m_norm_ffn2, m_ffn2_w_in, m_ffn2_w_out, m_norm_final, v_ada_w, v_ada_b, v_norm_ffn1, v_ffn1_w_in, v_ffn1_w_out, v_norm_mix, v_mix_w_in, v_hgrn_lb, v_hgrn_g, v_hgrn_w_o, v_conv_w, v_conv_b, v_conv_ln_g, v_conv_ln_b, v_conv_w_o, v_mix_w_out, v_norm_ffn2, v_ffn2_w_in, v_ffn2_w_out, v_norm_final):
    mx, my, mc = _me()
    me = 4 * mx + 2 * my + mc
    ncol = ada_w.shape[2]

    sh = dict(ffn1_w_out=ffn1_w_out, mix_w_in=mix_w_in, hgrn_w_o=hgrn_w_o, conv_w_o=conv_w_o,
              mix_w_out=mix_w_out, ffn2_w_out=ffn2_w_out)
    sh = {n: w[0].astype(MM) for n, w in sh.items()}
    sh["ffn1_w_in"] = ffn1_w_in[0].T.astype(MM)
    sh["ffn2_w_in"] = ffn2_w_in[0].T.astype(MM)
    sh["conv_w"] = conv_w[0]
    small = dict(norm_ffn1=norm_ffn1, norm_mix=norm_mix, hgrn_lb=hgrn_lb, hgrn_g=hgrn_g, conv_b=conv_b,
                 conv_ln_g=conv_ln_g, conv_ln_b=conv_ln_b, norm_ffn2=norm_ffn2, norm_final=norm_final.reshape(1, D))

    cs = jnp.broadcast_to(c * jax.nn.sigmoid(c), (8, D))
    ada_b_cols = lax.dynamic_slice(ada_b, (0, me * ncol), (1, ncol))
    cs_all, mod_all, w1 = _prologue(cs, ada_w[0], ada_b_cols, [sh["ffn1_w_in"], sh["ffn1_w_out"]])
    cs_all = cs_all.reshape(N_DEV, 8, D)[:, 0, :]
    mod = lax.dynamic_index_in_dim(mod_all.reshape(N_DEV, N_DEV, ncol), me, axis=1, keepdims=False).reshape(9, D)

    loss_local, dx, dmod, gsmall, recv = _local_step(x[0], loss_target[0], mod, small, sh, w1)
    loss = lax.psum(loss_local, ("x", "y", "c"))

    pack = jnp.concatenate([dmod] + [gsmall[n] for n in SMALL_ORDER]
                           + [jnp.zeros((PACK_ROWS - 9 - len(SMALL_ORDER), D), F32)], axis=0)
    pack_all = _allgather_small(pack).reshape(N_DEV, PACK_ROWS, D)
    tot = _sum_slots(pack_all, "sum_small", PACK_ROWS)
    gs = {n: tot[9 + i:10 + i] for i, n in enumerate(SMALL_ORDER)}
    dmod_all = pack_all[:, 0:9, :].reshape(N_DEV, 9 * D)
    g_ada_b = tot[0:9].reshape(1, 9 * D)
    g_ada_w = _ada_wgrad(cs_all, lax.dynamic_slice(dmod_all, (0, me * ncol), (N_DEV, ncol)))
    z = hgrn_lb.astype(F32)
    p0 = jax.nn.sigmoid(z[0:1] - z[1:2])
    dz0 = p0 * (1.0 - p0) * gs["lb0"]
    g_hgrn_lb = jnp.concatenate([dz0, -dz0], axis=0)

    res = {}
    res["ada_w"] = _adamw(ada_w[0], m_ada_w[0], v_ada_w[0], g_ada_w, "adamw_ada_w")
    big = dict(ffn1_w_in=(ffn1_w_in, m_ffn1_w_in, v_ffn1_w_in), ffn1_w_out=(ffn1_w_out, m_ffn1_w_out, v_ffn1_w_out),
               mix_w_in=(mix_w_in, m_mix_w_in, v_mix_w_in), hgrn_w_o=(hgrn_w_o, m_hgrn_w_o, v_hgrn_w_o),
               conv_w=(conv_w, m_conv_w, v_conv_w), conv_w_o=(conv_w_o, m_conv_w_o, v_conv_w_o),
               mix_w_out=(mix_w_out, m_mix_w_out, v_mix_w_out), ffn2_w_in=(ffn2_w_in, m_ffn2_w_in, v_ffn2_w_in),
               ffn2_w_out=(ffn2_w_out, m_ffn2_w_out, v_ffn2_w_out))
    for n, (w, m, v) in big.items():
        g = recv[n]
        if n in ("ffn1_w_in", "ffn2_w_in"):
            g = _sum_slots(g, "sum_" + n, g.shape[1] // 4).T
        res[n] = _adamw(w[0], m[0], v[0], g, "adamw_" + n)
    sm_names = ("ada_b", "norm_ffn1", "norm_mix", "hgrn_lb", "hgrn_g", "conv_b", "conv_ln_g", "conv_ln_b",
                "norm_ffn2", "norm_final")
    sm_w = dict(ada_b=(ada_b, m_ada_b, v_ada_b), norm_ffn1=(norm_ffn1, m_norm_ffn1, v_norm_ffn1),
                norm_mix=(norm_mix, m_norm_mix, v_norm_mix), hgrn_lb=(hgrn_lb, m_hgrn_lb, v_hgrn_lb),
                hgrn_g=(hgrn_g, m_hgrn_g, v_hgrn_g), conv_b=(conv_b, m_conv_b, v_conv_b),
                conv_ln_g=(conv_ln_g, m_conv_ln_g, v_conv_ln_g), conv_ln_b=(conv_ln_b, m_conv_ln_b, v_conv_ln_b),
                norm_ffn2=(norm_ffn2, m_norm_ffn2, v_norm_ffn2), norm_final=(norm_final, m_norm_final, v_norm_final))
    sm_g = dict(gs, ada_b=g_ada_b, hgrn_lb=g_hgrn_lb)
    rows = {n: sm_w[n][0].size // D for n in sm_names}
    n_rows = sum(rows.values())
    pad = (-n_rows) % 8
    stack = lambda parts: jnp.concatenate([q.reshape(-1, D) for q in parts] + [jnp.ones((pad, D), F32)], axis=0)
    st = _adamw(stack([sm_w[n][0] for n in sm_names]), stack([sm_w[n][1] for n in sm_names]),
                stack([sm_w[n][2] for n in sm_names]), stack([sm_g[n] for n in sm_names]), "adamw_small")
    off = 0
    for n in sm_names:
        res[n] = tuple(t[off:off + rows[n]].reshape(sm_w[n][0].shape) for t in st)
        off += rows[n]

    order = ("ada_w", "ada_b", "norm_ffn1", "ffn1_w_in", "ffn1_w_out", "norm_mix", "mix_w_in", "hgrn_lb", "hgrn_g",
             "hgrn_w_o", "conv_w", "conv_b", "conv_ln_g", "conv_ln_b", "conv_w_o", "mix_w_out", "norm_ffn2",
             "ffn2_w_in", "ffn2_w_out", "norm_final")
    lead = lambda n, t: t[None] if n in big or n == "ada_w" else t
    outs = [loss, dx[None]]
    for j in range(4):
        outs += [lead(n, res[n][j]) for n in order]
    return tuple(outs)
```

```python
import jax
import jax.numpy as jnp
from jax import lax
from jax.experimental import pallas as pl
from jax.experimental.pallas import tpu as pltpu

F32 = jnp.float32
MM = jnp.bfloat16
ACT = jnp.bfloat16

D = 1024
D_FF = 2816
HEADS = 8
HD = 128
CHUNK = 64
SUB = 16
NSUB = CHUNK // SUB
HGRN_BLOCK = 1024
SAFE_EXP = 60.0
CONV_K = 31
HALO = 32
EPS = 1e-6
N_DEV = 8
NEG = -1e30
Q_SCALE = HD ** -0.5

ADAM_LR = 0.001
ADAM_B1 = 0.9
ADAM_B2 = 0.999
ADAM_EPS = 1e-08
ADAM_WD = 0.01
ADAM_STEP = 10

V7X_VMEM_BYTES = 64 * 1024 * 1024
VMEM_LIMIT = V7X_VMEM_BYTES - 4 * 1024 * 1024
MESH = pl.DeviceIdType.MESH


def _cparams(n_axes):
    return pltpu.CompilerParams(dimension_semantics=("arbitrary",) * n_axes, vmem_limit_bytes=VMEM_LIMIT)


def _mm(a, b):
    return lax.dot_general(a.astype(MM), b.astype(MM), (((1,), (0,)), ((), ())), preferred_element_type=F32)


def _mm_nt(a, b):
    return lax.dot_general(a.astype(MM), b.astype(MM), (((1,), (1,)), ((), ())), preferred_element_type=F32)


def _mm_tn(a, b):
    return lax.dot_general(a.astype(MM), b.astype(MM), (((0,), (0,)), ((), ())), preferred_element_type=F32)


def _sig(x):
    return 1.0 / (1.0 + jnp.exp(-x))


def _colsum(x):
    return jnp.sum(x, axis=0, keepdims=True)


def _rowmean(x):
    return jnp.mean(x, axis=-1, keepdims=True)


def _modnorm_fwd(xv, g, sh, sc):
    r = lax.rsqrt(_rowmean(xv * xv) + EPS)
    xh = xv * r
    n = xh * g
    return n * (1.0 + sc) + sh, xh, n, r


def _modnorm_bwd(dh, xh, n, r, g, sc):
    dsc = _colsum(dh * n)
    dsh = _colsum(dh)
    dn = dh * (1.0 + sc)
    dg = _colsum(dn * xh)
    dxh = dn * g
    dx = r * (dxh - xh * _rowmean(dxh * xh))
    return dx, dsh, dsc, dg


def _ffn_fwd(x, mod, mo, gnorm, w_in_t, w_out, res, name, carry):
    T = x.shape[0]
    tm = min(512, T)
    tn = D_FF // 2

    def body(x_ref, mod_ref, g_ref, wi_ref, wo_ref, xo_ref, a_ref, b_ref, f_ref, h_ref):
        xv = x_ref[...]
        h, _, _, _ = _modnorm_fwd(xv, g_ref[...], mod_ref[mo:mo + 1, :], mod_ref[mo + 1:mo + 2, :])
        h = h.astype(ACT)
        h_ref[...] = h
        f = None
        for c0 in range(0, D_FF, tn):
            a = _mm_nt(h, wi_ref[0, c0:c0 + tn, :])
            b = _mm_nt(h, wi_ref[1, c0:c0 + tn, :])
            a_ref[:, c0:c0 + tn] = a.astype(ACT)
            b_ref[:, c0:c0 + tn] = b.astype(ACT)
            part = _mm(a * _sig(a) * b, wo_ref[c0:c0 + tn, :])
            f = part if f is None else f + part
        f_ref[...] = f
        xo_ref[...] = xv + res * mod_ref[mo + 2:mo + 3, :] * f

    tile = pl.BlockSpec((tm, D), lambda i: (i, 0))
    wide = pl.BlockSpec((tm, D_FF), lambda i: (i, 0))
    out = _gridded(
        body, carry, name=name, grid=(T // tm,),
        in_specs=[
            tile,
            pl.BlockSpec((9, D), lambda i: (0, 0)),
            pl.BlockSpec((1, D), lambda i: (0, 0)),
            pl.BlockSpec((2, D_FF, D), lambda i: (0, 0, 0), pipeline_mode=pl.Buffered(1)),
            pl.BlockSpec((D_FF, D), lambda i: (0, 0), pipeline_mode=pl.Buffered(1)),
        ],
        out_specs=[tile, wide, wide, tile, tile],
        out_shape=[
            jax.ShapeDtypeStruct((T, D), F32),
            jax.ShapeDtypeStruct((T, D_FF), ACT),
            jax.ShapeDtypeStruct((T, D_FF), ACT),
            jax.ShapeDtypeStruct((T, D), F32),
            jax.ShapeDtypeStruct((T, D), ACT),
        ],
    )(x, mod, gnorm, w_in_t, w_out)
    return out[:5], out[5:]


def _ffn_bwd_w(h, df, a, b, w_out, name, carry):
    T = h.shape[0]
    tm = min(2048, T)
    ni = T // tm
    tn = 256
    nj = D_FF // tn

    def body(h_ref, df_ref, a_ref, b_ref, wo_ref, da_ref, db_ref, dwi_ref, dwo_ref, acc_i, acc_o):
        i = pl.program_id(1)

        @pl.when(i == 0)
        def _():
            acc_i[...] = jnp.zeros_like(acc_i)
            acc_o[...] = jnp.zeros_like(acc_o)

        hb = h_ref[...]
        df = df_ref[...]
        av = a_ref[...].astype(F32)
        bv = b_ref[...].astype(F32)
        sg = _sig(av)
        sa = av * sg
        s = (sa * bv).astype(MM)
        ds = _mm_nt(df, wo_ref[...])
        da = (ds * bv * sg * (1.0 + av * (1.0 - sg))).astype(MM)
        db = (ds * sa).astype(MM)
        da_ref[...] = da
        db_ref[...] = db
        acc_o[...] += _mm_tn(s, df)
        acc_i[0] += _mm_tn(da, hb)
        acc_i[1] += _mm_tn(db, hb)

        @pl.when(i == ni - 1)
        def _():
            dwi_ref[...] = acc_i[...].astype(MM)
            dwo_ref[...] = acc_o[...].astype(MM)

    out = _gridded(
        body, carry, name=name, grid=(nj, ni),
        in_specs=[
            pl.BlockSpec((tm, D), lambda j, i: (i, 0)),
            pl.BlockSpec((tm, D), lambda j, i: (i, 0)),
            pl.BlockSpec((tm, tn), lambda j, i: (i, j)),
            pl.BlockSpec((tm, tn), lambda j, i: (i, j)),
            pl.BlockSpec((tn, D), lambda j, i: (j, 0)),
        ],
        out_specs=[
            pl.BlockSpec((tm, tn), lambda j, i: (i, j)),
            pl.BlockSpec((tm, tn), lambda j, i: (i, j)),
            pl.BlockSpec((2, tn, D), lambda j, i: (0, j, 0)),
            pl.BlockSpec((tn, D), lambda j, i: (j, 0)),
        ],
        out_shape=[
            jax.ShapeDtypeStruct((T, D_FF), MM),
            jax.ShapeDtypeStruct((T, D_FF), MM),
            jax.ShapeDtypeStruct((2, D_FF, D), MM),
            jax.ShapeDtypeStruct((D_FF, D), MM),
        ],
        scratch_shapes=[pltpu.VMEM((2, tn, D), F32), pltpu.VMEM((tn, D), F32)],
    )(h, df, a, b, w_out)
    return out[:4], out[4:]


def _ffn_bwd_x(x, dxo, f, da, db, mod, mo, gnorm, w_in_t, res, name, carry):
    T = x.shape[0]
    tm = min(512, T)
    ni = T // tm
    tn = D_FF // 2
    nj = D_FF // tn

    def body(x_ref, dxo_ref, f_ref, da_ref, db_ref, mod_ref, g_ref, wi_ref, dx_ref, sm_ref, dh_scr):
        j = pl.program_id(0)
        i = pl.program_id(1)

        @pl.when((j == 0) & (i == 0))
        def _():
            sm_ref[...] = jnp.zeros_like(sm_ref)

        @pl.when(j == 0)
        def _():
            dh_scr[i] = jnp.zeros((tm, D), F32)

        dh_scr[i] += _mm(da_ref[...], wi_ref[0]) + _mm(db_ref[...], wi_ref[1])

        @pl.when(j == nj - 1)
        def _():
            sc = mod_ref[mo + 1:mo + 2, :]
            _, xh, n, r = _modnorm_fwd(x_ref[...], g_ref[...], mod_ref[mo:mo + 1, :], sc)
            dxn, dsh, dsc, dg = _modnorm_bwd(dh_scr[i], xh, n, r, g_ref[...], sc)
            dxo_v = dxo_ref[...]
            dx_ref[...] = dxo_v + dxn
            sm_ref[0:1, :] += dsh
            sm_ref[1:2, :] += dsc
            sm_ref[2:3, :] += _colsum(dxo_v * f_ref[...]) * res
            sm_ref[3:4, :] += dg

    last = pl.BlockSpec((tm, D), lambda j, i: (jnp.where(j == nj - 1, i, 0), 0))
    out = _gridded(
        body, carry, name=name, grid=(nj, ni),
        in_specs=[last, last, last,
                  pl.BlockSpec((tm, tn), lambda j, i: (i, j)), pl.BlockSpec((tm, tn), lambda j, i: (i, j)),
                  pl.BlockSpec((9, D), lambda j, i: (0, 0)), pl.BlockSpec((1, D), lambda j, i: (0, 0)),
                  pl.BlockSpec((2, tn, D), lambda j, i: (0, j, 0))],
        out_specs=[last, pl.BlockSpec((8, D), lambda j, i: (0, 0))],
        out_shape=[jax.ShapeDtypeStruct((T, D), F32), jax.ShapeDtypeStruct((8, D), F32)],
        scratch_shapes=[pltpu.VMEM((ni, tm, D), F32)],
    )(x, dxo, f, da, db, mod, gnorm, w_in_t)
    return out[:2], out[2:]


def _head(x, target, gfin, mod, gate_row, res):
    T = x.shape[0]
    tm = min(512, T)
    ni = T // tm

    def body(x_ref, t_ref, g_ref, mod_ref, dx_ref, df_ref, sm_ref):
        i = pl.program_id(0)

        @pl.when(i == 0)
        def _():
            sm_ref[...] = jnp.zeros_like(sm_ref)

        xv = x_ref[...]
        g = g_ref[...]
        r = lax.rsqrt(_rowmean(xv * xv) + EPS)
        xh = xv * r
        e = xh * g - t_ref[...]
        sm_ref[1:2, :] += _colsum(e * e) * (0.5 / D)
        dy = e * (1.0 / D)
        sm_ref[0:1, :] += _colsum(dy * xh)
        dxh = dy * g
        dx = r * (dxh - xh * _rowmean(dxh * xh))
        dx_ref[...] = dx
        df_ref[...] = (res * mod_ref[gate_row:gate_row + 1, :] * dx).astype(MM)

        @pl.when(i == ni - 1)
        def _():
            sm_ref[1:2, :] = jnp.broadcast_to(jnp.sum(sm_ref[1:2, :], axis=-1, keepdims=True), (1, D))

    tile = pl.BlockSpec((tm, D), lambda i: (i, 0))
    return pl.pallas_call(
        body, name="head_loss", grid=(ni,),
        in_specs=[tile, tile, pl.BlockSpec((1, D), lambda i: (0, 0)), pl.BlockSpec((9, D), lambda i: (0, 0))],
        out_specs=[tile, tile, pl.BlockSpec((8, D), lambda i: (0, 0))],
        out_shape=[jax.ShapeDtypeStruct((T, D), F32), jax.ShapeDtypeStruct((T, D), MM),
                   jax.ShapeDtypeStruct((8, D), F32)],
        compiler_params=_cparams(1),
    )(x, target, gfin, mod)


def _mixin_fwd(x, mod, mo, gnorm, w, carry):
    T = x.shape[0]
    tm = min(1024, T)
    ni = T // tm

    def body(x_ref, mod_ref, g_ref, w_ref, p_ref, h_ref, h_all):
        i = pl.program_id(1)

        @pl.when(pl.program_id(0) == 0)
        def _():
            h, _, _, _ = _modnorm_fwd(x_ref[...], g_ref[...], mod_ref[mo:mo + 1, :], mod_ref[mo + 1:mo + 2, :])
            h_all[i] = h.astype(ACT)
            h_ref[...] = h.astype(ACT)

        p_ref[0] = _mm(h_all[i], w_ref[0])

    first = lambda k, i: (jnp.where(k == 0, i, ni - 1), 0)
    out = _gridded(
        body, carry, name="mixin_fwd", grid=(8, ni),
        in_specs=[pl.BlockSpec((tm, D), first), pl.BlockSpec((9, D), lambda k, i: (0, 0)),
                  pl.BlockSpec((1, D), lambda k, i: (0, 0)), pl.BlockSpec((1, D, D), lambda k, i: (k, 0, 0))],
        out_specs=[pl.BlockSpec((1, tm, D), lambda k, i: (k, i, 0)), pl.BlockSpec((tm, D), first)],
        out_shape=[jax.ShapeDtypeStruct((8, T, D), F32), jax.ShapeDtypeStruct((T, D), ACT)],
        scratch_shapes=[pltpu.VMEM((ni, tm, D), ACT)],
    )(x, mod, gnorm, w)
    return out[:2], out[2:]


def _mixin_bwd(x, h, dxo, dp, mod, mo, gnorm, w, next_gate, next_res, carry):
    T = x.shape[0]
    tm = min(512, T)
    ni = T // tm

    def body(x_ref, h_ref, dxo_ref, dp_ref, mod_ref, g_ref, w_ref, dx_ref, dw_ref, sm_ref, df_ref, dh_scr, acc):
        k = pl.program_id(0)
        i = pl.program_id(1)

        @pl.when(i == 0)
        def _():
            acc[...] = jnp.zeros_like(acc)

        @pl.when(k == 0)
        def _():
            dh_scr[i] = jnp.zeros((tm, D), F32)

        @pl.when((k == 0) & (i == 0))
        def _():
            sm_ref[...] = jnp.zeros_like(sm_ref)

        dpk = dp_ref[0].astype(MM)
        acc[...] += _mm_tn(h_ref[...], dpk)
        dh_scr[i] += _mm_nt(dpk, w_ref[0])

        @pl.when(i == ni - 1)
        def _():
            dw_ref[0] = acc[...].astype(MM)

        @pl.when(k == 7)
        def _():
            sc = mod_ref[mo + 1:mo + 2, :]
            _, xh, n, r = _modnorm_fwd(x_ref[...], g_ref[...], mod_ref[mo:mo + 1, :], sc)
            dxn, dsh, dsc, dg = _modnorm_bwd(dh_scr[i], xh, n, r, g_ref[...], sc)
            dx = dxo_ref[...] + dxn
            dx_ref[...] = dx
            df_ref[...] = (next_res * mod_ref[next_gate:next_gate + 1, :] * dx).astype(MM)
            sm_ref[0:1, :] += dsh
            sm_ref[1:2, :] += dsc
            sm_ref[3:4, :] += dg

    last = pl.BlockSpec((tm, D), lambda k, i: (jnp.where(k == 7, i, 0), 0))
    out = _gridded(
        body, carry, name="mixin_bwd", grid=(8, ni),
        in_specs=[pl.BlockSpec((tm, D), lambda k, i: (jnp.where(k == 7, i, 0), 0)),
                  pl.BlockSpec((tm, D), lambda k, i: (i, 0)),
                  pl.BlockSpec((tm, D), lambda k, i: (jnp.where(k == 7, i, 0), 0)),
                  pl.BlockSpec((1, tm, D), lambda k, i: (k, i, 0)), pl.BlockSpec((9, D), lambda k, i: (0, 0)),
                  pl.BlockSpec((1, D), lambda k, i: (0, 0)), pl.BlockSpec((1, D, D), lambda k, i: (k, 0, 0))],
        out_specs=[last, pl.BlockSpec((1, D, D), lambda k, i: (k, 0, 0)), pl.BlockSpec((8, D), lambda k, i: (0, 0)),
                   last],
        out_shape=[jax.ShapeDtypeStruct((T, D), F32), jax.ShapeDtypeStruct((8, D, D), MM),
                   jax.ShapeDtypeStruct((8, D), F32), jax.ShapeDtypeStruct((T, D), MM)],
        scratch_shapes=[pltpu.VMEM((ni, tm, D), F32), pltpu.VMEM((D, D), F32)],
    )(x, h, dxo, dp, mod, gnorm, w)
    return out[:4], out[4:]


def _hgrn_consts():
    rows = jnp.arange(SUB * HD) // HD
    e = (rows[:, None] == jnp.arange(HD)[None, :]).astype(MM)
    return e, e.T


def _rows_bcast(ref, cb, first, n):
    parts = [jnp.broadcast_to(ref[pl.ds(c * CHUNK + first, 1), :], (n, HD)) for c in range(cb // CHUNK)]
    return jnp.concatenate(parts, axis=0)


def _hgrn_pre(qr, fr, lb_ref, b_scr, cb):
    z = lb_ref[...]
    lb = _sig(z[0:1, :] - z[1:2, :])
    sq = _sig(qr)
    q = qr * sq * Q_SCALE
    sf = _sig(fr)
    fg = lb + (1.0 - lb) * sf
    lf = jnp.log(fg)
    k = 1.0 - fg
    tl = lax.broadcasted_iota(jnp.int32, (cb, HD), 0) % CHUNK
    bc = lf
    sh = 1
    while sh < CHUNK:
        bc = bc + jnp.where(tl >= sh, pltpu.roll(bc, sh, 0), 0.0)
        sh *= 2
    b_scr[...] = bc
    bl = _rows_bcast(b_scr, cb, CHUNK - 1, CHUNK)
    eb = jnp.exp(bc)
    ekd = jnp.exp(bl - bc)
    ekf = jnp.exp(jnp.minimum(-bc, SAFE_EXP))
    return dict(lb=lb, sq=sq, q=q, sf=sf, fg=fg, k=k, tl=tl, b=bc, bl=bl, eb=eb, ekd=ekd, ekf=ekf,
                qe=q * eb, kd=k * ekd, kf=k * ekf, safe=jnp.max(-bc) < SAFE_EXP)


def _hgrn_pre_fused(p_ref, lb_ref, b_scr, q_scr, k_scr, qe_scr, kf_scr, kd_scr, cb):
    z = lb_ref[...]
    lb = _sig(z[0:1, :] - z[1:2, :])
    tl = lax.broadcasted_iota(jnp.int32, (CHUNK, HD), 0)

    def chunk(c, worst):
        r0 = pl.multiple_of(c * CHUNK, CHUNK)
        rs = pl.ds(r0, CHUNK)
        qr = p_ref[0, rs, :]
        q = qr * _sig(qr) * Q_SCALE
        fg = lb + (1.0 - lb) * _sig(p_ref[1, rs, :])
        k = 1.0 - fg
        bc = jnp.log(fg)
        sh = 1
        while sh < CHUNK:
            bc = bc + jnp.where(tl >= sh, pltpu.roll(bc, sh, 0), 0.0)
            sh *= 2
        b_scr[rs, :] = bc
        q_scr[rs, :] = q
        k_scr[rs, :] = k
        qe_scr[rs, :] = (q * jnp.exp(bc)).astype(MM)
        kf_scr[rs, :] = (k * jnp.exp(jnp.minimum(-bc, SAFE_EXP))).astype(MM)
        kd_scr[rs, :] = (k * jnp.exp(b_scr[pl.ds(r0 + CHUNK - 1, 1), :] - bc)).astype(MM)
        return jnp.maximum(worst, -bc)

    worst = lax.fori_loop(0, cb // CHUNK, chunk, jnp.zeros((CHUNK, HD), F32))
    return jnp.max(worst) < SAFE_EXP


def _hgrn_sub(pre, b_scr, cb):
    bc, tl, q, k = pre["b"], pre["tl"], pre["q"], pre["k"]
    br = [None] + [_rows_bcast(b_scr, cb, SUB * i - 1, CHUNK) for i in range(1, NSUB)]
    sb = tl // SUB
    bref = jnp.where(sb == 0, bc, jnp.where(sb == 1, br[1], jnp.where(sb == 2, br[2], br[3])))
    eqo = jnp.exp(bc - bref)
    eko = [None] + [jnp.exp(jnp.where(tl < SUB * i, br[i] - bc, NEG)) for i in range(1, NSUB)]
    return dict(eqo=eqo, eko=eko, qo=q * eqo, ko=[None] + [k * eko[i] for i in range(1, NSUB)])


def _pad_rows(x):
    return jnp.concatenate([x, jnp.zeros_like(x)], axis=0)


def _by_subblock(sbc, parts):
    out = jnp.zeros_like(parts[1])
    for i in range(1, NSUB):
        out = jnp.where(sbc == i, parts[i], out)
    return out


def _hgrn_fwd(p, hgrn_lb, hgrn_g, carry):
    T = p.shape[1]
    cb = min(HGRN_BLOCK, T)
    nch = cb // CHUNK
    ncb = T // cb
    e_mat, _ = _hgrn_consts()

    def body(p_ref, lb_ref, g_ref, e_ref, o_ref, oa_ref, a_ref, s_ref, st_scr, q_scr, k_scr, b_scr, z_scr, ad_scr,
             qe_scr, kf_scr, kd_scr):
        @pl.when(pl.program_id(1) == 0)
        def _():
            st_scr[...] = jnp.zeros_like(st_scr)

        safe = _hgrn_pre_fused(p_ref, lb_ref, b_scr, q_scr, k_scr, qe_scr, kf_scr, kd_scr, cb)
        chunks = [slice(c * CHUNK, (c + 1) * CHUNK) for c in range(nch)]
        row_i = lax.broadcasted_iota(jnp.int32, (CHUNK, HD), 0)
        lane_i = lax.broadcasted_iota(jnp.int32, (CHUNK, HD), 1)
        sbc = row_i // SUB
        causal = lane_i <= row_i

        @pl.when(safe)
        def _():
            for rs in chunks:
                ad_scr[rs, :] = jnp.where(causal, _mm_nt(qe_scr[rs, :], _pad_rows(kf_scr[rs, :])), 0.0)

        @pl.when(jnp.logical_not(safe))
        def _():
            tl = lax.broadcasted_iota(jnp.int32, (cb, HD), 0) % CHUNK
            sub = _hgrn_sub(dict(b=b_scr[...], tl=tl, q=q_scr[...], k=k_scr[...]), b_scr, cb)
            ti = lax.broadcasted_iota(jnp.int32, (SUB, HD), 0)

            def zbody(c, carry):
                for i in range(NSUB):
                    r0 = pl.multiple_of(c * CHUNK + SUB * i, SUB)
                    qi = q_scr[pl.ds(r0, SUB), :]
                    bi = b_scr[pl.ds(r0, SUB), :]
                    for s in range(SUB):
                        krow = k_scr[pl.ds(r0 + s, 1), :]
                        brow = b_scr[pl.ds(r0 + s, 1), :]
                        if s < 8:
                            zz = qi * krow * jnp.exp(jnp.where(ti >= s, bi - brow, NEG))
                        else:
                            lo = qi[8:] * krow * jnp.exp(jnp.where(ti[8:] >= s, bi[8:] - brow, NEG))
                            zz = jnp.concatenate([jnp.zeros((8, HD), F32), lo], axis=0)
                        z_scr[i, pl.ds(pl.multiple_of(c * SUB, SUB), SUB), s * HD:(s + 1) * HD] = zz.astype(MM)
                return carry

            lax.fori_loop(0, nch, zbody, 0)
            adiag = [_mm(z_scr[i], e_ref[...]) for i in range(NSUB)]
            offs = [[_mm_nt(sub["qo"][rs], _pad_rows(sub["ko"][i][rs])) for i in range(1, NSUB)] for rs in chunks]
            for c, rs in enumerate(chunks):
                dparts = []
                for i in range(NSUB):
                    blk = adiag[i][c * SUB:(c + 1) * SUB]
                    dparts.append(blk if i == 0 else pltpu.roll(blk, SUB * i, 1))
                ad_scr[rs, :] = _by_subblock(sbc, [None] + offs[c]) + jnp.concatenate(dparts, axis=0)

        kv = [_mm_tn(p_ref[2, rs, :], kd_scr[rs, :]) for rs in chunks]
        a_ref[0] = ad_scr[...]
        o_intra = [_mm(ad_scr[rs, :], _pad_rows(p_ref[2, rs, :])) for rs in chunks]
        states = []
        st = st_scr[...]
        for c in range(nch):
            states.append(st)
            st = st * jnp.exp(b_scr[pl.ds(c * CHUNK + CHUNK - 1, 1), :]) + kv[c]
        st_scr[...] = st
        g = g_ref[...]
        for c, rs in enumerate(chunks):
            s_ref[0, c] = states[c]
            o = o_intra[c] + _mm_nt(qe_scr[rs, :], states[c])
            o_ref[rs, :] = o
            og = p_ref[3, rs, :]
            oa_ref[rs, :] = (o * lax.rsqrt(_rowmean(o * o) + EPS) * g * og * _sig(og)).astype(ACT)

    out = _gridded(
        body, carry, name="hgrn_fwd", grid=(HEADS, ncb),
        in_specs=[pl.BlockSpec((4, cb, HD), lambda h, c: (0, c, h)),
                  pl.BlockSpec((2, HD), lambda h, c: (0, h)),
                  pl.BlockSpec((1, HD), lambda h, c: (0, h)),
                  pl.BlockSpec((SUB * HD, HD), lambda h, c: (0, 0))],
        out_specs=[pl.BlockSpec((cb, HD), lambda h, c: (c, h)),
                   pl.BlockSpec((cb, HD), lambda h, c: (c, h)),
                   pl.BlockSpec((1, cb, HD), lambda h, c: (h, c, 0)),
                   pl.BlockSpec((1, nch, HD, HD), lambda h, c: (h, c, 0, 0))],
        out_shape=[jax.ShapeDtypeStruct((T, D), F32), jax.ShapeDtypeStruct((T, D), ACT),
                   jax.ShapeDtypeStruct((HEADS, T, HD), F32),
                   jax.ShapeDtypeStruct((HEADS, T // CHUNK, HD, HD), F32)],
        scratch_shapes=[pltpu.VMEM((HD, HD), F32), pltpu.VMEM((cb, HD), F32), pltpu.VMEM((cb, HD), F32),
                        pltpu.VMEM((cb, HD), F32), pltpu.VMEM((NSUB, nch * SUB, SUB * HD), MM),
                        pltpu.VMEM((cb, HD), F32), pltpu.VMEM((cb, HD), MM), pltpu.VMEM((cb, HD), MM),
                        pltpu.VMEM((cb, HD), MM)],
    )(p, hgrn_lb, hgrn_g, e_mat)
    return out[:4], out[4:]


def _hgrn_bwd(p, o, a_all, s_all, doa, hgrn_lb, hgrn_g, dp, carry):
    T = p.shape[1]
    cb = min(HGRN_BLOCK, T)
    nch = cb // CHUNK
    ncb = T // cb
    _, et_mat = _hgrn_consts()

    def body(p_ref, o_ref, a_ref, s_ref, doa_ref, lb_ref, g_ref, et_ref, dp_in, dp_ref, sm_ref,
             dst_scr, q_scr, k_scr, b_scr, x_scr, dqd_scr, dkd_scr):
        del dp_in

        @pl.when(pl.program_id(1) == 0)
        def _():
            dst_scr[...] = jnp.zeros_like(dst_scr)
            sm_ref[...] = jnp.zeros_like(sm_ref)

        qr = p_ref[0]
        v = p_ref[2]
        og = p_ref[3]
        pre = _hgrn_pre(qr, p_ref[1], lb_ref, b_scr, cb)
        q, k = pre["q"], pre["k"]
        g = g_ref[...]
        ov = o_ref[...]
        r = lax.rsqrt(_rowmean(ov * ov) + EPS)
        oh = ov * r
        sgo = _sig(og)
        doa_v = doa_ref[...]
        don = doa_v * og * sgo
        dog = doa_v * oh * g * sgo * (1.0 + og * (1.0 - sgo))
        sm_ref[1:2, :] += _colsum(don * oh)
        doh = don * g
        do = r * (doh - oh * _rowmean(doh * oh))

        sbc = lax.broadcasted_iota(jnp.int32, (CHUNK, HD), 0) // SUB
        row_i = lax.broadcasted_iota(jnp.int32, (CHUNK, HD), 0)
        lane_i = lax.broadcasted_iota(jnp.int32, (CHUNK, HD), 1)
        causal = lane_i <= row_i
        chunks = [slice(c * CHUNK, (c + 1) * CHUNK) for c in range(nch)]
        da_parts = [jnp.where(causal, _mm_nt(do[rs], _pad_rows(v[rs])), 0.0) for rs in chunks]
        dv_parts = [_mm_tn(a_ref[0, rs, :], do[rs])[:CHUNK] for rs in chunks]

        @pl.when(pre["safe"])
        def _():
            hi = dict(preferred_element_type=F32, precision=lax.Precision.HIGH)
            for c, rs in enumerate(chunks):
                dqd_scr[rs, :] = pre["eb"][rs] * lax.dot_general(
                    da_parts[c], _pad_rows(pre["kf"][rs]), (((1,), (0,)), ((), ())), **hi)
                dkd_scr[rs, :] = pre["ekf"][rs] * lax.dot_general(
                    da_parts[c], pre["qe"][rs], (((0,), (0,)), ((), ())), **hi)[:CHUNK]

        @pl.when(jnp.logical_not(pre["safe"]))
        def _():
            sub = _hgrn_sub(pre, b_scr, cb)
            dqoff_mm = [[_mm(da_parts[c], _pad_rows(sub["ko"][i][rs])) for i in range(1, NSUB)]
                        for c, rs in enumerate(chunks)]
            dkoff_mm = [[_mm_tn(jnp.where(sbc == i, da_parts[c], 0.0), sub["qo"][rs])[:CHUNK]
                         for i in range(1, NSUB)] for c, rs in enumerate(chunks)]
            dqoff_parts = [_by_subblock(sbc, [None] + dqoff_mm[c]) for c in range(nch)]
            dkoff_parts = []
            for c, rs in enumerate(chunks):
                dko = sub["eko"][1][rs] * dkoff_mm[c][0]
                for i in range(2, NSUB):
                    dko = dko + sub["eko"][i][rs] * dkoff_mm[c][i - 1]
                dkoff_parts.append(dko)
            q_scr[...] = q
            k_scr[...] = k
            for i in range(NSUB):
                rows = []
                for c in range(nch):
                    blk = da_parts[c][SUB * i:SUB * (i + 1)]
                    rows.append(blk if i == 0 else pltpu.roll(blk, HD - SUB * i, 1))
                x_scr[i] = _mm(jnp.concatenate(rows, axis=0), et_ref[...])
            ti = lax.broadcasted_iota(jnp.int32, (SUB, HD), 0)

            def dbody(c, carry):
                for i in range(NSUB):
                    r0 = pl.multiple_of(c * CHUNK + SUB * i, SUB)
                    qi = q_scr[pl.ds(r0, SUB), :]
                    bi = b_scr[pl.ds(r0, SUB), :]
                    dq_hi = jnp.zeros((8, HD), F32)
                    dq_lo = jnp.zeros((8, HD), F32)
                    dk_hi = jnp.zeros((8, HD), F32)
                    dk_lo = jnp.zeros((8, HD), F32)
                    c0 = pl.multiple_of(c * SUB, SUB)
                    t8 = ti[:8]
                    for s in range(SUB):
                        krow = k_scr[pl.ds(r0 + s, 1), :]
                        brow = b_scr[pl.ds(r0 + s, 1), :]
                        w_lo = (x_scr[i, pl.ds(c0 + 8, 8), s * HD:(s + 1) * HD]
                                * jnp.exp(jnp.where(t8 + 8 >= s, bi[8:] - brow, NEG)))
                        dq_lo = dq_lo + w_lo * krow
                        col = _colsum(w_lo * qi[8:])
                        if s < 8:
                            w_hi = (x_scr[i, pl.ds(c0, 8), s * HD:(s + 1) * HD]
                                    * jnp.exp(jnp.where(t8 >= s, bi[:8] - brow, NEG)))
                            dq_hi = dq_hi + w_hi * krow
                            dk_hi = jnp.where(t8 == s, col + _colsum(w_hi * qi[:8]), dk_hi)
                        else:
                            dk_lo = jnp.where(t8 + 8 == s, col, dk_lo)
                    dqd_scr[pl.ds(r0, SUB), :] = jnp.concatenate([dq_hi, dq_lo], axis=0)
                    dkd_scr[pl.ds(r0, SUB), :] = jnp.concatenate([dk_hi, dk_lo], axis=0)
                return carry

            lax.fori_loop(0, nch, dbody, 0)
            dqd_scr[...] += jnp.concatenate(dqoff_parts, axis=0) * sub["eqo"]
            dkd_scr[...] += jnp.concatenate(dkoff_parts, axis=0)

        qdo = [_mm_tn(do[rs], pre["qe"][rs]) for rs in chunks]
        dsts = [None] * nch
        dst = dst_scr[...]
        for c in reversed(range(nch)):
            dsts[c] = dst
            dst = dst * jnp.exp(b_scr[pl.ds(c * CHUNK + CHUNK - 1, 1), :]) + qdo[c]
        dst_scr[...] = dst
        sts = [s_ref[0, c] for c in range(nch)]
        dqe_parts = [_mm(do[rs], sts[c]) for c, rs in enumerate(chunks)]
        dkdec_parts = [_mm(v[rs], dsts[c]) for c, rs in enumerate(chunks)]
        dvi_parts = [_mm_nt(pre["kd"][rs], dsts[c]) for c, rs in enumerate(chunks)]
        debl_parts = [_colsum(dsts[c] * sts[c]) for c in range(nch)]
        dqe = jnp.concatenate(dqe_parts, axis=0)
        dkdec = jnp.concatenate(dkdec_parts, axis=0)
        dq_tot = dqd_scr[...] + dqe * pre["eb"]
        dk_inter = dkdec * pre["ekd"]
        dk_tot = dkd_scr[...] + dk_inter
        db = q * dq_tot - k * dk_tot
        kdk = k * dk_inter
        dbl = jnp.concatenate(
            [jnp.broadcast_to(jnp.exp(b_scr[pl.ds(c * CHUNK + CHUNK - 1, 1), :]) * debl_parts[c]
                              + _colsum(kdk[c * CHUNK:(c + 1) * CHUNK]), (CHUNK, HD)) for c in range(nch)], axis=0)
        tl = pre["tl"]
        rc = db
        sh = 1
        while sh < CHUNK:
            rc = rc + jnp.where(tl + sh < CHUNK, pltpu.roll(rc, cb - sh, 0), 0.0)
            sh *= 2
        dlf = rc + dbl
        dfg = dlf / pre["fg"] - dk_tot
        sf = pre["sf"]
        lb = pre["lb"]
        sm_ref[0:1, :] += _colsum(dfg * (1.0 - sf))
        sq = pre["sq"]
        dp_ref[0] = (dq_tot * Q_SCALE * sq * (1.0 + qr * (1.0 - sq))).astype(ACT)
        dp_ref[1] = (dfg * (1.0 - lb) * sf * (1.0 - sf)).astype(ACT)
        dp_ref[2] = (jnp.concatenate(dv_parts, axis=0) + jnp.concatenate(dvi_parts, axis=0)).astype(ACT)
        dp_ref[3] = dog.astype(ACT)

    rev = lambda c: ncb - 1 - c
    out = _gridded(
        body, carry, name="hgrn_bwd", grid=(HEADS, ncb),
        in_specs=[pl.BlockSpec((4, cb, HD), lambda h, c: (0, rev(c), h)),
                  pl.BlockSpec((cb, HD), lambda h, c: (rev(c), h)),
                  pl.BlockSpec((1, cb, HD), lambda h, c: (h, rev(c), 0)),
                  pl.BlockSpec((1, nch, HD, HD), lambda h, c: (h, rev(c), 0, 0)),
                  pl.BlockSpec((cb, HD), lambda h, c: (rev(c), h)),
                  pl.BlockSpec((2, HD), lambda h, c: (0, h)),
                  pl.BlockSpec((1, HD), lambda h, c: (0, h)),
                  pl.BlockSpec((HD, SUB * HD), lambda h, c: (0, 0)),
                  pl.BlockSpec(memory_space=pl.ANY)],
        out_specs=[pl.BlockSpec((4, cb, HD), lambda h, c: (0, rev(c), h)),
                   pl.BlockSpec((8, HD), lambda h, c: (0, h))],
        out_shape=[jax.ShapeDtypeStruct(dp.shape, dp.dtype), jax.ShapeDtypeStruct((8, D), F32)],
        aliases={8: 0},
        scratch_shapes=[pltpu.VMEM((HD, HD), F32), pltpu.VMEM((cb, HD), F32), pltpu.VMEM((cb, HD), F32),
                        pltpu.VMEM((cb, HD), F32), pltpu.VMEM((NSUB, nch * SUB, SUB * HD), F32),
                        pltpu.VMEM((cb, HD), F32), pltpu.VMEM((cb, HD), F32)],
    )(p, o, a_all, s_all, doa, hgrn_lb, hgrn_g, et_mat, dp)
    return out[:2], out[2:]


def _ln_fwd(u1, g, b):
    mu = _rowmean(u1)
    xc = u1 - mu
    rs = lax.rsqrt(_rowmean(xc * xc) + EPS)
    xh = xc * rs
    return xh * g + b, xh, rs


CONV_RB = 64
LANES = 128


def _shift_rows(src, sh, ls, n):
    for r in range(1, 8):
        sh[r - 1, 0:n, :] = src[pl.ds(r, n), ls]


def _tap(src, sh, ls, off, r0, rows):
    r = off % 8
    if r == 0:
        return src[pl.ds(r0 + off, rows), ls]
    return sh[r - 1, pl.ds(r0 + off - r, rows), :]


def _conv_fwd(p, cw, cb_, lng, lnb, carry):
    T = p.shape[1]
    tm = min(512, T)
    n = HALO + tm - 8

    def body(p_ref, cw_ref, cb_ref, g_ref, b_ref, u1_ref, u2_ref, buf, sh):
        @pl.when(pl.program_id(0) == 0)
        def _():
            buf[0:HALO, :] = jnp.zeros((HALO, D), F32)

        buf[HALO:HALO + tm, :] = p_ref[0] * _sig(p_ref[1])
        for lb in range(D // LANES):
            ls = slice(lb * LANES, (lb + 1) * LANES)
            _shift_rows(buf, sh, ls, n)
            taps = [cw_ref[j:j + 1, ls] for j in range(CONV_K)]
            bias = cb_ref[:, ls]

            def rows_body(rb, carry):
                r0 = pl.multiple_of(rb * CONV_RB, CONV_RB)
                acc = jnp.broadcast_to(bias, (CONV_RB, LANES))
                for j in range(CONV_K):
                    acc = acc + taps[j] * _tap(buf, sh, ls, HALO - (CONV_K - 1) + j, r0, CONV_RB)
                u1_ref[pl.ds(r0, CONV_RB), ls] = acc
                return carry

            lax.fori_loop(0, tm // CONV_RB, rows_body, 0)
        y, _, _ = _ln_fwd(u1_ref[...], g_ref[...], b_ref[...])
        u2_ref[...] = (y * _sig(y)).astype(ACT)
        buf[0:HALO, :] = buf[tm:tm + HALO, :]

    out = _gridded(
        body, carry, name="conv_fwd", grid=(T // tm,),
        in_specs=[pl.BlockSpec((2, tm, D), lambda i: (2, i, 0)), pl.BlockSpec((HALO, D), lambda i: (0, 0)),
                  pl.BlockSpec((1, D), lambda i: (0, 0)), pl.BlockSpec((1, D), lambda i: (0, 0)),
                  pl.BlockSpec((1, D), lambda i: (0, 0))],
        out_specs=[pl.BlockSpec((tm, D), lambda i: (i, 0)), pl.BlockSpec((tm, D), lambda i: (i, 0))],
        out_shape=[jax.ShapeDtypeStruct((T, D), F32), jax.ShapeDtypeStruct((T, D), ACT)],
        scratch_shapes=[pltpu.VMEM((HALO + tm, D), F32), pltpu.VMEM((7, n, LANES), F32)],
    )(p, cw, cb_, lng, lnb)
    return out[:2], out[2:]


def _conv_bwd(p, u1, du2, cw, lng, lnb, dp, carry):
    T = p.shape[1]
    tm = min(512, T)
    ni = T // tm
    hb = tm // HALO

    n = HALO + tm - 8

    def body(p_ref, ph_ref, u1_ref, du2_ref, cw_ref, g_ref, b_ref, dp_in, dp_ref, dcw_ref, sm_ref, ubuf, dbuf,
             sh, dacc):
        del dp_in
        step = pl.program_id(0)

        @pl.when(step == 0)
        def _():
            dbuf[tm:tm + HALO, :] = jnp.zeros((HALO, D), F32)
            dcw_ref[...] = jnp.zeros_like(dcw_ref)
            sm_ref[...] = jnp.zeros_like(sm_ref)

        ua = p_ref[0]
        sgb = _sig(p_ref[1])
        halo = ph_ref[0] * _sig(ph_ref[1])
        ubuf[0:HALO, :] = jnp.where(step == ni - 1, 0.0, halo)
        ubuf[HALO:HALO + tm, :] = ua * sgb
        g = g_ref[...]
        y, xh, rs = _ln_fwd(u1_ref[...], g, b_ref[...])
        sy = _sig(y)
        dy = du2_ref[...] * sy * (1.0 + y * (1.0 - sy))
        sm_ref[1:2, :] += _colsum(dy * xh)
        sm_ref[2:3, :] += _colsum(dy)
        dxh = dy * g
        du1 = rs * (dxh - _rowmean(dxh) - xh * _rowmean(dxh * xh))
        sm_ref[0:1, :] += _colsum(du1)
        dbuf[0:tm, :] = du1
        for lb in range(D // LANES):
            ls = slice(lb * LANES, (lb + 1) * LANES)
            taps = [cw_ref[j:j + 1, ls] for j in range(CONV_K)]
            _shift_rows(dbuf, sh, ls, n)

            def du0_body(rb, carry):
                r0 = pl.multiple_of(rb * CONV_RB, CONV_RB)
                acc = jnp.zeros((CONV_RB, LANES), F32)
                for j in range(CONV_K):
                    acc = acc + taps[j] * _tap(dbuf, sh, ls, CONV_K - 1 - j, r0, CONV_RB)
                dp_ref[0, pl.ds(r0, CONV_RB), ls] = acc.astype(ACT)
                return carry

            lax.fori_loop(0, tm // CONV_RB, du0_body, 0)
            _shift_rows(ubuf, sh, ls, n)
            dacc[...] = jnp.zeros_like(dacc)

            def dcw_body(rb, carry):
                r0 = pl.multiple_of(rb * CONV_RB, CONV_RB)
                d = dbuf[pl.ds(r0, CONV_RB), ls]
                for j in range(CONV_K):
                    prod = d * _tap(ubuf, sh, ls, HALO - (CONV_K - 1) + j, r0, CONV_RB)
                    dacc[8 * j:8 * j + 8, :] += jnp.sum(prod.reshape(CONV_RB // 8, 8, LANES), axis=0)
                return carry

            lax.fori_loop(0, tm // CONV_RB, dcw_body, 0)
            for j in range(CONV_K):
                dcw_ref[j:j + 1, ls] += _colsum(dacc[8 * j:8 * j + 8, :])
        du0 = dp_ref[0].astype(F32)
        dp_ref[0] = (du0 * sgb).astype(ACT)
        dp_ref[1] = (du0 * ua * sgb * (1.0 - sgb)).astype(ACT)
        dbuf[tm:tm + HALO, :] = dbuf[0:HALO, :]

    rev = lambda i: ni - 1 - i
    out = _gridded(
        body, carry, name="conv_bwd", grid=(ni,),
        in_specs=[pl.BlockSpec((2, tm, D), lambda i: (2, rev(i), 0)),
                  pl.BlockSpec((2, HALO, D), lambda i: (2, jnp.maximum(rev(i) * hb - 1, 0), 0)),
                  pl.BlockSpec((tm, D), lambda i: (rev(i), 0)), pl.BlockSpec((tm, D), lambda i: (rev(i), 0)),
                  pl.BlockSpec((HALO, D), lambda i: (0, 0)), pl.BlockSpec((1, D), lambda i: (0, 0)),
                  pl.BlockSpec((1, D), lambda i: (0, 0)), pl.BlockSpec(memory_space=pl.ANY)],
        out_specs=[pl.BlockSpec((2, tm, D), lambda i: (2, rev(i), 0)),
                   pl.BlockSpec((HALO, D), lambda i: (0, 0)), pl.BlockSpec((8, D), lambda i: (0, 0))],
        out_shape=[jax.ShapeDtypeStruct(dp.shape, dp.dtype), jax.ShapeDtypeStruct((HALO, D), F32),
                   jax.ShapeDtypeStruct((8, D), F32)],
        aliases={7: 0},
        scratch_shapes=[pltpu.VMEM((HALO + tm, D), F32), pltpu.VMEM((tm + HALO, D), F32),
                        pltpu.VMEM((7, n, LANES), F32), pltpu.VMEM((8 * CONV_K, LANES), F32)],
    )(p, p, u1, du2, cw, lng, lnb, dp)
    return out[:3], out[3:]


def _mixout_fwd(x, oa, u2, p, mod, mo, w_a, w_b, w_o):
    T = x.shape[0]
    tm = min(512, T)

    def body(x_ref, oa_ref, u2_ref, p_ref, mod_ref, wa_ref, wb_ref, wo_ref, xo_ref, ya_ref, yb_ref, mo_ref):
        ya = _mm(oa_ref[...], wa_ref[...])
        yb = _mm(u2_ref[...], wb_ref[...])
        ya_ref[...] = ya.astype(ACT)
        yb_ref[...] = yb.astype(ACT)
        merged = _sig(p_ref[0]) * ya + _sig(p_ref[1]) * yb
        out = _mm(merged, wo_ref[...])
        mo_ref[...] = out
        xo_ref[...] = x_ref[...] + mod_ref[mo + 2:mo + 3, :] * out

    tile = pl.BlockSpec((tm, D), lambda i: (i, 0))
    wspec = pl.BlockSpec((D, D), lambda i: (0, 0))
    return pl.pallas_call(
        body, name="mixout_fwd", grid=(T // tm,),
        in_specs=[tile, tile, tile, pl.BlockSpec((2, tm, D), lambda i: (3, i, 0)),
                  pl.BlockSpec((9, D), lambda i: (0, 0)), wspec, wspec, wspec],
        out_specs=[tile, tile, tile, tile],
        out_shape=[jax.ShapeDtypeStruct((T, D), F32), jax.ShapeDtypeStruct((T, D), ACT),
                   jax.ShapeDtypeStruct((T, D), ACT), jax.ShapeDtypeStruct((T, D), F32)],
        compiler_params=_cparams(1),
    )(x, oa, u2, p, mod, w_a, w_b, w_o)


def _mixout_bwd(dxo, oa, u2, ya, yb, mout, p, mod, mo, w_a, w_b, w_o):
    T = dxo.shape[0]
    tm = min(256, T)

    def body(dxo_ref, oa_ref, u2_ref, ya_ref, yb_ref, mo_ref, p_ref, mod_ref, wa_ref, wb_ref, wo_ref,
             dp_ref, doa_ref, du2_ref, dwa_ref, dwb_ref, dwo_ref, sm_ref):
        @pl.when(pl.program_id(0) == 0)
        def _():
            dwa_ref[...] = jnp.zeros_like(dwa_ref)
            dwb_ref[...] = jnp.zeros_like(dwb_ref)
            dwo_ref[...] = jnp.zeros_like(dwo_ref)
            sm_ref[...] = jnp.zeros_like(sm_ref)

        dxo_v = dxo_ref[...]
        sm_ref[2:3, :] += _colsum(dxo_v * mo_ref[...])
        dmo = (mod_ref[mo + 2:mo + 3, :] * dxo_v).astype(MM)
        ya = ya_ref[...].astype(F32)
        yb = yb_ref[...].astype(F32)
        sga = _sig(p_ref[0])
        sgb = _sig(p_ref[1])
        merged = (sga * ya + sgb * yb).astype(MM)
        dwo_ref[...] += _mm_tn(merged, dmo)
        dmg = _mm_nt(dmo, wo_ref[...])
        dp_ref[0] = (dmg * ya * sga * (1.0 - sga)).astype(ACT)
        dp_ref[1] = (dmg * yb * sgb * (1.0 - sgb)).astype(ACT)
        dya = (dmg * sga).astype(MM)
        dyb = (dmg * sgb).astype(MM)
        dwa_ref[...] += _mm_tn(oa_ref[...], dya)
        dwb_ref[...] += _mm_tn(u2_ref[...], dyb)
        doa_ref[...] = _mm_nt(dya, wa_ref[...])
        du2_ref[...] = _mm_nt(dyb, wb_ref[...])

    tile = pl.BlockSpec((tm, D), lambda i: (i, 0))
    wspec = pl.BlockSpec((D, D), lambda i: (0, 0))
    return pl.pallas_call(
        body, name="mixout_bwd", grid=(T // tm,),
        in_specs=[tile, tile, tile, tile, tile, tile, pl.BlockSpec((2, tm, D), lambda i: (3, i, 0)),
                  pl.BlockSpec((9, D), lambda i: (0, 0)), wspec, wspec, wspec],
        out_specs=[pl.BlockSpec((2, tm, D), lambda i: (3, i, 0)), tile, tile, wspec, wspec, wspec,
                   pl.BlockSpec((8, D), lambda i: (0, 0))],
        out_shape=[jax.ShapeDtypeStruct((8, T, D), ACT), jax.ShapeDtypeStruct((T, D), F32),
                   jax.ShapeDtypeStruct((T, D), F32), jax.ShapeDtypeStruct((D, D), F32),
                   jax.ShapeDtypeStruct((D, D), F32), jax.ShapeDtypeStruct((D, D), F32),
                   jax.ShapeDtypeStruct((8, D), F32)],
        compiler_params=_cparams(1),
    )(dxo, oa, u2, ya, yb, mout, p, mod, w_a, w_b, w_o)


def _ada_wgrad(cs_all, dmod_cols):
    cs_t = jnp.pad(cs_all.T, ((0, 0), (0, HD - N_DEV)))
    dm = jnp.pad(dmod_cols, ((0, HD - N_DEV), (0, 0)))

    def body(cs_ref, d_ref, out_ref):
        out_ref[...] = jnp.dot(cs_ref[...], d_ref[...], preferred_element_type=F32,
                               precision=lax.Precision.HIGHEST)

    return pl.pallas_call(
        body, name="ada_wgrad", out_shape=jax.ShapeDtypeStruct((D, dmod_cols.shape[1]), F32),
        compiler_params=pltpu.CompilerParams(vmem_limit_bytes=VMEM_LIMIT),
    )(cs_t, dm)


def _adam_math(w, g, m, v):
    m2 = ADAM_B1 * m + (1.0 - ADAM_B1) * g
    v2 = ADAM_B2 * v + (1.0 - ADAM_B2) * (g * g)
    m_hat = m2 / (1.0 - ADAM_B1 ** ADAM_STEP)
    v_hat = v2 / (1.0 - ADAM_B2 ** ADAM_STEP)
    delta = -ADAM_LR * (m_hat / (jnp.sqrt(v_hat) + ADAM_EPS) + ADAM_WD * w)
    return delta, m2, v2


def _adamw(w, m, v, g, name):
    R, C = w.shape
    slots = g.ndim == 3
    n_slots = g.shape[0] if slots else 0
    tr = R
    for cand in (256, 176):
        if R % cand == 0 and R > cand:
            tr = cand
            break

    def body(w_ref, m_ref, v_ref, g_ref, go_ref, d_ref, mo_ref, vo_ref):
        if slots:
            gv = g_ref[0].astype(F32)
            for s in range(1, n_slots):
                gv = gv + g_ref[s].astype(F32)
        else:
            gv = g_ref[...]
        go_ref[...] = gv
        d_ref[...], mo_ref[...], vo_ref[...] = _adam_math(w_ref[...], gv, m_ref[...], v_ref[...])

    tile = pl.BlockSpec((tr, C), lambda i: (i, 0))
    gspec = pl.BlockSpec((n_slots, tr, C), lambda i: (0, i, 0)) if slots else tile
    sds = jax.ShapeDtypeStruct((R, C), F32)
    return pl.pallas_call(
        body, name=name, grid=(R // tr,), in_specs=[tile, tile, tile, gspec], out_specs=[tile] * 4,
        out_shape=[sds] * 4, compiler_params=_cparams(1),
    )(w, m, v, g)


def _sum_slots(pack, name, tr):
    n, R, C = pack.shape

    def body(p_ref, out_ref):
        acc = p_ref[0].astype(F32)
        for s in range(1, n):
            acc = acc + p_ref[s].astype(F32)
        out_ref[...] = acc

    return pl.pallas_call(
        body, name=name, grid=(R // tr,), in_specs=[pl.BlockSpec((n, tr, C), lambda i: (0, i, 0))],
        out_specs=pl.BlockSpec((tr, C), lambda i: (i, 0)), out_shape=jax.ShapeDtypeStruct((R, C), F32),
        compiler_params=_cparams(1))(pack)


def _me():
    return lax.axis_index("x"), lax.axis_index("y"), lax.axis_index("c")


def _peer(r):
    x, y, c = _me()
    px = 1 - x if r & 4 else x
    py = 1 - y if r & 2 else y
    pc = 1 - c if r & 1 else c
    return (px, py, pc), 4 * px + 2 * py + pc


def _small_gather(x_ref, out_ref, send_sems, recv_sems):
    R = x_ref.shape[0]
    mx, my, mc = _me()
    me = 4 * mx + 2 * my + mc
    mine = out_ref.at[pl.ds(pl.multiple_of(me * R, 8), R), :]
    copies = []
    for r in range(1, N_DEV):
        dev, _ = _peer(r)
        copies.append(pltpu.make_async_remote_copy(
            src_ref=x_ref, dst_ref=mine, send_sem=send_sems.at[r - 1], recv_sem=recv_sems.at[r - 1],
            device_id=dev, device_id_type=MESH))
    for cp in copies:
        cp.start()
    mine[...] = x_ref[...]
    for r in range(1, N_DEV):
        dev, idx = _peer(r)
        theirs = out_ref.at[pl.ds(pl.multiple_of(idx * R, 8), R), :]
        pltpu.make_async_remote_copy(
            src_ref=x_ref, dst_ref=theirs, send_sem=send_sems.at[r - 1], recv_sem=recv_sems.at[r - 1],
            device_id=dev, device_id_type=MESH).wait_recv()
    for cp in copies:
        cp.wait_send()


def _prologue(cs, ada_w, ada_b_cols, big):
    n = len(big)
    ncol = ada_w.shape[1]
    big_shape, big_sems = _xchg_specs(big, "gather")

    def body(cs_ref, w_ref, b_ref, *rest):
        big_in, cs_all, mod_all, big_out = rest[:n], rest[n], rest[n + 1], rest[n + 2:2 * n + 2]
        mod_scr, s1, r1, s2, r2 = rest[2 * n + 2:2 * n + 7]
        sems = rest[2 * n + 7:]
        _xchg_start(big_in, big_out, sems, "gather")
        _small_gather(cs_ref, cs_all, s1, r1)
        pick = (lax.broadcasted_iota(jnp.int32, (N_DEV, N_DEV * 8), 1)
                == 8 * lax.broadcasted_iota(jnp.int32, (N_DEV, N_DEV * 8), 0)).astype(F32)
        per_device = jnp.dot(pick, cs_all[...], preferred_element_type=F32, precision=lax.Precision.HIGHEST)
        mod_scr[...] = jnp.dot(per_device, w_ref[...], preferred_element_type=F32,
                               precision=lax.Precision.HIGHEST) + b_ref[...]
        _small_gather(mod_scr, mod_all, s2, r2)
        _xchg_wait(big_in, big_out, sems, "gather")

    vmem = pl.BlockSpec(memory_space=pltpu.VMEM)
    hbm = pl.BlockSpec(memory_space=pl.ANY)
    dma7 = pltpu.SemaphoreType.DMA((N_DEV - 1,))
    out = pl.pallas_call(
        body, name="prologue",
        out_shape=[jax.ShapeDtypeStruct((N_DEV * 8, D), F32), jax.ShapeDtypeStruct((N_DEV * 8, ncol), F32)]
        + big_shape,
        in_specs=[vmem, vmem, vmem] + [hbm] * n, out_specs=[vmem, vmem] + [hbm] * n,
        scratch_shapes=[pltpu.VMEM((8, ncol), F32), dma7, dma7, dma7, dma7] + big_sems,
        compiler_params=pltpu.CompilerParams(vmem_limit_bytes=VMEM_LIMIT),
    )(cs, ada_w, ada_b_cols, *big)
    return out[0], out[1], out[2:]


def _allgather_small(x):
    R, C = x.shape

    def body(x_ref, out_ref, send_sems, recv_sems):
        _small_gather(x_ref, out_ref, send_sems, recv_sems)

    return pl.pallas_call(
        body, name="allgather_small_%dx%d" % (R, C),
        out_shape=jax.ShapeDtypeStruct((N_DEV * R, C), F32),
        in_specs=[pl.BlockSpec(memory_space=pltpu.VMEM)], out_specs=pl.BlockSpec(memory_space=pltpu.VMEM),
        scratch_shapes=[pltpu.SemaphoreType.DMA((N_DEV - 1,)), pltpu.SemaphoreType.DMA((N_DEV - 1,))],
    )(x)


N_CHIP = N_DEV // 2


def _xchg_copies(ins, outs, sems, mode):
    send_sems, recv_sems, local_sems = sems
    mx, my, mc = _me()
    me = 4 * mx + 2 * my + mc
    my_chip = 2 * mx + my
    sibling = _peer(1)[0]

    def rdma(a, r, dev, src, slot):
        k = a * (N_DEV - 1) + r - 1
        return pltpu.make_async_remote_copy(
            src_ref=src, dst_ref=outs[a].at[slot], send_sem=send_sems.at[k], recv_sem=recv_sems.at[k],
            device_id=dev, device_id_type=MESH)

    own, sends, relays, recvs = [], [], [], []
    for a in range(len(ins)):
        if mode == "pair":
            for chip in range(N_CHIP):
                src = ins[a].at[2 * chip + 1 - mc]
                sends.append(rdma(a, chip + 1, sibling, src, chip))
                recvs.append(rdma(a, chip + 1, sibling, src, chip))
            continue
        if mode == "quad":
            own.append(pltpu.make_async_copy(ins[a].at[my_chip], outs[a].at[my_chip], local_sems.at[a]))
            for r in (2, 4, 6):
                dev, idx = _peer(r)
                chip = idx // 2
                sends.append(rdma(a, r, dev, ins[a].at[chip], my_chip))
                recvs.append(rdma(a, r, dev, ins[a].at[chip], chip))
            continue
        gather = mode == "gather"
        own.append(pltpu.make_async_copy(ins[a] if gather else ins[a].at[me], outs[a].at[me], local_sems.at[a]))
        for r in range(1, N_DEV):
            dev, idx = _peer(r)
            if not gather:
                sends.append(rdma(a, r, dev, ins[a].at[idx], me))
                recvs.append(rdma(a, r, dev, ins[a].at[idx], idx))
            elif r == 1:
                sends.append(rdma(a, r, dev, ins[a], me))
                recvs.append(rdma(a, r, dev, ins[a], idx))
            elif r % 2 == 0:
                sends.append(rdma(a, r, dev, ins[a], me))
                relays.append((rdma(a, r, dev, ins[a], idx), rdma(a, r + 1, sibling, outs[a].at[idx], idx)))
            else:
                recvs.append(rdma(a, r, sibling, ins[a], idx))
    return own, sends, relays, recvs


def _xchg_start(ins, outs, sems, mode):
    own, sends, _, _ = _xchg_copies(ins, outs, sems, mode)
    for cp in own + sends:
        cp.start()


def _xchg_wait(ins, outs, sems, mode):
    own, sends, relays, recvs = _xchg_copies(ins, outs, sems, mode)
    for arrival, relay in relays:
        arrival.wait_recv()
        relay.start()
    for cp in recvs:
        cp.wait_recv()
    for cp in own:
        cp.wait()
    for cp in sends + [relay for _, relay in relays]:
        cp.wait_send()


def _xchg_specs(arrays, mode):
    n = len(arrays)
    shape = {"gather": lambda s: (N_DEV,) + s, "scatter": lambda s: s, "pair": lambda s: (N_CHIP,) + s[1:],
             "quad": lambda s: s}[mode]
    out_shape = [jax.ShapeDtypeStruct(shape(a.shape), a.dtype) for a in arrays]
    sems = [pltpu.SemaphoreType.DMA((n * (N_DEV - 1),)), pltpu.SemaphoreType.DMA((n * (N_DEV - 1),)),
            pltpu.SemaphoreType.DMA((n,))]
    return out_shape, sems


def _exchange(arrays, mode, name):
    n = len(arrays)

    def body(*refs):
        _xchg_start(refs[:n], refs[n:2 * n], refs[2 * n:], mode)
        _xchg_wait(refs[:n], refs[n:2 * n], refs[2 * n:], mode)

    out_shape, sems = _xchg_specs(arrays, mode)
    return pl.pallas_call(
        body, name=name, out_shape=out_shape,
        in_specs=[pl.BlockSpec(memory_space=pl.ANY)] * n, out_specs=[pl.BlockSpec(memory_space=pl.ANY)] * n,
        scratch_shapes=sems,
    )(*arrays)


def _gridded(body, carry, *, name, grid, in_specs, out_specs, out_shape, scratch_shapes=(), aliases=None):
    if carry is None:
        return pl.pallas_call(
            body, name=name, grid=grid, in_specs=list(in_specs), out_specs=list(out_specs),
            out_shape=list(out_shape), scratch_shapes=list(scratch_shapes), input_output_aliases=aliases or {},
            compiler_params=_cparams(len(grid)))
    arrays, mode = carry
    n, n_in, n_out, n_scr = len(arrays), len(in_specs), len(out_specs), len(scratch_shapes)
    c_shape, c_sems = _xchg_specs(arrays, mode)

    def wrapped(*refs):
        ins, cin = refs[:n_in], refs[n_in:n_in + n]
        o0 = n_in + n
        outs, cout = refs[o0:o0 + n_out], refs[o0 + n_out:o0 + n_out + n]
        s0 = o0 + n_out + n
        scr, sems = refs[s0:s0 + n_scr], refs[s0 + n_scr:]
        first = pl.program_id(0) == 0
        last = pl.program_id(0) == grid[0] - 1
        for ax in range(1, len(grid)):
            first = first & (pl.program_id(ax) == 0)
            last = last & (pl.program_id(ax) == grid[ax] - 1)

        @pl.when(first)
        def _():
            _xchg_start(cin, cout, sems, mode)

        body(*ins, *outs, *scr)

        @pl.when(last)
        def _():
            _xchg_wait(cin, cout, sems, mode)

    hbm = pl.BlockSpec(memory_space=pl.ANY)
    res = pl.pallas_call(
        wrapped, name=name, grid=grid, in_specs=list(in_specs) + [hbm] * n, out_specs=list(out_specs) + [hbm] * n,
        out_shape=list(out_shape) + c_shape, scratch_shapes=list(scratch_shapes) + c_sems,
        input_output_aliases=aliases or {}, compiler_params=_cparams(len(grid)),
    )
    return lambda *args: res(*args, *arrays)


def _local_step(x, target, mod, small, sh, w1):
    w1_in, w1_out = w1[0].reshape(2, D_FF, D), w1[1].reshape(D_FF, D)
    (x1, a1, b1, f1, h1), (wm_in,) = _ffn_fwd(x, mod, 0, small["norm_ffn1"], w1_in, w1_out, 0.5, "ffn1_fwd",
                                              ([sh["mix_w_in"]], "gather"))
    (p, h2), (wh_o, wc_o, wm_o, cw) = _mixin_fwd(
        x1, mod, 3, small["norm_mix"], wm_in,
        ([sh["hgrn_w_o"], sh["conv_w_o"], sh["mix_w_out"], sh["conv_w"]], "gather"))
    wh_o, wc_o, wm_o = wh_o.reshape(D, D), wc_o.reshape(D, D), wm_o.reshape(D, D)
    cw = jnp.pad(cw.transpose(1, 0, 2).reshape(CONV_K, D), ((0, HALO - CONV_K), (0, 0)))
    (o, oa, a_all, s_all), (w2_in,) = _hgrn_fwd(p, small["hgrn_lb"], small["hgrn_g"], ([sh["ffn2_w_in"]], "gather"))
    (u1, u2), (w2_out,) = _conv_fwd(p, cw, small["conv_b"], small["conv_ln_g"], small["conv_ln_b"],
                                    ([sh["ffn2_w_out"]], "gather"))
    w2_in, w2_out = w2_in.reshape(2, D_FF, D), w2_out.reshape(D_FF, D)
    x2, ya, yb, mout = _mixout_fwd(x1, oa, u2, p, mod, 3, wh_o, wc_o, wm_o)
    (x3, a3, b3, f3, h3), _ = _ffn_fwd(x2, mod, 6, small["norm_ffn2"], w2_in, w2_out, 0.5, "ffn2_fwd", None)
    dx3, df3, sm_head = _head(x3, target, small["norm_final"], mod, 8, 0.5)

    (da3, db3, dw2_in, dw2_out), _ = _ffn_bwd_w(h3, df3, a3, b3, w2_out, "ffn2_bwd_w", None)
    rows = lambda t: t.reshape(N_DEV, -1, D).astype(MM)
    (dx2, sm3), (r2_out,) = _ffn_bwd_x(x2, dx3, f3, da3, db3, mod, 6, small["norm_ffn2"], w2_in, 0.5, "ffn2_bwd_x",
                                       ([rows(dw2_out)], "scatter"))
    dp, doa, du2, dwh_o, dwc_o, dwm_o, sm_mo = _mixout_bwd(dx2, oa, u2, ya, yb, mout, p, mod, 3, wh_o, wc_o, wm_o)
    (dp, dcw, sm_cv), (r2_in,) = _conv_bwd(p, u1, du2, cw, small["conv_ln_g"], small["conv_ln_b"], dp,
                                           ([rows(dw2_in)], "scatter"))
    (dp, sm_hg), _ = _hgrn_bwd(p, o, a_all, s_all, doa, small["hgrn_lb"], small["hgrn_g"], dp, None)
    (dx1, dwm_in, sm2, df1), (rh_o, rc_o, rm_o, rcw) = _mixin_bwd(
        x1, h2, dx2, dp, mod, 3, small["norm_mix"], wm_in, 2, 0.5,
        ([rows(dwh_o), rows(dwc_o), rows(dwm_o), dcw[:CONV_K].reshape(CONV_K, N_DEV, -1).transpose(1, 0, 2)],
         "scatter"))
    (da1, db1, dw1_in, dw1_out), (rm_in,) = _ffn_bwd_w(h1, df1, a1, b1, w1_out, "ffn1_bwd_w",
                                                      (_pair_reduce([dwm_in], "pair_mix"), "quad"))
    (dx0, sm1), (r1_in, r1_out) = _ffn_bwd_x(
        x, dx1, f1, da1, db1, mod, 0, small["norm_ffn1"], w1_in, 0.5, "ffn1_bwd_x",
        (_pair_reduce([rows(dw1_in), rows(dw1_out)], "pair_ffn1"), "quad"))

    dmod = jnp.concatenate([sm1[0:3], sm2[0:2], sm_mo[2:3], sm3[0:3]], axis=0)
    gsmall = dict(norm_ffn1=sm1[3:4], norm_mix=sm2[3:4], lb0=sm_hg[0:1], hgrn_g=sm_hg[1:2], conv_b=sm_cv[0:1],
                  conv_ln_g=sm_cv[1:2], conv_ln_b=sm_cv[2:3], norm_ffn2=sm3[3:4], norm_final=sm_head[0:1])
    recv = dict(ffn1_w_in=r1_in, ffn1_w_out=r1_out, mix_w_in=rm_in, hgrn_w_o=rh_o, conv_w=rcw, conv_w_o=rc_o,
                mix_w_out=rm_o, ffn2_w_in=r2_in, ffn2_w_out=r2_out)
    return sm_head[1, 0], dx0, dmod, gsmall, recv


def _pair_add(mine, theirs, core, name):
    _, R, C = theirs.shape

    def body(core_ref, a_ref, b_ref, out_ref):
        del core_ref
        out_ref[0] = (a_ref[0, 0].astype(F32) + b_ref[0].astype(F32)).astype(out_ref.dtype)

    blk = pl.BlockSpec((1, R, C), lambda s, core_ref: (s, 0, 0))
    grid_spec = pltpu.PrefetchScalarGridSpec(
        num_scalar_prefetch=1, grid=(N_CHIP,),
        in_specs=[pl.BlockSpec((1, 1, R, C), lambda s, core_ref: (s, core_ref[0], 0, 0)), blk], out_specs=blk)
    return pl.pallas_call(body, name=name, grid_spec=grid_spec,
                          out_shape=jax.ShapeDtypeStruct(theirs.shape, mine.dtype), compiler_params=_cparams(1),
                          )(core, mine.reshape(N_CHIP, 2, R, C), theirs)


def _pair_reduce(arrays, name):
    theirs = _exchange(arrays, "pair", name)
    core = lax.axis_index("c").astype(jnp.int32).reshape(1)
    return [_pair_add(a, t, core, "%s_add%d" % (name, i)) for i, (a, t) in enumerate(zip(arrays, theirs))]


SMALL_ORDER = ("norm_ffn1", "norm_mix", "lb0", "hgrn_g", "conv_b", "conv_ln_g", "conv_ln_b", "norm_ffn2",
               "norm_final")
PACK_ROWS = 24


def kernel(x, c, ada_w, ada_b, norm_ffn1, ffn1_w_in, ffn1_w_out, norm_mix, mix_w_in, hgrn_lb, hgrn_g, hgrn_w_o, conv_w, conv_b, conv_ln_g, conv_ln_b, conv_w_o, mix_w_out, norm_ffn2, ffn2_w_in, ffn2_w_out, norm_final, loss_target, m_ada_w, m_ada_b, m_norm_ffn1, m_ffn1_w_in, m_ffn1_w_out, m_norm_mix, m_mix_w_in, m_hgrn_lb, m_hgrn_g, m_hgrn_w_o, m_conv_w, m_conv_b, m_conv_ln_g, m_conv_ln_b, m_conv_w_o, m_mix_w_out, m_norm_ffn2, m_ffn2_w_in, m_ffn2_w_out, m_norm_final, v_ada_w, v_ada_b, v_norm_ffn1, v_ffn1_w_in, v_ffn1_w_out, v_norm_mix, v_mix_w_in, v_hgrn_lb, v_hgrn_g, v_hgrn_w_o, v_conv_w, v_conv_b, v_conv_ln_g, v_conv_ln_b, v_conv_w_o, v_mix_w_out, v_norm_ffn2, v_ffn2_w_in, v_ffn2_w_out, v_norm_final):
    mx, my, mc = _me()
    me = 4 * mx + 2 * my + mc
    ncol = ada_w.shape[2]

    sh = dict(ffn1_w_out=ffn1_w_out, mix_w_in=mix_w_in, hgrn_w_o=hgrn_w_o, conv_w_o=conv_w_o,
              mix_w_out=mix_w_out, ffn2_w_out=ffn2_w_out)
    sh = {n: w[0].astype(MM) for n, w in sh.items()}
    sh["ffn1_w_in"] = ffn1_w_in[0].T.astype(MM)
    sh["ffn2_w_in"] = ffn2_w_in[0].T.astype(MM)
    sh["conv_w"] = conv_w[0]
    small = dict(norm_ffn1=norm_ffn1, norm_mix=norm_mix, hgrn_lb=hgrn_lb, hgrn_g=hgrn_g, conv_b=conv_b,
                 conv_ln_g=conv_ln_g, conv_ln_b=conv_ln_b, norm_ffn2=norm_ffn2, norm_final=norm_final.reshape(1, D))

    cs = jnp.broadcast_to(c * jax.nn.sigmoid(c), (8, D))
    ada_b_cols = lax.dynamic_slice(ada_b, (0, me * ncol), (1, ncol))
    cs_all, mod_all, w1 = _prologue(cs, ada_w[0], ada_b_cols, [sh["ffn1_w_in"], sh["ffn1_w_out"]])
    cs_all = cs_all.reshape(N_DEV, 8, D)[:, 0, :]
    mod = lax.dynamic_index_in_dim(mod_all.reshape(N_DEV, N_DEV, ncol), me, axis=1, keepdims=False).reshape(9, D)

    loss_local, dx, dmod, gsmall, recv = _local_step(x[0], loss_target[0], mod, small, sh, w1)
    loss = lax.psum(loss_local, ("x", "y", "c"))

    pack = jnp.concatenate([dmod] + [gsmall[n] for n in SMALL_ORDER]
                           + [jnp.zeros((PACK_ROWS - 9 - len(SMALL_ORDER), D), F32)], axis=0)
    pack_all = _allgather_small(pack).reshape(N_DEV, PACK_ROWS, D)
    tot = _sum_slots(pack_all, "sum_small", PACK_ROWS)
    gs = {n: tot[9 + i:10 + i] for i, n in enumerate(SMALL_ORDER)}
    dmod_all = pack_all[:, 0:9, :].reshape(N_DEV, 9 * D)
    g_ada_b = tot[0:9].reshape(1, 9 * D)
    g_ada_w = _ada_wgrad(cs_all, lax.dynamic_slice(dmod_all, (0, me * ncol), (N_DEV, ncol)))
    z = hgrn_lb.astype(F32)
    p0 = jax.nn.sigmoid(z[0:1] - z[1:2])
    dz0 = p0 * (1.0 - p0) * gs["lb0"]
    g_hgrn_lb = jnp.concatenate([dz0, -dz0], axis=0)

    res = {}
    res["ada_w"] = _adamw(ada_w[0], m_ada_w[0], v_ada_w[0], g_ada_w, "adamw_ada_w")
    big = dict(ffn1_w_in=(ffn1_w_in, m_ffn1_w_in, v_ffn1_w_in), ffn1_w_out=(ffn1_w_out, m_ffn1_w_out, v_ffn1_w_out),
               mix_w_in=(mix_w_in, m_mix_w_in, v_mix_w_in), hgrn_w_o=(hgrn_w_o, m_hgrn_w_o, v_hgrn_w_o),
               conv_w=(conv_w, m_conv_w, v_conv_w), conv_w_o=(conv_w_o, m_conv_w_o, v_conv_w_o),
               mix_w_out=(mix_w_out, m_mix_w_out, v_mix_w_out), ffn2_w_in=(ffn2_w_in, m_ffn2_w_in, v_ffn2_w_in),
               ffn2_w_out=(ffn2_w_out, m_ffn2_w_out, v_ffn2_w_out))
    for n, (w, m, v) in big.items():
        g = recv[n]
        if n in ("ffn1_w_in", "ffn2_w_in"):
            g = _sum_slots(g, "sum_" + n, g.shape[1] // 4).T
        res[n] = _adamw(w[0], m[0], v[0], g, "adamw_" + n)
    sm_names = ("ada_b", "norm_ffn1", "norm_mix", "hgrn_lb", "hgrn_g", "conv_b", "conv_ln_g", "conv_ln_b",
                "norm_ffn2", "norm_final")
    sm_w = dict(ada_b=(ada_b, m_ada_b, v_ada_b), norm_ffn1=(norm_ffn1, m_norm_ffn1, v_norm_ffn1),
                norm_mix=(norm_mix, m_norm_mix, v_norm_mix), hgrn_lb=(hgrn_lb, m_hgrn_lb, v_hgrn_lb),
                hgrn_g=(hgrn_g, m_hgrn_g, v_hgrn_g), conv_b=(conv_b, m_conv_b, v_conv_b),
                conv_ln_g=(conv_ln_g, m_conv_ln_g, v_conv_ln_g), conv_ln_b=(conv_ln_b, m_conv_ln_b, v_conv_ln_b),
                norm_ffn2=(norm_ffn2, m_norm_ffn2, v_norm_ffn2), norm_final=(norm_final, m_norm_final, v_norm_final))
    sm_g = dict(gs, ada_b=g_ada_b, hgrn_lb=g_hgrn_lb)
    rows = {n: sm_w[n][0].size // D for n in sm_names}
    n_rows = sum(rows.values())
    pad = (-n_rows) % 8
    stack = lambda parts: jnp.concatenate([q.reshape(-1, D) for q in parts] + [jnp.ones((pad, D), F32)], axis=0)
    st = _adamw(stack([sm_w[n][0] for n in sm_names]), stack([sm_w[n][1] for n in sm_names]),
                stack([sm_w[n][2] for n in sm_names]), stack([sm_g[n] for n in sm_names]), "adamw_small")
    off = 0
    for n in sm_names:
        res[n] = tuple(t[off:off + rows[n]].reshape(sm_w[n][0].shape) for t in st)
        off += rows[n]

    order = ("ada_w", "ada_b", "norm_ffn1", "ffn1_w_in", "ffn1_w_out", "norm_mix", "mix_w_in", "hgrn_lb", "hgrn_g",
             "hgrn_w_o", "conv_w", "conv_b", "conv_ln_g", "conv_ln_b", "conv_w_o", "mix_w_out", "norm_ffn2",
             "ffn2_w_in", "ffn2_w_out", "norm_final")
    lead = lambda n, t: t[None] if n in big or n == "ada_w" else t
    outs = [loss, dx[None]]
    for j in range(4):
        outs += [lead(n, res[n][j]) for n in order]
    return tuple(outs)
```

```python
import jax
import jax.numpy as jnp
from jax import lax
from jax.experimental import pallas as pl
from jax.experimental.pallas import tpu as pltpu

F32 = jnp.float32
MM = jnp.bfloat16
ACT = jnp.bfloat16

D = 1024
D_FF = 2816
HEADS = 8
HD = 128
CHUNK = 64
SUB = 16
NSUB = CHUNK // SUB
HGRN_BLOCK = 1024
SAFE_EXP = 60.0
CONV_K = 31
HALO = 32
EPS = 1e-6
N_DEV = 8
NEG = -1e30
Q_SCALE = HD ** -0.5

ADAM_LR = 0.001
ADAM_B1 = 0.9
ADAM_B2 = 0.999
ADAM_EPS = 1e-08
ADAM_WD = 0.01
ADAM_STEP = 10

V7X_VMEM_BYTES = 64 * 1024 * 1024
VMEM_LIMIT = V7X_VMEM_BYTES - 4 * 1024 * 1024
MESH = pl.DeviceIdType.MESH


def _cparams(n_axes):
    return pltpu.CompilerParams(dimension_semantics=("arbitrary",) * n_axes, vmem_limit_bytes=VMEM_LIMIT)


def _mm(a, b):
    return lax.dot_general(a.astype(MM), b.astype(MM), (((1,), (0,)), ((), ())), preferred_element_type=F32)


def _mm_nt(a, b):
    return lax.dot_general(a.astype(MM), b.astype(MM), (((1,), (1,)), ((), ())), preferred_element_type=F32)


def _mm_tn(a, b):
    return lax.dot_general(a.astype(MM), b.astype(MM), (((0,), (0,)), ((), ())), preferred_element_type=F32)


def _sig(x):
    return 1.0 / (1.0 + jnp.exp(-x))


def _colsum(x):
    return jnp.sum(x, axis=0, keepdims=True)


def _rowmean(x):
    return jnp.mean(x, axis=-1, keepdims=True)


def _modnorm_fwd(xv, g, sh, sc):
    r = lax.rsqrt(_rowmean(xv * xv) + EPS)
    xh = xv * r
    n = xh * g
    return n * (1.0 + sc) + sh, xh, n, r


def _modnorm_bwd(dh, xh, n, r, g, sc):
    dsc = _colsum(dh * n)
    dsh = _colsum(dh)
    dn = dh * (1.0 + sc)
    dg = _colsum(dn * xh)
    dxh = dn * g
    dx = r * (dxh - xh * _rowmean(dxh * xh))
    return dx, dsh, dsc, dg


def _ffn_fwd(x, mod, mo, gnorm, w_in_t, w_out, res, name, carry, nxt=None):
    T = x.shape[0]
    tm = min(512, T)
    tn = D_FF // 2

    def body(x_ref, mod_ref, g_ref, wi_ref, wo_ref, *rest):
        if nxt is None:
            xo_ref, a_ref, b_ref, f_ref, h_ref = rest
        else:
            gn_ref, xo_ref, a_ref, b_ref, f_ref, h_ref, hn_ref = rest
        xv = x_ref[...]
        h, _, _, _ = _modnorm_fwd(xv, g_ref[...], mod_ref[mo:mo + 1, :], mod_ref[mo + 1:mo + 2, :])
        h = h.astype(ACT)
        h_ref[...] = h
        f = None
        for c0 in range(0, D_FF, tn):
            a = _mm_nt(h, wi_ref[0, c0:c0 + tn, :])
            b = _mm_nt(h, wi_ref[1, c0:c0 + tn, :])
            a_ref[:, c0:c0 + tn] = a.astype(ACT)
            b_ref[:, c0:c0 + tn] = b.astype(ACT)
            part = _mm(a * _sig(a) * b, wo_ref[c0:c0 + tn, :])
            f = part if f is None else f + part
        f_ref[...] = f
        xo = xv + res * mod_ref[mo + 2:mo + 3, :] * f
        xo_ref[...] = xo
        if nxt is not None:
            hn, _, _, _ = _modnorm_fwd(xo, gn_ref[...], mod_ref[nxt[1]:nxt[1] + 1, :], mod_ref[nxt[1] + 1:nxt[1] + 2, :])
            hn_ref[...] = hn.astype(ACT)

    tile = pl.BlockSpec((tm, D), lambda i: (i, 0))
    wide = pl.BlockSpec((tm, D_FF), lambda i: (i, 0))
    row = pl.BlockSpec((1, D), lambda i: (0, 0))
    n_out = 5 if nxt is None else 6
    out = _gridded(
        body, carry, name=name, grid=(T // tm,),
        in_specs=[
            tile,
            pl.BlockSpec((9, D), lambda i: (0, 0)),
            row,
            pl.BlockSpec((2, D_FF, D), lambda i: (0, 0, 0), pipeline_mode=pl.Buffered(1)),
            pl.BlockSpec((D_FF, D), lambda i: (0, 0), pipeline_mode=pl.Buffered(1)),
        ] + ([] if nxt is None else [row]),
        out_specs=[tile, wide, wide, tile, tile] + ([] if nxt is None else [tile]),
        out_shape=[
            jax.ShapeDtypeStruct((T, D), F32),
            jax.ShapeDtypeStruct((T, D_FF), ACT),
            jax.ShapeDtypeStruct((T, D_FF), ACT),
            jax.ShapeDtypeStruct((T, D), F32),
            jax.ShapeDtypeStruct((T, D), ACT),
        ] + ([] if nxt is None else [jax.ShapeDtypeStruct((T, D), ACT)]),
    )(*((x, mod, gnorm, w_in_t, w_out) + (() if nxt is None else (nxt[0],))))
    return out[:n_out], out[n_out:]


def _ffn_bwd_w(h, df, a, b, w_out, name, carry):
    T = h.shape[0]
    tm = min(2048, T)
    ni = T // tm
    tn = 256
    nj = D_FF // tn

    def body(h_ref, df_ref, a_ref, b_ref, wo_ref, da_ref, db_ref, dwi_ref, dwo_ref, acc_i, acc_o):
        i = pl.program_id(1)

        @pl.when(i == 0)
        def _():
            acc_i[...] = jnp.zeros_like(acc_i)
            acc_o[...] = jnp.zeros_like(acc_o)

        hb = h_ref[...]
        df = df_ref[...]
        av = a_ref[...].astype(F32)
        bv = b_ref[...].astype(F32)
        sg = _sig(av)
        sa = av * sg
        s = (sa * bv).astype(MM)
        ds = _mm_nt(df, wo_ref[...])
        da = (ds * bv * sg * (1.0 + av * (1.0 - sg))).astype(MM)
        db = (ds * sa).astype(MM)
        da_ref[...] = da
        db_ref[...] = db
        acc_o[...] += _mm_tn(s, df)
        acc_i[0] += _mm_tn(da, hb)
        acc_i[1] += _mm_tn(db, hb)

        @pl.when(i == ni - 1)
        def _():
            dwi_ref[...] = acc_i[...].astype(MM)
            dwo_ref[...] = acc_o[...].astype(MM)

    out = _gridded(
        body, carry, name=name, grid=(nj, ni),
        in_specs=[
            pl.BlockSpec((tm, D), lambda j, i: (i, 0)),
            pl.BlockSpec((tm, D), lambda j, i: (i, 0)),
            pl.BlockSpec((tm, tn), lambda j, i: (i, j)),
            pl.BlockSpec((tm, tn), lambda j, i: (i, j)),
            pl.BlockSpec((tn, D), lambda j, i: (j, 0)),
        ],
        out_specs=[
            pl.BlockSpec((tm, tn), lambda j, i: (i, j)),
            pl.BlockSpec((tm, tn), lambda j, i: (i, j)),
            pl.BlockSpec((2, tn, D), lambda j, i: (0, j, 0)),
            pl.BlockSpec((tn, D), lambda j, i: (j, 0)),
        ],
        out_shape=[
            jax.ShapeDtypeStruct((T, D_FF), MM),
            jax.ShapeDtypeStruct((T, D_FF), MM),
            jax.ShapeDtypeStruct((2, D_FF, D), MM),
            jax.ShapeDtypeStruct((D_FF, D), MM),
        ],
        scratch_shapes=[pltpu.VMEM((2, tn, D), F32), pltpu.VMEM((tn, D), F32)],
    )(h, df, a, b, w_out)
    return out[:4], out[4:]


def _ffn_bwd_x(x, dxo, f, da, db, mod, mo, gnorm, w_in_t, res, name, carry):
    T = x.shape[0]
    tm = min(512, T)
    ni = T // tm
    tn = D_FF // 2
    nj = D_FF // tn

    def body(x_ref, dxo_ref, f_ref, da_ref, db_ref, mod_ref, g_ref, wi_ref, dx_ref, sm_ref, dh_scr):
        j = pl.program_id(0)
        i = pl.program_id(1)

        @pl.when((j == 0) & (i == 0))
        def _():
            sm_ref[...] = jnp.zeros_like(sm_ref)

        @pl.when(j == 0)
        def _():
            dh_scr[i] = jnp.zeros((tm, D), F32)

        dh_scr[i] += _mm(da_ref[...], wi_ref[0]) + _mm(db_ref[...], wi_ref[1])

        @pl.when(j == nj - 1)
        def _():
            sc = mod_ref[mo + 1:mo + 2, :]
            _, xh, n, r = _modnorm_fwd(x_ref[...], g_ref[...], mod_ref[mo:mo + 1, :], sc)
            dxn, dsh, dsc, dg = _modnorm_bwd(dh_scr[i], xh, n, r, g_ref[...], sc)
            dxo_v = dxo_ref[...]
            dx_ref[...] = dxo_v + dxn
            sm_ref[0:1, :] += dsh
            sm_ref[1:2, :] += dsc
            sm_ref[2:3, :] += _colsum(dxo_v * f_ref[...]) * res
            sm_ref[3:4, :] += dg

    last = pl.BlockSpec((tm, D), lambda j, i: (jnp.where(j == nj - 1, i, 0), 0))
    out = _gridded(
        body, carry, name=name, grid=(nj, ni),
        in_specs=[last, last, last,
                  pl.BlockSpec((tm, tn), lambda j, i: (i, j)), pl.BlockSpec((tm, tn), lambda j, i: (i, j)),
                  pl.BlockSpec((9, D), lambda j, i: (0, 0)), pl.BlockSpec((1, D), lambda j, i: (0, 0)),
                  pl.BlockSpec((2, tn, D), lambda j, i: (0, j, 0))],
        out_specs=[last, pl.BlockSpec((8, D), lambda j, i: (0, 0))],
        out_shape=[jax.ShapeDtypeStruct((T, D), F32), jax.ShapeDtypeStruct((8, D), F32)],
        scratch_shapes=[pltpu.VMEM((ni, tm, D), F32)],
    )(x, dxo, f, da, db, mod, gnorm, w_in_t)
    return out[:2], out[2:]


def _head(x, target, gfin, mod, gate_row, res):
    T = x.shape[0]
    tm = min(512, T)
    ni = T // tm

    def body(x_ref, t_ref, g_ref, mod_ref, dx_ref, df_ref, sm_ref):
        i = pl.program_id(0)

        @pl.when(i == 0)
        def _():
            sm_ref[...] = jnp.zeros_like(sm_ref)

        xv = x_ref[...]
        g = g_ref[...]
        r = lax.rsqrt(_rowmean(xv * xv) + EPS)
        xh = xv * r
        e = xh * g - t_ref[...]
        sm_ref[1:2, :] += _colsum(e * e) * (0.5 / D)
        dy = e * (1.0 / D)
        sm_ref[0:1, :] += _colsum(dy * xh)
        dxh = dy * g
        dx = r * (dxh - xh * _rowmean(dxh * xh))
        dx_ref[...] = dx
        df_ref[...] = (res * mod_ref[gate_row:gate_row + 1, :] * dx).astype(MM)

        @pl.when(i == ni - 1)
        def _():
            sm_ref[1:2, :] = jnp.broadcast_to(jnp.sum(sm_ref[1:2, :], axis=-1, keepdims=True), (1, D))

    tile = pl.BlockSpec((tm, D), lambda i: (i, 0))
    return pl.pallas_call(
        body, name="head_loss", grid=(ni,),
        in_specs=[tile, tile, pl.BlockSpec((1, D), lambda i: (0, 0)), pl.BlockSpec((9, D), lambda i: (0, 0))],
        out_specs=[tile, tile, pl.BlockSpec((8, D), lambda i: (0, 0))],
        out_shape=[jax.ShapeDtypeStruct((T, D), F32), jax.ShapeDtypeStruct((T, D), MM),
                   jax.ShapeDtypeStruct((8, D), F32)],
        compiler_params=_cparams(1),
    )(x, target, gfin, mod)


def _mixin_fwd(h, w, carry):
    T = h.shape[0]
    tm = min(2048, T)
    ni = T // tm

    def body(h_ref, w_ref, p_ref, h_all):
        i = pl.program_id(1)

        @pl.when(pl.program_id(0) == 0)
        def _():
            h_all[i] = h_ref[...]

        p_ref[0] = _mm(h_all[i], w_ref[0])

    first = lambda k, i: (jnp.where(k == 0, i, ni - 1), 0)
    out = _gridded(
        body, carry, name="mixin_fwd", grid=(8, ni),
        in_specs=[pl.BlockSpec((tm, D), first), pl.BlockSpec((1, D, D), lambda k, i: (k, 0, 0))],
        out_specs=[pl.BlockSpec((1, tm, D), lambda k, i: (k, i, 0))],
        out_shape=[jax.ShapeDtypeStruct((8, T, D), F32)],
        scratch_shapes=[pltpu.VMEM((ni, tm, D), ACT)],
    )(h, w)
    return out[:1], out[1:]


def _mixin_bwd(x, h, dxo, dp, mod, mo, gnorm, w, next_gate, next_res, carry):
    T = x.shape[0]
    tm = min(512, T)
    ni = T // tm

    def body(x_ref, h_ref, dxo_ref, dp_ref, mod_ref, g_ref, w_ref, dx_ref, dw_ref, sm_ref, df_ref, dh_scr, acc):
        k = pl.program_id(0)
        i = pl.program_id(1)

        @pl.when(i == 0)
        def _():
            acc[...] = jnp.zeros_like(acc)

        @pl.when(k == 0)
        def _():
            dh_scr[i] = jnp.zeros((tm, D), F32)

        @pl.when((k == 0) & (i == 0))
        def _():
            sm_ref[...] = jnp.zeros_like(sm_ref)

        dpk = dp_ref[0].astype(MM)
        acc[...] += _mm_tn(h_ref[...], dpk)
        dh_scr[i] += _mm_nt(dpk, w_ref[0])

        @pl.when(i == ni - 1)
        def _():
            dw_ref[0] = acc[...].astype(MM)

        @pl.when(k == 7)
        def _():
            sc = mod_ref[mo + 1:mo + 2, :]
            _, xh, n, r = _modnorm_fwd(x_ref[...], g_ref[...], mod_ref[mo:mo + 1, :], sc)
            dxn, dsh, dsc, dg = _modnorm_bwd(dh_scr[i], xh, n, r, g_ref[...], sc)
            dx = dxo_ref[...] + dxn
            dx_ref[...] = dx
            df_ref[...] = (next_res * mod_ref[next_gate:next_gate + 1, :] * dx).astype(MM)
            sm_ref[0:1, :] += dsh
            sm_ref[1:2, :] += dsc
            sm_ref[3:4, :] += dg

    last = pl.BlockSpec((tm, D), lambda k, i: (jnp.where(k == 7, i, 0), 0))
    out = _gridded(
        body, carry, name="mixin_bwd", grid=(8, ni),
        in_specs=[pl.BlockSpec((tm, D), lambda k, i: (jnp.where(k == 7, i, 0), 0)),
                  pl.BlockSpec((tm, D), lambda k, i: (i, 0)),
                  pl.BlockSpec((tm, D), lambda k, i: (jnp.where(k == 7, i, 0), 0)),
                  pl.BlockSpec((1, tm, D), lambda k, i: (k, i, 0)), pl.BlockSpec((9, D), lambda k, i: (0, 0)),
                  pl.BlockSpec((1, D), lambda k, i: (0, 0)), pl.BlockSpec((1, D, D), lambda k, i: (k, 0, 0))],
        out_specs=[last, pl.BlockSpec((1, D, D), lambda k, i: (k, 0, 0)), pl.BlockSpec((8, D), lambda k, i: (0, 0)),
                   last],
        out_shape=[jax.ShapeDtypeStruct((T, D), F32), jax.ShapeDtypeStruct((8, D, D), MM),
                   jax.ShapeDtypeStruct((8, D), F32), jax.ShapeDtypeStruct((T, D), MM)],
        scratch_shapes=[pltpu.VMEM((ni, tm, D), F32), pltpu.VMEM((D, D), F32)],
    )(x, h, dxo, dp, mod, gnorm, w)
    return out[:4], out[4:]


def _hgrn_consts():
    rows = jnp.arange(SUB * HD) // HD
    e = (rows[:, None] == jnp.arange(HD)[None, :]).astype(MM)
    return e, e.T


def _rows_bcast(ref, cb, first, n):
    parts = [jnp.broadcast_to(ref[pl.ds(c * CHUNK + first, 1), :], (n, HD)) for c in range(cb // CHUNK)]
    return jnp.concatenate(parts, axis=0)


def _hgrn_pre(qr, fr, lb_ref, b_scr, cb):
    z = lb_ref[...]
    lb = _sig(z[0:1, :] - z[1:2, :])
    sq = _sig(qr)
    q = qr * sq * Q_SCALE
    sf = _sig(fr)
    fg = lb + (1.0 - lb) * sf
    lf = jnp.log(fg)
    k = 1.0 - fg
    tl = lax.broadcasted_iota(jnp.int32, (cb, HD), 0) % CHUNK
    bc = lf
    sh = 1
    while sh < CHUNK:
        bc = bc + jnp.where(tl >= sh, pltpu.roll(bc, sh, 0), 0.0)
        sh *= 2
    b_scr[...] = bc
    bl = _rows_bcast(b_scr, cb, CHUNK - 1, CHUNK)
    eb = jnp.exp(bc)
    ekd = jnp.exp(bl - bc)
    ekf = jnp.exp(jnp.minimum(-bc, SAFE_EXP))
    return dict(lb=lb, sq=sq, q=q, sf=sf, fg=fg, k=k, tl=tl, b=bc, bl=bl, eb=eb, ekd=ekd, ekf=ekf,
                qe=q * eb, kd=k * ekd, kf=k * ekf, safe=jnp.max(-bc) < SAFE_EXP)


def _hgrn_pre_fused(p_ref, lb_ref, b_scr, q_scr, k_scr, qe_scr, kf_scr, kd_scr, cb):
    z = lb_ref[...]
    lb = _sig(z[0:1, :] - z[1:2, :])
    tl = lax.broadcasted_iota(jnp.int32, (CHUNK, HD), 0)

    def chunk(c, worst):
        r0 = pl.multiple_of(c * CHUNK, CHUNK)
        rs = pl.ds(r0, CHUNK)
        qr = p_ref[0, rs, :]
        q = qr * _sig(qr) * Q_SCALE
        fg = lb + (1.0 - lb) * _sig(p_ref[1, rs, :])
        k = 1.0 - fg
        bc = jnp.log(fg)
        sh = 1
        while sh < CHUNK:
            bc = bc + jnp.where(tl >= sh, pltpu.roll(bc, sh, 0), 0.0)
            sh *= 2
        b_scr[rs, :] = bc
        q_scr[rs, :] = q
        k_scr[rs, :] = k
        qe_scr[rs, :] = (q * jnp.exp(bc)).astype(MM)
        kf_scr[rs, :] = (k * jnp.exp(jnp.minimum(-bc, SAFE_EXP))).astype(MM)
        kd_scr[rs, :] = (k * jnp.exp(b_scr[pl.ds(r0 + CHUNK - 1, 1), :] - bc)).astype(MM)
        return jnp.maximum(worst, -bc)

    worst = lax.fori_loop(0, cb // CHUNK, chunk, jnp.zeros((CHUNK, HD), F32))
    return jnp.max(worst) < SAFE_EXP


def _hgrn_sub(pre, b_scr, cb):
    bc, tl, q, k = pre["b"], pre["tl"], pre["q"], pre["k"]
    br = [None] + [_rows_bcast(b_scr, cb, SUB * i - 1, CHUNK) for i in range(1, NSUB)]
    sb = tl // SUB
    bref = jnp.where(sb == 0, bc, jnp.where(sb == 1, br[1], jnp.where(sb == 2, br[2], br[3])))
    eqo = jnp.exp(bc - bref)
    eko = [None] + [jnp.exp(jnp.where(tl < SUB * i, br[i] - bc, NEG)) for i in range(1, NSUB)]
    return dict(eqo=eqo, eko=eko, qo=q * eqo, ko=[None] + [k * eko[i] for i in range(1, NSUB)])


def _pad_rows(x):
    return jnp.concatenate([x, jnp.zeros_like(x)], axis=0)


def _by_subblock(sbc, parts):
    out = jnp.zeros_like(parts[1])
    for i in range(1, NSUB):
        out = jnp.where(sbc == i, parts[i], out)
    return out


def _hgrn_fwd(p, hgrn_lb, hgrn_g, carry):
    T = p.shape[1]
    cb = min(HGRN_BLOCK, T)
    nch = cb // CHUNK
    ncb = T // cb
    e_mat, _ = _hgrn_consts()

    def body(p_ref, lb_ref, g_ref, e_ref, o_ref, oa_ref, a_ref, s_ref, st_scr, q_scr, k_scr, b_scr, z_scr, ad_scr,
             qe_scr, kf_scr, kd_scr):
        @pl.when(pl.program_id(1) == 0)
        def _():
            st_scr[...] = jnp.zeros_like(st_scr)

        safe = _hgrn_pre_fused(p_ref, lb_ref, b_scr, q_scr, k_scr, qe_scr, kf_scr, kd_scr, cb)
        chunks = [slice(c * CHUNK, (c + 1) * CHUNK) for c in range(nch)]
        row_i = lax.broadcasted_iota(jnp.int32, (CHUNK, HD), 0)
        lane_i = lax.broadcasted_iota(jnp.int32, (CHUNK, HD), 1)
        sbc = row_i // SUB
        causal = lane_i <= row_i

        @pl.when(safe)
        def _():
            for rs in chunks:
                ad_scr[rs, :] = jnp.where(causal, _mm_nt(qe_scr[rs, :], _pad_rows(kf_scr[rs, :])), 0.0)

        @pl.when(jnp.logical_not(safe))
        def _():
            tl = lax.broadcasted_iota(jnp.int32, (cb, HD), 0) % CHUNK
            sub = _hgrn_sub(dict(b=b_scr[...], tl=tl, q=q_scr[...], k=k_scr[...]), b_scr, cb)
            ti = lax.broadcasted_iota(jnp.int32, (SUB, HD), 0)

            def zbody(c, carry):
                for i in range(NSUB):
                    r0 = pl.multiple_of(c * CHUNK + SUB * i, SUB)
                    qi = q_scr[pl.ds(r0, SUB), :]
                    bi = b_scr[pl.ds(r0, SUB), :]
                    for s in range(SUB):
                        krow = k_scr[pl.ds(r0 + s, 1), :]
                        brow = b_scr[pl.ds(r0 + s, 1), :]
                        if s < 8:
                            zz = qi * krow * jnp.exp(jnp.where(ti >= s, bi - brow, NEG))
                        else:
                            lo = qi[8:] * krow * jnp.exp(jnp.where(ti[8:] >= s, bi[8:] - brow, NEG))
                            zz = jnp.concatenate([jnp.zeros((8, HD), F32), lo], axis=0)
                        z_scr[i, pl.ds(pl.multiple_of(c * SUB, SUB), SUB), s * HD:(s + 1) * HD] = zz.astype(MM)
                return carry

            lax.fori_loop(0, nch, zbody, 0)
            adiag = [_mm(z_scr[i], e_ref[...]) for i in range(NSUB)]
            offs = [[_mm_nt(sub["qo"][rs], _pad_rows(sub["ko"][i][rs])) for i in range(1, NSUB)] for rs in chunks]
            for c, rs in enumerate(chunks):
                dparts = []
                for i in range(NSUB):
                    blk = adiag[i][c * SUB:(c + 1) * SUB]
                    dparts.append(blk if i == 0 else pltpu.roll(blk, SUB * i, 1))
                ad_scr[rs, :] = _by_subblock(sbc, [None] + offs[c]) + jnp.concatenate(dparts, axis=0)

        kv = [_mm_tn(p_ref[2, rs, :], kd_scr[rs, :]) for rs in chunks]
        a_ref[0] = ad_scr[...]
        o_intra = [_mm(ad_scr[rs, :], _pad_rows(p_ref[2, rs, :])) for rs in chunks]
        states = []
        st = st_scr[...]
        for c in range(nch):
            states.append(st)
            st = st * jnp.exp(b_scr[pl.ds(c * CHUNK + CHUNK - 1, 1), :]) + kv[c]
        st_scr[...] = st
        g = g_ref[...]
        for c, rs in enumerate(chunks):
            s_ref[0, c] = states[c]
            o = o_intra[c] + _mm_nt(qe_scr[rs, :], states[c])
            o_ref[rs, :] = o
            og = p_ref[3, rs, :]
            oa_ref[rs, :] = (o * lax.rsqrt(_rowmean(o * o) + EPS) * g * og * _sig(og)).astype(ACT)

    out = _gridded(
        body, carry, name="hgrn_fwd", grid=(HEADS, ncb),
        in_specs=[pl.BlockSpec((4, cb, HD), lambda h, c: (0, c, h)),
                  pl.BlockSpec((2, HD), lambda h, c: (0, h)),
                  pl.BlockSpec((1, HD), lambda h, c: (0, h)),
                  pl.BlockSpec((SUB * HD, HD), lambda h, c: (0, 0))],
        out_specs=[pl.BlockSpec((cb, HD), lambda h, c: (c, h)),
                   pl.BlockSpec((cb, HD), lambda h, c: (c, h)),
                   pl.BlockSpec((1, cb, HD), lambda h, c: (h, c, 0)),
                   pl.BlockSpec((1, nch, HD, HD), lambda h, c: (h, c, 0, 0))],
        out_shape=[jax.ShapeDtypeStruct((T, D), F32), jax.ShapeDtypeStruct((T, D), ACT),
                   jax.ShapeDtypeStruct((HEADS, T, HD), F32),
                   jax.ShapeDtypeStruct((HEADS, T // CHUNK, HD, HD), F32)],
        scratch_shapes=[pltpu.VMEM((HD, HD), F32), pltpu.VMEM((cb, HD), F32), pltpu.VMEM((cb, HD), F32),
                        pltpu.VMEM((cb, HD), F32), pltpu.VMEM((NSUB, nch * SUB, SUB * HD), MM),
                        pltpu.VMEM((cb, HD), F32), pltpu.VMEM((cb, HD), MM), pltpu.VMEM((cb, HD), MM),
                        pltpu.VMEM((cb, HD), MM)],
    )(p, hgrn_lb, hgrn_g, e_mat)
    return out[:4], out[4:]


def _hgrn_bwd(p, o, a_all, s_all, doa, hgrn_lb, hgrn_g, dp, carry):
    T = p.shape[1]
    cb = min(HGRN_BLOCK, T)
    nch = cb // CHUNK
    ncb = T // cb
    _, et_mat = _hgrn_consts()

    def body(p_ref, o_ref, a_ref, s_ref, doa_ref, lb_ref, g_ref, et_ref, dp_in, dp_ref, sm_ref,
             dst_scr, q_scr, k_scr, b_scr, x_scr, dqd_scr, dkd_scr):
        del dp_in

        @pl.when(pl.program_id(1) == 0)
        def _():
            dst_scr[...] = jnp.zeros_like(dst_scr)
            sm_ref[...] = jnp.zeros_like(sm_ref)

        qr = p_ref[0]
        v = p_ref[2]
        og = p_ref[3]
        pre = _hgrn_pre(qr, p_ref[1], lb_ref, b_scr, cb)
        q, k = pre["q"], pre["k"]
        g = g_ref[...]
        ov = o_ref[...]
        r = lax.rsqrt(_rowmean(ov * ov) + EPS)
        oh = ov * r
        sgo = _sig(og)
        doa_v = doa_ref[...]
        don = doa_v * og * sgo
        dog = doa_v * oh * g * sgo * (1.0 + og * (1.0 - sgo))
        sm_ref[1:2, :] += _colsum(don * oh)
        doh = don * g
        do = r * (doh - oh * _rowmean(doh * oh))

        sbc = lax.broadcasted_iota(jnp.int32, (CHUNK, HD), 0) // SUB
        row_i = lax.broadcasted_iota(jnp.int32, (CHUNK, HD), 0)
        lane_i = lax.broadcasted_iota(jnp.int32, (CHUNK, HD), 1)
        causal = lane_i <= row_i
        chunks = [slice(c * CHUNK, (c + 1) * CHUNK) for c in range(nch)]
        da_parts = [jnp.where(causal, _mm_nt(do[rs], _pad_rows(v[rs])), 0.0) for rs in chunks]
        dv_parts = [_mm_tn(a_ref[0, rs, :], do[rs])[:CHUNK] for rs in chunks]

        @pl.when(pre["safe"])
        def _():
            hi = dict(preferred_element_type=F32, precision=lax.Precision.HIGH)
            for c, rs in enumerate(chunks):
                dqd_scr[rs, :] = pre["eb"][rs] * lax.dot_general(
                    da_parts[c], _pad_rows(pre["kf"][rs]), (((1,), (0,)), ((), ())), **hi)
                dkd_scr[rs, :] = pre["ekf"][rs] * lax.dot_general(
                    da_parts[c], pre["qe"][rs], (((0,), (0,)), ((), ())), **hi)[:CHUNK]

        @pl.when(jnp.logical_not(pre["safe"]))
        def _():
            sub = _hgrn_sub(pre, b_scr, cb)
            dqoff_mm = [[_mm(da_parts[c], _pad_rows(sub["ko"][i][rs])) for i in range(1, NSUB)]
                        for c, rs in enumerate(chunks)]
            dkoff_mm = [[_mm_tn(jnp.where(sbc == i, da_parts[c], 0.0), sub["qo"][rs])[:CHUNK]
                         for i in range(1, NSUB)] for c, rs in enumerate(chunks)]
            dqoff_parts = [_by_subblock(sbc, [None] + dqoff_mm[c]) for c in range(nch)]
            dkoff_parts = []
            for c, rs in enumerate(chunks):
                dko = sub["eko"][1][rs] * dkoff_mm[c][0]
                for i in range(2, NSUB):
                    dko = dko + sub["eko"][i][rs] * dkoff_mm[c][i - 1]
                dkoff_parts.append(dko)
            q_scr[...] = q
            k_scr[...] = k
            for i in range(NSUB):
                rows = []
                for c in range(nch):
                    blk = da_parts[c][SUB * i:SUB * (i + 1)]
                    rows.append(blk if i == 0 else pltpu.roll(blk, HD - SUB * i, 1))
                x_scr[i] = _mm(jnp.concatenate(rows, axis=0), et_ref[...])
            ti = lax.broadcasted_iota(jnp.int32, (SUB, HD), 0)

            def dbody(c, carry):
                for i in range(NSUB):
                    r0 = pl.multiple_of(c * CHUNK + SUB * i, SUB)
                    qi = q_scr[pl.ds(r0, SUB), :]
                    bi = b_scr[pl.ds(r0, SUB), :]
                    dq_hi = jnp.zeros((8, HD), F32)
                    dq_lo = jnp.zeros((8, HD), F32)
                    dk_hi = jnp.zeros((8, HD), F32)
                    dk_lo = jnp.zeros((8, HD), F32)
                    c0 = pl.multiple_of(c * SUB, SUB)
                    t8 = ti[:8]
                    for s in range(SUB):
                        krow = k_scr[pl.ds(r0 + s, 1), :]
                        brow = b_scr[pl.ds(r0 + s, 1), :]
                        w_lo = (x_scr[i, pl.ds(c0 + 8, 8), s * HD:(s + 1) * HD]
                                * jnp.exp(jnp.where(t8 + 8 >= s, bi[8:] - brow, NEG)))
                        dq_lo = dq_lo + w_lo * krow
                        col = _colsum(w_lo * qi[8:])
                        if s < 8:
                            w_hi = (x_scr[i, pl.ds(c0, 8), s * HD:(s + 1) * HD]
                                    * jnp.exp(jnp.where(t8 >= s, bi[:8] - brow, NEG)))
                            dq_hi = dq_hi + w_hi * krow
                            dk_hi = jnp.where(t8 == s, col + _colsum(w_hi * qi[:8]), dk_hi)
                        else:
                            dk_lo = jnp.where(t8 + 8 == s, col, dk_lo)
                    dqd_scr[pl.ds(r0, SUB), :] = jnp.concatenate([dq_hi, dq_lo], axis=0)
                    dkd_scr[pl.ds(r0, SUB), :] = jnp.concatenate([dk_hi, dk_lo], axis=0)
                return carry

            lax.fori_loop(0, nch, dbody, 0)
            dqd_scr[...] += jnp.concatenate(dqoff_parts, axis=0) * sub["eqo"]
            dkd_scr[...] += jnp.concatenate(dkoff_parts, axis=0)

        qdo = [_mm_tn(do[rs], pre["qe"][rs]) for rs in chunks]
        dsts = [None] * nch
        dst = dst_scr[...]
        for c in reversed(range(nch)):
            dsts[c] = dst
            dst = dst * jnp.exp(b_scr[pl.ds(c * CHUNK + CHUNK - 1, 1), :]) + qdo[c]
        dst_scr[...] = dst
        sts = [s_ref[0, c] for c in range(nch)]
        dqe_parts = [_mm(do[rs], sts[c]) for c, rs in enumerate(chunks)]
        dkdec_parts = [_mm(v[rs], dsts[c]) for c, rs in enumerate(chunks)]
        dvi_parts = [_mm_nt(pre["kd"][rs], dsts[c]) for c, rs in enumerate(chunks)]
        debl_parts = [_colsum(dsts[c] * sts[c]) for c in range(nch)]
        dqe = jnp.concatenate(dqe_parts, axis=0)
        dkdec = jnp.concatenate(dkdec_parts, axis=0)
        dq_tot = dqd_scr[...] + dqe * pre["eb"]
        dk_inter = dkdec * pre["ekd"]
        dk_tot = dkd_scr[...] + dk_inter
        db = q * dq_tot - k * dk_tot
        kdk = k * dk_inter
        dbl = jnp.concatenate(
            [jnp.broadcast_to(jnp.exp(b_scr[pl.ds(c * CHUNK + CHUNK - 1, 1), :]) * debl_parts[c]
                              + _colsum(kdk[c * CHUNK:(c + 1) * CHUNK]), (CHUNK, HD)) for c in range(nch)], axis=0)
        tl = pre["tl"]
        rc = db
        sh = 1
        while sh < CHUNK:
            rc = rc + jnp.where(tl + sh < CHUNK, pltpu.roll(rc, cb - sh, 0), 0.0)
            sh *= 2
        dlf = rc + dbl
        dfg = dlf / pre["fg"] - dk_tot
        sf = pre["sf"]
        lb = pre["lb"]
        sm_ref[0:1, :] += _colsum(dfg * (1.0 - sf))
        sq = pre["sq"]
        dp_ref[0] = (dq_tot * Q_SCALE * sq * (1.0 + qr * (1.0 - sq))).astype(ACT)
        dp_ref[1] = (dfg * (1.0 - lb) * sf * (1.0 - sf)).astype(ACT)
        dp_ref[2] = (jnp.concatenate(dv_parts, axis=0) + jnp.concatenate(dvi_parts, axis=0)).astype(ACT)
        dp_ref[3] = dog.astype(ACT)

    rev = lambda c: ncb - 1 - c
    out = _gridded(
        body, carry, name="hgrn_bwd", grid=(HEADS, ncb),
        in_specs=[pl.BlockSpec((4, cb, HD), lambda h, c: (0, rev(c), h)),
                  pl.BlockSpec((cb, HD), lambda h, c: (rev(c), h)),
                  pl.BlockSpec((1, cb, HD), lambda h, c: (h, rev(c), 0)),
                  pl.BlockSpec((1, nch, HD, HD), lambda h, c: (h, rev(c), 0, 0)),
                  pl.BlockSpec((cb, HD), lambda h, c: (rev(c), h)),
                  pl.BlockSpec((2, HD), lambda h, c: (0, h)),
                  pl.BlockSpec((1, HD), lambda h, c: (0, h)),
                  pl.BlockSpec((HD, SUB * HD), lambda h, c: (0, 0)),
                  pl.BlockSpec(memory_space=pl.ANY)],
        out_specs=[pl.BlockSpec((4, cb, HD), lambda h, c: (0, rev(c), h)),
                   pl.BlockSpec((8, HD), lambda h, c: (0, h))],
        out_shape=[jax.ShapeDtypeStruct(dp.shape, dp.dtype), jax.ShapeDtypeStruct((8, D), F32)],
        aliases={8: 0},
        scratch_shapes=[pltpu.VMEM((HD, HD), F32), pltpu.VMEM((cb, HD), F32), pltpu.VMEM((cb, HD), F32),
                        pltpu.VMEM((cb, HD), F32), pltpu.VMEM((NSUB, nch * SUB, SUB * HD), F32),
                        pltpu.VMEM((cb, HD), F32), pltpu.VMEM((cb, HD), F32)],
    )(p, o, a_all, s_all, doa, hgrn_lb, hgrn_g, et_mat, dp)
    return out[:2], out[2:]


def _ln_fwd(u1, g, b):
    mu = _rowmean(u1)
    xc = u1 - mu
    rs = lax.rsqrt(_rowmean(xc * xc) + EPS)
    xh = xc * rs
    return xh * g + b, xh, rs


CONV_RB = 64
LANES = 128


def _shift_rows(src, sh, ls, n):
    for r in range(1, 8):
        sh[r - 1, 0:n, :] = src[pl.ds(r, n), ls]


def _tap(src, sh, ls, off, r0, rows):
    r = off % 8
    if r == 0:
        return src[pl.ds(r0 + off, rows), ls]
    return sh[r - 1, pl.ds(r0 + off - r, rows), :]


def _conv_fwd(p, cw, cb_, lng, lnb, carry):
    T = p.shape[1]
    tm = min(512, T)
    n = HALO + tm - 8

    def body(p_ref, cw_ref, cb_ref, g_ref, b_ref, u1_ref, u2_ref, buf, sh):
        @pl.when(pl.program_id(0) == 0)
        def _():
            buf[0:HALO, :] = jnp.zeros((HALO, D), F32)

        buf[HALO:HALO + tm, :] = p_ref[0] * _sig(p_ref[1])
        for lb in range(D // LANES):
            ls = slice(lb * LANES, (lb + 1) * LANES)
            _shift_rows(buf, sh, ls, n)
            taps = [cw_ref[j:j + 1, ls] for j in range(CONV_K)]
            bias = cb_ref[:, ls]

            def rows_body(rb, carry):
                r0 = pl.multiple_of(rb * CONV_RB, CONV_RB)
                acc = jnp.broadcast_to(bias, (CONV_RB, LANES))
                for j in range(CONV_K):
                    acc = acc + taps[j] * _tap(buf, sh, ls, HALO - (CONV_K - 1) + j, r0, CONV_RB)
                u1_ref[pl.ds(r0, CONV_RB), ls] = acc
                return carry

            lax.fori_loop(0, tm // CONV_RB, rows_body, 0)
        y, _, _ = _ln_fwd(u1_ref[...], g_ref[...], b_ref[...])
        u2_ref[...] = (y * _sig(y)).astype(ACT)
        buf[0:HALO, :] = buf[tm:tm + HALO, :]

    out = _gridded(
        body, carry, name="conv_fwd", grid=(T // tm,),
        in_specs=[pl.BlockSpec((2, tm, D), lambda i: (2, i, 0)), pl.BlockSpec((HALO, D), lambda i: (0, 0)),
                  pl.BlockSpec((1, D), lambda i: (0, 0)), pl.BlockSpec((1, D), lambda i: (0, 0)),
                  pl.BlockSpec((1, D), lambda i: (0, 0))],
        out_specs=[pl.BlockSpec((tm, D), lambda i: (i, 0)), pl.BlockSpec((tm, D), lambda i: (i, 0))],
        out_shape=[jax.ShapeDtypeStruct((T, D), F32), jax.ShapeDtypeStruct((T, D), ACT)],
        scratch_shapes=[pltpu.VMEM((HALO + tm, D), F32), pltpu.VMEM((7, n, LANES), F32)],
    )(p, cw, cb_, lng, lnb)
    return out[:2], out[2:]


def _conv_bwd(p, u1, du2, cw, lng, lnb, dp, carry):
    T = p.shape[1]
    tm = min(512, T)
    ni = T // tm
    hb = tm // HALO

    n = HALO + tm - 8

    def body(p_ref, ph_ref, u1_ref, du2_ref, cw_ref, g_ref, b_ref, dp_in, dp_ref, dcw_ref, sm_ref, ubuf, dbuf,
             sh, dacc):
        del dp_in
        step = pl.program_id(0)

        @pl.when(step == 0)
        def _():
            dbuf[tm:tm + HALO, :] = jnp.zeros((HALO, D), F32)
            dcw_ref[...] = jnp.zeros_like(dcw_ref)
            sm_ref[...] = jnp.zeros_like(sm_ref)

        ua = p_ref[0]
        sgb = _sig(p_ref[1])
        halo = ph_ref[0] * _sig(ph_ref[1])
        ubuf[0:HALO, :] = jnp.where(step == ni - 1, 0.0, halo)
        ubuf[HALO:HALO + tm, :] = ua * sgb
        g = g_ref[...]
        y, xh, rs = _ln_fwd(u1_ref[...], g, b_ref[...])
        sy = _sig(y)
        dy = du2_ref[...] * sy * (1.0 + y * (1.0 - sy))
        sm_ref[1:2, :] += _colsum(dy * xh)
        sm_ref[2:3, :] += _colsum(dy)
        dxh = dy * g
        du1 = rs * (dxh - _rowmean(dxh) - xh * _rowmean(dxh * xh))
        sm_ref[0:1, :] += _colsum(du1)
        dbuf[0:tm, :] = du1
        for lb in range(D // LANES):
            ls = slice(lb * LANES, (lb + 1) * LANES)
            taps = [cw_ref[j:j + 1, ls] for j in range(CONV_K)]
            _shift_rows(dbuf, sh, ls, n)

            def du0_body(rb, carry):
                r0 = pl.multiple_of(rb * CONV_RB, CONV_RB)
                acc = jnp.zeros((CONV_RB, LANES), F32)
                for j in range(CONV_K):
                    acc = acc + taps[j] * _tap(dbuf, sh, ls, CONV_K - 1 - j, r0, CONV_RB)
                dp_ref[0, pl.ds(r0, CONV_RB), ls] = acc.astype(ACT)
                return carry

            lax.fori_loop(0, tm // CONV_RB, du0_body, 0)
            _shift_rows(ubuf, sh, ls, n)
            dacc[...] = jnp.zeros_like(dacc)

            def dcw_body(rb, carry):
                r0 = pl.multiple_of(rb * CONV_RB, CONV_RB)
                d = dbuf[pl.ds(r0, CONV_RB), ls]
                for j in range(CONV_K):
                    prod = d * _tap(ubuf, sh, ls, HALO - (CONV_K - 1) + j, r0, CONV_RB)
                    dacc[8 * j:8 * j + 8, :] += jnp.sum(prod.reshape(CONV_RB // 8, 8, LANES), axis=0)
                return carry

            lax.fori_loop(0, tm // CONV_RB, dcw_body, 0)
            for j in range(CONV_K):
                dcw_ref[j:j + 1, ls] += _colsum(dacc[8 * j:8 * j + 8, :])
        du0 = dp_ref[0].astype(F32)
        dp_ref[0] = (du0 * sgb).astype(ACT)
        dp_ref[1] = (du0 * ua * sgb * (1.0 - sgb)).astype(ACT)
        dbuf[tm:tm + HALO, :] = dbuf[0:HALO, :]

    rev = lambda i: ni - 1 - i
    out = _gridded(
        body, carry, name="conv_bwd", grid=(ni,),
        in_specs=[pl.BlockSpec((2, tm, D), lambda i: (2, rev(i), 0)),
                  pl.BlockSpec((2, HALO, D), lambda i: (2, jnp.maximum(rev(i) * hb - 1, 0), 0)),
                  pl.BlockSpec((tm, D), lambda i: (rev(i), 0)), pl.BlockSpec((tm, D), lambda i: (rev(i), 0)),
                  pl.BlockSpec((HALO, D), lambda i: (0, 0)), pl.BlockSpec((1, D), lambda i: (0, 0)),
                  pl.BlockSpec((1, D), lambda i: (0, 0)), pl.BlockSpec(memory_space=pl.ANY)],
        out_specs=[pl.BlockSpec((2, tm, D), lambda i: (2, rev(i), 0)),
                   pl.BlockSpec((HALO, D), lambda i: (0, 0)), pl.BlockSpec((8, D), lambda i: (0, 0))],
        out_shape=[jax.ShapeDtypeStruct(dp.shape, dp.dtype), jax.ShapeDtypeStruct((HALO, D), F32),
                   jax.ShapeDtypeStruct((8, D), F32)],
        aliases={7: 0},
        scratch_shapes=[pltpu.VMEM((HALO + tm, D), F32), pltpu.VMEM((tm + HALO, D), F32),
                        pltpu.VMEM((7, n, LANES), F32), pltpu.VMEM((8 * CONV_K, LANES), F32)],
    )(p, p, u1, du2, cw, lng, lnb, dp)
    return out[:3], out[3:]


def _mixout_fwd(x, oa, u2, p, mod, mo, w_a, w_b, w_o):
    T = x.shape[0]
    tm = min(512, T)

    def body(x_ref, oa_ref, u2_ref, p_ref, mod_ref, wa_ref, wb_ref, wo_ref, xo_ref, ya_ref, yb_ref, mo_ref):
        ya = _mm(oa_ref[...], wa_ref[...])
        yb = _mm(u2_ref[...], wb_ref[...])
        ya_ref[...] = ya.astype(ACT)
        yb_ref[...] = yb.astype(ACT)
        merged = _sig(p_ref[0]) * ya + _sig(p_ref[1]) * yb
        out = _mm(merged, wo_ref[...])
        mo_ref[...] = out
        xo_ref[...] = x_ref[...] + mod_ref[mo + 2:mo + 3, :] * out

    tile = pl.BlockSpec((tm, D), lambda i: (i, 0))
    wspec = pl.BlockSpec((D, D), lambda i: (0, 0))
    return pl.pallas_call(
        body, name="mixout_fwd", grid=(T // tm,),
        in_specs=[tile, tile, tile, pl.BlockSpec((2, tm, D), lambda i: (3, i, 0)),
                  pl.BlockSpec((9, D), lambda i: (0, 0)), wspec, wspec, wspec],
        out_specs=[tile, tile, tile, tile],
        out_shape=[jax.ShapeDtypeStruct((T, D), F32), jax.ShapeDtypeStruct((T, D), ACT),
                   jax.ShapeDtypeStruct((T, D), ACT), jax.ShapeDtypeStruct((T, D), F32)],
        compiler_params=_cparams(1),
    )(x, oa, u2, p, mod, w_a, w_b, w_o)


def _mixout_bwd(dxo, oa, u2, ya, yb, mout, p, mod, mo, w_a, w_b, w_o):
    T = dxo.shape[0]
    tm = min(256, T)

    def body(dxo_ref, oa_ref, u2_ref, ya_ref, yb_ref, mo_ref, p_ref, mod_ref, wa_ref, wb_ref, wo_ref,
             dp_ref, doa_ref, du2_ref, dwa_ref, dwb_ref, dwo_ref, sm_ref):
        @pl.when(pl.program_id(0) == 0)
        def _():
            dwa_ref[...] = jnp.zeros_like(dwa_ref)
            dwb_ref[...] = jnp.zeros_like(dwb_ref)
            dwo_ref[...] = jnp.zeros_like(dwo_ref)
            sm_ref[...] = jnp.zeros_like(sm_ref)

        dxo_v = dxo_ref[...]
        sm_ref[2:3, :] += _colsum(dxo_v * mo_ref[...])
        dmo = (mod_ref[mo + 2:mo + 3, :] * dxo_v).astype(MM)
        ya = ya_ref[...].astype(F32)
        yb = yb_ref[...].astype(F32)
        sga = _sig(p_ref[0])
        sgb = _sig(p_ref[1])
        merged = (sga * ya + sgb * yb).astype(MM)
        dwo_ref[...] += _mm_tn(merged, dmo)
        dmg = _mm_nt(dmo, wo_ref[...])
        dp_ref[0] = (dmg * ya * sga * (1.0 - sga)).astype(ACT)
        dp_ref[1] = (dmg * yb * sgb * (1.0 - sgb)).astype(ACT)
        dya = (dmg * sga).astype(MM)
        dyb = (dmg * sgb).astype(MM)
        dwa_ref[...] += _mm_tn(oa_ref[...], dya)
        dwb_ref[...] += _mm_tn(u2_ref[...], dyb)
        doa_ref[...] = _mm_nt(dya, wa_ref[...])
        du2_ref[...] = _mm_nt(dyb, wb_ref[...])

    tile = pl.BlockSpec((tm, D), lambda i: (i, 0))
    wspec = pl.BlockSpec((D, D), lambda i: (0, 0))
    return pl.pallas_call(
        body, name="mixout_bwd", grid=(T // tm,),
        in_specs=[tile, tile, tile, tile, tile, tile, pl.BlockSpec((2, tm, D), lambda i: (3, i, 0)),
                  pl.BlockSpec((9, D), lambda i: (0, 0)), wspec, wspec, wspec],
        out_specs=[pl.BlockSpec((2, tm, D), lambda i: (3, i, 0)), tile, tile, wspec, wspec, wspec,
                   pl.BlockSpec((8, D), lambda i: (0, 0))],
        out_shape=[jax.ShapeDtypeStruct((8, T, D), ACT), jax.ShapeDtypeStruct((T, D), F32),
                   jax.ShapeDtypeStruct((T, D), F32), jax.ShapeDtypeStruct((D, D), F32),
                   jax.ShapeDtypeStruct((D, D), F32), jax.ShapeDtypeStruct((D, D), F32),
                   jax.ShapeDtypeStruct((8, D), F32)],
        compiler_params=_cparams(1),
    )(dxo, oa, u2, ya, yb, mout, p, mod, w_a, w_b, w_o)


def _ada_wgrad(cs_all, dmod_cols):
    cs_t = jnp.pad(cs_all.T, ((0, 0), (0, HD - N_DEV)))
    dm = jnp.pad(dmod_cols, ((0, HD - N_DEV), (0, 0)))

    def body(cs_ref, d_ref, out_ref):
        out_ref[...] = jnp.dot(cs_ref[...], d_ref[...], preferred_element_type=F32,
                               precision=lax.Precision.HIGHEST)

    return pl.pallas_call(
        body, name="ada_wgrad", out_shape=jax.ShapeDtypeStruct((D, dmod_cols.shape[1]), F32),
        compiler_params=pltpu.CompilerParams(vmem_limit_bytes=VMEM_LIMIT),
    )(cs_t, dm)


def _adam_math(w, g, m, v):
    m2 = ADAM_B1 * m + (1.0 - ADAM_B1) * g
    v2 = ADAM_B2 * v + (1.0 - ADAM_B2) * (g * g)
    m_hat = m2 / (1.0 - ADAM_B1 ** ADAM_STEP)
    v_hat = v2 / (1.0 - ADAM_B2 ** ADAM_STEP)
    delta = -ADAM_LR * (m_hat / (jnp.sqrt(v_hat) + ADAM_EPS) + ADAM_WD * w)
    return delta, m2, v2


def _adamw(w, m, v, g, name):
    R, C = w.shape
    slots = g.ndim == 3
    n_slots = g.shape[0] if slots else 0
    tr = R
    for cand in (256, 176):
        if R % cand == 0 and R > cand:
            tr = cand
            break

    def body(w_ref, m_ref, v_ref, g_ref, go_ref, d_ref, mo_ref, vo_ref):
        if slots:
            gv = g_ref[0].astype(F32)
            for s in range(1, n_slots):
                gv = gv + g_ref[s].astype(F32)
        else:
            gv = g_ref[...]
        go_ref[...] = gv
        d_ref[...], mo_ref[...], vo_ref[...] = _adam_math(w_ref[...], gv, m_ref[...], v_ref[...])

    tile = pl.BlockSpec((tr, C), lambda i: (i, 0))
    gspec = pl.BlockSpec((n_slots, tr, C), lambda i: (0, i, 0)) if slots else tile
    sds = jax.ShapeDtypeStruct((R, C), F32)
    return pl.pallas_call(
        body, name=name, grid=(R // tr,), in_specs=[tile, tile, tile, gspec], out_specs=[tile] * 4,
        out_shape=[sds] * 4, compiler_params=_cparams(1),
    )(w, m, v, g)


def _sum_slots(pack, name, tr):
    n, R, C = pack.shape

    def body(p_ref, out_ref):
        acc = p_ref[0].astype(F32)
        for s in range(1, n):
            acc = acc + p_ref[s].astype(F32)
        out_ref[...] = acc

    return pl.pallas_call(
        body, name=name, grid=(R // tr,), in_specs=[pl.BlockSpec((n, tr, C), lambda i: (0, i, 0))],
        out_specs=pl.BlockSpec((tr, C), lambda i: (i, 0)), out_shape=jax.ShapeDtypeStruct((R, C), F32),
        compiler_params=_cparams(1))(pack)


def _me():
    return lax.axis_index("x"), lax.axis_index("y"), lax.axis_index("c")


def _peer(r):
    x, y, c = _me()
    px = 1 - x if r & 4 else x
    py = 1 - y if r & 2 else y
    pc = 1 - c if r & 1 else c
    return (px, py, pc), 4 * px + 2 * py + pc


def _small_gather(x_ref, out_ref, send_sems, recv_sems):
    R = x_ref.shape[0]
    mx, my, mc = _me()
    me = 4 * mx + 2 * my + mc
    mine = out_ref.at[pl.ds(pl.multiple_of(me * R, 8), R), :]
    copies = []
    for r in range(1, N_DEV):
        dev, _ = _peer(r)
        copies.append(pltpu.make_async_remote_copy(
            src_ref=x_ref, dst_ref=mine, send_sem=send_sems.at[r - 1], recv_sem=recv_sems.at[r - 1],
            device_id=dev, device_id_type=MESH))
    for cp in copies:
        cp.start()
    mine[...] = x_ref[...]
    for r in range(1, N_DEV):
        dev, idx = _peer(r)
        theirs = out_ref.at[pl.ds(pl.multiple_of(idx * R, 8), R), :]
        pltpu.make_async_remote_copy(
            src_ref=x_ref, dst_ref=theirs, send_sem=send_sems.at[r - 1], recv_sem=recv_sems.at[r - 1],
            device_id=dev, device_id_type=MESH).wait_recv()
    for cp in copies:
        cp.wait_send()


def _prologue(cs, ada_w, ada_b_cols, big):
    n = len(big)
    ncol = ada_w.shape[1]
    big_shape, big_sems = _xchg_specs(big, "gather")

    def body(cs_ref, w_ref, b_ref, *rest):
        big_in, cs_all, mod_all, big_out = rest[:n], rest[n], rest[n + 1], rest[n + 2:2 * n + 2]
        mod_scr, s1, r1, s2, r2 = rest[2 * n + 2:2 * n + 7]
        sems = rest[2 * n + 7:]
        _small_gather(cs_ref, cs_all, s1, r1)
        pick = (lax.broadcasted_iota(jnp.int32, (N_DEV, N_DEV * 8), 1)
                == 8 * lax.broadcasted_iota(jnp.int32, (N_DEV, N_DEV * 8), 0)).astype(F32)
        per_device = jnp.dot(pick, cs_all[...], preferred_element_type=F32, precision=lax.Precision.HIGHEST)
        mod_scr[...] = jnp.dot(per_device, w_ref[...], preferred_element_type=F32,
                               precision=lax.Precision.HIGHEST) + b_ref[...]
        _small_gather(mod_scr, mod_all, s2, r2)
        _xchg_start(big_in, big_out, sems, "gather")
        _xchg_wait(big_in, big_out, sems, "gather")

    vmem = pl.BlockSpec(memory_space=pltpu.VMEM)
    hbm = pl.BlockSpec(memory_space=pl.ANY)
    dma7 = pltpu.SemaphoreType.DMA((N_DEV - 1,))
    out = pl.pallas_call(
        body, name="prologue",
        out_shape=[jax.ShapeDtypeStruct((N_DEV * 8, D), F32), jax.ShapeDtypeStruct((N_DEV * 8, ncol), F32)]
        + big_shape,
        in_specs=[vmem, vmem, vmem] + [hbm] * n, out_specs=[vmem, vmem] + [hbm] * n,
        scratch_shapes=[pltpu.VMEM((8, ncol), F32), dma7, dma7, dma7, dma7] + big_sems,
        compiler_params=pltpu.CompilerParams(vmem_limit_bytes=VMEM_LIMIT),
    )(cs, ada_w, ada_b_cols, *big)
    return out[0], out[1], out[2:]


def _allgather_small(x):
    R, C = x.shape

    def body(x_ref, out_ref, send_sems, recv_sems):
        _small_gather(x_ref, out_ref, send_sems, recv_sems)

    return pl.pallas_call(
        body, name="allgather_small_%dx%d" % (R, C),
        out_shape=jax.ShapeDtypeStruct((N_DEV * R, C), F32),
        in_specs=[pl.BlockSpec(memory_space=pltpu.VMEM)], out_specs=pl.BlockSpec(memory_space=pltpu.VMEM),
        scratch_shapes=[pltpu.SemaphoreType.DMA((N_DEV - 1,)), pltpu.SemaphoreType.DMA((N_DEV - 1,))],
    )(x)


N_CHIP = N_DEV // 2


def _xchg_copies(ins, outs, sems, mode):
    send_sems, recv_sems, local_sems = sems
    mx, my, mc = _me()
    me = 4 * mx + 2 * my + mc
    my_chip = 2 * mx + my
    sibling = _peer(1)[0]

    def rdma(a, r, dev, src, slot):
        k = a * (N_DEV - 1) + r - 1
        return pltpu.make_async_remote_copy(
            src_ref=src, dst_ref=outs[a].at[slot], send_sem=send_sems.at[k], recv_sem=recv_sems.at[k],
            device_id=dev, device_id_type=MESH)

    own, sends, relays, recvs = [], [], [], []
    for a in range(len(ins)):
        if mode == "pair":
            for chip in range(N_CHIP):
                src = ins[a].at[2 * chip + 1 - mc]
                sends.append(rdma(a, chip + 1, sibling, src, chip))
                recvs.append(rdma(a, chip + 1, sibling, src, chip))
            continue
        if mode == "quad":
            own.append(pltpu.make_async_copy(ins[a].at[my_chip], outs[a].at[my_chip], local_sems.at[a]))
            for r in (2, 4, 6):
                dev, idx = _peer(r)
                chip = idx // 2
                sends.append(rdma(a, r, dev, ins[a].at[chip], my_chip))
                recvs.append(rdma(a, r, dev, ins[a].at[chip], chip))
            continue
        gather = mode == "gather"
        own.append(pltpu.make_async_copy(ins[a] if gather else ins[a].at[me], outs[a].at[me], local_sems.at[a]))
        for r in range(1, N_DEV):
            dev, idx = _peer(r)
            if not gather:
                sends.append(rdma(a, r, dev, ins[a].at[idx], me))
                recvs.append(rdma(a, r, dev, ins[a].at[idx], idx))
            elif r == 1:
                sends.append(rdma(a, r, dev, ins[a], me))
                recvs.append(rdma(a, r, dev, ins[a], idx))
            elif r % 2 == 0:
                sends.append(rdma(a, r, dev, ins[a], me))
                relays.append((rdma(a, r, dev, ins[a], idx), rdma(a, r + 1, sibling, outs[a].at[idx], idx)))
            else:
                recvs.append(rdma(a, r, sibling, ins[a], idx))
    return own, sends, relays, recvs


def _xchg_start(ins, outs, sems, mode):
    own, sends, _, _ = _xchg_copies(ins, outs, sems, mode)
    for cp in own + sends:
        cp.start()


def _xchg_wait(ins, outs, sems, mode):
    own, sends, relays, recvs = _xchg_copies(ins, outs, sems, mode)
    for arrival, relay in relays:
        arrival.wait_recv()
        relay.start()
    for cp in recvs:
        cp.wait_recv()
    for cp in own:
        cp.wait()
    for cp in sends + [relay for _, relay in relays]:
        cp.wait_send()


def _xchg_specs(arrays, mode):
    n = len(arrays)
    shape = {"gather": lambda s: (N_DEV,) + s, "scatter": lambda s: s, "pair": lambda s: (N_CHIP,) + s[1:],
             "quad": lambda s: s}[mode]
    out_shape = [jax.ShapeDtypeStruct(shape(a.shape), a.dtype) for a in arrays]
    sems = [pltpu.SemaphoreType.DMA((n * (N_DEV - 1),)), pltpu.SemaphoreType.DMA((n * (N_DEV - 1),)),
            pltpu.SemaphoreType.DMA((n,))]
    return out_shape, sems


def _exchange(arrays, mode, name):
    n = len(arrays)

    def body(*refs):
        _xchg_start(refs[:n], refs[n:2 * n], refs[2 * n:], mode)
        _xchg_wait(refs[:n], refs[n:2 * n], refs[2 * n:], mode)

    out_shape, sems = _xchg_specs(arrays, mode)
    return pl.pallas_call(
        body, name=name, out_shape=out_shape,
        in_specs=[pl.BlockSpec(memory_space=pl.ANY)] * n, out_specs=[pl.BlockSpec(memory_space=pl.ANY)] * n,
        scratch_shapes=sems,
    )(*arrays)


def _gridded(body, carry, *, name, grid, in_specs, out_specs, out_shape, scratch_shapes=(), aliases=None):
    if carry is None:
        return pl.pallas_call(
            body, name=name, grid=grid, in_specs=list(in_specs), out_specs=list(out_specs),
            out_shape=list(out_shape), scratch_shapes=list(scratch_shapes), input_output_aliases=aliases or {},
            compiler_params=_cparams(len(grid)))
    arrays, mode = carry
    n, n_in, n_out, n_scr = len(arrays), len(in_specs), len(out_specs), len(scratch_shapes)
    c_shape, c_sems = _xchg_specs(arrays, mode)

    def wrapped(*refs):
        ins, cin = refs[:n_in], refs[n_in:n_in + n]
        o0 = n_in + n
        outs, cout = refs[o0:o0 + n_out], refs[o0 + n_out:o0 + n_out + n]
        s0 = o0 + n_out + n
        scr, sems = refs[s0:s0 + n_scr], refs[s0 + n_scr:]
        first = pl.program_id(0) == 0
        last = pl.program_id(0) == grid[0] - 1
        for ax in range(1, len(grid)):
            first = first & (pl.program_id(ax) == 0)
            last = last & (pl.program_id(ax) == grid[ax] - 1)

        @pl.when(first)
        def _():
            _xchg_start(cin, cout, sems, mode)

        body(*ins, *outs, *scr)

        @pl.when(last)
        def _():
            _xchg_wait(cin, cout, sems, mode)

    hbm = pl.BlockSpec(memory_space=pl.ANY)
    res = pl.pallas_call(
        wrapped, name=name, grid=grid, in_specs=list(in_specs) + [hbm] * n, out_specs=list(out_specs) + [hbm] * n,
        out_shape=list(out_shape) + c_shape, scratch_shapes=list(scratch_shapes) + c_sems,
        input_output_aliases=aliases or {}, compiler_params=_cparams(len(grid)),
    )
    return lambda *args: res(*args, *arrays)


def _local_step(x, target, mod, small, sh, w1):
    w1_in, w1_out = w1[0].reshape(2, D_FF, D), w1[1].reshape(D_FF, D)
    (x1, a1, b1, f1, h1, h2), (wm_in,) = _ffn_fwd(x, mod, 0, small["norm_ffn1"], w1_in, w1_out, 0.5, "ffn1_fwd",
                                                  ([sh["mix_w_in"]], "gather"), nxt=(small["norm_mix"], 3))
    (p,), (wh_o, wc_o, wm_o, cw) = _mixin_fwd(
        h2, wm_in, ([sh["hgrn_w_o"], sh["conv_w_o"], sh["mix_w_out"], sh["conv_w"]], "gather"))
    wh_o, wc_o, wm_o = wh_o.reshape(D, D), wc_o.reshape(D, D), wm_o.reshape(D, D)
    cw = jnp.pad(cw.transpose(1, 0, 2).reshape(CONV_K, D), ((0, HALO - CONV_K), (0, 0)))
    (o, oa, a_all, s_all), (w2_in,) = _hgrn_fwd(p, small["hgrn_lb"], small["hgrn_g"], ([sh["ffn2_w_in"]], "gather"))
    (u1, u2), (w2_out,) = _conv_fwd(p, cw, small["conv_b"], small["conv_ln_g"], small["conv_ln_b"],
                                    ([sh["ffn2_w_out"]], "gather"))
    w2_in, w2_out = w2_in.reshape(2, D_FF, D), w2_out.reshape(D_FF, D)
    x2, ya, yb, mout = _mixout_fwd(x1, oa, u2, p, mod, 3, wh_o, wc_o, wm_o)
    (x3, a3, b3, f3, h3), _ = _ffn_fwd(x2, mod, 6, small["norm_ffn2"], w2_in, w2_out, 0.5, "ffn2_fwd", None)
    dx3, df3, sm_head = _head(x3, target, small["norm_final"], mod, 8, 0.5)

    (da3, db3, dw2_in, dw2_out), _ = _ffn_bwd_w(h3, df3, a3, b3, w2_out, "ffn2_bwd_w", None)
    rows = lambda t: t.reshape(N_DEV, -1, D).astype(MM)
    (dx2, sm3), (r2_out,) = _ffn_bwd_x(x2, dx3, f3, da3, db3, mod, 6, small["norm_ffn2"], w2_in, 0.5, "ffn2_bwd_x",
                                       ([rows(dw2_out)], "scatter"))
    dp, doa, du2, dwh_o, dwc_o, dwm_o, sm_mo = _mixout_bwd(dx2, oa, u2, ya, yb, mout, p, mod, 3, wh_o, wc_o, wm_o)
    (dp, dcw, sm_cv), (r2_in,) = _conv_bwd(p, u1, du2, cw, small["conv_ln_g"], small["conv_ln_b"], dp,
                                           ([rows(dw2_in)], "scatter"))
    (dp, sm_hg), _ = _hgrn_bwd(p, o, a_all, s_all, doa, small["hgrn_lb"], small["hgrn_g"], dp, None)
    (dx1, dwm_in, sm2, df1), (rh_o, rc_o, rm_o, rcw) = _mixin_bwd(
        x1, h2, dx2, dp, mod, 3, small["norm_mix"], wm_in, 2, 0.5,
        ([rows(dwh_o), rows(dwc_o), rows(dwm_o), dcw[:CONV_K].reshape(CONV_K, N_DEV, -1).transpose(1, 0, 2)],
         "scatter"))
    (da1, db1, dw1_in, dw1_out), (rm_in,) = _ffn_bwd_w(h1, df1, a1, b1, w1_out, "ffn1_bwd_w",
                                                      (_pair_reduce([dwm_in], "pair_mix"), "quad"))
    (dx0, sm1), (r1_in, r1_out) = _ffn_bwd_x(
        x, dx1, f1, da1, db1, mod, 0, small["norm_ffn1"], w1_in, 0.5, "ffn1_bwd_x",
        (_pair_reduce([rows(dw1_in), rows(dw1_out)], "pair_ffn1"), "quad"))

    dmod = jnp.concatenate([sm1[0:3], sm2[0:2], sm_mo[2:3], sm3[0:3]], axis=0)
    gsmall = dict(norm_ffn1=sm1[3:4], norm_mix=sm2[3:4], lb0=sm_hg[0:1], hgrn_g=sm_hg[1:2], conv_b=sm_cv[0:1],
                  conv_ln_g=sm_cv[1:2], conv_ln_b=sm_cv[2:3], norm_ffn2=sm3[3:4], norm_final=sm_head[0:1])
    recv = dict(ffn1_w_in=r1_in, ffn1_w_out=r1_out, mix_w_in=rm_in, hgrn_w_o=rh_o, conv_w=rcw, conv_w_o=rc_o,
                mix_w_out=rm_o, ffn2_w_in=r2_in, ffn2_w_out=r2_out)
    return sm_head[1, 0], dx0, dmod, gsmall, recv


def _pair_add(mine, theirs, core, name):
    _, R, C = theirs.shape

    def body(core_ref, a_ref, b_ref, out_ref):
        del core_ref
        out_ref[0] = (a_ref[0, 0].astype(F32) + b_ref[0].astype(F32)).astype(out_ref.dtype)

    blk = pl.BlockSpec((1, R, C), lambda s, core_ref: (s, 0, 0))
    grid_spec = pltpu.PrefetchScalarGridSpec(
        num_scalar_prefetch=1, grid=(N_CHIP,),
        in_specs=[pl.BlockSpec((1, 1, R, C), lambda s, core_ref: (s, core_ref[0], 0, 0)), blk], out_specs=blk)
    return pl.pallas_call(body, name=name, grid_spec=grid_spec,
                          out_shape=jax.ShapeDtypeStruct(theirs.shape, mine.dtype), compiler_params=_cparams(1),
                          )(core, mine.reshape(N_CHIP, 2, R, C), theirs)


def _pair_reduce(arrays, name):
    theirs = _exchange(arrays, "pair", name)
    core = lax.axis_index("c").astype(jnp.int32).reshape(1)
    return [_pair_add(a, t, core, "%s_add%d" % (name, i)) for i, (a, t) in enumerate(zip(arrays, theirs))]


SMALL_ORDER = ("norm_ffn1", "norm_mix", "lb0", "hgrn_g", "conv_b", "conv_ln_g", "conv_ln_b", "norm_ffn2",
               "norm_final")
PACK_ROWS = 24


def kernel(x, c, ada_w, ada_b, norm_ffn1, ffn1_w_in, ffn1_w_out, norm_mix, mix_w_in, hgrn_lb, hgrn_g, hgrn_w_o, conv_w, conv_b, conv_ln_g, conv_ln_b, conv_w_o, mix_w_out, norm_ffn2, ffn2_w_in, ffn2_w_out, norm_final, loss_target, m_ada_w, m_ada_b, m_norm_ffn1, m_ffn1_w_in, m_ffn1_w_out, m_norm_mix, m_mix_w_in, m_hgrn_lb, m_hgrn_g, m_hgrn_w_o, m_conv_w, m_conv_b, m_conv_ln_g, m_conv_ln_b, m_conv_w_o, m_mix_w_out, m_norm_ffn2, m_ffn2_w_in, m_ffn2_w_out, m_norm_final, v_ada_w, v_ada_b, v_norm_ffn1, v_ffn1_w_in, v_ffn1_w_out, v_norm_mix, v_mix_w_in, v_hgrn_lb, v_hgrn_g, v_hgrn_w_o, v_conv_w, v_conv_b, v_conv_ln_g, v_conv_ln_b, v_conv_w_o, v_mix_w_out, v_norm_ffn2, v_ffn2_w_in, v_ffn2_w_out, v_norm_final):
    mx, my, mc = _me()
    me = 4 * mx + 2 * my + mc
    ncol = ada_w.shape[2]

    sh = dict(ffn1_w_out=ffn1_w_out, mix_w_in=mix_w_in, hgrn_w_o=hgrn_w_o, conv_w_o=conv_w_o,
              mix_w_out=mix_w_out, ffn2_w_out=ffn2_w_out)
    sh = {n: w[0].astype(MM) for n, w in sh.items()}
    sh["ffn1_w_in"] = ffn1_w_in[0].T.astype(MM)
    sh["ffn2_w_in"] = ffn2_w_in[0].T.astype(MM)
    sh["conv_w"] = conv_w[0]
    small = dict(norm_ffn1=norm_ffn1, norm_mix=norm_mix, hgrn_lb=hgrn_lb, hgrn_g=hgrn_g, conv_b=conv_b,
                 conv_ln_g=conv_ln_g, conv_ln_b=conv_ln_b, norm_ffn2=norm_ffn2, norm_final=norm_final.reshape(1, D))

    cs = jnp.broadcast_to(c * jax.nn.sigmoid(c), (8, D))
    ada_b_cols = lax.dynamic_slice(ada_b, (0, me * ncol), (1, ncol))
    cs_all, mod_all, w1 = _prologue(cs, ada_w[0], ada_b_cols, [sh["ffn1_w_in"], sh["ffn1_w_out"]])
    cs_all = cs_all.reshape(N_DEV, 8, D)[:, 0, :]
    mod = lax.dynamic_index_in_dim(mod_all.reshape(N_DEV, N_DEV, ncol), me, axis=1, keepdims=False).reshape(9, D)

    loss_local, dx, dmod, gsmall, recv = _local_step(x[0], loss_target[0], mod, small, sh, w1)
    loss = lax.psum(loss_local, ("x", "y", "c"))

    pack = jnp.concatenate([dmod] + [gsmall[n] for n in SMALL_ORDER]
                           + [jnp.zeros((PACK_ROWS - 9 - len(SMALL_ORDER), D), F32)], axis=0)
    pack_all = _allgather_small(pack).reshape(N_DEV, PACK_ROWS, D)
    tot = _sum_slots(pack_all, "sum_small", PACK_ROWS)
    gs = {n: tot[9 + i:10 + i] for i, n in enumerate(SMALL_ORDER)}
    dmod_all = pack_all[:, 0:9, :].reshape(N_DEV, 9 * D)
    g_ada_b = tot[0:9].reshape(1, 9 * D)
    g_ada_w = _ada_wgrad(cs_all, lax.dynamic_slice(dmod_all, (0, me * ncol), (N_DEV, ncol)))
    z = hgrn_lb.astype(F32)
    p0 = jax.nn.sigmoid(z[0:1] - z[1:2])
    dz0 = p0 * (1.0 - p0) * gs["lb0"]
    g_hgrn_lb = jnp.concatenate([dz0, -dz0], axis=0)

    res = {}
    res["ada_w"] = _adamw(ada_w[0], m_ada_w[0], v_ada_w[0], g_ada_w, "adamw_ada_w")
    big = dict(ffn1_w_in=(ffn1_w_in, m_ffn1_w_in, v_ffn1_w_in), ffn1_w_out=(ffn1_w_out, m_ffn1_w_out, v_ffn1_w_out),
               mix_w_in=(mix_w_in, m_mix_w_in, v_mix_w_in), hgrn_w_o=(hgrn_w_o, m_hgrn_w_o, v_hgrn_w_o),
               conv_w=(conv_w, m_conv_w, v_conv_w), conv_w_o=(conv_w_o, m_conv_w_o, v_conv_w_o),
               mix_w_out=(mix_w_out, m_mix_w_out, v_mix_w_out), ffn2_w_in=(ffn2_w_in, m_ffn2_w_in, v_ffn2_w_in),
               ffn2_w_out=(ffn2_w_out, m_ffn2_w_out, v_ffn2_w_out))
    for n, (w, m, v) in big.items():
        g = recv[n]
        if n in ("ffn1_w_in", "ffn2_w_in"):
            g = _sum_slots(g, "sum_" + n, g.shape[1] // 4).T
        res[n] = _adamw(w[0], m[0], v[0], g, "adamw_" + n)
    sm_names = ("ada_b", "norm_ffn1", "norm_mix", "hgrn_lb", "hgrn_g", "conv_b", "conv_ln_g", "conv_ln_b",
                "norm_ffn2", "norm_final")
    sm_w = dict(ada_b=(ada_b, m_ada_b, v_ada_b), norm_ffn1=(norm_ffn1, m_norm_ffn1, v_norm_ffn1),
                norm_mix=(norm_mix, m_norm_mix, v_norm_mix), hgrn_lb=(hgrn_lb, m_hgrn_lb, v_hgrn_lb),
                hgrn_g=(hgrn_g, m_hgrn_g, v_hgrn_g), conv_b=(conv_b, m_conv_b, v_conv_b),
                conv_ln_g=(conv_ln_g, m_conv_ln_g, v_conv_ln_g), conv_ln_b=(conv_ln_b, m_conv_ln_b, v_conv_ln_b),
                norm_ffn2=(norm_ffn2, m_norm_ffn2, v_norm_ffn2), norm_final=(norm_final, m_norm_final, v_norm_final))
    sm_g = dict(gs, ada_b=g_ada_b, hgrn_lb=g_hgrn_lb)
    rows = {n: sm_w[n][0].size // D for n in sm_names}
    n_rows = sum(rows.values())
    pad = (-n_rows) % 8
    stack = lambda parts: jnp.concatenate([q.reshape(-1, D) for q in parts] + [jnp.ones((pad, D), F32)], axis=0)
    st = _adamw(stack([sm_w[n][0] for n in sm_names]), stack([sm_w[n][1] for n in sm_names]),
                stack([sm_w[n][2] for n in sm_names]), stack([sm_g[n] for n in sm_names]), "adamw_small")
    off = 0
    for n in sm_names:
        res[n] = tuple(t[off:off + rows[n]].reshape(sm_w[n][0].shape) for t in st)
        off += rows[n]

    order = ("ada_w", "ada_b", "norm_ffn1", "ffn1_w_in", "ffn1_w_out", "norm_mix", "mix_w_in", "hgrn_lb", "hgrn_g",
             "hgrn_w_o", "conv_w", "conv_b", "conv_ln_g", "conv_ln_b", "conv_w_o", "mix_w_out", "norm_ffn2",
             "ffn2_w_in", "ffn2_w_out", "norm_final")
    lead = lambda n, t: t[None] if n in big or n == "ada_w" else t
    outs = [loss, dx[None]]
    for j in range(4):
        outs += [lead(n, res[n][j]) for n in order]
    return tuple(outs)
```

```python
import jax
import jax.numpy as jnp
from jax import lax
from jax.experimental import pallas as pl
from jax.experimental.pallas import tpu as pltpu

F32 = jnp.float32
MM = jnp.bfloat16
ACT = jnp.bfloat16

D = 1024
D_FF = 2816
HEADS = 8
HD = 128
CHUNK = 64
SUB = 16
NSUB = CHUNK // SUB
HGRN_BLOCK = 1024
SAFE_EXP = 60.0
CONV_K = 31
HALO = 32
EPS = 1e-6
N_DEV = 8
NEG = -1e30
Q_SCALE = HD ** -0.5

ADAM_LR = 0.001
ADAM_B1 = 0.9
ADAM_B2 = 0.999
ADAM_EPS = 1e-08
ADAM_WD = 0.01
ADAM_STEP = 10

V7X_VMEM_BYTES = 64 * 1024 * 1024
VMEM_LIMIT = V7X_VMEM_BYTES - 4 * 1024 * 1024
MESH = pl.DeviceIdType.MESH


def _cparams(n_axes):
    return pltpu.CompilerParams(dimension_semantics=("arbitrary",) * n_axes, vmem_limit_bytes=VMEM_LIMIT)


def _mm(a, b):
    return lax.dot_general(a.astype(MM), b.astype(MM), (((1,), (0,)), ((), ())), preferred_element_type=F32)


def _mm_nt(a, b):
    return lax.dot_general(a.astype(MM), b.astype(MM), (((1,), (1,)), ((), ())), preferred_element_type=F32)


def _mm_tn(a, b):
    return lax.dot_general(a.astype(MM), b.astype(MM), (((0,), (0,)), ((), ())), preferred_element_type=F32)


def _sig(x):
    return 1.0 / (1.0 + jnp.exp(-x))


def _colsum(x):
    return jnp.sum(x, axis=0, keepdims=True)


def _rowmean(x):
    return jnp.mean(x, axis=-1, keepdims=True)


def _modnorm_fwd(xv, g, sh, sc):
    r = lax.rsqrt(_rowmean(xv * xv) + EPS)
    xh = xv * r
    n = xh * g
    return n * (1.0 + sc) + sh, xh, n, r


def _modnorm_bwd(dh, xh, n, r, g, sc):
    dsc = _colsum(dh * n)
    dsh = _colsum(dh)
    dn = dh * (1.0 + sc)
    dg = _colsum(dn * xh)
    dxh = dn * g
    dx = r * (dxh - xh * _rowmean(dxh * xh))
    return dx, dsh, dsc, dg


def _ffn_fwd(x, mod, mo, gnorm, w_in_t, w_out, res, name, carry, nxt=None):
    T = x.shape[0]
    tm = min(512, T)
    tn = D_FF // 2

    def body(x_ref, mod_ref, g_ref, wi_ref, wo_ref, *rest):
        if nxt is None:
            xo_ref, a_ref, b_ref, f_ref, h_ref = rest
        else:
            gn_ref, xo_ref, a_ref, b_ref, f_ref, h_ref, hn_ref = rest
        xv = x_ref[...]
        h, _, _, _ = _modnorm_fwd(xv, g_ref[...], mod_ref[mo:mo + 1, :], mod_ref[mo + 1:mo + 2, :])
        h = h.astype(ACT)
        h_ref[...] = h
        f = None
        for c0 in range(0, D_FF, tn):
            a = _mm_nt(h, wi_ref[0, c0:c0 + tn, :])
            b = _mm_nt(h, wi_ref[1, c0:c0 + tn, :])
            a_ref[:, c0:c0 + tn] = a.astype(ACT)
            b_ref[:, c0:c0 + tn] = b.astype(ACT)
            part = _mm(a * _sig(a) * b, wo_ref[c0:c0 + tn, :])
            f = part if f is None else f + part
        f_ref[...] = f
        xo = xv + res * mod_ref[mo + 2:mo + 3, :] * f
        xo_ref[...] = xo
        if nxt is not None:
            hn, _, _, _ = _modnorm_fwd(xo, gn_ref[...], mod_ref[nxt[1]:nxt[1] + 1, :], mod_ref[nxt[1] + 1:nxt[1] + 2, :])
            hn_ref[...] = hn.astype(ACT)

    tile = pl.BlockSpec((tm, D), lambda i: (i, 0))
    wide = pl.BlockSpec((tm, D_FF), lambda i: (i, 0))
    row = pl.BlockSpec((1, D), lambda i: (0, 0))
    n_out = 5 if nxt is None else 6
    out = _gridded(
        body, carry, name=name, grid=(T // tm,),
        in_specs=[
            tile,
            pl.BlockSpec((9, D), lambda i: (0, 0)),
            row,
            pl.BlockSpec((2, D_FF, D), lambda i: (0, 0, 0), pipeline_mode=pl.Buffered(1)),
            pl.BlockSpec((D_FF, D), lambda i: (0, 0), pipeline_mode=pl.Buffered(1)),
        ] + ([] if nxt is None else [row]),
        out_specs=[tile, wide, wide, tile, tile] + ([] if nxt is None else [tile]),
        out_shape=[
            jax.ShapeDtypeStruct((T, D), F32),
            jax.ShapeDtypeStruct((T, D_FF), ACT),
            jax.ShapeDtypeStruct((T, D_FF), ACT),
            jax.ShapeDtypeStruct((T, D), F32),
            jax.ShapeDtypeStruct((T, D), ACT),
        ] + ([] if nxt is None else [jax.ShapeDtypeStruct((T, D), ACT)]),
    )(*((x, mod, gnorm, w_in_t, w_out) + (() if nxt is None else (nxt[0],))))
    return out[:n_out], out[n_out:]


def _ffn_bwd_w(h, df, a, b, w_out, name, carry):
    T = h.shape[0]
    tm = min(2048, T)
    ni = T // tm
    tn = 256
    nj = D_FF // tn

    def body(h_ref, df_ref, a_ref, b_ref, wo_ref, da_ref, db_ref, dwi_ref, dwo_ref, acc_i, acc_o):
        i = pl.program_id(1)

        @pl.when(i == 0)
        def _():
            acc_i[...] = jnp.zeros_like(acc_i)
            acc_o[...] = jnp.zeros_like(acc_o)

        hb = h_ref[...]
        df = df_ref[...]
        av = a_ref[...].astype(F32)
        bv = b_ref[...].astype(F32)
        sg = _sig(av)
        sa = av * sg
        s = (sa * bv).astype(MM)
        ds = _mm_nt(df, wo_ref[...])
        da = (ds * bv * sg * (1.0 + av * (1.0 - sg))).astype(MM)
        db = (ds * sa).astype(MM)
        da_ref[...] = da
        db_ref[...] = db
        acc_o[...] += _mm_tn(s, df)
        acc_i[0] += _mm_tn(da, hb)
        acc_i[1] += _mm_tn(db, hb)

        @pl.when(i == ni - 1)
        def _():
            dwi_ref[...] = acc_i[...].astype(MM)
            dwo_ref[...] = acc_o[...].astype(MM)

    out = _gridded(
        body, carry, name=name, grid=(nj, ni),
        in_specs=[
            pl.BlockSpec((tm, D), lambda j, i: (i, 0)),
            pl.BlockSpec((tm, D), lambda j, i: (i, 0)),
            pl.BlockSpec((tm, tn), lambda j, i: (i, j)),
            pl.BlockSpec((tm, tn), lambda j, i: (i, j)),
            pl.BlockSpec((tn, D), lambda j, i: (j, 0)),
        ],
        out_specs=[
            pl.BlockSpec((tm, tn), lambda j, i: (i, j)),
            pl.BlockSpec((tm, tn), lambda j, i: (i, j)),
            pl.BlockSpec((2, tn, D), lambda j, i: (0, j, 0)),
            pl.BlockSpec((tn, D), lambda j, i: (j, 0)),
        ],
        out_shape=[
            jax.ShapeDtypeStruct((T, D_FF), MM),
            jax.ShapeDtypeStruct((T, D_FF), MM),
            jax.ShapeDtypeStruct((2, D_FF, D), MM),
            jax.ShapeDtypeStruct((D_FF, D), MM),
        ],
        scratch_shapes=[pltpu.VMEM((2, tn, D), F32), pltpu.VMEM((tn, D), F32)],
    )(h, df, a, b, w_out)
    return out[:4], out[4:]


def _ffn_bwd_x(x, dxo, f, da, db, mod, mo, gnorm, w_in_t, res, name, carry):
    T = x.shape[0]
    tm = min(512, T)
    ni = T // tm
    tn = D_FF // 2
    nj = D_FF // tn

    def body(x_ref, dxo_ref, f_ref, da_ref, db_ref, mod_ref, g_ref, wi_ref, dx_ref, sm_ref, dh_scr):
        j = pl.program_id(0)
        i = pl.program_id(1)

        @pl.when((j == 0) & (i == 0))
        def _():
            sm_ref[...] = jnp.zeros_like(sm_ref)

        @pl.when(j == 0)
        def _():
            dh_scr[i] = jnp.zeros((tm, D), F32)

        dh_scr[i] += _mm(da_ref[...], wi_ref[0]) + _mm(db_ref[...], wi_ref[1])

        @pl.when(j == nj - 1)
        def _():
            sc = mod_ref[mo + 1:mo + 2, :]
            _, xh, n, r = _modnorm_fwd(x_ref[...], g_ref[...], mod_ref[mo:mo + 1, :], sc)
            dxn, dsh, dsc, dg = _modnorm_bwd(dh_scr[i], xh, n, r, g_ref[...], sc)
            dxo_v = dxo_ref[...]
            dx_ref[...] = dxo_v + dxn
            sm_ref[0:1, :] += dsh
            sm_ref[1:2, :] += dsc
            sm_ref[2:3, :] += _colsum(dxo_v * f_ref[...]) * res
            sm_ref[3:4, :] += dg

    last = pl.BlockSpec((tm, D), lambda j, i: (jnp.where(j == nj - 1, i, 0), 0))
    out = _gridded(
        body, carry, name=name, grid=(nj, ni),
        in_specs=[last, last, last,
                  pl.BlockSpec((tm, tn), lambda j, i: (i, j)), pl.BlockSpec((tm, tn), lambda j, i: (i, j)),
                  pl.BlockSpec((9, D), lambda j, i: (0, 0)), pl.BlockSpec((1, D), lambda j, i: (0, 0)),
                  pl.BlockSpec((2, tn, D), lambda j, i: (0, j, 0))],
        out_specs=[last, pl.BlockSpec((8, D), lambda j, i: (0, 0))],
        out_shape=[jax.ShapeDtypeStruct((T, D), F32), jax.ShapeDtypeStruct((8, D), F32)],
        scratch_shapes=[pltpu.VMEM((ni, tm, D), F32)],
    )(x, dxo, f, da, db, mod, gnorm, w_in_t)
    return out[:2], out[2:]


def _head(x, target, gfin, mod, gate_row, res):
    T = x.shape[0]
    tm = min(512, T)
    ni = T // tm

    def body(x_ref, t_ref, g_ref, mod_ref, dx_ref, df_ref, sm_ref):
        i = pl.program_id(0)

        @pl.when(i == 0)
        def _():
            sm_ref[...] = jnp.zeros_like(sm_ref)

        xv = x_ref[...]
        g = g_ref[...]
        r = lax.rsqrt(_rowmean(xv * xv) + EPS)
        xh = xv * r
        e = xh * g - t_ref[...]
        sm_ref[1:2, :] += _colsum(e * e) * (0.5 / D)
        dy = e * (1.0 / D)
        sm_ref[0:1, :] += _colsum(dy * xh)
        dxh = dy * g
        dx = r * (dxh - xh * _rowmean(dxh * xh))
        dx_ref[...] = dx
        df_ref[...] = (res * mod_ref[gate_row:gate_row + 1, :] * dx).astype(MM)

        @pl.when(i == ni - 1)
        def _():
            sm_ref[1:2, :] = jnp.broadcast_to(jnp.sum(sm_ref[1:2, :], axis=-1, keepdims=True), (1, D))

    tile = pl.BlockSpec((tm, D), lambda i: (i, 0))
    return pl.pallas_call(
        body, name="head_loss", grid=(ni,),
        in_specs=[tile, tile, pl.BlockSpec((1, D), lambda i: (0, 0)), pl.BlockSpec((9, D), lambda i: (0, 0))],
        out_specs=[tile, tile, pl.BlockSpec((8, D), lambda i: (0, 0))],
        out_shape=[jax.ShapeDtypeStruct((T, D), F32), jax.ShapeDtypeStruct((T, D), MM),
                   jax.ShapeDtypeStruct((8, D), F32)],
        compiler_params=_cparams(1),
    )(x, target, gfin, mod)


def _mixin_fwd(h, w, carry):
    T = h.shape[0]
    tm = min(2048, T)
    ni = T // tm

    def body(h_ref, w_ref, p_ref, h_all):
        i = pl.program_id(1)

        @pl.when(pl.program_id(0) == 0)
        def _():
            h_all[i] = h_ref[...]

        p_ref[0] = _mm(h_all[i], w_ref[0])

    first = lambda k, i: (jnp.where(k == 0, i, ni - 1), 0)
    out = _gridded(
        body, carry, name="mixin_fwd", grid=(8, ni),
        in_specs=[pl.BlockSpec((tm, D), first), pl.BlockSpec((1, D, D), lambda k, i: (k, 0, 0))],
        out_specs=[pl.BlockSpec((1, tm, D), lambda k, i: (k, i, 0))],
        out_shape=[jax.ShapeDtypeStruct((8, T, D), F32)],
        scratch_shapes=[pltpu.VMEM((ni, tm, D), ACT)],
    )(h, w)
    return out[:1], out[1:]


def _mixin_bwd(x, h, dxo, dp, mod, mo, gnorm, w, next_gate, next_res, carry):
    T = x.shape[0]
    tm = min(512, T)
    ni = T // tm

    def body(x_ref, h_ref, dxo_ref, dp_ref, mod_ref, g_ref, w_ref, dx_ref, dw_ref, sm_ref, df_ref, dh_scr, acc):
        k = pl.program_id(0)
        i = pl.program_id(1)

        @pl.when(i == 0)
        def _():
            acc[...] = jnp.zeros_like(acc)

        @pl.when(k == 0)
        def _():
            dh_scr[i] = jnp.zeros((tm, D), F32)

        @pl.when((k == 0) & (i == 0))
        def _():
            sm_ref[...] = jnp.zeros_like(sm_ref)

        dpk = dp_ref[0].astype(MM)
        acc[...] += _mm_tn(h_ref[...], dpk)
        dh_scr[i] += _mm_nt(dpk, w_ref[0])

        @pl.when(i == ni - 1)
        def _():
            dw_ref[0] = acc[...].astype(MM)

        @pl.when(k == 7)
        def _():
            sc = mod_ref[mo + 1:mo + 2, :]
            _, xh, n, r = _modnorm_fwd(x_ref[...], g_ref[...], mod_ref[mo:mo + 1, :], sc)
            dxn, dsh, dsc, dg = _modnorm_bwd(dh_scr[i], xh, n, r, g_ref[...], sc)
            dx = dxo_ref[...] + dxn
            dx_ref[...] = dx
            df_ref[...] = (next_res * mod_ref[next_gate:next_gate + 1, :] * dx).astype(MM)
            sm_ref[0:1, :] += dsh
            sm_ref[1:2, :] += dsc
            sm_ref[3:4, :] += dg

    last = pl.BlockSpec((tm, D), lambda k, i: (jnp.where(k == 7, i, 0), 0))
    out = _gridded(
        body, carry, name="mixin_bwd", grid=(8, ni),
        in_specs=[pl.BlockSpec((tm, D), lambda k, i: (jnp.where(k == 7, i, 0), 0)),
                  pl.BlockSpec((tm, D), lambda k, i: (i, 0)),
                  pl.BlockSpec((tm, D), lambda k, i: (jnp.where(k == 7, i, 0), 0)),
                  pl.BlockSpec((1, tm, D), lambda k, i: (k, i, 0)), pl.BlockSpec((9, D), lambda k, i: (0, 0)),
                  pl.BlockSpec((1, D), lambda k, i: (0, 0)), pl.BlockSpec((1, D, D), lambda k, i: (k, 0, 0))],
        out_specs=[last, pl.BlockSpec((1, D, D), lambda k, i: (k, 0, 0)), pl.BlockSpec((8, D), lambda k, i: (0, 0)),
                   last],
        out_shape=[jax.ShapeDtypeStruct((T, D), F32), jax.ShapeDtypeStruct((8, D, D), MM),
                   jax.ShapeDtypeStruct((8, D), F32), jax.ShapeDtypeStruct((T, D), MM)],
        scratch_shapes=[pltpu.VMEM((ni, tm, D), F32), pltpu.VMEM((D, D), F32)],
    )(x, h, dxo, dp, mod, gnorm, w)
    return out[:4], out[4:]


def _hgrn_consts():
    rows = jnp.arange(SUB * HD) // HD
    e = (rows[:, None] == jnp.arange(HD)[None, :]).astype(MM)
    return e, e.T


def _rows_bcast(ref, cb, first, n):
    parts = [jnp.broadcast_to(ref[pl.ds(c * CHUNK + first, 1), :], (n, HD)) for c in range(cb // CHUNK)]
    return jnp.concatenate(parts, axis=0)


def _hgrn_pre_fused(p_ref, lb_ref, b_scr, q_scr, k_scr, qe_scr, kf_scr, kd_scr, cb):
    z = lb_ref[...]
    lb = _sig(z[0:1, :] - z[1:2, :])
    tl = lax.broadcasted_iota(jnp.int32, (CHUNK, HD), 0)

    def chunk(c, worst):
        r0 = pl.multiple_of(c * CHUNK, CHUNK)
        rs = pl.ds(r0, CHUNK)
        qr = p_ref[0, rs, :]
        q = qr * _sig(qr) * Q_SCALE
        fg = lb + (1.0 - lb) * _sig(p_ref[1, rs, :])
        k = 1.0 - fg
        bc = jnp.log(fg)
        sh = 1
        while sh < CHUNK:
            bc = bc + jnp.where(tl >= sh, pltpu.roll(bc, sh, 0), 0.0)
            sh *= 2
        b_scr[rs, :] = bc
        q_scr[rs, :] = q
        k_scr[rs, :] = k
        qe_scr[rs, :] = (q * jnp.exp(bc)).astype(MM)
        kf_scr[rs, :] = (k * jnp.exp(jnp.minimum(-bc, SAFE_EXP))).astype(MM)
        kd_scr[rs, :] = (k * jnp.exp(b_scr[pl.ds(r0 + CHUNK - 1, 1), :] - bc)).astype(MM)
        return jnp.maximum(worst, -bc)

    worst = lax.fori_loop(0, cb // CHUNK, chunk, jnp.zeros((CHUNK, HD), F32))
    return jnp.max(worst) < SAFE_EXP


def _hgrn_sub(pre, b_scr, cb):
    bc, tl, q, k = pre["b"], pre["tl"], pre["q"], pre["k"]
    br = [None] + [_rows_bcast(b_scr, cb, SUB * i - 1, CHUNK) for i in range(1, NSUB)]
    sb = tl // SUB
    bref = jnp.where(sb == 0, bc, jnp.where(sb == 1, br[1], jnp.where(sb == 2, br[2], br[3])))
    eqo = jnp.exp(bc - bref)
    eko = [None] + [jnp.exp(jnp.where(tl < SUB * i, br[i] - bc, NEG)) for i in range(1, NSUB)]
    return dict(eqo=eqo, eko=eko, qo=q * eqo, ko=[None] + [k * eko[i] for i in range(1, NSUB)])


def _pad_rows(x):
    return jnp.concatenate([x, jnp.zeros_like(x)], axis=0)


def _by_subblock(sbc, parts):
    out = jnp.zeros_like(parts[1])
    for i in range(1, NSUB):
        out = jnp.where(sbc == i, parts[i], out)
    return out


def _hgrn_fwd(p, hgrn_lb, hgrn_g, carry):
    T = p.shape[1]
    cb = min(HGRN_BLOCK, T)
    nch = cb // CHUNK
    ncb = T // cb
    e_mat, _ = _hgrn_consts()

    def body(p_ref, lb_ref, g_ref, e_ref, o_ref, oa_ref, a_ref, s_ref, st_scr, q_scr, k_scr, b_scr, z_scr, ad_scr,
             qe_scr, kf_scr, kd_scr):
        @pl.when(pl.program_id(1) == 0)
        def _():
            st_scr[...] = jnp.zeros_like(st_scr)

        safe = _hgrn_pre_fused(p_ref, lb_ref, b_scr, q_scr, k_scr, qe_scr, kf_scr, kd_scr, cb)
        chunks = [slice(c * CHUNK, (c + 1) * CHUNK) for c in range(nch)]
        row_i = lax.broadcasted_iota(jnp.int32, (CHUNK, HD), 0)
        lane_i = lax.broadcasted_iota(jnp.int32, (CHUNK, HD), 1)
        sbc = row_i // SUB
        causal = lane_i <= row_i

        @pl.when(safe)
        def _():
            for rs in chunks:
                ad_scr[rs, :] = jnp.where(causal, _mm_nt(qe_scr[rs, :], _pad_rows(kf_scr[rs, :])), 0.0)

        @pl.when(jnp.logical_not(safe))
        def _():
            tl = lax.broadcasted_iota(jnp.int32, (cb, HD), 0) % CHUNK
            sub = _hgrn_sub(dict(b=b_scr[...], tl=tl, q=q_scr[...], k=k_scr[...]), b_scr, cb)
            ti = lax.broadcasted_iota(jnp.int32, (SUB, HD), 0)

            def zbody(c, carry):
                for i in range(NSUB):
                    r0 = pl.multiple_of(c * CHUNK + SUB * i, SUB)
                    qi = q_scr[pl.ds(r0, SUB), :]
                    bi = b_scr[pl.ds(r0, SUB), :]
                    for s in range(SUB):
                        krow = k_scr[pl.ds(r0 + s, 1), :]
                        brow = b_scr[pl.ds(r0 + s, 1), :]
                        if s < 8:
                            zz = qi * krow * jnp.exp(jnp.where(ti >= s, bi - brow, NEG))
                        else:
                            lo = qi[8:] * krow * jnp.exp(jnp.where(ti[8:] >= s, bi[8:] - brow, NEG))
                            zz = jnp.concatenate([jnp.zeros((8, HD), F32), lo], axis=0)
                        z_scr[i, pl.ds(pl.multiple_of(c * SUB, SUB), SUB), s * HD:(s + 1) * HD] = zz.astype(MM)
                return carry

            lax.fori_loop(0, nch, zbody, 0)
            adiag = [_mm(z_scr[i], e_ref[...]) for i in range(NSUB)]
            offs = [[_mm_nt(sub["qo"][rs], _pad_rows(sub["ko"][i][rs])) for i in range(1, NSUB)] for rs in chunks]
            for c, rs in enumerate(chunks):
                dparts = []
                for i in range(NSUB):
                    blk = adiag[i][c * SUB:(c + 1) * SUB]
                    dparts.append(blk if i == 0 else pltpu.roll(blk, SUB * i, 1))
                ad_scr[rs, :] = _by_subblock(sbc, [None] + offs[c]) + jnp.concatenate(dparts, axis=0)

        kv = [_mm_tn(p_ref[2, rs, :], kd_scr[rs, :]) for rs in chunks]
        a_ref[0] = ad_scr[...]
        o_intra = [_mm(ad_scr[rs, :], _pad_rows(p_ref[2, rs, :])) for rs in chunks]
        states = []
        st = st_scr[...]
        for c in range(nch):
            states.append(st)
            st = st * jnp.exp(b_scr[pl.ds(c * CHUNK + CHUNK - 1, 1), :]) + kv[c]
        st_scr[...] = st
        g = g_ref[...]
        for c, rs in enumerate(chunks):
            s_ref[0, c] = states[c]
            o = o_intra[c] + _mm_nt(qe_scr[rs, :], states[c])
            o_ref[rs, :] = o
            og = p_ref[3, rs, :]
            oa_ref[rs, :] = (o * lax.rsqrt(_rowmean(o * o) + EPS) * g * og * _sig(og)).astype(ACT)

    out = _gridded(
        body, carry, name="hgrn_fwd", grid=(HEADS, ncb),
        in_specs=[pl.BlockSpec((4, cb, HD), lambda h, c: (0, c, h)),
                  pl.BlockSpec((2, HD), lambda h, c: (0, h)),
                  pl.BlockSpec((1, HD), lambda h, c: (0, h)),
                  pl.BlockSpec((SUB * HD, HD), lambda h, c: (0, 0))],
        out_specs=[pl.BlockSpec((cb, HD), lambda h, c: (c, h)),
                   pl.BlockSpec((cb, HD), lambda h, c: (c, h)),
                   pl.BlockSpec((1, cb, HD), lambda h, c: (h, c, 0)),
                   pl.BlockSpec((1, nch, HD, HD), lambda h, c: (h, c, 0, 0))],
        out_shape=[jax.ShapeDtypeStruct((T, D), F32), jax.ShapeDtypeStruct((T, D), ACT),
                   jax.ShapeDtypeStruct((HEADS, T, HD), F32),
                   jax.ShapeDtypeStruct((HEADS, T // CHUNK, HD, HD), F32)],
        scratch_shapes=[pltpu.VMEM((HD, HD), F32), pltpu.VMEM((cb, HD), F32), pltpu.VMEM((cb, HD), F32),
                        pltpu.VMEM((cb, HD), F32), pltpu.VMEM((NSUB, nch * SUB, SUB * HD), MM),
                        pltpu.VMEM((cb, HD), F32), pltpu.VMEM((cb, HD), MM), pltpu.VMEM((cb, HD), MM),
                        pltpu.VMEM((cb, HD), MM)],
    )(p, hgrn_lb, hgrn_g, e_mat)
    return out[:4], out[4:]


def _hgrn_bwd(p, o, a_all, s_all, doa, hgrn_lb, hgrn_g, dp, carry):
    T = p.shape[1]
    cb = min(HGRN_BLOCK, T)
    nch = cb // CHUNK
    ncb = T // cb
    _, et_mat = _hgrn_consts()

    def body(p_ref, o_ref, a_ref, s_ref, doa_ref, lb_ref, g_ref, et_ref, dp_in, dp_ref, sm_ref,
             dst_scr, q_scr, k_scr, b_scr, x_scr, dqd_scr, dkd_scr, qe_scr, kf_scr, kd_scr, do_scr, dqe_scr, dkc_scr):
        del dp_in

        @pl.when(pl.program_id(1) == 0)
        def _():
            dst_scr[...] = jnp.zeros_like(dst_scr)
            sm_ref[...] = jnp.zeros_like(sm_ref)

        z = lb_ref[...]
        lb = _sig(z[0:1, :] - z[1:2, :])
        g = g_ref[...]
        tl64 = lax.broadcasted_iota(jnp.int32, (CHUNK, HD), 0)

        def first(c, carry):
            worst, dg = carry
            r0 = pl.multiple_of(c * CHUNK, CHUNK)
            rs = pl.ds(r0, CHUNK)
            qr = p_ref[0, rs, :]
            q = qr * _sig(qr) * Q_SCALE
            fg = lb + (1.0 - lb) * _sig(p_ref[1, rs, :])
            k = 1.0 - fg
            bc = jnp.log(fg)
            sh = 1
            while sh < CHUNK:
                bc = bc + jnp.where(tl64 >= sh, pltpu.roll(bc, sh, 0), 0.0)
                sh *= 2
            b_scr[rs, :] = bc
            q_scr[rs, :] = q
            k_scr[rs, :] = k
            qe_scr[rs, :] = q * jnp.exp(bc)
            kf_scr[rs, :] = k * jnp.exp(jnp.minimum(-bc, SAFE_EXP))
            kd_scr[rs, :] = (k * jnp.exp(b_scr[pl.ds(r0 + CHUNK - 1, 1), :] - bc)).astype(MM)
            ov = o_ref[rs, :]
            og = p_ref[3, rs, :]
            r = lax.rsqrt(_rowmean(ov * ov) + EPS)
            oh = ov * r
            sgo = _sig(og)
            doa_v = doa_ref[rs, :]
            don = doa_v * og * sgo
            dp_ref[3, rs, :] = (doa_v * oh * g * sgo * (1.0 + og * (1.0 - sgo))).astype(ACT)
            doh = don * g
            do_scr[rs, :] = r * (doh - oh * _rowmean(doh * oh))
            return jnp.maximum(worst, -bc), dg + _colsum(don * oh)

        worst, dg = lax.fori_loop(0, nch, first, (jnp.zeros((CHUNK, HD), F32), jnp.zeros((1, HD), F32)))
        sm_ref[1:2, :] += dg
        safe = jnp.max(worst) < SAFE_EXP

        sbc = tl64 // SUB
        causal = lax.broadcasted_iota(jnp.int32, (CHUNK, HD), 1) <= tl64
        chunks = [slice(c * CHUNK, (c + 1) * CHUNK) for c in range(nch)]
        da_parts = [jnp.where(causal, _mm_nt(do_scr[rs, :], _pad_rows(p_ref[2, rs, :])), 0.0) for rs in chunks]
        dv_parts = [_mm_tn(a_ref[0, rs, :], do_scr[rs, :])[:CHUNK] for rs in chunks]

        @pl.when(safe)
        def _():
            hi = dict(preferred_element_type=F32, precision=lax.Precision.HIGH)
            for c, rs in enumerate(chunks):
                bc = b_scr[rs, :]
                dqd_scr[rs, :] = jnp.exp(bc) * lax.dot_general(
                    da_parts[c], _pad_rows(kf_scr[rs, :]), (((1,), (0,)), ((), ())), **hi)
                dkd_scr[rs, :] = jnp.exp(jnp.minimum(-bc, SAFE_EXP)) * lax.dot_general(
                    da_parts[c], qe_scr[rs, :], (((0,), (0,)), ((), ())), **hi)[:CHUNK]

        @pl.when(jnp.logical_not(safe))
        def _():
            tl = lax.broadcasted_iota(jnp.int32, (cb, HD), 0) % CHUNK
            sub = _hgrn_sub(dict(b=b_scr[...], tl=tl, q=q_scr[...], k=k_scr[...]), b_scr, cb)
            dqoff_mm = [[_mm(da_parts[c], _pad_rows(sub["ko"][i][rs])) for i in range(1, NSUB)]
                        for c, rs in enumerate(chunks)]
            dkoff_mm = [[_mm_tn(jnp.where(sbc == i, da_parts[c], 0.0), sub["qo"][rs])[:CHUNK]
                         for i in range(1, NSUB)] for c, rs in enumerate(chunks)]
            dqoff_parts = [_by_subblock(sbc, [None] + dqoff_mm[c]) for c in range(nch)]
            dkoff_parts = []
            for c, rs in enumerate(chunks):
                dko = sub["eko"][1][rs] * dkoff_mm[c][0]
                for i in range(2, NSUB):
                    dko = dko + sub["eko"][i][rs] * dkoff_mm[c][i - 1]
                dkoff_parts.append(dko)
            for i in range(NSUB):
                rows = []
                for c in range(nch):
                    blk = da_parts[c][SUB * i:SUB * (i + 1)]
                    rows.append(blk if i == 0 else pltpu.roll(blk, HD - SUB * i, 1))
                x_scr[i] = _mm(jnp.concatenate(rows, axis=0), et_ref[...])
            ti = lax.broadcasted_iota(jnp.int32, (SUB, HD), 0)

            def dbody(c, carry):
                for i in range(NSUB):
                    r0 = pl.multiple_of(c * CHUNK + SUB * i, SUB)
                    qi = q_scr[pl.ds(r0, SUB), :]
                    bi = b_scr[pl.ds(r0, SUB), :]
                    dq_hi = jnp.zeros((8, HD), F32)
                    dq_lo = jnp.zeros((8, HD), F32)
                    dk_hi = jnp.zeros((8, HD), F32)
                    dk_lo = jnp.zeros((8, HD), F32)
                    c0 = pl.multiple_of(c * SUB, SUB)
                    t8 = ti[:8]
                    for s in range(SUB):
                        krow = k_scr[pl.ds(r0 + s, 1), :]
                        brow = b_scr[pl.ds(r0 + s, 1), :]
                        w_lo = (x_scr[i, pl.ds(c0 + 8, 8), s * HD:(s + 1) * HD]
                                * jnp.exp(jnp.where(t8 + 8 >= s, bi[8:] - brow, NEG)))
                        dq_lo = dq_lo + w_lo * krow
                        col = _colsum(w_lo * qi[8:])
                        if s < 8:
                            w_hi = (x_scr[i, pl.ds(c0, 8), s * HD:(s + 1) * HD]
                                    * jnp.exp(jnp.where(t8 >= s, bi[:8] - brow, NEG)))
                            dq_hi = dq_hi + w_hi * krow
                            dk_hi = jnp.where(t8 == s, col + _colsum(w_hi * qi[:8]), dk_hi)
                        else:
                            dk_lo = jnp.where(t8 + 8 == s, col, dk_lo)
                    dqd_scr[pl.ds(r0, SUB), :] = jnp.concatenate([dq_hi, dq_lo], axis=0)
                    dkd_scr[pl.ds(r0, SUB), :] = jnp.concatenate([dk_hi, dk_lo], axis=0)
                return carry

            lax.fori_loop(0, nch, dbody, 0)
            dqd_scr[...] += jnp.concatenate(dqoff_parts, axis=0) * sub["eqo"]
            dkd_scr[...] += jnp.concatenate(dkoff_parts, axis=0)

        qdo = [_mm_tn(do_scr[rs, :], qe_scr[rs, :]) for rs in chunks]
        dsts = [None] * nch
        dst = dst_scr[...]
        for c in reversed(range(nch)):
            dsts[c] = dst
            dst = dst * jnp.exp(b_scr[pl.ds(c * CHUNK + CHUNK - 1, 1), :]) + qdo[c]
        dst_scr[...] = dst
        sts = [s_ref[0, c] for c in range(nch)]
        debl = []
        for c, rs in enumerate(chunks):
            dqe_scr[rs, :] = _mm(do_scr[rs, :], sts[c])
            dkc_scr[rs, :] = _mm(p_ref[2, rs, :], dsts[c])
            dp_ref[2, rs, :] = (dv_parts[c] + _mm_nt(kd_scr[rs, :], dsts[c])).astype(ACT)
            debl.append(_colsum(dsts[c] * sts[c]))

        dlb = jnp.zeros((1, HD), F32)
        for c, rs in enumerate(chunks):
            qr = p_ref[0, rs, :]
            sq = _sig(qr)
            q = qr * sq * Q_SCALE
            sf = _sig(p_ref[1, rs, :])
            fg = lb + (1.0 - lb) * sf
            k = 1.0 - fg
            bc = b_scr[rs, :]
            bl = b_scr[pl.ds(c * CHUNK + CHUNK - 1, 1), :]
            dq_tot = dqd_scr[rs, :] + dqe_scr[rs, :] * jnp.exp(bc)
            dk_inter = dkc_scr[rs, :] * jnp.exp(bl - bc)
            dk_tot = dkd_scr[rs, :] + dk_inter
            rc = q * dq_tot - k * dk_tot
            sh = 1
            while sh < CHUNK:
                rc = rc + jnp.where(tl64 + sh < CHUNK, pltpu.roll(rc, CHUNK - sh, 0), 0.0)
                sh *= 2
            dfg = (rc + jnp.exp(bl) * debl[c] + _colsum(k * dk_inter)) / fg - dk_tot
            dlb = dlb + _colsum(dfg * (1.0 - sf))
            dp_ref[0, rs, :] = (dq_tot * Q_SCALE * sq * (1.0 + qr * (1.0 - sq))).astype(ACT)
            dp_ref[1, rs, :] = (dfg * (1.0 - lb) * sf * (1.0 - sf)).astype(ACT)
        sm_ref[0:1, :] += dlb

    rev = lambda c: ncb - 1 - c
    out = _gridded(
        body, carry, name="hgrn_bwd", grid=(HEADS, ncb),
        in_specs=[pl.BlockSpec((4, cb, HD), lambda h, c: (0, rev(c), h)),
                  pl.BlockSpec((cb, HD), lambda h, c: (rev(c), h)),
                  pl.BlockSpec((1, cb, HD), lambda h, c: (h, rev(c), 0)),
                  pl.BlockSpec((1, nch, HD, HD), lambda h, c: (h, rev(c), 0, 0)),
                  pl.BlockSpec((cb, HD), lambda h, c: (rev(c), h)),
                  pl.BlockSpec((2, HD), lambda h, c: (0, h)),
                  pl.BlockSpec((1, HD), lambda h, c: (0, h)),
                  pl.BlockSpec((HD, SUB * HD), lambda h, c: (0, 0)),
                  pl.BlockSpec(memory_space=pl.ANY)],
        out_specs=[pl.BlockSpec((4, cb, HD), lambda h, c: (0, rev(c), h)),
                   pl.BlockSpec((8, HD), lambda h, c: (0, h))],
        out_shape=[jax.ShapeDtypeStruct(dp.shape, dp.dtype), jax.ShapeDtypeStruct((8, D), F32)],
        aliases={8: 0},
        scratch_shapes=[pltpu.VMEM((HD, HD), F32), pltpu.VMEM((cb, HD), F32), pltpu.VMEM((cb, HD), F32),
                        pltpu.VMEM((cb, HD), F32), pltpu.VMEM((NSUB, nch * SUB, SUB * HD), F32),
                        pltpu.VMEM((cb, HD), F32), pltpu.VMEM((cb, HD), F32),
                        pltpu.VMEM((cb, HD), F32), pltpu.VMEM((cb, HD), F32), pltpu.VMEM((cb, HD), MM),
                        pltpu.VMEM((cb, HD), F32), pltpu.VMEM((cb, HD), F32), pltpu.VMEM((cb, HD), F32)],
    )(p, o, a_all, s_all, doa, hgrn_lb, hgrn_g, et_mat, dp)
    return out[:2], out[2:]


def _ln_fwd(u1, g, b):
    mu = _rowmean(u1)
    xc = u1 - mu
    rs = lax.rsqrt(_rowmean(xc * xc) + EPS)
    xh = xc * rs
    return xh * g + b, xh, rs


CONV_RB = 64
LANES = 128


def _shift_rows(src, sh, ls, n):
    for r in range(1, 8):
        sh[r - 1, 0:n, :] = src[pl.ds(r, n), ls]


def _tap(src, sh, ls, off, r0, rows):
    r = off % 8
    if r == 0:
        return src[pl.ds(r0 + off, rows), ls]
    return sh[r - 1, pl.ds(r0 + off - r, rows), :]


def _conv_fwd(p, cw, cb_, lng, lnb, carry):
    T = p.shape[1]
    tm = min(512, T)
    n = HALO + tm - 8

    def body(p_ref, cw_ref, cb_ref, g_ref, b_ref, u1_ref, u2_ref, buf, sh):
        @pl.when(pl.program_id(0) == 0)
        def _():
            buf[0:HALO, :] = jnp.zeros((HALO, D), F32)

        buf[HALO:HALO + tm, :] = p_ref[0] * _sig(p_ref[1])
        for lb in range(D // LANES):
            ls = slice(lb * LANES, (lb + 1) * LANES)
            _shift_rows(buf, sh, ls, n)
            taps = [cw_ref[j:j + 1, ls] for j in range(CONV_K)]
            bias = cb_ref[:, ls]

            def rows_body(rb, carry):
                r0 = pl.multiple_of(rb * CONV_RB, CONV_RB)
                acc = jnp.broadcast_to(bias, (CONV_RB, LANES))
                for j in range(CONV_K):
                    acc = acc + taps[j] * _tap(buf, sh, ls, HALO - (CONV_K - 1) + j, r0, CONV_RB)
                u1_ref[pl.ds(r0, CONV_RB), ls] = acc
                return carry

            lax.fori_loop(0, tm // CONV_RB, rows_body, 0)
        y, _, _ = _ln_fwd(u1_ref[...], g_ref[...], b_ref[...])
        u2_ref[...] = (y * _sig(y)).astype(ACT)
        buf[0:HALO, :] = buf[tm:tm + HALO, :]

    out = _gridded(
        body, carry, name="conv_fwd", grid=(T // tm,),
        in_specs=[pl.BlockSpec((2, tm, D), lambda i: (2, i, 0)), pl.BlockSpec((HALO, D), lambda i: (0, 0)),
                  pl.BlockSpec((1, D), lambda i: (0, 0)), pl.BlockSpec((1, D), lambda i: (0, 0)),
                  pl.BlockSpec((1, D), lambda i: (0, 0))],
        out_specs=[pl.BlockSpec((tm, D), lambda i: (i, 0)), pl.BlockSpec((tm, D), lambda i: (i, 0))],
        out_shape=[jax.ShapeDtypeStruct((T, D), F32), jax.ShapeDtypeStruct((T, D), ACT)],
        scratch_shapes=[pltpu.VMEM((HALO + tm, D), F32), pltpu.VMEM((7, n, LANES), F32)],
    )(p, cw, cb_, lng, lnb)
    return out[:2], out[2:]


def _conv_bwd(p, u1, du2, cw, lng, lnb, dp, carry):
    T = p.shape[1]
    tm = min(512, T)
    ni = T // tm
    hb = tm // HALO

    n = HALO + tm - 8

    def body(p_ref, ph_ref, u1_ref, du2_ref, cw_ref, g_ref, b_ref, dp_in, dp_ref, dcw_ref, sm_ref, ubuf, dbuf,
             sh, dacc):
        del dp_in
        step = pl.program_id(0)

        @pl.when(step == 0)
        def _():
            dbuf[tm:tm + HALO, :] = jnp.zeros((HALO, D), F32)
            dcw_ref[...] = jnp.zeros_like(dcw_ref)
            sm_ref[...] = jnp.zeros_like(sm_ref)

        ua = p_ref[0]
        sgb = _sig(p_ref[1])
        halo = ph_ref[0] * _sig(ph_ref[1])
        ubuf[0:HALO, :] = jnp.where(step == ni - 1, 0.0, halo)
        ubuf[HALO:HALO + tm, :] = ua * sgb
        g = g_ref[...]
        y, xh, rs = _ln_fwd(u1_ref[...], g, b_ref[...])
        sy = _sig(y)
        dy = du2_ref[...] * sy * (1.0 + y * (1.0 - sy))
        sm_ref[1:2, :] += _colsum(dy * xh)
        sm_ref[2:3, :] += _colsum(dy)
        dxh = dy * g
        du1 = rs * (dxh - _rowmean(dxh) - xh * _rowmean(dxh * xh))
        sm_ref[0:1, :] += _colsum(du1)
        dbuf[0:tm, :] = du1
        for lb in range(D // LANES):
            ls = slice(lb * LANES, (lb + 1) * LANES)
            taps = [cw_ref[j:j + 1, ls] for j in range(CONV_K)]
            _shift_rows(dbuf, sh, ls, n)

            def du0_body(rb, carry):
                r0 = pl.multiple_of(rb * CONV_RB, CONV_RB)
                acc = jnp.zeros((CONV_RB, LANES), F32)
                for j in range(CONV_K):
                    acc = acc + taps[j] * _tap(dbuf, sh, ls, CONV_K - 1 - j, r0, CONV_RB)
                dp_ref[0, pl.ds(r0, CONV_RB), ls] = acc.astype(ACT)
                return carry

            lax.fori_loop(0, tm // CONV_RB, du0_body, 0)
            _shift_rows(ubuf, sh, ls, n)
            dacc[...] = jnp.zeros_like(dacc)

            def dcw_body(rb, carry):
                r0 = pl.multiple_of(rb * CONV_RB, CONV_RB)
                d = dbuf[pl.ds(r0, CONV_RB), ls]
                for j in range(CONV_K):
                    prod = d * _tap(ubuf, sh, ls, HALO - (CONV_K - 1) + j, r0, CONV_RB)
                    dacc[8 * j:8 * j + 8, :] += jnp.sum(prod.reshape(CONV_RB // 8, 8, LANES), axis=0)
                return carry

            lax.fori_loop(0, tm // CONV_RB, dcw_body, 0)
            for j in range(CONV_K):
                dcw_ref[j:j + 1, ls] += _colsum(dacc[8 * j:8 * j + 8, :])
        du0 = dp_ref[0].astype(F32)
        dp_ref[0] = (du0 * sgb).astype(ACT)
        dp_ref[1] = (du0 * ua * sgb * (1.0 - sgb)).astype(ACT)
        dbuf[tm:tm + HALO, :] = dbuf[0:HALO, :]

    rev = lambda i: ni - 1 - i
    out = _gridded(
        body, carry, name="conv_bwd", grid=(ni,),
        in_specs=[pl.BlockSpec((2, tm, D), lambda i: (2, rev(i), 0)),
                  pl.BlockSpec((2, HALO, D), lambda i: (2, jnp.maximum(rev(i) * hb - 1, 0), 0)),
                  pl.BlockSpec((tm, D), lambda i: (rev(i), 0)), pl.BlockSpec((tm, D), lambda i: (rev(i), 0)),
                  pl.BlockSpec((HALO, D), lambda i: (0, 0)), pl.BlockSpec((1, D), lambda i: (0, 0)),
                  pl.BlockSpec((1, D), lambda i: (0, 0)), pl.BlockSpec(memory_space=pl.ANY)],
        out_specs=[pl.BlockSpec((2, tm, D), lambda i: (2, rev(i), 0)),
                   pl.BlockSpec((HALO, D), lambda i: (0, 0)), pl.BlockSpec((8, D), lambda i: (0, 0))],
        out_shape=[jax.ShapeDtypeStruct(dp.shape, dp.dtype), jax.ShapeDtypeStruct((HALO, D), F32),
                   jax.ShapeDtypeStruct((8, D), F32)],
        aliases={7: 0},
        scratch_shapes=[pltpu.VMEM((HALO + tm, D), F32), pltpu.VMEM((tm + HALO, D), F32),
                        pltpu.VMEM((7, n, LANES), F32), pltpu.VMEM((8 * CONV_K, LANES), F32)],
    )(p, p, u1, du2, cw, lng, lnb, dp)
    return out[:3], out[3:]


def _mixout_fwd(x, oa, u2, p, mod, mo, w_a, w_b, w_o):
    T = x.shape[0]
    tm = min(512, T)

    def body(x_ref, oa_ref, u2_ref, p_ref, mod_ref, wa_ref, wb_ref, wo_ref, xo_ref, ya_ref, yb_ref, mo_ref):
        ya = _mm(oa_ref[...], wa_ref[...])
        yb = _mm(u2_ref[...], wb_ref[...])
        ya_ref[...] = ya.astype(ACT)
        yb_ref[...] = yb.astype(ACT)
        merged = _sig(p_ref[0]) * ya + _sig(p_ref[1]) * yb
        out = _mm(merged, wo_ref[...])
        mo_ref[...] = out
        xo_ref[...] = x_ref[...] + mod_ref[mo + 2:mo + 3, :] * out

    tile = pl.BlockSpec((tm, D), lambda i: (i, 0))
    wspec = pl.BlockSpec((D, D), lambda i: (0, 0))
    return pl.pallas_call(
        body, name="mixout_fwd", grid=(T // tm,),
        in_specs=[tile, tile, tile, pl.BlockSpec((2, tm, D), lambda i: (3, i, 0)),
                  pl.BlockSpec((9, D), lambda i: (0, 0)), wspec, wspec, wspec],
        out_specs=[tile, tile, tile, tile],
        out_shape=[jax.ShapeDtypeStruct((T, D), F32), jax.ShapeDtypeStruct((T, D), ACT),
                   jax.ShapeDtypeStruct((T, D), ACT), jax.ShapeDtypeStruct((T, D), F32)],
        compiler_params=_cparams(1),
    )(x, oa, u2, p, mod, w_a, w_b, w_o)


def _mixout_bwd(dxo, oa, u2, ya, yb, mout, p, mod, mo, w_a, w_b, w_o):
    T = dxo.shape[0]
    tm = min(256, T)

    def body(dxo_ref, oa_ref, u2_ref, ya_ref, yb_ref, mo_ref, p_ref, mod_ref, wa_ref, wb_ref, wo_ref,
             dp_ref, doa_ref, du2_ref, dwa_ref, dwb_ref, dwo_ref, sm_ref):
        @pl.when(pl.program_id(0) == 0)
        def _():
            dwa_ref[...] = jnp.zeros_like(dwa_ref)
            dwb_ref[...] = jnp.zeros_like(dwb_ref)
            dwo_ref[...] = jnp.zeros_like(dwo_ref)
            sm_ref[...] = jnp.zeros_like(sm_ref)

        dxo_v = dxo_ref[...]
        sm_ref[2:3, :] += _colsum(dxo_v * mo_ref[...])
        dmo = (mod_ref[mo + 2:mo + 3, :] * dxo_v).astype(MM)
        ya = ya_ref[...].astype(F32)
        yb = yb_ref[...].astype(F32)
        sga = _sig(p_ref[0])
        sgb = _sig(p_ref[1])
        merged = (sga * ya + sgb * yb).astype(MM)
        dwo_ref[...] += _mm_tn(merged, dmo)
        dmg = _mm_nt(dmo, wo_ref[...])
        dp_ref[0] = (dmg * ya * sga * (1.0 - sga)).astype(ACT)
        dp_ref[1] = (dmg * yb * sgb * (1.0 - sgb)).astype(ACT)
        dya = (dmg * sga).astype(MM)
        dyb = (dmg * sgb).astype(MM)
        dwa_ref[...] += _mm_tn(oa_ref[...], dya)
        dwb_ref[...] += _mm_tn(u2_ref[...], dyb)
        doa_ref[...] = _mm_nt(dya, wa_ref[...])
        du2_ref[...] = _mm_nt(dyb, wb_ref[...])

    tile = pl.BlockSpec((tm, D), lambda i: (i, 0))
    wspec = pl.BlockSpec((D, D), lambda i: (0, 0))
    return pl.pallas_call(
        body, name="mixout_bwd", grid=(T // tm,),
        in_specs=[tile, tile, tile, tile, tile, tile, pl.BlockSpec((2, tm, D), lambda i: (3, i, 0)),
                  pl.BlockSpec((9, D), lambda i: (0, 0)), wspec, wspec, wspec],
        out_specs=[pl.BlockSpec((2, tm, D), lambda i: (3, i, 0)), tile, tile, wspec, wspec, wspec,
                   pl.BlockSpec((8, D), lambda i: (0, 0))],
        out_shape=[jax.ShapeDtypeStruct((8, T, D), ACT), jax.ShapeDtypeStruct((T, D), F32),
                   jax.ShapeDtypeStruct((T, D), F32), jax.ShapeDtypeStruct((D, D), F32),
                   jax.ShapeDtypeStruct((D, D), F32), jax.ShapeDtypeStruct((D, D), F32),
                   jax.ShapeDtypeStruct((8, D), F32)],
        compiler_params=_cparams(1),
    )(dxo, oa, u2, ya, yb, mout, p, mod, w_a, w_b, w_o)


def _ada_wgrad(cs_all, dmod_cols):
    cs_t = jnp.pad(cs_all.T, ((0, 0), (0, HD - N_DEV)))
    dm = jnp.pad(dmod_cols, ((0, HD - N_DEV), (0, 0)))

    def body(cs_ref, d_ref, out_ref):
        out_ref[...] = jnp.dot(cs_ref[...], d_ref[...], preferred_element_type=F32,
                               precision=lax.Precision.HIGHEST)

    return pl.pallas_call(
        body, name="ada_wgrad", out_shape=jax.ShapeDtypeStruct((D, dmod_cols.shape[1]), F32),
        compiler_params=pltpu.CompilerParams(vmem_limit_bytes=VMEM_LIMIT),
    )(cs_t, dm)


def _adam_math(w, g, m, v):
    m2 = ADAM_B1 * m + (1.0 - ADAM_B1) * g
    v2 = ADAM_B2 * v + (1.0 - ADAM_B2) * (g * g)
    m_hat = m2 / (1.0 - ADAM_B1 ** ADAM_STEP)
    v_hat = v2 / (1.0 - ADAM_B2 ** ADAM_STEP)
    delta = -ADAM_LR * (m_hat / (jnp.sqrt(v_hat) + ADAM_EPS) + ADAM_WD * w)
    return delta, m2, v2


def _adamw(w, m, v, g, name):
    R, C = w.shape
    slots = g.ndim == 3
    n_slots = g.shape[0] if slots else 0
    tr = R
    for cand in (256, 176):
        if R % cand == 0 and R > cand:
            tr = cand
            break

    def body(w_ref, m_ref, v_ref, g_ref, go_ref, d_ref, mo_ref, vo_ref):
        if slots:
            gv = g_ref[0].astype(F32)
            for s in range(1, n_slots):
                gv = gv + g_ref[s].astype(F32)
        else:
            gv = g_ref[...]
        go_ref[...] = gv
        d_ref[...], mo_ref[...], vo_ref[...] = _adam_math(w_ref[...], gv, m_ref[...], v_ref[...])

    tile = pl.BlockSpec((tr, C), lambda i: (i, 0))
    gspec = pl.BlockSpec((n_slots, tr, C), lambda i: (0, i, 0)) if slots else tile
    sds = jax.ShapeDtypeStruct((R, C), F32)
    return pl.pallas_call(
        body, name=name, grid=(R // tr,), in_specs=[tile, tile, tile, gspec], out_specs=[tile] * 4,
        out_shape=[sds] * 4, compiler_params=_cparams(1),
    )(w, m, v, g)


def _sum_slots(pack, name, tr):
    n, R, C = pack.shape

    def body(p_ref, out_ref):
        acc = p_ref[0].astype(F32)
        for s in range(1, n):
            acc = acc + p_ref[s].astype(F32)
        out_ref[...] = acc

    return pl.pallas_call(
        body, name=name, grid=(R // tr,), in_specs=[pl.BlockSpec((n, tr, C), lambda i: (0, i, 0))],
        out_specs=pl.BlockSpec((tr, C), lambda i: (i, 0)), out_shape=jax.ShapeDtypeStruct((R, C), F32),
        compiler_params=_cparams(1))(pack)


def _me():
    return lax.axis_index("x"), lax.axis_index("y"), lax.axis_index("c")


def _peer(r):
    x, y, c = _me()
    px = 1 - x if r & 4 else x
    py = 1 - y if r & 2 else y
    pc = 1 - c if r & 1 else c
    return (px, py, pc), 4 * px + 2 * py + pc


def _small_gather(x_ref, out_ref, send_sems, recv_sems):
    R = x_ref.shape[0]
    mx, my, mc = _me()
    me = 4 * mx + 2 * my + mc
    mine = out_ref.at[pl.ds(pl.multiple_of(me * R, 8), R), :]
    copies = []
    for r in range(1, N_DEV):
        dev, _ = _peer(r)
        copies.append(pltpu.make_async_remote_copy(
            src_ref=x_ref, dst_ref=mine, send_sem=send_sems.at[r - 1], recv_sem=recv_sems.at[r - 1],
            device_id=dev, device_id_type=MESH))
    for cp in copies:
        cp.start()
    mine[...] = x_ref[...]
    for r in range(1, N_DEV):
        dev, idx = _peer(r)
        theirs = out_ref.at[pl.ds(pl.multiple_of(idx * R, 8), R), :]
        pltpu.make_async_remote_copy(
            src_ref=x_ref, dst_ref=theirs, send_sem=send_sems.at[r - 1], recv_sem=recv_sems.at[r - 1],
            device_id=dev, device_id_type=MESH).wait_recv()
    for cp in copies:
        cp.wait_send()


def _prologue(cs, ada_w, ada_b_cols, big):
    n = len(big)
    ncol = ada_w.shape[1]
    big_shape, big_sems = _xchg_specs(big, "gather")

    def body(cs_ref, w_ref, b_ref, *rest):
        big_in, cs_all, mod_all, big_out = rest[:n], rest[n], rest[n + 1], rest[n + 2:2 * n + 2]
        mod_scr, s1, r1, s2, r2 = rest[2 * n + 2:2 * n + 7]
        sems = rest[2 * n + 7:]
        _small_gather(cs_ref, cs_all, s1, r1)
        pick = (lax.broadcasted_iota(jnp.int32, (N_DEV, N_DEV * 8), 1)
                == 8 * lax.broadcasted_iota(jnp.int32, (N_DEV, N_DEV * 8), 0)).astype(F32)
        per_device = jnp.dot(pick, cs_all[...], preferred_element_type=F32, precision=lax.Precision.HIGHEST)
        mod_scr[...] = jnp.dot(per_device, w_ref[...], preferred_element_type=F32,
                               precision=lax.Precision.HIGHEST) + b_ref[...]
        _small_gather(mod_scr, mod_all, s2, r2)
        _xchg_start(big_in, big_out, sems, "gather")
        _xchg_wait(big_in, big_out, sems, "gather")

    vmem = pl.BlockSpec(memory_space=pltpu.VMEM)
    hbm = pl.BlockSpec(memory_space=pl.ANY)
    dma7 = pltpu.SemaphoreType.DMA((N_DEV - 1,))
    out = pl.pallas_call(
        body, name="prologue",
        out_shape=[jax.ShapeDtypeStruct((N_DEV * 8, D), F32), jax.ShapeDtypeStruct((N_DEV * 8, ncol), F32)]
        + big_shape,
        in_specs=[vmem, vmem, vmem] + [hbm] * n, out_specs=[vmem, vmem] + [hbm] * n,
        scratch_shapes=[pltpu.VMEM((8, ncol), F32), dma7, dma7, dma7, dma7] + big_sems,
        compiler_params=pltpu.CompilerParams(vmem_limit_bytes=VMEM_LIMIT),
    )(cs, ada_w, ada_b_cols, *big)
    return out[0], out[1], out[2:]


def _allgather_small(x):
    R, C = x.shape

    def body(x_ref, out_ref, send_sems, recv_sems):
        _small_gather(x_ref, out_ref, send_sems, recv_sems)

    return pl.pallas_call(
        body, name="allgather_small_%dx%d" % (R, C),
        out_shape=jax.ShapeDtypeStruct((N_DEV * R, C), F32),
        in_specs=[pl.BlockSpec(memory_space=pltpu.VMEM)], out_specs=pl.BlockSpec(memory_space=pltpu.VMEM),
        scratch_shapes=[pltpu.SemaphoreType.DMA((N_DEV - 1,)), pltpu.SemaphoreType.DMA((N_DEV - 1,))],
    )(x)


N_CHIP = N_DEV // 2


def _xchg_copies(ins, outs, sems, mode):
    send_sems, recv_sems, local_sems = sems
    mx, my, mc = _me()
    me = 4 * mx + 2 * my + mc
    my_chip = 2 * mx + my
    sibling = _peer(1)[0]

    def rdma(a, r, dev, src, slot):
        k = a * (N_DEV - 1) + r - 1
        return pltpu.make_async_remote_copy(
            src_ref=src, dst_ref=outs[a].at[slot], send_sem=send_sems.at[k], recv_sem=recv_sems.at[k],
            device_id=dev, device_id_type=MESH)

    own, sends, relays, recvs = [], [], [], []
    for a in range(len(ins)):
        if mode == "pair":
            for chip in range(N_CHIP):
                src = ins[a].at[2 * chip + 1 - mc]
                sends.append(rdma(a, chip + 1, sibling, src, chip))
                recvs.append(rdma(a, chip + 1, sibling, src, chip))
            continue
        if mode == "quad":
            own.append(pltpu.make_async_copy(ins[a].at[my_chip], outs[a].at[my_chip], local_sems.at[a]))
            for r in (2, 4, 6):
                dev, idx = _peer(r)
                chip = idx // 2
                sends.append(rdma(a, r, dev, ins[a].at[chip], my_chip))
                recvs.append(rdma(a, r, dev, ins[a].at[chip], chip))
            continue
        gather = mode == "gather"
        own.append(pltpu.make_async_copy(ins[a] if gather else ins[a].at[me], outs[a].at[me], local_sems.at[a]))
        for r in range(1, N_DEV):
            dev, idx = _peer(r)
            if not gather:
                sends.append(rdma(a, r, dev, ins[a].at[idx], me))
                recvs.append(rdma(a, r, dev, ins[a].at[idx], idx))
            elif r == 1:
                sends.append(rdma(a, r, dev, ins[a], me))
                recvs.append(rdma(a, r, dev, ins[a], idx))
            elif r % 2 == 0:
                sends.append(rdma(a, r, dev, ins[a], me))
                relays.append((rdma(a, r, dev, ins[a], idx), rdma(a, r + 1, sibling, outs[a].at[idx], idx)))
            else:
                recvs.append(rdma(a, r, sibling, ins[a], idx))
    return own, sends, relays, recvs


def _xchg_start(ins, outs, sems, mode):
    own, sends, _, _ = _xchg_copies(ins, outs, sems, mode)
    for cp in own + sends:
        cp.start()


def _xchg_wait(ins, outs, sems, mode):
    own, sends, relays, recvs = _xchg_copies(ins, outs, sems, mode)
    for arrival, relay in relays:
        arrival.wait_recv()
        relay.start()
    for cp in recvs:
        cp.wait_recv()
    for cp in own:
        cp.wait()
    for cp in sends + [relay for _, relay in relays]:
        cp.wait_send()


def _xchg_specs(arrays, mode):
    n = len(arrays)
    shape = {"gather": lambda s: (N_DEV,) + s, "scatter": lambda s: s, "pair": lambda s: (N_CHIP,) + s[1:],
             "quad": lambda s: s}[mode]
    out_shape = [jax.ShapeDtypeStruct(shape(a.shape), a.dtype) for a in arrays]
    sems = [pltpu.SemaphoreType.DMA((n * (N_DEV - 1),)), pltpu.SemaphoreType.DMA((n * (N_DEV - 1),)),
            pltpu.SemaphoreType.DMA((n,))]
    return out_shape, sems


def _exchange(arrays, mode, name):
    n = len(arrays)

    def body(*refs):
        _xchg_start(refs[:n], refs[n:2 * n], refs[2 * n:], mode)
        _xchg_wait(refs[:n], refs[n:2 * n], refs[2 * n:], mode)

    out_shape, sems = _xchg_specs(arrays, mode)
    return pl.pallas_call(
        body, name=name, out_shape=out_shape,
        in_specs=[pl.BlockSpec(memory_space=pl.ANY)] * n, out_specs=[pl.BlockSpec(memory_space=pl.ANY)] * n,
        scratch_shapes=sems,
    )(*arrays)


def _gridded(body, carry, *, name, grid, in_specs, out_specs, out_shape, scratch_shapes=(), aliases=None):
    if carry is None:
        return pl.pallas_call(
            body, name=name, grid=grid, in_specs=list(in_specs), out_specs=list(out_specs),
            out_shape=list(out_shape), scratch_shapes=list(scratch_shapes), input_output_aliases=aliases or {},
            compiler_params=_cparams(len(grid)))
    arrays, mode = carry
    n, n_in, n_out, n_scr = len(arrays), len(in_specs), len(out_specs), len(scratch_shapes)
    c_shape, c_sems = _xchg_specs(arrays, mode)

    def wrapped(*refs):
        ins, cin = refs[:n_in], refs[n_in:n_in + n]
        o0 = n_in + n
        outs, cout = refs[o0:o0 + n_out], refs[o0 + n_out:o0 + n_out + n]
        s0 = o0 + n_out + n
        scr, sems = refs[s0:s0 + n_scr], refs[s0 + n_scr:]
        first = pl.program_id(0) == 0
        last = pl.program_id(0) == grid[0] - 1
        for ax in range(1, len(grid)):
            first = first & (pl.program_id(ax) == 0)
            last = last & (pl.program_id(ax) == grid[ax] - 1)

        @pl.when(first)
        def _():
            _xchg_start(cin, cout, sems, mode)

        body(*ins, *outs, *scr)

        @pl.when(last)
        def _():
            _xchg_wait(cin, cout, sems, mode)

    hbm = pl.BlockSpec(memory_space=pl.ANY)
    res = pl.pallas_call(
        wrapped, name=name, grid=grid, in_specs=list(in_specs) + [hbm] * n, out_specs=list(out_specs) + [hbm] * n,
        out_shape=list(out_shape) + c_shape, scratch_shapes=list(scratch_shapes) + c_sems,
        input_output_aliases=aliases or {}, compiler_params=_cparams(len(grid)),
    )
    return lambda *args: res(*args, *arrays)


def _local_step(x, target, mod, small, sh, w1):
    w1_in, w1_out = w1[0].reshape(2, D_FF, D), w1[1].reshape(D_FF, D)
    (x1, a1, b1, f1, h1, h2), (wm_in,) = _ffn_fwd(x, mod, 0, small["norm_ffn1"], w1_in, w1_out, 0.5, "ffn1_fwd",
                                                  ([sh["mix_w_in"]], "gather"), nxt=(small["norm_mix"], 3))
    (p,), (wh_o, wc_o, wm_o, cw) = _mixin_fwd(
        h2, wm_in, ([sh["hgrn_w_o"], sh["conv_w_o"], sh["mix_w_out"], sh["conv_w"]], "gather"))
    wh_o, wc_o, wm_o = wh_o.reshape(D, D), wc_o.reshape(D, D), wm_o.reshape(D, D)
    cw = jnp.pad(cw.transpose(1, 0, 2).reshape(CONV_K, D), ((0, HALO - CONV_K), (0, 0)))
    (o, oa, a_all, s_all), (w2_in,) = _hgrn_fwd(p, small["hgrn_lb"], small["hgrn_g"], ([sh["ffn2_w_in"]], "gather"))
    (u1, u2), (w2_out,) = _conv_fwd(p, cw, small["conv_b"], small["conv_ln_g"], small["conv_ln_b"],
                                    ([sh["ffn2_w_out"]], "gather"))
    w2_in, w2_out = w2_in.reshape(2, D_FF, D), w2_out.reshape(D_FF, D)
    x2, ya, yb, mout = _mixout_fwd(x1, oa, u2, p, mod, 3, wh_o, wc_o, wm_o)
    (x3, a3, b3, f3, h3), _ = _ffn_fwd(x2, mod, 6, small["norm_ffn2"], w2_in, w2_out, 0.5, "ffn2_fwd", None)
    dx3, df3, sm_head = _head(x3, target, small["norm_final"], mod, 8, 0.5)

    (da3, db3, dw2_in, dw2_out), _ = _ffn_bwd_w(h3, df3, a3, b3, w2_out, "ffn2_bwd_w", None)
    rows = lambda t: t.reshape(N_DEV, -1, D).astype(MM)
    (dx2, sm3), (r2_out,) = _ffn_bwd_x(x2, dx3, f3, da3, db3, mod, 6, small["norm_ffn2"], w2_in, 0.5, "ffn2_bwd_x",
                                       ([rows(dw2_out)], "scatter"))
    dp, doa, du2, dwh_o, dwc_o, dwm_o, sm_mo = _mixout_bwd(dx2, oa, u2, ya, yb, mout, p, mod, 3, wh_o, wc_o, wm_o)
    (dp, dcw, sm_cv), (r2_in,) = _conv_bwd(p, u1, du2, cw, small["conv_ln_g"], small["conv_ln_b"], dp,
                                           ([rows(dw2_in)], "scatter"))
    (dp, sm_hg), _ = _hgrn_bwd(p, o, a_all, s_all, doa, small["hgrn_lb"], small["hgrn_g"], dp, None)
    (dx1, dwm_in, sm2, df1), (rh_o, rc_o, rm_o, rcw) = _mixin_bwd(
        x1, h2, dx2, dp, mod, 3, small["norm_mix"], wm_in, 2, 0.5,
        ([rows(dwh_o), rows(dwc_o), rows(dwm_o), dcw[:CONV_K].reshape(CONV_K, N_DEV, -1).transpose(1, 0, 2)],
         "scatter"))
    (da1, db1, dw1_in, dw1_out), (rm_in,) = _ffn_bwd_w(h1, df1, a1, b1, w1_out, "ffn1_bwd_w",
                                                      (_pair_reduce([dwm_in], "pair_mix"), "quad"))
    (dx0, sm1), (r1_in, r1_out) = _ffn_bwd_x(
        x, dx1, f1, da1, db1, mod, 0, small["norm_ffn1"], w1_in, 0.5, "ffn1_bwd_x",
        (_pair_reduce([rows(dw1_in), rows(dw1_out)], "pair_ffn1"), "quad"))

    dmod = jnp.concatenate([sm1[0:3], sm2[0:2], sm_mo[2:3], sm3[0:3]], axis=0)
    gsmall = dict(norm_ffn1=sm1[3:4], norm_mix=sm2[3:4], lb0=sm_hg[0:1], hgrn_g=sm_hg[1:2], conv_b=sm_cv[0:1],
                  conv_ln_g=sm_cv[1:2], conv_ln_b=sm_cv[2:3], norm_ffn2=sm3[3:4], norm_final=sm_head[0:1])
    recv = dict(ffn1_w_in=r1_in, ffn1_w_out=r1_out, mix_w_in=rm_in, hgrn_w_o=rh_o, conv_w=rcw, conv_w_o=rc_o,
                mix_w_out=rm_o, ffn2_w_in=r2_in, ffn2_w_out=r2_out)
    return sm_head[1, 0], dx0, dmod, gsmall, recv


def _pair_add(mine, theirs, core, name):
    _, R, C = theirs.shape

    def body(core_ref, a_ref, b_ref, out_ref):
        del core_ref
        out_ref[0] = (a_ref[0, 0].astype(F32) + b_ref[0].astype(F32)).astype(out_ref.dtype)

    blk = pl.BlockSpec((1, R, C), lambda s, core_ref: (s, 0, 0))
    grid_spec = pltpu.PrefetchScalarGridSpec(
        num_scalar_prefetch=1, grid=(N_CHIP,),
        in_specs=[pl.BlockSpec((1, 1, R, C), lambda s, core_ref: (s, core_ref[0], 0, 0)), blk], out_specs=blk)
    return pl.pallas_call(body, name=name, grid_spec=grid_spec,
                          out_shape=jax.ShapeDtypeStruct(theirs.shape, mine.dtype), compiler_params=_cparams(1),
                          )(core, mine.reshape(N_CHIP, 2, R, C), theirs)


def _pair_reduce(arrays, name):
    theirs = _exchange(arrays, "pair", name)
    core = lax.axis_index("c").astype(jnp.int32).reshape(1)
    return [_pair_add(a, t, core, "%s_add%d" % (name, i)) for i, (a, t) in enumerate(zip(arrays, theirs))]


SMALL_ORDER = ("norm_ffn1", "norm_mix", "lb0", "hgrn_g", "conv_b", "conv_ln_g", "conv_ln_b", "norm_ffn2",
               "norm_final")
PACK_ROWS = 24


def kernel(x, c, ada_w, ada_b, norm_ffn1, ffn1_w_in, ffn1_w_out, norm_mix, mix_w_in, hgrn_lb, hgrn_g, hgrn_w_o, conv_w, conv_b, conv_ln_g, conv_ln_b, conv_w_o, mix_w_out, norm_ffn2, ffn2_w_in, ffn2_w_out, norm_final, loss_target, m_ada_w, m_ada_b, m_norm_ffn1, m_ffn1_w_in, m_ffn1_w_out, m_norm_mix, m_mix_w_in, m_hgrn_lb, m_hgrn_g, m_hgrn_w_o, m_conv_w, m_conv_b, m_conv_ln_g, m_conv_ln_b, m_conv_w_o, m_mix_w_out, m_norm_ffn2, m_ffn2_w_in, m_ffn2_w_out, m_norm_final, v_ada_w, v_ada_b, v_norm_ffn1, v_ffn1_w_in, v_ffn1_w_out, v_norm_mix, v_mix_w_in, v_hgrn_lb, v_hgrn_g, v_hgrn_w_o, v_conv_w, v_conv_b, v_conv_ln_g, v_conv_ln_b, v_conv_w_o, v_mix_w_out, v_norm_ffn2, v_ffn2_w_in, v_ffn2_w_out, v_norm_final):
    mx, my, mc = _me()
    me = 4 * mx + 2 * my + mc
    ncol = ada_w.shape[2]

    sh = dict(ffn1_w_out=ffn1_w_out, mix_w_in=mix_w_in, hgrn_w_o=hgrn_w_o, conv_w_o=conv_w_o,
              mix_w_out=mix_w_out, ffn2_w_out=ffn2_w_out)
    sh = {n: w[0].astype(MM) for n, w in sh.items()}
    sh["ffn1_w_in"] = ffn1_w_in[0].T.astype(MM)
    sh["ffn2_w_in"] = ffn2_w_in[0].T.astype(MM)
    sh["conv_w"] = conv_w[0]
    small = dict(norm_ffn1=norm_ffn1, norm_mix=norm_mix, hgrn_lb=hgrn_lb, hgrn_g=hgrn_g, conv_b=conv_b,
                 conv_ln_g=conv_ln_g, conv_ln_b=conv_ln_b, norm_ffn2=norm_ffn2, norm_final=norm_final.reshape(1, D))

    cs = jnp.broadcast_to(c * jax.nn.sigmoid(c), (8, D))
    ada_b_cols = lax.dynamic_slice(ada_b, (0, me * ncol), (1, ncol))
    cs_all, mod_all, w1 = _prologue(cs, ada_w[0], ada_b_cols, [sh["ffn1_w_in"], sh["ffn1_w_out"]])
    cs_all = cs_all.reshape(N_DEV, 8, D)[:, 0, :]
    mod = lax.dynamic_index_in_dim(mod_all.reshape(N_DEV, N_DEV, ncol), me, axis=1, keepdims=False).reshape(9, D)

    loss_local, dx, dmod, gsmall, recv = _local_step(x[0], loss_target[0], mod, small, sh, w1)
    loss = lax.psum(loss_local, ("x", "y", "c"))

    pack = jnp.concatenate([dmod] + [gsmall[n] for n in SMALL_ORDER]
                           + [jnp.zeros((PACK_ROWS - 9 - len(SMALL_ORDER), D), F32)], axis=0)
    pack_all = _allgather_small(pack).reshape(N_DEV, PACK_ROWS, D)
    tot = _sum_slots(pack_all, "sum_small", PACK_ROWS)
    gs = {n: tot[9 + i:10 + i] for i, n in enumerate(SMALL_ORDER)}
    dmod_all = pack_all[:, 0:9, :].reshape(N_DEV, 9 * D)
    g_ada_b = tot[0:9].reshape(1, 9 * D)
    g_ada_w = _ada_wgrad(cs_all, lax.dynamic_slice(dmod_all, (0, me * ncol), (N_DEV, ncol)))
    z = hgrn_lb.astype(F32)
    p0 = jax.nn.sigmoid(z[0:1] - z[1:2])
    dz0 = p0 * (1.0 - p0) * gs["lb0"]
    g_hgrn_lb = jnp.concatenate([dz0, -dz0], axis=0)

    res = {}
    res["ada_w"] = _adamw(ada_w[0], m_ada_w[0], v_ada_w[0], g_ada_w, "adamw_ada_w")
    big = dict(ffn1_w_in=(ffn1_w_in, m_ffn1_w_in, v_ffn1_w_in), ffn1_w_out=(ffn1_w_out, m_ffn1_w_out, v_ffn1_w_out),
               mix_w_in=(mix_w_in, m_mix_w_in, v_mix_w_in), hgrn_w_o=(hgrn_w_o, m_hgrn_w_o, v_hgrn_w_o),
               conv_w=(conv_w, m_conv_w, v_conv_w), conv_w_o=(conv_w_o, m_conv_w_o, v_conv_w_o),
               mix_w_out=(mix_w_out, m_mix_w_out, v_mix_w_out), ffn2_w_in=(ffn2_w_in, m_ffn2_w_in, v_ffn2_w_in),
               ffn2_w_out=(ffn2_w_out, m_ffn2_w_out, v_ffn2_w_out))
    for n, (w, m, v) in big.items():
        g = recv[n]
        if n in ("ffn1_w_in", "ffn2_w_in"):
            g = _sum_slots(g, "sum_" + n, g.shape[1] // 4).T
        res[n] = _adamw(w[0], m[0], v[0], g, "adamw_" + n)
    sm_names = ("ada_b", "norm_ffn1", "norm_mix", "hgrn_lb", "hgrn_g", "conv_b", "conv_ln_g", "conv_ln_b",
                "norm_ffn2", "norm_final")
    sm_w = dict(ada_b=(ada_b, m_ada_b, v_ada_b), norm_ffn1=(norm_ffn1, m_norm_ffn1, v_norm_ffn1),
                norm_mix=(norm_mix, m_norm_mix, v_norm_mix), hgrn_lb=(hgrn_lb, m_hgrn_lb, v_hgrn_lb),
                hgrn_g=(hgrn_g, m_hgrn_g, v_hgrn_g), conv_b=(conv_b, m_conv_b, v_conv_b),
                conv_ln_g=(conv_ln_g, m_conv_ln_g, v_conv_ln_g), conv_ln_b=(conv_ln_b, m_conv_ln_b, v_conv_ln_b),
                norm_ffn2=(norm_ffn2, m_norm_ffn2, v_norm_ffn2), norm_final=(norm_final, m_norm_final, v_norm_final))
    sm_g = dict(gs, ada_b=g_ada_b, hgrn_lb=g_hgrn_lb)
    rows = {n: sm_w[n][0].size // D for n in sm_names}
    n_rows = sum(rows.values())
    pad = (-n_rows) % 8
    stack = lambda parts: jnp.concatenate([q.reshape(-1, D) for q in parts] + [jnp.ones((pad, D), F32)], axis=0)
    st = _adamw(stack([sm_w[n][0] for n in sm_names]), stack([sm_w[n][1] for n in sm_names]),
                stack([sm_w[n][2] for n in sm_names]), stack([sm_g[n] for n in sm_names]), "adamw_small")
    off = 0
    for n in sm_names:
        res[n] = tuple(t[off:off + rows[n]].reshape(sm_w[n][0].shape) for t in st)
        off += rows[n]

    order = ("ada_w", "ada_b", "norm_ffn1", "ffn1_w_in", "ffn1_w_out", "norm_mix", "mix_w_in", "hgrn_lb", "hgrn_g",
             "hgrn_w_o", "conv_w", "conv_b", "conv_ln_g", "conv_ln_b", "conv_w_o", "mix_w_out", "norm_ffn2",
             "ffn2_w_in", "ffn2_w_out", "norm_final")
    lead = lambda n, t: t[None] if n in big or n == "ada_w" else t
    outs = [loss, dx[None]]
    for j in range(4):
        outs += [lead(n, res[n][j]) for n in order]
    return tuple(outs)
```

```python
import jax
import jax.numpy as jnp
from jax import lax
from jax.experimental import pallas as pl
from jax.experimental.pallas import tpu as pltpu

F32 = jnp.float32
MM = jnp.bfloat16
ACT = jnp.bfloat16

D = 1024
D_FF = 2816
HEADS = 8
HD = 128
CHUNK = 64
SUB = 16
NSUB = CHUNK // SUB
HGRN_BLOCK = 1024
SAFE_EXP = 60.0
CONV_K = 31
HALO = 32
EPS = 1e-6
N_DEV = 8
NEG = -1e30
Q_SCALE = HD ** -0.5

ADAM_LR = 0.001
ADAM_B1 = 0.9
ADAM_B2 = 0.999
ADAM_EPS = 1e-08
ADAM_WD = 0.01
ADAM_STEP = 10

V7X_VMEM_BYTES = 64 * 1024 * 1024
VMEM_LIMIT = V7X_VMEM_BYTES - 4 * 1024 * 1024
MESH = pl.DeviceIdType.MESH


def _cparams(n_axes):
    return pltpu.CompilerParams(dimension_semantics=("arbitrary",) * n_axes, vmem_limit_bytes=VMEM_LIMIT)


def _mm(a, b):
    return lax.dot_general(a.astype(MM), b.astype(MM), (((1,), (0,)), ((), ())), preferred_element_type=F32)


def _mm_nt(a, b):
    return lax.dot_general(a.astype(MM), b.astype(MM), (((1,), (1,)), ((), ())), preferred_element_type=F32)


def _mm_tn(a, b):
    return lax.dot_general(a.astype(MM), b.astype(MM), (((0,), (0,)), ((), ())), preferred_element_type=F32)


def _sig(x):
    return 1.0 / (1.0 + jnp.exp(-x))


def _colsum(x):
    return jnp.sum(x, axis=0, keepdims=True)


def _rowmean(x):
    return jnp.mean(x, axis=-1, keepdims=True)


def _modnorm_fwd(xv, g, sh, sc):
    r = lax.rsqrt(_rowmean(xv * xv) + EPS)
    xh = xv * r
    n = xh * g
    return n * (1.0 + sc) + sh, xh, n, r


def _modnorm_bwd(dh, xh, n, r, g, sc):
    dsc = _colsum(dh * n)
    dsh = _colsum(dh)
    dn = dh * (1.0 + sc)
    dg = _colsum(dn * xh)
    dxh = dn * g
    dx = r * (dxh - xh * _rowmean(dxh * xh))
    return dx, dsh, dsc, dg


def _ffn_fwd(x, mod, mo, gnorm, w_in_t, w_out, res, name, carry, nxt=None):
    T = x.shape[0]
    tm = min(512, T)
    tn = D_FF // 2

    def body(x_ref, mod_ref, g_ref, wi_ref, wo_ref, *rest):
        if nxt is None:
            xo_ref, a_ref, b_ref, f_ref, h_ref = rest
        else:
            gn_ref, xo_ref, a_ref, b_ref, f_ref, h_ref, hn_ref = rest
        xv = x_ref[...]
        h, _, _, _ = _modnorm_fwd(xv, g_ref[...], mod_ref[mo:mo + 1, :], mod_ref[mo + 1:mo + 2, :])
        h = h.astype(ACT)
        h_ref[...] = h
        f = None
        for c0 in range(0, D_FF, tn):
            a = _mm_nt(h, wi_ref[0, c0:c0 + tn, :])
            b = _mm_nt(h, wi_ref[1, c0:c0 + tn, :])
            a_ref[:, c0:c0 + tn] = a.astype(ACT)
            b_ref[:, c0:c0 + tn] = b.astype(ACT)
            part = _mm(a * _sig(a) * b, wo_ref[c0:c0 + tn, :])
            f = part if f is None else f + part
        f_ref[...] = f
        xo = xv + res * mod_ref[mo + 2:mo + 3, :] * f
        xo_ref[...] = xo
        if nxt is not None:
            hn, _, _, _ = _modnorm_fwd(xo, gn_ref[...], mod_ref[nxt[1]:nxt[1] + 1, :], mod_ref[nxt[1] + 1:nxt[1] + 2, :])
            hn_ref[...] = hn.astype(ACT)

    tile = pl.BlockSpec((tm, D), lambda i: (i, 0))
    wide = pl.BlockSpec((tm, D_FF), lambda i: (i, 0))
    row = pl.BlockSpec((1, D), lambda i: (0, 0))
    n_out = 5 if nxt is None else 6
    out = _gridded(
        body, carry, name=name, grid=(T // tm,),
        in_specs=[
            tile,
            pl.BlockSpec((9, D), lambda i: (0, 0)),
            row,
            pl.BlockSpec((2, D_FF, D), lambda i: (0, 0, 0), pipeline_mode=pl.Buffered(1)),
            pl.BlockSpec((D_FF, D), lambda i: (0, 0), pipeline_mode=pl.Buffered(1)),
        ] + ([] if nxt is None else [row]),
        out_specs=[tile, wide, wide, tile, tile] + ([] if nxt is None else [tile]),
        out_shape=[
            jax.ShapeDtypeStruct((T, D), F32),
            jax.ShapeDtypeStruct((T, D_FF), ACT),
            jax.ShapeDtypeStruct((T, D_FF), ACT),
            jax.ShapeDtypeStruct((T, D), F32),
            jax.ShapeDtypeStruct((T, D), ACT),
        ] + ([] if nxt is None else [jax.ShapeDtypeStruct((T, D), ACT)]),
    )(*((x, mod, gnorm, w_in_t, w_out) + (() if nxt is None else (nxt[0],))))
    return out[:n_out], out[n_out:]


def _ffn_bwd_w(h, df, a, b, w_out, name, carry):
    T = h.shape[0]
    tm = min(2048, T)
    ni = T // tm
    tn = 256
    nj = D_FF // tn

    def body(h_ref, df_ref, a_ref, b_ref, wo_ref, da_ref, db_ref, dwi_ref, dwo_ref, acc_i, acc_o):
        i = pl.program_id(1)

        @pl.when(i == 0)
        def _():
            acc_i[...] = jnp.zeros_like(acc_i)
            acc_o[...] = jnp.zeros_like(acc_o)

        hb = h_ref[...]
        df = df_ref[...]
        av = a_ref[...].astype(F32)
        bv = b_ref[...].astype(F32)
        sg = _sig(av)
        sa = av * sg
        s = (sa * bv).astype(MM)
        ds = _mm_nt(df, wo_ref[...])
        da = (ds * bv * sg * (1.0 + av * (1.0 - sg))).astype(MM)
        db = (ds * sa).astype(MM)
        da_ref[...] = da
        db_ref[...] = db
        acc_o[...] += _mm_tn(s, df)
        acc_i[0] += _mm_tn(da, hb)
        acc_i[1] += _mm_tn(db, hb)

        @pl.when(i == ni - 1)
        def _():
            dwi_ref[...] = acc_i[...].astype(MM)
            dwo_ref[...] = acc_o[...].astype(MM)

    out = _gridded(
        body, carry, name=name, grid=(nj, ni),
        in_specs=[
            pl.BlockSpec((tm, D), lambda j, i: (i, 0)),
            pl.BlockSpec((tm, D), lambda j, i: (i, 0)),
            pl.BlockSpec((tm, tn), lambda j, i: (i, j)),
            pl.BlockSpec((tm, tn), lambda j, i: (i, j)),
            pl.BlockSpec((tn, D), lambda j, i: (j, 0)),
        ],
        out_specs=[
            pl.BlockSpec((tm, tn), lambda j, i: (i, j)),
            pl.BlockSpec((tm, tn), lambda j, i: (i, j)),
            pl.BlockSpec((2, tn, D), lambda j, i: (0, j, 0)),
            pl.BlockSpec((tn, D), lambda j, i: (j, 0)),
        ],
        out_shape=[
            jax.ShapeDtypeStruct((T, D_FF), MM),
            jax.ShapeDtypeStruct((T, D_FF), MM),
            jax.ShapeDtypeStruct((2, D_FF, D), MM),
            jax.ShapeDtypeStruct((D_FF, D), MM),
        ],
        scratch_shapes=[pltpu.VMEM((2, tn, D), F32), pltpu.VMEM((tn, D), F32)],
    )(h, df, a, b, w_out)
    return out[:4], out[4:]


def _ffn_bwd_x(x, dxo, f, da, db, mod, mo, gnorm, w_in_t, res, name, carry):
    T = x.shape[0]
    tm = min(512, T)
    ni = T // tm
    tn = D_FF // 2
    nj = D_FF // tn

    def body(x_ref, dxo_ref, f_ref, da_ref, db_ref, mod_ref, g_ref, wi_ref, dx_ref, sm_ref, dh_scr):
        j = pl.program_id(0)
        i = pl.program_id(1)

        @pl.when((j == 0) & (i == 0))
        def _():
            sm_ref[...] = jnp.zeros_like(sm_ref)

        @pl.when(j == 0)
        def _():
            dh_scr[i] = jnp.zeros((tm, D), F32)

        dh_scr[i] += _mm(da_ref[...], wi_ref[0]) + _mm(db_ref[...], wi_ref[1])

        @pl.when(j == nj - 1)
        def _():
            sc = mod_ref[mo + 1:mo + 2, :]
            _, xh, n, r = _modnorm_fwd(x_ref[...], g_ref[...], mod_ref[mo:mo + 1, :], sc)
            dxn, dsh, dsc, dg = _modnorm_bwd(dh_scr[i], xh, n, r, g_ref[...], sc)
            dxo_v = dxo_ref[...]
            dx_ref[...] = dxo_v + dxn
            sm_ref[0:1, :] += dsh
            sm_ref[1:2, :] += dsc
            sm_ref[2:3, :] += _colsum(dxo_v * f_ref[...]) * res
            sm_ref[3:4, :] += dg

    last = pl.BlockSpec((tm, D), lambda j, i: (jnp.where(j == nj - 1, i, 0), 0))
    out = _gridded(
        body, carry, name=name, grid=(nj, ni),
        in_specs=[last, last, last,
                  pl.BlockSpec((tm, tn), lambda j, i: (i, j)), pl.BlockSpec((tm, tn), lambda j, i: (i, j)),
                  pl.BlockSpec((9, D), lambda j, i: (0, 0)), pl.BlockSpec((1, D), lambda j, i: (0, 0)),
                  pl.BlockSpec((2, tn, D), lambda j, i: (0, j, 0))],
        out_specs=[last, pl.BlockSpec((8, D), lambda j, i: (0, 0))],
        out_shape=[jax.ShapeDtypeStruct((T, D), F32), jax.ShapeDtypeStruct((8, D), F32)],
        scratch_shapes=[pltpu.VMEM((ni, tm, D), F32)],
    )(x, dxo, f, da, db, mod, gnorm, w_in_t)
    return out[:2], out[2:]


def _head(x, target, gfin, mod, gate_row, res):
    T = x.shape[0]
    tm = min(512, T)
    ni = T // tm

    def body(x_ref, t_ref, g_ref, mod_ref, dx_ref, df_ref, sm_ref):
        i = pl.program_id(0)

        @pl.when(i == 0)
        def _():
            sm_ref[...] = jnp.zeros_like(sm_ref)

        xv = x_ref[...]
        g = g_ref[...]
        r = lax.rsqrt(_rowmean(xv * xv) + EPS)
        xh = xv * r
        e = xh * g - t_ref[...]
        sm_ref[1:2, :] += _colsum(e * e) * (0.5 / D)
        dy = e * (1.0 / D)
        sm_ref[0:1, :] += _colsum(dy * xh)
        dxh = dy * g
        dx = r * (dxh - xh * _rowmean(dxh * xh))
        dx_ref[...] = dx
        df_ref[...] = (res * mod_ref[gate_row:gate_row + 1, :] * dx).astype(MM)

        @pl.when(i == ni - 1)
        def _():
            sm_ref[1:2, :] = jnp.broadcast_to(jnp.sum(sm_ref[1:2, :], axis=-1, keepdims=True), (1, D))

    tile = pl.BlockSpec((tm, D), lambda i: (i, 0))
    return pl.pallas_call(
        body, name="head_loss", grid=(ni,),
        in_specs=[tile, tile, pl.BlockSpec((1, D), lambda i: (0, 0)), pl.BlockSpec((9, D), lambda i: (0, 0))],
        out_specs=[tile, tile, pl.BlockSpec((8, D), lambda i: (0, 0))],
        out_shape=[jax.ShapeDtypeStruct((T, D), F32), jax.ShapeDtypeStruct((T, D), MM),
                   jax.ShapeDtypeStruct((8, D), F32)],
        compiler_params=_cparams(1),
    )(x, target, gfin, mod)


def _mixin_fwd(h, w, carry):
    T = h.shape[0]
    tm = min(2048, T)
    ni = T // tm

    def body(h_ref, w_ref, p_ref, h_all):
        i = pl.program_id(1)

        @pl.when(pl.program_id(0) == 0)
        def _():
            h_all[i] = h_ref[...]

        p_ref[0] = _mm(h_all[i], w_ref[0])

    first = lambda k, i: (jnp.where(k == 0, i, ni - 1), 0)
    out = _gridded(
        body, carry, name="mixin_fwd", grid=(8, ni),
        in_specs=[pl.BlockSpec((tm, D), first), pl.BlockSpec((1, D, D), lambda k, i: (k, 0, 0))],
        out_specs=[pl.BlockSpec((1, tm, D), lambda k, i: (k, i, 0))],
        out_shape=[jax.ShapeDtypeStruct((8, T, D), F32)],
        scratch_shapes=[pltpu.VMEM((ni, tm, D), ACT)],
    )(h, w)
    return out[:1], out[1:]


def _mixin_bwd(x, h, dxo, dp, mod, mo, gnorm, w, next_gate, next_res, carry):
    T = x.shape[0]
    tm = min(512, T)
    ni = T // tm

    def body(x_ref, h_ref, dxo_ref, dp_ref, mod_ref, g_ref, w_ref, dx_ref, dw_ref, sm_ref, df_ref, dh_scr, acc):
        k = pl.program_id(0)
        i = pl.program_id(1)

        @pl.when(i == 0)
        def _():
            acc[...] = jnp.zeros_like(acc)

        @pl.when(k == 0)
        def _():
            dh_scr[i] = jnp.zeros((tm, D), F32)

        @pl.when((k == 0) & (i == 0))
        def _():
            sm_ref[...] = jnp.zeros_like(sm_ref)

        dpk = dp_ref[0].astype(MM)
        acc[...] += _mm_tn(h_ref[...], dpk)
        dh_scr[i] += _mm_nt(dpk, w_ref[0])

        @pl.when(i == ni - 1)
        def _():
            dw_ref[0] = acc[...].astype(MM)

        @pl.when(k == 7)
        def _():
            sc = mod_ref[mo + 1:mo + 2, :]
            _, xh, n, r = _modnorm_fwd(x_ref[...], g_ref[...], mod_ref[mo:mo + 1, :], sc)
            dxn, dsh, dsc, dg = _modnorm_bwd(dh_scr[i], xh, n, r, g_ref[...], sc)
            dx = dxo_ref[...] + dxn
            dx_ref[...] = dx
            df_ref[...] = (next_res * mod_ref[next_gate:next_gate + 1, :] * dx).astype(MM)
            sm_ref[0:1, :] += dsh
            sm_ref[1:2, :] += dsc
            sm_ref[3:4, :] += dg

    last = pl.BlockSpec((tm, D), lambda k, i: (jnp.where(k == 7, i, 0), 0))
    out = _gridded(
        body, carry, name="mixin_bwd", grid=(8, ni),
        in_specs=[pl.BlockSpec((tm, D), lambda k, i: (jnp.where(k == 7, i, 0), 0)),
                  pl.BlockSpec((tm, D), lambda k, i: (i, 0)),
                  pl.BlockSpec((tm, D), lambda k, i: (jnp.where(k == 7, i, 0), 0)),
                  pl.BlockSpec((1, tm, D), lambda k, i: (k, i, 0)), pl.BlockSpec((9, D), lambda k, i: (0, 0)),
                  pl.BlockSpec((1, D), lambda k, i: (0, 0)), pl.BlockSpec((1, D, D), lambda k, i: (k, 0, 0))],
        out_specs=[last, pl.BlockSpec((1, D, D), lambda k, i: (k, 0, 0)), pl.BlockSpec((8, D), lambda k, i: (0, 0)),
                   last],
        out_shape=[jax.ShapeDtypeStruct((T, D), F32), jax.ShapeDtypeStruct((8, D, D), MM),
                   jax.ShapeDtypeStruct((8, D), F32), jax.ShapeDtypeStruct((T, D), MM)],
        scratch_shapes=[pltpu.VMEM((ni, tm, D), F32), pltpu.VMEM((D, D), F32)],
    )(x, h, dxo, dp, mod, gnorm, w)
    return out[:4], out[4:]


def _hgrn_consts():
    rows = jnp.arange(SUB * HD) // HD
    e = (rows[:, None] == jnp.arange(HD)[None, :]).astype(MM)
    return e, e.T


def _rows_bcast(ref, cb, first, n):
    parts = [jnp.broadcast_to(ref[pl.ds(c * CHUNK + first, 1), :], (n, HD)) for c in range(cb // CHUNK)]
    return jnp.concatenate(parts, axis=0)


def _hgrn_pre(qr, fr, lb_ref, b_scr, cb):
    z = lb_ref[...]
    lb = _sig(z[0:1, :] - z[1:2, :])
    sq = _sig(qr)
    q = qr * sq * Q_SCALE
    sf = _sig(fr)
    fg = lb + (1.0 - lb) * sf
    lf = jnp.log(fg)
    k = 1.0 - fg
    tl = lax.broadcasted_iota(jnp.int32, (cb, HD), 0) % CHUNK
    bc = lf
    sh = 1
    while sh < CHUNK:
        bc = bc + jnp.where(tl >= sh, pltpu.roll(bc, sh, 0), 0.0)
        sh *= 2
    b_scr[...] = bc
    bl = _rows_bcast(b_scr, cb, CHUNK - 1, CHUNK)
    eb = jnp.exp(bc)
    ekd = jnp.exp(bl - bc)
    ekf = jnp.exp(jnp.minimum(-bc, SAFE_EXP))
    return dict(lb=lb, sq=sq, q=q, sf=sf, fg=fg, k=k, tl=tl, b=bc, bl=bl, eb=eb, ekd=ekd, ekf=ekf,
                qe=q * eb, kd=k * ekd, kf=k * ekf, safe=jnp.max(-bc) < SAFE_EXP)


def _hgrn_pre_fused(p_ref, lb_ref, b_scr, q_scr, k_scr, qe_scr, kf_scr, kd_scr, cb):
    z = lb_ref[...]
    lb = _sig(z[0:1, :] - z[1:2, :])
    tl = lax.broadcasted_iota(jnp.int32, (CHUNK, HD), 0)

    def chunk(c, worst):
        r0 = pl.multiple_of(c * CHUNK, CHUNK)
        rs = pl.ds(r0, CHUNK)
        qr = p_ref[0, rs, :]
        q = qr * _sig(qr) * Q_SCALE
        fg = lb + (1.0 - lb) * _sig(p_ref[1, rs, :])
        k = 1.0 - fg
        bc = jnp.log(fg)
        sh = 1
        while sh < CHUNK:
            bc = bc + jnp.where(tl >= sh, pltpu.roll(bc, sh, 0), 0.0)
            sh *= 2
        b_scr[rs, :] = bc
        q_scr[rs, :] = q
        k_scr[rs, :] = k
        qe_scr[rs, :] = (q * jnp.exp(bc)).astype(MM)
        kf_scr[rs, :] = (k * jnp.exp(jnp.minimum(-bc, SAFE_EXP))).astype(MM)
        kd_scr[rs, :] = (k * jnp.exp(b_scr[pl.ds(r0 + CHUNK - 1, 1), :] - bc)).astype(MM)
        return jnp.maximum(worst, -bc)

    worst = lax.fori_loop(0, cb // CHUNK, chunk, jnp.zeros((CHUNK, HD), F32))
    return jnp.max(worst) < SAFE_EXP


def _hgrn_sub(pre, b_scr, cb):
    bc, tl, q, k = pre["b"], pre["tl"], pre["q"], pre["k"]
    br = [None] + [_rows_bcast(b_scr, cb, SUB * i - 1, CHUNK) for i in range(1, NSUB)]
    sb = tl // SUB
    bref = jnp.where(sb == 0, bc, jnp.where(sb == 1, br[1], jnp.where(sb == 2, br[2], br[3])))
    eqo = jnp.exp(bc - bref)
    eko = [None] + [jnp.exp(jnp.where(tl < SUB * i, br[i] - bc, NEG)) for i in range(1, NSUB)]
    return dict(eqo=eqo, eko=eko, qo=q * eqo, ko=[None] + [k * eko[i] for i in range(1, NSUB)])


def _pad_rows(x):
    return jnp.concatenate([x, jnp.zeros_like(x)], axis=0)


def _by_subblock(sbc, parts):
    out = jnp.zeros_like(parts[1])
    for i in range(1, NSUB):
        out = jnp.where(sbc == i, parts[i], out)
    return out


def _hgrn_fwd(p, hgrn_lb, hgrn_g, carry):
    T = p.shape[1]
    cb = min(HGRN_BLOCK, T)
    nch = cb // CHUNK
    ncb = T // cb
    e_mat, _ = _hgrn_consts()

    def body(p_ref, lb_ref, g_ref, e_ref, o_ref, oa_ref, a_ref, s_ref, st_scr, q_scr, k_scr, b_scr, z_scr, ad_scr,
             qe_scr, kf_scr, kd_scr):
        @pl.when(pl.program_id(1) == 0)
        def _():
            st_scr[...] = jnp.zeros_like(st_scr)

        safe = _hgrn_pre_fused(p_ref, lb_ref, b_scr, q_scr, k_scr, qe_scr, kf_scr, kd_scr, cb)
        chunks = [slice(c * CHUNK, (c + 1) * CHUNK) for c in range(nch)]
        row_i = lax.broadcasted_iota(jnp.int32, (CHUNK, HD), 0)
        lane_i = lax.broadcasted_iota(jnp.int32, (CHUNK, HD), 1)
        sbc = row_i // SUB
        causal = lane_i <= row_i

        @pl.when(safe)
        def _():
            for rs in chunks:
                ad_scr[rs, :] = jnp.where(causal, _mm_nt(qe_scr[rs, :], _pad_rows(kf_scr[rs, :])), 0.0)

        @pl.when(jnp.logical_not(safe))
        def _():
            tl = lax.broadcasted_iota(jnp.int32, (cb, HD), 0) % CHUNK
            sub = _hgrn_sub(dict(b=b_scr[...], tl=tl, q=q_scr[...], k=k_scr[...]), b_scr, cb)
            ti = lax.broadcasted_iota(jnp.int32, (SUB, HD), 0)

            def zbody(c, carry):
                for i in range(NSUB):
                    r0 = pl.multiple_of(c * CHUNK + SUB * i, SUB)
                    qi = q_scr[pl.ds(r0, SUB), :]
                    bi = b_scr[pl.ds(r0, SUB), :]
                    for s in range(SUB):
                        krow = k_scr[pl.ds(r0 + s, 1), :]
                        brow = b_scr[pl.ds(r0 + s, 1), :]
                        if s < 8:
                            zz = qi * krow * jnp.exp(jnp.where(ti >= s, bi - brow, NEG))
                        else:
                            lo = qi[8:] * krow * jnp.exp(jnp.where(ti[8:] >= s, bi[8:] - brow, NEG))
                            zz = jnp.concatenate([jnp.zeros((8, HD), F32), lo], axis=0)
                        z_scr[i, pl.ds(pl.multiple_of(c * SUB, SUB), SUB), s * HD:(s + 1) * HD] = zz.astype(MM)
                return carry

            lax.fori_loop(0, nch, zbody, 0)
            adiag = [_mm(z_scr[i], e_ref[...]) for i in range(NSUB)]
            offs = [[_mm_nt(sub["qo"][rs], _pad_rows(sub["ko"][i][rs])) for i in range(1, NSUB)] for rs in chunks]
            for c, rs in enumerate(chunks):
                dparts = []
                for i in range(NSUB):
                    blk = adiag[i][c * SUB:(c + 1) * SUB]
                    dparts.append(blk if i == 0 else pltpu.roll(blk, SUB * i, 1))
                ad_scr[rs, :] = _by_subblock(sbc, [None] + offs[c]) + jnp.concatenate(dparts, axis=0)

        kv = [_mm_tn(p_ref[2, rs, :], kd_scr[rs, :]) for rs in chunks]
        a_ref[0] = ad_scr[...]
        o_intra = [_mm(ad_scr[rs, :], _pad_rows(p_ref[2, rs, :])) for rs in chunks]
        states = []
        st = st_scr[...]
        for c in range(nch):
            states.append(st)
            st = st * jnp.exp(b_scr[pl.ds(c * CHUNK + CHUNK - 1, 1), :]) + kv[c]
        st_scr[...] = st
        g = g_ref[...]
        for c, rs in enumerate(chunks):
            s_ref[0, c] = states[c]
            o = o_intra[c] + _mm_nt(qe_scr[rs, :], states[c])
            o_ref[rs, :] = o
            og = p_ref[3, rs, :]
            oa_ref[rs, :] = (o * lax.rsqrt(_rowmean(o * o) + EPS) * g * og * _sig(og)).astype(ACT)

    out = _gridded(
        body, carry, name="hgrn_fwd", grid=(HEADS, ncb),
        in_specs=[pl.BlockSpec((4, cb, HD), lambda h, c: (0, c, h)),
                  pl.BlockSpec((2, HD), lambda h, c: (0, h)),
                  pl.BlockSpec((1, HD), lambda h, c: (0, h)),
                  pl.BlockSpec((SUB * HD, HD), lambda h, c: (0, 0))],
        out_specs=[pl.BlockSpec((cb, HD), lambda h, c: (c, h)),
                   pl.BlockSpec((cb, HD), lambda h, c: (c, h)),
                   pl.BlockSpec((1, cb, HD), lambda h, c: (h, c, 0)),
                   pl.BlockSpec((1, nch, HD, HD), lambda h, c: (h, c, 0, 0))],
        out_shape=[jax.ShapeDtypeStruct((T, D), F32), jax.ShapeDtypeStruct((T, D), ACT),
                   jax.ShapeDtypeStruct((HEADS, T, HD), F32),
                   jax.ShapeDtypeStruct((HEADS, T // CHUNK, HD, HD), F32)],
        scratch_shapes=[pltpu.VMEM((HD, HD), F32), pltpu.VMEM((cb, HD), F32), pltpu.VMEM((cb, HD), F32),
                        pltpu.VMEM((cb, HD), F32), pltpu.VMEM((NSUB, nch * SUB, SUB * HD), MM),
                        pltpu.VMEM((cb, HD), F32), pltpu.VMEM((cb, HD), MM), pltpu.VMEM((cb, HD), MM),
                        pltpu.VMEM((cb, HD), MM)],
    )(p, hgrn_lb, hgrn_g, e_mat)
    return out[:4], out[4:]


def _hgrn_bwd(p, o, a_all, s_all, doa, hgrn_lb, hgrn_g, dp, carry):
    T = p.shape[1]
    cb = min(HGRN_BLOCK, T)
    nch = cb // CHUNK
    ncb = T // cb
    _, et_mat = _hgrn_consts()

    def body(p_ref, o_ref, a_ref, s_ref, doa_ref, lb_ref, g_ref, et_ref, dp_in, dp_ref, sm_ref,
             dst_scr, q_scr, k_scr, b_scr, x_scr, dqd_scr, dkd_scr):
        del dp_in

        @pl.when(pl.program_id(1) == 0)
        def _():
            dst_scr[...] = jnp.zeros_like(dst_scr)
            sm_ref[...] = jnp.zeros_like(sm_ref)

        qr = p_ref[0]
        v = p_ref[2]
        og = p_ref[3]
        pre = _hgrn_pre(qr, p_ref[1], lb_ref, b_scr, cb)
        q, k = pre["q"], pre["k"]
        g = g_ref[...]
        ov = o_ref[...]
        r = lax.rsqrt(_rowmean(ov * ov) + EPS)
        oh = ov * r
        sgo = _sig(og)
        doa_v = doa_ref[...]
        don = doa_v * og * sgo
        dog = doa_v * oh * g * sgo * (1.0 + og * (1.0 - sgo))
        sm_ref[1:2, :] += _colsum(don * oh)
        doh = don * g
        do = r * (doh - oh * _rowmean(doh * oh))

        sbc = lax.broadcasted_iota(jnp.int32, (CHUNK, HD), 0) // SUB
        row_i = lax.broadcasted_iota(jnp.int32, (CHUNK, HD), 0)
        lane_i = lax.broadcasted_iota(jnp.int32, (CHUNK, HD), 1)
        causal = lane_i <= row_i
        chunks = [slice(c * CHUNK, (c + 1) * CHUNK) for c in range(nch)]
        da_parts = [jnp.where(causal, _mm_nt(do[rs], _pad_rows(v[rs])), 0.0) for rs in chunks]
        dv_parts = [_mm_tn(a_ref[0, rs, :], do[rs])[:CHUNK] for rs in chunks]

        @pl.when(pre["safe"])
        def _():
            hi = dict(preferred_element_type=F32, precision=lax.Precision.HIGH)
            for c, rs in enumerate(chunks):
                dqd_scr[rs, :] = pre["eb"][rs] * lax.dot_general(
                    da_parts[c], _pad_rows(pre["kf"][rs]), (((1,), (0,)), ((), ())), **hi)
                dkd_scr[rs, :] = pre["ekf"][rs] * lax.dot_general(
                    da_parts[c], pre["qe"][rs], (((0,), (0,)), ((), ())), **hi)[:CHUNK]

        @pl.when(jnp.logical_not(pre["safe"]))
        def _():
            sub = _hgrn_sub(pre, b_scr, cb)
            dqoff_mm = [[_mm(da_parts[c], _pad_rows(sub["ko"][i][rs])) for i in range(1, NSUB)]
                        for c, rs in enumerate(chunks)]
            dkoff_mm = [[_mm_tn(jnp.where(sbc == i, da_parts[c], 0.0), sub["qo"][rs])[:CHUNK]
                         for i in range(1, NSUB)] for c, rs in enumerate(chunks)]
            dqoff_parts = [_by_subblock(sbc, [None] + dqoff_mm[c]) for c in range(nch)]
            dkoff_parts = []
            for c, rs in enumerate(chunks):
                dko = sub["eko"][1][rs] * dkoff_mm[c][0]
                for i in range(2, NSUB):
                    dko = dko + sub["eko"][i][rs] * dkoff_mm[c][i - 1]
                dkoff_parts.append(dko)
            q_scr[...] = q
            k_scr[...] = k
            for i in range(NSUB):
                rows = []
                for c in range(nch):
                    blk = da_parts[c][SUB * i:SUB * (i + 1)]
                    rows.append(blk if i == 0 else pltpu.roll(blk, HD - SUB * i, 1))
                x_scr[i] = _mm(jnp.concatenate(rows, axis=0), et_ref[...])
            ti = lax.broadcasted_iota(jnp.int32, (SUB, HD), 0)

            def dbody(c, carry):
                for i in range(NSUB):
                    r0 = pl.multiple_of(c * CHUNK + SUB * i, SUB)
                    qi = q_scr[pl.ds(r0, SUB), :]
                    bi = b_scr[pl.ds(r0, SUB), :]
                    dq_hi = jnp.zeros((8, HD), F32)
                    dq_lo = jnp.zeros((8, HD), F32)
                    dk_hi = jnp.zeros((8, HD), F32)
                    dk_lo = jnp.zeros((8, HD), F32)
                    c0 = pl.multiple_of(c * SUB, SUB)
                    t8 = ti[:8]
                    for s in range(SUB):
                        krow = k_scr[pl.ds(r0 + s, 1), :]
                        brow = b_scr[pl.ds(r0 + s, 1), :]
                        w_lo = (x_scr[i, pl.ds(c0 + 8, 8), s * HD:(s + 1) * HD]
                                * jnp.exp(jnp.where(t8 + 8 >= s, bi[8:] - brow, NEG)))
                        dq_lo = dq_lo + w_lo * krow
                        col = _colsum(w_lo * qi[8:])
                        if s < 8:
                            w_hi = (x_scr[i, pl.ds(c0, 8), s * HD:(s + 1) * HD]
                                    * jnp.exp(jnp.where(t8 >= s, bi[:8] - brow, NEG)))
                            dq_hi = dq_hi + w_hi * krow
                            dk_hi = jnp.where(t8 == s, col + _colsum(w_hi * qi[:8]), dk_hi)
                        else:
                            dk_lo = jnp.where(t8 + 8 == s, col, dk_lo)
                    dqd_scr[pl.ds(r0, SUB), :] = jnp.concatenate([dq_hi, dq_lo], axis=0)
                    dkd_scr[pl.ds(r0, SUB), :] = jnp.concatenate([dk_hi, dk_lo], axis=0)
                return carry

            lax.fori_loop(0, nch, dbody, 0)
            dqd_scr[...] += jnp.concatenate(dqoff_parts, axis=0) * sub["eqo"]
            dkd_scr[...] += jnp.concatenate(dkoff_parts, axis=0)

        qdo = [_mm_tn(do[rs], pre["qe"][rs]) for rs in chunks]
        dsts = [None] * nch
        dst = dst_scr[...]
        for c in reversed(range(nch)):
            dsts[c] = dst
            dst = dst * jnp.exp(b_scr[pl.ds(c * CHUNK + CHUNK - 1, 1), :]) + qdo[c]
        dst_scr[...] = dst
        sts = [s_ref[0, c] for c in range(nch)]
        dqe_parts = [_mm(do[rs], sts[c]) for c, rs in enumerate(chunks)]
        dkdec_parts = [_mm(v[rs], dsts[c]) for c, rs in enumerate(chunks)]
        dvi_parts = [_mm_nt(pre["kd"][rs], dsts[c]) for c, rs in enumerate(chunks)]
        debl_parts = [_colsum(dsts[c] * sts[c]) for c in range(nch)]
        dqe = jnp.concatenate(dqe_parts, axis=0)
        dkdec = jnp.concatenate(dkdec_parts, axis=0)
        dq_tot = dqd_scr[...] + dqe * pre["eb"]
        dk_inter = dkdec * pre["ekd"]
        dk_tot = dkd_scr[...] + dk_inter
        db = q * dq_tot - k * dk_tot
        kdk = k * dk_inter
        dbl = jnp.concatenate(
            [jnp.broadcast_to(jnp.exp(b_scr[pl.ds(c * CHUNK + CHUNK - 1, 1), :]) * debl_parts[c]
                              + _colsum(kdk[c * CHUNK:(c + 1) * CHUNK]), (CHUNK, HD)) for c in range(nch)], axis=0)
        tl = pre["tl"]
        rc = db
        sh = 1
        while sh < CHUNK:
            rc = rc + jnp.where(tl + sh < CHUNK, pltpu.roll(rc, cb - sh, 0), 0.0)
            sh *= 2
        dlf = rc + dbl
        dfg = dlf / pre["fg"] - dk_tot
        sf = pre["sf"]
        lb = pre["lb"]
        sm_ref[0:1, :] += _colsum(dfg * (1.0 - sf))
        sq = pre["sq"]
        dp_ref[0] = (dq_tot * Q_SCALE * sq * (1.0 + qr * (1.0 - sq))).astype(ACT)
        dp_ref[1] = (dfg * (1.0 - lb) * sf * (1.0 - sf)).astype(ACT)
        dp_ref[2] = (jnp.concatenate(dv_parts, axis=0) + jnp.concatenate(dvi_parts, axis=0)).astype(ACT)
        dp_ref[3] = dog.astype(ACT)

    rev = lambda c: ncb - 1 - c
    out = _gridded(
        body, carry, name="hgrn_bwd", grid=(HEADS, ncb),
        in_specs=[pl.BlockSpec((4, cb, HD), lambda h, c: (0, rev(c), h)),
                  pl.BlockSpec((cb, HD), lambda h, c: (rev(c), h)),
                  pl.BlockSpec((1, cb, HD), lambda h, c: (h, rev(c), 0)),
                  pl.BlockSpec((1, nch, HD, HD), lambda h, c: (h, rev(c), 0, 0)),
                  pl.BlockSpec((cb, HD), lambda h, c: (rev(c), h)),
                  pl.BlockSpec((2, HD), lambda h, c: (0, h)),
                  pl.BlockSpec((1, HD), lambda h, c: (0, h)),
                  pl.BlockSpec((HD, SUB * HD), lambda h, c: (0, 0)),
                  pl.BlockSpec(memory_space=pl.ANY)],
        out_specs=[pl.BlockSpec((4, cb, HD), lambda h, c: (0, rev(c), h)),
                   pl.BlockSpec((8, HD), lambda h, c: (0, h))],
        out_shape=[jax.ShapeDtypeStruct(dp.shape, dp.dtype), jax.ShapeDtypeStruct((8, D), F32)],
        aliases={8: 0},
        scratch_shapes=[pltpu.VMEM((HD, HD), F32), pltpu.VMEM((cb, HD), F32), pltpu.VMEM((cb, HD), F32),
                        pltpu.VMEM((cb, HD), F32), pltpu.VMEM((NSUB, nch * SUB, SUB * HD), F32),
                        pltpu.VMEM((cb, HD), F32), pltpu.VMEM((cb, HD), F32)],
    )(p, o, a_all, s_all, doa, hgrn_lb, hgrn_g, et_mat, dp)
    return out[:2], out[2:]


def _ln_fwd(u1, g, b):
    mu = _rowmean(u1)
    xc = u1 - mu
    rs = lax.rsqrt(_rowmean(xc * xc) + EPS)
    xh = xc * rs
    return xh * g + b, xh, rs


CONV_RB = 64
LANES = 128


def _shift_rows(src, sh, ls, n):
    for r in range(1, 8):
        sh[r - 1, 0:n, :] = src[pl.ds(r, n), ls]


def _tap(src, sh, ls, off, r0, rows):
    r = off % 8
    if r == 0:
        return src[pl.ds(r0 + off, rows), ls]
    return sh[r - 1, pl.ds(r0 + off - r, rows), :]


def _conv_fwd(p, cw, cb_, lng, lnb, carry):
    T = p.shape[1]
    tm = min(512, T)
    n = HALO + tm - 8

    def body(p_ref, cw_ref, cb_ref, g_ref, b_ref, u1_ref, u2_ref, buf, sh):
        @pl.when(pl.program_id(0) == 0)
        def _():
            buf[0:HALO, :] = jnp.zeros((HALO, D), F32)

        buf[HALO:HALO + tm, :] = p_ref[0] * _sig(p_ref[1])
        for lb in range(D // LANES):
            ls = slice(lb * LANES, (lb + 1) * LANES)
            _shift_rows(buf, sh, ls, n)
            taps = [cw_ref[j:j + 1, ls] for j in range(CONV_K)]
            bias = cb_ref[:, ls]

            def rows_body(rb, carry):
                r0 = pl.multiple_of(rb * CONV_RB, CONV_RB)
                acc = jnp.broadcast_to(bias, (CONV_RB, LANES))
                for j in range(CONV_K):
                    acc = acc + taps[j] * _tap(buf, sh, ls, HALO - (CONV_K - 1) + j, r0, CONV_RB)
                u1_ref[pl.ds(r0, CONV_RB), ls] = acc
                return carry

            lax.fori_loop(0, tm // CONV_RB, rows_body, 0)
        y, _, _ = _ln_fwd(u1_ref[...], g_ref[...], b_ref[...])
        u2_ref[...] = (y * _sig(y)).astype(ACT)
        buf[0:HALO, :] = buf[tm:tm + HALO, :]

    out = _gridded(
        body, carry, name="conv_fwd", grid=(T // tm,),
        in_specs=[pl.BlockSpec((2, tm, D), lambda i: (2, i, 0)), pl.BlockSpec((HALO, D), lambda i: (0, 0)),
                  pl.BlockSpec((1, D), lambda i: (0, 0)), pl.BlockSpec((1, D), lambda i: (0, 0)),
                  pl.BlockSpec((1, D), lambda i: (0, 0))],
        out_specs=[pl.BlockSpec((tm, D), lambda i: (i, 0)), pl.BlockSpec((tm, D), lambda i: (i, 0))],
        out_shape=[jax.ShapeDtypeStruct((T, D), F32), jax.ShapeDtypeStruct((T, D), ACT)],
        scratch_shapes=[pltpu.VMEM((HALO + tm, D), F32), pltpu.VMEM((7, n, LANES), F32)],
    )(p, cw, cb_, lng, lnb)
    return out[:2], out[2:]


def _conv_bwd(p, u1, du2, cw, lng, lnb, dp, carry):
    T = p.shape[1]
    tm = min(512, T)
    ni = T // tm
    hb = tm // HALO

    n = HALO + tm - 8

    def body(p_ref, ph_ref, u1_ref, du2_ref, cw_ref, g_ref, b_ref, dp_in, dp_ref, dcw_ref, sm_ref, ubuf, dbuf,
             sh, dacc):
        del dp_in
        step = pl.program_id(0)

        @pl.when(step == 0)
        def _():
            dbuf[tm:tm + HALO, :] = jnp.zeros((HALO, D), F32)
            dcw_ref[...] = jnp.zeros_like(dcw_ref)
            sm_ref[...] = jnp.zeros_like(sm_ref)

        ua = p_ref[0]
        sgb = _sig(p_ref[1])
        halo = ph_ref[0] * _sig(ph_ref[1])
        ubuf[0:HALO, :] = jnp.where(step == ni - 1, 0.0, halo)
        ubuf[HALO:HALO + tm, :] = ua * sgb
        g = g_ref[...]
        y, xh, rs = _ln_fwd(u1_ref[...], g, b_ref[...])
        sy = _sig(y)
        dy = du2_ref[...] * sy * (1.0 + y * (1.0 - sy))
        sm_ref[1:2, :] += _colsum(dy * xh)
        sm_ref[2:3, :] += _colsum(dy)
        dxh = dy * g
        du1 = rs * (dxh - _rowmean(dxh) - xh * _rowmean(dxh * xh))
        sm_ref[0:1, :] += _colsum(du1)
        dbuf[0:tm, :] = du1
        for lb in range(D // LANES):
            ls = slice(lb * LANES, (lb + 1) * LANES)
            taps = [cw_ref[j:j + 1, ls] for j in range(CONV_K)]
            _shift_rows(dbuf, sh, ls, n)

            def du0_body(rb, carry):
                r0 = pl.multiple_of(rb * CONV_RB, CONV_RB)
                acc = jnp.zeros((CONV_RB, LANES), F32)
                for j in range(CONV_K):
                    acc = acc + taps[j] * _tap(dbuf, sh, ls, CONV_K - 1 - j, r0, CONV_RB)
                dp_ref[0, pl.ds(r0, CONV_RB), ls] = acc.astype(ACT)
                return carry

            lax.fori_loop(0, tm // CONV_RB, du0_body, 0)
            _shift_rows(ubuf, sh, ls, n)
            dacc[...] = jnp.zeros_like(dacc)

            def dcw_body(rb, carry):
                r0 = pl.multiple_of(rb * CONV_RB, CONV_RB)
                d = dbuf[pl.ds(r0, CONV_RB), ls]
                for j in range(CONV_K):
                    prod = d * _tap(ubuf, sh, ls, HALO - (CONV_K - 1) + j, r0, CONV_RB)
                    dacc[8 * j:8 * j + 8, :] += jnp.sum(prod.reshape(CONV_RB // 8, 8, LANES), axis=0)
                return carry

            lax.fori_loop(0, tm // CONV_RB, dcw_body, 0)
            for j in range(CONV_K):
                dcw_ref[j:j + 1, ls] += _colsum(dacc[8 * j:8 * j + 8, :])
        du0 = dp_ref[0].astype(F32)
        dp_ref[0] = (du0 * sgb).astype(ACT)
        dp_ref[1] = (du0 * ua * sgb * (1.0 - sgb)).astype(ACT)
        dbuf[tm:tm + HALO, :] = dbuf[0:HALO, :]

    rev = lambda i: ni - 1 - i
    out = _gridded(
        body, carry, name="conv_bwd", grid=(ni,),
        in_specs=[pl.BlockSpec((2, tm, D), lambda i: (2, rev(i), 0)),
                  pl.BlockSpec((2, HALO, D), lambda i: (2, jnp.maximum(rev(i) * hb - 1, 0), 0)),
                  pl.BlockSpec((tm, D), lambda i: (rev(i), 0)), pl.BlockSpec((tm, D), lambda i: (rev(i), 0)),
                  pl.BlockSpec((HALO, D), lambda i: (0, 0)), pl.BlockSpec((1, D), lambda i: (0, 0)),
                  pl.BlockSpec((1, D), lambda i: (0, 0)), pl.BlockSpec(memory_space=pl.ANY)],
        out_specs=[pl.BlockSpec((2, tm, D), lambda i: (2, rev(i), 0)),
                   pl.BlockSpec((HALO, D), lambda i: (0, 0)), pl.BlockSpec((8, D), lambda i: (0, 0))],
        out_shape=[jax.ShapeDtypeStruct(dp.shape, dp.dtype), jax.ShapeDtypeStruct((HALO, D), F32),
                   jax.ShapeDtypeStruct((8, D), F32)],
        aliases={7: 0},
        scratch_shapes=[pltpu.VMEM((HALO + tm, D), F32), pltpu.VMEM((tm + HALO, D), F32),
                        pltpu.VMEM((7, n, LANES), F32), pltpu.VMEM((8 * CONV_K, LANES), F32)],
    )(p, p, u1, du2, cw, lng, lnb, dp)
    return out[:3], out[3:]


def _mixout_fwd(x, oa, u2, p, mod, mo, w_a, w_b, w_o):
    T = x.shape[0]
    tm = min(512, T)

    def body(x_ref, oa_ref, u2_ref, p_ref, mod_ref, wa_ref, wb_ref, wo_ref, xo_ref, ya_ref, yb_ref, mo_ref):
        ya = _mm(oa_ref[...], wa_ref[...])
        yb = _mm(u2_ref[...], wb_ref[...])
        ya_ref[...] = ya.astype(ACT)
        yb_ref[...] = yb.astype(ACT)
        merged = _sig(p_ref[0]) * ya + _sig(p_ref[1]) * yb
        out = _mm(merged, wo_ref[...])
        mo_ref[...] = out
        xo_ref[...] = x_ref[...] + mod_ref[mo + 2:mo + 3, :] * out

    tile = pl.BlockSpec((tm, D), lambda i: (i, 0))
    wspec = pl.BlockSpec((D, D), lambda i: (0, 0))
    return pl.pallas_call(
        body, name="mixout_fwd", grid=(T // tm,),
        in_specs=[tile, tile, tile, pl.BlockSpec((2, tm, D), lambda i: (3, i, 0)),
                  pl.BlockSpec((9, D), lambda i: (0, 0)), wspec, wspec, wspec],
        out_specs=[tile, tile, tile, tile],
        out_shape=[jax.ShapeDtypeStruct((T, D), F32), jax.ShapeDtypeStruct((T, D), ACT),
                   jax.ShapeDtypeStruct((T, D), ACT), jax.ShapeDtypeStruct((T, D), F32)],
        compiler_params=_cparams(1),
    )(x, oa, u2, p, mod, w_a, w_b, w_o)


def _mixout_bwd(dxo, oa, u2, ya, yb, mout, p, mod, mo, w_a, w_b, w_o):
    T = dxo.shape[0]
    tm = min(256, T)

    def body(dxo_ref, oa_ref, u2_ref, ya_ref, yb_ref, mo_ref, p_ref, mod_ref, wa_ref, wb_ref, wo_ref,
             dp_ref, doa_ref, du2_ref, dwa_ref, dwb_ref, dwo_ref, sm_ref):
        @pl.when(pl.program_id(0) == 0)
        def _():
            dwa_ref[...] = jnp.zeros_like(dwa_ref)
            dwb_ref[...] = jnp.zeros_like(dwb_ref)
            dwo_ref[...] = jnp.zeros_like(dwo_ref)
            sm_ref[...] = jnp.zeros_like(sm_ref)

        dxo_v = dxo_ref[...]
        sm_ref[2:3, :] += _colsum(dxo_v * mo_ref[...])
        dmo = (mod_ref[mo + 2:mo + 3, :] * dxo_v).astype(MM)
        ya = ya_ref[...].astype(F32)
        yb = yb_ref[...].astype(F32)
        sga = _sig(p_ref[0])
        sgb = _sig(p_ref[1])
        merged = (sga * ya + sgb * yb).astype(MM)
        dwo_ref[...] += _mm_tn(merged, dmo)
        dmg = _mm_nt(dmo, wo_ref[...])
        dp_ref[0] = (dmg * ya * sga * (1.0 - sga)).astype(ACT)
        dp_ref[1] = (dmg * yb * sgb * (1.0 - sgb)).astype(ACT)
        dya = (dmg * sga).astype(MM)
        dyb = (dmg * sgb).astype(MM)
        dwa_ref[...] += _mm_tn(oa_ref[...], dya)
        dwb_ref[...] += _mm_tn(u2_ref[...], dyb)
        doa_ref[...] = _mm_nt(dya, wa_ref[...])
        du2_ref[...] = _mm_nt(dyb, wb_ref[...])

    tile = pl.BlockSpec((tm, D), lambda i: (i, 0))
    wspec = pl.BlockSpec((D, D), lambda i: (0, 0))
    return pl.pallas_call(
        body, name="mixout_bwd", grid=(T // tm,),
        in_specs=[tile, tile, tile, tile, tile, tile, pl.BlockSpec((2, tm, D), lambda i: (3, i, 0)),
                  pl.BlockSpec((9, D), lambda i: (0, 0)), wspec, wspec, wspec],
        out_specs=[pl.BlockSpec((2, tm, D), lambda i: (3, i, 0)), tile, tile, wspec, wspec, wspec,
                   pl.BlockSpec((8, D), lambda i: (0, 0))],
        out_shape=[jax.ShapeDtypeStruct((8, T, D), ACT), jax.ShapeDtypeStruct((T, D), F32),
                   jax.ShapeDtypeStruct((T, D), F32), jax.ShapeDtypeStruct((D, D), F32),
                   jax.ShapeDtypeStruct((D, D), F32), jax.ShapeDtypeStruct((D, D), F32),
                   jax.ShapeDtypeStruct((8, D), F32)],
        compiler_params=_cparams(1),
    )(dxo, oa, u2, ya, yb, mout, p, mod, w_a, w_b, w_o)


def _ada_wgrad(cs_all, dmod_cols):
    cs_t = jnp.pad(cs_all.T, ((0, 0), (0, HD - N_DEV)))
    dm = jnp.pad(dmod_cols, ((0, HD - N_DEV), (0, 0)))

    def body(cs_ref, d_ref, out_ref):
        out_ref[...] = jnp.dot(cs_ref[...], d_ref[...], preferred_element_type=F32,
                               precision=lax.Precision.HIGHEST)

    return pl.pallas_call(
        body, name="ada_wgrad", out_shape=jax.ShapeDtypeStruct((D, dmod_cols.shape[1]), F32),
        compiler_params=pltpu.CompilerParams(vmem_limit_bytes=VMEM_LIMIT),
    )(cs_t, dm)


def _adam_math(w, g, m, v):
    m2 = ADAM_B1 * m + (1.0 - ADAM_B1) * g
    v2 = ADAM_B2 * v + (1.0 - ADAM_B2) * (g * g)
    m_hat = m2 / (1.0 - ADAM_B1 ** ADAM_STEP)
    v_hat = v2 / (1.0 - ADAM_B2 ** ADAM_STEP)
    delta = -ADAM_LR * (m_hat / (jnp.sqrt(v_hat) + ADAM_EPS) + ADAM_WD * w)
    return delta, m2, v2


def _adamw(w, m, v, g, name):
    R, C = w.shape
    slots = g.ndim == 3
    n_slots = g.shape[0] if slots else 0
    tr = R
    for cand in (256, 176):
        if R % cand == 0 and R > cand:
            tr = cand
            break

    def body(w_ref, m_ref, v_ref, g_ref, go_ref, d_ref, mo_ref, vo_ref):
        if slots:
            gv = g_ref[0].astype(F32)
            for s in range(1, n_slots):
                gv = gv + g_ref[s].astype(F32)
        else:
            gv = g_ref[...]
        go_ref[...] = gv
        d_ref[...], mo_ref[...], vo_ref[...] = _adam_math(w_ref[...], gv, m_ref[...], v_ref[...])

    tile = pl.BlockSpec((tr, C), lambda i: (i, 0))
    gspec = pl.BlockSpec((n_slots, tr, C), lambda i: (0, i, 0)) if slots else tile
    sds = jax.ShapeDtypeStruct((R, C), F32)
    return pl.pallas_call(
        body, name=name, grid=(R // tr,), in_specs=[tile, tile, tile, gspec], out_specs=[tile] * 4,
        out_shape=[sds] * 4, compiler_params=_cparams(1),
    )(w, m, v, g)


def _sum_slots(pack, name, tr):
    n, R, C = pack.shape

    def body(p_ref, out_ref):
        acc = p_ref[0].astype(F32)
        for s in range(1, n):
            acc = acc + p_ref[s].astype(F32)
        out_ref[...] = acc

    return pl.pallas_call(
        body, name=name, grid=(R // tr,), in_specs=[pl.BlockSpec((n, tr, C), lambda i: (0, i, 0))],
        out_specs=pl.BlockSpec((tr, C), lambda i: (i, 0)), out_shape=jax.ShapeDtypeStruct((R, C), F32),
        compiler_params=_cparams(1))(pack)


def _me():
    return lax.axis_index("x"), lax.axis_index("y"), lax.axis_index("c")


def _peer(r):
    x, y, c = _me()
    px = 1 - x if r & 4 else x
    py = 1 - y if r & 2 else y
    pc = 1 - c if r & 1 else c
    return (px, py, pc), 4 * px + 2 * py + pc


def _small_gather(x_ref, out_ref, send_sems, recv_sems):
    R = x_ref.shape[0]
    mx, my, mc = _me()
    me = 4 * mx + 2 * my + mc
    mine = out_ref.at[pl.ds(pl.multiple_of(me * R, 8), R), :]
    copies = []
    for r in range(1, N_DEV):
        dev, _ = _peer(r)
        copies.append(pltpu.make_async_remote_copy(
            src_ref=x_ref, dst_ref=mine, send_sem=send_sems.at[r - 1], recv_sem=recv_sems.at[r - 1],
            device_id=dev, device_id_type=MESH))
    for cp in copies:
        cp.start()
    mine[...] = x_ref[...]
    for r in range(1, N_DEV):
        dev, idx = _peer(r)
        theirs = out_ref.at[pl.ds(pl.multiple_of(idx * R, 8), R), :]
        pltpu.make_async_remote_copy(
            src_ref=x_ref, dst_ref=theirs, send_sem=send_sems.at[r - 1], recv_sem=recv_sems.at[r - 1],
            device_id=dev, device_id_type=MESH).wait_recv()
    for cp in copies:
        cp.wait_send()


def _prologue(cs, ada_w, ada_b_cols, big):
    n = len(big)
    ncol = ada_w.shape[1]
    big_shape, big_sems = _xchg_specs(big, "gather")

    def body(cs_ref, w_ref, b_ref, *rest):
        big_in, cs_all, mod_all, big_out = rest[:n], rest[n], rest[n + 1], rest[n + 2:2 * n + 2]
        mod_scr, s1, r1, s2, r2 = rest[2 * n + 2:2 * n + 7]
        sems = rest[2 * n + 7:]
        _small_gather(cs_ref, cs_all, s1, r1)
        pick = (lax.broadcasted_iota(jnp.int32, (N_DEV, N_DEV * 8), 1)
                == 8 * lax.broadcasted_iota(jnp.int32, (N_DEV, N_DEV * 8), 0)).astype(F32)
        per_device = jnp.dot(pick, cs_all[...], preferred_element_type=F32, precision=lax.Precision.HIGHEST)
        mod_scr[...] = jnp.dot(per_device, w_ref[...], preferred_element_type=F32,
                               precision=lax.Precision.HIGHEST) + b_ref[...]
        _small_gather(mod_scr, mod_all, s2, r2)
        _xchg_start(big_in, big_out, sems, "gather")
        _xchg_wait(big_in, big_out, sems, "gather")

    vmem = pl.BlockSpec(memory_space=pltpu.VMEM)
    hbm = pl.BlockSpec(memory_space=pl.ANY)
    dma7 = pltpu.SemaphoreType.DMA((N_DEV - 1,))
    out = pl.pallas_call(
        body, name="prologue",
        out_shape=[jax.ShapeDtypeStruct((N_DEV * 8, D), F32), jax.ShapeDtypeStruct((N_DEV * 8, ncol), F32)]
        + big_shape,
        in_specs=[vmem, vmem, vmem] + [hbm] * n, out_specs=[vmem, vmem] + [hbm] * n,
        scratch_shapes=[pltpu.VMEM((8, ncol), F32), dma7, dma7, dma7, dma7] + big_sems,
        compiler_params=pltpu.CompilerParams(vmem_limit_bytes=VMEM_LIMIT),
    )(cs, ada_w, ada_b_cols, *big)
    return out[0], out[1], out[2:]


def _allgather_small(x):
    R, C = x.shape

    def body(x_ref, out_ref, send_sems, recv_sems):
        _small_gather(x_ref, out_ref, send_sems, recv_sems)

    return pl.pallas_call(
        body, name="allgather_small_%dx%d" % (R, C),
        out_shape=jax.ShapeDtypeStruct((N_DEV * R, C), F32),
        in_specs=[pl.BlockSpec(memory_space=pltpu.VMEM)], out_specs=pl.BlockSpec(memory_space=pltpu.VMEM),
        scratch_shapes=[pltpu.SemaphoreType.DMA((N_DEV - 1,)), pltpu.SemaphoreType.DMA((N_DEV - 1,))],
    )(x)


N_CHIP = N_DEV // 2


def _xchg_copies(ins, outs, sems, mode):
    send_sems, recv_sems, local_sems = sems
    mx, my, mc = _me()
    me = 4 * mx + 2 * my + mc
    my_chip = 2 * mx + my
    sibling = _peer(1)[0]

    def rdma(a, r, dev, src, slot):
        k = a * (N_DEV - 1) + r - 1
        return pltpu.make_async_remote_copy(
            src_ref=src, dst_ref=outs[a].at[slot], send_sem=send_sems.at[k], recv_sem=recv_sems.at[k],
            device_id=dev, device_id_type=MESH)

    own, sends, relays, recvs = [], [], [], []
    for a in range(len(ins)):
        if mode == "pair":
            for chip in range(N_CHIP):
                src = ins[a].at[2 * chip + 1 - mc]
                sends.append(rdma(a, chip + 1, sibling, src, chip))
                recvs.append(rdma(a, chip + 1, sibling, src, chip))
            continue
        if mode == "quad":
            own.append(pltpu.make_async_copy(ins[a].at[my_chip], outs[a].at[my_chip], local_sems.at[a]))
            for r in (2, 4, 6):
                dev, idx = _peer(r)
                chip = idx // 2
                sends.append(rdma(a, r, dev, ins[a].at[chip], my_chip))
                recvs.append(rdma(a, r, dev, ins[a].at[chip], chip))
            continue
        gather = mode == "gather"
        own.append(pltpu.make_async_copy(ins[a] if gather else ins[a].at[me], outs[a].at[me], local_sems.at[a]))
        for r in range(1, N_DEV):
            dev, idx = _peer(r)
            if not gather:
                sends.append(rdma(a, r, dev, ins[a].at[idx], me))
                recvs.append(rdma(a, r, dev, ins[a].at[idx], idx))
            elif r == 1:
                sends.append(rdma(a, r, dev, ins[a], me))
                recvs.append(rdma(a, r, dev, ins[a], idx))
            elif r % 2 == 0:
                sends.append(rdma(a, r, dev, ins[a], me))
                relays.append((rdma(a, r, dev, ins[a], idx), rdma(a, r + 1, sibling, outs[a].at[idx], idx)))
            else:
                recvs.append(rdma(a, r, sibling, ins[a], idx))
    return own, sends, relays, recvs


def _xchg_start(ins, outs, sems, mode):
    own, sends, _, _ = _xchg_copies(ins, outs, sems, mode)
    for cp in own + sends:
        cp.start()


def _xchg_wait(ins, outs, sems, mode):
    own, sends, relays, recvs = _xchg_copies(ins, outs, sems, mode)
    for arrival, relay in relays:
        arrival.wait_recv()
        relay.start()
    for cp in recvs:
        cp.wait_recv()
    for cp in own:
        cp.wait()
    for cp in sends + [relay for _, relay in relays]:
        cp.wait_send()


def _xchg_specs(arrays, mode):
    n = len(arrays)
    shape = {"gather": lambda s: (N_DEV,) + s, "scatter": lambda s: s, "pair": lambda s: (N_CHIP,) + s[1:],
             "quad": lambda s: s}[mode]
    out_shape = [jax.ShapeDtypeStruct(shape(a.shape), a.dtype) for a in arrays]
    sems = [pltpu.SemaphoreType.DMA((n * (N_DEV - 1),)), pltpu.SemaphoreType.DMA((n * (N_DEV - 1),)),
            pltpu.SemaphoreType.DMA((n,))]
    return out_shape, sems


def _exchange(arrays, mode, name):
    n = len(arrays)

    def body(*refs):
        _xchg_start(refs[:n], refs[n:2 * n], refs[2 * n:], mode)
        _xchg_wait(refs[:n], refs[n:2 * n], refs[2 * n:], mode)

    out_shape, sems = _xchg_specs(arrays, mode)
    return pl.pallas_call(
        body, name=name, out_shape=out_shape,
        in_specs=[pl.BlockSpec(memory_space=pl.ANY)] * n, out_specs=[pl.BlockSpec(memory_space=pl.ANY)] * n,
        scratch_shapes=sems,
    )(*arrays)


def _gridded(body, carry, *, name, grid, in_specs, out_specs, out_shape, scratch_shapes=(), aliases=None):
    if carry is None:
        return pl.pallas_call(
            body, name=name, grid=grid, in_specs=list(in_specs), out_specs=list(out_specs),
            out_shape=list(out_shape), scratch_shapes=list(scratch_shapes), input_output_aliases=aliases or {},
            compiler_params=_cparams(len(grid)))
    arrays, mode = carry
    n, n_in, n_out, n_scr = len(arrays), len(in_specs), len(out_specs), len(scratch_shapes)
    c_shape, c_sems = _xchg_specs(arrays, mode)

    def wrapped(*refs):
        ins, cin = refs[:n_in], refs[n_in:n_in + n]
        o0 = n_in + n
        outs, cout = refs[o0:o0 + n_out], refs[o0 + n_out:o0 + n_out + n]
        s0 = o0 + n_out + n
        scr, sems = refs[s0:s0 + n_scr], refs[s0 + n_scr:]
        first = pl.program_id(0) == 0
        last = pl.program_id(0) == grid[0] - 1
        for ax in range(1, len(grid)):
            first = first & (pl.program_id(ax) == 0)
            last = last & (pl.program_id(ax) == grid[ax] - 1)

        @pl.when(first)
        def _():
            _xchg_start(cin, cout, sems, mode)

        body(*ins, *outs, *scr)

        @pl.when(last)
        def _():
            _xchg_wait(cin, cout, sems, mode)

    hbm = pl.BlockSpec(memory_space=pl.ANY)
    res = pl.pallas_call(
        wrapped, name=name, grid=grid, in_specs=list(in_specs) + [hbm] * n, out_specs=list(out_specs) + [hbm] * n,
        out_shape=list(out_shape) + c_shape, scratch_shapes=list(scratch_shapes) + c_sems,
        input_output_aliases=aliases or {}, compiler_params=_cparams(len(grid)),
    )
    return lambda *args: res(*args, *arrays)


def _local_step(x, target, mod, small, sh, w1):
    w1_in, w1_out = w1[0].reshape(2, D_FF, D), w1[1].reshape(D_FF, D)
    (x1, a1, b1, f1, h1, h2), (wm_in,) = _ffn_fwd(x, mod, 0, small["norm_ffn1"], w1_in, w1_out, 0.5, "ffn1_fwd",
                                                  ([sh["mix_w_in"]], "gather"), nxt=(small["norm_mix"], 3))
    (p,), (wh_o, wc_o, wm_o, cw) = _mixin_fwd(
        h2, wm_in, ([sh["hgrn_w_o"], sh["conv_w_o"], sh["mix_w_out"], sh["conv_w"]], "gather"))
    wh_o, wc_o, wm_o = wh_o.reshape(D, D), wc_o.reshape(D, D), wm_o.reshape(D, D)
    cw = jnp.pad(cw.transpose(1, 0, 2).reshape(CONV_K, D), ((0, HALO - CONV_K), (0, 0)))
    (o, oa, a_all, s_all), (w2_in,) = _hgrn_fwd(p, small["hgrn_lb"], small["hgrn_g"], ([sh["ffn2_w_in"]], "gather"))
    (u1, u2), (w2_out,) = _conv_fwd(p, cw, small["conv_b"], small["conv_ln_g"], small["conv_ln_b"],
                                    ([sh["ffn2_w_out"]], "gather"))
    w2_in, w2_out = w2_in.reshape(2, D_FF, D), w2_out.reshape(D_FF, D)
    x2, ya, yb, mout = _mixout_fwd(x1, oa, u2, p, mod, 3, wh_o, wc_o, wm_o)
    (x3, a3, b3, f3, h3), _ = _ffn_fwd(x2, mod, 6, small["norm_ffn2"], w2_in, w2_out, 0.5, "ffn2_fwd", None)
    dx3, df3, sm_head = _head(x3, target, small["norm_final"], mod, 8, 0.5)

    (da3, db3, dw2_in, dw2_out), _ = _ffn_bwd_w(h3, df3, a3, b3, w2_out, "ffn2_bwd_w", None)
    rows = lambda t: t.reshape(N_DEV, -1, D).astype(MM)
    (dx2, sm3), (r2_out,) = _ffn_bwd_x(x2, dx3, f3, da3, db3, mod, 6, small["norm_ffn2"], w2_in, 0.5, "ffn2_bwd_x",
                                       ([rows(dw2_out)], "scatter"))
    dp, doa, du2, dwh_o, dwc_o, dwm_o, sm_mo = _mixout_bwd(dx2, oa, u2, ya, yb, mout, p, mod, 3, wh_o, wc_o, wm_o)
    (dp, dcw, sm_cv), (r2_in,) = _conv_bwd(p, u1, du2, cw, small["conv_ln_g"], small["conv_ln_b"], dp,
                                           ([rows(dw2_in)], "scatter"))
    (dp, sm_hg), _ = _hgrn_bwd(p, o, a_all, s_all, doa, small["hgrn_lb"], small["hgrn_g"], dp, None)
    (dx1, dwm_in, sm2, df1), (rh_o, rc_o, rm_o, rcw) = _mixin_bwd(
        x1, h2, dx2, dp, mod, 3, small["norm_mix"], wm_in, 2, 0.5,
        ([rows(dwh_o), rows(dwc_o), rows(dwm_o), dcw[:CONV_K].reshape(CONV_K, N_DEV, -1).transpose(1, 0, 2)],
         "scatter"))
    (da1, db1, dw1_in, dw1_out), (rm_in,) = _ffn_bwd_w(h1, df1, a1, b1, w1_out, "ffn1_bwd_w",
                                                      (_pair_reduce([dwm_in], "pair_mix"), "quad"))
    (dx0, sm1), (r1_in, r1_out) = _ffn_bwd_x(
        x, dx1, f1, da1, db1, mod, 0, small["norm_ffn1"], w1_in, 0.5, "ffn1_bwd_x",
        (_pair_reduce([rows(dw1_in), rows(dw1_out)], "pair_ffn1"), "quad"))

    dmod = jnp.concatenate([sm1[0:3], sm2[0:2], sm_mo[2:3], sm3[0:3]], axis=0)
    gsmall = dict(norm_ffn1=sm1[3:4], norm_mix=sm2[3:4], lb0=sm_hg[0:1], hgrn_g=sm_hg[1:2], conv_b=sm_cv[0:1],
                  conv_ln_g=sm_cv[1:2], conv_ln_b=sm_cv[2:3], norm_ffn2=sm3[3:4], norm_final=sm_head[0:1])
    recv = dict(ffn1_w_in=r1_in, ffn1_w_out=r1_out, mix_w_in=rm_in, hgrn_w_o=rh_o, conv_w=rcw, conv_w_o=rc_o,
                mix_w_out=rm_o, ffn2_w_in=r2_in, ffn2_w_out=r2_out)
    return sm_head[1, 0], dx0, dmod, gsmall, recv


def _pair_add(mine, theirs, core, name):
    _, R, C = theirs.shape

    def body(core_ref, a_ref, b_ref, out_ref):
        del core_ref
        out_ref[0] = (a_ref[0, 0].astype(F32) + b_ref[0].astype(F32)).astype(out_ref.dtype)

    blk = pl.BlockSpec((1, R, C), lambda s, core_ref: (s, 0, 0))
    grid_spec = pltpu.PrefetchScalarGridSpec(
        num_scalar_prefetch=1, grid=(N_CHIP,),
        in_specs=[pl.BlockSpec((1, 1, R, C), lambda s, core_ref: (s, core_ref[0], 0, 0)), blk], out_specs=blk)
    return pl.pallas_call(body, name=name, grid_spec=grid_spec,
                          out_shape=jax.ShapeDtypeStruct(theirs.shape, mine.dtype), compiler_params=_cparams(1),
                          )(core, mine.reshape(N_CHIP, 2, R, C), theirs)


def _pair_reduce(arrays, name):
    theirs = _exchange(arrays, "pair", name)
    core = lax.axis_index("c").astype(jnp.int32).reshape(1)
    return [_pair_add(a, t, core, "%s_add%d" % (name, i)) for i, (a, t) in enumerate(zip(arrays, theirs))]


SMALL_ORDER = ("norm_ffn1", "norm_mix", "lb0", "hgrn_g", "conv_b", "conv_ln_g", "conv_ln_b", "norm_ffn2",
               "norm_final")
PACK_ROWS = 24


def kernel(x, c, ada_w, ada_b, norm_ffn1, ffn1_w_in, ffn1_w_out, norm_mix, mix_w_in, hgrn_lb, hgrn_g, hgrn_w_o, conv_w, conv_b, conv_ln_g, conv_ln_b, conv_w_o, mix_w_out, norm_ffn2, ffn2_w_in, ffn2_w_out, norm_final, loss_target, m_ada_w, m_ada_b, m_norm_ffn1, m_ffn1_w_in, m_ffn1_w_out, m_norm_mix, m_mix_w_in, m_hgrn_lb, m_hgrn_g, m_hgrn_w_o, m_conv_w, m_conv_b, m_conv_ln_g, m_conv_ln_b, m_conv_w_o, m_mix_w_out, m_norm_ffn2, m_ffn2_w_in, m_ffn2_w_out, m_norm_final, v_ada_w, v_ada_b, v_norm_ffn1, v_ffn1_w_in, v_ffn1_w_out, v_norm_mix, v_mix_w_in, v_hgrn_lb, v_hgrn_g, v_hgrn_w_o, v_conv_w, v_conv_b, v_conv_ln_g, v_conv_ln_b, v_conv_w_o, v_mix_w_out, v_norm_ffn2, v_ffn2_w_in, v_ffn2_w_out, v_norm_final):
    mx, my, mc = _me()
    me = 4 * mx + 2 * my + mc
    ncol = ada_w.shape[2]

    sh = dict(ffn1_w_out=ffn1_w_out, mix_w_in=mix_w_in, hgrn_w_o=hgrn_w_o, conv_w_o=conv_w_o,
              mix_w_out=mix_w_out, ffn2_w_out=ffn2_w_out)
    sh = {n: w[0].astype(MM) for n, w in sh.items()}
    sh["ffn1_w_in"] = ffn1_w_in[0].T.astype(MM)
    sh["ffn2_w_in"] = ffn2_w_in[0].T.astype(MM)
    sh["conv_w"] = conv_w[0]
    small = dict(norm_ffn1=norm_ffn1, norm_mix=norm_mix, hgrn_lb=hgrn_lb, hgrn_g=hgrn_g, conv_b=conv_b,
                 conv_ln_g=conv_ln_g, conv_ln_b=conv_ln_b, norm_ffn2=norm_ffn2, norm_final=norm_final.reshape(1, D))

    cs = jnp.broadcast_to(c * jax.nn.sigmoid(c), (8, D))
    ada_b_cols = lax.dynamic_slice(ada_b, (0, me * ncol), (1, ncol))
    cs_all, mod_all, w1 = _prologue(cs, ada_w[0], ada_b_cols, [sh["ffn1_w_in"], sh["ffn1_w_out"]])
    cs_all = cs_all.reshape(N_DEV, 8, D)[:, 0, :]
    mod = lax.dynamic_index_in_dim(mod_all.reshape(N_DEV, N_DEV, ncol), me, axis=1, keepdims=False).reshape(9, D)

    loss_local, dx, dmod, gsmall, recv = _local_step(x[0], loss_target[0], mod, small, sh, w1)

    n_used = 9 + len(SMALL_ORDER) + 1
    pack = jnp.concatenate([dmod] + [gsmall[n] for n in SMALL_ORDER] + [jnp.broadcast_to(loss_local, (1, D))]
                           + [jnp.zeros((PACK_ROWS - n_used, D), F32)], axis=0)
    pack_all = _allgather_small(pack).reshape(N_DEV, PACK_ROWS, D)
    tot = _sum_slots(pack_all, "sum_small", PACK_ROWS)
    loss = tot[n_used - 1, 0]
    gs = {n: tot[9 + i:10 + i] for i, n in enumerate(SMALL_ORDER)}
    dmod_all = pack_all[:, 0:9, :].reshape(N_DEV, 9 * D)
    g_ada_b = tot[0:9].reshape(1, 9 * D)
    g_ada_w = _ada_wgrad(cs_all, lax.dynamic_slice(dmod_all, (0, me * ncol), (N_DEV, ncol)))
    z = hgrn_lb.astype(F32)
    p0 = jax.nn.sigmoid(z[0:1] - z[1:2])
    dz0 = p0 * (1.0 - p0) * gs["lb0"]
    g_hgrn_lb = jnp.concatenate([dz0, -dz0], axis=0)

    res = {}
    res["ada_w"] = _adamw(ada_w[0], m_ada_w[0], v_ada_w[0], g_ada_w, "adamw_ada_w")
    big = dict(ffn1_w_in=(ffn1_w_in, m_ffn1_w_in, v_ffn1_w_in), ffn1_w_out=(ffn1_w_out, m_ffn1_w_out, v_ffn1_w_out),
               mix_w_in=(mix_w_in, m_mix_w_in, v_mix_w_in), hgrn_w_o=(hgrn_w_o, m_hgrn_w_o, v_hgrn_w_o),
               conv_w=(conv_w, m_conv_w, v_conv_w), conv_w_o=(conv_w_o, m_conv_w_o, v_conv_w_o),
               mix_w_out=(mix_w_out, m_mix_w_out, v_mix_w_out), ffn2_w_in=(ffn2_w_in, m_ffn2_w_in, v_ffn2_w_in),
               ffn2_w_out=(ffn2_w_out, m_ffn2_w_out, v_ffn2_w_out))
    for n, (w, m, v) in big.items():
        if n in ("ffn1_w_in", "ffn2_w_in"):
            res[n] = tuple(t.T for t in _adamw(w[0].T, m[0].T, v[0].T, recv[n], "adamw_" + n))
        else:
            res[n] = _adamw(w[0], m[0], v[0], recv[n], "adamw_" + n)
    sm_names = ("ada_b", "norm_ffn1", "norm_mix", "hgrn_lb", "hgrn_g", "conv_b", "conv_ln_g", "conv_ln_b",
                "norm_ffn2", "norm_final")
    sm_w = dict(ada_b=(ada_b, m_ada_b, v_ada_b), norm_ffn1=(norm_ffn1, m_norm_ffn1, v_norm_ffn1),
                norm_mix=(norm_mix, m_norm_mix, v_norm_mix), hgrn_lb=(hgrn_lb, m_hgrn_lb, v_hgrn_lb),
                hgrn_g=(hgrn_g, m_hgrn_g, v_hgrn_g), conv_b=(conv_b, m_conv_b, v_conv_b),
                conv_ln_g=(conv_ln_g, m_conv_ln_g, v_conv_ln_g), conv_ln_b=(conv_ln_b, m_conv_ln_b, v_conv_ln_b),
                norm_ffn2=(norm_ffn2, m_norm_ffn2, v_norm_ffn2), norm_final=(norm_final, m_norm_final, v_norm_final))
    sm_g = dict(gs, ada_b=g_ada_b, hgrn_lb=g_hgrn_lb)
    rows = {n: sm_w[n][0].size // D for n in sm_names}
    n_rows = sum(rows.values())
    pad = (-n_rows) % 8
    stack = lambda parts: jnp.concatenate([q.reshape(-1, D) for q in parts] + [jnp.ones((pad, D), F32)], axis=0)
    st = _adamw(stack([sm_w[n][0] for n in sm_names]), stack([sm_w[n][1] for n in sm_names]),
                stack([sm_w[n][2] for n in sm_names]), stack([sm_g[n] for n in sm_names]), "adamw_small")
    off = 0
    for n in sm_names:
        res[n] = tuple(t[off:off + rows[n]].reshape(sm_w[n][0].shape) for t in st)
        off += rows[n]

    order = ("ada_w", "ada_b", "norm_ffn1", "ffn1_w_in", "ffn1_w_out", "norm_mix", "mix_w_in", "hgrn_lb", "hgrn_g",
             "hgrn_w_o", "conv_w", "conv_b", "conv_ln_g", "conv_ln_b", "conv_w_o", "mix_w_out", "norm_ffn2",
             "ffn2_w_in", "ffn2_w_out", "norm_final")
    lead = lambda n, t: t[None] if n in big or n == "ada_w" else t
    outs = [loss, dx[None]]
    for j in range(4):
        outs += [lead(n, res[n][j]) for n in order]
    return tuple(outs)
```

```python
import jax
import jax.numpy as jnp
from jax import lax
from jax.experimental import pallas as pl
from jax.experimental.pallas import tpu as pltpu

F32 = jnp.float32
MM = jnp.bfloat16
ACT = jnp.bfloat16

D = 1024
D_FF = 2816
HEADS = 8
HD = 128
CHUNK = 64
SUB = 16
NSUB = CHUNK // SUB
HGRN_BLOCK = 1024
SAFE_EXP = 60.0
CONV_K = 31
HALO = 32
EPS = 1e-6
N_DEV = 8
NEG = -1e30
Q_SCALE = HD ** -0.5

ADAM_LR = 0.001
ADAM_B1 = 0.9
ADAM_B2 = 0.999
ADAM_EPS = 1e-08
ADAM_WD = 0.01
ADAM_STEP = 10

V7X_VMEM_BYTES = 64 * 1024 * 1024
VMEM_LIMIT = V7X_VMEM_BYTES - 4 * 1024 * 1024
MESH = pl.DeviceIdType.MESH


def _cparams(n_axes):
    return pltpu.CompilerParams(dimension_semantics=("arbitrary",) * n_axes, vmem_limit_bytes=VMEM_LIMIT)


def _mm(a, b):
    return lax.dot_general(a.astype(MM), b.astype(MM), (((1,), (0,)), ((), ())), preferred_element_type=F32)


def _mm_nt(a, b):
    return lax.dot_general(a.astype(MM), b.astype(MM), (((1,), (1,)), ((), ())), preferred_element_type=F32)


def _mm_tn(a, b):
    return lax.dot_general(a.astype(MM), b.astype(MM), (((0,), (0,)), ((), ())), preferred_element_type=F32)


def _sig(x):
    return 1.0 / (1.0 + jnp.exp(-x))


def _colsum(x):
    return jnp.sum(x, axis=0, keepdims=True)


def _rowmean(x):
    return jnp.mean(x, axis=-1, keepdims=True)


def _modnorm_fwd(xv, g, sh, sc):
    r = lax.rsqrt(_rowmean(xv * xv) + EPS)
    xh = xv * r
    n = xh * g
    return n * (1.0 + sc) + sh, xh, n, r


def _modnorm_bwd(dh, xh, n, r, g, sc):
    dsc = _colsum(dh * n)
    dsh = _colsum(dh)
    dn = dh * (1.0 + sc)
    dg = _colsum(dn * xh)
    dxh = dn * g
    dx = r * (dxh - xh * _rowmean(dxh * xh))
    return dx, dsh, dsc, dg


def _ffn_fwd(x, mod, mo, gnorm, w_in_t, w_out, res, name, carry, nxt=None):
    T = x.shape[0]
    tm = min(512, T)
    tn = D_FF // 2

    def body(x_ref, mod_ref, g_ref, wi_ref, wo_ref, *rest):
        if nxt is None:
            xo_ref, a_ref, b_ref, f_ref, h_ref = rest
        else:
            gn_ref, xo_ref, a_ref, b_ref, f_ref, h_ref, hn_ref = rest
        xv = x_ref[...]
        h, _, _, _ = _modnorm_fwd(xv, g_ref[...], mod_ref[mo:mo + 1, :], mod_ref[mo + 1:mo + 2, :])
        h = h.astype(ACT)
        h_ref[...] = h
        f = None
        for c0 in range(0, D_FF, tn):
            a = _mm_nt(h, wi_ref[0, c0:c0 + tn, :])
            b = _mm_nt(h, wi_ref[1, c0:c0 + tn, :])
            a_ref[:, c0:c0 + tn] = a.astype(ACT)
            b_ref[:, c0:c0 + tn] = b.astype(ACT)
            part = _mm(a * _sig(a) * b, wo_ref[c0:c0 + tn, :])
            f = part if f is None else f + part
        f_ref[...] = f
        xo = xv + res * mod_ref[mo + 2:mo + 3, :] * f
        xo_ref[...] = xo
        if nxt is not None:
            hn, _, _, _ = _modnorm_fwd(xo, gn_ref[...], mod_ref[nxt[1]:nxt[1] + 1, :], mod_ref[nxt[1] + 1:nxt[1] + 2, :])
            hn_ref[...] = hn.astype(ACT)

    tile = pl.BlockSpec((tm, D), lambda i: (i, 0))
    wide = pl.BlockSpec((tm, D_FF), lambda i: (i, 0))
    row = pl.BlockSpec((1, D), lambda i: (0, 0))
    n_out = 5 if nxt is None else 6
    out = _gridded(
        body, carry, name=name, grid=(T // tm,),
        in_specs=[
            tile,
            pl.BlockSpec((9, D), lambda i: (0, 0)),
            row,
            pl.BlockSpec((2, D_FF, D), lambda i: (0, 0, 0), pipeline_mode=pl.Buffered(1)),
            pl.BlockSpec((D_FF, D), lambda i: (0, 0), pipeline_mode=pl.Buffered(1)),
        ] + ([] if nxt is None else [row]),
        out_specs=[tile, wide, wide, tile, tile] + ([] if nxt is None else [tile]),
        out_shape=[
            jax.ShapeDtypeStruct((T, D), F32),
            jax.ShapeDtypeStruct((T, D_FF), ACT),
            jax.ShapeDtypeStruct((T, D_FF), ACT),
            jax.ShapeDtypeStruct((T, D), F32),
            jax.ShapeDtypeStruct((T, D), ACT),
        ] + ([] if nxt is None else [jax.ShapeDtypeStruct((T, D), ACT)]),
    )(*((x, mod, gnorm, w_in_t, w_out) + (() if nxt is None else (nxt[0],))))
    return out[:n_out], out[n_out:]


def _ffn_bwd_w(h, df, a, b, w_out, name, carry):
    T = h.shape[0]
    tm = min(2048, T)
    ni = T // tm
    tn = 256
    nj = D_FF // tn

    def body(h_ref, df_ref, a_ref, b_ref, wo_ref, da_ref, db_ref, dwi_ref, dwo_ref, acc_i, acc_o):
        i = pl.program_id(1)

        @pl.when(i == 0)
        def _():
            acc_i[...] = jnp.zeros_like(acc_i)
            acc_o[...] = jnp.zeros_like(acc_o)

        hb = h_ref[...]
        df = df_ref[...]
        av = a_ref[...].astype(F32)
        bv = b_ref[...].astype(F32)
        sg = _sig(av)
        sa = av * sg
        s = (sa * bv).astype(MM)
        ds = _mm_nt(df, wo_ref[...])
        da = (ds * bv * sg * (1.0 + av * (1.0 - sg))).astype(MM)
        db = (ds * sa).astype(MM)
        da_ref[...] = da
        db_ref[...] = db
        acc_o[...] += _mm_tn(s, df)
        acc_i[0] += _mm_tn(da, hb)
        acc_i[1] += _mm_tn(db, hb)

        @pl.when(i == ni - 1)
        def _():
            dwi_ref[...] = acc_i[...].astype(MM)
            dwo_ref[...] = acc_o[...].astype(MM)

    out = _gridded(
        body, carry, name=name, grid=(nj, ni),
        in_specs=[
            pl.BlockSpec((tm, D), lambda j, i: (i, 0)),
            pl.BlockSpec((tm, D), lambda j, i: (i, 0)),
            pl.BlockSpec((tm, tn), lambda j, i: (i, j)),
            pl.BlockSpec((tm, tn), lambda j, i: (i, j)),
            pl.BlockSpec((tn, D), lambda j, i: (j, 0)),
        ],
        out_specs=[
            pl.BlockSpec((tm, tn), lambda j, i: (i, j)),
            pl.BlockSpec((tm, tn), lambda j, i: (i, j)),
            pl.BlockSpec((2, tn, D), lambda j, i: (0, j, 0)),
            pl.BlockSpec((tn, D), lambda j, i: (j, 0)),
        ],
        out_shape=[
            jax.ShapeDtypeStruct((T, D_FF), MM),
            jax.ShapeDtypeStruct((T, D_FF), MM),
            jax.ShapeDtypeStruct((2, D_FF, D), MM),
            jax.ShapeDtypeStruct((D_FF, D), MM),
        ],
        scratch_shapes=[pltpu.VMEM((2, tn, D), F32), pltpu.VMEM((tn, D), F32)],
    )(h, df, a, b, w_out)
    return out[:4], out[4:]


def _ffn_bwd_x(x, dxo, f, da, db, mod, mo, gnorm, w_in_t, res, name, carry):
    T = x.shape[0]
    tm = min(512, T)
    ni = T // tm
    tn = D_FF // 2
    nj = D_FF // tn

    def body(x_ref, dxo_ref, f_ref, da_ref, db_ref, mod_ref, g_ref, wi_ref, dx_ref, sm_ref, dh_scr):
        j = pl.program_id(0)
        i = pl.program_id(1)

        @pl.when((j == 0) & (i == 0))
        def _():
            sm_ref[...] = jnp.zeros_like(sm_ref)

        @pl.when(j == 0)
        def _():
            dh_scr[i] = jnp.zeros((tm, D), F32)

        dh_scr[i] += _mm(da_ref[...], wi_ref[0]) + _mm(db_ref[...], wi_ref[1])

        @pl.when(j == nj - 1)
        def _():
            sc = mod_ref[mo + 1:mo + 2, :]
            _, xh, n, r = _modnorm_fwd(x_ref[...], g_ref[...], mod_ref[mo:mo + 1, :], sc)
            dxn, dsh, dsc, dg = _modnorm_bwd(dh_scr[i], xh, n, r, g_ref[...], sc)
            dxo_v = dxo_ref[...]
            dx_ref[...] = dxo_v + dxn
            sm_ref[0:1, :] += dsh
            sm_ref[1:2, :] += dsc
            sm_ref[2:3, :] += _colsum(dxo_v * f_ref[...]) * res
            sm_ref[3:4, :] += dg

    last = pl.BlockSpec((tm, D), lambda j, i: (jnp.where(j == nj - 1, i, 0), 0))
    out = _gridded(
        body, carry, name=name, grid=(nj, ni),
        in_specs=[last, last, last,
                  pl.BlockSpec((tm, tn), lambda j, i: (i, j)), pl.BlockSpec((tm, tn), lambda j, i: (i, j)),
                  pl.BlockSpec((9, D), lambda j, i: (0, 0)), pl.BlockSpec((1, D), lambda j, i: (0, 0)),
                  pl.BlockSpec((2, tn, D), lambda j, i: (0, j, 0))],
        out_specs=[last, pl.BlockSpec((8, D), lambda j, i: (0, 0))],
        out_shape=[jax.ShapeDtypeStruct((T, D), F32), jax.ShapeDtypeStruct((8, D), F32)],
        scratch_shapes=[pltpu.VMEM((ni, tm, D), F32)],
    )(x, dxo, f, da, db, mod, gnorm, w_in_t)
    return out[:2], out[2:]


def _head(x, target, gfin, mod, gate_row, res):
    T = x.shape[0]
    tm = min(512, T)
    ni = T // tm

    def body(x_ref, t_ref, g_ref, mod_ref, dx_ref, df_ref, sm_ref):
        i = pl.program_id(0)

        @pl.when(i == 0)
        def _():
            sm_ref[...] = jnp.zeros_like(sm_ref)

        xv = x_ref[...]
        g = g_ref[...]
        r = lax.rsqrt(_rowmean(xv * xv) + EPS)
        xh = xv * r
        e = xh * g - t_ref[...]
        sm_ref[1:2, :] += _colsum(e * e) * (0.5 / D)
        dy = e * (1.0 / D)
        sm_ref[0:1, :] += _colsum(dy * xh)
        dxh = dy * g
        dx = r * (dxh - xh * _rowmean(dxh * xh))
        dx_ref[...] = dx
        df_ref[...] = (res * mod_ref[gate_row:gate_row + 1, :] * dx).astype(MM)

        @pl.when(i == ni - 1)
        def _():
            sm_ref[1:2, :] = jnp.broadcast_to(jnp.sum(sm_ref[1:2, :], axis=-1, keepdims=True), (1, D))

    tile = pl.BlockSpec((tm, D), lambda i: (i, 0))
    return pl.pallas_call(
        body, name="head_loss", grid=(ni,),
        in_specs=[tile, tile, pl.BlockSpec((1, D), lambda i: (0, 0)), pl.BlockSpec((9, D), lambda i: (0, 0))],
        out_specs=[tile, tile, pl.BlockSpec((8, D), lambda i: (0, 0))],
        out_shape=[jax.ShapeDtypeStruct((T, D), F32), jax.ShapeDtypeStruct((T, D), MM),
                   jax.ShapeDtypeStruct((8, D), F32)],
        compiler_params=_cparams(1),
    )(x, target, gfin, mod)


def _mixin_fwd(h, w, carry):
    T = h.shape[0]
    tm = min(2048, T)
    ni = T // tm

    def body(h_ref, w_ref, p_ref, h_all):
        i = pl.program_id(1)

        @pl.when(pl.program_id(0) == 0)
        def _():
            h_all[i] = h_ref[...]

        p_ref[0] = _mm(h_all[i], w_ref[0])

    first = lambda k, i: (jnp.where(k == 0, i, ni - 1), 0)
    out = _gridded(
        body, carry, name="mixin_fwd", grid=(8, ni),
        in_specs=[pl.BlockSpec((tm, D), first), pl.BlockSpec((1, D, D), lambda k, i: (k, 0, 0))],
        out_specs=[pl.BlockSpec((1, tm, D), lambda k, i: (k, i, 0))],
        out_shape=[jax.ShapeDtypeStruct((8, T, D), F32)],
        scratch_shapes=[pltpu.VMEM((ni, tm, D), ACT)],
    )(h, w)
    return out[:1], out[1:]


def _mixin_bwd(x, h, dxo, dp, mod, mo, gnorm, w, next_gate, next_res, carry):
    T = x.shape[0]
    tm = min(512, T)
    ni = T // tm

    def body(x_ref, h_ref, dxo_ref, dp_ref, mod_ref, g_ref, w_ref, dx_ref, dw_ref, sm_ref, df_ref, dh_scr, acc):
        k = pl.program_id(0)
        i = pl.program_id(1)

        @pl.when(i == 0)
        def _():
            acc[...] = jnp.zeros_like(acc)

        @pl.when(k == 0)
        def _():
            dh_scr[i] = jnp.zeros((tm, D), F32)

        @pl.when((k == 0) & (i == 0))
        def _():
            sm_ref[...] = jnp.zeros_like(sm_ref)

        dpk = dp_ref[0].astype(MM)
        acc[...] += _mm_tn(h_ref[...], dpk)
        dh_scr[i] += _mm_nt(dpk, w_ref[0])

        @pl.when(i == ni - 1)
        def _():
            dw_ref[0] = acc[...].astype(MM)

        @pl.when(k == 7)
        def _():
            sc = mod_ref[mo + 1:mo + 2, :]
            _, xh, n, r = _modnorm_fwd(x_ref[...], g_ref[...], mod_ref[mo:mo + 1, :], sc)
            dxn, dsh, dsc, dg = _modnorm_bwd(dh_scr[i], xh, n, r, g_ref[...], sc)
            dx = dxo_ref[...] + dxn
            dx_ref[...] = dx
            df_ref[...] = (next_res * mod_ref[next_gate:next_gate + 1, :] * dx).astype(MM)
            sm_ref[0:1, :] += dsh
            sm_ref[1:2, :] += dsc
            sm_ref[3:4, :] += dg

    last = pl.BlockSpec((tm, D), lambda k, i: (jnp.where(k == 7, i, 0), 0))
    out = _gridded(
        body, carry, name="mixin_bwd", grid=(8, ni),
        in_specs=[pl.BlockSpec((tm, D), lambda k, i: (jnp.where(k == 7, i, 0), 0)),
                  pl.BlockSpec((tm, D), lambda k, i: (i, 0)),
                  pl.BlockSpec((tm, D), lambda k, i: (jnp.where(k == 7, i, 0), 0)),
                  pl.BlockSpec((1, tm, D), lambda k, i: (k, i, 0)), pl.BlockSpec((9, D), lambda k, i: (0, 0)),
                  pl.BlockSpec((1, D), lambda k, i: (0, 0)), pl.BlockSpec((1, D, D), lambda k, i: (k, 0, 0))],
        out_specs=[last, pl.BlockSpec((1, D, D), lambda k, i: (k, 0, 0)), pl.BlockSpec((8, D), lambda k, i: (0, 0)),
                   last],
        out_shape=[jax.ShapeDtypeStruct((T, D), F32), jax.ShapeDtypeStruct((8, D, D), MM),
                   jax.ShapeDtypeStruct((8, D), F32), jax.ShapeDtypeStruct((T, D), MM)],
        scratch_shapes=[pltpu.VMEM((ni, tm, D), F32), pltpu.VMEM((D, D), F32)],
    )(x, h, dxo, dp, mod, gnorm, w)
    return out[:4], out[4:]


def _hgrn_consts():
    rows = jnp.arange(SUB * HD) // HD
    e = (rows[:, None] == jnp.arange(HD)[None, :]).astype(MM)
    return e, e.T


def _rows_bcast(ref, cb, first, n):
    parts = [jnp.broadcast_to(ref[pl.ds(c * CHUNK + first, 1), :], (n, HD)) for c in range(cb // CHUNK)]
    return jnp.concatenate(parts, axis=0)


def _hgrn_pre(qr, fr, lb_ref, b_scr, cb):
    z = lb_ref[...]
    lb = _sig(z[0:1, :] - z[1:2, :])
    sq = _sig(qr)
    q = qr * sq * Q_SCALE
    sf = _sig(fr)
    fg = lb + (1.0 - lb) * sf
    lf = jnp.log(fg)
    k = 1.0 - fg
    tl = lax.broadcasted_iota(jnp.int32, (cb, HD), 0) % CHUNK
    bc = lf
    sh = 1
    while sh < CHUNK:
        bc = bc + jnp.where(tl >= sh, pltpu.roll(bc, sh, 0), 0.0)
        sh *= 2
    b_scr[...] = bc
    bl = _rows_bcast(b_scr, cb, CHUNK - 1, CHUNK)
    eb = jnp.exp(bc)
    ekd = jnp.exp(bl - bc)
    ekf = jnp.exp(jnp.minimum(-bc, SAFE_EXP))
    return dict(lb=lb, sq=sq, q=q, sf=sf, fg=fg, k=k, tl=tl, b=bc, bl=bl, eb=eb, ekd=ekd, ekf=ekf,
                qe=q * eb, kd=k * ekd, kf=k * ekf, safe=jnp.max(-bc) < SAFE_EXP)


def _hgrn_pre_fused(p_ref, lb_ref, b_scr, q_scr, k_scr, qe_scr, kf_scr, kd_scr, cb):
    z = lb_ref[...]
    lb = _sig(z[0:1, :] - z[1:2, :])
    tl = lax.broadcasted_iota(jnp.int32, (CHUNK, HD), 0)

    def chunk(c, worst):
        r0 = pl.multiple_of(c * CHUNK, CHUNK)
        rs = pl.ds(r0, CHUNK)
        qr = p_ref[0, rs, :]
        q = qr * _sig(qr) * Q_SCALE
        fg = lb + (1.0 - lb) * _sig(p_ref[1, rs, :])
        k = 1.0 - fg
        bc = jnp.log(fg)
        sh = 1
        while sh < CHUNK:
            bc = bc + jnp.where(tl >= sh, pltpu.roll(bc, sh, 0), 0.0)
            sh *= 2
        b_scr[rs, :] = bc
        q_scr[rs, :] = q
        k_scr[rs, :] = k
        qe_scr[rs, :] = (q * jnp.exp(bc)).astype(MM)
        kf_scr[rs, :] = (k * jnp.exp(jnp.minimum(-bc, SAFE_EXP))).astype(MM)
        kd_scr[rs, :] = (k * jnp.exp(b_scr[pl.ds(r0 + CHUNK - 1, 1), :] - bc)).astype(MM)
        return jnp.maximum(worst, -bc)

    worst = lax.fori_loop(0, cb // CHUNK, chunk, jnp.zeros((CHUNK, HD), F32))
    return jnp.max(worst) < SAFE_EXP


def _hgrn_sub(pre, b_scr, cb):
    bc, tl, q, k = pre["b"], pre["tl"], pre["q"], pre["k"]
    br = [None] + [_rows_bcast(b_scr, cb, SUB * i - 1, CHUNK) for i in range(1, NSUB)]
    sb = tl // SUB
    bref = jnp.where(sb == 0, bc, jnp.where(sb == 1, br[1], jnp.where(sb == 2, br[2], br[3])))
    eqo = jnp.exp(bc - bref)
    eko = [None] + [jnp.exp(jnp.where(tl < SUB * i, br[i] - bc, NEG)) for i in range(1, NSUB)]
    return dict(eqo=eqo, eko=eko, qo=q * eqo, ko=[None] + [k * eko[i] for i in range(1, NSUB)])


def _pad_rows(x):
    return jnp.concatenate([x, jnp.zeros_like(x)], axis=0)


def _by_subblock(sbc, parts):
    out = jnp.zeros_like(parts[1])
    for i in range(1, NSUB):
        out = jnp.where(sbc == i, parts[i], out)
    return out


def _hgrn_fwd(p, hgrn_lb, hgrn_g, carry):
    T = p.shape[1]
    cb = min(HGRN_BLOCK, T)
    nch = cb // CHUNK
    ncb = T // cb
    e_mat, _ = _hgrn_consts()

    def body(p_ref, lb_ref, g_ref, e_ref, o_ref, oa_ref, a_ref, s_ref, st_scr, q_scr, k_scr, b_scr, z_scr, ad_scr,
             qe_scr, kf_scr, kd_scr):
        @pl.when(pl.program_id(1) == 0)
        def _():
            st_scr[...] = jnp.zeros_like(st_scr)

        safe = _hgrn_pre_fused(p_ref, lb_ref, b_scr, q_scr, k_scr, qe_scr, kf_scr, kd_scr, cb)
        chunks = [slice(c * CHUNK, (c + 1) * CHUNK) for c in range(nch)]
        row_i = lax.broadcasted_iota(jnp.int32, (CHUNK, HD), 0)
        lane_i = lax.broadcasted_iota(jnp.int32, (CHUNK, HD), 1)
        sbc = row_i // SUB
        causal = lane_i <= row_i

        @pl.when(safe)
        def _():
            for rs in chunks:
                ad_scr[rs, :] = jnp.where(causal, _mm_nt(qe_scr[rs, :], _pad_rows(kf_scr[rs, :])), 0.0)

        @pl.when(jnp.logical_not(safe))
        def _():
            tl = lax.broadcasted_iota(jnp.int32, (cb, HD), 0) % CHUNK
            sub = _hgrn_sub(dict(b=b_scr[...], tl=tl, q=q_scr[...], k=k_scr[...]), b_scr, cb)
            ti = lax.broadcasted_iota(jnp.int32, (SUB, HD), 0)

            def zbody(c, carry):
                for i in range(NSUB):
                    r0 = pl.multiple_of(c * CHUNK + SUB * i, SUB)
                    qi = q_scr[pl.ds(r0, SUB), :]
                    bi = b_scr[pl.ds(r0, SUB), :]
                    for s in range(SUB):
                        krow = k_scr[pl.ds(r0 + s, 1), :]
                        brow = b_scr[pl.ds(r0 + s, 1), :]
                        if s < 8:
                            zz = qi * krow * jnp.exp(jnp.where(ti >= s, bi - brow, NEG))
                        else:
                            lo = qi[8:] * krow * jnp.exp(jnp.where(ti[8:] >= s, bi[8:] - brow, NEG))
                            zz = jnp.concatenate([jnp.zeros((8, HD), F32), lo], axis=0)
                        z_scr[i, pl.ds(pl.multiple_of(c * SUB, SUB), SUB), s * HD:(s + 1) * HD] = zz.astype(MM)
                return carry

            lax.fori_loop(0, nch, zbody, 0)
            adiag = [_mm(z_scr[i], e_ref[...]) for i in range(NSUB)]
            offs = [[_mm_nt(sub["qo"][rs], _pad_rows(sub["ko"][i][rs])) for i in range(1, NSUB)] for rs in chunks]
            for c, rs in enumerate(chunks):
                dparts = []
                for i in range(NSUB):
                    blk = adiag[i][c * SUB:(c + 1) * SUB]
                    dparts.append(blk if i == 0 else pltpu.roll(blk, SUB * i, 1))
                ad_scr[rs, :] = _by_subblock(sbc, [None] + offs[c]) + jnp.concatenate(dparts, axis=0)

        kv = [_mm_tn(p_ref[2, rs, :], kd_scr[rs, :]) for rs in chunks]
        a_ref[0] = ad_scr[...]
        o_intra = [_mm(ad_scr[rs, :], _pad_rows(p_ref[2, rs, :])) for rs in chunks]
        states = []
        st = st_scr[...]
        for c in range(nch):
            states.append(st)
            st = st * jnp.exp(b_scr[pl.ds(c * CHUNK + CHUNK - 1, 1), :]) + kv[c]
        st_scr[...] = st
        g = g_ref[...]
        for c, rs in enumerate(chunks):
            s_ref[0, c] = states[c]
            o = o_intra[c] + _mm_nt(qe_scr[rs, :], states[c])
            o_ref[rs, :] = o
            og = p_ref[3, rs, :]
            oa_ref[rs, :] = (o * lax.rsqrt(_rowmean(o * o) + EPS) * g * og * _sig(og)).astype(ACT)

    out = _gridded(
        body, carry, name="hgrn_fwd", grid=(HEADS, ncb),
        in_specs=[pl.BlockSpec((4, cb, HD), lambda h, c: (0, c, h)),
                  pl.BlockSpec((2, HD), lambda h, c: (0, h)),
                  pl.BlockSpec((1, HD), lambda h, c: (0, h)),
                  pl.BlockSpec((SUB * HD, HD), lambda h, c: (0, 0))],
        out_specs=[pl.BlockSpec((cb, HD), lambda h, c: (c, h)),
                   pl.BlockSpec((cb, HD), lambda h, c: (c, h)),
                   pl.BlockSpec((1, cb, HD), lambda h, c: (h, c, 0)),
                   pl.BlockSpec((1, nch, HD, HD), lambda h, c: (h, c, 0, 0))],
        out_shape=[jax.ShapeDtypeStruct((T, D), F32), jax.ShapeDtypeStruct((T, D), ACT),
                   jax.ShapeDtypeStruct((HEADS, T, HD), F32),
                   jax.ShapeDtypeStruct((HEADS, T // CHUNK, HD, HD), F32)],
        scratch_shapes=[pltpu.VMEM((HD, HD), F32), pltpu.VMEM((cb, HD), F32), pltpu.VMEM((cb, HD), F32),
                        pltpu.VMEM((cb, HD), F32), pltpu.VMEM((NSUB, nch * SUB, SUB * HD), MM),
                        pltpu.VMEM((cb, HD), F32), pltpu.VMEM((cb, HD), MM), pltpu.VMEM((cb, HD), MM),
                        pltpu.VMEM((cb, HD), MM)],
    )(p, hgrn_lb, hgrn_g, e_mat)
    return out[:4], out[4:]


def _hgrn_bwd(p, o, a_all, s_all, doa, hgrn_lb, hgrn_g, dp, carry):
    T = p.shape[1]
    cb = min(HGRN_BLOCK, T)
    nch = cb // CHUNK
    ncb = T // cb
    _, et_mat = _hgrn_consts()

    def body(p_ref, o_ref, a_ref, s_ref, doa_ref, lb_ref, g_ref, et_ref, dp_in, dp_ref, sm_ref,
             dst_scr, q_scr, k_scr, b_scr, x_scr, dqd_scr, dkd_scr):
        del dp_in

        @pl.when(pl.program_id(1) == 0)
        def _():
            dst_scr[...] = jnp.zeros_like(dst_scr)
            sm_ref[...] = jnp.zeros_like(sm_ref)

        qr = p_ref[0]
        v = p_ref[2]
        og = p_ref[3]
        pre = _hgrn_pre(qr, p_ref[1], lb_ref, b_scr, cb)
        q, k = pre["q"], pre["k"]
        g = g_ref[...]
        ov = o_ref[...]
        r = lax.rsqrt(_rowmean(ov * ov) + EPS)
        oh = ov * r
        sgo = _sig(og)
        doa_v = doa_ref[...]
        don = doa_v * og * sgo
        dog = doa_v * oh * g * sgo * (1.0 + og * (1.0 - sgo))
        sm_ref[1:2, :] += _colsum(don * oh)
        doh = don * g
        do = r * (doh - oh * _rowmean(doh * oh))

        sbc = lax.broadcasted_iota(jnp.int32, (CHUNK, HD), 0) // SUB
        row_i = lax.broadcasted_iota(jnp.int32, (CHUNK, HD), 0)
        lane_i = lax.broadcasted_iota(jnp.int32, (CHUNK, HD), 1)
        causal = lane_i <= row_i
        chunks = [slice(c * CHUNK, (c + 1) * CHUNK) for c in range(nch)]
        da_parts = [jnp.where(causal, _mm_nt(do[rs], _pad_rows(v[rs])), 0.0) for rs in chunks]
        dv_parts = [_mm_tn(a_ref[0, rs, :], do[rs])[:CHUNK] for rs in chunks]

        @pl.when(pre["safe"])
        def _():
            hi = dict(preferred_element_type=F32, precision=lax.Precision.HIGH)
            for c, rs in enumerate(chunks):
                dqd_scr[rs, :] = pre["eb"][rs] * lax.dot_general(
                    da_parts[c], _pad_rows(pre["kf"][rs]), (((1,), (0,)), ((), ())), **hi)
                dkd_scr[rs, :] = pre["ekf"][rs] * lax.dot_general(
                    da_parts[c], pre["qe"][rs], (((0,), (0,)), ((), ())), **hi)[:CHUNK]

        @pl.when(jnp.logical_not(pre["safe"]))
        def _():
            sub = _hgrn_sub(pre, b_scr, cb)
            dqoff_mm = [[_mm(da_parts[c], _pad_rows(sub["ko"][i][rs])) for i in range(1, NSUB)]
                        for c, rs in enumerate(chunks)]
            dkoff_mm = [[_mm_tn(jnp.where(sbc == i, da_parts[c], 0.0), sub["qo"][rs])[:CHUNK]
                         for i in range(1, NSUB)] for c, rs in enumerate(chunks)]
            dqoff_parts = [_by_subblock(sbc, [None] + dqoff_mm[c]) for c in range(nch)]
            dkoff_parts = []
            for c, rs in enumerate(chunks):
                dko = sub["eko"][1][rs] * dkoff_mm[c][0]
                for i in range(2, NSUB):
                    dko = dko + sub["eko"][i][rs] * dkoff_mm[c][i - 1]
                dkoff_parts.append(dko)
            q_scr[...] = q
            k_scr[...] = k
            for i in range(NSUB):
                rows = []
                for c in range(nch):
                    blk = da_parts[c][SUB * i:SUB * (i + 1)]
                    rows.append(blk if i == 0 else pltpu.roll(blk, HD - SUB * i, 1))
                x_scr[i] = _mm(jnp.concatenate(rows, axis=0), et_ref[...])
            ti = lax.broadcasted_iota(jnp.int32, (SUB, HD), 0)

            def dbody(c, carry):
                for i in range(NSUB):
                    r0 = pl.multiple_of(c * CHUNK + SUB * i, SUB)
                    qi = q_scr[pl.ds(r0, SUB), :]
                    bi = b_scr[pl.ds(r0, SUB), :]
                    dq_hi = jnp.zeros((8, HD), F32)
                    dq_lo = jnp.zeros((8, HD), F32)
                    dk_hi = jnp.zeros((8, HD), F32)
                    dk_lo = jnp.zeros((8, HD), F32)
                    c0 = pl.multiple_of(c * SUB, SUB)
                    t8 = ti[:8]
                    for s in range(SUB):
                        krow = k_scr[pl.ds(r0 + s, 1), :]
                        brow = b_scr[pl.ds(r0 + s, 1), :]
                        w_lo = (x_scr[i, pl.ds(c0 + 8, 8), s * HD:(s + 1) * HD]
                                * jnp.exp(jnp.where(t8 + 8 >= s, bi[8:] - brow, NEG)))
                        dq_lo = dq_lo + w_lo * krow
                        col = _colsum(w_lo * qi[8:])
                        if s < 8:
                            w_hi = (x_scr[i, pl.ds(c0, 8), s * HD:(s + 1) * HD]
                                    * jnp.exp(jnp.where(t8 >= s, bi[:8] - brow, NEG)))
                            dq_hi = dq_hi + w_hi * krow
                            dk_hi = jnp.where(t8 == s, col + _colsum(w_hi * qi[:8]), dk_hi)
                        else:
                            dk_lo = jnp.where(t8 + 8 == s, col, dk_lo)
                    dqd_scr[pl.ds(r0, SUB), :] = jnp.concatenate([dq_hi, dq_lo], axis=0)
                    dkd_scr[pl.ds(r0, SUB), :] = jnp.concatenate([dk_hi, dk_lo], axis=0)
                return carry

            lax.fori_loop(0, nch, dbody, 0)
            dqd_scr[...] += jnp.concatenate(dqoff_parts, axis=0) * sub["eqo"]
            dkd_scr[...] += jnp.concatenate(dkoff_parts, axis=0)

        qdo = [_mm_tn(do[rs], pre["qe"][rs]) for rs in chunks]
        dsts = [None] * nch
        dst = dst_scr[...]
        for c in reversed(range(nch)):
            dsts[c] = dst
            dst = dst * jnp.exp(b_scr[pl.ds(c * CHUNK + CHUNK - 1, 1), :]) + qdo[c]
        dst_scr[...] = dst
        sts = [s_ref[0, c] for c in range(nch)]
        dqe_parts = [_mm(do[rs], sts[c]) for c, rs in enumerate(chunks)]
        dkdec_parts = [_mm(v[rs], dsts[c]) for c, rs in enumerate(chunks)]
        dvi_parts = [_mm_nt(pre["kd"][rs], dsts[c]) for c, rs in enumerate(chunks)]
        debl_parts = [_colsum(dsts[c] * sts[c]) for c in range(nch)]
        dqe = jnp.concatenate(dqe_parts, axis=0)
        dkdec = jnp.concatenate(dkdec_parts, axis=0)
        dq_tot = dqd_scr[...] + dqe * pre["eb"]
        dk_inter = dkdec * pre["ekd"]
        dk_tot = dkd_scr[...] + dk_inter
        db = q * dq_tot - k * dk_tot
        kdk = k * dk_inter
        dbl = jnp.concatenate(
            [jnp.broadcast_to(jnp.exp(b_scr[pl.ds(c * CHUNK + CHUNK - 1, 1), :]) * debl_parts[c]
                              + _colsum(kdk[c * CHUNK:(c + 1) * CHUNK]), (CHUNK, HD)) for c in range(nch)], axis=0)
        tl = pre["tl"]
        rc = db
        sh = 1
        while sh < CHUNK:
            rc = rc + jnp.where(tl + sh < CHUNK, pltpu.roll(rc, cb - sh, 0), 0.0)
            sh *= 2
        dlf = rc + dbl
        dfg = dlf / pre["fg"] - dk_tot
        sf = pre["sf"]
        lb = pre["lb"]
        sm_ref[0:1, :] += _colsum(dfg * (1.0 - sf))
        sq = pre["sq"]
        dp_ref[0] = (dq_tot * Q_SCALE * sq * (1.0 + qr * (1.0 - sq))).astype(ACT)
        dp_ref[1] = (dfg * (1.0 - lb) * sf * (1.0 - sf)).astype(ACT)
        dp_ref[2] = (jnp.concatenate(dv_parts, axis=0) + jnp.concatenate(dvi_parts, axis=0)).astype(ACT)
        dp_ref[3] = dog.astype(ACT)

    rev = lambda c: ncb - 1 - c
    out = _gridded(
        body, carry, name="hgrn_bwd", grid=(HEADS, ncb),
        in_specs=[pl.BlockSpec((4, cb, HD), lambda h, c: (0, rev(c), h)),
                  pl.BlockSpec((cb, HD), lambda h, c: (rev(c), h)),
                  pl.BlockSpec((1, cb, HD), lambda h, c: (h, rev(c), 0)),
                  pl.BlockSpec((1, nch, HD, HD), lambda h, c: (h, rev(c), 0, 0)),
                  pl.BlockSpec((cb, HD), lambda h, c: (rev(c), h)),
                  pl.BlockSpec((2, HD), lambda h, c: (0, h)),
                  pl.BlockSpec((1, HD), lambda h, c: (0, h)),
                  pl.BlockSpec((HD, SUB * HD), lambda h, c: (0, 0)),
                  pl.BlockSpec(memory_space=pl.ANY)],
        out_specs=[pl.BlockSpec((4, cb, HD), lambda h, c: (0, rev(c), h)),
                   pl.BlockSpec((8, HD), lambda h, c: (0, h))],
        out_shape=[jax.ShapeDtypeStruct(dp.shape, dp.dtype), jax.ShapeDtypeStruct((8, D), F32)],
        aliases={8: 0},
        scratch_shapes=[pltpu.VMEM((HD, HD), F32), pltpu.VMEM((cb, HD), F32), pltpu.VMEM((cb, HD), F32),
                        pltpu.VMEM((cb, HD), F32), pltpu.VMEM((NSUB, nch * SUB, SUB * HD), F32),
                        pltpu.VMEM((cb, HD), F32), pltpu.VMEM((cb, HD), F32)],
    )(p, o, a_all, s_all, doa, hgrn_lb, hgrn_g, et_mat, dp)
    return out[:2], out[2:]


def _ln_fwd(u1, g, b):
    mu = _rowmean(u1)
    xc = u1 - mu
    rs = lax.rsqrt(_rowmean(xc * xc) + EPS)
    xh = xc * rs
    return xh * g + b, xh, rs


CONV_RB = 64
LANES = 128


def _shift_rows(src, sh, ls, n):
    for r in range(1, 8):
        sh[r - 1, 0:n, :] = src[pl.ds(r, n), ls]


def _tap(src, sh, ls, off, r0, rows):
    r = off % 8
    if r == 0:
        return src[pl.ds(r0 + off, rows), ls]
    return sh[r - 1, pl.ds(r0 + off - r, rows), :]


def _conv_fwd(p, cw, cb_, lng, lnb, carry):
    T = p.shape[1]
    tm = min(512, T)
    n = HALO + tm - 8

    def body(p_ref, cw_ref, cb_ref, g_ref, b_ref, u1_ref, u2_ref, buf, sh):
        @pl.when(pl.program_id(0) == 0)
        def _():
            buf[0:HALO, :] = jnp.zeros((HALO, D), F32)

        buf[HALO:HALO + tm, :] = p_ref[0] * _sig(p_ref[1])
        for lb in range(D // LANES):
            ls = slice(lb * LANES, (lb + 1) * LANES)
            _shift_rows(buf, sh, ls, n)
            taps = [cw_ref[j:j + 1, ls] for j in range(CONV_K)]
            bias = cb_ref[:, ls]

            def rows_body(rb, carry):
                r0 = pl.multiple_of(rb * CONV_RB, CONV_RB)
                acc = jnp.broadcast_to(bias, (CONV_RB, LANES))
                for j in range(CONV_K):
                    acc = acc + taps[j] * _tap(buf, sh, ls, HALO - (CONV_K - 1) + j, r0, CONV_RB)
                u1_ref[pl.ds(r0, CONV_RB), ls] = acc
                return carry

            lax.fori_loop(0, tm // CONV_RB, rows_body, 0)
        y, _, _ = _ln_fwd(u1_ref[...], g_ref[...], b_ref[...])
        u2_ref[...] = (y * _sig(y)).astype(ACT)
        buf[0:HALO, :] = buf[tm:tm + HALO, :]

    out = _gridded(
        body, carry, name="conv_fwd", grid=(T // tm,),
        in_specs=[pl.BlockSpec((2, tm, D), lambda i: (2, i, 0)), pl.BlockSpec((HALO, D), lambda i: (0, 0)),
                  pl.BlockSpec((1, D), lambda i: (0, 0)), pl.BlockSpec((1, D), lambda i: (0, 0)),
                  pl.BlockSpec((1, D), lambda i: (0, 0))],
        out_specs=[pl.BlockSpec((tm, D), lambda i: (i, 0)), pl.BlockSpec((tm, D), lambda i: (i, 0))],
        out_shape=[jax.ShapeDtypeStruct((T, D), F32), jax.ShapeDtypeStruct((T, D), ACT)],
        scratch_shapes=[pltpu.VMEM((HALO + tm, D), F32), pltpu.VMEM((7, n, LANES), F32)],
    )(p, cw, cb_, lng, lnb)
    return out[:2], out[2:]


def _conv_bwd(p, u1, du2, cw, lng, lnb, dp, carry):
    T = p.shape[1]
    tm = min(512, T)
    ni = T // tm
    hb = tm // HALO

    n = HALO + tm - 8

    def body(p_ref, ph_ref, u1_ref, du2_ref, cw_ref, g_ref, b_ref, dp_in, dp_ref, dcw_ref, sm_ref, ubuf, dbuf,
             sh, dacc):
        del dp_in
        step = pl.program_id(0)

        @pl.when(step == 0)
        def _():
            dbuf[tm:tm + HALO, :] = jnp.zeros((HALO, D), F32)
            dcw_ref[...] = jnp.zeros_like(dcw_ref)
            sm_ref[...] = jnp.zeros_like(sm_ref)

        ua = p_ref[0]
        sgb = _sig(p_ref[1])
        halo = ph_ref[0] * _sig(ph_ref[1])
        ubuf[0:HALO, :] = jnp.where(step == ni - 1, 0.0, halo)
        ubuf[HALO:HALO + tm, :] = ua * sgb
        g = g_ref[...]
        y, xh, rs = _ln_fwd(u1_ref[...], g, b_ref[...])
        sy = _sig(y)
        dy = du2_ref[...] * sy * (1.0 + y * (1.0 - sy))
        sm_ref[1:2, :] += _colsum(dy * xh)
        sm_ref[2:3, :] += _colsum(dy)
        dxh = dy * g
        du1 = rs * (dxh - _rowmean(dxh) - xh * _rowmean(dxh * xh))
        sm_ref[0:1, :] += _colsum(du1)
        dbuf[0:tm, :] = du1
        for lb in range(D // LANES):
            ls = slice(lb * LANES, (lb + 1) * LANES)
            taps = [cw_ref[j:j + 1, ls] for j in range(CONV_K)]
            _shift_rows(dbuf, sh, ls, n)

            def du0_body(rb, carry):
                r0 = pl.multiple_of(rb * CONV_RB, CONV_RB)
                acc = jnp.zeros((CONV_RB, LANES), F32)
                for j in range(CONV_K):
                    acc = acc + taps[j] * _tap(dbuf, sh, ls, CONV_K - 1 - j, r0, CONV_RB)
                dp_ref[0, pl.ds(r0, CONV_RB), ls] = acc.astype(ACT)
                return carry

            lax.fori_loop(0, tm // CONV_RB, du0_body, 0)
            _shift_rows(ubuf, sh, ls, n)
            dacc[...] = jnp.zeros_like(dacc)

            def dcw_body(rb, carry):
                r0 = pl.multiple_of(rb * CONV_RB, CONV_RB)
                d = dbuf[pl.ds(r0, CONV_RB), ls]
                for j in range(CONV_K):
                    prod = d * _tap(ubuf, sh, ls, HALO - (CONV_K - 1) + j, r0, CONV_RB)
                    dacc[8 * j:8 * j + 8, :] += jnp.sum(prod.reshape(CONV_RB // 8, 8, LANES), axis=0)
                return carry

            lax.fori_loop(0, tm // CONV_RB, dcw_body, 0)
            for j in range(CONV_K):
                dcw_ref[j:j + 1, ls] += _colsum(dacc[8 * j:8 * j + 8, :])
        du0 = dp_ref[0].astype(F32)
        dp_ref[0] = (du0 * sgb).astype(ACT)
        dp_ref[1] = (du0 * ua * sgb * (1.0 - sgb)).astype(ACT)
        dbuf[tm:tm + HALO, :] = dbuf[0:HALO, :]

    rev = lambda i: ni - 1 - i
    out = _gridded(
        body, carry, name="conv_bwd", grid=(ni,),
        in_specs=[pl.BlockSpec((2, tm, D), lambda i: (2, rev(i), 0)),
                  pl.BlockSpec((2, HALO, D), lambda i: (2, jnp.maximum(rev(i) * hb - 1, 0), 0)),
                  pl.BlockSpec((tm, D), lambda i: (rev(i), 0)), pl.BlockSpec((tm, D), lambda i: (rev(i), 0)),
                  pl.BlockSpec((HALO, D), lambda i: (0, 0)), pl.BlockSpec((1, D), lambda i: (0, 0)),
                  pl.BlockSpec((1, D), lambda i: (0, 0)), pl.BlockSpec(memory_space=pl.ANY)],
        out_specs=[pl.BlockSpec((2, tm, D), lambda i: (2, rev(i), 0)),
                   pl.BlockSpec((HALO, D), lambda i: (0, 0)), pl.BlockSpec((8, D), lambda i: (0, 0))],
        out_shape=[jax.ShapeDtypeStruct(dp.shape, dp.dtype), jax.ShapeDtypeStruct((HALO, D), F32),
                   jax.ShapeDtypeStruct((8, D), F32)],
        aliases={7: 0},
        scratch_shapes=[pltpu.VMEM((HALO + tm, D), F32), pltpu.VMEM((tm + HALO, D), F32),
                        pltpu.VMEM((7, n, LANES), F32), pltpu.VMEM((8 * CONV_K, LANES), F32)],
    )(p, p, u1, du2, cw, lng, lnb, dp)
    return out[:3], out[3:]


def _mixout_fwd(x, oa, u2, p, mod, mo, w_a, w_b, w_o):
    T = x.shape[0]
    tm = min(512, T)

    def body(x_ref, oa_ref, u2_ref, p_ref, mod_ref, wa_ref, wb_ref, wo_ref, xo_ref, ya_ref, yb_ref, mo_ref):
        ya = _mm(oa_ref[...], wa_ref[...])
        yb = _mm(u2_ref[...], wb_ref[...])
        ya_ref[...] = ya.astype(ACT)
        yb_ref[...] = yb.astype(ACT)
        merged = _sig(p_ref[0]) * ya + _sig(p_ref[1]) * yb
        out = _mm(merged, wo_ref[...])
        mo_ref[...] = out
        xo_ref[...] = x_ref[...] + mod_ref[mo + 2:mo + 3, :] * out

    tile = pl.BlockSpec((tm, D), lambda i: (i, 0))
    wspec = pl.BlockSpec((D, D), lambda i: (0, 0))
    return pl.pallas_call(
        body, name="mixout_fwd", grid=(T // tm,),
        in_specs=[tile, tile, tile, pl.BlockSpec((2, tm, D), lambda i: (3, i, 0)),
                  pl.BlockSpec((9, D), lambda i: (0, 0)), wspec, wspec, wspec],
        out_specs=[tile, tile, tile, tile],
        out_shape=[jax.ShapeDtypeStruct((T, D), F32), jax.ShapeDtypeStruct((T, D), ACT),
                   jax.ShapeDtypeStruct((T, D), ACT), jax.ShapeDtypeStruct((T, D), F32)],
        compiler_params=_cparams(1),
    )(x, oa, u2, p, mod, w_a, w_b, w_o)


def _mixout_bwd(dxo, oa, u2, ya, yb, mout, p, mod, mo, w_a, w_b, w_o):
    T = dxo.shape[0]
    tm = min(256, T)

    def body(dxo_ref, oa_ref, u2_ref, ya_ref, yb_ref, mo_ref, p_ref, mod_ref, wa_ref, wb_ref, wo_ref,
             dp_ref, doa_ref, du2_ref, dwa_ref, dwb_ref, dwo_ref, sm_ref):
        @pl.when(pl.program_id(0) == 0)
        def _():
            dwa_ref[...] = jnp.zeros_like(dwa_ref)
            dwb_ref[...] = jnp.zeros_like(dwb_ref)
            dwo_ref[...] = jnp.zeros_like(dwo_ref)
            sm_ref[...] = jnp.zeros_like(sm_ref)

        dxo_v = dxo_ref[...]
        sm_ref[2:3, :] += _colsum(dxo_v * mo_ref[...])
        dmo = (mod_ref[mo + 2:mo + 3, :] * dxo_v).astype(MM)
        ya = ya_ref[...].astype(F32)
        yb = yb_ref[...].astype(F32)
        sga = _sig(p_ref[0])
        sgb = _sig(p_ref[1])
        merged = (sga * ya + sgb * yb).astype(MM)
        dwo_ref[...] += _mm_tn(merged, dmo)
        dmg = _mm_nt(dmo, wo_ref[...])
        dp_ref[0] = (dmg * ya * sga * (1.0 - sga)).astype(ACT)
        dp_ref[1] = (dmg * yb * sgb * (1.0 - sgb)).astype(ACT)
        dya = (dmg * sga).astype(MM)
        dyb = (dmg * sgb).astype(MM)
        dwa_ref[...] += _mm_tn(oa_ref[...], dya)
        dwb_ref[...] += _mm_tn(u2_ref[...], dyb)
        doa_ref[...] = _mm_nt(dya, wa_ref[...])
        du2_ref[...] = _mm_nt(dyb, wb_ref[...])

    tile = pl.BlockSpec((tm, D), lambda i: (i, 0))
    wspec = pl.BlockSpec((D, D), lambda i: (0, 0))
    return pl.pallas_call(
        body, name="mixout_bwd", grid=(T // tm,),
        in_specs=[tile, tile, tile, tile, tile, tile, pl.BlockSpec((2, tm, D), lambda i: (3, i, 0)),
                  pl.BlockSpec((9, D), lambda i: (0, 0)), wspec, wspec, wspec],
        out_specs=[pl.BlockSpec((2, tm, D), lambda i: (3, i, 0)), tile, tile, wspec, wspec, wspec,
                   pl.BlockSpec((8, D), lambda i: (0, 0))],
        out_shape=[jax.ShapeDtypeStruct((8, T, D), ACT), jax.ShapeDtypeStruct((T, D), F32),
                   jax.ShapeDtypeStruct((T, D), F32), jax.ShapeDtypeStruct((D, D), F32),
                   jax.ShapeDtypeStruct((D, D), F32), jax.ShapeDtypeStruct((D, D), F32),
                   jax.ShapeDtypeStruct((8, D), F32)],
        compiler_params=_cparams(1),
    )(dxo, oa, u2, ya, yb, mout, p, mod, w_a, w_b, w_o)


def _ada_wgrad(cs_all, dmod_cols):
    cs_t = jnp.pad(cs_all.T, ((0, 0), (0, HD - N_DEV)))
    dm = jnp.pad(dmod_cols, ((0, HD - N_DEV), (0, 0)))

    def body(cs_ref, d_ref, out_ref):
        out_ref[...] = jnp.dot(cs_ref[...], d_ref[...], preferred_element_type=F32,
                               precision=lax.Precision.HIGHEST)

    return pl.pallas_call(
        body, name="ada_wgrad", out_shape=jax.ShapeDtypeStruct((D, dmod_cols.shape[1]), F32),
        compiler_params=pltpu.CompilerParams(vmem_limit_bytes=VMEM_LIMIT),
    )(cs_t, dm)


def _adam_math(w, g, m, v):
    m2 = ADAM_B1 * m + (1.0 - ADAM_B1) * g
    v2 = ADAM_B2 * v + (1.0 - ADAM_B2) * (g * g)
    m_hat = m2 / (1.0 - ADAM_B1 ** ADAM_STEP)
    v_hat = v2 / (1.0 - ADAM_B2 ** ADAM_STEP)
    delta = -ADAM_LR * (m_hat / (jnp.sqrt(v_hat) + ADAM_EPS) + ADAM_WD * w)
    return delta, m2, v2


def _adamw(w, m, v, g, name):
    R, C = w.shape
    slots = g.ndim == 3
    n_slots = g.shape[0] if slots else 0
    tr = R
    for cand in (256, 176):
        if R % cand == 0 and R > cand:
            tr = cand
            break

    def body(w_ref, m_ref, v_ref, g_ref, go_ref, d_ref, mo_ref, vo_ref):
        if slots:
            gv = g_ref[0].astype(F32)
            for s in range(1, n_slots):
                gv = gv + g_ref[s].astype(F32)
        else:
            gv = g_ref[...]
        go_ref[...] = gv
        d_ref[...], mo_ref[...], vo_ref[...] = _adam_math(w_ref[...], gv, m_ref[...], v_ref[...])

    tile = pl.BlockSpec((tr, C), lambda i: (i, 0))
    gspec = pl.BlockSpec((n_slots, tr, C), lambda i: (0, i, 0)) if slots else tile
    sds = jax.ShapeDtypeStruct((R, C), F32)
    return pl.pallas_call(
        body, name=name, grid=(R // tr,), in_specs=[tile, tile, tile, gspec], out_specs=[tile] * 4,
        out_shape=[sds] * 4, compiler_params=_cparams(1),
    )(w, m, v, g)


def _adamw_small(tot, names, params, grad_rows):
    k = len(names)

    def body(tot_ref, *refs):
        ins, outs = refs[:3 * k], refs[3 * k:]
        for i, n in enumerate(names):
            w_ref, m_ref, v_ref = ins[3 * i:3 * i + 3]
            go, do, mo, vo = outs[4 * i:4 * i + 4]
            row = grad_rows[n]
            for j in range(w_ref.shape[1] // D):
                ls = slice(j * D, (j + 1) * D)
                g = tot_ref[row + j:row + j + 1, :]
                w = w_ref[:, ls]
                if n == "hgrn_lb":
                    p0 = _sig(w[0:1] - w[1:2])
                    dz0 = p0 * (1.0 - p0) * g
                    g = jnp.concatenate([dz0, -dz0], axis=0)
                go[:, ls] = g
                do[:, ls], mo[:, ls], vo[:, ls] = _adam_math(w, g, m_ref[:, ls], v_ref[:, ls])

    flat = [t for n in names for t in params[n]]
    out_shape = [jax.ShapeDtypeStruct(params[n][0].shape, F32) for n in names for _ in range(4)]
    outs = pl.pallas_call(body, name="adamw_small", out_shape=out_shape)(tot, *flat)
    return {n: tuple(outs[4 * i:4 * i + 4]) for i, n in enumerate(names)}


def _sum_slots(pack, name, tr):
    n, R, C = pack.shape

    def body(p_ref, out_ref):
        acc = p_ref[0].astype(F32)
        for s in range(1, n):
            acc = acc + p_ref[s].astype(F32)
        out_ref[...] = acc

    return pl.pallas_call(
        body, name=name, grid=(R // tr,), in_specs=[pl.BlockSpec((n, tr, C), lambda i: (0, i, 0))],
        out_specs=pl.BlockSpec((tr, C), lambda i: (i, 0)), out_shape=jax.ShapeDtypeStruct((R, C), F32),
        compiler_params=_cparams(1))(pack)


def _me():
    return lax.axis_index("x"), lax.axis_index("y"), lax.axis_index("c")


def _peer(r):
    x, y, c = _me()
    px = 1 - x if r & 4 else x
    py = 1 - y if r & 2 else y
    pc = 1 - c if r & 1 else c
    return (px, py, pc), 4 * px + 2 * py + pc


def _small_gather(x_ref, out_ref, send_sems, recv_sems):
    R = x_ref.shape[0]
    mx, my, mc = _me()
    me = 4 * mx + 2 * my + mc
    mine = out_ref.at[pl.ds(pl.multiple_of(me * R, 8), R), :]
    copies = []
    for r in range(1, N_DEV):
        dev, _ = _peer(r)
        copies.append(pltpu.make_async_remote_copy(
            src_ref=x_ref, dst_ref=mine, send_sem=send_sems.at[r - 1], recv_sem=recv_sems.at[r - 1],
            device_id=dev, device_id_type=MESH))
    for cp in copies:
        cp.start()
    mine[...] = x_ref[...]
    for r in range(1, N_DEV):
        dev, idx = _peer(r)
        theirs = out_ref.at[pl.ds(pl.multiple_of(idx * R, 8), R), :]
        pltpu.make_async_remote_copy(
            src_ref=x_ref, dst_ref=theirs, send_sem=send_sems.at[r - 1], recv_sem=recv_sems.at[r - 1],
            device_id=dev, device_id_type=MESH).wait_recv()
    for cp in copies:
        cp.wait_send()


def _prologue(cs, ada_w, ada_b_cols, big):
    n = len(big)
    ncol = ada_w.shape[1]
    big_shape, big_sems = _xchg_specs(big, "gather")

    def body(cs_ref, w_ref, b_ref, *rest):
        big_in, cs_all, mod_all, big_out = rest[:n], rest[n], rest[n + 1], rest[n + 2:2 * n + 2]
        mod_scr, s1, r1, s2, r2 = rest[2 * n + 2:2 * n + 7]
        sems = rest[2 * n + 7:]
        _small_gather(cs_ref, cs_all, s1, r1)
        pick = (lax.broadcasted_iota(jnp.int32, (N_DEV, N_DEV * 8), 1)
                == 8 * lax.broadcasted_iota(jnp.int32, (N_DEV, N_DEV * 8), 0)).astype(F32)
        per_device = jnp.dot(pick, cs_all[...], preferred_element_type=F32, precision=lax.Precision.HIGHEST)
        mod_scr[...] = jnp.dot(per_device, w_ref[...], preferred_element_type=F32,
                               precision=lax.Precision.HIGHEST) + b_ref[...]
        _small_gather(mod_scr, mod_all, s2, r2)
        _xchg_start(big_in, big_out, sems, "gather")
        _xchg_wait(big_in, big_out, sems, "gather")

    vmem = pl.BlockSpec(memory_space=pltpu.VMEM)
    hbm = pl.BlockSpec(memory_space=pl.ANY)
    dma7 = pltpu.SemaphoreType.DMA((N_DEV - 1,))
    out = pl.pallas_call(
        body, name="prologue",
        out_shape=[jax.ShapeDtypeStruct((N_DEV * 8, D), F32), jax.ShapeDtypeStruct((N_DEV * 8, ncol), F32)]
        + big_shape,
        in_specs=[vmem, vmem, vmem] + [hbm] * n, out_specs=[vmem, vmem] + [hbm] * n,
        scratch_shapes=[pltpu.VMEM((8, ncol), F32), dma7, dma7, dma7, dma7] + big_sems,
        compiler_params=pltpu.CompilerParams(vmem_limit_bytes=VMEM_LIMIT),
    )(cs, ada_w, ada_b_cols, *big)
    return out[0], out[1], out[2:]


def _allgather_small(x):
    R, C = x.shape

    def body(x_ref, out_ref, send_sems, recv_sems):
        _small_gather(x_ref, out_ref, send_sems, recv_sems)

    return pl.pallas_call(
        body, name="allgather_small_%dx%d" % (R, C),
        out_shape=jax.ShapeDtypeStruct((N_DEV * R, C), F32),
        in_specs=[pl.BlockSpec(memory_space=pltpu.VMEM)], out_specs=pl.BlockSpec(memory_space=pltpu.VMEM),
        scratch_shapes=[pltpu.SemaphoreType.DMA((N_DEV - 1,)), pltpu.SemaphoreType.DMA((N_DEV - 1,))],
    )(x)


N_CHIP = N_DEV // 2


def _xchg_copies(ins, outs, sems, mode):
    send_sems, recv_sems, local_sems = sems
    mx, my, mc = _me()
    me = 4 * mx + 2 * my + mc
    my_chip = 2 * mx + my
    sibling = _peer(1)[0]

    def rdma(a, r, dev, src, slot):
        k = a * (N_DEV - 1) + r - 1
        return pltpu.make_async_remote_copy(
            src_ref=src, dst_ref=outs[a].at[slot], send_sem=send_sems.at[k], recv_sem=recv_sems.at[k],
            device_id=dev, device_id_type=MESH)

    own, sends, relays, recvs = [], [], [], []
    for a in range(len(ins)):
        if mode == "pair":
            for chip in range(N_CHIP):
                src = ins[a].at[2 * chip + 1 - mc]
                sends.append(rdma(a, chip + 1, sibling, src, chip))
                recvs.append(rdma(a, chip + 1, sibling, src, chip))
            continue
        if mode == "quad":
            own.append(pltpu.make_async_copy(ins[a].at[my_chip], outs[a].at[my_chip], local_sems.at[a]))
            for r in (2, 4, 6):
                dev, idx = _peer(r)
                chip = idx // 2
                sends.append(rdma(a, r, dev, ins[a].at[chip], my_chip))
                recvs.append(rdma(a, r, dev, ins[a].at[chip], chip))
            continue
        gather = mode == "gather"
        own.append(pltpu.make_async_copy(ins[a] if gather else ins[a].at[me], outs[a].at[me], local_sems.at[a]))
        for r in range(1, N_DEV):
            dev, idx = _peer(r)
            if not gather:
                sends.append(rdma(a, r, dev, ins[a].at[idx], me))
                recvs.append(rdma(a, r, dev, ins[a].at[idx], idx))
            elif r == 1:
                sends.append(rdma(a, r, dev, ins[a], me))
                recvs.append(rdma(a, r, dev, ins[a], idx))
            elif r % 2 == 0:
                sends.append(rdma(a, r, dev, ins[a], me))
                relays.append((rdma(a, r, dev, ins[a], idx), rdma(a, r + 1, sibling, outs[a].at[idx], idx)))
            else:
                recvs.append(rdma(a, r, sibling, ins[a], idx))
    return own, sends, relays, recvs


def _xchg_start(ins, outs, sems, mode):
    own, sends, _, _ = _xchg_copies(ins, outs, sems, mode)
    for cp in own + sends:
        cp.start()


def _xchg_wait(ins, outs, sems, mode):
    own, sends, relays, recvs = _xchg_copies(ins, outs, sems, mode)
    for arrival, relay in relays:
        arrival.wait_recv()
        relay.start()
    for cp in recvs:
        cp.wait_recv()
    for cp in own:
        cp.wait()
    for cp in sends + [relay for _, relay in relays]:
        cp.wait_send()


def _xchg_specs(arrays, mode):
    n = len(arrays)
    shape = {"gather": lambda s: (N_DEV,) + s, "scatter": lambda s: s, "pair": lambda s: (N_CHIP,) + s[1:],
             "quad": lambda s: s}[mode]
    out_shape = [jax.ShapeDtypeStruct(shape(a.shape), a.dtype) for a in arrays]
    sems = [pltpu.SemaphoreType.DMA((n * (N_DEV - 1),)), pltpu.SemaphoreType.DMA((n * (N_DEV - 1),)),
            pltpu.SemaphoreType.DMA((n,))]
    return out_shape, sems


def _exchange(arrays, mode, name):
    n = len(arrays)

    def body(*refs):
        _xchg_start(refs[:n], refs[n:2 * n], refs[2 * n:], mode)
        _xchg_wait(refs[:n], refs[n:2 * n], refs[2 * n:], mode)

    out_shape, sems = _xchg_specs(arrays, mode)
    return pl.pallas_call(
        body, name=name, out_shape=out_shape,
        in_specs=[pl.BlockSpec(memory_space=pl.ANY)] * n, out_specs=[pl.BlockSpec(memory_space=pl.ANY)] * n,
        scratch_shapes=sems,
    )(*arrays)


def _gridded(body, carry, *, name, grid, in_specs, out_specs, out_shape, scratch_shapes=(), aliases=None):
    if carry is None:
        return pl.pallas_call(
            body, name=name, grid=grid, in_specs=list(in_specs), out_specs=list(out_specs),
            out_shape=list(out_shape), scratch_shapes=list(scratch_shapes), input_output_aliases=aliases or {},
            compiler_params=_cparams(len(grid)))
    arrays, mode = carry
    n, n_in, n_out, n_scr = len(arrays), len(in_specs), len(out_specs), len(scratch_shapes)
    c_shape, c_sems = _xchg_specs(arrays, mode)

    def wrapped(*refs):
        ins, cin = refs[:n_in], refs[n_in:n_in + n]
        o0 = n_in + n
        outs, cout = refs[o0:o0 + n_out], refs[o0 + n_out:o0 + n_out + n]
        s0 = o0 + n_out + n
        scr, sems = refs[s0:s0 + n_scr], refs[s0 + n_scr:]
        first = pl.program_id(0) == 0
        last = pl.program_id(0) == grid[0] - 1
        for ax in range(1, len(grid)):
            first = first & (pl.program_id(ax) == 0)
            last = last & (pl.program_id(ax) == grid[ax] - 1)

        @pl.when(first)
        def _():
            _xchg_start(cin, cout, sems, mode)

        body(*ins, *outs, *scr)

        @pl.when(last)
        def _():
            _xchg_wait(cin, cout, sems, mode)

    hbm = pl.BlockSpec(memory_space=pl.ANY)
    res = pl.pallas_call(
        wrapped, name=name, grid=grid, in_specs=list(in_specs) + [hbm] * n, out_specs=list(out_specs) + [hbm] * n,
        out_shape=list(out_shape) + c_shape, scratch_shapes=list(scratch_shapes) + c_sems,
        input_output_aliases=aliases or {}, compiler_params=_cparams(len(grid)),
    )
    return lambda *args: res(*args, *arrays)


def _local_step(x, target, mod, small, sh, w1):
    w1_in, w1_out = w1[0].reshape(2, D_FF, D), w1[1].reshape(D_FF, D)
    (x1, a1, b1, f1, h1, h2), (wm_in,) = _ffn_fwd(x, mod, 0, small["norm_ffn1"], w1_in, w1_out, 0.5, "ffn1_fwd",
                                                  ([sh["mix_w_in"]], "gather"), nxt=(small["norm_mix"], 3))
    (p,), (wh_o, wc_o, wm_o, cw) = _mixin_fwd(
        h2, wm_in, ([sh["hgrn_w_o"], sh["conv_w_o"], sh["mix_w_out"], sh["conv_w"]], "gather"))
    wh_o, wc_o, wm_o = wh_o.reshape(D, D), wc_o.reshape(D, D), wm_o.reshape(D, D)
    cw = jnp.pad(cw.transpose(1, 0, 2).reshape(CONV_K, D), ((0, HALO - CONV_K), (0, 0)))
    (o, oa, a_all, s_all), (w2_in,) = _hgrn_fwd(p, small["hgrn_lb"], small["hgrn_g"], ([sh["ffn2_w_in"]], "gather"))
    (u1, u2), (w2_out,) = _conv_fwd(p, cw, small["conv_b"], small["conv_ln_g"], small["conv_ln_b"],
                                    ([sh["ffn2_w_out"]], "gather"))
    w2_in, w2_out = w2_in.reshape(2, D_FF, D), w2_out.reshape(D_FF, D)
    x2, ya, yb, mout = _mixout_fwd(x1, oa, u2, p, mod, 3, wh_o, wc_o, wm_o)
    (x3, a3, b3, f3, h3), _ = _ffn_fwd(x2, mod, 6, small["norm_ffn2"], w2_in, w2_out, 0.5, "ffn2_fwd", None)
    dx3, df3, sm_head = _head(x3, target, small["norm_final"], mod, 8, 0.5)

    (da3, db3, dw2_in, dw2_out), _ = _ffn_bwd_w(h3, df3, a3, b3, w2_out, "ffn2_bwd_w", None)
    rows = lambda t: t.reshape(N_DEV, -1, D).astype(MM)
    (dx2, sm3), (r2_out,) = _ffn_bwd_x(x2, dx3, f3, da3, db3, mod, 6, small["norm_ffn2"], w2_in, 0.5, "ffn2_bwd_x",
                                       ([rows(dw2_out)], "scatter"))
    dp, doa, du2, dwh_o, dwc_o, dwm_o, sm_mo = _mixout_bwd(dx2, oa, u2, ya, yb, mout, p, mod, 3, wh_o, wc_o, wm_o)
    (dp, dcw, sm_cv), (r2_in,) = _conv_bwd(p, u1, du2, cw, small["conv_ln_g"], small["conv_ln_b"], dp,
                                           ([rows(dw2_in)], "scatter"))
    (dp, sm_hg), _ = _hgrn_bwd(p, o, a_all, s_all, doa, small["hgrn_lb"], small["hgrn_g"], dp, None)
    (dx1, dwm_in, sm2, df1), (rh_o, rc_o, rm_o, rcw) = _mixin_bwd(
        x1, h2, dx2, dp, mod, 3, small["norm_mix"], wm_in, 2, 0.5,
        ([rows(dwh_o), rows(dwc_o), rows(dwm_o), dcw[:CONV_K].reshape(CONV_K, N_DEV, -1).transpose(1, 0, 2)],
         "scatter"))
    (da1, db1, dw1_in, dw1_out), (rm_in,) = _ffn_bwd_w(h1, df1, a1, b1, w1_out, "ffn1_bwd_w",
                                                      (_pair_reduce([dwm_in], "pair_mix"), "quad"))
    (dx0, sm1), (r1_in, r1_out) = _ffn_bwd_x(
        x, dx1, f1, da1, db1, mod, 0, small["norm_ffn1"], w1_in, 0.5, "ffn1_bwd_x",
        (_pair_reduce([rows(dw1_in), rows(dw1_out)], "pair_ffn1"), "quad"))

    dmod = jnp.concatenate([sm1[0:3], sm2[0:2], sm_mo[2:3], sm3[0:3]], axis=0)
    gsmall = dict(norm_ffn1=sm1[3:4], norm_mix=sm2[3:4], lb0=sm_hg[0:1], hgrn_g=sm_hg[1:2], conv_b=sm_cv[0:1],
                  conv_ln_g=sm_cv[1:2], conv_ln_b=sm_cv[2:3], norm_ffn2=sm3[3:4], norm_final=sm_head[0:1])
    recv = dict(ffn1_w_in=r1_in, ffn1_w_out=r1_out, mix_w_in=rm_in, hgrn_w_o=rh_o, conv_w=rcw, conv_w_o=rc_o,
                mix_w_out=rm_o, ffn2_w_in=r2_in, ffn2_w_out=r2_out)
    return sm_head[1, 0], dx0, dmod, gsmall, recv


def _pair_add(mine, theirs, core, name):
    _, R, C = theirs.shape

    def body(core_ref, a_ref, b_ref, out_ref):
        del core_ref
        out_ref[0] = (a_ref[0, 0].astype(F32) + b_ref[0].astype(F32)).astype(out_ref.dtype)

    parts = 4 if R % 64 == 0 else 2
    tr = R // parts
    blk = pl.BlockSpec((1, tr, C), lambda s, r, core_ref: (s, r, 0))
    grid_spec = pltpu.PrefetchScalarGridSpec(
        num_scalar_prefetch=1, grid=(N_CHIP, parts),
        in_specs=[pl.BlockSpec((1, 1, tr, C), lambda s, r, core_ref: (s, core_ref[0], r, 0)), blk], out_specs=blk)
    return pl.pallas_call(body, name=name, grid_spec=grid_spec,
                          out_shape=jax.ShapeDtypeStruct(theirs.shape, mine.dtype), compiler_params=_cparams(2),
                          )(core, mine.reshape(N_CHIP, 2, R, C), theirs)


def _pair_reduce(arrays, name):
    theirs = _exchange(arrays, "pair", name)
    core = lax.axis_index("c").astype(jnp.int32).reshape(1)
    return [_pair_add(a, t, core, "%s_add%d" % (name, i)) for i, (a, t) in enumerate(zip(arrays, theirs))]


SMALL_ORDER = ("norm_ffn1", "norm_mix", "lb0", "hgrn_g", "conv_b", "conv_ln_g", "conv_ln_b", "norm_ffn2",
               "norm_final")
PACK_ROWS = 24


def kernel(x, c, ada_w, ada_b, norm_ffn1, ffn1_w_in, ffn1_w_out, norm_mix, mix_w_in, hgrn_lb, hgrn_g, hgrn_w_o, conv_w, conv_b, conv_ln_g, conv_ln_b, conv_w_o, mix_w_out, norm_ffn2, ffn2_w_in, ffn2_w_out, norm_final, loss_target, m_ada_w, m_ada_b, m_norm_ffn1, m_ffn1_w_in, m_ffn1_w_out, m_norm_mix, m_mix_w_in, m_hgrn_lb, m_hgrn_g, m_hgrn_w_o, m_conv_w, m_conv_b, m_conv_ln_g, m_conv_ln_b, m_conv_w_o, m_mix_w_out, m_norm_ffn2, m_ffn2_w_in, m_ffn2_w_out, m_norm_final, v_ada_w, v_ada_b, v_norm_ffn1, v_ffn1_w_in, v_ffn1_w_out, v_norm_mix, v_mix_w_in, v_hgrn_lb, v_hgrn_g, v_hgrn_w_o, v_conv_w, v_conv_b, v_conv_ln_g, v_conv_ln_b, v_conv_w_o, v_mix_w_out, v_norm_ffn2, v_ffn2_w_in, v_ffn2_w_out, v_norm_final):
    mx, my, mc = _me()
    me = 4 * mx + 2 * my + mc
    ncol = ada_w.shape[2]

    sh = dict(ffn1_w_out=ffn1_w_out, mix_w_in=mix_w_in, hgrn_w_o=hgrn_w_o, conv_w_o=conv_w_o,
              mix_w_out=mix_w_out, ffn2_w_out=ffn2_w_out)
    sh = {n: w[0].astype(MM) for n, w in sh.items()}
    sh["ffn1_w_in"] = ffn1_w_in[0].T.astype(MM)
    sh["ffn2_w_in"] = ffn2_w_in[0].T.astype(MM)
    sh["conv_w"] = conv_w[0]
    small = dict(norm_ffn1=norm_ffn1, norm_mix=norm_mix, hgrn_lb=hgrn_lb, hgrn_g=hgrn_g, conv_b=conv_b,
                 conv_ln_g=conv_ln_g, conv_ln_b=conv_ln_b, norm_ffn2=norm_ffn2, norm_final=norm_final.reshape(1, D))

    cs = jnp.broadcast_to(c * jax.nn.sigmoid(c), (8, D))
    ada_b_cols = lax.dynamic_slice(ada_b, (0, me * ncol), (1, ncol))
    cs_all, mod_all, w1 = _prologue(cs, ada_w[0], ada_b_cols, [sh["ffn1_w_in"], sh["ffn1_w_out"]])
    cs_all = cs_all.reshape(N_DEV, 8, D)[:, 0, :]
    mod = lax.dynamic_index_in_dim(mod_all.reshape(N_DEV, N_DEV, ncol), me, axis=1, keepdims=False).reshape(9, D)

    loss_local, dx, dmod, gsmall, recv = _local_step(x[0], loss_target[0], mod, small, sh, w1)

    n_used = 9 + len(SMALL_ORDER) + 1
    pack = jnp.concatenate([dmod] + [gsmall[n] for n in SMALL_ORDER] + [jnp.broadcast_to(loss_local, (1, D))]
                           + [jnp.zeros((PACK_ROWS - n_used, D), F32)], axis=0)
    pack_all = _allgather_small(pack).reshape(N_DEV, PACK_ROWS, D)
    tot = _sum_slots(pack_all, "sum_small", PACK_ROWS)
    loss = tot[n_used - 1, 0]
    dmod_all = pack_all[:, 0:9, :].reshape(N_DEV, 9 * D)
    g_ada_w = _ada_wgrad(cs_all, lax.dynamic_slice(dmod_all, (0, me * ncol), (N_DEV, ncol)))

    res = {}
    res["ada_w"] = _adamw(ada_w[0], m_ada_w[0], v_ada_w[0], g_ada_w, "adamw_ada_w")
    big = dict(ffn1_w_in=(ffn1_w_in, m_ffn1_w_in, v_ffn1_w_in), ffn1_w_out=(ffn1_w_out, m_ffn1_w_out, v_ffn1_w_out),
               mix_w_in=(mix_w_in, m_mix_w_in, v_mix_w_in), hgrn_w_o=(hgrn_w_o, m_hgrn_w_o, v_hgrn_w_o),
               conv_w=(conv_w, m_conv_w, v_conv_w), conv_w_o=(conv_w_o, m_conv_w_o, v_conv_w_o),
               mix_w_out=(mix_w_out, m_mix_w_out, v_mix_w_out), ffn2_w_in=(ffn2_w_in, m_ffn2_w_in, v_ffn2_w_in),
               ffn2_w_out=(ffn2_w_out, m_ffn2_w_out, v_ffn2_w_out))
    for n, (w, m, v) in big.items():
        if n in ("ffn1_w_in", "ffn2_w_in"):
            res[n] = tuple(t.T for t in _adamw(w[0].T, m[0].T, v[0].T, recv[n], "adamw_" + n))
        else:
            res[n] = _adamw(w[0], m[0], v[0], recv[n], "adamw_" + n)
    sm_names = ("ada_b", "norm_ffn1", "norm_mix", "hgrn_lb", "hgrn_g", "conv_b", "conv_ln_g", "conv_ln_b",
                "norm_ffn2", "norm_final")
    sm_w = dict(ada_b=(ada_b, m_ada_b, v_ada_b), norm_ffn1=(norm_ffn1, m_norm_ffn1, v_norm_ffn1),
                norm_mix=(norm_mix, m_norm_mix, v_norm_mix), hgrn_lb=(hgrn_lb, m_hgrn_lb, v_hgrn_lb),
                hgrn_g=(hgrn_g, m_hgrn_g, v_hgrn_g), conv_b=(conv_b, m_conv_b, v_conv_b),
                conv_ln_g=(conv_ln_g, m_conv_ln_g, v_conv_ln_g), conv_ln_b=(conv_ln_b, m_conv_ln_b, v_conv_ln_b),
                norm_ffn2=(norm_ffn2, m_norm_ffn2, v_norm_ffn2), norm_final=(norm_final, m_norm_final, v_norm_final))
    sm_w["norm_final"] = tuple(t.reshape(1, D) for t in sm_w["norm_final"])
    grad_rows = dict({n: 9 + i for i, n in enumerate(SMALL_ORDER)}, ada_b=0, hgrn_lb=9 + SMALL_ORDER.index("lb0"))
    res.update(_adamw_small(tot, sm_names, sm_w, grad_rows))
    res["norm_final"] = tuple(t.reshape(norm_final.shape) for t in res["norm_final"])

    order = ("ada_w", "ada_b", "norm_ffn1", "ffn1_w_in", "ffn1_w_out", "norm_mix", "mix_w_in", "hgrn_lb", "hgrn_g",
             "hgrn_w_o", "conv_w", "conv_b", "conv_ln_g", "conv_ln_b", "conv_w_o", "mix_w_out", "norm_ffn2",
             "ffn2_w_in", "ffn2_w_out", "norm_final")
    lead = lambda n, t: t[None] if n in big or n == "ada_w" else t
    outs = [loss, dx[None]]
    for j in range(4):
        outs += [lead(n, res[n][j]) for n in order]
    return tuple(outs)
```

```python
import jax
import jax.numpy as jnp
from jax import lax
from jax.experimental import pallas as pl
from jax.experimental.pallas import tpu as pltpu

F32 = jnp.float32
MM = jnp.bfloat16
ACT = jnp.bfloat16

D = 1024
D_FF = 2816
HEADS = 8
HD = 128
CHUNK = 64
SUB = 16
NSUB = CHUNK // SUB
HGRN_BLOCK = 1024
SAFE_EXP = 60.0
CONV_K = 31
HALO = 32
EPS = 1e-6
N_DEV = 8
NEG = -1e30
Q_SCALE = HD ** -0.5

ADAM_LR = 0.001
ADAM_B1 = 0.9
ADAM_B2 = 0.999
ADAM_EPS = 1e-08
ADAM_WD = 0.01
ADAM_STEP = 10

V7X_VMEM_BYTES = 64 * 1024 * 1024
VMEM_LIMIT = V7X_VMEM_BYTES - 4 * 1024 * 1024
MESH = pl.DeviceIdType.MESH


def _cparams(n_axes):
    return pltpu.CompilerParams(dimension_semantics=("arbitrary",) * n_axes, vmem_limit_bytes=VMEM_LIMIT)


def _mm(a, b):
    return lax.dot_general(a.astype(MM), b.astype(MM), (((1,), (0,)), ((), ())), preferred_element_type=F32)


def _mm_nt(a, b):
    return lax.dot_general(a.astype(MM), b.astype(MM), (((1,), (1,)), ((), ())), preferred_element_type=F32)


def _mm_tn(a, b):
    return lax.dot_general(a.astype(MM), b.astype(MM), (((0,), (0,)), ((), ())), preferred_element_type=F32)


def _sig(x):
    return 1.0 / (1.0 + jnp.exp(-x))


def _colsum(x):
    return jnp.sum(x, axis=0, keepdims=True)


def _rowmean(x):
    return jnp.mean(x, axis=-1, keepdims=True)


def _modnorm_fwd(xv, g, sh, sc):
    r = lax.rsqrt(_rowmean(xv * xv) + EPS)
    xh = xv * r
    n = xh * g
    return n * (1.0 + sc) + sh, xh, n, r


def _modnorm_bwd(dh, xh, n, r, g, sc):
    dsc = _colsum(dh * n)
    dsh = _colsum(dh)
    dn = dh * (1.0 + sc)
    dg = _colsum(dn * xh)
    dxh = dn * g
    dx = r * (dxh - xh * _rowmean(dxh * xh))
    return dx, dsh, dsc, dg


def _ffn_fwd(x, mod, mo, gnorm, w_in_t, w_out, res, name, carry, nxt=None):
    T = x.shape[0]
    tm = min(512, T)
    tn = D_FF // 2

    def body(x_ref, mod_ref, g_ref, wi_ref, wo_ref, *rest):
        if nxt is None:
            xo_ref, a_ref, b_ref, f_ref, h_ref = rest
        else:
            gn_ref, xo_ref, a_ref, b_ref, f_ref, h_ref, hn_ref = rest
        xv = x_ref[...]
        h, _, _, _ = _modnorm_fwd(xv, g_ref[...], mod_ref[mo:mo + 1, :], mod_ref[mo + 1:mo + 2, :])
        h = h.astype(ACT)
        h_ref[...] = h
        f = None
        for c0 in range(0, D_FF, tn):
            a = _mm_nt(h, wi_ref[0, c0:c0 + tn, :])
            b = _mm_nt(h, wi_ref[1, c0:c0 + tn, :])
            a_ref[:, c0:c0 + tn] = a.astype(ACT)
            b_ref[:, c0:c0 + tn] = b.astype(ACT)
            part = _mm(a * _sig(a) * b, wo_ref[c0:c0 + tn, :])
            f = part if f is None else f + part
        f_ref[...] = f
        xo = xv + res * mod_ref[mo + 2:mo + 3, :] * f
        xo_ref[...] = xo
        if nxt is not None:
            hn, _, _, _ = _modnorm_fwd(xo, gn_ref[...], mod_ref[nxt[1]:nxt[1] + 1, :], mod_ref[nxt[1] + 1:nxt[1] + 2, :])
            hn_ref[...] = hn.astype(ACT)

    tile = pl.BlockSpec((tm, D), lambda i: (i, 0))
    wide = pl.BlockSpec((tm, D_FF), lambda i: (i, 0))
    row = pl.BlockSpec((1, D), lambda i: (0, 0))
    n_out = 5 if nxt is None else 6
    out = _gridded(
        body, carry, name=name, grid=(T // tm,),
        in_specs=[
            tile,
            pl.BlockSpec((9, D), lambda i: (0, 0)),
            row,
            pl.BlockSpec((2, D_FF, D), lambda i: (0, 0, 0), pipeline_mode=pl.Buffered(1)),
            pl.BlockSpec((D_FF, D), lambda i: (0, 0), pipeline_mode=pl.Buffered(1)),
        ] + ([] if nxt is None else [row]),
        out_specs=[tile, wide, wide, tile, tile] + ([] if nxt is None else [tile]),
        out_shape=[
            jax.ShapeDtypeStruct((T, D), F32),
            jax.ShapeDtypeStruct((T, D_FF), ACT),
            jax.ShapeDtypeStruct((T, D_FF), ACT),
            jax.ShapeDtypeStruct((T, D), F32),
            jax.ShapeDtypeStruct((T, D), ACT),
        ] + ([] if nxt is None else [jax.ShapeDtypeStruct((T, D), ACT)]),
    )(*((x, mod, gnorm, w_in_t, w_out) + (() if nxt is None else (nxt[0],))))
    return out[:n_out], out[n_out:]


def _ffn_bwd_w(h, df, a, b, w_out, name, carry):
    T = h.shape[0]
    tm = min(2048, T)
    ni = T // tm
    tn = 256
    nj = D_FF // tn

    def body(h_ref, df_ref, a_ref, b_ref, wo_ref, da_ref, db_ref, dwi_ref, dwo_ref, acc_i, acc_o):
        i = pl.program_id(1)

        @pl.when(i == 0)
        def _():
            acc_i[...] = jnp.zeros_like(acc_i)
            acc_o[...] = jnp.zeros_like(acc_o)

        hb = h_ref[...]
        df = df_ref[...]
        av = a_ref[...].astype(F32)
        bv = b_ref[...].astype(F32)
        sg = _sig(av)
        sa = av * sg
        s = (sa * bv).astype(MM)
        ds = _mm_nt(df, wo_ref[...])
        da = (ds * bv * sg * (1.0 + av * (1.0 - sg))).astype(MM)
        db = (ds * sa).astype(MM)
        da_ref[...] = da
        db_ref[...] = db
        acc_o[...] += _mm_tn(s, df)
        acc_i[0] += _mm_tn(da, hb)
        acc_i[1] += _mm_tn(db, hb)

        @pl.when(i == ni - 1)
        def _():
            dwi_ref[...] = acc_i[...].astype(MM)
            dwo_ref[...] = acc_o[...].astype(MM)

    out = _gridded(
        body, carry, name=name, grid=(nj, ni),
        in_specs=[
            pl.BlockSpec((tm, D), lambda j, i: (i, 0)),
            pl.BlockSpec((tm, D), lambda j, i: (i, 0)),
            pl.BlockSpec((tm, tn), lambda j, i: (i, j)),
            pl.BlockSpec((tm, tn), lambda j, i: (i, j)),
            pl.BlockSpec((tn, D), lambda j, i: (j, 0)),
        ],
        out_specs=[
            pl.BlockSpec((tm, tn), lambda j, i: (i, j)),
            pl.BlockSpec((tm, tn), lambda j, i: (i, j)),
            pl.BlockSpec((2, tn, D), lambda j, i: (0, j, 0)),
            pl.BlockSpec((tn, D), lambda j, i: (j, 0)),
        ],
        out_shape=[
            jax.ShapeDtypeStruct((T, D_FF), MM),
            jax.ShapeDtypeStruct((T, D_FF), MM),
            jax.ShapeDtypeStruct((2, D_FF, D), MM),
            jax.ShapeDtypeStruct((D_FF, D), MM),
        ],
        scratch_shapes=[pltpu.VMEM((2, tn, D), F32), pltpu.VMEM((tn, D), F32)],
    )(h, df, a, b, w_out)
    return out[:4], out[4:]


def _ffn_bwd_x(x, dxo, f, da, db, mod, mo, gnorm, w_in_t, res, name, carry):
    T = x.shape[0]
    tm = min(512, T)
    ni = T // tm
    tn = D_FF // 2
    nj = D_FF // tn

    def body(x_ref, dxo_ref, f_ref, da_ref, db_ref, mod_ref, g_ref, wi_ref, dx_ref, sm_ref, dh_scr):
        j = pl.program_id(0)
        i = pl.program_id(1)

        @pl.when((j == 0) & (i == 0))
        def _():
            sm_ref[...] = jnp.zeros_like(sm_ref)

        @pl.when(j == 0)
        def _():
            dh_scr[i] = jnp.zeros((tm, D), F32)

        dh_scr[i] += _mm(da_ref[...], wi_ref[0]) + _mm(db_ref[...], wi_ref[1])

        @pl.when(j == nj - 1)
        def _():
            sc = mod_ref[mo + 1:mo + 2, :]
            _, xh, n, r = _modnorm_fwd(x_ref[...], g_ref[...], mod_ref[mo:mo + 1, :], sc)
            dxn, dsh, dsc, dg = _modnorm_bwd(dh_scr[i], xh, n, r, g_ref[...], sc)
            dxo_v = dxo_ref[...]
            dx_ref[...] = dxo_v + dxn
            sm_ref[0:1, :] += dsh
            sm_ref[1:2, :] += dsc
            sm_ref[2:3, :] += _colsum(dxo_v * f_ref[...]) * res
            sm_ref[3:4, :] += dg

    last = pl.BlockSpec((tm, D), lambda j, i: (jnp.where(j == nj - 1, i, 0), 0))
    out = _gridded(
        body, carry, name=name, grid=(nj, ni),
        in_specs=[last, last, last,
                  pl.BlockSpec((tm, tn), lambda j, i: (i, j)), pl.BlockSpec((tm, tn), lambda j, i: (i, j)),
                  pl.BlockSpec((9, D), lambda j, i: (0, 0)), pl.BlockSpec((1, D), lambda j, i: (0, 0)),
                  pl.BlockSpec((2, tn, D), lambda j, i: (0, j, 0))],
        out_specs=[last, pl.BlockSpec((8, D), lambda j, i: (0, 0))],
        out_shape=[jax.ShapeDtypeStruct((T, D), F32), jax.ShapeDtypeStruct((8, D), F32)],
        scratch_shapes=[pltpu.VMEM((ni, tm, D), F32)],
    )(x, dxo, f, da, db, mod, gnorm, w_in_t)
    return out[:2], out[2:]


def _head(x, target, gfin, mod, gate_row, res):
    T = x.shape[0]
    tm = min(512, T)
    ni = T // tm

    def body(x_ref, t_ref, g_ref, mod_ref, dx_ref, df_ref, sm_ref):
        i = pl.program_id(0)

        @pl.when(i == 0)
        def _():
            sm_ref[...] = jnp.zeros_like(sm_ref)

        xv = x_ref[...]
        g = g_ref[...]
        r = lax.rsqrt(_rowmean(xv * xv) + EPS)
        xh = xv * r
        e = xh * g - t_ref[...]
        sm_ref[1:2, :] += _colsum(e * e) * (0.5 / D)
        dy = e * (1.0 / D)
        sm_ref[0:1, :] += _colsum(dy * xh)
        dxh = dy * g
        dx = r * (dxh - xh * _rowmean(dxh * xh))
        dx_ref[...] = dx
        df_ref[...] = (res * mod_ref[gate_row:gate_row + 1, :] * dx).astype(MM)

        @pl.when(i == ni - 1)
        def _():
            sm_ref[1:2, :] = jnp.broadcast_to(jnp.sum(sm_ref[1:2, :], axis=-1, keepdims=True), (1, D))

    tile = pl.BlockSpec((tm, D), lambda i: (i, 0))
    return pl.pallas_call(
        body, name="head_loss", grid=(ni,),
        in_specs=[tile, tile, pl.BlockSpec((1, D), lambda i: (0, 0)), pl.BlockSpec((9, D), lambda i: (0, 0))],
        out_specs=[tile, tile, pl.BlockSpec((8, D), lambda i: (0, 0))],
        out_shape=[jax.ShapeDtypeStruct((T, D), F32), jax.ShapeDtypeStruct((T, D), MM),
                   jax.ShapeDtypeStruct((8, D), F32)],
        compiler_params=_cparams(1),
    )(x, target, gfin, mod)


def _mixin_fwd(h, w, carry):
    T = h.shape[0]
    tm = min(2048, T)
    ni = T // tm

    def body(h_ref, w_ref, p_ref, h_all):
        i = pl.program_id(1)

        @pl.when(pl.program_id(0) == 0)
        def _():
            h_all[i] = h_ref[...]

        p_ref[0] = _mm(h_all[i], w_ref[0])

    first = lambda k, i: (jnp.where(k == 0, i, ni - 1), 0)
    out = _gridded(
        body, carry, name="mixin_fwd", grid=(8, ni),
        in_specs=[pl.BlockSpec((tm, D), first), pl.BlockSpec((1, D, D), lambda k, i: (k, 0, 0))],
        out_specs=[pl.BlockSpec((1, tm, D), lambda k, i: (k, i, 0))],
        out_shape=[jax.ShapeDtypeStruct((8, T, D), F32)],
        scratch_shapes=[pltpu.VMEM((ni, tm, D), ACT)],
    )(h, w)
    return out[:1], out[1:]


def _mixin_bwd(x, h, dxo, dp, mod, mo, gnorm, w, next_gate, next_res, carry):
    T = x.shape[0]
    tm = min(512, T)
    ni = T // tm

    def body(x_ref, h_ref, dxo_ref, dp_ref, mod_ref, g_ref, w_ref, dx_ref, dw_ref, sm_ref, df_ref, dh_scr, acc):
        k = pl.program_id(0)
        i = pl.program_id(1)

        @pl.when(i == 0)
        def _():
            acc[...] = jnp.zeros_like(acc)

        @pl.when(k == 0)
        def _():
            dh_scr[i] = jnp.zeros((tm, D), F32)

        @pl.when((k == 0) & (i == 0))
        def _():
            sm_ref[...] = jnp.zeros_like(sm_ref)

        dpk = dp_ref[0].astype(MM)
        acc[...] += _mm_tn(h_ref[...], dpk)
        dh_scr[i] += _mm_nt(dpk, w_ref[0])

        @pl.when(i == ni - 1)
        def _():
            dw_ref[0] = acc[...].astype(MM)

        @pl.when(k == 7)
        def _():
            sc = mod_ref[mo + 1:mo + 2, :]
            _, xh, n, r = _modnorm_fwd(x_ref[...], g_ref[...], mod_ref[mo:mo + 1, :], sc)
            dxn, dsh, dsc, dg = _modnorm_bwd(dh_scr[i], xh, n, r, g_ref[...], sc)
            dx = dxo_ref[...] + dxn
            dx_ref[...] = dx
            df_ref[...] = (next_res * mod_ref[next_gate:next_gate + 1, :] * dx).astype(MM)
            sm_ref[0:1, :] += dsh
            sm_ref[1:2, :] += dsc
            sm_ref[3:4, :] += dg

    last = pl.BlockSpec((tm, D), lambda k, i: (jnp.where(k == 7, i, 0), 0))
    out = _gridded(
        body, carry, name="mixin_bwd", grid=(8, ni),
        in_specs=[pl.BlockSpec((tm, D), lambda k, i: (jnp.where(k == 7, i, 0), 0)),
                  pl.BlockSpec((tm, D), lambda k, i: (i, 0)),
                  pl.BlockSpec((tm, D), lambda k, i: (jnp.where(k == 7, i, 0), 0)),
                  pl.BlockSpec((1, tm, D), lambda k, i: (k, i, 0)), pl.BlockSpec((9, D), lambda k, i: (0, 0)),
                  pl.BlockSpec((1, D), lambda k, i: (0, 0)), pl.BlockSpec((1, D, D), lambda k, i: (k, 0, 0))],
        out_specs=[last, pl.BlockSpec((1, D, D), lambda k, i: (k, 0, 0)), pl.BlockSpec((8, D), lambda k, i: (0, 0)),
                   last],
        out_shape=[jax.ShapeDtypeStruct((T, D), F32), jax.ShapeDtypeStruct((8, D, D), MM),
                   jax.ShapeDtypeStruct((8, D), F32), jax.ShapeDtypeStruct((T, D), MM)],
        scratch_shapes=[pltpu.VMEM((ni, tm, D), F32), pltpu.VMEM((D, D), F32)],
    )(x, h, dxo, dp, mod, gnorm, w)
    return out[:4], out[4:]


def _hgrn_consts():
    rows = jnp.arange(SUB * HD) // HD
    e = (rows[:, None] == jnp.arange(HD)[None, :]).astype(MM)
    return e, e.T


def _rows_bcast(ref, cb, first, n):
    parts = [jnp.broadcast_to(ref[pl.ds(c * CHUNK + first, 1), :], (n, HD)) for c in range(cb // CHUNK)]
    return jnp.concatenate(parts, axis=0)


def _hgrn_pre(qr, fr, lb_ref, b_scr, cb):
    z = lb_ref[...]
    lb = _sig(z[0:1, :] - z[1:2, :])
    sq = _sig(qr)
    q = qr * sq * Q_SCALE
    sf = _sig(fr)
    fg = lb + (1.0 - lb) * sf
    lf = jnp.log(fg)
    k = 1.0 - fg
    tl = lax.broadcasted_iota(jnp.int32, (cb, HD), 0) % CHUNK
    bc = lf
    sh = 1
    while sh < CHUNK:
        bc = bc + jnp.where(tl >= sh, pltpu.roll(bc, sh, 0), 0.0)
        sh *= 2
    b_scr[...] = bc
    bl = _rows_bcast(b_scr, cb, CHUNK - 1, CHUNK)
    eb = jnp.exp(bc)
    ekd = jnp.exp(bl - bc)
    ekf = jnp.exp(jnp.minimum(-bc, SAFE_EXP))
    return dict(lb=lb, sq=sq, q=q, sf=sf, fg=fg, k=k, tl=tl, b=bc, bl=bl, eb=eb, ekd=ekd, ekf=ekf,
                qe=q * eb, kd=k * ekd, kf=k * ekf, safe=jnp.max(-bc) < SAFE_EXP)


def _hgrn_pre_fused(p_ref, lb_ref, b_scr, q_scr, k_scr, qe_scr, kf_scr, kd_scr, cb):
    z = lb_ref[...]
    lb = _sig(z[0:1, :] - z[1:2, :])
    tl = lax.broadcasted_iota(jnp.int32, (CHUNK, HD), 0)

    def chunk(c, worst):
        r0 = pl.multiple_of(c * CHUNK, CHUNK)
        rs = pl.ds(r0, CHUNK)
        qr = p_ref[0, rs, :]
        q = qr * _sig(qr) * Q_SCALE
        fg = lb + (1.0 - lb) * _sig(p_ref[1, rs, :])
        k = 1.0 - fg
        bc = jnp.log(fg)
        sh = 1
        while sh < CHUNK:
            bc = bc + jnp.where(tl >= sh, pltpu.roll(bc, sh, 0), 0.0)
            sh *= 2
        b_scr[rs, :] = bc
        q_scr[rs, :] = q
        k_scr[rs, :] = k
        qe_scr[rs, :] = (q * jnp.exp(bc)).astype(MM)
        kf_scr[rs, :] = (k * jnp.exp(jnp.minimum(-bc, SAFE_EXP))).astype(MM)
        kd_scr[rs, :] = (k * jnp.exp(b_scr[pl.ds(r0 + CHUNK - 1, 1), :] - bc)).astype(MM)
        return jnp.maximum(worst, -bc)

    worst = lax.fori_loop(0, cb // CHUNK, chunk, jnp.zeros((CHUNK, HD), F32))
    return jnp.max(worst) < SAFE_EXP


def _hgrn_sub(pre, b_scr, cb):
    bc, tl, q, k = pre["b"], pre["tl"], pre["q"], pre["k"]
    br = [None] + [_rows_bcast(b_scr, cb, SUB * i - 1, CHUNK) for i in range(1, NSUB)]
    sb = tl // SUB
    bref = jnp.where(sb == 0, bc, jnp.where(sb == 1, br[1], jnp.where(sb == 2, br[2], br[3])))
    eqo = jnp.exp(bc - bref)
    eko = [None] + [jnp.exp(jnp.where(tl < SUB * i, br[i] - bc, NEG)) for i in range(1, NSUB)]
    return dict(eqo=eqo, eko=eko, qo=q * eqo, ko=[None] + [k * eko[i] for i in range(1, NSUB)])


def _pad_rows(x):
    return jnp.concatenate([x, jnp.zeros_like(x)], axis=0)


def _by_subblock(sbc, parts):
    out = jnp.zeros_like(parts[1])
    for i in range(1, NSUB):
        out = jnp.where(sbc == i, parts[i], out)
    return out


def _hgrn_fwd(p, hgrn_lb, hgrn_g, carry):
    T = p.shape[1]
    cb = min(HGRN_BLOCK, T)
    nch = cb // CHUNK
    ncb = T // cb
    e_mat, _ = _hgrn_consts()

    def body(p_ref, lb_ref, g_ref, e_ref, o_ref, oa_ref, a_ref, s_ref, st_scr, q_scr, k_scr, b_scr, z_scr, ad_scr,
             qe_scr, kf_scr, kd_scr):
        @pl.when(pl.program_id(1) == 0)
        def _():
            st_scr[...] = jnp.zeros_like(st_scr)

        safe = _hgrn_pre_fused(p_ref, lb_ref, b_scr, q_scr, k_scr, qe_scr, kf_scr, kd_scr, cb)
        chunks = [slice(c * CHUNK, (c + 1) * CHUNK) for c in range(nch)]
        row_i = lax.broadcasted_iota(jnp.int32, (CHUNK, HD), 0)
        lane_i = lax.broadcasted_iota(jnp.int32, (CHUNK, HD), 1)
        sbc = row_i // SUB
        causal = lane_i <= row_i

        @pl.when(safe)
        def _():
            for rs in chunks:
                ad_scr[rs, :] = jnp.where(causal, _mm_nt(qe_scr[rs, :], _pad_rows(kf_scr[rs, :])), 0.0)

        @pl.when(jnp.logical_not(safe))
        def _():
            tl = lax.broadcasted_iota(jnp.int32, (cb, HD), 0) % CHUNK
            sub = _hgrn_sub(dict(b=b_scr[...], tl=tl, q=q_scr[...], k=k_scr[...]), b_scr, cb)
            ti = lax.broadcasted_iota(jnp.int32, (SUB, HD), 0)

            def zbody(c, carry):
                for i in range(NSUB):
                    r0 = pl.multiple_of(c * CHUNK + SUB * i, SUB)
                    qi = q_scr[pl.ds(r0, SUB), :]
                    bi = b_scr[pl.ds(r0, SUB), :]
                    for s in range(SUB):
                        krow = k_scr[pl.ds(r0 + s, 1), :]
                        brow = b_scr[pl.ds(r0 + s, 1), :]
                        if s < 8:
                            zz = qi * krow * jnp.exp(jnp.where(ti >= s, bi - brow, NEG))
                        else:
                            lo = qi[8:] * krow * jnp.exp(jnp.where(ti[8:] >= s, bi[8:] - brow, NEG))
                            zz = jnp.concatenate([jnp.zeros((8, HD), F32), lo], axis=0)
                        z_scr[i, pl.ds(pl.multiple_of(c * SUB, SUB), SUB), s * HD:(s + 1) * HD] = zz.astype(MM)
                return carry

            lax.fori_loop(0, nch, zbody, 0)
            adiag = [_mm(z_scr[i], e_ref[...]) for i in range(NSUB)]
            offs = [[_mm_nt(sub["qo"][rs], _pad_rows(sub["ko"][i][rs])) for i in range(1, NSUB)] for rs in chunks]
            for c, rs in enumerate(chunks):
                dparts = []
                for i in range(NSUB):
                    blk = adiag[i][c * SUB:(c + 1) * SUB]
                    dparts.append(blk if i == 0 else pltpu.roll(blk, SUB * i, 1))
                ad_scr[rs, :] = _by_subblock(sbc, [None] + offs[c]) + jnp.concatenate(dparts, axis=0)

        kv = [_mm_tn(p_ref[2, rs, :], kd_scr[rs, :]) for rs in chunks]
        a_ref[0] = ad_scr[...]
        o_intra = [_mm(ad_scr[rs, :], _pad_rows(p_ref[2, rs, :])) for rs in chunks]
        states = []
        st = st_scr[...]
        for c in range(nch):
            states.append(st)
            st = st * jnp.exp(b_scr[pl.ds(c * CHUNK + CHUNK - 1, 1), :]) + kv[c]
        st_scr[...] = st
        g = g_ref[...]
        for c, rs in enumerate(chunks):
            s_ref[0, c] = states[c]
            o = o_intra[c] + _mm_nt(qe_scr[rs, :], states[c])
            o_ref[rs, :] = o
            og = p_ref[3, rs, :]
            oa_ref[rs, :] = (o * lax.rsqrt(_rowmean(o * o) + EPS) * g * og * _sig(og)).astype(ACT)

    out = _gridded(
        body, carry, name="hgrn_fwd", grid=(HEADS, ncb),
        in_specs=[pl.BlockSpec((4, cb, HD), lambda h, c: (0, c, h)),
                  pl.BlockSpec((2, HD), lambda h, c: (0, h)),
                  pl.BlockSpec((1, HD), lambda h, c: (0, h)),
                  pl.BlockSpec((SUB * HD, HD), lambda h, c: (0, 0))],
        out_specs=[pl.BlockSpec((cb, HD), lambda h, c: (c, h)),
                   pl.BlockSpec((cb, HD), lambda h, c: (c, h)),
                   pl.BlockSpec((1, cb, HD), lambda h, c: (h, c, 0)),
                   pl.BlockSpec((1, nch, HD, HD), lambda h, c: (h, c, 0, 0))],
        out_shape=[jax.ShapeDtypeStruct((T, D), F32), jax.ShapeDtypeStruct((T, D), ACT),
                   jax.ShapeDtypeStruct((HEADS, T, HD), F32),
                   jax.ShapeDtypeStruct((HEADS, T // CHUNK, HD, HD), F32)],
        scratch_shapes=[pltpu.VMEM((HD, HD), F32), pltpu.VMEM((cb, HD), F32), pltpu.VMEM((cb, HD), F32),
                        pltpu.VMEM((cb, HD), F32), pltpu.VMEM((NSUB, nch * SUB, SUB * HD), MM),
                        pltpu.VMEM((cb, HD), F32), pltpu.VMEM((cb, HD), MM), pltpu.VMEM((cb, HD), MM),
                        pltpu.VMEM((cb, HD), MM)],
    )(p, hgrn_lb, hgrn_g, e_mat)
    return out[:4], out[4:]


def _hgrn_bwd(p, o, a_all, s_all, doa, hgrn_lb, hgrn_g, dp, carry):
    T = p.shape[1]
    cb = min(HGRN_BLOCK, T)
    nch = cb // CHUNK
    ncb = T // cb
    _, et_mat = _hgrn_consts()

    def body(p_ref, o_ref, a_ref, s_ref, doa_ref, lb_ref, g_ref, et_ref, dp_in, dp_ref, sm_ref,
             dst_scr, q_scr, k_scr, b_scr, x_scr, dqd_scr, dkd_scr):
        del dp_in

        @pl.when(pl.program_id(1) == 0)
        def _():
            dst_scr[...] = jnp.zeros_like(dst_scr)
            sm_ref[...] = jnp.zeros_like(sm_ref)

        qr = p_ref[0]
        v = p_ref[2]
        og = p_ref[3]
        pre = _hgrn_pre(qr, p_ref[1], lb_ref, b_scr, cb)
        q, k = pre["q"], pre["k"]
        g = g_ref[...]
        ov = o_ref[...]
        r = lax.rsqrt(_rowmean(ov * ov) + EPS)
        oh = ov * r
        sgo = _sig(og)
        doa_v = doa_ref[...]
        don = doa_v * og * sgo
        dog = doa_v * oh * g * sgo * (1.0 + og * (1.0 - sgo))
        sm_ref[1:2, :] += _colsum(don * oh)
        doh = don * g
        do = r * (doh - oh * _rowmean(doh * oh))

        sbc = lax.broadcasted_iota(jnp.int32, (CHUNK, HD), 0) // SUB
        row_i = lax.broadcasted_iota(jnp.int32, (CHUNK, HD), 0)
        lane_i = lax.broadcasted_iota(jnp.int32, (CHUNK, HD), 1)
        causal = lane_i <= row_i
        chunks = [slice(c * CHUNK, (c + 1) * CHUNK) for c in range(nch)]
        da_parts = [jnp.where(causal, _mm_nt(do[rs], _pad_rows(v[rs])), 0.0) for rs in chunks]
        dv_parts = [_mm_tn(a_ref[0, rs, :], do[rs])[:CHUNK] for rs in chunks]

        @pl.when(pre["safe"])
        def _():
            hi = dict(preferred_element_type=F32, precision=lax.Precision.HIGH)
            for c, rs in enumerate(chunks):
                dqd_scr[rs, :] = pre["eb"][rs] * lax.dot_general(
                    da_parts[c], _pad_rows(pre["kf"][rs]), (((1,), (0,)), ((), ())), **hi)
                dkd_scr[rs, :] = pre["ekf"][rs] * lax.dot_general(
                    da_parts[c], pre["qe"][rs], (((0,), (0,)), ((), ())), **hi)[:CHUNK]

        @pl.when(jnp.logical_not(pre["safe"]))
        def _():
            sub = _hgrn_sub(pre, b_scr, cb)
            dqoff_mm = [[_mm(da_parts[c], _pad_rows(sub["ko"][i][rs])) for i in range(1, NSUB)]
                        for c, rs in enumerate(chunks)]
            dkoff_mm = [[_mm_tn(jnp.where(sbc == i, da_parts[c], 0.0), sub["qo"][rs])[:CHUNK]
                         for i in range(1, NSUB)] for c, rs in enumerate(chunks)]
            dqoff_parts = [_by_subblock(sbc, [None] + dqoff_mm[c]) for c in range(nch)]
            dkoff_parts = []
            for c, rs in enumerate(chunks):
                dko = sub["eko"][1][rs] * dkoff_mm[c][0]
                for i in range(2, NSUB):
                    dko = dko + sub["eko"][i][rs] * dkoff_mm[c][i - 1]
                dkoff_parts.append(dko)
            q_scr[...] = q
            k_scr[...] = k
            for i in range(NSUB):
                rows = []
                for c in range(nch):
                    blk = da_parts[c][SUB * i:SUB * (i + 1)]
                    rows.append(blk if i == 0 else pltpu.roll(blk, HD - SUB * i, 1))
                x_scr[i] = _mm(jnp.concatenate(rows, axis=0), et_ref[...])
            ti = lax.broadcasted_iota(jnp.int32, (SUB, HD), 0)

            def dbody(c, carry):
                for i in range(NSUB):
                    r0 = pl.multiple_of(c * CHUNK + SUB * i, SUB)
                    qi = q_scr[pl.ds(r0, SUB), :]
                    bi = b_scr[pl.ds(r0, SUB), :]
                    dq_hi = jnp.zeros((8, HD), F32)
                    dq_lo = jnp.zeros((8, HD), F32)
                    dk_hi = jnp.zeros((8, HD), F32)
                    dk_lo = jnp.zeros((8, HD), F32)
                    c0 = pl.multiple_of(c * SUB, SUB)
                    t8 = ti[:8]
                    for s in range(SUB):
                        krow = k_scr[pl.ds(r0 + s, 1), :]
                        brow = b_scr[pl.ds(r0 + s, 1), :]
                        w_lo = (x_scr[i, pl.ds(c0 + 8, 8), s * HD:(s + 1) * HD]
                                * jnp.exp(jnp.where(t8 + 8 >= s, bi[8:] - brow, NEG)))
                        dq_lo = dq_lo + w_lo * krow
                        col = _colsum(w_lo * qi[8:])
                        if s < 8:
                            w_hi = (x_scr[i, pl.ds(c0, 8), s * HD:(s + 1) * HD]
                                    * jnp.exp(jnp.where(t8 >= s, bi[:8] - brow, NEG)))
                            dq_hi = dq_hi + w_hi * krow
                            dk_hi = jnp.where(t8 == s, col + _colsum(w_hi * qi[:8]), dk_hi)
                        else:
                            dk_lo = jnp.where(t8 + 8 == s, col, dk_lo)
                    dqd_scr[pl.ds(r0, SUB), :] = jnp.concatenate([dq_hi, dq_lo], axis=0)
                    dkd_scr[pl.ds(r0, SUB), :] = jnp.concatenate([dk_hi, dk_lo], axis=0)
                return carry

            lax.fori_loop(0, nch, dbody, 0)
            dqd_scr[...] += jnp.concatenate(dqoff_parts, axis=0) * sub["eqo"]
            dkd_scr[...] += jnp.concatenate(dkoff_parts, axis=0)

        qdo = [_mm_tn(do[rs], pre["qe"][rs]) for rs in chunks]
        dsts = [None] * nch
        dst = dst_scr[...]
        for c in reversed(range(nch)):
            dsts[c] = dst
            dst = dst * jnp.exp(b_scr[pl.ds(c * CHUNK + CHUNK - 1, 1), :]) + qdo[c]
        dst_scr[...] = dst
        sts = [s_ref[0, c] for c in range(nch)]
        dqe_parts = [_mm(do[rs], sts[c]) for c, rs in enumerate(chunks)]
        dkdec_parts = [_mm(v[rs], dsts[c]) for c, rs in enumerate(chunks)]
        dvi_parts = [_mm_nt(pre["kd"][rs], dsts[c]) for c, rs in enumerate(chunks)]
        debl_parts = [_colsum(dsts[c] * sts[c]) for c in range(nch)]
        dqe = jnp.concatenate(dqe_parts, axis=0)
        dkdec = jnp.concatenate(dkdec_parts, axis=0)
        dq_tot = dqd_scr[...] + dqe * pre["eb"]
        dk_inter = dkdec * pre["ekd"]
        dk_tot = dkd_scr[...] + dk_inter
        db = q * dq_tot - k * dk_tot
        kdk = k * dk_inter
        dbl = jnp.concatenate(
            [jnp.broadcast_to(jnp.exp(b_scr[pl.ds(c * CHUNK + CHUNK - 1, 1), :]) * debl_parts[c]
                              + _colsum(kdk[c * CHUNK:(c + 1) * CHUNK]), (CHUNK, HD)) for c in range(nch)], axis=0)
        tl = pre["tl"]
        rc = db
        sh = 1
        while sh < CHUNK:
            rc = rc + jnp.where(tl + sh < CHUNK, pltpu.roll(rc, cb - sh, 0), 0.0)
            sh *= 2
        dlf = rc + dbl
        dfg = dlf / pre["fg"] - dk_tot
        sf = pre["sf"]
        lb = pre["lb"]
        sm_ref[0:1, :] += _colsum(dfg * (1.0 - sf))
        sq = pre["sq"]
        dp_ref[0] = (dq_tot * Q_SCALE * sq * (1.0 + qr * (1.0 - sq))).astype(ACT)
        dp_ref[1] = (dfg * (1.0 - lb) * sf * (1.0 - sf)).astype(ACT)
        dp_ref[2] = (jnp.concatenate(dv_parts, axis=0) + jnp.concatenate(dvi_parts, axis=0)).astype(ACT)
        dp_ref[3] = dog.astype(ACT)

    rev = lambda c: ncb - 1 - c
    out = _gridded(
        body, carry, name="hgrn_bwd", grid=(HEADS, ncb),
        in_specs=[pl.BlockSpec((4, cb, HD), lambda h, c: (0, rev(c), h)),
                  pl.BlockSpec((cb, HD), lambda h, c: (rev(c), h)),
                  pl.BlockSpec((1, cb, HD), lambda h, c: (h, rev(c), 0)),
                  pl.BlockSpec((1, nch, HD, HD), lambda h, c: (h, rev(c), 0, 0)),
                  pl.BlockSpec((cb, HD), lambda h, c: (rev(c), h)),
                  pl.BlockSpec((2, HD), lambda h, c: (0, h)),
                  pl.BlockSpec((1, HD), lambda h, c: (0, h)),
                  pl.BlockSpec((HD, SUB * HD), lambda h, c: (0, 0)),
                  pl.BlockSpec(memory_space=pl.ANY)],
        out_specs=[pl.BlockSpec((4, cb, HD), lambda h, c: (0, rev(c), h)),
                   pl.BlockSpec((8, HD), lambda h, c: (0, h))],
        out_shape=[jax.ShapeDtypeStruct(dp.shape, dp.dtype), jax.ShapeDtypeStruct((8, D), F32)],
        aliases={8: 0},
        scratch_shapes=[pltpu.VMEM((HD, HD), F32), pltpu.VMEM((cb, HD), F32), pltpu.VMEM((cb, HD), F32),
                        pltpu.VMEM((cb, HD), F32), pltpu.VMEM((NSUB, nch * SUB, SUB * HD), F32),
                        pltpu.VMEM((cb, HD), F32), pltpu.VMEM((cb, HD), F32)],
    )(p, o, a_all, s_all, doa, hgrn_lb, hgrn_g, et_mat, dp)
    return out[:2], out[2:]


def _ln_fwd(u1, g, b):
    mu = _rowmean(u1)
    xc = u1 - mu
    rs = lax.rsqrt(_rowmean(xc * xc) + EPS)
    xh = xc * rs
    return xh * g + b, xh, rs


CONV_RB = 64
LANES = 128


def _shift_rows(src, sh, ls, n):
    for r in range(1, 8):
        sh[r - 1, 0:n, :] = src[pl.ds(r, n), ls]


def _tap(src, sh, ls, off, r0, rows):
    r = off % 8
    if r == 0:
        return src[pl.ds(r0 + off, rows), ls]
    return sh[r - 1, pl.ds(r0 + off - r, rows), :]


def _conv_fwd(p, cw, cb_, lng, lnb, carry):
    T = p.shape[1]
    tm = min(512, T)
    n = HALO + tm - 8

    def body(p_ref, cw_ref, cb_ref, g_ref, b_ref, u1_ref, u2_ref, buf, sh):
        @pl.when(pl.program_id(0) == 0)
        def _():
            buf[0:HALO, :] = jnp.zeros((HALO, D), F32)

        buf[HALO:HALO + tm, :] = p_ref[0] * _sig(p_ref[1])
        for lb in range(D // LANES):
            ls = slice(lb * LANES, (lb + 1) * LANES)
            _shift_rows(buf, sh, ls, n)
            taps = [cw_ref[j:j + 1, ls] for j in range(CONV_K)]
            bias = cb_ref[:, ls]

            def rows_body(rb, carry):
                r0 = pl.multiple_of(rb * CONV_RB, CONV_RB)
                acc = jnp.broadcast_to(bias, (CONV_RB, LANES))
                for j in range(CONV_K):
                    acc = acc + taps[j] * _tap(buf, sh, ls, HALO - (CONV_K - 1) + j, r0, CONV_RB)
                u1_ref[pl.ds(r0, CONV_RB), ls] = acc
                return carry

            lax.fori_loop(0, tm // CONV_RB, rows_body, 0)
        y, _, _ = _ln_fwd(u1_ref[...], g_ref[...], b_ref[...])
        u2_ref[...] = (y * _sig(y)).astype(ACT)
        buf[0:HALO, :] = buf[tm:tm + HALO, :]

    out = _gridded(
        body, carry, name="conv_fwd", grid=(T // tm,),
        in_specs=[pl.BlockSpec((2, tm, D), lambda i: (2, i, 0)), pl.BlockSpec((HALO, D), lambda i: (0, 0)),
                  pl.BlockSpec((1, D), lambda i: (0, 0)), pl.BlockSpec((1, D), lambda i: (0, 0)),
                  pl.BlockSpec((1, D), lambda i: (0, 0))],
        out_specs=[pl.BlockSpec((tm, D), lambda i: (i, 0)), pl.BlockSpec((tm, D), lambda i: (i, 0))],
        out_shape=[jax.ShapeDtypeStruct((T, D), F32), jax.ShapeDtypeStruct((T, D), ACT)],
        scratch_shapes=[pltpu.VMEM((HALO + tm, D), F32), pltpu.VMEM((7, n, LANES), F32)],
    )(p, cw, cb_, lng, lnb)
    return out[:2], out[2:]


def _conv_bwd(p, u1, du2, cw, lng, lnb, dp, carry):
    T = p.shape[1]
    tm = min(512, T)
    ni = T // tm
    hb = tm // HALO

    n = HALO + tm - 8

    def body(p_ref, ph_ref, u1_ref, du2_ref, cw_ref, g_ref, b_ref, dp_in, dp_ref, dcw_ref, sm_ref, ubuf, dbuf,
             sh, dacc):
        del dp_in
        step = pl.program_id(0)

        @pl.when(step == 0)
        def _():
            dbuf[tm:tm + HALO, :] = jnp.zeros((HALO, D), F32)
            dcw_ref[...] = jnp.zeros_like(dcw_ref)
            sm_ref[...] = jnp.zeros_like(sm_ref)

        ua = p_ref[0]
        sgb = _sig(p_ref[1])
        halo = ph_ref[0] * _sig(ph_ref[1])
        ubuf[0:HALO, :] = jnp.where(step == ni - 1, 0.0, halo)
        ubuf[HALO:HALO + tm, :] = ua * sgb
        g = g_ref[...]
        y, xh, rs = _ln_fwd(u1_ref[...], g, b_ref[...])
        sy = _sig(y)
        dy = du2_ref[...] * sy * (1.0 + y * (1.0 - sy))
        sm_ref[1:2, :] += _colsum(dy * xh)
        sm_ref[2:3, :] += _colsum(dy)
        dxh = dy * g
        du1 = rs * (dxh - _rowmean(dxh) - xh * _rowmean(dxh * xh))
        sm_ref[0:1, :] += _colsum(du1)
        dbuf[0:tm, :] = du1
        for lb in range(D // LANES):
            ls = slice(lb * LANES, (lb + 1) * LANES)
            taps = [cw_ref[j:j + 1, ls] for j in range(CONV_K)]
            _shift_rows(dbuf, sh, ls, n)

            def du0_body(rb, carry):
                r0 = pl.multiple_of(rb * CONV_RB, CONV_RB)
                acc = jnp.zeros((CONV_RB, LANES), F32)
                for j in range(CONV_K):
                    acc = acc + taps[j] * _tap(dbuf, sh, ls, CONV_K - 1 - j, r0, CONV_RB)
                dp_ref[0, pl.ds(r0, CONV_RB), ls] = acc.astype(ACT)
                return carry

            lax.fori_loop(0, tm // CONV_RB, du0_body, 0)
            _shift_rows(ubuf, sh, ls, n)
            dacc[...] = jnp.zeros_like(dacc)

            def dcw_body(rb, carry):
                r0 = pl.multiple_of(rb * CONV_RB, CONV_RB)
                d = dbuf[pl.ds(r0, CONV_RB), ls]
                for j in range(CONV_K):
                    prod = d * _tap(ubuf, sh, ls, HALO - (CONV_K - 1) + j, r0, CONV_RB)
                    dacc[8 * j:8 * j + 8, :] += jnp.sum(prod.reshape(CONV_RB // 8, 8, LANES), axis=0)
                return carry

            lax.fori_loop(0, tm // CONV_RB, dcw_body, 0)
            for j in range(CONV_K):
                dcw_ref[j:j + 1, ls] += _colsum(dacc[8 * j:8 * j + 8, :])
        du0 = dp_ref[0].astype(F32)
        dp_ref[0] = (du0 * sgb).astype(ACT)
        dp_ref[1] = (du0 * ua * sgb * (1.0 - sgb)).astype(ACT)
        dbuf[tm:tm + HALO, :] = dbuf[0:HALO, :]

    rev = lambda i: ni - 1 - i
    out = _gridded(
        body, carry, name="conv_bwd", grid=(ni,),
        in_specs=[pl.BlockSpec((2, tm, D), lambda i: (2, rev(i), 0)),
                  pl.BlockSpec((2, HALO, D), lambda i: (2, jnp.maximum(rev(i) * hb - 1, 0), 0)),
                  pl.BlockSpec((tm, D), lambda i: (rev(i), 0)), pl.BlockSpec((tm, D), lambda i: (rev(i), 0)),
                  pl.BlockSpec((HALO, D), lambda i: (0, 0)), pl.BlockSpec((1, D), lambda i: (0, 0)),
                  pl.BlockSpec((1, D), lambda i: (0, 0)), pl.BlockSpec(memory_space=pl.ANY)],
        out_specs=[pl.BlockSpec((2, tm, D), lambda i: (2, rev(i), 0)),
                   pl.BlockSpec((HALO, D), lambda i: (0, 0)), pl.BlockSpec((8, D), lambda i: (0, 0))],
        out_shape=[jax.ShapeDtypeStruct(dp.shape, dp.dtype), jax.ShapeDtypeStruct((HALO, D), F32),
                   jax.ShapeDtypeStruct((8, D), F32)],
        aliases={7: 0},
        scratch_shapes=[pltpu.VMEM((HALO + tm, D), F32), pltpu.VMEM((tm + HALO, D), F32),
                        pltpu.VMEM((7, n, LANES), F32), pltpu.VMEM((8 * CONV_K, LANES), F32)],
    )(p, p, u1, du2, cw, lng, lnb, dp)
    return out[:3], out[3:]


def _mixout_fwd(x, oa, u2, p, mod, mo, w_a, w_b, w_o):
    T = x.shape[0]
    tm = min(512, T)

    def body(x_ref, oa_ref, u2_ref, p_ref, mod_ref, wa_ref, wb_ref, wo_ref, xo_ref, ya_ref, yb_ref, mo_ref):
        ya = _mm(oa_ref[...], wa_ref[...])
        yb = _mm(u2_ref[...], wb_ref[...])
        ya_ref[...] = ya.astype(ACT)
        yb_ref[...] = yb.astype(ACT)
        merged = _sig(p_ref[0]) * ya + _sig(p_ref[1]) * yb
        out = _mm(merged, wo_ref[...])
        mo_ref[...] = out
        xo_ref[...] = x_ref[...] + mod_ref[mo + 2:mo + 3, :] * out

    tile = pl.BlockSpec((tm, D), lambda i: (i, 0))
    wspec = pl.BlockSpec((D, D), lambda i: (0, 0))
    return pl.pallas_call(
        body, name="mixout_fwd", grid=(T // tm,),
        in_specs=[tile, tile, tile, pl.BlockSpec((2, tm, D), lambda i: (3, i, 0)),
                  pl.BlockSpec((9, D), lambda i: (0, 0)), wspec, wspec, wspec],
        out_specs=[tile, tile, tile, tile],
        out_shape=[jax.ShapeDtypeStruct((T, D), F32), jax.ShapeDtypeStruct((T, D), ACT),
                   jax.ShapeDtypeStruct((T, D), ACT), jax.ShapeDtypeStruct((T, D), F32)],
        compiler_params=_cparams(1),
    )(x, oa, u2, p, mod, w_a, w_b, w_o)


def _mixout_bwd(dxo, oa, u2, ya, yb, mout, p, mod, mo, w_a, w_b, w_o):
    T = dxo.shape[0]
    tm = min(256, T)

    def body(dxo_ref, oa_ref, u2_ref, ya_ref, yb_ref, mo_ref, p_ref, mod_ref, wa_ref, wb_ref, wo_ref,
             dp_ref, doa_ref, du2_ref, dwa_ref, dwb_ref, dwo_ref, sm_ref):
        @pl.when(pl.program_id(0) == 0)
        def _():
            dwa_ref[...] = jnp.zeros_like(dwa_ref)
            dwb_ref[...] = jnp.zeros_like(dwb_ref)
            dwo_ref[...] = jnp.zeros_like(dwo_ref)
            sm_ref[...] = jnp.zeros_like(sm_ref)

        dxo_v = dxo_ref[...]
        sm_ref[2:3, :] += _colsum(dxo_v * mo_ref[...])
        dmo = (mod_ref[mo + 2:mo + 3, :] * dxo_v).astype(MM)
        ya = ya_ref[...].astype(F32)
        yb = yb_ref[...].astype(F32)
        sga = _sig(p_ref[0])
        sgb = _sig(p_ref[1])
        merged = (sga * ya + sgb * yb).astype(MM)
        dwo_ref[...] += _mm_tn(merged, dmo)
        dmg = _mm_nt(dmo, wo_ref[...])
        dp_ref[0] = (dmg * ya * sga * (1.0 - sga)).astype(ACT)
        dp_ref[1] = (dmg * yb * sgb * (1.0 - sgb)).astype(ACT)
        dya = (dmg * sga).astype(MM)
        dyb = (dmg * sgb).astype(MM)
        dwa_ref[...] += _mm_tn(oa_ref[...], dya)
        dwb_ref[...] += _mm_tn(u2_ref[...], dyb)
        doa_ref[...] = _mm_nt(dya, wa_ref[...])
        du2_ref[...] = _mm_nt(dyb, wb_ref[...])

    tile = pl.BlockSpec((tm, D), lambda i: (i, 0))
    wspec = pl.BlockSpec((D, D), lambda i: (0, 0))
    return pl.pallas_call(
        body, name="mixout_bwd", grid=(T // tm,),
        in_specs=[tile, tile, tile, tile, tile, tile, pl.BlockSpec((2, tm, D), lambda i: (3, i, 0)),
                  pl.BlockSpec((9, D), lambda i: (0, 0)), wspec, wspec, wspec],
        out_specs=[pl.BlockSpec((2, tm, D), lambda i: (3, i, 0)), tile, tile, wspec, wspec, wspec,
                   pl.BlockSpec((8, D), lambda i: (0, 0))],
        out_shape=[jax.ShapeDtypeStruct((8, T, D), ACT), jax.ShapeDtypeStruct((T, D), F32),
                   jax.ShapeDtypeStruct((T, D), F32), jax.ShapeDtypeStruct((D, D), F32),
                   jax.ShapeDtypeStruct((D, D), F32), jax.ShapeDtypeStruct((D, D), F32),
                   jax.ShapeDtypeStruct((8, D), F32)],
        compiler_params=_cparams(1),
    )(dxo, oa, u2, ya, yb, mout, p, mod, w_a, w_b, w_o)


def _adamw_ada_w(w, m, v, cs_all, dmod_cols):
    R, C = w.shape
    tr = 256
    cs_t = jnp.pad(cs_all.T, ((0, 0), (0, HD - N_DEV)))
    dm = jnp.pad(dmod_cols, ((0, HD - N_DEV), (0, 0)))

    def body(w_ref, m_ref, v_ref, cs_ref, d_ref, go_ref, do_ref, mo_ref, vo_ref):
        gv = jnp.dot(cs_ref[...], d_ref[...], preferred_element_type=F32, precision=lax.Precision.HIGHEST)
        go_ref[...] = gv
        do_ref[...], mo_ref[...], vo_ref[...] = _adam_math(w_ref[...], gv, m_ref[...], v_ref[...])

    tile = pl.BlockSpec((tr, C), lambda i: (i, 0))
    sds = jax.ShapeDtypeStruct((R, C), F32)
    return pl.pallas_call(
        body, name="adamw_ada_w", grid=(R // tr,),
        in_specs=[tile, tile, tile, pl.BlockSpec((tr, HD), lambda i: (i, 0)), pl.BlockSpec((HD, C), lambda i: (0, 0))],
        out_specs=[tile] * 4, out_shape=[sds] * 4, compiler_params=_cparams(1))(w, m, v, cs_t, dm)


def _adam_math(w, g, m, v):
    m2 = ADAM_B1 * m + (1.0 - ADAM_B1) * g
    v2 = ADAM_B2 * v + (1.0 - ADAM_B2) * (g * g)
    m_hat = m2 / (1.0 - ADAM_B1 ** ADAM_STEP)
    v_hat = v2 / (1.0 - ADAM_B2 ** ADAM_STEP)
    delta = -ADAM_LR * (m_hat / (jnp.sqrt(v_hat) + ADAM_EPS) + ADAM_WD * w)
    return delta, m2, v2


def _adamw(w, m, v, g, name):
    R, C = w.shape
    slots = g.ndim == 3
    n_slots = g.shape[0] if slots else 0
    tr = R
    for cand in (256, 176):
        if R % cand == 0 and R > cand:
            tr = cand
            break

    def body(w_ref, m_ref, v_ref, g_ref, go_ref, d_ref, mo_ref, vo_ref):
        if slots:
            gv = g_ref[0].astype(F32)
            for s in range(1, n_slots):
                gv = gv + g_ref[s].astype(F32)
        else:
            gv = g_ref[...]
        go_ref[...] = gv
        d_ref[...], mo_ref[...], vo_ref[...] = _adam_math(w_ref[...], gv, m_ref[...], v_ref[...])

    tile = pl.BlockSpec((tr, C), lambda i: (i, 0))
    gspec = pl.BlockSpec((n_slots, tr, C), lambda i: (0, i, 0)) if slots else tile
    sds = jax.ShapeDtypeStruct((R, C), F32)
    return pl.pallas_call(
        body, name=name, grid=(R // tr,), in_specs=[tile, tile, tile, gspec], out_specs=[tile] * 4,
        out_shape=[sds] * 4, compiler_params=_cparams(1),
    )(w, m, v, g)


def _adamw_small(tot, names, params, grad_rows):
    k = len(names)

    def body(tot_ref, *refs):
        ins, outs = refs[:3 * k], refs[3 * k:]
        for i, n in enumerate(names):
            w_ref, m_ref, v_ref = ins[3 * i:3 * i + 3]
            go, do, mo, vo = outs[4 * i:4 * i + 4]
            row = grad_rows[n]
            for j in range(w_ref.shape[1] // D):
                ls = slice(j * D, (j + 1) * D)
                g = tot_ref[row + j:row + j + 1, :]
                w = w_ref[:, ls]
                if n == "hgrn_lb":
                    p0 = _sig(w[0:1] - w[1:2])
                    dz0 = p0 * (1.0 - p0) * g
                    g = jnp.concatenate([dz0, -dz0], axis=0)
                go[:, ls] = g
                do[:, ls], mo[:, ls], vo[:, ls] = _adam_math(w, g, m_ref[:, ls], v_ref[:, ls])

    flat = [t for n in names for t in params[n]]
    out_shape = [jax.ShapeDtypeStruct(params[n][0].shape, F32) for n in names for _ in range(4)]
    outs = pl.pallas_call(body, name="adamw_small", out_shape=out_shape)(tot, *flat)
    return {n: tuple(outs[4 * i:4 * i + 4]) for i, n in enumerate(names)}


def _sum_slots(pack, name, tr):
    n, R, C = pack.shape

    def body(p_ref, out_ref):
        acc = p_ref[0].astype(F32)
        for s in range(1, n):
            acc = acc + p_ref[s].astype(F32)
        out_ref[...] = acc

    return pl.pallas_call(
        body, name=name, grid=(R // tr,), in_specs=[pl.BlockSpec((n, tr, C), lambda i: (0, i, 0))],
        out_specs=pl.BlockSpec((tr, C), lambda i: (i, 0)), out_shape=jax.ShapeDtypeStruct((R, C), F32),
        compiler_params=_cparams(1))(pack)


def _me():
    return lax.axis_index("x"), lax.axis_index("y"), lax.axis_index("c")


def _peer(r):
    x, y, c = _me()
    px = 1 - x if r & 4 else x
    py = 1 - y if r & 2 else y
    pc = 1 - c if r & 1 else c
    return (px, py, pc), 4 * px + 2 * py + pc


def _small_gather(x_ref, out_ref, send_sems, recv_sems):
    R = x_ref.shape[0]
    mx, my, mc = _me()
    me = 4 * mx + 2 * my + mc
    mine = out_ref.at[pl.ds(pl.multiple_of(me * R, 8), R), :]
    copies = []
    for r in range(1, N_DEV):
        dev, _ = _peer(r)
        copies.append(pltpu.make_async_remote_copy(
            src_ref=x_ref, dst_ref=mine, send_sem=send_sems.at[r - 1], recv_sem=recv_sems.at[r - 1],
            device_id=dev, device_id_type=MESH))
    for cp in copies:
        cp.start()
    mine[...] = x_ref[...]
    for r in range(1, N_DEV):
        dev, idx = _peer(r)
        theirs = out_ref.at[pl.ds(pl.multiple_of(idx * R, 8), R), :]
        pltpu.make_async_remote_copy(
            src_ref=x_ref, dst_ref=theirs, send_sem=send_sems.at[r - 1], recv_sem=recv_sems.at[r - 1],
            device_id=dev, device_id_type=MESH).wait_recv()
    for cp in copies:
        cp.wait_send()


def _prologue(cs, ada_w, ada_b_cols, big):
    n = len(big)
    ncol = ada_w.shape[1]
    big_shape, big_sems = _xchg_specs(big, "gather")

    def body(cs_ref, w_ref, b_ref, *rest):
        big_in, cs_all, mod_all, big_out = rest[:n], rest[n], rest[n + 1], rest[n + 2:2 * n + 2]
        mod_scr, s1, r1, s2, r2 = rest[2 * n + 2:2 * n + 7]
        sems = rest[2 * n + 7:]
        _small_gather(cs_ref, cs_all, s1, r1)
        pick = (lax.broadcasted_iota(jnp.int32, (N_DEV, N_DEV * 8), 1)
                == 8 * lax.broadcasted_iota(jnp.int32, (N_DEV, N_DEV * 8), 0)).astype(F32)
        per_device = jnp.dot(pick, cs_all[...], preferred_element_type=F32, precision=lax.Precision.HIGHEST)
        mod_scr[...] = jnp.dot(per_device, w_ref[...], preferred_element_type=F32,
                               precision=lax.Precision.HIGHEST) + b_ref[...]
        _small_gather(mod_scr, mod_all, s2, r2)
        _xchg_start(big_in, big_out, sems, "gather")
        _xchg_wait(big_in, big_out, sems, "gather")

    vmem = pl.BlockSpec(memory_space=pltpu.VMEM)
    hbm = pl.BlockSpec(memory_space=pl.ANY)
    dma7 = pltpu.SemaphoreType.DMA((N_DEV - 1,))
    out = pl.pallas_call(
        body, name="prologue",
        out_shape=[jax.ShapeDtypeStruct((N_DEV * 8, D), F32), jax.ShapeDtypeStruct((N_DEV * 8, ncol), F32)]
        + big_shape,
        in_specs=[vmem, vmem, vmem] + [hbm] * n, out_specs=[vmem, vmem] + [hbm] * n,
        scratch_shapes=[pltpu.VMEM((8, ncol), F32), dma7, dma7, dma7, dma7] + big_sems,
        compiler_params=pltpu.CompilerParams(vmem_limit_bytes=VMEM_LIMIT),
    )(cs, ada_w, ada_b_cols, *big)
    return out[0], out[1], out[2:]


def _allgather_small(x):
    R, C = x.shape

    def body(x_ref, out_ref, send_sems, recv_sems):
        _small_gather(x_ref, out_ref, send_sems, recv_sems)

    return pl.pallas_call(
        body, name="allgather_small_%dx%d" % (R, C),
        out_shape=jax.ShapeDtypeStruct((N_DEV * R, C), F32),
        in_specs=[pl.BlockSpec(memory_space=pltpu.VMEM)], out_specs=pl.BlockSpec(memory_space=pltpu.VMEM),
        scratch_shapes=[pltpu.SemaphoreType.DMA((N_DEV - 1,)), pltpu.SemaphoreType.DMA((N_DEV - 1,))],
    )(x)


N_CHIP = N_DEV // 2


def _xchg_copies(ins, outs, sems, mode):
    send_sems, recv_sems, local_sems = sems
    mx, my, mc = _me()
    me = 4 * mx + 2 * my + mc
    my_chip = 2 * mx + my
    sibling = _peer(1)[0]

    def rdma(a, r, dev, src, slot):
        k = a * (N_DEV - 1) + r - 1
        return pltpu.make_async_remote_copy(
            src_ref=src, dst_ref=outs[a].at[slot], send_sem=send_sems.at[k], recv_sem=recv_sems.at[k],
            device_id=dev, device_id_type=MESH)

    own, sends, relays, recvs = [], [], [], []
    for a in range(len(ins)):
        if mode == "pair":
            for chip in range(N_CHIP):
                src = ins[a].at[2 * chip + 1 - mc]
                sends.append(rdma(a, chip + 1, sibling, src, chip))
                recvs.append(rdma(a, chip + 1, sibling, src, chip))
            continue
        if mode == "quad":
            own.append(pltpu.make_async_copy(ins[a].at[my_chip], outs[a].at[my_chip], local_sems.at[a]))
            for r in (2, 4, 6):
                dev, idx = _peer(r)
                chip = idx // 2
                sends.append(rdma(a, r, dev, ins[a].at[chip], my_chip))
                recvs.append(rdma(a, r, dev, ins[a].at[chip], chip))
            continue
        gather = mode == "gather"
        own.append(pltpu.make_async_copy(ins[a] if gather else ins[a].at[me], outs[a].at[me], local_sems.at[a]))
        for r in range(1, N_DEV):
            dev, idx = _peer(r)
            if not gather:
                sends.append(rdma(a, r, dev, ins[a].at[idx], me))
                recvs.append(rdma(a, r, dev, ins[a].at[idx], idx))
            elif r == 1:
                sends.append(rdma(a, r, dev, ins[a], me))
                recvs.append(rdma(a, r, dev, ins[a], idx))
            elif r % 2 == 0:
                sends.append(rdma(a, r, dev, ins[a], me))
                relays.append((rdma(a, r, dev, ins[a], idx), rdma(a, r + 1, sibling, outs[a].at[idx], idx)))
            else:
                recvs.append(rdma(a, r, sibling, ins[a], idx))
    return own, sends, relays, recvs


def _xchg_start(ins, outs, sems, mode):
    own, sends, _, _ = _xchg_copies(ins, outs, sems, mode)
    for cp in own + sends:
        cp.start()


def _xchg_wait(ins, outs, sems, mode):
    own, sends, relays, recvs = _xchg_copies(ins, outs, sems, mode)
    for arrival, relay in relays:
        arrival.wait_recv()
        relay.start()
    for cp in recvs:
        cp.wait_recv()
    for cp in own:
        cp.wait()
    for cp in sends + [relay for _, relay in relays]:
        cp.wait_send()


def _xchg_specs(arrays, mode):
    n = len(arrays)
    shape = {"gather": lambda s: (N_DEV,) + s, "scatter": lambda s: s, "pair": lambda s: (N_CHIP,) + s[1:],
             "quad": lambda s: s}[mode]
    out_shape = [jax.ShapeDtypeStruct(shape(a.shape), a.dtype) for a in arrays]
    sems = [pltpu.SemaphoreType.DMA((n * (N_DEV - 1),)), pltpu.SemaphoreType.DMA((n * (N_DEV - 1),)),
            pltpu.SemaphoreType.DMA((n,))]
    return out_shape, sems


def _exchange(arrays, mode, name):
    n = len(arrays)

    def body(*refs):
        _xchg_start(refs[:n], refs[n:2 * n], refs[2 * n:], mode)
        _xchg_wait(refs[:n], refs[n:2 * n], refs[2 * n:], mode)

    out_shape, sems = _xchg_specs(arrays, mode)
    return pl.pallas_call(
        body, name=name, out_shape=out_shape,
        in_specs=[pl.BlockSpec(memory_space=pl.ANY)] * n, out_specs=[pl.BlockSpec(memory_space=pl.ANY)] * n,
        scratch_shapes=sems,
    )(*arrays)


def _gridded(body, carry, *, name, grid, in_specs, out_specs, out_shape, scratch_shapes=(), aliases=None):
    if carry is None:
        return pl.pallas_call(
            body, name=name, grid=grid, in_specs=list(in_specs), out_specs=list(out_specs),
            out_shape=list(out_shape), scratch_shapes=list(scratch_shapes), input_output_aliases=aliases or {},
            compiler_params=_cparams(len(grid)))
    arrays, mode = carry
    n, n_in, n_out, n_scr = len(arrays), len(in_specs), len(out_specs), len(scratch_shapes)
    c_shape, c_sems = _xchg_specs(arrays, mode)

    def wrapped(*refs):
        ins, cin = refs[:n_in], refs[n_in:n_in + n]
        o0 = n_in + n
        outs, cout = refs[o0:o0 + n_out], refs[o0 + n_out:o0 + n_out + n]
        s0 = o0 + n_out + n
        scr, sems = refs[s0:s0 + n_scr], refs[s0 + n_scr:]
        first = pl.program_id(0) == 0
        last = pl.program_id(0) == grid[0] - 1
        for ax in range(1, len(grid)):
            first = first & (pl.program_id(ax) == 0)
            last = last & (pl.program_id(ax) == grid[ax] - 1)

        @pl.when(first)
        def _():
            _xchg_start(cin, cout, sems, mode)

        body(*ins, *outs, *scr)

        @pl.when(last)
        def _():
            _xchg_wait(cin, cout, sems, mode)

    hbm = pl.BlockSpec(memory_space=pl.ANY)
    res = pl.pallas_call(
        wrapped, name=name, grid=grid, in_specs=list(in_specs) + [hbm] * n, out_specs=list(out_specs) + [hbm] * n,
        out_shape=list(out_shape) + c_shape, scratch_shapes=list(scratch_shapes) + c_sems,
        input_output_aliases=aliases or {}, compiler_params=_cparams(len(grid)),
    )
    return lambda *args: res(*args, *arrays)


def _local_step(x, target, mod, small, sh, w1):
    w1_in, w1_out = w1[0].reshape(2, D_FF, D), w1[1].reshape(D_FF, D)
    (x1, a1, b1, f1, h1, h2), (wm_in,) = _ffn_fwd(x, mod, 0, small["norm_ffn1"], w1_in, w1_out, 0.5, "ffn1_fwd",
                                                  ([sh["mix_w_in"]], "gather"), nxt=(small["norm_mix"], 3))
    (p,), (wh_o, wc_o, wm_o, cw) = _mixin_fwd(
        h2, wm_in, ([sh["hgrn_w_o"], sh["conv_w_o"], sh["mix_w_out"], sh["conv_w"]], "gather"))
    wh_o, wc_o, wm_o = wh_o.reshape(D, D), wc_o.reshape(D, D), wm_o.reshape(D, D)
    cw = jnp.pad(cw.transpose(1, 0, 2).reshape(CONV_K, D), ((0, HALO - CONV_K), (0, 0)))
    (o, oa, a_all, s_all), (w2_in,) = _hgrn_fwd(p, small["hgrn_lb"], small["hgrn_g"], ([sh["ffn2_w_in"]], "gather"))
    (u1, u2), (w2_out,) = _conv_fwd(p, cw, small["conv_b"], small["conv_ln_g"], small["conv_ln_b"],
                                    ([sh["ffn2_w_out"]], "gather"))
    w2_in, w2_out = w2_in.reshape(2, D_FF, D), w2_out.reshape(D_FF, D)
    x2, ya, yb, mout = _mixout_fwd(x1, oa, u2, p, mod, 3, wh_o, wc_o, wm_o)
    (x3, a3, b3, f3, h3), _ = _ffn_fwd(x2, mod, 6, small["norm_ffn2"], w2_in, w2_out, 0.5, "ffn2_fwd", None)
    dx3, df3, sm_head = _head(x3, target, small["norm_final"], mod, 8, 0.5)

    (da3, db3, dw2_in, dw2_out), _ = _ffn_bwd_w(h3, df3, a3, b3, w2_out, "ffn2_bwd_w", None)
    rows = lambda t: t.reshape(N_DEV, -1, D).astype(MM)
    (dx2, sm3), (r2_out,) = _ffn_bwd_x(x2, dx3, f3, da3, db3, mod, 6, small["norm_ffn2"], w2_in, 0.5, "ffn2_bwd_x",
                                       ([rows(dw2_out)], "scatter"))
    dp, doa, du2, dwh_o, dwc_o, dwm_o, sm_mo = _mixout_bwd(dx2, oa, u2, ya, yb, mout, p, mod, 3, wh_o, wc_o, wm_o)
    (dp, dcw, sm_cv), (r2_in,) = _conv_bwd(p, u1, du2, cw, small["conv_ln_g"], small["conv_ln_b"], dp,
                                           ([rows(dw2_in)], "scatter"))
    (dp, sm_hg), _ = _hgrn_bwd(p, o, a_all, s_all, doa, small["hgrn_lb"], small["hgrn_g"], dp, None)
    (dx1, dwm_in, sm2, df1), (rh_o, rc_o, rm_o, rcw) = _mixin_bwd(
        x1, h2, dx2, dp, mod, 3, small["norm_mix"], wm_in, 2, 0.5,
        ([rows(dwh_o), rows(dwc_o), rows(dwm_o), dcw[:CONV_K].reshape(CONV_K, N_DEV, -1).transpose(1, 0, 2)],
         "scatter"))
    (da1, db1, dw1_in, dw1_out), (rm_in,) = _ffn_bwd_w(h1, df1, a1, b1, w1_out, "ffn1_bwd_w",
                                                      (_pair_reduce([dwm_in], "pair_mix"), "quad"))
    (dx0, sm1), (r1_in, r1_out) = _ffn_bwd_x(
        x, dx1, f1, da1, db1, mod, 0, small["norm_ffn1"], w1_in, 0.5, "ffn1_bwd_x",
        (_pair_reduce([rows(dw1_in), rows(dw1_out)], "pair_ffn1"), "quad"))

    dmod = jnp.concatenate([sm1[0:3], sm2[0:2], sm_mo[2:3], sm3[0:3]], axis=0)
    gsmall = dict(norm_ffn1=sm1[3:4], norm_mix=sm2[3:4], lb0=sm_hg[0:1], hgrn_g=sm_hg[1:2], conv_b=sm_cv[0:1],
                  conv_ln_g=sm_cv[1:2], conv_ln_b=sm_cv[2:3], norm_ffn2=sm3[3:4], norm_final=sm_head[0:1])
    recv = dict(ffn1_w_in=r1_in, ffn1_w_out=r1_out, mix_w_in=rm_in, hgrn_w_o=rh_o, conv_w=rcw, conv_w_o=rc_o,
                mix_w_out=rm_o, ffn2_w_in=r2_in, ffn2_w_out=r2_out)
    return sm_head[1, 0], dx0, dmod, gsmall, recv


def _pair_add(mine, theirs, core, name):
    _, R, C = theirs.shape

    def body(core_ref, a_ref, b_ref, out_ref):
        del core_ref
        out_ref[0] = (a_ref[0, 0].astype(F32) + b_ref[0].astype(F32)).astype(out_ref.dtype)

    blk = pl.BlockSpec((1, R, C), lambda s, core_ref: (s, 0, 0))
    grid_spec = pltpu.PrefetchScalarGridSpec(
        num_scalar_prefetch=1, grid=(N_CHIP,),
        in_specs=[pl.BlockSpec((1, 1, R, C), lambda s, core_ref: (s, core_ref[0], 0, 0)), blk], out_specs=blk)
    return pl.pallas_call(body, name=name, grid_spec=grid_spec,
                          out_shape=jax.ShapeDtypeStruct(theirs.shape, mine.dtype), compiler_params=_cparams(1),
                          )(core, mine.reshape(N_CHIP, 2, R, C), theirs)


def _pair_reduce(arrays, name):
    theirs = _exchange(arrays, "pair", name)
    core = lax.axis_index("c").astype(jnp.int32).reshape(1)
    return [_pair_add(a, t, core, "%s_add%d" % (name, i)) for i, (a, t) in enumerate(zip(arrays, theirs))]


SMALL_ORDER = ("norm_ffn1", "norm_mix", "lb0", "hgrn_g", "conv_b", "conv_ln_g", "conv_ln_b", "norm_ffn2",
               "norm_final")
PACK_ROWS = 24


def kernel(x, c, ada_w, ada_b, norm_ffn1, ffn1_w_in, ffn1_w_out, norm_mix, mix_w_in, hgrn_lb, hgrn_g, hgrn_w_o, conv_w, conv_b, conv_ln_g, conv_ln_b, conv_w_o, mix_w_out, norm_ffn2, ffn2_w_in, ffn2_w_out, norm_final, loss_target, m_ada_w, m_ada_b, m_norm_ffn1, m_ffn1_w_in, m_ffn1_w_out, m_norm_mix, m_mix_w_in, m_hgrn_lb, m_hgrn_g, m_hgrn_w_o, m_conv_w, m_conv_b, m_conv_ln_g, m_conv_ln_b, m_conv_w_o, m_mix_w_out, m_norm_ffn2, m_ffn2_w_in, m_ffn2_w_out, m_norm_final, v_ada_w, v_ada_b, v_norm_ffn1, v_ffn1_w_in, v_ffn1_w_out, v_norm_mix, v_mix_w_in, v_hgrn_lb, v_hgrn_g, v_hgrn_w_o, v_conv_w, v_conv_b, v_conv_ln_g, v_conv_ln_b, v_conv_w_o, v_mix_w_out, v_norm_ffn2, v_ffn2_w_in, v_ffn2_w_out, v_norm_final):
    mx, my, mc = _me()
    me = 4 * mx + 2 * my + mc
    ncol = ada_w.shape[2]

    sh = dict(ffn1_w_out=ffn1_w_out, mix_w_in=mix_w_in, hgrn_w_o=hgrn_w_o, conv_w_o=conv_w_o,
              mix_w_out=mix_w_out, ffn2_w_out=ffn2_w_out)
    sh = {n: w[0].astype(MM) for n, w in sh.items()}
    sh["ffn1_w_in"] = ffn1_w_in[0].T.astype(MM)
    sh["ffn2_w_in"] = ffn2_w_in[0].T.astype(MM)
    sh["conv_w"] = conv_w[0]
    small = dict(norm_ffn1=norm_ffn1, norm_mix=norm_mix, hgrn_lb=hgrn_lb, hgrn_g=hgrn_g, conv_b=conv_b,
                 conv_ln_g=conv_ln_g, conv_ln_b=conv_ln_b, norm_ffn2=norm_ffn2, norm_final=norm_final.reshape(1, D))

    cs = jnp.broadcast_to(c * jax.nn.sigmoid(c), (8, D))
    ada_b_cols = lax.dynamic_slice(ada_b, (0, me * ncol), (1, ncol))
    cs_all, mod_all, w1 = _prologue(cs, ada_w[0], ada_b_cols, [sh["ffn1_w_in"], sh["ffn1_w_out"]])
    cs_all = cs_all.reshape(N_DEV, 8, D)[:, 0, :]
    mod = lax.dynamic_index_in_dim(mod_all.reshape(N_DEV, N_DEV, ncol), me, axis=1, keepdims=False).reshape(9, D)

    loss_local, dx, dmod, gsmall, recv = _local_step(x[0], loss_target[0], mod, small, sh, w1)

    n_used = 9 + len(SMALL_ORDER) + 1
    pack = jnp.concatenate([dmod] + [gsmall[n] for n in SMALL_ORDER] + [jnp.broadcast_to(loss_local, (1, D))]
                           + [jnp.zeros((PACK_ROWS - n_used, D), F32)], axis=0)
    pack_all = _allgather_small(pack).reshape(N_DEV, PACK_ROWS, D)
    tot = _sum_slots(pack_all, "sum_small", PACK_ROWS)
    loss = tot[n_used - 1, 0]
    dmod_all = pack_all[:, 0:9, :].reshape(N_DEV, 9 * D)
    dmod_cols = lax.dynamic_slice(dmod_all, (0, me * ncol), (N_DEV, ncol))

    res = {}
    res["ada_w"] = _adamw_ada_w(ada_w[0], m_ada_w[0], v_ada_w[0], cs_all, dmod_cols)
    big = dict(ffn1_w_in=(ffn1_w_in, m_ffn1_w_in, v_ffn1_w_in), ffn1_w_out=(ffn1_w_out, m_ffn1_w_out, v_ffn1_w_out),
               mix_w_in=(mix_w_in, m_mix_w_in, v_mix_w_in), hgrn_w_o=(hgrn_w_o, m_hgrn_w_o, v_hgrn_w_o),
               conv_w=(conv_w, m_conv_w, v_conv_w), conv_w_o=(conv_w_o, m_conv_w_o, v_conv_w_o),
               mix_w_out=(mix_w_out, m_mix_w_out, v_mix_w_out), ffn2_w_in=(ffn2_w_in, m_ffn2_w_in, v_ffn2_w_in),
               ffn2_w_out=(ffn2_w_out, m_ffn2_w_out, v_ffn2_w_out))
    for n, (w, m, v) in big.items():
        if n in ("ffn1_w_in", "ffn2_w_in"):
            res[n] = tuple(t.T for t in _adamw(w[0].T, m[0].T, v[0].T, recv[n], "adamw_" + n))
        else:
            res[n] = _adamw(w[0], m[0], v[0], recv[n], "adamw_" + n)
    sm_names = ("ada_b", "norm_ffn1", "norm_mix", "hgrn_lb", "hgrn_g", "conv_b", "conv_ln_g", "conv_ln_b",
                "norm_ffn2", "norm_final")
    sm_w = dict(ada_b=(ada_b, m_ada_b, v_ada_b), norm_ffn1=(norm_ffn1, m_norm_ffn1, v_norm_ffn1),
                norm_mix=(norm_mix, m_norm_mix, v_norm_mix), hgrn_lb=(hgrn_lb, m_hgrn_lb, v_hgrn_lb),
                hgrn_g=(hgrn_g, m_hgrn_g, v_hgrn_g), conv_b=(conv_b, m_conv_b, v_conv_b),
                conv_ln_g=(conv_ln_g, m_conv_ln_g, v_conv_ln_g), conv_ln_b=(conv_ln_b, m_conv_ln_b, v_conv_ln_b),
                norm_ffn2=(norm_ffn2, m_norm_ffn2, v_norm_ffn2), norm_final=(norm_final, m_norm_final, v_norm_final))
    sm_w["norm_final"] = tuple(t.reshape(1, D) for t in sm_w["norm_final"])
    grad_rows = dict({n: 9 + i for i, n in enumerate(SMALL_ORDER)}, ada_b=0, hgrn_lb=9 + SMALL_ORDER.index("lb0"))
    res.update(_adamw_small(tot, sm_names, sm_w, grad_rows))
    res["norm_final"] = tuple(t.reshape(norm_final.shape) for t in res["norm_final"])

    order = ("ada_w", "ada_b", "norm_ffn1", "ffn1_w_in", "ffn1_w_out", "norm_mix", "mix_w_in", "hgrn_lb", "hgrn_g",
             "hgrn_w_o", "conv_w", "conv_b", "conv_ln_g", "conv_ln_b", "conv_w_o", "mix_w_out", "norm_ffn2",
             "ffn2_w_in", "ffn2_w_out", "norm_final")
    lead = lambda n, t: t[None] if n in big or n == "ada_w" else t
    outs = [loss, dx[None]]
    for j in range(4):
        outs += [lead(n, res[n][j]) for n in order]
    return tuple(outs)
```

```python
import jax
import jax.numpy as jnp
from jax import lax
from jax.experimental import pallas as pl
from jax.experimental.pallas import tpu as pltpu

F32 = jnp.float32
MM = jnp.bfloat16
ACT = jnp.bfloat16

D = 1024
D_FF = 2816
HEADS = 8
HD = 128
CHUNK = 64
SUB = 16
NSUB = CHUNK // SUB
HGRN_BLOCK = 1024
SAFE_EXP = 60.0
CONV_K = 31
HALO = 32
EPS = 1e-6
N_DEV = 8
NEG = -1e30
Q_SCALE = HD ** -0.5

ADAM_LR = 0.001
ADAM_B1 = 0.9
ADAM_B2 = 0.999
ADAM_EPS = 1e-08
ADAM_WD = 0.01
ADAM_STEP = 10

V7X_VMEM_BYTES = 64 * 1024 * 1024
VMEM_LIMIT = V7X_VMEM_BYTES - 4 * 1024 * 1024
MESH = pl.DeviceIdType.MESH


def _cparams(n_axes):
    return pltpu.CompilerParams(dimension_semantics=("arbitrary",) * n_axes, vmem_limit_bytes=VMEM_LIMIT)


def _mm(a, b):
    return lax.dot_general(a.astype(MM), b.astype(MM), (((1,), (0,)), ((), ())), preferred_element_type=F32)


def _mm_nt(a, b):
    return lax.dot_general(a.astype(MM), b.astype(MM), (((1,), (1,)), ((), ())), preferred_element_type=F32)


def _mm_tn(a, b):
    return lax.dot_general(a.astype(MM), b.astype(MM), (((0,), (0,)), ((), ())), preferred_element_type=F32)


def _sig(x):
    return 1.0 / (1.0 + jnp.exp(-x))


def _colsum(x):
    return jnp.sum(x, axis=0, keepdims=True)


def _rowmean(x):
    return jnp.mean(x, axis=-1, keepdims=True)


def _modnorm_fwd(xv, g, sh, sc):
    r = lax.rsqrt(_rowmean(xv * xv) + EPS)
    xh = xv * r
    n = xh * g
    return n * (1.0 + sc) + sh, xh, n, r


def _modnorm_bwd(dh, xh, n, r, g, sc):
    dsc = _colsum(dh * n)
    dsh = _colsum(dh)
    dn = dh * (1.0 + sc)
    dg = _colsum(dn * xh)
    dxh = dn * g
    dx = r * (dxh - xh * _rowmean(dxh * xh))
    return dx, dsh, dsc, dg


def _ffn_fwd(x, mod, mo, gnorm, w_in_t, w_out, res, name, carry, nxt=None):
    T = x.shape[0]
    tm = min(512, T)
    tn = D_FF // 2

    def body(x_ref, mod_ref, g_ref, wi_ref, wo_ref, *rest):
        if nxt is None:
            xo_ref, a_ref, b_ref, f_ref, h_ref = rest
        else:
            gn_ref, xo_ref, a_ref, b_ref, f_ref, h_ref, hn_ref = rest
        xv = x_ref[...]
        h, _, _, _ = _modnorm_fwd(xv, g_ref[...], mod_ref[mo:mo + 1, :], mod_ref[mo + 1:mo + 2, :])
        h = h.astype(ACT)
        h_ref[...] = h
        f = None
        for c0 in range(0, D_FF, tn):
            a = _mm_nt(h, wi_ref[0, c0:c0 + tn, :])
            b = _mm_nt(h, wi_ref[1, c0:c0 + tn, :])
            a_ref[:, c0:c0 + tn] = a.astype(ACT)
            b_ref[:, c0:c0 + tn] = b.astype(ACT)
            part = _mm(a * _sig(a) * b, wo_ref[c0:c0 + tn, :])
            f = part if f is None else f + part
        f_ref[...] = f
        xo = xv + res * mod_ref[mo + 2:mo + 3, :] * f
        xo_ref[...] = xo
        if nxt is not None:
            hn, _, _, _ = _modnorm_fwd(xo, gn_ref[...], mod_ref[nxt[1]:nxt[1] + 1, :], mod_ref[nxt[1] + 1:nxt[1] + 2, :])
            hn_ref[...] = hn.astype(ACT)

    tile = pl.BlockSpec((tm, D), lambda i: (i, 0))
    wide = pl.BlockSpec((tm, D_FF), lambda i: (i, 0))
    row = pl.BlockSpec((1, D), lambda i: (0, 0))
    n_out = 5 if nxt is None else 6
    out = _gridded(
        body, carry, name=name, grid=(T // tm,),
        in_specs=[
            tile,
            pl.BlockSpec((9, D), lambda i: (0, 0)),
            row,
            pl.BlockSpec((2, D_FF, D), lambda i: (0, 0, 0), pipeline_mode=pl.Buffered(1)),
            pl.BlockSpec((D_FF, D), lambda i: (0, 0), pipeline_mode=pl.Buffered(1)),
        ] + ([] if nxt is None else [row]),
        out_specs=[tile, wide, wide, tile, tile] + ([] if nxt is None else [tile]),
        out_shape=[
            jax.ShapeDtypeStruct((T, D), F32),
            jax.ShapeDtypeStruct((T, D_FF), ACT),
            jax.ShapeDtypeStruct((T, D_FF), ACT),
            jax.ShapeDtypeStruct((T, D), F32),
            jax.ShapeDtypeStruct((T, D), ACT),
        ] + ([] if nxt is None else [jax.ShapeDtypeStruct((T, D), ACT)]),
    )(*((x, mod, gnorm, w_in_t, w_out) + (() if nxt is None else (nxt[0],))))
    return out[:n_out], out[n_out:]


def _ffn_bwd_w(h, df, a, b, w_out, name, carry):
    T = h.shape[0]
    tm = min(2048, T)
    ni = T // tm
    tn = 256
    nj = D_FF // tn

    def body(h_ref, df_ref, a_ref, b_ref, wo_ref, da_ref, db_ref, dwi_ref, dwo_ref, acc_i, acc_o):
        i = pl.program_id(1)

        @pl.when(i == 0)
        def _():
            acc_i[...] = jnp.zeros_like(acc_i)
            acc_o[...] = jnp.zeros_like(acc_o)

        hb = h_ref[...]
        df = df_ref[...]
        av = a_ref[...].astype(F32)
        bv = b_ref[...].astype(F32)
        sg = _sig(av)
        sa = av * sg
        s = (sa * bv).astype(MM)
        ds = _mm_nt(df, wo_ref[...])
        da = (ds * bv * sg * (1.0 + av * (1.0 - sg))).astype(MM)
        db = (ds * sa).astype(MM)
        da_ref[...] = da
        db_ref[...] = db
        acc_o[...] += _mm_tn(s, df)
        acc_i[0] += _mm_tn(da, hb)
        acc_i[1] += _mm_tn(db, hb)

        @pl.when(i == ni - 1)
        def _():
            dwi_ref[...] = acc_i[...].astype(MM)
            dwo_ref[...] = acc_o[...].astype(MM)

    out = _gridded(
        body, carry, name=name, grid=(nj, ni),
        in_specs=[
            pl.BlockSpec((tm, D), lambda j, i: (i, 0)),
            pl.BlockSpec((tm, D), lambda j, i: (i, 0)),
            pl.BlockSpec((tm, tn), lambda j, i: (i, j)),
            pl.BlockSpec((tm, tn), lambda j, i: (i, j)),
            pl.BlockSpec((tn, D), lambda j, i: (j, 0)),
        ],
        out_specs=[
            pl.BlockSpec((tm, tn), lambda j, i: (i, j)),
            pl.BlockSpec((tm, tn), lambda j, i: (i, j)),
            pl.BlockSpec((2, tn, D), lambda j, i: (0, j, 0)),
            pl.BlockSpec((tn, D), lambda j, i: (j, 0)),
        ],
        out_shape=[
            jax.ShapeDtypeStruct((T, D_FF), MM),
            jax.ShapeDtypeStruct((T, D_FF), MM),
            jax.ShapeDtypeStruct((2, D_FF, D), MM),
            jax.ShapeDtypeStruct((D_FF, D), MM),
        ],
        scratch_shapes=[pltpu.VMEM((2, tn, D), F32), pltpu.VMEM((tn, D), F32)],
    )(h, df, a, b, w_out)
    return out[:4], out[4:]


def _ffn_bwd_x(x, dxo, f, da, db, mod, mo, gnorm, w_in_t, res, name, carry):
    T = x.shape[0]
    tm = min(512, T)
    ni = T // tm
    tn = D_FF // 2
    nj = D_FF // tn

    def body(x_ref, dxo_ref, f_ref, da_ref, db_ref, mod_ref, g_ref, wi_ref, dx_ref, sm_ref, dh_scr):
        j = pl.program_id(0)
        i = pl.program_id(1)

        @pl.when((j == 0) & (i == 0))
        def _():
            sm_ref[...] = jnp.zeros_like(sm_ref)

        @pl.when(j == 0)
        def _():
            dh_scr[i] = jnp.zeros((tm, D), F32)

        dh_scr[i] += _mm(da_ref[...], wi_ref[0]) + _mm(db_ref[...], wi_ref[1])

        @pl.when(j == nj - 1)
        def _():
            sc = mod_ref[mo + 1:mo + 2, :]
            _, xh, n, r = _modnorm_fwd(x_ref[...], g_ref[...], mod_ref[mo:mo + 1, :], sc)
            dxn, dsh, dsc, dg = _modnorm_bwd(dh_scr[i], xh, n, r, g_ref[...], sc)
            dxo_v = dxo_ref[...]
            dx_ref[...] = dxo_v + dxn
            sm_ref[0:1, :] += dsh
            sm_ref[1:2, :] += dsc
            sm_ref[2:3, :] += _colsum(dxo_v * f_ref[...]) * res
            sm_ref[3:4, :] += dg

    last = pl.BlockSpec((tm, D), lambda j, i: (jnp.where(j == nj - 1, i, 0), 0))
    out = _gridded(
        body, carry, name=name, grid=(nj, ni),
        in_specs=[last, last, last,
                  pl.BlockSpec((tm, tn), lambda j, i: (i, j)), pl.BlockSpec((tm, tn), lambda j, i: (i, j)),
                  pl.BlockSpec((9, D), lambda j, i: (0, 0)), pl.BlockSpec((1, D), lambda j, i: (0, 0)),
                  pl.BlockSpec((2, tn, D), lambda j, i: (0, j, 0))],
        out_specs=[last, pl.BlockSpec((8, D), lambda j, i: (0, 0))],
        out_shape=[jax.ShapeDtypeStruct((T, D), F32), jax.ShapeDtypeStruct((8, D), F32)],
        scratch_shapes=[pltpu.VMEM((ni, tm, D), F32)],
    )(x, dxo, f, da, db, mod, gnorm, w_in_t)
    return out[:2], out[2:]


def _head(x, target, gfin, mod, gate_row, res):
    T = x.shape[0]
    tm = min(512, T)
    ni = T // tm

    def body(x_ref, t_ref, g_ref, mod_ref, dx_ref, df_ref, sm_ref):
        i = pl.program_id(0)

        @pl.when(i == 0)
        def _():
            sm_ref[...] = jnp.zeros_like(sm_ref)

        xv = x_ref[...]
        g = g_ref[...]
        r = lax.rsqrt(_rowmean(xv * xv) + EPS)
        xh = xv * r
        e = xh * g - t_ref[...]
        sm_ref[1:2, :] += _colsum(e * e) * (0.5 / D)
        dy = e * (1.0 / D)
        sm_ref[0:1, :] += _colsum(dy * xh)
        dxh = dy * g
        dx = r * (dxh - xh * _rowmean(dxh * xh))
        dx_ref[...] = dx
        df_ref[...] = (res * mod_ref[gate_row:gate_row + 1, :] * dx).astype(MM)

        @pl.when(i == ni - 1)
        def _():
            sm_ref[1:2, :] = jnp.broadcast_to(jnp.sum(sm_ref[1:2, :], axis=-1, keepdims=True), (1, D))

    tile = pl.BlockSpec((tm, D), lambda i: (i, 0))
    return pl.pallas_call(
        body, name="head_loss", grid=(ni,),
        in_specs=[tile, tile, pl.BlockSpec((1, D), lambda i: (0, 0)), pl.BlockSpec((9, D), lambda i: (0, 0))],
        out_specs=[tile, tile, pl.BlockSpec((8, D), lambda i: (0, 0))],
        out_shape=[jax.ShapeDtypeStruct((T, D), F32), jax.ShapeDtypeStruct((T, D), MM),
                   jax.ShapeDtypeStruct((8, D), F32)],
        compiler_params=_cparams(1),
    )(x, target, gfin, mod)


def _mixin_fwd(h, w, carry):
    T = h.shape[0]
    tm = min(2048, T)
    ni = T // tm

    def body(h_ref, w_ref, p_ref, h_all):
        i = pl.program_id(1)

        @pl.when(pl.program_id(0) == 0)
        def _():
            h_all[i] = h_ref[...]

        p_ref[0] = _mm(h_all[i], w_ref[0])

    first = lambda k, i: (jnp.where(k == 0, i, ni - 1), 0)
    out = _gridded(
        body, carry, name="mixin_fwd", grid=(8, ni),
        in_specs=[pl.BlockSpec((tm, D), first), pl.BlockSpec((1, D, D), lambda k, i: (k, 0, 0))],
        out_specs=[pl.BlockSpec((1, tm, D), lambda k, i: (k, i, 0))],
        out_shape=[jax.ShapeDtypeStruct((8, T, D), F32)],
        scratch_shapes=[pltpu.VMEM((ni, tm, D), ACT)],
    )(h, w)
    return out[:1], out[1:]


def _mixin_bwd(x, h, dxo, dp, mod, mo, gnorm, w, next_gate, next_res, carry):
    T = x.shape[0]
    tm = min(512, T)
    ni = T // tm

    def body(x_ref, h_ref, dxo_ref, dp_ref, mod_ref, g_ref, w_ref, dx_ref, dw_ref, sm_ref, df_ref, dh_scr, acc):
        k = pl.program_id(0)
        i = pl.program_id(1)

        @pl.when(i == 0)
        def _():
            acc[...] = jnp.zeros_like(acc)

        @pl.when(k == 0)
        def _():
            dh_scr[i] = jnp.zeros((tm, D), F32)

        @pl.when((k == 0) & (i == 0))
        def _():
            sm_ref[...] = jnp.zeros_like(sm_ref)

        dpk = dp_ref[0].astype(MM)
        acc[...] += _mm_tn(h_ref[...], dpk)
        dh_scr[i] += _mm_nt(dpk, w_ref[0])

        @pl.when(i == ni - 1)
        def _():
            dw_ref[0] = acc[...].astype(MM)

        @pl.when(k == 7)
        def _():
            sc = mod_ref[mo + 1:mo + 2, :]
            _, xh, n, r = _modnorm_fwd(x_ref[...], g_ref[...], mod_ref[mo:mo + 1, :], sc)
            dxn, dsh, dsc, dg = _modnorm_bwd(dh_scr[i], xh, n, r, g_ref[...], sc)
            dx = dxo_ref[...] + dxn
            dx_ref[...] = dx
            df_ref[...] = (next_res * mod_ref[next_gate:next_gate + 1, :] * dx).astype(MM)
            sm_ref[0:1, :] += dsh
            sm_ref[1:2, :] += dsc
            sm_ref[3:4, :] += dg

    last = pl.BlockSpec((tm, D), lambda k, i: (jnp.where(k == 7, i, 0), 0))
    out = _gridded(
        body, carry, name="mixin_bwd", grid=(8, ni),
        in_specs=[pl.BlockSpec((tm, D), lambda k, i: (jnp.where(k == 7, i, 0), 0)),
                  pl.BlockSpec((tm, D), lambda k, i: (i, 0)),
                  pl.BlockSpec((tm, D), lambda k, i: (jnp.where(k == 7, i, 0), 0)),
                  pl.BlockSpec((1, tm, D), lambda k, i: (k, i, 0)), pl.BlockSpec((9, D), lambda k, i: (0, 0)),
                  pl.BlockSpec((1, D), lambda k, i: (0, 0)), pl.BlockSpec((1, D, D), lambda k, i: (k, 0, 0))],
        out_specs=[last, pl.BlockSpec((1, D, D), lambda k, i: (k, 0, 0)), pl.BlockSpec((8, D), lambda k, i: (0, 0)),
                   last],
        out_shape=[jax.ShapeDtypeStruct((T, D), F32), jax.ShapeDtypeStruct((8, D, D), MM),
                   jax.ShapeDtypeStruct((8, D), F32), jax.ShapeDtypeStruct((T, D), MM)],
        scratch_shapes=[pltpu.VMEM((ni, tm, D), F32), pltpu.VMEM((D, D), F32)],
    )(x, h, dxo, dp, mod, gnorm, w)
    return out[:4], out[4:]


def _hgrn_consts():
    rows = jnp.arange(SUB * HD) // HD
    e = (rows[:, None] == jnp.arange(HD)[None, :]).astype(MM)
    return e, e.T


def _rows_bcast(ref, cb, first, n):
    parts = [jnp.broadcast_to(ref[pl.ds(c * CHUNK + first, 1), :], (n, HD)) for c in range(cb // CHUNK)]
    return jnp.concatenate(parts, axis=0)


def _hgrn_pre(qr, fr, lb_ref, b_scr, cb):
    z = lb_ref[...]
    lb = _sig(z[0:1, :] - z[1:2, :])
    sq = _sig(qr)
    q = qr * sq * Q_SCALE
    sf = _sig(fr)
    fg = lb + (1.0 - lb) * sf
    lf = jnp.log(fg)
    k = 1.0 - fg
    tl = lax.broadcasted_iota(jnp.int32, (cb, HD), 0) % CHUNK
    bc = lf
    sh = 1
    while sh < CHUNK:
        bc = bc + jnp.where(tl >= sh, pltpu.roll(bc, sh, 0), 0.0)
        sh *= 2
    b_scr[...] = bc
    bl = _rows_bcast(b_scr, cb, CHUNK - 1, CHUNK)
    eb = jnp.exp(bc)
    ekd = jnp.exp(bl - bc)
    ekf = jnp.exp(jnp.minimum(-bc, SAFE_EXP))
    return dict(lb=lb, sq=sq, q=q, sf=sf, fg=fg, k=k, tl=tl, b=bc, bl=bl, eb=eb, ekd=ekd, ekf=ekf,
                qe=q * eb, kd=k * ekd, kf=k * ekf, safe=jnp.max(-bc) < SAFE_EXP)


def _hgrn_pre_fused(p_ref, lb_ref, b_scr, q_scr, k_scr, qe_scr, kf_scr, kd_scr, cb):
    z = lb_ref[...]
    lb = _sig(z[0:1, :] - z[1:2, :])
    tl = lax.broadcasted_iota(jnp.int32, (CHUNK, HD), 0)

    def chunk(c, worst):
        r0 = pl.multiple_of(c * CHUNK, CHUNK)
        rs = pl.ds(r0, CHUNK)
        qr = p_ref[0, rs, :]
        q = qr * _sig(qr) * Q_SCALE
        fg = lb + (1.0 - lb) * _sig(p_ref[1, rs, :])
        k = 1.0 - fg
        bc = jnp.log(fg)
        sh = 1
        while sh < CHUNK:
            bc = bc + jnp.where(tl >= sh, pltpu.roll(bc, sh, 0), 0.0)
            sh *= 2
        b_scr[rs, :] = bc
        q_scr[rs, :] = q
        k_scr[rs, :] = k
        qe_scr[rs, :] = (q * jnp.exp(bc)).astype(MM)
        kf_scr[rs, :] = (k * jnp.exp(jnp.minimum(-bc, SAFE_EXP))).astype(MM)
        kd_scr[rs, :] = (k * jnp.exp(b_scr[pl.ds(r0 + CHUNK - 1, 1), :] - bc)).astype(MM)
        return jnp.maximum(worst, -bc)

    worst = lax.fori_loop(0, cb // CHUNK, chunk, jnp.zeros((CHUNK, HD), F32))
    return jnp.max(worst) < SAFE_EXP


def _hgrn_sub(pre, b_scr, cb):
    bc, tl, q, k = pre["b"], pre["tl"], pre["q"], pre["k"]
    br = [None] + [_rows_bcast(b_scr, cb, SUB * i - 1, CHUNK) for i in range(1, NSUB)]
    sb = tl // SUB
    bref = jnp.where(sb == 0, bc, jnp.where(sb == 1, br[1], jnp.where(sb == 2, br[2], br[3])))
    eqo = jnp.exp(bc - bref)
    eko = [None] + [jnp.exp(jnp.where(tl < SUB * i, br[i] - bc, NEG)) for i in range(1, NSUB)]
    return dict(eqo=eqo, eko=eko, qo=q * eqo, ko=[None] + [k * eko[i] for i in range(1, NSUB)])


def _pad_rows(x):
    return jnp.concatenate([x, jnp.zeros_like(x)], axis=0)


def _by_subblock(sbc, parts):
    out = jnp.zeros_like(parts[1])
    for i in range(1, NSUB):
        out = jnp.where(sbc == i, parts[i], out)
    return out


def _hgrn_fwd(p, hgrn_lb, hgrn_g, carry):
    T = p.shape[1]
    cb = min(HGRN_BLOCK, T)
    nch = cb // CHUNK
    ncb = T // cb
    e_mat, _ = _hgrn_consts()

    def body(p_ref, lb_ref, g_ref, e_ref, o_ref, oa_ref, a_ref, s_ref, st_scr, q_scr, k_scr, b_scr, z_scr, ad_scr,
             qe_scr, kf_scr, kd_scr):
        @pl.when(pl.program_id(1) == 0)
        def _():
            st_scr[...] = jnp.zeros_like(st_scr)

        safe = _hgrn_pre_fused(p_ref, lb_ref, b_scr, q_scr, k_scr, qe_scr, kf_scr, kd_scr, cb)
        chunks = [slice(c * CHUNK, (c + 1) * CHUNK) for c in range(nch)]
        row_i = lax.broadcasted_iota(jnp.int32, (CHUNK, HD), 0)
        lane_i = lax.broadcasted_iota(jnp.int32, (CHUNK, HD), 1)
        sbc = row_i // SUB
        causal = lane_i <= row_i

        @pl.when(safe)
        def _():
            for rs in chunks:
                ad_scr[rs, :] = jnp.where(causal, _mm_nt(qe_scr[rs, :], _pad_rows(kf_scr[rs, :])), 0.0)

        @pl.when(jnp.logical_not(safe))
        def _():
            tl = lax.broadcasted_iota(jnp.int32, (cb, HD), 0) % CHUNK
            sub = _hgrn_sub(dict(b=b_scr[...], tl=tl, q=q_scr[...], k=k_scr[...]), b_scr, cb)
            ti = lax.broadcasted_iota(jnp.int32, (SUB, HD), 0)

            def zbody(c, carry):
                for i in range(NSUB):
                    r0 = pl.multiple_of(c * CHUNK + SUB * i, SUB)
                    qi = q_scr[pl.ds(r0, SUB), :]
                    bi = b_scr[pl.ds(r0, SUB), :]
                    for s in range(SUB):
                        krow = k_scr[pl.ds(r0 + s, 1), :]
                        brow = b_scr[pl.ds(r0 + s, 1), :]
                        if s < 8:
                            zz = qi * krow * jnp.exp(jnp.where(ti >= s, bi - brow, NEG))
                        else:
                            lo = qi[8:] * krow * jnp.exp(jnp.where(ti[8:] >= s, bi[8:] - brow, NEG))
                            zz = jnp.concatenate([jnp.zeros((8, HD), F32), lo], axis=0)
                        z_scr[i, pl.ds(pl.multiple_of(c * SUB, SUB), SUB), s * HD:(s + 1) * HD] = zz.astype(MM)
                return carry

            lax.fori_loop(0, nch, zbody, 0)
            adiag = [_mm(z_scr[i], e_ref[...]) for i in range(NSUB)]
            offs = [[_mm_nt(sub["qo"][rs], _pad_rows(sub["ko"][i][rs])) for i in range(1, NSUB)] for rs in chunks]
            for c, rs in enumerate(chunks):
                dparts = []
                for i in range(NSUB):
                    blk = adiag[i][c * SUB:(c + 1) * SUB]
                    dparts.append(blk if i == 0 else pltpu.roll(blk, SUB * i, 1))
                ad_scr[rs, :] = _by_subblock(sbc, [None] + offs[c]) + jnp.concatenate(dparts, axis=0)

        kv = [_mm_tn(p_ref[2, rs, :], kd_scr[rs, :]) for rs in chunks]
        a_ref[0] = ad_scr[...]
        o_intra = [_mm(ad_scr[rs, :], _pad_rows(p_ref[2, rs, :])) for rs in chunks]
        states = []
        st = st_scr[...]
        for c in range(nch):
            states.append(st)
            st = st * jnp.exp(b_scr[pl.ds(c * CHUNK + CHUNK - 1, 1), :]) + kv[c]
        st_scr[...] = st
        g = g_ref[...]
        for c, rs in enumerate(chunks):
            s_ref[0, c] = states[c]
            o = o_intra[c] + _mm_nt(qe_scr[rs, :], states[c])
            o_ref[rs, :] = o
            og = p_ref[3, rs, :]
            oa_ref[rs, :] = (o * lax.rsqrt(_rowmean(o * o) + EPS) * g * og * _sig(og)).astype(ACT)

    out = _gridded(
        body, carry, name="hgrn_fwd", grid=(HEADS, ncb),
        in_specs=[pl.BlockSpec((4, cb, HD), lambda h, c: (0, c, h)),
                  pl.BlockSpec((2, HD), lambda h, c: (0, h)),
                  pl.BlockSpec((1, HD), lambda h, c: (0, h)),
                  pl.BlockSpec((SUB * HD, HD), lambda h, c: (0, 0))],
        out_specs=[pl.BlockSpec((cb, HD), lambda h, c: (c, h)),
                   pl.BlockSpec((cb, HD), lambda h, c: (c, h)),
                   pl.BlockSpec((1, cb, HD), lambda h, c: (h, c, 0)),
                   pl.BlockSpec((1, nch, HD, HD), lambda h, c: (h, c, 0, 0))],
        out_shape=[jax.ShapeDtypeStruct((T, D), F32), jax.ShapeDtypeStruct((T, D), ACT),
                   jax.ShapeDtypeStruct((HEADS, T, HD), F32),
                   jax.ShapeDtypeStruct((HEADS, T // CHUNK, HD, HD), F32)],
        scratch_shapes=[pltpu.VMEM((HD, HD), F32), pltpu.VMEM((cb, HD), F32), pltpu.VMEM((cb, HD), F32),
                        pltpu.VMEM((cb, HD), F32), pltpu.VMEM((NSUB, nch * SUB, SUB * HD), MM),
                        pltpu.VMEM((cb, HD), F32), pltpu.VMEM((cb, HD), MM), pltpu.VMEM((cb, HD), MM),
                        pltpu.VMEM((cb, HD), MM)],
    )(p, hgrn_lb, hgrn_g, e_mat)
    return out[:4], out[4:]


def _hgrn_bwd(p, o, a_all, s_all, doa, hgrn_lb, hgrn_g, dp, carry):
    T = p.shape[1]
    cb = min(HGRN_BLOCK, T)
    nch = cb // CHUNK
    ncb = T // cb
    _, et_mat = _hgrn_consts()

    def body(p_ref, o_ref, a_ref, s_ref, doa_ref, lb_ref, g_ref, et_ref, dp_in, dp_ref, sm_ref,
             dst_scr, q_scr, k_scr, b_scr, x_scr, dqd_scr, dkd_scr):
        del dp_in

        @pl.when(pl.program_id(1) == 0)
        def _():
            dst_scr[...] = jnp.zeros_like(dst_scr)
            sm_ref[...] = jnp.zeros_like(sm_ref)

        qr = p_ref[0]
        v = p_ref[2]
        og = p_ref[3]
        pre = _hgrn_pre(qr, p_ref[1], lb_ref, b_scr, cb)
        q, k = pre["q"], pre["k"]
        g = g_ref[...]
        ov = o_ref[...]
        r = lax.rsqrt(_rowmean(ov * ov) + EPS)
        oh = ov * r
        sgo = _sig(og)
        doa_v = doa_ref[...]
        don = doa_v * og * sgo
        dog = doa_v * oh * g * sgo * (1.0 + og * (1.0 - sgo))
        sm_ref[1:2, :] += _colsum(don * oh)
        doh = don * g
        do = r * (doh - oh * _rowmean(doh * oh))

        sbc = lax.broadcasted_iota(jnp.int32, (CHUNK, HD), 0) // SUB
        row_i = lax.broadcasted_iota(jnp.int32, (CHUNK, HD), 0)
        lane_i = lax.broadcasted_iota(jnp.int32, (CHUNK, HD), 1)
        causal = lane_i <= row_i
        chunks = [slice(c * CHUNK, (c + 1) * CHUNK) for c in range(nch)]
        da_parts = [jnp.where(causal, _mm_nt(do[rs], _pad_rows(v[rs])), 0.0) for rs in chunks]
        dv_parts = [_mm_tn(a_ref[0, rs, :], do[rs])[:CHUNK] for rs in chunks]

        @pl.when(pre["safe"])
        def _():
            hi = dict(preferred_element_type=F32, precision=lax.Precision.HIGH)
            for c, rs in enumerate(chunks):
                dqd_scr[rs, :] = pre["eb"][rs] * lax.dot_general(
                    da_parts[c], _pad_rows(pre["kf"][rs]), (((1,), (0,)), ((), ())), **hi)
                dkd_scr[rs, :] = pre["ekf"][rs] * lax.dot_general(
                    da_parts[c], pre["qe"][rs], (((0,), (0,)), ((), ())), **hi)[:CHUNK]

        @pl.when(jnp.logical_not(pre["safe"]))
        def _():
            sub = _hgrn_sub(pre, b_scr, cb)
            dqoff_mm = [[_mm(da_parts[c], _pad_rows(sub["ko"][i][rs])) for i in range(1, NSUB)]
                        for c, rs in enumerate(chunks)]
            dkoff_mm = [[_mm_tn(jnp.where(sbc == i, da_parts[c], 0.0), sub["qo"][rs])[:CHUNK]
                         for i in range(1, NSUB)] for c, rs in enumerate(chunks)]
            dqoff_parts = [_by_subblock(sbc, [None] + dqoff_mm[c]) for c in range(nch)]
            dkoff_parts = []
            for c, rs in enumerate(chunks):
                dko = sub["eko"][1][rs] * dkoff_mm[c][0]
                for i in range(2, NSUB):
                    dko = dko + sub["eko"][i][rs] * dkoff_mm[c][i - 1]
                dkoff_parts.append(dko)
            q_scr[...] = q
            k_scr[...] = k
            for i in range(NSUB):
                rows = []
                for c in range(nch):
                    blk = da_parts[c][SUB * i:SUB * (i + 1)]
                    rows.append(blk if i == 0 else pltpu.roll(blk, HD - SUB * i, 1))
                x_scr[i] = _mm(jnp.concatenate(rows, axis=0), et_ref[...])
            ti = lax.broadcasted_iota(jnp.int32, (SUB, HD), 0)

            def dbody(c, carry):
                for i in range(NSUB):
                    r0 = pl.multiple_of(c * CHUNK + SUB * i, SUB)
                    qi = q_scr[pl.ds(r0, SUB), :]
                    bi = b_scr[pl.ds(r0, SUB), :]
                    dq_hi = jnp.zeros((8, HD), F32)
                    dq_lo = jnp.zeros((8, HD), F32)
                    dk_hi = jnp.zeros((8, HD), F32)
                    dk_lo = jnp.zeros((8, HD), F32)
                    c0 = pl.multiple_of(c * SUB, SUB)
                    t8 = ti[:8]
                    for s in range(SUB):
                        krow = k_scr[pl.ds(r0 + s, 1), :]
                        brow = b_scr[pl.ds(r0 + s, 1), :]
                        w_lo = (x_scr[i, pl.ds(c0 + 8, 8), s * HD:(s + 1) * HD]
                                * jnp.exp(jnp.where(t8 + 8 >= s, bi[8:] - brow, NEG)))
                        dq_lo = dq_lo + w_lo * krow
                        col = _colsum(w_lo * qi[8:])
                        if s < 8:
                            w_hi = (x_scr[i, pl.ds(c0, 8), s * HD:(s + 1) * HD]
                                    * jnp.exp(jnp.where(t8 >= s, bi[:8] - brow, NEG)))
                            dq_hi = dq_hi + w_hi * krow
                            dk_hi = jnp.where(t8 == s, col + _colsum(w_hi * qi[:8]), dk_hi)
                        else:
                            dk_lo = jnp.where(t8 + 8 == s, col, dk_lo)
                    dqd_scr[pl.ds(r0, SUB), :] = jnp.concatenate([dq_hi, dq_lo], axis=0)
                    dkd_scr[pl.ds(r0, SUB), :] = jnp.concatenate([dk_hi, dk_lo], axis=0)
                return carry

            lax.fori_loop(0, nch, dbody, 0)
            dqd_scr[...] += jnp.concatenate(dqoff_parts, axis=0) * sub["eqo"]
            dkd_scr[...] += jnp.concatenate(dkoff_parts, axis=0)

        qdo = [_mm_tn(do[rs], pre["qe"][rs]) for rs in chunks]
        dsts = [None] * nch
        dst = dst_scr[...]
        for c in reversed(range(nch)):
            dsts[c] = dst
            dst = dst * jnp.exp(b_scr[pl.ds(c * CHUNK + CHUNK - 1, 1), :]) + qdo[c]
        dst_scr[...] = dst
        sts = [s_ref[0, c] for c in range(nch)]
        dqe_parts = [_mm(do[rs], sts[c]) for c, rs in enumerate(chunks)]
        dkdec_parts = [_mm(v[rs], dsts[c]) for c, rs in enumerate(chunks)]
        dvi_parts = [_mm_nt(pre["kd"][rs], dsts[c]) for c, rs in enumerate(chunks)]
        debl_parts = [_colsum(dsts[c] * sts[c]) for c in range(nch)]
        dqe = jnp.concatenate(dqe_parts, axis=0)
        dkdec = jnp.concatenate(dkdec_parts, axis=0)
        dq_tot = dqd_scr[...] + dqe * pre["eb"]
        dk_inter = dkdec * pre["ekd"]
        dk_tot = dkd_scr[...] + dk_inter
        db = q * dq_tot - k * dk_tot
        kdk = k * dk_inter
        dbl = jnp.concatenate(
            [jnp.broadcast_to(jnp.exp(b_scr[pl.ds(c * CHUNK + CHUNK - 1, 1), :]) * debl_parts[c]
                              + _colsum(kdk[c * CHUNK:(c + 1) * CHUNK]), (CHUNK, HD)) for c in range(nch)], axis=0)
        tl = pre["tl"]
        rc = db
        sh = 1
        while sh < CHUNK:
            rc = rc + jnp.where(tl + sh < CHUNK, pltpu.roll(rc, cb - sh, 0), 0.0)
            sh *= 2
        dlf = rc + dbl
        dfg = dlf / pre["fg"] - dk_tot
        sf = pre["sf"]
        lb = pre["lb"]
        sm_ref[0:1, :] += _colsum(dfg * (1.0 - sf))
        sq = pre["sq"]
        dp_ref[0] = (dq_tot * Q_SCALE * sq * (1.0 + qr * (1.0 - sq))).astype(ACT)
        dp_ref[1] = (dfg * (1.0 - lb) * sf * (1.0 - sf)).astype(ACT)
        dp_ref[2] = (jnp.concatenate(dv_parts, axis=0) + jnp.concatenate(dvi_parts, axis=0)).astype(ACT)
        dp_ref[3] = dog.astype(ACT)

    rev = lambda c: ncb - 1 - c
    out = _gridded(
        body, carry, name="hgrn_bwd", grid=(HEADS, ncb),
        in_specs=[pl.BlockSpec((4, cb, HD), lambda h, c: (0, rev(c), h)),
                  pl.BlockSpec((cb, HD), lambda h, c: (rev(c), h)),
                  pl.BlockSpec((1, cb, HD), lambda h, c: (h, rev(c), 0)),
                  pl.BlockSpec((1, nch, HD, HD), lambda h, c: (h, rev(c), 0, 0)),
                  pl.BlockSpec((cb, HD), lambda h, c: (rev(c), h)),
                  pl.BlockSpec((2, HD), lambda h, c: (0, h)),
                  pl.BlockSpec((1, HD), lambda h, c: (0, h)),
                  pl.BlockSpec((HD, SUB * HD), lambda h, c: (0, 0)),
                  pl.BlockSpec(memory_space=pl.ANY)],
        out_specs=[pl.BlockSpec((4, cb, HD), lambda h, c: (0, rev(c), h)),
                   pl.BlockSpec((8, HD), lambda h, c: (0, h))],
        out_shape=[jax.ShapeDtypeStruct(dp.shape, dp.dtype), jax.ShapeDtypeStruct((8, D), F32)],
        aliases={8: 0},
        scratch_shapes=[pltpu.VMEM((HD, HD), F32), pltpu.VMEM((cb, HD), F32), pltpu.VMEM((cb, HD), F32),
                        pltpu.VMEM((cb, HD), F32), pltpu.VMEM((NSUB, nch * SUB, SUB * HD), F32),
                        pltpu.VMEM((cb, HD), F32), pltpu.VMEM((cb, HD), F32)],
    )(p, o, a_all, s_all, doa, hgrn_lb, hgrn_g, et_mat, dp)
    return out[:2], out[2:]


def _ln_fwd(u1, g, b):
    mu = _rowmean(u1)
    xc = u1 - mu
    rs = lax.rsqrt(_rowmean(xc * xc) + EPS)
    xh = xc * rs
    return xh * g + b, xh, rs


CONV_RB = 64
LANES = 128


def _shift_rows(src, sh, ls, n):
    for r in range(1, 8):
        sh[r - 1, 0:n, :] = src[pl.ds(r, n), ls]


def _tap(src, sh, ls, off, r0, rows):
    r = off % 8
    if r == 0:
        return src[pl.ds(r0 + off, rows), ls]
    return sh[r - 1, pl.ds(r0 + off - r, rows), :]


def _conv_fwd(p, cw, cb_, lng, lnb, carry):
    T = p.shape[1]
    tm = min(512, T)
    n = HALO + tm - 8

    def body(p_ref, cw_ref, cb_ref, g_ref, b_ref, u1_ref, u2_ref, buf, sh):
        @pl.when(pl.program_id(0) == 0)
        def _():
            buf[0:HALO, :] = jnp.zeros((HALO, D), F32)

        buf[HALO:HALO + tm, :] = p_ref[0] * _sig(p_ref[1])
        for lb in range(D // LANES):
            ls = slice(lb * LANES, (lb + 1) * LANES)
            _shift_rows(buf, sh, ls, n)
            taps = [cw_ref[j:j + 1, ls] for j in range(CONV_K)]
            bias = cb_ref[:, ls]

            def rows_body(rb, carry):
                r0 = pl.multiple_of(rb * CONV_RB, CONV_RB)
                acc = jnp.broadcast_to(bias, (CONV_RB, LANES))
                for j in range(CONV_K):
                    acc = acc + taps[j] * _tap(buf, sh, ls, HALO - (CONV_K - 1) + j, r0, CONV_RB)
                u1_ref[pl.ds(r0, CONV_RB), ls] = acc
                return carry

            lax.fori_loop(0, tm // CONV_RB, rows_body, 0)
        y, _, _ = _ln_fwd(u1_ref[...], g_ref[...], b_ref[...])
        u2_ref[...] = (y * _sig(y)).astype(ACT)
        buf[0:HALO, :] = buf[tm:tm + HALO, :]

    out = _gridded(
        body, carry, name="conv_fwd", grid=(T // tm,),
        in_specs=[pl.BlockSpec((2, tm, D), lambda i: (2, i, 0)), pl.BlockSpec((HALO, D), lambda i: (0, 0)),
                  pl.BlockSpec((1, D), lambda i: (0, 0)), pl.BlockSpec((1, D), lambda i: (0, 0)),
                  pl.BlockSpec((1, D), lambda i: (0, 0))],
        out_specs=[pl.BlockSpec((tm, D), lambda i: (i, 0)), pl.BlockSpec((tm, D), lambda i: (i, 0))],
        out_shape=[jax.ShapeDtypeStruct((T, D), F32), jax.ShapeDtypeStruct((T, D), ACT)],
        scratch_shapes=[pltpu.VMEM((HALO + tm, D), F32), pltpu.VMEM((7, n, LANES), F32)],
    )(p, cw, cb_, lng, lnb)
    return out[:2], out[2:]


def _conv_bwd(p, u1, du2, cw, lng, lnb, dp, carry):
    T = p.shape[1]
    tm = min(512, T)
    ni = T // tm
    hb = tm // HALO

    n = HALO + tm - 8

    def body(p_ref, ph_ref, u1_ref, du2_ref, cw_ref, g_ref, b_ref, dp_in, dp_ref, dcw_ref, sm_ref, ubuf, dbuf,
             sh, dacc):
        del dp_in
        step = pl.program_id(0)

        @pl.when(step == 0)
        def _():
            dbuf[tm:tm + HALO, :] = jnp.zeros((HALO, D), F32)
            dcw_ref[...] = jnp.zeros_like(dcw_ref)
            sm_ref[...] = jnp.zeros_like(sm_ref)

        ua = p_ref[0]
        sgb = _sig(p_ref[1])
        halo = ph_ref[0] * _sig(ph_ref[1])
        ubuf[0:HALO, :] = jnp.where(step == ni - 1, 0.0, halo)
        ubuf[HALO:HALO + tm, :] = ua * sgb
        g = g_ref[...]
        y, xh, rs = _ln_fwd(u1_ref[...], g, b_ref[...])
        sy = _sig(y)
        dy = du2_ref[...] * sy * (1.0 + y * (1.0 - sy))
        sm_ref[1:2, :] += _colsum(dy * xh)
        sm_ref[2:3, :] += _colsum(dy)
        dxh = dy * g
        du1 = rs * (dxh - _rowmean(dxh) - xh * _rowmean(dxh * xh))
        sm_ref[0:1, :] += _colsum(du1)
        dbuf[0:tm, :] = du1
        for lb in range(D // LANES):
            ls = slice(lb * LANES, (lb + 1) * LANES)
            taps = [cw_ref[j:j + 1, ls] for j in range(CONV_K)]
            _shift_rows(dbuf, sh, ls, n)

            def du0_body(rb, carry):
                r0 = pl.multiple_of(rb * CONV_RB, CONV_RB)
                acc = jnp.zeros((CONV_RB, LANES), F32)
                for j in range(CONV_K):
                    acc = acc + taps[j] * _tap(dbuf, sh, ls, CONV_K - 1 - j, r0, CONV_RB)
                dp_ref[0, pl.ds(r0, CONV_RB), ls] = acc.astype(ACT)
                return carry

            lax.fori_loop(0, tm // CONV_RB, du0_body, 0)
            _shift_rows(ubuf, sh, ls, n)
            dacc[...] = jnp.zeros_like(dacc)

            def dcw_body(rb, carry):
                r0 = pl.multiple_of(rb * CONV_RB, CONV_RB)
                d = dbuf[pl.ds(r0, CONV_RB), ls]
                for j in range(CONV_K):
                    prod = d * _tap(ubuf, sh, ls, HALO - (CONV_K - 1) + j, r0, CONV_RB)
                    dacc[8 * j:8 * j + 8, :] += jnp.sum(prod.reshape(CONV_RB // 8, 8, LANES), axis=0)
                return carry

            lax.fori_loop(0, tm // CONV_RB, dcw_body, 0)
            for j in range(CONV_K):
                dcw_ref[j:j + 1, ls] += _colsum(dacc[8 * j:8 * j + 8, :])
        du0 = dp_ref[0].astype(F32)
        dp_ref[0] = (du0 * sgb).astype(ACT)
        dp_ref[1] = (du0 * ua * sgb * (1.0 - sgb)).astype(ACT)
        dbuf[tm:tm + HALO, :] = dbuf[0:HALO, :]

    rev = lambda i: ni - 1 - i
    out = _gridded(
        body, carry, name="conv_bwd", grid=(ni,),
        in_specs=[pl.BlockSpec((2, tm, D), lambda i: (2, rev(i), 0)),
                  pl.BlockSpec((2, HALO, D), lambda i: (2, jnp.maximum(rev(i) * hb - 1, 0), 0)),
                  pl.BlockSpec((tm, D), lambda i: (rev(i), 0)), pl.BlockSpec((tm, D), lambda i: (rev(i), 0)),
                  pl.BlockSpec((HALO, D), lambda i: (0, 0)), pl.BlockSpec((1, D), lambda i: (0, 0)),
                  pl.BlockSpec((1, D), lambda i: (0, 0)), pl.BlockSpec(memory_space=pl.ANY)],
        out_specs=[pl.BlockSpec((2, tm, D), lambda i: (2, rev(i), 0)),
                   pl.BlockSpec((HALO, D), lambda i: (0, 0)), pl.BlockSpec((8, D), lambda i: (0, 0))],
        out_shape=[jax.ShapeDtypeStruct(dp.shape, dp.dtype), jax.ShapeDtypeStruct((HALO, D), F32),
                   jax.ShapeDtypeStruct((8, D), F32)],
        aliases={7: 0},
        scratch_shapes=[pltpu.VMEM((HALO + tm, D), F32), pltpu.VMEM((tm + HALO, D), F32),
                        pltpu.VMEM((7, n, LANES), F32), pltpu.VMEM((8 * CONV_K, LANES), F32)],
    )(p, p, u1, du2, cw, lng, lnb, dp)
    return out[:3], out[3:]


def _mixout_fwd(x, oa, u2, p, mod, mo, w_a, w_b, w_o):
    T = x.shape[0]
    tm = min(512, T)

    def body(x_ref, oa_ref, u2_ref, p_ref, mod_ref, wa_ref, wb_ref, wo_ref, xo_ref, ya_ref, yb_ref, mo_ref):
        ya = _mm(oa_ref[...], wa_ref[...])
        yb = _mm(u2_ref[...], wb_ref[...])
        ya_ref[...] = ya.astype(ACT)
        yb_ref[...] = yb.astype(ACT)
        merged = _sig(p_ref[0]) * ya + _sig(p_ref[1]) * yb
        out = _mm(merged, wo_ref[...])
        mo_ref[...] = out
        xo_ref[...] = x_ref[...] + mod_ref[mo + 2:mo + 3, :] * out

    tile = pl.BlockSpec((tm, D), lambda i: (i, 0))
    wspec = pl.BlockSpec((D, D), lambda i: (0, 0))
    return pl.pallas_call(
        body, name="mixout_fwd", grid=(T // tm,),
        in_specs=[tile, tile, tile, pl.BlockSpec((2, tm, D), lambda i: (3, i, 0)),
                  pl.BlockSpec((9, D), lambda i: (0, 0)), wspec, wspec, wspec],
        out_specs=[tile, tile, tile, tile],
        out_shape=[jax.ShapeDtypeStruct((T, D), F32), jax.ShapeDtypeStruct((T, D), ACT),
                   jax.ShapeDtypeStruct((T, D), ACT), jax.ShapeDtypeStruct((T, D), F32)],
        compiler_params=_cparams(1),
    )(x, oa, u2, p, mod, w_a, w_b, w_o)


def _mixout_bwd(dxo, oa, u2, ya, yb, mout, p, mod, mo, w_a, w_b, w_o):
    T = dxo.shape[0]
    tm = min(256, T)

    def body(dxo_ref, oa_ref, u2_ref, ya_ref, yb_ref, mo_ref, p_ref, mod_ref, wa_ref, wb_ref, wo_ref,
             dp_ref, doa_ref, du2_ref, dwa_ref, dwb_ref, dwo_ref, sm_ref):
        @pl.when(pl.program_id(0) == 0)
        def _():
            dwa_ref[...] = jnp.zeros_like(dwa_ref)
            dwb_ref[...] = jnp.zeros_like(dwb_ref)
            dwo_ref[...] = jnp.zeros_like(dwo_ref)
            sm_ref[...] = jnp.zeros_like(sm_ref)

        dxo_v = dxo_ref[...]
        sm_ref[2:3, :] += _colsum(dxo_v * mo_ref[...])
        dmo = (mod_ref[mo + 2:mo + 3, :] * dxo_v).astype(MM)
        ya = ya_ref[...].astype(F32)
        yb = yb_ref[...].astype(F32)
        sga = _sig(p_ref[0])
        sgb = _sig(p_ref[1])
        merged = (sga * ya + sgb * yb).astype(MM)
        dwo_ref[...] += _mm_tn(merged, dmo)
        dmg = _mm_nt(dmo, wo_ref[...])
        dp_ref[0] = (dmg * ya * sga * (1.0 - sga)).astype(ACT)
        dp_ref[1] = (dmg * yb * sgb * (1.0 - sgb)).astype(ACT)
        dya = (dmg * sga).astype(MM)
        dyb = (dmg * sgb).astype(MM)
        dwa_ref[...] += _mm_tn(oa_ref[...], dya)
        dwb_ref[...] += _mm_tn(u2_ref[...], dyb)
        doa_ref[...] = _mm_nt(dya, wa_ref[...])
        du2_ref[...] = _mm_nt(dyb, wb_ref[...])

    tile = pl.BlockSpec((tm, D), lambda i: (i, 0))
    wspec = pl.BlockSpec((D, D), lambda i: (0, 0))
    return pl.pallas_call(
        body, name="mixout_bwd", grid=(T // tm,),
        in_specs=[tile, tile, tile, tile, tile, tile, pl.BlockSpec((2, tm, D), lambda i: (3, i, 0)),
                  pl.BlockSpec((9, D), lambda i: (0, 0)), wspec, wspec, wspec],
        out_specs=[pl.BlockSpec((2, tm, D), lambda i: (3, i, 0)), tile, tile, wspec, wspec, wspec,
                   pl.BlockSpec((8, D), lambda i: (0, 0))],
        out_shape=[jax.ShapeDtypeStruct((8, T, D), ACT), jax.ShapeDtypeStruct((T, D), F32),
                   jax.ShapeDtypeStruct((T, D), F32), jax.ShapeDtypeStruct((D, D), F32),
                   jax.ShapeDtypeStruct((D, D), F32), jax.ShapeDtypeStruct((D, D), F32),
                   jax.ShapeDtypeStruct((8, D), F32)],
        compiler_params=_cparams(1),
    )(dxo, oa, u2, ya, yb, mout, p, mod, w_a, w_b, w_o)


def _adamw_ada_w(w, m, v, cs_all, dmod_cols):
    R, C = w.shape
    tr = 256
    cs_t = jnp.pad(cs_all.T, ((0, 0), (0, HD - N_DEV)))
    dm = jnp.pad(dmod_cols, ((0, HD - N_DEV), (0, 0)))

    def body(w_ref, m_ref, v_ref, cs_ref, d_ref, go_ref, do_ref, mo_ref, vo_ref):
        gv = jnp.dot(cs_ref[...], d_ref[...], preferred_element_type=F32, precision=lax.Precision.HIGHEST)
        go_ref[...] = gv
        do_ref[...], mo_ref[...], vo_ref[...] = _adam_math(w_ref[...], gv, m_ref[...], v_ref[...])

    tile = pl.BlockSpec((tr, C), lambda i: (i, 0))
    sds = jax.ShapeDtypeStruct((R, C), F32)
    return pl.pallas_call(
        body, name="adamw_ada_w", grid=(R // tr,),
        in_specs=[tile, tile, tile, pl.BlockSpec((tr, HD), lambda i: (i, 0)), pl.BlockSpec((HD, C), lambda i: (0, 0))],
        out_specs=[tile] * 4, out_shape=[sds] * 4, compiler_params=_cparams(1))(w, m, v, cs_t, dm)


def _adam_math(w, g, m, v):
    m2 = ADAM_B1 * m + (1.0 - ADAM_B1) * g
    v2 = ADAM_B2 * v + (1.0 - ADAM_B2) * (g * g)
    m_hat = m2 / (1.0 - ADAM_B1 ** ADAM_STEP)
    v_hat = v2 / (1.0 - ADAM_B2 ** ADAM_STEP)
    delta = -ADAM_LR * (m_hat / (jnp.sqrt(v_hat) + ADAM_EPS) + ADAM_WD * w)
    return delta, m2, v2


def _adamw(w, m, v, g, name):
    R, C = w.shape
    slots = g.ndim == 3
    n_slots = g.shape[0] if slots else 0
    tr = R
    for cand in (256, 176):
        if R % cand == 0 and R > cand:
            tr = cand
            break

    def body(w_ref, m_ref, v_ref, g_ref, go_ref, d_ref, mo_ref, vo_ref):
        if slots:
            gv = g_ref[0].astype(F32)
            for s in range(1, n_slots):
                gv = gv + g_ref[s].astype(F32)
        else:
            gv = g_ref[...]
        go_ref[...] = gv
        d_ref[...], mo_ref[...], vo_ref[...] = _adam_math(w_ref[...], gv, m_ref[...], v_ref[...])

    tile = pl.BlockSpec((tr, C), lambda i: (i, 0))
    gspec = pl.BlockSpec((n_slots, tr, C), lambda i: (0, i, 0)) if slots else tile
    sds = jax.ShapeDtypeStruct((R, C), F32)
    return pl.pallas_call(
        body, name=name, grid=(R // tr,), in_specs=[tile, tile, tile, gspec], out_specs=[tile] * 4,
        out_shape=[sds] * 4, compiler_params=_cparams(1),
    )(w, m, v, g)


def _adamw_small(tot, names, params, grad_rows):
    k = len(names)

    def body(tot_ref, *refs):
        ins, outs = refs[:3 * k], refs[3 * k:]
        for i, n in enumerate(names):
            w_ref, m_ref, v_ref = ins[3 * i:3 * i + 3]
            go, do, mo, vo = outs[4 * i:4 * i + 4]
            row = grad_rows[n]
            for j in range(w_ref.shape[1] // D):
                ls = slice(j * D, (j + 1) * D)
                g = tot_ref[row + j:row + j + 1, :]
                w = w_ref[:, ls]
                if n == "hgrn_lb":
                    p0 = _sig(w[0:1] - w[1:2])
                    dz0 = p0 * (1.0 - p0) * g
                    g = jnp.concatenate([dz0, -dz0], axis=0)
                go[:, ls] = g
                do[:, ls], mo[:, ls], vo[:, ls] = _adam_math(w, g, m_ref[:, ls], v_ref[:, ls])

    flat = [t for n in names for t in params[n]]
    out_shape = [jax.ShapeDtypeStruct(params[n][0].shape, F32) for n in names for _ in range(4)]
    outs = pl.pallas_call(body, name="adamw_small", out_shape=out_shape)(tot, *flat)
    return {n: tuple(outs[4 * i:4 * i + 4]) for i, n in enumerate(names)}


def _cast_shards(ws):
    n = len(ws)

    def body(*refs):
        for src, dst in zip(refs[:n], refs[n:]):
            dst[...] = src[...].astype(MM)

    return pl.pallas_call(body, name="cast_shards", out_shape=[jax.ShapeDtypeStruct(w.shape, MM) for w in ws],
                          compiler_params=pltpu.CompilerParams(vmem_limit_bytes=VMEM_LIMIT))(*ws)


def _sum_slots(pack, name, tr):
    n, R, C = pack.shape

    def body(p_ref, out_ref):
        acc = p_ref[0].astype(F32)
        for s in range(1, n):
            acc = acc + p_ref[s].astype(F32)
        out_ref[...] = acc

    return pl.pallas_call(
        body, name=name, grid=(R // tr,), in_specs=[pl.BlockSpec((n, tr, C), lambda i: (0, i, 0))],
        out_specs=pl.BlockSpec((tr, C), lambda i: (i, 0)), out_shape=jax.ShapeDtypeStruct((R, C), F32),
        compiler_params=_cparams(1))(pack)


def _me():
    return lax.axis_index("x"), lax.axis_index("y"), lax.axis_index("c")


def _peer(r):
    x, y, c = _me()
    px = 1 - x if r & 4 else x
    py = 1 - y if r & 2 else y
    pc = 1 - c if r & 1 else c
    return (px, py, pc), 4 * px + 2 * py + pc


def _small_gather(x_ref, out_ref, send_sems, recv_sems):
    R = x_ref.shape[0]
    mx, my, mc = _me()
    me = 4 * mx + 2 * my + mc
    mine = out_ref.at[pl.ds(pl.multiple_of(me * R, 8), R), :]
    copies = []
    for r in range(1, N_DEV):
        dev, _ = _peer(r)
        copies.append(pltpu.make_async_remote_copy(
            src_ref=x_ref, dst_ref=mine, send_sem=send_sems.at[r - 1], recv_sem=recv_sems.at[r - 1],
            device_id=dev, device_id_type=MESH))
    for cp in copies:
        cp.start()
    mine[...] = x_ref[...]
    for r in range(1, N_DEV):
        dev, idx = _peer(r)
        theirs = out_ref.at[pl.ds(pl.multiple_of(idx * R, 8), R), :]
        pltpu.make_async_remote_copy(
            src_ref=x_ref, dst_ref=theirs, send_sem=send_sems.at[r - 1], recv_sem=recv_sems.at[r - 1],
            device_id=dev, device_id_type=MESH).wait_recv()
    for cp in copies:
        cp.wait_send()


def _prologue(cs, ada_w, ada_b_cols, big):
    n = len(big)
    ncol = ada_w.shape[1]
    big_shape, big_sems = _xchg_specs(big, "gather")

    def body(cs_ref, w_ref, b_ref, *rest):
        big_in, cs_all, mod_all, big_out = rest[:n], rest[n], rest[n + 1], rest[n + 2:2 * n + 2]
        mod_scr, s1, r1, s2, r2 = rest[2 * n + 2:2 * n + 7]
        sems = rest[2 * n + 7:]
        _small_gather(cs_ref, cs_all, s1, r1)
        pick = (lax.broadcasted_iota(jnp.int32, (N_DEV, N_DEV * 8), 1)
                == 8 * lax.broadcasted_iota(jnp.int32, (N_DEV, N_DEV * 8), 0)).astype(F32)
        per_device = jnp.dot(pick, cs_all[...], preferred_element_type=F32, precision=lax.Precision.HIGHEST)
        mod_scr[...] = jnp.dot(per_device, w_ref[...], preferred_element_type=F32,
                               precision=lax.Precision.HIGHEST) + b_ref[...]
        _small_gather(mod_scr, mod_all, s2, r2)
        _xchg_start(big_in, big_out, sems, "gather")
        _xchg_wait(big_in, big_out, sems, "gather")

    vmem = pl.BlockSpec(memory_space=pltpu.VMEM)
    hbm = pl.BlockSpec(memory_space=pl.ANY)
    dma7 = pltpu.SemaphoreType.DMA((N_DEV - 1,))
    out = pl.pallas_call(
        body, name="prologue",
        out_shape=[jax.ShapeDtypeStruct((N_DEV * 8, D), F32), jax.ShapeDtypeStruct((N_DEV * 8, ncol), F32)]
        + big_shape,
        in_specs=[vmem, vmem, vmem] + [hbm] * n, out_specs=[vmem, vmem] + [hbm] * n,
        scratch_shapes=[pltpu.VMEM((8, ncol), F32), dma7, dma7, dma7, dma7] + big_sems,
        compiler_params=pltpu.CompilerParams(vmem_limit_bytes=VMEM_LIMIT),
    )(cs, ada_w, ada_b_cols, *big)
    return out[0], out[1], out[2:]


def _allgather_small(x):
    R, C = x.shape

    def body(x_ref, out_ref, send_sems, recv_sems):
        _small_gather(x_ref, out_ref, send_sems, recv_sems)

    return pl.pallas_call(
        body, name="allgather_small_%dx%d" % (R, C),
        out_shape=jax.ShapeDtypeStruct((N_DEV * R, C), F32),
        in_specs=[pl.BlockSpec(memory_space=pltpu.VMEM)], out_specs=pl.BlockSpec(memory_space=pltpu.VMEM),
        scratch_shapes=[pltpu.SemaphoreType.DMA((N_DEV - 1,)), pltpu.SemaphoreType.DMA((N_DEV - 1,))],
    )(x)


N_CHIP = N_DEV // 2


def _xchg_copies(ins, outs, sems, mode):
    send_sems, recv_sems, local_sems = sems
    mx, my, mc = _me()
    me = 4 * mx + 2 * my + mc
    my_chip = 2 * mx + my
    sibling = _peer(1)[0]

    def rdma(a, r, dev, src, slot):
        k = a * (N_DEV - 1) + r - 1
        return pltpu.make_async_remote_copy(
            src_ref=src, dst_ref=outs[a].at[slot], send_sem=send_sems.at[k], recv_sem=recv_sems.at[k],
            device_id=dev, device_id_type=MESH)

    own, sends, relays, recvs = [], [], [], []
    for a in range(len(ins)):
        if mode == "pair":
            for chip in range(N_CHIP):
                src = ins[a].at[2 * chip + 1 - mc]
                sends.append(rdma(a, chip + 1, sibling, src, chip))
                recvs.append(rdma(a, chip + 1, sibling, src, chip))
            continue
        if mode == "quad":
            own.append(pltpu.make_async_copy(ins[a].at[my_chip], outs[a].at[my_chip], local_sems.at[a]))
            for r in (2, 4, 6):
                dev, idx = _peer(r)
                chip = idx // 2
                sends.append(rdma(a, r, dev, ins[a].at[chip], my_chip))
                recvs.append(rdma(a, r, dev, ins[a].at[chip], chip))
            continue
        gather = mode == "gather"
        own.append(pltpu.make_async_copy(ins[a] if gather else ins[a].at[me], outs[a].at[me], local_sems.at[a]))
        for r in range(1, N_DEV):
            dev, idx = _peer(r)
            if not gather:
                sends.append(rdma(a, r, dev, ins[a].at[idx], me))
                recvs.append(rdma(a, r, dev, ins[a].at[idx], idx))
            elif r == 1:
                sends.append(rdma(a, r, dev, ins[a], me))
                recvs.append(rdma(a, r, dev, ins[a], idx))
            elif r % 2 == 0:
                sends.append(rdma(a, r, dev, ins[a], me))
                relays.append((rdma(a, r, dev, ins[a], idx), rdma(a, r + 1, sibling, outs[a].at[idx], idx)))
            else:
                recvs.append(rdma(a, r, sibling, ins[a], idx))
    return own, sends, relays, recvs


def _xchg_start(ins, outs, sems, mode):
    own, sends, _, _ = _xchg_copies(ins, outs, sems, mode)
    for cp in own + sends:
        cp.start()


def _xchg_wait(ins, outs, sems, mode):
    own, sends, relays, recvs = _xchg_copies(ins, outs, sems, mode)
    for arrival, relay in relays:
        arrival.wait_recv()
        relay.start()
    for cp in recvs:
        cp.wait_recv()
    for cp in own:
        cp.wait()
    for cp in sends + [relay for _, relay in relays]:
        cp.wait_send()


def _xchg_specs(arrays, mode):
    n = len(arrays)
    shape = {"gather": lambda s: (N_DEV,) + s, "scatter": lambda s: s, "pair": lambda s: (N_CHIP,) + s[1:],
             "quad": lambda s: s}[mode]
    out_shape = [jax.ShapeDtypeStruct(shape(a.shape), a.dtype) for a in arrays]
    sems = [pltpu.SemaphoreType.DMA((n * (N_DEV - 1),)), pltpu.SemaphoreType.DMA((n * (N_DEV - 1),)),
            pltpu.SemaphoreType.DMA((n,))]
    return out_shape, sems


def _exchange(arrays, mode, name):
    n = len(arrays)

    def body(*refs):
        _xchg_start(refs[:n], refs[n:2 * n], refs[2 * n:], mode)
        _xchg_wait(refs[:n], refs[n:2 * n], refs[2 * n:], mode)

    out_shape, sems = _xchg_specs(arrays, mode)
    return pl.pallas_call(
        body, name=name, out_shape=out_shape,
        in_specs=[pl.BlockSpec(memory_space=pl.ANY)] * n, out_specs=[pl.BlockSpec(memory_space=pl.ANY)] * n,
        scratch_shapes=sems,
    )(*arrays)


def _gridded(body, carry, *, name, grid, in_specs, out_specs, out_shape, scratch_shapes=(), aliases=None):
    if carry is None:
        return pl.pallas_call(
            body, name=name, grid=grid, in_specs=list(in_specs), out_specs=list(out_specs),
            out_shape=list(out_shape), scratch_shapes=list(scratch_shapes), input_output_aliases=aliases or {},
            compiler_params=_cparams(len(grid)))
    arrays, mode = carry
    n, n_in, n_out, n_scr = len(arrays), len(in_specs), len(out_specs), len(scratch_shapes)
    c_shape, c_sems = _xchg_specs(arrays, mode)

    def wrapped(*refs):
        ins, cin = refs[:n_in], refs[n_in:n_in + n]
        o0 = n_in + n
        outs, cout = refs[o0:o0 + n_out], refs[o0 + n_out:o0 + n_out + n]
        s0 = o0 + n_out + n
        scr, sems = refs[s0:s0 + n_scr], refs[s0 + n_scr:]
        first = pl.program_id(0) == 0
        last = pl.program_id(0) == grid[0] - 1
        for ax in range(1, len(grid)):
            first = first & (pl.program_id(ax) == 0)
            last = last & (pl.program_id(ax) == grid[ax] - 1)

        @pl.when(first)
        def _():
            _xchg_start(cin, cout, sems, mode)

        body(*ins, *outs, *scr)

        @pl.when(last)
        def _():
            _xchg_wait(cin, cout, sems, mode)

    hbm = pl.BlockSpec(memory_space=pl.ANY)
    res = pl.pallas_call(
        wrapped, name=name, grid=grid, in_specs=list(in_specs) + [hbm] * n, out_specs=list(out_specs) + [hbm] * n,
        out_shape=list(out_shape) + c_shape, scratch_shapes=list(scratch_shapes) + c_sems,
        input_output_aliases=aliases or {}, compiler_params=_cparams(len(grid)),
    )
    return lambda *args: res(*args, *arrays)


def _local_step(x, target, mod, small, sh, w1):
    w1_in, w1_out = w1[0].reshape(2, D_FF, D), w1[1].reshape(D_FF, D)
    (x1, a1, b1, f1, h1, h2), (wm_in,) = _ffn_fwd(x, mod, 0, small["norm_ffn1"], w1_in, w1_out, 0.5, "ffn1_fwd",
                                                  ([sh["mix_w_in"]], "gather"), nxt=(small["norm_mix"], 3))
    (p,), (wh_o, wc_o, wm_o, cw) = _mixin_fwd(
        h2, wm_in, ([sh["hgrn_w_o"], sh["conv_w_o"], sh["mix_w_out"], sh["conv_w"]], "gather"))
    wh_o, wc_o, wm_o = wh_o.reshape(D, D), wc_o.reshape(D, D), wm_o.reshape(D, D)
    cw = jnp.pad(cw.transpose(1, 0, 2).reshape(CONV_K, D), ((0, HALO - CONV_K), (0, 0)))
    (o, oa, a_all, s_all), (w2_in,) = _hgrn_fwd(p, small["hgrn_lb"], small["hgrn_g"], ([sh["ffn2_w_in"]], "gather"))
    (u1, u2), (w2_out,) = _conv_fwd(p, cw, small["conv_b"], small["conv_ln_g"], small["conv_ln_b"],
                                    ([sh["ffn2_w_out"]], "gather"))
    w2_in, w2_out = w2_in.reshape(2, D_FF, D), w2_out.reshape(D_FF, D)
    x2, ya, yb, mout = _mixout_fwd(x1, oa, u2, p, mod, 3, wh_o, wc_o, wm_o)
    (x3, a3, b3, f3, h3), _ = _ffn_fwd(x2, mod, 6, small["norm_ffn2"], w2_in, w2_out, 0.5, "ffn2_fwd", None)
    dx3, df3, sm_head = _head(x3, target, small["norm_final"], mod, 8, 0.5)

    (da3, db3, dw2_in, dw2_out), _ = _ffn_bwd_w(h3, df3, a3, b3, w2_out, "ffn2_bwd_w", None)
    rows = lambda t: t.reshape(N_DEV, -1, D).astype(MM)
    (dx2, sm3), (r2_out,) = _ffn_bwd_x(x2, dx3, f3, da3, db3, mod, 6, small["norm_ffn2"], w2_in, 0.5, "ffn2_bwd_x",
                                       ([rows(dw2_out)], "scatter"))
    dp, doa, du2, dwh_o, dwc_o, dwm_o, sm_mo = _mixout_bwd(dx2, oa, u2, ya, yb, mout, p, mod, 3, wh_o, wc_o, wm_o)
    (dp, dcw, sm_cv), (r2_in,) = _conv_bwd(p, u1, du2, cw, small["conv_ln_g"], small["conv_ln_b"], dp,
                                           ([rows(dw2_in)], "scatter"))
    (dp, sm_hg), _ = _hgrn_bwd(p, o, a_all, s_all, doa, small["hgrn_lb"], small["hgrn_g"], dp, None)
    (dx1, dwm_in, sm2, df1), (rh_o, rc_o, rm_o, rcw) = _mixin_bwd(
        x1, h2, dx2, dp, mod, 3, small["norm_mix"], wm_in, 2, 0.5,
        ([rows(dwh_o), rows(dwc_o), rows(dwm_o), dcw[:CONV_K].reshape(CONV_K, N_DEV, -1).transpose(1, 0, 2)],
         "scatter"))
    (da1, db1, dw1_in, dw1_out), (rm_in,) = _ffn_bwd_w(h1, df1, a1, b1, w1_out, "ffn1_bwd_w",
                                                      (_pair_reduce([dwm_in], "pair_mix"), "quad"))
    (dx0, sm1), (r1_in, r1_out) = _ffn_bwd_x(
        x, dx1, f1, da1, db1, mod, 0, small["norm_ffn1"], w1_in, 0.5, "ffn1_bwd_x",
        (_pair_reduce([rows(dw1_in), rows(dw1_out)], "pair_ffn1"), "quad"))

    dmod = jnp.concatenate([sm1[0:3], sm2[0:2], sm_mo[2:3], sm3[0:3]], axis=0)
    gsmall = dict(norm_ffn1=sm1[3:4], norm_mix=sm2[3:4], lb0=sm_hg[0:1], hgrn_g=sm_hg[1:2], conv_b=sm_cv[0:1],
                  conv_ln_g=sm_cv[1:2], conv_ln_b=sm_cv[2:3], norm_ffn2=sm3[3:4], norm_final=sm_head[0:1])
    recv = dict(ffn1_w_in=r1_in, ffn1_w_out=r1_out, mix_w_in=rm_in, hgrn_w_o=rh_o, conv_w=rcw, conv_w_o=rc_o,
                mix_w_out=rm_o, ffn2_w_in=r2_in, ffn2_w_out=r2_out)
    return sm_head[1, 0], dx0, dmod, gsmall, recv


def _pair_add(mine, theirs, core, name):
    _, R, C = theirs.shape

    def body(core_ref, a_ref, b_ref, out_ref):
        del core_ref
        out_ref[0] = (a_ref[0, 0].astype(F32) + b_ref[0].astype(F32)).astype(out_ref.dtype)

    blk = pl.BlockSpec((1, R, C), lambda s, core_ref: (s, 0, 0))
    grid_spec = pltpu.PrefetchScalarGridSpec(
        num_scalar_prefetch=1, grid=(N_CHIP,),
        in_specs=[pl.BlockSpec((1, 1, R, C), lambda s, core_ref: (s, core_ref[0], 0, 0)), blk], out_specs=blk)
    return pl.pallas_call(body, name=name, grid_spec=grid_spec,
                          out_shape=jax.ShapeDtypeStruct(theirs.shape, mine.dtype), compiler_params=_cparams(1),
                          )(core, mine.reshape(N_CHIP, 2, R, C), theirs)


def _pair_reduce(arrays, name):
    theirs = _exchange(arrays, "pair", name)
    core = lax.axis_index("c").astype(jnp.int32).reshape(1)
    return [_pair_add(a, t, core, "%s_add%d" % (name, i)) for i, (a, t) in enumerate(zip(arrays, theirs))]


SMALL_ORDER = ("norm_ffn1", "norm_mix", "lb0", "hgrn_g", "conv_b", "conv_ln_g", "conv_ln_b", "norm_ffn2",
               "norm_final")
PACK_ROWS = 24


def kernel(x, c, ada_w, ada_b, norm_ffn1, ffn1_w_in, ffn1_w_out, norm_mix, mix_w_in, hgrn_lb, hgrn_g, hgrn_w_o, conv_w, conv_b, conv_ln_g, conv_ln_b, conv_w_o, mix_w_out, norm_ffn2, ffn2_w_in, ffn2_w_out, norm_final, loss_target, m_ada_w, m_ada_b, m_norm_ffn1, m_ffn1_w_in, m_ffn1_w_out, m_norm_mix, m_mix_w_in, m_hgrn_lb, m_hgrn_g, m_hgrn_w_o, m_conv_w, m_conv_b, m_conv_ln_g, m_conv_ln_b, m_conv_w_o, m_mix_w_out, m_norm_ffn2, m_ffn2_w_in, m_ffn2_w_out, m_norm_final, v_ada_w, v_ada_b, v_norm_ffn1, v_ffn1_w_in, v_ffn1_w_out, v_norm_mix, v_mix_w_in, v_hgrn_lb, v_hgrn_g, v_hgrn_w_o, v_conv_w, v_conv_b, v_conv_ln_g, v_conv_ln_b, v_conv_w_o, v_mix_w_out, v_norm_ffn2, v_ffn2_w_in, v_ffn2_w_out, v_norm_final):
    mx, my, mc = _me()
    me = 4 * mx + 2 * my + mc
    ncol = ada_w.shape[2]

    sh = dict(ffn1_w_in=ffn1_w_in[0].T, ffn1_w_out=ffn1_w_out[0], mix_w_in=mix_w_in[0], hgrn_w_o=hgrn_w_o[0],
              conv_w_o=conv_w_o[0], mix_w_out=mix_w_out[0], ffn2_w_in=ffn2_w_in[0].T, ffn2_w_out=ffn2_w_out[0])
    sh = dict(zip(sh, _cast_shards(list(sh.values()))))
    sh["conv_w"] = conv_w[0]
    small = dict(norm_ffn1=norm_ffn1, norm_mix=norm_mix, hgrn_lb=hgrn_lb, hgrn_g=hgrn_g, conv_b=conv_b,
                 conv_ln_g=conv_ln_g, conv_ln_b=conv_ln_b, norm_ffn2=norm_ffn2, norm_final=norm_final.reshape(1, D))

    cs = jnp.broadcast_to(c * jax.nn.sigmoid(c), (8, D))
    ada_b_cols = lax.dynamic_slice(ada_b, (0, me * ncol), (1, ncol))
    cs_all, mod_all, w1 = _prologue(cs, ada_w[0], ada_b_cols, [sh["ffn1_w_in"], sh["ffn1_w_out"]])
    cs_all = cs_all.reshape(N_DEV, 8, D)[:, 0, :]
    mod = lax.dynamic_index_in_dim(mod_all.reshape(N_DEV, N_DEV, ncol), me, axis=1, keepdims=False).reshape(9, D)

    loss_local, dx, dmod, gsmall, recv = _local_step(x[0], loss_target[0], mod, small, sh, w1)

    n_used = 9 + len(SMALL_ORDER) + 1
    pack = jnp.concatenate([dmod] + [gsmall[n] for n in SMALL_ORDER] + [jnp.broadcast_to(loss_local, (1, D))]
                           + [jnp.zeros((PACK_ROWS - n_used, D), F32)], axis=0)
    pack_all = _allgather_small(pack).reshape(N_DEV, PACK_ROWS, D)
    tot = _sum_slots(pack_all, "sum_small", PACK_ROWS)
    loss = tot[n_used - 1, 0]
    dmod_all = pack_all[:, 0:9, :].reshape(N_DEV, 9 * D)
    dmod_cols = lax.dynamic_slice(dmod_all, (0, me * ncol), (N_DEV, ncol))

    res = {}
    res["ada_w"] = _adamw_ada_w(ada_w[0], m_ada_w[0], v_ada_w[0], cs_all, dmod_cols)
    big = dict(ffn1_w_in=(ffn1_w_in, m_ffn1_w_in, v_ffn1_w_in), ffn1_w_out=(ffn1_w_out, m_ffn1_w_out, v_ffn1_w_out),
               mix_w_in=(mix_w_in, m_mix_w_in, v_mix_w_in), hgrn_w_o=(hgrn_w_o, m_hgrn_w_o, v_hgrn_w_o),
               conv_w=(conv_w, m_conv_w, v_conv_w), conv_w_o=(conv_w_o, m_conv_w_o, v_conv_w_o),
               mix_w_out=(mix_w_out, m_mix_w_out, v_mix_w_out), ffn2_w_in=(ffn2_w_in, m_ffn2_w_in, v_ffn2_w_in),
               ffn2_w_out=(ffn2_w_out, m_ffn2_w_out, v_ffn2_w_out))
    for n, (w, m, v) in big.items():
        if n in ("ffn1_w_in", "ffn2_w_in"):
            res[n] = tuple(t.T for t in _adamw(w[0].T, m[0].T, v[0].T, recv[n], "adamw_" + n))
        else:
            res[n] = _adamw(w[0], m[0], v[0], recv[n], "adamw_" + n)
    sm_names = ("ada_b", "norm_ffn1", "norm_mix", "hgrn_lb", "hgrn_g", "conv_b", "conv_ln_g", "conv_ln_b",
                "norm_ffn2", "norm_final")
    sm_w = dict(ada_b=(ada_b, m_ada_b, v_ada_b), norm_ffn1=(norm_ffn1, m_norm_ffn1, v_norm_ffn1),
                norm_mix=(norm_mix, m_norm_mix, v_norm_mix), hgrn_lb=(hgrn_lb, m_hgrn_lb, v_hgrn_lb),
                hgrn_g=(hgrn_g, m_hgrn_g, v_hgrn_g), conv_b=(conv_b, m_conv_b, v_conv_b),
                conv_ln_g=(conv_ln_g, m_conv_ln_g, v_conv_ln_g), conv_ln_b=(conv_ln_b, m_conv_ln_b, v_conv_ln_b),
                norm_ffn2=(norm_ffn2, m_norm_ffn2, v_norm_ffn2), norm_final=(norm_final, m_norm_final, v_norm_final))
    sm_w["norm_final"] = tuple(t.reshape(1, D) for t in sm_w["norm_final"])
    grad_rows = dict({n: 9 + i for i, n in enumerate(SMALL_ORDER)}, ada_b=0, hgrn_lb=9 + SMALL_ORDER.index("lb0"))
    res.update(_adamw_small(tot, sm_names, sm_w, grad_rows))
    res["norm_final"] = tuple(t.reshape(norm_final.shape) for t in res["norm_final"])

    order = ("ada_w", "ada_b", "norm_ffn1", "ffn1_w_in", "ffn1_w_out", "norm_mix", "mix_w_in", "hgrn_lb", "hgrn_g",
             "hgrn_w_o", "conv_w", "conv_b", "conv_ln_g", "conv_ln_b", "conv_w_o", "mix_w_out", "norm_ffn2",
             "ffn2_w_in", "ffn2_w_out", "norm_final")
    lead = lambda n, t: t[None] if n in big or n == "ada_w" else t
    outs = [loss, dx[None]]
    for j in range(4):
        outs += [lead(n, res[n][j]) for n in order]
    return tuple(outs)
```

```python
import jax
import jax.numpy as jnp
from jax import lax
from jax.experimental import pallas as pl
from jax.experimental.pallas import tpu as pltpu

F32 = jnp.float32
MM = jnp.bfloat16
ACT = jnp.bfloat16

D = 1024
D_FF = 2816
HEADS = 8
HD = 128
CHUNK = 64
SUB = 16
NSUB = CHUNK // SUB
HGRN_BLOCK = 1024
SAFE_EXP = 60.0
CONV_K = 31
HALO = 32
EPS = 1e-6
N_DEV = 8
NEG = -1e30
Q_SCALE = HD ** -0.5

ADAM_LR = 0.001
ADAM_B1 = 0.9
ADAM_B2 = 0.999
ADAM_EPS = 1e-08
ADAM_WD = 0.01
ADAM_STEP = 10

V7X_VMEM_BYTES = 64 * 1024 * 1024
VMEM_LIMIT = V7X_VMEM_BYTES - 4 * 1024 * 1024
MESH = pl.DeviceIdType.MESH


def _cparams(n_axes):
    return pltpu.CompilerParams(dimension_semantics=("arbitrary",) * n_axes, vmem_limit_bytes=VMEM_LIMIT)


def _mm(a, b):
    return lax.dot_general(a.astype(MM), b.astype(MM), (((1,), (0,)), ((), ())), preferred_element_type=F32)


def _mm_nt(a, b):
    return lax.dot_general(a.astype(MM), b.astype(MM), (((1,), (1,)), ((), ())), preferred_element_type=F32)


def _mm_tn(a, b):
    return lax.dot_general(a.astype(MM), b.astype(MM), (((0,), (0,)), ((), ())), preferred_element_type=F32)


def _sig(x):
    return 1.0 / (1.0 + jnp.exp(-x))


def _colsum(x):
    return jnp.sum(x, axis=0, keepdims=True)


def _rowmean(x):
    return jnp.mean(x, axis=-1, keepdims=True)


def _modnorm_fwd(xv, g, sh, sc):
    r = lax.rsqrt(_rowmean(xv * xv) + EPS)
    xh = xv * r
    n = xh * g
    return n * (1.0 + sc) + sh, xh, n, r


def _modnorm_bwd(dh, xh, n, r, g, sc):
    dsc = _colsum(dh * n)
    dsh = _colsum(dh)
    dn = dh * (1.0 + sc)
    dg = _colsum(dn * xh)
    dxh = dn * g
    dx = r * (dxh - xh * _rowmean(dxh * xh))
    return dx, dsh, dsc, dg


def _ffn_fwd(x, mod, mo, gnorm, w_in_t, w_out, res, name, carry, nxt=None):
    T = x.shape[0]
    tm = min(512, T)
    tn = D_FF // 2

    def body(x_ref, mod_ref, g_ref, wi_ref, wo_ref, *rest):
        if nxt is None:
            xo_ref, a_ref, b_ref, f_ref, h_ref = rest
        else:
            gn_ref, xo_ref, a_ref, b_ref, f_ref, h_ref, hn_ref = rest
        xv = x_ref[...]
        h, _, _, _ = _modnorm_fwd(xv, g_ref[...], mod_ref[mo:mo + 1, :], mod_ref[mo + 1:mo + 2, :])
        h = h.astype(ACT)
        h_ref[...] = h
        f = None
        for c0 in range(0, D_FF, tn):
            a = _mm_nt(h, wi_ref[0, c0:c0 + tn, :])
            b = _mm_nt(h, wi_ref[1, c0:c0 + tn, :])
            a_ref[:, c0:c0 + tn] = a.astype(ACT)
            b_ref[:, c0:c0 + tn] = b.astype(ACT)
            part = _mm(a * _sig(a) * b, wo_ref[c0:c0 + tn, :])
            f = part if f is None else f + part
        f_ref[...] = f
        xo = xv + res * mod_ref[mo + 2:mo + 3, :] * f
        xo_ref[...] = xo
        if nxt is not None:
            hn, _, _, _ = _modnorm_fwd(xo, gn_ref[...], mod_ref[nxt[1]:nxt[1] + 1, :], mod_ref[nxt[1] + 1:nxt[1] + 2, :])
            hn_ref[...] = hn.astype(ACT)

    tile = pl.BlockSpec((tm, D), lambda i: (i, 0))
    wide = pl.BlockSpec((tm, D_FF), lambda i: (i, 0))
    row = pl.BlockSpec((1, D), lambda i: (0, 0))
    n_out = 5 if nxt is None else 6
    out = _gridded(
        body, carry, name=name, grid=(T // tm,),
        in_specs=[
            tile,
            pl.BlockSpec((9, D), lambda i: (0, 0)),
            row,
            pl.BlockSpec((2, D_FF, D), lambda i: (0, 0, 0), pipeline_mode=pl.Buffered(1)),
            pl.BlockSpec((D_FF, D), lambda i: (0, 0), pipeline_mode=pl.Buffered(1)),
        ] + ([] if nxt is None else [row]),
        out_specs=[tile, wide, wide, tile, tile] + ([] if nxt is None else [tile]),
        out_shape=[
            jax.ShapeDtypeStruct((T, D), F32),
            jax.ShapeDtypeStruct((T, D_FF), ACT),
            jax.ShapeDtypeStruct((T, D_FF), ACT),
            jax.ShapeDtypeStruct((T, D), F32),
            jax.ShapeDtypeStruct((T, D), ACT),
        ] + ([] if nxt is None else [jax.ShapeDtypeStruct((T, D), ACT)]),
    )(*((x, mod, gnorm, w_in_t, w_out) + (() if nxt is None else (nxt[0],))))
    return out[:n_out], out[n_out:]


def _ffn_bwd_w(h, df, a, b, w_out, name, carry):
    T = h.shape[0]
    tm = min(2048, T)
    ni = T // tm
    tn = 256
    nj = D_FF // tn

    def body(h_ref, df_ref, a_ref, b_ref, wo_ref, da_ref, db_ref, dwi_ref, dwo_ref, acc_i, acc_o):
        i = pl.program_id(1)

        @pl.when(i == 0)
        def _():
            acc_i[...] = jnp.zeros_like(acc_i)
            acc_o[...] = jnp.zeros_like(acc_o)

        hb = h_ref[...]
        df = df_ref[...]
        av = a_ref[...].astype(F32)
        bv = b_ref[...].astype(F32)
        sg = _sig(av)
        sa = av * sg
        s = (sa * bv).astype(MM)
        ds = _mm_nt(df, wo_ref[...])
        da = (ds * bv * sg * (1.0 + av * (1.0 - sg))).astype(MM)
        db = (ds * sa).astype(MM)
        da_ref[...] = da
        db_ref[...] = db
        acc_o[...] += _mm_tn(s, df)
        acc_i[0] += _mm_tn(da, hb)
        acc_i[1] += _mm_tn(db, hb)

        @pl.when(i == ni - 1)
        def _():
            dwi_ref[...] = acc_i[...].astype(MM)
            dwo_ref[...] = acc_o[...].astype(MM)

    out = _gridded(
        body, carry, name=name, grid=(nj, ni),
        in_specs=[
            pl.BlockSpec((tm, D), lambda j, i: (i, 0)),
            pl.BlockSpec((tm, D), lambda j, i: (i, 0)),
            pl.BlockSpec((tm, tn), lambda j, i: (i, j)),
            pl.BlockSpec((tm, tn), lambda j, i: (i, j)),
            pl.BlockSpec((tn, D), lambda j, i: (j, 0)),
        ],
        out_specs=[
            pl.BlockSpec((tm, tn), lambda j, i: (i, j)),
            pl.BlockSpec((tm, tn), lambda j, i: (i, j)),
            pl.BlockSpec((2, tn, D), lambda j, i: (0, j, 0)),
            pl.BlockSpec((tn, D), lambda j, i: (j, 0)),
        ],
        out_shape=[
            jax.ShapeDtypeStruct((T, D_FF), MM),
            jax.ShapeDtypeStruct((T, D_FF), MM),
            jax.ShapeDtypeStruct((2, D_FF, D), MM),
            jax.ShapeDtypeStruct((D_FF, D), MM),
        ],
        scratch_shapes=[pltpu.VMEM((2, tn, D), F32), pltpu.VMEM((tn, D), F32)],
    )(h, df, a, b, w_out)
    return out[:4], out[4:]


def _ffn_bwd_x(x, dxo, f, da, db, mod, mo, gnorm, w_in_t, res, name, carry):
    T = x.shape[0]
    tm = min(512, T)
    ni = T // tm
    tn = D_FF // 2
    nj = D_FF // tn

    def body(x_ref, dxo_ref, f_ref, da_ref, db_ref, mod_ref, g_ref, wi_ref, dx_ref, sm_ref, dh_scr):
        j = pl.program_id(0)
        i = pl.program_id(1)

        @pl.when((j == 0) & (i == 0))
        def _():
            sm_ref[...] = jnp.zeros_like(sm_ref)

        @pl.when(j == 0)
        def _():
            dh_scr[i] = jnp.zeros((tm, D), F32)

        dh_scr[i] += _mm(da_ref[...], wi_ref[0]) + _mm(db_ref[...], wi_ref[1])

        @pl.when(j == nj - 1)
        def _():
            sc = mod_ref[mo + 1:mo + 2, :]
            _, xh, n, r = _modnorm_fwd(x_ref[...], g_ref[...], mod_ref[mo:mo + 1, :], sc)
            dxn, dsh, dsc, dg = _modnorm_bwd(dh_scr[i], xh, n, r, g_ref[...], sc)
            dxo_v = dxo_ref[...]
            dx_ref[...] = dxo_v + dxn
            sm_ref[0:1, :] += dsh
            sm_ref[1:2, :] += dsc
            sm_ref[2:3, :] += _colsum(dxo_v * f_ref[...]) * res
            sm_ref[3:4, :] += dg

    last = pl.BlockSpec((tm, D), lambda j, i: (jnp.where(j == nj - 1, i, 0), 0))
    out = _gridded(
        body, carry, name=name, grid=(nj, ni),
        in_specs=[last, last, last,
                  pl.BlockSpec((tm, tn), lambda j, i: (i, j)), pl.BlockSpec((tm, tn), lambda j, i: (i, j)),
                  pl.BlockSpec((9, D), lambda j, i: (0, 0)), pl.BlockSpec((1, D), lambda j, i: (0, 0)),
                  pl.BlockSpec((2, tn, D), lambda j, i: (0, j, 0))],
        out_specs=[last, pl.BlockSpec((8, D), lambda j, i: (0, 0))],
        out_shape=[jax.ShapeDtypeStruct((T, D), F32), jax.ShapeDtypeStruct((8, D), F32)],
        scratch_shapes=[pltpu.VMEM((ni, tm, D), F32)],
    )(x, dxo, f, da, db, mod, gnorm, w_in_t)
    return out[:2], out[2:]


def _head(x, target, gfin, mod, gate_row, res):
    T = x.shape[0]
    tm = min(512, T)
    ni = T // tm

    def body(x_ref, t_ref, g_ref, mod_ref, dx_ref, df_ref, sm_ref):
        i = pl.program_id(0)

        @pl.when(i == 0)
        def _():
            sm_ref[...] = jnp.zeros_like(sm_ref)

        xv = x_ref[...]
        g = g_ref[...]
        r = lax.rsqrt(_rowmean(xv * xv) + EPS)
        xh = xv * r
        e = xh * g - t_ref[...]
        sm_ref[1:2, :] += _colsum(e * e) * (0.5 / D)
        dy = e * (1.0 / D)
        sm_ref[0:1, :] += _colsum(dy * xh)
        dxh = dy * g
        dx = r * (dxh - xh * _rowmean(dxh * xh))
        dx_ref[...] = dx
        df_ref[...] = (res * mod_ref[gate_row:gate_row + 1, :] * dx).astype(MM)

        @pl.when(i == ni - 1)
        def _():
            sm_ref[1:2, :] = jnp.broadcast_to(jnp.sum(sm_ref[1:2, :], axis=-1, keepdims=True), (1, D))

    tile = pl.BlockSpec((tm, D), lambda i: (i, 0))
    return pl.pallas_call(
        body, name="head_loss", grid=(ni,),
        in_specs=[tile, tile, pl.BlockSpec((1, D), lambda i: (0, 0)), pl.BlockSpec((9, D), lambda i: (0, 0))],
        out_specs=[tile, tile, pl.BlockSpec((8, D), lambda i: (0, 0))],
        out_shape=[jax.ShapeDtypeStruct((T, D), F32), jax.ShapeDtypeStruct((T, D), MM),
                   jax.ShapeDtypeStruct((8, D), F32)],
        compiler_params=_cparams(1),
    )(x, target, gfin, mod)


def _mixin_fwd(h, w, carry):
    T = h.shape[0]
    tm = min(2048, T)
    ni = T // tm

    def body(h_ref, w_ref, p_ref, h_all):
        i = pl.program_id(1)

        @pl.when(pl.program_id(0) == 0)
        def _():
            h_all[i] = h_ref[...]

        p_ref[0] = _mm(h_all[i], w_ref[0])

    first = lambda k, i: (jnp.where(k == 0, i, ni - 1), 0)
    out = _gridded(
        body, carry, name="mixin_fwd", grid=(8, ni),
        in_specs=[pl.BlockSpec((tm, D), first), pl.BlockSpec((1, D, D), lambda k, i: (k, 0, 0))],
        out_specs=[pl.BlockSpec((1, tm, D), lambda k, i: (k, i, 0))],
        out_shape=[jax.ShapeDtypeStruct((8, T, D), F32)],
        scratch_shapes=[pltpu.VMEM((ni, tm, D), ACT)],
    )(h, w)
    return out[:1], out[1:]


def _mixin_bwd(x, h, dxo, dp, mod, mo, gnorm, w, next_gate, next_res, carry):
    T = x.shape[0]
    tm = min(512, T)
    ni = T // tm

    def body(x_ref, h_ref, dxo_ref, dp_ref, mod_ref, g_ref, w_ref, dx_ref, dw_ref, sm_ref, df_ref, dh_scr, acc):
        k = pl.program_id(0)
        i = pl.program_id(1)

        @pl.when(i == 0)
        def _():
            acc[...] = jnp.zeros_like(acc)

        @pl.when(k == 0)
        def _():
            dh_scr[i] = jnp.zeros((tm, D), F32)

        @pl.when((k == 0) & (i == 0))
        def _():
            sm_ref[...] = jnp.zeros_like(sm_ref)

        dpk = dp_ref[0].astype(MM)
        acc[...] += _mm_tn(h_ref[...], dpk)
        dh_scr[i] += _mm_nt(dpk, w_ref[0])

        @pl.when(i == ni - 1)
        def _():
            dw_ref[0] = acc[...].astype(MM)

        @pl.when(k == 7)
        def _():
            sc = mod_ref[mo + 1:mo + 2, :]
            _, xh, n, r = _modnorm_fwd(x_ref[...], g_ref[...], mod_ref[mo:mo + 1, :], sc)
            dxn, dsh, dsc, dg = _modnorm_bwd(dh_scr[i], xh, n, r, g_ref[...], sc)
            dx = dxo_ref[...] + dxn
            dx_ref[...] = dx
            df_ref[...] = (next_res * mod_ref[next_gate:next_gate + 1, :] * dx).astype(MM)
            sm_ref[0:1, :] += dsh
            sm_ref[1:2, :] += dsc
            sm_ref[3:4, :] += dg

    last = pl.BlockSpec((tm, D), lambda k, i: (jnp.where(k == 7, i, 0), 0))
    out = _gridded(
        body, carry, name="mixin_bwd", grid=(8, ni),
        in_specs=[pl.BlockSpec((tm, D), lambda k, i: (jnp.where(k == 7, i, 0), 0)),
                  pl.BlockSpec((tm, D), lambda k, i: (i, 0)),
                  pl.BlockSpec((tm, D), lambda k, i: (jnp.where(k == 7, i, 0), 0)),
                  pl.BlockSpec((1, tm, D), lambda k, i: (k, i, 0)), pl.BlockSpec((9, D), lambda k, i: (0, 0)),
                  pl.BlockSpec((1, D), lambda k, i: (0, 0)), pl.BlockSpec((1, D, D), lambda k, i: (k, 0, 0))],
        out_specs=[last, pl.BlockSpec((1, D, D), lambda k, i: (k, 0, 0)), pl.BlockSpec((8, D), lambda k, i: (0, 0)),
                   last],
        out_shape=[jax.ShapeDtypeStruct((T, D), F32), jax.ShapeDtypeStruct((8, D, D), MM),
                   jax.ShapeDtypeStruct((8, D), F32), jax.ShapeDtypeStruct((T, D), MM)],
        scratch_shapes=[pltpu.VMEM((ni, tm, D), F32), pltpu.VMEM((D, D), F32)],
    )(x, h, dxo, dp, mod, gnorm, w)
    return out[:4], out[4:]


def _hgrn_consts():
    rows = jnp.arange(SUB * HD) // HD
    e = (rows[:, None] == jnp.arange(HD)[None, :]).astype(MM)
    return e, e.T


def _rows_bcast(ref, cb, first, n):
    parts = [jnp.broadcast_to(ref[pl.ds(c * CHUNK + first, 1), :], (n, HD)) for c in range(cb // CHUNK)]
    return jnp.concatenate(parts, axis=0)


def _hgrn_pre(qr, fr, lb_ref, b_scr, cb):
    z = lb_ref[...]
    lb = _sig(z[0:1, :] - z[1:2, :])
    sq = _sig(qr)
    q = qr * sq * Q_SCALE
    sf = _sig(fr)
    fg = lb + (1.0 - lb) * sf
    lf = jnp.log(fg)
    k = 1.0 - fg
    tl = lax.broadcasted_iota(jnp.int32, (cb, HD), 0) % CHUNK
    bc = lf
    sh = 1
    while sh < CHUNK:
        bc = bc + jnp.where(tl >= sh, pltpu.roll(bc, sh, 0), 0.0)
        sh *= 2
    b_scr[...] = bc
    bl = _rows_bcast(b_scr, cb, CHUNK - 1, CHUNK)
    eb = jnp.exp(bc)
    ekd = jnp.exp(bl - bc)
    ekf = jnp.exp(jnp.minimum(-bc, SAFE_EXP))
    return dict(lb=lb, sq=sq, q=q, sf=sf, fg=fg, k=k, tl=tl, b=bc, bl=bl, eb=eb, ekd=ekd, ekf=ekf,
                qe=q * eb, kd=k * ekd, kf=k * ekf, safe=jnp.max(-bc) < SAFE_EXP)


def _hgrn_pre_fused(p_ref, lb_ref, b_scr, q_scr, k_scr, qe_scr, kf_scr, kd_scr, cb):
    z = lb_ref[...]
    lb = _sig(z[0:1, :] - z[1:2, :])
    tl = lax.broadcasted_iota(jnp.int32, (CHUNK, HD), 0)

    def chunk(c, worst):
        r0 = pl.multiple_of(c * CHUNK, CHUNK)
        rs = pl.ds(r0, CHUNK)
        qr = p_ref[0, rs, :]
        q = qr * _sig(qr) * Q_SCALE
        fg = lb + (1.0 - lb) * _sig(p_ref[1, rs, :])
        k = 1.0 - fg
        bc = jnp.log(fg)
        sh = 1
        while sh < CHUNK:
            bc = bc + jnp.where(tl >= sh, pltpu.roll(bc, sh, 0), 0.0)
            sh *= 2
        b_scr[rs, :] = bc
        q_scr[rs, :] = q
        k_scr[rs, :] = k
        qe_scr[rs, :] = (q * jnp.exp(bc)).astype(MM)
        kf_scr[rs, :] = (k * jnp.exp(jnp.minimum(-bc, SAFE_EXP))).astype(MM)
        kd_scr[rs, :] = (k * jnp.exp(b_scr[pl.ds(r0 + CHUNK - 1, 1), :] - bc)).astype(MM)
        return jnp.maximum(worst, -bc)

    worst = lax.fori_loop(0, cb // CHUNK, chunk, jnp.zeros((CHUNK, HD), F32))
    return jnp.max(worst) < SAFE_EXP


def _hgrn_sub(pre, b_scr, cb):
    bc, tl, q, k = pre["b"], pre["tl"], pre["q"], pre["k"]
    br = [None] + [_rows_bcast(b_scr, cb, SUB * i - 1, CHUNK) for i in range(1, NSUB)]
    sb = tl // SUB
    bref = jnp.where(sb == 0, bc, jnp.where(sb == 1, br[1], jnp.where(sb == 2, br[2], br[3])))
    eqo = jnp.exp(bc - bref)
    eko = [None] + [jnp.exp(jnp.where(tl < SUB * i, br[i] - bc, NEG)) for i in range(1, NSUB)]
    return dict(eqo=eqo, eko=eko, qo=q * eqo, ko=[None] + [k * eko[i] for i in range(1, NSUB)])


def _pad_rows(x):
    return jnp.concatenate([x, jnp.zeros_like(x)], axis=0)


def _by_subblock(sbc, parts):
    out = jnp.zeros_like(parts[1])
    for i in range(1, NSUB):
        out = jnp.where(sbc == i, parts[i], out)
    return out


def _hgrn_fwd(p, hgrn_lb, hgrn_g, carry):
    T = p.shape[1]
    cb = min(HGRN_BLOCK, T)
    nch = cb // CHUNK
    ncb = T // cb
    e_mat, _ = _hgrn_consts()

    def body(p_ref, lb_ref, g_ref, e_ref, o_ref, oa_ref, a_ref, s_ref, st_scr, q_scr, k_scr, b_scr, z_scr, ad_scr,
             qe_scr, kf_scr, kd_scr):
        @pl.when(pl.program_id(1) == 0)
        def _():
            st_scr[...] = jnp.zeros_like(st_scr)

        safe = _hgrn_pre_fused(p_ref, lb_ref, b_scr, q_scr, k_scr, qe_scr, kf_scr, kd_scr, cb)
        chunks = [slice(c * CHUNK, (c + 1) * CHUNK) for c in range(nch)]
        row_i = lax.broadcasted_iota(jnp.int32, (CHUNK, HD), 0)
        lane_i = lax.broadcasted_iota(jnp.int32, (CHUNK, HD), 1)
        sbc = row_i // SUB
        causal = lane_i <= row_i

        @pl.when(safe)
        def _():
            for rs in chunks:
                ad_scr[rs, :] = jnp.where(causal, _mm_nt(qe_scr[rs, :], _pad_rows(kf_scr[rs, :])), 0.0)

        @pl.when(jnp.logical_not(safe))
        def _():
            tl = lax.broadcasted_iota(jnp.int32, (cb, HD), 0) % CHUNK
            sub = _hgrn_sub(dict(b=b_scr[...], tl=tl, q=q_scr[...], k=k_scr[...]), b_scr, cb)
            ti = lax.broadcasted_iota(jnp.int32, (SUB, HD), 0)

            def zbody(c, carry):
                for i in range(NSUB):
                    r0 = pl.multiple_of(c * CHUNK + SUB * i, SUB)
                    qi = q_scr[pl.ds(r0, SUB), :]
                    bi = b_scr[pl.ds(r0, SUB), :]
                    for s in range(SUB):
                        krow = k_scr[pl.ds(r0 + s, 1), :]
                        brow = b_scr[pl.ds(r0 + s, 1), :]
                        if s < 8:
                            zz = qi * krow * jnp.exp(jnp.where(ti >= s, bi - brow, NEG))
                        else:
                            lo = qi[8:] * krow * jnp.exp(jnp.where(ti[8:] >= s, bi[8:] - brow, NEG))
                            zz = jnp.concatenate([jnp.zeros((8, HD), F32), lo], axis=0)
                        z_scr[i, pl.ds(pl.multiple_of(c * SUB, SUB), SUB), s * HD:(s + 1) * HD] = zz.astype(MM)
                return carry

            lax.fori_loop(0, nch, zbody, 0)
            adiag = [_mm(z_scr[i], e_ref[...]) for i in range(NSUB)]
            offs = [[_mm_nt(sub["qo"][rs], _pad_rows(sub["ko"][i][rs])) for i in range(1, NSUB)] for rs in chunks]
            for c, rs in enumerate(chunks):
                dparts = []
                for i in range(NSUB):
                    blk = adiag[i][c * SUB:(c + 1) * SUB]
                    dparts.append(blk if i == 0 else pltpu.roll(blk, SUB * i, 1))
                ad_scr[rs, :] = _by_subblock(sbc, [None] + offs[c]) + jnp.concatenate(dparts, axis=0)

        kv = [_mm_tn(p_ref[2, rs, :], kd_scr[rs, :]) for rs in chunks]
        a_ref[0] = ad_scr[...]
        o_intra = [_mm(ad_scr[rs, :], _pad_rows(p_ref[2, rs, :])) for rs in chunks]
        states = []
        st = st_scr[...]
        for c in range(nch):
            states.append(st)
            st = st * jnp.exp(b_scr[pl.ds(c * CHUNK + CHUNK - 1, 1), :]) + kv[c]
        st_scr[...] = st
        g = g_ref[...]
        for c, rs in enumerate(chunks):
            s_ref[0, c] = states[c]
            o = o_intra[c] + _mm_nt(qe_scr[rs, :], states[c])
            o_ref[rs, :] = o
            og = p_ref[3, rs, :]
            oa_ref[rs, :] = (o * lax.rsqrt(_rowmean(o * o) + EPS) * g * og * _sig(og)).astype(ACT)

    out = _gridded(
        body, carry, name="hgrn_fwd", grid=(HEADS, ncb),
        in_specs=[pl.BlockSpec((4, cb, HD), lambda h, c: (0, c, h)),
                  pl.BlockSpec((2, HD), lambda h, c: (0, h)),
                  pl.BlockSpec((1, HD), lambda h, c: (0, h)),
                  pl.BlockSpec((SUB * HD, HD), lambda h, c: (0, 0))],
        out_specs=[pl.BlockSpec((cb, HD), lambda h, c: (c, h)),
                   pl.BlockSpec((cb, HD), lambda h, c: (c, h)),
                   pl.BlockSpec((1, cb, HD), lambda h, c: (h, c, 0)),
                   pl.BlockSpec((1, nch, HD, HD), lambda h, c: (h, c, 0, 0))],
        out_shape=[jax.ShapeDtypeStruct((T, D), F32), jax.ShapeDtypeStruct((T, D), ACT),
                   jax.ShapeDtypeStruct((HEADS, T, HD), F32),
                   jax.ShapeDtypeStruct((HEADS, T // CHUNK, HD, HD), F32)],
        scratch_shapes=[pltpu.VMEM((HD, HD), F32), pltpu.VMEM((cb, HD), F32), pltpu.VMEM((cb, HD), F32),
                        pltpu.VMEM((cb, HD), F32), pltpu.VMEM((NSUB, nch * SUB, SUB * HD), MM),
                        pltpu.VMEM((cb, HD), F32), pltpu.VMEM((cb, HD), MM), pltpu.VMEM((cb, HD), MM),
                        pltpu.VMEM((cb, HD), MM)],
    )(p, hgrn_lb, hgrn_g, e_mat)
    return out[:4], out[4:]


def _hgrn_bwd(p, o, a_all, s_all, doa, hgrn_lb, hgrn_g, dp, carry):
    T = p.shape[1]
    cb = min(HGRN_BLOCK, T)
    nch = cb // CHUNK
    ncb = T // cb
    _, et_mat = _hgrn_consts()

    def body(p_ref, o_ref, a_ref, s_ref, doa_ref, lb_ref, g_ref, et_ref, dp_in, dp_ref, sm_ref,
             dst_scr, q_scr, k_scr, b_scr, x_scr, dqd_scr, dkd_scr):
        del dp_in

        @pl.when(pl.program_id(1) == 0)
        def _():
            dst_scr[...] = jnp.zeros_like(dst_scr)
            sm_ref[...] = jnp.zeros_like(sm_ref)

        qr = p_ref[0]
        v = p_ref[2]
        og = p_ref[3]
        pre = _hgrn_pre(qr, p_ref[1], lb_ref, b_scr, cb)
        q, k = pre["q"], pre["k"]
        g = g_ref[...]
        ov = o_ref[...]
        r = lax.rsqrt(_rowmean(ov * ov) + EPS)
        oh = ov * r
        sgo = _sig(og)
        doa_v = doa_ref[...]
        don = doa_v * og * sgo
        dog = doa_v * oh * g * sgo * (1.0 + og * (1.0 - sgo))
        sm_ref[1:2, :] += _colsum(don * oh)
        doh = don * g
        do = r * (doh - oh * _rowmean(doh * oh))

        sbc = lax.broadcasted_iota(jnp.int32, (CHUNK, HD), 0) // SUB
        row_i = lax.broadcasted_iota(jnp.int32, (CHUNK, HD), 0)
        lane_i = lax.broadcasted_iota(jnp.int32, (CHUNK, HD), 1)
        causal = lane_i <= row_i
        chunks = [slice(c * CHUNK, (c + 1) * CHUNK) for c in range(nch)]
        da_parts = [jnp.where(causal, _mm_nt(do[rs], _pad_rows(v[rs])), 0.0) for rs in chunks]
        dv_parts = [_mm_tn(a_ref[0, rs, :], do[rs])[:CHUNK] for rs in chunks]

        @pl.when(pre["safe"])
        def _():
            hi = dict(preferred_element_type=F32, precision=lax.Precision.HIGH)
            for c, rs in enumerate(chunks):
                dqd_scr[rs, :] = pre["eb"][rs] * lax.dot_general(
                    da_parts[c], _pad_rows(pre["kf"][rs]), (((1,), (0,)), ((), ())), **hi)
                dkd_scr[rs, :] = pre["ekf"][rs] * lax.dot_general(
                    da_parts[c], pre["qe"][rs], (((0,), (0,)), ((), ())), **hi)[:CHUNK]

        @pl.when(jnp.logical_not(pre["safe"]))
        def _():
            sub = _hgrn_sub(pre, b_scr, cb)
            dqoff_mm = [[_mm(da_parts[c], _pad_rows(sub["ko"][i][rs])) for i in range(1, NSUB)]
                        for c, rs in enumerate(chunks)]
            dkoff_mm = [[_mm_tn(jnp.where(sbc == i, da_parts[c], 0.0), sub["qo"][rs])[:CHUNK]
                         for i in range(1, NSUB)] for c, rs in enumerate(chunks)]
            dqoff_parts = [_by_subblock(sbc, [None] + dqoff_mm[c]) for c in range(nch)]
            dkoff_parts = []
            for c, rs in enumerate(chunks):
                dko = sub["eko"][1][rs] * dkoff_mm[c][0]
                for i in range(2, NSUB):
                    dko = dko + sub["eko"][i][rs] * dkoff_mm[c][i - 1]
                dkoff_parts.append(dko)
            q_scr[...] = q
            k_scr[...] = k
            for i in range(NSUB):
                rows = []
                for c in range(nch):
                    blk = da_parts[c][SUB * i:SUB * (i + 1)]
                    rows.append(blk if i == 0 else pltpu.roll(blk, HD - SUB * i, 1))
                x_scr[i] = _mm(jnp.concatenate(rows, axis=0), et_ref[...])
            ti = lax.broadcasted_iota(jnp.int32, (SUB, HD), 0)

            def dbody(c, carry):
                for i in range(NSUB):
                    r0 = pl.multiple_of(c * CHUNK + SUB * i, SUB)
                    qi = q_scr[pl.ds(r0, SUB), :]
                    bi = b_scr[pl.ds(r0, SUB), :]
                    dq_hi = jnp.zeros((8, HD), F32)
                    dq_lo = jnp.zeros((8, HD), F32)
                    dk_hi = jnp.zeros((8, HD), F32)
                    dk_lo = jnp.zeros((8, HD), F32)
                    c0 = pl.multiple_of(c * SUB, SUB)
                    t8 = ti[:8]
                    for s in range(SUB):
                        krow = k_scr[pl.ds(r0 + s, 1), :]
                        brow = b_scr[pl.ds(r0 + s, 1), :]
                        w_lo = (x_scr[i, pl.ds(c0 + 8, 8), s * HD:(s + 1) * HD]
                                * jnp.exp(jnp.where(t8 + 8 >= s, bi[8:] - brow, NEG)))
                        dq_lo = dq_lo + w_lo * krow
                        col = _colsum(w_lo * qi[8:])
                        if s < 8:
                            w_hi = (x_scr[i, pl.ds(c0, 8), s * HD:(s + 1) * HD]
                                    * jnp.exp(jnp.where(t8 >= s, bi[:8] - brow, NEG)))
                            dq_hi = dq_hi + w_hi * krow
                            dk_hi = jnp.where(t8 == s, col + _colsum(w_hi * qi[:8]), dk_hi)
                        else:
                            dk_lo = jnp.where(t8 + 8 == s, col, dk_lo)
                    dqd_scr[pl.ds(r0, SUB), :] = jnp.concatenate([dq_hi, dq_lo], axis=0)
                    dkd_scr[pl.ds(r0, SUB), :] = jnp.concatenate([dk_hi, dk_lo], axis=0)
                return carry

            lax.fori_loop(0, nch, dbody, 0)
            dqd_scr[...] += jnp.concatenate(dqoff_parts, axis=0) * sub["eqo"]
            dkd_scr[...] += jnp.concatenate(dkoff_parts, axis=0)

        qdo = [_mm_tn(do[rs], pre["qe"][rs]) for rs in chunks]
        dsts = [None] * nch
        dst = dst_scr[...]
        for c in reversed(range(nch)):
            dsts[c] = dst
            dst = dst * jnp.exp(b_scr[pl.ds(c * CHUNK + CHUNK - 1, 1), :]) + qdo[c]
        dst_scr[...] = dst
        sts = [s_ref[0, c] for c in range(nch)]
        dqe_parts = [_mm(do[rs], sts[c]) for c, rs in enumerate(chunks)]
        dkdec_parts = [_mm(v[rs], dsts[c]) for c, rs in enumerate(chunks)]
        dvi_parts = [_mm_nt(pre["kd"][rs], dsts[c]) for c, rs in enumerate(chunks)]
        debl_parts = [_colsum(dsts[c] * sts[c]) for c in range(nch)]
        dqe = jnp.concatenate(dqe_parts, axis=0)
        dkdec = jnp.concatenate(dkdec_parts, axis=0)
        dq_tot = dqd_scr[...] + dqe * pre["eb"]
        dk_inter = dkdec * pre["ekd"]
        dk_tot = dkd_scr[...] + dk_inter
        db = q * dq_tot - k * dk_tot
        kdk = k * dk_inter
        dbl = jnp.concatenate(
            [jnp.broadcast_to(jnp.exp(b_scr[pl.ds(c * CHUNK + CHUNK - 1, 1), :]) * debl_parts[c]
                              + _colsum(kdk[c * CHUNK:(c + 1) * CHUNK]), (CHUNK, HD)) for c in range(nch)], axis=0)
        tl = pre["tl"]
        rc = db
        sh = 1
        while sh < CHUNK:
            rc = rc + jnp.where(tl + sh < CHUNK, pltpu.roll(rc, cb - sh, 0), 0.0)
            sh *= 2
        dlf = rc + dbl
        dfg = dlf / pre["fg"] - dk_tot
        sf = pre["sf"]
        lb = pre["lb"]
        sm_ref[0:1, :] += _colsum(dfg * (1.0 - sf))
        sq = pre["sq"]
        dp_ref[0] = (dq_tot * Q_SCALE * sq * (1.0 + qr * (1.0 - sq))).astype(ACT)
        dp_ref[1] = (dfg * (1.0 - lb) * sf * (1.0 - sf)).astype(ACT)
        dp_ref[2] = (jnp.concatenate(dv_parts, axis=0) + jnp.concatenate(dvi_parts, axis=0)).astype(ACT)
        dp_ref[3] = dog.astype(ACT)

    rev = lambda c: ncb - 1 - c
    out = _gridded(
        body, carry, name="hgrn_bwd", grid=(HEADS, ncb),
        in_specs=[pl.BlockSpec((4, cb, HD), lambda h, c: (0, rev(c), h)),
                  pl.BlockSpec((cb, HD), lambda h, c: (rev(c), h)),
                  pl.BlockSpec((1, cb, HD), lambda h, c: (h, rev(c), 0)),
                  pl.BlockSpec((1, nch, HD, HD), lambda h, c: (h, rev(c), 0, 0)),
                  pl.BlockSpec((cb, HD), lambda h, c: (rev(c), h)),
                  pl.BlockSpec((2, HD), lambda h, c: (0, h)),
                  pl.BlockSpec((1, HD), lambda h, c: (0, h)),
                  pl.BlockSpec((HD, SUB * HD), lambda h, c: (0, 0)),
                  pl.BlockSpec(memory_space=pl.ANY)],
        out_specs=[pl.BlockSpec((4, cb, HD), lambda h, c: (0, rev(c), h)),
                   pl.BlockSpec((8, HD), lambda h, c: (0, h))],
        out_shape=[jax.ShapeDtypeStruct(dp.shape, dp.dtype), jax.ShapeDtypeStruct((8, D), F32)],
        aliases={8: 0},
        scratch_shapes=[pltpu.VMEM((HD, HD), F32), pltpu.VMEM((cb, HD), F32), pltpu.VMEM((cb, HD), F32),
                        pltpu.VMEM((cb, HD), F32), pltpu.VMEM((NSUB, nch * SUB, SUB * HD), F32),
                        pltpu.VMEM((cb, HD), F32), pltpu.VMEM((cb, HD), F32)],
    )(p, o, a_all, s_all, doa, hgrn_lb, hgrn_g, et_mat, dp)
    return out[:2], out[2:]


def _ln_fwd(u1, g, b):
    mu = _rowmean(u1)
    xc = u1 - mu
    rs = lax.rsqrt(_rowmean(xc * xc) + EPS)
    xh = xc * rs
    return xh * g + b, xh, rs


CONV_RB = 64
LANES = 128


def _shift_rows(src, sh, ls, n):
    for r in range(1, 8):
        sh[r - 1, 0:n, :] = src[pl.ds(r, n), ls]


def _tap(src, sh, ls, off, r0, rows):
    r = off % 8
    if r == 0:
        return src[pl.ds(r0 + off, rows), ls]
    return sh[r - 1, pl.ds(r0 + off - r, rows), :]


def _conv_fwd(p, cw, cb_, lng, lnb, carry):
    T = p.shape[1]
    tm = min(512, T)
    n = HALO + tm - 8

    def body(p_ref, cw_ref, cb_ref, g_ref, b_ref, u1_ref, u2_ref, buf, sh):
        @pl.when(pl.program_id(0) == 0)
        def _():
            buf[0:HALO, :] = jnp.zeros((HALO, D), F32)

        buf[HALO:HALO + tm, :] = p_ref[0] * _sig(p_ref[1])
        for lb in range(D // LANES):
            ls = slice(lb * LANES, (lb + 1) * LANES)
            _shift_rows(buf, sh, ls, n)
            taps = [cw_ref[j:j + 1, ls] for j in range(CONV_K)]
            bias = cb_ref[:, ls]

            def rows_body(rb, carry):
                r0 = pl.multiple_of(rb * CONV_RB, CONV_RB)
                acc = jnp.broadcast_to(bias, (CONV_RB, LANES))
                for j in range(CONV_K):
                    acc = acc + taps[j] * _tap(buf, sh, ls, HALO - (CONV_K - 1) + j, r0, CONV_RB)
                u1_ref[pl.ds(r0, CONV_RB), ls] = acc
                return carry

            lax.fori_loop(0, tm // CONV_RB, rows_body, 0)
        y, _, _ = _ln_fwd(u1_ref[...], g_ref[...], b_ref[...])
        u2_ref[...] = (y * _sig(y)).astype(ACT)
        buf[0:HALO, :] = buf[tm:tm + HALO, :]

    out = _gridded(
        body, carry, name="conv_fwd", grid=(T // tm,),
        in_specs=[pl.BlockSpec((2, tm, D), lambda i: (2, i, 0)), pl.BlockSpec((HALO, D), lambda i: (0, 0)),
                  pl.BlockSpec((1, D), lambda i: (0, 0)), pl.BlockSpec((1, D), lambda i: (0, 0)),
                  pl.BlockSpec((1, D), lambda i: (0, 0))],
        out_specs=[pl.BlockSpec((tm, D), lambda i: (i, 0)), pl.BlockSpec((tm, D), lambda i: (i, 0))],
        out_shape=[jax.ShapeDtypeStruct((T, D), F32), jax.ShapeDtypeStruct((T, D), ACT)],
        scratch_shapes=[pltpu.VMEM((HALO + tm, D), F32), pltpu.VMEM((7, n, LANES), F32)],
    )(p, cw, cb_, lng, lnb)
    return out[:2], out[2:]


def _conv_bwd(p, u1, du2, cw, lng, lnb, dp, carry):
    T = p.shape[1]
    tm = min(512, T)
    ni = T // tm
    hb = tm // HALO

    n = HALO + tm - 8

    def body(p_ref, ph_ref, u1_ref, du2_ref, cw_ref, g_ref, b_ref, dp_in, dp_ref, dcw_ref, sm_ref, ubuf, dbuf,
             sh, dacc):
        del dp_in
        step = pl.program_id(0)

        @pl.when(step == 0)
        def _():
            dbuf[tm:tm + HALO, :] = jnp.zeros((HALO, D), F32)
            dcw_ref[...] = jnp.zeros_like(dcw_ref)
            sm_ref[...] = jnp.zeros_like(sm_ref)

        ua = p_ref[0]
        sgb = _sig(p_ref[1])
        halo = ph_ref[0] * _sig(ph_ref[1])
        ubuf[0:HALO, :] = jnp.where(step == ni - 1, 0.0, halo)
        ubuf[HALO:HALO + tm, :] = ua * sgb
        g = g_ref[...]
        y, xh, rs = _ln_fwd(u1_ref[...], g, b_ref[...])
        sy = _sig(y)
        dy = du2_ref[...] * sy * (1.0 + y * (1.0 - sy))
        sm_ref[1:2, :] += _colsum(dy * xh)
        sm_ref[2:3, :] += _colsum(dy)
        dxh = dy * g
        du1 = rs * (dxh - _rowmean(dxh) - xh * _rowmean(dxh * xh))
        sm_ref[0:1, :] += _colsum(du1)
        dbuf[0:tm, :] = du1
        for lb in range(D // LANES):
            ls = slice(lb * LANES, (lb + 1) * LANES)
            taps = [cw_ref[j:j + 1, ls] for j in range(CONV_K)]
            _shift_rows(dbuf, sh, ls, n)

            def du0_body(rb, carry):
                r0 = pl.multiple_of(rb * CONV_RB, CONV_RB)
                acc = jnp.zeros((CONV_RB, LANES), F32)
                for j in range(CONV_K):
                    acc = acc + taps[j] * _tap(dbuf, sh, ls, CONV_K - 1 - j, r0, CONV_RB)
                dp_ref[0, pl.ds(r0, CONV_RB), ls] = acc.astype(ACT)
                return carry

            lax.fori_loop(0, tm // CONV_RB, du0_body, 0)
            _shift_rows(ubuf, sh, ls, n)
            dacc[...] = jnp.zeros_like(dacc)

            def dcw_body(rb, carry):
                r0 = pl.multiple_of(rb * CONV_RB, CONV_RB)
                d = dbuf[pl.ds(r0, CONV_RB), ls]
                for j in range(CONV_K):
                    prod = d * _tap(ubuf, sh, ls, HALO - (CONV_K - 1) + j, r0, CONV_RB)
                    dacc[8 * j:8 * j + 8, :] += jnp.sum(prod.reshape(CONV_RB // 8, 8, LANES), axis=0)
                return carry

            lax.fori_loop(0, tm // CONV_RB, dcw_body, 0)
            for j in range(CONV_K):
                dcw_ref[j:j + 1, ls] += _colsum(dacc[8 * j:8 * j + 8, :])
        du0 = dp_ref[0].astype(F32)
        dp_ref[0] = (du0 * sgb).astype(ACT)
        dp_ref[1] = (du0 * ua * sgb * (1.0 - sgb)).astype(ACT)
        dbuf[tm:tm + HALO, :] = dbuf[0:HALO, :]

    rev = lambda i: ni - 1 - i
    out = _gridded(
        body, carry, name="conv_bwd", grid=(ni,),
        in_specs=[pl.BlockSpec((2, tm, D), lambda i: (2, rev(i), 0)),
                  pl.BlockSpec((2, HALO, D), lambda i: (2, jnp.maximum(rev(i) * hb - 1, 0), 0)),
                  pl.BlockSpec((tm, D), lambda i: (rev(i), 0)), pl.BlockSpec((tm, D), lambda i: (rev(i), 0)),
                  pl.BlockSpec((HALO, D), lambda i: (0, 0)), pl.BlockSpec((1, D), lambda i: (0, 0)),
                  pl.BlockSpec((1, D), lambda i: (0, 0)), pl.BlockSpec(memory_space=pl.ANY)],
        out_specs=[pl.BlockSpec((2, tm, D), lambda i: (2, rev(i), 0)),
                   pl.BlockSpec((HALO, D), lambda i: (0, 0)), pl.BlockSpec((8, D), lambda i: (0, 0))],
        out_shape=[jax.ShapeDtypeStruct(dp.shape, dp.dtype), jax.ShapeDtypeStruct((HALO, D), F32),
                   jax.ShapeDtypeStruct((8, D), F32)],
        aliases={7: 0},
        scratch_shapes=[pltpu.VMEM((HALO + tm, D), F32), pltpu.VMEM((tm + HALO, D), F32),
                        pltpu.VMEM((7, n, LANES), F32), pltpu.VMEM((8 * CONV_K, LANES), F32)],
    )(p, p, u1, du2, cw, lng, lnb, dp)
    return out[:3], out[3:]


def _mixout_fwd(x, oa, u2, p, mod, mo, w_a, w_b, w_o):
    T = x.shape[0]
    tm = min(512, T)

    def body(x_ref, oa_ref, u2_ref, p_ref, mod_ref, wa_ref, wb_ref, wo_ref, xo_ref, ya_ref, yb_ref, mo_ref):
        ya = _mm(oa_ref[...], wa_ref[...])
        yb = _mm(u2_ref[...], wb_ref[...])
        ya_ref[...] = ya.astype(ACT)
        yb_ref[...] = yb.astype(ACT)
        merged = _sig(p_ref[0]) * ya + _sig(p_ref[1]) * yb
        out = _mm(merged, wo_ref[...])
        mo_ref[...] = out
        xo_ref[...] = x_ref[...] + mod_ref[mo + 2:mo + 3, :] * out

    tile = pl.BlockSpec((tm, D), lambda i: (i, 0))
    wspec = pl.BlockSpec((D, D), lambda i: (0, 0))
    return pl.pallas_call(
        body, name="mixout_fwd", grid=(T // tm,),
        in_specs=[tile, tile, tile, pl.BlockSpec((2, tm, D), lambda i: (3, i, 0)),
                  pl.BlockSpec((9, D), lambda i: (0, 0)), wspec, wspec, wspec],
        out_specs=[tile, tile, tile, tile],
        out_shape=[jax.ShapeDtypeStruct((T, D), F32), jax.ShapeDtypeStruct((T, D), ACT),
                   jax.ShapeDtypeStruct((T, D), ACT), jax.ShapeDtypeStruct((T, D), F32)],
        compiler_params=_cparams(1),
    )(x, oa, u2, p, mod, w_a, w_b, w_o)


def _mixout_bwd(dxo, oa, u2, ya, yb, mout, p, mod, mo, w_a, w_b, w_o):
    T = dxo.shape[0]
    tm = min(256, T)

    def body(dxo_ref, oa_ref, u2_ref, ya_ref, yb_ref, mo_ref, p_ref, mod_ref, wa_ref, wb_ref, wo_ref,
             dp_ref, doa_ref, du2_ref, dwa_ref, dwb_ref, dwo_ref, sm_ref):
        @pl.when(pl.program_id(0) == 0)
        def _():
            dwa_ref[...] = jnp.zeros_like(dwa_ref)
            dwb_ref[...] = jnp.zeros_like(dwb_ref)
            dwo_ref[...] = jnp.zeros_like(dwo_ref)
            sm_ref[...] = jnp.zeros_like(sm_ref)

        dxo_v = dxo_ref[...]
        sm_ref[2:3, :] += _colsum(dxo_v * mo_ref[...])
        dmo = (mod_ref[mo + 2:mo + 3, :] * dxo_v).astype(MM)
        ya = ya_ref[...].astype(F32)
        yb = yb_ref[...].astype(F32)
        sga = _sig(p_ref[0])
        sgb = _sig(p_ref[1])
        merged = (sga * ya + sgb * yb).astype(MM)
        dwo_ref[...] += _mm_tn(merged, dmo)
        dmg = _mm_nt(dmo, wo_ref[...])
        dp_ref[0] = (dmg * ya * sga * (1.0 - sga)).astype(ACT)
        dp_ref[1] = (dmg * yb * sgb * (1.0 - sgb)).astype(ACT)
        dya = (dmg * sga).astype(MM)
        dyb = (dmg * sgb).astype(MM)
        dwa_ref[...] += _mm_tn(oa_ref[...], dya)
        dwb_ref[...] += _mm_tn(u2_ref[...], dyb)
        doa_ref[...] = _mm_nt(dya, wa_ref[...])
        du2_ref[...] = _mm_nt(dyb, wb_ref[...])

    tile = pl.BlockSpec((tm, D), lambda i: (i, 0))
    wspec = pl.BlockSpec((D, D), lambda i: (0, 0))
    return pl.pallas_call(
        body, name="mixout_bwd", grid=(T // tm,),
        in_specs=[tile, tile, tile, tile, tile, tile, pl.BlockSpec((2, tm, D), lambda i: (3, i, 0)),
                  pl.BlockSpec((9, D), lambda i: (0, 0)), wspec, wspec, wspec],
        out_specs=[pl.BlockSpec((2, tm, D), lambda i: (3, i, 0)), tile, tile, wspec, wspec, wspec,
                   pl.BlockSpec((8, D), lambda i: (0, 0))],
        out_shape=[jax.ShapeDtypeStruct((8, T, D), ACT), jax.ShapeDtypeStruct((T, D), F32),
                   jax.ShapeDtypeStruct((T, D), F32), jax.ShapeDtypeStruct((D, D), F32),
                   jax.ShapeDtypeStruct((D, D), F32), jax.ShapeDtypeStruct((D, D), F32),
                   jax.ShapeDtypeStruct((8, D), F32)],
        compiler_params=_cparams(1),
    )(dxo, oa, u2, ya, yb, mout, p, mod, w_a, w_b, w_o)


def _adamw_ada_w(w, m, v, cs_all, dmod_cols):
    R, C = w.shape
    tr = 256
    cs_t = jnp.pad(cs_all.T, ((0, 0), (0, HD - N_DEV)))
    dm = jnp.pad(dmod_cols, ((0, HD - N_DEV), (0, 0)))

    def body(w_ref, m_ref, v_ref, cs_ref, d_ref, go_ref, do_ref, mo_ref, vo_ref):
        gv = jnp.dot(cs_ref[...], d_ref[...], preferred_element_type=F32, precision=lax.Precision.HIGHEST)
        go_ref[...] = gv
        do_ref[...], mo_ref[...], vo_ref[...] = _adam_math(w_ref[...], gv, m_ref[...], v_ref[...])

    tile = pl.BlockSpec((tr, C), lambda i: (i, 0))
    sds = jax.ShapeDtypeStruct((R, C), F32)
    return pl.pallas_call(
        body, name="adamw_ada_w", grid=(R // tr,),
        in_specs=[tile, tile, tile, pl.BlockSpec((tr, HD), lambda i: (i, 0)), pl.BlockSpec((HD, C), lambda i: (0, 0))],
        out_specs=[tile] * 4, out_shape=[sds] * 4, compiler_params=_cparams(1))(w, m, v, cs_t, dm)


def _adam_math(w, g, m, v):
    m2 = ADAM_B1 * m + (1.0 - ADAM_B1) * g
    v2 = ADAM_B2 * v + (1.0 - ADAM_B2) * (g * g)
    m_hat = m2 / (1.0 - ADAM_B1 ** ADAM_STEP)
    v_hat = v2 / (1.0 - ADAM_B2 ** ADAM_STEP)
    delta = -ADAM_LR * (m_hat / (jnp.sqrt(v_hat) + ADAM_EPS) + ADAM_WD * w)
    return delta, m2, v2


def _adamw(w, m, v, g, name, carry=None):
    R, C = w.shape
    slots = g.ndim == 3
    n_slots = g.shape[0] if slots else 0
    tr = R
    for cand in (256, 176):
        if R % cand == 0 and R > cand:
            tr = cand
            break

    def body(w_ref, m_ref, v_ref, g_ref, go_ref, d_ref, mo_ref, vo_ref):
        if slots:
            gv = g_ref[0].astype(F32)
            for s in range(1, n_slots):
                gv = gv + g_ref[s].astype(F32)
        else:
            gv = g_ref[...]
        go_ref[...] = gv
        d_ref[...], mo_ref[...], vo_ref[...] = _adam_math(w_ref[...], gv, m_ref[...], v_ref[...])

    tile = pl.BlockSpec((tr, C), lambda i: (i, 0))
    gspec = pl.BlockSpec((n_slots, tr, C), lambda i: (0, i, 0)) if slots else tile
    sds = jax.ShapeDtypeStruct((R, C), F32)
    return _gridded(body, carry, name=name, grid=(R // tr,), in_specs=[tile, tile, tile, gspec],
                    out_specs=[tile] * 4, out_shape=[sds] * 4)(w, m, v, g)


def _adamw_small(tot, names, params, grad_rows):
    k = len(names)

    def body(tot_ref, *refs):
        ins, outs = refs[:3 * k], refs[3 * k:]
        for i, n in enumerate(names):
            w_ref, m_ref, v_ref = ins[3 * i:3 * i + 3]
            go, do, mo, vo = outs[4 * i:4 * i + 4]
            row = grad_rows[n]
            for j in range(w_ref.shape[1] // D):
                ls = slice(j * D, (j + 1) * D)
                g = tot_ref[row + j:row + j + 1, :]
                w = w_ref[:, ls]
                if n == "hgrn_lb":
                    p0 = _sig(w[0:1] - w[1:2])
                    dz0 = p0 * (1.0 - p0) * g
                    g = jnp.concatenate([dz0, -dz0], axis=0)
                go[:, ls] = g
                do[:, ls], mo[:, ls], vo[:, ls] = _adam_math(w, g, m_ref[:, ls], v_ref[:, ls])

    flat = [t for n in names for t in params[n]]
    out_shape = [jax.ShapeDtypeStruct(params[n][0].shape, F32) for n in names for _ in range(4)]
    outs = pl.pallas_call(body, name="adamw_small", out_shape=out_shape)(tot, *flat)
    return {n: tuple(outs[4 * i:4 * i + 4]) for i, n in enumerate(names)}


def _cast_shards(ws):
    n = len(ws)

    def body(*refs):
        for src, dst in zip(refs[:n], refs[n:]):
            dst[...] = src[...].astype(MM)

    return pl.pallas_call(body, name="cast_shards", out_shape=[jax.ShapeDtypeStruct(w.shape, MM) for w in ws],
                          compiler_params=pltpu.CompilerParams(vmem_limit_bytes=VMEM_LIMIT))(*ws)


def _sum_slots(pack, name, tr):
    n, R, C = pack.shape

    def body(p_ref, out_ref):
        acc = p_ref[0].astype(F32)
        for s in range(1, n):
            acc = acc + p_ref[s].astype(F32)
        out_ref[...] = acc

    return pl.pallas_call(
        body, name=name, grid=(R // tr,), in_specs=[pl.BlockSpec((n, tr, C), lambda i: (0, i, 0))],
        out_specs=pl.BlockSpec((tr, C), lambda i: (i, 0)), out_shape=jax.ShapeDtypeStruct((R, C), F32),
        compiler_params=_cparams(1))(pack)


def _me():
    return lax.axis_index("x"), lax.axis_index("y"), lax.axis_index("c")


def _peer(r):
    x, y, c = _me()
    px = 1 - x if r & 4 else x
    py = 1 - y if r & 2 else y
    pc = 1 - c if r & 1 else c
    return (px, py, pc), 4 * px + 2 * py + pc


def _small_gather(x_ref, out_ref, send_sems, recv_sems):
    R = x_ref.shape[0]
    mx, my, mc = _me()
    me = 4 * mx + 2 * my + mc
    mine = out_ref.at[pl.ds(pl.multiple_of(me * R, 8), R), :]
    copies = []
    for r in range(1, N_DEV):
        dev, _ = _peer(r)
        copies.append(pltpu.make_async_remote_copy(
            src_ref=x_ref, dst_ref=mine, send_sem=send_sems.at[r - 1], recv_sem=recv_sems.at[r - 1],
            device_id=dev, device_id_type=MESH))
    for cp in copies:
        cp.start()
    mine[...] = x_ref[...]
    for r in range(1, N_DEV):
        dev, idx = _peer(r)
        theirs = out_ref.at[pl.ds(pl.multiple_of(idx * R, 8), R), :]
        pltpu.make_async_remote_copy(
            src_ref=x_ref, dst_ref=theirs, send_sem=send_sems.at[r - 1], recv_sem=recv_sems.at[r - 1],
            device_id=dev, device_id_type=MESH).wait_recv()
    for cp in copies:
        cp.wait_send()


def _prologue(cs, ada_w, ada_b_cols, big):
    n = len(big)
    ncol = ada_w.shape[1]
    big_shape, big_sems = _xchg_specs(big, "gather")

    def body(cs_ref, w_ref, b_ref, *rest):
        big_in, cs_all, mod_all, big_out = rest[:n], rest[n], rest[n + 1], rest[n + 2:2 * n + 2]
        mod_scr, s1, r1, s2, r2 = rest[2 * n + 2:2 * n + 7]
        sems = rest[2 * n + 7:]
        _small_gather(cs_ref, cs_all, s1, r1)
        pick = (lax.broadcasted_iota(jnp.int32, (N_DEV, N_DEV * 8), 1)
                == 8 * lax.broadcasted_iota(jnp.int32, (N_DEV, N_DEV * 8), 0)).astype(F32)
        per_device = jnp.dot(pick, cs_all[...], preferred_element_type=F32, precision=lax.Precision.HIGHEST)
        mod_scr[...] = jnp.dot(per_device, w_ref[...], preferred_element_type=F32,
                               precision=lax.Precision.HIGHEST) + b_ref[...]
        _small_gather(mod_scr, mod_all, s2, r2)
        _xchg_start(big_in, big_out, sems, "gather")
        _xchg_wait(big_in, big_out, sems, "gather")

    vmem = pl.BlockSpec(memory_space=pltpu.VMEM)
    hbm = pl.BlockSpec(memory_space=pl.ANY)
    dma7 = pltpu.SemaphoreType.DMA((N_DEV - 1,))
    out = pl.pallas_call(
        body, name="prologue",
        out_shape=[jax.ShapeDtypeStruct((N_DEV * 8, D), F32), jax.ShapeDtypeStruct((N_DEV * 8, ncol), F32)]
        + big_shape,
        in_specs=[vmem, vmem, vmem] + [hbm] * n, out_specs=[vmem, vmem] + [hbm] * n,
        scratch_shapes=[pltpu.VMEM((8, ncol), F32), dma7, dma7, dma7, dma7] + big_sems,
        compiler_params=pltpu.CompilerParams(vmem_limit_bytes=VMEM_LIMIT),
    )(cs, ada_w, ada_b_cols, *big)
    return out[0], out[1], out[2:]


N_CHIP = N_DEV // 2


def _xchg_copies(ins, outs, sems, mode):
    send_sems, recv_sems, local_sems = sems
    mx, my, mc = _me()
    me = 4 * mx + 2 * my + mc
    my_chip = 2 * mx + my
    sibling = _peer(1)[0]

    def rdma(a, r, dev, src, slot):
        k = a * (N_DEV - 1) + r - 1
        return pltpu.make_async_remote_copy(
            src_ref=src, dst_ref=outs[a].at[slot], send_sem=send_sems.at[k], recv_sem=recv_sems.at[k],
            device_id=dev, device_id_type=MESH)

    own, sends, relays, recvs = [], [], [], []
    for a in range(len(ins)):
        if mode == "pair":
            for chip in range(N_CHIP):
                src = ins[a].at[2 * chip + 1 - mc]
                sends.append(rdma(a, chip + 1, sibling, src, chip))
                recvs.append(rdma(a, chip + 1, sibling, src, chip))
            continue
        if mode == "quad":
            own.append(pltpu.make_async_copy(ins[a].at[my_chip], outs[a].at[my_chip], local_sems.at[a]))
            for r in (2, 4, 6):
                dev, idx = _peer(r)
                chip = idx // 2
                sends.append(rdma(a, r, dev, ins[a].at[chip], my_chip))
                recvs.append(rdma(a, r, dev, ins[a].at[chip], chip))
            continue
        gather = mode == "gather"
        own.append(pltpu.make_async_copy(ins[a] if gather else ins[a].at[me], outs[a].at[me], local_sems.at[a]))
        for r in range(1, N_DEV):
            dev, idx = _peer(r)
            if not gather:
                sends.append(rdma(a, r, dev, ins[a].at[idx], me))
                recvs.append(rdma(a, r, dev, ins[a].at[idx], idx))
            elif r == 1:
                sends.append(rdma(a, r, dev, ins[a], me))
                recvs.append(rdma(a, r, dev, ins[a], idx))
            elif r % 2 == 0:
                sends.append(rdma(a, r, dev, ins[a], me))
                relays.append((rdma(a, r, dev, ins[a], idx), rdma(a, r + 1, sibling, outs[a].at[idx], idx)))
            else:
                recvs.append(rdma(a, r, sibling, ins[a], idx))
    return own, sends, relays, recvs


def _xchg_start(ins, outs, sems, mode):
    own, sends, _, _ = _xchg_copies(ins, outs, sems, mode)
    for cp in own + sends:
        cp.start()


def _xchg_wait(ins, outs, sems, mode):
    own, sends, relays, recvs = _xchg_copies(ins, outs, sems, mode)
    for arrival, relay in relays:
        arrival.wait_recv()
        relay.start()
    for cp in recvs:
        cp.wait_recv()
    for cp in own:
        cp.wait()
    for cp in sends + [relay for _, relay in relays]:
        cp.wait_send()


def _xchg_specs(arrays, mode):
    n = len(arrays)
    shape = {"gather": lambda s: (N_DEV,) + s, "scatter": lambda s: s, "pair": lambda s: (N_CHIP,) + s[1:],
             "quad": lambda s: s}[mode]
    out_shape = [jax.ShapeDtypeStruct(shape(a.shape), a.dtype) for a in arrays]
    sems = [pltpu.SemaphoreType.DMA((n * (N_DEV - 1),)), pltpu.SemaphoreType.DMA((n * (N_DEV - 1),)),
            pltpu.SemaphoreType.DMA((n,))]
    return out_shape, sems


def _exchange(arrays, mode, name):
    n = len(arrays)

    def body(*refs):
        _xchg_start(refs[:n], refs[n:2 * n], refs[2 * n:], mode)
        _xchg_wait(refs[:n], refs[n:2 * n], refs[2 * n:], mode)

    out_shape, sems = _xchg_specs(arrays, mode)
    return pl.pallas_call(
        body, name=name, out_shape=out_shape,
        in_specs=[pl.BlockSpec(memory_space=pl.ANY)] * n, out_specs=[pl.BlockSpec(memory_space=pl.ANY)] * n,
        scratch_shapes=sems,
    )(*arrays)


def _gridded(body, carry, *, name, grid, in_specs, out_specs, out_shape, scratch_shapes=(), aliases=None):
    if carry is None:
        return pl.pallas_call(
            body, name=name, grid=grid, in_specs=list(in_specs), out_specs=list(out_specs),
            out_shape=list(out_shape), scratch_shapes=list(scratch_shapes), input_output_aliases=aliases or {},
            compiler_params=_cparams(len(grid)))
    arrays, mode = carry
    n, n_in, n_out, n_scr = len(arrays), len(in_specs), len(out_specs), len(scratch_shapes)
    c_shape, c_sems = _xchg_specs(arrays, mode)

    def wrapped(*refs):
        ins, cin = refs[:n_in], refs[n_in:n_in + n]
        o0 = n_in + n
        outs, cout = refs[o0:o0 + n_out], refs[o0 + n_out:o0 + n_out + n]
        s0 = o0 + n_out + n
        scr, sems = refs[s0:s0 + n_scr], refs[s0 + n_scr:]
        first = pl.program_id(0) == 0
        last = pl.program_id(0) == grid[0] - 1
        for ax in range(1, len(grid)):
            first = first & (pl.program_id(ax) == 0)
            last = last & (pl.program_id(ax) == grid[ax] - 1)

        @pl.when(first)
        def _():
            _xchg_start(cin, cout, sems, mode)

        body(*ins, *outs, *scr)

        @pl.when(last)
        def _():
            _xchg_wait(cin, cout, sems, mode)

    hbm = pl.BlockSpec(memory_space=pl.ANY)
    res = pl.pallas_call(
        wrapped, name=name, grid=grid, in_specs=list(in_specs) + [hbm] * n, out_specs=list(out_specs) + [hbm] * n,
        out_shape=list(out_shape) + c_shape, scratch_shapes=list(scratch_shapes) + c_sems,
        input_output_aliases=aliases or {}, compiler_params=_cparams(len(grid)),
    )
    return lambda *args: res(*args, *arrays)


def _local_step(x, target, mod, small, sh, w1):
    w1_in, w1_out = w1[0].reshape(2, D_FF, D), w1[1].reshape(D_FF, D)
    (x1, a1, b1, f1, h1, h2), (wm_in,) = _ffn_fwd(x, mod, 0, small["norm_ffn1"], w1_in, w1_out, 0.5, "ffn1_fwd",
                                                  ([sh["mix_w_in"]], "gather"), nxt=(small["norm_mix"], 3))
    (p,), (wh_o, wc_o, wm_o, cw) = _mixin_fwd(
        h2, wm_in, ([sh["hgrn_w_o"], sh["conv_w_o"], sh["mix_w_out"], sh["conv_w"]], "gather"))
    wh_o, wc_o, wm_o = wh_o.reshape(D, D), wc_o.reshape(D, D), wm_o.reshape(D, D)
    cw = jnp.pad(cw.transpose(1, 0, 2).reshape(CONV_K, D), ((0, HALO - CONV_K), (0, 0)))
    (o, oa, a_all, s_all), (w2_in,) = _hgrn_fwd(p, small["hgrn_lb"], small["hgrn_g"], ([sh["ffn2_w_in"]], "gather"))
    (u1, u2), (w2_out,) = _conv_fwd(p, cw, small["conv_b"], small["conv_ln_g"], small["conv_ln_b"],
                                    ([sh["ffn2_w_out"]], "gather"))
    w2_in, w2_out = w2_in.reshape(2, D_FF, D), w2_out.reshape(D_FF, D)
    x2, ya, yb, mout = _mixout_fwd(x1, oa, u2, p, mod, 3, wh_o, wc_o, wm_o)
    (x3, a3, b3, f3, h3), _ = _ffn_fwd(x2, mod, 6, small["norm_ffn2"], w2_in, w2_out, 0.5, "ffn2_fwd", None)
    dx3, df3, sm_head = _head(x3, target, small["norm_final"], mod, 8, 0.5)

    (da3, db3, dw2_in, dw2_out), _ = _ffn_bwd_w(h3, df3, a3, b3, w2_out, "ffn2_bwd_w", None)
    rows = lambda t: t.reshape(N_DEV, -1, D).astype(MM)
    (dx2, sm3), (r2_out,) = _ffn_bwd_x(x2, dx3, f3, da3, db3, mod, 6, small["norm_ffn2"], w2_in, 0.5, "ffn2_bwd_x",
                                       ([rows(dw2_out)], "scatter"))
    dp, doa, du2, dwh_o, dwc_o, dwm_o, sm_mo = _mixout_bwd(dx2, oa, u2, ya, yb, mout, p, mod, 3, wh_o, wc_o, wm_o)
    (dp, dcw, sm_cv), (r2_in,) = _conv_bwd(p, u1, du2, cw, small["conv_ln_g"], small["conv_ln_b"], dp,
                                           ([rows(dw2_in)], "scatter"))
    (dp, sm_hg), _ = _hgrn_bwd(p, o, a_all, s_all, doa, small["hgrn_lb"], small["hgrn_g"], dp, None)
    (dx1, dwm_in, sm2, df1), (rh_o, rc_o, rm_o, rcw) = _mixin_bwd(
        x1, h2, dx2, dp, mod, 3, small["norm_mix"], wm_in, 2, 0.5,
        ([rows(dwh_o), rows(dwc_o), rows(dwm_o), dcw[:CONV_K].reshape(CONV_K, N_DEV, -1).transpose(1, 0, 2)],
         "scatter"))
    (da1, db1, dw1_in, dw1_out), (rm_in,) = _ffn_bwd_w(h1, df1, a1, b1, w1_out, "ffn1_bwd_w",
                                                      (_pair_reduce([dwm_in], "pair_mix"), "quad"))
    (dx0, sm1), (r1_in, r1_out) = _ffn_bwd_x(
        x, dx1, f1, da1, db1, mod, 0, small["norm_ffn1"], w1_in, 0.5, "ffn1_bwd_x",
        (_pair_reduce([rows(dw1_in), rows(dw1_out)], "pair_ffn1"), "quad"))

    dmod = jnp.concatenate([sm1[0:3], sm2[0:2], sm_mo[2:3], sm3[0:3]], axis=0)
    gsmall = dict(norm_ffn1=sm1[3:4], norm_mix=sm2[3:4], lb0=sm_hg[0:1], hgrn_g=sm_hg[1:2], conv_b=sm_cv[0:1],
                  conv_ln_g=sm_cv[1:2], conv_ln_b=sm_cv[2:3], norm_ffn2=sm3[3:4], norm_final=sm_head[0:1])
    recv = dict(ffn1_w_in=r1_in, ffn1_w_out=r1_out, mix_w_in=rm_in, hgrn_w_o=rh_o, conv_w=rcw, conv_w_o=rc_o,
                mix_w_out=rm_o, ffn2_w_in=r2_in, ffn2_w_out=r2_out)
    return sm_head[1, 0], dx0, dmod, gsmall, recv


def _pair_add(mine, theirs, core, name):
    _, R, C = theirs.shape

    def body(core_ref, a_ref, b_ref, out_ref):
        del core_ref
        out_ref[0] = (a_ref[0, 0].astype(F32) + b_ref[0].astype(F32)).astype(out_ref.dtype)

    blk = pl.BlockSpec((1, R, C), lambda s, core_ref: (s, 0, 0))
    grid_spec = pltpu.PrefetchScalarGridSpec(
        num_scalar_prefetch=1, grid=(N_CHIP,),
        in_specs=[pl.BlockSpec((1, 1, R, C), lambda s, core_ref: (s, core_ref[0], 0, 0)), blk], out_specs=blk)
    return pl.pallas_call(body, name=name, grid_spec=grid_spec,
                          out_shape=jax.ShapeDtypeStruct(theirs.shape, mine.dtype), compiler_params=_cparams(1),
                          )(core, mine.reshape(N_CHIP, 2, R, C), theirs)


def _pair_reduce(arrays, name):
    theirs = _exchange(arrays, "pair", name)
    core = lax.axis_index("c").astype(jnp.int32).reshape(1)
    return [_pair_add(a, t, core, "%s_add%d" % (name, i)) for i, (a, t) in enumerate(zip(arrays, theirs))]


SMALL_ORDER = ("norm_ffn1", "norm_mix", "lb0", "hgrn_g", "conv_b", "conv_ln_g", "conv_ln_b", "norm_ffn2",
               "norm_final")
PACK_ROWS = 24


def kernel(x, c, ada_w, ada_b, norm_ffn1, ffn1_w_in, ffn1_w_out, norm_mix, mix_w_in, hgrn_lb, hgrn_g, hgrn_w_o, conv_w, conv_b, conv_ln_g, conv_ln_b, conv_w_o, mix_w_out, norm_ffn2, ffn2_w_in, ffn2_w_out, norm_final, loss_target, m_ada_w, m_ada_b, m_norm_ffn1, m_ffn1_w_in, m_ffn1_w_out, m_norm_mix, m_mix_w_in, m_hgrn_lb, m_hgrn_g, m_hgrn_w_o, m_conv_w, m_conv_b, m_conv_ln_g, m_conv_ln_b, m_conv_w_o, m_mix_w_out, m_norm_ffn2, m_ffn2_w_in, m_ffn2_w_out, m_norm_final, v_ada_w, v_ada_b, v_norm_ffn1, v_ffn1_w_in, v_ffn1_w_out, v_norm_mix, v_mix_w_in, v_hgrn_lb, v_hgrn_g, v_hgrn_w_o, v_conv_w, v_conv_b, v_conv_ln_g, v_conv_ln_b, v_conv_w_o, v_mix_w_out, v_norm_ffn2, v_ffn2_w_in, v_ffn2_w_out, v_norm_final):
    mx, my, mc = _me()
    me = 4 * mx + 2 * my + mc
    ncol = ada_w.shape[2]

    sh = dict(ffn1_w_in=ffn1_w_in[0].T, ffn1_w_out=ffn1_w_out[0], mix_w_in=mix_w_in[0], hgrn_w_o=hgrn_w_o[0],
              conv_w_o=conv_w_o[0], mix_w_out=mix_w_out[0], ffn2_w_in=ffn2_w_in[0].T, ffn2_w_out=ffn2_w_out[0])
    sh = dict(zip(sh, _cast_shards(list(sh.values()))))
    sh["conv_w"] = conv_w[0]
    small = dict(norm_ffn1=norm_ffn1, norm_mix=norm_mix, hgrn_lb=hgrn_lb, hgrn_g=hgrn_g, conv_b=conv_b,
                 conv_ln_g=conv_ln_g, conv_ln_b=conv_ln_b, norm_ffn2=norm_ffn2, norm_final=norm_final.reshape(1, D))

    cs = jnp.broadcast_to(c * jax.nn.sigmoid(c), (8, D))
    ada_b_cols = lax.dynamic_slice(ada_b, (0, me * ncol), (1, ncol))
    cs_all, mod_all, w1 = _prologue(cs, ada_w[0], ada_b_cols, [sh["ffn1_w_in"], sh["ffn1_w_out"]])
    cs_all = cs_all.reshape(N_DEV, 8, D)[:, 0, :]
    mod = lax.dynamic_index_in_dim(mod_all.reshape(N_DEV, N_DEV, ncol), me, axis=1, keepdims=False).reshape(9, D)

    loss_local, dx, dmod, gsmall, recv = _local_step(x[0], loss_target[0], mod, small, sh, w1)

    n_used = 9 + len(SMALL_ORDER) + 1
    pack = jnp.concatenate([dmod] + [gsmall[n] for n in SMALL_ORDER] + [jnp.broadcast_to(loss_local, (1, D))]
                           + [jnp.zeros((PACK_ROWS - n_used, D), F32)], axis=0)
    res = {}
    *res["mix_w_in"], pack_all = _adamw(mix_w_in[0], m_mix_w_in[0], v_mix_w_in[0], recv["mix_w_in"],
                                        "adamw_mix_w_in", ([pack], "gather"))
    tot = _sum_slots(pack_all, "sum_small", PACK_ROWS)
    loss = tot[n_used - 1, 0]
    dmod_all = pack_all[:, 0:9, :].reshape(N_DEV, 9 * D)
    dmod_cols = lax.dynamic_slice(dmod_all, (0, me * ncol), (N_DEV, ncol))

    res["ada_w"] = _adamw_ada_w(ada_w[0], m_ada_w[0], v_ada_w[0], cs_all, dmod_cols)
    big = dict(ffn1_w_in=(ffn1_w_in, m_ffn1_w_in, v_ffn1_w_in), ffn1_w_out=(ffn1_w_out, m_ffn1_w_out, v_ffn1_w_out),
               mix_w_in=(mix_w_in, m_mix_w_in, v_mix_w_in), hgrn_w_o=(hgrn_w_o, m_hgrn_w_o, v_hgrn_w_o),
               conv_w=(conv_w, m_conv_w, v_conv_w), conv_w_o=(conv_w_o, m_conv_w_o, v_conv_w_o),
               mix_w_out=(mix_w_out, m_mix_w_out, v_mix_w_out), ffn2_w_in=(ffn2_w_in, m_ffn2_w_in, v_ffn2_w_in),
               ffn2_w_out=(ffn2_w_out, m_ffn2_w_out, v_ffn2_w_out))
    for n, (w, m, v) in big.items():
        if n in res:
            continue
        if n in ("ffn1_w_in", "ffn2_w_in"):
            res[n] = tuple(t.T for t in _adamw(w[0].T, m[0].T, v[0].T, recv[n], "adamw_" + n))
        else:
            res[n] = _adamw(w[0], m[0], v[0], recv[n], "adamw_" + n)
    sm_names = ("ada_b", "norm_ffn1", "norm_mix", "hgrn_lb", "hgrn_g", "conv_b", "conv_ln_g", "conv_ln_b",
                "norm_ffn2", "norm_final")
    sm_w = dict(ada_b=(ada_b, m_ada_b, v_ada_b), norm_ffn1=(norm_ffn1, m_norm_ffn1, v_norm_ffn1),
                norm_mix=(norm_mix, m_norm_mix, v_norm_mix), hgrn_lb=(hgrn_lb, m_hgrn_lb, v_hgrn_lb),
                hgrn_g=(hgrn_g, m_hgrn_g, v_hgrn_g), conv_b=(conv_b, m_conv_b, v_conv_b),
                conv_ln_g=(conv_ln_g, m_conv_ln_g, v_conv_ln_g), conv_ln_b=(conv_ln_b, m_conv_ln_b, v_conv_ln_b),
                norm_ffn2=(norm_ffn2, m_norm_ffn2, v_norm_ffn2), norm_final=(norm_final, m_norm_final, v_norm_final))
    sm_w["norm_final"] = tuple(t.reshape(1, D) for t in sm_w["norm_final"])
    grad_rows = dict({n: 9 + i for i, n in enumerate(SMALL_ORDER)}, ada_b=0, hgrn_lb=9 + SMALL_ORDER.index("lb0"))
    res.update(_adamw_small(tot, sm_names, sm_w, grad_rows))
    res["norm_final"] = tuple(t.reshape(norm_final.shape) for t in res["norm_final"])

    order = ("ada_w", "ada_b", "norm_ffn1", "ffn1_w_in", "ffn1_w_out", "norm_mix", "mix_w_in", "hgrn_lb", "hgrn_g",
             "hgrn_w_o", "conv_w", "conv_b", "conv_ln_g", "conv_ln_b", "conv_w_o", "mix_w_out", "norm_ffn2",
             "ffn2_w_in", "ffn2_w_out", "norm_final")
    lead = lambda n, t: t[None] if n in big or n == "ada_w" else t
    outs = [loss, dx[None]]
    for j in range(4):
        outs += [lead(n, res[n][j]) for n in order]
    return tuple(outs)
```

```python
import jax
import jax.numpy as jnp
from jax import lax
from jax.experimental import pallas as pl
from jax.experimental.pallas import tpu as pltpu

F32 = jnp.float32
MM = jnp.bfloat16
ACT = jnp.bfloat16

D = 1024
D_FF = 2816
HEADS = 8
HD = 128
CHUNK = 64
SUB = 16
NSUB = CHUNK // SUB
HGRN_BLOCK = 1024
SAFE_EXP = 60.0
CONV_K = 31
HALO = 32
EPS = 1e-6
N_DEV = 8
NEG = -1e30
Q_SCALE = HD ** -0.5

ADAM_LR = 0.001
ADAM_B1 = 0.9
ADAM_B2 = 0.999
ADAM_EPS = 1e-08
ADAM_WD = 0.01
ADAM_STEP = 10

V7X_VMEM_BYTES = 64 * 1024 * 1024
VMEM_LIMIT = V7X_VMEM_BYTES - 4 * 1024 * 1024
MESH = pl.DeviceIdType.MESH


def _cparams(n_axes):
    return pltpu.CompilerParams(dimension_semantics=("arbitrary",) * n_axes, vmem_limit_bytes=VMEM_LIMIT)


def _mm(a, b):
    return lax.dot_general(a.astype(MM), b.astype(MM), (((1,), (0,)), ((), ())), preferred_element_type=F32)


def _mm_nt(a, b):
    return lax.dot_general(a.astype(MM), b.astype(MM), (((1,), (1,)), ((), ())), preferred_element_type=F32)


def _mm_tn(a, b):
    return lax.dot_general(a.astype(MM), b.astype(MM), (((0,), (0,)), ((), ())), preferred_element_type=F32)


def _sig(x):
    return 1.0 / (1.0 + jnp.exp(-x))


def _colsum(x):
    return jnp.sum(x, axis=0, keepdims=True)


def _rowmean(x):
    return jnp.mean(x, axis=-1, keepdims=True)


def _modnorm_fwd(xv, g, sh, sc):
    r = lax.rsqrt(_rowmean(xv * xv) + EPS)
    xh = xv * r
    n = xh * g
    return n * (1.0 + sc) + sh, xh, n, r


def _modnorm_bwd(dh, xh, n, r, g, sc):
    dsc = _colsum(dh * n)
    dsh = _colsum(dh)
    dn = dh * (1.0 + sc)
    dg = _colsum(dn * xh)
    dxh = dn * g
    dx = r * (dxh - xh * _rowmean(dxh * xh))
    return dx, dsh, dsc, dg


def _ffn_fwd(x, mod, mo, gnorm, w_in_t, w_out, res, name, carry, nxt=None):
    T = x.shape[0]
    tm = min(512, T)
    tn = D_FF // 2

    def body(x_ref, mod_ref, g_ref, wi_ref, wo_ref, *rest):
        if nxt is None:
            xo_ref, a_ref, b_ref, f_ref, h_ref = rest
        else:
            gn_ref, xo_ref, a_ref, b_ref, f_ref, h_ref, hn_ref = rest
        xv = x_ref[...]
        h, _, _, _ = _modnorm_fwd(xv, g_ref[...], mod_ref[mo:mo + 1, :], mod_ref[mo + 1:mo + 2, :])
        h = h.astype(ACT)
        h_ref[...] = h
        f = None
        for c0 in range(0, D_FF, tn):
            a = _mm_nt(h, wi_ref[0, c0:c0 + tn, :])
            b = _mm_nt(h, wi_ref[1, c0:c0 + tn, :])
            a_ref[:, c0:c0 + tn] = a.astype(ACT)
            b_ref[:, c0:c0 + tn] = b.astype(ACT)
            part = _mm(a * _sig(a) * b, wo_ref[c0:c0 + tn, :])
            f = part if f is None else f + part
        f_ref[...] = f
        xo = xv + res * mod_ref[mo + 2:mo + 3, :] * f
        xo_ref[...] = xo
        if nxt is not None:
            hn, _, _, _ = _modnorm_fwd(xo, gn_ref[...], mod_ref[nxt[1]:nxt[1] + 1, :], mod_ref[nxt[1] + 1:nxt[1] + 2, :])
            hn_ref[...] = hn.astype(ACT)

    tile = pl.BlockSpec((tm, D), lambda i: (i, 0))
    wide = pl.BlockSpec((tm, D_FF), lambda i: (i, 0))
    row = pl.BlockSpec((1, D), lambda i: (0, 0))
    n_out = 5 if nxt is None else 6
    out = _gridded(
        body, carry, name=name, grid=(T // tm,),
        in_specs=[
            tile,
            pl.BlockSpec((9, D), lambda i: (0, 0)),
            row,
            pl.BlockSpec((2, D_FF, D), lambda i: (0, 0, 0), pipeline_mode=pl.Buffered(1)),
            pl.BlockSpec((D_FF, D), lambda i: (0, 0), pipeline_mode=pl.Buffered(1)),
        ] + ([] if nxt is None else [row]),
        out_specs=[tile, wide, wide, tile, tile] + ([] if nxt is None else [tile]),
        out_shape=[
            jax.ShapeDtypeStruct((T, D), F32),
            jax.ShapeDtypeStruct((T, D_FF), ACT),
            jax.ShapeDtypeStruct((T, D_FF), ACT),
            jax.ShapeDtypeStruct((T, D), F32),
            jax.ShapeDtypeStruct((T, D), ACT),
        ] + ([] if nxt is None else [jax.ShapeDtypeStruct((T, D), ACT)]),
    )(*((x, mod, gnorm, w_in_t, w_out) + (() if nxt is None else (nxt[0],))))
    return out[:n_out], out[n_out:]


def _ffn_bwd_w(h, df, a, b, w_out, name, carry):
    T = h.shape[0]
    tm = min(2048, T)
    ni = T // tm
    tn = 256
    nj = D_FF // tn

    def body(h_ref, df_ref, a_ref, b_ref, wo_ref, da_ref, db_ref, dwi_ref, dwo_ref, acc_i, acc_o):
        i = pl.program_id(1)

        @pl.when(i == 0)
        def _():
            acc_i[...] = jnp.zeros_like(acc_i)
            acc_o[...] = jnp.zeros_like(acc_o)

        hb = h_ref[...]
        df = df_ref[...]
        av = a_ref[...].astype(F32)
        bv = b_ref[...].astype(F32)
        sg = _sig(av)
        sa = av * sg
        s = (sa * bv).astype(MM)
        ds = _mm_nt(df, wo_ref[...])
        da = (ds * bv * sg * (1.0 + av * (1.0 - sg))).astype(MM)
        db = (ds * sa).astype(MM)
        da_ref[...] = da
        db_ref[...] = db
        acc_o[...] += _mm_tn(s, df)
        acc_i[0] += _mm_tn(da, hb)
        acc_i[1] += _mm_tn(db, hb)

        @pl.when(i == ni - 1)
        def _():
            dwi_ref[...] = acc_i[...].astype(MM)
            dwo_ref[...] = acc_o[...].astype(MM)

    out = _gridded(
        body, carry, name=name, grid=(nj, ni),
        in_specs=[
            pl.BlockSpec((tm, D), lambda j, i: (i, 0)),
            pl.BlockSpec((tm, D), lambda j, i: (i, 0)),
            pl.BlockSpec((tm, tn), lambda j, i: (i, j)),
            pl.BlockSpec((tm, tn), lambda j, i: (i, j)),
            pl.BlockSpec((tn, D), lambda j, i: (j, 0)),
        ],
        out_specs=[
            pl.BlockSpec((tm, tn), lambda j, i: (i, j)),
            pl.BlockSpec((tm, tn), lambda j, i: (i, j)),
            pl.BlockSpec((2, tn, D), lambda j, i: (0, j, 0)),
            pl.BlockSpec((tn, D), lambda j, i: (j, 0)),
        ],
        out_shape=[
            jax.ShapeDtypeStruct((T, D_FF), MM),
            jax.ShapeDtypeStruct((T, D_FF), MM),
            jax.ShapeDtypeStruct((2, D_FF, D), MM),
            jax.ShapeDtypeStruct((D_FF, D), MM),
        ],
        scratch_shapes=[pltpu.VMEM((2, tn, D), F32), pltpu.VMEM((tn, D), F32)],
    )(h, df, a, b, w_out)
    return out[:4], out[4:]


def _ffn_bwd_x(x, dxo, f, da, db, mod, mo, gnorm, w_in_t, res, name, carry):
    T = x.shape[0]
    tm = min(512, T)
    ni = T // tm
    tn = D_FF // 2
    nj = D_FF // tn

    def body(x_ref, dxo_ref, f_ref, da_ref, db_ref, mod_ref, g_ref, wi_ref, dx_ref, sm_ref, dh_scr):
        j = pl.program_id(0)
        i = pl.program_id(1)

        @pl.when((j == 0) & (i == 0))
        def _():
            sm_ref[...] = jnp.zeros_like(sm_ref)

        @pl.when(j == 0)
        def _():
            dh_scr[i] = jnp.zeros((tm, D), F32)

        dh_scr[i] += _mm(da_ref[...], wi_ref[0]) + _mm(db_ref[...], wi_ref[1])

        @pl.when(j == nj - 1)
        def _():
            sc = mod_ref[mo + 1:mo + 2, :]
            _, xh, n, r = _modnorm_fwd(x_ref[...], g_ref[...], mod_ref[mo:mo + 1, :], sc)
            dxn, dsh, dsc, dg = _modnorm_bwd(dh_scr[i], xh, n, r, g_ref[...], sc)
            dxo_v = dxo_ref[...]
            dx_ref[...] = dxo_v + dxn
            sm_ref[0:1, :] += dsh
            sm_ref[1:2, :] += dsc
            sm_ref[2:3, :] += _colsum(dxo_v * f_ref[...]) * res
            sm_ref[3:4, :] += dg

    last = pl.BlockSpec((tm, D), lambda j, i: (jnp.where(j == nj - 1, i, 0), 0))
    out = _gridded(
        body, carry, name=name, grid=(nj, ni),
        in_specs=[last, last, last,
                  pl.BlockSpec((tm, tn), lambda j, i: (i, j)), pl.BlockSpec((tm, tn), lambda j, i: (i, j)),
                  pl.BlockSpec((9, D), lambda j, i: (0, 0)), pl.BlockSpec((1, D), lambda j, i: (0, 0)),
                  pl.BlockSpec((2, tn, D), lambda j, i: (0, j, 0))],
        out_specs=[last, pl.BlockSpec((8, D), lambda j, i: (0, 0))],
        out_shape=[jax.ShapeDtypeStruct((T, D), F32), jax.ShapeDtypeStruct((8, D), F32)],
        scratch_shapes=[pltpu.VMEM((ni, tm, D), F32)],
    )(x, dxo, f, da, db, mod, gnorm, w_in_t)
    return out[:2], out[2:]


def _head(x, target, gfin, mod, gate_row, res):
    T = x.shape[0]
    tm = min(512, T)
    ni = T // tm

    def body(x_ref, t_ref, g_ref, mod_ref, dx_ref, df_ref, sm_ref):
        i = pl.program_id(0)

        @pl.when(i == 0)
        def _():
            sm_ref[...] = jnp.zeros_like(sm_ref)

        xv = x_ref[...]
        g = g_ref[...]
        r = lax.rsqrt(_rowmean(xv * xv) + EPS)
        xh = xv * r
        e = xh * g - t_ref[...]
        sm_ref[1:2, :] += _colsum(e * e) * (0.5 / D)
        dy = e * (1.0 / D)
        sm_ref[0:1, :] += _colsum(dy * xh)
        dxh = dy * g
        dx = r * (dxh - xh * _rowmean(dxh * xh))
        dx_ref[...] = dx
        df_ref[...] = (res * mod_ref[gate_row:gate_row + 1, :] * dx).astype(MM)

        @pl.when(i == ni - 1)
        def _():
            sm_ref[1:2, :] = jnp.broadcast_to(jnp.sum(sm_ref[1:2, :], axis=-1, keepdims=True), (1, D))

    tile = pl.BlockSpec((tm, D), lambda i: (i, 0))
    return pl.pallas_call(
        body, name="head_loss", grid=(ni,),
        in_specs=[tile, tile, pl.BlockSpec((1, D), lambda i: (0, 0)), pl.BlockSpec((9, D), lambda i: (0, 0))],
        out_specs=[tile, tile, pl.BlockSpec((8, D), lambda i: (0, 0))],
        out_shape=[jax.ShapeDtypeStruct((T, D), F32), jax.ShapeDtypeStruct((T, D), MM),
                   jax.ShapeDtypeStruct((8, D), F32)],
        compiler_params=_cparams(1),
    )(x, target, gfin, mod)


def _mixin_fwd(h, w, carry):
    T = h.shape[0]
    tm = min(2048, T)
    ni = T // tm

    def body(h_ref, w_ref, p_ref, h_all):
        i = pl.program_id(1)

        @pl.when(pl.program_id(0) == 0)
        def _():
            h_all[i] = h_ref[...]

        p_ref[0] = _mm(h_all[i], w_ref[0])

    first = lambda k, i: (jnp.where(k == 0, i, ni - 1), 0)
    out = _gridded(
        body, carry, name="mixin_fwd", grid=(8, ni),
        in_specs=[pl.BlockSpec((tm, D), first), pl.BlockSpec((1, D, D), lambda k, i: (k, 0, 0))],
        out_specs=[pl.BlockSpec((1, tm, D), lambda k, i: (k, i, 0))],
        out_shape=[jax.ShapeDtypeStruct((8, T, D), F32)],
        scratch_shapes=[pltpu.VMEM((ni, tm, D), ACT)],
    )(h, w)
    return out[:1], out[1:]


def _mixin_bwd(x, h, dxo, dp, mod, mo, gnorm, w, next_gate, next_res, carry):
    T = x.shape[0]
    tm = min(512, T)
    ni = T // tm

    def body(x_ref, h_ref, dxo_ref, dp_ref, mod_ref, g_ref, w_ref, dx_ref, dw_ref, sm_ref, df_ref, dh_scr, acc):
        k = pl.program_id(0)
        i = pl.program_id(1)

        @pl.when(i == 0)
        def _():
            acc[...] = jnp.zeros_like(acc)

        @pl.when(k == 0)
        def _():
            dh_scr[i] = jnp.zeros((tm, D), F32)

        @pl.when((k == 0) & (i == 0))
        def _():
            sm_ref[...] = jnp.zeros_like(sm_ref)

        dpk = dp_ref[0].astype(MM)
        acc[...] += _mm_tn(h_ref[...], dpk)
        dh_scr[i] += _mm_nt(dpk, w_ref[0])

        @pl.when(i == ni - 1)
        def _():
            dw_ref[0] = acc[...].astype(MM)

        @pl.when(k == 7)
        def _():
            sc = mod_ref[mo + 1:mo + 2, :]
            _, xh, n, r = _modnorm_fwd(x_ref[...], g_ref[...], mod_ref[mo:mo + 1, :], sc)
            dxn, dsh, dsc, dg = _modnorm_bwd(dh_scr[i], xh, n, r, g_ref[...], sc)
            dx = dxo_ref[...] + dxn
            dx_ref[...] = dx
            df_ref[...] = (next_res * mod_ref[next_gate:next_gate + 1, :] * dx).astype(MM)
            sm_ref[0:1, :] += dsh
            sm_ref[1:2, :] += dsc
            sm_ref[3:4, :] += dg

    last = pl.BlockSpec((tm, D), lambda k, i: (jnp.where(k == 7, i, 0), 0))
    out = _gridded(
        body, carry, name="mixin_bwd", grid=(8, ni),
        in_specs=[pl.BlockSpec((tm, D), lambda k, i: (jnp.where(k == 7, i, 0), 0)),
                  pl.BlockSpec((tm, D), lambda k, i: (i, 0)),
                  pl.BlockSpec((tm, D), lambda k, i: (jnp.where(k == 7, i, 0), 0)),
                  pl.BlockSpec((1, tm, D), lambda k, i: (k, i, 0)), pl.BlockSpec((9, D), lambda k, i: (0, 0)),
                  pl.BlockSpec((1, D), lambda k, i: (0, 0)), pl.BlockSpec((1, D, D), lambda k, i: (k, 0, 0))],
        out_specs=[last, pl.BlockSpec((1, D, D), lambda k, i: (k, 0, 0)), pl.BlockSpec((8, D), lambda k, i: (0, 0)),
                   last],
        out_shape=[jax.ShapeDtypeStruct((T, D), F32), jax.ShapeDtypeStruct((8, D, D), MM),
                   jax.ShapeDtypeStruct((8, D), F32), jax.ShapeDtypeStruct((T, D), MM)],
        scratch_shapes=[pltpu.VMEM((ni, tm, D), F32), pltpu.VMEM((D, D), F32)],
    )(x, h, dxo, dp, mod, gnorm, w)
    return out[:4], out[4:]


def _hgrn_consts():
    rows = jnp.arange(SUB * HD) // HD
    e = (rows[:, None] == jnp.arange(HD)[None, :]).astype(MM)
    return e, e.T


def _rows_bcast(ref, cb, first, n):
    parts = [jnp.broadcast_to(ref[pl.ds(c * CHUNK + first, 1), :], (n, HD)) for c in range(cb // CHUNK)]
    return jnp.concatenate(parts, axis=0)


def _hgrn_pre(qr, fr, lb_ref, b_scr, cb):
    z = lb_ref[...]
    lb = _sig(z[0:1, :] - z[1:2, :])
    sq = _sig(qr)
    q = qr * sq * Q_SCALE
    sf = _sig(fr)
    fg = lb + (1.0 - lb) * sf
    lf = jnp.log(fg)
    k = 1.0 - fg
    tl = lax.broadcasted_iota(jnp.int32, (cb, HD), 0) % CHUNK
    bc = lf
    sh = 1
    while sh < CHUNK:
        bc = bc + jnp.where(tl >= sh, pltpu.roll(bc, sh, 0), 0.0)
        sh *= 2
    b_scr[...] = bc
    bl = _rows_bcast(b_scr, cb, CHUNK - 1, CHUNK)
    eb = jnp.exp(bc)
    ekd = jnp.exp(bl - bc)
    ekf = jnp.exp(jnp.minimum(-bc, SAFE_EXP))
    return dict(lb=lb, sq=sq, q=q, sf=sf, fg=fg, k=k, tl=tl, b=bc, bl=bl, eb=eb, ekd=ekd, ekf=ekf,
                qe=q * eb, kd=k * ekd, kf=k * ekf, safe=jnp.max(-bc) < SAFE_EXP)


def _hgrn_pre_fused(p_ref, lb_ref, b_scr, q_scr, k_scr, qe_scr, kf_scr, kd_scr, cb):
    z = lb_ref[...]
    lb = _sig(z[0:1, :] - z[1:2, :])
    tl = lax.broadcasted_iota(jnp.int32, (CHUNK, HD), 0)

    def chunk(c, worst):
        r0 = pl.multiple_of(c * CHUNK, CHUNK)
        rs = pl.ds(r0, CHUNK)
        qr = p_ref[0, rs, :]
        q = qr * _sig(qr) * Q_SCALE
        fg = lb + (1.0 - lb) * _sig(p_ref[1, rs, :])
        k = 1.0 - fg
        bc = jnp.log(fg)
        sh = 1
        while sh < CHUNK:
            bc = bc + jnp.where(tl >= sh, pltpu.roll(bc, sh, 0), 0.0)
            sh *= 2
        b_scr[rs, :] = bc
        q_scr[rs, :] = q
        k_scr[rs, :] = k
        qe_scr[rs, :] = (q * jnp.exp(bc)).astype(MM)
        kf_scr[rs, :] = (k * jnp.exp(jnp.minimum(-bc, SAFE_EXP))).astype(MM)
        kd_scr[rs, :] = (k * jnp.exp(b_scr[pl.ds(r0 + CHUNK - 1, 1), :] - bc)).astype(MM)
        return jnp.maximum(worst, -bc)

    worst = lax.fori_loop(0, cb // CHUNK, chunk, jnp.zeros((CHUNK, HD), F32))
    return jnp.max(worst) < SAFE_EXP


def _hgrn_sub(pre, b_scr, cb):
    bc, tl, q, k = pre["b"], pre["tl"], pre["q"], pre["k"]
    br = [None] + [_rows_bcast(b_scr, cb, SUB * i - 1, CHUNK) for i in range(1, NSUB)]
    sb = tl // SUB
    bref = jnp.where(sb == 0, bc, jnp.where(sb == 1, br[1], jnp.where(sb == 2, br[2], br[3])))
    eqo = jnp.exp(bc - bref)
    eko = [None] + [jnp.exp(jnp.where(tl < SUB * i, br[i] - bc, NEG)) for i in range(1, NSUB)]
    return dict(eqo=eqo, eko=eko, qo=q * eqo, ko=[None] + [k * eko[i] for i in range(1, NSUB)])


def _pad_rows(x):
    return jnp.concatenate([x, jnp.zeros_like(x)], axis=0)


def _by_subblock(sbc, parts):
    out = jnp.zeros_like(parts[1])
    for i in range(1, NSUB):
        out = jnp.where(sbc == i, parts[i], out)
    return out


def _hgrn_fwd(p, hgrn_lb, hgrn_g, carry):
    T = p.shape[1]
    cb = min(HGRN_BLOCK, T)
    nch = cb // CHUNK
    ncb = T // cb
    e_mat, _ = _hgrn_consts()

    def body(p_ref, lb_ref, g_ref, e_ref, o_ref, oa_ref, a_ref, s_ref, st_scr, q_scr, k_scr, b_scr, z_scr, ad_scr,
             qe_scr, kf_scr, kd_scr):
        @pl.when(pl.program_id(1) == 0)
        def _():
            st_scr[...] = jnp.zeros_like(st_scr)

        safe = _hgrn_pre_fused(p_ref, lb_ref, b_scr, q_scr, k_scr, qe_scr, kf_scr, kd_scr, cb)
        chunks = [slice(c * CHUNK, (c + 1) * CHUNK) for c in range(nch)]
        row_i = lax.broadcasted_iota(jnp.int32, (CHUNK, HD), 0)
        lane_i = lax.broadcasted_iota(jnp.int32, (CHUNK, HD), 1)
        sbc = row_i // SUB
        causal = lane_i <= row_i

        @pl.when(safe)
        def _():
            for rs in chunks:
                ad_scr[rs, :] = jnp.where(causal, _mm_nt(qe_scr[rs, :], _pad_rows(kf_scr[rs, :])), 0.0)

        @pl.when(jnp.logical_not(safe))
        def _():
            tl = lax.broadcasted_iota(jnp.int32, (cb, HD), 0) % CHUNK
            sub = _hgrn_sub(dict(b=b_scr[...], tl=tl, q=q_scr[...], k=k_scr[...]), b_scr, cb)
            ti = lax.broadcasted_iota(jnp.int32, (SUB, HD), 0)

            def zbody(c, carry):
                for i in range(NSUB):
                    r0 = pl.multiple_of(c * CHUNK + SUB * i, SUB)
                    qi = q_scr[pl.ds(r0, SUB), :]
                    bi = b_scr[pl.ds(r0, SUB), :]
                    for s in range(SUB):
                        krow = k_scr[pl.ds(r0 + s, 1), :]
                        brow = b_scr[pl.ds(r0 + s, 1), :]
                        if s < 8:
                            zz = qi * krow * jnp.exp(jnp.where(ti >= s, bi - brow, NEG))
                        else:
                            lo = qi[8:] * krow * jnp.exp(jnp.where(ti[8:] >= s, bi[8:] - brow, NEG))
                            zz = jnp.concatenate([jnp.zeros((8, HD), F32), lo], axis=0)
                        z_scr[i, pl.ds(pl.multiple_of(c * SUB, SUB), SUB), s * HD:(s + 1) * HD] = zz.astype(MM)
                return carry

            lax.fori_loop(0, nch, zbody, 0)
            adiag = [_mm(z_scr[i], e_ref[...]) for i in range(NSUB)]
            offs = [[_mm_nt(sub["qo"][rs], _pad_rows(sub["ko"][i][rs])) for i in range(1, NSUB)] for rs in chunks]
            for c, rs in enumerate(chunks):
                dparts = []
                for i in range(NSUB):
                    blk = adiag[i][c * SUB:(c + 1) * SUB]
                    dparts.append(blk if i == 0 else pltpu.roll(blk, SUB * i, 1))
                ad_scr[rs, :] = _by_subblock(sbc, [None] + offs[c]) + jnp.concatenate(dparts, axis=0)

        kv = [_mm_tn(p_ref[2, rs, :], kd_scr[rs, :]) for rs in chunks]
        a_ref[0] = ad_scr[...]
        o_intra = [_mm(ad_scr[rs, :], _pad_rows(p_ref[2, rs, :])) for rs in chunks]
        states = []
        st = st_scr[...]
        for c in range(nch):
            states.append(st)
            st = st * jnp.exp(b_scr[pl.ds(c * CHUNK + CHUNK - 1, 1), :]) + kv[c]
        st_scr[...] = st
        g = g_ref[...]
        for c, rs in enumerate(chunks):
            s_ref[0, c] = states[c]
            o = o_intra[c] + _mm_nt(qe_scr[rs, :], states[c])
            o_ref[rs, :] = o
            og = p_ref[3, rs, :]
            oa_ref[rs, :] = (o * lax.rsqrt(_rowmean(o * o) + EPS) * g * og * _sig(og)).astype(ACT)

    out = _gridded(
        body, carry, name="hgrn_fwd", grid=(HEADS, ncb),
        in_specs=[pl.BlockSpec((4, cb, HD), lambda h, c: (0, c, h)),
                  pl.BlockSpec((2, HD), lambda h, c: (0, h)),
                  pl.BlockSpec((1, HD), lambda h, c: (0, h)),
                  pl.BlockSpec((SUB * HD, HD), lambda h, c: (0, 0))],
        out_specs=[pl.BlockSpec((cb, HD), lambda h, c: (c, h)),
                   pl.BlockSpec((cb, HD), lambda h, c: (c, h)),
                   pl.BlockSpec((1, cb, HD), lambda h, c: (h, c, 0)),
                   pl.BlockSpec((1, nch, HD, HD), lambda h, c: (h, c, 0, 0))],
        out_shape=[jax.ShapeDtypeStruct((T, D), F32), jax.ShapeDtypeStruct((T, D), ACT),
                   jax.ShapeDtypeStruct((HEADS, T, HD), F32),
                   jax.ShapeDtypeStruct((HEADS, T // CHUNK, HD, HD), F32)],
        scratch_shapes=[pltpu.VMEM((HD, HD), F32), pltpu.VMEM((cb, HD), F32), pltpu.VMEM((cb, HD), F32),
                        pltpu.VMEM((cb, HD), F32), pltpu.VMEM((NSUB, nch * SUB, SUB * HD), MM),
                        pltpu.VMEM((cb, HD), F32), pltpu.VMEM((cb, HD), MM), pltpu.VMEM((cb, HD), MM),
                        pltpu.VMEM((cb, HD), MM)],
    )(p, hgrn_lb, hgrn_g, e_mat)
    return out[:4], out[4:]


def _hgrn_bwd(p, o, a_all, s_all, doa, hgrn_lb, hgrn_g, dp, carry):
    T = p.shape[1]
    cb = min(HGRN_BLOCK, T)
    nch = cb // CHUNK
    ncb = T // cb
    _, et_mat = _hgrn_consts()

    def body(p_ref, o_ref, a_ref, s_ref, doa_ref, lb_ref, g_ref, et_ref, dp_in, dp_ref, sm_ref,
             dst_scr, q_scr, k_scr, b_scr, x_scr, dqd_scr, dkd_scr):
        del dp_in

        @pl.when(pl.program_id(1) == 0)
        def _():
            dst_scr[...] = jnp.zeros_like(dst_scr)
            sm_ref[...] = jnp.zeros_like(sm_ref)

        qr = p_ref[0]
        v = p_ref[2]
        og = p_ref[3]
        pre = _hgrn_pre(qr, p_ref[1], lb_ref, b_scr, cb)
        q, k = pre["q"], pre["k"]
        g = g_ref[...]
        ov = o_ref[...]
        r = lax.rsqrt(_rowmean(ov * ov) + EPS)
        oh = ov * r
        sgo = _sig(og)
        doa_v = doa_ref[...]
        don = doa_v * og * sgo
        dog = doa_v * oh * g * sgo * (1.0 + og * (1.0 - sgo))
        sm_ref[1:2, :] += _colsum(don * oh)
        doh = don * g
        do = r * (doh - oh * _rowmean(doh * oh))

        sbc = lax.broadcasted_iota(jnp.int32, (CHUNK, HD), 0) // SUB
        row_i = lax.broadcasted_iota(jnp.int32, (CHUNK, HD), 0)
        lane_i = lax.broadcasted_iota(jnp.int32, (CHUNK, HD), 1)
        causal = lane_i <= row_i
        chunks = [slice(c * CHUNK, (c + 1) * CHUNK) for c in range(nch)]
        da_parts = [jnp.where(causal, _mm_nt(do[rs], _pad_rows(v[rs])), 0.0) for rs in chunks]
        dv_parts = [_mm_tn(a_ref[0, rs, :], do[rs])[:CHUNK] for rs in chunks]

        @pl.when(pre["safe"])
        def _():
            hi = dict(preferred_element_type=F32, precision=lax.Precision.HIGH)
            for c, rs in enumerate(chunks):
                dqd_scr[rs, :] = pre["eb"][rs] * lax.dot_general(
                    da_parts[c], _pad_rows(pre["kf"][rs]), (((1,), (0,)), ((), ())), **hi)
                dkd_scr[rs, :] = pre["ekf"][rs] * lax.dot_general(
                    da_parts[c], pre["qe"][rs], (((0,), (0,)), ((), ())), **hi)[:CHUNK]

        @pl.when(jnp.logical_not(pre["safe"]))
        def _():
            sub = _hgrn_sub(pre, b_scr, cb)
            dqoff_mm = [[_mm(da_parts[c], _pad_rows(sub["ko"][i][rs])) for i in range(1, NSUB)]
                        for c, rs in enumerate(chunks)]
            dkoff_mm = [[_mm_tn(jnp.where(sbc == i, da_parts[c], 0.0), sub["qo"][rs])[:CHUNK]
                         for i in range(1, NSUB)] for c, rs in enumerate(chunks)]
            dqoff_parts = [_by_subblock(sbc, [None] + dqoff_mm[c]) for c in range(nch)]
            dkoff_parts = []
            for c, rs in enumerate(chunks):
                dko = sub["eko"][1][rs] * dkoff_mm[c][0]
                for i in range(2, NSUB):
                    dko = dko + sub["eko"][i][rs] * dkoff_mm[c][i - 1]
                dkoff_parts.append(dko)
            q_scr[...] = q
            k_scr[...] = k
            for i in range(NSUB):
                rows = []
                for c in range(nch):
                    blk = da_parts[c][SUB * i:SUB * (i + 1)]
                    rows.append(blk if i == 0 else pltpu.roll(blk, HD - SUB * i, 1))
                x_scr[i] = _mm(jnp.concatenate(rows, axis=0), et_ref[...])
            ti = lax.broadcasted_iota(jnp.int32, (SUB, HD), 0)

            def dbody(c, carry):
                for i in range(NSUB):
                    r0 = pl.multiple_of(c * CHUNK + SUB * i, SUB)
                    qi = q_scr[pl.ds(r0, SUB), :]
                    bi = b_scr[pl.ds(r0, SUB), :]
                    dq_hi = jnp.zeros((8, HD), F32)
                    dq_lo = jnp.zeros((8, HD), F32)
                    dk_hi = jnp.zeros((8, HD), F32)
                    dk_lo = jnp.zeros((8, HD), F32)
                    c0 = pl.multiple_of(c * SUB, SUB)
                    t8 = ti[:8]
                    for s in range(SUB):
                        krow = k_scr[pl.ds(r0 + s, 1), :]
                        brow = b_scr[pl.ds(r0 + s, 1), :]
                        w_lo = (x_scr[i, pl.ds(c0 + 8, 8), s * HD:(s + 1) * HD]
                                * jnp.exp(jnp.where(t8 + 8 >= s, bi[8:] - brow, NEG)))
                        dq_lo = dq_lo + w_lo * krow
                        col = _colsum(w_lo * qi[8:])
                        if s < 8:
                            w_hi = (x_scr[i, pl.ds(c0, 8), s * HD:(s + 1) * HD]
                                    * jnp.exp(jnp.where(t8 >= s, bi[:8] - brow, NEG)))
                            dq_hi = dq_hi + w_hi * krow
                            dk_hi = jnp.where(t8 == s, col + _colsum(w_hi * qi[:8]), dk_hi)
                        else:
                            dk_lo = jnp.where(t8 + 8 == s, col, dk_lo)
                    dqd_scr[pl.ds(r0, SUB), :] = jnp.concatenate([dq_hi, dq_lo], axis=0)
                    dkd_scr[pl.ds(r0, SUB), :] = jnp.concatenate([dk_hi, dk_lo], axis=0)
                return carry

            lax.fori_loop(0, nch, dbody, 0)
            dqd_scr[...] += jnp.concatenate(dqoff_parts, axis=0) * sub["eqo"]
            dkd_scr[...] += jnp.concatenate(dkoff_parts, axis=0)

        qdo = [_mm_tn(do[rs], pre["qe"][rs]) for rs in chunks]
        dsts = [None] * nch
        dst = dst_scr[...]
        for c in reversed(range(nch)):
            dsts[c] = dst
            dst = dst * jnp.exp(b_scr[pl.ds(c * CHUNK + CHUNK - 1, 1), :]) + qdo[c]
        dst_scr[...] = dst
        sts = [s_ref[0, c] for c in range(nch)]
        dqe_parts = [_mm(do[rs], sts[c]) for c, rs in enumerate(chunks)]
        dkdec_parts = [_mm(v[rs], dsts[c]) for c, rs in enumerate(chunks)]
        dvi_parts = [_mm_nt(pre["kd"][rs], dsts[c]) for c, rs in enumerate(chunks)]
        debl_parts = [_colsum(dsts[c] * sts[c]) for c in range(nch)]
        dqe = jnp.concatenate(dqe_parts, axis=0)
        dkdec = jnp.concatenate(dkdec_parts, axis=0)
        dq_tot = dqd_scr[...] + dqe * pre["eb"]
        dk_inter = dkdec * pre["ekd"]
        dk_tot = dkd_scr[...] + dk_inter
        db = q * dq_tot - k * dk_tot
        kdk = k * dk_inter
        dbl = jnp.concatenate(
            [jnp.broadcast_to(jnp.exp(b_scr[pl.ds(c * CHUNK + CHUNK - 1, 1), :]) * debl_parts[c]
                              + _colsum(kdk[c * CHUNK:(c + 1) * CHUNK]), (CHUNK, HD)) for c in range(nch)], axis=0)
        tl = pre["tl"]
        rc = db
        sh = 1
        while sh < CHUNK:
            rc = rc + jnp.where(tl + sh < CHUNK, pltpu.roll(rc, cb - sh, 0), 0.0)
            sh *= 2
        dlf = rc + dbl
        dfg = dlf / pre["fg"] - dk_tot
        sf = pre["sf"]
        lb = pre["lb"]
        sm_ref[0:1, :] += _colsum(dfg * (1.0 - sf))
        sq = pre["sq"]
        dp_ref[0] = (dq_tot * Q_SCALE * sq * (1.0 + qr * (1.0 - sq))).astype(ACT)
        dp_ref[1] = (dfg * (1.0 - lb) * sf * (1.0 - sf)).astype(ACT)
        dp_ref[2] = (jnp.concatenate(dv_parts, axis=0) + jnp.concatenate(dvi_parts, axis=0)).astype(ACT)
        dp_ref[3] = dog.astype(ACT)

    rev = lambda c: ncb - 1 - c
    out = _gridded(
        body, carry, name="hgrn_bwd", grid=(HEADS, ncb),
        in_specs=[pl.BlockSpec((4, cb, HD), lambda h, c: (0, rev(c), h)),
                  pl.BlockSpec((cb, HD), lambda h, c: (rev(c), h)),
                  pl.BlockSpec((1, cb, HD), lambda h, c: (h, rev(c), 0)),
                  pl.BlockSpec((1, nch, HD, HD), lambda h, c: (h, rev(c), 0, 0)),
                  pl.BlockSpec((cb, HD), lambda h, c: (rev(c), h)),
                  pl.BlockSpec((2, HD), lambda h, c: (0, h)),
                  pl.BlockSpec((1, HD), lambda h, c: (0, h)),
                  pl.BlockSpec((HD, SUB * HD), lambda h, c: (0, 0)),
                  pl.BlockSpec(memory_space=pl.ANY)],
        out_specs=[pl.BlockSpec((4, cb, HD), lambda h, c: (0, rev(c), h)),
                   pl.BlockSpec((8, HD), lambda h, c: (0, h))],
        out_shape=[jax.ShapeDtypeStruct(dp.shape, dp.dtype), jax.ShapeDtypeStruct((8, D), F32)],
        aliases={8: 0},
        scratch_shapes=[pltpu.VMEM((HD, HD), F32), pltpu.VMEM((cb, HD), F32), pltpu.VMEM((cb, HD), F32),
                        pltpu.VMEM((cb, HD), F32), pltpu.VMEM((NSUB, nch * SUB, SUB * HD), F32),
                        pltpu.VMEM((cb, HD), F32), pltpu.VMEM((cb, HD), F32)],
    )(p, o, a_all, s_all, doa, hgrn_lb, hgrn_g, et_mat, dp)
    return out[:2], out[2:]


def _ln_fwd(u1, g, b):
    mu = _rowmean(u1)
    xc = u1 - mu
    rs = lax.rsqrt(_rowmean(xc * xc) + EPS)
    xh = xc * rs
    return xh * g + b, xh, rs


CONV_RB = 64
LANES = 128


def _shift_rows(src, sh, ls, n):
    for r in range(1, 8):
        sh[r - 1, 0:n, :] = src[pl.ds(r, n), ls]


def _tap(src, sh, ls, off, r0, rows):
    r = off % 8
    if r == 0:
        return src[pl.ds(r0 + off, rows), ls]
    return sh[r - 1, pl.ds(r0 + off - r, rows), :]


def _conv_fwd(p, cw, cb_, lng, lnb, carry):
    T = p.shape[1]
    tm = min(512, T)
    n = HALO + tm - 8

    def body(p_ref, cw_ref, cb_ref, g_ref, b_ref, u1_ref, u2_ref, buf, sh):
        @pl.when(pl.program_id(0) == 0)
        def _():
            buf[0:HALO, :] = jnp.zeros((HALO, D), F32)

        buf[HALO:HALO + tm, :] = p_ref[0] * _sig(p_ref[1])
        for lb in range(D // LANES):
            ls = slice(lb * LANES, (lb + 1) * LANES)
            _shift_rows(buf, sh, ls, n)
            taps = [cw_ref[j:j + 1, ls] for j in range(CONV_K)]
            bias = cb_ref[:, ls]

            def rows_body(rb, carry):
                r0 = pl.multiple_of(rb * CONV_RB, CONV_RB)
                acc = jnp.broadcast_to(bias, (CONV_RB, LANES))
                for j in range(CONV_K):
                    acc = acc + taps[j] * _tap(buf, sh, ls, HALO - (CONV_K - 1) + j, r0, CONV_RB)
                u1_ref[pl.ds(r0, CONV_RB), ls] = acc
                return carry

            lax.fori_loop(0, tm // CONV_RB, rows_body, 0)
        y, _, _ = _ln_fwd(u1_ref[...], g_ref[...], b_ref[...])
        u2_ref[...] = (y * _sig(y)).astype(ACT)
        buf[0:HALO, :] = buf[tm:tm + HALO, :]

    out = _gridded(
        body, carry, name="conv_fwd", grid=(T // tm,),
        in_specs=[pl.BlockSpec((2, tm, D), lambda i: (2, i, 0)), pl.BlockSpec((HALO, D), lambda i: (0, 0)),
                  pl.BlockSpec((1, D), lambda i: (0, 0)), pl.BlockSpec((1, D), lambda i: (0, 0)),
                  pl.BlockSpec((1, D), lambda i: (0, 0))],
        out_specs=[pl.BlockSpec((tm, D), lambda i: (i, 0)), pl.BlockSpec((tm, D), lambda i: (i, 0))],
        out_shape=[jax.ShapeDtypeStruct((T, D), F32), jax.ShapeDtypeStruct((T, D), ACT)],
        scratch_shapes=[pltpu.VMEM((HALO + tm, D), F32), pltpu.VMEM((7, n, LANES), F32)],
    )(p, cw, cb_, lng, lnb)
    return out[:2], out[2:]


def _conv_bwd(p, u1, du2, cw, lng, lnb, dp, carry):
    T = p.shape[1]
    tm = min(512, T)
    ni = T // tm
    hb = tm // HALO

    n = HALO + tm - 8

    def body(p_ref, ph_ref, u1_ref, du2_ref, cw_ref, g_ref, b_ref, dp_in, dp_ref, dcw_ref, sm_ref, ubuf, dbuf,
             sh, dacc):
        del dp_in
        step = pl.program_id(0)

        @pl.when(step == 0)
        def _():
            dbuf[tm:tm + HALO, :] = jnp.zeros((HALO, D), F32)
            dcw_ref[...] = jnp.zeros_like(dcw_ref)
            sm_ref[...] = jnp.zeros_like(sm_ref)

        ua = p_ref[0]
        sgb = _sig(p_ref[1])
        halo = ph_ref[0] * _sig(ph_ref[1])
        ubuf[0:HALO, :] = jnp.where(step == ni - 1, 0.0, halo)
        ubuf[HALO:HALO + tm, :] = ua * sgb
        g = g_ref[...]
        y, xh, rs = _ln_fwd(u1_ref[...], g, b_ref[...])
        sy = _sig(y)
        dy = du2_ref[...] * sy * (1.0 + y * (1.0 - sy))
        sm_ref[1:2, :] += _colsum(dy * xh)
        sm_ref[2:3, :] += _colsum(dy)
        dxh = dy * g
        du1 = rs * (dxh - _rowmean(dxh) - xh * _rowmean(dxh * xh))
        sm_ref[0:1, :] += _colsum(du1)
        dbuf[0:tm, :] = du1
        for lb in range(D // LANES):
            ls = slice(lb * LANES, (lb + 1) * LANES)
            taps = [cw_ref[j:j + 1, ls] for j in range(CONV_K)]
            _shift_rows(dbuf, sh, ls, n)

            def du0_body(rb, carry):
                r0 = pl.multiple_of(rb * CONV_RB, CONV_RB)
                acc = jnp.zeros((CONV_RB, LANES), F32)
                for j in range(CONV_K):
                    acc = acc + taps[j] * _tap(dbuf, sh, ls, CONV_K - 1 - j, r0, CONV_RB)
                dp_ref[0, pl.ds(r0, CONV_RB), ls] = acc.astype(ACT)
                return carry

            lax.fori_loop(0, tm // CONV_RB, du0_body, 0)
            _shift_rows(ubuf, sh, ls, n)
            dacc[...] = jnp.zeros_like(dacc)

            def dcw_body(rb, carry):
                r0 = pl.multiple_of(rb * CONV_RB, CONV_RB)
                d = dbuf[pl.ds(r0, CONV_RB), ls]
                for j in range(CONV_K):
                    prod = d * _tap(ubuf, sh, ls, HALO - (CONV_K - 1) + j, r0, CONV_RB)
                    dacc[8 * j:8 * j + 8, :] += jnp.sum(prod.reshape(CONV_RB // 8, 8, LANES), axis=0)
                return carry

            lax.fori_loop(0, tm // CONV_RB, dcw_body, 0)
            for j in range(CONV_K):
                dcw_ref[j:j + 1, ls] += _colsum(dacc[8 * j:8 * j + 8, :])
        du0 = dp_ref[0].astype(F32)
        dp_ref[0] = (du0 * sgb).astype(ACT)
        dp_ref[1] = (du0 * ua * sgb * (1.0 - sgb)).astype(ACT)
        dbuf[tm:tm + HALO, :] = dbuf[0:HALO, :]

    rev = lambda i: ni - 1 - i
    out = _gridded(
        body, carry, name="conv_bwd", grid=(ni,),
        in_specs=[pl.BlockSpec((2, tm, D), lambda i: (2, rev(i), 0)),
                  pl.BlockSpec((2, HALO, D), lambda i: (2, jnp.maximum(rev(i) * hb - 1, 0), 0)),
                  pl.BlockSpec((tm, D), lambda i: (rev(i), 0)), pl.BlockSpec((tm, D), lambda i: (rev(i), 0)),
                  pl.BlockSpec((HALO, D), lambda i: (0, 0)), pl.BlockSpec((1, D), lambda i: (0, 0)),
                  pl.BlockSpec((1, D), lambda i: (0, 0)), pl.BlockSpec(memory_space=pl.ANY)],
        out_specs=[pl.BlockSpec((2, tm, D), lambda i: (2, rev(i), 0)),
                   pl.BlockSpec((HALO, D), lambda i: (0, 0)), pl.BlockSpec((8, D), lambda i: (0, 0))],
        out_shape=[jax.ShapeDtypeStruct(dp.shape, dp.dtype), jax.ShapeDtypeStruct((HALO, D), F32),
                   jax.ShapeDtypeStruct((8, D), F32)],
        aliases={7: 0},
        scratch_shapes=[pltpu.VMEM((HALO + tm, D), F32), pltpu.VMEM((tm + HALO, D), F32),
                        pltpu.VMEM((7, n, LANES), F32), pltpu.VMEM((8 * CONV_K, LANES), F32)],
    )(p, p, u1, du2, cw, lng, lnb, dp)
    return out[:3], out[3:]


def _mixout_fwd(x, oa, u2, p, mod, mo, w_a, w_b, w_o):
    T = x.shape[0]
    tm = min(512, T)

    def body(x_ref, oa_ref, u2_ref, p_ref, mod_ref, wa_ref, wb_ref, wo_ref, xo_ref, ya_ref, yb_ref, mo_ref):
        ya = _mm(oa_ref[...], wa_ref[...])
        yb = _mm(u2_ref[...], wb_ref[...])
        ya_ref[...] = ya.astype(ACT)
        yb_ref[...] = yb.astype(ACT)
        merged = _sig(p_ref[0]) * ya + _sig(p_ref[1]) * yb
        out = _mm(merged, wo_ref[...])
        mo_ref[...] = out
        xo_ref[...] = x_ref[...] + mod_ref[mo + 2:mo + 3, :] * out

    tile = pl.BlockSpec((tm, D), lambda i: (i, 0))
    wspec = pl.BlockSpec((D, D), lambda i: (0, 0))
    return pl.pallas_call(
        body, name="mixout_fwd", grid=(T // tm,),
        in_specs=[tile, tile, tile, pl.BlockSpec((2, tm, D), lambda i: (3, i, 0)),
                  pl.BlockSpec((9, D), lambda i: (0, 0)), wspec, wspec, wspec],
        out_specs=[tile, tile, tile, tile],
        out_shape=[jax.ShapeDtypeStruct((T, D), F32), jax.ShapeDtypeStruct((T, D), ACT),
                   jax.ShapeDtypeStruct((T, D), ACT), jax.ShapeDtypeStruct((T, D), F32)],
        compiler_params=_cparams(1),
    )(x, oa, u2, p, mod, w_a, w_b, w_o)


def _mixout_bwd(dxo, oa, u2, ya, yb, mout, p, mod, mo, w_a, w_b, w_o):
    T = dxo.shape[0]
    tm = min(256, T)

    def body(dxo_ref, oa_ref, u2_ref, ya_ref, yb_ref, mo_ref, p_ref, mod_ref, wa_ref, wb_ref, wo_ref,
             dp_ref, doa_ref, du2_ref, dwa_ref, dwb_ref, dwo_ref, sm_ref):
        @pl.when(pl.program_id(0) == 0)
        def _():
            dwa_ref[...] = jnp.zeros_like(dwa_ref)
            dwb_ref[...] = jnp.zeros_like(dwb_ref)
            dwo_ref[...] = jnp.zeros_like(dwo_ref)
            sm_ref[...] = jnp.zeros_like(sm_ref)

        dxo_v = dxo_ref[...]
        sm_ref[2:3, :] += _colsum(dxo_v * mo_ref[...])
        dmo = (mod_ref[mo + 2:mo + 3, :] * dxo_v).astype(MM)
        ya = ya_ref[...].astype(F32)
        yb = yb_ref[...].astype(F32)
        sga = _sig(p_ref[0])
        sgb = _sig(p_ref[1])
        merged = (sga * ya + sgb * yb).astype(MM)
        dwo_ref[...] += _mm_tn(merged, dmo)
        dmg = _mm_nt(dmo, wo_ref[...])
        dp_ref[0] = (dmg * ya * sga * (1.0 - sga)).astype(ACT)
        dp_ref[1] = (dmg * yb * sgb * (1.0 - sgb)).astype(ACT)
        dya = (dmg * sga).astype(MM)
        dyb = (dmg * sgb).astype(MM)
        dwa_ref[...] += _mm_tn(oa_ref[...], dya)
        dwb_ref[...] += _mm_tn(u2_ref[...], dyb)
        doa_ref[...] = _mm_nt(dya, wa_ref[...])
        du2_ref[...] = _mm_nt(dyb, wb_ref[...])

    tile = pl.BlockSpec((tm, D), lambda i: (i, 0))
    wspec = pl.BlockSpec((D, D), lambda i: (0, 0))
    return pl.pallas_call(
        body, name="mixout_bwd", grid=(T // tm,),
        in_specs=[tile, tile, tile, tile, tile, tile, pl.BlockSpec((2, tm, D), lambda i: (3, i, 0)),
                  pl.BlockSpec((9, D), lambda i: (0, 0)), wspec, wspec, wspec],
        out_specs=[pl.BlockSpec((2, tm, D), lambda i: (3, i, 0)), tile, tile, wspec, wspec, wspec,
                   pl.BlockSpec((8, D), lambda i: (0, 0))],
        out_shape=[jax.ShapeDtypeStruct((8, T, D), ACT), jax.ShapeDtypeStruct((T, D), F32),
                   jax.ShapeDtypeStruct((T, D), F32), jax.ShapeDtypeStruct((D, D), F32),
                   jax.ShapeDtypeStruct((D, D), F32), jax.ShapeDtypeStruct((D, D), F32),
                   jax.ShapeDtypeStruct((8, D), F32)],
        compiler_params=_cparams(1),
    )(dxo, oa, u2, ya, yb, mout, p, mod, w_a, w_b, w_o)


def _adamw_ada_w(w, m, v, cs_all, dmod_cols):
    R, C = w.shape
    tr = 256
    cs_t = jnp.pad(cs_all.T, ((0, 0), (0, HD - N_DEV)))
    dm = jnp.pad(dmod_cols, ((0, HD - N_DEV), (0, 0)))

    def body(w_ref, m_ref, v_ref, cs_ref, d_ref, go_ref, do_ref, mo_ref, vo_ref):
        gv = jnp.dot(cs_ref[...], d_ref[...], preferred_element_type=F32, precision=lax.Precision.HIGHEST)
        go_ref[...] = gv
        do_ref[...], mo_ref[...], vo_ref[...] = _adam_math(w_ref[...], gv, m_ref[...], v_ref[...])

    tile = pl.BlockSpec((tr, C), lambda i: (i, 0))
    sds = jax.ShapeDtypeStruct((R, C), F32)
    return pl.pallas_call(
        body, name="adamw_ada_w", grid=(R // tr,),
        in_specs=[tile, tile, tile, pl.BlockSpec((tr, HD), lambda i: (i, 0)), pl.BlockSpec((HD, C), lambda i: (0, 0))],
        out_specs=[tile] * 4, out_shape=[sds] * 4, compiler_params=_cparams(1))(w, m, v, cs_t, dm)


def _adam_math(w, g, m, v):
    m2 = ADAM_B1 * m + (1.0 - ADAM_B1) * g
    v2 = ADAM_B2 * v + (1.0 - ADAM_B2) * (g * g)
    m_hat = m2 / (1.0 - ADAM_B1 ** ADAM_STEP)
    v_hat = v2 / (1.0 - ADAM_B2 ** ADAM_STEP)
    delta = -ADAM_LR * (m_hat / (jnp.sqrt(v_hat) + ADAM_EPS) + ADAM_WD * w)
    return delta, m2, v2


def _adamw(w, m, v, g, name, carry=None):
    R, C = w.shape
    slots = g.ndim == 3
    n_slots = g.shape[0] if slots else 0
    tr = R
    for cand in (256, 176):
        if R % cand == 0 and R > cand:
            tr = cand
            break

    def body(w_ref, m_ref, v_ref, g_ref, go_ref, d_ref, mo_ref, vo_ref):
        if slots:
            gv = g_ref[0].astype(F32)
            for s in range(1, n_slots):
                gv = gv + g_ref[s].astype(F32)
        else:
            gv = g_ref[...]
        go_ref[...] = gv
        d_ref[...], mo_ref[...], vo_ref[...] = _adam_math(w_ref[...], gv, m_ref[...], v_ref[...])

    tile = pl.BlockSpec((tr, C), lambda i: (i, 0))
    gspec = pl.BlockSpec((n_slots, tr, C), lambda i: (0, i, 0)) if slots else tile
    sds = jax.ShapeDtypeStruct((R, C), F32)
    return _gridded(body, carry, name=name, grid=(R // tr,), in_specs=[tile, tile, tile, gspec],
                    out_specs=[tile] * 4, out_shape=[sds] * 4)(w, m, v, g)


def _adamw_small(tot, names, params, grad_rows):
    k = len(names)

    def body(tot_ref, *refs):
        ins, outs = refs[:3 * k], refs[3 * k:]
        for i, n in enumerate(names):
            w_ref, m_ref, v_ref = ins[3 * i:3 * i + 3]
            go, do, mo, vo = outs[4 * i:4 * i + 4]
            row = grad_rows[n]
            for j in range(w_ref.shape[1] // D):
                ls = slice(j * D, (j + 1) * D)
                g = tot_ref[row + j:row + j + 1, :]
                w = w_ref[:, ls]
                if n == "hgrn_lb":
                    p0 = _sig(w[0:1] - w[1:2])
                    dz0 = p0 * (1.0 - p0) * g
                    g = jnp.concatenate([dz0, -dz0], axis=0)
                go[:, ls] = g
                do[:, ls], mo[:, ls], vo[:, ls] = _adam_math(w, g, m_ref[:, ls], v_ref[:, ls])

    flat = [t for n in names for t in params[n]]
    out_shape = [jax.ShapeDtypeStruct(params[n][0].shape, F32) for n in names for _ in range(4)]
    outs = pl.pallas_call(body, name="adamw_small", out_shape=out_shape)(tot, *flat)
    return {n: tuple(outs[4 * i:4 * i + 4]) for i, n in enumerate(names)}


def _cast_shards(ws):
    n = len(ws)

    def body(*refs):
        for src, dst in zip(refs[:n], refs[n:]):
            dst[...] = src[...].astype(MM)

    return pl.pallas_call(body, name="cast_shards", out_shape=[jax.ShapeDtypeStruct(w.shape, MM) for w in ws],
                          compiler_params=pltpu.CompilerParams(vmem_limit_bytes=VMEM_LIMIT))(*ws)


def _sum_slots(pack, name, tr):
    n, R, C = pack.shape

    def body(p_ref, out_ref):
        acc = p_ref[0].astype(F32)
        for s in range(1, n):
            acc = acc + p_ref[s].astype(F32)
        out_ref[...] = acc

    return pl.pallas_call(
        body, name=name, grid=(R // tr,), in_specs=[pl.BlockSpec((n, tr, C), lambda i: (0, i, 0))],
        out_specs=pl.BlockSpec((tr, C), lambda i: (i, 0)), out_shape=jax.ShapeDtypeStruct((R, C), F32),
        compiler_params=_cparams(1))(pack)


def _me():
    return lax.axis_index("x"), lax.axis_index("y"), lax.axis_index("c")


def _peer(r):
    x, y, c = _me()
    px = 1 - x if r & 4 else x
    py = 1 - y if r & 2 else y
    pc = 1 - c if r & 1 else c
    return (px, py, pc), 4 * px + 2 * py + pc


def _small_gather(x_ref, out_ref, send_sems, recv_sems):
    R = x_ref.shape[0]
    mx, my, mc = _me()
    me = 4 * mx + 2 * my + mc
    mine = out_ref.at[pl.ds(pl.multiple_of(me * R, 8), R), :]
    copies = []
    for r in range(1, N_DEV):
        dev, _ = _peer(r)
        copies.append(pltpu.make_async_remote_copy(
            src_ref=x_ref, dst_ref=mine, send_sem=send_sems.at[r - 1], recv_sem=recv_sems.at[r - 1],
            device_id=dev, device_id_type=MESH))
    for cp in copies:
        cp.start()
    mine[...] = x_ref[...]
    for r in range(1, N_DEV):
        dev, idx = _peer(r)
        theirs = out_ref.at[pl.ds(pl.multiple_of(idx * R, 8), R), :]
        pltpu.make_async_remote_copy(
            src_ref=x_ref, dst_ref=theirs, send_sem=send_sems.at[r - 1], recv_sem=recv_sems.at[r - 1],
            device_id=dev, device_id_type=MESH).wait_recv()
    for cp in copies:
        cp.wait_send()


def _prologue(cs, ada_w, ada_b_cols, big):
    n = len(big)
    ncol = ada_w.shape[1]
    big_shape, big_sems = _xchg_specs(big, "gather")

    def body(cs_ref, w_ref, b_ref, *rest):
        big_in, cs_all, mod_all, big_out = rest[:n], rest[n], rest[n + 1], rest[n + 2:2 * n + 2]
        mod_scr, s1, r1, s2, r2 = rest[2 * n + 2:2 * n + 7]
        sems = rest[2 * n + 7:]
        _small_gather(cs_ref, cs_all, s1, r1)
        pick = (lax.broadcasted_iota(jnp.int32, (N_DEV, N_DEV * 8), 1)
                == 8 * lax.broadcasted_iota(jnp.int32, (N_DEV, N_DEV * 8), 0)).astype(F32)
        per_device = jnp.dot(pick, cs_all[...], preferred_element_type=F32, precision=lax.Precision.HIGHEST)
        mod_scr[...] = jnp.dot(per_device, w_ref[...], preferred_element_type=F32,
                               precision=lax.Precision.HIGHEST) + b_ref[...]
        _small_gather(mod_scr, mod_all, s2, r2)
        _xchg_start(big_in, big_out, sems, "gather")
        _xchg_wait(big_in, big_out, sems, "gather")

    vmem = pl.BlockSpec(memory_space=pltpu.VMEM)
    hbm = pl.BlockSpec(memory_space=pl.ANY)
    dma7 = pltpu.SemaphoreType.DMA((N_DEV - 1,))
    out = pl.pallas_call(
        body, name="prologue",
        out_shape=[jax.ShapeDtypeStruct((N_DEV * 8, D), F32), jax.ShapeDtypeStruct((N_DEV * 8, ncol), F32)]
        + big_shape,
        in_specs=[vmem, vmem, vmem] + [hbm] * n, out_specs=[vmem, vmem] + [hbm] * n,
        scratch_shapes=[pltpu.VMEM((8, ncol), F32), dma7, dma7, dma7, dma7] + big_sems,
        compiler_params=pltpu.CompilerParams(vmem_limit_bytes=VMEM_LIMIT),
    )(cs, ada_w, ada_b_cols, *big)
    return out[0], out[1], out[2:]


N_CHIP = N_DEV // 2


def _xchg_copies(ins, outs, sems, mode):
    send_sems, recv_sems, local_sems = sems
    mx, my, mc = _me()
    me = 4 * mx + 2 * my + mc
    my_chip = 2 * mx + my
    sibling = _peer(1)[0]

    def rdma(a, r, dev, src, slot):
        k = a * (N_DEV - 1) + r - 1
        return pltpu.make_async_remote_copy(
            src_ref=src, dst_ref=outs[a].at[slot], send_sem=send_sems.at[k], recv_sem=recv_sems.at[k],
            device_id=dev, device_id_type=MESH)

    own, sends, relays, recvs = [], [], [], []
    for a in range(len(ins)):
        if mode == "pair":
            for chip in range(N_CHIP):
                src = ins[a].at[2 * chip + 1 - mc]
                sends.append(rdma(a, chip + 1, sibling, src, chip))
                recvs.append(rdma(a, chip + 1, sibling, src, chip))
            continue
        if mode == "quad":
            own.append(pltpu.make_async_copy(ins[a].at[my_chip], outs[a].at[my_chip], local_sems.at[a]))
            for r in (2, 4, 6):
                dev, idx = _peer(r)
                chip = idx // 2
                sends.append(rdma(a, r, dev, ins[a].at[chip], my_chip))
                recvs.append(rdma(a, r, dev, ins[a].at[chip], chip))
            continue
        gather = mode == "gather"
        own.append(pltpu.make_async_copy(ins[a] if gather else ins[a].at[me], outs[a].at[me], local_sems.at[a]))
        for r in range(1, N_DEV):
            dev, idx = _peer(r)
            if not gather:
                sends.append(rdma(a, r, dev, ins[a].at[idx], me))
                recvs.append(rdma(a, r, dev, ins[a].at[idx], idx))
            elif r == 1:
                sends.append(rdma(a, r, dev, ins[a], me))
                recvs.append(rdma(a, r, dev, ins[a], idx))
            elif r % 2 == 0:
                sends.append(rdma(a, r, dev, ins[a], me))
                relays.append((rdma(a, r, dev, ins[a], idx), rdma(a, r + 1, sibling, outs[a].at[idx], idx)))
            else:
                recvs.append(rdma(a, r, sibling, ins[a], idx))
    return own, sends, relays, recvs


def _xchg_start(ins, outs, sems, mode):
    own, sends, _, _ = _xchg_copies(ins, outs, sems, mode)
    for cp in own + sends:
        cp.start()


def _xchg_wait(ins, outs, sems, mode):
    own, sends, relays, recvs = _xchg_copies(ins, outs, sems, mode)
    for arrival, relay in relays:
        arrival.wait_recv()
        relay.start()
    for cp in recvs:
        cp.wait_recv()
    for cp in own:
        cp.wait()
    for cp in sends + [relay for _, relay in relays]:
        cp.wait_send()


def _xchg_specs(arrays, mode):
    n = len(arrays)
    shape = {"gather": lambda s: (N_DEV,) + s, "scatter": lambda s: s, "pair": lambda s: (N_CHIP,) + s[1:],
             "quad": lambda s: s}[mode]
    out_shape = [jax.ShapeDtypeStruct(shape(a.shape), a.dtype) for a in arrays]
    sems = [pltpu.SemaphoreType.DMA((n * (N_DEV - 1),)), pltpu.SemaphoreType.DMA((n * (N_DEV - 1),)),
            pltpu.SemaphoreType.DMA((n,))]
    return out_shape, sems


def _exchange(arrays, mode, name):
    n = len(arrays)

    def body(*refs):
        _xchg_start(refs[:n], refs[n:2 * n], refs[2 * n:], mode)
        _xchg_wait(refs[:n], refs[n:2 * n], refs[2 * n:], mode)

    out_shape, sems = _xchg_specs(arrays, mode)
    return pl.pallas_call(
        body, name=name, out_shape=out_shape,
        in_specs=[pl.BlockSpec(memory_space=pl.ANY)] * n, out_specs=[pl.BlockSpec(memory_space=pl.ANY)] * n,
        scratch_shapes=sems,
    )(*arrays)


def _gridded(body, carry, *, name, grid, in_specs, out_specs, out_shape, scratch_shapes=(), aliases=None):
    if carry is None:
        return pl.pallas_call(
            body, name=name, grid=grid, in_specs=list(in_specs), out_specs=list(out_specs),
            out_shape=list(out_shape), scratch_shapes=list(scratch_shapes), input_output_aliases=aliases or {},
            compiler_params=_cparams(len(grid)))
    arrays, mode = carry
    n, n_in, n_out, n_scr = len(arrays), len(in_specs), len(out_specs), len(scratch_shapes)
    c_shape, c_sems = _xchg_specs(arrays, mode)

    def wrapped(*refs):
        ins, cin = refs[:n_in], refs[n_in:n_in + n]
        o0 = n_in + n
        outs, cout = refs[o0:o0 + n_out], refs[o0 + n_out:o0 + n_out + n]
        s0 = o0 + n_out + n
        scr, sems = refs[s0:s0 + n_scr], refs[s0 + n_scr:]
        first = pl.program_id(0) == 0
        last = pl.program_id(0) == grid[0] - 1
        for ax in range(1, len(grid)):
            first = first & (pl.program_id(ax) == 0)
            last = last & (pl.program_id(ax) == grid[ax] - 1)

        @pl.when(first)
        def _():
            _xchg_start(cin, cout, sems, mode)

        body(*ins, *outs, *scr)

        @pl.when(last)
        def _():
            _xchg_wait(cin, cout, sems, mode)

    hbm = pl.BlockSpec(memory_space=pl.ANY)
    res = pl.pallas_call(
        wrapped, name=name, grid=grid, in_specs=list(in_specs) + [hbm] * n, out_specs=list(out_specs) + [hbm] * n,
        out_shape=list(out_shape) + c_shape, scratch_shapes=list(scratch_shapes) + c_sems,
        input_output_aliases=aliases or {}, compiler_params=_cparams(len(grid)),
    )
    return lambda *args: res(*args, *arrays)


def _local_step(x, target, mod, small, sh, w1):
    w1_in, w1_out = w1[0].reshape(2, D_FF, D), w1[1].reshape(D_FF, D)
    (x1, a1, b1, f1, h1, h2), (wm_in,) = _ffn_fwd(x, mod, 0, small["norm_ffn1"], w1_in, w1_out, 0.5, "ffn1_fwd",
                                                  ([sh["mix_w_in"]], "gather"), nxt=(small["norm_mix"], 3))
    (p,), (wh_o, wc_o, wm_o, cw) = _mixin_fwd(
        h2, wm_in, ([sh["hgrn_w_o"], sh["conv_w_o"], sh["mix_w_out"], sh["conv_w"]], "gather"))
    wh_o, wc_o, wm_o = wh_o.reshape(D, D), wc_o.reshape(D, D), wm_o.reshape(D, D)
    cw = jnp.pad(cw.transpose(1, 0, 2).reshape(CONV_K, D), ((0, HALO - CONV_K), (0, 0)))
    (o, oa, a_all, s_all), (w2_in,) = _hgrn_fwd(p, small["hgrn_lb"], small["hgrn_g"], ([sh["ffn2_w_in"]], "gather"))
    (u1, u2), (w2_out,) = _conv_fwd(p, cw, small["conv_b"], small["conv_ln_g"], small["conv_ln_b"],
                                    ([sh["ffn2_w_out"]], "gather"))
    w2_in, w2_out = w2_in.reshape(2, D_FF, D), w2_out.reshape(D_FF, D)
    x2, ya, yb, mout = _mixout_fwd(x1, oa, u2, p, mod, 3, wh_o, wc_o, wm_o)
    (x3, a3, b3, f3, h3), _ = _ffn_fwd(x2, mod, 6, small["norm_ffn2"], w2_in, w2_out, 0.5, "ffn2_fwd", None)
    dx3, df3, sm_head = _head(x3, target, small["norm_final"], mod, 8, 0.5)

    (da3, db3, dw2_in, dw2_out), _ = _ffn_bwd_w(h3, df3, a3, b3, w2_out, "ffn2_bwd_w", None)
    rows = lambda t: t.reshape(N_DEV, -1, D).astype(MM)
    (dx2, sm3), (r2_out,) = _ffn_bwd_x(x2, dx3, f3, da3, db3, mod, 6, small["norm_ffn2"], w2_in, 0.5, "ffn2_bwd_x",
                                       ([rows(dw2_out)], "scatter"))
    dp, doa, du2, dwh_o, dwc_o, dwm_o, sm_mo = _mixout_bwd(dx2, oa, u2, ya, yb, mout, p, mod, 3, wh_o, wc_o, wm_o)
    (dp, dcw, sm_cv), (r2_in,) = _conv_bwd(p, u1, du2, cw, small["conv_ln_g"], small["conv_ln_b"], dp,
                                           ([rows(dw2_in)], "scatter"))
    (dp, sm_hg), _ = _hgrn_bwd(p, o, a_all, s_all, doa, small["hgrn_lb"], small["hgrn_g"], dp, None)
    (dx1, dwm_in, sm2, df1), (rh_o, rc_o, rm_o, rcw) = _mixin_bwd(
        x1, h2, dx2, dp, mod, 3, small["norm_mix"], wm_in, 2, 0.5,
        ([rows(dwh_o), rows(dwc_o), rows(dwm_o), dcw[:CONV_K].reshape(CONV_K, N_DEV, -1).transpose(1, 0, 2)],
         "scatter"))
    (da1, db1, dw1_in, dw1_out), (rm_in,) = _ffn_bwd_w(h1, df1, a1, b1, w1_out, "ffn1_bwd_w",
                                                      (_pair_reduce([dwm_in], "pair_mix"), "quad"))
    (dx0, sm1), (r1_in, r1_out) = _ffn_bwd_x(
        x, dx1, f1, da1, db1, mod, 0, small["norm_ffn1"], w1_in, 0.5, "ffn1_bwd_x",
        (_pair_reduce([rows(dw1_in), rows(dw1_out)], "pair_ffn1"), "quad"))

    dmod = jnp.concatenate([sm1[0:3], sm2[0:2], sm_mo[2:3], sm3[0:3]], axis=0)
    gsmall = dict(norm_ffn1=sm1[3:4], norm_mix=sm2[3:4], lb0=sm_hg[0:1], hgrn_g=sm_hg[1:2], conv_b=sm_cv[0:1],
                  conv_ln_g=sm_cv[1:2], conv_ln_b=sm_cv[2:3], norm_ffn2=sm3[3:4], norm_final=sm_head[0:1])
    recv = dict(ffn1_w_in=r1_in, ffn1_w_out=r1_out, mix_w_in=rm_in, hgrn_w_o=rh_o, conv_w=rcw, conv_w_o=rc_o,
                mix_w_out=rm_o, ffn2_w_in=r2_in, ffn2_w_out=r2_out)
    return sm_head[1, 0], dx0, dmod, gsmall, recv


def _pair_add(mine, theirs, core, name):
    _, R, C = theirs.shape

    def body(core_ref, a_ref, b_ref, out_ref):
        del core_ref
        out_ref[0] = (a_ref[0, 0].astype(F32) + b_ref[0].astype(F32)).astype(out_ref.dtype)

    blk = pl.BlockSpec((1, R, C), lambda s, core_ref: (s, 0, 0))
    grid_spec = pltpu.PrefetchScalarGridSpec(
        num_scalar_prefetch=1, grid=(N_CHIP,),
        in_specs=[pl.BlockSpec((1, 1, R, C), lambda s, core_ref: (s, core_ref[0], 0, 0)), blk], out_specs=blk)
    return pl.pallas_call(body, name=name, grid_spec=grid_spec,
                          out_shape=jax.ShapeDtypeStruct(theirs.shape, mine.dtype), compiler_params=_cparams(1),
                          )(core, mine.reshape(N_CHIP, 2, R, C), theirs)


def _pair_reduce(arrays, name):
    theirs = _exchange(arrays, "pair", name)
    core = lax.axis_index("c").astype(jnp.int32).reshape(1)
    return [_pair_add(a, t, core, "%s_add%d" % (name, i)) for i, (a, t) in enumerate(zip(arrays, theirs))]


SMALL_ORDER = ("norm_ffn1", "norm_mix", "lb0", "hgrn_g", "conv_b", "conv_ln_g", "conv_ln_b", "norm_ffn2",
               "norm_final")
PACK_ROWS = 24


def kernel(x, c, ada_w, ada_b, norm_ffn1, ffn1_w_in, ffn1_w_out, norm_mix, mix_w_in, hgrn_lb, hgrn_g, hgrn_w_o, conv_w, conv_b, conv_ln_g, conv_ln_b, conv_w_o, mix_w_out, norm_ffn2, ffn2_w_in, ffn2_w_out, norm_final, loss_target, m_ada_w, m_ada_b, m_norm_ffn1, m_ffn1_w_in, m_ffn1_w_out, m_norm_mix, m_mix_w_in, m_hgrn_lb, m_hgrn_g, m_hgrn_w_o, m_conv_w, m_conv_b, m_conv_ln_g, m_conv_ln_b, m_conv_w_o, m_mix_w_out, m_norm_ffn2, m_ffn2_w_in, m_ffn2_w_out, m_norm_final, v_ada_w, v_ada_b, v_norm_ffn1, v_ffn1_w_in, v_ffn1_w_out, v_norm_mix, v_mix_w_in, v_hgrn_lb, v_hgrn_g, v_hgrn_w_o, v_conv_w, v_conv_b, v_conv_ln_g, v_conv_ln_b, v_conv_w_o, v_mix_w_out, v_norm_ffn2, v_ffn2_w_in, v_ffn2_w_out, v_norm_final):
    mx, my, mc = _me()
    me = 4 * mx + 2 * my + mc
    ncol = ada_w.shape[2]

    sh = dict(ffn1_w_in=ffn1_w_in[0].T, ffn1_w_out=ffn1_w_out[0], mix_w_in=mix_w_in[0], hgrn_w_o=hgrn_w_o[0],
              conv_w_o=conv_w_o[0], mix_w_out=mix_w_out[0], ffn2_w_in=ffn2_w_in[0].T, ffn2_w_out=ffn2_w_out[0])
    sh = dict(zip(sh, _cast_shards(list(sh.values()))))
    sh["conv_w"] = conv_w[0]
    small = dict(norm_ffn1=norm_ffn1, norm_mix=norm_mix, hgrn_lb=hgrn_lb, hgrn_g=hgrn_g, conv_b=conv_b,
                 conv_ln_g=conv_ln_g, conv_ln_b=conv_ln_b, norm_ffn2=norm_ffn2, norm_final=norm_final.reshape(1, D))

    cs = jnp.broadcast_to(c * jax.nn.sigmoid(c), (8, D))
    ada_b_cols = lax.dynamic_slice(ada_b, (0, me * ncol), (1, ncol))
    cs_all, mod_all, w1 = _prologue(cs, ada_w[0], ada_b_cols, [sh["ffn1_w_in"], sh["ffn1_w_out"]])
    cs_all = cs_all.reshape(N_DEV, 8, D)[:, 0, :]
    mod = lax.dynamic_index_in_dim(mod_all.reshape(N_DEV, N_DEV, ncol), me, axis=1, keepdims=False).reshape(9, D)

    loss_local, dx, dmod, gsmall, recv = _local_step(x[0], loss_target[0], mod, small, sh, w1)

    n_used = 9 + len(SMALL_ORDER) + 1
    pack = jnp.concatenate([dmod] + [gsmall[n] for n in SMALL_ORDER] + [jnp.broadcast_to(loss_local, (1, D))]
                           + [jnp.zeros((PACK_ROWS - n_used, D), F32)], axis=0)
    res = {}
    *res["ffn2_w_out"], pack_all = _adamw(ffn2_w_out[0], m_ffn2_w_out[0], v_ffn2_w_out[0], recv["ffn2_w_out"],
                                          "adamw_ffn2_w_out", ([pack], "gather"))
    tot = _sum_slots(pack_all, "sum_small", PACK_ROWS)
    loss = tot[n_used - 1, 0]
    dmod_all = pack_all[:, 0:9, :].reshape(N_DEV, 9 * D)
    dmod_cols = lax.dynamic_slice(dmod_all, (0, me * ncol), (N_DEV, ncol))

    res["ada_w"] = _adamw_ada_w(ada_w[0], m_ada_w[0], v_ada_w[0], cs_all, dmod_cols)
    big = dict(ffn1_w_in=(ffn1_w_in, m_ffn1_w_in, v_ffn1_w_in), ffn1_w_out=(ffn1_w_out, m_ffn1_w_out, v_ffn1_w_out),
               mix_w_in=(mix_w_in, m_mix_w_in, v_mix_w_in), hgrn_w_o=(hgrn_w_o, m_hgrn_w_o, v_hgrn_w_o),
               conv_w=(conv_w, m_conv_w, v_conv_w), conv_w_o=(conv_w_o, m_conv_w_o, v_conv_w_o),
               mix_w_out=(mix_w_out, m_mix_w_out, v_mix_w_out), ffn2_w_in=(ffn2_w_in, m_ffn2_w_in, v_ffn2_w_in),
               ffn2_w_out=(ffn2_w_out, m_ffn2_w_out, v_ffn2_w_out))
    for n, (w, m, v) in big.items():
        if n in res:
            continue
        if n in ("ffn1_w_in", "ffn2_w_in"):
            res[n] = tuple(t.T for t in _adamw(w[0].T, m[0].T, v[0].T, recv[n], "adamw_" + n))
        else:
            res[n] = _adamw(w[0], m[0], v[0], recv[n], "adamw_" + n)
    sm_names = ("ada_b", "norm_ffn1", "norm_mix", "hgrn_lb", "hgrn_g", "conv_b", "conv_ln_g", "conv_ln_b",
                "norm_ffn2", "norm_final")
    sm_w = dict(ada_b=(ada_b, m_ada_b, v_ada_b), norm_ffn1=(norm_ffn1, m_norm_ffn1, v_norm_ffn1),
                norm_mix=(norm_mix, m_norm_mix, v_norm_mix), hgrn_lb=(hgrn_lb, m_hgrn_lb, v_hgrn_lb),
                hgrn_g=(hgrn_g, m_hgrn_g, v_hgrn_g), conv_b=(conv_b, m_conv_b, v_conv_b),
                conv_ln_g=(conv_ln_g, m_conv_ln_g, v_conv_ln_g), conv_ln_b=(conv_ln_b, m_conv_ln_b, v_conv_ln_b),
                norm_ffn2=(norm_ffn2, m_norm_ffn2, v_norm_ffn2), norm_final=(norm_final, m_norm_final, v_norm_final))
    sm_w["norm_final"] = tuple(t.reshape(1, D) for t in sm_w["norm_final"])
    grad_rows = dict({n: 9 + i for i, n in enumerate(SMALL_ORDER)}, ada_b=0, hgrn_lb=9 + SMALL_ORDER.index("lb0"))
    res.update(_adamw_small(tot, sm_names, sm_w, grad_rows))
    res["norm_final"] = tuple(t.reshape(norm_final.shape) for t in res["norm_final"])

    order = ("ada_w", "ada_b", "norm_ffn1", "ffn1_w_in", "ffn1_w_out", "norm_mix", "mix_w_in", "hgrn_lb", "hgrn_g",
             "hgrn_w_o", "conv_w", "conv_b", "conv_ln_g", "conv_ln_b", "conv_w_o", "mix_w_out", "norm_ffn2",
             "ffn2_w_in", "ffn2_w_out", "norm_final")
    lead = lambda n, t: t[None] if n in big or n == "ada_w" else t
    outs = [loss, dx[None]]
    for j in range(4):
        outs += [lead(n, res[n][j]) for n in order]
    return tuple(outs)
```

```python
import jax
import jax.numpy as jnp
from jax import lax
from jax.experimental import pallas as pl
from jax.experimental.pallas import tpu as pltpu

F32 = jnp.float32
MM = jnp.bfloat16
ACT = jnp.bfloat16

D = 1024
D_FF = 2816
HEADS = 8
HD = 128
CHUNK = 64
SUB = 16
NSUB = CHUNK // SUB
HGRN_BLOCK = 1024
SAFE_EXP = 60.0
CONV_K = 31
HALO = 32
EPS = 1e-6
N_DEV = 8
NEG = -1e30
Q_SCALE = HD ** -0.5

ADAM_LR = 0.001
ADAM_B1 = 0.9
ADAM_B2 = 0.999
ADAM_EPS = 1e-08
ADAM_WD = 0.01
ADAM_STEP = 10

V7X_VMEM_BYTES = 64 * 1024 * 1024
VMEM_LIMIT = V7X_VMEM_BYTES - 4 * 1024 * 1024
MESH = pl.DeviceIdType.MESH


def _cparams(n_axes):
    return pltpu.CompilerParams(dimension_semantics=("arbitrary",) * n_axes, vmem_limit_bytes=VMEM_LIMIT)


def _mm(a, b):
    return lax.dot_general(a.astype(MM), b.astype(MM), (((1,), (0,)), ((), ())), preferred_element_type=F32)


def _mm_nt(a, b):
    return lax.dot_general(a.astype(MM), b.astype(MM), (((1,), (1,)), ((), ())), preferred_element_type=F32)


def _mm_tn(a, b):
    return lax.dot_general(a.astype(MM), b.astype(MM), (((0,), (0,)), ((), ())), preferred_element_type=F32)


def _sig(x):
    return 1.0 / (1.0 + jnp.exp(-x))


def _colsum(x):
    return jnp.sum(x, axis=0, keepdims=True)


def _rowmean(x):
    return jnp.mean(x, axis=-1, keepdims=True)


def _modnorm_fwd(xv, g, sh, sc):
    r = lax.rsqrt(_rowmean(xv * xv) + EPS)
    xh = xv * r
    n = xh * g
    return n * (1.0 + sc) + sh, xh, n, r


def _modnorm_bwd(dh, xh, n, r, g, sc):
    dsc = _colsum(dh * n)
    dsh = _colsum(dh)
    dn = dh * (1.0 + sc)
    dg = _colsum(dn * xh)
    dxh = dn * g
    dx = r * (dxh - xh * _rowmean(dxh * xh))
    return dx, dsh, dsc, dg


def _ffn_fwd(x, mod, mo, gnorm, w_in_t, w_out, res, name, carry, nxt=None, head=None):
    T = x.shape[0]
    tm = min(512, T)
    tn = D_FF // 2
    ni = T // tm

    def body(x_ref, mod_ref, g_ref, wi_ref, wo_ref, *rest):
        if head is not None:
            t_ref, gf_ref, xo_ref, a_ref, b_ref, f_ref, h_ref, df_ref, sm_ref = rest
        elif nxt is None:
            xo_ref, a_ref, b_ref, f_ref, h_ref = rest
        else:
            gn_ref, xo_ref, a_ref, b_ref, f_ref, h_ref, hn_ref = rest
        xv = x_ref[...]
        h, _, _, _ = _modnorm_fwd(xv, g_ref[...], mod_ref[mo:mo + 1, :], mod_ref[mo + 1:mo + 2, :])
        h = h.astype(ACT)
        h_ref[...] = h
        f = None
        for c0 in range(0, D_FF, tn):
            a = _mm_nt(h, wi_ref[0, c0:c0 + tn, :])
            b = _mm_nt(h, wi_ref[1, c0:c0 + tn, :])
            a_ref[:, c0:c0 + tn] = a.astype(ACT)
            b_ref[:, c0:c0 + tn] = b.astype(ACT)
            part = _mm(a * _sig(a) * b, wo_ref[c0:c0 + tn, :])
            f = part if f is None else f + part
        f_ref[...] = f
        xo = xv + res * mod_ref[mo + 2:mo + 3, :] * f
        if nxt is not None:
            hn, _, _, _ = _modnorm_fwd(xo, gn_ref[...], mod_ref[nxt[1]:nxt[1] + 1, :], mod_ref[nxt[1] + 1:nxt[1] + 2, :])
            hn_ref[...] = hn.astype(ACT)
        if head is None:
            xo_ref[...] = xo
            return
        i = pl.program_id(0)

        @pl.when(i == 0)
        def _():
            sm_ref[...] = jnp.zeros_like(sm_ref)

        gf = gf_ref[...]
        r = lax.rsqrt(_rowmean(xo * xo) + EPS)
        xh = xo * r
        e = xh * gf - t_ref[...]
        sm_ref[1:2, :] += _colsum(e * e) * (0.5 / D)
        dy = e * (1.0 / D)
        sm_ref[0:1, :] += _colsum(dy * xh)
        dxh = dy * gf
        dx = r * (dxh - xh * _rowmean(dxh * xh))
        xo_ref[...] = dx
        df_ref[...] = (head[3] * mod_ref[head[2]:head[2] + 1, :] * dx).astype(MM)

        @pl.when(i == ni - 1)
        def _():
            sm_ref[1:2, :] = jnp.broadcast_to(jnp.sum(sm_ref[1:2, :], axis=-1, keepdims=True), (1, D))

    tile = pl.BlockSpec((tm, D), lambda i: (i, 0))
    wide = pl.BlockSpec((tm, D_FF), lambda i: (i, 0))
    row = pl.BlockSpec((1, D), lambda i: (0, 0))
    extra_in, extra_specs, extra_out, extra_shape = (), [], [], []
    if nxt is not None:
        extra_in, extra_specs = (nxt[0],), [row]
        extra_out, extra_shape = [tile], [jax.ShapeDtypeStruct((T, D), ACT)]
    if head is not None:
        extra_in, extra_specs = (head[0], head[1]), [tile, row]
        extra_out = [tile, pl.BlockSpec((8, D), lambda i: (0, 0))]
        extra_shape = [jax.ShapeDtypeStruct((T, D), MM), jax.ShapeDtypeStruct((8, D), F32)]
    n_out = 5 + len(extra_out)
    out = _gridded(
        body, carry, name=name, grid=(ni,),
        in_specs=[
            tile,
            pl.BlockSpec((9, D), lambda i: (0, 0)),
            row,
            pl.BlockSpec((2, D_FF, D), lambda i: (0, 0, 0), pipeline_mode=pl.Buffered(1)),
            pl.BlockSpec((D_FF, D), lambda i: (0, 0), pipeline_mode=pl.Buffered(1)),
        ] + extra_specs,
        out_specs=[tile, wide, wide, tile, tile] + extra_out,
        out_shape=[
            jax.ShapeDtypeStruct((T, D), F32),
            jax.ShapeDtypeStruct((T, D_FF), ACT),
            jax.ShapeDtypeStruct((T, D_FF), ACT),
            jax.ShapeDtypeStruct((T, D), F32),
            jax.ShapeDtypeStruct((T, D), ACT),
        ] + extra_shape,
    )(*((x, mod, gnorm, w_in_t, w_out) + extra_in))
    return out[:n_out], out[n_out:]


def _ffn_bwd_w(h, df, a, b, w_out, name, carry):
    T = h.shape[0]
    tm = min(2048, T)
    ni = T // tm
    tn = 256
    nj = D_FF // tn

    def body(h_ref, df_ref, a_ref, b_ref, wo_ref, da_ref, db_ref, dwi_ref, dwo_ref, acc_i, acc_o):
        i = pl.program_id(1)

        @pl.when(i == 0)
        def _():
            acc_i[...] = jnp.zeros_like(acc_i)
            acc_o[...] = jnp.zeros_like(acc_o)

        hb = h_ref[...]
        df = df_ref[...]
        av = a_ref[...].astype(F32)
        bv = b_ref[...].astype(F32)
        sg = _sig(av)
        sa = av * sg
        s = (sa * bv).astype(MM)
        ds = _mm_nt(df, wo_ref[...])
        da = (ds * bv * sg * (1.0 + av * (1.0 - sg))).astype(MM)
        db = (ds * sa).astype(MM)
        da_ref[...] = da
        db_ref[...] = db
        acc_o[...] += _mm_tn(s, df)
        acc_i[0] += _mm_tn(da, hb)
        acc_i[1] += _mm_tn(db, hb)

        @pl.when(i == ni - 1)
        def _():
            dwi_ref[...] = acc_i[...].astype(MM)
            dwo_ref[...] = acc_o[...].astype(MM)

    out = _gridded(
        body, carry, name=name, grid=(nj, ni),
        in_specs=[
            pl.BlockSpec((tm, D), lambda j, i: (i, 0)),
            pl.BlockSpec((tm, D), lambda j, i: (i, 0)),
            pl.BlockSpec((tm, tn), lambda j, i: (i, j)),
            pl.BlockSpec((tm, tn), lambda j, i: (i, j)),
            pl.BlockSpec((tn, D), lambda j, i: (j, 0)),
        ],
        out_specs=[
            pl.BlockSpec((tm, tn), lambda j, i: (i, j)),
            pl.BlockSpec((tm, tn), lambda j, i: (i, j)),
            pl.BlockSpec((2, tn, D), lambda j, i: (0, j, 0)),
            pl.BlockSpec((tn, D), lambda j, i: (j, 0)),
        ],
        out_shape=[
            jax.ShapeDtypeStruct((T, D_FF), MM),
            jax.ShapeDtypeStruct((T, D_FF), MM),
            jax.ShapeDtypeStruct((2, D_FF, D), MM),
            jax.ShapeDtypeStruct((D_FF, D), MM),
        ],
        scratch_shapes=[pltpu.VMEM((2, tn, D), F32), pltpu.VMEM((tn, D), F32)],
    )(h, df, a, b, w_out)
    return out[:4], out[4:]


def _ffn_bwd_x(x, dxo, f, da, db, mod, mo, gnorm, w_in_t, res, name, carry):
    T = x.shape[0]
    tm = min(512, T)
    ni = T // tm
    tn = D_FF // 2
    nj = D_FF // tn

    def body(x_ref, dxo_ref, f_ref, da_ref, db_ref, mod_ref, g_ref, wi_ref, dx_ref, sm_ref, dh_scr):
        j = pl.program_id(0)
        i = pl.program_id(1)

        @pl.when((j == 0) & (i == 0))
        def _():
            sm_ref[...] = jnp.zeros_like(sm_ref)

        @pl.when(j == 0)
        def _():
            dh_scr[i] = jnp.zeros((tm, D), F32)

        dh_scr[i] += _mm(da_ref[...], wi_ref[0]) + _mm(db_ref[...], wi_ref[1])

        @pl.when(j == nj - 1)
        def _():
            sc = mod_ref[mo + 1:mo + 2, :]
            _, xh, n, r = _modnorm_fwd(x_ref[...], g_ref[...], mod_ref[mo:mo + 1, :], sc)
            dxn, dsh, dsc, dg = _modnorm_bwd(dh_scr[i], xh, n, r, g_ref[...], sc)
            dxo_v = dxo_ref[...]
            dx_ref[...] = dxo_v + dxn
            sm_ref[0:1, :] += dsh
            sm_ref[1:2, :] += dsc
            sm_ref[2:3, :] += _colsum(dxo_v * f_ref[...]) * res
            sm_ref[3:4, :] += dg

    last = pl.BlockSpec((tm, D), lambda j, i: (jnp.where(j == nj - 1, i, 0), 0))
    out = _gridded(
        body, carry, name=name, grid=(nj, ni),
        in_specs=[last, last, last,
                  pl.BlockSpec((tm, tn), lambda j, i: (i, j)), pl.BlockSpec((tm, tn), lambda j, i: (i, j)),
                  pl.BlockSpec((9, D), lambda j, i: (0, 0)), pl.BlockSpec((1, D), lambda j, i: (0, 0)),
                  pl.BlockSpec((2, tn, D), lambda j, i: (0, j, 0))],
        out_specs=[last, pl.BlockSpec((8, D), lambda j, i: (0, 0))],
        out_shape=[jax.ShapeDtypeStruct((T, D), F32), jax.ShapeDtypeStruct((8, D), F32)],
        scratch_shapes=[pltpu.VMEM((ni, tm, D), F32)],
    )(x, dxo, f, da, db, mod, gnorm, w_in_t)
    return out[:2], out[2:]


def _mixin_fwd(h, w, carry):
    T = h.shape[0]
    tm = min(2048, T)
    ni = T // tm

    def body(h_ref, w_ref, p_ref, h_all):
        i = pl.program_id(1)

        @pl.when(pl.program_id(0) == 0)
        def _():
            h_all[i] = h_ref[...]

        p_ref[0] = _mm(h_all[i], w_ref[0])

    first = lambda k, i: (jnp.where(k == 0, i, ni - 1), 0)
    out = _gridded(
        body, carry, name="mixin_fwd", grid=(8, ni),
        in_specs=[pl.BlockSpec((tm, D), first), pl.BlockSpec((1, D, D), lambda k, i: (k, 0, 0))],
        out_specs=[pl.BlockSpec((1, tm, D), lambda k, i: (k, i, 0))],
        out_shape=[jax.ShapeDtypeStruct((8, T, D), F32)],
        scratch_shapes=[pltpu.VMEM((ni, tm, D), ACT)],
    )(h, w)
    return out[:1], out[1:]


def _mixin_bwd(x, h, dxo, dp, mod, mo, gnorm, w, next_gate, next_res, carry):
    T = x.shape[0]
    tm = min(512, T)
    ni = T // tm

    def body(x_ref, h_ref, dxo_ref, dp_ref, mod_ref, g_ref, w_ref, dx_ref, dw_ref, sm_ref, df_ref, dh_scr, acc):
        k = pl.program_id(0)
        i = pl.program_id(1)

        @pl.when(i == 0)
        def _():
            acc[...] = jnp.zeros_like(acc)

        @pl.when(k == 0)
        def _():
            dh_scr[i] = jnp.zeros((tm, D), F32)

        @pl.when((k == 0) & (i == 0))
        def _():
            sm_ref[...] = jnp.zeros_like(sm_ref)

        dpk = dp_ref[0].astype(MM)
        acc[...] += _mm_tn(h_ref[...], dpk)
        dh_scr[i] += _mm_nt(dpk, w_ref[0])

        @pl.when(i == ni - 1)
        def _():
            dw_ref[0] = acc[...].astype(MM)

        @pl.when(k == 7)
        def _():
            sc = mod_ref[mo + 1:mo + 2, :]
            _, xh, n, r = _modnorm_fwd(x_ref[...], g_ref[...], mod_ref[mo:mo + 1, :], sc)
            dxn, dsh, dsc, dg = _modnorm_bwd(dh_scr[i], xh, n, r, g_ref[...], sc)
            dx = dxo_ref[...] + dxn
            dx_ref[...] = dx
            df_ref[...] = (next_res * mod_ref[next_gate:next_gate + 1, :] * dx).astype(MM)
            sm_ref[0:1, :] += dsh
            sm_ref[1:2, :] += dsc
            sm_ref[3:4, :] += dg

    last = pl.BlockSpec((tm, D), lambda k, i: (jnp.where(k == 7, i, 0), 0))
    out = _gridded(
        body, carry, name="mixin_bwd", grid=(8, ni),
        in_specs=[pl.BlockSpec((tm, D), lambda k, i: (jnp.where(k == 7, i, 0), 0)),
                  pl.BlockSpec((tm, D), lambda k, i: (i, 0)),
                  pl.BlockSpec((tm, D), lambda k, i: (jnp.where(k == 7, i, 0), 0)),
                  pl.BlockSpec((1, tm, D), lambda k, i: (k, i, 0)), pl.BlockSpec((9, D), lambda k, i: (0, 0)),
                  pl.BlockSpec((1, D), lambda k, i: (0, 0)), pl.BlockSpec((1, D, D), lambda k, i: (k, 0, 0))],
        out_specs=[last, pl.BlockSpec((1, D, D), lambda k, i: (k, 0, 0)), pl.BlockSpec((8, D), lambda k, i: (0, 0)),
                   last],
        out_shape=[jax.ShapeDtypeStruct((T, D), F32), jax.ShapeDtypeStruct((8, D, D), MM),
                   jax.ShapeDtypeStruct((8, D), F32), jax.ShapeDtypeStruct((T, D), MM)],
        scratch_shapes=[pltpu.VMEM((ni, tm, D), F32), pltpu.VMEM((D, D), F32)],
    )(x, h, dxo, dp, mod, gnorm, w)
    return out[:4], out[4:]


def _hgrn_consts():
    rows = jnp.arange(SUB * HD) // HD
    e = (rows[:, None] == jnp.arange(HD)[None, :]).astype(MM)
    return e, e.T


def _rows_bcast(ref, cb, first, n):
    parts = [jnp.broadcast_to(ref[pl.ds(c * CHUNK + first, 1), :], (n, HD)) for c in range(cb // CHUNK)]
    return jnp.concatenate(parts, axis=0)


def _hgrn_pre(qr, fr, lb_ref, b_scr, cb):
    z = lb_ref[...]
    lb = _sig(z[0:1, :] - z[1:2, :])
    sq = _sig(qr)
    q = qr * sq * Q_SCALE
    sf = _sig(fr)
    fg = lb + (1.0 - lb) * sf
    lf = jnp.log(fg)
    k = 1.0 - fg
    tl = lax.broadcasted_iota(jnp.int32, (cb, HD), 0) % CHUNK
    bc = lf
    sh = 1
    while sh < CHUNK:
        bc = bc + jnp.where(tl >= sh, pltpu.roll(bc, sh, 0), 0.0)
        sh *= 2
    b_scr[...] = bc
    bl = _rows_bcast(b_scr, cb, CHUNK - 1, CHUNK)
    eb = jnp.exp(bc)
    ekd = jnp.exp(bl - bc)
    ekf = jnp.exp(jnp.minimum(-bc, SAFE_EXP))
    return dict(lb=lb, sq=sq, q=q, sf=sf, fg=fg, k=k, tl=tl, b=bc, bl=bl, eb=eb, ekd=ekd, ekf=ekf,
                qe=q * eb, kd=k * ekd, kf=k * ekf, safe=jnp.max(-bc) < SAFE_EXP)


def _hgrn_pre_fused(p_ref, lb_ref, b_scr, q_scr, k_scr, qe_scr, kf_scr, kd_scr, cb):
    z = lb_ref[...]
    lb = _sig(z[0:1, :] - z[1:2, :])
    tl = lax.broadcasted_iota(jnp.int32, (CHUNK, HD), 0)

    def chunk(c, worst):
        r0 = pl.multiple_of(c * CHUNK, CHUNK)
        rs = pl.ds(r0, CHUNK)
        qr = p_ref[0, rs, :]
        q = qr * _sig(qr) * Q_SCALE
        fg = lb + (1.0 - lb) * _sig(p_ref[1, rs, :])
        k = 1.0 - fg
        bc = jnp.log(fg)
        sh = 1
        while sh < CHUNK:
            bc = bc + jnp.where(tl >= sh, pltpu.roll(bc, sh, 0), 0.0)
            sh *= 2
        b_scr[rs, :] = bc
        q_scr[rs, :] = q
        k_scr[rs, :] = k
        qe_scr[rs, :] = (q * jnp.exp(bc)).astype(MM)
        kf_scr[rs, :] = (k * jnp.exp(jnp.minimum(-bc, SAFE_EXP))).astype(MM)
        kd_scr[rs, :] = (k * jnp.exp(b_scr[pl.ds(r0 + CHUNK - 1, 1), :] - bc)).astype(MM)
        return jnp.maximum(worst, -bc)

    worst = lax.fori_loop(0, cb // CHUNK, chunk, jnp.zeros((CHUNK, HD), F32))
    return jnp.max(worst) < SAFE_EXP


def _hgrn_sub(pre, b_scr, cb):
    bc, tl, q, k = pre["b"], pre["tl"], pre["q"], pre["k"]
    br = [None] + [_rows_bcast(b_scr, cb, SUB * i - 1, CHUNK) for i in range(1, NSUB)]
    sb = tl // SUB
    bref = jnp.where(sb == 0, bc, jnp.where(sb == 1, br[1], jnp.where(sb == 2, br[2], br[3])))
    eqo = jnp.exp(bc - bref)
    eko = [None] + [jnp.exp(jnp.where(tl < SUB * i, br[i] - bc, NEG)) for i in range(1, NSUB)]
    return dict(eqo=eqo, eko=eko, qo=q * eqo, ko=[None] + [k * eko[i] for i in range(1, NSUB)])


def _pad_rows(x):
    return jnp.concatenate([x, jnp.zeros_like(x)], axis=0)


def _by_subblock(sbc, parts):
    out = jnp.zeros_like(parts[1])
    for i in range(1, NSUB):
        out = jnp.where(sbc == i, parts[i], out)
    return out


def _hgrn_fwd(p, hgrn_lb, hgrn_g, carry):
    T = p.shape[1]
    cb = min(HGRN_BLOCK, T)
    nch = cb // CHUNK
    ncb = T // cb
    e_mat, _ = _hgrn_consts()

    def body(p_ref, lb_ref, g_ref, e_ref, o_ref, oa_ref, a_ref, s_ref, st_scr, q_scr, k_scr, b_scr, z_scr, ad_scr,
             qe_scr, kf_scr, kd_scr):
        @pl.when(pl.program_id(1) == 0)
        def _():
            st_scr[...] = jnp.zeros_like(st_scr)

        safe = _hgrn_pre_fused(p_ref, lb_ref, b_scr, q_scr, k_scr, qe_scr, kf_scr, kd_scr, cb)
        chunks = [slice(c * CHUNK, (c + 1) * CHUNK) for c in range(nch)]
        row_i = lax.broadcasted_iota(jnp.int32, (CHUNK, HD), 0)
        lane_i = lax.broadcasted_iota(jnp.int32, (CHUNK, HD), 1)
        sbc = row_i // SUB
        causal = lane_i <= row_i

        @pl.when(safe)
        def _():
            for rs in chunks:
                ad_scr[rs, :] = jnp.where(causal, _mm_nt(qe_scr[rs, :], _pad_rows(kf_scr[rs, :])), 0.0)

        @pl.when(jnp.logical_not(safe))
        def _():
            tl = lax.broadcasted_iota(jnp.int32, (cb, HD), 0) % CHUNK
            sub = _hgrn_sub(dict(b=b_scr[...], tl=tl, q=q_scr[...], k=k_scr[...]), b_scr, cb)
            ti = lax.broadcasted_iota(jnp.int32, (SUB, HD), 0)

            def zbody(c, carry):
                for i in range(NSUB):
                    r0 = pl.multiple_of(c * CHUNK + SUB * i, SUB)
                    qi = q_scr[pl.ds(r0, SUB), :]
                    bi = b_scr[pl.ds(r0, SUB), :]
                    for s in range(SUB):
                        krow = k_scr[pl.ds(r0 + s, 1), :]
                        brow = b_scr[pl.ds(r0 + s, 1), :]
                        if s < 8:
                            zz = qi * krow * jnp.exp(jnp.where(ti >= s, bi - brow, NEG))
                        else:
                            lo = qi[8:] * krow * jnp.exp(jnp.where(ti[8:] >= s, bi[8:] - brow, NEG))
                            zz = jnp.concatenate([jnp.zeros((8, HD), F32), lo], axis=0)
                        z_scr[i, pl.ds(pl.multiple_of(c * SUB, SUB), SUB), s * HD:(s + 1) * HD] = zz.astype(MM)
                return carry

            lax.fori_loop(0, nch, zbody, 0)
            adiag = [_mm(z_scr[i], e_ref[...]) for i in range(NSUB)]
            offs = [[_mm_nt(sub["qo"][rs], _pad_rows(sub["ko"][i][rs])) for i in range(1, NSUB)] for rs in chunks]
            for c, rs in enumerate(chunks):
                dparts = []
                for i in range(NSUB):
                    blk = adiag[i][c * SUB:(c + 1) * SUB]
                    dparts.append(blk if i == 0 else pltpu.roll(blk, SUB * i, 1))
                ad_scr[rs, :] = _by_subblock(sbc, [None] + offs[c]) + jnp.concatenate(dparts, axis=0)

        kv = [_mm_tn(p_ref[2, rs, :], kd_scr[rs, :]) for rs in chunks]
        a_ref[0] = ad_scr[...]
        o_intra = [_mm(ad_scr[rs, :], _pad_rows(p_ref[2, rs, :])) for rs in chunks]
        states = []
        st = st_scr[...]
        for c in range(nch):
            states.append(st)
            st = st * jnp.exp(b_scr[pl.ds(c * CHUNK + CHUNK - 1, 1), :]) + kv[c]
        st_scr[...] = st
        g = g_ref[...]
        for c, rs in enumerate(chunks):
            s_ref[0, c] = states[c]
            o = o_intra[c] + _mm_nt(qe_scr[rs, :], states[c])
            o_ref[rs, :] = o
            og = p_ref[3, rs, :]
            oa_ref[rs, :] = (o * lax.rsqrt(_rowmean(o * o) + EPS) * g * og * _sig(og)).astype(ACT)

    out = _gridded(
        body, carry, name="hgrn_fwd", grid=(HEADS, ncb),
        in_specs=[pl.BlockSpec((4, cb, HD), lambda h, c: (0, c, h)),
                  pl.BlockSpec((2, HD), lambda h, c: (0, h)),
                  pl.BlockSpec((1, HD), lambda h, c: (0, h)),
                  pl.BlockSpec((SUB * HD, HD), lambda h, c: (0, 0))],
        out_specs=[pl.BlockSpec((cb, HD), lambda h, c: (c, h)),
                   pl.BlockSpec((cb, HD), lambda h, c: (c, h)),
                   pl.BlockSpec((1, cb, HD), lambda h, c: (h, c, 0)),
                   pl.BlockSpec((1, nch, HD, HD), lambda h, c: (h, c, 0, 0))],
        out_shape=[jax.ShapeDtypeStruct((T, D), F32), jax.ShapeDtypeStruct((T, D), ACT),
                   jax.ShapeDtypeStruct((HEADS, T, HD), F32),
                   jax.ShapeDtypeStruct((HEADS, T // CHUNK, HD, HD), F32)],
        scratch_shapes=[pltpu.VMEM((HD, HD), F32), pltpu.VMEM((cb, HD), F32), pltpu.VMEM((cb, HD), F32),
                        pltpu.VMEM((cb, HD), F32), pltpu.VMEM((NSUB, nch * SUB, SUB * HD), MM),
                        pltpu.VMEM((cb, HD), F32), pltpu.VMEM((cb, HD), MM), pltpu.VMEM((cb, HD), MM),
                        pltpu.VMEM((cb, HD), MM)],
    )(p, hgrn_lb, hgrn_g, e_mat)
    return out[:4], out[4:]


def _hgrn_bwd(p, o, a_all, s_all, doa, hgrn_lb, hgrn_g, dp, carry):
    T = p.shape[1]
    cb = min(HGRN_BLOCK, T)
    nch = cb // CHUNK
    ncb = T // cb
    _, et_mat = _hgrn_consts()

    def body(p_ref, o_ref, a_ref, s_ref, doa_ref, lb_ref, g_ref, et_ref, dp_in, dp_ref, sm_ref,
             dst_scr, q_scr, k_scr, b_scr, x_scr, dqd_scr, dkd_scr):
        del dp_in

        @pl.when(pl.program_id(1) == 0)
        def _():
            dst_scr[...] = jnp.zeros_like(dst_scr)
            sm_ref[...] = jnp.zeros_like(sm_ref)

        qr = p_ref[0]
        v = p_ref[2]
        og = p_ref[3]
        pre = _hgrn_pre(qr, p_ref[1], lb_ref, b_scr, cb)
        q, k = pre["q"], pre["k"]
        g = g_ref[...]
        ov = o_ref[...]
        r = lax.rsqrt(_rowmean(ov * ov) + EPS)
        oh = ov * r
        sgo = _sig(og)
        doa_v = doa_ref[...]
        don = doa_v * og * sgo
        dog = doa_v * oh * g * sgo * (1.0 + og * (1.0 - sgo))
        sm_ref[1:2, :] += _colsum(don * oh)
        doh = don * g
        do = r * (doh - oh * _rowmean(doh * oh))

        sbc = lax.broadcasted_iota(jnp.int32, (CHUNK, HD), 0) // SUB
        row_i = lax.broadcasted_iota(jnp.int32, (CHUNK, HD), 0)
        lane_i = lax.broadcasted_iota(jnp.int32, (CHUNK, HD), 1)
        causal = lane_i <= row_i
        chunks = [slice(c * CHUNK, (c + 1) * CHUNK) for c in range(nch)]
        da_parts = [jnp.where(causal, _mm_nt(do[rs], _pad_rows(v[rs])), 0.0) for rs in chunks]
        dv_parts = [_mm_tn(a_ref[0, rs, :], do[rs])[:CHUNK] for rs in chunks]

        @pl.when(pre["safe"])
        def _():
            hi = dict(preferred_element_type=F32, precision=lax.Precision.HIGH)
            for c, rs in enumerate(chunks):
                dqd_scr[rs, :] = pre["eb"][rs] * lax.dot_general(
                    da_parts[c], _pad_rows(pre["kf"][rs]), (((1,), (0,)), ((), ())), **hi)
                dkd_scr[rs, :] = pre["ekf"][rs] * lax.dot_general(
                    da_parts[c], pre["qe"][rs], (((0,), (0,)), ((), ())), **hi)[:CHUNK]

        @pl.when(jnp.logical_not(pre["safe"]))
        def _():
            sub = _hgrn_sub(pre, b_scr, cb)
            dqoff_mm = [[_mm(da_parts[c], _pad_rows(sub["ko"][i][rs])) for i in range(1, NSUB)]
                        for c, rs in enumerate(chunks)]
            dkoff_mm = [[_mm_tn(jnp.where(sbc == i, da_parts[c], 0.0), sub["qo"][rs])[:CHUNK]
                         for i in range(1, NSUB)] for c, rs in enumerate(chunks)]
            dqoff_parts = [_by_subblock(sbc, [None] + dqoff_mm[c]) for c in range(nch)]
            dkoff_parts = []
            for c, rs in enumerate(chunks):
                dko = sub["eko"][1][rs] * dkoff_mm[c][0]
                for i in range(2, NSUB):
                    dko = dko + sub["eko"][i][rs] * dkoff_mm[c][i - 1]
                dkoff_parts.append(dko)
            q_scr[...] = q
            k_scr[...] = k
            for i in range(NSUB):
                rows = []
                for c in range(nch):
                    blk = da_parts[c][SUB * i:SUB * (i + 1)]
                    rows.append(blk if i == 0 else pltpu.roll(blk, HD - SUB * i, 1))
                x_scr[i] = _mm(jnp.concatenate(rows, axis=0), et_ref[...])
            ti = lax.broadcasted_iota(jnp.int32, (SUB, HD), 0)

            def dbody(c, carry):
                for i in range(NSUB):
                    r0 = pl.multiple_of(c * CHUNK + SUB * i, SUB)
                    qi = q_scr[pl.ds(r0, SUB), :]
                    bi = b_scr[pl.ds(r0, SUB), :]
                    dq_hi = jnp.zeros((8, HD), F32)
                    dq_lo = jnp.zeros((8, HD), F32)
                    dk_hi = jnp.zeros((8, HD), F32)
                    dk_lo = jnp.zeros((8, HD), F32)
                    c0 = pl.multiple_of(c * SUB, SUB)
                    t8 = ti[:8]
                    for s in range(SUB):
                        krow = k_scr[pl.ds(r0 + s, 1), :]
                        brow = b_scr[pl.ds(r0 + s, 1), :]
                        w_lo = (x_scr[i, pl.ds(c0 + 8, 8), s * HD:(s + 1) * HD]
                                * jnp.exp(jnp.where(t8 + 8 >= s, bi[8:] - brow, NEG)))
                        dq_lo = dq_lo + w_lo * krow
                        col = _colsum(w_lo * qi[8:])
                        if s < 8:
                            w_hi = (x_scr[i, pl.ds(c0, 8), s * HD:(s + 1) * HD]
                                    * jnp.exp(jnp.where(t8 >= s, bi[:8] - brow, NEG)))
                            dq_hi = dq_hi + w_hi * krow
                            dk_hi = jnp.where(t8 == s, col + _colsum(w_hi * qi[:8]), dk_hi)
                        else:
                            dk_lo = jnp.where(t8 + 8 == s, col, dk_lo)
                    dqd_scr[pl.ds(r0, SUB), :] = jnp.concatenate([dq_hi, dq_lo], axis=0)
                    dkd_scr[pl.ds(r0, SUB), :] = jnp.concatenate([dk_hi, dk_lo], axis=0)
                return carry

            lax.fori_loop(0, nch, dbody, 0)
            dqd_scr[...] += jnp.concatenate(dqoff_parts, axis=0) * sub["eqo"]
            dkd_scr[...] += jnp.concatenate(dkoff_parts, axis=0)

        qdo = [_mm_tn(do[rs], pre["qe"][rs]) for rs in chunks]
        dsts = [None] * nch
        dst = dst_scr[...]
        for c in reversed(range(nch)):
            dsts[c] = dst
            dst = dst * jnp.exp(b_scr[pl.ds(c * CHUNK + CHUNK - 1, 1), :]) + qdo[c]
        dst_scr[...] = dst
        sts = [s_ref[0, c] for c in range(nch)]
        dqe_parts = [_mm(do[rs], sts[c]) for c, rs in enumerate(chunks)]
        dkdec_parts = [_mm(v[rs], dsts[c]) for c, rs in enumerate(chunks)]
        dvi_parts = [_mm_nt(pre["kd"][rs], dsts[c]) for c, rs in enumerate(chunks)]
        debl_parts = [_colsum(dsts[c] * sts[c]) for c in range(nch)]
        dqe = jnp.concatenate(dqe_parts, axis=0)
        dkdec = jnp.concatenate(dkdec_parts, axis=0)
        dq_tot = dqd_scr[...] + dqe * pre["eb"]
        dk_inter = dkdec * pre["ekd"]
        dk_tot = dkd_scr[...] + dk_inter
        db = q * dq_tot - k * dk_tot
        kdk = k * dk_inter
        dbl = jnp.concatenate(
            [jnp.broadcast_to(jnp.exp(b_scr[pl.ds(c * CHUNK + CHUNK - 1, 1), :]) * debl_parts[c]
                              + _colsum(kdk[c * CHUNK:(c + 1) * CHUNK]), (CHUNK, HD)) for c in range(nch)], axis=0)
        tl = pre["tl"]
        rc = db
        sh = 1
        while sh < CHUNK:
            rc = rc + jnp.where(tl + sh < CHUNK, pltpu.roll(rc, cb - sh, 0), 0.0)
            sh *= 2
        dlf = rc + dbl
        dfg = dlf / pre["fg"] - dk_tot
        sf = pre["sf"]
        lb = pre["lb"]
        sm_ref[0:1, :] += _colsum(dfg * (1.0 - sf))
        sq = pre["sq"]
        dp_ref[0] = (dq_tot * Q_SCALE * sq * (1.0 + qr * (1.0 - sq))).astype(ACT)
        dp_ref[1] = (dfg * (1.0 - lb) * sf * (1.0 - sf)).astype(ACT)
        dp_ref[2] = (jnp.concatenate(dv_parts, axis=0) + jnp.concatenate(dvi_parts, axis=0)).astype(ACT)
        dp_ref[3] = dog.astype(ACT)

    rev = lambda c: ncb - 1 - c
    out = _gridded(
        body, carry, name="hgrn_bwd", grid=(HEADS, ncb),
        in_specs=[pl.BlockSpec((4, cb, HD), lambda h, c: (0, rev(c), h)),
                  pl.BlockSpec((cb, HD), lambda h, c: (rev(c), h)),
                  pl.BlockSpec((1, cb, HD), lambda h, c: (h, rev(c), 0)),
                  pl.BlockSpec((1, nch, HD, HD), lambda h, c: (h, rev(c), 0, 0)),
                  pl.BlockSpec((cb, HD), lambda h, c: (rev(c), h)),
                  pl.BlockSpec((2, HD), lambda h, c: (0, h)),
                  pl.BlockSpec((1, HD), lambda h, c: (0, h)),
                  pl.BlockSpec((HD, SUB * HD), lambda h, c: (0, 0)),
                  pl.BlockSpec(memory_space=pl.ANY)],
        out_specs=[pl.BlockSpec((4, cb, HD), lambda h, c: (0, rev(c), h)),
                   pl.BlockSpec((8, HD), lambda h, c: (0, h))],
        out_shape=[jax.ShapeDtypeStruct(dp.shape, dp.dtype), jax.ShapeDtypeStruct((8, D), F32)],
        aliases={8: 0},
        scratch_shapes=[pltpu.VMEM((HD, HD), F32), pltpu.VMEM((cb, HD), F32), pltpu.VMEM((cb, HD), F32),
                        pltpu.VMEM((cb, HD), F32), pltpu.VMEM((NSUB, nch * SUB, SUB * HD), F32),
                        pltpu.VMEM((cb, HD), F32), pltpu.VMEM((cb, HD), F32)],
    )(p, o, a_all, s_all, doa, hgrn_lb, hgrn_g, et_mat, dp)
    return out[:2], out[2:]


def _ln_fwd(u1, g, b):
    mu = _rowmean(u1)
    xc = u1 - mu
    rs = lax.rsqrt(_rowmean(xc * xc) + EPS)
    xh = xc * rs
    return xh * g + b, xh, rs


CONV_RB = 64
LANES = 128


def _shift_rows(src, sh, ls, n):
    for r in range(1, 8):
        sh[r - 1, 0:n, :] = src[pl.ds(r, n), ls]


def _tap(src, sh, ls, off, r0, rows):
    r = off % 8
    if r == 0:
        return src[pl.ds(r0 + off, rows), ls]
    return sh[r - 1, pl.ds(r0 + off - r, rows), :]


def _conv_fwd(p, cw, cb_, lng, lnb, carry):
    T = p.shape[1]
    tm = min(512, T)
    n = HALO + tm - 8

    def body(p_ref, cw_ref, cb_ref, g_ref, b_ref, u1_ref, u2_ref, buf, sh):
        @pl.when(pl.program_id(0) == 0)
        def _():
            buf[0:HALO, :] = jnp.zeros((HALO, D), F32)

        buf[HALO:HALO + tm, :] = p_ref[0] * _sig(p_ref[1])
        for lb in range(D // LANES):
            ls = slice(lb * LANES, (lb + 1) * LANES)
            _shift_rows(buf, sh, ls, n)
            taps = [cw_ref[j:j + 1, ls] for j in range(CONV_K)]
            bias = cb_ref[:, ls]

            def rows_body(rb, carry):
                r0 = pl.multiple_of(rb * CONV_RB, CONV_RB)
                acc = jnp.broadcast_to(bias, (CONV_RB, LANES))
                for j in range(CONV_K):
                    acc = acc + taps[j] * _tap(buf, sh, ls, HALO - (CONV_K - 1) + j, r0, CONV_RB)
                u1_ref[pl.ds(r0, CONV_RB), ls] = acc
                return carry

            lax.fori_loop(0, tm // CONV_RB, rows_body, 0)
        y, _, _ = _ln_fwd(u1_ref[...], g_ref[...], b_ref[...])
        u2_ref[...] = (y * _sig(y)).astype(ACT)
        buf[0:HALO, :] = buf[tm:tm + HALO, :]

    out = _gridded(
        body, carry, name="conv_fwd", grid=(T // tm,),
        in_specs=[pl.BlockSpec((2, tm, D), lambda i: (2, i, 0)), pl.BlockSpec((HALO, D), lambda i: (0, 0)),
                  pl.BlockSpec((1, D), lambda i: (0, 0)), pl.BlockSpec((1, D), lambda i: (0, 0)),
                  pl.BlockSpec((1, D), lambda i: (0, 0))],
        out_specs=[pl.BlockSpec((tm, D), lambda i: (i, 0)), pl.BlockSpec((tm, D), lambda i: (i, 0))],
        out_shape=[jax.ShapeDtypeStruct((T, D), F32), jax.ShapeDtypeStruct((T, D), ACT)],
        scratch_shapes=[pltpu.VMEM((HALO + tm, D), F32), pltpu.VMEM((7, n, LANES), F32)],
    )(p, cw, cb_, lng, lnb)
    return out[:2], out[2:]


def _conv_bwd(p, u1, du2, cw, lng, lnb, dp, carry):
    T = p.shape[1]
    tm = min(512, T)
    ni = T // tm
    hb = tm // HALO

    n = HALO + tm - 8

    def body(p_ref, ph_ref, u1_ref, du2_ref, cw_ref, g_ref, b_ref, dp_in, dp_ref, dcw_ref, sm_ref, ubuf, dbuf,
             sh, dacc):
        del dp_in
        step = pl.program_id(0)

        @pl.when(step == 0)
        def _():
            dbuf[tm:tm + HALO, :] = jnp.zeros((HALO, D), F32)
            dcw_ref[...] = jnp.zeros_like(dcw_ref)
            sm_ref[...] = jnp.zeros_like(sm_ref)

        ua = p_ref[0]
        sgb = _sig(p_ref[1])
        halo = ph_ref[0] * _sig(ph_ref[1])
        ubuf[0:HALO, :] = jnp.where(step == ni - 1, 0.0, halo)
        ubuf[HALO:HALO + tm, :] = ua * sgb
        g = g_ref[...]
        y, xh, rs = _ln_fwd(u1_ref[...], g, b_ref[...])
        sy = _sig(y)
        dy = du2_ref[...] * sy * (1.0 + y * (1.0 - sy))
        sm_ref[1:2, :] += _colsum(dy * xh)
        sm_ref[2:3, :] += _colsum(dy)
        dxh = dy * g
        du1 = rs * (dxh - _rowmean(dxh) - xh * _rowmean(dxh * xh))
        sm_ref[0:1, :] += _colsum(du1)
        dbuf[0:tm, :] = du1
        for lb in range(D // LANES):
            ls = slice(lb * LANES, (lb + 1) * LANES)
            taps = [cw_ref[j:j + 1, ls] for j in range(CONV_K)]
            _shift_rows(dbuf, sh, ls, n)

            def du0_body(rb, carry):
                r0 = pl.multiple_of(rb * CONV_RB, CONV_RB)
                acc = jnp.zeros((CONV_RB, LANES), F32)
                for j in range(CONV_K):
                    acc = acc + taps[j] * _tap(dbuf, sh, ls, CONV_K - 1 - j, r0, CONV_RB)
                dp_ref[0, pl.ds(r0, CONV_RB), ls] = acc.astype(ACT)
                return carry

            lax.fori_loop(0, tm // CONV_RB, du0_body, 0)
            _shift_rows(ubuf, sh, ls, n)
            dacc[...] = jnp.zeros_like(dacc)

            def dcw_body(rb, carry):
                r0 = pl.multiple_of(rb * CONV_RB, CONV_RB)
                d = dbuf[pl.ds(r0, CONV_RB), ls]
                for j in range(CONV_K):
                    prod = d * _tap(ubuf, sh, ls, HALO - (CONV_K - 1) + j, r0, CONV_RB)
                    dacc[8 * j:8 * j + 8, :] += jnp.sum(prod.reshape(CONV_RB // 8, 8, LANES), axis=0)
                return carry

            lax.fori_loop(0, tm // CONV_RB, dcw_body, 0)
            for j in range(CONV_K):
                dcw_ref[j:j + 1, ls] += _colsum(dacc[8 * j:8 * j + 8, :])
        du0 = dp_ref[0].astype(F32)
        dp_ref[0] = (du0 * sgb).astype(ACT)
        dp_ref[1] = (du0 * ua * sgb * (1.0 - sgb)).astype(ACT)
        dbuf[tm:tm + HALO, :] = dbuf[0:HALO, :]

    rev = lambda i: ni - 1 - i
    out = _gridded(
        body, carry, name="conv_bwd", grid=(ni,),
        in_specs=[pl.BlockSpec((2, tm, D), lambda i: (2, rev(i), 0)),
                  pl.BlockSpec((2, HALO, D), lambda i: (2, jnp.maximum(rev(i) * hb - 1, 0), 0)),
                  pl.BlockSpec((tm, D), lambda i: (rev(i), 0)), pl.BlockSpec((tm, D), lambda i: (rev(i), 0)),
                  pl.BlockSpec((HALO, D), lambda i: (0, 0)), pl.BlockSpec((1, D), lambda i: (0, 0)),
                  pl.BlockSpec((1, D), lambda i: (0, 0)), pl.BlockSpec(memory_space=pl.ANY)],
        out_specs=[pl.BlockSpec((2, tm, D), lambda i: (2, rev(i), 0)),
                   pl.BlockSpec((HALO, D), lambda i: (0, 0)), pl.BlockSpec((8, D), lambda i: (0, 0))],
        out_shape=[jax.ShapeDtypeStruct(dp.shape, dp.dtype), jax.ShapeDtypeStruct((HALO, D), F32),
                   jax.ShapeDtypeStruct((8, D), F32)],
        aliases={7: 0},
        scratch_shapes=[pltpu.VMEM((HALO + tm, D), F32), pltpu.VMEM((tm + HALO, D), F32),
                        pltpu.VMEM((7, n, LANES), F32), pltpu.VMEM((8 * CONV_K, LANES), F32)],
    )(p, p, u1, du2, cw, lng, lnb, dp)
    return out[:3], out[3:]


def _mixout_fwd(x, oa, u2, p, mod, mo, w_a, w_b, w_o):
    T = x.shape[0]
    tm = min(512, T)

    def body(x_ref, oa_ref, u2_ref, p_ref, mod_ref, wa_ref, wb_ref, wo_ref, xo_ref, ya_ref, yb_ref, mo_ref):
        ya = _mm(oa_ref[...], wa_ref[...])
        yb = _mm(u2_ref[...], wb_ref[...])
        ya_ref[...] = ya.astype(ACT)
        yb_ref[...] = yb.astype(ACT)
        merged = _sig(p_ref[0]) * ya + _sig(p_ref[1]) * yb
        out = _mm(merged, wo_ref[...])
        mo_ref[...] = out
        xo_ref[...] = x_ref[...] + mod_ref[mo + 2:mo + 3, :] * out

    tile = pl.BlockSpec((tm, D), lambda i: (i, 0))
    wspec = pl.BlockSpec((D, D), lambda i: (0, 0))
    return pl.pallas_call(
        body, name="mixout_fwd", grid=(T // tm,),
        in_specs=[tile, tile, tile, pl.BlockSpec((2, tm, D), lambda i: (3, i, 0)),
                  pl.BlockSpec((9, D), lambda i: (0, 0)), wspec, wspec, wspec],
        out_specs=[tile, tile, tile, tile],
        out_shape=[jax.ShapeDtypeStruct((T, D), F32), jax.ShapeDtypeStruct((T, D), ACT),
                   jax.ShapeDtypeStruct((T, D), ACT), jax.ShapeDtypeStruct((T, D), F32)],
        compiler_params=_cparams(1),
    )(x, oa, u2, p, mod, w_a, w_b, w_o)


def _mixout_bwd(dxo, oa, u2, ya, yb, mout, p, mod, mo, w_a, w_b, w_o):
    T = dxo.shape[0]
    tm = min(256, T)

    def body(dxo_ref, oa_ref, u2_ref, ya_ref, yb_ref, mo_ref, p_ref, mod_ref, wa_ref, wb_ref, wo_ref,
             dp_ref, doa_ref, du2_ref, dwa_ref, dwb_ref, dwo_ref, sm_ref):
        @pl.when(pl.program_id(0) == 0)
        def _():
            dwa_ref[...] = jnp.zeros_like(dwa_ref)
            dwb_ref[...] = jnp.zeros_like(dwb_ref)
            dwo_ref[...] = jnp.zeros_like(dwo_ref)
            sm_ref[...] = jnp.zeros_like(sm_ref)

        dxo_v = dxo_ref[...]
        sm_ref[2:3, :] += _colsum(dxo_v * mo_ref[...])
        dmo = (mod_ref[mo + 2:mo + 3, :] * dxo_v).astype(MM)
        ya = ya_ref[...].astype(F32)
        yb = yb_ref[...].astype(F32)
        sga = _sig(p_ref[0])
        sgb = _sig(p_ref[1])
        merged = (sga * ya + sgb * yb).astype(MM)
        dwo_ref[...] += _mm_tn(merged, dmo)
        dmg = _mm_nt(dmo, wo_ref[...])
        dp_ref[0] = (dmg * ya * sga * (1.0 - sga)).astype(ACT)
        dp_ref[1] = (dmg * yb * sgb * (1.0 - sgb)).astype(ACT)
        dya = (dmg * sga).astype(MM)
        dyb = (dmg * sgb).astype(MM)
        dwa_ref[...] += _mm_tn(oa_ref[...], dya)
        dwb_ref[...] += _mm_tn(u2_ref[...], dyb)
        doa_ref[...] = _mm_nt(dya, wa_ref[...])
        du2_ref[...] = _mm_nt(dyb, wb_ref[...])

    tile = pl.BlockSpec((tm, D), lambda i: (i, 0))
    wspec = pl.BlockSpec((D, D), lambda i: (0, 0))
    return pl.pallas_call(
        body, name="mixout_bwd", grid=(T // tm,),
        in_specs=[tile, tile, tile, tile, tile, tile, pl.BlockSpec((2, tm, D), lambda i: (3, i, 0)),
                  pl.BlockSpec((9, D), lambda i: (0, 0)), wspec, wspec, wspec],
        out_specs=[pl.BlockSpec((2, tm, D), lambda i: (3, i, 0)), tile, tile, wspec, wspec, wspec,
                   pl.BlockSpec((8, D), lambda i: (0, 0))],
        out_shape=[jax.ShapeDtypeStruct((8, T, D), ACT), jax.ShapeDtypeStruct((T, D), F32),
                   jax.ShapeDtypeStruct((T, D), F32), jax.ShapeDtypeStruct((D, D), F32),
                   jax.ShapeDtypeStruct((D, D), F32), jax.ShapeDtypeStruct((D, D), F32),
                   jax.ShapeDtypeStruct((8, D), F32)],
        compiler_params=_cparams(1),
    )(dxo, oa, u2, ya, yb, mout, p, mod, w_a, w_b, w_o)


def _adamw_ada_w(w, m, v, cs_all, dmod_cols):
    R, C = w.shape
    tr = 256
    cs_t = jnp.pad(cs_all.T, ((0, 0), (0, HD - N_DEV)))
    dm = jnp.pad(dmod_cols, ((0, HD - N_DEV), (0, 0)))

    def body(w_ref, m_ref, v_ref, cs_ref, d_ref, go_ref, do_ref, mo_ref, vo_ref):
        gv = jnp.dot(cs_ref[...], d_ref[...], preferred_element_type=F32, precision=lax.Precision.HIGHEST)
        go_ref[...] = gv
        do_ref[...], mo_ref[...], vo_ref[...] = _adam_math(w_ref[...], gv, m_ref[...], v_ref[...])

    tile = pl.BlockSpec((tr, C), lambda i: (i, 0))
    sds = jax.ShapeDtypeStruct((R, C), F32)
    return pl.pallas_call(
        body, name="adamw_ada_w", grid=(R // tr,),
        in_specs=[tile, tile, tile, pl.BlockSpec((tr, HD), lambda i: (i, 0)), pl.BlockSpec((HD, C), lambda i: (0, 0))],
        out_specs=[tile] * 4, out_shape=[sds] * 4, compiler_params=_cparams(1))(w, m, v, cs_t, dm)


def _adam_math(w, g, m, v):
    m2 = ADAM_B1 * m + (1.0 - ADAM_B1) * g
    v2 = ADAM_B2 * v + (1.0 - ADAM_B2) * (g * g)
    m_hat = m2 / (1.0 - ADAM_B1 ** ADAM_STEP)
    v_hat = v2 / (1.0 - ADAM_B2 ** ADAM_STEP)
    delta = -ADAM_LR * (m_hat / (jnp.sqrt(v_hat) + ADAM_EPS) + ADAM_WD * w)
    return delta, m2, v2


def _adamw(w, m, v, g, name, carry=None):
    R, C = w.shape
    slots = g.ndim == 3
    n_slots = g.shape[0] if slots else 0
    tr = R
    for cand in (256, 176):
        if R % cand == 0 and R > cand:
            tr = cand
            break

    def body(w_ref, m_ref, v_ref, g_ref, go_ref, d_ref, mo_ref, vo_ref):
        if slots:
            gv = g_ref[0].astype(F32)
            for s in range(1, n_slots):
                gv = gv + g_ref[s].astype(F32)
        else:
            gv = g_ref[...]
        go_ref[...] = gv
        d_ref[...], mo_ref[...], vo_ref[...] = _adam_math(w_ref[...], gv, m_ref[...], v_ref[...])

    tile = pl.BlockSpec((tr, C), lambda i: (i, 0))
    gspec = pl.BlockSpec((n_slots, tr, C), lambda i: (0, i, 0)) if slots else tile
    sds = jax.ShapeDtypeStruct((R, C), F32)
    return _gridded(body, carry, name=name, grid=(R // tr,), in_specs=[tile, tile, tile, gspec],
                    out_specs=[tile] * 4, out_shape=[sds] * 4)(w, m, v, g)


def _adamw_small(tot, names, params, grad_rows):
    k = len(names)

    def body(tot_ref, *refs):
        ins, outs = refs[:3 * k], refs[3 * k:]
        for i, n in enumerate(names):
            w_ref, m_ref, v_ref = ins[3 * i:3 * i + 3]
            go, do, mo, vo = outs[4 * i:4 * i + 4]
            row = grad_rows[n]
            for j in range(w_ref.shape[1] // D):
                ls = slice(j * D, (j + 1) * D)
                g = tot_ref[row + j:row + j + 1, :]
                w = w_ref[:, ls]
                if n == "hgrn_lb":
                    p0 = _sig(w[0:1] - w[1:2])
                    dz0 = p0 * (1.0 - p0) * g
                    g = jnp.concatenate([dz0, -dz0], axis=0)
                go[:, ls] = g
                do[:, ls], mo[:, ls], vo[:, ls] = _adam_math(w, g, m_ref[:, ls], v_ref[:, ls])

    flat = [t for n in names for t in params[n]]
    out_shape = [jax.ShapeDtypeStruct(params[n][0].shape, F32) for n in names for _ in range(4)]
    outs = pl.pallas_call(body, name="adamw_small", out_shape=out_shape)(tot, *flat)
    return {n: tuple(outs[4 * i:4 * i + 4]) for i, n in enumerate(names)}


def _cast_shards(ws):
    n = len(ws)

    def body(*refs):
        for src, dst in zip(refs[:n], refs[n:]):
            dst[...] = src[...].astype(MM)

    return pl.pallas_call(body, name="cast_shards", out_shape=[jax.ShapeDtypeStruct(w.shape, MM) for w in ws],
                          compiler_params=pltpu.CompilerParams(vmem_limit_bytes=VMEM_LIMIT))(*ws)


def _sum_slots(pack, name, tr):
    n, R, C = pack.shape

    def body(p_ref, out_ref):
        acc = p_ref[0].astype(F32)
        for s in range(1, n):
            acc = acc + p_ref[s].astype(F32)
        out_ref[...] = acc

    return pl.pallas_call(
        body, name=name, grid=(R // tr,), in_specs=[pl.BlockSpec((n, tr, C), lambda i: (0, i, 0))],
        out_specs=pl.BlockSpec((tr, C), lambda i: (i, 0)), out_shape=jax.ShapeDtypeStruct((R, C), F32),
        compiler_params=_cparams(1))(pack)


def _me():
    return lax.axis_index("x"), lax.axis_index("y"), lax.axis_index("c")


def _peer(r):
    x, y, c = _me()
    px = 1 - x if r & 4 else x
    py = 1 - y if r & 2 else y
    pc = 1 - c if r & 1 else c
    return (px, py, pc), 4 * px + 2 * py + pc


def _small_gather(x_ref, out_ref, send_sems, recv_sems):
    R = x_ref.shape[0]
    mx, my, mc = _me()
    me = 4 * mx + 2 * my + mc
    mine = out_ref.at[pl.ds(pl.multiple_of(me * R, 8), R), :]
    copies = []
    for r in range(1, N_DEV):
        dev, _ = _peer(r)
        copies.append(pltpu.make_async_remote_copy(
            src_ref=x_ref, dst_ref=mine, send_sem=send_sems.at[r - 1], recv_sem=recv_sems.at[r - 1],
            device_id=dev, device_id_type=MESH))
    for cp in copies:
        cp.start()
    mine[...] = x_ref[...]
    for r in range(1, N_DEV):
        dev, idx = _peer(r)
        theirs = out_ref.at[pl.ds(pl.multiple_of(idx * R, 8), R), :]
        pltpu.make_async_remote_copy(
            src_ref=x_ref, dst_ref=theirs, send_sem=send_sems.at[r - 1], recv_sem=recv_sems.at[r - 1],
            device_id=dev, device_id_type=MESH).wait_recv()
    for cp in copies:
        cp.wait_send()


def _prologue(cs, ada_w, ada_b_cols, big):
    n = len(big)
    ncol = ada_w.shape[1]
    big_shape, big_sems = _xchg_specs(big, "gather")

    def body(cs_ref, w_ref, b_ref, *rest):
        big_in, cs_all, mod_all, big_out = rest[:n], rest[n], rest[n + 1], rest[n + 2:2 * n + 2]
        mod_scr, s1, r1, s2, r2 = rest[2 * n + 2:2 * n + 7]
        sems = rest[2 * n + 7:]
        _small_gather(cs_ref, cs_all, s1, r1)
        pick = (lax.broadcasted_iota(jnp.int32, (N_DEV, N_DEV * 8), 1)
                == 8 * lax.broadcasted_iota(jnp.int32, (N_DEV, N_DEV * 8), 0)).astype(F32)
        per_device = jnp.dot(pick, cs_all[...], preferred_element_type=F32, precision=lax.Precision.HIGHEST)
        mod_scr[...] = jnp.dot(per_device, w_ref[...], preferred_element_type=F32,
                               precision=lax.Precision.HIGHEST) + b_ref[...]
        _small_gather(mod_scr, mod_all, s2, r2)
        _xchg_start(big_in, big_out, sems, "gather")
        _xchg_wait(big_in, big_out, sems, "gather")

    vmem = pl.BlockSpec(memory_space=pltpu.VMEM)
    hbm = pl.BlockSpec(memory_space=pl.ANY)
    dma7 = pltpu.SemaphoreType.DMA((N_DEV - 1,))
    out = pl.pallas_call(
        body, name="prologue",
        out_shape=[jax.ShapeDtypeStruct((N_DEV * 8, D), F32), jax.ShapeDtypeStruct((N_DEV * 8, ncol), F32)]
        + big_shape,
        in_specs=[vmem, vmem, vmem] + [hbm] * n, out_specs=[vmem, vmem] + [hbm] * n,
        scratch_shapes=[pltpu.VMEM((8, ncol), F32), dma7, dma7, dma7, dma7] + big_sems,
        compiler_params=pltpu.CompilerParams(vmem_limit_bytes=VMEM_LIMIT),
    )(cs, ada_w, ada_b_cols, *big)
    return out[0], out[1], out[2:]


N_CHIP = N_DEV // 2


def _xchg_copies(ins, outs, sems, mode):
    send_sems, recv_sems, local_sems = sems
    mx, my, mc = _me()
    me = 4 * mx + 2 * my + mc
    my_chip = 2 * mx + my
    sibling = _peer(1)[0]

    def rdma(a, r, dev, src, slot):
        k = a * (N_DEV - 1) + r - 1
        return pltpu.make_async_remote_copy(
            src_ref=src, dst_ref=outs[a].at[slot], send_sem=send_sems.at[k], recv_sem=recv_sems.at[k],
            device_id=dev, device_id_type=MESH)

    own, sends, relays, recvs = [], [], [], []
    for a in range(len(ins)):
        if mode == "pair":
            for chip in range(N_CHIP):
                src = ins[a].at[2 * chip + 1 - mc]
                sends.append(rdma(a, chip + 1, sibling, src, chip))
                recvs.append(rdma(a, chip + 1, sibling, src, chip))
            continue
        if mode == "quad":
            own.append(pltpu.make_async_copy(ins[a].at[my_chip], outs[a].at[my_chip], local_sems.at[a]))
            for r in (2, 4, 6):
                dev, idx = _peer(r)
                chip = idx // 2
                sends.append(rdma(a, r, dev, ins[a].at[chip], my_chip))
                recvs.append(rdma(a, r, dev, ins[a].at[chip], chip))
            continue
        gather = mode == "gather"
        own.append(pltpu.make_async_copy(ins[a] if gather else ins[a].at[me], outs[a].at[me], local_sems.at[a]))
        for r in range(1, N_DEV):
            dev, idx = _peer(r)
            if not gather:
                sends.append(rdma(a, r, dev, ins[a].at[idx], me))
                recvs.append(rdma(a, r, dev, ins[a].at[idx], idx))
            elif r == 1:
                sends.append(rdma(a, r, dev, ins[a], me))
                recvs.append(rdma(a, r, dev, ins[a], idx))
            elif r % 2 == 0:
                sends.append(rdma(a, r, dev, ins[a], me))
                relays.append((rdma(a, r, dev, ins[a], idx), rdma(a, r + 1, sibling, outs[a].at[idx], idx)))
            else:
                recvs.append(rdma(a, r, sibling, ins[a], idx))
    return own, sends, relays, recvs


def _xchg_start(ins, outs, sems, mode):
    own, sends, _, _ = _xchg_copies(ins, outs, sems, mode)
    for cp in own + sends:
        cp.start()


def _xchg_wait(ins, outs, sems, mode):
    own, sends, relays, recvs = _xchg_copies(ins, outs, sems, mode)
    for arrival, relay in relays:
        arrival.wait_recv()
        relay.start()
    for cp in recvs:
        cp.wait_recv()
    for cp in own:
        cp.wait()
    for cp in sends + [relay for _, relay in relays]:
        cp.wait_send()


def _xchg_specs(arrays, mode):
    n = len(arrays)
    shape = {"gather": lambda s: (N_DEV,) + s, "scatter": lambda s: s, "pair": lambda s: (N_CHIP,) + s[1:],
             "quad": lambda s: s}[mode]
    out_shape = [jax.ShapeDtypeStruct(shape(a.shape), a.dtype) for a in arrays]
    sems = [pltpu.SemaphoreType.DMA((n * (N_DEV - 1),)), pltpu.SemaphoreType.DMA((n * (N_DEV - 1),)),
            pltpu.SemaphoreType.DMA((n,))]
    return out_shape, sems


def _exchange(arrays, mode, name):
    n = len(arrays)

    def body(*refs):
        _xchg_start(refs[:n], refs[n:2 * n], refs[2 * n:], mode)
        _xchg_wait(refs[:n], refs[n:2 * n], refs[2 * n:], mode)

    out_shape, sems = _xchg_specs(arrays, mode)
    return pl.pallas_call(
        body, name=name, out_shape=out_shape,
        in_specs=[pl.BlockSpec(memory_space=pl.ANY)] * n, out_specs=[pl.BlockSpec(memory_space=pl.ANY)] * n,
        scratch_shapes=sems,
    )(*arrays)


def _gridded(body, carry, *, name, grid, in_specs, out_specs, out_shape, scratch_shapes=(), aliases=None):
    if carry is None:
        return pl.pallas_call(
            body, name=name, grid=grid, in_specs=list(in_specs), out_specs=list(out_specs),
            out_shape=list(out_shape), scratch_shapes=list(scratch_shapes), input_output_aliases=aliases or {},
            compiler_params=_cparams(len(grid)))
    arrays, mode = carry
    n, n_in, n_out, n_scr = len(arrays), len(in_specs), len(out_specs), len(scratch_shapes)
    c_shape, c_sems = _xchg_specs(arrays, mode)

    def wrapped(*refs):
        ins, cin = refs[:n_in], refs[n_in:n_in + n]
        o0 = n_in + n
        outs, cout = refs[o0:o0 + n_out], refs[o0 + n_out:o0 + n_out + n]
        s0 = o0 + n_out + n
        scr, sems = refs[s0:s0 + n_scr], refs[s0 + n_scr:]
        first = pl.program_id(0) == 0
        last = pl.program_id(0) == grid[0] - 1
        for ax in range(1, len(grid)):
            first = first & (pl.program_id(ax) == 0)
            last = last & (pl.program_id(ax) == grid[ax] - 1)

        @pl.when(first)
        def _():
            _xchg_start(cin, cout, sems, mode)

        body(*ins, *outs, *scr)

        @pl.when(last)
        def _():
            _xchg_wait(cin, cout, sems, mode)

    hbm = pl.BlockSpec(memory_space=pl.ANY)
    res = pl.pallas_call(
        wrapped, name=name, grid=grid, in_specs=list(in_specs) + [hbm] * n, out_specs=list(out_specs) + [hbm] * n,
        out_shape=list(out_shape) + c_shape, scratch_shapes=list(scratch_shapes) + c_sems,
        input_output_aliases=aliases or {}, compiler_params=_cparams(len(grid)),
    )
    return lambda *args: res(*args, *arrays)


def _local_step(x, target, mod, small, sh, w1):
    w1_in, w1_out = w1[0].reshape(2, D_FF, D), w1[1].reshape(D_FF, D)
    (x1, a1, b1, f1, h1, h2), (wm_in,) = _ffn_fwd(x, mod, 0, small["norm_ffn1"], w1_in, w1_out, 0.5, "ffn1_fwd",
                                                  ([sh["mix_w_in"]], "gather"), nxt=(small["norm_mix"], 3))
    (p,), (wh_o, wc_o, wm_o, cw) = _mixin_fwd(
        h2, wm_in, ([sh["hgrn_w_o"], sh["conv_w_o"], sh["mix_w_out"], sh["conv_w"]], "gather"))
    wh_o, wc_o, wm_o = wh_o.reshape(D, D), wc_o.reshape(D, D), wm_o.reshape(D, D)
    cw = jnp.pad(cw.transpose(1, 0, 2).reshape(CONV_K, D), ((0, HALO - CONV_K), (0, 0)))
    (o, oa, a_all, s_all), (w2_in,) = _hgrn_fwd(p, small["hgrn_lb"], small["hgrn_g"], ([sh["ffn2_w_in"]], "gather"))
    (u1, u2), (w2_out,) = _conv_fwd(p, cw, small["conv_b"], small["conv_ln_g"], small["conv_ln_b"],
                                    ([sh["ffn2_w_out"]], "gather"))
    w2_in, w2_out = w2_in.reshape(2, D_FF, D), w2_out.reshape(D_FF, D)
    x2, ya, yb, mout = _mixout_fwd(x1, oa, u2, p, mod, 3, wh_o, wc_o, wm_o)
    (dx3, a3, b3, f3, h3, df3, sm_head), _ = _ffn_fwd(x2, mod, 6, small["norm_ffn2"], w2_in, w2_out, 0.5, "ffn2_fwd",
                                                      None, head=(target, small["norm_final"], 8, 0.5))

    (da3, db3, dw2_in, dw2_out), _ = _ffn_bwd_w(h3, df3, a3, b3, w2_out, "ffn2_bwd_w", None)
    rows = lambda t: t.reshape(N_DEV, -1, D).astype(MM)
    (dx2, sm3), (r2_out,) = _ffn_bwd_x(x2, dx3, f3, da3, db3, mod, 6, small["norm_ffn2"], w2_in, 0.5, "ffn2_bwd_x",
                                       ([rows(dw2_out)], "scatter"))
    dp, doa, du2, dwh_o, dwc_o, dwm_o, sm_mo = _mixout_bwd(dx2, oa, u2, ya, yb, mout, p, mod, 3, wh_o, wc_o, wm_o)
    (dp, dcw, sm_cv), (r2_in,) = _conv_bwd(p, u1, du2, cw, small["conv_ln_g"], small["conv_ln_b"], dp,
                                           ([rows(dw2_in)], "scatter"))
    (dp, sm_hg), _ = _hgrn_bwd(p, o, a_all, s_all, doa, small["hgrn_lb"], small["hgrn_g"], dp, None)
    (dx1, dwm_in, sm2, df1), (rh_o, rc_o, rm_o, rcw) = _mixin_bwd(
        x1, h2, dx2, dp, mod, 3, small["norm_mix"], wm_in, 2, 0.5,
        ([rows(dwh_o), rows(dwc_o), rows(dwm_o), dcw[:CONV_K].reshape(CONV_K, N_DEV, -1).transpose(1, 0, 2)],
         "scatter"))
    (da1, db1, dw1_in, dw1_out), (rm_in,) = _ffn_bwd_w(h1, df1, a1, b1, w1_out, "ffn1_bwd_w",
                                                      (_pair_reduce([dwm_in], "pair_mix"), "quad"))
    (dx0, sm1), (r1_in, r1_out) = _ffn_bwd_x(
        x, dx1, f1, da1, db1, mod, 0, small["norm_ffn1"], w1_in, 0.5, "ffn1_bwd_x",
        (_pair_reduce([rows(dw1_in), rows(dw1_out)], "pair_ffn1"), "quad"))

    dmod = jnp.concatenate([sm1[0:3], sm2[0:2], sm_mo[2:3], sm3[0:3]], axis=0)
    gsmall = dict(norm_ffn1=sm1[3:4], norm_mix=sm2[3:4], lb0=sm_hg[0:1], hgrn_g=sm_hg[1:2], conv_b=sm_cv[0:1],
                  conv_ln_g=sm_cv[1:2], conv_ln_b=sm_cv[2:3], norm_ffn2=sm3[3:4], norm_final=sm_head[0:1])
    recv = dict(ffn1_w_in=r1_in, ffn1_w_out=r1_out, mix_w_in=rm_in, hgrn_w_o=rh_o, conv_w=rcw, conv_w_o=rc_o,
                mix_w_out=rm_o, ffn2_w_in=r2_in, ffn2_w_out=r2_out)
    return sm_head[1, 0], dx0, dmod, gsmall, recv


def _pair_add(mine, theirs, core, name):
    _, R, C = theirs.shape

    def body(core_ref, a_ref, b_ref, out_ref):
        del core_ref
        out_ref[0] = (a_ref[0, 0].astype(F32) + b_ref[0].astype(F32)).astype(out_ref.dtype)

    blk = pl.BlockSpec((1, R, C), lambda s, core_ref: (s, 0, 0))
    grid_spec = pltpu.PrefetchScalarGridSpec(
        num_scalar_prefetch=1, grid=(N_CHIP,),
        in_specs=[pl.BlockSpec((1, 1, R, C), lambda s, core_ref: (s, core_ref[0], 0, 0)), blk], out_specs=blk)
    return pl.pallas_call(body, name=name, grid_spec=grid_spec,
                          out_shape=jax.ShapeDtypeStruct(theirs.shape, mine.dtype), compiler_params=_cparams(1),
                          )(core, mine.reshape(N_CHIP, 2, R, C), theirs)


def _pair_reduce(arrays, name):
    theirs = _exchange(arrays, "pair", name)
    core = lax.axis_index("c").astype(jnp.int32).reshape(1)
    return [_pair_add(a, t, core, "%s_add%d" % (name, i)) for i, (a, t) in enumerate(zip(arrays, theirs))]


SMALL_ORDER = ("norm_ffn1", "norm_mix", "lb0", "hgrn_g", "conv_b", "conv_ln_g", "conv_ln_b", "norm_ffn2",
               "norm_final")
PACK_ROWS = 24


def kernel(x, c, ada_w, ada_b, norm_ffn1, ffn1_w_in, ffn1_w_out, norm_mix, mix_w_in, hgrn_lb, hgrn_g, hgrn_w_o, conv_w, conv_b, conv_ln_g, conv_ln_b, conv_w_o, mix_w_out, norm_ffn2, ffn2_w_in, ffn2_w_out, norm_final, loss_target, m_ada_w, m_ada_b, m_norm_ffn1, m_ffn1_w_in, m_ffn1_w_out, m_norm_mix, m_mix_w_in, m_hgrn_lb, m_hgrn_g, m_hgrn_w_o, m_conv_w, m_conv_b, m_conv_ln_g, m_conv_ln_b, m_conv_w_o, m_mix_w_out, m_norm_ffn2, m_ffn2_w_in, m_ffn2_w_out, m_norm_final, v_ada_w, v_ada_b, v_norm_ffn1, v_ffn1_w_in, v_ffn1_w_out, v_norm_mix, v_mix_w_in, v_hgrn_lb, v_hgrn_g, v_hgrn_w_o, v_conv_w, v_conv_b, v_conv_ln_g, v_conv_ln_b, v_conv_w_o, v_mix_w_out, v_norm_ffn2, v_ffn2_w_in, v_ffn2_w_out, v_norm_final):
    mx, my, mc = _me()
    me = 4 * mx + 2 * my + mc
    ncol = ada_w.shape[2]

    sh = dict(ffn1_w_in=ffn1_w_in[0].T, ffn1_w_out=ffn1_w_out[0], mix_w_in=mix_w_in[0], hgrn_w_o=hgrn_w_o[0],
              conv_w_o=conv_w_o[0], mix_w_out=mix_w_out[0], ffn2_w_in=ffn2_w_in[0].T, ffn2_w_out=ffn2_w_out[0])
    sh = dict(zip(sh, _cast_shards(list(sh.values()))))
    sh["conv_w"] = conv_w[0]
    small = dict(norm_ffn1=norm_ffn1, norm_mix=norm_mix, hgrn_lb=hgrn_lb, hgrn_g=hgrn_g, conv_b=conv_b,
                 conv_ln_g=conv_ln_g, conv_ln_b=conv_ln_b, norm_ffn2=norm_ffn2, norm_final=norm_final.reshape(1, D))

    cs = jnp.broadcast_to(c * jax.nn.sigmoid(c), (8, D))
    ada_b_cols = lax.dynamic_slice(ada_b, (0, me * ncol), (1, ncol))
    cs_all, mod_all, w1 = _prologue(cs, ada_w[0], ada_b_cols, [sh["ffn1_w_in"], sh["ffn1_w_out"]])
    cs_all = cs_all.reshape(N_DEV, 8, D)[:, 0, :]
    mod = lax.dynamic_index_in_dim(mod_all.reshape(N_DEV, N_DEV, ncol), me, axis=1, keepdims=False).reshape(9, D)

    loss_local, dx, dmod, gsmall, recv = _local_step(x[0], loss_target[0], mod, small, sh, w1)

    n_used = 9 + len(SMALL_ORDER) + 1
    pack = jnp.concatenate([dmod] + [gsmall[n] for n in SMALL_ORDER] + [jnp.broadcast_to(loss_local, (1, D))]
                           + [jnp.zeros((PACK_ROWS - n_used, D), F32)], axis=0)
    res = {}
    *res["ffn2_w_out"], pack_all = _adamw(ffn2_w_out[0], m_ffn2_w_out[0], v_ffn2_w_out[0], recv["ffn2_w_out"],
                                          "adamw_ffn2_w_out", ([pack], "gather"))
    tot = _sum_slots(pack_all, "sum_small", PACK_ROWS)
    loss = tot[n_used - 1, 0]
    dmod_all = pack_all[:, 0:9, :].reshape(N_DEV, 9 * D)
    dmod_cols = lax.dynamic_slice(dmod_all, (0, me * ncol), (N_DEV, ncol))

    res["ada_w"] = _adamw_ada_w(ada_w[0], m_ada_w[0], v_ada_w[0], cs_all, dmod_cols)
    big = dict(ffn1_w_in=(ffn1_w_in, m_ffn1_w_in, v_ffn1_w_in), ffn1_w_out=(ffn1_w_out, m_ffn1_w_out, v_ffn1_w_out),
               mix_w_in=(mix_w_in, m_mix_w_in, v_mix_w_in), hgrn_w_o=(hgrn_w_o, m_hgrn_w_o, v_hgrn_w_o),
               conv_w=(conv_w, m_conv_w, v_conv_w), conv_w_o=(conv_w_o, m_conv_w_o, v_conv_w_o),
               mix_w_out=(mix_w_out, m_mix_w_out, v_mix_w_out), ffn2_w_in=(ffn2_w_in, m_ffn2_w_in, v_ffn2_w_in),
               ffn2_w_out=(ffn2_w_out, m_ffn2_w_out, v_ffn2_w_out))
    for n, (w, m, v) in big.items():
        if n in res:
            continue
        if n in ("ffn1_w_in", "ffn2_w_in"):
            res[n] = tuple(t.T for t in _adamw(w[0].T, m[0].T, v[0].T, recv[n], "adamw_" + n))
        else:
            res[n] = _adamw(w[0], m[0], v[0], recv[n], "adamw_" + n)
    sm_names = ("ada_b", "norm_ffn1", "norm_mix", "hgrn_lb", "hgrn_g", "conv_b", "conv_ln_g", "conv_ln_b",
                "norm_ffn2", "norm_final")
    sm_w = dict(ada_b=(ada_b, m_ada_b, v_ada_b), norm_ffn1=(norm_ffn1, m_norm_ffn1, v_norm_ffn1),
                norm_mix=(norm_mix, m_norm_mix, v_norm_mix), hgrn_lb=(hgrn_lb, m_hgrn_lb, v_hgrn_lb),
                hgrn_g=(hgrn_g, m_hgrn_g, v_hgrn_g), conv_b=(conv_b, m_conv_b, v_conv_b),
                conv_ln_g=(conv_ln_g, m_conv_ln_g, v_conv_ln_g), conv_ln_b=(conv_ln_b, m_conv_ln_b, v_conv_ln_b),
                norm_ffn2=(norm_ffn2, m_norm_ffn2, v_norm_ffn2), norm_final=(norm_final, m_norm_final, v_norm_final))
    sm_w["norm_final"] = tuple(t.reshape(1, D) for t in sm_w["norm_final"])
    grad_rows = dict({n: 9 + i for i, n in enumerate(SMALL_ORDER)}, ada_b=0, hgrn_lb=9 + SMALL_ORDER.index("lb0"))
    res.update(_adamw_small(tot, sm_names, sm_w, grad_rows))
    res["norm_final"] = tuple(t.reshape(norm_final.shape) for t in res["norm_final"])

    order = ("ada_w", "ada_b", "norm_ffn1", "ffn1_w_in", "ffn1_w_out", "norm_mix", "mix_w_in", "hgrn_lb", "hgrn_g",
             "hgrn_w_o", "conv_w", "conv_b", "conv_ln_g", "conv_ln_b", "conv_w_o", "mix_w_out", "norm_ffn2",
             "ffn2_w_in", "ffn2_w_out", "norm_final")
    lead = lambda n, t: t[None] if n in big or n == "ada_w" else t
    outs = [loss, dx[None]]
    for j in range(4):
        outs += [lead(n, res[n][j]) for n in order]
    return tuple(outs)
```

```python
import jax
import jax.numpy as jnp
from jax import lax
from jax.experimental import pallas as pl
from jax.experimental.pallas import tpu as pltpu

F32 = jnp.float32
MM = jnp.bfloat16
ACT = jnp.bfloat16

D = 1024
D_FF = 2816
HEADS = 8
HD = 128
CHUNK = 64
SUB = 16
NSUB = CHUNK // SUB
HGRN_BLOCK = 1024
SAFE_EXP = 60.0
CONV_K = 31
HALO = 32
EPS = 1e-6
N_DEV = 8
NEG = -1e30
Q_SCALE = HD ** -0.5

ADAM_LR = 0.001
ADAM_B1 = 0.9
ADAM_B2 = 0.999
ADAM_EPS = 1e-08
ADAM_WD = 0.01
ADAM_STEP = 10

V7X_VMEM_BYTES = 64 * 1024 * 1024
VMEM_LIMIT = V7X_VMEM_BYTES - 4 * 1024 * 1024
MESH = pl.DeviceIdType.MESH


def _cparams(n_axes):
    return pltpu.CompilerParams(dimension_semantics=("arbitrary",) * n_axes, vmem_limit_bytes=VMEM_LIMIT)


def _mm(a, b):
    return lax.dot_general(a.astype(MM), b.astype(MM), (((1,), (0,)), ((), ())), preferred_element_type=F32)


def _mm_nt(a, b):
    return lax.dot_general(a.astype(MM), b.astype(MM), (((1,), (1,)), ((), ())), preferred_element_type=F32)


def _mm_tn(a, b):
    return lax.dot_general(a.astype(MM), b.astype(MM), (((0,), (0,)), ((), ())), preferred_element_type=F32)


def _sig(x):
    return 1.0 / (1.0 + jnp.exp(-x))


def _colsum(x):
    return jnp.sum(x, axis=0, keepdims=True)


def _rowmean(x):
    return jnp.mean(x, axis=-1, keepdims=True)


def _modnorm_fwd(xv, g, sh, sc):
    r = lax.rsqrt(_rowmean(xv * xv) + EPS)
    xh = xv * r
    n = xh * g
    return n * (1.0 + sc) + sh, xh, n, r


def _modnorm_bwd(dh, xh, n, r, g, sc):
    dsc = _colsum(dh * n)
    dsh = _colsum(dh)
    dn = dh * (1.0 + sc)
    dg = _colsum(dn * xh)
    dxh = dn * g
    dx = r * (dxh - xh * _rowmean(dxh * xh))
    return dx, dsh, dsc, dg


def _ffn_fwd(x, mod, mo, gnorm, w_in_t, w_out, res, name, carry, nxt=None, head=None):
    T = x.shape[0]
    tm = min(512, T)
    tn = D_FF // 2
    ni = T // tm

    def body(x_ref, mod_ref, g_ref, wi_ref, wo_ref, *rest):
        if head is not None:
            t_ref, gf_ref, xo_ref, a_ref, b_ref, f_ref, h_ref, df_ref, sm_ref = rest
        elif nxt is None:
            xo_ref, a_ref, b_ref, f_ref, h_ref = rest
        else:
            gn_ref, xo_ref, a_ref, b_ref, f_ref, h_ref, hn_ref = rest
        xv = x_ref[...]
        h, _, _, _ = _modnorm_fwd(xv, g_ref[...], mod_ref[mo:mo + 1, :], mod_ref[mo + 1:mo + 2, :])
        h = h.astype(ACT)
        h_ref[...] = h
        f = None
        for c0 in range(0, D_FF, tn):
            a = _mm_nt(h, wi_ref[0, c0:c0 + tn, :])
            b = _mm_nt(h, wi_ref[1, c0:c0 + tn, :])
            a_ref[:, c0:c0 + tn] = a.astype(ACT)
            b_ref[:, c0:c0 + tn] = b.astype(ACT)
            part = _mm(a * _sig(a) * b, wo_ref[c0:c0 + tn, :])
            f = part if f is None else f + part
        f_ref[...] = f
        xo = xv + res * mod_ref[mo + 2:mo + 3, :] * f
        if nxt is not None:
            hn, _, _, _ = _modnorm_fwd(xo, gn_ref[...], mod_ref[nxt[1]:nxt[1] + 1, :], mod_ref[nxt[1] + 1:nxt[1] + 2, :])
            hn_ref[...] = hn.astype(ACT)
        if head is None:
            xo_ref[...] = xo
            return
        i = pl.program_id(0)

        @pl.when(i == 0)
        def _():
            sm_ref[...] = jnp.zeros_like(sm_ref)

        gf = gf_ref[...]
        r = lax.rsqrt(_rowmean(xo * xo) + EPS)
        xh = xo * r
        e = xh * gf - t_ref[...]
        sm_ref[1:2, :] += _colsum(e * e) * (0.5 / D)
        dy = e * (1.0 / D)
        sm_ref[0:1, :] += _colsum(dy * xh)
        dxh = dy * gf
        dx = r * (dxh - xh * _rowmean(dxh * xh))
        xo_ref[...] = dx
        df_ref[...] = (head[3] * mod_ref[head[2]:head[2] + 1, :] * dx).astype(MM)

        @pl.when(i == ni - 1)
        def _():
            sm_ref[1:2, :] = jnp.broadcast_to(jnp.sum(sm_ref[1:2, :], axis=-1, keepdims=True), (1, D))

    tile = pl.BlockSpec((tm, D), lambda i: (i, 0))
    wide = pl.BlockSpec((tm, D_FF), lambda i: (i, 0))
    row = pl.BlockSpec((1, D), lambda i: (0, 0))
    extra_in, extra_specs, extra_out, extra_shape = (), [], [], []
    if nxt is not None:
        extra_in, extra_specs = (nxt[0],), [row]
        extra_out, extra_shape = [tile], [jax.ShapeDtypeStruct((T, D), ACT)]
    if head is not None:
        extra_in, extra_specs = (head[0], head[1]), [tile, row]
        extra_out = [tile, pl.BlockSpec((8, D), lambda i: (0, 0))]
        extra_shape = [jax.ShapeDtypeStruct((T, D), MM), jax.ShapeDtypeStruct((8, D), F32)]
    n_out = 5 + len(extra_out)
    out = _gridded(
        body, carry, name=name, grid=(ni,),
        in_specs=[
            tile,
            pl.BlockSpec((9, D), lambda i: (0, 0)),
            row,
            pl.BlockSpec((2, D_FF, D), lambda i: (0, 0, 0), pipeline_mode=pl.Buffered(1)),
            pl.BlockSpec((D_FF, D), lambda i: (0, 0), pipeline_mode=pl.Buffered(1)),
        ] + extra_specs,
        out_specs=[tile, wide, wide, tile, tile] + extra_out,
        out_shape=[
            jax.ShapeDtypeStruct((T, D), F32),
            jax.ShapeDtypeStruct((T, D_FF), ACT),
            jax.ShapeDtypeStruct((T, D_FF), ACT),
            jax.ShapeDtypeStruct((T, D), F32),
            jax.ShapeDtypeStruct((T, D), ACT),
        ] + extra_shape,
    )(*((x, mod, gnorm, w_in_t, w_out) + extra_in))
    return out[:n_out], out[n_out:]


def _ffn_bwd_w(h, df, a, b, w_out, name, carry):
    T = h.shape[0]
    tm = min(2048, T)
    ni = T // tm
    tn = 256
    nj = D_FF // tn

    def body(h_ref, df_ref, a_ref, b_ref, wo_ref, da_ref, db_ref, dwi_ref, dwo_ref, acc_i, acc_o):
        i = pl.program_id(1)

        @pl.when(i == 0)
        def _():
            acc_i[...] = jnp.zeros_like(acc_i)
            acc_o[...] = jnp.zeros_like(acc_o)

        hb = h_ref[...]
        df = df_ref[...]
        av = a_ref[...].astype(F32)
        bv = b_ref[...].astype(F32)
        sg = _sig(av)
        sa = av * sg
        s = (sa * bv).astype(MM)
        ds = _mm_nt(df, wo_ref[...])
        da = (ds * bv * sg * (1.0 + av * (1.0 - sg))).astype(MM)
        db = (ds * sa).astype(MM)
        da_ref[...] = da
        db_ref[...] = db
        acc_o[...] += _mm_tn(s, df)
        acc_i[0] += _mm_tn(da, hb)
        acc_i[1] += _mm_tn(db, hb)

        @pl.when(i == ni - 1)
        def _():
            dwi_ref[...] = acc_i[...].astype(MM)
            dwo_ref[...] = acc_o[...].astype(MM)

    out = _gridded(
        body, carry, name=name, grid=(nj, ni),
        in_specs=[
            pl.BlockSpec((tm, D), lambda j, i: (i, 0)),
            pl.BlockSpec((tm, D), lambda j, i: (i, 0)),
            pl.BlockSpec((tm, tn), lambda j, i: (i, j)),
            pl.BlockSpec((tm, tn), lambda j, i: (i, j)),
            pl.BlockSpec((tn, D), lambda j, i: (j, 0)),
        ],
        out_specs=[
            pl.BlockSpec((tm, tn), lambda j, i: (i, j)),
            pl.BlockSpec((tm, tn), lambda j, i: (i, j)),
            pl.BlockSpec((2, tn, D), lambda j, i: (0, j, 0)),
            pl.BlockSpec((tn, D), lambda j, i: (j, 0)),
        ],
        out_shape=[
            jax.ShapeDtypeStruct((T, D_FF), MM),
            jax.ShapeDtypeStruct((T, D_FF), MM),
            jax.ShapeDtypeStruct((2, D_FF, D), MM),
            jax.ShapeDtypeStruct((D_FF, D), MM),
        ],
        scratch_shapes=[pltpu.VMEM((2, tn, D), F32), pltpu.VMEM((tn, D), F32)],
    )(h, df, a, b, w_out)
    return out[:4], out[4:]


def _ffn_bwd_x(x, dxo, f, da, db, mod, mo, gnorm, w_in_t, res, name, carry):
    T = x.shape[0]
    tm = min(512, T)
    ni = T // tm
    tn = D_FF // 2
    nj = D_FF // tn

    def body(x_ref, dxo_ref, f_ref, da_ref, db_ref, mod_ref, g_ref, wi_ref, dx_ref, sm_ref, dh_scr):
        j = pl.program_id(0)
        i = pl.program_id(1)

        @pl.when((j == 0) & (i == 0))
        def _():
            sm_ref[...] = jnp.zeros_like(sm_ref)

        @pl.when(j == 0)
        def _():
            dh_scr[i] = jnp.zeros((tm, D), F32)

        dh_scr[i] += _mm(da_ref[...], wi_ref[0]) + _mm(db_ref[...], wi_ref[1])

        @pl.when(j == nj - 1)
        def _():
            sc = mod_ref[mo + 1:mo + 2, :]
            _, xh, n, r = _modnorm_fwd(x_ref[...], g_ref[...], mod_ref[mo:mo + 1, :], sc)
            dxn, dsh, dsc, dg = _modnorm_bwd(dh_scr[i], xh, n, r, g_ref[...], sc)
            dxo_v = dxo_ref[...]
            dx_ref[...] = dxo_v + dxn
            sm_ref[0:1, :] += dsh
            sm_ref[1:2, :] += dsc
            sm_ref[2:3, :] += _colsum(dxo_v * f_ref[...]) * res
            sm_ref[3:4, :] += dg

    last = pl.BlockSpec((tm, D), lambda j, i: (jnp.where(j == nj - 1, i, 0), 0))
    out = _gridded(
        body, carry, name=name, grid=(nj, ni),
        in_specs=[last, last, last,
                  pl.BlockSpec((tm, tn), lambda j, i: (i, j)), pl.BlockSpec((tm, tn), lambda j, i: (i, j)),
                  pl.BlockSpec((9, D), lambda j, i: (0, 0)), pl.BlockSpec((1, D), lambda j, i: (0, 0)),
                  pl.BlockSpec((2, tn, D), lambda j, i: (0, j, 0))],
        out_specs=[last, pl.BlockSpec((8, D), lambda j, i: (0, 0))],
        out_shape=[jax.ShapeDtypeStruct((T, D), F32), jax.ShapeDtypeStruct((8, D), F32)],
        scratch_shapes=[pltpu.VMEM((ni, tm, D), F32)],
    )(x, dxo, f, da, db, mod, gnorm, w_in_t)
    return out[:2], out[2:]


def _mixin_fwd(h, w, carry):
    T = h.shape[0]
    tm = min(2048, T)
    ni = T // tm

    def body(h_ref, w_ref, p_ref, h_all):
        i = pl.program_id(1)

        @pl.when(pl.program_id(0) == 0)
        def _():
            h_all[i] = h_ref[...]

        p_ref[0] = _mm(h_all[i], w_ref[0])

    first = lambda k, i: (jnp.where(k == 0, i, ni - 1), 0)
    out = _gridded(
        body, carry, name="mixin_fwd", grid=(8, ni),
        in_specs=[pl.BlockSpec((tm, D), first), pl.BlockSpec((1, D, D), lambda k, i: (k, 0, 0))],
        out_specs=[pl.BlockSpec((1, tm, D), lambda k, i: (k, i, 0))],
        out_shape=[jax.ShapeDtypeStruct((8, T, D), F32)],
        scratch_shapes=[pltpu.VMEM((ni, tm, D), ACT)],
    )(h, w)
    return out[:1], out[1:]


def _mixin_bwd(x, h, dxo, dp, mod, mo, gnorm, w, next_gate, next_res, carry):
    T = x.shape[0]
    tm = min(512, T)
    ni = T // tm

    def body(x_ref, h_ref, dxo_ref, dp_ref, mod_ref, g_ref, w_ref, dx_ref, dw_ref, sm_ref, df_ref, dh_scr, acc):
        k = pl.program_id(0)
        i = pl.program_id(1)

        @pl.when(i == 0)
        def _():
            acc[...] = jnp.zeros_like(acc)

        @pl.when(k == 0)
        def _():
            dh_scr[i] = jnp.zeros((tm, D), F32)

        @pl.when((k == 0) & (i == 0))
        def _():
            sm_ref[...] = jnp.zeros_like(sm_ref)

        dpk = dp_ref[0].astype(MM)
        acc[...] += _mm_tn(h_ref[...], dpk)
        dh_scr[i] += _mm_nt(dpk, w_ref[0])

        @pl.when(i == ni - 1)
        def _():
            dw_ref[0] = acc[...].astype(MM)

        @pl.when(k == 7)
        def _():
            sc = mod_ref[mo + 1:mo + 2, :]
            _, xh, n, r = _modnorm_fwd(x_ref[...], g_ref[...], mod_ref[mo:mo + 1, :], sc)
            dxn, dsh, dsc, dg = _modnorm_bwd(dh_scr[i], xh, n, r, g_ref[...], sc)
            dx = dxo_ref[...] + dxn
            dx_ref[...] = dx
            df_ref[...] = (next_res * mod_ref[next_gate:next_gate + 1, :] * dx).astype(MM)
            sm_ref[0:1, :] += dsh
            sm_ref[1:2, :] += dsc
            sm_ref[3:4, :] += dg

    last = pl.BlockSpec((tm, D), lambda k, i: (jnp.where(k == 7, i, 0), 0))
    out = _gridded(
        body, carry, name="mixin_bwd", grid=(8, ni),
        in_specs=[pl.BlockSpec((tm, D), lambda k, i: (jnp.where(k == 7, i, 0), 0)),
                  pl.BlockSpec((tm, D), lambda k, i: (i, 0)),
                  pl.BlockSpec((tm, D), lambda k, i: (jnp.where(k == 7, i, 0), 0)),
                  pl.BlockSpec((1, tm, D), lambda k, i: (k, i, 0)), pl.BlockSpec((9, D), lambda k, i: (0, 0)),
                  pl.BlockSpec((1, D), lambda k, i: (0, 0)), pl.BlockSpec((1, D, D), lambda k, i: (k, 0, 0))],
        out_specs=[last, pl.BlockSpec((1, D, D), lambda k, i: (k, 0, 0)), pl.BlockSpec((8, D), lambda k, i: (0, 0)),
                   last],
        out_shape=[jax.ShapeDtypeStruct((T, D), F32), jax.ShapeDtypeStruct((8, D, D), MM),
                   jax.ShapeDtypeStruct((8, D), F32), jax.ShapeDtypeStruct((T, D), MM)],
        scratch_shapes=[pltpu.VMEM((ni, tm, D), F32), pltpu.VMEM((D, D), F32)],
    )(x, h, dxo, dp, mod, gnorm, w)
    return out[:4], out[4:]


def _hgrn_consts():
    rows = jnp.arange(SUB * HD) // HD
    e = (rows[:, None] == jnp.arange(HD)[None, :]).astype(MM)
    return e, e.T


def _rows_bcast(ref, cb, first, n):
    parts = [jnp.broadcast_to(ref[pl.ds(c * CHUNK + first, 1), :], (n, HD)) for c in range(cb // CHUNK)]
    return jnp.concatenate(parts, axis=0)


def _hgrn_pre(qr, fr, lb_ref, b_scr, cb):
    z = lb_ref[...]
    lb = _sig(z[0:1, :] - z[1:2, :])
    sq = _sig(qr)
    q = qr * sq * Q_SCALE
    sf = _sig(fr)
    fg = lb + (1.0 - lb) * sf
    lf = jnp.log(fg)
    k = 1.0 - fg
    tl = lax.broadcasted_iota(jnp.int32, (cb, HD), 0) % CHUNK
    bc = lf
    sh = 1
    while sh < CHUNK:
        bc = bc + jnp.where(tl >= sh, pltpu.roll(bc, sh, 0), 0.0)
        sh *= 2
    b_scr[...] = bc
    bl = _rows_bcast(b_scr, cb, CHUNK - 1, CHUNK)
    eb = jnp.exp(bc)
    ekd = jnp.exp(bl - bc)
    ekf = jnp.exp(jnp.minimum(-bc, SAFE_EXP))
    return dict(lb=lb, sq=sq, q=q, sf=sf, fg=fg, k=k, tl=tl, b=bc, bl=bl, eb=eb, ekd=ekd, ekf=ekf,
                qe=q * eb, kd=k * ekd, kf=k * ekf, safe=jnp.max(-bc) < SAFE_EXP)


def _hgrn_pre_fused(p_ref, lb_ref, b_scr, q_scr, k_scr, qe_scr, kf_scr, kd_scr, cb):
    z = lb_ref[...]
    lb = _sig(z[0:1, :] - z[1:2, :])
    tl = lax.broadcasted_iota(jnp.int32, (CHUNK, HD), 0)

    def chunk(c, worst):
        r0 = pl.multiple_of(c * CHUNK, CHUNK)
        rs = pl.ds(r0, CHUNK)
        qr = p_ref[0, rs, :]
        q = qr * _sig(qr) * Q_SCALE
        fg = lb + (1.0 - lb) * _sig(p_ref[1, rs, :])
        k = 1.0 - fg
        bc = jnp.log(fg)
        sh = 1
        while sh < CHUNK:
            bc = bc + jnp.where(tl >= sh, pltpu.roll(bc, sh, 0), 0.0)
            sh *= 2
        b_scr[rs, :] = bc
        q_scr[rs, :] = q
        k_scr[rs, :] = k
        qe_scr[rs, :] = (q * jnp.exp(bc)).astype(MM)
        kf_scr[rs, :] = (k * jnp.exp(jnp.minimum(-bc, SAFE_EXP))).astype(MM)
        kd_scr[rs, :] = (k * jnp.exp(b_scr[pl.ds(r0 + CHUNK - 1, 1), :] - bc)).astype(MM)
        return jnp.maximum(worst, -bc)

    worst = lax.fori_loop(0, cb // CHUNK, chunk, jnp.zeros((CHUNK, HD), F32))
    return jnp.max(worst) < SAFE_EXP


def _hgrn_sub(pre, b_scr, cb):
    bc, tl, q, k = pre["b"], pre["tl"], pre["q"], pre["k"]
    br = [None] + [_rows_bcast(b_scr, cb, SUB * i - 1, CHUNK) for i in range(1, NSUB)]
    sb = tl // SUB
    bref = jnp.where(sb == 0, bc, jnp.where(sb == 1, br[1], jnp.where(sb == 2, br[2], br[3])))
    eqo = jnp.exp(bc - bref)
    eko = [None] + [jnp.exp(jnp.where(tl < SUB * i, br[i] - bc, NEG)) for i in range(1, NSUB)]
    return dict(eqo=eqo, eko=eko, qo=q * eqo, ko=[None] + [k * eko[i] for i in range(1, NSUB)])


def _pad_rows(x):
    return jnp.concatenate([x, jnp.zeros_like(x)], axis=0)


def _by_subblock(sbc, parts):
    out = jnp.zeros_like(parts[1])
    for i in range(1, NSUB):
        out = jnp.where(sbc == i, parts[i], out)
    return out


def _hgrn_fwd(p, hgrn_lb, hgrn_g, carry):
    T = p.shape[1]
    cb = min(HGRN_BLOCK, T)
    nch = cb // CHUNK
    ncb = T // cb
    e_mat, _ = _hgrn_consts()

    def body(p_ref, lb_ref, g_ref, e_ref, o_ref, oa_ref, a_ref, s_ref, st_scr, q_scr, k_scr, b_scr, z_scr, ad_scr,
             qe_scr, kf_scr, kd_scr):
        @pl.when(pl.program_id(1) == 0)
        def _():
            st_scr[...] = jnp.zeros_like(st_scr)

        safe = _hgrn_pre_fused(p_ref, lb_ref, b_scr, q_scr, k_scr, qe_scr, kf_scr, kd_scr, cb)
        chunks = [slice(c * CHUNK, (c + 1) * CHUNK) for c in range(nch)]
        row_i = lax.broadcasted_iota(jnp.int32, (CHUNK, HD), 0)
        lane_i = lax.broadcasted_iota(jnp.int32, (CHUNK, HD), 1)
        sbc = row_i // SUB
        causal = lane_i <= row_i

        @pl.when(safe)
        def _():
            for rs in chunks:
                ad_scr[rs, :] = jnp.where(causal, _mm_nt(qe_scr[rs, :], _pad_rows(kf_scr[rs, :])), 0.0)

        @pl.when(jnp.logical_not(safe))
        def _():
            tl = lax.broadcasted_iota(jnp.int32, (cb, HD), 0) % CHUNK
            sub = _hgrn_sub(dict(b=b_scr[...], tl=tl, q=q_scr[...], k=k_scr[...]), b_scr, cb)
            ti = lax.broadcasted_iota(jnp.int32, (SUB, HD), 0)

            def zbody(c, carry):
                for i in range(NSUB):
                    r0 = pl.multiple_of(c * CHUNK + SUB * i, SUB)
                    qi = q_scr[pl.ds(r0, SUB), :]
                    bi = b_scr[pl.ds(r0, SUB), :]
                    for s in range(SUB):
                        krow = k_scr[pl.ds(r0 + s, 1), :]
                        brow = b_scr[pl.ds(r0 + s, 1), :]
                        if s < 8:
                            zz = qi * krow * jnp.exp(jnp.where(ti >= s, bi - brow, NEG))
                        else:
                            lo = qi[8:] * krow * jnp.exp(jnp.where(ti[8:] >= s, bi[8:] - brow, NEG))
                            zz = jnp.concatenate([jnp.zeros((8, HD), F32), lo], axis=0)
                        z_scr[i, pl.ds(pl.multiple_of(c * SUB, SUB), SUB), s * HD:(s + 1) * HD] = zz.astype(MM)
                return carry

            lax.fori_loop(0, nch, zbody, 0)
            adiag = [_mm(z_scr[i], e_ref[...]) for i in range(NSUB)]
            offs = [[_mm_nt(sub["qo"][rs], _pad_rows(sub["ko"][i][rs])) for i in range(1, NSUB)] for rs in chunks]
            for c, rs in enumerate(chunks):
                dparts = []
                for i in range(NSUB):
                    blk = adiag[i][c * SUB:(c + 1) * SUB]
                    dparts.append(blk if i == 0 else pltpu.roll(blk, SUB * i, 1))
                ad_scr[rs, :] = _by_subblock(sbc, [None] + offs[c]) + jnp.concatenate(dparts, axis=0)

        kv = [_mm_tn(p_ref[2, rs, :], kd_scr[rs, :]) for rs in chunks]
        a_ref[0] = ad_scr[...]
        o_intra = [_mm(ad_scr[rs, :], _pad_rows(p_ref[2, rs, :])) for rs in chunks]
        states = []
        st = st_scr[...]
        for c in range(nch):
            states.append(st)
            st = st * jnp.exp(b_scr[pl.ds(c * CHUNK + CHUNK - 1, 1), :]) + kv[c]
        st_scr[...] = st
        g = g_ref[...]
        for c, rs in enumerate(chunks):
            s_ref[0, c] = states[c]
            o = o_intra[c] + _mm_nt(qe_scr[rs, :], states[c])
            o_ref[rs, :] = o
            og = p_ref[3, rs, :]
            oa_ref[rs, :] = (o * lax.rsqrt(_rowmean(o * o) + EPS) * g * og * _sig(og)).astype(ACT)

    out = _gridded(
        body, carry, name="hgrn_fwd", grid=(HEADS, ncb),
        in_specs=[pl.BlockSpec((4, cb, HD), lambda h, c: (0, c, h)),
                  pl.BlockSpec((2, HD), lambda h, c: (0, h)),
                  pl.BlockSpec((1, HD), lambda h, c: (0, h)),
                  pl.BlockSpec((SUB * HD, HD), lambda h, c: (0, 0))],
        out_specs=[pl.BlockSpec((cb, HD), lambda h, c: (c, h)),
                   pl.BlockSpec((cb, HD), lambda h, c: (c, h)),
                   pl.BlockSpec((1, cb, HD), lambda h, c: (h, c, 0)),
                   pl.BlockSpec((1, nch, HD, HD), lambda h, c: (h, c, 0, 0))],
        out_shape=[jax.ShapeDtypeStruct((T, D), F32), jax.ShapeDtypeStruct((T, D), ACT),
                   jax.ShapeDtypeStruct((HEADS, T, HD), F32),
                   jax.ShapeDtypeStruct((HEADS, T // CHUNK, HD, HD), F32)],
        scratch_shapes=[pltpu.VMEM((HD, HD), F32), pltpu.VMEM((cb, HD), F32), pltpu.VMEM((cb, HD), F32),
                        pltpu.VMEM((cb, HD), F32), pltpu.VMEM((NSUB, nch * SUB, SUB * HD), MM),
                        pltpu.VMEM((cb, HD), F32), pltpu.VMEM((cb, HD), MM), pltpu.VMEM((cb, HD), MM),
                        pltpu.VMEM((cb, HD), MM)],
    )(p, hgrn_lb, hgrn_g, e_mat)
    return out[:4], out[4:]


def _hgrn_bwd(p, o, a_all, s_all, doa, hgrn_lb, hgrn_g, dp, carry):
    T = p.shape[1]
    cb = min(HGRN_BLOCK, T)
    nch = cb // CHUNK
    ncb = T // cb
    _, et_mat = _hgrn_consts()

    def body(p_ref, o_ref, a_ref, s_ref, doa_ref, lb_ref, g_ref, et_ref, dp_in, dp_ref, sm_ref,
             dst_scr, q_scr, k_scr, b_scr, x_scr, dqd_scr, dkd_scr):
        del dp_in

        @pl.when(pl.program_id(1) == 0)
        def _():
            dst_scr[...] = jnp.zeros_like(dst_scr)
            sm_ref[...] = jnp.zeros_like(sm_ref)

        qr = p_ref[0]
        v = p_ref[2]
        og = p_ref[3]
        pre = _hgrn_pre(qr, p_ref[1], lb_ref, b_scr, cb)
        q, k = pre["q"], pre["k"]
        g = g_ref[...]
        ov = o_ref[...]
        r = lax.rsqrt(_rowmean(ov * ov) + EPS)
        oh = ov * r
        sgo = _sig(og)
        doa_v = doa_ref[...]
        don = doa_v * og * sgo
        dog = doa_v * oh * g * sgo * (1.0 + og * (1.0 - sgo))
        sm_ref[1:2, :] += _colsum(don * oh)
        doh = don * g
        do = r * (doh - oh * _rowmean(doh * oh))

        sbc = lax.broadcasted_iota(jnp.int32, (CHUNK, HD), 0) // SUB
        row_i = lax.broadcasted_iota(jnp.int32, (CHUNK, HD), 0)
        lane_i = lax.broadcasted_iota(jnp.int32, (CHUNK, HD), 1)
        causal = lane_i <= row_i
        chunks = [slice(c * CHUNK, (c + 1) * CHUNK) for c in range(nch)]
        da_parts = [jnp.where(causal, _mm_nt(do[rs], _pad_rows(v[rs])), 0.0) for rs in chunks]
        dv_parts = [_mm_tn(a_ref[0, rs, :], do[rs])[:CHUNK] for rs in chunks]

        @pl.when(pre["safe"])
        def _():
            hi = dict(preferred_element_type=F32, precision=lax.Precision.HIGH)
            for c, rs in enumerate(chunks):
                dqd_scr[rs, :] = pre["eb"][rs] * lax.dot_general(
                    da_parts[c], _pad_rows(pre["kf"][rs]), (((1,), (0,)), ((), ())), **hi)
                dkd_scr[rs, :] = pre["ekf"][rs] * lax.dot_general(
                    da_parts[c], pre["qe"][rs], (((0,), (0,)), ((), ())), **hi)[:CHUNK]

        @pl.when(jnp.logical_not(pre["safe"]))
        def _():
            sub = _hgrn_sub(pre, b_scr, cb)
            dqoff_mm = [[_mm(da_parts[c], _pad_rows(sub["ko"][i][rs])) for i in range(1, NSUB)]
                        for c, rs in enumerate(chunks)]
            dkoff_mm = [[_mm_tn(jnp.where(sbc == i, da_parts[c], 0.0), sub["qo"][rs])[:CHUNK]
                         for i in range(1, NSUB)] for c, rs in enumerate(chunks)]
            dqoff_parts = [_by_subblock(sbc, [None] + dqoff_mm[c]) for c in range(nch)]
            dkoff_parts = []
            for c, rs in enumerate(chunks):
                dko = sub["eko"][1][rs] * dkoff_mm[c][0]
                for i in range(2, NSUB):
                    dko = dko + sub["eko"][i][rs] * dkoff_mm[c][i - 1]
                dkoff_parts.append(dko)
            q_scr[...] = q
            k_scr[...] = k
            for i in range(NSUB):
                rows = []
                for c in range(nch):
                    blk = da_parts[c][SUB * i:SUB * (i + 1)]
                    rows.append(blk if i == 0 else pltpu.roll(blk, HD - SUB * i, 1))
                x_scr[i] = _mm(jnp.concatenate(rows, axis=0), et_ref[...])
            ti = lax.broadcasted_iota(jnp.int32, (SUB, HD), 0)

            def dbody(c, carry):
                for i in range(NSUB):
                    r0 = pl.multiple_of(c * CHUNK + SUB * i, SUB)
                    qi = q_scr[pl.ds(r0, SUB), :]
                    bi = b_scr[pl.ds(r0, SUB), :]
                    dq_hi = jnp.zeros((8, HD), F32)
                    dq_lo = jnp.zeros((8, HD), F32)
                    dk_hi = jnp.zeros((8, HD), F32)
                    dk_lo = jnp.zeros((8, HD), F32)
                    c0 = pl.multiple_of(c * SUB, SUB)
                    t8 = ti[:8]
                    for s in range(SUB):
                        krow = k_scr[pl.ds(r0 + s, 1), :]
                        brow = b_scr[pl.ds(r0 + s, 1), :]
                        w_lo = (x_scr[i, pl.ds(c0 + 8, 8), s * HD:(s + 1) * HD]
                                * jnp.exp(jnp.where(t8 + 8 >= s, bi[8:] - brow, NEG)))
                        dq_lo = dq_lo + w_lo * krow
                        col = _colsum(w_lo * qi[8:])
                        if s < 8:
                            w_hi = (x_scr[i, pl.ds(c0, 8), s * HD:(s + 1) * HD]
                                    * jnp.exp(jnp.where(t8 >= s, bi[:8] - brow, NEG)))
                            dq_hi = dq_hi + w_hi * krow
                            dk_hi = jnp.where(t8 == s, col + _colsum(w_hi * qi[:8]), dk_hi)
                        else:
                            dk_lo = jnp.where(t8 + 8 == s, col, dk_lo)
                    dqd_scr[pl.ds(r0, SUB), :] = jnp.concatenate([dq_hi, dq_lo], axis=0)
                    dkd_scr[pl.ds(r0, SUB), :] = jnp.concatenate([dk_hi, dk_lo], axis=0)
                return carry

            lax.fori_loop(0, nch, dbody, 0)
            dqd_scr[...] += jnp.concatenate(dqoff_parts, axis=0) * sub["eqo"]
            dkd_scr[...] += jnp.concatenate(dkoff_parts, axis=0)

        qdo = [_mm_tn(do[rs], pre["qe"][rs]) for rs in chunks]
        dsts = [None] * nch
        dst = dst_scr[...]
        for c in reversed(range(nch)):
            dsts[c] = dst
            dst = dst * jnp.exp(b_scr[pl.ds(c * CHUNK + CHUNK - 1, 1), :]) + qdo[c]
        dst_scr[...] = dst
        sts = [s_ref[0, c] for c in range(nch)]
        dqe_parts = [_mm(do[rs], sts[c]) for c, rs in enumerate(chunks)]
        dkdec_parts = [_mm(v[rs], dsts[c]) for c, rs in enumerate(chunks)]
        dvi_parts = [_mm_nt(pre["kd"][rs], dsts[c]) for c, rs in enumerate(chunks)]
        debl_parts = [_colsum(dsts[c] * sts[c]) for c in range(nch)]
        dqe = jnp.concatenate(dqe_parts, axis=0)
        dkdec = jnp.concatenate(dkdec_parts, axis=0)
        dq_tot = dqd_scr[...] + dqe * pre["eb"]
        dk_inter = dkdec * pre["ekd"]
        dk_tot = dkd_scr[...] + dk_inter
        db = q * dq_tot - k * dk_tot
        kdk = k * dk_inter
        dbl = jnp.concatenate(
            [jnp.broadcast_to(jnp.exp(b_scr[pl.ds(c * CHUNK + CHUNK - 1, 1), :]) * debl_parts[c]
                              + _colsum(kdk[c * CHUNK:(c + 1) * CHUNK]), (CHUNK, HD)) for c in range(nch)], axis=0)
        tl = pre["tl"]
        rc = db
        sh = 1
        while sh < CHUNK:
            rc = rc + jnp.where(tl + sh < CHUNK, pltpu.roll(rc, cb - sh, 0), 0.0)
            sh *= 2
        dlf = rc + dbl
        dfg = dlf / pre["fg"] - dk_tot
        sf = pre["sf"]
        lb = pre["lb"]
        sm_ref[0:1, :] += _colsum(dfg * (1.0 - sf))
        sq = pre["sq"]
        dp_ref[0] = (dq_tot * Q_SCALE * sq * (1.0 + qr * (1.0 - sq))).astype(ACT)
        dp_ref[1] = (dfg * (1.0 - lb) * sf * (1.0 - sf)).astype(ACT)
        dp_ref[2] = (jnp.concatenate(dv_parts, axis=0) + jnp.concatenate(dvi_parts, axis=0)).astype(ACT)
        dp_ref[3] = dog.astype(ACT)

    rev = lambda c: ncb - 1 - c
    out = _gridded(
        body, carry, name="hgrn_bwd", grid=(HEADS, ncb),
        in_specs=[pl.BlockSpec((4, cb, HD), lambda h, c: (0, rev(c), h)),
                  pl.BlockSpec((cb, HD), lambda h, c: (rev(c), h)),
                  pl.BlockSpec((1, cb, HD), lambda h, c: (h, rev(c), 0)),
                  pl.BlockSpec((1, nch, HD, HD), lambda h, c: (h, rev(c), 0, 0)),
                  pl.BlockSpec((cb, HD), lambda h, c: (rev(c), h)),
                  pl.BlockSpec((2, HD), lambda h, c: (0, h)),
                  pl.BlockSpec((1, HD), lambda h, c: (0, h)),
                  pl.BlockSpec((HD, SUB * HD), lambda h, c: (0, 0)),
                  pl.BlockSpec(memory_space=pl.ANY)],
        out_specs=[pl.BlockSpec((4, cb, HD), lambda h, c: (0, rev(c), h)),
                   pl.BlockSpec((8, HD), lambda h, c: (0, h))],
        out_shape=[jax.ShapeDtypeStruct(dp.shape, dp.dtype), jax.ShapeDtypeStruct((8, D), F32)],
        aliases={8: 0},
        scratch_shapes=[pltpu.VMEM((HD, HD), F32), pltpu.VMEM((cb, HD), F32), pltpu.VMEM((cb, HD), F32),
                        pltpu.VMEM((cb, HD), F32), pltpu.VMEM((NSUB, nch * SUB, SUB * HD), F32),
                        pltpu.VMEM((cb, HD), F32), pltpu.VMEM((cb, HD), F32)],
    )(p, o, a_all, s_all, doa, hgrn_lb, hgrn_g, et_mat, dp)
    return out[:2], out[2:]


def _ln_fwd(u1, g, b):
    mu = _rowmean(u1)
    xc = u1 - mu
    rs = lax.rsqrt(_rowmean(xc * xc) + EPS)
    xh = xc * rs
    return xh * g + b, xh, rs


CONV_RB = 64
LANES = 128


def _shift_rows(src, sh, ls, n):
    for r in range(1, 8):
        sh[r - 1, 0:n, :] = src[pl.ds(r, n), ls]


def _tap(src, sh, ls, off, r0, rows):
    r = off % 8
    if r == 0:
        return src[pl.ds(r0 + off, rows), ls]
    return sh[r - 1, pl.ds(r0 + off - r, rows), :]


def _conv_fwd(p, cw, cb_, lng, lnb, carry):
    T = p.shape[1]
    tm = min(512, T)
    n = HALO + tm - 8

    def body(p_ref, cw_ref, cb_ref, g_ref, b_ref, u1_ref, u2_ref, buf, sh):
        @pl.when(pl.program_id(0) == 0)
        def _():
            buf[0:HALO, :] = jnp.zeros((HALO, D), F32)

        buf[HALO:HALO + tm, :] = p_ref[0] * _sig(p_ref[1])
        for lb in range(D // LANES):
            ls = slice(lb * LANES, (lb + 1) * LANES)
            _shift_rows(buf, sh, ls, n)
            taps = [cw_ref[j:j + 1, ls] for j in range(CONV_K)]
            bias = cb_ref[:, ls]

            def rows_body(rb, carry):
                r0 = pl.multiple_of(rb * CONV_RB, CONV_RB)
                acc = jnp.broadcast_to(bias, (CONV_RB, LANES))
                for j in range(CONV_K):
                    acc = acc + taps[j] * _tap(buf, sh, ls, HALO - (CONV_K - 1) + j, r0, CONV_RB)
                u1_ref[pl.ds(r0, CONV_RB), ls] = acc
                return carry

            lax.fori_loop(0, tm // CONV_RB, rows_body, 0)
        y, _, _ = _ln_fwd(u1_ref[...], g_ref[...], b_ref[...])
        u2_ref[...] = (y * _sig(y)).astype(ACT)
        buf[0:HALO, :] = buf[tm:tm + HALO, :]

    out = _gridded(
        body, carry, name="conv_fwd", grid=(T // tm,),
        in_specs=[pl.BlockSpec((2, tm, D), lambda i: (2, i, 0)), pl.BlockSpec((HALO, D), lambda i: (0, 0)),
                  pl.BlockSpec((1, D), lambda i: (0, 0)), pl.BlockSpec((1, D), lambda i: (0, 0)),
                  pl.BlockSpec((1, D), lambda i: (0, 0))],
        out_specs=[pl.BlockSpec((tm, D), lambda i: (i, 0)), pl.BlockSpec((tm, D), lambda i: (i, 0))],
        out_shape=[jax.ShapeDtypeStruct((T, D), F32), jax.ShapeDtypeStruct((T, D), ACT)],
        scratch_shapes=[pltpu.VMEM((HALO + tm, D), F32), pltpu.VMEM((7, n, LANES), F32)],
    )(p, cw, cb_, lng, lnb)
    return out[:2], out[2:]


def _conv_bwd(p, u1, du2, cw, lng, lnb, dp, carry):
    T = p.shape[1]
    tm = min(512, T)
    ni = T // tm
    hb = tm // HALO

    n = HALO + tm - 8

    def body(p_ref, ph_ref, u1_ref, du2_ref, cw_ref, g_ref, b_ref, dp_in, dp_ref, dcw_ref, sm_ref, ubuf, dbuf,
             sh, dacc):
        del dp_in
        step = pl.program_id(0)

        @pl.when(step == 0)
        def _():
            dbuf[tm:tm + HALO, :] = jnp.zeros((HALO, D), F32)
            dcw_ref[...] = jnp.zeros_like(dcw_ref)
            sm_ref[...] = jnp.zeros_like(sm_ref)

        ua = p_ref[0]
        sgb = _sig(p_ref[1])
        halo = ph_ref[0] * _sig(ph_ref[1])
        ubuf[0:HALO, :] = jnp.where(step == ni - 1, 0.0, halo)
        ubuf[HALO:HALO + tm, :] = ua * sgb
        g = g_ref[...]
        y, xh, rs = _ln_fwd(u1_ref[...], g, b_ref[...])
        sy = _sig(y)
        dy = du2_ref[...] * sy * (1.0 + y * (1.0 - sy))
        sm_ref[1:2, :] += _colsum(dy * xh)
        sm_ref[2:3, :] += _colsum(dy)
        dxh = dy * g
        du1 = rs * (dxh - _rowmean(dxh) - xh * _rowmean(dxh * xh))
        sm_ref[0:1, :] += _colsum(du1)
        dbuf[0:tm, :] = du1
        for lb in range(D // LANES):
            ls = slice(lb * LANES, (lb + 1) * LANES)
            taps = [cw_ref[j:j + 1, ls] for j in range(CONV_K)]
            _shift_rows(dbuf, sh, ls, n)

            def du0_body(rb, carry):
                r0 = pl.multiple_of(rb * CONV_RB, CONV_RB)
                acc = jnp.zeros((CONV_RB, LANES), F32)
                for j in range(CONV_K):
                    acc = acc + taps[j] * _tap(dbuf, sh, ls, CONV_K - 1 - j, r0, CONV_RB)
                dp_ref[0, pl.ds(r0, CONV_RB), ls] = acc.astype(ACT)
                return carry

            lax.fori_loop(0, tm // CONV_RB, du0_body, 0)
            _shift_rows(ubuf, sh, ls, n)
            dacc[...] = jnp.zeros_like(dacc)

            def dcw_body(rb, carry):
                r0 = pl.multiple_of(rb * CONV_RB, CONV_RB)
                d = dbuf[pl.ds(r0, CONV_RB), ls]
                for j in range(CONV_K):
                    prod = d * _tap(ubuf, sh, ls, HALO - (CONV_K - 1) + j, r0, CONV_RB)
                    dacc[8 * j:8 * j + 8, :] += jnp.sum(prod.reshape(CONV_RB // 8, 8, LANES), axis=0)
                return carry

            lax.fori_loop(0, tm // CONV_RB, dcw_body, 0)
            for j in range(CONV_K):
                dcw_ref[j:j + 1, ls] += _colsum(dacc[8 * j:8 * j + 8, :])
        du0 = dp_ref[0].astype(F32)
        dp_ref[0] = (du0 * sgb).astype(ACT)
        dp_ref[1] = (du0 * ua * sgb * (1.0 - sgb)).astype(ACT)
        dbuf[tm:tm + HALO, :] = dbuf[0:HALO, :]

    rev = lambda i: ni - 1 - i
    out = _gridded(
        body, carry, name="conv_bwd", grid=(ni,),
        in_specs=[pl.BlockSpec((2, tm, D), lambda i: (2, rev(i), 0)),
                  pl.BlockSpec((2, HALO, D), lambda i: (2, jnp.maximum(rev(i) * hb - 1, 0), 0)),
                  pl.BlockSpec((tm, D), lambda i: (rev(i), 0)), pl.BlockSpec((tm, D), lambda i: (rev(i), 0)),
                  pl.BlockSpec((HALO, D), lambda i: (0, 0)), pl.BlockSpec((1, D), lambda i: (0, 0)),
                  pl.BlockSpec((1, D), lambda i: (0, 0)), pl.BlockSpec(memory_space=pl.ANY)],
        out_specs=[pl.BlockSpec((2, tm, D), lambda i: (2, rev(i), 0)),
                   pl.BlockSpec((HALO, D), lambda i: (0, 0)), pl.BlockSpec((8, D), lambda i: (0, 0))],
        out_shape=[jax.ShapeDtypeStruct(dp.shape, dp.dtype), jax.ShapeDtypeStruct((HALO, D), F32),
                   jax.ShapeDtypeStruct((8, D), F32)],
        aliases={7: 0},
        scratch_shapes=[pltpu.VMEM((HALO + tm, D), F32), pltpu.VMEM((tm + HALO, D), F32),
                        pltpu.VMEM((7, n, LANES), F32), pltpu.VMEM((8 * CONV_K, LANES), F32)],
    )(p, p, u1, du2, cw, lng, lnb, dp)
    return out[:3], out[3:]


def _mixout_fwd(x, oa, u2, p, mod, mo, w_a, w_b, w_o):
    T = x.shape[0]
    tm = min(512, T)

    def body(x_ref, oa_ref, u2_ref, p_ref, mod_ref, wa_ref, wb_ref, wo_ref, xo_ref, ya_ref, yb_ref, mo_ref):
        ya = _mm(oa_ref[...], wa_ref[...])
        yb = _mm(u2_ref[...], wb_ref[...])
        ya_ref[...] = ya.astype(ACT)
        yb_ref[...] = yb.astype(ACT)
        merged = _sig(p_ref[0]) * ya + _sig(p_ref[1]) * yb
        out = _mm(merged, wo_ref[...])
        mo_ref[...] = out
        xo_ref[...] = x_ref[...] + mod_ref[mo + 2:mo + 3, :] * out

    tile = pl.BlockSpec((tm, D), lambda i: (i, 0))
    wspec = pl.BlockSpec((D, D), lambda i: (0, 0))
    return pl.pallas_call(
        body, name="mixout_fwd", grid=(T // tm,),
        in_specs=[tile, tile, tile, pl.BlockSpec((2, tm, D), lambda i: (3, i, 0)),
                  pl.BlockSpec((9, D), lambda i: (0, 0)), wspec, wspec, wspec],
        out_specs=[tile, tile, tile, tile],
        out_shape=[jax.ShapeDtypeStruct((T, D), F32), jax.ShapeDtypeStruct((T, D), ACT),
                   jax.ShapeDtypeStruct((T, D), ACT), jax.ShapeDtypeStruct((T, D), F32)],
        compiler_params=_cparams(1),
    )(x, oa, u2, p, mod, w_a, w_b, w_o)


def _mixout_bwd(dxo, oa, u2, ya, yb, mout, p, mod, mo, w_a, w_b, w_o):
    T = dxo.shape[0]
    tm = min(256, T)

    def body(dxo_ref, oa_ref, u2_ref, ya_ref, yb_ref, mo_ref, p_ref, mod_ref, wa_ref, wb_ref, wo_ref,
             dp_ref, doa_ref, du2_ref, dwa_ref, dwb_ref, dwo_ref, sm_ref):
        @pl.when(pl.program_id(0) == 0)
        def _():
            dwa_ref[...] = jnp.zeros_like(dwa_ref)
            dwb_ref[...] = jnp.zeros_like(dwb_ref)
            dwo_ref[...] = jnp.zeros_like(dwo_ref)
            sm_ref[...] = jnp.zeros_like(sm_ref)

        dxo_v = dxo_ref[...]
        sm_ref[2:3, :] += _colsum(dxo_v * mo_ref[...])
        dmo = (mod_ref[mo + 2:mo + 3, :] * dxo_v).astype(MM)
        ya = ya_ref[...].astype(F32)
        yb = yb_ref[...].astype(F32)
        sga = _sig(p_ref[0])
        sgb = _sig(p_ref[1])
        merged = (sga * ya + sgb * yb).astype(MM)
        dwo_ref[...] += _mm_tn(merged, dmo)
        dmg = _mm_nt(dmo, wo_ref[...])
        dp_ref[0] = (dmg * ya * sga * (1.0 - sga)).astype(ACT)
        dp_ref[1] = (dmg * yb * sgb * (1.0 - sgb)).astype(ACT)
        dya = (dmg * sga).astype(MM)
        dyb = (dmg * sgb).astype(MM)
        dwa_ref[...] += _mm_tn(oa_ref[...], dya)
        dwb_ref[...] += _mm_tn(u2_ref[...], dyb)
        doa_ref[...] = _mm_nt(dya, wa_ref[...])
        du2_ref[...] = _mm_nt(dyb, wb_ref[...])

    tile = pl.BlockSpec((tm, D), lambda i: (i, 0))
    wspec = pl.BlockSpec((D, D), lambda i: (0, 0))
    return pl.pallas_call(
        body, name="mixout_bwd", grid=(T // tm,),
        in_specs=[tile, tile, tile, tile, tile, tile, pl.BlockSpec((2, tm, D), lambda i: (3, i, 0)),
                  pl.BlockSpec((9, D), lambda i: (0, 0)), wspec, wspec, wspec],
        out_specs=[pl.BlockSpec((2, tm, D), lambda i: (3, i, 0)), tile, tile, wspec, wspec, wspec,
                   pl.BlockSpec((8, D), lambda i: (0, 0))],
        out_shape=[jax.ShapeDtypeStruct((8, T, D), ACT), jax.ShapeDtypeStruct((T, D), F32),
                   jax.ShapeDtypeStruct((T, D), F32), jax.ShapeDtypeStruct((D, D), F32),
                   jax.ShapeDtypeStruct((D, D), F32), jax.ShapeDtypeStruct((D, D), F32),
                   jax.ShapeDtypeStruct((8, D), F32)],
        compiler_params=_cparams(1),
    )(dxo, oa, u2, ya, yb, mout, p, mod, w_a, w_b, w_o)


def _adamw_ada_w(w, m, v, cs_all, dmod_cols):
    R, C = w.shape
    tr = 256
    cs_t = jnp.pad(cs_all.T, ((0, 0), (0, HD - N_DEV)))
    dm = jnp.pad(dmod_cols, ((0, HD - N_DEV), (0, 0)))

    def body(w_ref, m_ref, v_ref, cs_ref, d_ref, go_ref, do_ref, mo_ref, vo_ref):
        gv = jnp.dot(cs_ref[...], d_ref[...], preferred_element_type=F32, precision=lax.Precision.HIGHEST)
        go_ref[...] = gv
        do_ref[...], mo_ref[...], vo_ref[...] = _adam_math(w_ref[...], gv, m_ref[...], v_ref[...])

    tile = pl.BlockSpec((tr, C), lambda i: (i, 0))
    sds = jax.ShapeDtypeStruct((R, C), F32)
    return pl.pallas_call(
        body, name="adamw_ada_w", grid=(R // tr,),
        in_specs=[tile, tile, tile, pl.BlockSpec((tr, HD), lambda i: (i, 0)), pl.BlockSpec((HD, C), lambda i: (0, 0))],
        out_specs=[tile] * 4, out_shape=[sds] * 4, compiler_params=_cparams(1))(w, m, v, cs_t, dm)


def _adam_math(w, g, m, v):
    m2 = ADAM_B1 * m + (1.0 - ADAM_B1) * g
    v2 = ADAM_B2 * v + (1.0 - ADAM_B2) * (g * g)
    m_hat = m2 / (1.0 - ADAM_B1 ** ADAM_STEP)
    v_hat = v2 / (1.0 - ADAM_B2 ** ADAM_STEP)
    delta = -ADAM_LR * (m_hat / (jnp.sqrt(v_hat) + ADAM_EPS) + ADAM_WD * w)
    return delta, m2, v2


def _adamw(w, m, v, g, name, carry=None):
    R, C = w.shape
    slots = g.ndim == 3
    n_slots = g.shape[0] if slots else 0
    tr = R
    for cand in (256, 176):
        if R % cand == 0 and R > cand:
            tr = cand
            break

    def body(w_ref, m_ref, v_ref, g_ref, go_ref, d_ref, mo_ref, vo_ref):
        if slots:
            gv = g_ref[0].astype(F32)
            for s in range(1, n_slots):
                gv = gv + g_ref[s].astype(F32)
        else:
            gv = g_ref[...]
        go_ref[...] = gv
        d_ref[...], mo_ref[...], vo_ref[...] = _adam_math(w_ref[...], gv, m_ref[...], v_ref[...])

    tile = pl.BlockSpec((tr, C), lambda i: (i, 0))
    gspec = pl.BlockSpec((n_slots, tr, C), lambda i: (0, i, 0)) if slots else tile
    sds = jax.ShapeDtypeStruct((R, C), F32)
    return _gridded(body, carry, name=name, grid=(R // tr,), in_specs=[tile, tile, tile, gspec],
                    out_specs=[tile] * 4, out_shape=[sds] * 4)(w, m, v, g)


def _adamw_small(tot, names, params, grad_rows):
    k = len(names)

    def body(tot_ref, *refs):
        ins, outs = refs[:3 * k], refs[3 * k:]
        for i, n in enumerate(names):
            w_ref, m_ref, v_ref = ins[3 * i:3 * i + 3]
            go, do, mo, vo = outs[4 * i:4 * i + 4]
            row = grad_rows[n]
            for j in range(w_ref.shape[1] // D):
                ls = slice(j * D, (j + 1) * D)
                g = tot_ref[row + j:row + j + 1, :]
                w = w_ref[:, ls]
                if n == "hgrn_lb":
                    p0 = _sig(w[0:1] - w[1:2])
                    dz0 = p0 * (1.0 - p0) * g
                    g = jnp.concatenate([dz0, -dz0], axis=0)
                go[:, ls] = g
                do[:, ls], mo[:, ls], vo[:, ls] = _adam_math(w, g, m_ref[:, ls], v_ref[:, ls])

    flat = [t for n in names for t in params[n]]
    out_shape = [jax.ShapeDtypeStruct(params[n][0].shape, F32) for n in names for _ in range(4)]
    outs = pl.pallas_call(body, name="adamw_small", out_shape=out_shape)(tot, *flat)
    return {n: tuple(outs[4 * i:4 * i + 4]) for i, n in enumerate(names)}


def _cast_shards(ws):
    n = len(ws)

    def body(*refs):
        for src, dst in zip(refs[:n], refs[n:]):
            dst[...] = src[...].astype(MM)

    return pl.pallas_call(body, name="cast_shards", out_shape=[jax.ShapeDtypeStruct(w.shape, MM) for w in ws],
                          compiler_params=pltpu.CompilerParams(vmem_limit_bytes=VMEM_LIMIT))(*ws)


def _sum_slots(pack, name, tr):
    n, R, C = pack.shape

    def body(p_ref, out_ref):
        acc = p_ref[0].astype(F32)
        for s in range(1, n):
            acc = acc + p_ref[s].astype(F32)
        out_ref[...] = acc

    return pl.pallas_call(
        body, name=name, grid=(R // tr,), in_specs=[pl.BlockSpec((n, tr, C), lambda i: (0, i, 0))],
        out_specs=pl.BlockSpec((tr, C), lambda i: (i, 0)), out_shape=jax.ShapeDtypeStruct((R, C), F32),
        compiler_params=_cparams(1))(pack)


def _me():
    return lax.axis_index("x"), lax.axis_index("y"), lax.axis_index("c")


def _peer(r):
    x, y, c = _me()
    px = 1 - x if r & 4 else x
    py = 1 - y if r & 2 else y
    pc = 1 - c if r & 1 else c
    return (px, py, pc), 4 * px + 2 * py + pc


def _small_gather(x_ref, out_ref, send_sems, recv_sems):
    R = x_ref.shape[0]
    mx, my, mc = _me()
    me = 4 * mx + 2 * my + mc
    mine = out_ref.at[pl.ds(pl.multiple_of(me * R, 8), R), :]
    copies = []
    for r in range(1, N_DEV):
        dev, _ = _peer(r)
        copies.append(pltpu.make_async_remote_copy(
            src_ref=x_ref, dst_ref=mine, send_sem=send_sems.at[r - 1], recv_sem=recv_sems.at[r - 1],
            device_id=dev, device_id_type=MESH))
    for cp in copies:
        cp.start()
    mine[...] = x_ref[...]
    for r in range(1, N_DEV):
        dev, idx = _peer(r)
        theirs = out_ref.at[pl.ds(pl.multiple_of(idx * R, 8), R), :]
        pltpu.make_async_remote_copy(
            src_ref=x_ref, dst_ref=theirs, send_sem=send_sems.at[r - 1], recv_sem=recv_sems.at[r - 1],
            device_id=dev, device_id_type=MESH).wait_recv()
    for cp in copies:
        cp.wait_send()


def _prologue(cs, ada_w, ada_b_cols, big):
    n = len(big)
    ncol = ada_w.shape[1]
    big_shape, big_sems = _xchg_specs(big, "gather")

    def body(cs_ref, w_ref, b_ref, *rest):
        big_in, cs_all, mod_all, big_out = rest[:n], rest[n], rest[n + 1], rest[n + 2:2 * n + 2]
        mod_scr, s1, r1, s2, r2 = rest[2 * n + 2:2 * n + 7]
        sems = rest[2 * n + 7:]
        _small_gather(cs_ref, cs_all, s1, r1)
        pick = (lax.broadcasted_iota(jnp.int32, (N_DEV, N_DEV * 8), 1)
                == 8 * lax.broadcasted_iota(jnp.int32, (N_DEV, N_DEV * 8), 0)).astype(F32)
        per_device = jnp.dot(pick, cs_all[...], preferred_element_type=F32, precision=lax.Precision.HIGHEST)
        mod_scr[...] = jnp.dot(per_device, w_ref[...], preferred_element_type=F32,
                               precision=lax.Precision.HIGHEST) + b_ref[...]
        _small_gather(mod_scr, mod_all, s2, r2)
        _xchg_start(big_in, big_out, sems, "gather")
        _xchg_wait(big_in, big_out, sems, "gather")

    vmem = pl.BlockSpec(memory_space=pltpu.VMEM)
    hbm = pl.BlockSpec(memory_space=pl.ANY)
    dma7 = pltpu.SemaphoreType.DMA((N_DEV - 1,))
    out = pl.pallas_call(
        body, name="prologue",
        out_shape=[jax.ShapeDtypeStruct((N_DEV * 8, D), F32), jax.ShapeDtypeStruct((N_DEV * 8, ncol), F32)]
        + big_shape,
        in_specs=[vmem, vmem, vmem] + [hbm] * n, out_specs=[vmem, vmem] + [hbm] * n,
        scratch_shapes=[pltpu.VMEM((8, ncol), F32), dma7, dma7, dma7, dma7] + big_sems,
        compiler_params=pltpu.CompilerParams(vmem_limit_bytes=VMEM_LIMIT),
    )(cs, ada_w, ada_b_cols, *big)
    return out[0], out[1], out[2:]


N_CHIP = N_DEV // 2


def _xchg_copies(ins, outs, sems, mode):
    send_sems, recv_sems, local_sems = sems
    mx, my, mc = _me()
    me = 4 * mx + 2 * my + mc
    my_chip = 2 * mx + my
    sibling = _peer(1)[0]

    def rdma(a, r, dev, src, slot):
        k = a * (N_DEV - 1) + r - 1
        return pltpu.make_async_remote_copy(
            src_ref=src, dst_ref=outs[a].at[slot], send_sem=send_sems.at[k], recv_sem=recv_sems.at[k],
            device_id=dev, device_id_type=MESH)

    own, sends, relays, recvs = [], [], [], []
    for a in range(len(ins)):
        if mode == "pair":
            for chip in range(N_CHIP):
                src = ins[a].at[2 * chip + 1 - mc]
                sends.append(rdma(a, chip + 1, sibling, src, chip))
                recvs.append(rdma(a, chip + 1, sibling, src, chip))
            continue
        if mode == "quad":
            own.append(pltpu.make_async_copy(ins[a].at[my_chip], outs[a].at[my_chip], local_sems.at[a]))
            for r in (2, 4, 6):
                dev, idx = _peer(r)
                chip = idx // 2
                sends.append(rdma(a, r, dev, ins[a].at[chip], my_chip))
                recvs.append(rdma(a, r, dev, ins[a].at[chip], chip))
            continue
        gather = mode == "gather"
        own.append(pltpu.make_async_copy(ins[a] if gather else ins[a].at[me], outs[a].at[me], local_sems.at[a]))
        for r in range(1, N_DEV):
            dev, idx = _peer(r)
            if not gather:
                sends.append(rdma(a, r, dev, ins[a].at[idx], me))
                recvs.append(rdma(a, r, dev, ins[a].at[idx], idx))
            elif r == 1:
                sends.append(rdma(a, r, dev, ins[a], me))
                recvs.append(rdma(a, r, dev, ins[a], idx))
            elif r % 2 == 0:
                sends.append(rdma(a, r, dev, ins[a], me))
                relays.append((rdma(a, r, dev, ins[a], idx), rdma(a, r + 1, sibling, outs[a].at[idx], idx)))
            else:
                recvs.append(rdma(a, r, sibling, ins[a], idx))
    return own, sends, relays, recvs


def _xchg_start(ins, outs, sems, mode):
    own, sends, _, _ = _xchg_copies(ins, outs, sems, mode)
    for cp in own + sends:
        cp.start()


def _xchg_wait(ins, outs, sems, mode):
    own, sends, relays, recvs = _xchg_copies(ins, outs, sems, mode)
    for arrival, relay in relays:
        arrival.wait_recv()
        relay.start()
    for cp in recvs:
        cp.wait_recv()
    for cp in own:
        cp.wait()
    for cp in sends + [relay for _, relay in relays]:
        cp.wait_send()


def _xchg_specs(arrays, mode):
    n = len(arrays)
    shape = {"gather": lambda s: (N_DEV,) + s, "scatter": lambda s: s, "pair": lambda s: (N_CHIP,) + s[1:],
             "quad": lambda s: s}[mode]
    out_shape = [jax.ShapeDtypeStruct(shape(a.shape), a.dtype) for a in arrays]
    sems = [pltpu.SemaphoreType.DMA((n * (N_DEV - 1),)), pltpu.SemaphoreType.DMA((n * (N_DEV - 1),)),
            pltpu.SemaphoreType.DMA((n,))]
    return out_shape, sems


def _exchange(arrays, mode, name):
    n = len(arrays)

    def body(*refs):
        _xchg_start(refs[:n], refs[n:2 * n], refs[2 * n:], mode)
        _xchg_wait(refs[:n], refs[n:2 * n], refs[2 * n:], mode)

    out_shape, sems = _xchg_specs(arrays, mode)
    return pl.pallas_call(
        body, name=name, out_shape=out_shape,
        in_specs=[pl.BlockSpec(memory_space=pl.ANY)] * n, out_specs=[pl.BlockSpec(memory_space=pl.ANY)] * n,
        scratch_shapes=sems,
    )(*arrays)


def _gridded(body, carry, *, name, grid, in_specs, out_specs, out_shape, scratch_shapes=(), aliases=None):
    if carry is None:
        return pl.pallas_call(
            body, name=name, grid=grid, in_specs=list(in_specs), out_specs=list(out_specs),
            out_shape=list(out_shape), scratch_shapes=list(scratch_shapes), input_output_aliases=aliases or {},
            compiler_params=_cparams(len(grid)))
    arrays, mode = carry
    n, n_in, n_out, n_scr = len(arrays), len(in_specs), len(out_specs), len(scratch_shapes)
    c_shape, c_sems = _xchg_specs(arrays, mode)

    def wrapped(*refs):
        ins, cin = refs[:n_in], refs[n_in:n_in + n]
        o0 = n_in + n
        outs, cout = refs[o0:o0 + n_out], refs[o0 + n_out:o0 + n_out + n]
        s0 = o0 + n_out + n
        scr, sems = refs[s0:s0 + n_scr], refs[s0 + n_scr:]
        first = pl.program_id(0) == 0
        last = pl.program_id(0) == grid[0] - 1
        for ax in range(1, len(grid)):
            first = first & (pl.program_id(ax) == 0)
            last = last & (pl.program_id(ax) == grid[ax] - 1)

        @pl.when(first)
        def _():
            _xchg_start(cin, cout, sems, mode)

        body(*ins, *outs, *scr)

        @pl.when(last)
        def _():
            _xchg_wait(cin, cout, sems, mode)

    hbm = pl.BlockSpec(memory_space=pl.ANY)
    res = pl.pallas_call(
        wrapped, name=name, grid=grid, in_specs=list(in_specs) + [hbm] * n, out_specs=list(out_specs) + [hbm] * n,
        out_shape=list(out_shape) + c_shape, scratch_shapes=list(scratch_shapes) + c_sems,
        input_output_aliases=aliases or {}, compiler_params=_cparams(len(grid)),
    )
    return lambda *args: res(*args, *arrays)


SMALL_ORDER = ("norm_ffn1", "norm_mix", "lb0", "hgrn_g", "conv_b", "conv_ln_g", "conv_ln_b", "norm_ffn2",
               "norm_final")
PACK_ROWS = 24
PACK_USED = 9 + len(SMALL_ORDER) + 1


def _pack_small(sm1, sm2, sm_mo, sm3, sm_hg, sm_cv, sm_head):
    arrays = (sm1, sm2, sm_mo, sm3, sm_hg, sm_cv, sm_head)
    src = ((0, 0, 3), (1, 0, 2), (2, 2, 1), (3, 0, 3),
           (0, 3, 1), (1, 3, 1), (4, 0, 2), (5, 0, 3), (3, 3, 1), (6, 0, 2))
    assert sum(n for _, _, n in src) == PACK_USED and all(a.shape == (8, D) for a in arrays)

    def body(*refs):
        out = refs[-1]
        out[...] = jnp.zeros_like(out)
        r = 0
        for a, first, n in src:
            out[r:r + n, :] = refs[a][first:first + n, :]
            r += n

    return pl.pallas_call(body, name="pack_small", out_shape=jax.ShapeDtypeStruct((PACK_ROWS, D), F32))(*arrays)


def _local_step(x, target, mod, small, sh, w1):
    w1_in, w1_out = w1[0].reshape(2, D_FF, D), w1[1].reshape(D_FF, D)
    (x1, a1, b1, f1, h1, h2), (wm_in,) = _ffn_fwd(x, mod, 0, small["norm_ffn1"], w1_in, w1_out, 0.5, "ffn1_fwd",
                                                  ([sh["mix_w_in"]], "gather"), nxt=(small["norm_mix"], 3))
    (p,), (wh_o, wc_o, wm_o, cw) = _mixin_fwd(
        h2, wm_in, ([sh["hgrn_w_o"], sh["conv_w_o"], sh["mix_w_out"], sh["conv_w"]], "gather"))
    wh_o, wc_o, wm_o = wh_o.reshape(D, D), wc_o.reshape(D, D), wm_o.reshape(D, D)
    cw = jnp.pad(cw.transpose(1, 0, 2).reshape(CONV_K, D), ((0, HALO - CONV_K), (0, 0)))
    (o, oa, a_all, s_all), (w2_in,) = _hgrn_fwd(p, small["hgrn_lb"], small["hgrn_g"], ([sh["ffn2_w_in"]], "gather"))
    (u1, u2), (w2_out,) = _conv_fwd(p, cw, small["conv_b"], small["conv_ln_g"], small["conv_ln_b"],
                                    ([sh["ffn2_w_out"]], "gather"))
    w2_in, w2_out = w2_in.reshape(2, D_FF, D), w2_out.reshape(D_FF, D)
    x2, ya, yb, mout = _mixout_fwd(x1, oa, u2, p, mod, 3, wh_o, wc_o, wm_o)
    (dx3, a3, b3, f3, h3, df3, sm_head), _ = _ffn_fwd(x2, mod, 6, small["norm_ffn2"], w2_in, w2_out, 0.5, "ffn2_fwd",
                                                      None, head=(target, small["norm_final"], 8, 0.5))

    (da3, db3, dw2_in, dw2_out), _ = _ffn_bwd_w(h3, df3, a3, b3, w2_out, "ffn2_bwd_w", None)
    rows = lambda t: t.reshape(N_DEV, -1, D).astype(MM)
    (dx2, sm3), (r2_out,) = _ffn_bwd_x(x2, dx3, f3, da3, db3, mod, 6, small["norm_ffn2"], w2_in, 0.5, "ffn2_bwd_x",
                                       ([rows(dw2_out)], "scatter"))
    dp, doa, du2, dwh_o, dwc_o, dwm_o, sm_mo = _mixout_bwd(dx2, oa, u2, ya, yb, mout, p, mod, 3, wh_o, wc_o, wm_o)
    (dp, dcw, sm_cv), (r2_in,) = _conv_bwd(p, u1, du2, cw, small["conv_ln_g"], small["conv_ln_b"], dp,
                                           ([rows(dw2_in)], "scatter"))
    (dp, sm_hg), _ = _hgrn_bwd(p, o, a_all, s_all, doa, small["hgrn_lb"], small["hgrn_g"], dp, None)
    (dx1, dwm_in, sm2, df1), (rh_o, rc_o, rm_o, rcw) = _mixin_bwd(
        x1, h2, dx2, dp, mod, 3, small["norm_mix"], wm_in, 2, 0.5,
        ([rows(dwh_o), rows(dwc_o), rows(dwm_o), dcw[:CONV_K].reshape(CONV_K, N_DEV, -1).transpose(1, 0, 2)],
         "scatter"))
    (da1, db1, dw1_in, dw1_out), (rm_in,) = _ffn_bwd_w(h1, df1, a1, b1, w1_out, "ffn1_bwd_w",
                                                      (_pair_reduce([dwm_in], "pair_mix"), "quad"))
    (dx0, sm1), (r1_in, r1_out) = _ffn_bwd_x(
        x, dx1, f1, da1, db1, mod, 0, small["norm_ffn1"], w1_in, 0.5, "ffn1_bwd_x",
        (_pair_reduce([rows(dw1_in), rows(dw1_out)], "pair_ffn1"), "quad"))

    pack = _pack_small(sm1, sm2, sm_mo, sm3, sm_hg, sm_cv, sm_head)
    recv = dict(ffn1_w_in=r1_in, ffn1_w_out=r1_out, mix_w_in=rm_in, hgrn_w_o=rh_o, conv_w=rcw, conv_w_o=rc_o,
                mix_w_out=rm_o, ffn2_w_in=r2_in, ffn2_w_out=r2_out)
    return dx0, pack, recv


def _pair_add(mine, theirs, core, name):
    _, R, C = theirs.shape

    def body(core_ref, a_ref, b_ref, out_ref):
        del core_ref
        out_ref[0] = (a_ref[0, 0].astype(F32) + b_ref[0].astype(F32)).astype(out_ref.dtype)

    blk = pl.BlockSpec((1, R, C), lambda s, core_ref: (s, 0, 0))
    grid_spec = pltpu.PrefetchScalarGridSpec(
        num_scalar_prefetch=1, grid=(N_CHIP,),
        in_specs=[pl.BlockSpec((1, 1, R, C), lambda s, core_ref: (s, core_ref[0], 0, 0)), blk], out_specs=blk)
    return pl.pallas_call(body, name=name, grid_spec=grid_spec,
                          out_shape=jax.ShapeDtypeStruct(theirs.shape, mine.dtype), compiler_params=_cparams(1),
                          )(core, mine.reshape(N_CHIP, 2, R, C), theirs)


def _pair_reduce(arrays, name):
    theirs = _exchange(arrays, "pair", name)
    core = lax.axis_index("c").astype(jnp.int32).reshape(1)
    return [_pair_add(a, t, core, "%s_add%d" % (name, i)) for i, (a, t) in enumerate(zip(arrays, theirs))]


def kernel(x, c, ada_w, ada_b, norm_ffn1, ffn1_w_in, ffn1_w_out, norm_mix, mix_w_in, hgrn_lb, hgrn_g, hgrn_w_o, conv_w, conv_b, conv_ln_g, conv_ln_b, conv_w_o, mix_w_out, norm_ffn2, ffn2_w_in, ffn2_w_out, norm_final, loss_target, m_ada_w, m_ada_b, m_norm_ffn1, m_ffn1_w_in, m_ffn1_w_out, m_norm_mix, m_mix_w_in, m_hgrn_lb, m_hgrn_g, m_hgrn_w_o, m_conv_w, m_conv_b, m_conv_ln_g, m_conv_ln_b, m_conv_w_o, m_mix_w_out, m_norm_ffn2, m_ffn2_w_in, m_ffn2_w_out, m_norm_final, v_ada_w, v_ada_b, v_norm_ffn1, v_ffn1_w_in, v_ffn1_w_out, v_norm_mix, v_mix_w_in, v_hgrn_lb, v_hgrn_g, v_hgrn_w_o, v_conv_w, v_conv_b, v_conv_ln_g, v_conv_ln_b, v_conv_w_o, v_mix_w_out, v_norm_ffn2, v_ffn2_w_in, v_ffn2_w_out, v_norm_final):
    mx, my, mc = _me()
    me = 4 * mx + 2 * my + mc
    ncol = ada_w.shape[2]

    sh = dict(ffn1_w_in=ffn1_w_in[0].T, ffn1_w_out=ffn1_w_out[0], mix_w_in=mix_w_in[0], hgrn_w_o=hgrn_w_o[0],
              conv_w_o=conv_w_o[0], mix_w_out=mix_w_out[0], ffn2_w_in=ffn2_w_in[0].T, ffn2_w_out=ffn2_w_out[0])
    sh = dict(zip(sh, _cast_shards(list(sh.values()))))
    sh["conv_w"] = conv_w[0]
    small = dict(norm_ffn1=norm_ffn1, norm_mix=norm_mix, hgrn_lb=hgrn_lb, hgrn_g=hgrn_g, conv_b=conv_b,
                 conv_ln_g=conv_ln_g, conv_ln_b=conv_ln_b, norm_ffn2=norm_ffn2, norm_final=norm_final.reshape(1, D))

    cs = jnp.broadcast_to(c * jax.nn.sigmoid(c), (8, D))
    ada_b_cols = lax.dynamic_slice(ada_b, (0, me * ncol), (1, ncol))
    cs_all, mod_all, w1 = _prologue(cs, ada_w[0], ada_b_cols, [sh["ffn1_w_in"], sh["ffn1_w_out"]])
    cs_all = cs_all.reshape(N_DEV, 8, D)[:, 0, :]
    mod = lax.dynamic_index_in_dim(mod_all.reshape(N_DEV, N_DEV, ncol), me, axis=1, keepdims=False).reshape(9, D)

    dx, pack, recv = _local_step(x[0], loss_target[0], mod, small, sh, w1)

    res = {}
    *res["ffn2_w_out"], pack_all = _adamw(ffn2_w_out[0], m_ffn2_w_out[0], v_ffn2_w_out[0], recv["ffn2_w_out"],
                                          "adamw_ffn2_w_out", ([pack], "gather"))
    tot = _sum_slots(pack_all, "sum_small", PACK_ROWS)
    loss = tot[PACK_USED - 1, 0]
    dmod_all = pack_all[:, 0:9, :].reshape(N_DEV, 9 * D)
    dmod_cols = lax.dynamic_slice(dmod_all, (0, me * ncol), (N_DEV, ncol))

    res["ada_w"] = _adamw_ada_w(ada_w[0], m_ada_w[0], v_ada_w[0], cs_all, dmod_cols)
    big = dict(ffn1_w_in=(ffn1_w_in, m_ffn1_w_in, v_ffn1_w_in), ffn1_w_out=(ffn1_w_out, m_ffn1_w_out, v_ffn1_w_out),
               mix_w_in=(mix_w_in, m_mix_w_in, v_mix_w_in), hgrn_w_o=(hgrn_w_o, m_hgrn_w_o, v_hgrn_w_o),
               conv_w=(conv_w, m_conv_w, v_conv_w), conv_w_o=(conv_w_o, m_conv_w_o, v_conv_w_o),
               mix_w_out=(mix_w_out, m_mix_w_out, v_mix_w_out), ffn2_w_in=(ffn2_w_in, m_ffn2_w_in, v_ffn2_w_in),
               ffn2_w_out=(ffn2_w_out, m_ffn2_w_out, v_ffn2_w_out))
    for n, (w, m, v) in big.items():
        if n in res:
            continue
        if n in ("ffn1_w_in", "ffn2_w_in"):
            res[n] = tuple(t.T for t in _adamw(w[0].T, m[0].T, v[0].T, recv[n], "adamw_" + n))
        else:
            res[n] = _adamw(w[0], m[0], v[0], recv[n], "adamw_" + n)
    sm_names = ("ada_b", "norm_ffn1", "norm_mix", "hgrn_lb", "hgrn_g", "conv_b", "conv_ln_g", "conv_ln_b",
                "norm_ffn2", "norm_final")
    sm_w = dict(ada_b=(ada_b, m_ada_b, v_ada_b), norm_ffn1=(norm_ffn1, m_norm_ffn1, v_norm_ffn1),
                norm_mix=(norm_mix, m_norm_mix, v_norm_mix), hgrn_lb=(hgrn_lb, m_hgrn_lb, v_hgrn_lb),
                hgrn_g=(hgrn_g, m_hgrn_g, v_hgrn_g), conv_b=(conv_b, m_conv_b, v_conv_b),
                conv_ln_g=(conv_ln_g, m_conv_ln_g, v_conv_ln_g), conv_ln_b=(conv_ln_b, m_conv_ln_b, v_conv_ln_b),
                norm_ffn2=(norm_ffn2, m_norm_ffn2, v_norm_ffn2), norm_final=(norm_final, m_norm_final, v_norm_final))
    sm_w["norm_final"] = tuple(t.reshape(1, D) for t in sm_w["norm_final"])
    grad_rows = dict({n: 9 + i for i, n in enumerate(SMALL_ORDER)}, ada_b=0, hgrn_lb=9 + SMALL_ORDER.index("lb0"))
    res.update(_adamw_small(tot, sm_names, sm_w, grad_rows))
    res["norm_final"] = tuple(t.reshape(norm_final.shape) for t in res["norm_final"])

    order = ("ada_w", "ada_b", "norm_ffn1", "ffn1_w_in", "ffn1_w_out", "norm_mix", "mix_w_in", "hgrn_lb", "hgrn_g",
             "hgrn_w_o", "conv_w", "conv_b", "conv_ln_g", "conv_ln_b", "conv_w_o", "mix_w_out", "norm_ffn2",
             "ffn2_w_in", "ffn2_w_out", "norm_final")
    lead = lambda n, t: t[None] if n in big or n == "ada_w" else t
    outs = [loss, dx[None]]
    for j in range(4):
        outs += [lead(n, res[n][j]) for n in order]
    return tuple(outs)
```

```python
import jax
import jax.numpy as jnp
from jax import lax
from jax.experimental import pallas as pl
from jax.experimental.pallas import tpu as pltpu

F32 = jnp.float32
MM = jnp.bfloat16
ACT = jnp.bfloat16

D = 1024
D_FF = 2816
HEADS = 8
HD = 128
CHUNK = 64
SUB = 16
NSUB = CHUNK // SUB
HGRN_BLOCK = 1024
SAFE_EXP = 60.0
CONV_K = 31
HALO = 32
EPS = 1e-6
N_DEV = 8
NEG = -1e30
Q_SCALE = HD ** -0.5

ADAM_LR = 0.001
ADAM_B1 = 0.9
ADAM_B2 = 0.999
ADAM_EPS = 1e-08
ADAM_WD = 0.01
ADAM_STEP = 10

V7X_VMEM_BYTES = 64 * 1024 * 1024
VMEM_LIMIT = V7X_VMEM_BYTES - 4 * 1024 * 1024
MESH = pl.DeviceIdType.MESH


def _cparams(n_axes):
    return pltpu.CompilerParams(dimension_semantics=("arbitrary",) * n_axes, vmem_limit_bytes=VMEM_LIMIT)


def _mm(a, b):
    return lax.dot_general(a.astype(MM), b.astype(MM), (((1,), (0,)), ((), ())), preferred_element_type=F32)


def _mm_nt(a, b):
    return lax.dot_general(a.astype(MM), b.astype(MM), (((1,), (1,)), ((), ())), preferred_element_type=F32)


def _mm_tn(a, b):
    return lax.dot_general(a.astype(MM), b.astype(MM), (((0,), (0,)), ((), ())), preferred_element_type=F32)


def _sig(x):
    return 1.0 / (1.0 + jnp.exp(-x))


def _colsum(x):
    return jnp.sum(x, axis=0, keepdims=True)


def _rowmean(x):
    return jnp.mean(x, axis=-1, keepdims=True)


def _modnorm_fwd(xv, g, sh, sc):
    r = lax.rsqrt(_rowmean(xv * xv) + EPS)
    xh = xv * r
    n = xh * g
    return n * (1.0 + sc) + sh, xh, n, r


def _modnorm_bwd(dh, xh, n, r, g, sc):
    dsc = _colsum(dh * n)
    dsh = _colsum(dh)
    dn = dh * (1.0 + sc)
    dg = _colsum(dn * xh)
    dxh = dn * g
    dx = r * (dxh - xh * _rowmean(dxh * xh))
    return dx, dsh, dsc, dg


def _ffn_fwd(x, mod, mo, gnorm, w_in_t, w_out, res, name, carry, nxt=None, head=None):
    T = x.shape[0]
    tm = min(512, T)
    tn = D_FF // 2
    ni = T // tm

    def body(x_ref, mod_ref, g_ref, wi_ref, wo_ref, *rest):
        if head is not None:
            t_ref, gf_ref, xo_ref, a_ref, b_ref, f_ref, h_ref, df_ref, sm_ref = rest
        elif nxt is None:
            xo_ref, a_ref, b_ref, f_ref, h_ref = rest
        else:
            gn_ref, xo_ref, a_ref, b_ref, f_ref, h_ref, hn_ref = rest
        xv = x_ref[...]
        h, _, _, _ = _modnorm_fwd(xv, g_ref[...], mod_ref[mo:mo + 1, :], mod_ref[mo + 1:mo + 2, :])
        h = h.astype(ACT)
        h_ref[...] = h
        f = None
        for c0 in range(0, D_FF, tn):
            a = _mm_nt(h, wi_ref[0, c0:c0 + tn, :])
            b = _mm_nt(h, wi_ref[1, c0:c0 + tn, :])
            a_ref[:, c0:c0 + tn] = a.astype(ACT)
            b_ref[:, c0:c0 + tn] = b.astype(ACT)
            part = _mm(a * _sig(a) * b, wo_ref[c0:c0 + tn, :])
            f = part if f is None else f + part
        f_ref[...] = f
        xo = xv + res * mod_ref[mo + 2:mo + 3, :] * f
        if nxt is not None:
            hn, _, _, _ = _modnorm_fwd(xo, gn_ref[...], mod_ref[nxt[1]:nxt[1] + 1, :], mod_ref[nxt[1] + 1:nxt[1] + 2, :])
            hn_ref[...] = hn.astype(ACT)
        if head is None:
            xo_ref[...] = xo
            return
        i = pl.program_id(0)

        @pl.when(i == 0)
        def _():
            sm_ref[...] = jnp.zeros_like(sm_ref)

        gf = gf_ref[...]
        r = lax.rsqrt(_rowmean(xo * xo) + EPS)
        xh = xo * r
        e = xh * gf - t_ref[...]
        sm_ref[1:2, :] += _colsum(e * e) * (0.5 / D)
        dy = e * (1.0 / D)
        sm_ref[0:1, :] += _colsum(dy * xh)
        dxh = dy * gf
        dx = r * (dxh - xh * _rowmean(dxh * xh))
        xo_ref[...] = dx
        df_ref[...] = (head[3] * mod_ref[head[2]:head[2] + 1, :] * dx).astype(MM)

        @pl.when(i == ni - 1)
        def _():
            sm_ref[1:2, :] = jnp.broadcast_to(jnp.sum(sm_ref[1:2, :], axis=-1, keepdims=True), (1, D))

    tile = pl.BlockSpec((tm, D), lambda i: (i, 0))
    wide = pl.BlockSpec((tm, D_FF), lambda i: (i, 0))
    row = pl.BlockSpec((1, D), lambda i: (0, 0))
    extra_in, extra_specs, extra_out, extra_shape = (), [], [], []
    if nxt is not None:
        extra_in, extra_specs = (nxt[0],), [row]
        extra_out, extra_shape = [tile], [jax.ShapeDtypeStruct((T, D), ACT)]
    if head is not None:
        extra_in, extra_specs = (head[0], head[1]), [tile, row]
        extra_out = [tile, pl.BlockSpec((8, D), lambda i: (0, 0))]
        extra_shape = [jax.ShapeDtypeStruct((T, D), MM), jax.ShapeDtypeStruct((8, D), F32)]
    n_out = 5 + len(extra_out)
    out = _gridded(
        body, carry, name=name, grid=(ni,),
        in_specs=[
            tile,
            pl.BlockSpec((9, D), lambda i: (0, 0)),
            row,
            pl.BlockSpec((2, D_FF, D), lambda i: (0, 0, 0), pipeline_mode=pl.Buffered(1)),
            pl.BlockSpec((D_FF, D), lambda i: (0, 0), pipeline_mode=pl.Buffered(1)),
        ] + extra_specs,
        out_specs=[tile, wide, wide, tile, tile] + extra_out,
        out_shape=[
            jax.ShapeDtypeStruct((T, D), F32),
            jax.ShapeDtypeStruct((T, D_FF), ACT),
            jax.ShapeDtypeStruct((T, D_FF), ACT),
            jax.ShapeDtypeStruct((T, D), F32),
            jax.ShapeDtypeStruct((T, D), ACT),
        ] + extra_shape,
    )(*((x, mod, gnorm, w_in_t, w_out) + extra_in))
    return out[:n_out], out[n_out:]


def _ffn_bwd_w(h, df, a, b, w_out, name, carry):
    T = h.shape[0]
    tm = min(2048, T)
    ni = T // tm
    tn = 256
    nj = D_FF // tn

    def body(h_ref, df_ref, a_ref, b_ref, wo_ref, da_ref, db_ref, dwi_ref, dwo_ref, acc_i, acc_o):
        i = pl.program_id(1)

        @pl.when(i == 0)
        def _():
            acc_i[...] = jnp.zeros_like(acc_i)
            acc_o[...] = jnp.zeros_like(acc_o)

        hb = h_ref[...]
        df = df_ref[...]
        av = a_ref[...].astype(F32)
        bv = b_ref[...].astype(F32)
        sg = _sig(av)
        sa = av * sg
        s = (sa * bv).astype(MM)
        ds = _mm_nt(df, wo_ref[...])
        da = (ds * bv * sg * (1.0 + av * (1.0 - sg))).astype(MM)
        db = (ds * sa).astype(MM)
        da_ref[...] = da
        db_ref[...] = db
        acc_o[...] += _mm_tn(s, df)
        acc_i[0] += _mm_tn(da, hb)
        acc_i[1] += _mm_tn(db, hb)

        @pl.when(i == ni - 1)
        def _():
            dwi_ref[...] = acc_i[...].astype(MM)
            dwo_ref[...] = acc_o[...].astype(MM)

    out = _gridded(
        body, carry, name=name, grid=(nj, ni),
        in_specs=[
            pl.BlockSpec((tm, D), lambda j, i: (i, 0)),
            pl.BlockSpec((tm, D), lambda j, i: (i, 0)),
            pl.BlockSpec((tm, tn), lambda j, i: (i, j)),
            pl.BlockSpec((tm, tn), lambda j, i: (i, j)),
            pl.BlockSpec((tn, D), lambda j, i: (j, 0)),
        ],
        out_specs=[
            pl.BlockSpec((tm, tn), lambda j, i: (i, j)),
            pl.BlockSpec((tm, tn), lambda j, i: (i, j)),
            pl.BlockSpec((2, tn, D), lambda j, i: (0, j, 0)),
            pl.BlockSpec((tn, D), lambda j, i: (j, 0)),
        ],
        out_shape=[
            jax.ShapeDtypeStruct((T, D_FF), MM),
            jax.ShapeDtypeStruct((T, D_FF), MM),
            jax.ShapeDtypeStruct((2, D_FF, D), MM),
            jax.ShapeDtypeStruct((D_FF, D), MM),
        ],
        scratch_shapes=[pltpu.VMEM((2, tn, D), F32), pltpu.VMEM((tn, D), F32)],
    )(h, df, a, b, w_out)
    return out[:4], out[4:]


def _ffn_bwd_x(x, dxo, f, da, db, mod, mo, gnorm, w_in_t, res, name, carry):
    T = x.shape[0]
    tm = min(512, T)
    ni = T // tm
    tn = D_FF // 2
    nj = D_FF // tn

    def body(x_ref, dxo_ref, f_ref, da_ref, db_ref, mod_ref, g_ref, wi_ref, dx_ref, sm_ref, dh_scr):
        j = pl.program_id(0)
        i = pl.program_id(1)

        @pl.when((j == 0) & (i == 0))
        def _():
            sm_ref[...] = jnp.zeros_like(sm_ref)

        @pl.when(j == 0)
        def _():
            dh_scr[i] = jnp.zeros((tm, D), F32)

        dh_scr[i] += _mm(da_ref[...], wi_ref[0]) + _mm(db_ref[...], wi_ref[1])

        @pl.when(j == nj - 1)
        def _():
            sc = mod_ref[mo + 1:mo + 2, :]
            _, xh, n, r = _modnorm_fwd(x_ref[...], g_ref[...], mod_ref[mo:mo + 1, :], sc)
            dxn, dsh, dsc, dg = _modnorm_bwd(dh_scr[i], xh, n, r, g_ref[...], sc)
            dxo_v = dxo_ref[...]
            dx_ref[...] = dxo_v + dxn
            sm_ref[0:1, :] += dsh
            sm_ref[1:2, :] += dsc
            sm_ref[2:3, :] += _colsum(dxo_v * f_ref[...]) * res
            sm_ref[3:4, :] += dg

    last = pl.BlockSpec((tm, D), lambda j, i: (jnp.where(j == nj - 1, i, 0), 0))
    out = _gridded(
        body, carry, name=name, grid=(nj, ni),
        in_specs=[last, last, last,
                  pl.BlockSpec((tm, tn), lambda j, i: (i, j)), pl.BlockSpec((tm, tn), lambda j, i: (i, j)),
                  pl.BlockSpec((9, D), lambda j, i: (0, 0)), pl.BlockSpec((1, D), lambda j, i: (0, 0)),
                  pl.BlockSpec((2, tn, D), lambda j, i: (0, j, 0))],
        out_specs=[last, pl.BlockSpec((8, D), lambda j, i: (0, 0))],
        out_shape=[jax.ShapeDtypeStruct((T, D), F32), jax.ShapeDtypeStruct((8, D), F32)],
        scratch_shapes=[pltpu.VMEM((ni, tm, D), F32)],
    )(x, dxo, f, da, db, mod, gnorm, w_in_t)
    return out[:2], out[2:]


def _mixin_fwd(h, w, carry):
    T = h.shape[0]
    tm = min(2048, T)
    ni = T // tm

    def body(h_ref, w_ref, p_ref, h_all):
        i = pl.program_id(1)

        @pl.when(pl.program_id(0) == 0)
        def _():
            h_all[i] = h_ref[...]

        p_ref[0] = _mm(h_all[i], w_ref[0])

    first = lambda k, i: (jnp.where(k == 0, i, ni - 1), 0)
    out = _gridded(
        body, carry, name="mixin_fwd", grid=(8, ni),
        in_specs=[pl.BlockSpec((tm, D), first), pl.BlockSpec((1, D, D), lambda k, i: (k, 0, 0))],
        out_specs=[pl.BlockSpec((1, tm, D), lambda k, i: (k, i, 0))],
        out_shape=[jax.ShapeDtypeStruct((8, T, D), F32)],
        scratch_shapes=[pltpu.VMEM((ni, tm, D), ACT)],
    )(h, w)
    return out[:1], out[1:]


def _mixin_bwd(x, h, dxo, dp, mod, mo, gnorm, w, next_gate, next_res, carry):
    T = x.shape[0]
    tm = min(512, T)
    ni = T // tm

    def body(x_ref, h_ref, dxo_ref, dp_ref, mod_ref, g_ref, w_ref, dx_ref, dw_ref, sm_ref, df_ref, dh_scr, acc):
        k = pl.program_id(0)
        i = pl.program_id(1)

        @pl.when(i == 0)
        def _():
            acc[...] = jnp.zeros_like(acc)

        @pl.when(k == 0)
        def _():
            dh_scr[i] = jnp.zeros((tm, D), F32)

        @pl.when((k == 0) & (i == 0))
        def _():
            sm_ref[...] = jnp.zeros_like(sm_ref)

        dpk = dp_ref[0].astype(MM)
        acc[...] += _mm_tn(h_ref[...], dpk)
        dh_scr[i] += _mm_nt(dpk, w_ref[0])

        @pl.when(i == ni - 1)
        def _():
            dw_ref[0] = acc[...].astype(MM)

        @pl.when(k == 7)
        def _():
            sc = mod_ref[mo + 1:mo + 2, :]
            _, xh, n, r = _modnorm_fwd(x_ref[...], g_ref[...], mod_ref[mo:mo + 1, :], sc)
            dxn, dsh, dsc, dg = _modnorm_bwd(dh_scr[i], xh, n, r, g_ref[...], sc)
            dx = dxo_ref[...] + dxn
            dx_ref[...] = dx
            df_ref[...] = (next_res * mod_ref[next_gate:next_gate + 1, :] * dx).astype(MM)
            sm_ref[0:1, :] += dsh
            sm_ref[1:2, :] += dsc
            sm_ref[3:4, :] += dg

    last = pl.BlockSpec((tm, D), lambda k, i: (jnp.where(k == 7, i, 0), 0))
    out = _gridded(
        body, carry, name="mixin_bwd", grid=(8, ni),
        in_specs=[pl.BlockSpec((tm, D), lambda k, i: (jnp.where(k == 7, i, 0), 0)),
                  pl.BlockSpec((tm, D), lambda k, i: (i, 0)),
                  pl.BlockSpec((tm, D), lambda k, i: (jnp.where(k == 7, i, 0), 0)),
                  pl.BlockSpec((1, tm, D), lambda k, i: (k, i, 0)), pl.BlockSpec((9, D), lambda k, i: (0, 0)),
                  pl.BlockSpec((1, D), lambda k, i: (0, 0)), pl.BlockSpec((1, D, D), lambda k, i: (k, 0, 0))],
        out_specs=[last, pl.BlockSpec((1, D, D), lambda k, i: (k, 0, 0)), pl.BlockSpec((8, D), lambda k, i: (0, 0)),
                   last],
        out_shape=[jax.ShapeDtypeStruct((T, D), F32), jax.ShapeDtypeStruct((8, D, D), MM),
                   jax.ShapeDtypeStruct((8, D), F32), jax.ShapeDtypeStruct((T, D), MM)],
        scratch_shapes=[pltpu.VMEM((ni, tm, D), F32), pltpu.VMEM((D, D), F32)],
    )(x, h, dxo, dp, mod, gnorm, w)
    return out[:4], out[4:]


def _hgrn_consts():
    rows = jnp.arange(SUB * HD) // HD
    e = (rows[:, None] == jnp.arange(HD)[None, :]).astype(MM)
    return e, e.T


def _rows_bcast(ref, cb, first, n):
    parts = [jnp.broadcast_to(ref[pl.ds(c * CHUNK + first, 1), :], (n, HD)) for c in range(cb // CHUNK)]
    return jnp.concatenate(parts, axis=0)


def _hgrn_pre(qr, fr, lb_ref, b_scr, cb):
    z = lb_ref[...]
    lb = _sig(z[0:1, :] - z[1:2, :])
    sq = _sig(qr)
    q = qr * sq * Q_SCALE
    sf = _sig(fr)
    fg = lb + (1.0 - lb) * sf
    lf = jnp.log(fg)
    k = 1.0 - fg
    tl = lax.broadcasted_iota(jnp.int32, (cb, HD), 0) % CHUNK
    bc = lf
    sh = 1
    while sh < CHUNK:
        bc = bc + jnp.where(tl >= sh, pltpu.roll(bc, sh, 0), 0.0)
        sh *= 2
    b_scr[...] = bc
    bl = _rows_bcast(b_scr, cb, CHUNK - 1, CHUNK)
    eb = jnp.exp(bc)
    ekd = jnp.exp(bl - bc)
    ekf = jnp.exp(jnp.minimum(-bc, SAFE_EXP))
    return dict(lb=lb, sq=sq, q=q, sf=sf, fg=fg, k=k, tl=tl, b=bc, bl=bl, eb=eb, ekd=ekd, ekf=ekf,
                qe=q * eb, kd=k * ekd, kf=k * ekf, safe=jnp.max(-bc) < SAFE_EXP)


def _hgrn_pre_fused(p_ref, lb_ref, b_scr, q_scr, k_scr, qe_scr, kf_scr, kd_scr, cb):
    z = lb_ref[...]
    lb = _sig(z[0:1, :] - z[1:2, :])
    tl = lax.broadcasted_iota(jnp.int32, (CHUNK, HD), 0)

    def chunk(c, worst):
        r0 = pl.multiple_of(c * CHUNK, CHUNK)
        rs = pl.ds(r0, CHUNK)
        qr = p_ref[0, rs, :]
        q = qr * _sig(qr) * Q_SCALE
        fg = lb + (1.0 - lb) * _sig(p_ref[1, rs, :])
        k = 1.0 - fg
        bc = jnp.log(fg)
        sh = 1
        while sh < CHUNK:
            bc = bc + jnp.where(tl >= sh, pltpu.roll(bc, sh, 0), 0.0)
            sh *= 2
        b_scr[rs, :] = bc
        q_scr[rs, :] = q
        k_scr[rs, :] = k
        qe_scr[rs, :] = (q * jnp.exp(bc)).astype(MM)
        kf_scr[rs, :] = (k * jnp.exp(jnp.minimum(-bc, SAFE_EXP))).astype(MM)
        kd_scr[rs, :] = (k * jnp.exp(b_scr[pl.ds(r0 + CHUNK - 1, 1), :] - bc)).astype(MM)
        return jnp.maximum(worst, -bc)

    worst = lax.fori_loop(0, cb // CHUNK, chunk, jnp.zeros((CHUNK, HD), F32))
    return jnp.max(worst) < SAFE_EXP


def _hgrn_sub(pre, b_scr, cb):
    bc, tl, q, k = pre["b"], pre["tl"], pre["q"], pre["k"]
    br = [None] + [_rows_bcast(b_scr, cb, SUB * i - 1, CHUNK) for i in range(1, NSUB)]
    sb = tl // SUB
    bref = jnp.where(sb == 0, bc, jnp.where(sb == 1, br[1], jnp.where(sb == 2, br[2], br[3])))
    eqo = jnp.exp(bc - bref)
    eko = [None] + [jnp.exp(jnp.where(tl < SUB * i, br[i] - bc, NEG)) for i in range(1, NSUB)]
    return dict(eqo=eqo, eko=eko, qo=q * eqo, ko=[None] + [k * eko[i] for i in range(1, NSUB)])


def _pad_rows(x):
    return jnp.concatenate([x, jnp.zeros_like(x)], axis=0)


def _by_subblock(sbc, parts):
    out = jnp.zeros_like(parts[1])
    for i in range(1, NSUB):
        out = jnp.where(sbc == i, parts[i], out)
    return out


def _hgrn_fwd(p, hgrn_lb, hgrn_g, carry):
    T = p.shape[1]
    cb = min(HGRN_BLOCK, T)
    nch = cb // CHUNK
    ncb = T // cb
    e_mat, _ = _hgrn_consts()

    def body(p_ref, lb_ref, g_ref, e_ref, o_ref, oa_ref, a_ref, s_ref, st_scr, q_scr, k_scr, b_scr, z_scr, ad_scr,
             qe_scr, kf_scr, kd_scr):
        @pl.when(pl.program_id(1) == 0)
        def _():
            st_scr[...] = jnp.zeros_like(st_scr)

        safe = _hgrn_pre_fused(p_ref, lb_ref, b_scr, q_scr, k_scr, qe_scr, kf_scr, kd_scr, cb)
        chunks = [slice(c * CHUNK, (c + 1) * CHUNK) for c in range(nch)]
        row_i = lax.broadcasted_iota(jnp.int32, (CHUNK, HD), 0)
        lane_i = lax.broadcasted_iota(jnp.int32, (CHUNK, HD), 1)
        sbc = row_i // SUB
        causal = lane_i <= row_i

        @pl.when(safe)
        def _():
            for rs in chunks:
                ad_scr[rs, :] = jnp.where(causal, _mm_nt(qe_scr[rs, :], _pad_rows(kf_scr[rs, :])), 0.0)

        @pl.when(jnp.logical_not(safe))
        def _():
            tl = lax.broadcasted_iota(jnp.int32, (cb, HD), 0) % CHUNK
            sub = _hgrn_sub(dict(b=b_scr[...], tl=tl, q=q_scr[...], k=k_scr[...]), b_scr, cb)
            ti = lax.broadcasted_iota(jnp.int32, (SUB, HD), 0)

            def zbody(c, carry):
                for i in range(NSUB):
                    r0 = pl.multiple_of(c * CHUNK + SUB * i, SUB)
                    qi = q_scr[pl.ds(r0, SUB), :]
                    bi = b_scr[pl.ds(r0, SUB), :]
                    for s in range(SUB):
                        krow = k_scr[pl.ds(r0 + s, 1), :]
                        brow = b_scr[pl.ds(r0 + s, 1), :]
                        if s < 8:
                            zz = qi * krow * jnp.exp(jnp.where(ti >= s, bi - brow, NEG))
                        else:
                            lo = qi[8:] * krow * jnp.exp(jnp.where(ti[8:] >= s, bi[8:] - brow, NEG))
                            zz = jnp.concatenate([jnp.zeros((8, HD), F32), lo], axis=0)
                        z_scr[i, pl.ds(pl.multiple_of(c * SUB, SUB), SUB), s * HD:(s + 1) * HD] = zz.astype(MM)
                return carry

            lax.fori_loop(0, nch, zbody, 0)
            adiag = [_mm(z_scr[i], e_ref[...]) for i in range(NSUB)]
            offs = [[_mm_nt(sub["qo"][rs], _pad_rows(sub["ko"][i][rs])) for i in range(1, NSUB)] for rs in chunks]
            for c, rs in enumerate(chunks):
                dparts = []
                for i in range(NSUB):
                    blk = adiag[i][c * SUB:(c + 1) * SUB]
                    dparts.append(blk if i == 0 else pltpu.roll(blk, SUB * i, 1))
                ad_scr[rs, :] = _by_subblock(sbc, [None] + offs[c]) + jnp.concatenate(dparts, axis=0)

        kv = [_mm_tn(p_ref[2, rs, :], kd_scr[rs, :]) for rs in chunks]
        a_ref[0] = ad_scr[...]
        o_intra = [_mm(ad_scr[rs, :], _pad_rows(p_ref[2, rs, :])) for rs in chunks]
        states = []
        st = st_scr[...]
        for c in range(nch):
            states.append(st)
            st = st * jnp.exp(b_scr[pl.ds(c * CHUNK + CHUNK - 1, 1), :]) + kv[c]
        st_scr[...] = st
        g = g_ref[...]
        for c, rs in enumerate(chunks):
            s_ref[0, c] = states[c]
            o = o_intra[c] + _mm_nt(qe_scr[rs, :], states[c])
            o_ref[rs, :] = o
            og = p_ref[3, rs, :]
            oa_ref[rs, :] = (o * lax.rsqrt(_rowmean(o * o) + EPS) * g * og * _sig(og)).astype(ACT)

    out = _gridded(
        body, carry, name="hgrn_fwd", grid=(HEADS, ncb),
        in_specs=[pl.BlockSpec((4, cb, HD), lambda h, c: (0, c, h)),
                  pl.BlockSpec((2, HD), lambda h, c: (0, h)),
                  pl.BlockSpec((1, HD), lambda h, c: (0, h)),
                  pl.BlockSpec((SUB * HD, HD), lambda h, c: (0, 0))],
        out_specs=[pl.BlockSpec((cb, HD), lambda h, c: (c, h)),
                   pl.BlockSpec((cb, HD), lambda h, c: (c, h)),
                   pl.BlockSpec((1, cb, HD), lambda h, c: (h, c, 0)),
                   pl.BlockSpec((1, nch, HD, HD), lambda h, c: (h, c, 0, 0))],
        out_shape=[jax.ShapeDtypeStruct((T, D), F32), jax.ShapeDtypeStruct((T, D), ACT),
                   jax.ShapeDtypeStruct((HEADS, T, HD), F32),
                   jax.ShapeDtypeStruct((HEADS, T // CHUNK, HD, HD), F32)],
        scratch_shapes=[pltpu.VMEM((HD, HD), F32), pltpu.VMEM((cb, HD), F32), pltpu.VMEM((cb, HD), F32),
                        pltpu.VMEM((cb, HD), F32), pltpu.VMEM((NSUB, nch * SUB, SUB * HD), MM),
                        pltpu.VMEM((cb, HD), F32), pltpu.VMEM((cb, HD), MM), pltpu.VMEM((cb, HD), MM),
                        pltpu.VMEM((cb, HD), MM)],
    )(p, hgrn_lb, hgrn_g, e_mat)
    return out[:4], out[4:]


def _hgrn_bwd(p, o, a_all, s_all, doa, hgrn_lb, hgrn_g, dp, carry):
    T = p.shape[1]
    cb = min(HGRN_BLOCK, T)
    nch = cb // CHUNK
    ncb = T // cb
    _, et_mat = _hgrn_consts()

    def body(p_ref, o_ref, a_ref, s_ref, doa_ref, lb_ref, g_ref, et_ref, dp_in, dp_ref, sm_ref,
             dst_scr, q_scr, k_scr, b_scr, x_scr, dqd_scr, dkd_scr):
        del dp_in

        @pl.when(pl.program_id(1) == 0)
        def _():
            dst_scr[...] = jnp.zeros_like(dst_scr)
            sm_ref[...] = jnp.zeros_like(sm_ref)

        qr = p_ref[0]
        v = p_ref[2]
        og = p_ref[3]
        pre = _hgrn_pre(qr, p_ref[1], lb_ref, b_scr, cb)
        q, k = pre["q"], pre["k"]
        g = g_ref[...]
        ov = o_ref[...]
        r = lax.rsqrt(_rowmean(ov * ov) + EPS)
        oh = ov * r
        sgo = _sig(og)
        doa_v = doa_ref[...]
        don = doa_v * og * sgo
        dog = doa_v * oh * g * sgo * (1.0 + og * (1.0 - sgo))
        sm_ref[1:2, :] += _colsum(don * oh)
        doh = don * g
        do = r * (doh - oh * _rowmean(doh * oh))

        sbc = lax.broadcasted_iota(jnp.int32, (CHUNK, HD), 0) // SUB
        row_i = lax.broadcasted_iota(jnp.int32, (CHUNK, HD), 0)
        lane_i = lax.broadcasted_iota(jnp.int32, (CHUNK, HD), 1)
        causal = lane_i <= row_i
        chunks = [slice(c * CHUNK, (c + 1) * CHUNK) for c in range(nch)]
        da_parts = [jnp.where(causal, _mm_nt(do[rs], _pad_rows(v[rs])), 0.0) for rs in chunks]
        dv_parts = [_mm_tn(a_ref[0, rs, :], do[rs])[:CHUNK] for rs in chunks]

        @pl.when(pre["safe"])
        def _():
            hi = dict(preferred_element_type=F32, precision=lax.Precision.HIGH)
            for c, rs in enumerate(chunks):
                dqd_scr[rs, :] = pre["eb"][rs] * lax.dot_general(
                    da_parts[c], _pad_rows(pre["kf"][rs]), (((1,), (0,)), ((), ())), **hi)
                dkd_scr[rs, :] = pre["ekf"][rs] * lax.dot_general(
                    da_parts[c], pre["qe"][rs], (((0,), (0,)), ((), ())), **hi)[:CHUNK]

        @pl.when(jnp.logical_not(pre["safe"]))
        def _():
            sub = _hgrn_sub(pre, b_scr, cb)
            dqoff_mm = [[_mm(da_parts[c], _pad_rows(sub["ko"][i][rs])) for i in range(1, NSUB)]
                        for c, rs in enumerate(chunks)]
            dkoff_mm = [[_mm_tn(jnp.where(sbc == i, da_parts[c], 0.0), sub["qo"][rs])[:CHUNK]
                         for i in range(1, NSUB)] for c, rs in enumerate(chunks)]
            dqoff_parts = [_by_subblock(sbc, [None] + dqoff_mm[c]) for c in range(nch)]
            dkoff_parts = []
            for c, rs in enumerate(chunks):
                dko = sub["eko"][1][rs] * dkoff_mm[c][0]
                for i in range(2, NSUB):
                    dko = dko + sub["eko"][i][rs] * dkoff_mm[c][i - 1]
                dkoff_parts.append(dko)
            q_scr[...] = q
            k_scr[...] = k
            for i in range(NSUB):
                rows = []
                for c in range(nch):
                    blk = da_parts[c][SUB * i:SUB * (i + 1)]
                    rows.append(blk if i == 0 else pltpu.roll(blk, HD - SUB * i, 1))
                x_scr[i] = _mm(jnp.concatenate(rows, axis=0), et_ref[...])
            ti = lax.broadcasted_iota(jnp.int32, (SUB, HD), 0)

            def dbody(c, carry):
                for i in range(NSUB):
                    r0 = pl.multiple_of(c * CHUNK + SUB * i, SUB)
                    qi = q_scr[pl.ds(r0, SUB), :]
                    bi = b_scr[pl.ds(r0, SUB), :]
                    dq_hi = jnp.zeros((8, HD), F32)
                    dq_lo = jnp.zeros((8, HD), F32)
                    dk_hi = jnp.zeros((8, HD), F32)
                    dk_lo = jnp.zeros((8, HD), F32)
                    c0 = pl.multiple_of(c * SUB, SUB)
                    t8 = ti[:8]
                    for s in range(SUB):
                        krow = k_scr[pl.ds(r0 + s, 1), :]
                        brow = b_scr[pl.ds(r0 + s, 1), :]
                        w_lo = (x_scr[i, pl.ds(c0 + 8, 8), s * HD:(s + 1) * HD]
                                * jnp.exp(jnp.where(t8 + 8 >= s, bi[8:] - brow, NEG)))
                        dq_lo = dq_lo + w_lo * krow
                        col = _colsum(w_lo * qi[8:])
                        if s < 8:
                            w_hi = (x_scr[i, pl.ds(c0, 8), s * HD:(s + 1) * HD]
                                    * jnp.exp(jnp.where(t8 >= s, bi[:8] - brow, NEG)))
                            dq_hi = dq_hi + w_hi * krow
                            dk_hi = jnp.where(t8 == s, col + _colsum(w_hi * qi[:8]), dk_hi)
                        else:
                            dk_lo = jnp.where(t8 + 8 == s, col, dk_lo)
                    dqd_scr[pl.ds(r0, SUB), :] = jnp.concatenate([dq_hi, dq_lo], axis=0)
                    dkd_scr[pl.ds(r0, SUB), :] = jnp.concatenate([dk_hi, dk_lo], axis=0)
                return carry

            lax.fori_loop(0, nch, dbody, 0)
            dqd_scr[...] += jnp.concatenate(dqoff_parts, axis=0) * sub["eqo"]
            dkd_scr[...] += jnp.concatenate(dkoff_parts, axis=0)

        qdo = [_mm_tn(do[rs], pre["qe"][rs]) for rs in chunks]
        dsts = [None] * nch
        dst = dst_scr[...]
        for c in reversed(range(nch)):
            dsts[c] = dst
            dst = dst * jnp.exp(b_scr[pl.ds(c * CHUNK + CHUNK - 1, 1), :]) + qdo[c]
        dst_scr[...] = dst
        sts = [s_ref[0, c] for c in range(nch)]
        dqe_parts = [_mm(do[rs], sts[c]) for c, rs in enumerate(chunks)]
        dkdec_parts = [_mm(v[rs], dsts[c]) for c, rs in enumerate(chunks)]
        dvi_parts = [_mm_nt(pre["kd"][rs], dsts[c]) for c, rs in enumerate(chunks)]
        debl_parts = [_colsum(dsts[c] * sts[c]) for c in range(nch)]
        dqe = jnp.concatenate(dqe_parts, axis=0)
        dkdec = jnp.concatenate(dkdec_parts, axis=0)
        dq_tot = dqd_scr[...] + dqe * pre["eb"]
        dk_inter = dkdec * pre["ekd"]
        dk_tot = dkd_scr[...] + dk_inter
        db = q * dq_tot - k * dk_tot
        kdk = k * dk_inter
        dbl = jnp.concatenate(
            [jnp.broadcast_to(jnp.exp(b_scr[pl.ds(c * CHUNK + CHUNK - 1, 1), :]) * debl_parts[c]
                              + _colsum(kdk[c * CHUNK:(c + 1) * CHUNK]), (CHUNK, HD)) for c in range(nch)], axis=0)
        tl = pre["tl"]
        rc = db
        sh = 1
        while sh < CHUNK:
            rc = rc + jnp.where(tl + sh < CHUNK, pltpu.roll(rc, cb - sh, 0), 0.0)
            sh *= 2
        dlf = rc + dbl
        dfg = dlf / pre["fg"] - dk_tot
        sf = pre["sf"]
        lb = pre["lb"]
        sm_ref[0:1, :] += _colsum(dfg * (1.0 - sf))
        sq = pre["sq"]
        dp_ref[0] = (dq_tot * Q_SCALE * sq * (1.0 + qr * (1.0 - sq))).astype(ACT)
        dp_ref[1] = (dfg * (1.0 - lb) * sf * (1.0 - sf)).astype(ACT)
        dp_ref[2] = (jnp.concatenate(dv_parts, axis=0) + jnp.concatenate(dvi_parts, axis=0)).astype(ACT)
        dp_ref[3] = dog.astype(ACT)

    rev = lambda c: ncb - 1 - c
    out = _gridded(
        body, carry, name="hgrn_bwd", grid=(HEADS, ncb),
        in_specs=[pl.BlockSpec((4, cb, HD), lambda h, c: (0, rev(c), h)),
                  pl.BlockSpec((cb, HD), lambda h, c: (rev(c), h)),
                  pl.BlockSpec((1, cb, HD), lambda h, c: (h, rev(c), 0)),
                  pl.BlockSpec((1, nch, HD, HD), lambda h, c: (h, rev(c), 0, 0)),
                  pl.BlockSpec((cb, HD), lambda h, c: (rev(c), h)),
                  pl.BlockSpec((2, HD), lambda h, c: (0, h)),
                  pl.BlockSpec((1, HD), lambda h, c: (0, h)),
                  pl.BlockSpec((HD, SUB * HD), lambda h, c: (0, 0)),
                  pl.BlockSpec(memory_space=pl.ANY)],
        out_specs=[pl.BlockSpec((4, cb, HD), lambda h, c: (0, rev(c), h)),
                   pl.BlockSpec((8, HD), lambda h, c: (0, h))],
        out_shape=[jax.ShapeDtypeStruct(dp.shape, dp.dtype), jax.ShapeDtypeStruct((8, D), F32)],
        aliases={8: 0},
        scratch_shapes=[pltpu.VMEM((HD, HD), F32), pltpu.VMEM((cb, HD), F32), pltpu.VMEM((cb, HD), F32),
                        pltpu.VMEM((cb, HD), F32), pltpu.VMEM((NSUB, nch * SUB, SUB * HD), F32),
                        pltpu.VMEM((cb, HD), F32), pltpu.VMEM((cb, HD), F32)],
    )(p, o, a_all, s_all, doa, hgrn_lb, hgrn_g, et_mat, dp)
    return out[:2], out[2:]


def _ln_fwd(u1, g, b):
    mu = _rowmean(u1)
    xc = u1 - mu
    rs = lax.rsqrt(_rowmean(xc * xc) + EPS)
    xh = xc * rs
    return xh * g + b, xh, rs


CONV_RB = 64
LANES = 128


def _shift_rows(src, sh, ls, n):
    for r in range(1, 8):
        sh[r - 1, 0:n, :] = src[pl.ds(r, n), ls]


def _tap(src, sh, ls, off, r0, rows):
    r = off % 8
    if r == 0:
        return src[pl.ds(r0 + off, rows), ls]
    return sh[r - 1, pl.ds(r0 + off - r, rows), :]


def _conv_fwd(p, cw, cb_, lng, lnb, carry):
    T = p.shape[1]
    tm = min(512, T)
    n = HALO + tm - 8

    def body(p_ref, cw_ref, cb_ref, g_ref, b_ref, u1_ref, u2_ref, buf, sh):
        @pl.when(pl.program_id(0) == 0)
        def _():
            buf[0:HALO, :] = jnp.zeros((HALO, D), F32)

        buf[HALO:HALO + tm, :] = p_ref[0] * _sig(p_ref[1])
        for lb in range(D // LANES):
            ls = slice(lb * LANES, (lb + 1) * LANES)
            _shift_rows(buf, sh, ls, n)
            taps = [cw_ref[j:j + 1, ls] for j in range(CONV_K)]
            bias = cb_ref[:, ls]

            def rows_body(rb, carry):
                r0 = pl.multiple_of(rb * CONV_RB, CONV_RB)
                acc = jnp.broadcast_to(bias, (CONV_RB, LANES))
                for j in range(CONV_K):
                    acc = acc + taps[j] * _tap(buf, sh, ls, HALO - (CONV_K - 1) + j, r0, CONV_RB)
                u1_ref[pl.ds(r0, CONV_RB), ls] = acc
                return carry

            lax.fori_loop(0, tm // CONV_RB, rows_body, 0)
        y, _, _ = _ln_fwd(u1_ref[...], g_ref[...], b_ref[...])
        u2_ref[...] = (y * _sig(y)).astype(ACT)
        buf[0:HALO, :] = buf[tm:tm + HALO, :]

    out = _gridded(
        body, carry, name="conv_fwd", grid=(T // tm,),
        in_specs=[pl.BlockSpec((2, tm, D), lambda i: (2, i, 0)), pl.BlockSpec((HALO, D), lambda i: (0, 0)),
                  pl.BlockSpec((1, D), lambda i: (0, 0)), pl.BlockSpec((1, D), lambda i: (0, 0)),
                  pl.BlockSpec((1, D), lambda i: (0, 0))],
        out_specs=[pl.BlockSpec((tm, D), lambda i: (i, 0)), pl.BlockSpec((tm, D), lambda i: (i, 0))],
        out_shape=[jax.ShapeDtypeStruct((T, D), F32), jax.ShapeDtypeStruct((T, D), ACT)],
        scratch_shapes=[pltpu.VMEM((HALO + tm, D), F32), pltpu.VMEM((7, n, LANES), F32)],
    )(p, cw, cb_, lng, lnb)
    return out[:2], out[2:]


def _conv_bwd(p, u1, du2, cw, lng, lnb, dp, carry):
    T = p.shape[1]
    tm = min(512, T)
    ni = T // tm
    hb = tm // HALO

    n = HALO + tm - 8

    def body(p_ref, ph_ref, u1_ref, du2_ref, cw_ref, g_ref, b_ref, dp_in, dp_ref, dcw_ref, sm_ref, ubuf, dbuf,
             sh, dacc):
        del dp_in
        step = pl.program_id(0)

        @pl.when(step == 0)
        def _():
            dbuf[tm:tm + HALO, :] = jnp.zeros((HALO, D), F32)
            dcw_ref[...] = jnp.zeros_like(dcw_ref)
            sm_ref[...] = jnp.zeros_like(sm_ref)

        ua = p_ref[0]
        sgb = _sig(p_ref[1])
        halo = ph_ref[0] * _sig(ph_ref[1])
        ubuf[0:HALO, :] = jnp.where(step == ni - 1, 0.0, halo)
        ubuf[HALO:HALO + tm, :] = ua * sgb
        g = g_ref[...]
        y, xh, rs = _ln_fwd(u1_ref[...], g, b_ref[...])
        sy = _sig(y)
        dy = du2_ref[...] * sy * (1.0 + y * (1.0 - sy))
        sm_ref[1:2, :] += _colsum(dy * xh)
        sm_ref[2:3, :] += _colsum(dy)
        dxh = dy * g
        du1 = rs * (dxh - _rowmean(dxh) - xh * _rowmean(dxh * xh))
        sm_ref[0:1, :] += _colsum(du1)
        dbuf[0:tm, :] = du1
        for lb in range(D // LANES):
            ls = slice(lb * LANES, (lb + 1) * LANES)
            taps = [cw_ref[j:j + 1, ls] for j in range(CONV_K)]
            _shift_rows(dbuf, sh, ls, n)

            def du0_body(rb, carry):
                r0 = pl.multiple_of(rb * CONV_RB, CONV_RB)
                acc = jnp.zeros((CONV_RB, LANES), F32)
                for j in range(CONV_K):
                    acc = acc + taps[j] * _tap(dbuf, sh, ls, CONV_K - 1 - j, r0, CONV_RB)
                dp_ref[0, pl.ds(r0, CONV_RB), ls] = acc.astype(ACT)
                return carry

            lax.fori_loop(0, tm // CONV_RB, du0_body, 0)
            _shift_rows(ubuf, sh, ls, n)
            dacc[...] = jnp.zeros_like(dacc)

            def dcw_body(rb, carry):
                r0 = pl.multiple_of(rb * CONV_RB, CONV_RB)
                d = dbuf[pl.ds(r0, CONV_RB), ls]
                for j in range(CONV_K):
                    prod = d * _tap(ubuf, sh, ls, HALO - (CONV_K - 1) + j, r0, CONV_RB)
                    dacc[8 * j:8 * j + 8, :] += jnp.sum(prod.reshape(CONV_RB // 8, 8, LANES), axis=0)
                return carry

            lax.fori_loop(0, tm // CONV_RB, dcw_body, 0)
            for j in range(CONV_K):
                dcw_ref[j:j + 1, ls] += _colsum(dacc[8 * j:8 * j + 8, :])
        du0 = dp_ref[0].astype(F32)
        dp_ref[0] = (du0 * sgb).astype(ACT)
        dp_ref[1] = (du0 * ua * sgb * (1.0 - sgb)).astype(ACT)
        dbuf[tm:tm + HALO, :] = dbuf[0:HALO, :]

    rev = lambda i: ni - 1 - i
    out = _gridded(
        body, carry, name="conv_bwd", grid=(ni,),
        in_specs=[pl.BlockSpec((2, tm, D), lambda i: (2, rev(i), 0)),
                  pl.BlockSpec((2, HALO, D), lambda i: (2, jnp.maximum(rev(i) * hb - 1, 0), 0)),
                  pl.BlockSpec((tm, D), lambda i: (rev(i), 0)), pl.BlockSpec((tm, D), lambda i: (rev(i), 0)),
                  pl.BlockSpec((HALO, D), lambda i: (0, 0)), pl.BlockSpec((1, D), lambda i: (0, 0)),
                  pl.BlockSpec((1, D), lambda i: (0, 0)), pl.BlockSpec(memory_space=pl.ANY)],
        out_specs=[pl.BlockSpec((2, tm, D), lambda i: (2, rev(i), 0)),
                   pl.BlockSpec((HALO, D), lambda i: (0, 0)), pl.BlockSpec((8, D), lambda i: (0, 0))],
        out_shape=[jax.ShapeDtypeStruct(dp.shape, dp.dtype), jax.ShapeDtypeStruct((HALO, D), F32),
                   jax.ShapeDtypeStruct((8, D), F32)],
        aliases={7: 0},
        scratch_shapes=[pltpu.VMEM((HALO + tm, D), F32), pltpu.VMEM((tm + HALO, D), F32),
                        pltpu.VMEM((7, n, LANES), F32), pltpu.VMEM((8 * CONV_K, LANES), F32)],
    )(p, p, u1, du2, cw, lng, lnb, dp)
    return out[:3], out[3:]


def _mixout_fwd(x, oa, u2, p, mod, mo, w_a, w_b, w_o):
    T = x.shape[0]
    tm = min(512, T)

    def body(x_ref, oa_ref, u2_ref, p_ref, mod_ref, wa_ref, wb_ref, wo_ref, xo_ref, ya_ref, yb_ref, mo_ref):
        ya = _mm(oa_ref[...], wa_ref[...])
        yb = _mm(u2_ref[...], wb_ref[...])
        ya_ref[...] = ya.astype(ACT)
        yb_ref[...] = yb.astype(ACT)
        merged = _sig(p_ref[0]) * ya + _sig(p_ref[1]) * yb
        out = _mm(merged, wo_ref[...])
        mo_ref[...] = out
        xo_ref[...] = x_ref[...] + mod_ref[mo + 2:mo + 3, :] * out

    tile = pl.BlockSpec((tm, D), lambda i: (i, 0))
    wspec = pl.BlockSpec((D, D), lambda i: (0, 0))
    return pl.pallas_call(
        body, name="mixout_fwd", grid=(T // tm,),
        in_specs=[tile, tile, tile, pl.BlockSpec((2, tm, D), lambda i: (3, i, 0)),
                  pl.BlockSpec((9, D), lambda i: (0, 0)), wspec, wspec, wspec],
        out_specs=[tile, tile, tile, tile],
        out_shape=[jax.ShapeDtypeStruct((T, D), F32), jax.ShapeDtypeStruct((T, D), ACT),
                   jax.ShapeDtypeStruct((T, D), ACT), jax.ShapeDtypeStruct((T, D), F32)],
        compiler_params=_cparams(1),
    )(x, oa, u2, p, mod, w_a, w_b, w_o)


def _mixout_bwd(dxo, oa, u2, ya, yb, mout, p, mod, mo, w_a, w_b, w_o):
    T = dxo.shape[0]
    tm = min(256, T)

    def body(dxo_ref, oa_ref, u2_ref, ya_ref, yb_ref, mo_ref, p_ref, mod_ref, wa_ref, wb_ref, wo_ref,
             dp_ref, doa_ref, du2_ref, dwa_ref, dwb_ref, dwo_ref, sm_ref):
        @pl.when(pl.program_id(0) == 0)
        def _():
            dwa_ref[...] = jnp.zeros_like(dwa_ref)
            dwb_ref[...] = jnp.zeros_like(dwb_ref)
            dwo_ref[...] = jnp.zeros_like(dwo_ref)
            sm_ref[...] = jnp.zeros_like(sm_ref)

        dxo_v = dxo_ref[...]
        sm_ref[2:3, :] += _colsum(dxo_v * mo_ref[...])
        dmo = (mod_ref[mo + 2:mo + 3, :] * dxo_v).astype(MM)
        ya = ya_ref[...].astype(F32)
        yb = yb_ref[...].astype(F32)
        sga = _sig(p_ref[0])
        sgb = _sig(p_ref[1])
        merged = (sga * ya + sgb * yb).astype(MM)
        dwo_ref[...] += _mm_tn(merged, dmo)
        dmg = _mm_nt(dmo, wo_ref[...])
        dp_ref[0] = (dmg * ya * sga * (1.0 - sga)).astype(ACT)
        dp_ref[1] = (dmg * yb * sgb * (1.0 - sgb)).astype(ACT)
        dya = (dmg * sga).astype(MM)
        dyb = (dmg * sgb).astype(MM)
        dwa_ref[...] += _mm_tn(oa_ref[...], dya)
        dwb_ref[...] += _mm_tn(u2_ref[...], dyb)
        doa_ref[...] = _mm_nt(dya, wa_ref[...])
        du2_ref[...] = _mm_nt(dyb, wb_ref[...])

    tile = pl.BlockSpec((tm, D), lambda i: (i, 0))
    wspec = pl.BlockSpec((D, D), lambda i: (0, 0))
    return pl.pallas_call(
        body, name="mixout_bwd", grid=(T // tm,),
        in_specs=[tile, tile, tile, tile, tile, tile, pl.BlockSpec((2, tm, D), lambda i: (3, i, 0)),
                  pl.BlockSpec((9, D), lambda i: (0, 0)), wspec, wspec, wspec],
        out_specs=[pl.BlockSpec((2, tm, D), lambda i: (3, i, 0)), tile, tile, wspec, wspec, wspec,
                   pl.BlockSpec((8, D), lambda i: (0, 0))],
        out_shape=[jax.ShapeDtypeStruct((8, T, D), ACT), jax.ShapeDtypeStruct((T, D), F32),
                   jax.ShapeDtypeStruct((T, D), F32), jax.ShapeDtypeStruct((D, D), F32),
                   jax.ShapeDtypeStruct((D, D), F32), jax.ShapeDtypeStruct((D, D), F32),
                   jax.ShapeDtypeStruct((8, D), F32)],
        compiler_params=_cparams(1),
    )(dxo, oa, u2, ya, yb, mout, p, mod, w_a, w_b, w_o)


def _adamw_ada_w(w, m, v, cs_all, dmod_cols):
    R, C = w.shape
    tr = 256
    cs_t = jnp.pad(cs_all.T, ((0, 0), (0, HD - N_DEV)))
    dm = jnp.pad(dmod_cols, ((0, HD - N_DEV), (0, 0)))

    def body(w_ref, m_ref, v_ref, cs_ref, d_ref, go_ref, do_ref, mo_ref, vo_ref):
        gv = jnp.dot(cs_ref[...], d_ref[...], preferred_element_type=F32, precision=lax.Precision.HIGHEST)
        go_ref[...] = gv
        do_ref[...], mo_ref[...], vo_ref[...] = _adam_math(w_ref[...], gv, m_ref[...], v_ref[...])

    tile = pl.BlockSpec((tr, C), lambda i: (i, 0))
    sds = jax.ShapeDtypeStruct((R, C), F32)
    return pl.pallas_call(
        body, name="adamw_ada_w", grid=(R // tr,),
        in_specs=[tile, tile, tile, pl.BlockSpec((tr, HD), lambda i: (i, 0)), pl.BlockSpec((HD, C), lambda i: (0, 0))],
        out_specs=[tile] * 4, out_shape=[sds] * 4, compiler_params=_cparams(1))(w, m, v, cs_t, dm)


def _adam_math(w, g, m, v):
    m2 = ADAM_B1 * m + (1.0 - ADAM_B1) * g
    v2 = ADAM_B2 * v + (1.0 - ADAM_B2) * (g * g)
    m_hat = m2 / (1.0 - ADAM_B1 ** ADAM_STEP)
    v_hat = v2 / (1.0 - ADAM_B2 ** ADAM_STEP)
    delta = -ADAM_LR * (m_hat / (jnp.sqrt(v_hat) + ADAM_EPS) + ADAM_WD * w)
    return delta, m2, v2


def _adamw(w, m, v, g, name, carry=None):
    R, C = w.shape
    slots = g.ndim == 3
    n_slots = g.shape[0] if slots else 0
    tr = R
    for cand in (256, 176):
        if R % cand == 0 and R > cand:
            tr = cand
            break

    def body(w_ref, m_ref, v_ref, g_ref, go_ref, d_ref, mo_ref, vo_ref):
        if slots:
            gv = g_ref[0].astype(F32)
            for s in range(1, n_slots):
                gv = gv + g_ref[s].astype(F32)
        else:
            gv = g_ref[...]
        go_ref[...] = gv
        d_ref[...], mo_ref[...], vo_ref[...] = _adam_math(w_ref[...], gv, m_ref[...], v_ref[...])

    tile = pl.BlockSpec((tr, C), lambda i: (i, 0))
    gspec = pl.BlockSpec((n_slots, tr, C), lambda i: (0, i, 0)) if slots else tile
    sds = jax.ShapeDtypeStruct((R, C), F32)
    return _gridded(body, carry, name=name, grid=(R // tr,), in_specs=[tile, tile, tile, gspec],
                    out_specs=[tile] * 4, out_shape=[sds] * 4)(w, m, v, g)


def _adamw_small(tot, names, params, grad_rows):
    k = len(names)

    def body(tot_ref, *refs):
        ins, outs = refs[:3 * k], refs[3 * k:]
        for i, n in enumerate(names):
            w_ref, m_ref, v_ref = ins[3 * i:3 * i + 3]
            go, do, mo, vo = outs[4 * i:4 * i + 4]
            row = grad_rows[n]
            for j in range(w_ref.shape[1] // D):
                ls = slice(j * D, (j + 1) * D)
                g = tot_ref[row + j:row + j + 1, :]
                w = w_ref[:, ls]
                if n == "hgrn_lb":
                    p0 = _sig(w[0:1] - w[1:2])
                    dz0 = p0 * (1.0 - p0) * g
                    g = jnp.concatenate([dz0, -dz0], axis=0)
                go[:, ls] = g
                do[:, ls], mo[:, ls], vo[:, ls] = _adam_math(w, g, m_ref[:, ls], v_ref[:, ls])

    flat = [t for n in names for t in params[n]]
    out_shape = [jax.ShapeDtypeStruct(params[n][0].shape, F32) for n in names for _ in range(4)]
    outs = pl.pallas_call(body, name="adamw_small", out_shape=out_shape)(tot, *flat)
    return {n: tuple(outs[4 * i:4 * i + 4]) for i, n in enumerate(names)}


def _cast_shards(ws, name="cast_shards"):
    n = len(ws)

    def body(*refs):
        for src, dst in zip(refs[:n], refs[n:]):
            dst[...] = src[...].astype(MM)

    return pl.pallas_call(body, name=name, out_shape=[jax.ShapeDtypeStruct(w.shape, MM) for w in ws],
                          compiler_params=pltpu.CompilerParams(vmem_limit_bytes=VMEM_LIMIT))(*ws)


def _sum_slots(pack, name, tr):
    n, R, C = pack.shape

    def body(p_ref, out_ref):
        acc = p_ref[0].astype(F32)
        for s in range(1, n):
            acc = acc + p_ref[s].astype(F32)
        out_ref[...] = acc

    return pl.pallas_call(
        body, name=name, grid=(R // tr,), in_specs=[pl.BlockSpec((n, tr, C), lambda i: (0, i, 0))],
        out_specs=pl.BlockSpec((tr, C), lambda i: (i, 0)), out_shape=jax.ShapeDtypeStruct((R, C), F32),
        compiler_params=_cparams(1))(pack)


def _me():
    return lax.axis_index("x"), lax.axis_index("y"), lax.axis_index("c")


def _peer(r):
    x, y, c = _me()
    px = 1 - x if r & 4 else x
    py = 1 - y if r & 2 else y
    pc = 1 - c if r & 1 else c
    return (px, py, pc), 4 * px + 2 * py + pc


def _small_gather(x_ref, out_ref, send_sems, recv_sems):
    R = x_ref.shape[0]
    mx, my, mc = _me()
    me = 4 * mx + 2 * my + mc
    mine = out_ref.at[pl.ds(pl.multiple_of(me * R, 8), R), :]
    copies = []
    for r in range(1, N_DEV):
        dev, _ = _peer(r)
        copies.append(pltpu.make_async_remote_copy(
            src_ref=x_ref, dst_ref=mine, send_sem=send_sems.at[r - 1], recv_sem=recv_sems.at[r - 1],
            device_id=dev, device_id_type=MESH))
    for cp in copies:
        cp.start()
    mine[...] = x_ref[...]
    for r in range(1, N_DEV):
        dev, idx = _peer(r)
        theirs = out_ref.at[pl.ds(pl.multiple_of(idx * R, 8), R), :]
        pltpu.make_async_remote_copy(
            src_ref=x_ref, dst_ref=theirs, send_sem=send_sems.at[r - 1], recv_sem=recv_sems.at[r - 1],
            device_id=dev, device_id_type=MESH).wait_recv()
    for cp in copies:
        cp.wait_send()


def _prologue(cs, ada_w, ada_b_cols, big):
    n = len(big)
    ncol = ada_w.shape[1]
    big_shape, big_sems = _xchg_specs(big, "gather")

    def body(cs_ref, w_ref, b_ref, *rest):
        big_in, cs_all, mod_all, big_out = rest[:n], rest[n], rest[n + 1], rest[n + 2:2 * n + 2]
        mod_scr, s1, r1, s2, r2 = rest[2 * n + 2:2 * n + 7]
        sems = rest[2 * n + 7:]
        _small_gather(cs_ref, cs_all, s1, r1)
        pick = (lax.broadcasted_iota(jnp.int32, (N_DEV, N_DEV * 8), 1)
                == 8 * lax.broadcasted_iota(jnp.int32, (N_DEV, N_DEV * 8), 0)).astype(F32)
        per_device = jnp.dot(pick, cs_all[...], preferred_element_type=F32, precision=lax.Precision.HIGHEST)
        mod_scr[...] = jnp.dot(per_device, w_ref[...], preferred_element_type=F32,
                               precision=lax.Precision.HIGHEST) + b_ref[...]
        _small_gather(mod_scr, mod_all, s2, r2)
        _xchg_start(big_in, big_out, sems, "gather")
        _xchg_wait(big_in, big_out, sems, "gather")

    vmem = pl.BlockSpec(memory_space=pltpu.VMEM)
    hbm = pl.BlockSpec(memory_space=pl.ANY)
    dma7 = pltpu.SemaphoreType.DMA((N_DEV - 1,))
    out = pl.pallas_call(
        body, name="prologue",
        out_shape=[jax.ShapeDtypeStruct((N_DEV * 8, D), F32), jax.ShapeDtypeStruct((N_DEV * 8, ncol), F32)]
        + big_shape,
        in_specs=[vmem, vmem, vmem] + [hbm] * n, out_specs=[vmem, vmem] + [hbm] * n,
        scratch_shapes=[pltpu.VMEM((8, ncol), F32), dma7, dma7, dma7, dma7] + big_sems,
        compiler_params=pltpu.CompilerParams(vmem_limit_bytes=VMEM_LIMIT),
    )(cs, ada_w, ada_b_cols, *big)
    return out[0], out[1], out[2:]


N_CHIP = N_DEV // 2


def _xchg_copies(ins, outs, sems, mode):
    send_sems, recv_sems, local_sems = sems
    mx, my, mc = _me()
    me = 4 * mx + 2 * my + mc
    my_chip = 2 * mx + my
    sibling = _peer(1)[0]

    def rdma(a, r, dev, src, slot):
        k = a * (N_DEV - 1) + r - 1
        return pltpu.make_async_remote_copy(
            src_ref=src, dst_ref=outs[a].at[slot], send_sem=send_sems.at[k], recv_sem=recv_sems.at[k],
            device_id=dev, device_id_type=MESH)

    own, sends, relays, recvs = [], [], [], []
    for a in range(len(ins)):
        if mode == "pair":
            for chip in range(N_CHIP):
                src = ins[a].at[2 * chip + 1 - mc]
                sends.append(rdma(a, chip + 1, sibling, src, chip))
                recvs.append(rdma(a, chip + 1, sibling, src, chip))
            continue
        if mode == "quad":
            own.append(pltpu.make_async_copy(ins[a].at[my_chip], outs[a].at[my_chip], local_sems.at[a]))
            for r in (2, 4, 6):
                dev, idx = _peer(r)
                chip = idx // 2
                sends.append(rdma(a, r, dev, ins[a].at[chip], my_chip))
                recvs.append(rdma(a, r, dev, ins[a].at[chip], chip))
            continue
        gather = mode == "gather"
        own.append(pltpu.make_async_copy(ins[a] if gather else ins[a].at[me], outs[a].at[me], local_sems.at[a]))
        for r in range(1, N_DEV):
            dev, idx = _peer(r)
            if not gather:
                sends.append(rdma(a, r, dev, ins[a].at[idx], me))
                recvs.append(rdma(a, r, dev, ins[a].at[idx], idx))
            elif r == 1:
                sends.append(rdma(a, r, dev, ins[a], me))
                recvs.append(rdma(a, r, dev, ins[a], idx))
            elif r % 2 == 0:
                sends.append(rdma(a, r, dev, ins[a], me))
                relays.append((rdma(a, r, dev, ins[a], idx), rdma(a, r + 1, sibling, outs[a].at[idx], idx)))
            else:
                recvs.append(rdma(a, r, sibling, ins[a], idx))
    return own, sends, relays, recvs


def _xchg_start(ins, outs, sems, mode):
    own, sends, _, _ = _xchg_copies(ins, outs, sems, mode)
    for cp in own + sends:
        cp.start()


def _xchg_wait(ins, outs, sems, mode):
    own, sends, relays, recvs = _xchg_copies(ins, outs, sems, mode)
    for arrival, relay in relays:
        arrival.wait_recv()
        relay.start()
    for cp in recvs:
        cp.wait_recv()
    for cp in own:
        cp.wait()
    for cp in sends + [relay for _, relay in relays]:
        cp.wait_send()


def _xchg_specs(arrays, mode):
    n = len(arrays)
    shape = {"gather": lambda s: (N_DEV,) + s, "scatter": lambda s: s, "pair": lambda s: (N_CHIP,) + s[1:],
             "quad": lambda s: s}[mode]
    out_shape = [jax.ShapeDtypeStruct(shape(a.shape), a.dtype) for a in arrays]
    sems = [pltpu.SemaphoreType.DMA((n * (N_DEV - 1),)), pltpu.SemaphoreType.DMA((n * (N_DEV - 1),)),
            pltpu.SemaphoreType.DMA((n,))]
    return out_shape, sems


def _exchange(arrays, mode, name):
    n = len(arrays)

    def body(*refs):
        _xchg_start(refs[:n], refs[n:2 * n], refs[2 * n:], mode)
        _xchg_wait(refs[:n], refs[n:2 * n], refs[2 * n:], mode)

    out_shape, sems = _xchg_specs(arrays, mode)
    return pl.pallas_call(
        body, name=name, out_shape=out_shape,
        in_specs=[pl.BlockSpec(memory_space=pl.ANY)] * n, out_specs=[pl.BlockSpec(memory_space=pl.ANY)] * n,
        scratch_shapes=sems,
    )(*arrays)


def _gridded(body, carry, *, name, grid, in_specs, out_specs, out_shape, scratch_shapes=(), aliases=None):
    if carry is None:
        return pl.pallas_call(
            body, name=name, grid=grid, in_specs=list(in_specs), out_specs=list(out_specs),
            out_shape=list(out_shape), scratch_shapes=list(scratch_shapes), input_output_aliases=aliases or {},
            compiler_params=_cparams(len(grid)))
    arrays, mode = carry
    n, n_in, n_out, n_scr = len(arrays), len(in_specs), len(out_specs), len(scratch_shapes)
    c_shape, c_sems = _xchg_specs(arrays, mode)

    def wrapped(*refs):
        ins, cin = refs[:n_in], refs[n_in:n_in + n]
        o0 = n_in + n
        outs, cout = refs[o0:o0 + n_out], refs[o0 + n_out:o0 + n_out + n]
        s0 = o0 + n_out + n
        scr, sems = refs[s0:s0 + n_scr], refs[s0 + n_scr:]
        first = pl.program_id(0) == 0
        last = pl.program_id(0) == grid[0] - 1
        for ax in range(1, len(grid)):
            first = first & (pl.program_id(ax) == 0)
            last = last & (pl.program_id(ax) == grid[ax] - 1)

        @pl.when(first)
        def _():
            _xchg_start(cin, cout, sems, mode)

        body(*ins, *outs, *scr)

        @pl.when(last)
        def _():
            _xchg_wait(cin, cout, sems, mode)

    hbm = pl.BlockSpec(memory_space=pl.ANY)
    res = pl.pallas_call(
        wrapped, name=name, grid=grid, in_specs=list(in_specs) + [hbm] * n, out_specs=list(out_specs) + [hbm] * n,
        out_shape=list(out_shape) + c_shape, scratch_shapes=list(scratch_shapes) + c_sems,
        input_output_aliases=aliases or {}, compiler_params=_cparams(len(grid)),
    )
    return lambda *args: res(*args, *arrays)


SMALL_ORDER = ("norm_ffn1", "norm_mix", "lb0", "hgrn_g", "conv_b", "conv_ln_g", "conv_ln_b", "norm_ffn2",
               "norm_final")
PACK_ROWS = 24
PACK_USED = 9 + len(SMALL_ORDER) + 1


def _pack_small(sm1, sm2, sm_mo, sm3, sm_hg, sm_cv, sm_head):
    arrays = (sm1, sm2, sm_mo, sm3, sm_hg, sm_cv, sm_head)
    src = ((0, 0, 3), (1, 0, 2), (2, 2, 1), (3, 0, 3),
           (0, 3, 1), (1, 3, 1), (4, 0, 2), (5, 0, 3), (3, 3, 1), (6, 0, 2))
    assert sum(n for _, _, n in src) == PACK_USED and all(a.shape == (8, D) for a in arrays)

    def body(*refs):
        out = refs[-1]
        out[...] = jnp.zeros_like(out)
        r = 0
        for a, first, n in src:
            out[r:r + n, :] = refs[a][first:first + n, :]
            r += n

    return pl.pallas_call(body, name="pack_small", out_shape=jax.ShapeDtypeStruct((PACK_ROWS, D), F32))(*arrays)


def _local_step(x, target, mod, small, sh, w1):
    w1_in, w1_out = w1[0].reshape(2, D_FF, D), w1[1].reshape(D_FF, D)
    (x1, a1, b1, f1, h1, h2), (wm_in,) = _ffn_fwd(x, mod, 0, small["norm_ffn1"], w1_in, w1_out, 0.5, "ffn1_fwd",
                                                  ([sh["mix_w_in"]], "gather"), nxt=(small["norm_mix"], 3))
    (p,), (wh_o, wc_o, wm_o, cw) = _mixin_fwd(
        h2, wm_in, ([sh["hgrn_w_o"], sh["conv_w_o"], sh["mix_w_out"], sh["conv_w"]], "gather"))
    wh_o, wc_o, wm_o = wh_o.reshape(D, D), wc_o.reshape(D, D), wm_o.reshape(D, D)
    cw = jnp.pad(cw.transpose(1, 0, 2).reshape(CONV_K, D), ((0, HALO - CONV_K), (0, 0)))
    (o, oa, a_all, s_all), (w2_in,) = _hgrn_fwd(p, small["hgrn_lb"], small["hgrn_g"], ([sh["ffn2_w_in"]], "gather"))
    (u1, u2), (w2_out,) = _conv_fwd(p, cw, small["conv_b"], small["conv_ln_g"], small["conv_ln_b"],
                                    ([sh["ffn2_w_out"]], "gather"))
    w2_in, w2_out = w2_in.reshape(2, D_FF, D), w2_out.reshape(D_FF, D)
    x2, ya, yb, mout = _mixout_fwd(x1, oa, u2, p, mod, 3, wh_o, wc_o, wm_o)
    (dx3, a3, b3, f3, h3, df3, sm_head), _ = _ffn_fwd(x2, mod, 6, small["norm_ffn2"], w2_in, w2_out, 0.5, "ffn2_fwd",
                                                      None, head=(target, small["norm_final"], 8, 0.5))

    (da3, db3, dw2_in, dw2_out), _ = _ffn_bwd_w(h3, df3, a3, b3, w2_out, "ffn2_bwd_w", None)
    rows = lambda t: t.reshape(N_DEV, -1, D).astype(MM)
    (dx2, sm3), (r2_out,) = _ffn_bwd_x(x2, dx3, f3, da3, db3, mod, 6, small["norm_ffn2"], w2_in, 0.5, "ffn2_bwd_x",
                                       ([rows(dw2_out)], "scatter"))
    dp, doa, du2, dwh_o, dwc_o, dwm_o, sm_mo = _mixout_bwd(dx2, oa, u2, ya, yb, mout, p, mod, 3, wh_o, wc_o, wm_o)
    (dp, dcw, sm_cv), (r2_in,) = _conv_bwd(p, u1, du2, cw, small["conv_ln_g"], small["conv_ln_b"], dp,
                                           ([rows(dw2_in)], "scatter"))
    (dp, sm_hg), _ = _hgrn_bwd(p, o, a_all, s_all, doa, small["hgrn_lb"], small["hgrn_g"], dp, None)
    dwh_o, dwc_o, dwm_o = _cast_shards([dwh_o, dwc_o, dwm_o], "cast_grads")
    (dx1, dwm_in, sm2, df1), (rh_o, rc_o, rm_o, rcw) = _mixin_bwd(
        x1, h2, dx2, dp, mod, 3, small["norm_mix"], wm_in, 2, 0.5,
        ([rows(dwh_o), rows(dwc_o), rows(dwm_o), dcw[:CONV_K].reshape(CONV_K, N_DEV, -1).transpose(1, 0, 2)],
         "scatter"))
    (da1, db1, dw1_in, dw1_out), (rm_in,) = _ffn_bwd_w(h1, df1, a1, b1, w1_out, "ffn1_bwd_w",
                                                      (_pair_reduce([dwm_in], "pair_mix"), "quad"))
    (dx0, sm1), (r1_in, r1_out) = _ffn_bwd_x(
        x, dx1, f1, da1, db1, mod, 0, small["norm_ffn1"], w1_in, 0.5, "ffn1_bwd_x",
        (_pair_reduce([rows(dw1_in), rows(dw1_out)], "pair_ffn1"), "quad"))

    pack = _pack_small(sm1, sm2, sm_mo, sm3, sm_hg, sm_cv, sm_head)
    recv = dict(ffn1_w_in=r1_in, ffn1_w_out=r1_out, mix_w_in=rm_in, hgrn_w_o=rh_o, conv_w=rcw, conv_w_o=rc_o,
                mix_w_out=rm_o, ffn2_w_in=r2_in, ffn2_w_out=r2_out)
    return dx0, pack, recv


def _pair_add(mine, theirs, core, name):
    _, R, C = theirs.shape

    def body(core_ref, a_ref, b_ref, out_ref):
        del core_ref
        out_ref[0] = (a_ref[0, 0].astype(F32) + b_ref[0].astype(F32)).astype(out_ref.dtype)

    blk = pl.BlockSpec((1, R, C), lambda s, core_ref: (s, 0, 0))
    grid_spec = pltpu.PrefetchScalarGridSpec(
        num_scalar_prefetch=1, grid=(N_CHIP,),
        in_specs=[pl.BlockSpec((1, 1, R, C), lambda s, core_ref: (s, core_ref[0], 0, 0)), blk], out_specs=blk)
    return pl.pallas_call(body, name=name, grid_spec=grid_spec,
                          out_shape=jax.ShapeDtypeStruct(theirs.shape, mine.dtype), compiler_params=_cparams(1),
                          )(core, mine.reshape(N_CHIP, 2, R, C), theirs)


def _pair_reduce(arrays, name):
    theirs = _exchange(arrays, "pair", name)
    core = lax.axis_index("c").astype(jnp.int32).reshape(1)
    return [_pair_add(a, t, core, "%s_add%d" % (name, i)) for i, (a, t) in enumerate(zip(arrays, theirs))]


def kernel(x, c, ada_w, ada_b, norm_ffn1, ffn1_w_in, ffn1_w_out, norm_mix, mix_w_in, hgrn_lb, hgrn_g, hgrn_w_o, conv_w, conv_b, conv_ln_g, conv_ln_b, conv_w_o, mix_w_out, norm_ffn2, ffn2_w_in, ffn2_w_out, norm_final, loss_target, m_ada_w, m_ada_b, m_norm_ffn1, m_ffn1_w_in, m_ffn1_w_out, m_norm_mix, m_mix_w_in, m_hgrn_lb, m_hgrn_g, m_hgrn_w_o, m_conv_w, m_conv_b, m_conv_ln_g, m_conv_ln_b, m_conv_w_o, m_mix_w_out, m_norm_ffn2, m_ffn2_w_in, m_ffn2_w_out, m_norm_final, v_ada_w, v_ada_b, v_norm_ffn1, v_ffn1_w_in, v_ffn1_w_out, v_norm_mix, v_mix_w_in, v_hgrn_lb, v_hgrn_g, v_hgrn_w_o, v_conv_w, v_conv_b, v_conv_ln_g, v_conv_ln_b, v_conv_w_o, v_mix_w_out, v_norm_ffn2, v_ffn2_w_in, v_ffn2_w_out, v_norm_final):
    mx, my, mc = _me()
    me = 4 * mx + 2 * my + mc
    ncol = ada_w.shape[2]

    sh = dict(ffn1_w_in=ffn1_w_in[0].T, ffn1_w_out=ffn1_w_out[0], mix_w_in=mix_w_in[0], hgrn_w_o=hgrn_w_o[0],
              conv_w_o=conv_w_o[0], mix_w_out=mix_w_out[0], ffn2_w_in=ffn2_w_in[0].T, ffn2_w_out=ffn2_w_out[0])
    sh = dict(zip(sh, _cast_shards(list(sh.values()))))
    sh["conv_w"] = conv_w[0]
    small = dict(norm_ffn1=norm_ffn1, norm_mix=norm_mix, hgrn_lb=hgrn_lb, hgrn_g=hgrn_g, conv_b=conv_b,
                 conv_ln_g=conv_ln_g, conv_ln_b=conv_ln_b, norm_ffn2=norm_ffn2, norm_final=norm_final.reshape(1, D))

    cs = jnp.broadcast_to(c * jax.nn.sigmoid(c), (8, D))
    ada_b_cols = lax.dynamic_slice(ada_b, (0, me * ncol), (1, ncol))
    cs_all, mod_all, w1 = _prologue(cs, ada_w[0], ada_b_cols, [sh["ffn1_w_in"], sh["ffn1_w_out"]])
    cs_all = cs_all.reshape(N_DEV, 8, D)[:, 0, :]
    mod = lax.dynamic_index_in_dim(mod_all.reshape(N_DEV, N_DEV, ncol), me, axis=1, keepdims=False).reshape(9, D)

    dx, pack, recv = _local_step(x[0], loss_target[0], mod, small, sh, w1)

    res = {}
    *res["ffn2_w_out"], pack_all = _adamw(ffn2_w_out[0], m_ffn2_w_out[0], v_ffn2_w_out[0], recv["ffn2_w_out"],
                                          "adamw_ffn2_w_out", ([pack], "gather"))
    tot = _sum_slots(pack_all, "sum_small", PACK_ROWS)
    loss = tot[PACK_USED - 1, 0]
    dmod_all = pack_all[:, 0:9, :].reshape(N_DEV, 9 * D)
    dmod_cols = lax.dynamic_slice(dmod_all, (0, me * ncol), (N_DEV, ncol))

    res["ada_w"] = _adamw_ada_w(ada_w[0], m_ada_w[0], v_ada_w[0], cs_all, dmod_cols)
    big = dict(ffn1_w_in=(ffn1_w_in, m_ffn1_w_in, v_ffn1_w_in), ffn1_w_out=(ffn1_w_out, m_ffn1_w_out, v_ffn1_w_out),
               mix_w_in=(mix_w_in, m_mix_w_in, v_mix_w_in), hgrn_w_o=(hgrn_w_o, m_hgrn_w_o, v_hgrn_w_o),
               conv_w=(conv_w, m_conv_w, v_conv_w), conv_w_o=(conv_w_o, m_conv_w_o, v_conv_w_o),
               mix_w_out=(mix_w_out, m_mix_w_out, v_mix_w_out), ffn2_w_in=(ffn2_w_in, m_ffn2_w_in, v_ffn2_w_in),
               ffn2_w_out=(ffn2_w_out, m_ffn2_w_out, v_ffn2_w_out))
    for n, (w, m, v) in big.items():
        if n in res:
            continue
        if n in ("ffn1_w_in", "ffn2_w_in"):
            res[n] = tuple(t.T for t in _adamw(w[0].T, m[0].T, v[0].T, recv[n], "adamw_" + n))
        else:
            res[n] = _adamw(w[0], m[0], v[0], recv[n], "adamw_" + n)
    sm_names = ("ada_b", "norm_ffn1", "norm_mix", "hgrn_lb", "hgrn_g", "conv_b", "conv_ln_g", "conv_ln_b",
                "norm_ffn2", "norm_final")
    sm_w = dict(ada_b=(ada_b, m_ada_b, v_ada_b), norm_ffn1=(norm_ffn1, m_norm_ffn1, v_norm_ffn1),
                norm_mix=(norm_mix, m_norm_mix, v_norm_mix), hgrn_lb=(hgrn_lb, m_hgrn_lb, v_hgrn_lb),
                hgrn_g=(hgrn_g, m_hgrn_g, v_hgrn_g), conv_b=(conv_b, m_conv_b, v_conv_b),
                conv_ln_g=(conv_ln_g, m_conv_ln_g, v_conv_ln_g), conv_ln_b=(conv_ln_b, m_conv_ln_b, v_conv_ln_b),
                norm_ffn2=(norm_ffn2, m_norm_ffn2, v_norm_ffn2), norm_final=(norm_final, m_norm_final, v_norm_final))
    sm_w["norm_final"] = tuple(t.reshape(1, D) for t in sm_w["norm_final"])
    grad_rows = dict({n: 9 + i for i, n in enumerate(SMALL_ORDER)}, ada_b=0, hgrn_lb=9 + SMALL_ORDER.index("lb0"))
    res.update(_adamw_small(tot, sm_names, sm_w, grad_rows))
    res["norm_final"] = tuple(t.reshape(norm_final.shape) for t in res["norm_final"])

    order = ("ada_w", "ada_b", "norm_ffn1", "ffn1_w_in", "ffn1_w_out", "norm_mix", "mix_w_in", "hgrn_lb", "hgrn_g",
             "hgrn_w_o", "conv_w", "conv_b", "conv_ln_g", "conv_ln_b", "conv_w_o", "mix_w_out", "norm_ffn2",
             "ffn2_w_in", "ffn2_w_out", "norm_final")
    lead = lambda n, t: t[None] if n in big or n == "ada_w" else t
    outs = [loss, dx[None]]
    for j in range(4):
        outs += [lead(n, res[n][j]) for n in order]
    return tuple(outs)
```
